```python
import math
import jax, jax.numpy as jnp
from jax import lax
import numpy as np

D_MODEL = 1024
BATCH = 8
SEQ = 8192
DEPTH = 1

SSD_EXPAND = 2
D_INNER = SSD_EXPAND * D_MODEL
SSD_HEAD_DIM = 64
SSD_HEADS = D_INNER // SSD_HEAD_DIM
SSD_GROUPS = 4
SSD_HEADS_PER_GROUP = SSD_HEADS // SSD_GROUPS
D_STATE = 128
D_CONV = 5
SSD_CHUNK = 128
CONV_DIM = D_INNER + 2 * SSD_GROUPS * D_STATE
NORM_EPS = 1e-5

ATTN_HEAD_DIM = 64
DIL_PATTERNS = ((128, 1), (512, 4), (2048, 16))
N_PATTERNS = len(DIL_PATTERNS)
HEADS_PER_PATTERN = 4
ATTN_HEADS = N_PATTERNS * HEADS_PER_PATTERN
ATTN_WIDTH = ATTN_HEADS * ATTN_HEAD_DIM
ATTN_OUT_WIDTH = HEADS_PER_PATTERN * ATTN_HEAD_DIM

D_FF = 4 * D_MODEL
N_BRANCHES = 2
IN_SPLITS = (D_INNER, CONV_DIM, SSD_HEADS, SSD_HEADS, ATTN_WIDTH, ATTN_WIDTH, ATTN_WIDTH, N_BRANCHES * D_MODEL)
IN_COLS = sum(IN_SPLITS)

kernel_name = "hybrid_ssd_dilated_attn_gated_deepnorm"


def layer_norm(x, g, b):
    xf = x.astype(jnp.float32)
    mu = jnp.mean(xf, axis=-1, keepdims=True)
    var = jnp.mean(jnp.square(xf - mu), axis=-1, keepdims=True)
    return ((xf - mu) * lax.rsqrt(var + NORM_EPS) * g.astype(jnp.float32) + b.astype(jnp.float32)).astype(x.dtype)


def centred_depthwise_conv(u, w, b):
    pad = D_CONV // 2
    out = lax.conv_general_dilated(u, w[:, None, :].astype(u.dtype), window_strides=(1,), padding=((pad, pad),),
                                   dimension_numbers=('NWC', 'WIO', 'NWC'), feature_group_count=u.shape[-1])
    return out + b.astype(u.dtype)


def segsum_exp(a_cs):
    q = a_cs.shape[-1]
    diff = a_cs[..., :, None] - a_cs[..., None, :]
    mask = jnp.tril(jnp.ones((q, q), dtype=bool))
    return jnp.where(mask, jnp.exp(jnp.where(mask, diff, 0.0)), 0.0)


def ssd_chunked(xh, dt, a_coef, bm, cm):
    bsz, s = xh.shape[:2]
    nc = s // SSD_CHUNK
    g, r, p = SSD_GROUPS, SSD_HEADS_PER_GROUP, SSD_HEAD_DIM
    xc = (xh * dt[..., None]).reshape(bsz, nc, SSD_CHUNK, g, r, p)
    a = (dt * a_coef).reshape(bsz, nc, SSD_CHUNK, g, r).transpose(0, 3, 4, 1, 2)
    bc = bm.reshape(bsz, nc, SSD_CHUNK, g, D_STATE)
    cc = cm.reshape(bsz, nc, SSD_CHUNK, g, D_STATE)
    a_cs = jnp.cumsum(a, axis=-1)
    lmat = segsum_exp(a_cs)
    cb = jnp.einsum('bclgn,bcsgn->bgcls', cc, bc)
    y_diag = jnp.einsum('bgcls,bgrcls,bcsgrp->bclgrp', cb, lmat, xc)
    decay_states = jnp.exp(a_cs[..., -1:] - a_cs)
    states = jnp.einsum('bclgn,bgrcl,bclgrp->bcgrpn', bc, decay_states, xc)
    chunk_decay = jnp.exp(a_cs[..., -1])

    def step(h, inp):
        dec, st = inp
        return dec[..., None, None] * h + st, h

    h0 = jnp.zeros_like(states[:, 0])
    _, prev = lax.scan(step, h0, (jnp.moveaxis(chunk_decay, -1, 0), jnp.moveaxis(states, 1, 0)))
    prev = jnp.moveaxis(prev, 0, 1)
    y_off = jnp.einsum('bclgn,bcgrpn,bgrcl->bclgrp', cc, prev, jnp.exp(a_cs))
    return (y_diag + y_off).reshape(bsz, s, g * r, p)


def ssd_branch(z, xbc, dt_f_raw, dt_b_raw, conv_w, conv_b, dt_bias_f, dt_bias_b, a_log_f, a_log_b, d_skip, ssd_norm_w):
    bsz, s = z.shape[:2]
    xbc = jax.nn.silu(centred_depthwise_conv(xbc, conv_w, conv_b)).astype(jnp.float32)
    xs, bm, cm = jnp.split(xbc, [D_INNER, D_INNER + SSD_GROUPS * D_STATE], axis=-1)
    xh = xs.reshape(bsz, s, SSD_HEADS, SSD_HEAD_DIM)
    bm = bm.reshape(bsz, s, SSD_GROUPS, D_STATE)
    cm = cm.reshape(bsz, s, SSD_GROUPS, D_STATE)
    dt_f = jax.nn.softplus(dt_f_raw.astype(jnp.float32) + dt_bias_f.astype(jnp.float32))
    dt_b = jax.nn.softplus(dt_b_raw.astype(jnp.float32) + dt_bias_b.astype(jnp.float32))
    a_f = -jnp.exp(a_log_f.astype(jnp.float32))
    a_b = -jnp.exp(a_log_b.astype(jnp.float32))
    y_f = ssd_chunked(xh, dt_f, a_f, bm, cm)
    flip = lambda t: jnp.flip(t, axis=1)
    y_b = flip(ssd_chunked(flip(xh), flip(dt_b), a_b, flip(bm), flip(cm)))
    y = y_f + y_b + d_skip.astype(jnp.float32)[:, None] * xh
    y = y.reshape(bsz, s, D_INNER) * jax.nn.silu(z.astype(jnp.float32))
    yg = y.reshape(bsz, s, SSD_GROUPS, D_INNER // SSD_GROUPS)
    yg = yg * lax.rsqrt(jnp.mean(jnp.square(yg), axis=-1, keepdims=True) + NORM_EPS)
    return (yg.reshape(bsz, s, D_INNER) * ssd_norm_w.astype(jnp.float32)).astype(z.dtype)


def dilated_window_attention(q, k, v, slopes, dilation, half):
    bsz, s, h, e = q.shape
    seq_l = s // dilation
    blk = half
    nb = -(-seq_l // blk)
    lp = nb * blk

    def to_strided(a):
        return a.astype(jnp.float32).reshape(bsz, seq_l, dilation, h, e).transpose(0, 2, 3, 1, 4)

    qs = jnp.pad(to_strided(q), ((0, 0), (0, 0), (0, 0), (0, lp - seq_l), (0, 0))).reshape(bsz, dilation, h, nb, blk, e)

    def windows(a):
        a = jnp.pad(to_strided(a), ((0, 0), (0, 0), (0, 0), (blk, blk + lp - seq_l), (0, 0)))
        a = a.reshape(bsz, dilation, h, nb + 2, blk, e)
        return jnp.concatenate([a[:, :, :, :-2], a[:, :, :, 1:-1], a[:, :, :, 2:]], axis=4)

    ks, vs = windows(k), windows(v)
    qpos = jnp.arange(nb)[:, None] * blk + jnp.arange(blk)[None, :]
    kpos = jnp.arange(nb)[:, None] * blk - blk + jnp.arange(3 * blk)[None, :]
    rel = kpos[:, None, :] - qpos[:, :, None]
    valid = (jnp.abs(rel) <= half) & (kpos[:, None, :] >= 0) & (kpos[:, None, :] < seq_l)
    dist = (jnp.abs(rel) * dilation).astype(jnp.float32)
    scores = jnp.einsum('bdhine,bdhime->bdhinm', qs, ks) * (1.0 / math.sqrt(e))
    scores = scores - slopes.astype(jnp.float32)[:, None, None, None] * dist
    scores = jnp.where(valid, scores, -jnp.inf)
    m = jnp.max(scores, axis=-1, keepdims=True)
    p = jnp.exp(scores - m)
    den = jnp.sum(p, axis=-1, keepdims=True)
    o = jnp.einsum('bdhinm,bdhime->bdhine', p, vs) / den
    lse = (m + jnp.log(den))[..., 0]

    def from_strided(a):
        a = a.reshape(bsz, dilation, h, lp, *a.shape[5:])[:, :, :, :seq_l]
        a = jnp.moveaxis(a, 3, 1)
        return a.reshape(bsz, s, h, *a.shape[4:])

    return from_strided(o), from_strided(lse)


def attention_branch(q, k, v):
    bsz, s = q.shape[:2]
    q = q.reshape(bsz, s, ATTN_HEADS, ATTN_HEAD_DIM)
    k = k.reshape(bsz, s, ATTN_HEADS, ATTN_HEAD_DIM)
    v = v.reshape(bsz, s, ATTN_HEADS, ATTN_HEAD_DIM)
    slopes = jnp.asarray(2.0 ** (-8.0 * np.arange(1, ATTN_HEADS + 1) / ATTN_HEADS), dtype=jnp.float32)
    outs, lses = [], []
    for gi, (window, dilation) in enumerate(DIL_PATTERNS):
        hs = slice(gi * HEADS_PER_PATTERN, (gi + 1) * HEADS_PER_PATTERN)
        o, l = dilated_window_attention(q[:, :, hs], k[:, :, hs], v[:, :, hs], slopes[hs], dilation, window // (2 * dilation))
        outs.append(o)
        lses.append(l)
    o = jnp.stack(outs, axis=0)
    lse = jnp.stack(lses, axis=0)
    w = jax.nn.softmax(lse, axis=0)
    y = jnp.sum(w[..., None] * o, axis=0)
    return y.reshape(bsz, s, ATTN_OUT_WIDTH).astype(q.dtype)


def _fwd_setup_inputs(seed: int = 0) -> dict:
    key = jax.random.key(seed)
    ks = jax.random.split(key, 24)
    beta = (8.0 * DEPTH) ** -0.25
    nrm = lambda k, shape: jax.random.normal(k, shape, dtype=jnp.float32)

    def dt_bias(k):
        dt = jnp.exp(jax.random.uniform(k, (SSD_HEADS,), minval=math.log(1e-3), maxval=math.log(1e-1)))
        return dt + jnp.log(-jnp.expm1(-dt))

    return {
        "x": nrm(ks[0], (BATCH, SEQ, D_MODEL)),
        "w_in": nrm(ks[1], (D_MODEL, IN_COLS)) * D_MODEL ** -0.5,
        "b_gate": 0.02 * nrm(ks[2], (N_BRANCHES * D_MODEL,)),
        "conv_w": nrm(ks[3], (D_CONV, CONV_DIM)) * D_CONV ** -0.5,
        "conv_b": 0.02 * nrm(ks[4], (CONV_DIM,)),
        "dt_bias_f": dt_bias(ks[5]),
        "dt_bias_b": dt_bias(ks[6]),
        "a_log_f": jnp.log(jax.random.uniform(ks[7], (SSD_HEADS,), minval=1.0, maxval=16.0)),
        "a_log_b": jnp.log(jax.random.uniform(ks[8], (SSD_HEADS,), minval=1.0, maxval=16.0)),
        "d_skip": 1.0 + 0.1 * nrm(ks[9], (SSD_HEADS,)),
        "ssd_norm_w": 1.0 + 0.1 * nrm(ks[10], (D_INNER,)),
        "w_proj_ssd": nrm(ks[11], (D_INNER, D_MODEL)) * D_INNER ** -0.5 * beta,
        "w_proj_attn": nrm(ks[12], (ATTN_OUT_WIDTH, D_MODEL)) * ATTN_OUT_WIDTH ** -0.5 * beta,
        "w_out": nrm(ks[13], (D_MODEL, D_MODEL)) * D_MODEL ** -0.5 * beta,
        "ln1_g": 1.0 + 0.1 * nrm(ks[14], (D_MODEL,)),
        "ln1_b": 0.02 * nrm(ks[15], (D_MODEL,)),
        "w_up": nrm(ks[16], (D_MODEL, D_FF)) * D_MODEL ** -0.5 * beta,
        "w_down": nrm(ks[17], (D_FF, D_MODEL)) * D_FF ** -0.5 * beta,
        "ln2_g": 1.0 + 0.1 * nrm(ks[18], (D_MODEL,)),
        "ln2_b": 0.02 * nrm(ks[19], (D_MODEL,)),
    }


def _fwd_reference(x, w_in, b_gate, conv_w, conv_b, dt_bias_f, dt_bias_b, a_log_f, a_log_b, d_skip, ssd_norm_w,
              w_proj_ssd, w_proj_attn, w_out, ln1_g, ln1_b, w_up, w_down, ln2_g, ln2_b):
    alpha = (2.0 * DEPTH) ** 0.25
    h = x
    for _ in range(DEPTH):
        bsz, s = h.shape[:2]
        u = h @ w_in
        z, xbc, dt_f_raw, dt_b_raw, q, k, v, gate_logits = jnp.split(u, list(np.cumsum(IN_SPLITS)[:-1]), axis=-1)
        y_ssd = ssd_branch(z, xbc, dt_f_raw, dt_b_raw, conv_w, conv_b, dt_bias_f, dt_bias_b,
                           a_log_f, a_log_b, d_skip, ssd_norm_w) @ w_proj_ssd
        y_att = attention_branch(q, k, v) @ w_proj_attn
        gates = jax.nn.sigmoid(gate_logits + b_gate).reshape(bsz, s, N_BRANCHES, D_MODEL)
        mix = (gates[:, :, 0] * y_ssd + gates[:, :, 1] * y_att) @ w_out
        h = layer_norm(alpha * h + mix, ln1_g, ln1_b)
        f = jnp.square(jax.nn.relu(h @ w_up)) @ w_down
        h = layer_norm(alpha * h + f, ln2_g, ln2_b)
    return h


import jax as _jax
import jax.numpy as _jnp

TWIN_FORMAT = 'train_step'
FWD_PARAMS = ['x', 'w_in', 'b_gate', 'conv_w', 'conv_b', 'dt_bias_f', 'dt_bias_b', 'a_log_f', 'a_log_b', 'd_skip', 'ssd_norm_w', 'w_proj_ssd', 'w_proj_attn', 'w_out', 'ln1_g', 'ln1_b', 'w_up', 'w_down', 'ln2_g', 'ln2_b']
TWIN_WEIGHTS = ['w_in', 'b_gate', 'conv_w', 'conv_b', 'dt_bias_f', 'dt_bias_b', 'a_log_f', 'a_log_b', 'd_skip', 'ssd_norm_w', 'w_proj_ssd', 'w_proj_attn', 'w_out', 'ln1_g', 'ln1_b', 'w_up', 'w_down', 'ln2_g', 'ln2_b']
TWIN_DIFF_INPUT = 'x'
TWIN_INPUTS = ['x', 'w_in', 'b_gate', 'conv_w', 'conv_b', 'dt_bias_f', 'dt_bias_b', 'a_log_f', 'a_log_b', 'd_skip', 'ssd_norm_w', 'w_proj_ssd', 'w_proj_attn', 'w_out', 'ln1_g', 'ln1_b', 'w_up', 'w_down', 'ln2_g', 'ln2_b', 'loss_target', 'm_w_in', 'm_b_gate', 'm_conv_w', 'm_conv_b', 'm_dt_bias_f', 'm_dt_bias_b', 'm_a_log_f', 'm_a_log_b', 'm_d_skip', 'm_ssd_norm_w', 'm_w_proj_ssd', 'm_w_proj_attn', 'm_w_out', 'm_ln1_g', 'm_ln1_b', 'm_w_up', 'm_w_down', 'm_ln2_g', 'm_ln2_b', 'v_w_in', 'v_b_gate', 'v_conv_w', 'v_conv_b', 'v_dt_bias_f', 'v_dt_bias_b', 'v_a_log_f', 'v_a_log_b', 'v_d_skip', 'v_ssd_norm_w', 'v_w_proj_ssd', 'v_w_proj_attn', 'v_w_out', 'v_ln1_g', 'v_ln1_b', 'v_w_up', 'v_w_down', 'v_ln2_g', 'v_ln2_b']
TWIN_OUTPUTS = ['loss', 'grad_x', 'grad_w_in', 'grad_b_gate', 'grad_conv_w', 'grad_conv_b', 'grad_dt_bias_f', 'grad_dt_bias_b', 'grad_a_log_f', 'grad_a_log_b', 'grad_d_skip', 'grad_ssd_norm_w', 'grad_w_proj_ssd', 'grad_w_proj_attn', 'grad_w_out', 'grad_ln1_g', 'grad_ln1_b', 'grad_w_up', 'grad_w_down', 'grad_ln2_g', 'grad_ln2_b', 'delta_w_in', 'delta_b_gate', 'delta_conv_w', 'delta_conv_b', 'delta_dt_bias_f', 'delta_dt_bias_b', 'delta_a_log_f', 'delta_a_log_b', 'delta_d_skip', 'delta_ssd_norm_w', 'delta_w_proj_ssd', 'delta_w_proj_attn', 'delta_w_out', 'delta_ln1_g', 'delta_ln1_b', 'delta_w_up', 'delta_w_down', 'delta_ln2_g', 'delta_ln2_b', 'new_m_w_in', 'new_m_b_gate', 'new_m_conv_w', 'new_m_conv_b', 'new_m_dt_bias_f', 'new_m_dt_bias_b', 'new_m_a_log_f', 'new_m_a_log_b', 'new_m_d_skip', 'new_m_ssd_norm_w', 'new_m_w_proj_ssd', 'new_m_w_proj_attn', 'new_m_w_out', 'new_m_ln1_g', 'new_m_ln1_b', 'new_m_w_up', 'new_m_w_down', 'new_m_ln2_g', 'new_m_ln2_b', 'new_v_w_in', 'new_v_b_gate', 'new_v_conv_w', 'new_v_conv_b', 'new_v_dt_bias_f', 'new_v_dt_bias_b', 'new_v_a_log_f', 'new_v_a_log_b', 'new_v_d_skip', 'new_v_ssd_norm_w', 'new_v_w_proj_ssd', 'new_v_w_proj_attn', 'new_v_w_out', 'new_v_ln1_g', 'new_v_ln1_b', 'new_v_w_up', 'new_v_w_down', 'new_v_ln2_g', 'new_v_ln2_b']
TWIN_LEAF_KINDS = {'loss': 'loss', 'grad_x': 'grad_x', 'grad_w_in': 'grad_w', 'grad_b_gate': 'grad_w', 'grad_conv_w': 'grad_w', 'grad_conv_b': 'grad_w', 'grad_dt_bias_f': 'grad_w', 'grad_dt_bias_b': 'grad_w', 'grad_a_log_f': 'grad_w', 'grad_a_log_b': 'grad_w', 'grad_d_skip': 'grad_w', 'grad_ssd_norm_w': 'grad_w', 'grad_w_proj_ssd': 'grad_w', 'grad_w_proj_attn': 'grad_w', 'grad_w_out': 'grad_w', 'grad_ln1_g': 'grad_w', 'grad_ln1_b': 'grad_w', 'grad_w_up': 'grad_w', 'grad_w_down': 'grad_w', 'grad_ln2_g': 'grad_w', 'grad_ln2_b': 'grad_w', 'delta_w_in': 'delta_w', 'delta_b_gate': 'delta_w', 'delta_conv_w': 'delta_w', 'delta_conv_b': 'delta_w', 'delta_dt_bias_f': 'delta_w', 'delta_dt_bias_b': 'delta_w', 'delta_a_log_f': 'delta_w', 'delta_a_log_b': 'delta_w', 'delta_d_skip': 'delta_w', 'delta_ssd_norm_w': 'delta_w', 'delta_w_proj_ssd': 'delta_w', 'delta_w_proj_attn': 'delta_w', 'delta_w_out': 'delta_w', 'delta_ln1_g': 'delta_w', 'delta_ln1_b': 'delta_w', 'delta_w_up': 'delta_w', 'delta_w_down': 'delta_w', 'delta_ln2_g': 'delta_w', 'delta_ln2_b': 'delta_w', 'new_m_w_in': 'new_m', 'new_m_b_gate': 'new_m', 'new_m_conv_w': 'new_m', 'new_m_conv_b': 'new_m', 'new_m_dt_bias_f': 'new_m', 'new_m_dt_bias_b': 'new_m', 'new_m_a_log_f': 'new_m', 'new_m_a_log_b': 'new_m', 'new_m_d_skip': 'new_m', 'new_m_ssd_norm_w': 'new_m', 'new_m_w_proj_ssd': 'new_m', 'new_m_w_proj_attn': 'new_m', 'new_m_w_out': 'new_m', 'new_m_ln1_g': 'new_m', 'new_m_ln1_b': 'new_m', 'new_m_w_up': 'new_m', 'new_m_w_down': 'new_m', 'new_m_ln2_g': 'new_m', 'new_m_ln2_b': 'new_m', 'new_v_w_in': 'new_v', 'new_v_b_gate': 'new_v', 'new_v_conv_w': 'new_v', 'new_v_conv_b': 'new_v', 'new_v_dt_bias_f': 'new_v', 'new_v_dt_bias_b': 'new_v', 'new_v_a_log_f': 'new_v', 'new_v_a_log_b': 'new_v', 'new_v_d_skip': 'new_v', 'new_v_ssd_norm_w': 'new_v', 'new_v_w_proj_ssd': 'new_v', 'new_v_w_proj_attn': 'new_v', 'new_v_w_out': 'new_v', 'new_v_ln1_g': 'new_v', 'new_v_ln1_b': 'new_v', 'new_v_w_up': 'new_v', 'new_v_w_down': 'new_v', 'new_v_ln2_g': 'new_v', 'new_v_ln2_b': 'new_v'}


def _forward(args):
    return _fwd_reference(*[args[k] for k in FWD_PARAMS])


def _output_shape():
    out = _jax.eval_shape(lambda: _forward(_fwd_setup_inputs(0)))
    return out.shape, out.dtype

N_MICROBATCH = 1
ADAM_LR = 0.001
ADAM_B1 = 0.9
ADAM_B2 = 0.999
ADAM_EPS = 1e-08
ADAM_WD = 0.01
ADAM_STEP = 10
PER_EXAMPLE_BATCH_AXIS = {'x': 0, 'loss_target': 0}
SHARED_INPUTS = []
_WEIGHT_DTYPES = {'w_in': _jnp.float32, 'b_gate': _jnp.float32, 'conv_w': _jnp.float32, 'conv_b': _jnp.float32, 'dt_bias_f': _jnp.float32, 'dt_bias_b': _jnp.float32, 'a_log_f': _jnp.float32, 'a_log_b': _jnp.float32, 'd_skip': _jnp.float32, 'ssd_norm_w': _jnp.float32, 'w_proj_ssd': _jnp.float32, 'w_proj_attn': _jnp.float32, 'w_out': _jnp.float32, 'ln1_g': _jnp.float32, 'ln1_b': _jnp.float32, 'w_up': _jnp.float32, 'w_down': _jnp.float32, 'ln2_g': _jnp.float32, 'ln2_b': _jnp.float32}
MOMENT_SCALE = {'w_in': 2.185680e-02, 'b_gate': 1.435889e-02, 'conv_w': 2.619591e-02, 'conv_b': 5.131956e-02, 'dt_bias_f': 4.798931e-02, 'dt_bias_b': 4.598499e-02, 'a_log_f': 1.076622e-01, 'a_log_b': 9.876688e-02, 'd_skip': 9.789241e-02, 'ssd_norm_w': 3.861641e-02, 'w_proj_ssd': 8.293135e-02, 'w_proj_attn': 1.678349e-02, 'w_out': 8.348510e-02, 'ln1_g': 1.190623e+01, 'ln1_b': 1.030002e+00, 'w_up': 5.663964e-02, 'w_down': 1.519413e-01, 'ln2_g': 6.531958e+01, 'ln2_b': 5.430254e+00}


def _to_microbatches(a, axis):
    t = _jnp.moveaxis(a, axis, 0)
    t = t.reshape((N_MICROBATCH, t.shape[0] // N_MICROBATCH) + t.shape[1:])
    return _jnp.moveaxis(t, 1, axis + 1)


def setup_inputs(seed: int = 0) -> dict:
    inp = _fwd_setup_inputs(seed)
    key = _jax.random.fold_in(_jax.random.key(seed), 7919)
    shape, _ = _output_shape()
    out = dict(inp)
    out["loss_target"] = _jax.random.normal(_jax.random.fold_in(key, 0), shape, _jnp.float32)
    for i, name in enumerate(TWIN_WEIGHTS):
        w = inp[name].astype(_jnp.float32)
        if MOMENT_SCALE is None:
            s = _jnp.sqrt(_jnp.mean(_jnp.square(w)) + 1e-30)
        else:
            s = MOMENT_SCALE[name]
        km, kv = _jax.random.split(_jax.random.fold_in(key, i + 1))
        out[name] = w
        out["m_" + name] = s * _jax.random.normal(km, w.shape, _jnp.float32)
        out["v_" + name] = (s * s) * _jax.random.uniform(kv, w.shape, _jnp.float32, 0.5, 1.5)
    if N_MICROBATCH > 1:
        for name, axis in PER_EXAMPLE_BATCH_AXIS.items():
            out[name] = _to_microbatches(out[name], axis)
    return {'x': out['x'], 'w_in': out['w_in'], 'b_gate': out['b_gate'], 'conv_w': out['conv_w'], 'conv_b': out['conv_b'], 'dt_bias_f': out['dt_bias_f'], 'dt_bias_b': out['dt_bias_b'], 'a_log_f': out['a_log_f'], 'a_log_b': out['a_log_b'], 'd_skip': out['d_skip'], 'ssd_norm_w': out['ssd_norm_w'], 'w_proj_ssd': out['w_proj_ssd'], 'w_proj_attn': out['w_proj_attn'], 'w_out': out['w_out'], 'ln1_g': out['ln1_g'], 'ln1_b': out['ln1_b'], 'w_up': out['w_up'], 'w_down': out['w_down'], 'ln2_g': out['ln2_g'], 'ln2_b': out['ln2_b'], 'loss_target': out['loss_target'], 'm_w_in': out['m_w_in'], 'm_b_gate': out['m_b_gate'], 'm_conv_w': out['m_conv_w'], 'm_conv_b': out['m_conv_b'], 'm_dt_bias_f': out['m_dt_bias_f'], 'm_dt_bias_b': out['m_dt_bias_b'], 'm_a_log_f': out['m_a_log_f'], 'm_a_log_b': out['m_a_log_b'], 'm_d_skip': out['m_d_skip'], 'm_ssd_norm_w': out['m_ssd_norm_w'], 'm_w_proj_ssd': out['m_w_proj_ssd'], 'm_w_proj_attn': out['m_w_proj_attn'], 'm_w_out': out['m_w_out'], 'm_ln1_g': out['m_ln1_g'], 'm_ln1_b': out['m_ln1_b'], 'm_w_up': out['m_w_up'], 'm_w_down': out['m_w_down'], 'm_ln2_g': out['m_ln2_g'], 'm_ln2_b': out['m_ln2_b'], 'v_w_in': out['v_w_in'], 'v_b_gate': out['v_b_gate'], 'v_conv_w': out['v_conv_w'], 'v_conv_b': out['v_conv_b'], 'v_dt_bias_f': out['v_dt_bias_f'], 'v_dt_bias_b': out['v_dt_bias_b'], 'v_a_log_f': out['v_a_log_f'], 'v_a_log_b': out['v_a_log_b'], 'v_d_skip': out['v_d_skip'], 'v_ssd_norm_w': out['v_ssd_norm_w'], 'v_w_proj_ssd': out['v_w_proj_ssd'], 'v_w_proj_attn': out['v_w_proj_attn'], 'v_w_out': out['v_w_out'], 'v_ln1_g': out['v_ln1_g'], 'v_ln1_b': out['v_ln1_b'], 'v_w_up': out['v_w_up'], 'v_w_down': out['v_w_down'], 'v_ln2_g': out['v_ln2_g'], 'v_ln2_b': out['v_ln2_b']}


def _loss(weights, diff, rest, loss_target):
    with _jax.named_scope("forward"):
        args = {**rest, TWIN_DIFF_INPUT: diff, **{k: w.astype(_WEIGHT_DTYPES[k]) for k, w in weights.items()}}
        y = _forward(args)
    with _jax.named_scope("loss_head"):
        err = _jnp.square(y.astype(_jnp.float32) - loss_target)
        return 0.5 * _jnp.sum(_jnp.mean(err, axis=-1)) if err.ndim else 0.5 * err


def _adamw(w, g, m, v):
    m = ADAM_B1 * m + (1.0 - ADAM_B1) * g
    v = ADAM_B2 * v + (1.0 - ADAM_B2) * _jnp.square(g)
    m_hat = m / (1.0 - ADAM_B1 ** ADAM_STEP)
    v_hat = v / (1.0 - ADAM_B2 ** ADAM_STEP)
    delta = -ADAM_LR * (m_hat / (_jnp.sqrt(v_hat) + ADAM_EPS) + ADAM_WD * w)
    return delta, m, v


def reference(x, w_in, b_gate, conv_w, conv_b, dt_bias_f, dt_bias_b, a_log_f, a_log_b, d_skip, ssd_norm_w, w_proj_ssd, w_proj_attn, w_out, ln1_g, ln1_b, w_up, w_down, ln2_g, ln2_b, loss_target, m_w_in, m_b_gate, m_conv_w, m_conv_b, m_dt_bias_f, m_dt_bias_b, m_a_log_f, m_a_log_b, m_d_skip, m_ssd_norm_w, m_w_proj_ssd, m_w_proj_attn, m_w_out, m_ln1_g, m_ln1_b, m_w_up, m_w_down, m_ln2_g, m_ln2_b, v_w_in, v_b_gate, v_conv_w, v_conv_b, v_dt_bias_f, v_dt_bias_b, v_a_log_f, v_a_log_b, v_d_skip, v_ssd_norm_w, v_w_proj_ssd, v_w_proj_attn, v_w_out, v_ln1_g, v_ln1_b, v_w_up, v_w_down, v_ln2_g, v_ln2_b):
    given = dict(x=x, w_in=w_in, b_gate=b_gate, conv_w=conv_w, conv_b=conv_b, dt_bias_f=dt_bias_f, dt_bias_b=dt_bias_b, a_log_f=a_log_f, a_log_b=a_log_b, d_skip=d_skip, ssd_norm_w=ssd_norm_w, w_proj_ssd=w_proj_ssd, w_proj_attn=w_proj_attn, w_out=w_out, ln1_g=ln1_g, ln1_b=ln1_b, w_up=w_up, w_down=w_down, ln2_g=ln2_g, ln2_b=ln2_b, loss_target=loss_target, m_w_in=m_w_in, m_b_gate=m_b_gate, m_conv_w=m_conv_w, m_conv_b=m_conv_b, m_dt_bias_f=m_dt_bias_f, m_dt_bias_b=m_dt_bias_b, m_a_log_f=m_a_log_f, m_a_log_b=m_a_log_b, m_d_skip=m_d_skip, m_ssd_norm_w=m_ssd_norm_w, m_w_proj_ssd=m_w_proj_ssd, m_w_proj_attn=m_w_proj_attn, m_w_out=m_w_out, m_ln1_g=m_ln1_g, m_ln1_b=m_ln1_b, m_w_up=m_w_up, m_w_down=m_w_down, m_ln2_g=m_ln2_g, m_ln2_b=m_ln2_b, v_w_in=v_w_in, v_b_gate=v_b_gate, v_conv_w=v_conv_w, v_conv_b=v_conv_b, v_dt_bias_f=v_dt_bias_f, v_dt_bias_b=v_dt_bias_b, v_a_log_f=v_a_log_f, v_a_log_b=v_a_log_b, v_d_skip=v_d_skip, v_ssd_norm_w=v_ssd_norm_w, v_w_proj_ssd=v_w_proj_ssd, v_w_proj_attn=v_w_proj_attn, v_w_out=v_w_out, v_ln1_g=v_ln1_g, v_ln1_b=v_ln1_b, v_w_up=v_w_up, v_w_down=v_w_down, v_ln2_g=v_ln2_g, v_ln2_b=v_ln2_b)
    weights = {n: given[n] for n in TWIN_WEIGHTS}
    shared = {n: given[n] for n in SHARED_INPUTS}
    per_example = {n: given[n] for n in ['x']}
    grad_fn = _jax.value_and_grad(_loss, argnums=(0, 1))

    def one_microbatch(ex, loss_target):
        ex = dict(ex)
        diff = ex.pop(TWIN_DIFF_INPUT)
        return grad_fn(weights, diff, {**shared, **ex}, loss_target)

    if N_MICROBATCH == 1:
        loss, (grad_w, grad_x) = one_microbatch(per_example, given["loss_target"])
    else:
        def body(carry, xs):
            loss_sum, grad_sum = carry
            l_k, (gw_k, gx_k) = one_microbatch(xs[0], xs[1])
            with _jax.named_scope("update"):
                return (loss_sum + l_k, _jax.tree.map(_jnp.add, grad_sum, gw_k)), gx_k

        init = (_jnp.zeros((), _jnp.float32), _jax.tree.map(_jnp.zeros_like, weights))
        (loss, grad_w), grad_x = _jax.lax.scan(body, init, (per_example, given["loss_target"]))
    with _jax.named_scope("update"):
        delta_w, new_m, new_v = {}, {}, {}
        for n in TWIN_WEIGHTS:
            delta_w[n], new_m[n], new_v[n] = _adamw(weights[n], grad_w[n], given["m_" + n], given["v_" + n])
    return (loss, grad_x, *[grad_w[n] for n in TWIN_WEIGHTS], *[delta_w[n] for n in TWIN_WEIGHTS],
            *[new_m[n] for n in TWIN_WEIGHTS], *[new_v[n] for n in TWIN_WEIGHTS])
```

```python
import functools
import math

import jax
import jax.numpy as jnp
import numpy as np
from jax import lax
from jax.experimental import pallas as pl
from jax.experimental.pallas import tpu as pltpu

f32 = jnp.float32
bf16 = jnp.bfloat16
MXU_DTYPE = jnp.bfloat16

N_DEV = 8
D_MODEL = 1024
D_INNER = 2048
SSD_HEADS = 32
SSD_HEAD_DIM = 64
SSD_GROUPS = 4
D_STATE = 128
D_CONV = 5
CHUNK = 128
CONV_DIM = D_INNER + 2 * SSD_GROUPS * D_STATE
NORM_EPS = 1e-5
ATTN_HEAD_DIM = 64
DIL_PATTERNS = ((128, 1), (512, 4), (2048, 16))
HEADS_PER_PATTERN = 4
ATTN_HEADS = 12
ATTN_WIDTH = 768
ATTN_OUT = 256
D_FF = 4096
ALPHA = 2.0 ** 0.25
IN_SPLITS = (D_INNER, CONV_DIM, SSD_HEADS, SSD_HEADS, ATTN_WIDTH, ATTN_WIDTH, ATTN_WIDTH, 2 * D_MODEL)
IN_COLS = sum(IN_SPLITS)
ADAM_LR, ADAM_B1, ADAM_B2, ADAM_EPS, ADAM_WD, ADAM_STEP = 0.001, 0.9, 0.999, 1e-08, 0.01, 10
NEG_BIG = -1e30
VMEM_LIMIT = 56 * 1024 * 1024
MESH = pl.DeviceIdType.MESH

BIG = ("w_in", "w_proj_ssd", "w_proj_attn", "w_out", "w_up", "w_down", "conv_w")
SMALL = ("b_gate", "conv_b", "dt_bias_f", "dt_bias_b", "a_log_f", "a_log_b", "d_skip", "ssd_norm_w",
         "ln1_g", "ln1_b", "ln2_g", "ln2_b")
WEIGHTS = ("w_in", "b_gate", "conv_w", "conv_b", "dt_bias_f", "dt_bias_b", "a_log_f", "a_log_b", "d_skip",
           "ssd_norm_w", "w_proj_ssd", "w_proj_attn", "w_out", "ln1_g", "ln1_b", "w_up", "w_down", "ln2_g", "ln2_b")
BIG_SHAPES = {"w_in": ((D_MODEL, IN_COLS), 1), "w_proj_ssd": ((D_INNER, D_MODEL), 0), "w_proj_attn": ((ATTN_OUT, D_MODEL), 1),
              "w_out": ((D_MODEL, D_MODEL), 0), "w_up": ((D_MODEL, D_FF), 1), "w_down": ((D_FF, D_MODEL), 0),
              "conv_w": ((D_CONV, CONV_DIM), 1)}
SMALL_SIZES = {"b_gate": 2 * D_MODEL, "conv_b": CONV_DIM, "dt_bias_f": 32, "dt_bias_b": 32, "a_log_f": 32, "a_log_b": 32,
               "d_skip": 32, "ssd_norm_w": D_INNER, "ln1_g": D_MODEL, "ln1_b": D_MODEL, "ln2_g": D_MODEL, "ln2_b": D_MODEL}
PACK_ELEMS = sum(int(np.prod(s)) // N_DEV for s, _ in BIG_SHAPES.values()) + sum(SMALL_SIZES.values()) + 1
PACK_TILE = 256
PACK_ROWS = -(-PACK_ELEMS // (1024 * PACK_TILE)) * PACK_TILE


def _cparams(sem=None, **kw):
    return pltpu.CompilerParams(dimension_semantics=sem, vmem_limit_bytes=VMEM_LIMIT, **kw)


def _mx(v):
    return v.astype(MXU_DTYPE)


def _dot(a, b):
    return jnp.dot(_mx(a), _mx(b), preferred_element_type=f32)


def _dot_nt(a, b):
    return lax.dot_general(_mx(a), _mx(b), (((1,), (1,)), ((), ())), preferred_element_type=f32)


def _dot_tn(a, b):
    return lax.dot_general(_mx(a), _mx(b), (((0,), (0,)), ((), ())), preferred_element_type=f32)


def _dot_exact(a, b):
    return jnp.dot(a, b, precision=lax.Precision.HIGHEST, preferred_element_type=f32)


def _sigmoid(v):
    return 1.0 / (1.0 + jnp.exp(-v))


def _pick(n, prefs):
    for p in prefs:
        if n % p == 0:
            return p
    return n


def mm_nn(a, b, name, out_dtype=f32, acc_in=None, acc_scale=1.0, tm=512):
    m, k = a.shape
    n = b.shape[1]
    tn = _pick(n, (1024, 768, 512, 384, 256, 128))
    tk = _pick(k, (2048, 1536, 1024, 512, 256, 128))
    nk = k // tk

    def body(*refs):
        if acc_in is None:
            a_ref, b_ref, o_ref, acc_ref = refs
        else:
            a_ref, b_ref, c_ref, o_ref, acc_ref = refs
        kk = pl.program_id(2)

        @pl.when(kk == 0)
        def _():
            acc_ref[...] = jnp.zeros_like(acc_ref)

        acc_ref[...] += _dot(a_ref[...], b_ref[...])

        @pl.when(kk == nk - 1)
        def _():
            r = acc_ref[...]
            if acc_in is not None:
                r = r + acc_scale * c_ref[...]
            o_ref[...] = r.astype(o_ref.dtype)

    in_specs = [pl.BlockSpec((tm, tk), lambda i, j, kk: (i, kk)), pl.BlockSpec((tk, tn), lambda i, j, kk: (kk, j))]
    args = [a, b]
    if acc_in is not None:
        in_specs.append(pl.BlockSpec((tm, tn), lambda i, j, kk: (i, j)))
        args.append(acc_in)
    return pl.pallas_call(
        body, name=name, grid=(m // tm, n // tn, nk), in_specs=in_specs,
        out_specs=pl.BlockSpec((tm, tn), lambda i, j, kk: (i, j)),
        out_shape=jax.ShapeDtypeStruct((m, n), out_dtype), scratch_shapes=[pltpu.VMEM((tm, tn), f32)],
        compiler_params=_cparams(("parallel", "parallel", "arbitrary")))(*args)


def mm_tn(a, b, name, tk=1024):
    t, m = a.shape
    n = b.shape[1]
    tm = _pick(m, (512, 256, 128))
    tn = _pick(n, (1024, 768, 512, 384, 256, 128))
    nk = t // tk

    def body(a_ref, b_ref, o_ref):
        kk = pl.program_id(2)

        @pl.when(kk == 0)
        def _():
            o_ref[...] = jnp.zeros_like(o_ref)

        o_ref[...] += _dot_tn(a_ref[...], b_ref[...])

    return pl.pallas_call(
        body, name=name, grid=(m // tm, n // tn, nk),
        in_specs=[pl.BlockSpec((tk, tm), lambda i, j, kk: (kk, i)), pl.BlockSpec((tk, tn), lambda i, j, kk: (kk, j))],
        out_specs=pl.BlockSpec((tm, tn), lambda i, j, kk: (i, j)),
        out_shape=jax.ShapeDtypeStruct((m, n), f32),
        compiler_params=_cparams(("parallel", "parallel", "arbitrary")))(a, b)


def _halo_specs(tb, cb, nt):
    r = tb // 8
    return [pl.BlockSpec((8, cb), lambda j, i: (jnp.maximum(i * r - 1, 0), j)),
            pl.BlockSpec((tb, cb), lambda j, i: (i, j)),
            pl.BlockSpec((8, cb), lambda j, i: (jnp.minimum((i + 1) * r, nt * r - 1), j))]


def _with_halo(prev_ref, own_ref, next_ref, i, nt):
    prev = jnp.where(i > 0, prev_ref[...].astype(f32), 0.0)
    nxt = jnp.where(i < nt - 1, next_ref[...].astype(f32), 0.0)
    return jnp.concatenate([prev, own_ref[...].astype(f32), nxt], axis=0)


def _shifted(xcat, s, tb):
    n = xcat.shape[0]
    return pltpu.roll(xcat, (-s) % n, 0)[8:8 + tb]


def conv_fwd(xbc, w8, b_row, tb=512, cb=512):
    t, c = xbc.shape
    nt = t // tb

    def body(prev_ref, own_ref, next_ref, w_ref, b_ref, o_ref):
        i = pl.program_id(1)
        xcat = _with_halo(prev_ref, own_ref, next_ref, i, nt)
        w = w_ref[...]
        pre = b_ref[...] + w[0:1] * _shifted(xcat, -2, tb)
        for k in range(1, D_CONV):
            pre = pre + w[k:k + 1] * _shifted(xcat, k - 2, tb)
        o_ref[...] = pre * _sigmoid(pre)

    return pl.pallas_call(
        body, name="conv_fwd", grid=(c // cb, nt),
        in_specs=_halo_specs(tb, cb, nt) + [pl.BlockSpec((8, cb), lambda j, i: (0, j)), pl.BlockSpec((1, cb), lambda j, i: (0, j))],
        out_specs=pl.BlockSpec((tb, cb), lambda j, i: (i, j)), out_shape=jax.ShapeDtypeStruct((t, c), f32),
        compiler_params=_cparams(("parallel", "parallel")))(xbc, xbc, xbc, w8, b_row)


def conv_bwd(xbc_part, grads, scales, w8, b_row, name, tb=512, cb=512):
    t, c = xbc_part.shape
    nt = t // tb
    ng = len(grads)
    has_scale = [s is not None for s in scales]

    def body(*refs):
        i = pl.program_id(1)
        xr = refs[0:3]
        gr = [refs[3 + 3 * q: 6 + 3 * q] for q in range(ng)]
        pos = 3 + 3 * ng
        sr = []
        for q in range(ng):
            if has_scale[q]:
                sr.append(refs[pos])
                pos += 1
            else:
                sr.append(None)
        w_ref, b_ref, dx_ref, dw_ref, db_ref = refs[pos:pos + 5]
        xcat = _with_halo(*xr, i, nt)
        gcat = None
        for q in range(ng):
            gq = _with_halo(*gr[q], i, nt)
            if sr[q] is not None:
                gq = gq * sr[q][...]
            gcat = gq if gcat is None else gcat + gq
        w = w_ref[...]
        n = tb + 16
        pre = b_ref[...] + w[0:1] * pltpu.roll(xcat, 2, 0)
        for k in range(1, D_CONV):
            pre = pre + w[k:k + 1] * pltpu.roll(xcat, (2 - k) % n, 0)
        sg = _sigmoid(pre)
        dpre = gcat * sg * (1.0 + pre * (1.0 - sg))
        dx = w[0:1] * _shifted(dpre, 2, tb)
        for k in range(1, D_CONV):
            dx = dx + w[k:k + 1] * _shifted(dpre, 2 - k, tb)
        dx_ref[...] = dx.astype(dx_ref.dtype)
        dp_own = dpre[8:8 + tb]
        rows = [jnp.sum(dp_own * _shifted(xcat, k - 2, tb), axis=0, keepdims=True) for k in range(D_CONV)]
        dw = jnp.concatenate(rows + [jnp.zeros((8 - D_CONV, cb), f32)], axis=0)
        db = jnp.sum(dp_own, axis=0, keepdims=True)

        @pl.when(i == 0)
        def _():
            dw_ref[...] = jnp.zeros_like(dw_ref)
            db_ref[...] = jnp.zeros_like(db_ref)

        dw_ref[...] += dw
        db_ref[...] += db

    in_specs = _halo_specs(tb, cb, nt)
    args = [xbc_part] * 3
    for g in grads:
        in_specs += _halo_specs(tb, cb, nt)
        args += [g] * 3
    for s in scales:
        if s is not None:
            in_specs.append(pl.BlockSpec((1, cb), lambda j, i: (0, j)))
            args.append(s)
    in_specs += [pl.BlockSpec((8, cb), lambda j, i: (0, j)), pl.BlockSpec((1, cb), lambda j, i: (0, j))]
    args += [w8, b_row]
    return pl.pallas_call(
        body, name=name, grid=(c // cb, nt), in_specs=in_specs,
        out_specs=[pl.BlockSpec((tb, cb), lambda j, i: (i, j)), pl.BlockSpec((8, cb), lambda j, i: (0, j)),
                   pl.BlockSpec((1, cb), lambda j, i: (0, j))],
        out_shape=[jax.ShapeDtypeStruct((t, c), bf16), jax.ShapeDtypeStruct((8, c), f32), jax.ShapeDtypeStruct((1, c), f32)],
        compiler_params=_cparams(("parallel", "arbitrary")))(*args)


def dt_fwd(u_dt, bias_row, tb=1024):
    t = u_dt.shape[0]

    def body(u_ref, b_ref, o_ref):
        v = u_ref[...] + b_ref[...]
        sp = jnp.maximum(v, 0.0) + jnp.log(1.0 + jnp.exp(-jnp.abs(v)))
        lane = lax.broadcasted_iota(jnp.int32, v.shape, 1)
        o_ref[...] = jnp.where((lane & 127) < SSD_HEADS, sp, 0.0)

    return pl.pallas_call(
        body, name="dt_fwd", grid=(t // tb,),
        in_specs=[pl.BlockSpec((tb, 256), lambda i: (i, 0)), pl.BlockSpec((1, 256), lambda i: (0, 0))],
        out_specs=pl.BlockSpec((tb, 256), lambda i: (i, 0)), out_shape=jax.ShapeDtypeStruct((t, 256), f32),
        compiler_params=_cparams(("parallel",)))(u_dt, bias_row)


def dt_bwd(ddt_f, ddt_b, u_dt, bias_row, tb=1024):
    t = u_dt.shape[0]

    def body(gf_ref, gb_ref, u_ref, b_ref, du_ref, db_ref):
        g = jnp.concatenate([jnp.sum(gf_ref[...], axis=0), jnp.sum(gb_ref[...], axis=0)], axis=1)
        du = g * _sigmoid(u_ref[...] + b_ref[...])
        du_ref[...] = du.astype(du_ref.dtype)

        @pl.when(pl.program_id(0) == 0)
        def _():
            db_ref[...] = jnp.zeros_like(db_ref)

        db_ref[...] += jnp.sum(du, axis=0, keepdims=True)

    return pl.pallas_call(
        body, name="dt_bwd", grid=(t // tb,),
        in_specs=[pl.BlockSpec((4, tb, 128), lambda i: (0, i, 0)), pl.BlockSpec((4, tb, 128), lambda i: (0, i, 0)),
                  pl.BlockSpec((tb, 256), lambda i: (i, 0)), pl.BlockSpec((1, 256), lambda i: (0, 0))],
        out_specs=[pl.BlockSpec((tb, 256), lambda i: (i, 0)), pl.BlockSpec((1, 256), lambda i: (0, 0))],
        out_shape=[jax.ShapeDtypeStruct((t, 256), bf16), jax.ShapeDtypeStruct((1, 256), f32)],
        compiler_params=_cparams(("arbitrary",)))(ddt_f, ddt_b, u_dt, bias_row)


def _ssd_common(dt_blk, a_row, reverse):
    row = lax.broadcasted_iota(jnp.int32, (CHUNK, CHUNK), 0)
    col = lax.broadcasted_iota(jnp.int32, (CHUNK, CHUNK), 1)
    mask = (row <= col) if reverse else (row >= col)
    tri = mask.astype(f32)
    a = dt_blk * a_row
    acs = _dot_exact(tri, a)
    atot = jnp.sum(a, axis=0, keepdims=True)
    return mask, tri, a, acs, atot, col


def _lane_col(mat, lane_idx, h):
    return jnp.sum(jnp.where(lane_idx == h, mat, 0.0), axis=1, keepdims=True)


def ssd_fwd(xbc_c, dt2, a_rows, reverse, name):
    t = xbc_c.shape[0]
    nc = t // CHUNK
    d_off = 1 if reverse else 0

    def cidx(c):
        return nc - 1 - c if reverse else c

    def body(x_ref, b_ref, c_ref, dt_ref, a_ref, y_ref, hp_ref, h_scr, acst_scr):
        g = pl.program_id(0)
        c = pl.program_id(1)

        @pl.when(c == 0)
        def _():
            h_scr[...] = jnp.zeros_like(h_scr)

        dt_blk = dt_ref[...]
        mask, tri, a, acs, atot, lane = _ssd_common(dt_blk, a_ref[...], reverse)
        acst_scr[...] = acs.T
        bm = b_ref[...]
        cm = c_ref[...]
        cb = _dot_nt(cm, bm)
        half = lane >= SSD_HEAD_DIM
        sub_half = lax.broadcasted_iota(jnp.int32, (CHUNK, 1), 0) >= SSD_HEAD_DIM
        for j in range(4):
            x = x_ref[:, 128 * j:128 * (j + 1)]
            cols, dts, tots = [], [], []
            y = None
            for e in range(2):
                h = 8 * g + 2 * j + e
                col_h = _lane_col(acs, lane, h)
                row_h = acst_scr[pl.ds(h, 1), :]
                dt_h = _lane_col(dt_blk, lane, h)
                lmat = jnp.where(mask, jnp.exp(jnp.where(mask, col_h - row_h, 0.0)), 0.0)
                xdt_e = jnp.where(half == (e == 1), x * dt_h, 0.0)
                ye = _dot(cb * lmat, xdt_e)
                y = ye if y is None else y + ye
                cols.append(col_h)
                dts.append(dt_h)
                tots.append(jnp.sum(jnp.where(lane[0:1] == h, atot, 0.0), axis=1, keepdims=True))
            hp = h_scr[j]
            hp_ref[0, j] = hp
            ecol = jnp.where(half, jnp.exp(cols[1]), jnp.exp(cols[0]))
            y = y + _dot_nt(cm, hp) * ecol
            y_ref[:, 128 * j:128 * (j + 1)] = y
            dec = jnp.where(half, jnp.exp(tots[1] - cols[1]), jnp.exp(tots[0] - cols[0]))
            xdt = x * jnp.where(half, dts[1], dts[0])
            s_new = _dot_tn(xdt * dec, bm)
            cd = jnp.where(sub_half, jnp.exp(tots[1]), jnp.exp(tots[0]))
            h_scr[j] = cd * hp + s_new

    return pl.pallas_call(
        body, name=name, grid=(SSD_GROUPS, nc),
        in_specs=[pl.BlockSpec((CHUNK, 512), lambda g, c: (cidx(c), g)),
                  pl.BlockSpec((CHUNK, 128), lambda g, c: (cidx(c), 16 + g)),
                  pl.BlockSpec((CHUNK, 128), lambda g, c: (cidx(c), 20 + g)),
                  pl.BlockSpec((CHUNK, 128), lambda g, c: (cidx(c), d_off)),
                  pl.BlockSpec((1, 128), lambda g, c: (0, d_off))],
        out_specs=[pl.BlockSpec((CHUNK, 512), lambda g, c: (cidx(c), g)),
                   pl.BlockSpec((1, 4, 128, 128), lambda g, c: (cidx(c), g, 0, 0))],
        out_shape=[jax.ShapeDtypeStruct((t, D_INNER), f32), jax.ShapeDtypeStruct((nc, 16, 128, 128), f32)],
        scratch_shapes=[pltpu.VMEM((4, 128, 128), f32), pltpu.VMEM((CHUNK, CHUNK), f32)],
        compiler_params=_cparams(("parallel", "arbitrary")))(xbc_c, xbc_c, xbc_c, dt2, a_rows)


def ssd_bwd(xbc_c, dt2, a_rows, dy, hprev, reverse, name):
    t = xbc_c.shape[0]
    nc = t // CHUNK
    d_off = 1 if reverse else 0

    def cidx(c):
        return c if reverse else nc - 1 - c

    def body(x_ref, b_ref, c_ref, dt_ref, a_ref, dy_ref, hp_ref, dx_ref, db_ref, dc_ref, ddt_ref, da_ref,
             dh_scr, acst_scr):
        g = pl.program_id(0)
        c = pl.program_id(1)

        @pl.when(c == 0)
        def _():
            dh_scr[...] = jnp.zeros_like(dh_scr)
            da_ref[...] = jnp.zeros_like(da_ref)

        dt_blk = dt_ref[...]
        a_row = a_ref[...]
        mask, tri, a, acs, atot, lane = _ssd_common(dt_blk, a_row, reverse)
        acst_scr[...] = acs.T
        sub = lax.broadcasted_iota(jnp.int32, (CHUNK, CHUNK), 0)
        bm = b_ref[...]
        cm = c_ref[...]
        cb = _dot_nt(cm, bm)
        half = lane >= SSD_HEAD_DIM
        sub_half = sub[:, 0:1] >= SSD_HEAD_DIM
        dcb = jnp.zeros((CHUNK, CHUNK), f32)
        dacs = jnp.zeros((CHUNK, CHUNK), f32)
        dacs_t = jnp.zeros((CHUNK, CHUNK), f32)
        dtot = jnp.zeros((1, CHUNK), f32)
        ddt_x = jnp.zeros((CHUNK, CHUNK), f32)
        dbm = jnp.zeros((CHUNK, D_STATE), f32)
        dcm = jnp.zeros((CHUNK, D_STATE), f32)
        for j in range(4):
            x = x_ref[:, 128 * j:128 * (j + 1)]
            dyp = dy_ref[:, 128 * j:128 * (j + 1)]
            hp = hp_ref[0, j]
            dhn = dh_scr[j]
            cols, dts, tots, hs = [], [], [], []
            dxdt = None
            for e in range(2):
                h = 8 * g + 2 * j + e
                sel = half == (e == 1)
                col_h = _lane_col(acs, lane, h)
                row_h = acst_scr[pl.ds(h, 1), :]
                dt_h = _lane_col(dt_blk, lane, h)
                lmat = jnp.where(mask, jnp.exp(jnp.where(mask, col_h - row_h, 0.0)), 0.0)
                xdt_e = jnp.where(sel, x * dt_h, 0.0)
                dy_e = jnp.where(sel, dyp, 0.0)
                ml = _dot_nt(dy_e, xdt_e) * lmat
                dcb = dcb + ml
                w = ml * cb
                dacs = dacs + jnp.where(lane == h, jnp.sum(w, axis=1, keepdims=True), 0.0)
                dacs_t = dacs_t - jnp.where(sub == h, jnp.sum(w, axis=0, keepdims=True), 0.0)
                de = _dot_tn(cb * lmat, dy_e)
                dxdt = de if dxdt is None else dxdt + de
                cols.append(col_h)
                dts.append(dt_h)
                tots.append(jnp.sum(jnp.where(lane[0:1] == h, atot, 0.0), axis=1, keepdims=True))
                hs.append(h)
            ecol = jnp.where(half, jnp.exp(cols[1]), jnp.exp(cols[0]))
            dec = jnp.where(half, jnp.exp(tots[1] - cols[1]), jnp.exp(tots[0] - cols[0]))
            cd = jnp.where(sub_half, jnp.exp(tots[1]), jnp.exp(tots[0]))
            dtp = jnp.where(half, dts[1], dts[0])
            xdt = x * dtp
            yoff = _dot_nt(cm, hp) * ecol
            dye = dyp * ecol
            dcm = dcm + _dot(dye, hp)
            dhp = _dot_tn(dye, cm)
            gmat = _dot_nt(bm, dhn)
            dxdt = dxdt + dec * gmat
            dbm = dbm + _dot(xdt * dec, dhn)
            r_off = dyp * yoff
            r_dec = xdt * gmat * dec
            r_x = dxdt * x
            hh = dhn * hp
            for e in range(2):
                sel = half == (e == 1)
                h = hs[e]
                s_off = jnp.sum(jnp.where(sel, r_off, 0.0), axis=1, keepdims=True)
                s_dec = jnp.sum(jnp.where(sel, r_dec, 0.0), axis=1, keepdims=True)
                dacs = dacs + jnp.where(lane == h, s_off - s_dec, 0.0)
                dcd = jnp.sum(jnp.sum(jnp.where(sub_half == (e == 1), hh, 0.0), axis=1, keepdims=True), axis=0, keepdims=True)
                tot_e = jnp.sum(s_dec, axis=0, keepdims=True) + jnp.exp(tots[e]) * dcd
                dtot = dtot + jnp.where(lane[0:1] == h, tot_e, 0.0)
                ddt_x = ddt_x + jnp.where(lane == h, jnp.sum(jnp.where(sel, r_x, 0.0), axis=1, keepdims=True), 0.0)
            dx_ref[:, 128 * j:128 * (j + 1)] = dxdt * dtp
            dh_scr[j] = cd * dhn + dhp
        dcm = dcm + _dot(dcb, bm)
        dbm = dbm + _dot_tn(dcb, cm)
        db_ref[...] = dbm
        dc_ref[...] = dcm
        dacs = dacs + dacs_t.T
        da = _dot_exact(tri.T, dacs) + dtot
        ddt_ref[0] = da * a_row + ddt_x
        da_ref[0] += jnp.sum(da * dt_blk, axis=0, keepdims=True)

    return pl.pallas_call(
        body, name=name, grid=(SSD_GROUPS, nc),
        in_specs=[pl.BlockSpec((CHUNK, 512), lambda g, c: (cidx(c), g)),
                  pl.BlockSpec((CHUNK, 128), lambda g, c: (cidx(c), 16 + g)),
                  pl.BlockSpec((CHUNK, 128), lambda g, c: (cidx(c), 20 + g)),
                  pl.BlockSpec((CHUNK, 128), lambda g, c: (cidx(c), d_off)),
                  pl.BlockSpec((1, 128), lambda g, c: (0, d_off)),
                  pl.BlockSpec((CHUNK, 512), lambda g, c: (cidx(c), g)),
                  pl.BlockSpec((1, 4, 128, 128), lambda g, c: (cidx(c), g, 0, 0))],
        out_specs=[pl.BlockSpec((CHUNK, 512), lambda g, c: (cidx(c), g)),
                   pl.BlockSpec((CHUNK, 128), lambda g, c: (cidx(c), g)),
                   pl.BlockSpec((CHUNK, 128), lambda g, c: (cidx(c), g)),
                   pl.BlockSpec((1, CHUNK, 128), lambda g, c: (g, cidx(c), 0)),
                   pl.BlockSpec((1, 1, 128), lambda g, c: (g, 0, 0))],
        out_shape=[jax.ShapeDtypeStruct((t, D_INNER), f32), jax.ShapeDtypeStruct((t, 512), f32),
                   jax.ShapeDtypeStruct((t, 512), f32), jax.ShapeDtypeStruct((4, t, 128), f32),
                   jax.ShapeDtypeStruct((4, 1, 128), f32)],
        scratch_shapes=[pltpu.VMEM((4, 128, 128), f32), pltpu.VMEM((CHUNK, CHUNK), f32)],
        compiler_params=_cparams(("parallel", "arbitrary")))(xbc_c, xbc_c, xbc_c, dt2, a_rows, dy, hprev)


def tail_fwd(y_f, y_b, xbc_c, z, dskip_row, nw_row, tb=512):
    t = y_f.shape[0]

    def body(yf_ref, yb_ref, x_ref, z_ref, d_ref, w_ref, o_ref):
        zz = z_ref[...]
        y = (yf_ref[...] + yb_ref[...] + d_ref[...] * x_ref[...]) * (zz * _sigmoid(zz))
        rstd = lax.rsqrt(jnp.mean(y * y, axis=1, keepdims=True) + NORM_EPS)
        o_ref[...] = (y * rstd * w_ref[...]).astype(o_ref.dtype)

    blk = pl.BlockSpec((tb, 512), lambda i, g: (i, g))
    row = pl.BlockSpec((1, 512), lambda i, g: (0, g))
    return pl.pallas_call(
        body, name="tail_fwd", grid=(t // tb, SSD_GROUPS), in_specs=[blk, blk, blk, blk, row, row], out_specs=blk,
        out_shape=jax.ShapeDtypeStruct((t, D_INNER), bf16),
        compiler_params=_cparams(("parallel", "parallel")))(y_f, y_b, xbc_c, z, dskip_row, nw_row)


def tail_bwd(dyn, y_f, y_b, xbc_c, z, dskip_row, nw_row, tb=512):
    t = y_f.shape[0]

    def body(g_ref, yf_ref, yb_ref, x_ref, z_ref, d_ref, w_ref, dy_ref, dz_ref, dw_ref, dd_ref):
        zz = z_ref[...]
        sg = _sigmoid(zz)
        sl = zz * sg
        x = x_ref[...]
        y = yf_ref[...] + yb_ref[...] + d_ref[...] * x
        yz = y * sl
        rstd = lax.rsqrt(jnp.mean(yz * yz, axis=1, keepdims=True) + NORM_EPS)
        yhat = yz * rstd
        g = g_ref[...]
        dyhat = g * w_ref[...]
        dyz = rstd * (dyhat - yhat * jnp.mean(dyhat * yhat, axis=1, keepdims=True))
        dy = dyz * sl
        dy_ref[...] = dy
        dz_ref[...] = (dyz * y * sg * (1.0 + zz * (1.0 - sg))).astype(dz_ref.dtype)

        @pl.when(pl.program_id(1) == 0)
        def _():
            dw_ref[...] = jnp.zeros_like(dw_ref)
            dd_ref[...] = jnp.zeros_like(dd_ref)

        dw_ref[...] += jnp.sum(g * yhat, axis=0, keepdims=True)
        dd_ref[...] += jnp.sum(dy * x, axis=0, keepdims=True)

    blk = pl.BlockSpec((tb, 512), lambda g, i: (i, g))
    row = pl.BlockSpec((1, 512), lambda g, i: (0, g))
    return pl.pallas_call(
        body, name="tail_bwd", grid=(SSD_GROUPS, t // tb), in_specs=[blk, blk, blk, blk, blk, row, row],
        out_specs=[blk, blk, row, row],
        out_shape=[jax.ShapeDtypeStruct((t, D_INNER), f32), jax.ShapeDtypeStruct((t, D_INNER), bf16),
                   jax.ShapeDtypeStruct((1, D_INNER), f32), jax.ShapeDtypeStruct((1, D_INNER), f32)],
        compiler_params=_cparams(("parallel", "arbitrary")))(dyn, y_f, y_b, xbc_c, z, dskip_row, nw_row)


def _slopes(p):
    return [2.0 ** (-8.0 * (HEADS_PER_PATTERN * p + j + 1) / ATTN_HEADS) for j in range(HEADS_PER_PATTERN)]


def _win_specs(nq, col_of):
    return [pl.BlockSpec((64, 256), lambda r, i: (jnp.maximum(2 * i - 1, 0), col_of(r))),
            pl.BlockSpec((128, 256), lambda r, i: (i, col_of(r))),
            pl.BlockSpec((64, 256), lambda r, i: (jnp.minimum(2 * i + 2, 2 * nq - 1), col_of(r)))]


def _lane_head(shape):
    return lax.broadcasted_iota(jnp.int32, shape, 1) >> 6


def _stack_heads(m):
    lane_head = _lane_head(m.shape)
    return jnp.concatenate([jnp.where(lane_head == j, m, 0.0) for j in range(HEADS_PER_PATTERN)], axis=0)


def _unstack_heads(m4, n):
    lane_head = _lane_head((n, 256))
    out = jnp.where(lane_head == 0, m4[0:n], 0.0)
    for j in range(1, HEADS_PER_PATTERN):
        out = out + jnp.where(lane_head == j, m4[j * n:(j + 1) * n], 0.0)
    return out


def _head_cols(m, n):
    lane = lax.broadcasted_iota(jnp.int32, (n, 256), 1)
    return jnp.concatenate([jnp.sum(jnp.where(lane == ATTN_HEAD_DIM * j, m, 0.0), axis=1, keepdims=True)
                            for j in range(HEADS_PER_PATTERN)], axis=0)


def _q_scores(q, kcat, i, nq, p, dil):
    s = _dot_nt(_stack_heads(q * 0.125), kcat)
    row = lax.broadcasted_iota(jnp.int32, s.shape, 0)
    col = lax.broadcasted_iota(jnp.int32, s.shape, 1)
    rel = col - 64 - (row & 127)
    valid = (jnp.abs(rel) <= 64) & ((i > 0) | (col >= 64)) & ((i < nq - 1) | (col < 192))
    sl = _slopes(p)
    hd = row >> 7
    slope = jnp.where(hd == 0, sl[0], jnp.where(hd == 1, sl[1], jnp.where(hd == 2, sl[2], sl[3])))
    s = s - slope * (jnp.abs(rel) * dil).astype(f32)
    return jnp.where(valid, s, NEG_BIG)


def attn_fwd(q, k, v, p, dil, name):
    l = q.shape[0]
    nq = l // 128

    def body(q_ref, kp_ref, ko_ref, kn_ref, vp_ref, vo_ref, vn_ref, o_ref, lse_ref):
        i = pl.program_id(1)
        kcat = jnp.concatenate([kp_ref[...], ko_ref[...], kn_ref[...]], axis=0)
        vcat = jnp.concatenate([vp_ref[...], vo_ref[...], vn_ref[...]], axis=0)
        s = _q_scores(q_ref[...], kcat, i, nq, p, dil)
        m = jnp.max(s, axis=1, keepdims=True)
        pr = jnp.exp(s - m)
        den = jnp.sum(pr, axis=1, keepdims=True)
        o4 = _dot(pr, vcat) / den
        o_ref[...] = _unstack_heads(o4, 128)
        lse_ref[...] = _unstack_heads(jnp.broadcast_to(m + jnp.log(den), (512, 256)), 128)

    col = lambda r: r
    return pl.pallas_call(
        body, name=name, grid=(dil, nq),
        in_specs=[pl.BlockSpec((128, 256), lambda r, i: (i, r))] + _win_specs(nq, col) + _win_specs(nq, col),
        out_specs=[pl.BlockSpec((128, 256), lambda r, i: (i, r))] * 2,
        out_shape=[jax.ShapeDtypeStruct(q.shape, f32)] * 2,
        compiler_params=_cparams(("parallel", "parallel")))(q, k, k, k, v, v, v)


def attn_combine(os_, lses, tb=1024):
    t = os_[0].shape[0]

    def body(o0, o1, o2, l0, l1, l2, y_ref, lse_ref):
        a0, a1, a2 = l0[...], l1[...], l2[...]
        m = jnp.maximum(jnp.maximum(a0, a1), a2)
        e0, e1, e2 = jnp.exp(a0 - m), jnp.exp(a1 - m), jnp.exp(a2 - m)
        den = e0 + e1 + e2
        y_ref[...] = (e0 * o0[...] + e1 * o1[...] + e2 * o2[...]) / den
        lse_ref[...] = m + jnp.log(den)

    blk = pl.BlockSpec((tb, 256), lambda i: (i, 0))
    return pl.pallas_call(
        body, name="attn_combine", grid=(t // tb,), in_specs=[blk] * 6, out_specs=[blk, blk],
        out_shape=[jax.ShapeDtypeStruct((t, 256), f32)] * 2,
        compiler_params=_cparams(("parallel",)))(*os_, *lses)


def attn_delta(dy, y, tb=1024):
    t = dy.shape[0]

    def body(dy_ref, y_ref, d_ref):
        pr = dy_ref[...] * y_ref[...]
        lane_head = _lane_head(pr.shape)
        out = jnp.zeros_like(pr)
        for j in range(HEADS_PER_PATTERN):
            sj = jnp.sum(jnp.where(lane_head == j, pr, 0.0), axis=1, keepdims=True)
            out = out + jnp.where(lane_head == j, sj, 0.0)
        d_ref[...] = out

    blk = pl.BlockSpec((tb, 256), lambda i: (i, 0))
    return pl.pallas_call(body, name="attn_delta", grid=(t // tb,), in_specs=[blk, blk], out_specs=blk,
                          out_shape=jax.ShapeDtypeStruct((t, 256), f32),
                          compiler_params=_cparams(("parallel",)))(dy, y)


def attn_dq(q, k, v, dy, lse, delta, p, dil, name):
    l = q.shape[0]
    nq = l // 128

    def body(q_ref, kp_ref, ko_ref, kn_ref, vp_ref, vo_ref, vn_ref, dy_ref, lse_ref, d_ref, dq_ref):
        i = pl.program_id(1)
        kcat = jnp.concatenate([kp_ref[...], ko_ref[...], kn_ref[...]], axis=0)
        vcat = jnp.concatenate([vp_ref[...], vo_ref[...], vn_ref[...]], axis=0)
        s = _q_scores(q_ref[...], kcat, i, nq, p, dil)
        pr = jnp.exp(s - _head_cols(lse_ref[...], 128))
        dp = _dot_nt(_stack_heads(dy_ref[...]), vcat)
        ds = pr * (dp - _head_cols(d_ref[...], 128))
        dq_ref[...] = (_unstack_heads(_dot(ds, kcat), 128) * 0.125).astype(dq_ref.dtype)

    col = lambda r: r
    own = pl.BlockSpec((128, 256), lambda r, i: (i, r))
    return pl.pallas_call(
        body, name=name, grid=(dil, nq),
        in_specs=[own] + _win_specs(nq, col) + _win_specs(nq, col) + [own, own, own], out_specs=own,
        out_shape=jax.ShapeDtypeStruct(q.shape, bf16),
        compiler_params=_cparams(("parallel", "parallel")))(q, k, k, k, v, v, v, dy, lse, delta)


def attn_dkv(q, k, v, dy, lse, delta, p, dil, name):
    l = q.shape[0]
    nq = l // 128

    def body(qp_ref, qo_ref, qn_ref, gp_ref, go_ref, gn_ref, lp_ref, lo_ref, ln_ref, dp_ref, do_ref, dn_ref,
             k_ref, v_ref, dk_ref, dv_ref):
        i = pl.program_id(1)
        cat = lambda a, b, c: jnp.concatenate([a[...], b[...], c[...]], axis=0)
        q4 = _stack_heads(cat(qp_ref, qo_ref, qn_ref) * 0.125)
        dy4 = _stack_heads(cat(gp_ref, go_ref, gn_ref))
        lse4 = _head_cols(cat(lp_ref, lo_ref, ln_ref), 256)
        del4 = _head_cols(cat(dp_ref, do_ref, dn_ref), 256)
        s = _dot_nt(q4, k_ref[...])
        row = lax.broadcasted_iota(jnp.int32, s.shape, 0)
        col = lax.broadcasted_iota(jnp.int32, s.shape, 1)
        qoff = row & 255
        rel = col - (qoff - 64)
        valid = (jnp.abs(rel) <= 64) & ((i > 0) | (qoff >= 64)) & ((i < nq - 1) | (qoff < 192))
        sl = _slopes(p)
        hd = row >> 8
        slope = jnp.where(hd == 0, sl[0], jnp.where(hd == 1, sl[1], jnp.where(hd == 2, sl[2], sl[3])))
        s = s - slope * (jnp.abs(rel) * dil).astype(f32)
        pr = jnp.where(valid, jnp.exp(jnp.where(valid, s, NEG_BIG) - lse4), 0.0)
        dpm = _dot_nt(dy4, v_ref[...])
        ds = pr * (dpm - del4)
        dv_ref[...] = _dot_tn(pr, dy4).astype(dv_ref.dtype)
        dk_ref[...] = _dot_tn(ds, q4).astype(dk_ref.dtype)

    col = lambda r: r
    own = pl.BlockSpec((128, 256), lambda r, i: (i, r))
    win = _win_specs(nq, col)
    return pl.pallas_call(
        body, name=name, grid=(dil, nq), in_specs=win * 4 + [own, own], out_specs=[own, own],
        out_shape=[jax.ShapeDtypeStruct(q.shape, bf16)] * 2,
        compiler_params=_cparams(("parallel", "parallel")))(q, q, q, dy, dy, dy, lse, lse, lse, delta, delta, delta, k, v)


def merge_fwd(u_gate, bg_row, y_ssd, y_att, tb=512):
    t = y_ssd.shape[0]

    def body(ga_ref, gb_ref, ba_ref, bb_ref, ys_ref, ya_ref, o_ref):
        o_ref[...] = (_sigmoid(ga_ref[...] + ba_ref[...]) * ys_ref[...]
                      + _sigmoid(gb_ref[...] + bb_ref[...]) * ya_ref[...]).astype(o_ref.dtype)

    blk = pl.BlockSpec((tb, 512), lambda i, j: (i, j))
    blk2 = pl.BlockSpec((tb, 512), lambda i, j: (i, 2 + j))
    row = pl.BlockSpec((1, 512), lambda i, j: (0, j))
    row2 = pl.BlockSpec((1, 512), lambda i, j: (0, 2 + j))
    return pl.pallas_call(
        body, name="merge_fwd", grid=(t // tb, 2), in_specs=[blk, blk2, row, row2, blk, blk], out_specs=blk,
        out_shape=jax.ShapeDtypeStruct((t, D_MODEL), bf16),
        compiler_params=_cparams(("parallel", "parallel")))(u_gate, u_gate, bg_row, bg_row, y_ssd, y_att)


def merge_bwd(dm, u_gate, bg_row, y_ssd, y_att, tb=512):
    t = dm.shape[0]

    def body(dm_ref, ga_ref, gb_ref, ba_ref, bb_ref, ys_ref, ya_ref, dys_ref, dya_ref, dga_ref, dgb_ref, dba_ref, dbb_ref):
        d = dm_ref[...]
        sa = _sigmoid(ga_ref[...] + ba_ref[...])
        sb = _sigmoid(gb_ref[...] + bb_ref[...])
        dys_ref[...] = (d * sa).astype(dys_ref.dtype)
        dya_ref[...] = (d * sb).astype(dya_ref.dtype)
        dla = d * ys_ref[...] * sa * (1.0 - sa)
        dlb = d * ya_ref[...] * sb * (1.0 - sb)
        dga_ref[...] = dla.astype(dga_ref.dtype)
        dgb_ref[...] = dlb.astype(dgb_ref.dtype)

        @pl.when(pl.program_id(1) == 0)
        def _():
            dba_ref[...] = jnp.zeros_like(dba_ref)
            dbb_ref[...] = jnp.zeros_like(dbb_ref)

        dba_ref[...] += jnp.sum(dla, axis=0, keepdims=True)
        dbb_ref[...] += jnp.sum(dlb, axis=0, keepdims=True)

    blk = pl.BlockSpec((tb, 512), lambda j, i: (i, j))
    blk2 = pl.BlockSpec((tb, 512), lambda j, i: (i, 2 + j))
    row = pl.BlockSpec((1, 512), lambda j, i: (0, j))
    row2 = pl.BlockSpec((1, 512), lambda j, i: (0, 2 + j))
    act = jax.ShapeDtypeStruct((t, D_MODEL), bf16)
    vec = jax.ShapeDtypeStruct((1, D_MODEL), f32)
    return pl.pallas_call(
        body, name="merge_bwd", grid=(2, t // tb), in_specs=[blk, blk, blk2, row, row2, blk, blk],
        out_specs=[blk, blk, blk, blk, row, row], out_shape=[act, act, act, act, vec, vec],
        compiler_params=_cparams(("parallel", "arbitrary")))(dm, u_gate, u_gate, bg_row, bg_row, y_ssd, y_att)


def _ln_stats(r):
    mu = jnp.mean(r, axis=1, keepdims=True)
    xc = r - mu
    rstd = lax.rsqrt(jnp.mean(xc * xc, axis=1, keepdims=True) + NORM_EPS)
    return xc * rstd, rstd


def _ln_bwd(dy, xhat, rstd, g_row):
    dxh = dy * g_row
    return rstd * (dxh - jnp.mean(dxh, axis=1, keepdims=True) - xhat * jnp.mean(dxh * xhat, axis=1, keepdims=True))


def ln1_fwd(x, mix, g_row, b_row, tb=512):
    t = x.shape[0]

    def body(x_ref, m_ref, g_ref, b_ref, o_ref):
        xhat, _ = _ln_stats(ALPHA * x_ref[...] + m_ref[...])
        o_ref[...] = xhat * g_ref[...] + b_ref[...]

    blk = pl.BlockSpec((tb, D_MODEL), lambda i: (i, 0))
    row = pl.BlockSpec((1, D_MODEL), lambda i: (0, 0))
    return pl.pallas_call(body, name="ln1_fwd", grid=(t // tb,), in_specs=[blk, blk, row, row], out_specs=blk,
                          out_shape=jax.ShapeDtypeStruct((t, D_MODEL), f32),
                          compiler_params=_cparams(("parallel",)))(x, mix, g_row, b_row)


def ln1_bwd(dh, x, mix, g_row, tb=512):
    t = x.shape[0]

    def body(dh_ref, x_ref, m_ref, g_ref, dr_ref, dg_ref, db_ref):
        xhat, rstd = _ln_stats(ALPHA * x_ref[...] + m_ref[...])
        dy = dh_ref[...]
        dr_ref[...] = _ln_bwd(dy, xhat, rstd, g_ref[...])

        @pl.when(pl.program_id(0) == 0)
        def _():
            dg_ref[...] = jnp.zeros_like(dg_ref)
            db_ref[...] = jnp.zeros_like(db_ref)

        dg_ref[...] += jnp.sum(dy * xhat, axis=0, keepdims=True)
        db_ref[...] += jnp.sum(dy, axis=0, keepdims=True)

    blk = pl.BlockSpec((tb, D_MODEL), lambda i: (i, 0))
    row = pl.BlockSpec((1, D_MODEL), lambda i: (0, 0))
    return pl.pallas_call(
        body, name="ln1_bwd", grid=(t // tb,), in_specs=[blk, blk, blk, row], out_specs=[blk, row, row],
        out_shape=[jax.ShapeDtypeStruct((t, D_MODEL), f32), jax.ShapeDtypeStruct((1, D_MODEL), f32),
                   jax.ShapeDtypeStruct((1, D_MODEL), f32)],
        compiler_params=_cparams(("arbitrary",)))(dh, x, mix, g_row)


def ln2_loss(h1, f, g_row, b_row, target, tb=512):
    t = h1.shape[0]

    def body(h_ref, f_ref, g_ref, b_ref, t_ref, dr_ref, dg_ref, db_ref, loss_ref):
        xhat, rstd = _ln_stats(ALPHA * h_ref[...] + f_ref[...])
        g = g_ref[...]
        err = xhat * g + b_ref[...] - t_ref[...]
        dy = err * (1.0 / D_MODEL)
        dr_ref[...] = _ln_bwd(dy, xhat, rstd, g)

        @pl.when(pl.program_id(0) == 0)
        def _():
            dg_ref[...] = jnp.zeros_like(dg_ref)
            db_ref[...] = jnp.zeros_like(db_ref)
            loss_ref[...] = jnp.zeros_like(loss_ref)

        dg_ref[...] += jnp.sum(dy * xhat, axis=0, keepdims=True)
        db_ref[...] += jnp.sum(dy, axis=0, keepdims=True)
        part = jnp.sum(jnp.mean(err * err, axis=1, keepdims=True), axis=0, keepdims=True)
        loss_ref[...] += 0.5 * part

    blk = pl.BlockSpec((tb, D_MODEL), lambda i: (i, 0))
    row = pl.BlockSpec((1, D_MODEL), lambda i: (0, 0))
    return pl.pallas_call(
        body, name="ln2_loss", grid=(t // tb,), in_specs=[blk, blk, row, row, blk],
        out_specs=[blk, row, row, pl.BlockSpec((8, 128), lambda i: (0, 0))],
        out_shape=[jax.ShapeDtypeStruct((t, D_MODEL), f32), jax.ShapeDtypeStruct((1, D_MODEL), f32),
                   jax.ShapeDtypeStruct((1, D_MODEL), f32), jax.ShapeDtypeStruct((8, 128), f32)],
        compiler_params=_cparams(("arbitrary",)))(h1, f, g_row, b_row, target)


def relu2_fwd(a, tb=512):
    t, n = a.shape

    def body(a_ref, o_ref):
        r = jnp.maximum(a_ref[...], 0.0)
        o_ref[...] = (r * r).astype(o_ref.dtype)

    blk = pl.BlockSpec((tb, 1024), lambda i, j: (i, j))
    return pl.pallas_call(body, name="relu2_fwd", grid=(t // tb, n // 1024), in_specs=[blk], out_specs=blk,
                          out_shape=jax.ShapeDtypeStruct((t, n), bf16),
                          compiler_params=_cparams(("parallel", "parallel")))(a)


def relu2_bwd(dp, a, tb=512):
    t, n = a.shape

    def body(dp_ref, a_ref, o_ref):
        o_ref[...] = (dp_ref[...] * (2.0 * jnp.maximum(a_ref[...], 0.0))).astype(o_ref.dtype)

    blk = pl.BlockSpec((tb, 1024), lambda i, j: (i, j))
    return pl.pallas_call(body, name="relu2_bwd", grid=(t // tb, n // 1024), in_specs=[blk, blk], out_specs=blk,
                          out_shape=jax.ShapeDtypeStruct((t, n), bf16),
                          compiler_params=_cparams(("parallel", "parallel")))(dp, a)


def adamw(parts, w, m, v):
    rows = w.shape[0]
    c1 = 1.0 - ADAM_B1 ** ADAM_STEP
    c2 = 1.0 - ADAM_B2 ** ADAM_STEP

    def body(p_ref, w_ref, m_ref, v_ref, g_ref, d_ref, nm_ref, nv_ref):
        g = ((p_ref[0] + p_ref[1]) + p_ref[2]) + p_ref[3]
        nm = ADAM_B1 * m_ref[...] + (1.0 - ADAM_B1) * g
        nv = ADAM_B2 * v_ref[...] + (1.0 - ADAM_B2) * (g * g)
        g_ref[...] = g
        nm_ref[...] = nm
        nv_ref[...] = nv
        d_ref[...] = -ADAM_LR * ((nm / c1) / (jnp.sqrt(nv / c2) + ADAM_EPS) + ADAM_WD * w_ref[...])

    blk = pl.BlockSpec((PACK_TILE, 1024), lambda i: (i, 0))
    out = jax.ShapeDtypeStruct((rows, 1024), f32)
    return pl.pallas_call(
        body, name="adamw", grid=(rows // PACK_TILE,),
        in_specs=[pl.BlockSpec((4, PACK_TILE, 1024), lambda i: (0, i, 0)), blk, blk, blk], out_specs=[blk] * 4,
        out_shape=[out] * 4, compiler_params=_cparams(("parallel",)))(parts, w, m, v)


def pair_sum(keep, recv):
    rows = keep.shape[1]

    def body(a_ref, b_ref, o_ref):
        o_ref[...] = a_ref[...] + b_ref[...]

    blk = pl.BlockSpec((1, PACK_TILE, 1024), lambda j, i: (j, i, 0))
    return pl.pallas_call(body, name="pair_sum", grid=(4, rows // PACK_TILE), in_specs=[blk, blk], out_specs=blk,
                          out_shape=jax.ShapeDtypeStruct(keep.shape, f32),
                          compiler_params=_cparams(("parallel", "parallel")))(keep, recv)


def _place():
    return lax.axis_index("x"), lax.axis_index("y"), lax.axis_index("c")


def all_gather_blocks(shard):
    rows, cols = shard.shape

    def body(x_ref, out_ref, send_sems, recv_sems, local_sem):
        x, y, c = _place()
        me, sibling = (x, y, c), (x, y, 1 - c)
        chips = [(1 - x, y), (x, 1 - y), (1 - x, 1 - y)]

        def slot(px, py, pc):
            return out_ref.at[4 * px + 2 * py + pc]

        def copy(k, block, to, src=None):
            return pltpu.make_async_remote_copy(
                src_ref=slot(*block) if src is None else src, dst_ref=slot(*block), send_sem=send_sems.at[k],
                recv_sem=recv_sems.at[k], device_id=to, device_id_type=MESH)

        mine = pltpu.make_async_copy(x_ref, slot(*me), local_sem)
        mine.start()
        first = [copy(0, me, sibling, src=x_ref)]
        first += [copy(1 + j, me, (*chip, c), src=x_ref) for j, chip in enumerate(chips)]
        for cp in first:
            cp.start()
        passed = [copy(4 + j, (*chip, c), sibling) for j, chip in enumerate(chips)]
        for j, chip in enumerate(chips):
            copy(1 + j, (*chip, c), me).wait_recv()
            passed[j].start()
        copy(0, sibling, me).wait_recv()
        for j, chip in enumerate(chips):
            copy(4 + j, (*chip, 1 - c), me).wait_recv()
        for cp in first + passed:
            cp.wait_send()
        mine.wait()

    return pl.pallas_call(
        body, name="all_gather_blocks", out_shape=jax.ShapeDtypeStruct((N_DEV, rows, cols), shard.dtype),
        in_specs=[pl.BlockSpec(memory_space=pl.ANY)], out_specs=pl.BlockSpec(memory_space=pl.ANY),
        scratch_shapes=[pltpu.SemaphoreType.DMA((7,)), pltpu.SemaphoreType.DMA((7,)), pltpu.SemaphoreType.DMA],
        compiler_params=pltpu.CompilerParams(has_side_effects=True))(shard)


def pair_exchange(parts):
    _, rows, cols = parts.shape

    def body(p_ref, keep_ref, recv_ref, send_sems, recv_sems, local_sems):
        x, y, c = _place()
        sibling = (x, y, 1 - c)
        local = [pltpu.make_async_copy(p_ref.at[2 * j + c], keep_ref.at[j], local_sems.at[j]) for j in range(4)]
        remote = [pltpu.make_async_remote_copy(
            src_ref=p_ref.at[2 * j + 1 - c], dst_ref=recv_ref.at[j], send_sem=send_sems.at[j], recv_sem=recv_sems.at[j],
            device_id=sibling, device_id_type=MESH) for j in range(4)]
        for cp in remote + local:
            cp.start()
        for cp in remote:
            cp.wait_recv()
        for cp in remote:
            cp.wait_send()
        for cp in local:
            cp.wait()

    out = jax.ShapeDtypeStruct((4, rows, cols), parts.dtype)
    return pl.pallas_call(
        body, name="pair_exchange", out_shape=[out, out], in_specs=[pl.BlockSpec(memory_space=pl.ANY)],
        out_specs=[pl.BlockSpec(memory_space=pl.ANY)] * 2,
        scratch_shapes=[pltpu.SemaphoreType.DMA((4,)), pltpu.SemaphoreType.DMA((4,)), pltpu.SemaphoreType.DMA((4,))],
        compiler_params=pltpu.CompilerParams(has_side_effects=True))(parts)


def chip_exchange(parts):
    _, rows, cols = parts.shape

    def body(p_ref, out_ref, send_sems, recv_sems, local_sem):
        x, y, c = _place()
        mine = 2 * x + y
        flips = [(x, 1 - y), (1 - x, y), (1 - x, 1 - y)]
        local = pltpu.make_async_copy(p_ref.at[mine], out_ref.at[mine], local_sem)
        local.start()
        sends = [pltpu.make_async_remote_copy(
            src_ref=p_ref.at[2 * px + py], dst_ref=out_ref.at[mine], send_sem=send_sems.at[k], recv_sem=recv_sems.at[k],
            device_id=(px, py, c), device_id_type=MESH) for k, (px, py) in enumerate(flips)]
        for cp in sends:
            cp.start()
        for k, (px, py) in enumerate(flips):
            pltpu.make_async_remote_copy(
                src_ref=p_ref.at[mine], dst_ref=out_ref.at[2 * px + py], send_sem=send_sems.at[k],
                recv_sem=recv_sems.at[k], device_id=(px, py, c), device_id_type=MESH).wait_recv()
        for cp in sends:
            cp.wait_send()
        local.wait()

    return pl.pallas_call(
        body, name="chip_exchange", out_shape=jax.ShapeDtypeStruct((4, rows, cols), parts.dtype),
        in_specs=[pl.BlockSpec(memory_space=pl.ANY)], out_specs=pl.BlockSpec(memory_space=pl.ANY),
        scratch_shapes=[pltpu.SemaphoreType.DMA((3,)), pltpu.SemaphoreType.DMA((3,)), pltpu.SemaphoreType.DMA],
        compiler_params=pltpu.CompilerParams(has_side_effects=True))(parts)


def _shard_shape(name):
    shape, ax = BIG_SHAPES[name]
    return tuple(s // N_DEV if i == ax else s for i, s in enumerate(shape))


def _pack_shard(vals, extra=None):
    flat = [vals[n].reshape(-1) for n in BIG] + [vals[n].reshape(-1) for n in SMALL]
    flat.append(jnp.zeros((1,), f32) if extra is None else extra.reshape(1))
    flat.append(jnp.zeros((PACK_ROWS * 1024 - PACK_ELEMS,), f32))
    return jnp.concatenate(flat).reshape(PACK_ROWS, 1024)


def _unpack_shard(packed):
    flat = packed.reshape(-1)
    out, off = {}, 0
    for n in BIG:
        shp = _shard_shape(n)
        out[n] = flat[off:off + int(np.prod(shp))].reshape(shp)
        off += int(np.prod(shp))
    for n in SMALL:
        out[n] = flat[off:off + SMALL_SIZES[n]]
        off += SMALL_SIZES[n]
    out["_extra"] = flat[off]
    return out


def _pack_parts(full, small, extra):
    cols = []
    for n in BIG:
        shape, ax = BIG_SHAPES[n]
        g = full[n]
        if ax == 0:
            cols.append(g.reshape(N_DEV, -1))
        else:
            cols.append(g.reshape(shape[0], N_DEV, shape[1] // N_DEV).transpose(1, 0, 2).reshape(N_DEV, -1))
    rep = jnp.concatenate([small[n].reshape(-1) for n in SMALL] + [extra.reshape(1)])
    cols.append(jnp.broadcast_to(rep[None, :], (N_DEV, rep.shape[0])))
    cols.append(jnp.zeros((N_DEV, PACK_ROWS * 1024 - PACK_ELEMS), f32))
    return jnp.concatenate(cols, axis=1).reshape(N_DEV, PACK_ROWS, 1024)


def _gather_weights(w):
    pieces = [w[n].astype(bf16).reshape(-1) for n in BIG if n != "conv_w"]
    pieces.append(lax.bitcast_convert_type(w["conv_w"], bf16).reshape(-1))
    n_el = sum(int(p.shape[0]) for p in pieces)
    rows = -(-n_el // (1024 * 16)) * 16
    pieces.append(jnp.zeros((rows * 1024 - n_el,), bf16))
    got = all_gather_blocks(jnp.concatenate(pieces).reshape(rows, 1024)).reshape(N_DEV, rows * 1024)
    out, off = {}, 0
    for n in BIG:
        shape, ax = BIG_SHAPES[n]
        shp = _shard_shape(n)
        cnt = int(np.prod(shp)) * (2 if n == "conv_w" else 1)
        blk = got[:, off:off + cnt]
        off += cnt
        if n == "conv_w":
            blk = lax.bitcast_convert_type(blk.reshape(N_DEV, *shp, 2), f32)
        else:
            blk = blk.reshape(N_DEV, *shp)
        out[n] = blk.reshape(shape) if ax == 0 else blk.transpose(1, 0, 2).reshape(shape)
    return out


def _row(v, width=None):
    v = v.reshape(1, -1).astype(f32)
    return v if width is None else jnp.pad(v, ((0, 0), (0, width - v.shape[1])))


def _lanes256(vf, vb):
    z = jnp.zeros((96,), f32)
    return jnp.concatenate([vf.astype(f32), z, vb.astype(f32), z]).reshape(1, 256)


def _local_step(x2, tgt, wf, p):
    t = x2.shape[0]
    o = np.cumsum((0,) + IN_SPLITS)
    w_in = wf["w_in"]
    w_z, w_xbc = w_in[:, o[0]:o[1]], w_in[:, o[1]:o[2]]
    zpad = jnp.zeros((D_MODEL, 96), w_in.dtype)
    w_dt = jnp.concatenate([w_in[:, o[2]:o[3]], zpad, w_in[:, o[3]:o[4]], zpad], axis=1)
    w_qkv, w_gate = w_in[:, o[4]:o[7]], w_in[:, o[7]:o[8]]

    conv_w8 = jnp.pad(wf["conv_w"].astype(f32), ((0, 8 - D_CONV), (0, 0)))
    conv_b = _row(p["conv_b"])
    dt_bias = _lanes256(p["dt_bias_f"], p["dt_bias_b"])
    a_f, a_b = -jnp.exp(p["a_log_f"].astype(f32)), -jnp.exp(p["a_log_b"].astype(f32))
    a_rows = _lanes256(a_f, a_b)
    dskip_row = jnp.repeat(p["d_skip"].astype(f32), SSD_HEAD_DIM).reshape(1, D_INNER)
    nw_row, bg_row = _row(p["ssd_norm_w"]), _row(p["b_gate"])
    g1, b1, g2, b2 = _row(p["ln1_g"]), _row(p["ln1_b"]), _row(p["ln2_g"]), _row(p["ln2_b"])

    u_z = mm_nn(x2, w_z, "in_z")
    u_xbc = mm_nn(x2, w_xbc, "in_xbc")
    u_dt = mm_nn(x2, w_dt, "in_dt")
    u_qkv = mm_nn(x2, w_qkv, "in_qkv")
    u_gate = mm_nn(x2, w_gate, "in_gate")
    xbc_c = conv_fwd(u_xbc, conv_w8, conv_b)
    dt2 = dt_fwd(u_dt, dt_bias)
    y_f, h_f = ssd_fwd(xbc_c, dt2, a_rows, False, "ssd_fwd_f")
    y_b, h_b = ssd_fwd(xbc_c, dt2, a_rows, True, "ssd_fwd_b")
    yn = tail_fwd(y_f, y_b, xbc_c, u_z, dskip_row, nw_row)
    y_ssd = mm_nn(yn, wf["w_proj_ssd"], "proj_ssd")

    def strided(a, dil):
        return a.reshape(t // dil, dil * 256)

    qkv, outs, lses = [], [], []
    for pi, (_, dil) in enumerate(DIL_PATTERNS):
        q, k, v = (strided(u_qkv[:, ATTN_WIDTH * s + 256 * pi: ATTN_WIDTH * s + 256 * (pi + 1)], dil) for s in range(3))
        qkv.append((q, k, v))
        op, lp = attn_fwd(q, k, v, pi, dil, f"attn_fwd_{pi}")
        outs.append(op.reshape(t, 256))
        lses.append(lp.reshape(t, 256))
    ya, lse = attn_combine(outs, lses)
    y_att = mm_nn(ya, wf["w_proj_attn"], "proj_attn")
    m = merge_fwd(u_gate, bg_row, y_ssd, y_att)
    mix = mm_nn(m, wf["w_out"], "out_proj")
    h1 = ln1_fwd(x2, mix, g1, b1)
    a_up = mm_nn(h1, wf["w_up"], "mlp_up")
    p_act = relu2_fwd(a_up)
    f_dn = mm_nn(p_act, wf["w_down"], "mlp_down")
    dr2, dg2, db2, loss8 = ln2_loss(h1, f_dn, g2, b2, tgt)

    full, small = {}, {}
    dp = mm_nn(dr2, wf["w_down"].T, "d_mlp_act")
    da = relu2_bwd(dp, a_up)
    full["w_down"] = mm_tn(p_act, dr2, "dw_down")
    full["w_up"] = mm_tn(h1, da, "dw_up")
    dh1 = mm_nn(da, wf["w_up"].T, "d_h1", acc_in=dr2, acc_scale=ALPHA)
    dr1, dg1, db1 = ln1_bwd(dh1, x2, mix, g1)
    dm = mm_nn(dr1, wf["w_out"].T, "d_merge")
    full["w_out"] = mm_tn(m, dr1, "dw_out")
    dys, dya_p, dga, dgb, dba, dbb = merge_bwd(dm, u_gate, bg_row, y_ssd, y_att)
    dyn = mm_nn(dys, wf["w_proj_ssd"].T, "d_yn")
    full["w_proj_ssd"] = mm_tn(yn, dys, "dw_proj_ssd")
    dya = mm_nn(dya_p, wf["w_proj_attn"].T, "d_ya")
    full["w_proj_attn"] = mm_tn(ya, dya_p, "dw_proj_attn")

    dy, dz, dnw, ddx = tail_bwd(dyn, y_f, y_b, xbc_c, u_z, dskip_row, nw_row)
    dxf, dbf, dcf, ddtf, daf = ssd_bwd(xbc_c, dt2, a_rows, dy, h_f, False, "ssd_bwd_f")
    dxb, dbb_, dcb_, ddtb, dab = ssd_bwd(xbc_c, dt2, a_rows, dy, h_b, True, "ssd_bwd_b")
    sx = slice(0, D_INNER)
    sb = slice(D_INNER, D_INNER + 512)
    sc = slice(D_INNER + 512, CONV_DIM)
    dxbc_x, dcw_x, dcb_x = conv_bwd(u_xbc[:, sx], [dxf, dxb, dy], [None, None, dskip_row], conv_w8[:, sx], conv_b[:, sx], "conv_bwd_x")
    dxbc_b, dcw_b, dcb_b = conv_bwd(u_xbc[:, sb], [dbf, dbb_], [None, None], conv_w8[:, sb], conv_b[:, sb], "conv_bwd_b")
    dxbc_c, dcw_c, dcb_c = conv_bwd(u_xbc[:, sc], [dcf, dcb_], [None, None], conv_w8[:, sc], conv_b[:, sc], "conv_bwd_c")
    du_dt, dbias = dt_bwd(ddtf, ddtb, u_dt, dt_bias)

    delta = attn_delta(dya, ya)
    dqs, dks, dvs = [], [], []
    for pi, (_, dil) in enumerate(DIL_PATTERNS):
        q, k, v = qkv[pi]
        sd, sl_, sdel = strided(dya, dil), strided(lse, dil), strided(delta, dil)
        dqs.append(attn_dq(q, k, v, sd, sl_, sdel, pi, dil, f"attn_dq_{pi}").reshape(t, 256))
        dk, dv = attn_dkv(q, k, v, sd, sl_, sdel, pi, dil, f"attn_dkv_{pi}")
        dks.append(dk.reshape(t, 256))
        dvs.append(dv.reshape(t, 256))
    du_qkv = jnp.concatenate(dqs + dks + dvs, axis=1)
    du_xbc = jnp.concatenate([dxbc_x, dxbc_b, dxbc_c], axis=1)
    du_gate = jnp.concatenate([dga, dgb], axis=1)

    dx = mm_nn(dz, w_z.T, "dx_z", acc_in=dr1, acc_scale=ALPHA)
    dx = mm_nn(du_xbc, w_xbc.T, "dx_xbc", acc_in=dx)
    dx = mm_nn(du_dt, w_dt.T, "dx_dt", acc_in=dx)
    dx = mm_nn(du_qkv, w_qkv.T, "dx_qkv", acc_in=dx)
    dx = mm_nn(du_gate, w_gate.T, "dx_gate", acc_in=dx)
    dw_dt = mm_tn(x2, du_dt, "dw_in_dt")
    full["w_in"] = jnp.concatenate(
        [mm_tn(x2, dz, "dw_in_z"), mm_tn(x2, du_xbc, "dw_in_xbc"), dw_dt[:, 0:32], dw_dt[:, 128:160],
         mm_tn(x2, du_qkv, "dw_in_qkv"), mm_tn(x2, du_gate, "dw_in_gate")], axis=1)
    full["conv_w"] = jnp.concatenate([dcw_x[:D_CONV], dcw_b[:D_CONV], dcw_c[:D_CONV]], axis=1)

    small["b_gate"] = jnp.concatenate([dba, dbb], axis=1)
    small["conv_b"] = jnp.concatenate([dcb_x, dcb_b, dcb_c], axis=1)
    small["dt_bias_f"], small["dt_bias_b"] = dbias[0, 0:32], dbias[0, 128:160]
    small["a_log_f"] = jnp.sum(daf, axis=(0, 1))[0:32] * a_f
    small["a_log_b"] = jnp.sum(dab, axis=(0, 1))[0:32] * a_b
    small["d_skip"] = jnp.sum(ddx.reshape(SSD_HEADS, SSD_HEAD_DIM), axis=1)
    small["ssd_norm_w"] = dnw
    small["ln1_g"], small["ln1_b"], small["ln2_g"], small["ln2_b"] = dg1, db1, dg2, db2
    return loss8[0, 0], dx, full, small


def kernel(x, w_in, b_gate, conv_w, conv_b, dt_bias_f, dt_bias_b, a_log_f, a_log_b, d_skip, ssd_norm_w, w_proj_ssd, w_proj_attn, w_out, ln1_g, ln1_b, w_up, w_down, ln2_g, ln2_b, loss_target, m_w_in, m_b_gate, m_conv_w, m_conv_b, m_dt_bias_f, m_dt_bias_b, m_a_log_f, m_a_log_b, m_d_skip, m_ssd_norm_w, m_w_proj_ssd, m_w_proj_attn, m_w_out, m_ln1_g, m_ln1_b, m_w_up, m_w_down, m_ln2_g, m_ln2_b, v_w_in, v_b_gate, v_conv_w, v_conv_b, v_dt_bias_f, v_dt_bias_b, v_a_log_f, v_a_log_b, v_d_skip, v_ssd_norm_w, v_w_proj_ssd, v_w_proj_attn, v_w_out, v_ln1_g, v_ln1_b, v_w_up, v_w_down, v_ln2_g, v_ln2_b):
    given = dict(locals())
    w = {n: given[n] for n in WEIGHTS}
    mom = {n: given["m_" + n] for n in WEIGHTS}
    var = {n: given["v_" + n] for n in WEIGHTS}
    t = x.shape[1]
    wf = _gather_weights(w)
    loss, dx, full, small = _local_step(x.reshape(t, D_MODEL), loss_target.reshape(t, D_MODEL), wf, w)
    keep, recv = pair_exchange(_pack_parts(full, small, loss))
    parts = chip_exchange(pair_sum(keep, recv))
    g, delta, new_m, new_v = (_unpack_shard(a) for a in adamw(parts, _pack_shard(w), _pack_shard(mom), _pack_shard(var)))
    outs = [g["_extra"], dx.reshape(x.shape)]
    for d in (g, delta, new_m, new_v):
        outs += [d[n].reshape(w[n].shape) for n in WEIGHTS]
    return tuple(outs)
```

```python
import functools
import math

import jax
import jax.numpy as jnp
import numpy as np
from jax import lax
from jax.experimental import pallas as pl
from jax.experimental.pallas import tpu as pltpu

f32 = jnp.float32
bf16 = jnp.bfloat16
MXU_DTYPE = jnp.bfloat16

N_DEV = 8
D_MODEL = 1024
D_INNER = 2048
SSD_HEADS = 32
SSD_HEAD_DIM = 64
SSD_GROUPS = 4
D_STATE = 128
D_CONV = 5
CHUNK = 128
CONV_DIM = D_INNER + 2 * SSD_GROUPS * D_STATE
NORM_EPS = 1e-5
ATTN_HEAD_DIM = 64
DIL_PATTERNS = ((128, 1), (512, 4), (2048, 16))
HEADS_PER_PATTERN = 4
ATTN_HEADS = 12
ATTN_WIDTH = 768
ATTN_OUT = 256
D_FF = 4096
ALPHA = 2.0 ** 0.25
IN_SPLITS = (D_INNER, CONV_DIM, SSD_HEADS, SSD_HEADS, ATTN_WIDTH, ATTN_WIDTH, ATTN_WIDTH, 2 * D_MODEL)
IN_COLS = sum(IN_SPLITS)
ADAM_LR, ADAM_B1, ADAM_B2, ADAM_EPS, ADAM_WD, ADAM_STEP = 0.001, 0.9, 0.999, 1e-08, 0.01, 10
NEG_BIG = -1e30
VMEM_LIMIT = 56 * 1024 * 1024
MESH = pl.DeviceIdType.MESH

SMALL = ("b_gate", "conv_b", "dt_bias_f", "dt_bias_b", "a_log_f", "a_log_b", "d_skip", "ssd_norm_w",
         "ln1_g", "ln1_b", "ln2_g", "ln2_b")
WEIGHTS = ("w_in", "b_gate", "conv_w", "conv_b", "dt_bias_f", "dt_bias_b", "a_log_f", "a_log_b", "d_skip",
           "ssd_norm_w", "w_proj_ssd", "w_proj_attn", "w_out", "ln1_g", "ln1_b", "w_up", "w_down", "ln2_g", "ln2_b")
SMALL_SIZES = {"b_gate": 2 * D_MODEL, "conv_b": CONV_DIM, "dt_bias_f": 32, "dt_bias_b": 32, "a_log_f": 32, "a_log_b": 32,
               "d_skip": 32, "ssd_norm_w": D_INNER, "ln1_g": D_MODEL, "ln1_b": D_MODEL, "ln2_g": D_MODEL, "ln2_b": D_MODEL}
IN_SHARD = IN_COLS // N_DEV
ROWS_IN = 1200
ROWS_PS, ROWS_OUT, ROWS_UP, ROWS_DOWN, ROWS_PA = D_INNER // N_DEV, D_MODEL // N_DEV, D_FF // N_DEV, D_FF // N_DEV, 32
OFF_PS = ROWS_IN
OFF_OUT = OFF_PS + ROWS_PS
OFF_UP = OFF_OUT + ROWS_OUT
OFF_DOWN = OFF_UP + ROWS_UP
OFF_PA = OFF_DOWN + ROWS_DOWN
OFF_TAIL = OFF_PA + ROWS_PA
CONV_SHARD = D_CONV * CONV_DIM // N_DEV
TAIL_ELEMS = CONV_SHARD + sum(SMALL_SIZES.values()) + 1
ROWS_TAIL = 16
PACK_TILE = 128
PACK_ROWS = -(-(OFF_TAIL + ROWS_TAIL) // PACK_TILE) * PACK_TILE


def _cparams(sem=None, **kw):
    return pltpu.CompilerParams(dimension_semantics=sem, vmem_limit_bytes=VMEM_LIMIT, **kw)


def _mx(v):
    return v.astype(MXU_DTYPE)


def _dot(a, b):
    return jnp.dot(_mx(a), _mx(b), preferred_element_type=f32)


def _dot_nt(a, b):
    return lax.dot_general(_mx(a), _mx(b), (((1,), (1,)), ((), ())), preferred_element_type=f32)


def _dot_tn(a, b):
    return lax.dot_general(_mx(a), _mx(b), (((0,), (0,)), ((), ())), preferred_element_type=f32)


def _dot_exact(a, b):
    return jnp.dot(a, b, precision=lax.Precision.HIGHEST, preferred_element_type=f32)


def _sigmoid(v):
    return 1.0 / (1.0 + jnp.exp(-v))


def _pick(n, prefs):
    for p in prefs:
        if n % p == 0:
            return p
    return n


def mm_nn(a, b, name, out_dtype=f32, acc_in=None, acc_scale=1.0, tm=512):
    m, k = a.shape
    n = b.shape[1]
    tn = _pick(n, (1024, 768, 512, 384, 256, 128))
    tk = _pick(k, (2048, 1536, 1024, 512, 256, 128))
    nk = k // tk

    def body(*refs):
        if acc_in is None:
            a_ref, b_ref, o_ref, acc_ref = refs
        else:
            a_ref, b_ref, c_ref, o_ref, acc_ref = refs
        kk = pl.program_id(2)

        @pl.when(kk == 0)
        def _():
            acc_ref[...] = jnp.zeros_like(acc_ref)

        acc_ref[...] += _dot(a_ref[...], b_ref[...])

        @pl.when(kk == nk - 1)
        def _():
            r = acc_ref[...]
            if acc_in is not None:
                r = r + acc_scale * c_ref[...]
            o_ref[...] = r.astype(o_ref.dtype)

    in_specs = [pl.BlockSpec((tm, tk), lambda i, j, kk: (i, kk)), pl.BlockSpec((tk, tn), lambda i, j, kk: (kk, j))]
    args = [a, b]
    if acc_in is not None:
        in_specs.append(pl.BlockSpec((tm, tn), lambda i, j, kk: (i, j)))
        args.append(acc_in)
    return pl.pallas_call(
        body, name=name, grid=(m // tm, n // tn, nk), in_specs=in_specs,
        out_specs=pl.BlockSpec((tm, tn), lambda i, j, kk: (i, j)),
        out_shape=jax.ShapeDtypeStruct((m, n), out_dtype), scratch_shapes=[pltpu.VMEM((tm, tn), f32)],
        compiler_params=_cparams(("parallel", "parallel", "arbitrary")))(*args)


def mm_nt(a, b, name, out_dtype=f32, tm=512):
    m, k = a.shape
    n = b.shape[0]
    tn = _pick(n, (1024, 768, 512, 384, 256, 128))

    def body(a_ref, b_ref, o_ref):
        o_ref[...] = _dot_nt(a_ref[...], b_ref[...]).astype(o_ref.dtype)

    return pl.pallas_call(
        body, name=name, grid=(m // tm, n // tn),
        in_specs=[pl.BlockSpec((tm, k), lambda i, j: (i, 0)), pl.BlockSpec((tn, k), lambda i, j: (j, 0))],
        out_specs=pl.BlockSpec((tm, tn), lambda i, j: (i, j)), out_shape=jax.ShapeDtypeStruct((m, n), out_dtype),
        compiler_params=_cparams(("parallel", "parallel")))(a, b)


def mm_tn(a, b, name, tk=1024):
    t, m = a.shape
    n = b.shape[1]
    tm = _pick(m, (512, 256, 128))
    tn = _pick(n, (1024, 768, 512, 384, 256, 128))
    nk = t // tk

    def body(a_ref, b_ref, o_ref):
        kk = pl.program_id(2)

        @pl.when(kk == 0)
        def _():
            o_ref[...] = jnp.zeros_like(o_ref)

        o_ref[...] += _dot_tn(a_ref[...], b_ref[...])

    return pl.pallas_call(
        body, name=name, grid=(m // tm, n // tn, nk),
        in_specs=[pl.BlockSpec((tk, tm), lambda i, j, kk: (kk, i)), pl.BlockSpec((tk, tn), lambda i, j, kk: (kk, j))],
        out_specs=pl.BlockSpec((tm, tn), lambda i, j, kk: (i, j)),
        out_shape=jax.ShapeDtypeStruct((m, n), f32),
        compiler_params=_cparams(("parallel", "parallel", "arbitrary")))(a, b)


def _halo_specs(tb, cb, nt, off=0):
    r = tb // 8
    return [pl.BlockSpec((8, cb), lambda j, i: (jnp.maximum(i * r - 1, 0), j + off)),
            pl.BlockSpec((tb, cb), lambda j, i: (i, j + off)),
            pl.BlockSpec((8, cb), lambda j, i: (jnp.minimum((i + 1) * r, nt * r - 1), j + off))]


def _with_halo(prev_ref, own_ref, next_ref, i, nt):
    prev = jnp.where(i > 0, prev_ref[...].astype(f32), 0.0)
    nxt = jnp.where(i < nt - 1, next_ref[...].astype(f32), 0.0)
    return jnp.concatenate([prev, own_ref[...].astype(f32), nxt], axis=0)


def _shifted(xcat, s, tb):
    n = xcat.shape[0]
    return pltpu.roll(xcat, (-s) % n, 0)[8:8 + tb]


def conv_fwd(xbc, w8, b_row, tb=512, cb=512):
    t, c = xbc.shape
    nt = t // tb

    def body(prev_ref, own_ref, next_ref, w_ref, b_ref, o_ref):
        i = pl.program_id(1)
        xcat = _with_halo(prev_ref, own_ref, next_ref, i, nt)
        w = w_ref[...]
        pre = b_ref[...] + w[0:1] * _shifted(xcat, -2, tb)
        for k in range(1, D_CONV):
            pre = pre + w[k:k + 1] * _shifted(xcat, k - 2, tb)
        o_ref[...] = pre * _sigmoid(pre)

    return pl.pallas_call(
        body, name="conv_fwd", grid=(c // cb, nt),
        in_specs=_halo_specs(tb, cb, nt) + [pl.BlockSpec((8, cb), lambda j, i: (0, j)), pl.BlockSpec((1, cb), lambda j, i: (0, j))],
        out_specs=pl.BlockSpec((tb, cb), lambda j, i: (i, j)), out_shape=jax.ShapeDtypeStruct((t, c), f32),
        compiler_params=_cparams(("parallel", "parallel")))(xbc, xbc, xbc, w8, b_row)


def conv_bwd(xbc, xoff, grads, scales, w8, b_row, name, tb=512, cb=512):
    t, c = grads[0].shape
    nt = t // tb
    ng = len(grads)
    has_scale = [s is not None for s in scales]

    def body(*refs):
        i = pl.program_id(1)
        xr = refs[0:3]
        gr = [refs[3 + 3 * q: 6 + 3 * q] for q in range(ng)]
        pos = 3 + 3 * ng
        sr = []
        for q in range(ng):
            if has_scale[q]:
                sr.append(refs[pos])
                pos += 1
            else:
                sr.append(None)
        w_ref, b_ref, dx_ref, dw_ref, db_ref = refs[pos:pos + 5]
        xcat = _with_halo(*xr, i, nt)
        gcat = None
        for q in range(ng):
            gq = _with_halo(*gr[q], i, nt)
            if sr[q] is not None:
                gq = gq * sr[q][...]
            gcat = gq if gcat is None else gcat + gq
        w = w_ref[...]
        n = tb + 16
        pre = b_ref[...] + w[0:1] * pltpu.roll(xcat, 2, 0)
        for k in range(1, D_CONV):
            pre = pre + w[k:k + 1] * pltpu.roll(xcat, (2 - k) % n, 0)
        sg = _sigmoid(pre)
        dpre = gcat * sg * (1.0 + pre * (1.0 - sg))
        dx = w[0:1] * _shifted(dpre, 2, tb)
        for k in range(1, D_CONV):
            dx = dx + w[k:k + 1] * _shifted(dpre, 2 - k, tb)
        dx_ref[...] = dx.astype(dx_ref.dtype)
        dp_own = dpre[8:8 + tb]
        rows = [jnp.sum(dp_own * _shifted(xcat, k - 2, tb), axis=0, keepdims=True) for k in range(D_CONV)]
        dw = jnp.concatenate(rows + [jnp.zeros((8 - D_CONV, cb), f32)], axis=0)
        db = jnp.sum(dp_own, axis=0, keepdims=True)

        @pl.when(i == 0)
        def _():
            dw_ref[...] = jnp.zeros_like(dw_ref)
            db_ref[...] = jnp.zeros_like(db_ref)

        dw_ref[...] += dw
        db_ref[...] += db

    in_specs = _halo_specs(tb, cb, nt, xoff)
    args = [xbc] * 3
    for g in grads:
        in_specs += _halo_specs(tb, cb, nt)
        args += [g] * 3
    for s in scales:
        if s is not None:
            in_specs.append(pl.BlockSpec((1, cb), lambda j, i: (0, j)))
            args.append(s)
    in_specs += [pl.BlockSpec((8, cb), lambda j, i: (0, j)), pl.BlockSpec((1, cb), lambda j, i: (0, j))]
    args += [w8, b_row]
    return pl.pallas_call(
        body, name=name, grid=(c // cb, nt), in_specs=in_specs,
        out_specs=[pl.BlockSpec((tb, cb), lambda j, i: (i, j)), pl.BlockSpec((8, cb), lambda j, i: (0, j)),
                   pl.BlockSpec((1, cb), lambda j, i: (0, j))],
        out_shape=[jax.ShapeDtypeStruct((t, c), bf16), jax.ShapeDtypeStruct((8, c), f32), jax.ShapeDtypeStruct((1, c), f32)],
        compiler_params=_cparams(("parallel", "arbitrary")))(*args)


def dt_fwd(u_dt, bias_row, tb=1024):
    t = u_dt.shape[0]

    def body(u_ref, b_ref, o_ref):
        v = u_ref[...] + b_ref[...]
        sp = jnp.maximum(v, 0.0) + jnp.log(1.0 + jnp.exp(-jnp.abs(v)))
        lane = lax.broadcasted_iota(jnp.int32, v.shape, 1)
        o_ref[...] = jnp.where((lane & 127) < SSD_HEADS, sp, 0.0)

    return pl.pallas_call(
        body, name="dt_fwd", grid=(t // tb,),
        in_specs=[pl.BlockSpec((tb, 256), lambda i: (i, 0)), pl.BlockSpec((1, 256), lambda i: (0, 0))],
        out_specs=pl.BlockSpec((tb, 256), lambda i: (i, 0)), out_shape=jax.ShapeDtypeStruct((t, 256), f32),
        compiler_params=_cparams(("parallel",)))(u_dt, bias_row)


def dt_bwd(ddt_f, ddt_b, u_dt, bias_row, tb=1024):
    t = u_dt.shape[0]

    def body(gf_ref, gb_ref, u_ref, b_ref, du_ref, db_ref):
        g = jnp.concatenate([jnp.sum(gf_ref[...], axis=0), jnp.sum(gb_ref[...], axis=0)], axis=1)
        du = g * _sigmoid(u_ref[...] + b_ref[...])
        du_ref[...] = du.astype(du_ref.dtype)

        @pl.when(pl.program_id(0) == 0)
        def _():
            db_ref[...] = jnp.zeros_like(db_ref)

        db_ref[...] += jnp.sum(du, axis=0, keepdims=True)

    return pl.pallas_call(
        body, name="dt_bwd", grid=(t // tb,),
        in_specs=[pl.BlockSpec((4, tb, 128), lambda i: (0, i, 0)), pl.BlockSpec((4, tb, 128), lambda i: (0, i, 0)),
                  pl.BlockSpec((tb, 256), lambda i: (i, 0)), pl.BlockSpec((1, 256), lambda i: (0, 0))],
        out_specs=[pl.BlockSpec((tb, 256), lambda i: (i, 0)), pl.BlockSpec((1, 256), lambda i: (0, 0))],
        out_shape=[jax.ShapeDtypeStruct((t, 256), bf16), jax.ShapeDtypeStruct((1, 256), f32)],
        compiler_params=_cparams(("arbitrary",)))(ddt_f, ddt_b, u_dt, bias_row)


def _ssd_common(dt_blk, a_row, reverse):
    row = lax.broadcasted_iota(jnp.int32, (CHUNK, CHUNK), 0)
    col = lax.broadcasted_iota(jnp.int32, (CHUNK, CHUNK), 1)
    mask = (row <= col) if reverse else (row >= col)
    tri = mask.astype(f32)
    a = dt_blk * a_row
    acs = _dot_exact(tri, a)
    atot = jnp.sum(a, axis=0, keepdims=True)
    return mask, tri, a, acs, atot, col


def _lane_col(mat, lane_idx, h):
    return jnp.sum(jnp.where(lane_idx == h, mat, 0.0), axis=1, keepdims=True)


def ssd_fwd(xbc_c, dt2, a_rows, reverse, name):
    t = xbc_c.shape[0]
    nc = t // CHUNK
    d_off = 1 if reverse else 0

    def cidx(c):
        return nc - 1 - c if reverse else c

    def body(x_ref, b_ref, c_ref, dt_ref, a_ref, y_ref, hp_ref, h_scr, acst_scr):
        g = pl.program_id(0)
        c = pl.program_id(1)

        @pl.when(c == 0)
        def _():
            h_scr[...] = jnp.zeros_like(h_scr)

        dt_blk = dt_ref[...]
        mask, tri, a, acs, atot, lane = _ssd_common(dt_blk, a_ref[...], reverse)
        acst_scr[...] = acs.T
        bm = b_ref[...]
        cm = c_ref[...]
        cb = _dot_nt(cm, bm)
        half = lane >= SSD_HEAD_DIM
        sub_half = lax.broadcasted_iota(jnp.int32, (CHUNK, 1), 0) >= SSD_HEAD_DIM
        for j in range(4):
            x = x_ref[:, 128 * j:128 * (j + 1)]
            cols, dts, tots = [], [], []
            y = None
            for e in range(2):
                h = 8 * g + 2 * j + e
                col_h = _lane_col(acs, lane, h)
                row_h = acst_scr[pl.ds(h, 1), :]
                dt_h = _lane_col(dt_blk, lane, h)
                lmat = jnp.where(mask, jnp.exp(jnp.where(mask, col_h - row_h, 0.0)), 0.0)
                xdt_e = jnp.where(half == (e == 1), x * dt_h, 0.0)
                ye = _dot(cb * lmat, xdt_e)
                y = ye if y is None else y + ye
                cols.append(col_h)
                dts.append(dt_h)
                tots.append(jnp.sum(jnp.where(lane[0:1] == h, atot, 0.0), axis=1, keepdims=True))
            hp = h_scr[j]
            hp_ref[0, j] = hp
            ecol = jnp.where(half, jnp.exp(cols[1]), jnp.exp(cols[0]))
            y = y + _dot_nt(cm, hp) * ecol
            y_ref[:, 128 * j:128 * (j + 1)] = y
            dec = jnp.where(half, jnp.exp(tots[1] - cols[1]), jnp.exp(tots[0] - cols[0]))
            xdt = x * jnp.where(half, dts[1], dts[0])
            s_new = _dot_tn(xdt * dec, bm)
            cd = jnp.where(sub_half, jnp.exp(tots[1]), jnp.exp(tots[0]))
            h_scr[j] = cd * hp + s_new

    return pl.pallas_call(
        body, name=name, grid=(SSD_GROUPS, nc),
        in_specs=[pl.BlockSpec((CHUNK, 512), lambda g, c: (cidx(c), g)),
                  pl.BlockSpec((CHUNK, 128), lambda g, c: (cidx(c), 16 + g)),
                  pl.BlockSpec((CHUNK, 128), lambda g, c: (cidx(c), 20 + g)),
                  pl.BlockSpec((CHUNK, 128), lambda g, c: (cidx(c), d_off)),
                  pl.BlockSpec((1, 128), lambda g, c: (0, d_off))],
        out_specs=[pl.BlockSpec((CHUNK, 512), lambda g, c: (cidx(c), g)),
                   pl.BlockSpec((1, 4, 128, 128), lambda g, c: (cidx(c), g, 0, 0))],
        out_shape=[jax.ShapeDtypeStruct((t, D_INNER), f32), jax.ShapeDtypeStruct((nc, 16, 128, 128), f32)],
        scratch_shapes=[pltpu.VMEM((4, 128, 128), f32), pltpu.VMEM((CHUNK, CHUNK), f32)],
        compiler_params=_cparams(("parallel", "arbitrary")))(xbc_c, xbc_c, xbc_c, dt2, a_rows)


def ssd_bwd(xbc_c, dt2, a_rows, dy, hprev, reverse, name):
    t = xbc_c.shape[0]
    nc = t // CHUNK
    d_off = 1 if reverse else 0

    def cidx(c):
        return c if reverse else nc - 1 - c

    def body(x_ref, b_ref, c_ref, dt_ref, a_ref, dy_ref, hp_ref, dx_ref, db_ref, dc_ref, ddt_ref, da_ref,
             dh_scr, acst_scr):
        g = pl.program_id(0)
        c = pl.program_id(1)

        @pl.when(c == 0)
        def _():
            dh_scr[...] = jnp.zeros_like(dh_scr)
            da_ref[...] = jnp.zeros_like(da_ref)

        dt_blk = dt_ref[...]
        a_row = a_ref[...]
        mask, tri, a, acs, atot, lane = _ssd_common(dt_blk, a_row, reverse)
        acst_scr[...] = acs.T
        sub = lax.broadcasted_iota(jnp.int32, (CHUNK, CHUNK), 0)
        bm = b_ref[...]
        cm = c_ref[...]
        cb = _dot_nt(cm, bm)
        half = lane >= SSD_HEAD_DIM
        sub_half = sub[:, 0:1] >= SSD_HEAD_DIM
        dcb = jnp.zeros((CHUNK, CHUNK), f32)
        dacs = jnp.zeros((CHUNK, CHUNK), f32)
        dacs_t = jnp.zeros((CHUNK, CHUNK), f32)
        dtot = jnp.zeros((1, CHUNK), f32)
        ddt_x = jnp.zeros((CHUNK, CHUNK), f32)
        dbm = jnp.zeros((CHUNK, D_STATE), f32)
        dcm = jnp.zeros((CHUNK, D_STATE), f32)
        for j in range(4):
            x = x_ref[:, 128 * j:128 * (j + 1)]
            dyp = dy_ref[:, 128 * j:128 * (j + 1)]
            hp = hp_ref[0, j]
            dhn = dh_scr[j]
            cols, dts, tots, hs = [], [], [], []
            dxdt = None
            for e in range(2):
                h = 8 * g + 2 * j + e
                sel = half == (e == 1)
                col_h = _lane_col(acs, lane, h)
                row_h = acst_scr[pl.ds(h, 1), :]
                dt_h = _lane_col(dt_blk, lane, h)
                lmat = jnp.where(mask, jnp.exp(jnp.where(mask, col_h - row_h, 0.0)), 0.0)
                xdt_e = jnp.where(sel, x * dt_h, 0.0)
                dy_e = jnp.where(sel, dyp, 0.0)
                ml = _dot_nt(dy_e, xdt_e) * lmat
                dcb = dcb + ml
                w = ml * cb
                dacs = dacs + jnp.where(lane == h, jnp.sum(w, axis=1, keepdims=True), 0.0)
                dacs_t = dacs_t - jnp.where(sub == h, jnp.sum(w, axis=0, keepdims=True), 0.0)
                de = _dot_tn(cb * lmat, dy_e)
                dxdt = de if dxdt is None else dxdt + de
                cols.append(col_h)
                dts.append(dt_h)
                tots.append(jnp.sum(jnp.where(lane[0:1] == h, atot, 0.0), axis=1, keepdims=True))
                hs.append(h)
            ecol = jnp.where(half, jnp.exp(cols[1]), jnp.exp(cols[0]))
            dec = jnp.where(half, jnp.exp(tots[1] - cols[1]), jnp.exp(tots[0] - cols[0]))
            cd = jnp.where(sub_half, jnp.exp(tots[1]), jnp.exp(tots[0]))
            dtp = jnp.where(half, dts[1], dts[0])
            xdt = x * dtp
            yoff = _dot_nt(cm, hp) * ecol
            dye = dyp * ecol
            dcm = dcm + _dot(dye, hp)
            dhp = _dot_tn(dye, cm)
            gmat = _dot_nt(bm, dhn)
            dxdt = dxdt + dec * gmat
            dbm = dbm + _dot(xdt * dec, dhn)
            r_off = dyp * yoff
            r_dec = xdt * gmat * dec
            r_x = dxdt * x
            hh = dhn * hp
            for e in range(2):
                sel = half == (e == 1)
                h = hs[e]
                s_off = jnp.sum(jnp.where(sel, r_off, 0.0), axis=1, keepdims=True)
                s_dec = jnp.sum(jnp.where(sel, r_dec, 0.0), axis=1, keepdims=True)
                dacs = dacs + jnp.where(lane == h, s_off - s_dec, 0.0)
                dcd = jnp.sum(jnp.sum(jnp.where(sub_half == (e == 1), hh, 0.0), axis=1, keepdims=True), axis=0, keepdims=True)
                tot_e = jnp.sum(s_dec, axis=0, keepdims=True) + jnp.exp(tots[e]) * dcd
                dtot = dtot + jnp.where(lane[0:1] == h, tot_e, 0.0)
                ddt_x = ddt_x + jnp.where(lane == h, jnp.sum(jnp.where(sel, r_x, 0.0), axis=1, keepdims=True), 0.0)
            dx_ref[:, 128 * j:128 * (j + 1)] = dxdt * dtp
            dh_scr[j] = cd * dhn + dhp
        dcm = dcm + _dot(dcb, bm)
        dbm = dbm + _dot_tn(dcb, cm)
        db_ref[...] = dbm
        dc_ref[...] = dcm
        dacs = dacs + dacs_t.T
        da = _dot_exact(tri.T, dacs) + dtot
        ddt_ref[0] = da * a_row + ddt_x
        da_ref[0] += jnp.sum(da * dt_blk, axis=0, keepdims=True)

    return pl.pallas_call(
        body, name=name, grid=(SSD_GROUPS, nc),
        in_specs=[pl.BlockSpec((CHUNK, 512), lambda g, c: (cidx(c), g)),
                  pl.BlockSpec((CHUNK, 128), lambda g, c: (cidx(c), 16 + g)),
                  pl.BlockSpec((CHUNK, 128), lambda g, c: (cidx(c), 20 + g)),
                  pl.BlockSpec((CHUNK, 128), lambda g, c: (cidx(c), d_off)),
                  pl.BlockSpec((1, 128), lambda g, c: (0, d_off)),
                  pl.BlockSpec((CHUNK, 512), lambda g, c: (cidx(c), g)),
                  pl.BlockSpec((1, 4, 128, 128), lambda g, c: (cidx(c), g, 0, 0))],
        out_specs=[pl.BlockSpec((CHUNK, 512), lambda g, c: (cidx(c), g)),
                   pl.BlockSpec((CHUNK, 128), lambda g, c: (cidx(c), g)),
                   pl.BlockSpec((CHUNK, 128), lambda g, c: (cidx(c), g)),
                   pl.BlockSpec((1, CHUNK, 128), lambda g, c: (g, cidx(c), 0)),
                   pl.BlockSpec((1, 1, 128), lambda g, c: (g, 0, 0))],
        out_shape=[jax.ShapeDtypeStruct((t, D_INNER), f32), jax.ShapeDtypeStruct((t, 512), f32),
                   jax.ShapeDtypeStruct((t, 512), f32), jax.ShapeDtypeStruct((4, t, 128), f32),
                   jax.ShapeDtypeStruct((4, 1, 128), f32)],
        scratch_shapes=[pltpu.VMEM((4, 128, 128), f32), pltpu.VMEM((CHUNK, CHUNK), f32)],
        compiler_params=_cparams(("parallel", "arbitrary")))(xbc_c, xbc_c, xbc_c, dt2, a_rows, dy, hprev)


def tail_fwd(y_f, y_b, xbc_c, z, dskip_row, nw_row, tb=512):
    t = y_f.shape[0]

    def body(yf_ref, yb_ref, x_ref, z_ref, d_ref, w_ref, o_ref):
        zz = z_ref[...]
        y = (yf_ref[...] + yb_ref[...] + d_ref[...] * x_ref[...]) * (zz * _sigmoid(zz))
        rstd = lax.rsqrt(jnp.mean(y * y, axis=1, keepdims=True) + NORM_EPS)
        o_ref[...] = (y * rstd * w_ref[...]).astype(o_ref.dtype)

    blk = pl.BlockSpec((tb, 512), lambda i, g: (i, g))
    row = pl.BlockSpec((1, 512), lambda i, g: (0, g))
    return pl.pallas_call(
        body, name="tail_fwd", grid=(t // tb, SSD_GROUPS), in_specs=[blk, blk, blk, blk, row, row], out_specs=blk,
        out_shape=jax.ShapeDtypeStruct((t, D_INNER), bf16),
        compiler_params=_cparams(("parallel", "parallel")))(y_f, y_b, xbc_c, z, dskip_row, nw_row)


def tail_bwd(dyn, y_f, y_b, xbc_c, z, dskip_row, nw_row, tb=512):
    t = y_f.shape[0]

    def body(g_ref, yf_ref, yb_ref, x_ref, z_ref, d_ref, w_ref, dy_ref, dz_ref, dw_ref, dd_ref):
        zz = z_ref[...]
        sg = _sigmoid(zz)
        sl = zz * sg
        x = x_ref[...]
        y = yf_ref[...] + yb_ref[...] + d_ref[...] * x
        yz = y * sl
        rstd = lax.rsqrt(jnp.mean(yz * yz, axis=1, keepdims=True) + NORM_EPS)
        yhat = yz * rstd
        g = g_ref[...]
        dyhat = g * w_ref[...]
        dyz = rstd * (dyhat - yhat * jnp.mean(dyhat * yhat, axis=1, keepdims=True))
        dy = dyz * sl
        dy_ref[...] = dy
        dz_ref[...] = (dyz * y * sg * (1.0 + zz * (1.0 - sg))).astype(dz_ref.dtype)

        @pl.when(pl.program_id(1) == 0)
        def _():
            dw_ref[...] = jnp.zeros_like(dw_ref)
            dd_ref[...] = jnp.zeros_like(dd_ref)

        dw_ref[...] += jnp.sum(g * yhat, axis=0, keepdims=True)
        dd_ref[...] += jnp.sum(dy * x, axis=0, keepdims=True)

    blk = pl.BlockSpec((tb, 512), lambda g, i: (i, g))
    row = pl.BlockSpec((1, 512), lambda g, i: (0, g))
    return pl.pallas_call(
        body, name="tail_bwd", grid=(SSD_GROUPS, t // tb), in_specs=[blk, blk, blk, blk, blk, row, row],
        out_specs=[blk, blk, row, row],
        out_shape=[jax.ShapeDtypeStruct((t, D_INNER), f32), jax.ShapeDtypeStruct((t, D_INNER), bf16),
                   jax.ShapeDtypeStruct((1, D_INNER), f32), jax.ShapeDtypeStruct((1, D_INNER), f32)],
        compiler_params=_cparams(("parallel", "arbitrary")))(dyn, y_f, y_b, xbc_c, z, dskip_row, nw_row)


def _slopes(p):
    return [2.0 ** (-8.0 * (HEADS_PER_PATTERN * p + j + 1) / ATTN_HEADS) for j in range(HEADS_PER_PATTERN)]


def _win_specs(nq, col_of):
    return [pl.BlockSpec((64, 256), lambda r, i: (jnp.maximum(2 * i - 1, 0), col_of(r))),
            pl.BlockSpec((128, 256), lambda r, i: (i, col_of(r))),
            pl.BlockSpec((64, 256), lambda r, i: (jnp.minimum(2 * i + 2, 2 * nq - 1), col_of(r)))]


def _lane_head(shape):
    return lax.broadcasted_iota(jnp.int32, shape, 1) >> 6


def _stack_heads(m):
    lane_head = _lane_head(m.shape)
    return jnp.concatenate([jnp.where(lane_head == j, m, 0.0) for j in range(HEADS_PER_PATTERN)], axis=0)


def _unstack_heads(m4, n):
    lane_head = _lane_head((n, 256))
    out = jnp.where(lane_head == 0, m4[0:n], 0.0)
    for j in range(1, HEADS_PER_PATTERN):
        out = out + jnp.where(lane_head == j, m4[j * n:(j + 1) * n], 0.0)
    return out


def _head_cols(m, n):
    lane = lax.broadcasted_iota(jnp.int32, (n, 256), 1)
    return jnp.concatenate([jnp.sum(jnp.where(lane == ATTN_HEAD_DIM * j, m, 0.0), axis=1, keepdims=True)
                            for j in range(HEADS_PER_PATTERN)], axis=0)


def _q_scores(q, kcat, i, nq, p, dil):
    s = _dot_nt(_stack_heads(q * 0.125), kcat)
    row = lax.broadcasted_iota(jnp.int32, s.shape, 0)
    col = lax.broadcasted_iota(jnp.int32, s.shape, 1)
    rel = col - 64 - (row & 127)
    valid = (jnp.abs(rel) <= 64) & ((i > 0) | (col >= 64)) & ((i < nq - 1) | (col < 192))
    sl = _slopes(p)
    hd = row >> 7
    slope = jnp.where(hd == 0, sl[0], jnp.where(hd == 1, sl[1], jnp.where(hd == 2, sl[2], sl[3])))
    s = s - slope * (jnp.abs(rel) * dil).astype(f32)
    return jnp.where(valid, s, NEG_BIG)


def attn_fwd(q, k, v, p, dil, name):
    l = q.shape[0]
    nq = l // 128

    def body(q_ref, kp_ref, ko_ref, kn_ref, vp_ref, vo_ref, vn_ref, o_ref, lse_ref):
        i = pl.program_id(1)
        kcat = jnp.concatenate([kp_ref[...], ko_ref[...], kn_ref[...]], axis=0)
        vcat = jnp.concatenate([vp_ref[...], vo_ref[...], vn_ref[...]], axis=0)
        s = _q_scores(q_ref[...], kcat, i, nq, p, dil)
        m = jnp.max(s, axis=1, keepdims=True)
        pr = jnp.exp(s - m)
        den = jnp.sum(pr, axis=1, keepdims=True)
        o4 = _dot(pr, vcat) / den
        o_ref[...] = _unstack_heads(o4, 128)
        lse_ref[...] = _unstack_heads(jnp.broadcast_to(m + jnp.log(den), (512, 256)), 128)

    col = lambda r: r
    return pl.pallas_call(
        body, name=name, grid=(dil, nq),
        in_specs=[pl.BlockSpec((128, 256), lambda r, i: (i, r))] + _win_specs(nq, col) + _win_specs(nq, col),
        out_specs=[pl.BlockSpec((128, 256), lambda r, i: (i, r))] * 2,
        out_shape=[jax.ShapeDtypeStruct(q.shape, f32)] * 2,
        compiler_params=_cparams(("parallel", "parallel")))(q, k, k, k, v, v, v)


def attn_combine(os_, lses, tb=1024):
    t = os_[0].shape[0]

    def body(o0, o1, o2, l0, l1, l2, y_ref, lse_ref):
        a0, a1, a2 = l0[...], l1[...], l2[...]
        m = jnp.maximum(jnp.maximum(a0, a1), a2)
        e0, e1, e2 = jnp.exp(a0 - m), jnp.exp(a1 - m), jnp.exp(a2 - m)
        den = e0 + e1 + e2
        y_ref[...] = (e0 * o0[...] + e1 * o1[...] + e2 * o2[...]) / den
        lse_ref[...] = m + jnp.log(den)

    blk = pl.BlockSpec((tb, 256), lambda i: (i, 0))
    return pl.pallas_call(
        body, name="attn_combine", grid=(t // tb,), in_specs=[blk] * 6, out_specs=[blk, blk],
        out_shape=[jax.ShapeDtypeStruct((t, 256), f32)] * 2,
        compiler_params=_cparams(("parallel",)))(*os_, *lses)


def attn_delta(dy, y, tb=1024):
    t = dy.shape[0]

    def body(dy_ref, y_ref, d_ref):
        pr = dy_ref[...] * y_ref[...]
        lane_head = _lane_head(pr.shape)
        out = jnp.zeros_like(pr)
        for j in range(HEADS_PER_PATTERN):
            sj = jnp.sum(jnp.where(lane_head == j, pr, 0.0), axis=1, keepdims=True)
            out = out + jnp.where(lane_head == j, sj, 0.0)
        d_ref[...] = out

    blk = pl.BlockSpec((tb, 256), lambda i: (i, 0))
    return pl.pallas_call(body, name="attn_delta", grid=(t // tb,), in_specs=[blk, blk], out_specs=blk,
                          out_shape=jax.ShapeDtypeStruct((t, 256), f32),
                          compiler_params=_cparams(("parallel",)))(dy, y)


def attn_dq(q, k, v, dy, lse, delta, p, dil, name):
    l = q.shape[0]
    nq = l // 128

    def body(q_ref, kp_ref, ko_ref, kn_ref, vp_ref, vo_ref, vn_ref, dy_ref, lse_ref, d_ref, dq_ref):
        i = pl.program_id(1)
        kcat = jnp.concatenate([kp_ref[...], ko_ref[...], kn_ref[...]], axis=0)
        vcat = jnp.concatenate([vp_ref[...], vo_ref[...], vn_ref[...]], axis=0)
        s = _q_scores(q_ref[...], kcat, i, nq, p, dil)
        pr = jnp.exp(s - _head_cols(lse_ref[...], 128))
        dp = _dot_nt(_stack_heads(dy_ref[...]), vcat)
        ds = pr * (dp - _head_cols(d_ref[...], 128))
        dq_ref[...] = (_unstack_heads(_dot(ds, kcat), 128) * 0.125).astype(dq_ref.dtype)

    col = lambda r: r
    own = pl.BlockSpec((128, 256), lambda r, i: (i, r))
    return pl.pallas_call(
        body, name=name, grid=(dil, nq),
        in_specs=[own] + _win_specs(nq, col) + _win_specs(nq, col) + [own, own, own], out_specs=own,
        out_shape=jax.ShapeDtypeStruct(q.shape, bf16),
        compiler_params=_cparams(("parallel", "parallel")))(q, k, k, k, v, v, v, dy, lse, delta)


def attn_dkv(q, k, v, dy, lse, delta, p, dil, name):
    l = q.shape[0]
    nq = l // 128

    def body(qp_ref, qo_ref, qn_ref, gp_ref, go_ref, gn_ref, lp_ref, lo_ref, ln_ref, dp_ref, do_ref, dn_ref,
             k_ref, v_ref, dk_ref, dv_ref):
        i = pl.program_id(1)
        cat = lambda a, b, c: jnp.concatenate([a[...], b[...], c[...]], axis=0)
        q4 = _stack_heads(cat(qp_ref, qo_ref, qn_ref) * 0.125)
        dy4 = _stack_heads(cat(gp_ref, go_ref, gn_ref))
        lse4 = _head_cols(cat(lp_ref, lo_ref, ln_ref), 256)
        del4 = _head_cols(cat(dp_ref, do_ref, dn_ref), 256)
        s = _dot_nt(q4, k_ref[...])
        row = lax.broadcasted_iota(jnp.int32, s.shape, 0)
        col = lax.broadcasted_iota(jnp.int32, s.shape, 1)
        qoff = row & 255
        rel = col - (qoff - 64)
        valid = (jnp.abs(rel) <= 64) & ((i > 0) | (qoff >= 64)) & ((i < nq - 1) | (qoff < 192))
        sl = _slopes(p)
        hd = row >> 8
        slope = jnp.where(hd == 0, sl[0], jnp.where(hd == 1, sl[1], jnp.where(hd == 2, sl[2], sl[3])))
        s = s - slope * (jnp.abs(rel) * dil).astype(f32)
        pr = jnp.where(valid, jnp.exp(jnp.where(valid, s, NEG_BIG) - lse4), 0.0)
        dpm = _dot_nt(dy4, v_ref[...])
        ds = pr * (dpm - del4)
        dv_ref[...] = _dot_tn(pr, dy4).astype(dv_ref.dtype)
        dk_ref[...] = _dot_tn(ds, q4).astype(dk_ref.dtype)

    col = lambda r: r
    own = pl.BlockSpec((128, 256), lambda r, i: (i, r))
    win = _win_specs(nq, col)
    return pl.pallas_call(
        body, name=name, grid=(dil, nq), in_specs=win * 4 + [own, own], out_specs=[own, own],
        out_shape=[jax.ShapeDtypeStruct(q.shape, bf16)] * 2,
        compiler_params=_cparams(("parallel", "parallel")))(q, q, q, dy, dy, dy, lse, lse, lse, delta, delta, delta, k, v)


def merge_fwd(u_gate, bg_row, y_ssd, y_att, tb=512):
    t = y_ssd.shape[0]

    def body(ga_ref, gb_ref, ba_ref, bb_ref, ys_ref, ya_ref, o_ref):
        o_ref[...] = (_sigmoid(ga_ref[...] + ba_ref[...]) * ys_ref[...]
                      + _sigmoid(gb_ref[...] + bb_ref[...]) * ya_ref[...]).astype(o_ref.dtype)

    blk = pl.BlockSpec((tb, 512), lambda i, j: (i, j))
    blk2 = pl.BlockSpec((tb, 512), lambda i, j: (i, 2 + j))
    row = pl.BlockSpec((1, 512), lambda i, j: (0, j))
    row2 = pl.BlockSpec((1, 512), lambda i, j: (0, 2 + j))
    return pl.pallas_call(
        body, name="merge_fwd", grid=(t // tb, 2), in_specs=[blk, blk2, row, row2, blk, blk], out_specs=blk,
        out_shape=jax.ShapeDtypeStruct((t, D_MODEL), bf16),
        compiler_params=_cparams(("parallel", "parallel")))(u_gate, u_gate, bg_row, bg_row, y_ssd, y_att)


def merge_bwd(dm, u_gate, bg_row, y_ssd, y_att, tb=512):
    t = dm.shape[0]

    def body(dm_ref, ga_ref, gb_ref, ba_ref, bb_ref, ys_ref, ya_ref, dys_ref, dya_ref, dga_ref, dgb_ref, dba_ref, dbb_ref):
        d = dm_ref[...]
        sa = _sigmoid(ga_ref[...] + ba_ref[...])
        sb = _sigmoid(gb_ref[...] + bb_ref[...])
        dys_ref[...] = (d * sa).astype(dys_ref.dtype)
        dya_ref[...] = (d * sb).astype(dya_ref.dtype)
        dla = d * ys_ref[...] * sa * (1.0 - sa)
        dlb = d * ya_ref[...] * sb * (1.0 - sb)
        dga_ref[...] = dla.astype(dga_ref.dtype)
        dgb_ref[...] = dlb.astype(dgb_ref.dtype)

        @pl.when(pl.program_id(1) == 0)
        def _():
            dba_ref[...] = jnp.zeros_like(dba_ref)
            dbb_ref[...] = jnp.zeros_like(dbb_ref)

        dba_ref[...] += jnp.sum(dla, axis=0, keepdims=True)
        dbb_ref[...] += jnp.sum(dlb, axis=0, keepdims=True)

    blk = pl.BlockSpec((tb, 512), lambda j, i: (i, j))
    blk2 = pl.BlockSpec((tb, 512), lambda j, i: (i, 2 + j))
    row = pl.BlockSpec((1, 512), lambda j, i: (0, j))
    row2 = pl.BlockSpec((1, 512), lambda j, i: (0, 2 + j))
    act = jax.ShapeDtypeStruct((t, D_MODEL), bf16)
    vec = jax.ShapeDtypeStruct((1, D_MODEL), f32)
    return pl.pallas_call(
        body, name="merge_bwd", grid=(2, t // tb), in_specs=[blk, blk, blk2, row, row2, blk, blk],
        out_specs=[blk, blk, blk, blk, row, row], out_shape=[act, act, act, act, vec, vec],
        compiler_params=_cparams(("parallel", "arbitrary")))(dm, u_gate, u_gate, bg_row, bg_row, y_ssd, y_att)


def _ln_stats(r):
    mu = jnp.mean(r, axis=1, keepdims=True)
    xc = r - mu
    rstd = lax.rsqrt(jnp.mean(xc * xc, axis=1, keepdims=True) + NORM_EPS)
    return xc * rstd, rstd


def _ln_bwd(dy, xhat, rstd, g_row):
    dxh = dy * g_row
    return rstd * (dxh - jnp.mean(dxh, axis=1, keepdims=True) - xhat * jnp.mean(dxh * xhat, axis=1, keepdims=True))


def ln1_fwd(x, mix, g_row, b_row, tb=512):
    t = x.shape[0]

    def body(x_ref, m_ref, g_ref, b_ref, o_ref):
        xhat, _ = _ln_stats(ALPHA * x_ref[...] + m_ref[...])
        o_ref[...] = xhat * g_ref[...] + b_ref[...]

    blk = pl.BlockSpec((tb, D_MODEL), lambda i: (i, 0))
    row = pl.BlockSpec((1, D_MODEL), lambda i: (0, 0))
    return pl.pallas_call(body, name="ln1_fwd", grid=(t // tb,), in_specs=[blk, blk, row, row], out_specs=blk,
                          out_shape=jax.ShapeDtypeStruct((t, D_MODEL), f32),
                          compiler_params=_cparams(("parallel",)))(x, mix, g_row, b_row)


def ln1_bwd(dh, x, mix, g_row, tb=512):
    t = x.shape[0]

    def body(dh_ref, x_ref, m_ref, g_ref, dr_ref, dg_ref, db_ref):
        xhat, rstd = _ln_stats(ALPHA * x_ref[...] + m_ref[...])
        dy = dh_ref[...]
        dr_ref[...] = _ln_bwd(dy, xhat, rstd, g_ref[...])

        @pl.when(pl.program_id(0) == 0)
        def _():
            dg_ref[...] = jnp.zeros_like(dg_ref)
            db_ref[...] = jnp.zeros_like(db_ref)

        dg_ref[...] += jnp.sum(dy * xhat, axis=0, keepdims=True)
        db_ref[...] += jnp.sum(dy, axis=0, keepdims=True)

    blk = pl.BlockSpec((tb, D_MODEL), lambda i: (i, 0))
    row = pl.BlockSpec((1, D_MODEL), lambda i: (0, 0))
    return pl.pallas_call(
        body, name="ln1_bwd", grid=(t // tb,), in_specs=[blk, blk, blk, row], out_specs=[blk, row, row],
        out_shape=[jax.ShapeDtypeStruct((t, D_MODEL), f32), jax.ShapeDtypeStruct((1, D_MODEL), f32),
                   jax.ShapeDtypeStruct((1, D_MODEL), f32)],
        compiler_params=_cparams(("arbitrary",)))(dh, x, mix, g_row)


def ln2_loss(h1, f, g_row, b_row, target, tb=512):
    t = h1.shape[0]

    def body(h_ref, f_ref, g_ref, b_ref, t_ref, dr_ref, dg_ref, db_ref, loss_ref):
        xhat, rstd = _ln_stats(ALPHA * h_ref[...] + f_ref[...])
        g = g_ref[...]
        err = xhat * g + b_ref[...] - t_ref[...]
        dy = err * (1.0 / D_MODEL)
        dr_ref[...] = _ln_bwd(dy, xhat, rstd, g)

        @pl.when(pl.program_id(0) == 0)
        def _():
            dg_ref[...] = jnp.zeros_like(dg_ref)
            db_ref[...] = jnp.zeros_like(db_ref)
            loss_ref[...] = jnp.zeros_like(loss_ref)

        dg_ref[...] += jnp.sum(dy * xhat, axis=0, keepdims=True)
        db_ref[...] += jnp.sum(dy, axis=0, keepdims=True)
        part = jnp.sum(jnp.mean(err * err, axis=1, keepdims=True), axis=0, keepdims=True)
        loss_ref[...] += 0.5 * part

    blk = pl.BlockSpec((tb, D_MODEL), lambda i: (i, 0))
    row = pl.BlockSpec((1, D_MODEL), lambda i: (0, 0))
    return pl.pallas_call(
        body, name="ln2_loss", grid=(t // tb,), in_specs=[blk, blk, row, row, blk],
        out_specs=[blk, row, row, pl.BlockSpec((8, 128), lambda i: (0, 0))],
        out_shape=[jax.ShapeDtypeStruct((t, D_MODEL), f32), jax.ShapeDtypeStruct((1, D_MODEL), f32),
                   jax.ShapeDtypeStruct((1, D_MODEL), f32), jax.ShapeDtypeStruct((8, 128), f32)],
        compiler_params=_cparams(("arbitrary",)))(h1, f, g_row, b_row, target)


def relu2_fwd(a, tb=512):
    t, n = a.shape

    def body(a_ref, o_ref):
        r = jnp.maximum(a_ref[...], 0.0)
        o_ref[...] = (r * r).astype(o_ref.dtype)

    blk = pl.BlockSpec((tb, 1024), lambda i, j: (i, j))
    return pl.pallas_call(body, name="relu2_fwd", grid=(t // tb, n // 1024), in_specs=[blk], out_specs=blk,
                          out_shape=jax.ShapeDtypeStruct((t, n), bf16),
                          compiler_params=_cparams(("parallel", "parallel")))(a)


def relu2_bwd(dp, a, tb=512):
    t, n = a.shape

    def body(dp_ref, a_ref, o_ref):
        o_ref[...] = (dp_ref[...] * (2.0 * jnp.maximum(a_ref[...], 0.0))).astype(o_ref.dtype)

    blk = pl.BlockSpec((tb, 1024), lambda i, j: (i, j))
    return pl.pallas_call(body, name="relu2_bwd", grid=(t // tb, n // 1024), in_specs=[blk, blk], out_specs=blk,
                          out_shape=jax.ShapeDtypeStruct((t, n), bf16),
                          compiler_params=_cparams(("parallel", "parallel")))(dp, a)


def adamw(parts, w, m, v):
    rows = w.shape[0]
    c1 = 1.0 - ADAM_B1 ** ADAM_STEP
    c2 = 1.0 - ADAM_B2 ** ADAM_STEP

    def body(p_ref, w_ref, m_ref, v_ref, g_ref, d_ref, nm_ref, nv_ref):
        g = ((p_ref[0] + p_ref[1]) + p_ref[2]) + p_ref[3]
        nm = ADAM_B1 * m_ref[...] + (1.0 - ADAM_B1) * g
        nv = ADAM_B2 * v_ref[...] + (1.0 - ADAM_B2) * (g * g)
        g_ref[...] = g
        nm_ref[...] = nm
        nv_ref[...] = nv
        d_ref[...] = -ADAM_LR * ((nm / c1) / (jnp.sqrt(nv / c2) + ADAM_EPS) + ADAM_WD * w_ref[...])

    blk = pl.BlockSpec((PACK_TILE, 1024), lambda i: (i, 0))
    out = jax.ShapeDtypeStruct((rows, 1024), f32)
    return pl.pallas_call(
        body, name="adamw", grid=(rows // PACK_TILE,),
        in_specs=[pl.BlockSpec((4, PACK_TILE, 1024), lambda i: (0, i, 0)), blk, blk, blk], out_specs=[blk] * 4,
        out_shape=[out] * 4, compiler_params=_cparams(("parallel",)))(parts, w, m, v)


def pair_sum(parts, recv, core):
    rows = parts.shape[1]

    def body(c_ref, a_ref, b_ref, o_ref):
        o_ref[...] = a_ref[...] + b_ref[...]

    grid_spec = pltpu.PrefetchScalarGridSpec(
        num_scalar_prefetch=1, grid=(4, rows // PACK_TILE),
        in_specs=[pl.BlockSpec((1, PACK_TILE, 1024), lambda j, i, c_ref: (2 * j + c_ref[0], i, 0)),
                  pl.BlockSpec((1, PACK_TILE, 1024), lambda j, i, c_ref: (j, i, 0))],
        out_specs=pl.BlockSpec((1, PACK_TILE, 1024), lambda j, i, c_ref: (j, i, 0)))
    return pl.pallas_call(body, name="pair_sum", grid_spec=grid_spec, out_shape=jax.ShapeDtypeStruct(recv.shape, f32),
                          compiler_params=_cparams(("parallel", "parallel")))(core, parts, recv)


def _place():
    return lax.axis_index("x"), lax.axis_index("y"), lax.axis_index("c")


def all_gather_blocks(shard):
    rows, cols = shard.shape

    def body(x_ref, out_ref, send_sems, recv_sems, local_sem):
        x, y, c = _place()
        me, sibling = (x, y, c), (x, y, 1 - c)
        chips = [(1 - x, y), (x, 1 - y), (1 - x, 1 - y)]

        def slot(px, py, pc):
            return out_ref.at[4 * px + 2 * py + pc]

        def copy(k, block, to, src=None):
            return pltpu.make_async_remote_copy(
                src_ref=slot(*block) if src is None else src, dst_ref=slot(*block), send_sem=send_sems.at[k],
                recv_sem=recv_sems.at[k], device_id=to, device_id_type=MESH)

        mine = pltpu.make_async_copy(x_ref, slot(*me), local_sem)
        mine.start()
        first = [copy(0, me, sibling, src=x_ref)]
        first += [copy(1 + j, me, (*chip, c), src=x_ref) for j, chip in enumerate(chips)]
        for cp in first:
            cp.start()
        passed = [copy(4 + j, (*chip, c), sibling) for j, chip in enumerate(chips)]
        for j, chip in enumerate(chips):
            copy(1 + j, (*chip, c), me).wait_recv()
            passed[j].start()
        copy(0, sibling, me).wait_recv()
        for j, chip in enumerate(chips):
            copy(4 + j, (*chip, 1 - c), me).wait_recv()
        for cp in first + passed:
            cp.wait_send()
        mine.wait()

    return pl.pallas_call(
        body, name="all_gather_blocks", out_shape=jax.ShapeDtypeStruct((N_DEV, rows, cols), shard.dtype),
        in_specs=[pl.BlockSpec(memory_space=pl.ANY)], out_specs=pl.BlockSpec(memory_space=pl.ANY),
        scratch_shapes=[pltpu.SemaphoreType.DMA((7,)), pltpu.SemaphoreType.DMA((7,)), pltpu.SemaphoreType.DMA],
        compiler_params=pltpu.CompilerParams(has_side_effects=True))(shard)


def pair_exchange(parts):
    _, rows, cols = parts.shape

    def body(p_ref, recv_ref, send_sems, recv_sems):
        x, y, c = _place()
        copies = [pltpu.make_async_remote_copy(
            src_ref=p_ref.at[2 * j + 1 - c], dst_ref=recv_ref.at[j], send_sem=send_sems.at[j], recv_sem=recv_sems.at[j],
            device_id=(x, y, 1 - c), device_id_type=MESH) for j in range(4)]
        for cp in copies:
            cp.start()
        for cp in copies:
            cp.wait_recv()
        for cp in copies:
            cp.wait_send()

    return pl.pallas_call(
        body, name="pair_exchange", out_shape=jax.ShapeDtypeStruct((4, rows, cols), parts.dtype),
        in_specs=[pl.BlockSpec(memory_space=pl.ANY)], out_specs=pl.BlockSpec(memory_space=pl.ANY),
        scratch_shapes=[pltpu.SemaphoreType.DMA((4,)), pltpu.SemaphoreType.DMA((4,))],
        compiler_params=pltpu.CompilerParams(has_side_effects=True))(parts)


def chip_exchange(parts):
    _, rows, cols = parts.shape

    def body(p_ref, out_ref, send_sems, recv_sems, local_sem):
        x, y, c = _place()
        mine = 2 * x + y
        flips = [(x, 1 - y), (1 - x, y), (1 - x, 1 - y)]
        local = pltpu.make_async_copy(p_ref.at[mine], out_ref.at[mine], local_sem)
        local.start()
        sends = [pltpu.make_async_remote_copy(
            src_ref=p_ref.at[2 * px + py], dst_ref=out_ref.at[mine], send_sem=send_sems.at[k], recv_sem=recv_sems.at[k],
            device_id=(px, py, c), device_id_type=MESH) for k, (px, py) in enumerate(flips)]
        for cp in sends:
            cp.start()
        for k, (px, py) in enumerate(flips):
            pltpu.make_async_remote_copy(
                src_ref=p_ref.at[mine], dst_ref=out_ref.at[2 * px + py], send_sem=send_sems.at[k],
                recv_sem=recv_sems.at[k], device_id=(px, py, c), device_id_type=MESH).wait_recv()
        for cp in sends:
            cp.wait_send()
        local.wait()

    return pl.pallas_call(
        body, name="chip_exchange", out_shape=jax.ShapeDtypeStruct((4, rows, cols), parts.dtype),
        in_specs=[pl.BlockSpec(memory_space=pl.ANY)], out_specs=pl.BlockSpec(memory_space=pl.ANY),
        scratch_shapes=[pltpu.SemaphoreType.DMA((3,)), pltpu.SemaphoreType.DMA((3,)), pltpu.SemaphoreType.DMA],
        compiler_params=pltpu.CompilerParams(has_side_effects=True))(parts)


def _tail_rows(conv_part, small, extra):
    lead = conv_part.shape[:-1]
    rep = jnp.concatenate([small[n].reshape(-1).astype(f32) for n in SMALL] + [extra.reshape(1).astype(f32)])
    flat = jnp.concatenate([conv_part, jnp.broadcast_to(rep, lead + rep.shape),
                            jnp.zeros(lead + (ROWS_TAIL * 1024 - TAIL_ELEMS,), f32)], axis=-1)
    return flat.reshape(lead + (ROWS_TAIL, 1024))


def _pack_rows(w_in_t, w_ps, w_out, w_up_t, w_down, w_pa_t, tail):
    lead = tail.shape[:-2]
    zeros = lambda r: jnp.zeros(lead + (r, 1024), f32)
    return jnp.concatenate([w_in_t, zeros(ROWS_IN - IN_SHARD), w_ps, w_out, w_up_t, w_down,
                            w_pa_t.reshape(lead + (ROWS_PA, 1024)), tail,
                            zeros(PACK_ROWS - OFF_TAIL - ROWS_TAIL)], axis=-2)


def _pack_shard(vals):
    tail = _tail_rows(vals["conv_w"].reshape(-1), vals, jnp.zeros((), f32))
    return _pack_rows(vals["w_in"].T, vals["w_proj_ssd"], vals["w_out"], vals["w_up"].T, vals["w_down"],
                      vals["w_proj_attn"].T, tail)


def _unpack_shard(packed):
    out = {"w_in": packed[0:IN_SHARD].T, "w_proj_ssd": packed[OFF_PS:OFF_OUT], "w_out": packed[OFF_OUT:OFF_UP],
           "w_up": packed[OFF_UP:OFF_DOWN].T, "w_down": packed[OFF_DOWN:OFF_PA],
           "w_proj_attn": packed[OFF_PA:OFF_TAIL].reshape(D_MODEL // N_DEV, ATTN_OUT).T}
    flat = packed[OFF_TAIL:OFF_TAIL + ROWS_TAIL].reshape(-1)
    out["conv_w"] = flat[0:CONV_SHARD].reshape(D_CONV, CONV_DIM // N_DEV)
    off = CONV_SHARD
    for n in SMALL:
        out[n] = flat[off:off + SMALL_SIZES[n]]
        off += SMALL_SIZES[n]
    out["_extra"] = flat[off]
    return out


def _pack_parts(full, small, extra):
    conv = full["conv_w"].reshape(D_CONV, N_DEV, CONV_DIM // N_DEV).transpose(1, 0, 2).reshape(N_DEV, CONV_SHARD)
    blocks = lambda g: g.reshape(N_DEV, g.shape[0] // N_DEV, g.shape[1])
    return _pack_rows(blocks(full["w_in_t"]), blocks(full["w_proj_ssd"]), blocks(full["w_out"]), blocks(full["w_up_t"]),
                      blocks(full["w_down"]), blocks(full["w_proj_attn_t"]), _tail_rows(conv, small, extra))


def _gather_weights(w):
    conv_bits = lax.bitcast_convert_type(w["conv_w"], bf16).reshape(-1)
    conv_rows = jnp.concatenate([conv_bits, jnp.zeros((16 * 1024 - 2 * CONV_SHARD,), bf16)]).reshape(16, 1024)
    big = _pack_shard(w)[0:OFF_TAIL].astype(bf16)
    got = all_gather_blocks(jnp.concatenate([big, conv_rows], axis=0))
    whole = lambda lo, hi: got[:, lo:hi].reshape(N_DEV * (hi - lo), 1024)
    conv = lax.bitcast_convert_type(got[:, OFF_TAIL:OFF_TAIL + 4].reshape(N_DEV, 4096)[:, 0:2 * CONV_SHARD]
                                    .reshape(N_DEV, D_CONV, CONV_DIM // N_DEV, 2), f32)
    return {"w_in_t": whole(0, IN_SHARD), "w_proj_ssd": whole(OFF_PS, OFF_OUT), "w_out": whole(OFF_OUT, OFF_UP),
            "w_up_t": whole(OFF_UP, OFF_DOWN), "w_down": whole(OFF_DOWN, OFF_PA),
            "w_proj_attn_t": got[:, OFF_PA:OFF_TAIL].reshape(D_MODEL, ATTN_OUT),
            "conv_w": conv.transpose(1, 0, 2).reshape(D_CONV, CONV_DIM)}


def _row(v, width=None):
    v = v.reshape(1, -1).astype(f32)
    return v if width is None else jnp.pad(v, ((0, 0), (0, width - v.shape[1])))


def _lanes256(vf, vb):
    z = jnp.zeros((96,), f32)
    return jnp.concatenate([vf.astype(f32), z, vb.astype(f32), z]).reshape(1, 256)


def _local_step(x2, tgt, wf, p):
    t = x2.shape[0]
    o = np.cumsum((0,) + IN_SPLITS)
    wt = wf["w_in_t"]
    wt_z, wt_xbc = wt[o[0]:o[1]], wt[o[1]:o[2]]
    zpad = jnp.zeros((96, D_MODEL), wt.dtype)
    wt_dt = jnp.concatenate([wt[o[2]:o[3]], zpad, wt[o[3]:o[4]], zpad], axis=0)
    wt_qkv, wt_gate = wt[o[4]:o[7]], wt[o[7]:o[8]]

    conv_w8 = jnp.pad(wf["conv_w"].astype(f32), ((0, 8 - D_CONV), (0, 0)))
    conv_b = _row(p["conv_b"])
    dt_bias = _lanes256(p["dt_bias_f"], p["dt_bias_b"])
    a_f, a_b = -jnp.exp(p["a_log_f"].astype(f32)), -jnp.exp(p["a_log_b"].astype(f32))
    a_rows = _lanes256(a_f, a_b)
    dskip_row = jnp.repeat(p["d_skip"].astype(f32), SSD_HEAD_DIM).reshape(1, D_INNER)
    nw_row, bg_row = _row(p["ssd_norm_w"]), _row(p["b_gate"])
    g1, b1, g2, b2 = _row(p["ln1_g"]), _row(p["ln1_b"]), _row(p["ln2_g"]), _row(p["ln2_b"])

    u_z = mm_nt(x2, wt_z, "in_z")
    u_xbc = mm_nt(x2, wt_xbc, "in_xbc")
    u_dt = mm_nt(x2, wt_dt, "in_dt")
    u_qkv = mm_nt(x2, wt_qkv, "in_qkv")
    u_gate = mm_nt(x2, wt_gate, "in_gate")
    xbc_c = conv_fwd(u_xbc, conv_w8, conv_b)
    dt2 = dt_fwd(u_dt, dt_bias)
    y_f, h_f = ssd_fwd(xbc_c, dt2, a_rows, False, "ssd_fwd_f")
    y_b, h_b = ssd_fwd(xbc_c, dt2, a_rows, True, "ssd_fwd_b")
    yn = tail_fwd(y_f, y_b, xbc_c, u_z, dskip_row, nw_row)
    y_ssd = mm_nn(yn, wf["w_proj_ssd"], "proj_ssd")

    def strided(a, dil):
        return a.reshape(t // dil, dil * 256)

    qkv, outs, lses = [], [], []
    for pi, (_, dil) in enumerate(DIL_PATTERNS):
        q, k, v = (strided(u_qkv[:, ATTN_WIDTH * s + 256 * pi: ATTN_WIDTH * s + 256 * (pi + 1)], dil) for s in range(3))
        qkv.append((q, k, v))
        op, lp = attn_fwd(q, k, v, pi, dil, f"attn_fwd_{pi}")
        outs.append(op.reshape(t, 256))
        lses.append(lp.reshape(t, 256))
    ya, lse = attn_combine(outs, lses)
    y_att = mm_nt(ya, wf["w_proj_attn_t"], "proj_attn")
    m = merge_fwd(u_gate, bg_row, y_ssd, y_att)
    mix = mm_nn(m, wf["w_out"], "out_proj")
    h1 = ln1_fwd(x2, mix, g1, b1)
    a_up = mm_nt(h1, wf["w_up_t"], "mlp_up")
    p_act = relu2_fwd(a_up)
    f_dn = mm_nn(p_act, wf["w_down"], "mlp_down")
    dr2, dg2, db2, loss8 = ln2_loss(h1, f_dn, g2, b2, tgt)

    full, small = {}, {}
    dp = mm_nt(dr2, wf["w_down"], "d_mlp_act")
    da = relu2_bwd(dp, a_up)
    full["w_down"] = mm_tn(p_act, dr2, "dw_down")
    full["w_up_t"] = mm_tn(da, h1, "dw_up")
    dh1 = mm_nn(da, wf["w_up_t"], "d_h1", acc_in=dr2, acc_scale=ALPHA)
    dr1, dg1, db1 = ln1_bwd(dh1, x2, mix, g1)
    dm = mm_nt(dr1, wf["w_out"], "d_merge")
    full["w_out"] = mm_tn(m, dr1, "dw_out")
    dys, dya_p, dga, dgb, dba, dbb = merge_bwd(dm, u_gate, bg_row, y_ssd, y_att)
    dyn = mm_nt(dys, wf["w_proj_ssd"], "d_yn")
    full["w_proj_ssd"] = mm_tn(yn, dys, "dw_proj_ssd")
    dya = mm_nn(dya_p, wf["w_proj_attn_t"], "d_ya")
    full["w_proj_attn_t"] = mm_tn(dya_p, ya, "dw_proj_attn")

    dy, dz, dnw, ddx = tail_bwd(dyn, y_f, y_b, xbc_c, u_z, dskip_row, nw_row)
    dxf, dbf, dcf, ddtf, daf = ssd_bwd(xbc_c, dt2, a_rows, dy, h_f, False, "ssd_bwd_f")
    dxb, dbb_, dcb_, ddtb, dab = ssd_bwd(xbc_c, dt2, a_rows, dy, h_b, True, "ssd_bwd_b")
    sx = slice(0, D_INNER)
    sb = slice(D_INNER, D_INNER + 512)
    sc = slice(D_INNER + 512, CONV_DIM)
    dxbc_x, dcw_x, dcb_x = conv_bwd(u_xbc, 0, [dxf, dxb, dy], [None, None, dskip_row], conv_w8[:, sx], conv_b[:, sx], "conv_bwd_x")
    dxbc_b, dcw_b, dcb_b = conv_bwd(u_xbc, 4, [dbf, dbb_], [None, None], conv_w8[:, sb], conv_b[:, sb], "conv_bwd_b")
    dxbc_c, dcw_c, dcb_c = conv_bwd(u_xbc, 5, [dcf, dcb_], [None, None], conv_w8[:, sc], conv_b[:, sc], "conv_bwd_c")
    du_dt, dbias = dt_bwd(ddtf, ddtb, u_dt, dt_bias)

    delta = attn_delta(dya, ya)
    dqs, dks, dvs = [], [], []
    for pi, (_, dil) in enumerate(DIL_PATTERNS):
        q, k, v = qkv[pi]
        sd, sl_, sdel = strided(dya, dil), strided(lse, dil), strided(delta, dil)
        dqs.append(attn_dq(q, k, v, sd, sl_, sdel, pi, dil, f"attn_dq_{pi}").reshape(t, 256))
        dk, dv = attn_dkv(q, k, v, sd, sl_, sdel, pi, dil, f"attn_dkv_{pi}")
        dks.append(dk.reshape(t, 256))
        dvs.append(dv.reshape(t, 256))
    du_qkv = jnp.concatenate(dqs + dks + dvs, axis=1)
    du_xbc = jnp.concatenate([dxbc_x, dxbc_b, dxbc_c], axis=1)
    du_gate = jnp.concatenate([dga, dgb], axis=1)

    dx = mm_nn(dz, wt_z, "dx_z", acc_in=dr1, acc_scale=ALPHA)
    dx = mm_nn(du_xbc, wt_xbc, "dx_xbc", acc_in=dx)
    dx = mm_nn(du_dt, wt_dt, "dx_dt", acc_in=dx)
    dx = mm_nn(du_qkv, wt_qkv, "dx_qkv", acc_in=dx)
    dx = mm_nn(du_gate, wt_gate, "dx_gate", acc_in=dx)
    dw_dt = mm_tn(du_dt, x2, "dw_in_dt")
    full["w_in_t"] = jnp.concatenate(
        [mm_tn(dz, x2, "dw_in_z"), mm_tn(du_xbc, x2, "dw_in_xbc"), dw_dt[0:32], dw_dt[128:160],
         mm_tn(du_qkv, x2, "dw_in_qkv"), mm_tn(du_gate, x2, "dw_in_gate")], axis=0)
    full["conv_w"] = jnp.concatenate([dcw_x[:D_CONV], dcw_b[:D_CONV], dcw_c[:D_CONV]], axis=1)

    small["b_gate"] = jnp.concatenate([dba, dbb], axis=1)
    small["conv_b"] = jnp.concatenate([dcb_x, dcb_b, dcb_c], axis=1)
    small["dt_bias_f"], small["dt_bias_b"] = dbias[0, 0:32], dbias[0, 128:160]
    small["a_log_f"] = jnp.sum(daf, axis=(0, 1))[0:32] * a_f
    small["a_log_b"] = jnp.sum(dab, axis=(0, 1))[0:32] * a_b
    small["d_skip"] = jnp.sum(ddx.reshape(SSD_HEADS, SSD_HEAD_DIM), axis=1)
    small["ssd_norm_w"] = dnw
    small["ln1_g"], small["ln1_b"], small["ln2_g"], small["ln2_b"] = dg1, db1, dg2, db2
    return loss8[0, 0], dx, full, small


def kernel(x, w_in, b_gate, conv_w, conv_b, dt_bias_f, dt_bias_b, a_log_f, a_log_b, d_skip, ssd_norm_w, w_proj_ssd, w_proj_attn, w_out, ln1_g, ln1_b, w_up, w_down, ln2_g, ln2_b, loss_target, m_w_in, m_b_gate, m_conv_w, m_conv_b, m_dt_bias_f, m_dt_bias_b, m_a_log_f, m_a_log_b, m_d_skip, m_ssd_norm_w, m_w_proj_ssd, m_w_proj_attn, m_w_out, m_ln1_g, m_ln1_b, m_w_up, m_w_down, m_ln2_g, m_ln2_b, v_w_in, v_b_gate, v_conv_w, v_conv_b, v_dt_bias_f, v_dt_bias_b, v_a_log_f, v_a_log_b, v_d_skip, v_ssd_norm_w, v_w_proj_ssd, v_w_proj_attn, v_w_out, v_ln1_g, v_ln1_b, v_w_up, v_w_down, v_ln2_g, v_ln2_b):
    given = dict(locals())
    w = {n: given[n] for n in WEIGHTS}
    mom = {n: given["m_" + n] for n in WEIGHTS}
    var = {n: given["v_" + n] for n in WEIGHTS}
    t = x.shape[1]
    wf = _gather_weights(w)
    loss, dx, full, small = _local_step(x.reshape(t, D_MODEL), loss_target.reshape(t, D_MODEL), wf, w)
    packed = _pack_parts(full, small, loss)
    core = lax.axis_index("c").astype(jnp.int32).reshape(1)
    parts = chip_exchange(pair_sum(packed, pair_exchange(packed), core))
    g, delta, new_m, new_v = (_unpack_shard(a) for a in adamw(parts, _pack_shard(w), _pack_shard(mom), _pack_shard(var)))
    outs = [g["_extra"], dx.reshape(x.shape)]
    for d in (g, delta, new_m, new_v):
        outs += [d[n].reshape(w[n].shape) for n in WEIGHTS]
    return tuple(outs)
```

```python
import functools
import math

import jax
import jax.numpy as jnp
import numpy as np
from jax import lax
from jax.experimental import pallas as pl
from jax.experimental.pallas import tpu as pltpu

f32 = jnp.float32
bf16 = jnp.bfloat16
MXU_DTYPE = jnp.bfloat16

N_DEV = 8
D_MODEL = 1024
D_INNER = 2048
SSD_HEADS = 32
SSD_HEAD_DIM = 64
SSD_GROUPS = 4
D_STATE = 128
D_CONV = 5
CHUNK = 128
CONV_DIM = D_INNER + 2 * SSD_GROUPS * D_STATE
NORM_EPS = 1e-5
ATTN_HEAD_DIM = 64
DIL_PATTERNS = ((128, 1), (512, 4), (2048, 16))
HEADS_PER_PATTERN = 4
ATTN_HEADS = 12
ATTN_WIDTH = 768
ATTN_OUT = 256
D_FF = 4096
ALPHA = 2.0 ** 0.25
IN_SPLITS = (D_INNER, CONV_DIM, SSD_HEADS, SSD_HEADS, ATTN_WIDTH, ATTN_WIDTH, ATTN_WIDTH, 2 * D_MODEL)
IN_COLS = sum(IN_SPLITS)
ADAM_LR, ADAM_B1, ADAM_B2, ADAM_EPS, ADAM_WD, ADAM_STEP = 0.001, 0.9, 0.999, 1e-08, 0.01, 10
NEG_BIG = -1e30
VMEM_LIMIT = 56 * 1024 * 1024
MESH = pl.DeviceIdType.MESH

SMALL = ("b_gate", "conv_b", "dt_bias_f", "dt_bias_b", "a_log_f", "a_log_b", "d_skip", "ssd_norm_w",
         "ln1_g", "ln1_b", "ln2_g", "ln2_b")
WEIGHTS = ("w_in", "b_gate", "conv_w", "conv_b", "dt_bias_f", "dt_bias_b", "a_log_f", "a_log_b", "d_skip",
           "ssd_norm_w", "w_proj_ssd", "w_proj_attn", "w_out", "ln1_g", "ln1_b", "w_up", "w_down", "ln2_g", "ln2_b")
SMALL_SIZES = {"b_gate": 2 * D_MODEL, "conv_b": CONV_DIM, "dt_bias_f": 32, "dt_bias_b": 32, "a_log_f": 32, "a_log_b": 32,
               "d_skip": 32, "ssd_norm_w": D_INNER, "ln1_g": D_MODEL, "ln1_b": D_MODEL, "ln2_g": D_MODEL, "ln2_b": D_MODEL}
IN_SHARD = IN_COLS // N_DEV
ROWS_IN = 1200
ROWS_PS, ROWS_OUT, ROWS_UP, ROWS_DOWN, ROWS_PA = D_INNER // N_DEV, D_MODEL // N_DEV, D_FF // N_DEV, D_FF // N_DEV, 32
OFF_PS = ROWS_IN
OFF_OUT = OFF_PS + ROWS_PS
OFF_UP = OFF_OUT + ROWS_OUT
OFF_DOWN = OFF_UP + ROWS_UP
OFF_PA = OFF_DOWN + ROWS_DOWN
OFF_TAIL = OFF_PA + ROWS_PA
CONV_SHARD = D_CONV * CONV_DIM // N_DEV
TAIL_ELEMS = CONV_SHARD + sum(SMALL_SIZES.values()) + 1
ROWS_TAIL = 16
PACK_TILE = 128
PACK_ROWS = -(-(OFF_TAIL + ROWS_TAIL) // PACK_TILE) * PACK_TILE


def _cparams(sem=None, **kw):
    return pltpu.CompilerParams(dimension_semantics=sem, vmem_limit_bytes=VMEM_LIMIT, **kw)


def _mx(v):
    return v.astype(MXU_DTYPE)


def _dot(a, b):
    return jnp.dot(_mx(a), _mx(b), preferred_element_type=f32)


def _dot_nt(a, b):
    return lax.dot_general(_mx(a), _mx(b), (((1,), (1,)), ((), ())), preferred_element_type=f32)


def _dot_tn(a, b):
    return lax.dot_general(_mx(a), _mx(b), (((0,), (0,)), ((), ())), preferred_element_type=f32)


def _dot_exact(a, b):
    return jnp.dot(a, b, precision=lax.Precision.HIGHEST, preferred_element_type=f32)


def _sigmoid(v):
    return 1.0 / (1.0 + jnp.exp(-v))


def _pick(n, prefs):
    for p in prefs:
        if n % p == 0:
            return p
    return n


MM_TILE = 1024


def mm_nn(a, b, name, out_dtype=f32, acc_in=None, acc_scale=1.0):
    m, k = a.shape
    n = b.shape[1]
    tm = MM_TILE
    tn = _pick(n, (MM_TILE, 512, 256, 128))
    tk = _pick(k, (2048, 1536, 1152, 1024, 768, 512, 256, 128))
    nk = k // tk

    def body(*refs):
        a_ref, b_ref = refs[0:2]
        c_ref = refs[2] if acc_in is not None else None
        o_ref = refs[3] if acc_in is not None else refs[2]

        def finish(r):
            if acc_in is not None:
                r = r + acc_scale * c_ref[...]
            o_ref[...] = r.astype(o_ref.dtype)

        if nk == 1:
            finish(_dot(a_ref[...], b_ref[...]))
            return
        acc_ref = refs[-1]
        kk = pl.program_id(2)

        @pl.when(kk == 0)
        def _():
            acc_ref[...] = jnp.zeros_like(acc_ref)

        acc_ref[...] += _dot(a_ref[...], b_ref[...])

        @pl.when(kk == nk - 1)
        def _():
            finish(acc_ref[...])

    in_specs = [pl.BlockSpec((tm, tk), lambda i, j, kk: (i, kk)), pl.BlockSpec((tk, tn), lambda i, j, kk: (kk, j))]
    args = [a, b]
    if acc_in is not None:
        in_specs.append(pl.BlockSpec((tm, tn), lambda i, j, kk: (i, j)))
        args.append(acc_in)
    return pl.pallas_call(
        body, name=name, grid=(m // tm, n // tn, nk), in_specs=in_specs,
        out_specs=pl.BlockSpec((tm, tn), lambda i, j, kk: (i, j)),
        out_shape=jax.ShapeDtypeStruct((m, n), out_dtype),
        scratch_shapes=[pltpu.VMEM((tm, tn), f32)] if nk > 1 else [],
        compiler_params=_cparams(("parallel", "parallel", "arbitrary")))(*args)


def mm_nt(a, b, name, out_dtype=f32, relu2=None, relu2_of=None):
    m, k = a.shape
    n = b.shape[0]
    tm = MM_TILE
    tn = _pick(n, (MM_TILE, 768, 512, 256, 128))

    def body(*refs):
        r = _dot_nt(refs[0][...], refs[1][...])
        if relu2:
            refs[2][...] = r
            pos = jnp.maximum(r, 0.0)
            refs[3][...] = (pos * pos).astype(refs[3].dtype)
        elif relu2_of is not None:
            refs[3][...] = (r * (2.0 * jnp.maximum(refs[2][...], 0.0))).astype(refs[3].dtype)
        else:
            refs[2][...] = r.astype(refs[2].dtype)

    blk = pl.BlockSpec((tm, tn), lambda i, j: (i, j))
    in_specs = [pl.BlockSpec((tm, k), lambda i, j: (i, 0)), pl.BlockSpec((tn, k), lambda i, j: (j, 0))]
    args = [a, b]
    if relu2_of is not None:
        in_specs.append(blk)
        args.append(relu2_of)
    if relu2:
        out_specs, out_shape = [blk, blk], [jax.ShapeDtypeStruct((m, n), f32), jax.ShapeDtypeStruct((m, n), bf16)]
    else:
        out_specs, out_shape = blk, jax.ShapeDtypeStruct((m, n), out_dtype)
    return pl.pallas_call(body, name=name, grid=(m // tm, n // tn), in_specs=in_specs, out_specs=out_specs,
                          out_shape=out_shape, compiler_params=_cparams(("parallel", "parallel")))(*args)


def mm_tn(a, b, name, tk=1024):
    t, m = a.shape
    n = b.shape[1]
    tm = _pick(m, (MM_TILE, 768, 512, 256, 128))
    tn = _pick(n, (MM_TILE, 512, 256, 128))
    nk = t // tk

    def body(a_ref, b_ref, o_ref):
        kk = pl.program_id(2)

        @pl.when(kk == 0)
        def _():
            o_ref[...] = jnp.zeros_like(o_ref)

        o_ref[...] += _dot_tn(a_ref[...], b_ref[...])

    return pl.pallas_call(
        body, name=name, grid=(m // tm, n // tn, nk),
        in_specs=[pl.BlockSpec((tk, tm), lambda i, j, kk: (kk, i)), pl.BlockSpec((tk, tn), lambda i, j, kk: (kk, j))],
        out_specs=pl.BlockSpec((tm, tn), lambda i, j, kk: (i, j)),
        out_shape=jax.ShapeDtypeStruct((m, n), f32),
        compiler_params=_cparams(("parallel", "parallel", "arbitrary")))(a, b)


def _halo_specs(tb, cb, nt, off=0):
    r = tb // 8
    return [pl.BlockSpec((8, cb), lambda j, i: (jnp.maximum(i * r - 1, 0), j + off)),
            pl.BlockSpec((tb, cb), lambda j, i: (i, j + off)),
            pl.BlockSpec((8, cb), lambda j, i: (jnp.minimum((i + 1) * r, nt * r - 1), j + off))]


def _with_halo(prev_ref, own_ref, next_ref, i, nt):
    prev = jnp.where(i > 0, prev_ref[...].astype(f32), 0.0)
    nxt = jnp.where(i < nt - 1, next_ref[...].astype(f32), 0.0)
    return jnp.concatenate([prev, own_ref[...].astype(f32), nxt], axis=0)


def _shifted(xcat, s, tb):
    n = xcat.shape[0]
    return pltpu.roll(xcat, (-s) % n, 0)[8:8 + tb]


def conv_fwd(xbc, w8, b_row, tb=512, cb=512):
    t, c = xbc.shape
    nt = t // tb

    def body(prev_ref, own_ref, next_ref, w_ref, b_ref, o_ref):
        i = pl.program_id(1)
        xcat = _with_halo(prev_ref, own_ref, next_ref, i, nt)
        w = w_ref[...]
        pre = b_ref[...] + w[0:1] * _shifted(xcat, -2, tb)
        for k in range(1, D_CONV):
            pre = pre + w[k:k + 1] * _shifted(xcat, k - 2, tb)
        o_ref[...] = pre * _sigmoid(pre)

    return pl.pallas_call(
        body, name="conv_fwd", grid=(c // cb, nt),
        in_specs=_halo_specs(tb, cb, nt) + [pl.BlockSpec((8, cb), lambda j, i: (0, j)), pl.BlockSpec((1, cb), lambda j, i: (0, j))],
        out_specs=pl.BlockSpec((tb, cb), lambda j, i: (i, j)), out_shape=jax.ShapeDtypeStruct((t, c), f32),
        compiler_params=_cparams(("parallel", "parallel")))(xbc, xbc, xbc, w8, b_row)


def conv_bwd(xbc, xoff, grads, scales, w8, b_row, name, tb=512, cb=512):
    t, c = grads[0].shape
    nt = t // tb
    ng = len(grads)
    has_scale = [s is not None for s in scales]

    def body(*refs):
        i = pl.program_id(1)
        xr = refs[0:3]
        gr = [refs[3 + 3 * q: 6 + 3 * q] for q in range(ng)]
        pos = 3 + 3 * ng
        sr = []
        for q in range(ng):
            if has_scale[q]:
                sr.append(refs[pos])
                pos += 1
            else:
                sr.append(None)
        w_ref, b_ref, dx_ref, dw_ref, db_ref = refs[pos:pos + 5]
        xcat = _with_halo(*xr, i, nt)
        gcat = None
        for q in range(ng):
            gq = _with_halo(*gr[q], i, nt)
            if sr[q] is not None:
                gq = gq * sr[q][...]
            gcat = gq if gcat is None else gcat + gq
        w = w_ref[...]
        n = tb + 16
        pre = b_ref[...] + w[0:1] * pltpu.roll(xcat, 2, 0)
        for k in range(1, D_CONV):
            pre = pre + w[k:k + 1] * pltpu.roll(xcat, (2 - k) % n, 0)
        sg = _sigmoid(pre)
        dpre = gcat * sg * (1.0 + pre * (1.0 - sg))
        dx = w[0:1] * _shifted(dpre, 2, tb)
        for k in range(1, D_CONV):
            dx = dx + w[k:k + 1] * _shifted(dpre, 2 - k, tb)
        dx_ref[...] = dx.astype(dx_ref.dtype)
        dp_own = dpre[8:8 + tb]
        rows = [jnp.sum(dp_own * _shifted(xcat, k - 2, tb), axis=0, keepdims=True) for k in range(D_CONV)]
        dw = jnp.concatenate(rows + [jnp.zeros((8 - D_CONV, cb), f32)], axis=0)
        db = jnp.sum(dp_own, axis=0, keepdims=True)

        @pl.when(i == 0)
        def _():
            dw_ref[...] = jnp.zeros_like(dw_ref)
            db_ref[...] = jnp.zeros_like(db_ref)

        dw_ref[...] += dw
        db_ref[...] += db

    in_specs = _halo_specs(tb, cb, nt, xoff)
    args = [xbc] * 3
    for g in grads:
        in_specs += _halo_specs(tb, cb, nt)
        args += [g] * 3
    for s in scales:
        if s is not None:
            in_specs.append(pl.BlockSpec((1, cb), lambda j, i: (0, j)))
            args.append(s)
    in_specs += [pl.BlockSpec((8, cb), lambda j, i: (0, j)), pl.BlockSpec((1, cb), lambda j, i: (0, j))]
    args += [w8, b_row]
    return pl.pallas_call(
        body, name=name, grid=(c // cb, nt), in_specs=in_specs,
        out_specs=[pl.BlockSpec((tb, cb), lambda j, i: (i, j)), pl.BlockSpec((8, cb), lambda j, i: (0, j)),
                   pl.BlockSpec((1, cb), lambda j, i: (0, j))],
        out_shape=[jax.ShapeDtypeStruct((t, c), bf16), jax.ShapeDtypeStruct((8, c), f32), jax.ShapeDtypeStruct((1, c), f32)],
        compiler_params=_cparams(("parallel", "arbitrary")))(*args)


def dt_fwd(u_dt, bias_row, tb=1024):
    t = u_dt.shape[0]

    def body(u_ref, b_ref, o_ref):
        v = u_ref[...] + b_ref[...]
        sp = jnp.maximum(v, 0.0) + jnp.log(1.0 + jnp.exp(-jnp.abs(v)))
        lane = lax.broadcasted_iota(jnp.int32, v.shape, 1)
        o_ref[...] = jnp.where((lane & 127) < SSD_HEADS, sp, 0.0)

    return pl.pallas_call(
        body, name="dt_fwd", grid=(t // tb,),
        in_specs=[pl.BlockSpec((tb, 256), lambda i: (i, 0)), pl.BlockSpec((1, 256), lambda i: (0, 0))],
        out_specs=pl.BlockSpec((tb, 256), lambda i: (i, 0)), out_shape=jax.ShapeDtypeStruct((t, 256), f32),
        compiler_params=_cparams(("parallel",)))(u_dt, bias_row)


def dt_bwd(ddt_f, ddt_b, u_dt, bias_row, tb=1024):
    t = u_dt.shape[0]

    def body(gf_ref, gb_ref, u_ref, b_ref, du_ref, db_ref):
        g = jnp.concatenate([jnp.sum(gf_ref[...], axis=0), jnp.sum(gb_ref[...], axis=0)], axis=1)
        du = g * _sigmoid(u_ref[...] + b_ref[...])
        du_ref[...] = du.astype(du_ref.dtype)

        @pl.when(pl.program_id(0) == 0)
        def _():
            db_ref[...] = jnp.zeros_like(db_ref)

        db_ref[...] += jnp.sum(du, axis=0, keepdims=True)

    return pl.pallas_call(
        body, name="dt_bwd", grid=(t // tb,),
        in_specs=[pl.BlockSpec((4, tb, 128), lambda i: (0, i, 0)), pl.BlockSpec((4, tb, 128), lambda i: (0, i, 0)),
                  pl.BlockSpec((tb, 256), lambda i: (i, 0)), pl.BlockSpec((1, 256), lambda i: (0, 0))],
        out_specs=[pl.BlockSpec((tb, 256), lambda i: (i, 0)), pl.BlockSpec((1, 256), lambda i: (0, 0))],
        out_shape=[jax.ShapeDtypeStruct((t, 256), bf16), jax.ShapeDtypeStruct((1, 256), f32)],
        compiler_params=_cparams(("arbitrary",)))(ddt_f, ddt_b, u_dt, bias_row)


def _ssd_common(dt_blk, a_row, reverse):
    row = lax.broadcasted_iota(jnp.int32, (CHUNK, CHUNK), 0)
    col = lax.broadcasted_iota(jnp.int32, (CHUNK, CHUNK), 1)
    mask = (row <= col) if reverse else (row >= col)
    tri = mask.astype(f32)
    a = dt_blk * a_row
    acs = _dot_exact(tri, a)
    atot = jnp.sum(a, axis=0, keepdims=True)
    return mask, tri, a, acs, atot, col


def _lane_col(mat, lane_idx, h):
    return jnp.sum(jnp.where(lane_idx == h, mat, 0.0), axis=1, keepdims=True)


def ssd_fwd(xbc_c, dt2, a_rows, reverse, name):
    t = xbc_c.shape[0]
    nc = t // CHUNK
    d_off = 1 if reverse else 0

    def cidx(c):
        return nc - 1 - c if reverse else c

    def body(x_ref, b_ref, c_ref, dt_ref, a_ref, y_ref, hp_ref, h_scr, acst_scr):
        g = pl.program_id(0)
        c = pl.program_id(1)

        @pl.when(c == 0)
        def _():
            h_scr[...] = jnp.zeros_like(h_scr)

        dt_blk = dt_ref[...]
        mask, tri, a, acs, atot, lane = _ssd_common(dt_blk, a_ref[...], reverse)
        acst_scr[...] = acs.T
        bm = b_ref[...]
        cm = c_ref[...]
        cb = _dot_nt(cm, bm)
        half = lane >= SSD_HEAD_DIM
        sub_half = lax.broadcasted_iota(jnp.int32, (CHUNK, 1), 0) >= SSD_HEAD_DIM
        for j in range(4):
            x = x_ref[:, 128 * j:128 * (j + 1)]
            cols, dts, tots = [], [], []
            y = None
            for e in range(2):
                h = 8 * g + 2 * j + e
                col_h = _lane_col(acs, lane, h)
                row_h = acst_scr[pl.ds(h, 1), :]
                dt_h = _lane_col(dt_blk, lane, h)
                lmat = jnp.where(mask, jnp.exp(jnp.where(mask, col_h - row_h, 0.0)), 0.0)
                xdt_e = jnp.where(half == (e == 1), x * dt_h, 0.0)
                ye = _dot(cb * lmat, xdt_e)
                y = ye if y is None else y + ye
                cols.append(col_h)
                dts.append(dt_h)
                tots.append(jnp.sum(jnp.where(lane[0:1] == h, atot, 0.0), axis=1, keepdims=True))
            hp = h_scr[j]
            hp_ref[0, j] = hp
            ecol = jnp.where(half, jnp.exp(cols[1]), jnp.exp(cols[0]))
            y = y + _dot_nt(cm, hp) * ecol
            y_ref[:, 128 * j:128 * (j + 1)] = y
            dec = jnp.where(half, jnp.exp(tots[1] - cols[1]), jnp.exp(tots[0] - cols[0]))
            xdt = x * jnp.where(half, dts[1], dts[0])
            s_new = _dot_tn(xdt * dec, bm)
            cd = jnp.where(sub_half, jnp.exp(tots[1]), jnp.exp(tots[0]))
            h_scr[j] = cd * hp + s_new

    return pl.pallas_call(
        body, name=name, grid=(SSD_GROUPS, nc),
        in_specs=[pl.BlockSpec((CHUNK, 512), lambda g, c: (cidx(c), g)),
                  pl.BlockSpec((CHUNK, 128), lambda g, c: (cidx(c), 16 + g)),
                  pl.BlockSpec((CHUNK, 128), lambda g, c: (cidx(c), 20 + g)),
                  pl.BlockSpec((CHUNK, 128), lambda g, c: (cidx(c), d_off)),
                  pl.BlockSpec((1, 128), lambda g, c: (0, d_off))],
        out_specs=[pl.BlockSpec((CHUNK, 512), lambda g, c: (cidx(c), g)),
                   pl.BlockSpec((1, 4, 128, 128), lambda g, c: (cidx(c), g, 0, 0))],
        out_shape=[jax.ShapeDtypeStruct((t, D_INNER), f32), jax.ShapeDtypeStruct((nc, 16, 128, 128), f32)],
        scratch_shapes=[pltpu.VMEM((4, 128, 128), f32), pltpu.VMEM((CHUNK, CHUNK), f32)],
        compiler_params=_cparams(("parallel", "arbitrary")))(xbc_c, xbc_c, xbc_c, dt2, a_rows)


def ssd_bwd(xbc_c, dt2, a_rows, dy, hprev, reverse, name):
    t = xbc_c.shape[0]
    nc = t // CHUNK
    d_off = 1 if reverse else 0

    def cidx(c):
        return c if reverse else nc - 1 - c

    def body(x_ref, b_ref, c_ref, dt_ref, a_ref, dy_ref, hp_ref, dx_ref, db_ref, dc_ref, ddt_ref, da_ref,
             dh_scr, acst_scr):
        g = pl.program_id(0)
        c = pl.program_id(1)

        @pl.when(c == 0)
        def _():
            dh_scr[...] = jnp.zeros_like(dh_scr)
            da_ref[...] = jnp.zeros_like(da_ref)

        dt_blk = dt_ref[...]
        a_row = a_ref[...]
        mask, tri, a, acs, atot, lane = _ssd_common(dt_blk, a_row, reverse)
        acst_scr[...] = acs.T
        sub = lax.broadcasted_iota(jnp.int32, (CHUNK, CHUNK), 0)
        bm = b_ref[...]
        cm = c_ref[...]
        cb = _dot_nt(cm, bm)
        half = lane >= SSD_HEAD_DIM
        sub_half = sub[:, 0:1] >= SSD_HEAD_DIM
        dcb = jnp.zeros((CHUNK, CHUNK), f32)
        dacs = jnp.zeros((CHUNK, CHUNK), f32)
        dacs_t = jnp.zeros((CHUNK, CHUNK), f32)
        dtot = jnp.zeros((1, CHUNK), f32)
        ddt_x = jnp.zeros((CHUNK, CHUNK), f32)
        dbm = jnp.zeros((CHUNK, D_STATE), f32)
        dcm = jnp.zeros((CHUNK, D_STATE), f32)
        for j in range(4):
            x = x_ref[:, 128 * j:128 * (j + 1)]
            dyp = dy_ref[:, 128 * j:128 * (j + 1)]
            hp = hp_ref[0, j]
            dhn = dh_scr[j]
            cols, dts, tots, hs = [], [], [], []
            dxdt = None
            for e in range(2):
                h = 8 * g + 2 * j + e
                sel = half == (e == 1)
                col_h = _lane_col(acs, lane, h)
                row_h = acst_scr[pl.ds(h, 1), :]
                dt_h = _lane_col(dt_blk, lane, h)
                lmat = jnp.where(mask, jnp.exp(jnp.where(mask, col_h - row_h, 0.0)), 0.0)
                xdt_e = jnp.where(sel, x * dt_h, 0.0)
                dy_e = jnp.where(sel, dyp, 0.0)
                ml = _dot_nt(dy_e, xdt_e) * lmat
                dcb = dcb + ml
                w = ml * cb
                dacs = dacs + jnp.where(lane == h, jnp.sum(w, axis=1, keepdims=True), 0.0)
                dacs_t = dacs_t - jnp.where(sub == h, jnp.sum(w, axis=0, keepdims=True), 0.0)
                de = _dot_tn(cb * lmat, dy_e)
                dxdt = de if dxdt is None else dxdt + de
                cols.append(col_h)
                dts.append(dt_h)
                tots.append(jnp.sum(jnp.where(lane[0:1] == h, atot, 0.0), axis=1, keepdims=True))
                hs.append(h)
            ecol = jnp.where(half, jnp.exp(cols[1]), jnp.exp(cols[0]))
            dec = jnp.where(half, jnp.exp(tots[1] - cols[1]), jnp.exp(tots[0] - cols[0]))
            cd = jnp.where(sub_half, jnp.exp(tots[1]), jnp.exp(tots[0]))
            dtp = jnp.where(half, dts[1], dts[0])
            xdt = x * dtp
            yoff = _dot_nt(cm, hp) * ecol
            dye = dyp * ecol
            dcm = dcm + _dot(dye, hp)
            dhp = _dot_tn(dye, cm)
            gmat = _dot_nt(bm, dhn)
            dxdt = dxdt + dec * gmat
            dbm = dbm + _dot(xdt * dec, dhn)
            r_off = dyp * yoff
            r_dec = xdt * gmat * dec
            r_x = dxdt * x
            hh = dhn * hp
            for e in range(2):
                sel = half == (e == 1)
                h = hs[e]
                s_off = jnp.sum(jnp.where(sel, r_off, 0.0), axis=1, keepdims=True)
                s_dec = jnp.sum(jnp.where(sel, r_dec, 0.0), axis=1, keepdims=True)
                dacs = dacs + jnp.where(lane == h, s_off - s_dec, 0.0)
                dcd = jnp.sum(jnp.sum(jnp.where(sub_half == (e == 1), hh, 0.0), axis=1, keepdims=True), axis=0, keepdims=True)
                tot_e = jnp.sum(s_dec, axis=0, keepdims=True) + jnp.exp(tots[e]) * dcd
                dtot = dtot + jnp.where(lane[0:1] == h, tot_e, 0.0)
                ddt_x = ddt_x + jnp.where(lane == h, jnp.sum(jnp.where(sel, r_x, 0.0), axis=1, keepdims=True), 0.0)
            dx_ref[:, 128 * j:128 * (j + 1)] = dxdt * dtp
            dh_scr[j] = cd * dhn + dhp
        dcm = dcm + _dot(dcb, bm)
        dbm = dbm + _dot_tn(dcb, cm)
        db_ref[...] = dbm
        dc_ref[...] = dcm
        dacs = dacs + dacs_t.T
        da = _dot_exact(tri.T, dacs) + dtot
        ddt_ref[0] = da * a_row + ddt_x
        da_ref[0] += jnp.sum(da * dt_blk, axis=0, keepdims=True)

    return pl.pallas_call(
        body, name=name, grid=(SSD_GROUPS, nc),
        in_specs=[pl.BlockSpec((CHUNK, 512), lambda g, c: (cidx(c), g)),
                  pl.BlockSpec((CHUNK, 128), lambda g, c: (cidx(c), 16 + g)),
                  pl.BlockSpec((CHUNK, 128), lambda g, c: (cidx(c), 20 + g)),
                  pl.BlockSpec((CHUNK, 128), lambda g, c: (cidx(c), d_off)),
                  pl.BlockSpec((1, 128), lambda g, c: (0, d_off)),
                  pl.BlockSpec((CHUNK, 512), lambda g, c: (cidx(c), g)),
                  pl.BlockSpec((1, 4, 128, 128), lambda g, c: (cidx(c), g, 0, 0))],
        out_specs=[pl.BlockSpec((CHUNK, 512), lambda g, c: (cidx(c), g)),
                   pl.BlockSpec((CHUNK, 128), lambda g, c: (cidx(c), g)),
                   pl.BlockSpec((CHUNK, 128), lambda g, c: (cidx(c), g)),
                   pl.BlockSpec((1, CHUNK, 128), lambda g, c: (g, cidx(c), 0)),
                   pl.BlockSpec((1, 1, 128), lambda g, c: (g, 0, 0))],
        out_shape=[jax.ShapeDtypeStruct((t, D_INNER), f32), jax.ShapeDtypeStruct((t, 512), f32),
                   jax.ShapeDtypeStruct((t, 512), f32), jax.ShapeDtypeStruct((4, t, 128), f32),
                   jax.ShapeDtypeStruct((4, 1, 128), f32)],
        scratch_shapes=[pltpu.VMEM((4, 128, 128), f32), pltpu.VMEM((CHUNK, CHUNK), f32)],
        compiler_params=_cparams(("parallel", "arbitrary")))(xbc_c, xbc_c, xbc_c, dt2, a_rows, dy, hprev)


def tail_fwd(y_f, y_b, xbc_c, z, dskip_row, nw_row, tb=512):
    t = y_f.shape[0]

    def body(yf_ref, yb_ref, x_ref, z_ref, d_ref, w_ref, o_ref):
        zz = z_ref[...]
        y = (yf_ref[...] + yb_ref[...] + d_ref[...] * x_ref[...]) * (zz * _sigmoid(zz))
        rstd = lax.rsqrt(jnp.mean(y * y, axis=1, keepdims=True) + NORM_EPS)
        o_ref[...] = (y * rstd * w_ref[...]).astype(o_ref.dtype)

    blk = pl.BlockSpec((tb, 512), lambda i, g: (i, g))
    row = pl.BlockSpec((1, 512), lambda i, g: (0, g))
    return pl.pallas_call(
        body, name="tail_fwd", grid=(t // tb, SSD_GROUPS), in_specs=[blk, blk, blk, blk, row, row], out_specs=blk,
        out_shape=jax.ShapeDtypeStruct((t, D_INNER), bf16),
        compiler_params=_cparams(("parallel", "parallel")))(y_f, y_b, xbc_c, z, dskip_row, nw_row)


def tail_bwd(dyn, y_f, y_b, xbc_c, z, dskip_row, nw_row, tb=512):
    t = y_f.shape[0]

    def body(g_ref, yf_ref, yb_ref, x_ref, z_ref, d_ref, w_ref, dy_ref, dz_ref, dw_ref, dd_ref):
        zz = z_ref[...]
        sg = _sigmoid(zz)
        sl = zz * sg
        x = x_ref[...]
        y = yf_ref[...] + yb_ref[...] + d_ref[...] * x
        yz = y * sl
        rstd = lax.rsqrt(jnp.mean(yz * yz, axis=1, keepdims=True) + NORM_EPS)
        yhat = yz * rstd
        g = g_ref[...]
        dyhat = g * w_ref[...]
        dyz = rstd * (dyhat - yhat * jnp.mean(dyhat * yhat, axis=1, keepdims=True))
        dy = dyz * sl
        dy_ref[...] = dy
        dz_ref[...] = (dyz * y * sg * (1.0 + zz * (1.0 - sg))).astype(dz_ref.dtype)

        @pl.when(pl.program_id(1) == 0)
        def _():
            dw_ref[...] = jnp.zeros_like(dw_ref)
            dd_ref[...] = jnp.zeros_like(dd_ref)

        dw_ref[...] += jnp.sum(g * yhat, axis=0, keepdims=True)
        dd_ref[...] += jnp.sum(dy * x, axis=0, keepdims=True)

    blk = pl.BlockSpec((tb, 512), lambda g, i: (i, g))
    row = pl.BlockSpec((1, 512), lambda g, i: (0, g))
    return pl.pallas_call(
        body, name="tail_bwd", grid=(SSD_GROUPS, t // tb), in_specs=[blk, blk, blk, blk, blk, row, row],
        out_specs=[blk, blk, row, row],
        out_shape=[jax.ShapeDtypeStruct((t, D_INNER), f32), jax.ShapeDtypeStruct((t, D_INNER), bf16),
                   jax.ShapeDtypeStruct((1, D_INNER), f32), jax.ShapeDtypeStruct((1, D_INNER), f32)],
        compiler_params=_cparams(("parallel", "arbitrary")))(dyn, y_f, y_b, xbc_c, z, dskip_row, nw_row)


def _slopes(p):
    return [2.0 ** (-8.0 * (HEADS_PER_PATTERN * p + j + 1) / ATTN_HEADS) for j in range(HEADS_PER_PATTERN)]


def _win_specs(nq, col_of):
    return [pl.BlockSpec((64, 256), lambda r, i: (jnp.maximum(2 * i - 1, 0), col_of(r))),
            pl.BlockSpec((128, 256), lambda r, i: (i, col_of(r))),
            pl.BlockSpec((64, 256), lambda r, i: (jnp.minimum(2 * i + 2, 2 * nq - 1), col_of(r)))]


def _lane_head(shape):
    return lax.broadcasted_iota(jnp.int32, shape, 1) >> 6


def _stack_heads(m):
    lane_head = _lane_head(m.shape)
    return jnp.concatenate([jnp.where(lane_head == j, m, 0.0) for j in range(HEADS_PER_PATTERN)], axis=0)


def _unstack_heads(m4, n):
    lane_head = _lane_head((n, 256))
    out = jnp.where(lane_head == 0, m4[0:n], 0.0)
    for j in range(1, HEADS_PER_PATTERN):
        out = out + jnp.where(lane_head == j, m4[j * n:(j + 1) * n], 0.0)
    return out


def _head_cols(m, n):
    lane = lax.broadcasted_iota(jnp.int32, (n, 256), 1)
    return jnp.concatenate([jnp.sum(jnp.where(lane == ATTN_HEAD_DIM * j, m, 0.0), axis=1, keepdims=True)
                            for j in range(HEADS_PER_PATTERN)], axis=0)


def _q_scores(q, kcat, i, nq, p, dil):
    s = _dot_nt(_stack_heads(q * 0.125), kcat)
    row = lax.broadcasted_iota(jnp.int32, s.shape, 0)
    col = lax.broadcasted_iota(jnp.int32, s.shape, 1)
    rel = col - 64 - (row & 127)
    valid = (jnp.abs(rel) <= 64) & ((i > 0) | (col >= 64)) & ((i < nq - 1) | (col < 192))
    sl = _slopes(p)
    hd = row >> 7
    slope = jnp.where(hd == 0, sl[0], jnp.where(hd == 1, sl[1], jnp.where(hd == 2, sl[2], sl[3])))
    s = s - slope * (jnp.abs(rel) * dil).astype(f32)
    return jnp.where(valid, s, NEG_BIG)


def attn_fwd(q, k, v, p, dil, name):
    l = q.shape[0]
    nq = l // 128

    def body(q_ref, kp_ref, ko_ref, kn_ref, vp_ref, vo_ref, vn_ref, o_ref, lse_ref):
        i = pl.program_id(1)
        kcat = jnp.concatenate([kp_ref[...], ko_ref[...], kn_ref[...]], axis=0)
        vcat = jnp.concatenate([vp_ref[...], vo_ref[...], vn_ref[...]], axis=0)
        s = _q_scores(q_ref[...], kcat, i, nq, p, dil)
        m = jnp.max(s, axis=1, keepdims=True)
        pr = jnp.exp(s - m)
        den = jnp.sum(pr, axis=1, keepdims=True)
        o4 = _dot(pr, vcat) / den
        o_ref[...] = _unstack_heads(o4, 128)
        lse_ref[...] = _unstack_heads(jnp.broadcast_to(m + jnp.log(den), (512, 256)), 128)

    col = lambda r: r
    return pl.pallas_call(
        body, name=name, grid=(dil, nq),
        in_specs=[pl.BlockSpec((128, 256), lambda r, i: (i, r))] + _win_specs(nq, col) + _win_specs(nq, col),
        out_specs=[pl.BlockSpec((128, 256), lambda r, i: (i, r))] * 2,
        out_shape=[jax.ShapeDtypeStruct(q.shape, f32)] * 2,
        compiler_params=_cparams(("parallel", "parallel")))(q, k, k, k, v, v, v)


def attn_combine(os_, lses, tb=1024):
    t = os_[0].shape[0]

    def body(o0, o1, o2, l0, l1, l2, y_ref, lse_ref):
        a0, a1, a2 = l0[...], l1[...], l2[...]
        m = jnp.maximum(jnp.maximum(a0, a1), a2)
        e0, e1, e2 = jnp.exp(a0 - m), jnp.exp(a1 - m), jnp.exp(a2 - m)
        den = e0 + e1 + e2
        y_ref[...] = (e0 * o0[...] + e1 * o1[...] + e2 * o2[...]) / den
        lse_ref[...] = m + jnp.log(den)

    blk = pl.BlockSpec((tb, 256), lambda i: (i, 0))
    return pl.pallas_call(
        body, name="attn_combine", grid=(t // tb,), in_specs=[blk] * 6, out_specs=[blk, blk],
        out_shape=[jax.ShapeDtypeStruct((t, 256), f32)] * 2,
        compiler_params=_cparams(("parallel",)))(*os_, *lses)


def attn_delta(dy, y, tb=1024):
    t = dy.shape[0]

    def body(dy_ref, y_ref, d_ref):
        pr = dy_ref[...] * y_ref[...]
        lane_head = _lane_head(pr.shape)
        out = jnp.zeros_like(pr)
        for j in range(HEADS_PER_PATTERN):
            sj = jnp.sum(jnp.where(lane_head == j, pr, 0.0), axis=1, keepdims=True)
            out = out + jnp.where(lane_head == j, sj, 0.0)
        d_ref[...] = out

    blk = pl.BlockSpec((tb, 256), lambda i: (i, 0))
    return pl.pallas_call(body, name="attn_delta", grid=(t // tb,), in_specs=[blk, blk], out_specs=blk,
                          out_shape=jax.ShapeDtypeStruct((t, 256), f32),
                          compiler_params=_cparams(("parallel",)))(dy, y)


def attn_dq(q, k, v, dy, lse, delta, p, dil, name):
    l = q.shape[0]
    nq = l // 128

    def body(q_ref, kp_ref, ko_ref, kn_ref, vp_ref, vo_ref, vn_ref, dy_ref, lse_ref, d_ref, dq_ref):
        i = pl.program_id(1)
        kcat = jnp.concatenate([kp_ref[...], ko_ref[...], kn_ref[...]], axis=0)
        vcat = jnp.concatenate([vp_ref[...], vo_ref[...], vn_ref[...]], axis=0)
        s = _q_scores(q_ref[...], kcat, i, nq, p, dil)
        pr = jnp.exp(s - _head_cols(lse_ref[...], 128))
        dp = _dot_nt(_stack_heads(dy_ref[...]), vcat)
        ds = pr * (dp - _head_cols(d_ref[...], 128))
        dq_ref[...] = (_unstack_heads(_dot(ds, kcat), 128) * 0.125).astype(dq_ref.dtype)

    col = lambda r: r
    own = pl.BlockSpec((128, 256), lambda r, i: (i, r))
    return pl.pallas_call(
        body, name=name, grid=(dil, nq),
        in_specs=[own] + _win_specs(nq, col) + _win_specs(nq, col) + [own, own, own], out_specs=own,
        out_shape=jax.ShapeDtypeStruct(q.shape, bf16),
        compiler_params=_cparams(("parallel", "parallel")))(q, k, k, k, v, v, v, dy, lse, delta)


def attn_dkv(q, k, v, dy, lse, delta, p, dil, name):
    l = q.shape[0]
    nq = l // 128

    def body(qp_ref, qo_ref, qn_ref, gp_ref, go_ref, gn_ref, lp_ref, lo_ref, ln_ref, dp_ref, do_ref, dn_ref,
             k_ref, v_ref, dk_ref, dv_ref):
        i = pl.program_id(1)
        cat = lambda a, b, c: jnp.concatenate([a[...], b[...], c[...]], axis=0)
        q4 = _stack_heads(cat(qp_ref, qo_ref, qn_ref) * 0.125)
        dy4 = _stack_heads(cat(gp_ref, go_ref, gn_ref))
        lse4 = _head_cols(cat(lp_ref, lo_ref, ln_ref), 256)
        del4 = _head_cols(cat(dp_ref, do_ref, dn_ref), 256)
        s = _dot_nt(q4, k_ref[...])
        row = lax.broadcasted_iota(jnp.int32, s.shape, 0)
        col = lax.broadcasted_iota(jnp.int32, s.shape, 1)
        qoff = row & 255
        rel = col - (qoff - 64)
        valid = (jnp.abs(rel) <= 64) & ((i > 0) | (qoff >= 64)) & ((i < nq - 1) | (qoff < 192))
        sl = _slopes(p)
        hd = row >> 8
        slope = jnp.where(hd == 0, sl[0], jnp.where(hd == 1, sl[1], jnp.where(hd == 2, sl[2], sl[3])))
        s = s - slope * (jnp.abs(rel) * dil).astype(f32)
        pr = jnp.where(valid, jnp.exp(jnp.where(valid, s, NEG_BIG) - lse4), 0.0)
        dpm = _dot_nt(dy4, v_ref[...])
        ds = pr * (dpm - del4)
        dv_ref[...] = _dot_tn(pr, dy4).astype(dv_ref.dtype)
        dk_ref[...] = _dot_tn(ds, q4).astype(dk_ref.dtype)

    col = lambda r: r
    own = pl.BlockSpec((128, 256), lambda r, i: (i, r))
    win = _win_specs(nq, col)
    return pl.pallas_call(
        body, name=name, grid=(dil, nq), in_specs=win * 4 + [own, own], out_specs=[own, own],
        out_shape=[jax.ShapeDtypeStruct(q.shape, bf16)] * 2,
        compiler_params=_cparams(("parallel", "parallel")))(q, q, q, dy, dy, dy, lse, lse, lse, delta, delta, delta, k, v)


def merge_fwd(u_gate, bg_row, y_ssd, y_att, tb=512):
    t = y_ssd.shape[0]

    def body(ga_ref, gb_ref, ba_ref, bb_ref, ys_ref, ya_ref, o_ref):
        o_ref[...] = (_sigmoid(ga_ref[...] + ba_ref[...]) * ys_ref[...]
                      + _sigmoid(gb_ref[...] + bb_ref[...]) * ya_ref[...]).astype(o_ref.dtype)

    blk = pl.BlockSpec((tb, 512), lambda i, j: (i, j))
    blk2 = pl.BlockSpec((tb, 512), lambda i, j: (i, 2 + j))
    row = pl.BlockSpec((1, 512), lambda i, j: (0, j))
    row2 = pl.BlockSpec((1, 512), lambda i, j: (0, 2 + j))
    return pl.pallas_call(
        body, name="merge_fwd", grid=(t // tb, 2), in_specs=[blk, blk2, row, row2, blk, blk], out_specs=blk,
        out_shape=jax.ShapeDtypeStruct((t, D_MODEL), bf16),
        compiler_params=_cparams(("parallel", "parallel")))(u_gate, u_gate, bg_row, bg_row, y_ssd, y_att)


def merge_bwd(dm, u_gate, bg_row, y_ssd, y_att, tb=512):
    t = dm.shape[0]

    def body(dm_ref, ga_ref, gb_ref, ba_ref, bb_ref, ys_ref, ya_ref, dys_ref, dya_ref, dga_ref, dgb_ref, dba_ref, dbb_ref):
        d = dm_ref[...]
        sa = _sigmoid(ga_ref[...] + ba_ref[...])
        sb = _sigmoid(gb_ref[...] + bb_ref[...])
        dys_ref[...] = (d * sa).astype(dys_ref.dtype)
        dya_ref[...] = (d * sb).astype(dya_ref.dtype)
        dla = d * ys_ref[...] * sa * (1.0 - sa)
        dlb = d * ya_ref[...] * sb * (1.0 - sb)
        dga_ref[...] = dla.astype(dga_ref.dtype)
        dgb_ref[...] = dlb.astype(dgb_ref.dtype)

        @pl.when(pl.program_id(1) == 0)
        def _():
            dba_ref[...] = jnp.zeros_like(dba_ref)
            dbb_ref[...] = jnp.zeros_like(dbb_ref)

        dba_ref[...] += jnp.sum(dla, axis=0, keepdims=True)
        dbb_ref[...] += jnp.sum(dlb, axis=0, keepdims=True)

    blk = pl.BlockSpec((tb, 512), lambda j, i: (i, j))
    blk2 = pl.BlockSpec((tb, 512), lambda j, i: (i, 2 + j))
    row = pl.BlockSpec((1, 512), lambda j, i: (0, j))
    row2 = pl.BlockSpec((1, 512), lambda j, i: (0, 2 + j))
    act = jax.ShapeDtypeStruct((t, D_MODEL), bf16)
    vec = jax.ShapeDtypeStruct((1, D_MODEL), f32)
    return pl.pallas_call(
        body, name="merge_bwd", grid=(2, t // tb), in_specs=[blk, blk, blk2, row, row2, blk, blk],
        out_specs=[blk, blk, blk, blk, row, row], out_shape=[act, act, act, act, vec, vec],
        compiler_params=_cparams(("parallel", "arbitrary")))(dm, u_gate, u_gate, bg_row, bg_row, y_ssd, y_att)


def _ln_stats(r):
    mu = jnp.mean(r, axis=1, keepdims=True)
    xc = r - mu
    rstd = lax.rsqrt(jnp.mean(xc * xc, axis=1, keepdims=True) + NORM_EPS)
    return xc * rstd, rstd


def _ln_bwd(dy, xhat, rstd, g_row):
    dxh = dy * g_row
    return rstd * (dxh - jnp.mean(dxh, axis=1, keepdims=True) - xhat * jnp.mean(dxh * xhat, axis=1, keepdims=True))


def ln1_fwd(x, mix, g_row, b_row, tb=512):
    t = x.shape[0]

    def body(x_ref, m_ref, g_ref, b_ref, o_ref, ob_ref):
        xhat, _ = _ln_stats(ALPHA * x_ref[...] + m_ref[...])
        h = xhat * g_ref[...] + b_ref[...]
        o_ref[...] = h
        ob_ref[...] = h.astype(ob_ref.dtype)

    blk = pl.BlockSpec((tb, D_MODEL), lambda i: (i, 0))
    row = pl.BlockSpec((1, D_MODEL), lambda i: (0, 0))
    return pl.pallas_call(body, name="ln1_fwd", grid=(t // tb,), in_specs=[blk, blk, row, row], out_specs=[blk, blk],
                          out_shape=[jax.ShapeDtypeStruct((t, D_MODEL), f32), jax.ShapeDtypeStruct((t, D_MODEL), bf16)],
                          compiler_params=_cparams(("parallel",)))(x, mix, g_row, b_row)


def ln1_bwd(dh, x, mix, g_row, tb=512):
    t = x.shape[0]

    def body(dh_ref, x_ref, m_ref, g_ref, dr_ref, drb_ref, dg_ref, db_ref):
        xhat, rstd = _ln_stats(ALPHA * x_ref[...] + m_ref[...])
        dy = dh_ref[...]
        dr = _ln_bwd(dy, xhat, rstd, g_ref[...])
        dr_ref[...] = dr
        drb_ref[...] = dr.astype(drb_ref.dtype)

        @pl.when(pl.program_id(0) == 0)
        def _():
            dg_ref[...] = jnp.zeros_like(dg_ref)
            db_ref[...] = jnp.zeros_like(db_ref)

        dg_ref[...] += jnp.sum(dy * xhat, axis=0, keepdims=True)
        db_ref[...] += jnp.sum(dy, axis=0, keepdims=True)

    blk = pl.BlockSpec((tb, D_MODEL), lambda i: (i, 0))
    row = pl.BlockSpec((1, D_MODEL), lambda i: (0, 0))
    return pl.pallas_call(
        body, name="ln1_bwd", grid=(t // tb,), in_specs=[blk, blk, blk, row], out_specs=[blk, blk, row, row],
        out_shape=[jax.ShapeDtypeStruct((t, D_MODEL), f32), jax.ShapeDtypeStruct((t, D_MODEL), bf16),
                   jax.ShapeDtypeStruct((1, D_MODEL), f32), jax.ShapeDtypeStruct((1, D_MODEL), f32)],
        compiler_params=_cparams(("arbitrary",)))(dh, x, mix, g_row)


def ln2_loss(h1, f, g_row, b_row, target, tb=512):
    t = h1.shape[0]

    def body(h_ref, f_ref, g_ref, b_ref, t_ref, dr_ref, drb_ref, dg_ref, db_ref, loss_ref):
        xhat, rstd = _ln_stats(ALPHA * h_ref[...] + f_ref[...])
        g = g_ref[...]
        err = xhat * g + b_ref[...] - t_ref[...]
        dy = err * (1.0 / D_MODEL)
        dr = _ln_bwd(dy, xhat, rstd, g)
        dr_ref[...] = dr
        drb_ref[...] = dr.astype(drb_ref.dtype)

        @pl.when(pl.program_id(0) == 0)
        def _():
            dg_ref[...] = jnp.zeros_like(dg_ref)
            db_ref[...] = jnp.zeros_like(db_ref)
            loss_ref[...] = jnp.zeros_like(loss_ref)

        dg_ref[...] += jnp.sum(dy * xhat, axis=0, keepdims=True)
        db_ref[...] += jnp.sum(dy, axis=0, keepdims=True)
        part = jnp.sum(jnp.mean(err * err, axis=1, keepdims=True), axis=0, keepdims=True)
        loss_ref[...] += 0.5 * part

    blk = pl.BlockSpec((tb, D_MODEL), lambda i: (i, 0))
    row = pl.BlockSpec((1, D_MODEL), lambda i: (0, 0))
    return pl.pallas_call(
        body, name="ln2_loss", grid=(t // tb,), in_specs=[blk, blk, row, row, blk],
        out_specs=[blk, blk, row, row, pl.BlockSpec((8, 128), lambda i: (0, 0))],
        out_shape=[jax.ShapeDtypeStruct((t, D_MODEL), f32), jax.ShapeDtypeStruct((t, D_MODEL), bf16),
                   jax.ShapeDtypeStruct((1, D_MODEL), f32), jax.ShapeDtypeStruct((1, D_MODEL), f32),
                   jax.ShapeDtypeStruct((8, 128), f32)],
        compiler_params=_cparams(("arbitrary",)))(h1, f, g_row, b_row, target)


TAIL_BLOCK, TAIL_AT = divmod(OFF_TAIL, PACK_TILE)


def _sum4(ref):
    return ((ref[0].astype(f32) + ref[1].astype(f32)) + ref[2].astype(f32)) + ref[3].astype(f32)


def adamw(parts, tails, w, m, v):
    rows = w.shape[0]
    c1 = 1.0 - ADAM_B1 ** ADAM_STEP
    c2 = 1.0 - ADAM_B2 ** ADAM_STEP

    def body(p_ref, t_ref, w_ref, m_ref, v_ref, g_ref, d_ref, nm_ref, nv_ref):
        g = _sum4(p_ref)
        with_tail = jnp.concatenate([g[0:TAIL_AT], _sum4(t_ref), g[TAIL_AT + ROWS_TAIL:]], axis=0)
        g = jnp.where(pl.program_id(0) == TAIL_BLOCK, with_tail, g)
        nm = ADAM_B1 * m_ref[...] + (1.0 - ADAM_B1) * g
        nv = ADAM_B2 * v_ref[...] + (1.0 - ADAM_B2) * (g * g)
        g_ref[...] = g
        nm_ref[...] = nm
        nv_ref[...] = nv
        d_ref[...] = -ADAM_LR * ((nm / c1) / (jnp.sqrt(nv / c2) + ADAM_EPS) + ADAM_WD * w_ref[...])

    blk = pl.BlockSpec((PACK_TILE, 1024), lambda i: (i, 0))
    out = jax.ShapeDtypeStruct((rows, 1024), f32)
    return pl.pallas_call(
        body, name="adamw", grid=(rows // PACK_TILE,),
        in_specs=[pl.BlockSpec((4, PACK_TILE, 1024), lambda i: (0, i, 0)),
                  pl.BlockSpec((4, ROWS_TAIL, 1024), lambda i: (0, 0, 0)), blk, blk, blk], out_specs=[blk] * 4,
        out_shape=[out] * 4, compiler_params=_cparams(("parallel",)))(parts, tails, w, m, v)


def pair_sum(parts, recv, core):
    rows = parts.shape[1]

    def body(c_ref, a_ref, b_ref, o_ref, t_ref):
        s = a_ref[...] + b_ref[...]
        o_ref[...] = s.astype(o_ref.dtype)

        @pl.when(pl.program_id(1) == TAIL_BLOCK)
        def _():
            t_ref[...] = s[:, TAIL_AT:TAIL_AT + ROWS_TAIL]

    grid_spec = pltpu.PrefetchScalarGridSpec(
        num_scalar_prefetch=1, grid=(4, rows // PACK_TILE),
        in_specs=[pl.BlockSpec((1, PACK_TILE, 1024), lambda j, i, c_ref: (2 * j + c_ref[0], i, 0)),
                  pl.BlockSpec((1, PACK_TILE, 1024), lambda j, i, c_ref: (j, i, 0))],
        out_specs=[pl.BlockSpec((1, PACK_TILE, 1024), lambda j, i, c_ref: (j, i, 0)),
                   pl.BlockSpec((1, ROWS_TAIL, 1024), lambda j, i, c_ref: (j, 0, 0))])
    return pl.pallas_call(
        body, name="pair_sum", grid_spec=grid_spec,
        out_shape=[jax.ShapeDtypeStruct(recv.shape, bf16), jax.ShapeDtypeStruct((4, ROWS_TAIL, 1024), f32)],
        compiler_params=_cparams(("parallel", "arbitrary")))(core, parts, recv)


def _place():
    return lax.axis_index("x"), lax.axis_index("y"), lax.axis_index("c")


def all_gather_blocks(shard):
    rows, cols = shard.shape

    def body(x_ref, out_ref, send_sems, recv_sems, local_sem):
        x, y, c = _place()
        me, sibling = (x, y, c), (x, y, 1 - c)
        chips = [(1 - x, y), (x, 1 - y), (1 - x, 1 - y)]

        def slot(px, py, pc):
            return out_ref.at[4 * px + 2 * py + pc]

        def copy(k, block, to, src=None):
            return pltpu.make_async_remote_copy(
                src_ref=slot(*block) if src is None else src, dst_ref=slot(*block), send_sem=send_sems.at[k],
                recv_sem=recv_sems.at[k], device_id=to, device_id_type=MESH)

        mine = pltpu.make_async_copy(x_ref, slot(*me), local_sem)
        mine.start()
        first = [copy(0, me, sibling, src=x_ref)]
        first += [copy(1 + j, me, (*chip, c), src=x_ref) for j, chip in enumerate(chips)]
        for cp in first:
            cp.start()
        passed = [copy(4 + j, (*chip, c), sibling) for j, chip in enumerate(chips)]
        for j, chip in enumerate(chips):
            copy(1 + j, (*chip, c), me).wait_recv()
            passed[j].start()
        copy(0, sibling, me).wait_recv()
        for j, chip in enumerate(chips):
            copy(4 + j, (*chip, 1 - c), me).wait_recv()
        for cp in first + passed:
            cp.wait_send()
        mine.wait()

    return pl.pallas_call(
        body, name="all_gather_blocks", out_shape=jax.ShapeDtypeStruct((N_DEV, rows, cols), shard.dtype),
        in_specs=[pl.BlockSpec(memory_space=pl.ANY)], out_specs=pl.BlockSpec(memory_space=pl.ANY),
        scratch_shapes=[pltpu.SemaphoreType.DMA((7,)), pltpu.SemaphoreType.DMA((7,)), pltpu.SemaphoreType.DMA],
        compiler_params=pltpu.CompilerParams(has_side_effects=True))(shard)


def pair_exchange(parts):
    _, rows, cols = parts.shape

    def body(p_ref, recv_ref, send_sems, recv_sems):
        x, y, c = _place()
        copies = [pltpu.make_async_remote_copy(
            src_ref=p_ref.at[2 * j + 1 - c], dst_ref=recv_ref.at[j], send_sem=send_sems.at[j], recv_sem=recv_sems.at[j],
            device_id=(x, y, 1 - c), device_id_type=MESH) for j in range(4)]
        for cp in copies:
            cp.start()
        for cp in copies:
            cp.wait_recv()
        for cp in copies:
            cp.wait_send()

    return pl.pallas_call(
        body, name="pair_exchange", out_shape=jax.ShapeDtypeStruct((4, rows, cols), parts.dtype),
        in_specs=[pl.BlockSpec(memory_space=pl.ANY)], out_specs=pl.BlockSpec(memory_space=pl.ANY),
        scratch_shapes=[pltpu.SemaphoreType.DMA((4,)), pltpu.SemaphoreType.DMA((4,))],
        compiler_params=pltpu.CompilerParams(has_side_effects=True))(parts)


def chip_exchange(parts):
    n = len(parts)

    def body(*refs):
        p_refs, out_refs = refs[0:n], refs[n:2 * n]
        send_sems, recv_sems, local_sems = refs[2 * n:]
        x, y, c = _place()
        mine = 2 * x + y
        flips = [(x, 1 - y), (1 - x, y), (1 - x, 1 - y)]

        def copy(a, k, src_slot, dst_slot):
            px, py = flips[k]
            return pltpu.make_async_remote_copy(
                src_ref=p_refs[a].at[src_slot], dst_ref=out_refs[a].at[dst_slot], send_sem=send_sems.at[3 * a + k],
                recv_sem=recv_sems.at[3 * a + k], device_id=(px, py, c), device_id_type=MESH)

        local = [pltpu.make_async_copy(p_refs[a].at[mine], out_refs[a].at[mine], local_sems.at[a]) for a in range(n)]
        sends = [copy(a, k, 2 * flips[k][0] + flips[k][1], mine) for a in range(n) for k in range(3)]
        for cp in local + sends:
            cp.start()
        for a in range(n):
            for k in range(3):
                copy(a, k, mine, 2 * flips[k][0] + flips[k][1]).wait_recv()
        for cp in sends:
            cp.wait_send()
        for cp in local:
            cp.wait()

    return pl.pallas_call(
        body, name="chip_exchange", out_shape=[jax.ShapeDtypeStruct(p.shape, p.dtype) for p in parts],
        in_specs=[pl.BlockSpec(memory_space=pl.ANY)] * n, out_specs=[pl.BlockSpec(memory_space=pl.ANY)] * n,
        scratch_shapes=[pltpu.SemaphoreType.DMA((3 * n,)), pltpu.SemaphoreType.DMA((3 * n,)), pltpu.SemaphoreType.DMA((n,))],
        compiler_params=pltpu.CompilerParams(has_side_effects=True))(*parts)


def _tail_rows(conv_part, small, extra):
    lead = conv_part.shape[:-1]
    rep = jnp.concatenate([small[n].reshape(-1).astype(f32) for n in SMALL] + [extra.reshape(1).astype(f32)])
    flat = jnp.concatenate([conv_part, jnp.broadcast_to(rep, lead + rep.shape),
                            jnp.zeros(lead + (ROWS_TAIL * 1024 - TAIL_ELEMS,), f32)], axis=-1)
    return flat.reshape(lead + (ROWS_TAIL, 1024))


def _pack_rows(w_in_t, w_ps, w_out, w_up_t, w_down, w_pa_t, tail):
    lead = tail.shape[:-2]
    zeros = lambda r: jnp.zeros(lead + (r, 1024), f32)
    return jnp.concatenate([w_in_t, zeros(ROWS_IN - IN_SHARD), w_ps, w_out, w_up_t, w_down,
                            w_pa_t.reshape(lead + (ROWS_PA, 1024)), tail,
                            zeros(PACK_ROWS - OFF_TAIL - ROWS_TAIL)], axis=-2)


def _pack_shard(vals):
    tail = _tail_rows(vals["conv_w"].reshape(-1), vals, jnp.zeros((), f32))
    return _pack_rows(vals["w_in"].T, vals["w_proj_ssd"], vals["w_out"], vals["w_up"].T, vals["w_down"],
                      vals["w_proj_attn"].T, tail)


def _unpack_shard(packed):
    out = {"w_in": packed[0:IN_SHARD].T, "w_proj_ssd": packed[OFF_PS:OFF_OUT], "w_out": packed[OFF_OUT:OFF_UP],
           "w_up": packed[OFF_UP:OFF_DOWN].T, "w_down": packed[OFF_DOWN:OFF_PA],
           "w_proj_attn": packed[OFF_PA:OFF_TAIL].reshape(D_MODEL // N_DEV, ATTN_OUT).T}
    flat = packed[OFF_TAIL:OFF_TAIL + ROWS_TAIL].reshape(-1)
    out["conv_w"] = flat[0:CONV_SHARD].reshape(D_CONV, CONV_DIM // N_DEV)
    off = CONV_SHARD
    for n in SMALL:
        out[n] = flat[off:off + SMALL_SIZES[n]]
        off += SMALL_SIZES[n]
    out["_extra"] = flat[off]
    return out


def _pack_parts(full, small, extra):
    conv = full["conv_w"].reshape(D_CONV, N_DEV, CONV_DIM // N_DEV).transpose(1, 0, 2).reshape(N_DEV, CONV_SHARD)
    blocks = lambda g: g.reshape(N_DEV, g.shape[0] // N_DEV, g.shape[1])
    return _pack_rows(blocks(full["w_in_t"]), blocks(full["w_proj_ssd"]), blocks(full["w_out"]), blocks(full["w_up_t"]),
                      blocks(full["w_down"]), blocks(full["w_proj_attn_t"]), _tail_rows(conv, small, extra))


def _gather_weights(w):
    conv_bits = lax.bitcast_convert_type(w["conv_w"], bf16).reshape(-1)
    conv_rows = jnp.concatenate([conv_bits, jnp.zeros((16 * 1024 - 2 * CONV_SHARD,), bf16)]).reshape(16, 1024)
    big = _pack_shard(w)[0:OFF_TAIL].astype(bf16)
    got = all_gather_blocks(jnp.concatenate([big, conv_rows], axis=0))
    whole = lambda lo, hi: got[:, lo:hi].reshape(N_DEV * (hi - lo), 1024)
    conv = lax.bitcast_convert_type(got[:, OFF_TAIL:OFF_TAIL + 4].reshape(N_DEV, 4096)[:, 0:2 * CONV_SHARD]
                                    .reshape(N_DEV, D_CONV, CONV_DIM // N_DEV, 2), f32)
    return {"w_in_t": whole(0, IN_SHARD), "w_proj_ssd": whole(OFF_PS, OFF_OUT), "w_out": whole(OFF_OUT, OFF_UP),
            "w_up_t": whole(OFF_UP, OFF_DOWN), "w_down": whole(OFF_DOWN, OFF_PA),
            "w_proj_attn_t": got[:, OFF_PA:OFF_TAIL].reshape(D_MODEL, ATTN_OUT),
            "conv_w": conv.transpose(1, 0, 2).reshape(D_CONV, CONV_DIM)}


def _row(v, width=None):
    v = v.reshape(1, -1).astype(f32)
    return v if width is None else jnp.pad(v, ((0, 0), (0, width - v.shape[1])))


def _lanes256(vf, vb):
    z = jnp.zeros((96,), f32)
    return jnp.concatenate([vf.astype(f32), z, vb.astype(f32), z]).reshape(1, 256)


def _local_step(x2, tgt, wf, p):
    t = x2.shape[0]
    o = np.cumsum((0,) + IN_SPLITS)
    wt = wf["w_in_t"]
    wt_z, wt_xbc = wt[o[0]:o[1]], wt[o[1]:o[2]]
    zpad = jnp.zeros((96, D_MODEL), wt.dtype)
    wt_dt = jnp.concatenate([wt[o[2]:o[3]], zpad, wt[o[3]:o[4]], zpad], axis=0)
    wt_qkv, wt_gate = wt[o[4]:o[7]], wt[o[7]:o[8]]

    conv_w8 = jnp.pad(wf["conv_w"].astype(f32), ((0, 8 - D_CONV), (0, 0)))
    conv_b = _row(p["conv_b"])
    dt_bias = _lanes256(p["dt_bias_f"], p["dt_bias_b"])
    a_f, a_b = -jnp.exp(p["a_log_f"].astype(f32)), -jnp.exp(p["a_log_b"].astype(f32))
    a_rows = _lanes256(a_f, a_b)
    dskip_row = jnp.repeat(p["d_skip"].astype(f32), SSD_HEAD_DIM).reshape(1, D_INNER)
    nw_row, bg_row = _row(p["ssd_norm_w"]), _row(p["b_gate"])
    g1, b1, g2, b2 = _row(p["ln1_g"]), _row(p["ln1_b"]), _row(p["ln2_g"]), _row(p["ln2_b"])

    xb = x2.astype(MXU_DTYPE)
    u_z = mm_nt(xb, wt_z, "in_z")
    u_xbc = mm_nt(xb, wt_xbc, "in_xbc")
    u_dt = mm_nt(xb, wt_dt, "in_dt")
    u_qkv = mm_nt(xb, wt_qkv, "in_qkv")
    u_gate = mm_nt(xb, wt_gate, "in_gate")
    xbc_c = conv_fwd(u_xbc, conv_w8, conv_b)
    dt2 = dt_fwd(u_dt, dt_bias)
    y_f, h_f = ssd_fwd(xbc_c, dt2, a_rows, False, "ssd_fwd_f")
    y_b, h_b = ssd_fwd(xbc_c, dt2, a_rows, True, "ssd_fwd_b")
    yn = tail_fwd(y_f, y_b, xbc_c, u_z, dskip_row, nw_row)
    y_ssd = mm_nn(yn, wf["w_proj_ssd"], "proj_ssd")

    def strided(a, dil):
        return a.reshape(t // dil, dil * 256)

    qkv, outs, lses = [], [], []
    for pi, (_, dil) in enumerate(DIL_PATTERNS):
        q, k, v = (strided(u_qkv[:, ATTN_WIDTH * s + 256 * pi: ATTN_WIDTH * s + 256 * (pi + 1)], dil) for s in range(3))
        qkv.append((q, k, v))
        op, lp = attn_fwd(q, k, v, pi, dil, f"attn_fwd_{pi}")
        outs.append(op.reshape(t, 256))
        lses.append(lp.reshape(t, 256))
    ya, lse = attn_combine(outs, lses)
    y_att = mm_nt(ya, wf["w_proj_attn_t"], "proj_attn")
    m = merge_fwd(u_gate, bg_row, y_ssd, y_att)
    mix = mm_nn(m, wf["w_out"], "out_proj")
    h1, h1b = ln1_fwd(x2, mix, g1, b1)
    a_up, p_act = mm_nt(h1b, wf["w_up_t"], "mlp_up", relu2=True)
    f_dn = mm_nn(p_act, wf["w_down"], "mlp_down")
    dr2, dr2b, dg2, db2, loss8 = ln2_loss(h1, f_dn, g2, b2, tgt)

    full, small = {}, {}
    da = mm_nt(dr2b, wf["w_down"], "d_mlp_act", out_dtype=bf16, relu2_of=a_up)
    full["w_down"] = mm_tn(p_act, dr2b, "dw_down")
    full["w_up_t"] = mm_tn(da, h1b, "dw_up")
    dh1 = mm_nn(da, wf["w_up_t"], "d_h1", acc_in=dr2, acc_scale=ALPHA)
    dr1, dr1b, dg1, db1 = ln1_bwd(dh1, x2, mix, g1)
    dm = mm_nt(dr1b, wf["w_out"], "d_merge")
    full["w_out"] = mm_tn(m, dr1b, "dw_out")
    dys, dya_p, dga, dgb, dba, dbb = merge_bwd(dm, u_gate, bg_row, y_ssd, y_att)
    dyn = mm_nt(dys, wf["w_proj_ssd"], "d_yn")
    full["w_proj_ssd"] = mm_tn(yn, dys, "dw_proj_ssd")
    dya = mm_nn(dya_p, wf["w_proj_attn_t"], "d_ya")
    full["w_proj_attn_t"] = mm_tn(dya_p, ya, "dw_proj_attn")

    dy, dz, dnw, ddx = tail_bwd(dyn, y_f, y_b, xbc_c, u_z, dskip_row, nw_row)
    dxf, dbf, dcf, ddtf, daf = ssd_bwd(xbc_c, dt2, a_rows, dy, h_f, False, "ssd_bwd_f")
    dxb, dbb_, dcb_, ddtb, dab = ssd_bwd(xbc_c, dt2, a_rows, dy, h_b, True, "ssd_bwd_b")
    sx = slice(0, D_INNER)
    sb = slice(D_INNER, D_INNER + 512)
    sc = slice(D_INNER + 512, CONV_DIM)
    dxbc_x, dcw_x, dcb_x = conv_bwd(u_xbc, 0, [dxf, dxb, dy], [None, None, dskip_row], conv_w8[:, sx], conv_b[:, sx], "conv_bwd_x")
    dxbc_b, dcw_b, dcb_b = conv_bwd(u_xbc, 4, [dbf, dbb_], [None, None], conv_w8[:, sb], conv_b[:, sb], "conv_bwd_b")
    dxbc_c, dcw_c, dcb_c = conv_bwd(u_xbc, 5, [dcf, dcb_], [None, None], conv_w8[:, sc], conv_b[:, sc], "conv_bwd_c")
    du_dt, dbias = dt_bwd(ddtf, ddtb, u_dt, dt_bias)

    delta = attn_delta(dya, ya)
    dqs, dks, dvs = [], [], []
    for pi, (_, dil) in enumerate(DIL_PATTERNS):
        q, k, v = qkv[pi]
        sd, sl_, sdel = strided(dya, dil), strided(lse, dil), strided(delta, dil)
        dqs.append(attn_dq(q, k, v, sd, sl_, sdel, pi, dil, f"attn_dq_{pi}").reshape(t, 256))
        dk, dv = attn_dkv(q, k, v, sd, sl_, sdel, pi, dil, f"attn_dkv_{pi}")
        dks.append(dk.reshape(t, 256))
        dvs.append(dv.reshape(t, 256))
    du_qkv = jnp.concatenate(dqs + dks + dvs, axis=1)
    du_xbc = jnp.concatenate([dxbc_x, dxbc_b, dxbc_c], axis=1)
    du_gate = jnp.concatenate([dga, dgb], axis=1)

    dx = mm_nn(dz, wt_z, "dx_z", acc_in=dr1, acc_scale=ALPHA)
    dx = mm_nn(du_xbc, wt_xbc, "dx_xbc", acc_in=dx)
    dx = mm_nn(du_dt, wt_dt, "dx_dt", acc_in=dx)
    dx = mm_nn(du_qkv, wt_qkv, "dx_qkv", acc_in=dx)
    dx = mm_nn(du_gate, wt_gate, "dx_gate", acc_in=dx)
    dw_dt = mm_tn(du_dt, xb, "dw_in_dt")
    full["w_in_t"] = jnp.concatenate(
        [mm_tn(dz, xb, "dw_in_z"), mm_tn(du_xbc, xb, "dw_in_xbc"), dw_dt[0:32], dw_dt[128:160],
         mm_tn(du_qkv, xb, "dw_in_qkv"), mm_tn(du_gate, xb, "dw_in_gate")], axis=0)
    full["conv_w"] = jnp.concatenate([dcw_x[:D_CONV], dcw_b[:D_CONV], dcw_c[:D_CONV]], axis=1)

    small["b_gate"] = jnp.concatenate([dba, dbb], axis=1)
    small["conv_b"] = jnp.concatenate([dcb_x, dcb_b, dcb_c], axis=1)
    small["dt_bias_f"], small["dt_bias_b"] = dbias[0, 0:32], dbias[0, 128:160]
    small["a_log_f"] = jnp.sum(daf, axis=(0, 1))[0:32] * a_f
    small["a_log_b"] = jnp.sum(dab, axis=(0, 1))[0:32] * a_b
    small["d_skip"] = jnp.sum(ddx.reshape(SSD_HEADS, SSD_HEAD_DIM), axis=1)
    small["ssd_norm_w"] = dnw
    small["ln1_g"], small["ln1_b"], small["ln2_g"], small["ln2_b"] = dg1, db1, dg2, db2
    return loss8[0, 0], dx, full, small


def kernel(x, w_in, b_gate, conv_w, conv_b, dt_bias_f, dt_bias_b, a_log_f, a_log_b, d_skip, ssd_norm_w, w_proj_ssd, w_proj_attn, w_out, ln1_g, ln1_b, w_up, w_down, ln2_g, ln2_b, loss_target, m_w_in, m_b_gate, m_conv_w, m_conv_b, m_dt_bias_f, m_dt_bias_b, m_a_log_f, m_a_log_b, m_d_skip, m_ssd_norm_w, m_w_proj_ssd, m_w_proj_attn, m_w_out, m_ln1_g, m_ln1_b, m_w_up, m_w_down, m_ln2_g, m_ln2_b, v_w_in, v_b_gate, v_conv_w, v_conv_b, v_dt_bias_f, v_dt_bias_b, v_a_log_f, v_a_log_b, v_d_skip, v_ssd_norm_w, v_w_proj_ssd, v_w_proj_attn, v_w_out, v_ln1_g, v_ln1_b, v_w_up, v_w_down, v_ln2_g, v_ln2_b):
    given = dict(locals())
    w = {n: given[n] for n in WEIGHTS}
    mom = {n: given["m_" + n] for n in WEIGHTS}
    var = {n: given["v_" + n] for n in WEIGHTS}
    t = x.shape[1]
    wf = _gather_weights(w)
    loss, dx, full, small = _local_step(x.reshape(t, D_MODEL), loss_target.reshape(t, D_MODEL), wf, w)
    packed = _pack_parts(full, small, loss)
    core = lax.axis_index("c").astype(jnp.int32).reshape(1)
    parts, tails = chip_exchange(pair_sum(packed, pair_exchange(packed), core))
    g, delta, new_m, new_v = (_unpack_shard(a) for a in
                              adamw(parts, tails, _pack_shard(w), _pack_shard(mom), _pack_shard(var)))
    outs = [g["_extra"], dx.reshape(x.shape)]
    for d in (g, delta, new_m, new_v):
        outs += [d[n].reshape(w[n].shape) for n in WEIGHTS]
    return tuple(outs)
```

```python
import functools
import math

import jax
import jax.numpy as jnp
import numpy as np
from jax import lax
from jax.experimental import pallas as pl
from jax.experimental.pallas import tpu as pltpu

f32 = jnp.float32
bf16 = jnp.bfloat16
MXU_DTYPE = jnp.bfloat16

N_DEV = 8
D_MODEL = 1024
D_INNER = 2048
SSD_HEADS = 32
SSD_HEAD_DIM = 64
SSD_GROUPS = 4
D_STATE = 128
D_CONV = 5
CHUNK = 128
CONV_DIM = D_INNER + 2 * SSD_GROUPS * D_STATE
NORM_EPS = 1e-5
ATTN_HEAD_DIM = 64
DIL_PATTERNS = ((128, 1), (512, 4), (2048, 16))
HEADS_PER_PATTERN = 4
ATTN_HEADS = 12
ATTN_WIDTH = 768
ATTN_OUT = 256
D_FF = 4096
ALPHA = 2.0 ** 0.25
IN_SPLITS = (D_INNER, CONV_DIM, SSD_HEADS, SSD_HEADS, ATTN_WIDTH, ATTN_WIDTH, ATTN_WIDTH, 2 * D_MODEL)
IN_COLS = sum(IN_SPLITS)
ADAM_LR, ADAM_B1, ADAM_B2, ADAM_EPS, ADAM_WD, ADAM_STEP = 0.001, 0.9, 0.999, 1e-08, 0.01, 10
NEG_BIG = -1e30
VMEM_LIMIT = 56 * 1024 * 1024
MESH = pl.DeviceIdType.MESH

SMALL = ("b_gate", "conv_b", "dt_bias_f", "dt_bias_b", "a_log_f", "a_log_b", "d_skip", "ssd_norm_w",
         "ln1_g", "ln1_b", "ln2_g", "ln2_b")
WEIGHTS = ("w_in", "b_gate", "conv_w", "conv_b", "dt_bias_f", "dt_bias_b", "a_log_f", "a_log_b", "d_skip",
           "ssd_norm_w", "w_proj_ssd", "w_proj_attn", "w_out", "ln1_g", "ln1_b", "w_up", "w_down", "ln2_g", "ln2_b")
SMALL_SIZES = {"b_gate": 2 * D_MODEL, "conv_b": CONV_DIM, "dt_bias_f": 32, "dt_bias_b": 32, "a_log_f": 32, "a_log_b": 32,
               "d_skip": 32, "ssd_norm_w": D_INNER, "ln1_g": D_MODEL, "ln1_b": D_MODEL, "ln2_g": D_MODEL, "ln2_b": D_MODEL}
IN_SHARD = IN_COLS // N_DEV
ROWS_IN = 1200
ROWS_PS, ROWS_OUT, ROWS_UP, ROWS_DOWN, ROWS_PA = D_INNER // N_DEV, D_MODEL // N_DEV, D_FF // N_DEV, D_FF // N_DEV, 32
OFF_PS = ROWS_IN
OFF_OUT = OFF_PS + ROWS_PS
OFF_UP = OFF_OUT + ROWS_OUT
OFF_DOWN = OFF_UP + ROWS_UP
OFF_PA = OFF_DOWN + ROWS_DOWN
OFF_TAIL = OFF_PA + ROWS_PA
CONV_SHARD = D_CONV * CONV_DIM // N_DEV
TAIL_ELEMS = CONV_SHARD + sum(SMALL_SIZES.values()) + 1
ROWS_TAIL = 16
PACK_TILE = 128
PACK_ROWS = -(-(OFF_TAIL + ROWS_TAIL) // PACK_TILE) * PACK_TILE


def _cparams(sem=None, **kw):
    return pltpu.CompilerParams(dimension_semantics=sem, vmem_limit_bytes=VMEM_LIMIT, **kw)


def _mx(v):
    return v.astype(MXU_DTYPE)


def _dot(a, b):
    return jnp.dot(_mx(a), _mx(b), preferred_element_type=f32)


def _dot_nt(a, b):
    return lax.dot_general(_mx(a), _mx(b), (((1,), (1,)), ((), ())), preferred_element_type=f32)


def _dot_tn(a, b):
    return lax.dot_general(_mx(a), _mx(b), (((0,), (0,)), ((), ())), preferred_element_type=f32)


def _dot_exact(a, b):
    return jnp.dot(a, b, precision=lax.Precision.HIGHEST, preferred_element_type=f32)


def _sigmoid(v):
    return 1.0 / (1.0 + jnp.exp(-v))


def _pick(n, prefs):
    for p in prefs:
        if n % p == 0:
            return p
    return n


MM_TILE = 1024


def mm_nn(a, b, name, out_dtype=f32, acc_in=None, acc_scale=1.0):
    m, k = a.shape
    n = b.shape[1]
    tm = _pick(m, (MM_TILE, 512, 256, 128, 64))
    tn = _pick(n, (MM_TILE, 512, 256, 128))
    tk = _pick(k, (2048, 1536, 1152, 1024, 768, 512, 256, 128))
    nk = k // tk

    def body(*refs):
        a_ref, b_ref = refs[0:2]
        c_ref = refs[2] if acc_in is not None else None
        o_ref = refs[3] if acc_in is not None else refs[2]

        def finish(r):
            if acc_in is not None:
                r = r + acc_scale * c_ref[...]
            o_ref[...] = r.astype(o_ref.dtype)

        if nk == 1:
            finish(_dot(a_ref[...], b_ref[...]))
            return
        acc_ref = refs[-1]
        kk = pl.program_id(2)

        @pl.when(kk == 0)
        def _():
            acc_ref[...] = jnp.zeros_like(acc_ref)

        acc_ref[...] += _dot(a_ref[...], b_ref[...])

        @pl.when(kk == nk - 1)
        def _():
            finish(acc_ref[...])

    in_specs = [pl.BlockSpec((tm, tk), lambda i, j, kk: (i, kk)), pl.BlockSpec((tk, tn), lambda i, j, kk: (kk, j))]
    args = [a, b]
    if acc_in is not None:
        in_specs.append(pl.BlockSpec((tm, tn), lambda i, j, kk: (i, j)))
        args.append(acc_in)
    return pl.pallas_call(
        body, name=name, grid=(m // tm, n // tn, nk), in_specs=in_specs,
        out_specs=pl.BlockSpec((tm, tn), lambda i, j, kk: (i, j)),
        out_shape=jax.ShapeDtypeStruct((m, n), out_dtype),
        scratch_shapes=[pltpu.VMEM((tm, tn), f32)] if nk > 1 else [],
        compiler_params=_cparams(("parallel", "parallel", "arbitrary")))(*args)


def mm_nt(a, b, name, out_dtype=f32, relu2=None, relu2_of=None):
    m, k = a.shape
    n = b.shape[0]
    tm = MM_TILE
    tn = _pick(n, (MM_TILE, 768, 512, 256, 128))

    def body(*refs):
        r = _dot_nt(refs[0][...], refs[1][...])
        if relu2:
            refs[2][...] = r
            pos = jnp.maximum(r, 0.0)
            refs[3][...] = (pos * pos).astype(refs[3].dtype)
        elif relu2_of is not None:
            refs[3][...] = (r * (2.0 * jnp.maximum(refs[2][...], 0.0))).astype(refs[3].dtype)
        else:
            refs[2][...] = r.astype(refs[2].dtype)

    blk = pl.BlockSpec((tm, tn), lambda i, j: (i, j))
    in_specs = [pl.BlockSpec((tm, k), lambda i, j: (i, 0)), pl.BlockSpec((tn, k), lambda i, j: (j, 0))]
    args = [a, b]
    if relu2_of is not None:
        in_specs.append(blk)
        args.append(relu2_of)
    if relu2:
        out_specs, out_shape = [blk, blk], [jax.ShapeDtypeStruct((m, n), f32), jax.ShapeDtypeStruct((m, n), bf16)]
    else:
        out_specs, out_shape = blk, jax.ShapeDtypeStruct((m, n), out_dtype)
    return pl.pallas_call(body, name=name, grid=(m // tm, n // tn), in_specs=in_specs, out_specs=out_specs,
                          out_shape=out_shape, compiler_params=_cparams(("parallel", "parallel")))(*args)


def mm_tn(a, b, name, acc_in=None, acc_scale=1.0):
    k, m = a.shape
    n = b.shape[1]
    tm = _pick(m, (MM_TILE, 768, 512, 256, 128))
    tn = _pick(n, (MM_TILE, 512, 256, 128))
    tk = _pick(k, (1024, 768, 512, 256, 128, 64))
    nk = k // tk

    def body(*refs):
        a_ref, b_ref, o_ref = refs[0], refs[1], refs[-1]
        kk = pl.program_id(2)

        @pl.when(kk == 0)
        def _():
            o_ref[...] = jnp.zeros_like(o_ref) if acc_in is None else acc_scale * refs[2][...]

        o_ref[...] += _dot_tn(a_ref[...], b_ref[...])

    in_specs = [pl.BlockSpec((tk, tm), lambda i, j, kk: (kk, i)), pl.BlockSpec((tk, tn), lambda i, j, kk: (kk, j))]
    args = [a, b]
    if acc_in is not None:
        in_specs.append(pl.BlockSpec((tm, tn), lambda i, j, kk: (i, j)))
        args.append(acc_in)
    return pl.pallas_call(
        body, name=name, grid=(m // tm, n // tn, nk), in_specs=in_specs,
        out_specs=pl.BlockSpec((tm, tn), lambda i, j, kk: (i, j)),
        out_shape=jax.ShapeDtypeStruct((m, n), f32),
        compiler_params=_cparams(("parallel", "parallel", "arbitrary")))(*args)


def _halo_specs(tb, cb, nt, off=0):
    r = tb // 8
    return [pl.BlockSpec((8, cb), lambda j, i: (jnp.maximum(i * r - 1, 0), j + off)),
            pl.BlockSpec((tb, cb), lambda j, i: (i, j + off)),
            pl.BlockSpec((8, cb), lambda j, i: (jnp.minimum((i + 1) * r, nt * r - 1), j + off))]


def _with_halo(prev_ref, own_ref, next_ref, i, nt):
    prev = jnp.where(i > 0, prev_ref[...].astype(f32), 0.0)
    nxt = jnp.where(i < nt - 1, next_ref[...].astype(f32), 0.0)
    return jnp.concatenate([prev, own_ref[...].astype(f32), nxt], axis=0)


def _shifted(xcat, s, tb):
    n = xcat.shape[0]
    return pltpu.roll(xcat, (-s) % n, 0)[8:8 + tb]


def conv_fwd(xbc, w8, b_row, tb=512, cb=512):
    t, c = xbc.shape
    nt = t // tb

    def body(prev_ref, own_ref, next_ref, w_ref, b_ref, o_ref):
        i = pl.program_id(1)
        xcat = _with_halo(prev_ref, own_ref, next_ref, i, nt)
        w = w_ref[...]
        pre = b_ref[...] + w[0:1] * _shifted(xcat, -2, tb)
        for k in range(1, D_CONV):
            pre = pre + w[k:k + 1] * _shifted(xcat, k - 2, tb)
        o_ref[...] = pre * _sigmoid(pre)

    return pl.pallas_call(
        body, name="conv_fwd", grid=(c // cb, nt),
        in_specs=_halo_specs(tb, cb, nt) + [pl.BlockSpec((8, cb), lambda j, i: (0, j)), pl.BlockSpec((1, cb), lambda j, i: (0, j))],
        out_specs=pl.BlockSpec((tb, cb), lambda j, i: (i, j)), out_shape=jax.ShapeDtypeStruct((t, c), f32),
        compiler_params=_cparams(("parallel", "parallel")))(xbc, xbc, xbc, w8, b_row)


def conv_bwd(xbc, xoff, grads, scales, w8, b_row, name, tb=512, cb=512):
    t, c = grads[0].shape
    nt = t // tb
    ng = len(grads)
    has_scale = [s is not None for s in scales]

    def body(*refs):
        i = pl.program_id(1)
        xr = refs[0:3]
        gr = [refs[3 + 3 * q: 6 + 3 * q] for q in range(ng)]
        pos = 3 + 3 * ng
        sr = []
        for q in range(ng):
            if has_scale[q]:
                sr.append(refs[pos])
                pos += 1
            else:
                sr.append(None)
        w_ref, b_ref, dx_ref, dw_ref, db_ref = refs[pos:pos + 5]
        xcat = _with_halo(*xr, i, nt)
        gcat = None
        for q in range(ng):
            gq = _with_halo(*gr[q], i, nt)
            if sr[q] is not None:
                gq = gq * sr[q][...]
            gcat = gq if gcat is None else gcat + gq
        w = w_ref[...]
        n = tb + 16
        pre = b_ref[...] + w[0:1] * pltpu.roll(xcat, 2, 0)
        for k in range(1, D_CONV):
            pre = pre + w[k:k + 1] * pltpu.roll(xcat, (2 - k) % n, 0)
        sg = _sigmoid(pre)
        dpre = gcat * sg * (1.0 + pre * (1.0 - sg))
        dx = w[0:1] * _shifted(dpre, 2, tb)
        for k in range(1, D_CONV):
            dx = dx + w[k:k + 1] * _shifted(dpre, 2 - k, tb)
        dx_ref[...] = dx.astype(dx_ref.dtype)
        dp_own = dpre[8:8 + tb]
        rows = [jnp.sum(dp_own * _shifted(xcat, k - 2, tb), axis=0, keepdims=True) for k in range(D_CONV)]
        dw = jnp.concatenate(rows + [jnp.zeros((8 - D_CONV, cb), f32)], axis=0)
        db = jnp.sum(dp_own, axis=0, keepdims=True)

        @pl.when(i == 0)
        def _():
            dw_ref[...] = jnp.zeros_like(dw_ref)
            db_ref[...] = jnp.zeros_like(db_ref)

        dw_ref[...] += dw
        db_ref[...] += db

    in_specs = _halo_specs(tb, cb, nt, xoff)
    args = [xbc] * 3
    for g in grads:
        in_specs += _halo_specs(tb, cb, nt)
        args += [g] * 3
    for s in scales:
        if s is not None:
            in_specs.append(pl.BlockSpec((1, cb), lambda j, i: (0, j)))
            args.append(s)
    in_specs += [pl.BlockSpec((8, cb), lambda j, i: (0, j)), pl.BlockSpec((1, cb), lambda j, i: (0, j))]
    args += [w8, b_row]
    return pl.pallas_call(
        body, name=name, grid=(c // cb, nt), in_specs=in_specs,
        out_specs=[pl.BlockSpec((tb, cb), lambda j, i: (i, j)), pl.BlockSpec((8, cb), lambda j, i: (0, j)),
                   pl.BlockSpec((1, cb), lambda j, i: (0, j))],
        out_shape=[jax.ShapeDtypeStruct((t, c), bf16), jax.ShapeDtypeStruct((8, c), f32), jax.ShapeDtypeStruct((1, c), f32)],
        compiler_params=_cparams(("parallel", "arbitrary")))(*args)


def dt_fwd(u_dt, bias_row, tb=1024):
    t = u_dt.shape[0]

    def body(u_ref, b_ref, o_ref):
        v = u_ref[...] + b_ref[...]
        sp = jnp.maximum(v, 0.0) + jnp.log(1.0 + jnp.exp(-jnp.abs(v)))
        lane = lax.broadcasted_iota(jnp.int32, v.shape, 1)
        o_ref[...] = jnp.where((lane & 127) < SSD_HEADS, sp, 0.0)

    return pl.pallas_call(
        body, name="dt_fwd", grid=(t // tb,),
        in_specs=[pl.BlockSpec((tb, 256), lambda i: (i, 0)), pl.BlockSpec((1, 256), lambda i: (0, 0))],
        out_specs=pl.BlockSpec((tb, 256), lambda i: (i, 0)), out_shape=jax.ShapeDtypeStruct((t, 256), f32),
        compiler_params=_cparams(("parallel",)))(u_dt, bias_row)


def dt_bwd(ddt_f, ddt_b, u_dt, bias_row, tb=1024):
    t = u_dt.shape[0]

    def body(gf_ref, gb_ref, u_ref, b_ref, du_ref, db_ref):
        g = jnp.concatenate([jnp.sum(gf_ref[...], axis=0), jnp.sum(gb_ref[...], axis=0)], axis=1)
        du = g * _sigmoid(u_ref[...] + b_ref[...])
        du_ref[...] = du.astype(du_ref.dtype)

        @pl.when(pl.program_id(0) == 0)
        def _():
            db_ref[...] = jnp.zeros_like(db_ref)

        db_ref[...] += jnp.sum(du, axis=0, keepdims=True)

    return pl.pallas_call(
        body, name="dt_bwd", grid=(t // tb,),
        in_specs=[pl.BlockSpec((4, tb, 128), lambda i: (0, i, 0)), pl.BlockSpec((4, tb, 128), lambda i: (0, i, 0)),
                  pl.BlockSpec((tb, 256), lambda i: (i, 0)), pl.BlockSpec((1, 256), lambda i: (0, 0))],
        out_specs=[pl.BlockSpec((tb, 256), lambda i: (i, 0)), pl.BlockSpec((1, 256), lambda i: (0, 0))],
        out_shape=[jax.ShapeDtypeStruct((t, 256), bf16), jax.ShapeDtypeStruct((1, 256), f32)],
        compiler_params=_cparams(("arbitrary",)))(ddt_f, ddt_b, u_dt, bias_row)


def _ssd_common(dt_blk, a_row, reverse):
    row = lax.broadcasted_iota(jnp.int32, (CHUNK, CHUNK), 0)
    col = lax.broadcasted_iota(jnp.int32, (CHUNK, CHUNK), 1)
    mask = (row <= col) if reverse else (row >= col)
    tri = mask.astype(f32)
    a = dt_blk * a_row
    acs = _dot_exact(tri, a)
    atot = jnp.sum(a, axis=0, keepdims=True)
    return mask, tri, a, acs, atot, col


def _lane_col(mat, lane_idx, h):
    return jnp.sum(jnp.where(lane_idx == h, mat, 0.0), axis=1, keepdims=True)


def ssd_fwd(xbc_c, dt2, a_rows, reverse, name):
    t = xbc_c.shape[0]
    nc = t // CHUNK
    d_off = 1 if reverse else 0

    def cidx(c):
        return nc - 1 - c if reverse else c

    def body(x_ref, b_ref, c_ref, dt_ref, a_ref, y_ref, hp_ref, h_scr, acst_scr):
        g = pl.program_id(0)
        c = pl.program_id(1)

        @pl.when(c == 0)
        def _():
            h_scr[...] = jnp.zeros_like(h_scr)

        dt_blk = dt_ref[...]
        mask, tri, a, acs, atot, lane = _ssd_common(dt_blk, a_ref[...], reverse)
        acst_scr[...] = acs.T
        bm = b_ref[...]
        cm = c_ref[...]
        cb = _dot_nt(cm, bm)
        half = lane >= SSD_HEAD_DIM
        sub_half = lax.broadcasted_iota(jnp.int32, (CHUNK, 1), 0) >= SSD_HEAD_DIM
        for j in range(4):
            x = x_ref[:, 128 * j:128 * (j + 1)]
            cols, dts, tots = [], [], []
            y = None
            for e in range(2):
                h = 8 * g + 2 * j + e
                col_h = _lane_col(acs, lane, h)
                row_h = acst_scr[pl.ds(h, 1), :]
                dt_h = _lane_col(dt_blk, lane, h)
                lmat = jnp.where(mask, jnp.exp(jnp.where(mask, col_h - row_h, 0.0)), 0.0)
                xdt_e = jnp.where(half == (e == 1), x * dt_h, 0.0)
                ye = _dot(cb * lmat, xdt_e)
                y = ye if y is None else y + ye
                cols.append(col_h)
                dts.append(dt_h)
                tots.append(jnp.sum(jnp.where(lane[0:1] == h, atot, 0.0), axis=1, keepdims=True))
            hp = h_scr[j]
            hp_ref[0, j] = hp
            ecol = jnp.where(half, jnp.exp(cols[1]), jnp.exp(cols[0]))
            y = y + _dot_nt(cm, hp) * ecol
            y_ref[:, 128 * j:128 * (j + 1)] = y
            dec = jnp.where(half, jnp.exp(tots[1] - cols[1]), jnp.exp(tots[0] - cols[0]))
            xdt = x * jnp.where(half, dts[1], dts[0])
            s_new = _dot_tn(xdt * dec, bm)
            cd = jnp.where(sub_half, jnp.exp(tots[1]), jnp.exp(tots[0]))
            h_scr[j] = cd * hp + s_new

    return pl.pallas_call(
        body, name=name, grid=(SSD_GROUPS, nc),
        in_specs=[pl.BlockSpec((CHUNK, 512), lambda g, c: (cidx(c), g)),
                  pl.BlockSpec((CHUNK, 128), lambda g, c: (cidx(c), 16 + g)),
                  pl.BlockSpec((CHUNK, 128), lambda g, c: (cidx(c), 20 + g)),
                  pl.BlockSpec((CHUNK, 128), lambda g, c: (cidx(c), d_off)),
                  pl.BlockSpec((1, 128), lambda g, c: (0, d_off))],
        out_specs=[pl.BlockSpec((CHUNK, 512), lambda g, c: (cidx(c), g)),
                   pl.BlockSpec((1, 4, 128, 128), lambda g, c: (cidx(c), g, 0, 0))],
        out_shape=[jax.ShapeDtypeStruct((t, D_INNER), f32), jax.ShapeDtypeStruct((nc, 16, 128, 128), f32)],
        scratch_shapes=[pltpu.VMEM((4, 128, 128), f32), pltpu.VMEM((CHUNK, CHUNK), f32)],
        compiler_params=_cparams(("parallel", "arbitrary")))(xbc_c, xbc_c, xbc_c, dt2, a_rows)


def ssd_bwd(xbc_c, dt2, a_rows, dy, hprev, reverse, name):
    t = xbc_c.shape[0]
    nc = t // CHUNK
    d_off = 1 if reverse else 0

    def cidx(c):
        return c if reverse else nc - 1 - c

    def body(x_ref, b_ref, c_ref, dt_ref, a_ref, dy_ref, hp_ref, dx_ref, db_ref, dc_ref, ddt_ref, da_ref,
             dh_scr, acst_scr):
        g = pl.program_id(0)
        c = pl.program_id(1)

        @pl.when(c == 0)
        def _():
            dh_scr[...] = jnp.zeros_like(dh_scr)
            da_ref[...] = jnp.zeros_like(da_ref)

        dt_blk = dt_ref[...]
        a_row = a_ref[...]
        mask, tri, a, acs, atot, lane = _ssd_common(dt_blk, a_row, reverse)
        acst_scr[...] = acs.T
        sub = lax.broadcasted_iota(jnp.int32, (CHUNK, CHUNK), 0)
        bm = b_ref[...]
        cm = c_ref[...]
        cb = _dot_nt(cm, bm)
        half = lane >= SSD_HEAD_DIM
        sub_half = sub[:, 0:1] >= SSD_HEAD_DIM
        dcb = jnp.zeros((CHUNK, CHUNK), f32)
        dacs = jnp.zeros((CHUNK, CHUNK), f32)
        dacs_t = jnp.zeros((CHUNK, CHUNK), f32)
        dtot = jnp.zeros((1, CHUNK), f32)
        ddt_x = jnp.zeros((CHUNK, CHUNK), f32)
        dbm = jnp.zeros((CHUNK, D_STATE), f32)
        dcm = jnp.zeros((CHUNK, D_STATE), f32)
        for j in range(4):
            x = x_ref[:, 128 * j:128 * (j + 1)]
            dyp = dy_ref[:, 128 * j:128 * (j + 1)]
            hp = hp_ref[0, j]
            dhn = dh_scr[j]
            cols, dts, tots, hs = [], [], [], []
            dxdt = None
            for e in range(2):
                h = 8 * g + 2 * j + e
                sel = half == (e == 1)
                col_h = _lane_col(acs, lane, h)
                row_h = acst_scr[pl.ds(h, 1), :]
                dt_h = _lane_col(dt_blk, lane, h)
                lmat = jnp.where(mask, jnp.exp(jnp.where(mask, col_h - row_h, 0.0)), 0.0)
                xdt_e = jnp.where(sel, x * dt_h, 0.0)
                dy_e = jnp.where(sel, dyp, 0.0)
                ml = _dot_nt(dy_e, xdt_e) * lmat
                dcb = dcb + ml
                w = ml * cb
                dacs = dacs + jnp.where(lane == h, jnp.sum(w, axis=1, keepdims=True), 0.0)
                dacs_t = dacs_t - jnp.where(sub == h, jnp.sum(w, axis=0, keepdims=True), 0.0)
                de = _dot_tn(cb * lmat, dy_e)
                dxdt = de if dxdt is None else dxdt + de
                cols.append(col_h)
                dts.append(dt_h)
                tots.append(jnp.sum(jnp.where(lane[0:1] == h, atot, 0.0), axis=1, keepdims=True))
                hs.append(h)
            ecol = jnp.where(half, jnp.exp(cols[1]), jnp.exp(cols[0]))
            dec = jnp.where(half, jnp.exp(tots[1] - cols[1]), jnp.exp(tots[0] - cols[0]))
            cd = jnp.where(sub_half, jnp.exp(tots[1]), jnp.exp(tots[0]))
            dtp = jnp.where(half, dts[1], dts[0])
            xdt = x * dtp
            yoff = _dot_nt(cm, hp) * ecol
            dye = dyp * ecol
            dcm = dcm + _dot(dye, hp)
            dhp = _dot_tn(dye, cm)
            gmat = _dot_nt(bm, dhn)
            dxdt = dxdt + dec * gmat
            dbm = dbm + _dot(xdt * dec, dhn)
            r_off = dyp * yoff
            r_dec = xdt * gmat * dec
            r_x = dxdt * x
            hh = dhn * hp
            for e in range(2):
                sel = half == (e == 1)
                h = hs[e]
                s_off = jnp.sum(jnp.where(sel, r_off, 0.0), axis=1, keepdims=True)
                s_dec = jnp.sum(jnp.where(sel, r_dec, 0.0), axis=1, keepdims=True)
                dacs = dacs + jnp.where(lane == h, s_off - s_dec, 0.0)
                dcd = jnp.sum(jnp.sum(jnp.where(sub_half == (e == 1), hh, 0.0), axis=1, keepdims=True), axis=0, keepdims=True)
                tot_e = jnp.sum(s_dec, axis=0, keepdims=True) + jnp.exp(tots[e]) * dcd
                dtot = dtot + jnp.where(lane[0:1] == h, tot_e, 0.0)
                ddt_x = ddt_x + jnp.where(lane == h, jnp.sum(jnp.where(sel, r_x, 0.0), axis=1, keepdims=True), 0.0)
            dx_ref[:, 128 * j:128 * (j + 1)] = dxdt * dtp
            dh_scr[j] = cd * dhn + dhp
        dcm = dcm + _dot(dcb, bm)
        dbm = dbm + _dot_tn(dcb, cm)
        db_ref[...] = dbm
        dc_ref[...] = dcm
        dacs = dacs + dacs_t.T
        da = _dot_exact(tri.T, dacs) + dtot
        ddt_ref[0] = da * a_row + ddt_x
        da_ref[0] += jnp.sum(da * dt_blk, axis=0, keepdims=True)

    return pl.pallas_call(
        body, name=name, grid=(SSD_GROUPS, nc),
        in_specs=[pl.BlockSpec((CHUNK, 512), lambda g, c: (cidx(c), g)),
                  pl.BlockSpec((CHUNK, 128), lambda g, c: (cidx(c), 16 + g)),
                  pl.BlockSpec((CHUNK, 128), lambda g, c: (cidx(c), 20 + g)),
                  pl.BlockSpec((CHUNK, 128), lambda g, c: (cidx(c), d_off)),
                  pl.BlockSpec((1, 128), lambda g, c: (0, d_off)),
                  pl.BlockSpec((CHUNK, 512), lambda g, c: (cidx(c), g)),
                  pl.BlockSpec((1, 4, 128, 128), lambda g, c: (cidx(c), g, 0, 0))],
        out_specs=[pl.BlockSpec((CHUNK, 512), lambda g, c: (cidx(c), g)),
                   pl.BlockSpec((CHUNK, 128), lambda g, c: (cidx(c), g)),
                   pl.BlockSpec((CHUNK, 128), lambda g, c: (cidx(c), g)),
                   pl.BlockSpec((1, CHUNK, 128), lambda g, c: (g, cidx(c), 0)),
                   pl.BlockSpec((1, 1, 128), lambda g, c: (g, 0, 0))],
        out_shape=[jax.ShapeDtypeStruct((t, D_INNER), f32), jax.ShapeDtypeStruct((t, 512), f32),
                   jax.ShapeDtypeStruct((t, 512), f32), jax.ShapeDtypeStruct((4, t, 128), f32),
                   jax.ShapeDtypeStruct((4, 1, 128), f32)],
        scratch_shapes=[pltpu.VMEM((4, 128, 128), f32), pltpu.VMEM((CHUNK, CHUNK), f32)],
        compiler_params=_cparams(("parallel", "arbitrary")))(xbc_c, xbc_c, xbc_c, dt2, a_rows, dy, hprev)


def tail_fwd(y_f, y_b, xbc_c, z, dskip_row, nw_row, tb=512):
    t = y_f.shape[0]

    def body(yf_ref, yb_ref, x_ref, z_ref, d_ref, w_ref, o_ref):
        zz = z_ref[...]
        y = (yf_ref[...] + yb_ref[...] + d_ref[...] * x_ref[...]) * (zz * _sigmoid(zz))
        rstd = lax.rsqrt(jnp.mean(y * y, axis=1, keepdims=True) + NORM_EPS)
        o_ref[...] = (y * rstd * w_ref[...]).astype(o_ref.dtype)

    blk = pl.BlockSpec((tb, 512), lambda i, g: (i, g))
    row = pl.BlockSpec((1, 512), lambda i, g: (0, g))
    return pl.pallas_call(
        body, name="tail_fwd", grid=(t // tb, SSD_GROUPS), in_specs=[blk, blk, blk, blk, row, row], out_specs=blk,
        out_shape=jax.ShapeDtypeStruct((t, D_INNER), bf16),
        compiler_params=_cparams(("parallel", "parallel")))(y_f, y_b, xbc_c, z, dskip_row, nw_row)


def tail_bwd(dyn, y_f, y_b, xbc_c, z, dskip_row, nw_row, tb=512):
    t = y_f.shape[0]

    def body(g_ref, yf_ref, yb_ref, x_ref, z_ref, d_ref, w_ref, dy_ref, dz_ref, dw_ref, dd_ref):
        zz = z_ref[...]
        sg = _sigmoid(zz)
        sl = zz * sg
        x = x_ref[...]
        y = yf_ref[...] + yb_ref[...] + d_ref[...] * x
        yz = y * sl
        rstd = lax.rsqrt(jnp.mean(yz * yz, axis=1, keepdims=True) + NORM_EPS)
        yhat = yz * rstd
        g = g_ref[...]
        dyhat = g * w_ref[...]
        dyz = rstd * (dyhat - yhat * jnp.mean(dyhat * yhat, axis=1, keepdims=True))
        dy = dyz * sl
        dy_ref[...] = dy
        dz_ref[...] = (dyz * y * sg * (1.0 + zz * (1.0 - sg))).astype(dz_ref.dtype)

        @pl.when(pl.program_id(1) == 0)
        def _():
            dw_ref[...] = jnp.zeros_like(dw_ref)
            dd_ref[...] = jnp.zeros_like(dd_ref)

        dw_ref[...] += jnp.sum(g * yhat, axis=0, keepdims=True)
        dd_ref[...] += jnp.sum(dy * x, axis=0, keepdims=True)

    blk = pl.BlockSpec((tb, 512), lambda g, i: (i, g))
    row = pl.BlockSpec((1, 512), lambda g, i: (0, g))
    return pl.pallas_call(
        body, name="tail_bwd", grid=(SSD_GROUPS, t // tb), in_specs=[blk, blk, blk, blk, blk, row, row],
        out_specs=[blk, blk, row, row],
        out_shape=[jax.ShapeDtypeStruct((t, D_INNER), f32), jax.ShapeDtypeStruct((t, D_INNER), bf16),
                   jax.ShapeDtypeStruct((1, D_INNER), f32), jax.ShapeDtypeStruct((1, D_INNER), f32)],
        compiler_params=_cparams(("parallel", "arbitrary")))(dyn, y_f, y_b, xbc_c, z, dskip_row, nw_row)


def _slopes(p):
    return [2.0 ** (-8.0 * (HEADS_PER_PATTERN * p + j + 1) / ATTN_HEADS) for j in range(HEADS_PER_PATTERN)]


def _win_specs(nq, col_of):
    return [pl.BlockSpec((64, 256), lambda r, i: (jnp.maximum(2 * i - 1, 0), col_of(r))),
            pl.BlockSpec((128, 256), lambda r, i: (i, col_of(r))),
            pl.BlockSpec((64, 256), lambda r, i: (jnp.minimum(2 * i + 2, 2 * nq - 1), col_of(r)))]


def _lane_head(shape):
    return lax.broadcasted_iota(jnp.int32, shape, 1) >> 6


def _stack_heads(m):
    lane_head = _lane_head(m.shape)
    return jnp.concatenate([jnp.where(lane_head == j, m, 0.0) for j in range(HEADS_PER_PATTERN)], axis=0)


def _unstack_heads(m4, n):
    lane_head = _lane_head((n, 256))
    out = jnp.where(lane_head == 0, m4[0:n], 0.0)
    for j in range(1, HEADS_PER_PATTERN):
        out = out + jnp.where(lane_head == j, m4[j * n:(j + 1) * n], 0.0)
    return out


def _head_cols(m, n):
    lane = lax.broadcasted_iota(jnp.int32, (n, 256), 1)
    return jnp.concatenate([jnp.sum(jnp.where(lane == ATTN_HEAD_DIM * j, m, 0.0), axis=1, keepdims=True)
                            for j in range(HEADS_PER_PATTERN)], axis=0)


def _q_scores(q, kcat, i, nq, p, dil):
    s = _dot_nt(_stack_heads(q * 0.125), kcat)
    row = lax.broadcasted_iota(jnp.int32, s.shape, 0)
    col = lax.broadcasted_iota(jnp.int32, s.shape, 1)
    rel = col - 64 - (row & 127)
    valid = (jnp.abs(rel) <= 64) & ((i > 0) | (col >= 64)) & ((i < nq - 1) | (col < 192))
    sl = _slopes(p)
    hd = row >> 7
    slope = jnp.where(hd == 0, sl[0], jnp.where(hd == 1, sl[1], jnp.where(hd == 2, sl[2], sl[3])))
    s = s - slope * (jnp.abs(rel) * dil).astype(f32)
    return jnp.where(valid, s, NEG_BIG)


def attn_fwd(q, k, v, p, dil, name):
    l = q.shape[0]
    nq = l // 128

    def body(q_ref, kp_ref, ko_ref, kn_ref, vp_ref, vo_ref, vn_ref, o_ref, lse_ref):
        i = pl.program_id(1)
        kcat = jnp.concatenate([kp_ref[...], ko_ref[...], kn_ref[...]], axis=0)
        vcat = jnp.concatenate([vp_ref[...], vo_ref[...], vn_ref[...]], axis=0)
        s = _q_scores(q_ref[...], kcat, i, nq, p, dil)
        m = jnp.max(s, axis=1, keepdims=True)
        pr = jnp.exp(s - m)
        den = jnp.sum(pr, axis=1, keepdims=True)
        o4 = _dot(pr, vcat) / den
        o_ref[...] = _unstack_heads(o4, 128)
        lse_ref[...] = _unstack_heads(jnp.broadcast_to(m + jnp.log(den), (512, 256)), 128)

    col = lambda r: r
    return pl.pallas_call(
        body, name=name, grid=(dil, nq),
        in_specs=[pl.BlockSpec((128, 256), lambda r, i: (i, r))] + _win_specs(nq, col) + _win_specs(nq, col),
        out_specs=[pl.BlockSpec((128, 256), lambda r, i: (i, r))] * 2,
        out_shape=[jax.ShapeDtypeStruct(q.shape, f32)] * 2,
        compiler_params=_cparams(("parallel", "parallel")))(q, k, k, k, v, v, v)


def attn_combine(os_, lses, tb=1024):
    t = os_[0].shape[0]

    def body(o0, o1, o2, l0, l1, l2, y_ref, lse_ref):
        a0, a1, a2 = l0[...], l1[...], l2[...]
        m = jnp.maximum(jnp.maximum(a0, a1), a2)
        e0, e1, e2 = jnp.exp(a0 - m), jnp.exp(a1 - m), jnp.exp(a2 - m)
        den = e0 + e1 + e2
        y_ref[...] = (e0 * o0[...] + e1 * o1[...] + e2 * o2[...]) / den
        lse_ref[...] = m + jnp.log(den)

    blk = pl.BlockSpec((tb, 256), lambda i: (i, 0))
    return pl.pallas_call(
        body, name="attn_combine", grid=(t // tb,), in_specs=[blk] * 6, out_specs=[blk, blk],
        out_shape=[jax.ShapeDtypeStruct((t, 256), f32)] * 2,
        compiler_params=_cparams(("parallel",)))(*os_, *lses)


def attn_delta(dy, y, tb=1024):
    t = dy.shape[0]

    def body(dy_ref, y_ref, d_ref):
        pr = dy_ref[...] * y_ref[...]
        lane_head = _lane_head(pr.shape)
        out = jnp.zeros_like(pr)
        for j in range(HEADS_PER_PATTERN):
            sj = jnp.sum(jnp.where(lane_head == j, pr, 0.0), axis=1, keepdims=True)
            out = out + jnp.where(lane_head == j, sj, 0.0)
        d_ref[...] = out

    blk = pl.BlockSpec((tb, 256), lambda i: (i, 0))
    return pl.pallas_call(body, name="attn_delta", grid=(t // tb,), in_specs=[blk, blk], out_specs=blk,
                          out_shape=jax.ShapeDtypeStruct((t, 256), f32),
                          compiler_params=_cparams(("parallel",)))(dy, y)


def attn_dq(q, k, v, dy, lse, delta, p, dil, name):
    l = q.shape[0]
    nq = l // 128

    def body(q_ref, kp_ref, ko_ref, kn_ref, vp_ref, vo_ref, vn_ref, dy_ref, lse_ref, d_ref, dq_ref):
        i = pl.program_id(1)
        kcat = jnp.concatenate([kp_ref[...], ko_ref[...], kn_ref[...]], axis=0)
        vcat = jnp.concatenate([vp_ref[...], vo_ref[...], vn_ref[...]], axis=0)
        s = _q_scores(q_ref[...], kcat, i, nq, p, dil)
        pr = jnp.exp(s - _head_cols(lse_ref[...], 128))
        dp = _dot_nt(_stack_heads(dy_ref[...]), vcat)
        ds = pr * (dp - _head_cols(d_ref[...], 128))
        dq_ref[...] = (_unstack_heads(_dot(ds, kcat), 128) * 0.125).astype(dq_ref.dtype)

    col = lambda r: r
    own = pl.BlockSpec((128, 256), lambda r, i: (i, r))
    return pl.pallas_call(
        body, name=name, grid=(dil, nq),
        in_specs=[own] + _win_specs(nq, col) + _win_specs(nq, col) + [own, own, own], out_specs=own,
        out_shape=jax.ShapeDtypeStruct(q.shape, bf16),
        compiler_params=_cparams(("parallel", "parallel")))(q, k, k, k, v, v, v, dy, lse, delta)


def attn_dkv(q, k, v, dy, lse, delta, p, dil, name):
    l = q.shape[0]
    nq = l // 128

    def body(qp_ref, qo_ref, qn_ref, gp_ref, go_ref, gn_ref, lp_ref, lo_ref, ln_ref, dp_ref, do_ref, dn_ref,
             k_ref, v_ref, dk_ref, dv_ref):
        i = pl.program_id(1)
        cat = lambda a, b, c: jnp.concatenate([a[...], b[...], c[...]], axis=0)
        q4 = _stack_heads(cat(qp_ref, qo_ref, qn_ref) * 0.125)
        dy4 = _stack_heads(cat(gp_ref, go_ref, gn_ref))
        lse4 = _head_cols(cat(lp_ref, lo_ref, ln_ref), 256)
        del4 = _head_cols(cat(dp_ref, do_ref, dn_ref), 256)
        s = _dot_nt(q4, k_ref[...])
        row = lax.broadcasted_iota(jnp.int32, s.shape, 0)
        col = lax.broadcasted_iota(jnp.int32, s.shape, 1)
        qoff = row & 255
        rel = col - (qoff - 64)
        valid = (jnp.abs(rel) <= 64) & ((i > 0) | (qoff >= 64)) & ((i < nq - 1) | (qoff < 192))
        sl = _slopes(p)
        hd = row >> 8
        slope = jnp.where(hd == 0, sl[0], jnp.where(hd == 1, sl[1], jnp.where(hd == 2, sl[2], sl[3])))
        s = s - slope * (jnp.abs(rel) * dil).astype(f32)
        pr = jnp.where(valid, jnp.exp(jnp.where(valid, s, NEG_BIG) - lse4), 0.0)
        dpm = _dot_nt(dy4, v_ref[...])
        ds = pr * (dpm - del4)
        dv_ref[...] = _dot_tn(pr, dy4).astype(dv_ref.dtype)
        dk_ref[...] = _dot_tn(ds, q4).astype(dk_ref.dtype)

    col = lambda r: r
    own = pl.BlockSpec((128, 256), lambda r, i: (i, r))
    win = _win_specs(nq, col)
    return pl.pallas_call(
        body, name=name, grid=(dil, nq), in_specs=win * 4 + [own, own], out_specs=[own, own],
        out_shape=[jax.ShapeDtypeStruct(q.shape, bf16)] * 2,
        compiler_params=_cparams(("parallel", "parallel")))(q, q, q, dy, dy, dy, lse, lse, lse, delta, delta, delta, k, v)


def _lanes(v, reps):
    return v if reps == 1 else jnp.tile(v, (1, reps))


def _lane_halo_specs(cb, tb, nt, off=0):
    r = tb // 128
    return [pl.BlockSpec((cb, 128), lambda j, i: (j + off, jnp.maximum(i * r - 1, 0))),
            pl.BlockSpec((cb, tb), lambda j, i: (j + off, i)),
            pl.BlockSpec((cb, 128), lambda j, i: (j + off, jnp.minimum((i + 1) * r, nt * r - 1)))]


def _with_lane_halo(prev_ref, own_ref, next_ref, i, nt):
    prev = jnp.where(i > 0, prev_ref[...].astype(f32), 0.0)
    nxt = jnp.where(i < nt - 1, next_ref[...].astype(f32), 0.0)
    return jnp.concatenate([prev, own_ref[...].astype(f32), nxt], axis=1)


def _lane_shifted(xcat, s, tb):
    n = xcat.shape[1]
    return pltpu.roll(xcat, (-s) % n, 1)[:, 128:128 + tb]


def conv_fwd_t(xbc_t, w_b, b_b, tb=1024, cb=256):
    c, t = xbc_t.shape
    nt = t // tb

    def body(prev_ref, own_ref, next_ref, w_ref, b_ref, o_ref):
        i = pl.program_id(1)
        xcat = _with_lane_halo(prev_ref, own_ref, next_ref, i, nt)
        reps = tb // 128
        pre = _lanes(b_ref[...], reps)
        for k in range(D_CONV):
            pre = pre + _lanes(w_ref[k], reps) * _lane_shifted(xcat, k - 2, tb)
        o_ref[...] = pre * _sigmoid(pre)

    return pl.pallas_call(
        body, name="conv_fwd", grid=(c // cb, nt),
        in_specs=_lane_halo_specs(cb, tb, nt) + [pl.BlockSpec((D_CONV, cb, 128), lambda j, i: (0, j, 0)),
                                                 pl.BlockSpec((cb, 128), lambda j, i: (j, 0))],
        out_specs=pl.BlockSpec((cb, tb), lambda j, i: (j, i)), out_shape=jax.ShapeDtypeStruct((c, t), f32),
        compiler_params=_cparams(("parallel", "parallel")))(xbc_t, xbc_t, xbc_t, w_b, b_b)


def conv_bwd_t(xbc_t, grad_t, w_b, b_b, into, name, row0, tb=1024, cb=256):
    c, t = grad_t.shape
    nt = t // tb
    off = row0 // cb
    reps = tb // 128

    def body(*refs):
        i = pl.program_id(1)
        xr, gr = refs[0:3], refs[3:6]
        w_ref, b_ref = refs[6:8]
        dx_ref, dw_ref, db_ref = refs[-3:]
        xcat = _with_lane_halo(*xr, i, nt)
        gcat = _with_lane_halo(*gr, i, nt)
        n = tb + 256
        wk = [_lanes(w_ref[k], reps + 2) for k in range(D_CONV)]
        pre = _lanes(b_ref[...], reps + 2)
        for k in range(D_CONV):
            pre = pre + wk[k] * pltpu.roll(xcat, (2 - k) % n, 1)
        sg = _sigmoid(pre)
        dpre = gcat * sg * (1.0 + pre * (1.0 - sg))
        dx = None
        for k in range(D_CONV):
            term = wk[k][:, 128:128 + tb] * _lane_shifted(dpre, 2 - k, tb)
            dx = term if dx is None else dx + term
        dx_ref[...] = dx.astype(dx_ref.dtype)
        dp_own = dpre[:, 128:128 + tb]

        def fold(v):
            s = v[:, 0:128]
            for q in range(1, reps):
                s = s + v[:, 128 * q:128 * (q + 1)]
            return s

        @pl.when(i == 0)
        def _():
            dw_ref[...] = jnp.zeros_like(dw_ref)
            db_ref[...] = jnp.zeros_like(db_ref)

        for k in range(D_CONV):
            dw_ref[k] += fold(dp_own * _lane_shifted(xcat, k - 2, tb))
        db_ref[...] += fold(dp_own)

    in_specs = (_lane_halo_specs(cb, tb, nt, off) + _lane_halo_specs(cb, tb, nt)
                + [pl.BlockSpec((D_CONV, cb, 128), lambda j, i: (0, j + off, 0)), pl.BlockSpec((cb, 128), lambda j, i: (j + off, 0))])
    args = [xbc_t] * 3 + [grad_t] * 3 + [w_b, b_b]
    aliases = {}
    if into is not None:
        in_specs.append(pl.BlockSpec(memory_space=pl.ANY))
        args.append(into)
        aliases = {len(args) - 1: 0}
    return pl.pallas_call(
        body, name=name, grid=(c // cb, nt), in_specs=in_specs,
        out_specs=[pl.BlockSpec((cb, tb), lambda j, i: (j + off, i)), pl.BlockSpec((D_CONV, cb, 128), lambda j, i: (0, j, 0)),
                   pl.BlockSpec((cb, 128), lambda j, i: (j, 0))],
        out_shape=[jax.ShapeDtypeStruct((CONV_DIM, t), bf16), jax.ShapeDtypeStruct((D_CONV, c, 128), f32),
                   jax.ShapeDtypeStruct((c, 128), f32)],
        input_output_aliases=aliases, compiler_params=_cparams(("parallel", "arbitrary")))(*args)


def dt_fwd_t(u_dt_t, bias_b, tb=2048):
    r, t = u_dt_t.shape

    def body(u_ref, b_ref, o_ref):
        v = u_ref[...] + _lanes(b_ref[...], tb // 128)
        o_ref[...] = jnp.maximum(v, 0.0) + jnp.log(1.0 + jnp.exp(-jnp.abs(v)))

    return pl.pallas_call(
        body, name="dt_fwd", grid=(t // tb,),
        in_specs=[pl.BlockSpec((r, tb), lambda i: (0, i)), pl.BlockSpec((r, 128), lambda i: (0, 0))],
        out_specs=pl.BlockSpec((r, tb), lambda i: (0, i)), out_shape=jax.ShapeDtypeStruct((r, t), f32),
        compiler_params=_cparams(("parallel",)))(u_dt_t, bias_b)


def dt_bwd_t(ddt_f, ddt_b, u_dt_t, bias_b, tb=2048):
    r, t = u_dt_t.shape
    reps = tb // 128

    def body(gf_ref, gb_ref, u_ref, b_ref, du_ref, db_ref):
        g = jnp.concatenate([gf_ref[...], gb_ref[...]], axis=0)
        du = g * _sigmoid(u_ref[...] + _lanes(b_ref[...], reps))
        du_ref[...] = du.astype(du_ref.dtype)

        @pl.when(pl.program_id(0) == 0)
        def _():
            db_ref[...] = jnp.zeros_like(db_ref)

        s = du[:, 0:128]
        for q in range(1, reps):
            s = s + du[:, 128 * q:128 * (q + 1)]
        db_ref[...] += s

    half = pl.BlockSpec((r // 2, tb), lambda i: (0, i))
    return pl.pallas_call(
        body, name="dt_bwd", grid=(t // tb,),
        in_specs=[half, half, pl.BlockSpec((r, tb), lambda i: (0, i)), pl.BlockSpec((r, 128), lambda i: (0, 0))],
        out_specs=[pl.BlockSpec((r, tb), lambda i: (0, i)), pl.BlockSpec((r, 128), lambda i: (0, 0))],
        out_shape=[jax.ShapeDtypeStruct((r, t), bf16), jax.ShapeDtypeStruct((r, 128), f32)],
        compiler_params=_cparams(("arbitrary",)))(ddt_f, ddt_b, u_dt_t, bias_b)


HEADS_PER_GROUP = SSD_HEADS // SSD_GROUPS


def _ssd_decays(dt_blk, a_blk, reverse):
    row = lax.broadcasted_iota(jnp.int32, (CHUNK, CHUNK), 0)
    col = lax.broadcasted_iota(jnp.int32, (CHUNK, CHUNK), 1)
    mask = (row <= col) if reverse else (row >= col)
    tri = mask.astype(f32)
    a8 = dt_blk * a_blk
    a = jnp.concatenate([a8, jnp.zeros((CHUNK - HEADS_PER_GROUP, CHUNK), f32)], axis=0).T
    acs = _dot_exact(tri, a)
    return mask, tri, a8, acs, acs.T, col


def ssd_fwd_t(xbc_ct, dt_t, a_b, reverse, name):
    t = xbc_ct.shape[1]
    nc = t // CHUNK
    d_off = 4 if reverse else 0

    def cidx(c):
        return nc - 1 - c if reverse else c

    def body(x_ref, b_ref, c_ref, dt_ref, a_ref, y_ref, hp_ref, h_scr):
        c = pl.program_id(1)

        @pl.when(c == 0)
        def _():
            h_scr[...] = jnp.zeros_like(h_scr)

        dt_blk = dt_ref[...]
        mask, tri, a8, acs, acs_t, lane = _ssd_decays(dt_blk, a_ref[...], reverse)
        bm = b_ref[...].T
        cm = c_ref[...].T
        cb = _dot_nt(cm, bm)
        tot = jnp.sum(a8, axis=1, keepdims=True)
        for j in range(HEADS_PER_GROUP):
            rows = slice(SSD_HEAD_DIM * j, SSD_HEAD_DIM * (j + 1))
            col_j = _lane_col(acs, lane, j)
            row_j = acs_t[j:j + 1, :]
            lmat = jnp.where(mask, jnp.exp(jnp.where(mask, col_j - row_j, 0.0)), 0.0)
            xdt = x_ref[rows, :] * dt_blk[j:j + 1, :]
            hp = h_scr[j]
            hp_ref[0, j] = hp
            y = _dot_nt(xdt, cb * lmat) + _dot_nt(hp, cm) * jnp.exp(row_j)
            y_ref[rows, :] = y
            tot_j = tot[j:j + 1, :]
            h_scr[j] = jnp.exp(tot_j) * hp + _dot(xdt * jnp.exp(tot_j - row_j), bm)

    return pl.pallas_call(
        body, name=name, grid=(SSD_GROUPS, nc),
        in_specs=[pl.BlockSpec((512, CHUNK), lambda g, c: (g, cidx(c))),
                  pl.BlockSpec((128, CHUNK), lambda g, c: (16 + g, cidx(c))),
                  pl.BlockSpec((128, CHUNK), lambda g, c: (20 + g, cidx(c))),
                  pl.BlockSpec((HEADS_PER_GROUP, CHUNK), lambda g, c: (d_off + g, cidx(c))),
                  pl.BlockSpec((HEADS_PER_GROUP, 128), lambda g, c: (d_off + g, 0))],
        out_specs=[pl.BlockSpec((512, CHUNK), lambda g, c: (g, cidx(c))),
                   pl.BlockSpec((1, HEADS_PER_GROUP, SSD_HEAD_DIM, D_STATE), lambda g, c: (cidx(c), g, 0, 0))],
        out_shape=[jax.ShapeDtypeStruct((D_INNER, t), f32), jax.ShapeDtypeStruct((nc, SSD_HEADS, SSD_HEAD_DIM, D_STATE), f32)],
        scratch_shapes=[pltpu.VMEM((HEADS_PER_GROUP, SSD_HEAD_DIM, D_STATE), f32)],
        compiler_params=_cparams(("parallel", "arbitrary")))(xbc_ct, xbc_ct, xbc_ct, dt_t, a_b)


def ssd_bwd_t(xbc_ct, dt_t, a_b, dy_t, hprev, reverse, name, skip_b=None, prev=None):
    t = xbc_ct.shape[1]
    nc = t // CHUNK
    d_off = 4 if reverse else 0

    def cidx(c):
        return c if reverse else nc - 1 - c

    def body(*refs):
        x_ref, b_ref, c_ref, dt_ref, a_ref, dy_ref, hp_ref = refs[0:7]
        pos = 7
        skip_ref = None
        if skip_b is not None:
            skip_ref = refs[pos]
            pos += 1
        prev_refs = None
        if prev is not None:
            prev_refs = refs[pos:pos + 3]
            pos += 3
        dx_ref, db_ref, dc_ref, ddt_ref, da_ref, dh_scr = refs[pos:pos + 6]
        c = pl.program_id(1)

        @pl.when(c == 0)
        def _():
            dh_scr[...] = jnp.zeros_like(dh_scr)
            da_ref[...] = jnp.zeros_like(da_ref)

        dt_blk = dt_ref[...]
        a_blk = a_ref[...]
        mask, tri, a8, acs, acs_t, lane = _ssd_decays(dt_blk, a_blk, reverse)
        sub = lax.broadcasted_iota(jnp.int32, (CHUNK, CHUNK), 0)
        mask_t = (sub >= lane) if reverse else (sub <= lane)
        bm = b_ref[...].T
        cm = c_ref[...].T
        cb = _dot_nt(cm, bm)
        cb_t = _dot_nt(bm, cm)
        tot = jnp.sum(a8, axis=1, keepdims=True)
        dcb = jnp.zeros((CHUNK, CHUNK), f32)
        dbm = jnp.zeros((CHUNK, D_STATE), f32)
        dcm = jnp.zeros((CHUNK, D_STATE), f32)
        dacs_rows, ddtx_rows = [], []
        for j in range(HEADS_PER_GROUP):
            rows = slice(SSD_HEAD_DIM * j, SSD_HEAD_DIM * (j + 1))
            col_j = _lane_col(acs, lane, j)
            row_j = acs_t[j:j + 1, :]
            dt_j = dt_blk[j:j + 1, :]
            tot_j = tot[j:j + 1, :]
            lmat = jnp.where(mask, jnp.exp(jnp.where(mask, col_j - row_j, 0.0)), 0.0)
            lmat_t = jnp.where(mask_t, jnp.exp(jnp.where(mask_t, row_j - col_j, 0.0)), 0.0)
            x = x_ref[rows, :]
            xdt = x * dt_j
            dyh = dy_ref[rows, :]
            hp = hp_ref[0, j]
            dhn = dh_scr[j]
            ml = _dot_tn(dyh, xdt) * lmat
            w_t = _dot_tn(xdt, dyh) * lmat_t * cb_t
            dcb = dcb + ml
            dacs = jnp.sum(w_t, axis=0, keepdims=True) - jnp.sum(ml * cb, axis=0, keepdims=True)
            ecol = jnp.exp(row_j)
            dec = jnp.exp(tot_j - row_j)
            dye = dyh * ecol
            yoff = _dot_nt(hp, cm) * ecol
            gmat = _dot_nt(dhn, bm)
            dxdt = _dot(dyh, cb * lmat) + dec * gmat
            s_dec = jnp.sum(xdt * gmat, axis=0, keepdims=True) * dec
            dacs = dacs + jnp.sum(dyh * yoff, axis=0, keepdims=True) - s_dec
            dcd = jnp.sum(jnp.sum(dhn * hp, axis=1, keepdims=True), axis=0, keepdims=True)
            dtot = jnp.sum(s_dec, axis=1, keepdims=True) + jnp.exp(tot_j) * dcd
            dacs_rows.append((dacs, dtot))
            ddtx_rows.append(jnp.sum(dxdt * x, axis=0, keepdims=True))
            dcm = dcm + _dot_tn(dye, hp)
            dbm = dbm + _dot_tn(xdt * dec, dhn)
            dxh = dxdt * dt_j
            if skip_ref is not None:
                dxh = dxh + skip_ref[rows, :] * dyh
            if prev_refs is not None:
                dxh = dxh + prev_refs[0][rows, :]
            dx_ref[rows, :] = dxh
            dh_scr[j] = jnp.exp(tot_j) * dhn + _dot(dye, cm)
        dcm = dcm + _dot(dcb, bm)
        dbm = dbm + _dot_tn(dcb, cm)
        dbt, dct = dbm.T, dcm.T
        if prev_refs is not None:
            dbt = dbt + prev_refs[1][...]
            dct = dct + prev_refs[2][...]
        db_ref[...] = dbt
        dc_ref[...] = dct
        dacs8 = jnp.concatenate([d for d, _ in dacs_rows], axis=0)
        dtot8 = jnp.concatenate([d for _, d in dacs_rows], axis=0)
        da8 = _dot_exact(dacs8, tri) + dtot8
        ddt_ref[...] = da8 * a_blk + jnp.concatenate(ddtx_rows, axis=0)
        da_ref[...] += da8 * dt_blk

    grp = pl.BlockSpec((512, CHUNK), lambda g, c: (g, cidx(c)))
    st = pl.BlockSpec((128, CHUNK), lambda g, c: (g, cidx(c)))
    in_specs = [grp, pl.BlockSpec((128, CHUNK), lambda g, c: (16 + g, cidx(c))),
                pl.BlockSpec((128, CHUNK), lambda g, c: (20 + g, cidx(c))),
                pl.BlockSpec((HEADS_PER_GROUP, CHUNK), lambda g, c: (d_off + g, cidx(c))),
                pl.BlockSpec((HEADS_PER_GROUP, 128), lambda g, c: (d_off + g, 0)), grp,
                pl.BlockSpec((1, HEADS_PER_GROUP, SSD_HEAD_DIM, D_STATE), lambda g, c: (cidx(c), g, 0, 0))]
    args = [xbc_ct, xbc_ct, xbc_ct, dt_t, a_b, dy_t, hprev]
    if skip_b is not None:
        in_specs.append(pl.BlockSpec((512, 128), lambda g, c: (g, 0)))
        args.append(skip_b)
    if prev is not None:
        in_specs += [grp, st, st]
        args += list(prev)
    return pl.pallas_call(
        body, name=name, grid=(SSD_GROUPS, nc), in_specs=in_specs,
        out_specs=[grp, st, st, pl.BlockSpec((HEADS_PER_GROUP, CHUNK), lambda g, c: (g, cidx(c))),
                   pl.BlockSpec((HEADS_PER_GROUP, 128), lambda g, c: (g, 0))],
        out_shape=[jax.ShapeDtypeStruct((D_INNER, t), f32), jax.ShapeDtypeStruct((512, t), f32),
                   jax.ShapeDtypeStruct((512, t), f32), jax.ShapeDtypeStruct((SSD_HEADS, t), f32),
                   jax.ShapeDtypeStruct((SSD_HEADS, 128), f32)],
        scratch_shapes=[pltpu.VMEM((HEADS_PER_GROUP, SSD_HEAD_DIM, D_STATE), f32)],
        compiler_params=_cparams(("parallel", "arbitrary")))(*args)


def tail_fwd_t(y_f, y_b, xbc_ct, z_t, skip_b, nw_b, tb=512):
    t = y_f.shape[1]
    reps = tb // 128

    def body(yf_ref, yb_ref, x_ref, z_ref, d_ref, w_ref, o_ref):
        zz = z_ref[...]
        y = (yf_ref[...] + yb_ref[...] + _lanes(d_ref[...], reps) * x_ref[...]) * (zz * _sigmoid(zz))
        rstd = lax.rsqrt(jnp.mean(y * y, axis=0, keepdims=True) + NORM_EPS)
        o_ref[...] = (y * rstd * _lanes(w_ref[...], reps)).astype(o_ref.dtype)

    blk = pl.BlockSpec((512, tb), lambda g, i: (g, i))
    par = pl.BlockSpec((512, 128), lambda g, i: (g, 0))
    return pl.pallas_call(
        body, name="tail_fwd", grid=(SSD_GROUPS, t // tb), in_specs=[blk, blk, blk, blk, par, par], out_specs=blk,
        out_shape=jax.ShapeDtypeStruct((D_INNER, t), bf16),
        compiler_params=_cparams(("parallel", "parallel")))(y_f, y_b, xbc_ct, z_t, skip_b, nw_b)


def tail_bwd_t(dyn_t, y_f, y_b, xbc_ct, z_t, skip_b, nw_b, tb=512):
    t = y_f.shape[1]
    reps = tb // 128

    def body(g_ref, yf_ref, yb_ref, x_ref, z_ref, d_ref, w_ref, dy_ref, dz_ref, dw_ref, dd_ref):
        zz = z_ref[...]
        sg = _sigmoid(zz)
        sl = zz * sg
        x = x_ref[...]
        y = yf_ref[...] + yb_ref[...] + _lanes(d_ref[...], reps) * x
        yz = y * sl
        rstd = lax.rsqrt(jnp.mean(yz * yz, axis=0, keepdims=True) + NORM_EPS)
        yhat = yz * rstd
        g = g_ref[...]
        dyhat = g * _lanes(w_ref[...], reps)
        dyz = rstd * (dyhat - yhat * jnp.mean(dyhat * yhat, axis=0, keepdims=True))
        dy = dyz * sl
        dy_ref[...] = dy
        dz_ref[...] = (dyz * y * sg * (1.0 + zz * (1.0 - sg))).astype(dz_ref.dtype)

        def fold(v):
            s = v[:, 0:128]
            for q in range(1, reps):
                s = s + v[:, 128 * q:128 * (q + 1)]
            return s

        @pl.when(pl.program_id(1) == 0)
        def _():
            dw_ref[...] = jnp.zeros_like(dw_ref)
            dd_ref[...] = jnp.zeros_like(dd_ref)

        dw_ref[...] += fold(g * yhat)
        dd_ref[...] += fold(dy * x)

    blk = pl.BlockSpec((512, tb), lambda g, i: (g, i))
    par = pl.BlockSpec((512, 128), lambda g, i: (g, 0))
    return pl.pallas_call(
        body, name="tail_bwd", grid=(SSD_GROUPS, t // tb), in_specs=[blk, blk, blk, blk, blk, par, par],
        out_specs=[blk, blk, par, par],
        out_shape=[jax.ShapeDtypeStruct((D_INNER, t), f32), jax.ShapeDtypeStruct((D_INNER, t), bf16),
                   jax.ShapeDtypeStruct((D_INNER, 128), f32), jax.ShapeDtypeStruct((D_INNER, 128), f32)],
        compiler_params=_cparams(("parallel", "arbitrary")))(dyn_t, y_f, y_b, xbc_ct, z_t, skip_b, nw_b)


def merge_fwd(u_gate, bg_row, y_ssd, y_att, tb=512):
    t = y_ssd.shape[0]

    def body(ga_ref, gb_ref, ba_ref, bb_ref, ys_ref, ya_ref, o_ref):
        o_ref[...] = (_sigmoid(ga_ref[...] + ba_ref[...]) * ys_ref[...]
                      + _sigmoid(gb_ref[...] + bb_ref[...]) * ya_ref[...]).astype(o_ref.dtype)

    blk = pl.BlockSpec((tb, 512), lambda i, j: (i, j))
    blk2 = pl.BlockSpec((tb, 512), lambda i, j: (i, 2 + j))
    row = pl.BlockSpec((1, 512), lambda i, j: (0, j))
    row2 = pl.BlockSpec((1, 512), lambda i, j: (0, 2 + j))
    return pl.pallas_call(
        body, name="merge_fwd", grid=(t // tb, 2), in_specs=[blk, blk2, row, row2, blk, blk], out_specs=blk,
        out_shape=jax.ShapeDtypeStruct((t, D_MODEL), bf16),
        compiler_params=_cparams(("parallel", "parallel")))(u_gate, u_gate, bg_row, bg_row, y_ssd, y_att)


def merge_bwd(dm, u_gate, bg_row, y_ssd, y_att, tb=512):
    t = dm.shape[0]

    def body(dm_ref, ga_ref, gb_ref, ba_ref, bb_ref, ys_ref, ya_ref, dys_ref, dya_ref, dga_ref, dgb_ref, dba_ref, dbb_ref):
        d = dm_ref[...]
        sa = _sigmoid(ga_ref[...] + ba_ref[...])
        sb = _sigmoid(gb_ref[...] + bb_ref[...])
        dys_ref[...] = (d * sa).astype(dys_ref.dtype)
        dya_ref[...] = (d * sb).astype(dya_ref.dtype)
        dla = d * ys_ref[...] * sa * (1.0 - sa)
        dlb = d * ya_ref[...] * sb * (1.0 - sb)
        dga_ref[...] = dla.astype(dga_ref.dtype)
        dgb_ref[...] = dlb.astype(dgb_ref.dtype)

        @pl.when(pl.program_id(1) == 0)
        def _():
            dba_ref[...] = jnp.zeros_like(dba_ref)
            dbb_ref[...] = jnp.zeros_like(dbb_ref)

        dba_ref[...] += jnp.sum(dla, axis=0, keepdims=True)
        dbb_ref[...] += jnp.sum(dlb, axis=0, keepdims=True)

    blk = pl.BlockSpec((tb, 512), lambda j, i: (i, j))
    blk2 = pl.BlockSpec((tb, 512), lambda j, i: (i, 2 + j))
    row = pl.BlockSpec((1, 512), lambda j, i: (0, j))
    row2 = pl.BlockSpec((1, 512), lambda j, i: (0, 2 + j))
    act = jax.ShapeDtypeStruct((t, D_MODEL), bf16)
    vec = jax.ShapeDtypeStruct((1, D_MODEL), f32)
    return pl.pallas_call(
        body, name="merge_bwd", grid=(2, t // tb), in_specs=[blk, blk, blk2, row, row2, blk, blk],
        out_specs=[blk, blk, blk, blk, row, row], out_shape=[act, act, act, act, vec, vec],
        compiler_params=_cparams(("parallel", "arbitrary")))(dm, u_gate, u_gate, bg_row, bg_row, y_ssd, y_att)


def _ln_stats(r):
    mu = jnp.mean(r, axis=1, keepdims=True)
    xc = r - mu
    rstd = lax.rsqrt(jnp.mean(xc * xc, axis=1, keepdims=True) + NORM_EPS)
    return xc * rstd, rstd


def _ln_bwd(dy, xhat, rstd, g_row):
    dxh = dy * g_row
    return rstd * (dxh - jnp.mean(dxh, axis=1, keepdims=True) - xhat * jnp.mean(dxh * xhat, axis=1, keepdims=True))


def ln1_fwd(x, mix, g_row, b_row, tb=512):
    t = x.shape[0]

    def body(x_ref, m_ref, g_ref, b_ref, o_ref, ob_ref):
        xhat, _ = _ln_stats(ALPHA * x_ref[...] + m_ref[...])
        h = xhat * g_ref[...] + b_ref[...]
        o_ref[...] = h
        ob_ref[...] = h.astype(ob_ref.dtype)

    blk = pl.BlockSpec((tb, D_MODEL), lambda i: (i, 0))
    row = pl.BlockSpec((1, D_MODEL), lambda i: (0, 0))
    return pl.pallas_call(body, name="ln1_fwd", grid=(t // tb,), in_specs=[blk, blk, row, row], out_specs=[blk, blk],
                          out_shape=[jax.ShapeDtypeStruct((t, D_MODEL), f32), jax.ShapeDtypeStruct((t, D_MODEL), bf16)],
                          compiler_params=_cparams(("parallel",)))(x, mix, g_row, b_row)


def ln1_bwd(dh, x, mix, g_row, tb=512):
    t = x.shape[0]

    def body(dh_ref, x_ref, m_ref, g_ref, dr_ref, drb_ref, dg_ref, db_ref):
        xhat, rstd = _ln_stats(ALPHA * x_ref[...] + m_ref[...])
        dy = dh_ref[...]
        dr = _ln_bwd(dy, xhat, rstd, g_ref[...])
        dr_ref[...] = dr
        drb_ref[...] = dr.astype(drb_ref.dtype)

        @pl.when(pl.program_id(0) == 0)
        def _():
            dg_ref[...] = jnp.zeros_like(dg_ref)
            db_ref[...] = jnp.zeros_like(db_ref)

        dg_ref[...] += jnp.sum(dy * xhat, axis=0, keepdims=True)
        db_ref[...] += jnp.sum(dy, axis=0, keepdims=True)

    blk = pl.BlockSpec((tb, D_MODEL), lambda i: (i, 0))
    row = pl.BlockSpec((1, D_MODEL), lambda i: (0, 0))
    return pl.pallas_call(
        body, name="ln1_bwd", grid=(t // tb,), in_specs=[blk, blk, blk, row], out_specs=[blk, blk, row, row],
        out_shape=[jax.ShapeDtypeStruct((t, D_MODEL), f32), jax.ShapeDtypeStruct((t, D_MODEL), bf16),
                   jax.ShapeDtypeStruct((1, D_MODEL), f32), jax.ShapeDtypeStruct((1, D_MODEL), f32)],
        compiler_params=_cparams(("arbitrary",)))(dh, x, mix, g_row)


def ln2_loss(h1, f, g_row, b_row, target, tb=512):
    t = h1.shape[0]

    def body(h_ref, f_ref, g_ref, b_ref, t_ref, dr_ref, drb_ref, dg_ref, db_ref, loss_ref):
        xhat, rstd = _ln_stats(ALPHA * h_ref[...] + f_ref[...])
        g = g_ref[...]
        err = xhat * g + b_ref[...] - t_ref[...]
        dy = err * (1.0 / D_MODEL)
        dr = _ln_bwd(dy, xhat, rstd, g)
        dr_ref[...] = dr
        drb_ref[...] = dr.astype(drb_ref.dtype)

        @pl.when(pl.program_id(0) == 0)
        def _():
            dg_ref[...] = jnp.zeros_like(dg_ref)
            db_ref[...] = jnp.zeros_like(db_ref)
            loss_ref[...] = jnp.zeros_like(loss_ref)

        dg_ref[...] += jnp.sum(dy * xhat, axis=0, keepdims=True)
        db_ref[...] += jnp.sum(dy, axis=0, keepdims=True)
        part = jnp.sum(jnp.mean(err * err, axis=1, keepdims=True), axis=0, keepdims=True)
        loss_ref[...] += 0.5 * part

    blk = pl.BlockSpec((tb, D_MODEL), lambda i: (i, 0))
    row = pl.BlockSpec((1, D_MODEL), lambda i: (0, 0))
    return pl.pallas_call(
        body, name="ln2_loss", grid=(t // tb,), in_specs=[blk, blk, row, row, blk],
        out_specs=[blk, blk, row, row, pl.BlockSpec((8, 128), lambda i: (0, 0))],
        out_shape=[jax.ShapeDtypeStruct((t, D_MODEL), f32), jax.ShapeDtypeStruct((t, D_MODEL), bf16),
                   jax.ShapeDtypeStruct((1, D_MODEL), f32), jax.ShapeDtypeStruct((1, D_MODEL), f32),
                   jax.ShapeDtypeStruct((8, 128), f32)],
        compiler_params=_cparams(("arbitrary",)))(h1, f, g_row, b_row, target)


TAIL_BLOCK, TAIL_AT = divmod(OFF_TAIL, PACK_TILE)


def _sum4(ref):
    return ((ref[0].astype(f32) + ref[1].astype(f32)) + ref[2].astype(f32)) + ref[3].astype(f32)


def adamw(parts, tails, w, m, v):
    rows = w.shape[0]
    c1 = 1.0 - ADAM_B1 ** ADAM_STEP
    c2 = 1.0 - ADAM_B2 ** ADAM_STEP

    def body(p_ref, t_ref, w_ref, m_ref, v_ref, g_ref, d_ref, nm_ref, nv_ref):
        g = _sum4(p_ref)
        with_tail = jnp.concatenate([g[0:TAIL_AT], _sum4(t_ref), g[TAIL_AT + ROWS_TAIL:]], axis=0)
        g = jnp.where(pl.program_id(0) == TAIL_BLOCK, with_tail, g)
        nm = ADAM_B1 * m_ref[...] + (1.0 - ADAM_B1) * g
        nv = ADAM_B2 * v_ref[...] + (1.0 - ADAM_B2) * (g * g)
        g_ref[...] = g
        nm_ref[...] = nm
        nv_ref[...] = nv
        d_ref[...] = -ADAM_LR * ((nm / c1) / (jnp.sqrt(nv / c2) + ADAM_EPS) + ADAM_WD * w_ref[...])

    blk = pl.BlockSpec((PACK_TILE, 1024), lambda i: (i, 0))
    out = jax.ShapeDtypeStruct((rows, 1024), f32)
    return pl.pallas_call(
        body, name="adamw", grid=(rows // PACK_TILE,),
        in_specs=[pl.BlockSpec((4, PACK_TILE, 1024), lambda i: (0, i, 0)),
                  pl.BlockSpec((4, ROWS_TAIL, 1024), lambda i: (0, 0, 0)), blk, blk, blk], out_specs=[blk] * 4,
        out_shape=[out] * 4, compiler_params=_cparams(("parallel",)))(parts, tails, w, m, v)


def pair_sum(parts, recv, core):
    rows = parts.shape[1]

    def body(c_ref, a_ref, b_ref, o_ref, t_ref):
        s = a_ref[...] + b_ref[...]
        o_ref[...] = s.astype(o_ref.dtype)

        @pl.when(pl.program_id(1) == TAIL_BLOCK)
        def _():
            t_ref[...] = s[:, TAIL_AT:TAIL_AT + ROWS_TAIL]

    grid_spec = pltpu.PrefetchScalarGridSpec(
        num_scalar_prefetch=1, grid=(4, rows // PACK_TILE),
        in_specs=[pl.BlockSpec((1, PACK_TILE, 1024), lambda j, i, c_ref: (2 * j + c_ref[0], i, 0)),
                  pl.BlockSpec((1, PACK_TILE, 1024), lambda j, i, c_ref: (j, i, 0))],
        out_specs=[pl.BlockSpec((1, PACK_TILE, 1024), lambda j, i, c_ref: (j, i, 0)),
                   pl.BlockSpec((1, ROWS_TAIL, 1024), lambda j, i, c_ref: (j, 0, 0))])
    return pl.pallas_call(
        body, name="pair_sum", grid_spec=grid_spec,
        out_shape=[jax.ShapeDtypeStruct(recv.shape, bf16), jax.ShapeDtypeStruct((4, ROWS_TAIL, 1024), f32)],
        compiler_params=_cparams(("parallel", "arbitrary")))(core, parts, recv)


def _place():
    return lax.axis_index("x"), lax.axis_index("y"), lax.axis_index("c")


def all_gather_blocks(shard):
    rows, cols = shard.shape

    def body(x_ref, out_ref, send_sems, recv_sems, local_sem):
        x, y, c = _place()
        me, sibling = (x, y, c), (x, y, 1 - c)
        chips = [(1 - x, y), (x, 1 - y), (1 - x, 1 - y)]

        def slot(px, py, pc):
            return out_ref.at[4 * px + 2 * py + pc]

        def copy(k, block, to, src=None):
            return pltpu.make_async_remote_copy(
                src_ref=slot(*block) if src is None else src, dst_ref=slot(*block), send_sem=send_sems.at[k],
                recv_sem=recv_sems.at[k], device_id=to, device_id_type=MESH)

        mine = pltpu.make_async_copy(x_ref, slot(*me), local_sem)
        mine.start()
        first = [copy(0, me, sibling, src=x_ref)]
        first += [copy(1 + j, me, (*chip, c), src=x_ref) for j, chip in enumerate(chips)]
        for cp in first:
            cp.start()
        passed = [copy(4 + j, (*chip, c), sibling) for j, chip in enumerate(chips)]
        for j, chip in enumerate(chips):
            copy(1 + j, (*chip, c), me).wait_recv()
            passed[j].start()
        copy(0, sibling, me).wait_recv()
        for j, chip in enumerate(chips):
            copy(4 + j, (*chip, 1 - c), me).wait_recv()
        for cp in first + passed:
            cp.wait_send()
        mine.wait()

    return pl.pallas_call(
        body, name="all_gather_blocks", out_shape=jax.ShapeDtypeStruct((N_DEV, rows, cols), shard.dtype),
        in_specs=[pl.BlockSpec(memory_space=pl.ANY)], out_specs=pl.BlockSpec(memory_space=pl.ANY),
        scratch_shapes=[pltpu.SemaphoreType.DMA((7,)), pltpu.SemaphoreType.DMA((7,)), pltpu.SemaphoreType.DMA],
        compiler_params=pltpu.CompilerParams(has_side_effects=True))(shard)


def pair_exchange(parts):
    _, rows, cols = parts.shape

    def body(p_ref, recv_ref, send_sems, recv_sems):
        x, y, c = _place()
        copies = [pltpu.make_async_remote_copy(
            src_ref=p_ref.at[2 * j + 1 - c], dst_ref=recv_ref.at[j], send_sem=send_sems.at[j], recv_sem=recv_sems.at[j],
            device_id=(x, y, 1 - c), device_id_type=MESH) for j in range(4)]
        for cp in copies:
            cp.start()
        for cp in copies:
            cp.wait_recv()
        for cp in copies:
            cp.wait_send()

    return pl.pallas_call(
        body, name="pair_exchange", out_shape=jax.ShapeDtypeStruct((4, rows, cols), parts.dtype),
        in_specs=[pl.BlockSpec(memory_space=pl.ANY)], out_specs=pl.BlockSpec(memory_space=pl.ANY),
        scratch_shapes=[pltpu.SemaphoreType.DMA((4,)), pltpu.SemaphoreType.DMA((4,))],
        compiler_params=pltpu.CompilerParams(has_side_effects=True))(parts)


def chip_exchange(parts):
    n = len(parts)

    def body(*refs):
        p_refs, out_refs = refs[0:n], refs[n:2 * n]
        send_sems, recv_sems, local_sems = refs[2 * n:]
        x, y, c = _place()
        mine = 2 * x + y
        flips = [(x, 1 - y), (1 - x, y), (1 - x, 1 - y)]

        def copy(a, k, src_slot, dst_slot):
            px, py = flips[k]
            return pltpu.make_async_remote_copy(
                src_ref=p_refs[a].at[src_slot], dst_ref=out_refs[a].at[dst_slot], send_sem=send_sems.at[3 * a + k],
                recv_sem=recv_sems.at[3 * a + k], device_id=(px, py, c), device_id_type=MESH)

        local = [pltpu.make_async_copy(p_refs[a].at[mine], out_refs[a].at[mine], local_sems.at[a]) for a in range(n)]
        sends = [copy(a, k, 2 * flips[k][0] + flips[k][1], mine) for a in range(n) for k in range(3)]
        for cp in local + sends:
            cp.start()
        for a in range(n):
            for k in range(3):
                copy(a, k, mine, 2 * flips[k][0] + flips[k][1]).wait_recv()
        for cp in sends:
            cp.wait_send()
        for cp in local:
            cp.wait()

    return pl.pallas_call(
        body, name="chip_exchange", out_shape=[jax.ShapeDtypeStruct(p.shape, p.dtype) for p in parts],
        in_specs=[pl.BlockSpec(memory_space=pl.ANY)] * n, out_specs=[pl.BlockSpec(memory_space=pl.ANY)] * n,
        scratch_shapes=[pltpu.SemaphoreType.DMA((3 * n,)), pltpu.SemaphoreType.DMA((3 * n,)), pltpu.SemaphoreType.DMA((n,))],
        compiler_params=pltpu.CompilerParams(has_side_effects=True))(*parts)


def _tail_rows(conv_part, small, extra):
    lead = conv_part.shape[:-1]
    rep = jnp.concatenate([small[n].reshape(-1).astype(f32) for n in SMALL] + [extra.reshape(1).astype(f32)])
    flat = jnp.concatenate([conv_part, jnp.broadcast_to(rep, lead + rep.shape),
                            jnp.zeros(lead + (ROWS_TAIL * 1024 - TAIL_ELEMS,), f32)], axis=-1)
    return flat.reshape(lead + (ROWS_TAIL, 1024))


def _pack_rows(w_in_t, w_ps, w_out, w_up_t, w_down, w_pa_t, tail):
    lead = tail.shape[:-2]
    zeros = lambda r: jnp.zeros(lead + (r, 1024), f32)
    return jnp.concatenate([w_in_t, zeros(ROWS_IN - IN_SHARD), w_ps, w_out, w_up_t, w_down,
                            w_pa_t.reshape(lead + (ROWS_PA, 1024)), tail,
                            zeros(PACK_ROWS - OFF_TAIL - ROWS_TAIL)], axis=-2)


def _pack_shard(vals):
    tail = _tail_rows(vals["conv_w"].reshape(-1), vals, jnp.zeros((), f32))
    return _pack_rows(vals["w_in"].T, vals["w_proj_ssd"], vals["w_out"], vals["w_up"].T, vals["w_down"],
                      vals["w_proj_attn"].T, tail)


def _unpack_shard(packed):
    out = {"w_in": packed[0:IN_SHARD].T, "w_proj_ssd": packed[OFF_PS:OFF_OUT], "w_out": packed[OFF_OUT:OFF_UP],
           "w_up": packed[OFF_UP:OFF_DOWN].T, "w_down": packed[OFF_DOWN:OFF_PA],
           "w_proj_attn": packed[OFF_PA:OFF_TAIL].reshape(D_MODEL // N_DEV, ATTN_OUT).T}
    flat = packed[OFF_TAIL:OFF_TAIL + ROWS_TAIL].reshape(-1)
    out["conv_w"] = flat[0:CONV_SHARD].reshape(D_CONV, CONV_DIM // N_DEV)
    off = CONV_SHARD
    for n in SMALL:
        out[n] = flat[off:off + SMALL_SIZES[n]]
        off += SMALL_SIZES[n]
    out["_extra"] = flat[off]
    return out


def _pack_parts(full, small, extra):
    conv = full["conv_w"].reshape(D_CONV, N_DEV, CONV_DIM // N_DEV).transpose(1, 0, 2).reshape(N_DEV, CONV_SHARD)
    blocks = lambda g: g.reshape(N_DEV, g.shape[0] // N_DEV, g.shape[1])
    return _pack_rows(blocks(full["w_in_t"]), blocks(full["w_proj_ssd"]), blocks(full["w_out"]), blocks(full["w_up_t"]),
                      blocks(full["w_down"]), blocks(full["w_proj_attn_t"]), _tail_rows(conv, small, extra))


def _gather_weights(w):
    conv_bits = lax.bitcast_convert_type(w["conv_w"], bf16).reshape(-1)
    conv_rows = jnp.concatenate([conv_bits, jnp.zeros((16 * 1024 - 2 * CONV_SHARD,), bf16)]).reshape(16, 1024)
    big = _pack_shard(w)[0:OFF_TAIL].astype(bf16)
    got = all_gather_blocks(jnp.concatenate([big, conv_rows], axis=0))
    whole = lambda lo, hi: got[:, lo:hi].reshape(N_DEV * (hi - lo), 1024)
    conv = lax.bitcast_convert_type(got[:, OFF_TAIL:OFF_TAIL + 4].reshape(N_DEV, 4096)[:, 0:2 * CONV_SHARD]
                                    .reshape(N_DEV, D_CONV, CONV_DIM // N_DEV, 2), f32)
    return {"w_in_t": whole(0, IN_SHARD), "w_proj_ssd": whole(OFF_PS, OFF_OUT), "w_out": whole(OFF_OUT, OFF_UP),
            "w_up_t": whole(OFF_UP, OFF_DOWN), "w_down": whole(OFF_DOWN, OFF_PA),
            "w_proj_attn_t": got[:, OFF_PA:OFF_TAIL].reshape(D_MODEL, ATTN_OUT),
            "conv_w": conv.transpose(1, 0, 2).reshape(D_CONV, CONV_DIM)}


def _row(v, width=None):
    v = v.reshape(1, -1).astype(f32)
    return v if width is None else jnp.pad(v, ((0, 0), (0, width - v.shape[1])))


def _lanes256(vf, vb):
    z = jnp.zeros((96,), f32)
    return jnp.concatenate([vf.astype(f32), z, vb.astype(f32), z]).reshape(1, 256)


def _local_step(x2, tgt, wf, p):
    t = x2.shape[0]
    o = np.cumsum((0,) + IN_SPLITS)
    wt = wf["w_in_t"]
    wt_z, wt_xbc, wt_dt = wt[o[0]:o[1]], wt[o[1]:o[2]], wt[o[2]:o[4]]
    wt_qkv, wt_gate = wt[o[4]:o[7]], wt[o[7]:o[8]]

    spread = lambda v: jnp.broadcast_to(v.astype(f32)[..., None], v.shape + (128,))
    conv_w_b, conv_b_b = spread(wf["conv_w"]), spread(p["conv_b"])
    dt_bias_b = spread(jnp.concatenate([p["dt_bias_f"], p["dt_bias_b"]]))
    a_f, a_b = -jnp.exp(p["a_log_f"].astype(f32)), -jnp.exp(p["a_log_b"].astype(f32))
    a_coef_b = spread(jnp.concatenate([a_f, a_b]))
    skip_b = spread(jnp.repeat(p["d_skip"], SSD_HEAD_DIM))
    nw_b, bg_row = spread(p["ssd_norm_w"]), _row(p["b_gate"])
    g1, b1, g2, b2 = _row(p["ln1_g"]), _row(p["ln1_b"]), _row(p["ln2_g"]), _row(p["ln2_b"])

    xb = x2.astype(MXU_DTYPE)
    xt = xb.T
    u_z = mm_nn(wt_z, xt, "in_z")
    u_xbc = mm_nn(wt_xbc, xt, "in_xbc")
    u_dt = mm_nn(wt_dt, xt, "in_dt")
    u_qkv = mm_nt(xb, wt_qkv, "in_qkv")
    u_gate = mm_nt(xb, wt_gate, "in_gate")
    xbc_c = conv_fwd_t(u_xbc, conv_w_b, conv_b_b)
    dt_t = dt_fwd_t(u_dt, dt_bias_b)
    y_f, h_f = ssd_fwd_t(xbc_c, dt_t, a_coef_b, False, "ssd_fwd_f")
    y_b, h_b = ssd_fwd_t(xbc_c, dt_t, a_coef_b, True, "ssd_fwd_b")
    yn = tail_fwd_t(y_f, y_b, xbc_c, u_z, skip_b, nw_b)
    y_ssd = mm_tn(yn, wf["w_proj_ssd"], "proj_ssd")

    def strided(a, dil):
        return a.reshape(t // dil, dil * 256)

    qkv, outs, lses = [], [], []
    for pi, (_, dil) in enumerate(DIL_PATTERNS):
        q, k, v = (strided(u_qkv[:, ATTN_WIDTH * s + 256 * pi: ATTN_WIDTH * s + 256 * (pi + 1)], dil) for s in range(3))
        qkv.append((q, k, v))
        op, lp = attn_fwd(q, k, v, pi, dil, f"attn_fwd_{pi}")
        outs.append(op.reshape(t, 256))
        lses.append(lp.reshape(t, 256))
    ya, lse = attn_combine(outs, lses)
    y_att = mm_nt(ya, wf["w_proj_attn_t"], "proj_attn")
    m = merge_fwd(u_gate, bg_row, y_ssd, y_att)
    mix = mm_nn(m, wf["w_out"], "out_proj")
    h1, h1b = ln1_fwd(x2, mix, g1, b1)
    a_up, p_act = mm_nt(h1b, wf["w_up_t"], "mlp_up", relu2=True)
    f_dn = mm_nn(p_act, wf["w_down"], "mlp_down")
    dr2, dr2b, dg2, db2, loss8 = ln2_loss(h1, f_dn, g2, b2, tgt)

    full, small = {}, {}
    da = mm_nt(dr2b, wf["w_down"], "d_mlp_act", out_dtype=bf16, relu2_of=a_up)
    full["w_down"] = mm_tn(p_act, dr2b, "dw_down")
    full["w_up_t"] = mm_tn(da, h1b, "dw_up")
    dh1 = mm_nn(da, wf["w_up_t"], "d_h1", acc_in=dr2, acc_scale=ALPHA)
    dr1, dr1b, dg1, db1 = ln1_bwd(dh1, x2, mix, g1)
    dm = mm_nt(dr1b, wf["w_out"], "d_merge")
    full["w_out"] = mm_tn(m, dr1b, "dw_out")
    dys, dya_p, dga, dgb, dba, dbb = merge_bwd(dm, u_gate, bg_row, y_ssd, y_att)
    dyn = mm_nt(wf["w_proj_ssd"], dys, "d_yn")
    full["w_proj_ssd"] = mm_nn(yn, dys, "dw_proj_ssd")
    dya = mm_nn(dya_p, wf["w_proj_attn_t"], "d_ya")
    full["w_proj_attn_t"] = mm_tn(dya_p, ya, "dw_proj_attn")

    dy, dz, dnw, ddx = tail_bwd_t(dyn, y_f, y_b, xbc_c, u_z, skip_b, nw_b)
    dxf, dbf, dcf, ddtf, daf = ssd_bwd_t(xbc_c, dt_t, a_coef_b, dy, h_f, False, "ssd_bwd_f", skip_b=skip_b)
    dxs, dbs, dcs, ddtb, dab = ssd_bwd_t(xbc_c, dt_t, a_coef_b, dy, h_b, True, "ssd_bwd_b", prev=(dxf, dbf, dcf))
    du_xbc, dcw_x, dcb_x = conv_bwd_t(u_xbc, dxs, conv_w_b, conv_b_b, None, "conv_bwd_x", 0)
    du_xbc, dcw_b, dcb_b = conv_bwd_t(u_xbc, dbs, conv_w_b, conv_b_b, du_xbc, "conv_bwd_b", D_INNER)
    du_xbc, dcw_c, dcb_c = conv_bwd_t(u_xbc, dcs, conv_w_b, conv_b_b, du_xbc, "conv_bwd_c", D_INNER + 512)
    du_dt, dbias = dt_bwd_t(ddtf, ddtb, u_dt, dt_bias_b)

    delta = attn_delta(dya, ya)
    dqs, dks, dvs = [], [], []
    for pi, (_, dil) in enumerate(DIL_PATTERNS):
        q, k, v = qkv[pi]
        sd, sl_, sdel = strided(dya, dil), strided(lse, dil), strided(delta, dil)
        dqs.append(attn_dq(q, k, v, sd, sl_, sdel, pi, dil, f"attn_dq_{pi}").reshape(t, 256))
        dk, dv = attn_dkv(q, k, v, sd, sl_, sdel, pi, dil, f"attn_dkv_{pi}")
        dks.append(dk.reshape(t, 256))
        dvs.append(dv.reshape(t, 256))
    du_qkv = jnp.concatenate(dqs + dks + dvs, axis=1)
    du_gate = jnp.concatenate([dga, dgb], axis=1)

    dx = mm_tn(dz, wt_z, "dx_z", acc_in=dr1, acc_scale=ALPHA)
    dx = mm_tn(du_xbc, wt_xbc, "dx_xbc", acc_in=dx)
    dx = mm_tn(du_dt, wt_dt, "dx_dt", acc_in=dx)
    dx = mm_nn(du_qkv, wt_qkv, "dx_qkv", acc_in=dx)
    dx = mm_nn(du_gate, wt_gate, "dx_gate", acc_in=dx)
    full["w_in_t"] = jnp.concatenate(
        [mm_nn(dz, xb, "dw_in_z"), mm_nn(du_xbc, xb, "dw_in_xbc"), mm_nn(du_dt, xb, "dw_in_dt"),
         mm_tn(du_qkv, xb, "dw_in_qkv"), mm_tn(du_gate, xb, "dw_in_gate")], axis=0)
    lanes = lambda v: jnp.sum(v, axis=-1)
    full["conv_w"] = jnp.concatenate([lanes(dcw_x), lanes(dcw_b), lanes(dcw_c)], axis=1)

    small["b_gate"] = jnp.concatenate([dba, dbb], axis=1)
    small["conv_b"] = jnp.concatenate([lanes(dcb_x), lanes(dcb_b), lanes(dcb_c)])
    dbias = lanes(dbias)
    small["dt_bias_f"], small["dt_bias_b"] = dbias[0:32], dbias[32:64]
    small["a_log_f"] = lanes(daf) * a_f
    small["a_log_b"] = lanes(dab) * a_b
    small["d_skip"] = jnp.sum(lanes(ddx).reshape(SSD_HEADS, SSD_HEAD_DIM), axis=1)
    small["ssd_norm_w"] = lanes(dnw)
    small["ln1_g"], small["ln1_b"], small["ln2_g"], small["ln2_b"] = dg1, db1, dg2, db2
    return loss8[0, 0], dx, full, small


def kernel(x, w_in, b_gate, conv_w, conv_b, dt_bias_f, dt_bias_b, a_log_f, a_log_b, d_skip, ssd_norm_w, w_proj_ssd, w_proj_attn, w_out, ln1_g, ln1_b, w_up, w_down, ln2_g, ln2_b, loss_target, m_w_in, m_b_gate, m_conv_w, m_conv_b, m_dt_bias_f, m_dt_bias_b, m_a_log_f, m_a_log_b, m_d_skip, m_ssd_norm_w, m_w_proj_ssd, m_w_proj_attn, m_w_out, m_ln1_g, m_ln1_b, m_w_up, m_w_down, m_ln2_g, m_ln2_b, v_w_in, v_b_gate, v_conv_w, v_conv_b, v_dt_bias_f, v_dt_bias_b, v_a_log_f, v_a_log_b, v_d_skip, v_ssd_norm_w, v_w_proj_ssd, v_w_proj_attn, v_w_out, v_ln1_g, v_ln1_b, v_w_up, v_w_down, v_ln2_g, v_ln2_b):
    given = dict(locals())
    w = {n: given[n] for n in WEIGHTS}
    mom = {n: given["m_" + n] for n in WEIGHTS}
    var = {n: given["v_" + n] for n in WEIGHTS}
    t = x.shape[1]
    wf = _gather_weights(w)
    loss, dx, full, small = _local_step(x.reshape(t, D_MODEL), loss_target.reshape(t, D_MODEL), wf, w)
    packed = _pack_parts(full, small, loss)
    core = lax.axis_index("c").astype(jnp.int32).reshape(1)
    parts, tails = chip_exchange(pair_sum(packed, pair_exchange(packed), core))
    g, delta, new_m, new_v = (_unpack_shard(a) for a in
                              adamw(parts, tails, _pack_shard(w), _pack_shard(mom), _pack_shard(var)))
    outs = [g["_extra"], dx.reshape(x.shape)]
    for d in (g, delta, new_m, new_v):
        outs += [d[n].reshape(w[n].shape) for n in WEIGHTS]
    return tuple(outs)
```

```python
import functools
import math

import jax
import jax.numpy as jnp
import numpy as np
from jax import lax
from jax.experimental import pallas as pl
from jax.experimental.pallas import tpu as pltpu

f32 = jnp.float32
bf16 = jnp.bfloat16
MXU_DTYPE = jnp.bfloat16

N_DEV = 8
D_MODEL = 1024
D_INNER = 2048
SSD_HEADS = 32
SSD_HEAD_DIM = 64
SSD_GROUPS = 4
D_STATE = 128
D_CONV = 5
CHUNK = 128
CONV_DIM = D_INNER + 2 * SSD_GROUPS * D_STATE
NORM_EPS = 1e-5
ATTN_HEAD_DIM = 64
DIL_PATTERNS = ((128, 1), (512, 4), (2048, 16))
HEADS_PER_PATTERN = 4
ATTN_HEADS = 12
ATTN_WIDTH = 768
ATTN_OUT = 256
D_FF = 4096
ALPHA = 2.0 ** 0.25
IN_SPLITS = (D_INNER, CONV_DIM, SSD_HEADS, SSD_HEADS, ATTN_WIDTH, ATTN_WIDTH, ATTN_WIDTH, 2 * D_MODEL)
IN_COLS = sum(IN_SPLITS)
ADAM_LR, ADAM_B1, ADAM_B2, ADAM_EPS, ADAM_WD, ADAM_STEP = 0.001, 0.9, 0.999, 1e-08, 0.01, 10
NEG_BIG = -1e30
VMEM_LIMIT = 56 * 1024 * 1024
MESH = pl.DeviceIdType.MESH

SMALL = ("b_gate", "conv_b", "dt_bias_f", "dt_bias_b", "a_log_f", "a_log_b", "d_skip", "ssd_norm_w",
         "ln1_g", "ln1_b", "ln2_g", "ln2_b")
WEIGHTS = ("w_in", "b_gate", "conv_w", "conv_b", "dt_bias_f", "dt_bias_b", "a_log_f", "a_log_b", "d_skip",
           "ssd_norm_w", "w_proj_ssd", "w_proj_attn", "w_out", "ln1_g", "ln1_b", "w_up", "w_down", "ln2_g", "ln2_b")
SMALL_SIZES = {"b_gate": 2 * D_MODEL, "conv_b": CONV_DIM, "dt_bias_f": 32, "dt_bias_b": 32, "a_log_f": 32, "a_log_b": 32,
               "d_skip": 32, "ssd_norm_w": D_INNER, "ln1_g": D_MODEL, "ln1_b": D_MODEL, "ln2_g": D_MODEL, "ln2_b": D_MODEL}
IN_SHARD = IN_COLS // N_DEV
ROWS_IN = 1200
ROWS_PS, ROWS_OUT, ROWS_UP, ROWS_DOWN, ROWS_PA = D_INNER // N_DEV, D_MODEL // N_DEV, D_FF // N_DEV, D_FF // N_DEV, 32
OFF_PS = ROWS_IN
OFF_OUT = OFF_PS + ROWS_PS
OFF_UP = OFF_OUT + ROWS_OUT
OFF_DOWN = OFF_UP + ROWS_UP
OFF_PA = OFF_DOWN + ROWS_DOWN
OFF_TAIL = OFF_PA + ROWS_PA
CONV_SHARD = D_CONV * CONV_DIM // N_DEV
TAIL_ELEMS = CONV_SHARD + sum(SMALL_SIZES.values()) + 1
ROWS_TAIL = 16
PACK_TILE = 128
PACK_ROWS = -(-(OFF_TAIL + ROWS_TAIL) // PACK_TILE) * PACK_TILE


def _cparams(sem=None, **kw):
    return pltpu.CompilerParams(dimension_semantics=sem, vmem_limit_bytes=VMEM_LIMIT, **kw)


def _mx(v):
    return v.astype(MXU_DTYPE)


def _dot(a, b):
    return jnp.dot(_mx(a), _mx(b), preferred_element_type=f32)


def _dot_nt(a, b):
    return lax.dot_general(_mx(a), _mx(b), (((1,), (1,)), ((), ())), preferred_element_type=f32)


def _dot_tn(a, b):
    return lax.dot_general(_mx(a), _mx(b), (((0,), (0,)), ((), ())), preferred_element_type=f32)


def _dot_exact(a, b):
    return jnp.dot(a, b, precision=lax.Precision.HIGHEST, preferred_element_type=f32)


def _sigmoid(v):
    return 1.0 / (1.0 + jnp.exp(-v))


def _pick(n, prefs):
    for p in prefs:
        if n % p == 0:
            return p
    return n


MM_TILE = 1024


def mm_nn(a, b, name, out_dtype=f32, acc_in=None, acc_scale=1.0):
    m, k = a.shape
    n = b.shape[1]
    tm = _pick(m, (MM_TILE, 512, 256, 128, 64))
    tn = _pick(n, (MM_TILE, 512, 256, 128))
    tk = _pick(k, (2048, 1536, 1152, 1024, 768, 512, 256, 128))
    nk = k // tk

    def body(*refs):
        a_ref, b_ref = refs[0:2]
        c_ref = refs[2] if acc_in is not None else None
        o_ref = refs[3] if acc_in is not None else refs[2]

        def finish(r):
            if acc_in is not None:
                r = r + acc_scale * c_ref[...]
            o_ref[...] = r.astype(o_ref.dtype)

        if nk == 1:
            finish(_dot(a_ref[...], b_ref[...]))
            return
        acc_ref = refs[-1]
        kk = pl.program_id(2)

        @pl.when(kk == 0)
        def _():
            acc_ref[...] = jnp.zeros_like(acc_ref)

        acc_ref[...] += _dot(a_ref[...], b_ref[...])

        @pl.when(kk == nk - 1)
        def _():
            finish(acc_ref[...])

    in_specs = [pl.BlockSpec((tm, tk), lambda i, j, kk: (i, kk)), pl.BlockSpec((tk, tn), lambda i, j, kk: (kk, j))]
    args = [a, b]
    if acc_in is not None:
        in_specs.append(pl.BlockSpec((tm, tn), lambda i, j, kk: (i, j)))
        args.append(acc_in)
    return pl.pallas_call(
        body, name=name, grid=(m // tm, n // tn, nk), in_specs=in_specs,
        out_specs=pl.BlockSpec((tm, tn), lambda i, j, kk: (i, j)),
        out_shape=jax.ShapeDtypeStruct((m, n), out_dtype),
        scratch_shapes=[pltpu.VMEM((tm, tn), f32)] if nk > 1 else [],
        compiler_params=_cparams(("parallel", "parallel", "arbitrary")))(*args)


def mm_nt(a, b, name, out_dtype=f32, relu2=None, relu2_of=None):
    m, k = a.shape
    n = b.shape[0]
    tm = MM_TILE
    tn = _pick(n, (MM_TILE, 768, 512, 256, 128))

    def body(*refs):
        r = _dot_nt(refs[0][...], refs[1][...])
        if relu2:
            refs[2][...] = r
            pos = jnp.maximum(r, 0.0)
            refs[3][...] = (pos * pos).astype(refs[3].dtype)
        elif relu2_of is not None:
            refs[3][...] = (r * (2.0 * jnp.maximum(refs[2][...], 0.0))).astype(refs[3].dtype)
        else:
            refs[2][...] = r.astype(refs[2].dtype)

    blk = pl.BlockSpec((tm, tn), lambda i, j: (i, j))
    in_specs = [pl.BlockSpec((tm, k), lambda i, j: (i, 0)), pl.BlockSpec((tn, k), lambda i, j: (j, 0))]
    args = [a, b]
    if relu2_of is not None:
        in_specs.append(blk)
        args.append(relu2_of)
    if relu2:
        out_specs, out_shape = [blk, blk], [jax.ShapeDtypeStruct((m, n), f32), jax.ShapeDtypeStruct((m, n), bf16)]
    else:
        out_specs, out_shape = blk, jax.ShapeDtypeStruct((m, n), out_dtype)
    return pl.pallas_call(body, name=name, grid=(m // tm, n // tn), in_specs=in_specs, out_specs=out_specs,
                          out_shape=out_shape, compiler_params=_cparams(("parallel", "parallel")))(*args)


def mm_tn(a, b, name, acc_in=None, acc_scale=1.0):
    k, m = a.shape
    n = b.shape[1]
    tm = _pick(m, (MM_TILE, 768, 512, 256, 128))
    tn = _pick(n, (MM_TILE, 512, 256, 128))
    tk = _pick(k, (1024, 768, 512, 256, 128, 64))
    nk = k // tk

    def body(*refs):
        a_ref, b_ref, o_ref = refs[0], refs[1], refs[-1]
        kk = pl.program_id(2)

        @pl.when(kk == 0)
        def _():
            o_ref[...] = jnp.zeros_like(o_ref) if acc_in is None else acc_scale * refs[2][...]

        o_ref[...] += _dot_tn(a_ref[...], b_ref[...])

    in_specs = [pl.BlockSpec((tk, tm), lambda i, j, kk: (kk, i)), pl.BlockSpec((tk, tn), lambda i, j, kk: (kk, j))]
    args = [a, b]
    if acc_in is not None:
        in_specs.append(pl.BlockSpec((tm, tn), lambda i, j, kk: (i, j)))
        args.append(acc_in)
    return pl.pallas_call(
        body, name=name, grid=(m // tm, n // tn, nk), in_specs=in_specs,
        out_specs=pl.BlockSpec((tm, tn), lambda i, j, kk: (i, j)),
        out_shape=jax.ShapeDtypeStruct((m, n), f32),
        compiler_params=_cparams(("parallel", "parallel", "arbitrary")))(*args)


def _halo_specs(tb, cb, nt, off=0):
    r = tb // 8
    return [pl.BlockSpec((8, cb), lambda j, i: (jnp.maximum(i * r - 1, 0), j + off)),
            pl.BlockSpec((tb, cb), lambda j, i: (i, j + off)),
            pl.BlockSpec((8, cb), lambda j, i: (jnp.minimum((i + 1) * r, nt * r - 1), j + off))]


def _with_halo(prev_ref, own_ref, next_ref, i, nt):
    prev = jnp.where(i > 0, prev_ref[...].astype(f32), 0.0)
    nxt = jnp.where(i < nt - 1, next_ref[...].astype(f32), 0.0)
    return jnp.concatenate([prev, own_ref[...].astype(f32), nxt], axis=0)


def _shifted(xcat, s, tb):
    n = xcat.shape[0]
    return pltpu.roll(xcat, (-s) % n, 0)[8:8 + tb]


def conv_fwd(xbc, w8, b_row, tb=512, cb=512):
    t, c = xbc.shape
    nt = t // tb

    def body(prev_ref, own_ref, next_ref, w_ref, b_ref, o_ref):
        i = pl.program_id(1)
        xcat = _with_halo(prev_ref, own_ref, next_ref, i, nt)
        w = w_ref[...]
        pre = b_ref[...] + w[0:1] * _shifted(xcat, -2, tb)
        for k in range(1, D_CONV):
            pre = pre + w[k:k + 1] * _shifted(xcat, k - 2, tb)
        o_ref[...] = pre * _sigmoid(pre)

    return pl.pallas_call(
        body, name="conv_fwd", grid=(c // cb, nt),
        in_specs=_halo_specs(tb, cb, nt) + [pl.BlockSpec((8, cb), lambda j, i: (0, j)), pl.BlockSpec((1, cb), lambda j, i: (0, j))],
        out_specs=pl.BlockSpec((tb, cb), lambda j, i: (i, j)), out_shape=jax.ShapeDtypeStruct((t, c), f32),
        compiler_params=_cparams(("parallel", "parallel")))(xbc, xbc, xbc, w8, b_row)


def conv_bwd(xbc, xoff, grads, scales, w8, b_row, name, tb=512, cb=512):
    t, c = grads[0].shape
    nt = t // tb
    ng = len(grads)
    has_scale = [s is not None for s in scales]

    def body(*refs):
        i = pl.program_id(1)
        xr = refs[0:3]
        gr = [refs[3 + 3 * q: 6 + 3 * q] for q in range(ng)]
        pos = 3 + 3 * ng
        sr = []
        for q in range(ng):
            if has_scale[q]:
                sr.append(refs[pos])
                pos += 1
            else:
                sr.append(None)
        w_ref, b_ref, dx_ref, dw_ref, db_ref = refs[pos:pos + 5]
        xcat = _with_halo(*xr, i, nt)
        gcat = None
        for q in range(ng):
            gq = _with_halo(*gr[q], i, nt)
            if sr[q] is not None:
                gq = gq * sr[q][...]
            gcat = gq if gcat is None else gcat + gq
        w = w_ref[...]
        n = tb + 16
        pre = b_ref[...] + w[0:1] * pltpu.roll(xcat, 2, 0)
        for k in range(1, D_CONV):
            pre = pre + w[k:k + 1] * pltpu.roll(xcat, (2 - k) % n, 0)
        sg = _sigmoid(pre)
        dpre = gcat * sg * (1.0 + pre * (1.0 - sg))
        dx = w[0:1] * _shifted(dpre, 2, tb)
        for k in range(1, D_CONV):
            dx = dx + w[k:k + 1] * _shifted(dpre, 2 - k, tb)
        dx_ref[...] = dx.astype(dx_ref.dtype)
        dp_own = dpre[8:8 + tb]
        rows = [jnp.sum(dp_own * _shifted(xcat, k - 2, tb), axis=0, keepdims=True) for k in range(D_CONV)]
        dw = jnp.concatenate(rows + [jnp.zeros((8 - D_CONV, cb), f32)], axis=0)
        db = jnp.sum(dp_own, axis=0, keepdims=True)

        @pl.when(i == 0)
        def _():
            dw_ref[...] = jnp.zeros_like(dw_ref)
            db_ref[...] = jnp.zeros_like(db_ref)

        dw_ref[...] += dw
        db_ref[...] += db

    in_specs = _halo_specs(tb, cb, nt, xoff)
    args = [xbc] * 3
    for g in grads:
        in_specs += _halo_specs(tb, cb, nt)
        args += [g] * 3
    for s in scales:
        if s is not None:
            in_specs.append(pl.BlockSpec((1, cb), lambda j, i: (0, j)))
            args.append(s)
    in_specs += [pl.BlockSpec((8, cb), lambda j, i: (0, j)), pl.BlockSpec((1, cb), lambda j, i: (0, j))]
    args += [w8, b_row]
    return pl.pallas_call(
        body, name=name, grid=(c // cb, nt), in_specs=in_specs,
        out_specs=[pl.BlockSpec((tb, cb), lambda j, i: (i, j)), pl.BlockSpec((8, cb), lambda j, i: (0, j)),
                   pl.BlockSpec((1, cb), lambda j, i: (0, j))],
        out_shape=[jax.ShapeDtypeStruct((t, c), bf16), jax.ShapeDtypeStruct((8, c), f32), jax.ShapeDtypeStruct((1, c), f32)],
        compiler_params=_cparams(("parallel", "arbitrary")))(*args)


def dt_fwd(u_dt, bias_row, tb=1024):
    t = u_dt.shape[0]

    def body(u_ref, b_ref, o_ref):
        v = u_ref[...] + b_ref[...]
        sp = jnp.maximum(v, 0.0) + jnp.log(1.0 + jnp.exp(-jnp.abs(v)))
        lane = lax.broadcasted_iota(jnp.int32, v.shape, 1)
        o_ref[...] = jnp.where((lane & 127) < SSD_HEADS, sp, 0.0)

    return pl.pallas_call(
        body, name="dt_fwd", grid=(t // tb,),
        in_specs=[pl.BlockSpec((tb, 256), lambda i: (i, 0)), pl.BlockSpec((1, 256), lambda i: (0, 0))],
        out_specs=pl.BlockSpec((tb, 256), lambda i: (i, 0)), out_shape=jax.ShapeDtypeStruct((t, 256), f32),
        compiler_params=_cparams(("parallel",)))(u_dt, bias_row)


def dt_bwd(ddt_f, ddt_b, u_dt, bias_row, tb=1024):
    t = u_dt.shape[0]

    def body(gf_ref, gb_ref, u_ref, b_ref, du_ref, db_ref):
        g = jnp.concatenate([jnp.sum(gf_ref[...], axis=0), jnp.sum(gb_ref[...], axis=0)], axis=1)
        du = g * _sigmoid(u_ref[...] + b_ref[...])
        du_ref[...] = du.astype(du_ref.dtype)

        @pl.when(pl.program_id(0) == 0)
        def _():
            db_ref[...] = jnp.zeros_like(db_ref)

        db_ref[...] += jnp.sum(du, axis=0, keepdims=True)

    return pl.pallas_call(
        body, name="dt_bwd", grid=(t // tb,),
        in_specs=[pl.BlockSpec((4, tb, 128), lambda i: (0, i, 0)), pl.BlockSpec((4, tb, 128), lambda i: (0, i, 0)),
                  pl.BlockSpec((tb, 256), lambda i: (i, 0)), pl.BlockSpec((1, 256), lambda i: (0, 0))],
        out_specs=[pl.BlockSpec((tb, 256), lambda i: (i, 0)), pl.BlockSpec((1, 256), lambda i: (0, 0))],
        out_shape=[jax.ShapeDtypeStruct((t, 256), bf16), jax.ShapeDtypeStruct((1, 256), f32)],
        compiler_params=_cparams(("arbitrary",)))(ddt_f, ddt_b, u_dt, bias_row)


def _ssd_common(dt_blk, a_row, reverse):
    row = lax.broadcasted_iota(jnp.int32, (CHUNK, CHUNK), 0)
    col = lax.broadcasted_iota(jnp.int32, (CHUNK, CHUNK), 1)
    mask = (row <= col) if reverse else (row >= col)
    tri = mask.astype(f32)
    a = dt_blk * a_row
    acs = _dot_exact(tri, a)
    atot = jnp.sum(a, axis=0, keepdims=True)
    return mask, tri, a, acs, atot, col


def _lane_col(mat, lane_idx, h):
    return jnp.sum(jnp.where(lane_idx == h, mat, 0.0), axis=1, keepdims=True)


def ssd_fwd(xbc_c, dt2, a_rows, reverse, name):
    t = xbc_c.shape[0]
    nc = t // CHUNK
    d_off = 1 if reverse else 0

    def cidx(c):
        return nc - 1 - c if reverse else c

    def body(x_ref, b_ref, c_ref, dt_ref, a_ref, y_ref, hp_ref, h_scr, acst_scr):
        g = pl.program_id(0)
        c = pl.program_id(1)

        @pl.when(c == 0)
        def _():
            h_scr[...] = jnp.zeros_like(h_scr)

        dt_blk = dt_ref[...]
        mask, tri, a, acs, atot, lane = _ssd_common(dt_blk, a_ref[...], reverse)
        acst_scr[...] = acs.T
        bm = b_ref[...]
        cm = c_ref[...]
        cb = _dot_nt(cm, bm)
        half = lane >= SSD_HEAD_DIM
        sub_half = lax.broadcasted_iota(jnp.int32, (CHUNK, 1), 0) >= SSD_HEAD_DIM
        for j in range(4):
            x = x_ref[:, 128 * j:128 * (j + 1)]
            cols, dts, tots = [], [], []
            y = None
            for e in range(2):
                h = 8 * g + 2 * j + e
                col_h = _lane_col(acs, lane, h)
                row_h = acst_scr[pl.ds(h, 1), :]
                dt_h = _lane_col(dt_blk, lane, h)
                lmat = jnp.where(mask, jnp.exp(jnp.where(mask, col_h - row_h, 0.0)), 0.0)
                xdt_e = jnp.where(half == (e == 1), x * dt_h, 0.0)
                ye = _dot(cb * lmat, xdt_e)
                y = ye if y is None else y + ye
                cols.append(col_h)
                dts.append(dt_h)
                tots.append(jnp.sum(jnp.where(lane[0:1] == h, atot, 0.0), axis=1, keepdims=True))
            hp = h_scr[j]
            hp_ref[0, j] = hp
            ecol = jnp.where(half, jnp.exp(cols[1]), jnp.exp(cols[0]))
            y = y + _dot_nt(cm, hp) * ecol
            y_ref[:, 128 * j:128 * (j + 1)] = y
            dec = jnp.where(half, jnp.exp(tots[1] - cols[1]), jnp.exp(tots[0] - cols[0]))
            xdt = x * jnp.where(half, dts[1], dts[0])
            s_new = _dot_tn(xdt * dec, bm)
            cd = jnp.where(sub_half, jnp.exp(tots[1]), jnp.exp(tots[0]))
            h_scr[j] = cd * hp + s_new

    return pl.pallas_call(
        body, name=name, grid=(SSD_GROUPS, nc),
        in_specs=[pl.BlockSpec((CHUNK, 512), lambda g, c: (cidx(c), g)),
                  pl.BlockSpec((CHUNK, 128), lambda g, c: (cidx(c), 16 + g)),
                  pl.BlockSpec((CHUNK, 128), lambda g, c: (cidx(c), 20 + g)),
                  pl.BlockSpec((CHUNK, 128), lambda g, c: (cidx(c), d_off)),
                  pl.BlockSpec((1, 128), lambda g, c: (0, d_off))],
        out_specs=[pl.BlockSpec((CHUNK, 512), lambda g, c: (cidx(c), g)),
                   pl.BlockSpec((1, 4, 128, 128), lambda g, c: (cidx(c), g, 0, 0))],
        out_shape=[jax.ShapeDtypeStruct((t, D_INNER), f32), jax.ShapeDtypeStruct((nc, 16, 128, 128), f32)],
        scratch_shapes=[pltpu.VMEM((4, 128, 128), f32), pltpu.VMEM((CHUNK, CHUNK), f32)],
        compiler_params=_cparams(("parallel", "arbitrary")))(xbc_c, xbc_c, xbc_c, dt2, a_rows)


def ssd_bwd(xbc_c, dt2, a_rows, dy, hprev, reverse, name):
    t = xbc_c.shape[0]
    nc = t // CHUNK
    d_off = 1 if reverse else 0

    def cidx(c):
        return c if reverse else nc - 1 - c

    def body(x_ref, b_ref, c_ref, dt_ref, a_ref, dy_ref, hp_ref, dx_ref, db_ref, dc_ref, ddt_ref, da_ref,
             dh_scr, acst_scr):
        g = pl.program_id(0)
        c = pl.program_id(1)

        @pl.when(c == 0)
        def _():
            dh_scr[...] = jnp.zeros_like(dh_scr)
            da_ref[...] = jnp.zeros_like(da_ref)

        dt_blk = dt_ref[...]
        a_row = a_ref[...]
        mask, tri, a, acs, atot, lane = _ssd_common(dt_blk, a_row, reverse)
        acst_scr[...] = acs.T
        sub = lax.broadcasted_iota(jnp.int32, (CHUNK, CHUNK), 0)
        bm = b_ref[...]
        cm = c_ref[...]
        cb = _dot_nt(cm, bm)
        half = lane >= SSD_HEAD_DIM
        sub_half = sub[:, 0:1] >= SSD_HEAD_DIM
        dcb = jnp.zeros((CHUNK, CHUNK), f32)
        dacs = jnp.zeros((CHUNK, CHUNK), f32)
        dacs_t = jnp.zeros((CHUNK, CHUNK), f32)
        dtot = jnp.zeros((1, CHUNK), f32)
        ddt_x = jnp.zeros((CHUNK, CHUNK), f32)
        dbm = jnp.zeros((CHUNK, D_STATE), f32)
        dcm = jnp.zeros((CHUNK, D_STATE), f32)
        for j in range(4):
            x = x_ref[:, 128 * j:128 * (j + 1)]
            dyp = dy_ref[:, 128 * j:128 * (j + 1)]
            hp = hp_ref[0, j]
            dhn = dh_scr[j]
            cols, dts, tots, hs = [], [], [], []
            dxdt = None
            for e in range(2):
                h = 8 * g + 2 * j + e
                sel = half == (e == 1)
                col_h = _lane_col(acs, lane, h)
                row_h = acst_scr[pl.ds(h, 1), :]
                dt_h = _lane_col(dt_blk, lane, h)
                lmat = jnp.where(mask, jnp.exp(jnp.where(mask, col_h - row_h, 0.0)), 0.0)
                xdt_e = jnp.where(sel, x * dt_h, 0.0)
                dy_e = jnp.where(sel, dyp, 0.0)
                ml = _dot_nt(dy_e, xdt_e) * lmat
                dcb = dcb + ml
                w = ml * cb
                dacs = dacs + jnp.where(lane == h, jnp.sum(w, axis=1, keepdims=True), 0.0)
                dacs_t = dacs_t - jnp.where(sub == h, jnp.sum(w, axis=0, keepdims=True), 0.0)
                de = _dot_tn(cb * lmat, dy_e)
                dxdt = de if dxdt is None else dxdt + de
                cols.append(col_h)
                dts.append(dt_h)
                tots.append(jnp.sum(jnp.where(lane[0:1] == h, atot, 0.0), axis=1, keepdims=True))
                hs.append(h)
            ecol = jnp.where(half, jnp.exp(cols[1]), jnp.exp(cols[0]))
            dec = jnp.where(half, jnp.exp(tots[1] - cols[1]), jnp.exp(tots[0] - cols[0]))
            cd = jnp.where(sub_half, jnp.exp(tots[1]), jnp.exp(tots[0]))
            dtp = jnp.where(half, dts[1], dts[0])
            xdt = x * dtp
            yoff = _dot_nt(cm, hp) * ecol
            dye = dyp * ecol
            dcm = dcm + _dot(dye, hp)
            dhp = _dot_tn(dye, cm)
            gmat = _dot_nt(bm, dhn)
            dxdt = dxdt + dec * gmat
            dbm = dbm + _dot(xdt * dec, dhn)
            r_off = dyp * yoff
            r_dec = xdt * gmat * dec
            r_x = dxdt * x
            hh = dhn * hp
            for e in range(2):
                sel = half == (e == 1)
                h = hs[e]
                s_off = jnp.sum(jnp.where(sel, r_off, 0.0), axis=1, keepdims=True)
                s_dec = jnp.sum(jnp.where(sel, r_dec, 0.0), axis=1, keepdims=True)
                dacs = dacs + jnp.where(lane == h, s_off - s_dec, 0.0)
                dcd = jnp.sum(jnp.sum(jnp.where(sub_half == (e == 1), hh, 0.0), axis=1, keepdims=True), axis=0, keepdims=True)
                tot_e = jnp.sum(s_dec, axis=0, keepdims=True) + jnp.exp(tots[e]) * dcd
                dtot = dtot + jnp.where(lane[0:1] == h, tot_e, 0.0)
                ddt_x = ddt_x + jnp.where(lane == h, jnp.sum(jnp.where(sel, r_x, 0.0), axis=1, keepdims=True), 0.0)
            dx_ref[:, 128 * j:128 * (j + 1)] = dxdt * dtp
            dh_scr[j] = cd * dhn + dhp
        dcm = dcm + _dot(dcb, bm)
        dbm = dbm + _dot_tn(dcb, cm)
        db_ref[...] = dbm
        dc_ref[...] = dcm
        dacs = dacs + dacs_t.T
        da = _dot_exact(tri.T, dacs) + dtot
        ddt_ref[0] = da * a_row + ddt_x
        da_ref[0] += jnp.sum(da * dt_blk, axis=0, keepdims=True)

    return pl.pallas_call(
        body, name=name, grid=(SSD_GROUPS, nc),
        in_specs=[pl.BlockSpec((CHUNK, 512), lambda g, c: (cidx(c), g)),
                  pl.BlockSpec((CHUNK, 128), lambda g, c: (cidx(c), 16 + g)),
                  pl.BlockSpec((CHUNK, 128), lambda g, c: (cidx(c), 20 + g)),
                  pl.BlockSpec((CHUNK, 128), lambda g, c: (cidx(c), d_off)),
                  pl.BlockSpec((1, 128), lambda g, c: (0, d_off)),
                  pl.BlockSpec((CHUNK, 512), lambda g, c: (cidx(c), g)),
                  pl.BlockSpec((1, 4, 128, 128), lambda g, c: (cidx(c), g, 0, 0))],
        out_specs=[pl.BlockSpec((CHUNK, 512), lambda g, c: (cidx(c), g)),
                   pl.BlockSpec((CHUNK, 128), lambda g, c: (cidx(c), g)),
                   pl.BlockSpec((CHUNK, 128), lambda g, c: (cidx(c), g)),
                   pl.BlockSpec((1, CHUNK, 128), lambda g, c: (g, cidx(c), 0)),
                   pl.BlockSpec((1, 1, 128), lambda g, c: (g, 0, 0))],
        out_shape=[jax.ShapeDtypeStruct((t, D_INNER), f32), jax.ShapeDtypeStruct((t, 512), f32),
                   jax.ShapeDtypeStruct((t, 512), f32), jax.ShapeDtypeStruct((4, t, 128), f32),
                   jax.ShapeDtypeStruct((4, 1, 128), f32)],
        scratch_shapes=[pltpu.VMEM((4, 128, 128), f32), pltpu.VMEM((CHUNK, CHUNK), f32)],
        compiler_params=_cparams(("parallel", "arbitrary")))(xbc_c, xbc_c, xbc_c, dt2, a_rows, dy, hprev)


def tail_fwd(y_f, y_b, xbc_c, z, dskip_row, nw_row, tb=512):
    t = y_f.shape[0]

    def body(yf_ref, yb_ref, x_ref, z_ref, d_ref, w_ref, o_ref):
        zz = z_ref[...]
        y = (yf_ref[...] + yb_ref[...] + d_ref[...] * x_ref[...]) * (zz * _sigmoid(zz))
        rstd = lax.rsqrt(jnp.mean(y * y, axis=1, keepdims=True) + NORM_EPS)
        o_ref[...] = (y * rstd * w_ref[...]).astype(o_ref.dtype)

    blk = pl.BlockSpec((tb, 512), lambda i, g: (i, g))
    row = pl.BlockSpec((1, 512), lambda i, g: (0, g))
    return pl.pallas_call(
        body, name="tail_fwd", grid=(t // tb, SSD_GROUPS), in_specs=[blk, blk, blk, blk, row, row], out_specs=blk,
        out_shape=jax.ShapeDtypeStruct((t, D_INNER), bf16),
        compiler_params=_cparams(("parallel", "parallel")))(y_f, y_b, xbc_c, z, dskip_row, nw_row)


def tail_bwd(dyn, y_f, y_b, xbc_c, z, dskip_row, nw_row, tb=512):
    t = y_f.shape[0]

    def body(g_ref, yf_ref, yb_ref, x_ref, z_ref, d_ref, w_ref, dy_ref, dz_ref, dw_ref, dd_ref):
        zz = z_ref[...]
        sg = _sigmoid(zz)
        sl = zz * sg
        x = x_ref[...]
        y = yf_ref[...] + yb_ref[...] + d_ref[...] * x
        yz = y * sl
        rstd = lax.rsqrt(jnp.mean(yz * yz, axis=1, keepdims=True) + NORM_EPS)
        yhat = yz * rstd
        g = g_ref[...]
        dyhat = g * w_ref[...]
        dyz = rstd * (dyhat - yhat * jnp.mean(dyhat * yhat, axis=1, keepdims=True))
        dy = dyz * sl
        dy_ref[...] = dy
        dz_ref[...] = (dyz * y * sg * (1.0 + zz * (1.0 - sg))).astype(dz_ref.dtype)

        @pl.when(pl.program_id(1) == 0)
        def _():
            dw_ref[...] = jnp.zeros_like(dw_ref)
            dd_ref[...] = jnp.zeros_like(dd_ref)

        dw_ref[...] += jnp.sum(g * yhat, axis=0, keepdims=True)
        dd_ref[...] += jnp.sum(dy * x, axis=0, keepdims=True)

    blk = pl.BlockSpec((tb, 512), lambda g, i: (i, g))
    row = pl.BlockSpec((1, 512), lambda g, i: (0, g))
    return pl.pallas_call(
        body, name="tail_bwd", grid=(SSD_GROUPS, t // tb), in_specs=[blk, blk, blk, blk, blk, row, row],
        out_specs=[blk, blk, row, row],
        out_shape=[jax.ShapeDtypeStruct((t, D_INNER), f32), jax.ShapeDtypeStruct((t, D_INNER), bf16),
                   jax.ShapeDtypeStruct((1, D_INNER), f32), jax.ShapeDtypeStruct((1, D_INNER), f32)],
        compiler_params=_cparams(("parallel", "arbitrary")))(dyn, y_f, y_b, xbc_c, z, dskip_row, nw_row)


def _slopes(p):
    return [2.0 ** (-8.0 * (HEADS_PER_PATTERN * p + j + 1) / ATTN_HEADS) for j in range(HEADS_PER_PATTERN)]


def _win_specs(nq, col_of):
    return [pl.BlockSpec((64, 256), lambda r, i: (jnp.maximum(2 * i - 1, 0), col_of(r))),
            pl.BlockSpec((128, 256), lambda r, i: (i, col_of(r))),
            pl.BlockSpec((64, 256), lambda r, i: (jnp.minimum(2 * i + 2, 2 * nq - 1), col_of(r)))]


def _lane_head(shape):
    return lax.broadcasted_iota(jnp.int32, shape, 1) >> 6


def _stack_heads(m):
    lane_head = _lane_head(m.shape)
    return jnp.concatenate([jnp.where(lane_head == j, m, 0.0) for j in range(HEADS_PER_PATTERN)], axis=0)


def _unstack_heads(m4, n):
    lane_head = _lane_head((n, 256))
    out = jnp.where(lane_head == 0, m4[0:n], 0.0)
    for j in range(1, HEADS_PER_PATTERN):
        out = out + jnp.where(lane_head == j, m4[j * n:(j + 1) * n], 0.0)
    return out


def _head_cols(m, n):
    lane = lax.broadcasted_iota(jnp.int32, (n, 256), 1)
    return jnp.concatenate([jnp.sum(jnp.where(lane == ATTN_HEAD_DIM * j, m, 0.0), axis=1, keepdims=True)
                            for j in range(HEADS_PER_PATTERN)], axis=0)


def _q_scores(q, kcat, i, nq, p, dil):
    s = _dot_nt(_stack_heads(q * 0.125), kcat)
    row = lax.broadcasted_iota(jnp.int32, s.shape, 0)
    col = lax.broadcasted_iota(jnp.int32, s.shape, 1)
    rel = col - 64 - (row & 127)
    valid = (jnp.abs(rel) <= 64) & ((i > 0) | (col >= 64)) & ((i < nq - 1) | (col < 192))
    sl = _slopes(p)
    hd = row >> 7
    slope = jnp.where(hd == 0, sl[0], jnp.where(hd == 1, sl[1], jnp.where(hd == 2, sl[2], sl[3])))
    s = s - slope * (jnp.abs(rel) * dil).astype(f32)
    return jnp.where(valid, s, NEG_BIG)


def attn_fwd(q, k, v, p, dil, name):
    l = q.shape[0]
    nq = l // 128

    def body(q_ref, kp_ref, ko_ref, kn_ref, vp_ref, vo_ref, vn_ref, o_ref, lse_ref):
        i = pl.program_id(1)
        kcat = jnp.concatenate([kp_ref[...], ko_ref[...], kn_ref[...]], axis=0)
        vcat = jnp.concatenate([vp_ref[...], vo_ref[...], vn_ref[...]], axis=0)
        s = _q_scores(q_ref[...], kcat, i, nq, p, dil)
        m = jnp.max(s, axis=1, keepdims=True)
        pr = jnp.exp(s - m)
        den = jnp.sum(pr, axis=1, keepdims=True)
        o4 = _dot(pr, vcat) / den
        o_ref[...] = _unstack_heads(o4, 128)
        lse_ref[...] = _unstack_heads(jnp.broadcast_to(m + jnp.log(den), (512, 256)), 128)

    col = lambda r: r
    return pl.pallas_call(
        body, name=name, grid=(dil, nq),
        in_specs=[pl.BlockSpec((128, 256), lambda r, i: (i, r))] + _win_specs(nq, col) + _win_specs(nq, col),
        out_specs=[pl.BlockSpec((128, 256), lambda r, i: (i, r))] * 2,
        out_shape=[jax.ShapeDtypeStruct(q.shape, f32)] * 2,
        compiler_params=_cparams(("parallel", "parallel")))(q, k, k, k, v, v, v)


def attn_combine(os_, lses, tb=1024):
    t = os_[0].shape[0]

    def body(o0, o1, o2, l0, l1, l2, y_ref, lse_ref):
        a0, a1, a2 = l0[...], l1[...], l2[...]
        m = jnp.maximum(jnp.maximum(a0, a1), a2)
        e0, e1, e2 = jnp.exp(a0 - m), jnp.exp(a1 - m), jnp.exp(a2 - m)
        den = e0 + e1 + e2
        y_ref[...] = (e0 * o0[...] + e1 * o1[...] + e2 * o2[...]) / den
        lse_ref[...] = m + jnp.log(den)

    blk = pl.BlockSpec((tb, 256), lambda i: (i, 0))
    return pl.pallas_call(
        body, name="attn_combine", grid=(t // tb,), in_specs=[blk] * 6, out_specs=[blk, blk],
        out_shape=[jax.ShapeDtypeStruct((t, 256), f32)] * 2,
        compiler_params=_cparams(("parallel",)))(*os_, *lses)


def attn_delta(dy, y, tb=1024):
    t = dy.shape[0]

    def body(dy_ref, y_ref, d_ref):
        pr = dy_ref[...] * y_ref[...]
        lane_head = _lane_head(pr.shape)
        out = jnp.zeros_like(pr)
        for j in range(HEADS_PER_PATTERN):
            sj = jnp.sum(jnp.where(lane_head == j, pr, 0.0), axis=1, keepdims=True)
            out = out + jnp.where(lane_head == j, sj, 0.0)
        d_ref[...] = out

    blk = pl.BlockSpec((tb, 256), lambda i: (i, 0))
    return pl.pallas_call(body, name="attn_delta", grid=(t // tb,), in_specs=[blk, blk], out_specs=blk,
                          out_shape=jax.ShapeDtypeStruct((t, 256), f32),
                          compiler_params=_cparams(("parallel",)))(dy, y)


def attn_dq(q, k, v, dy, lse, delta, p, dil, name):
    l = q.shape[0]
    nq = l // 128

    def body(q_ref, kp_ref, ko_ref, kn_ref, vp_ref, vo_ref, vn_ref, dy_ref, lse_ref, d_ref, dq_ref):
        i = pl.program_id(1)
        kcat = jnp.concatenate([kp_ref[...], ko_ref[...], kn_ref[...]], axis=0)
        vcat = jnp.concatenate([vp_ref[...], vo_ref[...], vn_ref[...]], axis=0)
        s = _q_scores(q_ref[...], kcat, i, nq, p, dil)
        pr = jnp.exp(s - _head_cols(lse_ref[...], 128))
        dp = _dot_nt(_stack_heads(dy_ref[...]), vcat)
        ds = pr * (dp - _head_cols(d_ref[...], 128))
        dq_ref[...] = (_unstack_heads(_dot(ds, kcat), 128) * 0.125).astype(dq_ref.dtype)

    col = lambda r: r
    own = pl.BlockSpec((128, 256), lambda r, i: (i, r))
    return pl.pallas_call(
        body, name=name, grid=(dil, nq),
        in_specs=[own] + _win_specs(nq, col) + _win_specs(nq, col) + [own, own, own], out_specs=own,
        out_shape=jax.ShapeDtypeStruct(q.shape, bf16),
        compiler_params=_cparams(("parallel", "parallel")))(q, k, k, k, v, v, v, dy, lse, delta)


def attn_dkv(q, k, v, dy, lse, delta, p, dil, name):
    l = q.shape[0]
    nq = l // 128

    def body(qp_ref, qo_ref, qn_ref, gp_ref, go_ref, gn_ref, lp_ref, lo_ref, ln_ref, dp_ref, do_ref, dn_ref,
             k_ref, v_ref, dk_ref, dv_ref):
        i = pl.program_id(1)
        cat = lambda a, b, c: jnp.concatenate([a[...], b[...], c[...]], axis=0)
        q4 = _stack_heads(cat(qp_ref, qo_ref, qn_ref) * 0.125)
        dy4 = _stack_heads(cat(gp_ref, go_ref, gn_ref))
        lse4 = _head_cols(cat(lp_ref, lo_ref, ln_ref), 256)
        del4 = _head_cols(cat(dp_ref, do_ref, dn_ref), 256)
        s = _dot_nt(q4, k_ref[...])
        row = lax.broadcasted_iota(jnp.int32, s.shape, 0)
        col = lax.broadcasted_iota(jnp.int32, s.shape, 1)
        qoff = row & 255
        rel = col - (qoff - 64)
        valid = (jnp.abs(rel) <= 64) & ((i > 0) | (qoff >= 64)) & ((i < nq - 1) | (qoff < 192))
        sl = _slopes(p)
        hd = row >> 8
        slope = jnp.where(hd == 0, sl[0], jnp.where(hd == 1, sl[1], jnp.where(hd == 2, sl[2], sl[3])))
        s = s - slope * (jnp.abs(rel) * dil).astype(f32)
        pr = jnp.where(valid, jnp.exp(jnp.where(valid, s, NEG_BIG) - lse4), 0.0)
        dpm = _dot_nt(dy4, v_ref[...])
        ds = pr * (dpm - del4)
        dv_ref[...] = _dot_tn(pr, dy4).astype(dv_ref.dtype)
        dk_ref[...] = _dot_tn(ds, q4).astype(dk_ref.dtype)

    col = lambda r: r
    own = pl.BlockSpec((128, 256), lambda r, i: (i, r))
    win = _win_specs(nq, col)
    return pl.pallas_call(
        body, name=name, grid=(dil, nq), in_specs=win * 4 + [own, own], out_specs=[own, own],
        out_shape=[jax.ShapeDtypeStruct(q.shape, bf16)] * 2,
        compiler_params=_cparams(("parallel", "parallel")))(q, q, q, dy, dy, dy, lse, lse, lse, delta, delta, delta, k, v)


def _lanes(v, reps):
    return v if reps == 1 else jnp.tile(v, (1, reps))


def _lane_halo_specs(cb, tb, nt, off=0):
    r = tb // 128
    return [pl.BlockSpec((cb, 128), lambda j, i: (j + off, jnp.maximum(i * r - 1, 0))),
            pl.BlockSpec((cb, tb), lambda j, i: (j + off, i)),
            pl.BlockSpec((cb, 128), lambda j, i: (j + off, jnp.minimum((i + 1) * r, nt * r - 1)))]


def _with_lane_halo(prev_ref, own_ref, next_ref, i, nt):
    prev = jnp.where(i > 0, prev_ref[...].astype(f32), 0.0)
    nxt = jnp.where(i < nt - 1, next_ref[...].astype(f32), 0.0)
    return jnp.concatenate([prev, own_ref[...].astype(f32), nxt], axis=1)


def _lane_shifted(xcat, s, tb):
    n = xcat.shape[1]
    return pltpu.roll(xcat, (-s) % n, 1)[:, 128:128 + tb]


def conv_fwd_t(xbc_t, w_b, b_b, tb=1024, cb=256):
    c, t = xbc_t.shape
    nt = t // tb

    def body(prev_ref, own_ref, next_ref, w_ref, b_ref, o_ref):
        i = pl.program_id(1)
        xcat = _with_lane_halo(prev_ref, own_ref, next_ref, i, nt)
        reps = tb // 128
        pre = _lanes(b_ref[...], reps)
        for k in range(D_CONV):
            pre = pre + _lanes(w_ref[k], reps) * _lane_shifted(xcat, k - 2, tb)
        o_ref[...] = pre * _sigmoid(pre)

    return pl.pallas_call(
        body, name="conv_fwd", grid=(c // cb, nt),
        in_specs=_lane_halo_specs(cb, tb, nt) + [pl.BlockSpec((D_CONV, cb, 128), lambda j, i: (0, j, 0)),
                                                 pl.BlockSpec((cb, 128), lambda j, i: (j, 0))],
        out_specs=pl.BlockSpec((cb, tb), lambda j, i: (j, i)), out_shape=jax.ShapeDtypeStruct((c, t), f32),
        compiler_params=_cparams(("parallel", "parallel")))(xbc_t, xbc_t, xbc_t, w_b, b_b)


def conv_bwd_t(xbc_t, grad_t, w_b, b_b, into, name, row0, tb=1024, cb=256):
    c, t = grad_t.shape
    nt = t // tb
    off = row0 // cb
    reps = tb // 128

    def body(*refs):
        i = pl.program_id(1)
        xr, gr = refs[0:3], refs[3:6]
        w_ref, b_ref = refs[6:8]
        dx_ref, dw_ref, db_ref = refs[-3:]
        xcat = _with_lane_halo(*xr, i, nt)
        gcat = _with_lane_halo(*gr, i, nt)
        n = tb + 256
        wk = [_lanes(w_ref[k], reps + 2) for k in range(D_CONV)]
        pre = _lanes(b_ref[...], reps + 2)
        for k in range(D_CONV):
            pre = pre + wk[k] * pltpu.roll(xcat, (2 - k) % n, 1)
        sg = _sigmoid(pre)
        dpre = gcat * sg * (1.0 + pre * (1.0 - sg))
        dx = None
        for k in range(D_CONV):
            term = wk[k][:, 128:128 + tb] * _lane_shifted(dpre, 2 - k, tb)
            dx = term if dx is None else dx + term
        dx_ref[...] = dx.astype(dx_ref.dtype)
        dp_own = dpre[:, 128:128 + tb]

        def fold(v):
            s = v[:, 0:128]
            for q in range(1, reps):
                s = s + v[:, 128 * q:128 * (q + 1)]
            return s

        @pl.when(i == 0)
        def _():
            dw_ref[...] = jnp.zeros_like(dw_ref)
            db_ref[...] = jnp.zeros_like(db_ref)

        for k in range(D_CONV):
            dw_ref[k] += fold(dp_own * _lane_shifted(xcat, k - 2, tb))
        db_ref[...] += fold(dp_own)

    in_specs = (_lane_halo_specs(cb, tb, nt, off) + _lane_halo_specs(cb, tb, nt)
                + [pl.BlockSpec((D_CONV, cb, 128), lambda j, i: (0, j + off, 0)), pl.BlockSpec((cb, 128), lambda j, i: (j + off, 0))])
    args = [xbc_t] * 3 + [grad_t] * 3 + [w_b, b_b]
    aliases = {}
    if into is not None:
        in_specs.append(pl.BlockSpec(memory_space=pl.ANY))
        args.append(into)
        aliases = {len(args) - 1: 0}
    return pl.pallas_call(
        body, name=name, grid=(c // cb, nt), in_specs=in_specs,
        out_specs=[pl.BlockSpec((cb, tb), lambda j, i: (j + off, i)), pl.BlockSpec((D_CONV, cb, 128), lambda j, i: (0, j, 0)),
                   pl.BlockSpec((cb, 128), lambda j, i: (j, 0))],
        out_shape=[jax.ShapeDtypeStruct((CONV_DIM, t), bf16), jax.ShapeDtypeStruct((D_CONV, c, 128), f32),
                   jax.ShapeDtypeStruct((c, 128), f32)],
        input_output_aliases=aliases, compiler_params=_cparams(("parallel", "arbitrary")))(*args)


def dt_fwd_t(u_dt_t, bias_b, tb=2048):
    r, t = u_dt_t.shape

    def body(u_ref, b_ref, o_ref):
        v = u_ref[...] + _lanes(b_ref[...], tb // 128)
        o_ref[...] = jnp.maximum(v, 0.0) + jnp.log(1.0 + jnp.exp(-jnp.abs(v)))

    return pl.pallas_call(
        body, name="dt_fwd", grid=(t // tb,),
        in_specs=[pl.BlockSpec((r, tb), lambda i: (0, i)), pl.BlockSpec((r, 128), lambda i: (0, 0))],
        out_specs=pl.BlockSpec((r, tb), lambda i: (0, i)), out_shape=jax.ShapeDtypeStruct((r, t), f32),
        compiler_params=_cparams(("parallel",)))(u_dt_t, bias_b)


def dt_bwd_t(ddt_f, ddt_b, u_dt_t, bias_b, tb=2048):
    r, t = u_dt_t.shape
    reps = tb // 128

    def body(gf_ref, gb_ref, u_ref, b_ref, du_ref, db_ref):
        g = jnp.concatenate([gf_ref[...], gb_ref[...]], axis=0)
        du = g * _sigmoid(u_ref[...] + _lanes(b_ref[...], reps))
        du_ref[...] = du.astype(du_ref.dtype)

        @pl.when(pl.program_id(0) == 0)
        def _():
            db_ref[...] = jnp.zeros_like(db_ref)

        s = du[:, 0:128]
        for q in range(1, reps):
            s = s + du[:, 128 * q:128 * (q + 1)]
        db_ref[...] += s

    half = pl.BlockSpec((r // 2, tb), lambda i: (0, i))
    return pl.pallas_call(
        body, name="dt_bwd", grid=(t // tb,),
        in_specs=[half, half, pl.BlockSpec((r, tb), lambda i: (0, i)), pl.BlockSpec((r, 128), lambda i: (0, 0))],
        out_specs=[pl.BlockSpec((r, tb), lambda i: (0, i)), pl.BlockSpec((r, 128), lambda i: (0, 0))],
        out_shape=[jax.ShapeDtypeStruct((r, t), bf16), jax.ShapeDtypeStruct((r, 128), f32)],
        compiler_params=_cparams(("arbitrary",)))(ddt_f, ddt_b, u_dt_t, bias_b)


HEADS_PER_GROUP = SSD_HEADS // SSD_GROUPS


def _group_rows(g, n):
    return pl.ds(pl.multiple_of(g * n, n), n)


def _ssd_decays(dt_blk, a_blk, reverse):
    row = lax.broadcasted_iota(jnp.int32, (CHUNK, CHUNK), 0)
    col = lax.broadcasted_iota(jnp.int32, (CHUNK, CHUNK), 1)
    mask = (row <= col) if reverse else (row >= col)
    tri = mask.astype(f32)
    a8 = dt_blk * a_blk
    a = jnp.concatenate([a8, jnp.zeros((CHUNK - HEADS_PER_GROUP, CHUNK), f32)], axis=0).T
    acs = _dot_exact(tri, a)
    return mask, tri, a8, acs, acs.T, col


def ssd_fwd_t(xbc_ct, dt_t, a_b, reverse, name):
    t = xbc_ct.shape[1]
    nc = t // CHUNK
    direction = 1 if reverse else 0

    def cidx(c):
        return nc - 1 - c if reverse else c

    def body(x_ref, b_ref, c_ref, dt_ref, a_ref, y_ref, hp_ref, h_scr):
        @pl.when(pl.program_id(0) == 0)
        def _():
            h_scr[...] = jnp.zeros_like(h_scr)

        def group(g, carry):
            x_v, y_v = x_ref.at[_group_rows(g, 512)], y_ref.at[_group_rows(g, 512)]
            heads = _group_rows(g, HEADS_PER_GROUP)
            hp_v, h_v = hp_ref.at[0, heads], h_scr.at[heads]
            dt_blk = dt_ref[heads, :]
            mask, tri, a8, acs, acs_t, lane = _ssd_decays(dt_blk, a_ref[heads, :], reverse)
            bm = b_ref[_group_rows(g, 128), :].T
            cm = c_ref[_group_rows(g, 128), :].T
            cb = _dot_nt(cm, bm)
            tot = jnp.sum(a8, axis=1, keepdims=True)
            for j in range(HEADS_PER_GROUP):
                rows = slice(SSD_HEAD_DIM * j, SSD_HEAD_DIM * (j + 1))
                col_j = _lane_col(acs, lane, j)
                row_j = acs_t[j:j + 1, :]
                lmat = jnp.where(mask, jnp.exp(jnp.where(mask, col_j - row_j, 0.0)), 0.0)
                xdt = x_v[rows, :] * dt_blk[j:j + 1, :]
                hp = h_v[j]
                hp_v[j] = hp
                y = _dot_nt(xdt, cb * lmat) + _dot_nt(hp, cm) * jnp.exp(row_j)
                y_v[rows, :] = y
                tot_j = tot[j:j + 1, :]
                h_v[j] = jnp.exp(tot_j) * hp + _dot(xdt * jnp.exp(tot_j - row_j), bm)
            return carry

        lax.fori_loop(0, SSD_GROUPS, group, 0)

    return pl.pallas_call(
        body, name=name, grid=(nc,),
        in_specs=[pl.BlockSpec((D_INNER, CHUNK), lambda c: (0, cidx(c))),
                  pl.BlockSpec((512, CHUNK), lambda c: (4, cidx(c))),
                  pl.BlockSpec((512, CHUNK), lambda c: (5, cidx(c))),
                  pl.BlockSpec((SSD_HEADS, CHUNK), lambda c: (direction, cidx(c))),
                  pl.BlockSpec((SSD_HEADS, 128), lambda c: (direction, 0))],
        out_specs=[pl.BlockSpec((D_INNER, CHUNK), lambda c: (0, cidx(c))),
                   pl.BlockSpec((1, SSD_HEADS, SSD_HEAD_DIM, D_STATE), lambda c: (cidx(c), 0, 0, 0))],
        out_shape=[jax.ShapeDtypeStruct((D_INNER, t), f32), jax.ShapeDtypeStruct((nc, SSD_HEADS, SSD_HEAD_DIM, D_STATE), f32)],
        scratch_shapes=[pltpu.VMEM((SSD_HEADS, SSD_HEAD_DIM, D_STATE), f32)],
        compiler_params=_cparams(("arbitrary",)))(xbc_ct, xbc_ct, xbc_ct, dt_t, a_b)


def ssd_bwd_t(xbc_ct, dt_t, a_b, dy_t, hprev, reverse, name, skip_b=None, prev=None):
    t = xbc_ct.shape[1]
    nc = t // CHUNK
    direction = 1 if reverse else 0

    def cidx(c):
        return c if reverse else nc - 1 - c

    def body(*refs):
        x_ref, b_ref, c_ref, dt_ref, a_ref, dy_ref, hp_ref = refs[0:7]
        pos = 7
        skip_ref = None
        if skip_b is not None:
            skip_ref = refs[pos]
            pos += 1
        prev_refs = None
        if prev is not None:
            prev_refs = refs[pos:pos + 3]
            pos += 3
        dx_ref, db_ref, dc_ref, ddt_ref, da_ref, dh_scr = refs[pos:pos + 6]

        @pl.when(pl.program_id(0) == 0)
        def _():
            dh_scr[...] = jnp.zeros_like(dh_scr)
            da_ref[...] = jnp.zeros_like(da_ref)

        def group(g, carry):
            big, st, heads = _group_rows(g, 512), _group_rows(g, 128), _group_rows(g, HEADS_PER_GROUP)
            x_v, dy_v, dx_v = x_ref.at[big], dy_ref.at[big], dx_ref.at[big]
            hp_v, dh_v = hp_ref.at[0, heads], dh_scr.at[heads]
            dt_blk = dt_ref[heads, :]
            a_blk = a_ref[heads, :]
            mask, tri, a8, acs, acs_t, lane = _ssd_decays(dt_blk, a_blk, reverse)
            sub = lax.broadcasted_iota(jnp.int32, (CHUNK, CHUNK), 0)
            mask_t = (sub >= lane) if reverse else (sub <= lane)
            bm = b_ref[st, :].T
            cm = c_ref[st, :].T
            cb = _dot_nt(cm, bm)
            cb_t = _dot_nt(bm, cm)
            tot = jnp.sum(a8, axis=1, keepdims=True)
            dcb = jnp.zeros((CHUNK, CHUNK), f32)
            dbm = jnp.zeros((CHUNK, D_STATE), f32)
            dcm = jnp.zeros((CHUNK, D_STATE), f32)
            dacs_rows, ddtx_rows = [], []
            for j in range(HEADS_PER_GROUP):
                rows = slice(SSD_HEAD_DIM * j, SSD_HEAD_DIM * (j + 1))
                col_j = _lane_col(acs, lane, j)
                row_j = acs_t[j:j + 1, :]
                dt_j = dt_blk[j:j + 1, :]
                tot_j = tot[j:j + 1, :]
                lmat = jnp.where(mask, jnp.exp(jnp.where(mask, col_j - row_j, 0.0)), 0.0)
                lmat_t = jnp.where(mask_t, jnp.exp(jnp.where(mask_t, row_j - col_j, 0.0)), 0.0)
                x = x_v[rows, :]
                xdt = x * dt_j
                dyh = dy_v[rows, :]
                hp = hp_v[j]
                dhn = dh_v[j]
                ml = _dot_tn(dyh, xdt) * lmat
                w_t = _dot_tn(xdt, dyh) * lmat_t * cb_t
                dcb = dcb + ml
                dacs = jnp.sum(w_t, axis=0, keepdims=True) - jnp.sum(ml * cb, axis=0, keepdims=True)
                ecol = jnp.exp(row_j)
                dec = jnp.exp(tot_j - row_j)
                dye = dyh * ecol
                yoff = _dot_nt(hp, cm) * ecol
                gmat = _dot_nt(dhn, bm)
                dxdt = _dot(dyh, cb * lmat) + dec * gmat
                s_dec = jnp.sum(xdt * gmat, axis=0, keepdims=True) * dec
                dacs = dacs + jnp.sum(dyh * yoff, axis=0, keepdims=True) - s_dec
                dcd = jnp.sum(jnp.sum(dhn * hp, axis=1, keepdims=True), axis=0, keepdims=True)
                dtot = jnp.sum(s_dec, axis=1, keepdims=True) + jnp.exp(tot_j) * dcd
                dacs_rows.append((dacs, dtot))
                ddtx_rows.append(jnp.sum(dxdt * x, axis=0, keepdims=True))
                dcm = dcm + _dot_tn(dye, hp)
                dbm = dbm + _dot_tn(xdt * dec, dhn)
                dxh = dxdt * dt_j
                if skip_ref is not None:
                    dxh = dxh + skip_ref.at[big][rows, :] * dyh
                if prev_refs is not None:
                    dxh = dxh + prev_refs[0].at[big][rows, :]
                dx_v[rows, :] = dxh
                dh_v[j] = jnp.exp(tot_j) * dhn + _dot(dye, cm)
            dcm = dcm + _dot(dcb, bm)
            dbm = dbm + _dot_tn(dcb, cm)
            dbt, dct = dbm.T, dcm.T
            if prev_refs is not None:
                dbt = dbt + prev_refs[1][st, :]
                dct = dct + prev_refs[2][st, :]
            db_ref[st, :] = dbt
            dc_ref[st, :] = dct
            dacs8 = jnp.concatenate([d for d, _ in dacs_rows], axis=0)
            dtot8 = jnp.concatenate([d for _, d in dacs_rows], axis=0)
            da8 = _dot_exact(dacs8, tri) + dtot8
            ddt_ref[heads, :] = da8 * a_blk + jnp.concatenate(ddtx_rows, axis=0)
            da_ref[heads, :] += da8 * dt_blk
            return carry

        lax.fori_loop(0, SSD_GROUPS, group, 0)

    big = pl.BlockSpec((D_INNER, CHUNK), lambda c: (0, cidx(c)))
    st = pl.BlockSpec((512, CHUNK), lambda c: (0, cidx(c)))
    in_specs = [big, pl.BlockSpec((512, CHUNK), lambda c: (4, cidx(c))), pl.BlockSpec((512, CHUNK), lambda c: (5, cidx(c))),
                pl.BlockSpec((SSD_HEADS, CHUNK), lambda c: (direction, cidx(c))),
                pl.BlockSpec((SSD_HEADS, 128), lambda c: (direction, 0)), big,
                pl.BlockSpec((1, SSD_HEADS, SSD_HEAD_DIM, D_STATE), lambda c: (cidx(c), 0, 0, 0))]
    args = [xbc_ct, xbc_ct, xbc_ct, dt_t, a_b, dy_t, hprev]
    if skip_b is not None:
        in_specs.append(pl.BlockSpec((D_INNER, 128), lambda c: (0, 0)))
        args.append(skip_b)
    if prev is not None:
        in_specs += [big, st, st]
        args += list(prev)
    return pl.pallas_call(
        body, name=name, grid=(nc,), in_specs=in_specs,
        out_specs=[big, st, st, pl.BlockSpec((SSD_HEADS, CHUNK), lambda c: (0, cidx(c))),
                   pl.BlockSpec((SSD_HEADS, 128), lambda c: (0, 0))],
        out_shape=[jax.ShapeDtypeStruct((D_INNER, t), f32), jax.ShapeDtypeStruct((512, t), f32),
                   jax.ShapeDtypeStruct((512, t), f32), jax.ShapeDtypeStruct((SSD_HEADS, t), f32),
                   jax.ShapeDtypeStruct((SSD_HEADS, 128), f32)],
        scratch_shapes=[pltpu.VMEM((SSD_HEADS, SSD_HEAD_DIM, D_STATE), f32)],
        compiler_params=_cparams(("arbitrary",)))(*args)


def tail_fwd_t(y_f, y_b, xbc_ct, z_t, skip_b, nw_b, tb=512):
    t = y_f.shape[1]
    reps = tb // 128

    def body(yf_ref, yb_ref, x_ref, z_ref, d_ref, w_ref, o_ref):
        zz = z_ref[...]
        y = (yf_ref[...] + yb_ref[...] + _lanes(d_ref[...], reps) * x_ref[...]) * (zz * _sigmoid(zz))
        rstd = lax.rsqrt(jnp.mean(y * y, axis=0, keepdims=True) + NORM_EPS)
        o_ref[...] = (y * rstd * _lanes(w_ref[...], reps)).astype(o_ref.dtype)

    blk = pl.BlockSpec((512, tb), lambda g, i: (g, i))
    par = pl.BlockSpec((512, 128), lambda g, i: (g, 0))
    return pl.pallas_call(
        body, name="tail_fwd", grid=(SSD_GROUPS, t // tb), in_specs=[blk, blk, blk, blk, par, par], out_specs=blk,
        out_shape=jax.ShapeDtypeStruct((D_INNER, t), bf16),
        compiler_params=_cparams(("parallel", "parallel")))(y_f, y_b, xbc_ct, z_t, skip_b, nw_b)


def tail_bwd_t(dyn_t, y_f, y_b, xbc_ct, z_t, skip_b, nw_b, tb=512):
    t = y_f.shape[1]
    reps = tb // 128

    def body(g_ref, yf_ref, yb_ref, x_ref, z_ref, d_ref, w_ref, dy_ref, dz_ref, dw_ref, dd_ref):
        zz = z_ref[...]
        sg = _sigmoid(zz)
        sl = zz * sg
        x = x_ref[...]
        y = yf_ref[...] + yb_ref[...] + _lanes(d_ref[...], reps) * x
        yz = y * sl
        rstd = lax.rsqrt(jnp.mean(yz * yz, axis=0, keepdims=True) + NORM_EPS)
        yhat = yz * rstd
        g = g_ref[...]
        dyhat = g * _lanes(w_ref[...], reps)
        dyz = rstd * (dyhat - yhat * jnp.mean(dyhat * yhat, axis=0, keepdims=True))
        dy = dyz * sl
        dy_ref[...] = dy
        dz_ref[...] = (dyz * y * sg * (1.0 + zz * (1.0 - sg))).astype(dz_ref.dtype)

        def fold(v):
            s = v[:, 0:128]
            for q in range(1, reps):
                s = s + v[:, 128 * q:128 * (q + 1)]
            return s

        @pl.when(pl.program_id(1) == 0)
        def _():
            dw_ref[...] = jnp.zeros_like(dw_ref)
            dd_ref[...] = jnp.zeros_like(dd_ref)

        dw_ref[...] += fold(g * yhat)
        dd_ref[...] += fold(dy * x)

    blk = pl.BlockSpec((512, tb), lambda g, i: (g, i))
    par = pl.BlockSpec((512, 128), lambda g, i: (g, 0))
    return pl.pallas_call(
        body, name="tail_bwd", grid=(SSD_GROUPS, t // tb), in_specs=[blk, blk, blk, blk, blk, par, par],
        out_specs=[blk, blk, par, par],
        out_shape=[jax.ShapeDtypeStruct((D_INNER, t), f32), jax.ShapeDtypeStruct((D_INNER, t), bf16),
                   jax.ShapeDtypeStruct((D_INNER, 128), f32), jax.ShapeDtypeStruct((D_INNER, 128), f32)],
        compiler_params=_cparams(("parallel", "arbitrary")))(dyn_t, y_f, y_b, xbc_ct, z_t, skip_b, nw_b)


def merge_fwd(u_gate, bg_row, y_ssd, y_att, tb=512):
    t = y_ssd.shape[0]

    def body(ga_ref, gb_ref, ba_ref, bb_ref, ys_ref, ya_ref, o_ref):
        o_ref[...] = (_sigmoid(ga_ref[...] + ba_ref[...]) * ys_ref[...]
                      + _sigmoid(gb_ref[...] + bb_ref[...]) * ya_ref[...]).astype(o_ref.dtype)

    blk = pl.BlockSpec((tb, 512), lambda i, j: (i, j))
    blk2 = pl.BlockSpec((tb, 512), lambda i, j: (i, 2 + j))
    row = pl.BlockSpec((1, 512), lambda i, j: (0, j))
    row2 = pl.BlockSpec((1, 512), lambda i, j: (0, 2 + j))
    return pl.pallas_call(
        body, name="merge_fwd", grid=(t // tb, 2), in_specs=[blk, blk2, row, row2, blk, blk], out_specs=blk,
        out_shape=jax.ShapeDtypeStruct((t, D_MODEL), bf16),
        compiler_params=_cparams(("parallel", "parallel")))(u_gate, u_gate, bg_row, bg_row, y_ssd, y_att)


def merge_bwd(dm, u_gate, bg_row, y_ssd, y_att, tb=512):
    t = dm.shape[0]

    def body(dm_ref, ga_ref, gb_ref, ba_ref, bb_ref, ys_ref, ya_ref, dys_ref, dya_ref, dga_ref, dgb_ref, dba_ref, dbb_ref):
        d = dm_ref[...]
        sa = _sigmoid(ga_ref[...] + ba_ref[...])
        sb = _sigmoid(gb_ref[...] + bb_ref[...])
        dys_ref[...] = (d * sa).astype(dys_ref.dtype)
        dya_ref[...] = (d * sb).astype(dya_ref.dtype)
        dla = d * ys_ref[...] * sa * (1.0 - sa)
        dlb = d * ya_ref[...] * sb * (1.0 - sb)
        dga_ref[...] = dla.astype(dga_ref.dtype)
        dgb_ref[...] = dlb.astype(dgb_ref.dtype)

        @pl.when(pl.program_id(1) == 0)
        def _():
            dba_ref[...] = jnp.zeros_like(dba_ref)
            dbb_ref[...] = jnp.zeros_like(dbb_ref)

        dba_ref[...] += jnp.sum(dla, axis=0, keepdims=True)
        dbb_ref[...] += jnp.sum(dlb, axis=0, keepdims=True)

    blk = pl.BlockSpec((tb, 512), lambda j, i: (i, j))
    blk2 = pl.BlockSpec((tb, 512), lambda j, i: (i, 2 + j))
    row = pl.BlockSpec((1, 512), lambda j, i: (0, j))
    row2 = pl.BlockSpec((1, 512), lambda j, i: (0, 2 + j))
    act = jax.ShapeDtypeStruct((t, D_MODEL), bf16)
    vec = jax.ShapeDtypeStruct((1, D_MODEL), f32)
    return pl.pallas_call(
        body, name="merge_bwd", grid=(2, t // tb), in_specs=[blk, blk, blk2, row, row2, blk, blk],
        out_specs=[blk, blk, blk, blk, row, row], out_shape=[act, act, act, act, vec, vec],
        compiler_params=_cparams(("parallel", "arbitrary")))(dm, u_gate, u_gate, bg_row, bg_row, y_ssd, y_att)


def _ln_stats(r):
    mu = jnp.mean(r, axis=1, keepdims=True)
    xc = r - mu
    rstd = lax.rsqrt(jnp.mean(xc * xc, axis=1, keepdims=True) + NORM_EPS)
    return xc * rstd, rstd


def _ln_bwd(dy, xhat, rstd, g_row):
    dxh = dy * g_row
    return rstd * (dxh - jnp.mean(dxh, axis=1, keepdims=True) - xhat * jnp.mean(dxh * xhat, axis=1, keepdims=True))


def ln1_fwd(x, mix, g_row, b_row, tb=512):
    t = x.shape[0]

    def body(x_ref, m_ref, g_ref, b_ref, o_ref, ob_ref):
        xhat, _ = _ln_stats(ALPHA * x_ref[...] + m_ref[...])
        h = xhat * g_ref[...] + b_ref[...]
        o_ref[...] = h
        ob_ref[...] = h.astype(ob_ref.dtype)

    blk = pl.BlockSpec((tb, D_MODEL), lambda i: (i, 0))
    row = pl.BlockSpec((1, D_MODEL), lambda i: (0, 0))
    return pl.pallas_call(body, name="ln1_fwd", grid=(t // tb,), in_specs=[blk, blk, row, row], out_specs=[blk, blk],
                          out_shape=[jax.ShapeDtypeStruct((t, D_MODEL), f32), jax.ShapeDtypeStruct((t, D_MODEL), bf16)],
                          compiler_params=_cparams(("parallel",)))(x, mix, g_row, b_row)


def ln1_bwd(dh, x, mix, g_row, tb=512):
    t = x.shape[0]

    def body(dh_ref, x_ref, m_ref, g_ref, dr_ref, drb_ref, dg_ref, db_ref):
        xhat, rstd = _ln_stats(ALPHA * x_ref[...] + m_ref[...])
        dy = dh_ref[...]
        dr = _ln_bwd(dy, xhat, rstd, g_ref[...])
        dr_ref[...] = dr
        drb_ref[...] = dr.astype(drb_ref.dtype)

        @pl.when(pl.program_id(0) == 0)
        def _():
            dg_ref[...] = jnp.zeros_like(dg_ref)
            db_ref[...] = jnp.zeros_like(db_ref)

        dg_ref[...] += jnp.sum(dy * xhat, axis=0, keepdims=True)
        db_ref[...] += jnp.sum(dy, axis=0, keepdims=True)

    blk = pl.BlockSpec((tb, D_MODEL), lambda i: (i, 0))
    row = pl.BlockSpec((1, D_MODEL), lambda i: (0, 0))
    return pl.pallas_call(
        body, name="ln1_bwd", grid=(t // tb,), in_specs=[blk, blk, blk, row], out_specs=[blk, blk, row, row],
        out_shape=[jax.ShapeDtypeStruct((t, D_MODEL), f32), jax.ShapeDtypeStruct((t, D_MODEL), bf16),
                   jax.ShapeDtypeStruct((1, D_MODEL), f32), jax.ShapeDtypeStruct((1, D_MODEL), f32)],
        compiler_params=_cparams(("arbitrary",)))(dh, x, mix, g_row)


def ln2_loss(h1, f, g_row, b_row, target, tb=512):
    t = h1.shape[0]

    def body(h_ref, f_ref, g_ref, b_ref, t_ref, dr_ref, drb_ref, dg_ref, db_ref, loss_ref):
        xhat, rstd = _ln_stats(ALPHA * h_ref[...] + f_ref[...])
        g = g_ref[...]
        err = xhat * g + b_ref[...] - t_ref[...]
        dy = err * (1.0 / D_MODEL)
        dr = _ln_bwd(dy, xhat, rstd, g)
        dr_ref[...] = dr
        drb_ref[...] = dr.astype(drb_ref.dtype)

        @pl.when(pl.program_id(0) == 0)
        def _():
            dg_ref[...] = jnp.zeros_like(dg_ref)
            db_ref[...] = jnp.zeros_like(db_ref)
            loss_ref[...] = jnp.zeros_like(loss_ref)

        dg_ref[...] += jnp.sum(dy * xhat, axis=0, keepdims=True)
        db_ref[...] += jnp.sum(dy, axis=0, keepdims=True)
        part = jnp.sum(jnp.mean(err * err, axis=1, keepdims=True), axis=0, keepdims=True)
        loss_ref[...] += 0.5 * part

    blk = pl.BlockSpec((tb, D_MODEL), lambda i: (i, 0))
    row = pl.BlockSpec((1, D_MODEL), lambda i: (0, 0))
    return pl.pallas_call(
        body, name="ln2_loss", grid=(t // tb,), in_specs=[blk, blk, row, row, blk],
        out_specs=[blk, blk, row, row, pl.BlockSpec((8, 128), lambda i: (0, 0))],
        out_shape=[jax.ShapeDtypeStruct((t, D_MODEL), f32), jax.ShapeDtypeStruct((t, D_MODEL), bf16),
                   jax.ShapeDtypeStruct((1, D_MODEL), f32), jax.ShapeDtypeStruct((1, D_MODEL), f32),
                   jax.ShapeDtypeStruct((8, 128), f32)],
        compiler_params=_cparams(("arbitrary",)))(h1, f, g_row, b_row, target)


TAIL_BLOCK, TAIL_AT = divmod(OFF_TAIL, PACK_TILE)


def _sum4(ref):
    return ((ref[0].astype(f32) + ref[1].astype(f32)) + ref[2].astype(f32)) + ref[3].astype(f32)


def adamw(parts, tails, w, m, v):
    rows = w.shape[0]
    c1 = 1.0 - ADAM_B1 ** ADAM_STEP
    c2 = 1.0 - ADAM_B2 ** ADAM_STEP

    def body(p_ref, t_ref, w_ref, m_ref, v_ref, g_ref, d_ref, nm_ref, nv_ref):
        g = _sum4(p_ref)
        with_tail = jnp.concatenate([g[0:TAIL_AT], _sum4(t_ref), g[TAIL_AT + ROWS_TAIL:]], axis=0)
        g = jnp.where(pl.program_id(0) == TAIL_BLOCK, with_tail, g)
        nm = ADAM_B1 * m_ref[...] + (1.0 - ADAM_B1) * g
        nv = ADAM_B2 * v_ref[...] + (1.0 - ADAM_B2) * (g * g)
        g_ref[...] = g
        nm_ref[...] = nm
        nv_ref[...] = nv
        d_ref[...] = -ADAM_LR * ((nm / c1) / (jnp.sqrt(nv / c2) + ADAM_EPS) + ADAM_WD * w_ref[...])

    blk = pl.BlockSpec((PACK_TILE, 1024), lambda i: (i, 0))
    out = jax.ShapeDtypeStruct((rows, 1024), f32)
    return pl.pallas_call(
        body, name="adamw", grid=(rows // PACK_TILE,),
        in_specs=[pl.BlockSpec((4, PACK_TILE, 1024), lambda i: (0, i, 0)),
                  pl.BlockSpec((4, ROWS_TAIL, 1024), lambda i: (0, 0, 0)), blk, blk, blk], out_specs=[blk] * 4,
        out_shape=[out] * 4, compiler_params=_cparams(("parallel",)))(parts, tails, w, m, v)


def pair_sum(parts, recv, core):
    rows = parts.shape[1]

    def body(c_ref, a_ref, b_ref, o_ref, t_ref):
        s = a_ref[...] + b_ref[...]
        o_ref[...] = s.astype(o_ref.dtype)

        @pl.when(pl.program_id(1) == TAIL_BLOCK)
        def _():
            t_ref[...] = s[:, TAIL_AT:TAIL_AT + ROWS_TAIL]

    grid_spec = pltpu.PrefetchScalarGridSpec(
        num_scalar_prefetch=1, grid=(4, rows // PACK_TILE),
        in_specs=[pl.BlockSpec((1, PACK_TILE, 1024), lambda j, i, c_ref: (2 * j + c_ref[0], i, 0)),
                  pl.BlockSpec((1, PACK_TILE, 1024), lambda j, i, c_ref: (j, i, 0))],
        out_specs=[pl.BlockSpec((1, PACK_TILE, 1024), lambda j, i, c_ref: (j, i, 0)),
                   pl.BlockSpec((1, ROWS_TAIL, 1024), lambda j, i, c_ref: (j, 0, 0))])
    return pl.pallas_call(
        body, name="pair_sum", grid_spec=grid_spec,
        out_shape=[jax.ShapeDtypeStruct(recv.shape, bf16), jax.ShapeDtypeStruct((4, ROWS_TAIL, 1024), f32)],
        compiler_params=_cparams(("parallel", "arbitrary")))(core, parts, recv)


def _place():
    return lax.axis_index("x"), lax.axis_index("y"), lax.axis_index("c")


def all_gather_blocks(shard):
    rows, cols = shard.shape

    def body(x_ref, out_ref, send_sems, recv_sems, local_sem):
        x, y, c = _place()
        me, sibling = (x, y, c), (x, y, 1 - c)
        chips = [(1 - x, y), (x, 1 - y), (1 - x, 1 - y)]

        def slot(px, py, pc):
            return out_ref.at[4 * px + 2 * py + pc]

        def copy(k, block, to, src=None):
            return pltpu.make_async_remote_copy(
                src_ref=slot(*block) if src is None else src, dst_ref=slot(*block), send_sem=send_sems.at[k],
                recv_sem=recv_sems.at[k], device_id=to, device_id_type=MESH)

        mine = pltpu.make_async_copy(x_ref, slot(*me), local_sem)
        mine.start()
        first = [copy(0, me, sibling, src=x_ref)]
        first += [copy(1 + j, me, (*chip, c), src=x_ref) for j, chip in enumerate(chips)]
        for cp in first:
            cp.start()
        passed = [copy(4 + j, (*chip, c), sibling) for j, chip in enumerate(chips)]
        for j, chip in enumerate(chips):
            copy(1 + j, (*chip, c), me).wait_recv()
            passed[j].start()
        copy(0, sibling, me).wait_recv()
        for j, chip in enumerate(chips):
            copy(4 + j, (*chip, 1 - c), me).wait_recv()
        for cp in first + passed:
            cp.wait_send()
        mine.wait()

    return pl.pallas_call(
        body, name="all_gather_blocks", out_shape=jax.ShapeDtypeStruct((N_DEV, rows, cols), shard.dtype),
        in_specs=[pl.BlockSpec(memory_space=pl.ANY)], out_specs=pl.BlockSpec(memory_space=pl.ANY),
        scratch_shapes=[pltpu.SemaphoreType.DMA((7,)), pltpu.SemaphoreType.DMA((7,)), pltpu.SemaphoreType.DMA],
        compiler_params=pltpu.CompilerParams(has_side_effects=True))(shard)


def pair_exchange(parts):
    _, rows, cols = parts.shape

    def body(p_ref, recv_ref, send_sems, recv_sems):
        x, y, c = _place()
        copies = [pltpu.make_async_remote_copy(
            src_ref=p_ref.at[2 * j + 1 - c], dst_ref=recv_ref.at[j], send_sem=send_sems.at[j], recv_sem=recv_sems.at[j],
            device_id=(x, y, 1 - c), device_id_type=MESH) for j in range(4)]
        for cp in copies:
            cp.start()
        for cp in copies:
            cp.wait_recv()
        for cp in copies:
            cp.wait_send()

    return pl.pallas_call(
        body, name="pair_exchange", out_shape=jax.ShapeDtypeStruct((4, rows, cols), parts.dtype),
        in_specs=[pl.BlockSpec(memory_space=pl.ANY)], out_specs=pl.BlockSpec(memory_space=pl.ANY),
        scratch_shapes=[pltpu.SemaphoreType.DMA((4,)), pltpu.SemaphoreType.DMA((4,))],
        compiler_params=pltpu.CompilerParams(has_side_effects=True))(parts)


def chip_exchange(parts):
    n = len(parts)

    def body(*refs):
        p_refs, out_refs = refs[0:n], refs[n:2 * n]
        send_sems, recv_sems, local_sems = refs[2 * n:]
        x, y, c = _place()
        mine = 2 * x + y
        flips = [(x, 1 - y), (1 - x, y), (1 - x, 1 - y)]

        def copy(a, k, src_slot, dst_slot):
            px, py = flips[k]
            return pltpu.make_async_remote_copy(
                src_ref=p_refs[a].at[src_slot], dst_ref=out_refs[a].at[dst_slot], send_sem=send_sems.at[3 * a + k],
                recv_sem=recv_sems.at[3 * a + k], device_id=(px, py, c), device_id_type=MESH)

        local = [pltpu.make_async_copy(p_refs[a].at[mine], out_refs[a].at[mine], local_sems.at[a]) for a in range(n)]
        sends = [copy(a, k, 2 * flips[k][0] + flips[k][1], mine) for a in range(n) for k in range(3)]
        for cp in local + sends:
            cp.start()
        for a in range(n):
            for k in range(3):
                copy(a, k, mine, 2 * flips[k][0] + flips[k][1]).wait_recv()
        for cp in sends:
            cp.wait_send()
        for cp in local:
            cp.wait()

    return pl.pallas_call(
        body, name="chip_exchange", out_shape=[jax.ShapeDtypeStruct(p.shape, p.dtype) for p in parts],
        in_specs=[pl.BlockSpec(memory_space=pl.ANY)] * n, out_specs=[pl.BlockSpec(memory_space=pl.ANY)] * n,
        scratch_shapes=[pltpu.SemaphoreType.DMA((3 * n,)), pltpu.SemaphoreType.DMA((3 * n,)), pltpu.SemaphoreType.DMA((n,))],
        compiler_params=pltpu.CompilerParams(has_side_effects=True))(*parts)


def _tail_rows(conv_part, small, extra):
    lead = conv_part.shape[:-1]
    rep = jnp.concatenate([small[n].reshape(-1).astype(f32) for n in SMALL] + [extra.reshape(1).astype(f32)])
    flat = jnp.concatenate([conv_part, jnp.broadcast_to(rep, lead + rep.shape),
                            jnp.zeros(lead + (ROWS_TAIL * 1024 - TAIL_ELEMS,), f32)], axis=-1)
    return flat.reshape(lead + (ROWS_TAIL, 1024))


def _pack_rows(w_in_t, w_ps, w_out, w_up_t, w_down, w_pa_t, tail):
    lead = tail.shape[:-2]
    zeros = lambda r: jnp.zeros(lead + (r, 1024), f32)
    return jnp.concatenate([w_in_t, zeros(ROWS_IN - IN_SHARD), w_ps, w_out, w_up_t, w_down,
                            w_pa_t.reshape(lead + (ROWS_PA, 1024)), tail,
                            zeros(PACK_ROWS - OFF_TAIL - ROWS_TAIL)], axis=-2)


def _pack_shard(vals):
    tail = _tail_rows(vals["conv_w"].reshape(-1), vals, jnp.zeros((), f32))
    return _pack_rows(vals["w_in"].T, vals["w_proj_ssd"], vals["w_out"], vals["w_up"].T, vals["w_down"],
                      vals["w_proj_attn"].T, tail)


def _unpack_shard(packed):
    out = {"w_in": packed[0:IN_SHARD].T, "w_proj_ssd": packed[OFF_PS:OFF_OUT], "w_out": packed[OFF_OUT:OFF_UP],
           "w_up": packed[OFF_UP:OFF_DOWN].T, "w_down": packed[OFF_DOWN:OFF_PA],
           "w_proj_attn": packed[OFF_PA:OFF_TAIL].reshape(D_MODEL // N_DEV, ATTN_OUT).T}
    flat = packed[OFF_TAIL:OFF_TAIL + ROWS_TAIL].reshape(-1)
    out["conv_w"] = flat[0:CONV_SHARD].reshape(D_CONV, CONV_DIM // N_DEV)
    off = CONV_SHARD
    for n in SMALL:
        out[n] = flat[off:off + SMALL_SIZES[n]]
        off += SMALL_SIZES[n]
    out["_extra"] = flat[off]
    return out


def _pack_parts(full, small, extra):
    conv = full["conv_w"].reshape(D_CONV, N_DEV, CONV_DIM // N_DEV).transpose(1, 0, 2).reshape(N_DEV, CONV_SHARD)
    blocks = lambda g: g.reshape(N_DEV, g.shape[0] // N_DEV, g.shape[1])
    return _pack_rows(blocks(full["w_in_t"]), blocks(full["w_proj_ssd"]), blocks(full["w_out"]), blocks(full["w_up_t"]),
                      blocks(full["w_down"]), blocks(full["w_proj_attn_t"]), _tail_rows(conv, small, extra))


def _gather_weights(w):
    conv_bits = lax.bitcast_convert_type(w["conv_w"], bf16).reshape(-1)
    conv_rows = jnp.concatenate([conv_bits, jnp.zeros((16 * 1024 - 2 * CONV_SHARD,), bf16)]).reshape(16, 1024)
    big = _pack_shard(w)[0:OFF_TAIL].astype(bf16)
    got = all_gather_blocks(jnp.concatenate([big, conv_rows], axis=0))
    whole = lambda lo, hi: got[:, lo:hi].reshape(N_DEV * (hi - lo), 1024)
    conv = lax.bitcast_convert_type(got[:, OFF_TAIL:OFF_TAIL + 4].reshape(N_DEV, 4096)[:, 0:2 * CONV_SHARD]
                                    .reshape(N_DEV, D_CONV, CONV_DIM // N_DEV, 2), f32)
    return {"w_in_t": whole(0, IN_SHARD), "w_proj_ssd": whole(OFF_PS, OFF_OUT), "w_out": whole(OFF_OUT, OFF_UP),
            "w_up_t": whole(OFF_UP, OFF_DOWN), "w_down": whole(OFF_DOWN, OFF_PA),
            "w_proj_attn_t": got[:, OFF_PA:OFF_TAIL].reshape(D_MODEL, ATTN_OUT),
            "conv_w": conv.transpose(1, 0, 2).reshape(D_CONV, CONV_DIM)}


def _row(v, width=None):
    v = v.reshape(1, -1).astype(f32)
    return v if width is None else jnp.pad(v, ((0, 0), (0, width - v.shape[1])))


def _lanes256(vf, vb):
    z = jnp.zeros((96,), f32)
    return jnp.concatenate([vf.astype(f32), z, vb.astype(f32), z]).reshape(1, 256)


def _local_step(x2, tgt, wf, p):
    t = x2.shape[0]
    o = np.cumsum((0,) + IN_SPLITS)
    wt = wf["w_in_t"]
    wt_z, wt_xbc, wt_dt = wt[o[0]:o[1]], wt[o[1]:o[2]], wt[o[2]:o[4]]
    wt_qkv, wt_gate = wt[o[4]:o[7]], wt[o[7]:o[8]]

    spread = lambda v: jnp.broadcast_to(v.astype(f32)[..., None], v.shape + (128,))
    conv_w_b, conv_b_b = spread(wf["conv_w"]), spread(p["conv_b"])
    dt_bias_b = spread(jnp.concatenate([p["dt_bias_f"], p["dt_bias_b"]]))
    a_f, a_b = -jnp.exp(p["a_log_f"].astype(f32)), -jnp.exp(p["a_log_b"].astype(f32))
    a_coef_b = spread(jnp.concatenate([a_f, a_b]))
    skip_b = spread(jnp.repeat(p["d_skip"], SSD_HEAD_DIM))
    nw_b, bg_row = spread(p["ssd_norm_w"]), _row(p["b_gate"])
    g1, b1, g2, b2 = _row(p["ln1_g"]), _row(p["ln1_b"]), _row(p["ln2_g"]), _row(p["ln2_b"])

    xb = x2.astype(MXU_DTYPE)
    xt = xb.T
    u_z = mm_nn(wt_z, xt, "in_z")
    u_xbc = mm_nn(wt_xbc, xt, "in_xbc")
    u_dt = mm_nn(wt_dt, xt, "in_dt")
    u_qkv = mm_nt(xb, wt_qkv, "in_qkv")
    u_gate = mm_nt(xb, wt_gate, "in_gate")
    xbc_c = conv_fwd_t(u_xbc, conv_w_b, conv_b_b)
    dt_t = dt_fwd_t(u_dt, dt_bias_b)
    y_f, h_f = ssd_fwd_t(xbc_c, dt_t, a_coef_b, False, "ssd_fwd_f")
    y_b, h_b = ssd_fwd_t(xbc_c, dt_t, a_coef_b, True, "ssd_fwd_b")
    yn = tail_fwd_t(y_f, y_b, xbc_c, u_z, skip_b, nw_b)
    y_ssd = mm_tn(yn, wf["w_proj_ssd"], "proj_ssd")

    def strided(a, dil):
        return a.reshape(t // dil, dil * 256)

    qkv, outs, lses = [], [], []
    for pi, (_, dil) in enumerate(DIL_PATTERNS):
        q, k, v = (strided(u_qkv[:, ATTN_WIDTH * s + 256 * pi: ATTN_WIDTH * s + 256 * (pi + 1)], dil) for s in range(3))
        qkv.append((q, k, v))
        op, lp = attn_fwd(q, k, v, pi, dil, f"attn_fwd_{pi}")
        outs.append(op.reshape(t, 256))
        lses.append(lp.reshape(t, 256))
    ya, lse = attn_combine(outs, lses)
    y_att = mm_nt(ya, wf["w_proj_attn_t"], "proj_attn")
    m = merge_fwd(u_gate, bg_row, y_ssd, y_att)
    mix = mm_nn(m, wf["w_out"], "out_proj")
    h1, h1b = ln1_fwd(x2, mix, g1, b1)
    a_up, p_act = mm_nt(h1b, wf["w_up_t"], "mlp_up", relu2=True)
    f_dn = mm_nn(p_act, wf["w_down"], "mlp_down")
    dr2, dr2b, dg2, db2, loss8 = ln2_loss(h1, f_dn, g2, b2, tgt)

    full, small = {}, {}
    da = mm_nt(dr2b, wf["w_down"], "d_mlp_act", out_dtype=bf16, relu2_of=a_up)
    full["w_down"] = mm_tn(p_act, dr2b, "dw_down")
    full["w_up_t"] = mm_tn(da, h1b, "dw_up")
    dh1 = mm_nn(da, wf["w_up_t"], "d_h1", acc_in=dr2, acc_scale=ALPHA)
    dr1, dr1b, dg1, db1 = ln1_bwd(dh1, x2, mix, g1)
    dm = mm_nt(dr1b, wf["w_out"], "d_merge")
    full["w_out"] = mm_tn(m, dr1b, "dw_out")
    dys, dya_p, dga, dgb, dba, dbb = merge_bwd(dm, u_gate, bg_row, y_ssd, y_att)
    dyn = mm_nt(wf["w_proj_ssd"], dys, "d_yn")
    full["w_proj_ssd"] = mm_nn(yn, dys, "dw_proj_ssd")
    dya = mm_nn(dya_p, wf["w_proj_attn_t"], "d_ya")
    full["w_proj_attn_t"] = mm_tn(dya_p, ya, "dw_proj_attn")

    dy, dz, dnw, ddx = tail_bwd_t(dyn, y_f, y_b, xbc_c, u_z, skip_b, nw_b)
    dxf, dbf, dcf, ddtf, daf = ssd_bwd_t(xbc_c, dt_t, a_coef_b, dy, h_f, False, "ssd_bwd_f", skip_b=skip_b)
    dxs, dbs, dcs, ddtb, dab = ssd_bwd_t(xbc_c, dt_t, a_coef_b, dy, h_b, True, "ssd_bwd_b", prev=(dxf, dbf, dcf))
    du_xbc, dcw_x, dcb_x = conv_bwd_t(u_xbc, dxs, conv_w_b, conv_b_b, None, "conv_bwd_x", 0)
    du_xbc, dcw_b, dcb_b = conv_bwd_t(u_xbc, dbs, conv_w_b, conv_b_b, du_xbc, "conv_bwd_b", D_INNER)
    du_xbc, dcw_c, dcb_c = conv_bwd_t(u_xbc, dcs, conv_w_b, conv_b_b, du_xbc, "conv_bwd_c", D_INNER + 512)
    du_dt, dbias = dt_bwd_t(ddtf, ddtb, u_dt, dt_bias_b)

    delta = attn_delta(dya, ya)
    dqs, dks, dvs = [], [], []
    for pi, (_, dil) in enumerate(DIL_PATTERNS):
        q, k, v = qkv[pi]
        sd, sl_, sdel = strided(dya, dil), strided(lse, dil), strided(delta, dil)
        dqs.append(attn_dq(q, k, v, sd, sl_, sdel, pi, dil, f"attn_dq_{pi}").reshape(t, 256))
        dk, dv = attn_dkv(q, k, v, sd, sl_, sdel, pi, dil, f"attn_dkv_{pi}")
        dks.append(dk.reshape(t, 256))
        dvs.append(dv.reshape(t, 256))
    du_qkv = jnp.concatenate(dqs + dks + dvs, axis=1)
    du_gate = jnp.concatenate([dga, dgb], axis=1)

    dx = mm_tn(dz, wt_z, "dx_z", acc_in=dr1, acc_scale=ALPHA)
    dx = mm_tn(du_xbc, wt_xbc, "dx_xbc", acc_in=dx)
    dx = mm_tn(du_dt, wt_dt, "dx_dt", acc_in=dx)
    dx = mm_nn(du_qkv, wt_qkv, "dx_qkv", acc_in=dx)
    dx = mm_nn(du_gate, wt_gate, "dx_gate", acc_in=dx)
    full["w_in_t"] = jnp.concatenate(
        [mm_nn(dz, xb, "dw_in_z"), mm_nn(du_xbc, xb, "dw_in_xbc"), mm_nn(du_dt, xb, "dw_in_dt"),
         mm_tn(du_qkv, xb, "dw_in_qkv"), mm_tn(du_gate, xb, "dw_in_gate")], axis=0)
    lanes = lambda v: jnp.sum(v, axis=-1)
    full["conv_w"] = jnp.concatenate([lanes(dcw_x), lanes(dcw_b), lanes(dcw_c)], axis=1)

    small["b_gate"] = jnp.concatenate([dba, dbb], axis=1)
    small["conv_b"] = jnp.concatenate([lanes(dcb_x), lanes(dcb_b), lanes(dcb_c)])
    dbias = lanes(dbias)
    small["dt_bias_f"], small["dt_bias_b"] = dbias[0:32], dbias[32:64]
    small["a_log_f"] = lanes(daf) * a_f
    small["a_log_b"] = lanes(dab) * a_b
    small["d_skip"] = jnp.sum(lanes(ddx).reshape(SSD_HEADS, SSD_HEAD_DIM), axis=1)
    small["ssd_norm_w"] = lanes(dnw)
    small["ln1_g"], small["ln1_b"], small["ln2_g"], small["ln2_b"] = dg1, db1, dg2, db2
    return loss8[0, 0], dx, full, small


def kernel(x, w_in, b_gate, conv_w, conv_b, dt_bias_f, dt_bias_b, a_log_f, a_log_b, d_skip, ssd_norm_w, w_proj_ssd, w_proj_attn, w_out, ln1_g, ln1_b, w_up, w_down, ln2_g, ln2_b, loss_target, m_w_in, m_b_gate, m_conv_w, m_conv_b, m_dt_bias_f, m_dt_bias_b, m_a_log_f, m_a_log_b, m_d_skip, m_ssd_norm_w, m_w_proj_ssd, m_w_proj_attn, m_w_out, m_ln1_g, m_ln1_b, m_w_up, m_w_down, m_ln2_g, m_ln2_b, v_w_in, v_b_gate, v_conv_w, v_conv_b, v_dt_bias_f, v_dt_bias_b, v_a_log_f, v_a_log_b, v_d_skip, v_ssd_norm_w, v_w_proj_ssd, v_w_proj_attn, v_w_out, v_ln1_g, v_ln1_b, v_w_up, v_w_down, v_ln2_g, v_ln2_b):
    given = dict(locals())
    w = {n: given[n] for n in WEIGHTS}
    mom = {n: given["m_" + n] for n in WEIGHTS}
    var = {n: given["v_" + n] for n in WEIGHTS}
    t = x.shape[1]
    wf = _gather_weights(w)
    loss, dx, full, small = _local_step(x.reshape(t, D_MODEL), loss_target.reshape(t, D_MODEL), wf, w)
    packed = _pack_parts(full, small, loss)
    core = lax.axis_index("c").astype(jnp.int32).reshape(1)
    parts, tails = chip_exchange(pair_sum(packed, pair_exchange(packed), core))
    g, delta, new_m, new_v = (_unpack_shard(a) for a in
                              adamw(parts, tails, _pack_shard(w), _pack_shard(mom), _pack_shard(var)))
    outs = [g["_extra"], dx.reshape(x.shape)]
    for d in (g, delta, new_m, new_v):
        outs += [d[n].reshape(w[n].shape) for n in WEIGHTS]
    return tuple(outs)
```

```python
import functools
import math

import jax
import jax.numpy as jnp
import numpy as np
from jax import lax
from jax.experimental import pallas as pl
from jax.experimental.pallas import tpu as pltpu

f32 = jnp.float32
bf16 = jnp.bfloat16
MXU_DTYPE = jnp.bfloat16

N_DEV = 8
D_MODEL = 1024
D_INNER = 2048
SSD_HEADS = 32
SSD_HEAD_DIM = 64
SSD_GROUPS = 4
D_STATE = 128
D_CONV = 5
CHUNK = 128
CONV_DIM = D_INNER + 2 * SSD_GROUPS * D_STATE
NORM_EPS = 1e-5
ATTN_HEAD_DIM = 64
DIL_PATTERNS = ((128, 1), (512, 4), (2048, 16))
HEADS_PER_PATTERN = 4
ATTN_HEADS = 12
ATTN_WIDTH = 768
ATTN_OUT = 256
D_FF = 4096
ALPHA = 2.0 ** 0.25
IN_SPLITS = (D_INNER, CONV_DIM, SSD_HEADS, SSD_HEADS, ATTN_WIDTH, ATTN_WIDTH, ATTN_WIDTH, 2 * D_MODEL)
IN_COLS = sum(IN_SPLITS)
ADAM_LR, ADAM_B1, ADAM_B2, ADAM_EPS, ADAM_WD, ADAM_STEP = 0.001, 0.9, 0.999, 1e-08, 0.01, 10
NEG_BIG = -1e30
VMEM_LIMIT = 56 * 1024 * 1024
MESH = pl.DeviceIdType.MESH

SMALL = ("b_gate", "conv_b", "dt_bias_f", "dt_bias_b", "a_log_f", "a_log_b", "d_skip", "ssd_norm_w",
         "ln1_g", "ln1_b", "ln2_g", "ln2_b")
WEIGHTS = ("w_in", "b_gate", "conv_w", "conv_b", "dt_bias_f", "dt_bias_b", "a_log_f", "a_log_b", "d_skip",
           "ssd_norm_w", "w_proj_ssd", "w_proj_attn", "w_out", "ln1_g", "ln1_b", "w_up", "w_down", "ln2_g", "ln2_b")
SMALL_SIZES = {"b_gate": 2 * D_MODEL, "conv_b": CONV_DIM, "dt_bias_f": 32, "dt_bias_b": 32, "a_log_f": 32, "a_log_b": 32,
               "d_skip": 32, "ssd_norm_w": D_INNER, "ln1_g": D_MODEL, "ln1_b": D_MODEL, "ln2_g": D_MODEL, "ln2_b": D_MODEL}
IN_SHARD = IN_COLS // N_DEV
ROWS_IN = 1200
ROWS_PS, ROWS_OUT, ROWS_UP, ROWS_DOWN, ROWS_PA = D_INNER // N_DEV, D_MODEL // N_DEV, D_FF // N_DEV, D_FF // N_DEV, 32
OFF_PS = ROWS_IN
OFF_OUT = OFF_PS + ROWS_PS
OFF_UP = OFF_OUT + ROWS_OUT
OFF_DOWN = OFF_UP + ROWS_UP
OFF_PA = OFF_DOWN + ROWS_DOWN
OFF_TAIL = OFF_PA + ROWS_PA
CONV_SHARD = D_CONV * CONV_DIM // N_DEV
TAIL_ELEMS = CONV_SHARD + sum(SMALL_SIZES.values()) + 1
ROWS_TAIL = 16
PACK_TILE = 128
PACK_ROWS = -(-(OFF_TAIL + ROWS_TAIL) // PACK_TILE) * PACK_TILE


def _cparams(sem=None, **kw):
    return pltpu.CompilerParams(dimension_semantics=sem, vmem_limit_bytes=VMEM_LIMIT, **kw)


def _mx(v):
    return v.astype(MXU_DTYPE)


def _dot(a, b):
    return jnp.dot(_mx(a), _mx(b), preferred_element_type=f32)


def _dot_nt(a, b):
    return lax.dot_general(_mx(a), _mx(b), (((1,), (1,)), ((), ())), preferred_element_type=f32)


def _dot_tn(a, b):
    return lax.dot_general(_mx(a), _mx(b), (((0,), (0,)), ((), ())), preferred_element_type=f32)


def _dot_exact(a, b):
    return jnp.dot(a, b, precision=lax.Precision.HIGHEST, preferred_element_type=f32)


def _sigmoid(v):
    return 1.0 / (1.0 + jnp.exp(-v))


def _pick(n, prefs):
    for p in prefs:
        if n % p == 0:
            return p
    return n


MM_TILE = 1024


def mm_nn(a, b, name, out_dtype=f32, acc_in=None, acc_scale=1.0):
    m, k = a.shape
    n = b.shape[1]
    tm = _pick(m, (MM_TILE, 512, 256, 128, 64))
    tn = _pick(n, (MM_TILE, 512, 256, 128))
    tk = _pick(k, (2048, 1536, 1152, 1024, 768, 512, 256, 128))
    nk = k // tk

    def body(*refs):
        a_ref, b_ref = refs[0:2]
        c_ref = refs[2] if acc_in is not None else None
        o_ref = refs[3] if acc_in is not None else refs[2]

        def finish(r):
            if acc_in is not None:
                r = r + acc_scale * c_ref[...]
            o_ref[...] = r.astype(o_ref.dtype)

        if nk == 1:
            finish(_dot(a_ref[...], b_ref[...]))
            return
        acc_ref = refs[-1]
        kk = pl.program_id(2)

        @pl.when(kk == 0)
        def _():
            acc_ref[...] = jnp.zeros_like(acc_ref)

        acc_ref[...] += _dot(a_ref[...], b_ref[...])

        @pl.when(kk == nk - 1)
        def _():
            finish(acc_ref[...])

    in_specs = [pl.BlockSpec((tm, tk), lambda i, j, kk: (i, kk)), pl.BlockSpec((tk, tn), lambda i, j, kk: (kk, j))]
    args = [a, b]
    if acc_in is not None:
        in_specs.append(pl.BlockSpec((tm, tn), lambda i, j, kk: (i, j)))
        args.append(acc_in)
    return pl.pallas_call(
        body, name=name, grid=(m // tm, n // tn, nk), in_specs=in_specs,
        out_specs=pl.BlockSpec((tm, tn), lambda i, j, kk: (i, j)),
        out_shape=jax.ShapeDtypeStruct((m, n), out_dtype),
        scratch_shapes=[pltpu.VMEM((tm, tn), f32)] if nk > 1 else [],
        compiler_params=_cparams(("parallel", "parallel", "arbitrary")))(*args)


def mm_nt(a, b, name, out_dtype=f32, relu2=None, relu2_of=None):
    m, k = a.shape
    n = b.shape[0]
    tm = MM_TILE
    tn = _pick(n, (MM_TILE, 768, 512, 256, 128))

    def body(*refs):
        r = _dot_nt(refs[0][...], refs[1][...])
        if relu2:
            refs[2][...] = r
            pos = jnp.maximum(r, 0.0)
            refs[3][...] = (pos * pos).astype(refs[3].dtype)
        elif relu2_of is not None:
            refs[3][...] = (r * (2.0 * jnp.maximum(refs[2][...], 0.0))).astype(refs[3].dtype)
        else:
            refs[2][...] = r.astype(refs[2].dtype)

    blk = pl.BlockSpec((tm, tn), lambda i, j: (i, j))
    in_specs = [pl.BlockSpec((tm, k), lambda i, j: (i, 0)), pl.BlockSpec((tn, k), lambda i, j: (j, 0))]
    args = [a, b]
    if relu2_of is not None:
        in_specs.append(blk)
        args.append(relu2_of)
    if relu2:
        out_specs, out_shape = [blk, blk], [jax.ShapeDtypeStruct((m, n), f32), jax.ShapeDtypeStruct((m, n), bf16)]
    else:
        out_specs, out_shape = blk, jax.ShapeDtypeStruct((m, n), out_dtype)
    return pl.pallas_call(body, name=name, grid=(m // tm, n // tn), in_specs=in_specs, out_specs=out_specs,
                          out_shape=out_shape, compiler_params=_cparams(("parallel", "parallel")))(*args)


def mm_tn(a, b, name, acc_in=None, acc_scale=1.0):
    k, m = a.shape
    n = b.shape[1]
    tm = _pick(m, (MM_TILE, 768, 512, 256, 128))
    tn = _pick(n, (MM_TILE, 512, 256, 128))
    tk = _pick(k, (1024, 768, 512, 256, 128, 64))
    nk = k // tk

    def body(*refs):
        a_ref, b_ref, o_ref = refs[0], refs[1], refs[-1]
        kk = pl.program_id(2)

        @pl.when(kk == 0)
        def _():
            o_ref[...] = jnp.zeros_like(o_ref) if acc_in is None else acc_scale * refs[2][...]

        o_ref[...] += _dot_tn(a_ref[...], b_ref[...])

    in_specs = [pl.BlockSpec((tk, tm), lambda i, j, kk: (kk, i)), pl.BlockSpec((tk, tn), lambda i, j, kk: (kk, j))]
    args = [a, b]
    if acc_in is not None:
        in_specs.append(pl.BlockSpec((tm, tn), lambda i, j, kk: (i, j)))
        args.append(acc_in)
    return pl.pallas_call(
        body, name=name, grid=(m // tm, n // tn, nk), in_specs=in_specs,
        out_specs=pl.BlockSpec((tm, tn), lambda i, j, kk: (i, j)),
        out_shape=jax.ShapeDtypeStruct((m, n), f32),
        compiler_params=_cparams(("parallel", "parallel", "arbitrary")))(*args)


def _halo_specs(tb, cb, nt, off=0):
    r = tb // 8
    return [pl.BlockSpec((8, cb), lambda j, i: (jnp.maximum(i * r - 1, 0), j + off)),
            pl.BlockSpec((tb, cb), lambda j, i: (i, j + off)),
            pl.BlockSpec((8, cb), lambda j, i: (jnp.minimum((i + 1) * r, nt * r - 1), j + off))]


def _with_halo(prev_ref, own_ref, next_ref, i, nt):
    prev = jnp.where(i > 0, prev_ref[...].astype(f32), 0.0)
    nxt = jnp.where(i < nt - 1, next_ref[...].astype(f32), 0.0)
    return jnp.concatenate([prev, own_ref[...].astype(f32), nxt], axis=0)


def _shifted(xcat, s, tb):
    n = xcat.shape[0]
    return pltpu.roll(xcat, (-s) % n, 0)[8:8 + tb]


def conv_fwd(xbc, w8, b_row, tb=512, cb=512):
    t, c = xbc.shape
    nt = t // tb

    def body(prev_ref, own_ref, next_ref, w_ref, b_ref, o_ref):
        i = pl.program_id(1)
        xcat = _with_halo(prev_ref, own_ref, next_ref, i, nt)
        w = w_ref[...]
        pre = b_ref[...] + w[0:1] * _shifted(xcat, -2, tb)
        for k in range(1, D_CONV):
            pre = pre + w[k:k + 1] * _shifted(xcat, k - 2, tb)
        o_ref[...] = pre * _sigmoid(pre)

    return pl.pallas_call(
        body, name="conv_fwd", grid=(c // cb, nt),
        in_specs=_halo_specs(tb, cb, nt) + [pl.BlockSpec((8, cb), lambda j, i: (0, j)), pl.BlockSpec((1, cb), lambda j, i: (0, j))],
        out_specs=pl.BlockSpec((tb, cb), lambda j, i: (i, j)), out_shape=jax.ShapeDtypeStruct((t, c), f32),
        compiler_params=_cparams(("parallel", "parallel")))(xbc, xbc, xbc, w8, b_row)


def conv_bwd(xbc, xoff, grads, scales, w8, b_row, name, tb=512, cb=512):
    t, c = grads[0].shape
    nt = t // tb
    ng = len(grads)
    has_scale = [s is not None for s in scales]

    def body(*refs):
        i = pl.program_id(1)
        xr = refs[0:3]
        gr = [refs[3 + 3 * q: 6 + 3 * q] for q in range(ng)]
        pos = 3 + 3 * ng
        sr = []
        for q in range(ng):
            if has_scale[q]:
                sr.append(refs[pos])
                pos += 1
            else:
                sr.append(None)
        w_ref, b_ref, dx_ref, dw_ref, db_ref = refs[pos:pos + 5]
        xcat = _with_halo(*xr, i, nt)
        gcat = None
        for q in range(ng):
            gq = _with_halo(*gr[q], i, nt)
            if sr[q] is not None:
                gq = gq * sr[q][...]
            gcat = gq if gcat is None else gcat + gq
        w = w_ref[...]
        n = tb + 16
        pre = b_ref[...] + w[0:1] * pltpu.roll(xcat, 2, 0)
        for k in range(1, D_CONV):
            pre = pre + w[k:k + 1] * pltpu.roll(xcat, (2 - k) % n, 0)
        sg = _sigmoid(pre)
        dpre = gcat * sg * (1.0 + pre * (1.0 - sg))
        dx = w[0:1] * _shifted(dpre, 2, tb)
        for k in range(1, D_CONV):
            dx = dx + w[k:k + 1] * _shifted(dpre, 2 - k, tb)
        dx_ref[...] = dx.astype(dx_ref.dtype)
        dp_own = dpre[8:8 + tb]
        rows = [jnp.sum(dp_own * _shifted(xcat, k - 2, tb), axis=0, keepdims=True) for k in range(D_CONV)]
        dw = jnp.concatenate(rows + [jnp.zeros((8 - D_CONV, cb), f32)], axis=0)
        db = jnp.sum(dp_own, axis=0, keepdims=True)

        @pl.when(i == 0)
        def _():
            dw_ref[...] = jnp.zeros_like(dw_ref)
            db_ref[...] = jnp.zeros_like(db_ref)

        dw_ref[...] += dw
        db_ref[...] += db

    in_specs = _halo_specs(tb, cb, nt, xoff)
    args = [xbc] * 3
    for g in grads:
        in_specs += _halo_specs(tb, cb, nt)
        args += [g] * 3
    for s in scales:
        if s is not None:
            in_specs.append(pl.BlockSpec((1, cb), lambda j, i: (0, j)))
            args.append(s)
    in_specs += [pl.BlockSpec((8, cb), lambda j, i: (0, j)), pl.BlockSpec((1, cb), lambda j, i: (0, j))]
    args += [w8, b_row]
    return pl.pallas_call(
        body, name=name, grid=(c // cb, nt), in_specs=in_specs,
        out_specs=[pl.BlockSpec((tb, cb), lambda j, i: (i, j)), pl.BlockSpec((8, cb), lambda j, i: (0, j)),
                   pl.BlockSpec((1, cb), lambda j, i: (0, j))],
        out_shape=[jax.ShapeDtypeStruct((t, c), bf16), jax.ShapeDtypeStruct((8, c), f32), jax.ShapeDtypeStruct((1, c), f32)],
        compiler_params=_cparams(("parallel", "arbitrary")))(*args)


def dt_fwd(u_dt, bias_row, tb=1024):
    t = u_dt.shape[0]

    def body(u_ref, b_ref, o_ref):
        v = u_ref[...] + b_ref[...]
        sp = jnp.maximum(v, 0.0) + jnp.log(1.0 + jnp.exp(-jnp.abs(v)))
        lane = lax.broadcasted_iota(jnp.int32, v.shape, 1)
        o_ref[...] = jnp.where((lane & 127) < SSD_HEADS, sp, 0.0)

    return pl.pallas_call(
        body, name="dt_fwd", grid=(t // tb,),
        in_specs=[pl.BlockSpec((tb, 256), lambda i: (i, 0)), pl.BlockSpec((1, 256), lambda i: (0, 0))],
        out_specs=pl.BlockSpec((tb, 256), lambda i: (i, 0)), out_shape=jax.ShapeDtypeStruct((t, 256), f32),
        compiler_params=_cparams(("parallel",)))(u_dt, bias_row)


def dt_bwd(ddt_f, ddt_b, u_dt, bias_row, tb=1024):
    t = u_dt.shape[0]

    def body(gf_ref, gb_ref, u_ref, b_ref, du_ref, db_ref):
        g = jnp.concatenate([jnp.sum(gf_ref[...], axis=0), jnp.sum(gb_ref[...], axis=0)], axis=1)
        du = g * _sigmoid(u_ref[...] + b_ref[...])
        du_ref[...] = du.astype(du_ref.dtype)

        @pl.when(pl.program_id(0) == 0)
        def _():
            db_ref[...] = jnp.zeros_like(db_ref)

        db_ref[...] += jnp.sum(du, axis=0, keepdims=True)

    return pl.pallas_call(
        body, name="dt_bwd", grid=(t // tb,),
        in_specs=[pl.BlockSpec((4, tb, 128), lambda i: (0, i, 0)), pl.BlockSpec((4, tb, 128), lambda i: (0, i, 0)),
                  pl.BlockSpec((tb, 256), lambda i: (i, 0)), pl.BlockSpec((1, 256), lambda i: (0, 0))],
        out_specs=[pl.BlockSpec((tb, 256), lambda i: (i, 0)), pl.BlockSpec((1, 256), lambda i: (0, 0))],
        out_shape=[jax.ShapeDtypeStruct((t, 256), bf16), jax.ShapeDtypeStruct((1, 256), f32)],
        compiler_params=_cparams(("arbitrary",)))(ddt_f, ddt_b, u_dt, bias_row)


def _ssd_common(dt_blk, a_row, reverse):
    row = lax.broadcasted_iota(jnp.int32, (CHUNK, CHUNK), 0)
    col = lax.broadcasted_iota(jnp.int32, (CHUNK, CHUNK), 1)
    mask = (row <= col) if reverse else (row >= col)
    tri = mask.astype(f32)
    a = dt_blk * a_row
    acs = _dot_exact(tri, a)
    atot = jnp.sum(a, axis=0, keepdims=True)
    return mask, tri, a, acs, atot, col


def _lane_col(mat, lane_idx, h):
    return jnp.sum(jnp.where(lane_idx == h, mat, 0.0), axis=1, keepdims=True)


def ssd_fwd(xbc_c, dt2, a_rows, reverse, name):
    t = xbc_c.shape[0]
    nc = t // CHUNK
    d_off = 1 if reverse else 0

    def cidx(c):
        return nc - 1 - c if reverse else c

    def body(x_ref, b_ref, c_ref, dt_ref, a_ref, y_ref, hp_ref, h_scr, acst_scr):
        g = pl.program_id(0)
        c = pl.program_id(1)

        @pl.when(c == 0)
        def _():
            h_scr[...] = jnp.zeros_like(h_scr)

        dt_blk = dt_ref[...]
        mask, tri, a, acs, atot, lane = _ssd_common(dt_blk, a_ref[...], reverse)
        acst_scr[...] = acs.T
        bm = b_ref[...]
        cm = c_ref[...]
        cb = _dot_nt(cm, bm)
        half = lane >= SSD_HEAD_DIM
        sub_half = lax.broadcasted_iota(jnp.int32, (CHUNK, 1), 0) >= SSD_HEAD_DIM
        for j in range(4):
            x = x_ref[:, 128 * j:128 * (j + 1)]
            cols, dts, tots = [], [], []
            y = None
            for e in range(2):
                h = 8 * g + 2 * j + e
                col_h = _lane_col(acs, lane, h)
                row_h = acst_scr[pl.ds(h, 1), :]
                dt_h = _lane_col(dt_blk, lane, h)
                lmat = jnp.where(mask, jnp.exp(jnp.where(mask, col_h - row_h, 0.0)), 0.0)
                xdt_e = jnp.where(half == (e == 1), x * dt_h, 0.0)
                ye = _dot(cb * lmat, xdt_e)
                y = ye if y is None else y + ye
                cols.append(col_h)
                dts.append(dt_h)
                tots.append(jnp.sum(jnp.where(lane[0:1] == h, atot, 0.0), axis=1, keepdims=True))
            hp = h_scr[j]
            hp_ref[0, j] = hp
            ecol = jnp.where(half, jnp.exp(cols[1]), jnp.exp(cols[0]))
            y = y + _dot_nt(cm, hp) * ecol
            y_ref[:, 128 * j:128 * (j + 1)] = y
            dec = jnp.where(half, jnp.exp(tots[1] - cols[1]), jnp.exp(tots[0] - cols[0]))
            xdt = x * jnp.where(half, dts[1], dts[0])
            s_new = _dot_tn(xdt * dec, bm)
            cd = jnp.where(sub_half, jnp.exp(tots[1]), jnp.exp(tots[0]))
            h_scr[j] = cd * hp + s_new

    return pl.pallas_call(
        body, name=name, grid=(SSD_GROUPS, nc),
        in_specs=[pl.BlockSpec((CHUNK, 512), lambda g, c: (cidx(c), g)),
                  pl.BlockSpec((CHUNK, 128), lambda g, c: (cidx(c), 16 + g)),
                  pl.BlockSpec((CHUNK, 128), lambda g, c: (cidx(c), 20 + g)),
                  pl.BlockSpec((CHUNK, 128), lambda g, c: (cidx(c), d_off)),
                  pl.BlockSpec((1, 128), lambda g, c: (0, d_off))],
        out_specs=[pl.BlockSpec((CHUNK, 512), lambda g, c: (cidx(c), g)),
                   pl.BlockSpec((1, 4, 128, 128), lambda g, c: (cidx(c), g, 0, 0))],
        out_shape=[jax.ShapeDtypeStruct((t, D_INNER), f32), jax.ShapeDtypeStruct((nc, 16, 128, 128), f32)],
        scratch_shapes=[pltpu.VMEM((4, 128, 128), f32), pltpu.VMEM((CHUNK, CHUNK), f32)],
        compiler_params=_cparams(("parallel", "arbitrary")))(xbc_c, xbc_c, xbc_c, dt2, a_rows)


def ssd_bwd(xbc_c, dt2, a_rows, dy, hprev, reverse, name):
    t = xbc_c.shape[0]
    nc = t // CHUNK
    d_off = 1 if reverse else 0

    def cidx(c):
        return c if reverse else nc - 1 - c

    def body(x_ref, b_ref, c_ref, dt_ref, a_ref, dy_ref, hp_ref, dx_ref, db_ref, dc_ref, ddt_ref, da_ref,
             dh_scr, acst_scr):
        g = pl.program_id(0)
        c = pl.program_id(1)

        @pl.when(c == 0)
        def _():
            dh_scr[...] = jnp.zeros_like(dh_scr)
            da_ref[...] = jnp.zeros_like(da_ref)

        dt_blk = dt_ref[...]
        a_row = a_ref[...]
        mask, tri, a, acs, atot, lane = _ssd_common(dt_blk, a_row, reverse)
        acst_scr[...] = acs.T
        sub = lax.broadcasted_iota(jnp.int32, (CHUNK, CHUNK), 0)
        bm = b_ref[...]
        cm = c_ref[...]
        cb = _dot_nt(cm, bm)
        half = lane >= SSD_HEAD_DIM
        sub_half = sub[:, 0:1] >= SSD_HEAD_DIM
        dcb = jnp.zeros((CHUNK, CHUNK), f32)
        dacs = jnp.zeros((CHUNK, CHUNK), f32)
        dacs_t = jnp.zeros((CHUNK, CHUNK), f32)
        dtot = jnp.zeros((1, CHUNK), f32)
        ddt_x = jnp.zeros((CHUNK, CHUNK), f32)
        dbm = jnp.zeros((CHUNK, D_STATE), f32)
        dcm = jnp.zeros((CHUNK, D_STATE), f32)
        for j in range(4):
            x = x_ref[:, 128 * j:128 * (j + 1)]
            dyp = dy_ref[:, 128 * j:128 * (j + 1)]
            hp = hp_ref[0, j]
            dhn = dh_scr[j]
            cols, dts, tots, hs = [], [], [], []
            dxdt = None
            for e in range(2):
                h = 8 * g + 2 * j + e
                sel = half == (e == 1)
                col_h = _lane_col(acs, lane, h)
                row_h = acst_scr[pl.ds(h, 1), :]
                dt_h = _lane_col(dt_blk, lane, h)
                lmat = jnp.where(mask, jnp.exp(jnp.where(mask, col_h - row_h, 0.0)), 0.0)
                xdt_e = jnp.where(sel, x * dt_h, 0.0)
                dy_e = jnp.where(sel, dyp, 0.0)
                ml = _dot_nt(dy_e, xdt_e) * lmat
                dcb = dcb + ml
                w = ml * cb
                dacs = dacs + jnp.where(lane == h, jnp.sum(w, axis=1, keepdims=True), 0.0)
                dacs_t = dacs_t - jnp.where(sub == h, jnp.sum(w, axis=0, keepdims=True), 0.0)
                de = _dot_tn(cb * lmat, dy_e)
                dxdt = de if dxdt is None else dxdt + de
                cols.append(col_h)
                dts.append(dt_h)
                tots.append(jnp.sum(jnp.where(lane[0:1] == h, atot, 0.0), axis=1, keepdims=True))
                hs.append(h)
            ecol = jnp.where(half, jnp.exp(cols[1]), jnp.exp(cols[0]))
            dec = jnp.where(half, jnp.exp(tots[1] - cols[1]), jnp.exp(tots[0] - cols[0]))
            cd = jnp.where(sub_half, jnp.exp(tots[1]), jnp.exp(tots[0]))
            dtp = jnp.where(half, dts[1], dts[0])
            xdt = x * dtp
            yoff = _dot_nt(cm, hp) * ecol
            dye = dyp * ecol
            dcm = dcm + _dot(dye, hp)
            dhp = _dot_tn(dye, cm)
            gmat = _dot_nt(bm, dhn)
            dxdt = dxdt + dec * gmat
            dbm = dbm + _dot(xdt * dec, dhn)
            r_off = dyp * yoff
            r_dec = xdt * gmat * dec
            r_x = dxdt * x
            hh = dhn * hp
            for e in range(2):
                sel = half == (e == 1)
                h = hs[e]
                s_off = jnp.sum(jnp.where(sel, r_off, 0.0), axis=1, keepdims=True)
                s_dec = jnp.sum(jnp.where(sel, r_dec, 0.0), axis=1, keepdims=True)
                dacs = dacs + jnp.where(lane == h, s_off - s_dec, 0.0)
                dcd = jnp.sum(jnp.sum(jnp.where(sub_half == (e == 1), hh, 0.0), axis=1, keepdims=True), axis=0, keepdims=True)
                tot_e = jnp.sum(s_dec, axis=0, keepdims=True) + jnp.exp(tots[e]) * dcd
                dtot = dtot + jnp.where(lane[0:1] == h, tot_e, 0.0)
                ddt_x = ddt_x + jnp.where(lane == h, jnp.sum(jnp.where(sel, r_x, 0.0), axis=1, keepdims=True), 0.0)
            dx_ref[:, 128 * j:128 * (j + 1)] = dxdt * dtp
            dh_scr[j] = cd * dhn + dhp
        dcm = dcm + _dot(dcb, bm)
        dbm = dbm + _dot_tn(dcb, cm)
        db_ref[...] = dbm
        dc_ref[...] = dcm
        dacs = dacs + dacs_t.T
        da = _dot_exact(tri.T, dacs) + dtot
        ddt_ref[0] = da * a_row + ddt_x
        da_ref[0] += jnp.sum(da * dt_blk, axis=0, keepdims=True)

    return pl.pallas_call(
        body, name=name, grid=(SSD_GROUPS, nc),
        in_specs=[pl.BlockSpec((CHUNK, 512), lambda g, c: (cidx(c), g)),
                  pl.BlockSpec((CHUNK, 128), lambda g, c: (cidx(c), 16 + g)),
                  pl.BlockSpec((CHUNK, 128), lambda g, c: (cidx(c), 20 + g)),
                  pl.BlockSpec((CHUNK, 128), lambda g, c: (cidx(c), d_off)),
                  pl.BlockSpec((1, 128), lambda g, c: (0, d_off)),
                  pl.BlockSpec((CHUNK, 512), lambda g, c: (cidx(c), g)),
                  pl.BlockSpec((1, 4, 128, 128), lambda g, c: (cidx(c), g, 0, 0))],
        out_specs=[pl.BlockSpec((CHUNK, 512), lambda g, c: (cidx(c), g)),
                   pl.BlockSpec((CHUNK, 128), lambda g, c: (cidx(c), g)),
                   pl.BlockSpec((CHUNK, 128), lambda g, c: (cidx(c), g)),
                   pl.BlockSpec((1, CHUNK, 128), lambda g, c: (g, cidx(c), 0)),
                   pl.BlockSpec((1, 1, 128), lambda g, c: (g, 0, 0))],
        out_shape=[jax.ShapeDtypeStruct((t, D_INNER), f32), jax.ShapeDtypeStruct((t, 512), f32),
                   jax.ShapeDtypeStruct((t, 512), f32), jax.ShapeDtypeStruct((4, t, 128), f32),
                   jax.ShapeDtypeStruct((4, 1, 128), f32)],
        scratch_shapes=[pltpu.VMEM((4, 128, 128), f32), pltpu.VMEM((CHUNK, CHUNK), f32)],
        compiler_params=_cparams(("parallel", "arbitrary")))(xbc_c, xbc_c, xbc_c, dt2, a_rows, dy, hprev)


def tail_fwd(y_f, y_b, xbc_c, z, dskip_row, nw_row, tb=512):
    t = y_f.shape[0]

    def body(yf_ref, yb_ref, x_ref, z_ref, d_ref, w_ref, o_ref):
        zz = z_ref[...]
        y = (yf_ref[...] + yb_ref[...] + d_ref[...] * x_ref[...]) * (zz * _sigmoid(zz))
        rstd = lax.rsqrt(jnp.mean(y * y, axis=1, keepdims=True) + NORM_EPS)
        o_ref[...] = (y * rstd * w_ref[...]).astype(o_ref.dtype)

    blk = pl.BlockSpec((tb, 512), lambda i, g: (i, g))
    row = pl.BlockSpec((1, 512), lambda i, g: (0, g))
    return pl.pallas_call(
        body, name="tail_fwd", grid=(t // tb, SSD_GROUPS), in_specs=[blk, blk, blk, blk, row, row], out_specs=blk,
        out_shape=jax.ShapeDtypeStruct((t, D_INNER), bf16),
        compiler_params=_cparams(("parallel", "parallel")))(y_f, y_b, xbc_c, z, dskip_row, nw_row)


def tail_bwd(dyn, y_f, y_b, xbc_c, z, dskip_row, nw_row, tb=512):
    t = y_f.shape[0]

    def body(g_ref, yf_ref, yb_ref, x_ref, z_ref, d_ref, w_ref, dy_ref, dz_ref, dw_ref, dd_ref):
        zz = z_ref[...]
        sg = _sigmoid(zz)
        sl = zz * sg
        x = x_ref[...]
        y = yf_ref[...] + yb_ref[...] + d_ref[...] * x
        yz = y * sl
        rstd = lax.rsqrt(jnp.mean(yz * yz, axis=1, keepdims=True) + NORM_EPS)
        yhat = yz * rstd
        g = g_ref[...]
        dyhat = g * w_ref[...]
        dyz = rstd * (dyhat - yhat * jnp.mean(dyhat * yhat, axis=1, keepdims=True))
        dy = dyz * sl
        dy_ref[...] = dy
        dz_ref[...] = (dyz * y * sg * (1.0 + zz * (1.0 - sg))).astype(dz_ref.dtype)

        @pl.when(pl.program_id(1) == 0)
        def _():
            dw_ref[...] = jnp.zeros_like(dw_ref)
            dd_ref[...] = jnp.zeros_like(dd_ref)

        dw_ref[...] += jnp.sum(g * yhat, axis=0, keepdims=True)
        dd_ref[...] += jnp.sum(dy * x, axis=0, keepdims=True)

    blk = pl.BlockSpec((tb, 512), lambda g, i: (i, g))
    row = pl.BlockSpec((1, 512), lambda g, i: (0, g))
    return pl.pallas_call(
        body, name="tail_bwd", grid=(SSD_GROUPS, t // tb), in_specs=[blk, blk, blk, blk, blk, row, row],
        out_specs=[blk, blk, row, row],
        out_shape=[jax.ShapeDtypeStruct((t, D_INNER), f32), jax.ShapeDtypeStruct((t, D_INNER), bf16),
                   jax.ShapeDtypeStruct((1, D_INNER), f32), jax.ShapeDtypeStruct((1, D_INNER), f32)],
        compiler_params=_cparams(("parallel", "arbitrary")))(dyn, y_f, y_b, xbc_c, z, dskip_row, nw_row)


def _slopes(p):
    return [2.0 ** (-8.0 * (HEADS_PER_PATTERN * p + j + 1) / ATTN_HEADS) for j in range(HEADS_PER_PATTERN)]


def _win_specs(nq, col_of):
    return [pl.BlockSpec((64, 256), lambda r, i: (jnp.maximum(2 * i - 1, 0), col_of(r))),
            pl.BlockSpec((128, 256), lambda r, i: (i, col_of(r))),
            pl.BlockSpec((64, 256), lambda r, i: (jnp.minimum(2 * i + 2, 2 * nq - 1), col_of(r)))]


def _lane_head(shape):
    return lax.broadcasted_iota(jnp.int32, shape, 1) >> 6


def _stack_heads(m):
    lane_head = _lane_head(m.shape)
    return jnp.concatenate([jnp.where(lane_head == j, m, 0.0) for j in range(HEADS_PER_PATTERN)], axis=0)


def _unstack_heads(m4, n):
    lane_head = _lane_head((n, 256))
    out = jnp.where(lane_head == 0, m4[0:n], 0.0)
    for j in range(1, HEADS_PER_PATTERN):
        out = out + jnp.where(lane_head == j, m4[j * n:(j + 1) * n], 0.0)
    return out


def _head_cols(m, n):
    lane = lax.broadcasted_iota(jnp.int32, (n, 256), 1)
    return jnp.concatenate([jnp.sum(jnp.where(lane == ATTN_HEAD_DIM * j, m, 0.0), axis=1, keepdims=True)
                            for j in range(HEADS_PER_PATTERN)], axis=0)


def _score_bias(p, dil, by_key):
    slopes = np.asarray(_slopes(p), np.float32)
    if by_key:
        win = np.arange(256)[:, None]
        rel = np.arange(128)[None, :] - (win - 64)
    else:
        win = np.arange(256)[None, :]
        rel = win - 64 - np.arange(128)[:, None]
    band = np.abs(rel) <= 64
    out = []
    for first, last in ((False, False), (True, False), (False, True), (True, True)):
        ok = band & ~(first & (win < 64)) & ~(last & (win >= 192))
        pen = -slopes[:, None, None] * (np.abs(rel) * dil).astype(np.float32)[None]
        out.append(np.where(ok[None], pen, np.float32(NEG_BIG)).reshape(-1, rel.shape[1]))
    return jnp.asarray(np.stack(out), f32)


def _bias_spec(nq, rows, cols):
    return pl.BlockSpec((1, rows, cols), lambda r, i: ((i == 0).astype(jnp.int32) + 2 * (i == nq - 1).astype(jnp.int32), 0, 0))


def attn_fwd(q, k, v, p, dil, name):
    l = q.shape[0]
    nq = l // 128

    def body(q_ref, kp_ref, ko_ref, kn_ref, vp_ref, vo_ref, vn_ref, bias_ref, o_ref, lse_ref):
        kcat = jnp.concatenate([kp_ref[...], ko_ref[...], kn_ref[...]], axis=0)
        vcat = jnp.concatenate([vp_ref[...], vo_ref[...], vn_ref[...]], axis=0)
        s = _dot_nt(_stack_heads(q_ref[...] * 0.125), kcat) + bias_ref[0]
        m = jnp.max(s, axis=1, keepdims=True)
        pr = jnp.exp(s - m)
        den = jnp.sum(pr, axis=1, keepdims=True)
        o4 = _dot(pr, vcat) / den
        o_ref[...] = _unstack_heads(o4, 128)
        lse_ref[...] = _unstack_heads(jnp.broadcast_to(m + jnp.log(den), (512, 256)), 128)

    col = lambda r: r
    return pl.pallas_call(
        body, name=name, grid=(dil, nq),
        in_specs=[pl.BlockSpec((128, 256), lambda r, i: (i, r))] + _win_specs(nq, col) + _win_specs(nq, col)
        + [_bias_spec(nq, 512, 256)],
        out_specs=[pl.BlockSpec((128, 256), lambda r, i: (i, r))] * 2,
        out_shape=[jax.ShapeDtypeStruct(q.shape, f32)] * 2,
        compiler_params=_cparams(("parallel", "parallel")))(q, k, k, k, v, v, v, _score_bias(p, dil, False))


def attn_combine(os_, lses, tb=1024):
    t = os_[0].shape[0]

    def body(o0, o1, o2, l0, l1, l2, y_ref, lse_ref):
        a0, a1, a2 = l0[...], l1[...], l2[...]
        m = jnp.maximum(jnp.maximum(a0, a1), a2)
        e0, e1, e2 = jnp.exp(a0 - m), jnp.exp(a1 - m), jnp.exp(a2 - m)
        den = e0 + e1 + e2
        y_ref[...] = (e0 * o0[...] + e1 * o1[...] + e2 * o2[...]) / den
        lse_ref[...] = m + jnp.log(den)

    blk = pl.BlockSpec((tb, 256), lambda i: (i, 0))
    return pl.pallas_call(
        body, name="attn_combine", grid=(t // tb,), in_specs=[blk] * 6, out_specs=[blk, blk],
        out_shape=[jax.ShapeDtypeStruct((t, 256), f32)] * 2,
        compiler_params=_cparams(("parallel",)))(*os_, *lses)


def attn_delta(dy, y, tb=1024):
    t = dy.shape[0]

    def body(dy_ref, y_ref, d_ref):
        pr = dy_ref[...] * y_ref[...]
        lane_head = _lane_head(pr.shape)
        out = jnp.zeros_like(pr)
        for j in range(HEADS_PER_PATTERN):
            sj = jnp.sum(jnp.where(lane_head == j, pr, 0.0), axis=1, keepdims=True)
            out = out + jnp.where(lane_head == j, sj, 0.0)
        d_ref[...] = out

    blk = pl.BlockSpec((tb, 256), lambda i: (i, 0))
    return pl.pallas_call(body, name="attn_delta", grid=(t // tb,), in_specs=[blk, blk], out_specs=blk,
                          out_shape=jax.ShapeDtypeStruct((t, 256), f32),
                          compiler_params=_cparams(("parallel",)))(dy, y)


def attn_dq(q, k, v, dy, lse, delta, p, dil, name):
    l = q.shape[0]
    nq = l // 128

    def body(q_ref, kp_ref, ko_ref, kn_ref, vp_ref, vo_ref, vn_ref, dy_ref, lse_ref, d_ref, bias_ref, dq_ref):
        kcat = jnp.concatenate([kp_ref[...], ko_ref[...], kn_ref[...]], axis=0)
        vcat = jnp.concatenate([vp_ref[...], vo_ref[...], vn_ref[...]], axis=0)
        s = _dot_nt(_stack_heads(q_ref[...] * 0.125), kcat) + bias_ref[0]
        pr = jnp.exp(s - _head_cols(lse_ref[...], 128))
        dp = _dot_nt(_stack_heads(dy_ref[...]), vcat)
        ds = pr * (dp - _head_cols(d_ref[...], 128))
        dq_ref[...] = (_unstack_heads(_dot(ds, kcat), 128) * 0.125).astype(dq_ref.dtype)

    col = lambda r: r
    own = pl.BlockSpec((128, 256), lambda r, i: (i, r))
    return pl.pallas_call(
        body, name=name, grid=(dil, nq),
        in_specs=[own] + _win_specs(nq, col) + _win_specs(nq, col) + [own, own, own, _bias_spec(nq, 512, 256)],
        out_specs=own, out_shape=jax.ShapeDtypeStruct(q.shape, bf16),
        compiler_params=_cparams(("parallel", "parallel")))(q, k, k, k, v, v, v, dy, lse, delta, _score_bias(p, dil, False))


def attn_dkv(q, k, v, dy, lse, delta, p, dil, name):
    l = q.shape[0]
    nq = l // 128

    def body(qp_ref, qo_ref, qn_ref, gp_ref, go_ref, gn_ref, lp_ref, lo_ref, ln_ref, dp_ref, do_ref, dn_ref,
             k_ref, v_ref, bias_ref, dk_ref, dv_ref):
        cat = lambda a, b, c: jnp.concatenate([a[...], b[...], c[...]], axis=0)
        q4 = _stack_heads(cat(qp_ref, qo_ref, qn_ref) * 0.125)
        dy4 = _stack_heads(cat(gp_ref, go_ref, gn_ref))
        lse4 = _head_cols(cat(lp_ref, lo_ref, ln_ref), 256)
        del4 = _head_cols(cat(dp_ref, do_ref, dn_ref), 256)
        s = _dot_nt(q4, k_ref[...]) + bias_ref[0]
        pr = jnp.exp(s - lse4)
        dpm = _dot_nt(dy4, v_ref[...])
        ds = pr * (dpm - del4)
        dv_ref[...] = _dot_tn(pr, dy4).astype(dv_ref.dtype)
        dk_ref[...] = _dot_tn(ds, q4).astype(dk_ref.dtype)

    col = lambda r: r
    own = pl.BlockSpec((128, 256), lambda r, i: (i, r))
    win = _win_specs(nq, col)
    return pl.pallas_call(
        body, name=name, grid=(dil, nq), in_specs=win * 4 + [own, own, _bias_spec(nq, 1024, 128)], out_specs=[own, own],
        out_shape=[jax.ShapeDtypeStruct(q.shape, bf16)] * 2,
        compiler_params=_cparams(("parallel", "parallel")))(q, q, q, dy, dy, dy, lse, lse, lse, delta, delta, delta, k, v,
                                                            _score_bias(p, dil, True))


def _lanes(v, reps):
    return v if reps == 1 else jnp.tile(v, (1, reps))


def _lane_halo_specs(cb, tb, nt, off=0):
    r = tb // 128
    return [pl.BlockSpec((cb, 128), lambda j, i: (j + off, jnp.maximum(i * r - 1, 0))),
            pl.BlockSpec((cb, tb), lambda j, i: (j + off, i)),
            pl.BlockSpec((cb, 128), lambda j, i: (j + off, jnp.minimum((i + 1) * r, nt * r - 1)))]


def _with_lane_halo(prev_ref, own_ref, next_ref, i, nt):
    prev = jnp.where(i > 0, prev_ref[...].astype(f32), 0.0)
    nxt = jnp.where(i < nt - 1, next_ref[...].astype(f32), 0.0)
    return jnp.concatenate([prev, own_ref[...].astype(f32), nxt], axis=1)


def _lane_shifted(xcat, s, tb):
    n = xcat.shape[1]
    return pltpu.roll(xcat, (-s) % n, 1)[:, 128:128 + tb]


def conv_fwd_t(xbc_t, w_b, b_b, tb=1024, cb=256):
    c, t = xbc_t.shape
    nt = t // tb

    def body(prev_ref, own_ref, next_ref, w_ref, b_ref, o_ref):
        i = pl.program_id(1)
        xcat = _with_lane_halo(prev_ref, own_ref, next_ref, i, nt)
        reps = tb // 128
        pre = _lanes(b_ref[...], reps)
        for k in range(D_CONV):
            pre = pre + _lanes(w_ref[k], reps) * _lane_shifted(xcat, k - 2, tb)
        o_ref[...] = pre * _sigmoid(pre)

    return pl.pallas_call(
        body, name="conv_fwd", grid=(c // cb, nt),
        in_specs=_lane_halo_specs(cb, tb, nt) + [pl.BlockSpec((D_CONV, cb, 128), lambda j, i: (0, j, 0)),
                                                 pl.BlockSpec((cb, 128), lambda j, i: (j, 0))],
        out_specs=pl.BlockSpec((cb, tb), lambda j, i: (j, i)), out_shape=jax.ShapeDtypeStruct((c, t), f32),
        compiler_params=_cparams(("parallel", "parallel")))(xbc_t, xbc_t, xbc_t, w_b, b_b)


def conv_bwd_t(xbc_t, grad_t, w_b, b_b, into, name, row0, tb=1024, cb=256):
    c, t = grad_t.shape
    nt = t // tb
    off = row0 // cb
    reps = tb // 128

    def body(*refs):
        i = pl.program_id(1)
        xr, gr = refs[0:3], refs[3:6]
        w_ref, b_ref = refs[6:8]
        dx_ref, dw_ref, db_ref = refs[-3:]
        xcat = _with_lane_halo(*xr, i, nt)
        gcat = _with_lane_halo(*gr, i, nt)
        n = tb + 256
        wk = [_lanes(w_ref[k], reps + 2) for k in range(D_CONV)]
        pre = _lanes(b_ref[...], reps + 2)
        for k in range(D_CONV):
            pre = pre + wk[k] * pltpu.roll(xcat, (2 - k) % n, 1)
        sg = _sigmoid(pre)
        dpre = gcat * sg * (1.0 + pre * (1.0 - sg))
        dx = None
        for k in range(D_CONV):
            term = wk[k][:, 128:128 + tb] * _lane_shifted(dpre, 2 - k, tb)
            dx = term if dx is None else dx + term
        dx_ref[...] = dx.astype(dx_ref.dtype)
        dp_own = dpre[:, 128:128 + tb]

        def fold(v):
            s = v[:, 0:128]
            for q in range(1, reps):
                s = s + v[:, 128 * q:128 * (q + 1)]
            return s

        @pl.when(i == 0)
        def _():
            dw_ref[...] = jnp.zeros_like(dw_ref)
            db_ref[...] = jnp.zeros_like(db_ref)

        for k in range(D_CONV):
            dw_ref[k] += fold(dp_own * _lane_shifted(xcat, k - 2, tb))
        db_ref[...] += fold(dp_own)

    in_specs = (_lane_halo_specs(cb, tb, nt, off) + _lane_halo_specs(cb, tb, nt)
                + [pl.BlockSpec((D_CONV, cb, 128), lambda j, i: (0, j + off, 0)), pl.BlockSpec((cb, 128), lambda j, i: (j + off, 0))])
    args = [xbc_t] * 3 + [grad_t] * 3 + [w_b, b_b]
    aliases = {}
    if into is not None:
        in_specs.append(pl.BlockSpec(memory_space=pl.ANY))
        args.append(into)
        aliases = {len(args) - 1: 0}
    return pl.pallas_call(
        body, name=name, grid=(c // cb, nt), in_specs=in_specs,
        out_specs=[pl.BlockSpec((cb, tb), lambda j, i: (j + off, i)), pl.BlockSpec((D_CONV, cb, 128), lambda j, i: (0, j, 0)),
                   pl.BlockSpec((cb, 128), lambda j, i: (j, 0))],
        out_shape=[jax.ShapeDtypeStruct((CONV_DIM, t), bf16), jax.ShapeDtypeStruct((D_CONV, c, 128), f32),
                   jax.ShapeDtypeStruct((c, 128), f32)],
        input_output_aliases=aliases, compiler_params=_cparams(("parallel", "arbitrary")))(*args)


def dt_fwd_t(u_dt_t, bias_b, tb=2048):
    r, t = u_dt_t.shape

    def body(u_ref, b_ref, o_ref):
        v = u_ref[...] + _lanes(b_ref[...], tb // 128)
        o_ref[...] = jnp.maximum(v, 0.0) + jnp.log(1.0 + jnp.exp(-jnp.abs(v)))

    return pl.pallas_call(
        body, name="dt_fwd", grid=(t // tb,),
        in_specs=[pl.BlockSpec((r, tb), lambda i: (0, i)), pl.BlockSpec((r, 128), lambda i: (0, 0))],
        out_specs=pl.BlockSpec((r, tb), lambda i: (0, i)), out_shape=jax.ShapeDtypeStruct((r, t), f32),
        compiler_params=_cparams(("parallel",)))(u_dt_t, bias_b)


def dt_bwd_t(ddt_f, ddt_b, u_dt_t, bias_b, tb=2048):
    r, t = u_dt_t.shape
    reps = tb // 128

    def body(gf_ref, gb_ref, u_ref, b_ref, du_ref, db_ref):
        g = jnp.concatenate([gf_ref[...], gb_ref[...]], axis=0)
        du = g * _sigmoid(u_ref[...] + _lanes(b_ref[...], reps))
        du_ref[...] = du.astype(du_ref.dtype)

        @pl.when(pl.program_id(0) == 0)
        def _():
            db_ref[...] = jnp.zeros_like(db_ref)

        s = du[:, 0:128]
        for q in range(1, reps):
            s = s + du[:, 128 * q:128 * (q + 1)]
        db_ref[...] += s

    half = pl.BlockSpec((r // 2, tb), lambda i: (0, i))
    return pl.pallas_call(
        body, name="dt_bwd", grid=(t // tb,),
        in_specs=[half, half, pl.BlockSpec((r, tb), lambda i: (0, i)), pl.BlockSpec((r, 128), lambda i: (0, 0))],
        out_specs=[pl.BlockSpec((r, tb), lambda i: (0, i)), pl.BlockSpec((r, 128), lambda i: (0, 0))],
        out_shape=[jax.ShapeDtypeStruct((r, t), bf16), jax.ShapeDtypeStruct((r, 128), f32)],
        compiler_params=_cparams(("arbitrary",)))(ddt_f, ddt_b, u_dt_t, bias_b)


HEADS_PER_GROUP = SSD_HEADS // SSD_GROUPS


def _group_rows(g, n):
    return pl.ds(pl.multiple_of(g * n, n), n)


def _ssd_decays(dt_blk, a_blk, reverse):
    row = lax.broadcasted_iota(jnp.int32, (CHUNK, CHUNK), 0)
    col = lax.broadcasted_iota(jnp.int32, (CHUNK, CHUNK), 1)
    mask = (row <= col) if reverse else (row >= col)
    tri = mask.astype(f32)
    a8 = dt_blk * a_blk
    a = jnp.concatenate([a8, jnp.zeros((CHUNK - HEADS_PER_GROUP, CHUNK), f32)], axis=0).T
    acs = _dot_exact(tri, a)
    return mask, tri, a8, acs, acs.T, col


def ssd_fwd_t(xbc_ct, dt_t, a_b, reverse, name, prev=None):
    t = xbc_ct.shape[1]
    nc = t // CHUNK
    direction = 1 if reverse else 0

    def cidx(c):
        return nc - 1 - c if reverse else c

    def body(*refs):
        x_ref, b_ref, c_ref, dt_ref, a_ref = refs[0:5]
        prev_ref = refs[5] if prev is not None else None
        y_ref, hp_ref, h_scr = refs[-3:]

        @pl.when(pl.program_id(0) == 0)
        def _():
            h_scr[...] = jnp.zeros_like(h_scr)

        def group(g, carry):
            x_v, y_v = x_ref.at[_group_rows(g, 512)], y_ref.at[_group_rows(g, 512)]
            heads = _group_rows(g, HEADS_PER_GROUP)
            hp_v, h_v = hp_ref.at[0, heads], h_scr.at[heads]
            dt_blk = dt_ref[heads, :]
            mask, tri, a8, acs, acs_t, lane = _ssd_decays(dt_blk, a_ref[heads, :], reverse)
            bm = b_ref[_group_rows(g, 128), :].T
            cm = c_ref[_group_rows(g, 128), :].T
            cb = _dot_nt(cm, bm)
            tot = jnp.sum(a8, axis=1, keepdims=True)
            for j in range(HEADS_PER_GROUP):
                rows = slice(SSD_HEAD_DIM * j, SSD_HEAD_DIM * (j + 1))
                col_j = _lane_col(acs, lane, j)
                row_j = acs_t[j:j + 1, :]
                lmat = jnp.where(mask, jnp.exp(jnp.where(mask, col_j - row_j, 0.0)), 0.0)
                xdt = x_v[rows, :] * dt_blk[j:j + 1, :]
                hp = h_v[j]
                hp_v[j] = hp
                y = _dot_nt(xdt, cb * lmat) + _dot_nt(hp, cm) * jnp.exp(row_j)
                if prev_ref is not None:
                    y = y + prev_ref.at[_group_rows(g, 512)][rows, :]
                y_v[rows, :] = y
                tot_j = tot[j:j + 1, :]
                h_v[j] = jnp.exp(tot_j) * hp + _dot(xdt * jnp.exp(tot_j - row_j), bm)
            return carry

        lax.fori_loop(0, SSD_GROUPS, group, 0)

    big = pl.BlockSpec((D_INNER, CHUNK), lambda c: (0, cidx(c)))
    in_specs = [big, pl.BlockSpec((512, CHUNK), lambda c: (4, cidx(c))), pl.BlockSpec((512, CHUNK), lambda c: (5, cidx(c))),
                pl.BlockSpec((SSD_HEADS, CHUNK), lambda c: (direction, cidx(c))),
                pl.BlockSpec((SSD_HEADS, 128), lambda c: (direction, 0))]
    args = [xbc_ct, xbc_ct, xbc_ct, dt_t, a_b]
    if prev is not None:
        in_specs.append(big)
        args.append(prev)
    return pl.pallas_call(
        body, name=name, grid=(nc,), in_specs=in_specs,
        out_specs=[big, pl.BlockSpec((1, SSD_HEADS, SSD_HEAD_DIM, D_STATE), lambda c: (cidx(c), 0, 0, 0))],
        out_shape=[jax.ShapeDtypeStruct((D_INNER, t), f32), jax.ShapeDtypeStruct((nc, SSD_HEADS, SSD_HEAD_DIM, D_STATE), f32)],
        scratch_shapes=[pltpu.VMEM((SSD_HEADS, SSD_HEAD_DIM, D_STATE), f32)],
        compiler_params=_cparams(("arbitrary",)))(*args)


def ssd_bwd_t(xbc_ct, dt_t, a_b, dy_t, hprev, reverse, name, skip_b=None, prev=None):
    t = xbc_ct.shape[1]
    nc = t // CHUNK
    direction = 1 if reverse else 0

    def cidx(c):
        return c if reverse else nc - 1 - c

    def body(*refs):
        x_ref, b_ref, c_ref, dt_ref, a_ref, dy_ref, hp_ref = refs[0:7]
        pos = 7
        skip_ref = None
        if skip_b is not None:
            skip_ref = refs[pos]
            pos += 1
        prev_refs = None
        if prev is not None:
            prev_refs = refs[pos:pos + 3]
            pos += 3
        dx_ref, db_ref, dc_ref, ddt_ref, da_ref, dh_scr = refs[pos:pos + 6]

        @pl.when(pl.program_id(0) == 0)
        def _():
            dh_scr[...] = jnp.zeros_like(dh_scr)
            da_ref[...] = jnp.zeros_like(da_ref)

        def group(g, carry):
            big, st, heads = _group_rows(g, 512), _group_rows(g, 128), _group_rows(g, HEADS_PER_GROUP)
            x_v, dy_v, dx_v = x_ref.at[big], dy_ref.at[big], dx_ref.at[big]
            hp_v, dh_v = hp_ref.at[0, heads], dh_scr.at[heads]
            dt_blk = dt_ref[heads, :]
            a_blk = a_ref[heads, :]
            mask, tri, a8, acs, acs_t, lane = _ssd_decays(dt_blk, a_blk, reverse)
            sub = lax.broadcasted_iota(jnp.int32, (CHUNK, CHUNK), 0)
            mask_t = (sub >= lane) if reverse else (sub <= lane)
            bm = b_ref[st, :].T
            cm = c_ref[st, :].T
            cb = _dot_nt(cm, bm)
            cb_t = _dot_nt(bm, cm)
            tot = jnp.sum(a8, axis=1, keepdims=True)
            dcb = jnp.zeros((CHUNK, CHUNK), f32)
            dbm = jnp.zeros((CHUNK, D_STATE), f32)
            dcm = jnp.zeros((CHUNK, D_STATE), f32)
            dacs_rows, ddtx_rows = [], []
            for j in range(HEADS_PER_GROUP):
                rows = slice(SSD_HEAD_DIM * j, SSD_HEAD_DIM * (j + 1))
                col_j = _lane_col(acs, lane, j)
                row_j = acs_t[j:j + 1, :]
                dt_j = dt_blk[j:j + 1, :]
                tot_j = tot[j:j + 1, :]
                lmat = jnp.where(mask, jnp.exp(jnp.where(mask, col_j - row_j, 0.0)), 0.0)
                lmat_t = jnp.where(mask_t, jnp.exp(jnp.where(mask_t, row_j - col_j, 0.0)), 0.0)
                x = x_v[rows, :]
                xdt = x * dt_j
                dyh = dy_v[rows, :]
                hp = hp_v[j]
                dhn = dh_v[j]
                ml = _dot_tn(dyh, xdt) * lmat
                w_t = _dot_tn(xdt, dyh) * lmat_t * cb_t
                dcb = dcb + ml
                dacs = jnp.sum(w_t, axis=0, keepdims=True) - jnp.sum(ml * cb, axis=0, keepdims=True)
                ecol = jnp.exp(row_j)
                dec = jnp.exp(tot_j - row_j)
                dye = dyh * ecol
                yoff = _dot_nt(hp, cm) * ecol
                gmat = _dot_nt(dhn, bm)
                dxdt = _dot(dyh, cb * lmat) + dec * gmat
                s_dec = jnp.sum(xdt * gmat, axis=0, keepdims=True) * dec
                dacs = dacs + jnp.sum(dyh * yoff, axis=0, keepdims=True) - s_dec
                dcd = jnp.sum(jnp.sum(dhn * hp, axis=1, keepdims=True), axis=0, keepdims=True)
                dtot = jnp.sum(s_dec, axis=1, keepdims=True) + jnp.exp(tot_j) * dcd
                dacs_rows.append((dacs, dtot))
                ddtx_rows.append(jnp.sum(dxdt * x, axis=0, keepdims=True))
                dcm = dcm + _dot_tn(dye, hp)
                dbm = dbm + _dot_tn(xdt * dec, dhn)
                dxh = dxdt * dt_j
                if skip_ref is not None:
                    dxh = dxh + skip_ref.at[big][rows, :] * dyh
                if prev_refs is not None:
                    dxh = dxh + prev_refs[0].at[big][rows, :]
                dx_v[rows, :] = dxh
                dh_v[j] = jnp.exp(tot_j) * dhn + _dot(dye, cm)
            dcm = dcm + _dot(dcb, bm)
            dbm = dbm + _dot_tn(dcb, cm)
            dbt, dct = dbm.T, dcm.T
            if prev_refs is not None:
                dbt = dbt + prev_refs[1][st, :]
                dct = dct + prev_refs[2][st, :]
            db_ref[st, :] = dbt
            dc_ref[st, :] = dct
            dacs8 = jnp.concatenate([d for d, _ in dacs_rows], axis=0)
            dtot8 = jnp.concatenate([d for _, d in dacs_rows], axis=0)
            da8 = _dot_exact(dacs8, tri) + dtot8
            ddt_ref[heads, :] = da8 * a_blk + jnp.concatenate(ddtx_rows, axis=0)
            da_ref[heads, :] += da8 * dt_blk
            return carry

        lax.fori_loop(0, SSD_GROUPS, group, 0)

    big = pl.BlockSpec((D_INNER, CHUNK), lambda c: (0, cidx(c)))
    st = pl.BlockSpec((512, CHUNK), lambda c: (0, cidx(c)))
    in_specs = [big, pl.BlockSpec((512, CHUNK), lambda c: (4, cidx(c))), pl.BlockSpec((512, CHUNK), lambda c: (5, cidx(c))),
                pl.BlockSpec((SSD_HEADS, CHUNK), lambda c: (direction, cidx(c))),
                pl.BlockSpec((SSD_HEADS, 128), lambda c: (direction, 0)), big,
                pl.BlockSpec((1, SSD_HEADS, SSD_HEAD_DIM, D_STATE), lambda c: (cidx(c), 0, 0, 0))]
    args = [xbc_ct, xbc_ct, xbc_ct, dt_t, a_b, dy_t, hprev]
    if skip_b is not None:
        in_specs.append(pl.BlockSpec((D_INNER, 128), lambda c: (0, 0)))
        args.append(skip_b)
    if prev is not None:
        in_specs += [big, st, st]
        args += list(prev)
    return pl.pallas_call(
        body, name=name, grid=(nc,), in_specs=in_specs,
        out_specs=[big, st, st, pl.BlockSpec((SSD_HEADS, CHUNK), lambda c: (0, cidx(c))),
                   pl.BlockSpec((SSD_HEADS, 128), lambda c: (0, 0))],
        out_shape=[jax.ShapeDtypeStruct((D_INNER, t), f32), jax.ShapeDtypeStruct((512, t), f32),
                   jax.ShapeDtypeStruct((512, t), f32), jax.ShapeDtypeStruct((SSD_HEADS, t), f32),
                   jax.ShapeDtypeStruct((SSD_HEADS, 128), f32)],
        scratch_shapes=[pltpu.VMEM((SSD_HEADS, SSD_HEAD_DIM, D_STATE), f32)],
        compiler_params=_cparams(("arbitrary",)))(*args)


def tail_fwd_t(y_scan, xbc_ct, z_t, skip_b, nw_b, tb=512):
    t = y_scan.shape[1]
    reps = tb // 128

    def body(ys_ref, x_ref, z_ref, d_ref, w_ref, o_ref):
        zz = z_ref[...]
        y = (ys_ref[...] + _lanes(d_ref[...], reps) * x_ref[...]) * (zz * _sigmoid(zz))
        rstd = lax.rsqrt(jnp.mean(y * y, axis=0, keepdims=True) + NORM_EPS)
        o_ref[...] = (y * rstd * _lanes(w_ref[...], reps)).astype(o_ref.dtype)

    blk = pl.BlockSpec((512, tb), lambda g, i: (g, i))
    par = pl.BlockSpec((512, 128), lambda g, i: (g, 0))
    return pl.pallas_call(
        body, name="tail_fwd", grid=(SSD_GROUPS, t // tb), in_specs=[blk, blk, blk, par, par], out_specs=blk,
        out_shape=jax.ShapeDtypeStruct((D_INNER, t), bf16),
        compiler_params=_cparams(("parallel", "parallel")))(y_scan, xbc_ct, z_t, skip_b, nw_b)


def tail_bwd_t(dyn_t, y_scan, xbc_ct, z_t, skip_b, nw_b, tb=512):
    t = y_scan.shape[1]
    reps = tb // 128

    def body(g_ref, ys_ref, x_ref, z_ref, d_ref, w_ref, dy_ref, dz_ref, dw_ref, dd_ref):
        zz = z_ref[...]
        sg = _sigmoid(zz)
        sl = zz * sg
        x = x_ref[...]
        y = ys_ref[...] + _lanes(d_ref[...], reps) * x
        yz = y * sl
        rstd = lax.rsqrt(jnp.mean(yz * yz, axis=0, keepdims=True) + NORM_EPS)
        yhat = yz * rstd
        g = g_ref[...]
        dyhat = g * _lanes(w_ref[...], reps)
        dyz = rstd * (dyhat - yhat * jnp.mean(dyhat * yhat, axis=0, keepdims=True))
        dy = dyz * sl
        dy_ref[...] = dy
        dz_ref[...] = (dyz * y * sg * (1.0 + zz * (1.0 - sg))).astype(dz_ref.dtype)

        def fold(v):
            s = v[:, 0:128]
            for q in range(1, reps):
                s = s + v[:, 128 * q:128 * (q + 1)]
            return s

        @pl.when(pl.program_id(1) == 0)
        def _():
            dw_ref[...] = jnp.zeros_like(dw_ref)
            dd_ref[...] = jnp.zeros_like(dd_ref)

        dw_ref[...] += fold(g * yhat)
        dd_ref[...] += fold(dy * x)

    blk = pl.BlockSpec((512, tb), lambda g, i: (g, i))
    par = pl.BlockSpec((512, 128), lambda g, i: (g, 0))
    return pl.pallas_call(
        body, name="tail_bwd", grid=(SSD_GROUPS, t // tb), in_specs=[blk, blk, blk, blk, par, par],
        out_specs=[blk, blk, par, par],
        out_shape=[jax.ShapeDtypeStruct((D_INNER, t), f32), jax.ShapeDtypeStruct((D_INNER, t), bf16),
                   jax.ShapeDtypeStruct((D_INNER, 128), f32), jax.ShapeDtypeStruct((D_INNER, 128), f32)],
        compiler_params=_cparams(("parallel", "arbitrary")))(dyn_t, y_scan, xbc_ct, z_t, skip_b, nw_b)


def merge_fwd(u_gate, bg_row, y_ssd, y_att, tb=512):
    t = y_ssd.shape[0]

    def body(ga_ref, gb_ref, ba_ref, bb_ref, ys_ref, ya_ref, o_ref):
        o_ref[...] = (_sigmoid(ga_ref[...] + ba_ref[...]) * ys_ref[...]
                      + _sigmoid(gb_ref[...] + bb_ref[...]) * ya_ref[...]).astype(o_ref.dtype)

    blk = pl.BlockSpec((tb, 512), lambda i, j: (i, j))
    blk2 = pl.BlockSpec((tb, 512), lambda i, j: (i, 2 + j))
    row = pl.BlockSpec((1, 512), lambda i, j: (0, j))
    row2 = pl.BlockSpec((1, 512), lambda i, j: (0, 2 + j))
    return pl.pallas_call(
        body, name="merge_fwd", grid=(t // tb, 2), in_specs=[blk, blk2, row, row2, blk, blk], out_specs=blk,
        out_shape=jax.ShapeDtypeStruct((t, D_MODEL), bf16),
        compiler_params=_cparams(("parallel", "parallel")))(u_gate, u_gate, bg_row, bg_row, y_ssd, y_att)


def merge_bwd(dm, u_gate, bg_row, y_ssd, y_att, tb=512):
    t = dm.shape[0]

    def body(dm_ref, ga_ref, gb_ref, ba_ref, bb_ref, ys_ref, ya_ref, dys_ref, dya_ref, dga_ref, dgb_ref, dba_ref, dbb_ref):
        d = dm_ref[...]
        sa = _sigmoid(ga_ref[...] + ba_ref[...])
        sb = _sigmoid(gb_ref[...] + bb_ref[...])
        dys_ref[...] = (d * sa).astype(dys_ref.dtype)
        dya_ref[...] = (d * sb).astype(dya_ref.dtype)
        dla = d * ys_ref[...] * sa * (1.0 - sa)
        dlb = d * ya_ref[...] * sb * (1.0 - sb)
        dga_ref[...] = dla.astype(dga_ref.dtype)
        dgb_ref[...] = dlb.astype(dgb_ref.dtype)

        @pl.when(pl.program_id(1) == 0)
        def _():
            dba_ref[...] = jnp.zeros_like(dba_ref)
            dbb_ref[...] = jnp.zeros_like(dbb_ref)

        dba_ref[...] += jnp.sum(dla, axis=0, keepdims=True)
        dbb_ref[...] += jnp.sum(dlb, axis=0, keepdims=True)

    blk = pl.BlockSpec((tb, 512), lambda j, i: (i, j))
    blk2 = pl.BlockSpec((tb, 512), lambda j, i: (i, 2 + j))
    row = pl.BlockSpec((1, 512), lambda j, i: (0, j))
    row2 = pl.BlockSpec((1, 512), lambda j, i: (0, 2 + j))
    act = jax.ShapeDtypeStruct((t, D_MODEL), bf16)
    vec = jax.ShapeDtypeStruct((1, D_MODEL), f32)
    return pl.pallas_call(
        body, name="merge_bwd", grid=(2, t // tb), in_specs=[blk, blk, blk2, row, row2, blk, blk],
        out_specs=[blk, blk, blk, blk, row, row], out_shape=[act, act, act, act, vec, vec],
        compiler_params=_cparams(("parallel", "arbitrary")))(dm, u_gate, u_gate, bg_row, bg_row, y_ssd, y_att)


def _ln_stats(r):
    mu = jnp.mean(r, axis=1, keepdims=True)
    xc = r - mu
    rstd = lax.rsqrt(jnp.mean(xc * xc, axis=1, keepdims=True) + NORM_EPS)
    return xc * rstd, rstd


def _ln_bwd(dy, xhat, rstd, g_row):
    dxh = dy * g_row
    return rstd * (dxh - jnp.mean(dxh, axis=1, keepdims=True) - xhat * jnp.mean(dxh * xhat, axis=1, keepdims=True))


def ln1_fwd(x, mix, g_row, b_row, tb=512):
    t = x.shape[0]

    def body(x_ref, m_ref, g_ref, b_ref, o_ref, ob_ref):
        xhat, _ = _ln_stats(ALPHA * x_ref[...] + m_ref[...])
        h = xhat * g_ref[...] + b_ref[...]
        o_ref[...] = h
        ob_ref[...] = h.astype(ob_ref.dtype)

    blk = pl.BlockSpec((tb, D_MODEL), lambda i: (i, 0))
    row = pl.BlockSpec((1, D_MODEL), lambda i: (0, 0))
    return pl.pallas_call(body, name="ln1_fwd", grid=(t // tb,), in_specs=[blk, blk, row, row], out_specs=[blk, blk],
                          out_shape=[jax.ShapeDtypeStruct((t, D_MODEL), f32), jax.ShapeDtypeStruct((t, D_MODEL), bf16)],
                          compiler_params=_cparams(("parallel",)))(x, mix, g_row, b_row)


def ln1_bwd(dh, x, mix, g_row, tb=512):
    t = x.shape[0]

    def body(dh_ref, x_ref, m_ref, g_ref, dr_ref, drb_ref, dg_ref, db_ref):
        xhat, rstd = _ln_stats(ALPHA * x_ref[...] + m_ref[...])
        dy = dh_ref[...]
        dr = _ln_bwd(dy, xhat, rstd, g_ref[...])
        dr_ref[...] = dr
        drb_ref[...] = dr.astype(drb_ref.dtype)

        @pl.when(pl.program_id(0) == 0)
        def _():
            dg_ref[...] = jnp.zeros_like(dg_ref)
            db_ref[...] = jnp.zeros_like(db_ref)

        dg_ref[...] += jnp.sum(dy * xhat, axis=0, keepdims=True)
        db_ref[...] += jnp.sum(dy, axis=0, keepdims=True)

    blk = pl.BlockSpec((tb, D_MODEL), lambda i: (i, 0))
    row = pl.BlockSpec((1, D_MODEL), lambda i: (0, 0))
    return pl.pallas_call(
        body, name="ln1_bwd", grid=(t // tb,), in_specs=[blk, blk, blk, row], out_specs=[blk, blk, row, row],
        out_shape=[jax.ShapeDtypeStruct((t, D_MODEL), f32), jax.ShapeDtypeStruct((t, D_MODEL), bf16),
                   jax.ShapeDtypeStruct((1, D_MODEL), f32), jax.ShapeDtypeStruct((1, D_MODEL), f32)],
        compiler_params=_cparams(("arbitrary",)))(dh, x, mix, g_row)


def ln2_loss(h1, f, g_row, b_row, target, tb=512):
    t = h1.shape[0]

    def body(h_ref, f_ref, g_ref, b_ref, t_ref, dr_ref, drb_ref, dg_ref, db_ref, loss_ref):
        xhat, rstd = _ln_stats(ALPHA * h_ref[...] + f_ref[...])
        g = g_ref[...]
        err = xhat * g + b_ref[...] - t_ref[...]
        dy = err * (1.0 / D_MODEL)
        dr = _ln_bwd(dy, xhat, rstd, g)
        dr_ref[...] = dr
        drb_ref[...] = dr.astype(drb_ref.dtype)

        @pl.when(pl.program_id(0) == 0)
        def _():
            dg_ref[...] = jnp.zeros_like(dg_ref)
            db_ref[...] = jnp.zeros_like(db_ref)
            loss_ref[...] = jnp.zeros_like(loss_ref)

        dg_ref[...] += jnp.sum(dy * xhat, axis=0, keepdims=True)
        db_ref[...] += jnp.sum(dy, axis=0, keepdims=True)
        part = jnp.sum(jnp.mean(err * err, axis=1, keepdims=True), axis=0, keepdims=True)
        loss_ref[...] += 0.5 * part

    blk = pl.BlockSpec((tb, D_MODEL), lambda i: (i, 0))
    row = pl.BlockSpec((1, D_MODEL), lambda i: (0, 0))
    return pl.pallas_call(
        body, name="ln2_loss", grid=(t // tb,), in_specs=[blk, blk, row, row, blk],
        out_specs=[blk, blk, row, row, pl.BlockSpec((8, 128), lambda i: (0, 0))],
        out_shape=[jax.ShapeDtypeStruct((t, D_MODEL), f32), jax.ShapeDtypeStruct((t, D_MODEL), bf16),
                   jax.ShapeDtypeStruct((1, D_MODEL), f32), jax.ShapeDtypeStruct((1, D_MODEL), f32),
                   jax.ShapeDtypeStruct((8, 128), f32)],
        compiler_params=_cparams(("arbitrary",)))(h1, f, g_row, b_row, target)


TAIL_BLOCK, TAIL_AT = divmod(OFF_TAIL, PACK_TILE)


def _sum4(ref):
    return ((ref[0].astype(f32) + ref[1].astype(f32)) + ref[2].astype(f32)) + ref[3].astype(f32)


def adamw(parts, tails, w, m, v):
    rows = w.shape[0]
    c1 = 1.0 - ADAM_B1 ** ADAM_STEP
    c2 = 1.0 - ADAM_B2 ** ADAM_STEP

    def body(p_ref, t_ref, w_ref, m_ref, v_ref, g_ref, d_ref, nm_ref, nv_ref):
        g = _sum4(p_ref)
        with_tail = jnp.concatenate([g[0:TAIL_AT], _sum4(t_ref), g[TAIL_AT + ROWS_TAIL:]], axis=0)
        g = jnp.where(pl.program_id(0) == TAIL_BLOCK, with_tail, g)
        nm = ADAM_B1 * m_ref[...] + (1.0 - ADAM_B1) * g
        nv = ADAM_B2 * v_ref[...] + (1.0 - ADAM_B2) * (g * g)
        g_ref[...] = g
        nm_ref[...] = nm
        nv_ref[...] = nv
        d_ref[...] = -ADAM_LR * ((nm / c1) / (jnp.sqrt(nv / c2) + ADAM_EPS) + ADAM_WD * w_ref[...])

    blk = pl.BlockSpec((PACK_TILE, 1024), lambda i: (i, 0))
    out = jax.ShapeDtypeStruct((rows, 1024), f32)
    return pl.pallas_call(
        body, name="adamw", grid=(rows // PACK_TILE,),
        in_specs=[pl.BlockSpec((4, PACK_TILE, 1024), lambda i: (0, i, 0)),
                  pl.BlockSpec((4, ROWS_TAIL, 1024), lambda i: (0, 0, 0)), blk, blk, blk], out_specs=[blk] * 4,
        out_shape=[out] * 4, compiler_params=_cparams(("parallel",)))(parts, tails, w, m, v)


def pair_sum(parts, recv, core):
    rows = parts.shape[1]

    def body(c_ref, a_ref, b_ref, o_ref, t_ref):
        s = a_ref[...] + b_ref[...]
        o_ref[...] = s.astype(o_ref.dtype)

        @pl.when(pl.program_id(1) == TAIL_BLOCK)
        def _():
            t_ref[...] = s[:, TAIL_AT:TAIL_AT + ROWS_TAIL]

    grid_spec = pltpu.PrefetchScalarGridSpec(
        num_scalar_prefetch=1, grid=(4, rows // PACK_TILE),
        in_specs=[pl.BlockSpec((1, PACK_TILE, 1024), lambda j, i, c_ref: (2 * j + c_ref[0], i, 0)),
                  pl.BlockSpec((1, PACK_TILE, 1024), lambda j, i, c_ref: (j, i, 0))],
        out_specs=[pl.BlockSpec((1, PACK_TILE, 1024), lambda j, i, c_ref: (j, i, 0)),
                   pl.BlockSpec((1, ROWS_TAIL, 1024), lambda j, i, c_ref: (j, 0, 0))])
    return pl.pallas_call(
        body, name="pair_sum", grid_spec=grid_spec,
        out_shape=[jax.ShapeDtypeStruct(recv.shape, bf16), jax.ShapeDtypeStruct((4, ROWS_TAIL, 1024), f32)],
        compiler_params=_cparams(("parallel", "arbitrary")))(core, parts, recv)


def _place():
    return lax.axis_index("x"), lax.axis_index("y"), lax.axis_index("c")


def all_gather_blocks(shard):
    rows, cols = shard.shape

    def body(x_ref, out_ref, send_sems, recv_sems, local_sem):
        x, y, c = _place()
        me, sibling = (x, y, c), (x, y, 1 - c)
        chips = [(1 - x, y), (x, 1 - y), (1 - x, 1 - y)]

        def slot(px, py, pc):
            return out_ref.at[4 * px + 2 * py + pc]

        def copy(k, block, to, src=None):
            return pltpu.make_async_remote_copy(
                src_ref=slot(*block) if src is None else src, dst_ref=slot(*block), send_sem=send_sems.at[k],
                recv_sem=recv_sems.at[k], device_id=to, device_id_type=MESH)

        mine = pltpu.make_async_copy(x_ref, slot(*me), local_sem)
        mine.start()
        first = [copy(0, me, sibling, src=x_ref)]
        first += [copy(1 + j, me, (*chip, c), src=x_ref) for j, chip in enumerate(chips)]
        for cp in first:
            cp.start()
        passed = [copy(4 + j, (*chip, c), sibling) for j, chip in enumerate(chips)]
        for j, chip in enumerate(chips):
            copy(1 + j, (*chip, c), me).wait_recv()
            passed[j].start()
        copy(0, sibling, me).wait_recv()
        for j, chip in enumerate(chips):
            copy(4 + j, (*chip, 1 - c), me).wait_recv()
        for cp in first + passed:
            cp.wait_send()
        mine.wait()

    return pl.pallas_call(
        body, name="all_gather_blocks", out_shape=jax.ShapeDtypeStruct((N_DEV, rows, cols), shard.dtype),
        in_specs=[pl.BlockSpec(memory_space=pl.ANY)], out_specs=pl.BlockSpec(memory_space=pl.ANY),
        scratch_shapes=[pltpu.SemaphoreType.DMA((7,)), pltpu.SemaphoreType.DMA((7,)), pltpu.SemaphoreType.DMA],
        compiler_params=pltpu.CompilerParams(has_side_effects=True))(shard)


def pair_exchange(parts):
    _, rows, cols = parts.shape

    def body(p_ref, recv_ref, send_sems, recv_sems):
        x, y, c = _place()
        copies = [pltpu.make_async_remote_copy(
            src_ref=p_ref.at[2 * j + 1 - c], dst_ref=recv_ref.at[j], send_sem=send_sems.at[j], recv_sem=recv_sems.at[j],
            device_id=(x, y, 1 - c), device_id_type=MESH) for j in range(4)]
        for cp in copies:
            cp.start()
        for cp in copies:
            cp.wait_recv()
        for cp in copies:
            cp.wait_send()

    return pl.pallas_call(
        body, name="pair_exchange", out_shape=jax.ShapeDtypeStruct((4, rows, cols), parts.dtype),
        in_specs=[pl.BlockSpec(memory_space=pl.ANY)], out_specs=pl.BlockSpec(memory_space=pl.ANY),
        scratch_shapes=[pltpu.SemaphoreType.DMA((4,)), pltpu.SemaphoreType.DMA((4,))],
        compiler_params=pltpu.CompilerParams(has_side_effects=True))(parts)


def chip_exchange(parts):
    n = len(parts)

    def body(*refs):
        p_refs, out_refs = refs[0:n], refs[n:2 * n]
        send_sems, recv_sems, local_sems = refs[2 * n:]
        x, y, c = _place()
        mine = 2 * x + y
        flips = [(x, 1 - y), (1 - x, y), (1 - x, 1 - y)]

        def copy(a, k, src_slot, dst_slot):
            px, py = flips[k]
            return pltpu.make_async_remote_copy(
                src_ref=p_refs[a].at[src_slot], dst_ref=out_refs[a].at[dst_slot], send_sem=send_sems.at[3 * a + k],
                recv_sem=recv_sems.at[3 * a + k], device_id=(px, py, c), device_id_type=MESH)

        local = [pltpu.make_async_copy(p_refs[a].at[mine], out_refs[a].at[mine], local_sems.at[a]) for a in range(n)]
        sends = [copy(a, k, 2 * flips[k][0] + flips[k][1], mine) for a in range(n) for k in range(3)]
        for cp in local + sends:
            cp.start()
        for a in range(n):
            for k in range(3):
                copy(a, k, mine, 2 * flips[k][0] + flips[k][1]).wait_recv()
        for cp in sends:
            cp.wait_send()
        for cp in local:
            cp.wait()

    return pl.pallas_call(
        body, name="chip_exchange", out_shape=[jax.ShapeDtypeStruct(p.shape, p.dtype) for p in parts],
        in_specs=[pl.BlockSpec(memory_space=pl.ANY)] * n, out_specs=[pl.BlockSpec(memory_space=pl.ANY)] * n,
        scratch_shapes=[pltpu.SemaphoreType.DMA((3 * n,)), pltpu.SemaphoreType.DMA((3 * n,)), pltpu.SemaphoreType.DMA((n,))],
        compiler_params=pltpu.CompilerParams(has_side_effects=True))(*parts)


def _tail_rows(conv_part, small, extra):
    lead = conv_part.shape[:-1]
    rep = jnp.concatenate([small[n].reshape(-1).astype(f32) for n in SMALL] + [extra.reshape(1).astype(f32)])
    flat = jnp.concatenate([conv_part, jnp.broadcast_to(rep, lead + rep.shape),
                            jnp.zeros(lead + (ROWS_TAIL * 1024 - TAIL_ELEMS,), f32)], axis=-1)
    return flat.reshape(lead + (ROWS_TAIL, 1024))


def _pack_rows(w_in_t, w_ps, w_out, w_up_t, w_down, w_pa_t, tail):
    lead = tail.shape[:-2]
    zeros = lambda r: jnp.zeros(lead + (r, 1024), f32)
    return jnp.concatenate([w_in_t, zeros(ROWS_IN - IN_SHARD), w_ps, w_out, w_up_t, w_down,
                            w_pa_t.reshape(lead + (ROWS_PA, 1024)), tail,
                            zeros(PACK_ROWS - OFF_TAIL - ROWS_TAIL)], axis=-2)


def _pack_shard(vals):
    tail = _tail_rows(vals["conv_w"].reshape(-1), vals, jnp.zeros((), f32))
    return _pack_rows(vals["w_in"].T, vals["w_proj_ssd"], vals["w_out"], vals["w_up"].T, vals["w_down"],
                      vals["w_proj_attn"].T, tail)


def _unpack_shard(packed):
    out = {"w_in": packed[0:IN_SHARD].T, "w_proj_ssd": packed[OFF_PS:OFF_OUT], "w_out": packed[OFF_OUT:OFF_UP],
           "w_up": packed[OFF_UP:OFF_DOWN].T, "w_down": packed[OFF_DOWN:OFF_PA],
           "w_proj_attn": packed[OFF_PA:OFF_TAIL].reshape(D_MODEL // N_DEV, ATTN_OUT).T}
    flat = packed[OFF_TAIL:OFF_TAIL + ROWS_TAIL].reshape(-1)
    out["conv_w"] = flat[0:CONV_SHARD].reshape(D_CONV, CONV_DIM // N_DEV)
    off = CONV_SHARD
    for n in SMALL:
        out[n] = flat[off:off + SMALL_SIZES[n]]
        off += SMALL_SIZES[n]
    out["_extra"] = flat[off]
    return out


def _pack_parts(full, small, extra):
    conv = full["conv_w"].reshape(D_CONV, N_DEV, CONV_DIM // N_DEV).transpose(1, 0, 2).reshape(N_DEV, CONV_SHARD)
    blocks = lambda g: g.reshape(N_DEV, g.shape[0] // N_DEV, g.shape[1])
    return _pack_rows(blocks(full["w_in_t"]), blocks(full["w_proj_ssd"]), blocks(full["w_out"]), blocks(full["w_up_t"]),
                      blocks(full["w_down"]), blocks(full["w_proj_attn_t"]), _tail_rows(conv, small, extra))


def _gather_weights(w):
    conv_bits = lax.bitcast_convert_type(w["conv_w"], bf16).reshape(-1)
    conv_rows = jnp.concatenate([conv_bits, jnp.zeros((16 * 1024 - 2 * CONV_SHARD,), bf16)]).reshape(16, 1024)
    big = _pack_shard(w)[0:OFF_TAIL].astype(bf16)
    got = all_gather_blocks(jnp.concatenate([big, conv_rows], axis=0))
    whole = lambda lo, hi: got[:, lo:hi].reshape(N_DEV * (hi - lo), 1024)
    conv = lax.bitcast_convert_type(got[:, OFF_TAIL:OFF_TAIL + 4].reshape(N_DEV, 4096)[:, 0:2 * CONV_SHARD]
                                    .reshape(N_DEV, D_CONV, CONV_DIM // N_DEV, 2), f32)
    return {"w_in_t": whole(0, IN_SHARD), "w_proj_ssd": whole(OFF_PS, OFF_OUT), "w_out": whole(OFF_OUT, OFF_UP),
            "w_up_t": whole(OFF_UP, OFF_DOWN), "w_down": whole(OFF_DOWN, OFF_PA),
            "w_proj_attn_t": got[:, OFF_PA:OFF_TAIL].reshape(D_MODEL, ATTN_OUT),
            "conv_w": conv.transpose(1, 0, 2).reshape(D_CONV, CONV_DIM)}


def _row(v, width=None):
    v = v.reshape(1, -1).astype(f32)
    return v if width is None else jnp.pad(v, ((0, 0), (0, width - v.shape[1])))


def _lanes256(vf, vb):
    z = jnp.zeros((96,), f32)
    return jnp.concatenate([vf.astype(f32), z, vb.astype(f32), z]).reshape(1, 256)


def _local_step(x2, tgt, wf, p):
    t = x2.shape[0]
    o = np.cumsum((0,) + IN_SPLITS)
    wt = wf["w_in_t"]
    wt_z, wt_xbc, wt_dt = wt[o[0]:o[1]], wt[o[1]:o[2]], wt[o[2]:o[4]]
    wt_qkv, wt_gate = wt[o[4]:o[7]], wt[o[7]:o[8]]

    spread = lambda v: jnp.broadcast_to(v.astype(f32)[..., None], v.shape + (128,))
    conv_w_b, conv_b_b = spread(wf["conv_w"]), spread(p["conv_b"])
    dt_bias_b = spread(jnp.concatenate([p["dt_bias_f"], p["dt_bias_b"]]))
    a_f, a_b = -jnp.exp(p["a_log_f"].astype(f32)), -jnp.exp(p["a_log_b"].astype(f32))
    a_coef_b = spread(jnp.concatenate([a_f, a_b]))
    skip_b = spread(jnp.repeat(p["d_skip"], SSD_HEAD_DIM))
    nw_b, bg_row = spread(p["ssd_norm_w"]), _row(p["b_gate"])
    g1, b1, g2, b2 = _row(p["ln1_g"]), _row(p["ln1_b"]), _row(p["ln2_g"]), _row(p["ln2_b"])

    xb = x2.astype(MXU_DTYPE)
    xt = xb.T
    u_z = mm_nn(wt_z, xt, "in_z")
    u_xbc = mm_nn(wt_xbc, xt, "in_xbc")
    u_dt = mm_nn(wt_dt, xt, "in_dt")
    u_qkv = mm_nt(xb, wt_qkv, "in_qkv")
    u_gate = mm_nt(xb, wt_gate, "in_gate")
    xbc_c = conv_fwd_t(u_xbc, conv_w_b, conv_b_b)
    dt_t = dt_fwd_t(u_dt, dt_bias_b)
    y_f, h_f = ssd_fwd_t(xbc_c, dt_t, a_coef_b, False, "ssd_fwd_f")
    y_scan, h_b = ssd_fwd_t(xbc_c, dt_t, a_coef_b, True, "ssd_fwd_b", prev=y_f)
    yn = tail_fwd_t(y_scan, xbc_c, u_z, skip_b, nw_b)
    y_ssd = mm_tn(yn, wf["w_proj_ssd"], "proj_ssd")

    def strided(a, dil):
        return a.reshape(t // dil, dil * 256)

    qkv, outs, lses = [], [], []
    for pi, (_, dil) in enumerate(DIL_PATTERNS):
        q, k, v = (strided(u_qkv[:, ATTN_WIDTH * s + 256 * pi: ATTN_WIDTH * s + 256 * (pi + 1)], dil) for s in range(3))
        qkv.append((q, k, v))
        op, lp = attn_fwd(q, k, v, pi, dil, f"attn_fwd_{pi}")
        outs.append(op.reshape(t, 256))
        lses.append(lp.reshape(t, 256))
    ya, lse = attn_combine(outs, lses)
    y_att = mm_nt(ya, wf["w_proj_attn_t"], "proj_attn")
    m = merge_fwd(u_gate, bg_row, y_ssd, y_att)
    mix = mm_nn(m, wf["w_out"], "out_proj")
    h1, h1b = ln1_fwd(x2, mix, g1, b1)
    a_up, p_act = mm_nt(h1b, wf["w_up_t"], "mlp_up", relu2=True)
    f_dn = mm_nn(p_act, wf["w_down"], "mlp_down")
    dr2, dr2b, dg2, db2, loss8 = ln2_loss(h1, f_dn, g2, b2, tgt)

    full, small = {}, {}
    da = mm_nt(dr2b, wf["w_down"], "d_mlp_act", out_dtype=bf16, relu2_of=a_up)
    full["w_down"] = mm_tn(p_act, dr2b, "dw_down")
    full["w_up_t"] = mm_tn(da, h1b, "dw_up")
    dh1 = mm_nn(da, wf["w_up_t"], "d_h1", acc_in=dr2, acc_scale=ALPHA)
    dr1, dr1b, dg1, db1 = ln1_bwd(dh1, x2, mix, g1)
    dm = mm_nt(dr1b, wf["w_out"], "d_merge")
    full["w_out"] = mm_tn(m, dr1b, "dw_out")
    dys, dya_p, dga, dgb, dba, dbb = merge_bwd(dm, u_gate, bg_row, y_ssd, y_att)
    dyn = mm_nt(wf["w_proj_ssd"], dys, "d_yn")
    full["w_proj_ssd"] = mm_nn(yn, dys, "dw_proj_ssd")
    dya = mm_nn(dya_p, wf["w_proj_attn_t"], "d_ya")
    full["w_proj_attn_t"] = mm_tn(dya_p, ya, "dw_proj_attn")

    dy, dz, dnw, ddx = tail_bwd_t(dyn, y_scan, xbc_c, u_z, skip_b, nw_b)
    dxf, dbf, dcf, ddtf, daf = ssd_bwd_t(xbc_c, dt_t, a_coef_b, dy, h_f, False, "ssd_bwd_f", skip_b=skip_b)
    dxs, dbs, dcs, ddtb, dab = ssd_bwd_t(xbc_c, dt_t, a_coef_b, dy, h_b, True, "ssd_bwd_b", prev=(dxf, dbf, dcf))
    du_xbc, dcw_x, dcb_x = conv_bwd_t(u_xbc, dxs, conv_w_b, conv_b_b, None, "conv_bwd_x", 0)
    du_xbc, dcw_b, dcb_b = conv_bwd_t(u_xbc, dbs, conv_w_b, conv_b_b, du_xbc, "conv_bwd_b", D_INNER)
    du_xbc, dcw_c, dcb_c = conv_bwd_t(u_xbc, dcs, conv_w_b, conv_b_b, du_xbc, "conv_bwd_c", D_INNER + 512)
    du_dt, dbias = dt_bwd_t(ddtf, ddtb, u_dt, dt_bias_b)

    delta = attn_delta(dya, ya)
    dqs, dks, dvs = [], [], []
    for pi, (_, dil) in enumerate(DIL_PATTERNS):
        q, k, v = qkv[pi]
        sd, sl_, sdel = strided(dya, dil), strided(lse, dil), strided(delta, dil)
        dqs.append(attn_dq(q, k, v, sd, sl_, sdel, pi, dil, f"attn_dq_{pi}").reshape(t, 256))
        dk, dv = attn_dkv(q, k, v, sd, sl_, sdel, pi, dil, f"attn_dkv_{pi}")
        dks.append(dk.reshape(t, 256))
        dvs.append(dv.reshape(t, 256))
    du_qkv = jnp.concatenate(dqs + dks + dvs, axis=1)
    du_gate = jnp.concatenate([dga, dgb], axis=1)

    dx = mm_tn(dz, wt_z, "dx_z", acc_in=dr1, acc_scale=ALPHA)
    dx = mm_tn(du_xbc, wt_xbc, "dx_xbc", acc_in=dx)
    dx = mm_tn(du_dt, wt_dt, "dx_dt", acc_in=dx)
    dx = mm_nn(du_qkv, wt_qkv, "dx_qkv", acc_in=dx)
    dx = mm_nn(du_gate, wt_gate, "dx_gate", acc_in=dx)
    full["w_in_t"] = jnp.concatenate(
        [mm_nn(dz, xb, "dw_in_z"), mm_nn(du_xbc, xb, "dw_in_xbc"), mm_nn(du_dt, xb, "dw_in_dt"),
         mm_tn(du_qkv, xb, "dw_in_qkv"), mm_tn(du_gate, xb, "dw_in_gate")], axis=0)
    lanes = lambda v: jnp.sum(v, axis=-1)
    full["conv_w"] = jnp.concatenate([lanes(dcw_x), lanes(dcw_b), lanes(dcw_c)], axis=1)

    small["b_gate"] = jnp.concatenate([dba, dbb], axis=1)
    small["conv_b"] = jnp.concatenate([lanes(dcb_x), lanes(dcb_b), lanes(dcb_c)])
    dbias = lanes(dbias)
    small["dt_bias_f"], small["dt_bias_b"] = dbias[0:32], dbias[32:64]
    small["a_log_f"] = lanes(daf) * a_f
    small["a_log_b"] = lanes(dab) * a_b
    small["d_skip"] = jnp.sum(lanes(ddx).reshape(SSD_HEADS, SSD_HEAD_DIM), axis=1)
    small["ssd_norm_w"] = lanes(dnw)
    small["ln1_g"], small["ln1_b"], small["ln2_g"], small["ln2_b"] = dg1, db1, dg2, db2
    return loss8[0, 0], dx, full, small


def kernel(x, w_in, b_gate, conv_w, conv_b, dt_bias_f, dt_bias_b, a_log_f, a_log_b, d_skip, ssd_norm_w, w_proj_ssd, w_proj_attn, w_out, ln1_g, ln1_b, w_up, w_down, ln2_g, ln2_b, loss_target, m_w_in, m_b_gate, m_conv_w, m_conv_b, m_dt_bias_f, m_dt_bias_b, m_a_log_f, m_a_log_b, m_d_skip, m_ssd_norm_w, m_w_proj_ssd, m_w_proj_attn, m_w_out, m_ln1_g, m_ln1_b, m_w_up, m_w_down, m_ln2_g, m_ln2_b, v_w_in, v_b_gate, v_conv_w, v_conv_b, v_dt_bias_f, v_dt_bias_b, v_a_log_f, v_a_log_b, v_d_skip, v_ssd_norm_w, v_w_proj_ssd, v_w_proj_attn, v_w_out, v_ln1_g, v_ln1_b, v_w_up, v_w_down, v_ln2_g, v_ln2_b):
    given = dict(locals())
    w = {n: given[n] for n in WEIGHTS}
    mom = {n: given["m_" + n] for n in WEIGHTS}
    var = {n: given["v_" + n] for n in WEIGHTS}
    t = x.shape[1]
    wf = _gather_weights(w)
    loss, dx, full, small = _local_step(x.reshape(t, D_MODEL), loss_target.reshape(t, D_MODEL), wf, w)
    packed = _pack_parts(full, small, loss)
    core = lax.axis_index("c").astype(jnp.int32).reshape(1)
    parts, tails = chip_exchange(pair_sum(packed, pair_exchange(packed), core))
    g, delta, new_m, new_v = (_unpack_shard(a) for a in
                              adamw(parts, tails, _pack_shard(w), _pack_shard(mom), _pack_shard(var)))
    outs = [g["_extra"], dx.reshape(x.shape)]
    for d in (g, delta, new_m, new_v):
        outs += [d[n].reshape(w[n].shape) for n in WEIGHTS]
    return tuple(outs)
```

```python
import functools
import math

import jax
import jax.numpy as jnp
import numpy as np
from jax import lax
from jax.experimental import pallas as pl
from jax.experimental.pallas import tpu as pltpu

f32 = jnp.float32
bf16 = jnp.bfloat16
MXU_DTYPE = jnp.bfloat16

N_DEV = 8
D_MODEL = 1024
D_INNER = 2048
SSD_HEADS = 32
SSD_HEAD_DIM = 64
SSD_GROUPS = 4
D_STATE = 128
D_CONV = 5
CHUNK = 128
CONV_DIM = D_INNER + 2 * SSD_GROUPS * D_STATE
NORM_EPS = 1e-5
ATTN_HEAD_DIM = 64
DIL_PATTERNS = ((128, 1), (512, 4), (2048, 16))
HEADS_PER_PATTERN = 4
ATTN_HEADS = 12
ATTN_WIDTH = 768
ATTN_OUT = 256
D_FF = 4096
ALPHA = 2.0 ** 0.25
IN_SPLITS = (D_INNER, CONV_DIM, SSD_HEADS, SSD_HEADS, ATTN_WIDTH, ATTN_WIDTH, ATTN_WIDTH, 2 * D_MODEL)
IN_COLS = sum(IN_SPLITS)
ADAM_LR, ADAM_B1, ADAM_B2, ADAM_EPS, ADAM_WD, ADAM_STEP = 0.001, 0.9, 0.999, 1e-08, 0.01, 10
NEG_BIG = -1e30
VMEM_LIMIT = 56 * 1024 * 1024
MESH = pl.DeviceIdType.MESH

SMALL = ("b_gate", "conv_b", "dt_bias_f", "dt_bias_b", "a_log_f", "a_log_b", "d_skip", "ssd_norm_w",
         "ln1_g", "ln1_b", "ln2_g", "ln2_b")
WEIGHTS = ("w_in", "b_gate", "conv_w", "conv_b", "dt_bias_f", "dt_bias_b", "a_log_f", "a_log_b", "d_skip",
           "ssd_norm_w", "w_proj_ssd", "w_proj_attn", "w_out", "ln1_g", "ln1_b", "w_up", "w_down", "ln2_g", "ln2_b")
SMALL_SIZES = {"b_gate": 2 * D_MODEL, "conv_b": CONV_DIM, "dt_bias_f": 32, "dt_bias_b": 32, "a_log_f": 32, "a_log_b": 32,
               "d_skip": 32, "ssd_norm_w": D_INNER, "ln1_g": D_MODEL, "ln1_b": D_MODEL, "ln2_g": D_MODEL, "ln2_b": D_MODEL}
IN_SHARD = IN_COLS // N_DEV
OFF_TAIL = 1200
ROWS_TAIL = 16
PACK_TILE = 128
LATE_ROWS = 1280
ROWS_PS, ROWS_OUT, ROWS_UP, ROWS_DOWN, ROWS_PA = D_INNER // N_DEV, D_MODEL // N_DEV, D_FF // N_DEV, D_FF // N_DEV, 32
OFF_PS = LATE_ROWS
OFF_OUT = OFF_PS + ROWS_PS
OFF_UP = OFF_OUT + ROWS_OUT
OFF_DOWN = OFF_UP + ROWS_UP
OFF_PA = OFF_DOWN + ROWS_DOWN
PACK_ROWS = OFF_PA + ROWS_PA
EARLY_ROWS = PACK_ROWS - LATE_ROWS
EARLY_TILE = 160
CONV_SHARD = D_CONV * CONV_DIM // N_DEV
TAIL_ELEMS = CONV_SHARD + sum(SMALL_SIZES.values()) + 1


def _cparams(sem=None, **kw):
    return pltpu.CompilerParams(dimension_semantics=sem, vmem_limit_bytes=VMEM_LIMIT, **kw)


def _mx(v):
    return v.astype(MXU_DTYPE)


def _dot(a, b):
    return jnp.dot(_mx(a), _mx(b), preferred_element_type=f32)


def _dot_nt(a, b):
    return lax.dot_general(_mx(a), _mx(b), (((1,), (1,)), ((), ())), preferred_element_type=f32)


def _dot_tn(a, b):
    return lax.dot_general(_mx(a), _mx(b), (((0,), (0,)), ((), ())), preferred_element_type=f32)


def _dot_exact(a, b):
    return jnp.dot(a, b, precision=lax.Precision.HIGHEST, preferred_element_type=f32)


def _sigmoid(v):
    return 1.0 / (1.0 + jnp.exp(-v))


def _pick(n, prefs):
    for p in prefs:
        if n % p == 0:
            return p
    return n


MM_TILE = 1024


def mm_nn(a, b, name, out_dtype=f32, acc_in=None, acc_scale=1.0):
    m, k = a.shape
    n = b.shape[1]
    tm = _pick(m, (MM_TILE, 512, 256, 128, 64))
    tn = _pick(n, (MM_TILE, 512, 256, 128))
    tk = _pick(k, (2048, 1536, 1152, 1024, 768, 512, 256, 128))
    nk = k // tk

    def body(*refs):
        a_ref, b_ref = refs[0:2]
        c_ref = refs[2] if acc_in is not None else None
        o_ref = refs[3] if acc_in is not None else refs[2]

        def finish(r):
            if acc_in is not None:
                r = r + acc_scale * c_ref[...]
            o_ref[...] = r.astype(o_ref.dtype)

        if nk == 1:
            finish(_dot(a_ref[...], b_ref[...]))
            return
        acc_ref = refs[-1]
        kk = pl.program_id(2)

        @pl.when(kk == 0)
        def _():
            acc_ref[...] = jnp.zeros_like(acc_ref)

        acc_ref[...] += _dot(a_ref[...], b_ref[...])

        @pl.when(kk == nk - 1)
        def _():
            finish(acc_ref[...])

    in_specs = [pl.BlockSpec((tm, tk), lambda i, j, kk: (i, kk)), pl.BlockSpec((tk, tn), lambda i, j, kk: (kk, j))]
    args = [a, b]
    if acc_in is not None:
        in_specs.append(pl.BlockSpec((tm, tn), lambda i, j, kk: (i, j)))
        args.append(acc_in)
    return pl.pallas_call(
        body, name=name, grid=(m // tm, n // tn, nk), in_specs=in_specs,
        out_specs=pl.BlockSpec((tm, tn), lambda i, j, kk: (i, j)),
        out_shape=jax.ShapeDtypeStruct((m, n), out_dtype),
        scratch_shapes=[pltpu.VMEM((tm, tn), f32)] if nk > 1 else [],
        compiler_params=_cparams(("parallel", "parallel", "arbitrary")))(*args)


def mm_nt(a, b, name, out_dtype=f32, relu2=None, relu2_of=None):
    m, k = a.shape
    n = b.shape[0]
    tm = MM_TILE
    tn = _pick(n, (MM_TILE, 768, 512, 256, 128))

    def body(*refs):
        r = _dot_nt(refs[0][...], refs[1][...])
        if relu2:
            refs[2][...] = r
            pos = jnp.maximum(r, 0.0)
            refs[3][...] = (pos * pos).astype(refs[3].dtype)
        elif relu2_of is not None:
            refs[3][...] = (r * (2.0 * jnp.maximum(refs[2][...], 0.0))).astype(refs[3].dtype)
        else:
            refs[2][...] = r.astype(refs[2].dtype)

    blk = pl.BlockSpec((tm, tn), lambda i, j: (i, j))
    in_specs = [pl.BlockSpec((tm, k), lambda i, j: (i, 0)), pl.BlockSpec((tn, k), lambda i, j: (j, 0))]
    args = [a, b]
    if relu2_of is not None:
        in_specs.append(blk)
        args.append(relu2_of)
    if relu2:
        out_specs, out_shape = [blk, blk], [jax.ShapeDtypeStruct((m, n), f32), jax.ShapeDtypeStruct((m, n), bf16)]
    else:
        out_specs, out_shape = blk, jax.ShapeDtypeStruct((m, n), out_dtype)
    return pl.pallas_call(body, name=name, grid=(m // tm, n // tn), in_specs=in_specs, out_specs=out_specs,
                          out_shape=out_shape, compiler_params=_cparams(("parallel", "parallel")))(*args)


def mm_tn(a, b, name, acc_in=None, acc_scale=1.0):
    k, m = a.shape
    n = b.shape[1]
    tm = _pick(m, (MM_TILE, 768, 512, 256, 128))
    tn = _pick(n, (MM_TILE, 512, 256, 128))
    tk = _pick(k, (1024, 768, 512, 256, 128, 64))
    nk = k // tk

    def body(*refs):
        a_ref, b_ref, o_ref = refs[0], refs[1], refs[-1]
        kk = pl.program_id(2)

        @pl.when(kk == 0)
        def _():
            o_ref[...] = jnp.zeros_like(o_ref) if acc_in is None else acc_scale * refs[2][...]

        o_ref[...] += _dot_tn(a_ref[...], b_ref[...])

    in_specs = [pl.BlockSpec((tk, tm), lambda i, j, kk: (kk, i)), pl.BlockSpec((tk, tn), lambda i, j, kk: (kk, j))]
    args = [a, b]
    if acc_in is not None:
        in_specs.append(pl.BlockSpec((tm, tn), lambda i, j, kk: (i, j)))
        args.append(acc_in)
    return pl.pallas_call(
        body, name=name, grid=(m // tm, n // tn, nk), in_specs=in_specs,
        out_specs=pl.BlockSpec((tm, tn), lambda i, j, kk: (i, j)),
        out_shape=jax.ShapeDtypeStruct((m, n), f32),
        compiler_params=_cparams(("parallel", "parallel", "arbitrary")))(*args)


def _halo_specs(tb, cb, nt, off=0):
    r = tb // 8
    return [pl.BlockSpec((8, cb), lambda j, i: (jnp.maximum(i * r - 1, 0), j + off)),
            pl.BlockSpec((tb, cb), lambda j, i: (i, j + off)),
            pl.BlockSpec((8, cb), lambda j, i: (jnp.minimum((i + 1) * r, nt * r - 1), j + off))]


def _with_halo(prev_ref, own_ref, next_ref, i, nt):
    prev = jnp.where(i > 0, prev_ref[...].astype(f32), 0.0)
    nxt = jnp.where(i < nt - 1, next_ref[...].astype(f32), 0.0)
    return jnp.concatenate([prev, own_ref[...].astype(f32), nxt], axis=0)


def _shifted(xcat, s, tb):
    n = xcat.shape[0]
    return pltpu.roll(xcat, (-s) % n, 0)[8:8 + tb]


def conv_fwd(xbc, w8, b_row, tb=512, cb=512):
    t, c = xbc.shape
    nt = t // tb

    def body(prev_ref, own_ref, next_ref, w_ref, b_ref, o_ref):
        i = pl.program_id(1)
        xcat = _with_halo(prev_ref, own_ref, next_ref, i, nt)
        w = w_ref[...]
        pre = b_ref[...] + w[0:1] * _shifted(xcat, -2, tb)
        for k in range(1, D_CONV):
            pre = pre + w[k:k + 1] * _shifted(xcat, k - 2, tb)
        o_ref[...] = pre * _sigmoid(pre)

    return pl.pallas_call(
        body, name="conv_fwd", grid=(c // cb, nt),
        in_specs=_halo_specs(tb, cb, nt) + [pl.BlockSpec((8, cb), lambda j, i: (0, j)), pl.BlockSpec((1, cb), lambda j, i: (0, j))],
        out_specs=pl.BlockSpec((tb, cb), lambda j, i: (i, j)), out_shape=jax.ShapeDtypeStruct((t, c), f32),
        compiler_params=_cparams(("parallel", "parallel")))(xbc, xbc, xbc, w8, b_row)


def conv_bwd(xbc, xoff, grads, scales, w8, b_row, name, tb=512, cb=512):
    t, c = grads[0].shape
    nt = t // tb
    ng = len(grads)
    has_scale = [s is not None for s in scales]

    def body(*refs):
        i = pl.program_id(1)
        xr = refs[0:3]
        gr = [refs[3 + 3 * q: 6 + 3 * q] for q in range(ng)]
        pos = 3 + 3 * ng
        sr = []
        for q in range(ng):
            if has_scale[q]:
                sr.append(refs[pos])
                pos += 1
            else:
                sr.append(None)
        w_ref, b_ref, dx_ref, dw_ref, db_ref = refs[pos:pos + 5]
        xcat = _with_halo(*xr, i, nt)
        gcat = None
        for q in range(ng):
            gq = _with_halo(*gr[q], i, nt)
            if sr[q] is not None:
                gq = gq * sr[q][...]
            gcat = gq if gcat is None else gcat + gq
        w = w_ref[...]
        n = tb + 16
        pre = b_ref[...] + w[0:1] * pltpu.roll(xcat, 2, 0)
        for k in range(1, D_CONV):
            pre = pre + w[k:k + 1] * pltpu.roll(xcat, (2 - k) % n, 0)
        sg = _sigmoid(pre)
        dpre = gcat * sg * (1.0 + pre * (1.0 - sg))
        dx = w[0:1] * _shifted(dpre, 2, tb)
        for k in range(1, D_CONV):
            dx = dx + w[k:k + 1] * _shifted(dpre, 2 - k, tb)
        dx_ref[...] = dx.astype(dx_ref.dtype)
        dp_own = dpre[8:8 + tb]
        rows = [jnp.sum(dp_own * _shifted(xcat, k - 2, tb), axis=0, keepdims=True) for k in range(D_CONV)]
        dw = jnp.concatenate(rows + [jnp.zeros((8 - D_CONV, cb), f32)], axis=0)
        db = jnp.sum(dp_own, axis=0, keepdims=True)

        @pl.when(i == 0)
        def _():
            dw_ref[...] = jnp.zeros_like(dw_ref)
            db_ref[...] = jnp.zeros_like(db_ref)

        dw_ref[...] += dw
        db_ref[...] += db

    in_specs = _halo_specs(tb, cb, nt, xoff)
    args = [xbc] * 3
    for g in grads:
        in_specs += _halo_specs(tb, cb, nt)
        args += [g] * 3
    for s in scales:
        if s is not None:
            in_specs.append(pl.BlockSpec((1, cb), lambda j, i: (0, j)))
            args.append(s)
    in_specs += [pl.BlockSpec((8, cb), lambda j, i: (0, j)), pl.BlockSpec((1, cb), lambda j, i: (0, j))]
    args += [w8, b_row]
    return pl.pallas_call(
        body, name=name, grid=(c // cb, nt), in_specs=in_specs,
        out_specs=[pl.BlockSpec((tb, cb), lambda j, i: (i, j)), pl.BlockSpec((8, cb), lambda j, i: (0, j)),
                   pl.BlockSpec((1, cb), lambda j, i: (0, j))],
        out_shape=[jax.ShapeDtypeStruct((t, c), bf16), jax.ShapeDtypeStruct((8, c), f32), jax.ShapeDtypeStruct((1, c), f32)],
        compiler_params=_cparams(("parallel", "arbitrary")))(*args)


def dt_fwd(u_dt, bias_row, tb=1024):
    t = u_dt.shape[0]

    def body(u_ref, b_ref, o_ref):
        v = u_ref[...] + b_ref[...]
        sp = jnp.maximum(v, 0.0) + jnp.log(1.0 + jnp.exp(-jnp.abs(v)))
        lane = lax.broadcasted_iota(jnp.int32, v.shape, 1)
        o_ref[...] = jnp.where((lane & 127) < SSD_HEADS, sp, 0.0)

    return pl.pallas_call(
        body, name="dt_fwd", grid=(t // tb,),
        in_specs=[pl.BlockSpec((tb, 256), lambda i: (i, 0)), pl.BlockSpec((1, 256), lambda i: (0, 0))],
        out_specs=pl.BlockSpec((tb, 256), lambda i: (i, 0)), out_shape=jax.ShapeDtypeStruct((t, 256), f32),
        compiler_params=_cparams(("parallel",)))(u_dt, bias_row)


def dt_bwd(ddt_f, ddt_b, u_dt, bias_row, tb=1024):
    t = u_dt.shape[0]

    def body(gf_ref, gb_ref, u_ref, b_ref, du_ref, db_ref):
        g = jnp.concatenate([jnp.sum(gf_ref[...], axis=0), jnp.sum(gb_ref[...], axis=0)], axis=1)
        du = g * _sigmoid(u_ref[...] + b_ref[...])
        du_ref[...] = du.astype(du_ref.dtype)

        @pl.when(pl.program_id(0) == 0)
        def _():
            db_ref[...] = jnp.zeros_like(db_ref)

        db_ref[...] += jnp.sum(du, axis=0, keepdims=True)

    return pl.pallas_call(
        body, name="dt_bwd", grid=(t // tb,),
        in_specs=[pl.BlockSpec((4, tb, 128), lambda i: (0, i, 0)), pl.BlockSpec((4, tb, 128), lambda i: (0, i, 0)),
                  pl.BlockSpec((tb, 256), lambda i: (i, 0)), pl.BlockSpec((1, 256), lambda i: (0, 0))],
        out_specs=[pl.BlockSpec((tb, 256), lambda i: (i, 0)), pl.BlockSpec((1, 256), lambda i: (0, 0))],
        out_shape=[jax.ShapeDtypeStruct((t, 256), bf16), jax.ShapeDtypeStruct((1, 256), f32)],
        compiler_params=_cparams(("arbitrary",)))(ddt_f, ddt_b, u_dt, bias_row)


def _ssd_common(dt_blk, a_row, reverse):
    row = lax.broadcasted_iota(jnp.int32, (CHUNK, CHUNK), 0)
    col = lax.broadcasted_iota(jnp.int32, (CHUNK, CHUNK), 1)
    mask = (row <= col) if reverse else (row >= col)
    tri = mask.astype(f32)
    a = dt_blk * a_row
    acs = _dot_exact(tri, a)
    atot = jnp.sum(a, axis=0, keepdims=True)
    return mask, tri, a, acs, atot, col


def _lane_col(mat, lane_idx, h):
    return jnp.sum(jnp.where(lane_idx == h, mat, 0.0), axis=1, keepdims=True)


def ssd_fwd(xbc_c, dt2, a_rows, reverse, name):
    t = xbc_c.shape[0]
    nc = t // CHUNK
    d_off = 1 if reverse else 0

    def cidx(c):
        return nc - 1 - c if reverse else c

    def body(x_ref, b_ref, c_ref, dt_ref, a_ref, y_ref, hp_ref, h_scr, acst_scr):
        g = pl.program_id(0)
        c = pl.program_id(1)

        @pl.when(c == 0)
        def _():
            h_scr[...] = jnp.zeros_like(h_scr)

        dt_blk = dt_ref[...]
        mask, tri, a, acs, atot, lane = _ssd_common(dt_blk, a_ref[...], reverse)
        acst_scr[...] = acs.T
        bm = b_ref[...]
        cm = c_ref[...]
        cb = _dot_nt(cm, bm)
        half = lane >= SSD_HEAD_DIM
        sub_half = lax.broadcasted_iota(jnp.int32, (CHUNK, 1), 0) >= SSD_HEAD_DIM
        for j in range(4):
            x = x_ref[:, 128 * j:128 * (j + 1)]
            cols, dts, tots = [], [], []
            y = None
            for e in range(2):
                h = 8 * g + 2 * j + e
                col_h = _lane_col(acs, lane, h)
                row_h = acst_scr[pl.ds(h, 1), :]
                dt_h = _lane_col(dt_blk, lane, h)
                lmat = jnp.where(mask, jnp.exp(jnp.where(mask, col_h - row_h, 0.0)), 0.0)
                xdt_e = jnp.where(half == (e == 1), x * dt_h, 0.0)
                ye = _dot(cb * lmat, xdt_e)
                y = ye if y is None else y + ye
                cols.append(col_h)
                dts.append(dt_h)
                tots.append(jnp.sum(jnp.where(lane[0:1] == h, atot, 0.0), axis=1, keepdims=True))
            hp = h_scr[j]
            hp_ref[0, j] = hp
            ecol = jnp.where(half, jnp.exp(cols[1]), jnp.exp(cols[0]))
            y = y + _dot_nt(cm, hp) * ecol
            y_ref[:, 128 * j:128 * (j + 1)] = y
            dec = jnp.where(half, jnp.exp(tots[1] - cols[1]), jnp.exp(tots[0] - cols[0]))
            xdt = x * jnp.where(half, dts[1], dts[0])
            s_new = _dot_tn(xdt * dec, bm)
            cd = jnp.where(sub_half, jnp.exp(tots[1]), jnp.exp(tots[0]))
            h_scr[j] = cd * hp + s_new

    return pl.pallas_call(
        body, name=name, grid=(SSD_GROUPS, nc),
        in_specs=[pl.BlockSpec((CHUNK, 512), lambda g, c: (cidx(c), g)),
                  pl.BlockSpec((CHUNK, 128), lambda g, c: (cidx(c), 16 + g)),
                  pl.BlockSpec((CHUNK, 128), lambda g, c: (cidx(c), 20 + g)),
                  pl.BlockSpec((CHUNK, 128), lambda g, c: (cidx(c), d_off)),
                  pl.BlockSpec((1, 128), lambda g, c: (0, d_off))],
        out_specs=[pl.BlockSpec((CHUNK, 512), lambda g, c: (cidx(c), g)),
                   pl.BlockSpec((1, 4, 128, 128), lambda g, c: (cidx(c), g, 0, 0))],
        out_shape=[jax.ShapeDtypeStruct((t, D_INNER), f32), jax.ShapeDtypeStruct((nc, 16, 128, 128), f32)],
        scratch_shapes=[pltpu.VMEM((4, 128, 128), f32), pltpu.VMEM((CHUNK, CHUNK), f32)],
        compiler_params=_cparams(("parallel", "arbitrary")))(xbc_c, xbc_c, xbc_c, dt2, a_rows)


def ssd_bwd(xbc_c, dt2, a_rows, dy, hprev, reverse, name):
    t = xbc_c.shape[0]
    nc = t // CHUNK
    d_off = 1 if reverse else 0

    def cidx(c):
        return c if reverse else nc - 1 - c

    def body(x_ref, b_ref, c_ref, dt_ref, a_ref, dy_ref, hp_ref, dx_ref, db_ref, dc_ref, ddt_ref, da_ref,
             dh_scr, acst_scr):
        g = pl.program_id(0)
        c = pl.program_id(1)

        @pl.when(c == 0)
        def _():
            dh_scr[...] = jnp.zeros_like(dh_scr)
            da_ref[...] = jnp.zeros_like(da_ref)

        dt_blk = dt_ref[...]
        a_row = a_ref[...]
        mask, tri, a, acs, atot, lane = _ssd_common(dt_blk, a_row, reverse)
        acst_scr[...] = acs.T
        sub = lax.broadcasted_iota(jnp.int32, (CHUNK, CHUNK), 0)
        bm = b_ref[...]
        cm = c_ref[...]
        cb = _dot_nt(cm, bm)
        half = lane >= SSD_HEAD_DIM
        sub_half = sub[:, 0:1] >= SSD_HEAD_DIM
        dcb = jnp.zeros((CHUNK, CHUNK), f32)
        dacs = jnp.zeros((CHUNK, CHUNK), f32)
        dacs_t = jnp.zeros((CHUNK, CHUNK), f32)
        dtot = jnp.zeros((1, CHUNK), f32)
        ddt_x = jnp.zeros((CHUNK, CHUNK), f32)
        dbm = jnp.zeros((CHUNK, D_STATE), f32)
        dcm = jnp.zeros((CHUNK, D_STATE), f32)
        for j in range(4):
            x = x_ref[:, 128 * j:128 * (j + 1)]
            dyp = dy_ref[:, 128 * j:128 * (j + 1)]
            hp = hp_ref[0, j]
            dhn = dh_scr[j]
            cols, dts, tots, hs = [], [], [], []
            dxdt = None
            for e in range(2):
                h = 8 * g + 2 * j + e
                sel = half == (e == 1)
                col_h = _lane_col(acs, lane, h)
                row_h = acst_scr[pl.ds(h, 1), :]
                dt_h = _lane_col(dt_blk, lane, h)
                lmat = jnp.where(mask, jnp.exp(jnp.where(mask, col_h - row_h, 0.0)), 0.0)
                xdt_e = jnp.where(sel, x * dt_h, 0.0)
                dy_e = jnp.where(sel, dyp, 0.0)
                ml = _dot_nt(dy_e, xdt_e) * lmat
                dcb = dcb + ml
                w = ml * cb
                dacs = dacs + jnp.where(lane == h, jnp.sum(w, axis=1, keepdims=True), 0.0)
                dacs_t = dacs_t - jnp.where(sub == h, jnp.sum(w, axis=0, keepdims=True), 0.0)
                de = _dot_tn(cb * lmat, dy_e)
                dxdt = de if dxdt is None else dxdt + de
                cols.append(col_h)
                dts.append(dt_h)
                tots.append(jnp.sum(jnp.where(lane[0:1] == h, atot, 0.0), axis=1, keepdims=True))
                hs.append(h)
            ecol = jnp.where(half, jnp.exp(cols[1]), jnp.exp(cols[0]))
            dec = jnp.where(half, jnp.exp(tots[1] - cols[1]), jnp.exp(tots[0] - cols[0]))
            cd = jnp.where(sub_half, jnp.exp(tots[1]), jnp.exp(tots[0]))
            dtp = jnp.where(half, dts[1], dts[0])
            xdt = x * dtp
            yoff = _dot_nt(cm, hp) * ecol
            dye = dyp * ecol
            dcm = dcm + _dot(dye, hp)
            dhp = _dot_tn(dye, cm)
            gmat = _dot_nt(bm, dhn)
            dxdt = dxdt + dec * gmat
            dbm = dbm + _dot(xdt * dec, dhn)
            r_off = dyp * yoff
            r_dec = xdt * gmat * dec
            r_x = dxdt * x
            hh = dhn * hp
            for e in range(2):
                sel = half == (e == 1)
                h = hs[e]
                s_off = jnp.sum(jnp.where(sel, r_off, 0.0), axis=1, keepdims=True)
                s_dec = jnp.sum(jnp.where(sel, r_dec, 0.0), axis=1, keepdims=True)
                dacs = dacs + jnp.where(lane == h, s_off - s_dec, 0.0)
                dcd = jnp.sum(jnp.sum(jnp.where(sub_half == (e == 1), hh, 0.0), axis=1, keepdims=True), axis=0, keepdims=True)
                tot_e = jnp.sum(s_dec, axis=0, keepdims=True) + jnp.exp(tots[e]) * dcd
                dtot = dtot + jnp.where(lane[0:1] == h, tot_e, 0.0)
                ddt_x = ddt_x + jnp.where(lane == h, jnp.sum(jnp.where(sel, r_x, 0.0), axis=1, keepdims=True), 0.0)
            dx_ref[:, 128 * j:128 * (j + 1)] = dxdt * dtp
            dh_scr[j] = cd * dhn + dhp
        dcm = dcm + _dot(dcb, bm)
        dbm = dbm + _dot_tn(dcb, cm)
        db_ref[...] = dbm
        dc_ref[...] = dcm
        dacs = dacs + dacs_t.T
        da = _dot_exact(tri.T, dacs) + dtot
        ddt_ref[0] = da * a_row + ddt_x
        da_ref[0] += jnp.sum(da * dt_blk, axis=0, keepdims=True)

    return pl.pallas_call(
        body, name=name, grid=(SSD_GROUPS, nc),
        in_specs=[pl.BlockSpec((CHUNK, 512), lambda g, c: (cidx(c), g)),
                  pl.BlockSpec((CHUNK, 128), lambda g, c: (cidx(c), 16 + g)),
                  pl.BlockSpec((CHUNK, 128), lambda g, c: (cidx(c), 20 + g)),
                  pl.BlockSpec((CHUNK, 128), lambda g, c: (cidx(c), d_off)),
                  pl.BlockSpec((1, 128), lambda g, c: (0, d_off)),
                  pl.BlockSpec((CHUNK, 512), lambda g, c: (cidx(c), g)),
                  pl.BlockSpec((1, 4, 128, 128), lambda g, c: (cidx(c), g, 0, 0))],
        out_specs=[pl.BlockSpec((CHUNK, 512), lambda g, c: (cidx(c), g)),
                   pl.BlockSpec((CHUNK, 128), lambda g, c: (cidx(c), g)),
                   pl.BlockSpec((CHUNK, 128), lambda g, c: (cidx(c), g)),
                   pl.BlockSpec((1, CHUNK, 128), lambda g, c: (g, cidx(c), 0)),
                   pl.BlockSpec((1, 1, 128), lambda g, c: (g, 0, 0))],
        out_shape=[jax.ShapeDtypeStruct((t, D_INNER), f32), jax.ShapeDtypeStruct((t, 512), f32),
                   jax.ShapeDtypeStruct((t, 512), f32), jax.ShapeDtypeStruct((4, t, 128), f32),
                   jax.ShapeDtypeStruct((4, 1, 128), f32)],
        scratch_shapes=[pltpu.VMEM((4, 128, 128), f32), pltpu.VMEM((CHUNK, CHUNK), f32)],
        compiler_params=_cparams(("parallel", "arbitrary")))(xbc_c, xbc_c, xbc_c, dt2, a_rows, dy, hprev)


def tail_fwd(y_f, y_b, xbc_c, z, dskip_row, nw_row, tb=512):
    t = y_f.shape[0]

    def body(yf_ref, yb_ref, x_ref, z_ref, d_ref, w_ref, o_ref):
        zz = z_ref[...]
        y = (yf_ref[...] + yb_ref[...] + d_ref[...] * x_ref[...]) * (zz * _sigmoid(zz))
        rstd = lax.rsqrt(jnp.mean(y * y, axis=1, keepdims=True) + NORM_EPS)
        o_ref[...] = (y * rstd * w_ref[...]).astype(o_ref.dtype)

    blk = pl.BlockSpec((tb, 512), lambda i, g: (i, g))
    row = pl.BlockSpec((1, 512), lambda i, g: (0, g))
    return pl.pallas_call(
        body, name="tail_fwd", grid=(t // tb, SSD_GROUPS), in_specs=[blk, blk, blk, blk, row, row], out_specs=blk,
        out_shape=jax.ShapeDtypeStruct((t, D_INNER), bf16),
        compiler_params=_cparams(("parallel", "parallel")))(y_f, y_b, xbc_c, z, dskip_row, nw_row)


def tail_bwd(dyn, y_f, y_b, xbc_c, z, dskip_row, nw_row, tb=512):
    t = y_f.shape[0]

    def body(g_ref, yf_ref, yb_ref, x_ref, z_ref, d_ref, w_ref, dy_ref, dz_ref, dw_ref, dd_ref):
        zz = z_ref[...]
        sg = _sigmoid(zz)
        sl = zz * sg
        x = x_ref[...]
        y = yf_ref[...] + yb_ref[...] + d_ref[...] * x
        yz = y * sl
        rstd = lax.rsqrt(jnp.mean(yz * yz, axis=1, keepdims=True) + NORM_EPS)
        yhat = yz * rstd
        g = g_ref[...]
        dyhat = g * w_ref[...]
        dyz = rstd * (dyhat - yhat * jnp.mean(dyhat * yhat, axis=1, keepdims=True))
        dy = dyz * sl
        dy_ref[...] = dy
        dz_ref[...] = (dyz * y * sg * (1.0 + zz * (1.0 - sg))).astype(dz_ref.dtype)

        @pl.when(pl.program_id(1) == 0)
        def _():
            dw_ref[...] = jnp.zeros_like(dw_ref)
            dd_ref[...] = jnp.zeros_like(dd_ref)

        dw_ref[...] += jnp.sum(g * yhat, axis=0, keepdims=True)
        dd_ref[...] += jnp.sum(dy * x, axis=0, keepdims=True)

    blk = pl.BlockSpec((tb, 512), lambda g, i: (i, g))
    row = pl.BlockSpec((1, 512), lambda g, i: (0, g))
    return pl.pallas_call(
        body, name="tail_bwd", grid=(SSD_GROUPS, t // tb), in_specs=[blk, blk, blk, blk, blk, row, row],
        out_specs=[blk, blk, row, row],
        out_shape=[jax.ShapeDtypeStruct((t, D_INNER), f32), jax.ShapeDtypeStruct((t, D_INNER), bf16),
                   jax.ShapeDtypeStruct((1, D_INNER), f32), jax.ShapeDtypeStruct((1, D_INNER), f32)],
        compiler_params=_cparams(("parallel", "arbitrary")))(dyn, y_f, y_b, xbc_c, z, dskip_row, nw_row)


def _slopes(p):
    return [2.0 ** (-8.0 * (HEADS_PER_PATTERN * p + j + 1) / ATTN_HEADS) for j in range(HEADS_PER_PATTERN)]


def _win_specs(nq, col_of):
    return [pl.BlockSpec((64, 256), lambda r, i: (jnp.maximum(2 * i - 1, 0), col_of(r))),
            pl.BlockSpec((128, 256), lambda r, i: (i, col_of(r))),
            pl.BlockSpec((64, 256), lambda r, i: (jnp.minimum(2 * i + 2, 2 * nq - 1), col_of(r)))]


def _lane_head(shape):
    return lax.broadcasted_iota(jnp.int32, shape, 1) >> 6


def _stack_heads(m):
    lane_head = _lane_head(m.shape)
    return jnp.concatenate([jnp.where(lane_head == j, m, 0.0) for j in range(HEADS_PER_PATTERN)], axis=0)


def _unstack_heads(m4, n):
    lane_head = _lane_head((n, 256))
    out = jnp.where(lane_head == 0, m4[0:n], 0.0)
    for j in range(1, HEADS_PER_PATTERN):
        out = out + jnp.where(lane_head == j, m4[j * n:(j + 1) * n], 0.0)
    return out


def _head_cols(m, n):
    lane = lax.broadcasted_iota(jnp.int32, (n, 256), 1)
    return jnp.concatenate([jnp.sum(jnp.where(lane == ATTN_HEAD_DIM * j, m, 0.0), axis=1, keepdims=True)
                            for j in range(HEADS_PER_PATTERN)], axis=0)


def _score_bias(p, dil, by_key):
    slopes = np.asarray(_slopes(p), np.float32)
    if by_key:
        win = np.arange(256)[:, None]
        rel = np.arange(128)[None, :] - (win - 64)
    else:
        win = np.arange(256)[None, :]
        rel = win - 64 - np.arange(128)[:, None]
    band = np.abs(rel) <= 64
    out = []
    for first, last in ((False, False), (True, False), (False, True), (True, True)):
        ok = band & ~(first & (win < 64)) & ~(last & (win >= 192))
        pen = -slopes[:, None, None] * (np.abs(rel) * dil).astype(np.float32)[None]
        out.append(np.where(ok[None], pen, np.float32(NEG_BIG)).reshape(-1, rel.shape[1]))
    return jnp.asarray(np.stack(out), f32)


def _bias_spec(nq, rows, cols):
    return pl.BlockSpec((1, rows, cols), lambda r, i: ((i == 0).astype(jnp.int32) + 2 * (i == nq - 1).astype(jnp.int32), 0, 0))


def attn_fwd(q, k, v, p, dil, name):
    l = q.shape[0]
    nq = l // 128

    def body(q_ref, kp_ref, ko_ref, kn_ref, vp_ref, vo_ref, vn_ref, bias_ref, o_ref, lse_ref):
        kcat = jnp.concatenate([kp_ref[...], ko_ref[...], kn_ref[...]], axis=0)
        vcat = jnp.concatenate([vp_ref[...], vo_ref[...], vn_ref[...]], axis=0)
        s = _dot_nt(_stack_heads(q_ref[...] * 0.125), kcat) + bias_ref[0]
        m = jnp.max(s, axis=1, keepdims=True)
        pr = jnp.exp(s - m)
        den = jnp.sum(pr, axis=1, keepdims=True)
        o4 = _dot(pr, vcat) / den
        o_ref[...] = _unstack_heads(o4, 128)
        lse_ref[...] = _unstack_heads(jnp.broadcast_to(m + jnp.log(den), (512, 256)), 128)

    col = lambda r: r
    return pl.pallas_call(
        body, name=name, grid=(dil, nq),
        in_specs=[pl.BlockSpec((128, 256), lambda r, i: (i, r))] + _win_specs(nq, col) + _win_specs(nq, col)
        + [_bias_spec(nq, 512, 256)],
        out_specs=[pl.BlockSpec((128, 256), lambda r, i: (i, r))] * 2,
        out_shape=[jax.ShapeDtypeStruct(q.shape, f32)] * 2,
        compiler_params=_cparams(("parallel", "parallel")))(q, k, k, k, v, v, v, _score_bias(p, dil, False))


def attn_combine(os_, lses, tb=1024):
    t = os_[0].shape[0]

    def body(o0, o1, o2, l0, l1, l2, y_ref, lse_ref):
        a0, a1, a2 = l0[...], l1[...], l2[...]
        m = jnp.maximum(jnp.maximum(a0, a1), a2)
        e0, e1, e2 = jnp.exp(a0 - m), jnp.exp(a1 - m), jnp.exp(a2 - m)
        den = e0 + e1 + e2
        y_ref[...] = (e0 * o0[...] + e1 * o1[...] + e2 * o2[...]) / den
        lse_ref[...] = m + jnp.log(den)

    blk = pl.BlockSpec((tb, 256), lambda i: (i, 0))
    return pl.pallas_call(
        body, name="attn_combine", grid=(t // tb,), in_specs=[blk] * 6, out_specs=[blk, blk],
        out_shape=[jax.ShapeDtypeStruct((t, 256), f32)] * 2,
        compiler_params=_cparams(("parallel",)))(*os_, *lses)


def attn_delta(dy, y, tb=1024):
    t = dy.shape[0]

    def body(dy_ref, y_ref, d_ref):
        pr = dy_ref[...] * y_ref[...]
        lane_head = _lane_head(pr.shape)
        out = jnp.zeros_like(pr)
        for j in range(HEADS_PER_PATTERN):
            sj = jnp.sum(jnp.where(lane_head == j, pr, 0.0), axis=1, keepdims=True)
            out = out + jnp.where(lane_head == j, sj, 0.0)
        d_ref[...] = out

    blk = pl.BlockSpec((tb, 256), lambda i: (i, 0))
    return pl.pallas_call(body, name="attn_delta", grid=(t // tb,), in_specs=[blk, blk], out_specs=blk,
                          out_shape=jax.ShapeDtypeStruct((t, 256), f32),
                          compiler_params=_cparams(("parallel",)))(dy, y)


def attn_dq(q, k, v, dy, lse, delta, p, dil, name):
    l = q.shape[0]
    nq = l // 128

    def body(q_ref, kp_ref, ko_ref, kn_ref, vp_ref, vo_ref, vn_ref, dy_ref, lse_ref, d_ref, bias_ref, dq_ref):
        kcat = jnp.concatenate([kp_ref[...], ko_ref[...], kn_ref[...]], axis=0)
        vcat = jnp.concatenate([vp_ref[...], vo_ref[...], vn_ref[...]], axis=0)
        s = _dot_nt(_stack_heads(q_ref[...] * 0.125), kcat) + bias_ref[0]
        pr = jnp.exp(s - _head_cols(lse_ref[...], 128))
        dp = _dot_nt(_stack_heads(dy_ref[...]), vcat)
        ds = pr * (dp - _head_cols(d_ref[...], 128))
        dq_ref[...] = (_unstack_heads(_dot(ds, kcat), 128) * 0.125).astype(dq_ref.dtype)

    col = lambda r: r
    own = pl.BlockSpec((128, 256), lambda r, i: (i, r))
    return pl.pallas_call(
        body, name=name, grid=(dil, nq),
        in_specs=[own] + _win_specs(nq, col) + _win_specs(nq, col) + [own, own, own, _bias_spec(nq, 512, 256)],
        out_specs=own, out_shape=jax.ShapeDtypeStruct(q.shape, bf16),
        compiler_params=_cparams(("parallel", "parallel")))(q, k, k, k, v, v, v, dy, lse, delta, _score_bias(p, dil, False))


def attn_dkv(q, k, v, dy, lse, delta, p, dil, name):
    l = q.shape[0]
    nq = l // 128

    def body(qp_ref, qo_ref, qn_ref, gp_ref, go_ref, gn_ref, lp_ref, lo_ref, ln_ref, dp_ref, do_ref, dn_ref,
             k_ref, v_ref, bias_ref, dk_ref, dv_ref):
        cat = lambda a, b, c: jnp.concatenate([a[...], b[...], c[...]], axis=0)
        q4 = _stack_heads(cat(qp_ref, qo_ref, qn_ref) * 0.125)
        dy4 = _stack_heads(cat(gp_ref, go_ref, gn_ref))
        lse4 = _head_cols(cat(lp_ref, lo_ref, ln_ref), 256)
        del4 = _head_cols(cat(dp_ref, do_ref, dn_ref), 256)
        s = _dot_nt(q4, k_ref[...]) + bias_ref[0]
        pr = jnp.exp(s - lse4)
        dpm = _dot_nt(dy4, v_ref[...])
        ds = pr * (dpm - del4)
        dv_ref[...] = _dot_tn(pr, dy4).astype(dv_ref.dtype)
        dk_ref[...] = _dot_tn(ds, q4).astype(dk_ref.dtype)

    col = lambda r: r
    own = pl.BlockSpec((128, 256), lambda r, i: (i, r))
    win = _win_specs(nq, col)
    return pl.pallas_call(
        body, name=name, grid=(dil, nq), in_specs=win * 4 + [own, own, _bias_spec(nq, 1024, 128)], out_specs=[own, own],
        out_shape=[jax.ShapeDtypeStruct(q.shape, bf16)] * 2,
        compiler_params=_cparams(("parallel", "parallel")))(q, q, q, dy, dy, dy, lse, lse, lse, delta, delta, delta, k, v,
                                                            _score_bias(p, dil, True))


def _lanes(v, reps):
    return v if reps == 1 else jnp.tile(v, (1, reps))


def _lane_halo_specs(cb, tb, nt, off=0):
    r = tb // 128
    return [pl.BlockSpec((cb, 128), lambda j, i: (j + off, jnp.maximum(i * r - 1, 0))),
            pl.BlockSpec((cb, tb), lambda j, i: (j + off, i)),
            pl.BlockSpec((cb, 128), lambda j, i: (j + off, jnp.minimum((i + 1) * r, nt * r - 1)))]


def _with_lane_halo(prev_ref, own_ref, next_ref, i, nt):
    prev = jnp.where(i > 0, prev_ref[...].astype(f32), 0.0)
    nxt = jnp.where(i < nt - 1, next_ref[...].astype(f32), 0.0)
    return jnp.concatenate([prev, own_ref[...].astype(f32), nxt], axis=1)


def _lane_shifted(xcat, s, tb):
    n = xcat.shape[1]
    return pltpu.roll(xcat, (-s) % n, 1)[:, 128:128 + tb]


def conv_fwd_t(xbc_t, w_b, b_b, tb=1024, cb=256):
    c, t = xbc_t.shape
    nt = t // tb

    def body(prev_ref, own_ref, next_ref, w_ref, b_ref, o_ref):
        i = pl.program_id(1)
        xcat = _with_lane_halo(prev_ref, own_ref, next_ref, i, nt)
        reps = tb // 128
        pre = _lanes(b_ref[...], reps)
        for k in range(D_CONV):
            pre = pre + _lanes(w_ref[k], reps) * _lane_shifted(xcat, k - 2, tb)
        o_ref[...] = pre * _sigmoid(pre)

    return pl.pallas_call(
        body, name="conv_fwd", grid=(c // cb, nt),
        in_specs=_lane_halo_specs(cb, tb, nt) + [pl.BlockSpec((D_CONV, cb, 128), lambda j, i: (0, j, 0)),
                                                 pl.BlockSpec((cb, 128), lambda j, i: (j, 0))],
        out_specs=pl.BlockSpec((cb, tb), lambda j, i: (j, i)), out_shape=jax.ShapeDtypeStruct((c, t), f32),
        compiler_params=_cparams(("parallel", "parallel")))(xbc_t, xbc_t, xbc_t, w_b, b_b)


def conv_bwd_t(xbc_t, grad_t, w_b, b_b, into, name, row0, tb=1024, cb=256):
    c, t = grad_t.shape
    nt = t // tb
    off = row0 // cb
    reps = tb // 128

    def body(*refs):
        i = pl.program_id(1)
        xr, gr = refs[0:3], refs[3:6]
        w_ref, b_ref = refs[6:8]
        dx_ref, dw_ref, db_ref = refs[-3:]
        xcat = _with_lane_halo(*xr, i, nt)
        gcat = _with_lane_halo(*gr, i, nt)
        n = tb + 256
        wk = [_lanes(w_ref[k], reps + 2) for k in range(D_CONV)]
        pre = _lanes(b_ref[...], reps + 2)
        for k in range(D_CONV):
            pre = pre + wk[k] * pltpu.roll(xcat, (2 - k) % n, 1)
        sg = _sigmoid(pre)
        dpre = gcat * sg * (1.0 + pre * (1.0 - sg))
        dx = None
        for k in range(D_CONV):
            term = wk[k][:, 128:128 + tb] * _lane_shifted(dpre, 2 - k, tb)
            dx = term if dx is None else dx + term
        dx_ref[...] = dx.astype(dx_ref.dtype)
        dp_own = dpre[:, 128:128 + tb]

        def fold(v):
            s = v[:, 0:128]
            for q in range(1, reps):
                s = s + v[:, 128 * q:128 * (q + 1)]
            return s

        @pl.when(i == 0)
        def _():
            dw_ref[...] = jnp.zeros_like(dw_ref)
            db_ref[...] = jnp.zeros_like(db_ref)

        for k in range(D_CONV):
            dw_ref[k] += fold(dp_own * _lane_shifted(xcat, k - 2, tb))
        db_ref[...] += fold(dp_own)

    in_specs = (_lane_halo_specs(cb, tb, nt, off) + _lane_halo_specs(cb, tb, nt)
                + [pl.BlockSpec((D_CONV, cb, 128), lambda j, i: (0, j + off, 0)), pl.BlockSpec((cb, 128), lambda j, i: (j + off, 0))])
    args = [xbc_t] * 3 + [grad_t] * 3 + [w_b, b_b]
    aliases = {}
    if into is not None:
        in_specs.append(pl.BlockSpec(memory_space=pl.ANY))
        args.append(into)
        aliases = {len(args) - 1: 0}
    return pl.pallas_call(
        body, name=name, grid=(c // cb, nt), in_specs=in_specs,
        out_specs=[pl.BlockSpec((cb, tb), lambda j, i: (j + off, i)), pl.BlockSpec((D_CONV, cb, 128), lambda j, i: (0, j, 0)),
                   pl.BlockSpec((cb, 128), lambda j, i: (j, 0))],
        out_shape=[jax.ShapeDtypeStruct((CONV_DIM, t), bf16), jax.ShapeDtypeStruct((D_CONV, c, 128), f32),
                   jax.ShapeDtypeStruct((c, 128), f32)],
        input_output_aliases=aliases, compiler_params=_cparams(("parallel", "arbitrary")))(*args)


def dt_fwd_t(u_dt_t, bias_b, tb=2048):
    r, t = u_dt_t.shape

    def body(u_ref, b_ref, o_ref):
        v = u_ref[...] + _lanes(b_ref[...], tb // 128)
        o_ref[...] = jnp.maximum(v, 0.0) + jnp.log(1.0 + jnp.exp(-jnp.abs(v)))

    return pl.pallas_call(
        body, name="dt_fwd", grid=(t // tb,),
        in_specs=[pl.BlockSpec((r, tb), lambda i: (0, i)), pl.BlockSpec((r, 128), lambda i: (0, 0))],
        out_specs=pl.BlockSpec((r, tb), lambda i: (0, i)), out_shape=jax.ShapeDtypeStruct((r, t), f32),
        compiler_params=_cparams(("parallel",)))(u_dt_t, bias_b)


def dt_bwd_t(ddt_f, ddt_b, u_dt_t, bias_b, tb=2048):
    r, t = u_dt_t.shape
    reps = tb // 128

    def body(gf_ref, gb_ref, u_ref, b_ref, du_ref, db_ref):
        g = jnp.concatenate([gf_ref[...], gb_ref[...]], axis=0)
        du = g * _sigmoid(u_ref[...] + _lanes(b_ref[...], reps))
        du_ref[...] = du.astype(du_ref.dtype)

        @pl.when(pl.program_id(0) == 0)
        def _():
            db_ref[...] = jnp.zeros_like(db_ref)

        s = du[:, 0:128]
        for q in range(1, reps):
            s = s + du[:, 128 * q:128 * (q + 1)]
        db_ref[...] += s

    half = pl.BlockSpec((r // 2, tb), lambda i: (0, i))
    return pl.pallas_call(
        body, name="dt_bwd", grid=(t // tb,),
        in_specs=[half, half, pl.BlockSpec((r, tb), lambda i: (0, i)), pl.BlockSpec((r, 128), lambda i: (0, 0))],
        out_specs=[pl.BlockSpec((r, tb), lambda i: (0, i)), pl.BlockSpec((r, 128), lambda i: (0, 0))],
        out_shape=[jax.ShapeDtypeStruct((r, t), bf16), jax.ShapeDtypeStruct((r, 128), f32)],
        compiler_params=_cparams(("arbitrary",)))(ddt_f, ddt_b, u_dt_t, bias_b)


HEADS_PER_GROUP = SSD_HEADS // SSD_GROUPS


def _group_rows(g, n):
    return pl.ds(pl.multiple_of(g * n, n), n)


def _ssd_decays(dt_blk, a_blk, reverse):
    row = lax.broadcasted_iota(jnp.int32, (CHUNK, CHUNK), 0)
    col = lax.broadcasted_iota(jnp.int32, (CHUNK, CHUNK), 1)
    mask = (row <= col) if reverse else (row >= col)
    tri = mask.astype(f32)
    a8 = dt_blk * a_blk
    acs_t = lax.dot_general(a8, tri, (((1,), (1,)), ((), ())), precision=lax.Precision.HIGHEST,
                            preferred_element_type=f32)
    acs = jnp.concatenate([acs_t, jnp.zeros((CHUNK - HEADS_PER_GROUP, CHUNK), f32)], axis=0).T
    return mask, tri, a8, acs, acs_t, col


def ssd_fwd_t(xbc_ct, dt_t, a_b, reverse, name, prev=None):
    t = xbc_ct.shape[1]
    nc = t // CHUNK
    direction = 1 if reverse else 0

    def cidx(c):
        return nc - 1 - c if reverse else c

    def body(*refs):
        x_ref, b_ref, c_ref, dt_ref, a_ref = refs[0:5]
        prev_ref = refs[5] if prev is not None else None
        y_ref, hp_ref, h_scr = refs[-3:]

        @pl.when(pl.program_id(0) == 0)
        def _():
            h_scr[...] = jnp.zeros_like(h_scr)

        def group(g, carry):
            x_v, y_v = x_ref.at[_group_rows(g, 512)], y_ref.at[_group_rows(g, 512)]
            heads = _group_rows(g, HEADS_PER_GROUP)
            hp_v, h_v = hp_ref.at[0, heads], h_scr.at[heads]
            dt_blk = dt_ref[heads, :]
            mask, tri, a8, acs, acs_t, lane = _ssd_decays(dt_blk, a_ref[heads, :], reverse)
            bm = b_ref[_group_rows(g, 128), :].T
            cm = c_ref[_group_rows(g, 128), :].T
            cb = _dot_nt(cm, bm)
            tot = jnp.sum(a8, axis=1, keepdims=True)
            for j in range(HEADS_PER_GROUP):
                rows = slice(SSD_HEAD_DIM * j, SSD_HEAD_DIM * (j + 1))
                col_j = _lane_col(acs, lane, j)
                row_j = acs_t[j:j + 1, :]
                lmat = jnp.where(mask, jnp.exp(jnp.where(mask, col_j - row_j, 0.0)), 0.0)
                xdt = x_v[rows, :] * dt_blk[j:j + 1, :]
                hp = h_v[j]
                hp_v[j] = hp
                y = _dot_nt(xdt, cb * lmat) + _dot_nt(hp, cm) * jnp.exp(row_j)
                if prev_ref is not None:
                    y = y + prev_ref.at[_group_rows(g, 512)][rows, :]
                y_v[rows, :] = y
                tot_j = tot[j:j + 1, :]
                h_v[j] = jnp.exp(tot_j) * hp + _dot(xdt * jnp.exp(tot_j - row_j), bm)
            return carry

        lax.fori_loop(0, SSD_GROUPS, group, 0)

    big = pl.BlockSpec((D_INNER, CHUNK), lambda c: (0, cidx(c)))
    in_specs = [big, pl.BlockSpec((512, CHUNK), lambda c: (4, cidx(c))), pl.BlockSpec((512, CHUNK), lambda c: (5, cidx(c))),
                pl.BlockSpec((SSD_HEADS, CHUNK), lambda c: (direction, cidx(c))),
                pl.BlockSpec((SSD_HEADS, 128), lambda c: (direction, 0))]
    args = [xbc_ct, xbc_ct, xbc_ct, dt_t, a_b]
    if prev is not None:
        in_specs.append(big)
        args.append(prev)
    return pl.pallas_call(
        body, name=name, grid=(nc,), in_specs=in_specs,
        out_specs=[big, pl.BlockSpec((1, SSD_HEADS, SSD_HEAD_DIM, D_STATE), lambda c: (cidx(c), 0, 0, 0))],
        out_shape=[jax.ShapeDtypeStruct((D_INNER, t), f32), jax.ShapeDtypeStruct((nc, SSD_HEADS, SSD_HEAD_DIM, D_STATE), f32)],
        scratch_shapes=[pltpu.VMEM((SSD_HEADS, SSD_HEAD_DIM, D_STATE), f32)],
        compiler_params=_cparams(("arbitrary",)))(*args)


def ssd_bwd_t(xbc_ct, dt_t, a_b, dy_t, hprev, reverse, name, skip_b=None, prev=None):
    t = xbc_ct.shape[1]
    nc = t // CHUNK
    direction = 1 if reverse else 0

    def cidx(c):
        return c if reverse else nc - 1 - c

    def body(*refs):
        x_ref, b_ref, c_ref, dt_ref, a_ref, dy_ref, hp_ref = refs[0:7]
        pos = 7
        skip_ref = None
        if skip_b is not None:
            skip_ref = refs[pos]
            pos += 1
        prev_refs = None
        if prev is not None:
            prev_refs = refs[pos:pos + 3]
            pos += 3
        dx_ref, db_ref, dc_ref, ddt_ref, da_ref, dh_scr = refs[pos:pos + 6]

        @pl.when(pl.program_id(0) == 0)
        def _():
            dh_scr[...] = jnp.zeros_like(dh_scr)
            da_ref[...] = jnp.zeros_like(da_ref)

        def group(g, carry):
            big, st, heads = _group_rows(g, 512), _group_rows(g, 128), _group_rows(g, HEADS_PER_GROUP)
            x_v, dy_v, dx_v = x_ref.at[big], dy_ref.at[big], dx_ref.at[big]
            hp_v, dh_v = hp_ref.at[0, heads], dh_scr.at[heads]
            dt_blk = dt_ref[heads, :]
            a_blk = a_ref[heads, :]
            mask, tri, a8, acs, acs_t, lane = _ssd_decays(dt_blk, a_blk, reverse)
            sub = lax.broadcasted_iota(jnp.int32, (CHUNK, CHUNK), 0)
            mask_t = (sub >= lane) if reverse else (sub <= lane)
            bm = b_ref[st, :].T
            cm = c_ref[st, :].T
            cb = _dot_nt(cm, bm)
            cb_t = _dot_nt(bm, cm)
            tot = jnp.sum(a8, axis=1, keepdims=True)
            dcb = jnp.zeros((CHUNK, CHUNK), f32)
            dbm = jnp.zeros((CHUNK, D_STATE), f32)
            dcm = jnp.zeros((CHUNK, D_STATE), f32)
            dacs_rows, ddtx_rows = [], []
            for j in range(HEADS_PER_GROUP):
                rows = slice(SSD_HEAD_DIM * j, SSD_HEAD_DIM * (j + 1))
                col_j = _lane_col(acs, lane, j)
                row_j = acs_t[j:j + 1, :]
                dt_j = dt_blk[j:j + 1, :]
                tot_j = tot[j:j + 1, :]
                lmat = jnp.where(mask, jnp.exp(jnp.where(mask, col_j - row_j, 0.0)), 0.0)
                lmat_t = jnp.where(mask_t, jnp.exp(jnp.where(mask_t, row_j - col_j, 0.0)), 0.0)
                x = x_v[rows, :]
                xdt = x * dt_j
                dyh = dy_v[rows, :]
                hp = hp_v[j]
                dhn = dh_v[j]
                ml = _dot_tn(dyh, xdt) * lmat
                w_t = _dot_tn(xdt, dyh) * lmat_t * cb_t
                dcb = dcb + ml
                dacs = jnp.sum(w_t, axis=0, keepdims=True) - jnp.sum(ml * cb, axis=0, keepdims=True)
                ecol = jnp.exp(row_j)
                dec = jnp.exp(tot_j - row_j)
                dye = dyh * ecol
                yoff = _dot_nt(hp, cm) * ecol
                gmat = _dot_nt(dhn, bm)
                dxdt = _dot(dyh, cb * lmat) + dec * gmat
                s_dec = jnp.sum(xdt * gmat, axis=0, keepdims=True) * dec
                dacs = dacs + jnp.sum(dyh * yoff, axis=0, keepdims=True) - s_dec
                dcd = jnp.sum(jnp.sum(dhn * hp, axis=1, keepdims=True), axis=0, keepdims=True)
                dtot = jnp.sum(s_dec, axis=1, keepdims=True) + jnp.exp(tot_j) * dcd
                dacs_rows.append((dacs, dtot))
                ddtx_rows.append(jnp.sum(dxdt * x, axis=0, keepdims=True))
                dcm = dcm + _dot_tn(dye, hp)
                dbm = dbm + _dot_tn(xdt * dec, dhn)
                dxh = dxdt * dt_j
                if skip_ref is not None:
                    dxh = dxh + skip_ref.at[big][rows, :] * dyh
                if prev_refs is not None:
                    dxh = dxh + prev_refs[0].at[big][rows, :]
                dx_v[rows, :] = dxh
                dh_v[j] = jnp.exp(tot_j) * dhn + _dot(dye, cm)
            dcm = dcm + _dot(dcb, bm)
            dbm = dbm + _dot_tn(dcb, cm)
            dbt, dct = dbm.T, dcm.T
            if prev_refs is not None:
                dbt = dbt + prev_refs[1][st, :]
                dct = dct + prev_refs[2][st, :]
            db_ref[st, :] = dbt
            dc_ref[st, :] = dct
            dacs8 = jnp.concatenate([d for d, _ in dacs_rows], axis=0)
            dtot8 = jnp.concatenate([d for _, d in dacs_rows], axis=0)
            da8 = _dot_exact(dacs8, tri) + dtot8
            ddt_ref[heads, :] = da8 * a_blk + jnp.concatenate(ddtx_rows, axis=0)
            da_ref[heads, :] += da8 * dt_blk
            return carry

        lax.fori_loop(0, SSD_GROUPS, group, 0)

    big = pl.BlockSpec((D_INNER, CHUNK), lambda c: (0, cidx(c)))
    st = pl.BlockSpec((512, CHUNK), lambda c: (0, cidx(c)))
    in_specs = [big, pl.BlockSpec((512, CHUNK), lambda c: (4, cidx(c))), pl.BlockSpec((512, CHUNK), lambda c: (5, cidx(c))),
                pl.BlockSpec((SSD_HEADS, CHUNK), lambda c: (direction, cidx(c))),
                pl.BlockSpec((SSD_HEADS, 128), lambda c: (direction, 0)), big,
                pl.BlockSpec((1, SSD_HEADS, SSD_HEAD_DIM, D_STATE), lambda c: (cidx(c), 0, 0, 0))]
    args = [xbc_ct, xbc_ct, xbc_ct, dt_t, a_b, dy_t, hprev]
    if skip_b is not None:
        in_specs.append(pl.BlockSpec((D_INNER, 128), lambda c: (0, 0)))
        args.append(skip_b)
    if prev is not None:
        in_specs += [big, st, st]
        args += list(prev)
    return pl.pallas_call(
        body, name=name, grid=(nc,), in_specs=in_specs,
        out_specs=[big, st, st, pl.BlockSpec((SSD_HEADS, CHUNK), lambda c: (0, cidx(c))),
                   pl.BlockSpec((SSD_HEADS, 128), lambda c: (0, 0))],
        out_shape=[jax.ShapeDtypeStruct((D_INNER, t), f32), jax.ShapeDtypeStruct((512, t), f32),
                   jax.ShapeDtypeStruct((512, t), f32), jax.ShapeDtypeStruct((SSD_HEADS, t), f32),
                   jax.ShapeDtypeStruct((SSD_HEADS, 128), f32)],
        scratch_shapes=[pltpu.VMEM((SSD_HEADS, SSD_HEAD_DIM, D_STATE), f32)],
        compiler_params=_cparams(("arbitrary",)))(*args)


def tail_fwd_t(y_scan, xbc_ct, z_t, skip_b, nw_b, tb=512):
    t = y_scan.shape[1]
    reps = tb // 128

    def body(ys_ref, x_ref, z_ref, d_ref, w_ref, o_ref):
        zz = z_ref[...]
        y = (ys_ref[...] + _lanes(d_ref[...], reps) * x_ref[...]) * (zz * _sigmoid(zz))
        rstd = lax.rsqrt(jnp.mean(y * y, axis=0, keepdims=True) + NORM_EPS)
        o_ref[...] = (y * rstd * _lanes(w_ref[...], reps)).astype(o_ref.dtype)

    blk = pl.BlockSpec((512, tb), lambda g, i: (g, i))
    par = pl.BlockSpec((512, 128), lambda g, i: (g, 0))
    return pl.pallas_call(
        body, name="tail_fwd", grid=(SSD_GROUPS, t // tb), in_specs=[blk, blk, blk, par, par], out_specs=blk,
        out_shape=jax.ShapeDtypeStruct((D_INNER, t), bf16),
        compiler_params=_cparams(("parallel", "parallel")))(y_scan, xbc_ct, z_t, skip_b, nw_b)


def tail_bwd_t(dyn_t, y_scan, xbc_ct, z_t, skip_b, nw_b, tb=512):
    t = y_scan.shape[1]
    reps = tb // 128

    def body(g_ref, ys_ref, x_ref, z_ref, d_ref, w_ref, dy_ref, dz_ref, dw_ref, dd_ref):
        zz = z_ref[...]
        sg = _sigmoid(zz)
        sl = zz * sg
        x = x_ref[...]
        y = ys_ref[...] + _lanes(d_ref[...], reps) * x
        yz = y * sl
        rstd = lax.rsqrt(jnp.mean(yz * yz, axis=0, keepdims=True) + NORM_EPS)
        yhat = yz * rstd
        g = g_ref[...]
        dyhat = g * _lanes(w_ref[...], reps)
        dyz = rstd * (dyhat - yhat * jnp.mean(dyhat * yhat, axis=0, keepdims=True))
        dy = dyz * sl
        dy_ref[...] = dy
        dz_ref[...] = (dyz * y * sg * (1.0 + zz * (1.0 - sg))).astype(dz_ref.dtype)

        def fold(v):
            s = v[:, 0:128]
            for q in range(1, reps):
                s = s + v[:, 128 * q:128 * (q + 1)]
            return s

        @pl.when(pl.program_id(1) == 0)
        def _():
            dw_ref[...] = jnp.zeros_like(dw_ref)
            dd_ref[...] = jnp.zeros_like(dd_ref)

        dw_ref[...] += fold(g * yhat)
        dd_ref[...] += fold(dy * x)

    blk = pl.BlockSpec((512, tb), lambda g, i: (g, i))
    par = pl.BlockSpec((512, 128), lambda g, i: (g, 0))
    return pl.pallas_call(
        body, name="tail_bwd", grid=(SSD_GROUPS, t // tb), in_specs=[blk, blk, blk, blk, par, par],
        out_specs=[blk, blk, par, par],
        out_shape=[jax.ShapeDtypeStruct((D_INNER, t), f32), jax.ShapeDtypeStruct((D_INNER, t), bf16),
                   jax.ShapeDtypeStruct((D_INNER, 128), f32), jax.ShapeDtypeStruct((D_INNER, 128), f32)],
        compiler_params=_cparams(("parallel", "arbitrary")))(dyn_t, y_scan, xbc_ct, z_t, skip_b, nw_b)


def merge_fwd(u_gate, bg_row, y_ssd, y_att, tb=512):
    t = y_ssd.shape[0]

    def body(ga_ref, gb_ref, ba_ref, bb_ref, ys_ref, ya_ref, o_ref):
        o_ref[...] = (_sigmoid(ga_ref[...] + ba_ref[...]) * ys_ref[...]
                      + _sigmoid(gb_ref[...] + bb_ref[...]) * ya_ref[...]).astype(o_ref.dtype)

    blk = pl.BlockSpec((tb, 512), lambda i, j: (i, j))
    blk2 = pl.BlockSpec((tb, 512), lambda i, j: (i, 2 + j))
    row = pl.BlockSpec((1, 512), lambda i, j: (0, j))
    row2 = pl.BlockSpec((1, 512), lambda i, j: (0, 2 + j))
    return pl.pallas_call(
        body, name="merge_fwd", grid=(t // tb, 2), in_specs=[blk, blk2, row, row2, blk, blk], out_specs=blk,
        out_shape=jax.ShapeDtypeStruct((t, D_MODEL), bf16),
        compiler_params=_cparams(("parallel", "parallel")))(u_gate, u_gate, bg_row, bg_row, y_ssd, y_att)


def merge_bwd(dm, u_gate, bg_row, y_ssd, y_att, tb=512):
    t = dm.shape[0]

    def body(dm_ref, ga_ref, gb_ref, ba_ref, bb_ref, ys_ref, ya_ref, dys_ref, dya_ref, dga_ref, dgb_ref, dba_ref, dbb_ref):
        d = dm_ref[...]
        sa = _sigmoid(ga_ref[...] + ba_ref[...])
        sb = _sigmoid(gb_ref[...] + bb_ref[...])
        dys_ref[...] = (d * sa).astype(dys_ref.dtype)
        dya_ref[...] = (d * sb).astype(dya_ref.dtype)
        dla = d * ys_ref[...] * sa * (1.0 - sa)
        dlb = d * ya_ref[...] * sb * (1.0 - sb)
        dga_ref[...] = dla.astype(dga_ref.dtype)
        dgb_ref[...] = dlb.astype(dgb_ref.dtype)

        @pl.when(pl.program_id(1) == 0)
        def _():
            dba_ref[...] = jnp.zeros_like(dba_ref)
            dbb_ref[...] = jnp.zeros_like(dbb_ref)

        dba_ref[...] += jnp.sum(dla, axis=0, keepdims=True)
        dbb_ref[...] += jnp.sum(dlb, axis=0, keepdims=True)

    blk = pl.BlockSpec((tb, 512), lambda j, i: (i, j))
    blk2 = pl.BlockSpec((tb, 512), lambda j, i: (i, 2 + j))
    row = pl.BlockSpec((1, 512), lambda j, i: (0, j))
    row2 = pl.BlockSpec((1, 512), lambda j, i: (0, 2 + j))
    act = jax.ShapeDtypeStruct((t, D_MODEL), bf16)
    vec = jax.ShapeDtypeStruct((1, D_MODEL), f32)
    return pl.pallas_call(
        body, name="merge_bwd", grid=(2, t // tb), in_specs=[blk, blk, blk2, row, row2, blk, blk],
        out_specs=[blk, blk, blk, blk, row, row], out_shape=[act, act, act, act, vec, vec],
        compiler_params=_cparams(("parallel", "arbitrary")))(dm, u_gate, u_gate, bg_row, bg_row, y_ssd, y_att)


def _ln_stats(r):
    mu = jnp.mean(r, axis=1, keepdims=True)
    xc = r - mu
    rstd = lax.rsqrt(jnp.mean(xc * xc, axis=1, keepdims=True) + NORM_EPS)
    return xc * rstd, rstd


def _ln_bwd(dy, xhat, rstd, g_row):
    dxh = dy * g_row
    return rstd * (dxh - jnp.mean(dxh, axis=1, keepdims=True) - xhat * jnp.mean(dxh * xhat, axis=1, keepdims=True))


def ln1_fwd(x, mix, g_row, b_row, tb=512):
    t = x.shape[0]

    def body(x_ref, m_ref, g_ref, b_ref, o_ref, ob_ref):
        xhat, _ = _ln_stats(ALPHA * x_ref[...] + m_ref[...])
        h = xhat * g_ref[...] + b_ref[...]
        o_ref[...] = h
        ob_ref[...] = h.astype(ob_ref.dtype)

    blk = pl.BlockSpec((tb, D_MODEL), lambda i: (i, 0))
    row = pl.BlockSpec((1, D_MODEL), lambda i: (0, 0))
    return pl.pallas_call(body, name="ln1_fwd", grid=(t // tb,), in_specs=[blk, blk, row, row], out_specs=[blk, blk],
                          out_shape=[jax.ShapeDtypeStruct((t, D_MODEL), f32), jax.ShapeDtypeStruct((t, D_MODEL), bf16)],
                          compiler_params=_cparams(("parallel",)))(x, mix, g_row, b_row)


def ln1_bwd(dh, x, mix, g_row, tb=512):
    t = x.shape[0]

    def body(dh_ref, x_ref, m_ref, g_ref, dr_ref, drb_ref, dg_ref, db_ref):
        xhat, rstd = _ln_stats(ALPHA * x_ref[...] + m_ref[...])
        dy = dh_ref[...]
        dr = _ln_bwd(dy, xhat, rstd, g_ref[...])
        dr_ref[...] = dr
        drb_ref[...] = dr.astype(drb_ref.dtype)

        @pl.when(pl.program_id(0) == 0)
        def _():
            dg_ref[...] = jnp.zeros_like(dg_ref)
            db_ref[...] = jnp.zeros_like(db_ref)

        dg_ref[...] += jnp.sum(dy * xhat, axis=0, keepdims=True)
        db_ref[...] += jnp.sum(dy, axis=0, keepdims=True)

    blk = pl.BlockSpec((tb, D_MODEL), lambda i: (i, 0))
    row = pl.BlockSpec((1, D_MODEL), lambda i: (0, 0))
    return pl.pallas_call(
        body, name="ln1_bwd", grid=(t // tb,), in_specs=[blk, blk, blk, row], out_specs=[blk, blk, row, row],
        out_shape=[jax.ShapeDtypeStruct((t, D_MODEL), f32), jax.ShapeDtypeStruct((t, D_MODEL), bf16),
                   jax.ShapeDtypeStruct((1, D_MODEL), f32), jax.ShapeDtypeStruct((1, D_MODEL), f32)],
        compiler_params=_cparams(("arbitrary",)))(dh, x, mix, g_row)


def ln2_loss(h1, f, g_row, b_row, target, tb=512):
    t = h1.shape[0]

    def body(h_ref, f_ref, g_ref, b_ref, t_ref, dr_ref, drb_ref, dg_ref, db_ref, loss_ref):
        xhat, rstd = _ln_stats(ALPHA * h_ref[...] + f_ref[...])
        g = g_ref[...]
        err = xhat * g + b_ref[...] - t_ref[...]
        dy = err * (1.0 / D_MODEL)
        dr = _ln_bwd(dy, xhat, rstd, g)
        dr_ref[...] = dr
        drb_ref[...] = dr.astype(drb_ref.dtype)

        @pl.when(pl.program_id(0) == 0)
        def _():
            dg_ref[...] = jnp.zeros_like(dg_ref)
            db_ref[...] = jnp.zeros_like(db_ref)
            loss_ref[...] = jnp.zeros_like(loss_ref)

        dg_ref[...] += jnp.sum(dy * xhat, axis=0, keepdims=True)
        db_ref[...] += jnp.sum(dy, axis=0, keepdims=True)
        part = jnp.sum(jnp.mean(err * err, axis=1, keepdims=True), axis=0, keepdims=True)
        loss_ref[...] += 0.5 * part

    blk = pl.BlockSpec((tb, D_MODEL), lambda i: (i, 0))
    row = pl.BlockSpec((1, D_MODEL), lambda i: (0, 0))
    return pl.pallas_call(
        body, name="ln2_loss", grid=(t // tb,), in_specs=[blk, blk, row, row, blk],
        out_specs=[blk, blk, row, row, pl.BlockSpec((8, 128), lambda i: (0, 0))],
        out_shape=[jax.ShapeDtypeStruct((t, D_MODEL), f32), jax.ShapeDtypeStruct((t, D_MODEL), bf16),
                   jax.ShapeDtypeStruct((1, D_MODEL), f32), jax.ShapeDtypeStruct((1, D_MODEL), f32),
                   jax.ShapeDtypeStruct((8, 128), f32)],
        compiler_params=_cparams(("arbitrary",)))(h1, f, g_row, b_row, target)


TAIL_BLOCK, TAIL_AT = divmod(OFF_TAIL, PACK_TILE)


def _sum4(ref):
    return ((ref[0].astype(f32) + ref[1].astype(f32)) + ref[2].astype(f32)) + ref[3].astype(f32)


def _adamw_update(g, w_ref, m_ref, v_ref, g_ref, d_ref, nm_ref, nv_ref):
    c1 = 1.0 - ADAM_B1 ** ADAM_STEP
    c2 = 1.0 - ADAM_B2 ** ADAM_STEP
    nm = ADAM_B1 * m_ref[...] + (1.0 - ADAM_B1) * g
    nv = ADAM_B2 * v_ref[...] + (1.0 - ADAM_B2) * (g * g)
    g_ref[...] = g
    nm_ref[...] = nm
    nv_ref[...] = nv
    d_ref[...] = -ADAM_LR * ((nm / c1) / (jnp.sqrt(nv / c2) + ADAM_EPS) + ADAM_WD * w_ref[...])


def adamw_early(landed, parts, me, w, m, v):
    off = LATE_ROWS // EARLY_TILE

    def body(me_ref, *refs):
        src = refs[0:N_DEV]
        own_ref, w_ref, m_ref, v_ref = refs[N_DEV:N_DEV + 4]
        mine = me_ref[0]
        g = None
        for s in range(N_DEV):
            term = jnp.where(mine == s, own_ref[0], src[s][0])
            g = term if g is None else g + term
        _adamw_update(g, w_ref, m_ref, v_ref, *refs[N_DEV + 4:])

    def slot(s):
        return pl.BlockSpec((1, EARLY_TILE, 1024), lambda i, me_ref: (jnp.where(me_ref[0] == s, (s + 1) % N_DEV, s), i, 0))

    shard = pl.BlockSpec((EARLY_TILE, 1024), lambda i, me_ref: (i + off, 0))
    out_blk = pl.BlockSpec((EARLY_TILE, 1024), lambda i, me_ref: (i, 0))
    grid_spec = pltpu.PrefetchScalarGridSpec(
        num_scalar_prefetch=1, grid=(EARLY_ROWS // EARLY_TILE,),
        in_specs=[slot(s) for s in range(N_DEV)]
        + [pl.BlockSpec((1, EARLY_TILE, 1024), lambda i, me_ref: (me_ref[0], i, 0)), shard, shard, shard],
        out_specs=[out_blk] * 4)
    out = jax.ShapeDtypeStruct((EARLY_ROWS, 1024), f32)
    return pl.pallas_call(body, name="adamw_early", grid_spec=grid_spec, out_shape=[out] * 4,
                          compiler_params=_cparams(("parallel",)))(me, *([landed] * N_DEV), parts, w, m, v)


def adamw(parts, tails, w, m, v):
    rows = parts.shape[1]

    def body(p_ref, t_ref, w_ref, m_ref, v_ref, g_ref, d_ref, nm_ref, nv_ref):
        g = _sum4(p_ref)
        with_tail = jnp.concatenate([g[0:TAIL_AT], _sum4(t_ref), g[TAIL_AT + ROWS_TAIL:]], axis=0)
        g = jnp.where(pl.program_id(0) == TAIL_BLOCK, with_tail, g)
        _adamw_update(g, w_ref, m_ref, v_ref, g_ref, d_ref, nm_ref, nv_ref)

    blk = pl.BlockSpec((PACK_TILE, 1024), lambda i: (i, 0))
    out = jax.ShapeDtypeStruct((rows, 1024), f32)
    return pl.pallas_call(
        body, name="adamw", grid=(rows // PACK_TILE,),
        in_specs=[pl.BlockSpec((4, PACK_TILE, 1024), lambda i: (0, i, 0)),
                  pl.BlockSpec((4, ROWS_TAIL, 1024), lambda i: (0, 0, 0)), blk, blk, blk], out_specs=[blk] * 4,
        out_shape=[out] * 4, compiler_params=_cparams(("parallel",)))(parts, tails, w, m, v)


def pair_sum(parts, recv, core):
    rows = parts.shape[1]

    def body(c_ref, a_ref, b_ref, o_ref, t_ref):
        s = a_ref[...] + b_ref[...]
        o_ref[...] = s.astype(o_ref.dtype)

        @pl.when(pl.program_id(1) == TAIL_BLOCK)
        def _():
            t_ref[...] = s[:, TAIL_AT:TAIL_AT + ROWS_TAIL]

    grid_spec = pltpu.PrefetchScalarGridSpec(
        num_scalar_prefetch=1, grid=(4, rows // PACK_TILE),
        in_specs=[pl.BlockSpec((1, PACK_TILE, 1024), lambda j, i, c_ref: (2 * j + c_ref[0], i, 0)),
                  pl.BlockSpec((1, PACK_TILE, 1024), lambda j, i, c_ref: (j, i, 0))],
        out_specs=[pl.BlockSpec((1, PACK_TILE, 1024), lambda j, i, c_ref: (j, i, 0)),
                   pl.BlockSpec((1, ROWS_TAIL, 1024), lambda j, i, c_ref: (j, 0, 0))])
    return pl.pallas_call(
        body, name="pair_sum", grid_spec=grid_spec,
        out_shape=[jax.ShapeDtypeStruct(recv.shape, bf16), jax.ShapeDtypeStruct((4, ROWS_TAIL, 1024), f32)],
        compiler_params=_cparams(("parallel", "arbitrary")))(core, parts, recv)


def _place():
    return lax.axis_index("x"), lax.axis_index("y"), lax.axis_index("c")


def all_gather_blocks(shard):
    rows, cols = shard.shape

    def body(x_ref, out_ref, send_sems, recv_sems, local_sem):
        x, y, c = _place()
        me, sibling = (x, y, c), (x, y, 1 - c)
        chips = [(1 - x, y), (x, 1 - y), (1 - x, 1 - y)]

        def slot(px, py, pc):
            return out_ref.at[4 * px + 2 * py + pc]

        def copy(k, block, to, src=None):
            return pltpu.make_async_remote_copy(
                src_ref=slot(*block) if src is None else src, dst_ref=slot(*block), send_sem=send_sems.at[k],
                recv_sem=recv_sems.at[k], device_id=to, device_id_type=MESH)

        mine = pltpu.make_async_copy(x_ref, slot(*me), local_sem)
        mine.start()
        first = [copy(0, me, sibling, src=x_ref)]
        first += [copy(1 + j, me, (*chip, c), src=x_ref) for j, chip in enumerate(chips)]
        for cp in first:
            cp.start()
        passed = [copy(4 + j, (*chip, c), sibling) for j, chip in enumerate(chips)]
        for j, chip in enumerate(chips):
            copy(1 + j, (*chip, c), me).wait_recv()
            passed[j].start()
        copy(0, sibling, me).wait_recv()
        for j, chip in enumerate(chips):
            copy(4 + j, (*chip, 1 - c), me).wait_recv()
        for cp in first + passed:
            cp.wait_send()
        mine.wait()

    return pl.pallas_call(
        body, name="all_gather_blocks", out_shape=jax.ShapeDtypeStruct((N_DEV, rows, cols), shard.dtype),
        in_specs=[pl.BlockSpec(memory_space=pl.ANY)], out_specs=pl.BlockSpec(memory_space=pl.ANY),
        scratch_shapes=[pltpu.SemaphoreType.DMA((7,)), pltpu.SemaphoreType.DMA((7,)), pltpu.SemaphoreType.DMA],
        compiler_params=pltpu.CompilerParams(has_side_effects=True))(shard)


def pair_exchange(parts):
    _, rows, cols = parts.shape

    def body(p_ref, recv_ref, send_sems, recv_sems):
        x, y, c = _place()
        copies = [pltpu.make_async_remote_copy(
            src_ref=p_ref.at[2 * j + 1 - c], dst_ref=recv_ref.at[j], send_sem=send_sems.at[j], recv_sem=recv_sems.at[j],
            device_id=(x, y, 1 - c), device_id_type=MESH) for j in range(4)]
        for cp in copies:
            cp.start()
        for cp in copies:
            cp.wait_recv()
        for cp in copies:
            cp.wait_send()

    return pl.pallas_call(
        body, name="pair_exchange", out_shape=jax.ShapeDtypeStruct((4, rows, cols), parts.dtype),
        in_specs=[pl.BlockSpec(memory_space=pl.ANY)], out_specs=pl.BlockSpec(memory_space=pl.ANY),
        scratch_shapes=[pltpu.SemaphoreType.DMA((4,)), pltpu.SemaphoreType.DMA((4,))],
        compiler_params=pltpu.CompilerParams(has_side_effects=True))(parts)


def chip_exchange(parts):
    n = len(parts)

    def body(*refs):
        p_refs, out_refs = refs[0:n], refs[n:2 * n]
        send_sems, recv_sems, local_sems = refs[2 * n:]
        x, y, c = _place()
        mine = 2 * x + y
        flips = [(x, 1 - y), (1 - x, y), (1 - x, 1 - y)]

        def copy(a, k, src_slot, dst_slot):
            px, py = flips[k]
            return pltpu.make_async_remote_copy(
                src_ref=p_refs[a].at[src_slot], dst_ref=out_refs[a].at[dst_slot], send_sem=send_sems.at[3 * a + k],
                recv_sem=recv_sems.at[3 * a + k], device_id=(px, py, c), device_id_type=MESH)

        local = [pltpu.make_async_copy(p_refs[a].at[mine], out_refs[a].at[mine], local_sems.at[a]) for a in range(n)]
        sends = [copy(a, k, 2 * flips[k][0] + flips[k][1], mine) for a in range(n) for k in range(3)]
        for cp in local + sends:
            cp.start()
        for a in range(n):
            for k in range(3):
                copy(a, k, mine, 2 * flips[k][0] + flips[k][1]).wait_recv()
        for cp in sends:
            cp.wait_send()
        for cp in local:
            cp.wait()

    return pl.pallas_call(
        body, name="chip_exchange", out_shape=[jax.ShapeDtypeStruct(p.shape, p.dtype) for p in parts],
        in_specs=[pl.BlockSpec(memory_space=pl.ANY)] * n, out_specs=[pl.BlockSpec(memory_space=pl.ANY)] * n,
        scratch_shapes=[pltpu.SemaphoreType.DMA((3 * n,)), pltpu.SemaphoreType.DMA((3 * n,)), pltpu.SemaphoreType.DMA((n,))],
        compiler_params=pltpu.CompilerParams(has_side_effects=True))(*parts)


_HBM = pl.BlockSpec(memory_space=pltpu.HBM)
_SEM = pl.BlockSpec(memory_space=pltpu.SEMAPHORE)


def _peer(k):
    x, y, c = _place()
    px, py, pc = (1 - x if k & 4 else x), (1 - y if k & 2 else y), (1 - c if k & 1 else c)
    return (px, py, pc), 4 * px + 2 * py + pc


def scatter_start(parts):
    def body(p_ref, land_ref, send_sems, recv_sems, p_thru, land_thru, token):
        x, y, c = _place()
        me = 4 * x + 2 * y + c
        for k in range(1, N_DEV):
            place, idx = _peer(k)
            pltpu.make_async_remote_copy(src_ref=p_ref.at[idx], dst_ref=land_ref.at[me], send_sem=send_sems.at[k - 1],
                                         recv_sem=recv_sems.at[k - 1], device_id=place, device_id_type=MESH).start()
        token[...] = jnp.zeros_like(token)

    landing = lax.empty(parts.shape, parts.dtype)
    return pl.pallas_call(
        body, name="scatter_start",
        out_shape=(pltpu.SemaphoreType.DMA((N_DEV - 1,)), pltpu.SemaphoreType.DMA((N_DEV - 1,)),
                   pltpu.HBM(parts.shape, parts.dtype), pltpu.HBM(parts.shape, parts.dtype),
                   jax.ShapeDtypeStruct((8, 128), f32)),
        in_specs=(_HBM, _HBM), out_specs=(_SEM, _SEM, _HBM, _HBM, pl.BlockSpec(memory_space=pltpu.VMEM)),
        input_output_aliases={0: 2, 1: 3},
        compiler_params=pltpu.CompilerParams(has_side_effects=pltpu.SideEffectType.DATAFLOW_SIDE_EFFECTING),
    )(pltpu.with_memory_space_constraint(parts, pltpu.HBM), pltpu.with_memory_space_constraint(landing, pltpu.HBM))


def scatter_wait(send_sems, recv_sems, parts_thru, land_thru, after):
    def body(p_ref, land_ref, send_sems, recv_sems, after_ref, p_out, land_out):
        for k in range(1, N_DEV):
            place, idx = _peer(k)
            copy = pltpu.make_async_remote_copy(src_ref=p_ref.at[idx], dst_ref=land_ref.at[idx], send_sem=send_sems.at[k - 1],
                                                recv_sem=recv_sems.at[k - 1], device_id=place, device_id_type=MESH)
            copy.wait_send()
            copy.wait_recv()

    return pl.pallas_call(
        body, name="scatter_wait",
        out_shape=(pltpu.HBM(parts_thru.shape, parts_thru.dtype), pltpu.HBM(land_thru.shape, land_thru.dtype)),
        in_specs=(_HBM, _HBM, _SEM, _SEM, pl.BlockSpec(memory_space=pl.ANY)), out_specs=(_HBM, _HBM),
        input_output_aliases={0: 0, 1: 1},
        compiler_params=pltpu.CompilerParams(has_side_effects=pltpu.SideEffectType.DATAFLOW_SIDE_EFFECTING),
    )(parts_thru, land_thru, send_sems, recv_sems, after)


def _tail_rows(conv_part, small, extra):
    lead = conv_part.shape[:-1]
    rep = jnp.concatenate([small[n].reshape(-1).astype(f32) for n in SMALL] + [extra.reshape(1).astype(f32)])
    flat = jnp.concatenate([conv_part, jnp.broadcast_to(rep, lead + rep.shape),
                            jnp.zeros(lead + (ROWS_TAIL * 1024 - TAIL_ELEMS,), f32)], axis=-1)
    return flat.reshape(lead + (ROWS_TAIL, 1024))


def _late_rows(w_in_t, tail):
    lead = tail.shape[:-2]
    zeros = lambda r: jnp.zeros(lead + (r, 1024), f32)
    return jnp.concatenate([w_in_t, zeros(OFF_TAIL - IN_SHARD), tail, zeros(LATE_ROWS - OFF_TAIL - ROWS_TAIL)], axis=-2)


def _early_rows(w_ps, w_out, w_up_t, w_down, w_pa_t):
    return jnp.concatenate([w_ps, w_out, w_up_t, w_down, w_pa_t.reshape(w_pa_t.shape[:-2] + (ROWS_PA, 1024))], axis=-2)


def _pack_shard(vals):
    tail = _tail_rows(vals["conv_w"].reshape(-1), vals, jnp.zeros((), f32))
    return jnp.concatenate([_late_rows(vals["w_in"].T, tail),
                            _early_rows(vals["w_proj_ssd"], vals["w_out"], vals["w_up"].T, vals["w_down"],
                                        vals["w_proj_attn"].T)], axis=0)


def _unpack_shard(late, early):
    e = lambda lo, hi: early[lo - LATE_ROWS:hi - LATE_ROWS]
    out = {"w_in": late[0:IN_SHARD].T, "w_proj_ssd": e(OFF_PS, OFF_OUT), "w_out": e(OFF_OUT, OFF_UP),
           "w_up": e(OFF_UP, OFF_DOWN).T, "w_down": e(OFF_DOWN, OFF_PA),
           "w_proj_attn": e(OFF_PA, PACK_ROWS).reshape(D_MODEL // N_DEV, ATTN_OUT).T}
    flat = late[OFF_TAIL:OFF_TAIL + ROWS_TAIL].reshape(-1)
    out["conv_w"] = flat[0:CONV_SHARD].reshape(D_CONV, CONV_DIM // N_DEV)
    off = CONV_SHARD
    for n in SMALL:
        out[n] = flat[off:off + SMALL_SIZES[n]]
        off += SMALL_SIZES[n]
    out["_extra"] = flat[off]
    return out


def _blocks(g):
    return g.reshape(N_DEV, g.shape[0] // N_DEV, g.shape[1])


def _pack_early_parts(full):
    return _early_rows(_blocks(full["w_proj_ssd"]), _blocks(full["w_out"]), _blocks(full["w_up_t"]),
                       _blocks(full["w_down"]), _blocks(full["w_proj_attn_t"]))


def _pack_late_parts(full, small, extra):
    conv = full["conv_w"].reshape(D_CONV, N_DEV, CONV_DIM // N_DEV).transpose(1, 0, 2).reshape(N_DEV, CONV_SHARD)
    return _late_rows(_blocks(full["w_in_t"]), _tail_rows(conv, small, extra))


def _gather_weights(w):
    conv_bits = lax.bitcast_convert_type(w["conv_w"], bf16).reshape(-1)
    conv_rows = jnp.concatenate([conv_bits, jnp.zeros((16 * 1024 - 2 * CONV_SHARD,), bf16)]).reshape(16, 1024)
    packed = _pack_shard(w)
    got = all_gather_blocks(jnp.concatenate([packed[0:OFF_TAIL].astype(bf16), conv_rows,
                                             packed[OFF_TAIL + ROWS_TAIL:].astype(bf16)], axis=0))
    whole = lambda lo, hi: got[:, lo:hi].reshape(N_DEV * (hi - lo), 1024)
    conv = lax.bitcast_convert_type(got[:, OFF_TAIL:OFF_TAIL + 4].reshape(N_DEV, 4096)[:, 0:2 * CONV_SHARD]
                                    .reshape(N_DEV, D_CONV, CONV_DIM // N_DEV, 2), f32)
    return {"w_in_t": whole(0, IN_SHARD), "w_proj_ssd": whole(OFF_PS, OFF_OUT), "w_out": whole(OFF_OUT, OFF_UP),
            "w_up_t": whole(OFF_UP, OFF_DOWN), "w_down": whole(OFF_DOWN, OFF_PA),
            "w_proj_attn_t": got[:, OFF_PA:PACK_ROWS].reshape(D_MODEL, ATTN_OUT),
            "conv_w": conv.transpose(1, 0, 2).reshape(D_CONV, CONV_DIM)}


def _row(v, width=None):
    v = v.reshape(1, -1).astype(f32)
    return v if width is None else jnp.pad(v, ((0, 0), (0, width - v.shape[1])))


def _lanes256(vf, vb):
    z = jnp.zeros((96,), f32)
    return jnp.concatenate([vf.astype(f32), z, vb.astype(f32), z]).reshape(1, 256)


def _local_step(x2, tgt, wf, p, send_early=None):
    t = x2.shape[0]
    o = np.cumsum((0,) + IN_SPLITS)
    wt = wf["w_in_t"]
    wt_z, wt_xbc, wt_dt = wt[o[0]:o[1]], wt[o[1]:o[2]], wt[o[2]:o[4]]
    wt_qkv, wt_gate = wt[o[4]:o[7]], wt[o[7]:o[8]]

    spread = lambda v: jnp.broadcast_to(v.astype(f32)[..., None], v.shape + (128,))
    conv_w_b, conv_b_b = spread(wf["conv_w"]), spread(p["conv_b"])
    dt_bias_b = spread(jnp.concatenate([p["dt_bias_f"], p["dt_bias_b"]]))
    a_f, a_b = -jnp.exp(p["a_log_f"].astype(f32)), -jnp.exp(p["a_log_b"].astype(f32))
    a_coef_b = spread(jnp.concatenate([a_f, a_b]))
    skip_b = spread(jnp.repeat(p["d_skip"], SSD_HEAD_DIM))
    nw_b, bg_row = spread(p["ssd_norm_w"]), _row(p["b_gate"])
    g1, b1, g2, b2 = _row(p["ln1_g"]), _row(p["ln1_b"]), _row(p["ln2_g"]), _row(p["ln2_b"])

    xb = x2.astype(MXU_DTYPE)
    xt = xb.T
    u_z = mm_nn(wt_z, xt, "in_z")
    u_xbc = mm_nn(wt_xbc, xt, "in_xbc")
    u_dt = mm_nn(wt_dt, xt, "in_dt")
    u_qkv = mm_nt(xb, wt_qkv, "in_qkv")
    u_gate = mm_nt(xb, wt_gate, "in_gate")
    xbc_c = conv_fwd_t(u_xbc, conv_w_b, conv_b_b)
    dt_t = dt_fwd_t(u_dt, dt_bias_b)
    y_f, h_f = ssd_fwd_t(xbc_c, dt_t, a_coef_b, False, "ssd_fwd_f")
    y_scan, h_b = ssd_fwd_t(xbc_c, dt_t, a_coef_b, True, "ssd_fwd_b", prev=y_f)
    yn = tail_fwd_t(y_scan, xbc_c, u_z, skip_b, nw_b)
    y_ssd = mm_tn(yn, wf["w_proj_ssd"], "proj_ssd")

    def strided(a, dil):
        return a.reshape(t // dil, dil * 256)

    qkv, outs, lses = [], [], []
    for pi, (_, dil) in enumerate(DIL_PATTERNS):
        q, k, v = (strided(u_qkv[:, ATTN_WIDTH * s + 256 * pi: ATTN_WIDTH * s + 256 * (pi + 1)], dil) for s in range(3))
        qkv.append((q, k, v))
        op, lp = attn_fwd(q, k, v, pi, dil, f"attn_fwd_{pi}")
        outs.append(op.reshape(t, 256))
        lses.append(lp.reshape(t, 256))
    ya, lse = attn_combine(outs, lses)
    y_att = mm_nt(ya, wf["w_proj_attn_t"], "proj_attn")
    m = merge_fwd(u_gate, bg_row, y_ssd, y_att)
    mix = mm_nn(m, wf["w_out"], "out_proj")
    h1, h1b = ln1_fwd(x2, mix, g1, b1)
    a_up, p_act = mm_nt(h1b, wf["w_up_t"], "mlp_up", relu2=True)
    f_dn = mm_nn(p_act, wf["w_down"], "mlp_down")
    dr2, dr2b, dg2, db2, loss8 = ln2_loss(h1, f_dn, g2, b2, tgt)

    full, small = {}, {}
    da = mm_nt(dr2b, wf["w_down"], "d_mlp_act", out_dtype=bf16, relu2_of=a_up)
    full["w_down"] = mm_tn(p_act, dr2b, "dw_down")
    full["w_up_t"] = mm_tn(da, h1b, "dw_up")
    dh1 = mm_nn(da, wf["w_up_t"], "d_h1", acc_in=dr2, acc_scale=ALPHA)
    dr1, dr1b, dg1, db1 = ln1_bwd(dh1, x2, mix, g1)
    dm = mm_nt(dr1b, wf["w_out"], "d_merge")
    full["w_out"] = mm_tn(m, dr1b, "dw_out")
    dys, dya_p, dga, dgb, dba, dbb = merge_bwd(dm, u_gate, bg_row, y_ssd, y_att)
    dyn = mm_nt(wf["w_proj_ssd"], dys, "d_yn")
    full["w_proj_ssd"] = mm_nn(yn, dys, "dw_proj_ssd")
    dya = mm_nn(dya_p, wf["w_proj_attn_t"], "d_ya")
    full["w_proj_attn_t"] = mm_tn(dya_p, ya, "dw_proj_attn")
    if send_early is not None:
        skip_b = skip_b + send_early(full)[0, 0]

    dy, dz, dnw, ddx = tail_bwd_t(dyn, y_scan, xbc_c, u_z, skip_b, nw_b)
    dxf, dbf, dcf, ddtf, daf = ssd_bwd_t(xbc_c, dt_t, a_coef_b, dy, h_f, False, "ssd_bwd_f", skip_b=skip_b)
    dxs, dbs, dcs, ddtb, dab = ssd_bwd_t(xbc_c, dt_t, a_coef_b, dy, h_b, True, "ssd_bwd_b", prev=(dxf, dbf, dcf))
    du_xbc, dcw_x, dcb_x = conv_bwd_t(u_xbc, dxs, conv_w_b, conv_b_b, None, "conv_bwd_x", 0)
    du_xbc, dcw_b, dcb_b = conv_bwd_t(u_xbc, dbs, conv_w_b, conv_b_b, du_xbc, "conv_bwd_b", D_INNER)
    du_xbc, dcw_c, dcb_c = conv_bwd_t(u_xbc, dcs, conv_w_b, conv_b_b, du_xbc, "conv_bwd_c", D_INNER + 512)
    du_dt, dbias = dt_bwd_t(ddtf, ddtb, u_dt, dt_bias_b)

    delta = attn_delta(dya, ya)
    dqs, dks, dvs = [], [], []
    for pi, (_, dil) in enumerate(DIL_PATTERNS):
        q, k, v = qkv[pi]
        sd, sl_, sdel = strided(dya, dil), strided(lse, dil), strided(delta, dil)
        dqs.append(attn_dq(q, k, v, sd, sl_, sdel, pi, dil, f"attn_dq_{pi}").reshape(t, 256))
        dk, dv = attn_dkv(q, k, v, sd, sl_, sdel, pi, dil, f"attn_dkv_{pi}")
        dks.append(dk.reshape(t, 256))
        dvs.append(dv.reshape(t, 256))
    du_qkv = jnp.concatenate(dqs + dks + dvs, axis=1)
    du_gate = jnp.concatenate([dga, dgb], axis=1)

    dx = mm_tn(dz, wt_z, "dx_z", acc_in=dr1, acc_scale=ALPHA)
    dx = mm_tn(du_xbc, wt_xbc, "dx_xbc", acc_in=dx)
    dx = mm_tn(du_dt, wt_dt, "dx_dt", acc_in=dx)
    dx = mm_nn(du_qkv, wt_qkv, "dx_qkv", acc_in=dx)
    dx = mm_nn(du_gate, wt_gate, "dx_gate", acc_in=dx)
    full["w_in_t"] = jnp.concatenate(
        [mm_nn(dz, xb, "dw_in_z"), mm_nn(du_xbc, xb, "dw_in_xbc"), mm_nn(du_dt, xb, "dw_in_dt"),
         mm_tn(du_qkv, xb, "dw_in_qkv"), mm_tn(du_gate, xb, "dw_in_gate")], axis=0)
    lanes = lambda v: jnp.sum(v, axis=-1)
    full["conv_w"] = jnp.concatenate([lanes(dcw_x), lanes(dcw_b), lanes(dcw_c)], axis=1)

    small["b_gate"] = jnp.concatenate([dba, dbb], axis=1)
    small["conv_b"] = jnp.concatenate([lanes(dcb_x), lanes(dcb_b), lanes(dcb_c)])
    dbias = lanes(dbias)
    small["dt_bias_f"], small["dt_bias_b"] = dbias[0:32], dbias[32:64]
    small["a_log_f"] = lanes(daf) * a_f
    small["a_log_b"] = lanes(dab) * a_b
    small["d_skip"] = jnp.sum(lanes(ddx).reshape(SSD_HEADS, SSD_HEAD_DIM), axis=1)
    small["ssd_norm_w"] = lanes(dnw)
    small["ln1_g"], small["ln1_b"], small["ln2_g"], small["ln2_b"] = dg1, db1, dg2, db2
    return loss8[0, 0], dx, full, small


def kernel(x, w_in, b_gate, conv_w, conv_b, dt_bias_f, dt_bias_b, a_log_f, a_log_b, d_skip, ssd_norm_w, w_proj_ssd, w_proj_attn, w_out, ln1_g, ln1_b, w_up, w_down, ln2_g, ln2_b, loss_target, m_w_in, m_b_gate, m_conv_w, m_conv_b, m_dt_bias_f, m_dt_bias_b, m_a_log_f, m_a_log_b, m_d_skip, m_ssd_norm_w, m_w_proj_ssd, m_w_proj_attn, m_w_out, m_ln1_g, m_ln1_b, m_w_up, m_w_down, m_ln2_g, m_ln2_b, v_w_in, v_b_gate, v_conv_w, v_conv_b, v_dt_bias_f, v_dt_bias_b, v_a_log_f, v_a_log_b, v_d_skip, v_ssd_norm_w, v_w_proj_ssd, v_w_proj_attn, v_w_out, v_ln1_g, v_ln1_b, v_w_up, v_w_down, v_ln2_g, v_ln2_b):
    given = dict(locals())
    w = {n: given[n] for n in WEIGHTS}
    mom = {n: given["m_" + n] for n in WEIGHTS}
    var = {n: given["v_" + n] for n in WEIGHTS}
    t = x.shape[1]
    wf = _gather_weights(w)
    in_flight = []

    def send_early(full):
        send_sems, recv_sems, parts_thru, land_thru, token = scatter_start(_pack_early_parts(full))
        in_flight.append((send_sems, recv_sems, parts_thru, land_thru))
        return token

    loss, dx, full, small = _local_step(x.reshape(t, D_MODEL), loss_target.reshape(t, D_MODEL), wf, w, send_early)
    late = _pack_late_parts(full, small, loss)
    x_, y_, c_ = _place()
    core = c_.astype(jnp.int32).reshape(1)
    me = (4 * x_ + 2 * y_ + c_).astype(jnp.int32).reshape(1)
    wp, mp, vp = _pack_shard(w), _pack_shard(mom), _pack_shard(var)
    early_parts, landed = scatter_wait(*in_flight[0], late)
    early_out = adamw_early(landed, early_parts, me, wp, mp, vp)
    parts, tails = chip_exchange(pair_sum(late, pair_exchange(late), core))
    late_out = adamw(parts, tails, wp, mp, vp)
    g, delta, new_m, new_v = (_unpack_shard(a, b) for a, b in zip(late_out, early_out))
    outs = [g["_extra"], dx.reshape(x.shape)]
    for d in (g, delta, new_m, new_v):
        outs += [d[n].reshape(w[n].shape) for n in WEIGHTS]
    return tuple(outs)
```

```python
import functools
import math

import jax
import jax.numpy as jnp
import numpy as np
from jax import lax
from jax.experimental import pallas as pl
from jax.experimental.pallas import tpu as pltpu

f32 = jnp.float32
bf16 = jnp.bfloat16
MXU_DTYPE = jnp.bfloat16

N_DEV = 8
D_MODEL = 1024
D_INNER = 2048
SSD_HEADS = 32
SSD_HEAD_DIM = 64
SSD_GROUPS = 4
D_STATE = 128
D_CONV = 5
CHUNK = 128
CONV_DIM = D_INNER + 2 * SSD_GROUPS * D_STATE
NORM_EPS = 1e-5
ATTN_HEAD_DIM = 64
DIL_PATTERNS = ((128, 1), (512, 4), (2048, 16))
HEADS_PER_PATTERN = 4
ATTN_HEADS = 12
ATTN_WIDTH = 768
ATTN_OUT = 256
D_FF = 4096
ALPHA = 2.0 ** 0.25
IN_SPLITS = (D_INNER, CONV_DIM, SSD_HEADS, SSD_HEADS, ATTN_WIDTH, ATTN_WIDTH, ATTN_WIDTH, 2 * D_MODEL)
IN_COLS = sum(IN_SPLITS)
ADAM_LR, ADAM_B1, ADAM_B2, ADAM_EPS, ADAM_WD, ADAM_STEP = 0.001, 0.9, 0.999, 1e-08, 0.01, 10
NEG_BIG = -1e30
VMEM_LIMIT = 56 * 1024 * 1024
MESH = pl.DeviceIdType.MESH

SMALL = ("b_gate", "conv_b", "dt_bias_f", "dt_bias_b", "a_log_f", "a_log_b", "d_skip", "ssd_norm_w",
         "ln1_g", "ln1_b", "ln2_g", "ln2_b")
WEIGHTS = ("w_in", "b_gate", "conv_w", "conv_b", "dt_bias_f", "dt_bias_b", "a_log_f", "a_log_b", "d_skip",
           "ssd_norm_w", "w_proj_ssd", "w_proj_attn", "w_out", "ln1_g", "ln1_b", "w_up", "w_down", "ln2_g", "ln2_b")
SMALL_SIZES = {"b_gate": 2 * D_MODEL, "conv_b": CONV_DIM, "dt_bias_f": 32, "dt_bias_b": 32, "a_log_f": 32, "a_log_b": 32,
               "d_skip": 32, "ssd_norm_w": D_INNER, "ln1_g": D_MODEL, "ln1_b": D_MODEL, "ln2_g": D_MODEL, "ln2_b": D_MODEL}
IN_SHARD = IN_COLS // N_DEV
OFF_TAIL = 1200
ROWS_TAIL = 16
PACK_TILE = 128
LATE_ROWS = 1280
ROWS_PS, ROWS_OUT, ROWS_UP, ROWS_DOWN, ROWS_PA = D_INNER // N_DEV, D_MODEL // N_DEV, D_FF // N_DEV, D_FF // N_DEV, 32
OFF_PS = LATE_ROWS
OFF_OUT = OFF_PS + ROWS_PS
OFF_UP = OFF_OUT + ROWS_OUT
OFF_DOWN = OFF_UP + ROWS_UP
OFF_PA = OFF_DOWN + ROWS_DOWN
PACK_ROWS = OFF_PA + ROWS_PA
EARLY_ROWS = PACK_ROWS - LATE_ROWS
EARLY_TILE = 160
CONV_SHARD = D_CONV * CONV_DIM // N_DEV
TAIL_ELEMS = CONV_SHARD + sum(SMALL_SIZES.values()) + 1


def _cparams(sem=None, **kw):
    return pltpu.CompilerParams(dimension_semantics=sem, vmem_limit_bytes=VMEM_LIMIT, **kw)


def _mx(v):
    return v.astype(MXU_DTYPE)


def _dot(a, b):
    return jnp.dot(_mx(a), _mx(b), preferred_element_type=f32)


def _dot_nt(a, b):
    return lax.dot_general(_mx(a), _mx(b), (((1,), (1,)), ((), ())), preferred_element_type=f32)


def _dot_tn(a, b):
    return lax.dot_general(_mx(a), _mx(b), (((0,), (0,)), ((), ())), preferred_element_type=f32)


def _dot_exact(a, b):
    return jnp.dot(a, b, precision=lax.Precision.HIGHEST, preferred_element_type=f32)


def _sigmoid(v):
    return 1.0 / (1.0 + jnp.exp(-v))


def _pick(n, prefs):
    for p in prefs:
        if n % p == 0:
            return p
    return n


MM_TILE = 1024


def mm_nn(a, b, name, out_dtype=f32, acc_in=None, acc_scale=1.0):
    m, k = a.shape
    n = b.shape[1]
    tm = _pick(m, (MM_TILE, 512, 256, 128, 64))
    tn = _pick(n, (MM_TILE, 512, 256, 128))
    tk = _pick(k, (2048, 1536, 1152, 1024, 768, 512, 256, 128))
    nk = k // tk

    def body(*refs):
        a_ref, b_ref = refs[0:2]
        c_ref = refs[2] if acc_in is not None else None
        o_ref = refs[3] if acc_in is not None else refs[2]

        def finish(r):
            if acc_in is not None:
                r = r + acc_scale * c_ref[...]
            o_ref[...] = r.astype(o_ref.dtype)

        if nk == 1:
            finish(_dot(a_ref[...], b_ref[...]))
            return
        acc_ref = refs[-1]
        kk = pl.program_id(2)

        @pl.when(kk == 0)
        def _():
            acc_ref[...] = jnp.zeros_like(acc_ref)

        acc_ref[...] += _dot(a_ref[...], b_ref[...])

        @pl.when(kk == nk - 1)
        def _():
            finish(acc_ref[...])

    in_specs = [pl.BlockSpec((tm, tk), lambda i, j, kk: (i, kk)), pl.BlockSpec((tk, tn), lambda i, j, kk: (kk, j))]
    args = [a, b]
    if acc_in is not None:
        in_specs.append(pl.BlockSpec((tm, tn), lambda i, j, kk: (i, j)))
        args.append(acc_in)
    return pl.pallas_call(
        body, name=name, grid=(m // tm, n // tn, nk), in_specs=in_specs,
        out_specs=pl.BlockSpec((tm, tn), lambda i, j, kk: (i, j)),
        out_shape=jax.ShapeDtypeStruct((m, n), out_dtype),
        scratch_shapes=[pltpu.VMEM((tm, tn), f32)] if nk > 1 else [],
        compiler_params=_cparams(("parallel", "parallel", "arbitrary")))(*args)


def mm_nt(a, b, name, out_dtype=f32, relu2=None, relu2_of=None):
    m, k = a.shape
    n = b.shape[0]
    tm = MM_TILE
    tn = _pick(n, (MM_TILE, 768, 512, 256, 128))

    def body(*refs):
        r = _dot_nt(refs[0][...], refs[1][...])
        if relu2:
            refs[2][...] = r
            pos = jnp.maximum(r, 0.0)
            refs[3][...] = (pos * pos).astype(refs[3].dtype)
        elif relu2_of is not None:
            refs[3][...] = (r * (2.0 * jnp.maximum(refs[2][...], 0.0))).astype(refs[3].dtype)
        else:
            refs[2][...] = r.astype(refs[2].dtype)

    blk = pl.BlockSpec((tm, tn), lambda i, j: (i, j))
    in_specs = [pl.BlockSpec((tm, k), lambda i, j: (i, 0)), pl.BlockSpec((tn, k), lambda i, j: (j, 0))]
    args = [a, b]
    if relu2_of is not None:
        in_specs.append(blk)
        args.append(relu2_of)
    if relu2:
        out_specs, out_shape = [blk, blk], [jax.ShapeDtypeStruct((m, n), f32), jax.ShapeDtypeStruct((m, n), bf16)]
    else:
        out_specs, out_shape = blk, jax.ShapeDtypeStruct((m, n), out_dtype)
    return pl.pallas_call(body, name=name, grid=(m // tm, n // tn), in_specs=in_specs, out_specs=out_specs,
                          out_shape=out_shape, compiler_params=_cparams(("parallel", "parallel")))(*args)


def mm_tn(a, b, name, acc_in=None, acc_scale=1.0):
    k, m = a.shape
    n = b.shape[1]
    tm = _pick(m, (MM_TILE, 768, 512, 256, 128))
    tn = _pick(n, (MM_TILE, 512, 256, 128))
    tk = _pick(k, (1024, 768, 512, 256, 128, 64))
    nk = k // tk

    def body(*refs):
        a_ref, b_ref, o_ref = refs[0], refs[1], refs[-1]
        kk = pl.program_id(2)

        @pl.when(kk == 0)
        def _():
            o_ref[...] = jnp.zeros_like(o_ref) if acc_in is None else acc_scale * refs[2][...]

        o_ref[...] += _dot_tn(a_ref[...], b_ref[...])

    in_specs = [pl.BlockSpec((tk, tm), lambda i, j, kk: (kk, i)), pl.BlockSpec((tk, tn), lambda i, j, kk: (kk, j))]
    args = [a, b]
    if acc_in is not None:
        in_specs.append(pl.BlockSpec((tm, tn), lambda i, j, kk: (i, j)))
        args.append(acc_in)
    return pl.pallas_call(
        body, name=name, grid=(m // tm, n // tn, nk), in_specs=in_specs,
        out_specs=pl.BlockSpec((tm, tn), lambda i, j, kk: (i, j)),
        out_shape=jax.ShapeDtypeStruct((m, n), f32),
        compiler_params=_cparams(("parallel", "parallel", "arbitrary")))(*args)


def _halo_specs(tb, cb, nt, off=0):
    r = tb // 8
    return [pl.BlockSpec((8, cb), lambda j, i: (jnp.maximum(i * r - 1, 0), j + off)),
            pl.BlockSpec((tb, cb), lambda j, i: (i, j + off)),
            pl.BlockSpec((8, cb), lambda j, i: (jnp.minimum((i + 1) * r, nt * r - 1), j + off))]


def _with_halo(prev_ref, own_ref, next_ref, i, nt):
    prev = jnp.where(i > 0, prev_ref[...].astype(f32), 0.0)
    nxt = jnp.where(i < nt - 1, next_ref[...].astype(f32), 0.0)
    return jnp.concatenate([prev, own_ref[...].astype(f32), nxt], axis=0)


def _shifted(xcat, s, tb):
    n = xcat.shape[0]
    return pltpu.roll(xcat, (-s) % n, 0)[8:8 + tb]


def conv_fwd(xbc, w8, b_row, tb=512, cb=512):
    t, c = xbc.shape
    nt = t // tb

    def body(prev_ref, own_ref, next_ref, w_ref, b_ref, o_ref):
        i = pl.program_id(1)
        xcat = _with_halo(prev_ref, own_ref, next_ref, i, nt)
        w = w_ref[...]
        pre = b_ref[...] + w[0:1] * _shifted(xcat, -2, tb)
        for k in range(1, D_CONV):
            pre = pre + w[k:k + 1] * _shifted(xcat, k - 2, tb)
        o_ref[...] = pre * _sigmoid(pre)

    return pl.pallas_call(
        body, name="conv_fwd", grid=(c // cb, nt),
        in_specs=_halo_specs(tb, cb, nt) + [pl.BlockSpec((8, cb), lambda j, i: (0, j)), pl.BlockSpec((1, cb), lambda j, i: (0, j))],
        out_specs=pl.BlockSpec((tb, cb), lambda j, i: (i, j)), out_shape=jax.ShapeDtypeStruct((t, c), f32),
        compiler_params=_cparams(("parallel", "parallel")))(xbc, xbc, xbc, w8, b_row)


def conv_bwd(xbc, xoff, grads, scales, w8, b_row, name, tb=512, cb=512):
    t, c = grads[0].shape
    nt = t // tb
    ng = len(grads)
    has_scale = [s is not None for s in scales]

    def body(*refs):
        i = pl.program_id(1)
        xr = refs[0:3]
        gr = [refs[3 + 3 * q: 6 + 3 * q] for q in range(ng)]
        pos = 3 + 3 * ng
        sr = []
        for q in range(ng):
            if has_scale[q]:
                sr.append(refs[pos])
                pos += 1
            else:
                sr.append(None)
        w_ref, b_ref, dx_ref, dw_ref, db_ref = refs[pos:pos + 5]
        xcat = _with_halo(*xr, i, nt)
        gcat = None
        for q in range(ng):
            gq = _with_halo(*gr[q], i, nt)
            if sr[q] is not None:
                gq = gq * sr[q][...]
            gcat = gq if gcat is None else gcat + gq
        w = w_ref[...]
        n = tb + 16
        pre = b_ref[...] + w[0:1] * pltpu.roll(xcat, 2, 0)
        for k in range(1, D_CONV):
            pre = pre + w[k:k + 1] * pltpu.roll(xcat, (2 - k) % n, 0)
        sg = _sigmoid(pre)
        dpre = gcat * sg * (1.0 + pre * (1.0 - sg))
        dx = w[0:1] * _shifted(dpre, 2, tb)
        for k in range(1, D_CONV):
            dx = dx + w[k:k + 1] * _shifted(dpre, 2 - k, tb)
        dx_ref[...] = dx.astype(dx_ref.dtype)
        dp_own = dpre[8:8 + tb]
        rows = [jnp.sum(dp_own * _shifted(xcat, k - 2, tb), axis=0, keepdims=True) for k in range(D_CONV)]
        dw = jnp.concatenate(rows + [jnp.zeros((8 - D_CONV, cb), f32)], axis=0)
        db = jnp.sum(dp_own, axis=0, keepdims=True)

        @pl.when(i == 0)
        def _():
            dw_ref[...] = jnp.zeros_like(dw_ref)
            db_ref[...] = jnp.zeros_like(db_ref)

        dw_ref[...] += dw
        db_ref[...] += db

    in_specs = _halo_specs(tb, cb, nt, xoff)
    args = [xbc] * 3
    for g in grads:
        in_specs += _halo_specs(tb, cb, nt)
        args += [g] * 3
    for s in scales:
        if s is not None:
            in_specs.append(pl.BlockSpec((1, cb), lambda j, i: (0, j)))
            args.append(s)
    in_specs += [pl.BlockSpec((8, cb), lambda j, i: (0, j)), pl.BlockSpec((1, cb), lambda j, i: (0, j))]
    args += [w8, b_row]
    return pl.pallas_call(
        body, name=name, grid=(c // cb, nt), in_specs=in_specs,
        out_specs=[pl.BlockSpec((tb, cb), lambda j, i: (i, j)), pl.BlockSpec((8, cb), lambda j, i: (0, j)),
                   pl.BlockSpec((1, cb), lambda j, i: (0, j))],
        out_shape=[jax.ShapeDtypeStruct((t, c), bf16), jax.ShapeDtypeStruct((8, c), f32), jax.ShapeDtypeStruct((1, c), f32)],
        compiler_params=_cparams(("parallel", "arbitrary")))(*args)


def dt_fwd(u_dt, bias_row, tb=1024):
    t = u_dt.shape[0]

    def body(u_ref, b_ref, o_ref):
        v = u_ref[...] + b_ref[...]
        sp = jnp.maximum(v, 0.0) + jnp.log(1.0 + jnp.exp(-jnp.abs(v)))
        lane = lax.broadcasted_iota(jnp.int32, v.shape, 1)
        o_ref[...] = jnp.where((lane & 127) < SSD_HEADS, sp, 0.0)

    return pl.pallas_call(
        body, name="dt_fwd", grid=(t // tb,),
        in_specs=[pl.BlockSpec((tb, 256), lambda i: (i, 0)), pl.BlockSpec((1, 256), lambda i: (0, 0))],
        out_specs=pl.BlockSpec((tb, 256), lambda i: (i, 0)), out_shape=jax.ShapeDtypeStruct((t, 256), f32),
        compiler_params=_cparams(("parallel",)))(u_dt, bias_row)


def dt_bwd(ddt_f, ddt_b, u_dt, bias_row, tb=1024):
    t = u_dt.shape[0]

    def body(gf_ref, gb_ref, u_ref, b_ref, du_ref, db_ref):
        g = jnp.concatenate([jnp.sum(gf_ref[...], axis=0), jnp.sum(gb_ref[...], axis=0)], axis=1)
        du = g * _sigmoid(u_ref[...] + b_ref[...])
        du_ref[...] = du.astype(du_ref.dtype)

        @pl.when(pl.program_id(0) == 0)
        def _():
            db_ref[...] = jnp.zeros_like(db_ref)

        db_ref[...] += jnp.sum(du, axis=0, keepdims=True)

    return pl.pallas_call(
        body, name="dt_bwd", grid=(t // tb,),
        in_specs=[pl.BlockSpec((4, tb, 128), lambda i: (0, i, 0)), pl.BlockSpec((4, tb, 128), lambda i: (0, i, 0)),
                  pl.BlockSpec((tb, 256), lambda i: (i, 0)), pl.BlockSpec((1, 256), lambda i: (0, 0))],
        out_specs=[pl.BlockSpec((tb, 256), lambda i: (i, 0)), pl.BlockSpec((1, 256), lambda i: (0, 0))],
        out_shape=[jax.ShapeDtypeStruct((t, 256), bf16), jax.ShapeDtypeStruct((1, 256), f32)],
        compiler_params=_cparams(("arbitrary",)))(ddt_f, ddt_b, u_dt, bias_row)


def _ssd_common(dt_blk, a_row, reverse):
    row = lax.broadcasted_iota(jnp.int32, (CHUNK, CHUNK), 0)
    col = lax.broadcasted_iota(jnp.int32, (CHUNK, CHUNK), 1)
    mask = (row <= col) if reverse else (row >= col)
    tri = mask.astype(f32)
    a = dt_blk * a_row
    acs = _dot_exact(tri, a)
    atot = jnp.sum(a, axis=0, keepdims=True)
    return mask, tri, a, acs, atot, col


def _lane_col(mat, lane_idx, h):
    return jnp.sum(jnp.where(lane_idx == h, mat, 0.0), axis=1, keepdims=True)


def ssd_fwd(xbc_c, dt2, a_rows, reverse, name):
    t = xbc_c.shape[0]
    nc = t // CHUNK
    d_off = 1 if reverse else 0

    def cidx(c):
        return nc - 1 - c if reverse else c

    def body(x_ref, b_ref, c_ref, dt_ref, a_ref, y_ref, hp_ref, h_scr, acst_scr):
        g = pl.program_id(0)
        c = pl.program_id(1)

        @pl.when(c == 0)
        def _():
            h_scr[...] = jnp.zeros_like(h_scr)

        dt_blk = dt_ref[...]
        mask, tri, a, acs, atot, lane = _ssd_common(dt_blk, a_ref[...], reverse)
        acst_scr[...] = acs.T
        bm = b_ref[...]
        cm = c_ref[...]
        cb = _dot_nt(cm, bm)
        half = lane >= SSD_HEAD_DIM
        sub_half = lax.broadcasted_iota(jnp.int32, (CHUNK, 1), 0) >= SSD_HEAD_DIM
        for j in range(4):
            x = x_ref[:, 128 * j:128 * (j + 1)]
            cols, dts, tots = [], [], []
            y = None
            for e in range(2):
                h = 8 * g + 2 * j + e
                col_h = _lane_col(acs, lane, h)
                row_h = acst_scr[pl.ds(h, 1), :]
                dt_h = _lane_col(dt_blk, lane, h)
                lmat = jnp.where(mask, jnp.exp(jnp.where(mask, col_h - row_h, 0.0)), 0.0)
                xdt_e = jnp.where(half == (e == 1), x * dt_h, 0.0)
                ye = _dot(cb * lmat, xdt_e)
                y = ye if y is None else y + ye
                cols.append(col_h)
                dts.append(dt_h)
                tots.append(jnp.sum(jnp.where(lane[0:1] == h, atot, 0.0), axis=1, keepdims=True))
            hp = h_scr[j]
            hp_ref[0, j] = hp
            ecol = jnp.where(half, jnp.exp(cols[1]), jnp.exp(cols[0]))
            y = y + _dot_nt(cm, hp) * ecol
            y_ref[:, 128 * j:128 * (j + 1)] = y
            dec = jnp.where(half, jnp.exp(tots[1] - cols[1]), jnp.exp(tots[0] - cols[0]))
            xdt = x * jnp.where(half, dts[1], dts[0])
            s_new = _dot_tn(xdt * dec, bm)
            cd = jnp.where(sub_half, jnp.exp(tots[1]), jnp.exp(tots[0]))
            h_scr[j] = cd * hp + s_new

    return pl.pallas_call(
        body, name=name, grid=(SSD_GROUPS, nc),
        in_specs=[pl.BlockSpec((CHUNK, 512), lambda g, c: (cidx(c), g)),
                  pl.BlockSpec((CHUNK, 128), lambda g, c: (cidx(c), 16 + g)),
                  pl.BlockSpec((CHUNK, 128), lambda g, c: (cidx(c), 20 + g)),
                  pl.BlockSpec((CHUNK, 128), lambda g, c: (cidx(c), d_off)),
                  pl.BlockSpec((1, 128), lambda g, c: (0, d_off))],
        out_specs=[pl.BlockSpec((CHUNK, 512), lambda g, c: (cidx(c), g)),
                   pl.BlockSpec((1, 4, 128, 128), lambda g, c: (cidx(c), g, 0, 0))],
        out_shape=[jax.ShapeDtypeStruct((t, D_INNER), f32), jax.ShapeDtypeStruct((nc, 16, 128, 128), f32)],
        scratch_shapes=[pltpu.VMEM((4, 128, 128), f32), pltpu.VMEM((CHUNK, CHUNK), f32)],
        compiler_params=_cparams(("parallel", "arbitrary")))(xbc_c, xbc_c, xbc_c, dt2, a_rows)


def ssd_bwd(xbc_c, dt2, a_rows, dy, hprev, reverse, name):
    t = xbc_c.shape[0]
    nc = t // CHUNK
    d_off = 1 if reverse else 0

    def cidx(c):
        return c if reverse else nc - 1 - c

    def body(x_ref, b_ref, c_ref, dt_ref, a_ref, dy_ref, hp_ref, dx_ref, db_ref, dc_ref, ddt_ref, da_ref,
             dh_scr, acst_scr):
        g = pl.program_id(0)
        c = pl.program_id(1)

        @pl.when(c == 0)
        def _():
            dh_scr[...] = jnp.zeros_like(dh_scr)
            da_ref[...] = jnp.zeros_like(da_ref)

        dt_blk = dt_ref[...]
        a_row = a_ref[...]
        mask, tri, a, acs, atot, lane = _ssd_common(dt_blk, a_row, reverse)
        acst_scr[...] = acs.T
        sub = lax.broadcasted_iota(jnp.int32, (CHUNK, CHUNK), 0)
        bm = b_ref[...]
        cm = c_ref[...]
        cb = _dot_nt(cm, bm)
        half = lane >= SSD_HEAD_DIM
        sub_half = sub[:, 0:1] >= SSD_HEAD_DIM
        dcb = jnp.zeros((CHUNK, CHUNK), f32)
        dacs = jnp.zeros((CHUNK, CHUNK), f32)
        dacs_t = jnp.zeros((CHUNK, CHUNK), f32)
        dtot = jnp.zeros((1, CHUNK), f32)
        ddt_x = jnp.zeros((CHUNK, CHUNK), f32)
        dbm = jnp.zeros((CHUNK, D_STATE), f32)
        dcm = jnp.zeros((CHUNK, D_STATE), f32)
        for j in range(4):
            x = x_ref[:, 128 * j:128 * (j + 1)]
            dyp = dy_ref[:, 128 * j:128 * (j + 1)]
            hp = hp_ref[0, j]
            dhn = dh_scr[j]
            cols, dts, tots, hs = [], [], [], []
            dxdt = None
            for e in range(2):
                h = 8 * g + 2 * j + e
                sel = half == (e == 1)
                col_h = _lane_col(acs, lane, h)
                row_h = acst_scr[pl.ds(h, 1), :]
                dt_h = _lane_col(dt_blk, lane, h)
                lmat = jnp.where(mask, jnp.exp(jnp.where(mask, col_h - row_h, 0.0)), 0.0)
                xdt_e = jnp.where(sel, x * dt_h, 0.0)
                dy_e = jnp.where(sel, dyp, 0.0)
                ml = _dot_nt(dy_e, xdt_e) * lmat
                dcb = dcb + ml
                w = ml * cb
                dacs = dacs + jnp.where(lane == h, jnp.sum(w, axis=1, keepdims=True), 0.0)
                dacs_t = dacs_t - jnp.where(sub == h, jnp.sum(w, axis=0, keepdims=True), 0.0)
                de = _dot_tn(cb * lmat, dy_e)
                dxdt = de if dxdt is None else dxdt + de
                cols.append(col_h)
                dts.append(dt_h)
                tots.append(jnp.sum(jnp.where(lane[0:1] == h, atot, 0.0), axis=1, keepdims=True))
                hs.append(h)
            ecol = jnp.where(half, jnp.exp(cols[1]), jnp.exp(cols[0]))
            dec = jnp.where(half, jnp.exp(tots[1] - cols[1]), jnp.exp(tots[0] - cols[0]))
            cd = jnp.where(sub_half, jnp.exp(tots[1]), jnp.exp(tots[0]))
            dtp = jnp.where(half, dts[1], dts[0])
            xdt = x * dtp
            yoff = _dot_nt(cm, hp) * ecol
            dye = dyp * ecol
            dcm = dcm + _dot(dye, hp)
            dhp = _dot_tn(dye, cm)
            gmat = _dot_nt(bm, dhn)
            dxdt = dxdt + dec * gmat
            dbm = dbm + _dot(xdt * dec, dhn)
            r_off = dyp * yoff
            r_dec = xdt * gmat * dec
            r_x = dxdt * x
            hh = dhn * hp
            for e in range(2):
                sel = half == (e == 1)
                h = hs[e]
                s_off = jnp.sum(jnp.where(sel, r_off, 0.0), axis=1, keepdims=True)
                s_dec = jnp.sum(jnp.where(sel, r_dec, 0.0), axis=1, keepdims=True)
                dacs = dacs + jnp.where(lane == h, s_off - s_dec, 0.0)
                dcd = jnp.sum(jnp.sum(jnp.where(sub_half == (e == 1), hh, 0.0), axis=1, keepdims=True), axis=0, keepdims=True)
                tot_e = jnp.sum(s_dec, axis=0, keepdims=True) + jnp.exp(tots[e]) * dcd
                dtot = dtot + jnp.where(lane[0:1] == h, tot_e, 0.0)
                ddt_x = ddt_x + jnp.where(lane == h, jnp.sum(jnp.where(sel, r_x, 0.0), axis=1, keepdims=True), 0.0)
            dx_ref[:, 128 * j:128 * (j + 1)] = dxdt * dtp
            dh_scr[j] = cd * dhn + dhp
        dcm = dcm + _dot(dcb, bm)
        dbm = dbm + _dot_tn(dcb, cm)
        db_ref[...] = dbm
        dc_ref[...] = dcm
        dacs = dacs + dacs_t.T
        da = _dot_exact(tri.T, dacs) + dtot
        ddt_ref[0] = da * a_row + ddt_x
        da_ref[0] += jnp.sum(da * dt_blk, axis=0, keepdims=True)

    return pl.pallas_call(
        body, name=name, grid=(SSD_GROUPS, nc),
        in_specs=[pl.BlockSpec((CHUNK, 512), lambda g, c: (cidx(c), g)),
                  pl.BlockSpec((CHUNK, 128), lambda g, c: (cidx(c), 16 + g)),
                  pl.BlockSpec((CHUNK, 128), lambda g, c: (cidx(c), 20 + g)),
                  pl.BlockSpec((CHUNK, 128), lambda g, c: (cidx(c), d_off)),
                  pl.BlockSpec((1, 128), lambda g, c: (0, d_off)),
                  pl.BlockSpec((CHUNK, 512), lambda g, c: (cidx(c), g)),
                  pl.BlockSpec((1, 4, 128, 128), lambda g, c: (cidx(c), g, 0, 0))],
        out_specs=[pl.BlockSpec((CHUNK, 512), lambda g, c: (cidx(c), g)),
                   pl.BlockSpec((CHUNK, 128), lambda g, c: (cidx(c), g)),
                   pl.BlockSpec((CHUNK, 128), lambda g, c: (cidx(c), g)),
                   pl.BlockSpec((1, CHUNK, 128), lambda g, c: (g, cidx(c), 0)),
                   pl.BlockSpec((1, 1, 128), lambda g, c: (g, 0, 0))],
        out_shape=[jax.ShapeDtypeStruct((t, D_INNER), f32), jax.ShapeDtypeStruct((t, 512), f32),
                   jax.ShapeDtypeStruct((t, 512), f32), jax.ShapeDtypeStruct((4, t, 128), f32),
                   jax.ShapeDtypeStruct((4, 1, 128), f32)],
        scratch_shapes=[pltpu.VMEM((4, 128, 128), f32), pltpu.VMEM((CHUNK, CHUNK), f32)],
        compiler_params=_cparams(("parallel", "arbitrary")))(xbc_c, xbc_c, xbc_c, dt2, a_rows, dy, hprev)


def tail_fwd(y_f, y_b, xbc_c, z, dskip_row, nw_row, tb=512):
    t = y_f.shape[0]

    def body(yf_ref, yb_ref, x_ref, z_ref, d_ref, w_ref, o_ref):
        zz = z_ref[...]
        y = (yf_ref[...] + yb_ref[...] + d_ref[...] * x_ref[...]) * (zz * _sigmoid(zz))
        rstd = lax.rsqrt(jnp.mean(y * y, axis=1, keepdims=True) + NORM_EPS)
        o_ref[...] = (y * rstd * w_ref[...]).astype(o_ref.dtype)

    blk = pl.BlockSpec((tb, 512), lambda i, g: (i, g))
    row = pl.BlockSpec((1, 512), lambda i, g: (0, g))
    return pl.pallas_call(
        body, name="tail_fwd", grid=(t // tb, SSD_GROUPS), in_specs=[blk, blk, blk, blk, row, row], out_specs=blk,
        out_shape=jax.ShapeDtypeStruct((t, D_INNER), bf16),
        compiler_params=_cparams(("parallel", "parallel")))(y_f, y_b, xbc_c, z, dskip_row, nw_row)


def tail_bwd(dyn, y_f, y_b, xbc_c, z, dskip_row, nw_row, tb=512):
    t = y_f.shape[0]

    def body(g_ref, yf_ref, yb_ref, x_ref, z_ref, d_ref, w_ref, dy_ref, dz_ref, dw_ref, dd_ref):
        zz = z_ref[...]
        sg = _sigmoid(zz)
        sl = zz * sg
        x = x_ref[...]
        y = yf_ref[...] + yb_ref[...] + d_ref[...] * x
        yz = y * sl
        rstd = lax.rsqrt(jnp.mean(yz * yz, axis=1, keepdims=True) + NORM_EPS)
        yhat = yz * rstd
        g = g_ref[...]
        dyhat = g * w_ref[...]
        dyz = rstd * (dyhat - yhat * jnp.mean(dyhat * yhat, axis=1, keepdims=True))
        dy = dyz * sl
        dy_ref[...] = dy
        dz_ref[...] = (dyz * y * sg * (1.0 + zz * (1.0 - sg))).astype(dz_ref.dtype)

        @pl.when(pl.program_id(1) == 0)
        def _():
            dw_ref[...] = jnp.zeros_like(dw_ref)
            dd_ref[...] = jnp.zeros_like(dd_ref)

        dw_ref[...] += jnp.sum(g * yhat, axis=0, keepdims=True)
        dd_ref[...] += jnp.sum(dy * x, axis=0, keepdims=True)

    blk = pl.BlockSpec((tb, 512), lambda g, i: (i, g))
    row = pl.BlockSpec((1, 512), lambda g, i: (0, g))
    return pl.pallas_call(
        body, name="tail_bwd", grid=(SSD_GROUPS, t // tb), in_specs=[blk, blk, blk, blk, blk, row, row],
        out_specs=[blk, blk, row, row],
        out_shape=[jax.ShapeDtypeStruct((t, D_INNER), f32), jax.ShapeDtypeStruct((t, D_INNER), bf16),
                   jax.ShapeDtypeStruct((1, D_INNER), f32), jax.ShapeDtypeStruct((1, D_INNER), f32)],
        compiler_params=_cparams(("parallel", "arbitrary")))(dyn, y_f, y_b, xbc_c, z, dskip_row, nw_row)


def _slopes(p):
    return [2.0 ** (-8.0 * (HEADS_PER_PATTERN * p + j + 1) / ATTN_HEADS) for j in range(HEADS_PER_PATTERN)]


def _win_specs(nq, col_of):
    return [pl.BlockSpec((64, 256), lambda r, i: (jnp.maximum(2 * i - 1, 0), col_of(r))),
            pl.BlockSpec((128, 256), lambda r, i: (i, col_of(r))),
            pl.BlockSpec((64, 256), lambda r, i: (jnp.minimum(2 * i + 2, 2 * nq - 1), col_of(r)))]


def _lane_head(shape):
    return lax.broadcasted_iota(jnp.int32, shape, 1) >> 6


def _stack_heads(m):
    lane_head = _lane_head(m.shape)
    return jnp.concatenate([jnp.where(lane_head == j, m, 0.0) for j in range(HEADS_PER_PATTERN)], axis=0)


def _unstack_heads(m4, n):
    lane_head = _lane_head((n, 256))
    out = jnp.where(lane_head == 0, m4[0:n], 0.0)
    for j in range(1, HEADS_PER_PATTERN):
        out = out + jnp.where(lane_head == j, m4[j * n:(j + 1) * n], 0.0)
    return out


def _head_cols(m, n):
    lane = lax.broadcasted_iota(jnp.int32, (n, 256), 1)
    return jnp.concatenate([jnp.sum(jnp.where(lane == ATTN_HEAD_DIM * j, m, 0.0), axis=1, keepdims=True)
                            for j in range(HEADS_PER_PATTERN)], axis=0)


def _score_bias(p, dil, by_key):
    slopes = np.asarray(_slopes(p), np.float32)
    if by_key:
        win = np.arange(256)[:, None]
        rel = np.arange(128)[None, :] - (win - 64)
    else:
        win = np.arange(256)[None, :]
        rel = win - 64 - np.arange(128)[:, None]
    band = np.abs(rel) <= 64
    out = []
    for first, last in ((False, False), (True, False), (False, True), (True, True)):
        ok = band & ~(first & (win < 64)) & ~(last & (win >= 192))
        pen = -slopes[:, None, None] * (np.abs(rel) * dil).astype(np.float32)[None]
        out.append(np.where(ok[None], pen, np.float32(NEG_BIG)).reshape(-1, rel.shape[1]))
    return jnp.asarray(np.stack(out), f32)


def _bias_spec(nq, rows, cols):
    return pl.BlockSpec((1, rows, cols), lambda r, i: ((i == 0).astype(jnp.int32) + 2 * (i == nq - 1).astype(jnp.int32), 0, 0))


def attn_fwd(q, k, v, p, dil, name):
    l = q.shape[0]
    nq = l // 128

    def body(q_ref, kp_ref, ko_ref, kn_ref, vp_ref, vo_ref, vn_ref, bias_ref, o_ref, lse_ref):
        kcat = jnp.concatenate([kp_ref[...], ko_ref[...], kn_ref[...]], axis=0)
        vcat = jnp.concatenate([vp_ref[...], vo_ref[...], vn_ref[...]], axis=0)
        s = _dot_nt(_stack_heads(q_ref[...] * 0.125), kcat) + bias_ref[0]
        m = jnp.max(s, axis=1, keepdims=True)
        pr = jnp.exp(s - m)
        den = jnp.sum(pr, axis=1, keepdims=True)
        o4 = _dot(pr, vcat) / den
        o_ref[...] = _unstack_heads(o4, 128)
        lse_ref[...] = _unstack_heads(jnp.broadcast_to(m + jnp.log(den), (512, 256)), 128)

    col = lambda r: r
    return pl.pallas_call(
        body, name=name, grid=(dil, nq),
        in_specs=[pl.BlockSpec((128, 256), lambda r, i: (i, r))] + _win_specs(nq, col) + _win_specs(nq, col)
        + [_bias_spec(nq, 512, 256)],
        out_specs=[pl.BlockSpec((128, 256), lambda r, i: (i, r))] * 2,
        out_shape=[jax.ShapeDtypeStruct(q.shape, f32)] * 2,
        compiler_params=_cparams(("parallel", "parallel")))(q, k, k, k, v, v, v, _score_bias(p, dil, False))


def attn_combine(os_, lses, tb=1024):
    t = os_[0].shape[0]

    def body(o0, o1, o2, l0, l1, l2, y_ref, lse_ref):
        a0, a1, a2 = l0[...], l1[...], l2[...]
        m = jnp.maximum(jnp.maximum(a0, a1), a2)
        e0, e1, e2 = jnp.exp(a0 - m), jnp.exp(a1 - m), jnp.exp(a2 - m)
        den = e0 + e1 + e2
        y_ref[...] = (e0 * o0[...] + e1 * o1[...] + e2 * o2[...]) / den
        lse_ref[...] = m + jnp.log(den)

    blk = pl.BlockSpec((tb, 256), lambda i: (i, 0))
    return pl.pallas_call(
        body, name="attn_combine", grid=(t // tb,), in_specs=[blk] * 6, out_specs=[blk, blk],
        out_shape=[jax.ShapeDtypeStruct((t, 256), f32)] * 2,
        compiler_params=_cparams(("parallel",)))(*os_, *lses)


def attn_delta(dy, y, tb=1024):
    t = dy.shape[0]

    def body(dy_ref, y_ref, d_ref):
        pr = dy_ref[...] * y_ref[...]
        lane_head = _lane_head(pr.shape)
        out = jnp.zeros_like(pr)
        for j in range(HEADS_PER_PATTERN):
            sj = jnp.sum(jnp.where(lane_head == j, pr, 0.0), axis=1, keepdims=True)
            out = out + jnp.where(lane_head == j, sj, 0.0)
        d_ref[...] = out

    blk = pl.BlockSpec((tb, 256), lambda i: (i, 0))
    return pl.pallas_call(body, name="attn_delta", grid=(t // tb,), in_specs=[blk, blk], out_specs=blk,
                          out_shape=jax.ShapeDtypeStruct((t, 256), f32),
                          compiler_params=_cparams(("parallel",)))(dy, y)


def attn_dq(q, k, v, dy, lse, delta, p, dil, name):
    l = q.shape[0]
    nq = l // 128

    def body(q_ref, kp_ref, ko_ref, kn_ref, vp_ref, vo_ref, vn_ref, dy_ref, lse_ref, d_ref, bias_ref, dq_ref):
        kcat = jnp.concatenate([kp_ref[...], ko_ref[...], kn_ref[...]], axis=0)
        vcat = jnp.concatenate([vp_ref[...], vo_ref[...], vn_ref[...]], axis=0)
        s = _dot_nt(_stack_heads(q_ref[...] * 0.125), kcat) + bias_ref[0]
        pr = jnp.exp(s - _head_cols(lse_ref[...], 128))
        dp = _dot_nt(_stack_heads(dy_ref[...]), vcat)
        ds = pr * (dp - _head_cols(d_ref[...], 128))
        dq_ref[...] = (_unstack_heads(_dot(ds, kcat), 128) * 0.125).astype(dq_ref.dtype)

    col = lambda r: r
    own = pl.BlockSpec((128, 256), lambda r, i: (i, r))
    return pl.pallas_call(
        body, name=name, grid=(dil, nq),
        in_specs=[own] + _win_specs(nq, col) + _win_specs(nq, col) + [own, own, own, _bias_spec(nq, 512, 256)],
        out_specs=own, out_shape=jax.ShapeDtypeStruct(q.shape, bf16),
        compiler_params=_cparams(("parallel", "parallel")))(q, k, k, k, v, v, v, dy, lse, delta, _score_bias(p, dil, False))


def attn_dkv(q, k, v, dy, lse, delta, p, dil, name):
    l = q.shape[0]
    nq = l // 128

    def body(qp_ref, qo_ref, qn_ref, gp_ref, go_ref, gn_ref, lp_ref, lo_ref, ln_ref, dp_ref, do_ref, dn_ref,
             k_ref, v_ref, bias_ref, dk_ref, dv_ref):
        cat = lambda a, b, c: jnp.concatenate([a[...], b[...], c[...]], axis=0)
        q4 = _stack_heads(cat(qp_ref, qo_ref, qn_ref) * 0.125)
        dy4 = _stack_heads(cat(gp_ref, go_ref, gn_ref))
        lse4 = _head_cols(cat(lp_ref, lo_ref, ln_ref), 256)
        del4 = _head_cols(cat(dp_ref, do_ref, dn_ref), 256)
        s = _dot_nt(q4, k_ref[...]) + bias_ref[0]
        pr = jnp.exp(s - lse4)
        dpm = _dot_nt(dy4, v_ref[...])
        ds = pr * (dpm - del4)
        dv_ref[...] = _dot_tn(pr, dy4).astype(dv_ref.dtype)
        dk_ref[...] = _dot_tn(ds, q4).astype(dk_ref.dtype)

    col = lambda r: r
    own = pl.BlockSpec((128, 256), lambda r, i: (i, r))
    win = _win_specs(nq, col)
    return pl.pallas_call(
        body, name=name, grid=(dil, nq), in_specs=win * 4 + [own, own, _bias_spec(nq, 1024, 128)], out_specs=[own, own],
        out_shape=[jax.ShapeDtypeStruct(q.shape, bf16)] * 2,
        compiler_params=_cparams(("parallel", "parallel")))(q, q, q, dy, dy, dy, lse, lse, lse, delta, delta, delta, k, v,
                                                            _score_bias(p, dil, True))


def _lanes(v, reps):
    return v if reps == 1 else jnp.tile(v, (1, reps))


def _lane_halo_specs(cb, tb, nt, off=0):
    r = tb // 128
    return [pl.BlockSpec((cb, 128), lambda j, i: (j + off, jnp.maximum(i * r - 1, 0))),
            pl.BlockSpec((cb, tb), lambda j, i: (j + off, i)),
            pl.BlockSpec((cb, 128), lambda j, i: (j + off, jnp.minimum((i + 1) * r, nt * r - 1)))]


def _with_lane_halo(prev_ref, own_ref, next_ref, i, nt):
    prev = jnp.where(i > 0, prev_ref[...].astype(f32), 0.0)
    nxt = jnp.where(i < nt - 1, next_ref[...].astype(f32), 0.0)
    return jnp.concatenate([prev, own_ref[...].astype(f32), nxt], axis=1)


def _lane_shifted(xcat, s, tb):
    n = xcat.shape[1]
    return pltpu.roll(xcat, (-s) % n, 1)[:, 128:128 + tb]


def conv_fwd_t(xbc_t, w_b, b_b, tb=1024, cb=256):
    c, t = xbc_t.shape
    nt = t // tb

    def body(prev_ref, own_ref, next_ref, w_ref, b_ref, o_ref):
        i = pl.program_id(1)
        xcat = _with_lane_halo(prev_ref, own_ref, next_ref, i, nt)
        reps = tb // 128
        pre = _lanes(b_ref[...], reps)
        for k in range(D_CONV):
            pre = pre + _lanes(w_ref[k], reps) * _lane_shifted(xcat, k - 2, tb)
        o_ref[...] = pre * _sigmoid(pre)

    return pl.pallas_call(
        body, name="conv_fwd", grid=(c // cb, nt),
        in_specs=_lane_halo_specs(cb, tb, nt) + [pl.BlockSpec((D_CONV, cb, 128), lambda j, i: (0, j, 0)),
                                                 pl.BlockSpec((cb, 128), lambda j, i: (j, 0))],
        out_specs=pl.BlockSpec((cb, tb), lambda j, i: (j, i)), out_shape=jax.ShapeDtypeStruct((c, t), f32),
        compiler_params=_cparams(("parallel", "parallel")))(xbc_t, xbc_t, xbc_t, w_b, b_b)


def conv_bwd_t(xbc_t, grad_t, w_b, b_b, into, name, row0, tb=1024, cb=256):
    c, t = grad_t.shape
    nt = t // tb
    off = row0 // cb
    reps = tb // 128

    def body(*refs):
        i = pl.program_id(1)
        xr, gr = refs[0:3], refs[3:6]
        w_ref, b_ref = refs[6:8]
        dx_ref, dw_ref, db_ref = refs[-3:]
        xcat = _with_lane_halo(*xr, i, nt)
        gcat = _with_lane_halo(*gr, i, nt)
        n = tb + 256
        wk = [_lanes(w_ref[k], reps + 2) for k in range(D_CONV)]
        pre = _lanes(b_ref[...], reps + 2)
        for k in range(D_CONV):
            pre = pre + wk[k] * pltpu.roll(xcat, (2 - k) % n, 1)
        sg = _sigmoid(pre)
        dpre = gcat * sg * (1.0 + pre * (1.0 - sg))
        dx = None
        for k in range(D_CONV):
            term = wk[k][:, 128:128 + tb] * _lane_shifted(dpre, 2 - k, tb)
            dx = term if dx is None else dx + term
        dx_ref[...] = dx.astype(dx_ref.dtype)
        dp_own = dpre[:, 128:128 + tb]

        def fold(v):
            s = v[:, 0:128]
            for q in range(1, reps):
                s = s + v[:, 128 * q:128 * (q + 1)]
            return s

        @pl.when(i == 0)
        def _():
            dw_ref[...] = jnp.zeros_like(dw_ref)
            db_ref[...] = jnp.zeros_like(db_ref)

        for k in range(D_CONV):
            dw_ref[k] += fold(dp_own * _lane_shifted(xcat, k - 2, tb))
        db_ref[...] += fold(dp_own)

    in_specs = (_lane_halo_specs(cb, tb, nt, off) + _lane_halo_specs(cb, tb, nt)
                + [pl.BlockSpec((D_CONV, cb, 128), lambda j, i: (0, j + off, 0)), pl.BlockSpec((cb, 128), lambda j, i: (j + off, 0))])
    args = [xbc_t] * 3 + [grad_t] * 3 + [w_b, b_b]
    aliases = {}
    if into is not None:
        in_specs.append(pl.BlockSpec(memory_space=pl.ANY))
        args.append(into)
        aliases = {len(args) - 1: 0}
    return pl.pallas_call(
        body, name=name, grid=(c // cb, nt), in_specs=in_specs,
        out_specs=[pl.BlockSpec((cb, tb), lambda j, i: (j + off, i)), pl.BlockSpec((D_CONV, cb, 128), lambda j, i: (0, j, 0)),
                   pl.BlockSpec((cb, 128), lambda j, i: (j, 0))],
        out_shape=[jax.ShapeDtypeStruct((CONV_DIM, t), bf16), jax.ShapeDtypeStruct((D_CONV, c, 128), f32),
                   jax.ShapeDtypeStruct((c, 128), f32)],
        input_output_aliases=aliases, compiler_params=_cparams(("parallel", "arbitrary")))(*args)


def dt_fwd_t(u_dt_t, bias_b, tb=2048):
    r, t = u_dt_t.shape

    def body(u_ref, b_ref, o_ref):
        v = u_ref[...] + _lanes(b_ref[...], tb // 128)
        o_ref[...] = jnp.maximum(v, 0.0) + jnp.log(1.0 + jnp.exp(-jnp.abs(v)))

    return pl.pallas_call(
        body, name="dt_fwd", grid=(t // tb,),
        in_specs=[pl.BlockSpec((r, tb), lambda i: (0, i)), pl.BlockSpec((r, 128), lambda i: (0, 0))],
        out_specs=pl.BlockSpec((r, tb), lambda i: (0, i)), out_shape=jax.ShapeDtypeStruct((r, t), f32),
        compiler_params=_cparams(("parallel",)))(u_dt_t, bias_b)


def dt_bwd_t(ddt_f, ddt_b, u_dt_t, bias_b, tb=2048):
    r, t = u_dt_t.shape
    reps = tb // 128

    def body(gf_ref, gb_ref, u_ref, b_ref, du_ref, db_ref):
        g = jnp.concatenate([gf_ref[...], gb_ref[...]], axis=0)
        du = g * _sigmoid(u_ref[...] + _lanes(b_ref[...], reps))
        du_ref[...] = du.astype(du_ref.dtype)

        @pl.when(pl.program_id(0) == 0)
        def _():
            db_ref[...] = jnp.zeros_like(db_ref)

        s = du[:, 0:128]
        for q in range(1, reps):
            s = s + du[:, 128 * q:128 * (q + 1)]
        db_ref[...] += s

    half = pl.BlockSpec((r // 2, tb), lambda i: (0, i))
    return pl.pallas_call(
        body, name="dt_bwd", grid=(t // tb,),
        in_specs=[half, half, pl.BlockSpec((r, tb), lambda i: (0, i)), pl.BlockSpec((r, 128), lambda i: (0, 0))],
        out_specs=[pl.BlockSpec((r, tb), lambda i: (0, i)), pl.BlockSpec((r, 128), lambda i: (0, 0))],
        out_shape=[jax.ShapeDtypeStruct((r, t), bf16), jax.ShapeDtypeStruct((r, 128), f32)],
        compiler_params=_cparams(("arbitrary",)))(ddt_f, ddt_b, u_dt_t, bias_b)


HEADS_PER_GROUP = SSD_HEADS // SSD_GROUPS


def _group_rows(g, n):
    return pl.ds(pl.multiple_of(g * n, n), n)


def _ssd_decays(dt_blk, a_blk, reverse):
    row = lax.broadcasted_iota(jnp.int32, (CHUNK, CHUNK), 0)
    col = lax.broadcasted_iota(jnp.int32, (CHUNK, CHUNK), 1)
    mask = (row <= col) if reverse else (row >= col)
    tri = mask.astype(f32)
    a8 = dt_blk * a_blk
    a = jnp.concatenate([a8, jnp.zeros((CHUNK - HEADS_PER_GROUP, CHUNK), f32)], axis=0).T
    acs = _dot_exact(tri, a)
    return mask, tri, a8, acs, acs.T, col


def ssd_fwd_t(xbc_ct, dt_t, a_b, reverse, name, prev=None):
    t = xbc_ct.shape[1]
    nc = t // CHUNK
    direction = 1 if reverse else 0

    def cidx(c):
        return nc - 1 - c if reverse else c

    def body(*refs):
        x_ref, b_ref, c_ref, dt_ref, a_ref = refs[0:5]
        prev_ref = refs[5] if prev is not None else None
        y_ref, hp_ref, h_scr = refs[-3:]

        @pl.when(pl.program_id(0) == 0)
        def _():
            h_scr[...] = jnp.zeros_like(h_scr)

        def group(g, carry):
            x_v, y_v = x_ref.at[_group_rows(g, 512)], y_ref.at[_group_rows(g, 512)]
            heads = _group_rows(g, HEADS_PER_GROUP)
            hp_v, h_v = hp_ref.at[0, heads], h_scr.at[heads]
            dt_blk = dt_ref[heads, :]
            mask, tri, a8, acs, acs_t, lane = _ssd_decays(dt_blk, a_ref[heads, :], reverse)
            bm = b_ref[_group_rows(g, 128), :].T
            cm = c_ref[_group_rows(g, 128), :].T
            cb = _dot_nt(cm, bm)
            tot = jnp.sum(a8, axis=1, keepdims=True)
            for j in range(HEADS_PER_GROUP):
                rows = slice(SSD_HEAD_DIM * j, SSD_HEAD_DIM * (j + 1))
                col_j = _lane_col(acs, lane, j)
                row_j = acs_t[j:j + 1, :]
                lmat = jnp.where(mask, jnp.exp(jnp.where(mask, col_j - row_j, 0.0)), 0.0)
                xdt = x_v[rows, :] * dt_blk[j:j + 1, :]
                hp = h_v[j]
                hp_v[j] = hp
                y = _dot_nt(xdt, cb * lmat) + _dot_nt(hp, cm) * jnp.exp(row_j)
                if prev_ref is not None:
                    y = y + prev_ref.at[_group_rows(g, 512)][rows, :]
                y_v[rows, :] = y
                tot_j = tot[j:j + 1, :]
                h_v[j] = jnp.exp(tot_j) * hp + _dot(xdt * jnp.exp(tot_j - row_j), bm)
            return carry

        lax.fori_loop(0, SSD_GROUPS, group, 0)

    big = pl.BlockSpec((D_INNER, CHUNK), lambda c: (0, cidx(c)))
    in_specs = [big, pl.BlockSpec((512, CHUNK), lambda c: (4, cidx(c))), pl.BlockSpec((512, CHUNK), lambda c: (5, cidx(c))),
                pl.BlockSpec((SSD_HEADS, CHUNK), lambda c: (direction, cidx(c))),
                pl.BlockSpec((SSD_HEADS, 128), lambda c: (direction, 0))]
    args = [xbc_ct, xbc_ct, xbc_ct, dt_t, a_b]
    if prev is not None:
        in_specs.append(big)
        args.append(prev)
    return pl.pallas_call(
        body, name=name, grid=(nc,), in_specs=in_specs,
        out_specs=[big, pl.BlockSpec((1, SSD_HEADS, SSD_HEAD_DIM, D_STATE), lambda c: (cidx(c), 0, 0, 0))],
        out_shape=[jax.ShapeDtypeStruct((D_INNER, t), f32), jax.ShapeDtypeStruct((nc, SSD_HEADS, SSD_HEAD_DIM, D_STATE), f32)],
        scratch_shapes=[pltpu.VMEM((SSD_HEADS, SSD_HEAD_DIM, D_STATE), f32)],
        compiler_params=_cparams(("arbitrary",)))(*args)


def ssd_bwd_t(xbc_ct, dt_t, a_b, dy_t, hprev, reverse, name, skip_b=None, prev=None):
    t = xbc_ct.shape[1]
    nc = t // CHUNK
    direction = 1 if reverse else 0

    def cidx(c):
        return c if reverse else nc - 1 - c

    def body(*refs):
        x_ref, b_ref, c_ref, dt_ref, a_ref, dy_ref, hp_ref = refs[0:7]
        pos = 7
        skip_ref = None
        if skip_b is not None:
            skip_ref = refs[pos]
            pos += 1
        prev_refs = None
        if prev is not None:
            prev_refs = refs[pos:pos + 3]
            pos += 3
        dx_ref, db_ref, dc_ref, ddt_ref, da_ref, dh_scr = refs[pos:pos + 6]

        @pl.when(pl.program_id(0) == 0)
        def _():
            dh_scr[...] = jnp.zeros_like(dh_scr)
            da_ref[...] = jnp.zeros_like(da_ref)

        def group(g, carry):
            big, st, heads = _group_rows(g, 512), _group_rows(g, 128), _group_rows(g, HEADS_PER_GROUP)
            x_v, dy_v, dx_v = x_ref.at[big], dy_ref.at[big], dx_ref.at[big]
            hp_v, dh_v = hp_ref.at[0, heads], dh_scr.at[heads]
            dt_blk = dt_ref[heads, :]
            a_blk = a_ref[heads, :]
            mask, tri, a8, acs, acs_t, lane = _ssd_decays(dt_blk, a_blk, reverse)
            sub = lax.broadcasted_iota(jnp.int32, (CHUNK, CHUNK), 0)
            mask_t = (sub >= lane) if reverse else (sub <= lane)
            bm = b_ref[st, :].T
            cm = c_ref[st, :].T
            cb = _dot_nt(cm, bm)
            cb_t = _dot_nt(bm, cm)
            tot = jnp.sum(a8, axis=1, keepdims=True)
            dcb = jnp.zeros((CHUNK, CHUNK), f32)
            dbm = jnp.zeros((CHUNK, D_STATE), f32)
            dcm = jnp.zeros((CHUNK, D_STATE), f32)
            dacs_rows, ddtx_rows = [], []
            for j in range(HEADS_PER_GROUP):
                rows = slice(SSD_HEAD_DIM * j, SSD_HEAD_DIM * (j + 1))
                col_j = _lane_col(acs, lane, j)
                row_j = acs_t[j:j + 1, :]
                dt_j = dt_blk[j:j + 1, :]
                tot_j = tot[j:j + 1, :]
                lmat = jnp.where(mask, jnp.exp(jnp.where(mask, col_j - row_j, 0.0)), 0.0)
                lmat_t = jnp.where(mask_t, jnp.exp(jnp.where(mask_t, row_j - col_j, 0.0)), 0.0)
                x = x_v[rows, :]
                xdt = x * dt_j
                dyh = dy_v[rows, :]
                hp = hp_v[j]
                dhn = dh_v[j]
                ml = _dot_tn(dyh, xdt) * lmat
                w_t = _dot_tn(xdt, dyh) * lmat_t * cb_t
                dcb = dcb + ml
                dacs = jnp.sum(w_t, axis=0, keepdims=True) - jnp.sum(ml * cb, axis=0, keepdims=True)
                ecol = jnp.exp(row_j)
                dec = jnp.exp(tot_j - row_j)
                dye = dyh * ecol
                yoff = _dot_nt(hp, cm) * ecol
                gmat = _dot_nt(dhn, bm)
                dxdt = _dot(dyh, cb * lmat) + dec * gmat
                s_dec = jnp.sum(xdt * gmat, axis=0, keepdims=True) * dec
                dacs = dacs + jnp.sum(dyh * yoff, axis=0, keepdims=True) - s_dec
                dcd = jnp.sum(jnp.sum(dhn * hp, axis=1, keepdims=True), axis=0, keepdims=True)
                dtot = jnp.sum(s_dec, axis=1, keepdims=True) + jnp.exp(tot_j) * dcd
                dacs_rows.append((dacs, dtot))
                ddtx_rows.append(jnp.sum(dxdt * x, axis=0, keepdims=True))
                dcm = dcm + _dot_tn(dye, hp)
                dbm = dbm + _dot_tn(xdt * dec, dhn)
                dxh = dxdt * dt_j
                if skip_ref is not None:
                    dxh = dxh + skip_ref.at[big][rows, :] * dyh
                if prev_refs is not None:
                    dxh = dxh + prev_refs[0].at[big][rows, :]
                dx_v[rows, :] = dxh
                dh_v[j] = jnp.exp(tot_j) * dhn + _dot(dye, cm)
            dcm = dcm + _dot(dcb, bm)
            dbm = dbm + _dot_tn(dcb, cm)
            dbt, dct = dbm.T, dcm.T
            if prev_refs is not None:
                dbt = dbt + prev_refs[1][st, :]
                dct = dct + prev_refs[2][st, :]
            db_ref[st, :] = dbt
            dc_ref[st, :] = dct
            dacs8 = jnp.concatenate([d for d, _ in dacs_rows], axis=0)
            dtot8 = jnp.concatenate([d for _, d in dacs_rows], axis=0)
            da8 = _dot_exact(dacs8, tri) + dtot8
            ddt_ref[heads, :] = da8 * a_blk + jnp.concatenate(ddtx_rows, axis=0)
            da_ref[heads, :] += da8 * dt_blk
            return carry

        lax.fori_loop(0, SSD_GROUPS, group, 0)

    big = pl.BlockSpec((D_INNER, CHUNK), lambda c: (0, cidx(c)))
    st = pl.BlockSpec((512, CHUNK), lambda c: (0, cidx(c)))
    in_specs = [big, pl.BlockSpec((512, CHUNK), lambda c: (4, cidx(c))), pl.BlockSpec((512, CHUNK), lambda c: (5, cidx(c))),
                pl.BlockSpec((SSD_HEADS, CHUNK), lambda c: (direction, cidx(c))),
                pl.BlockSpec((SSD_HEADS, 128), lambda c: (direction, 0)), big,
                pl.BlockSpec((1, SSD_HEADS, SSD_HEAD_DIM, D_STATE), lambda c: (cidx(c), 0, 0, 0))]
    args = [xbc_ct, xbc_ct, xbc_ct, dt_t, a_b, dy_t, hprev]
    if skip_b is not None:
        in_specs.append(pl.BlockSpec((D_INNER, 128), lambda c: (0, 0)))
        args.append(skip_b)
    if prev is not None:
        in_specs += [big, st, st]
        args += list(prev)
    return pl.pallas_call(
        body, name=name, grid=(nc,), in_specs=in_specs,
        out_specs=[big, st, st, pl.BlockSpec((SSD_HEADS, CHUNK), lambda c: (0, cidx(c))),
                   pl.BlockSpec((SSD_HEADS, 128), lambda c: (0, 0))],
        out_shape=[jax.ShapeDtypeStruct((D_INNER, t), f32), jax.ShapeDtypeStruct((512, t), f32),
                   jax.ShapeDtypeStruct((512, t), f32), jax.ShapeDtypeStruct((SSD_HEADS, t), f32),
                   jax.ShapeDtypeStruct((SSD_HEADS, 128), f32)],
        scratch_shapes=[pltpu.VMEM((SSD_HEADS, SSD_HEAD_DIM, D_STATE), f32)],
        compiler_params=_cparams(("arbitrary",)))(*args)


def tail_fwd_t(y_scan, xbc_ct, z_t, skip_b, nw_b, tb=512):
    t = y_scan.shape[1]
    reps = tb // 128

    def body(ys_ref, x_ref, z_ref, d_ref, w_ref, o_ref):
        zz = z_ref[...]
        y = (ys_ref[...] + _lanes(d_ref[...], reps) * x_ref[...]) * (zz * _sigmoid(zz))
        rstd = lax.rsqrt(jnp.mean(y * y, axis=0, keepdims=True) + NORM_EPS)
        o_ref[...] = (y * rstd * _lanes(w_ref[...], reps)).astype(o_ref.dtype)

    blk = pl.BlockSpec((512, tb), lambda g, i: (g, i))
    par = pl.BlockSpec((512, 128), lambda g, i: (g, 0))
    return pl.pallas_call(
        body, name="tail_fwd", grid=(SSD_GROUPS, t // tb), in_specs=[blk, blk, blk, par, par], out_specs=blk,
        out_shape=jax.ShapeDtypeStruct((D_INNER, t), bf16),
        compiler_params=_cparams(("parallel", "parallel")))(y_scan, xbc_ct, z_t, skip_b, nw_b)


def tail_bwd_t(dyn_t, y_scan, xbc_ct, z_t, skip_b, nw_b, tb=512):
    t = y_scan.shape[1]
    reps = tb // 128

    def body(g_ref, ys_ref, x_ref, z_ref, d_ref, w_ref, dy_ref, dz_ref, dw_ref, dd_ref):
        zz = z_ref[...]
        sg = _sigmoid(zz)
        sl = zz * sg
        x = x_ref[...]
        y = ys_ref[...] + _lanes(d_ref[...], reps) * x
        yz = y * sl
        rstd = lax.rsqrt(jnp.mean(yz * yz, axis=0, keepdims=True) + NORM_EPS)
        yhat = yz * rstd
        g = g_ref[...]
        dyhat = g * _lanes(w_ref[...], reps)
        dyz = rstd * (dyhat - yhat * jnp.mean(dyhat * yhat, axis=0, keepdims=True))
        dy = dyz * sl
        dy_ref[...] = dy
        dz_ref[...] = (dyz * y * sg * (1.0 + zz * (1.0 - sg))).astype(dz_ref.dtype)

        def fold(v):
            s = v[:, 0:128]
            for q in range(1, reps):
                s = s + v[:, 128 * q:128 * (q + 1)]
            return s

        @pl.when(pl.program_id(1) == 0)
        def _():
            dw_ref[...] = jnp.zeros_like(dw_ref)
            dd_ref[...] = jnp.zeros_like(dd_ref)

        dw_ref[...] += fold(g * yhat)
        dd_ref[...] += fold(dy * x)

    blk = pl.BlockSpec((512, tb), lambda g, i: (g, i))
    par = pl.BlockSpec((512, 128), lambda g, i: (g, 0))
    return pl.pallas_call(
        body, name="tail_bwd", grid=(SSD_GROUPS, t // tb), in_specs=[blk, blk, blk, blk, par, par],
        out_specs=[blk, blk, par, par],
        out_shape=[jax.ShapeDtypeStruct((D_INNER, t), f32), jax.ShapeDtypeStruct((D_INNER, t), bf16),
                   jax.ShapeDtypeStruct((D_INNER, 128), f32), jax.ShapeDtypeStruct((D_INNER, 128), f32)],
        compiler_params=_cparams(("parallel", "arbitrary")))(dyn_t, y_scan, xbc_ct, z_t, skip_b, nw_b)


def merge_fwd(u_gate, bg_row, y_ssd, y_att, tb=512):
    t = y_ssd.shape[0]

    def body(ga_ref, gb_ref, ba_ref, bb_ref, ys_ref, ya_ref, o_ref):
        o_ref[...] = (_sigmoid(ga_ref[...] + ba_ref[...]) * ys_ref[...]
                      + _sigmoid(gb_ref[...] + bb_ref[...]) * ya_ref[...]).astype(o_ref.dtype)

    blk = pl.BlockSpec((tb, 512), lambda i, j: (i, j))
    blk2 = pl.BlockSpec((tb, 512), lambda i, j: (i, 2 + j))
    row = pl.BlockSpec((1, 512), lambda i, j: (0, j))
    row2 = pl.BlockSpec((1, 512), lambda i, j: (0, 2 + j))
    return pl.pallas_call(
        body, name="merge_fwd", grid=(t // tb, 2), in_specs=[blk, blk2, row, row2, blk, blk], out_specs=blk,
        out_shape=jax.ShapeDtypeStruct((t, D_MODEL), bf16),
        compiler_params=_cparams(("parallel", "parallel")))(u_gate, u_gate, bg_row, bg_row, y_ssd, y_att)


def merge_bwd(dm, u_gate, bg_row, y_ssd, y_att, tb=512):
    t = dm.shape[0]

    def body(dm_ref, ga_ref, gb_ref, ba_ref, bb_ref, ys_ref, ya_ref, dys_ref, dya_ref, dga_ref, dgb_ref, dba_ref, dbb_ref):
        d = dm_ref[...]
        sa = _sigmoid(ga_ref[...] + ba_ref[...])
        sb = _sigmoid(gb_ref[...] + bb_ref[...])
        dys_ref[...] = (d * sa).astype(dys_ref.dtype)
        dya_ref[...] = (d * sb).astype(dya_ref.dtype)
        dla = d * ys_ref[...] * sa * (1.0 - sa)
        dlb = d * ya_ref[...] * sb * (1.0 - sb)
        dga_ref[...] = dla.astype(dga_ref.dtype)
        dgb_ref[...] = dlb.astype(dgb_ref.dtype)

        @pl.when(pl.program_id(1) == 0)
        def _():
            dba_ref[...] = jnp.zeros_like(dba_ref)
            dbb_ref[...] = jnp.zeros_like(dbb_ref)

        dba_ref[...] += jnp.sum(dla, axis=0, keepdims=True)
        dbb_ref[...] += jnp.sum(dlb, axis=0, keepdims=True)

    blk = pl.BlockSpec((tb, 512), lambda j, i: (i, j))
    blk2 = pl.BlockSpec((tb, 512), lambda j, i: (i, 2 + j))
    row = pl.BlockSpec((1, 512), lambda j, i: (0, j))
    row2 = pl.BlockSpec((1, 512), lambda j, i: (0, 2 + j))
    act = jax.ShapeDtypeStruct((t, D_MODEL), bf16)
    vec = jax.ShapeDtypeStruct((1, D_MODEL), f32)
    return pl.pallas_call(
        body, name="merge_bwd", grid=(2, t // tb), in_specs=[blk, blk, blk2, row, row2, blk, blk],
        out_specs=[blk, blk, blk, blk, row, row], out_shape=[act, act, act, act, vec, vec],
        compiler_params=_cparams(("parallel", "arbitrary")))(dm, u_gate, u_gate, bg_row, bg_row, y_ssd, y_att)


def _ln_stats(r):
    mu = jnp.mean(r, axis=1, keepdims=True)
    xc = r - mu
    rstd = lax.rsqrt(jnp.mean(xc * xc, axis=1, keepdims=True) + NORM_EPS)
    return xc * rstd, rstd


def _ln_bwd(dy, xhat, rstd, g_row):
    dxh = dy * g_row
    return rstd * (dxh - jnp.mean(dxh, axis=1, keepdims=True) - xhat * jnp.mean(dxh * xhat, axis=1, keepdims=True))


def ln1_fwd(x, mix, g_row, b_row, tb=512):
    t = x.shape[0]

    def body(x_ref, m_ref, g_ref, b_ref, o_ref, ob_ref):
        xhat, _ = _ln_stats(ALPHA * x_ref[...] + m_ref[...])
        h = xhat * g_ref[...] + b_ref[...]
        o_ref[...] = h
        ob_ref[...] = h.astype(ob_ref.dtype)

    blk = pl.BlockSpec((tb, D_MODEL), lambda i: (i, 0))
    row = pl.BlockSpec((1, D_MODEL), lambda i: (0, 0))
    return pl.pallas_call(body, name="ln1_fwd", grid=(t // tb,), in_specs=[blk, blk, row, row], out_specs=[blk, blk],
                          out_shape=[jax.ShapeDtypeStruct((t, D_MODEL), f32), jax.ShapeDtypeStruct((t, D_MODEL), bf16)],
                          compiler_params=_cparams(("parallel",)))(x, mix, g_row, b_row)


def ln1_bwd(dh, x, mix, g_row, tb=512):
    t = x.shape[0]

    def body(dh_ref, x_ref, m_ref, g_ref, dr_ref, drb_ref, dg_ref, db_ref):
        xhat, rstd = _ln_stats(ALPHA * x_ref[...] + m_ref[...])
        dy = dh_ref[...]
        dr = _ln_bwd(dy, xhat, rstd, g_ref[...])
        dr_ref[...] = dr
        drb_ref[...] = dr.astype(drb_ref.dtype)

        @pl.when(pl.program_id(0) == 0)
        def _():
            dg_ref[...] = jnp.zeros_like(dg_ref)
            db_ref[...] = jnp.zeros_like(db_ref)

        dg_ref[...] += jnp.sum(dy * xhat, axis=0, keepdims=True)
        db_ref[...] += jnp.sum(dy, axis=0, keepdims=True)

    blk = pl.BlockSpec((tb, D_MODEL), lambda i: (i, 0))
    row = pl.BlockSpec((1, D_MODEL), lambda i: (0, 0))
    return pl.pallas_call(
        body, name="ln1_bwd", grid=(t // tb,), in_specs=[blk, blk, blk, row], out_specs=[blk, blk, row, row],
        out_shape=[jax.ShapeDtypeStruct((t, D_MODEL), f32), jax.ShapeDtypeStruct((t, D_MODEL), bf16),
                   jax.ShapeDtypeStruct((1, D_MODEL), f32), jax.ShapeDtypeStruct((1, D_MODEL), f32)],
        compiler_params=_cparams(("arbitrary",)))(dh, x, mix, g_row)


def ln2_loss(h1, f, g_row, b_row, target, tb=512):
    t = h1.shape[0]

    def body(h_ref, f_ref, g_ref, b_ref, t_ref, dr_ref, drb_ref, dg_ref, db_ref, loss_ref):
        xhat, rstd = _ln_stats(ALPHA * h_ref[...] + f_ref[...])
        g = g_ref[...]
        err = xhat * g + b_ref[...] - t_ref[...]
        dy = err * (1.0 / D_MODEL)
        dr = _ln_bwd(dy, xhat, rstd, g)
        dr_ref[...] = dr
        drb_ref[...] = dr.astype(drb_ref.dtype)

        @pl.when(pl.program_id(0) == 0)
        def _():
            dg_ref[...] = jnp.zeros_like(dg_ref)
            db_ref[...] = jnp.zeros_like(db_ref)
            loss_ref[...] = jnp.zeros_like(loss_ref)

        dg_ref[...] += jnp.sum(dy * xhat, axis=0, keepdims=True)
        db_ref[...] += jnp.sum(dy, axis=0, keepdims=True)
        part = jnp.sum(jnp.mean(err * err, axis=1, keepdims=True), axis=0, keepdims=True)
        loss_ref[...] += 0.5 * part

    blk = pl.BlockSpec((tb, D_MODEL), lambda i: (i, 0))
    row = pl.BlockSpec((1, D_MODEL), lambda i: (0, 0))
    return pl.pallas_call(
        body, name="ln2_loss", grid=(t // tb,), in_specs=[blk, blk, row, row, blk],
        out_specs=[blk, blk, row, row, pl.BlockSpec((8, 128), lambda i: (0, 0))],
        out_shape=[jax.ShapeDtypeStruct((t, D_MODEL), f32), jax.ShapeDtypeStruct((t, D_MODEL), bf16),
                   jax.ShapeDtypeStruct((1, D_MODEL), f32), jax.ShapeDtypeStruct((1, D_MODEL), f32),
                   jax.ShapeDtypeStruct((8, 128), f32)],
        compiler_params=_cparams(("arbitrary",)))(h1, f, g_row, b_row, target)


TAIL_BLOCK, TAIL_AT = divmod(OFF_TAIL, PACK_TILE)


def _sum4(ref):
    return ((ref[0].astype(f32) + ref[1].astype(f32)) + ref[2].astype(f32)) + ref[3].astype(f32)


def _adamw_update(g, w_ref, m_ref, v_ref, g_ref, d_ref, nm_ref, nv_ref):
    c1 = 1.0 - ADAM_B1 ** ADAM_STEP
    c2 = 1.0 - ADAM_B2 ** ADAM_STEP
    nm = ADAM_B1 * m_ref[...] + (1.0 - ADAM_B1) * g
    nv = ADAM_B2 * v_ref[...] + (1.0 - ADAM_B2) * (g * g)
    g_ref[...] = g
    nm_ref[...] = nm
    nv_ref[...] = nv
    d_ref[...] = -ADAM_LR * ((nm / c1) / (jnp.sqrt(nv / c2) + ADAM_EPS) + ADAM_WD * w_ref[...])


def adamw_early(landed, parts, me, w, m, v):
    off = LATE_ROWS // EARLY_TILE

    def body(me_ref, *refs):
        src = refs[0:N_DEV]
        own_ref, w_ref, m_ref, v_ref = refs[N_DEV:N_DEV + 4]
        mine = me_ref[0]
        g = None
        for s in range(N_DEV):
            term = jnp.where(mine == s, own_ref[0], src[s][0])
            g = term if g is None else g + term
        _adamw_update(g, w_ref, m_ref, v_ref, *refs[N_DEV + 4:])

    def slot(s):
        return pl.BlockSpec((1, EARLY_TILE, 1024), lambda i, me_ref: (jnp.where(me_ref[0] == s, (s + 1) % N_DEV, s), i, 0))

    shard = pl.BlockSpec((EARLY_TILE, 1024), lambda i, me_ref: (i + off, 0))
    out_blk = pl.BlockSpec((EARLY_TILE, 1024), lambda i, me_ref: (i, 0))
    grid_spec = pltpu.PrefetchScalarGridSpec(
        num_scalar_prefetch=1, grid=(EARLY_ROWS // EARLY_TILE,),
        in_specs=[slot(s) for s in range(N_DEV)]
        + [pl.BlockSpec((1, EARLY_TILE, 1024), lambda i, me_ref: (me_ref[0], i, 0)), shard, shard, shard],
        out_specs=[out_blk] * 4)
    out = jax.ShapeDtypeStruct((EARLY_ROWS, 1024), f32)
    return pl.pallas_call(body, name="adamw_early", grid_spec=grid_spec, out_shape=[out] * 4,
                          compiler_params=_cparams(("parallel",)))(me, *([landed] * N_DEV), parts, w, m, v)


def adamw(parts, tails, w, m, v):
    rows = parts.shape[1]

    def body(p_ref, t_ref, w_ref, m_ref, v_ref, g_ref, d_ref, nm_ref, nv_ref):
        g = _sum4(p_ref)
        with_tail = jnp.concatenate([g[0:TAIL_AT], _sum4(t_ref), g[TAIL_AT + ROWS_TAIL:]], axis=0)
        g = jnp.where(pl.program_id(0) == TAIL_BLOCK, with_tail, g)
        _adamw_update(g, w_ref, m_ref, v_ref, g_ref, d_ref, nm_ref, nv_ref)

    blk = pl.BlockSpec((PACK_TILE, 1024), lambda i: (i, 0))
    out = jax.ShapeDtypeStruct((rows, 1024), f32)
    return pl.pallas_call(
        body, name="adamw", grid=(rows // PACK_TILE,),
        in_specs=[pl.BlockSpec((4, PACK_TILE, 1024), lambda i: (0, i, 0)),
                  pl.BlockSpec((4, ROWS_TAIL, 1024), lambda i: (0, 0, 0)), blk, blk, blk], out_specs=[blk] * 4,
        out_shape=[out] * 4, compiler_params=_cparams(("parallel",)))(parts, tails, w, m, v)


def pair_sum(parts, recv, core):
    rows = parts.shape[1]

    def body(c_ref, a_ref, b_ref, o_ref, t_ref):
        s = a_ref[...] + b_ref[...]
        o_ref[...] = s.astype(o_ref.dtype)

        @pl.when(pl.program_id(1) == TAIL_BLOCK)
        def _():
            t_ref[...] = s[:, TAIL_AT:TAIL_AT + ROWS_TAIL]

    grid_spec = pltpu.PrefetchScalarGridSpec(
        num_scalar_prefetch=1, grid=(4, rows // PACK_TILE),
        in_specs=[pl.BlockSpec((1, PACK_TILE, 1024), lambda j, i, c_ref: (2 * j + c_ref[0], i, 0)),
                  pl.BlockSpec((1, PACK_TILE, 1024), lambda j, i, c_ref: (j, i, 0))],
        out_specs=[pl.BlockSpec((1, PACK_TILE, 1024), lambda j, i, c_ref: (j, i, 0)),
                   pl.BlockSpec((1, ROWS_TAIL, 1024), lambda j, i, c_ref: (j, 0, 0))])
    return pl.pallas_call(
        body, name="pair_sum", grid_spec=grid_spec,
        out_shape=[jax.ShapeDtypeStruct(recv.shape, bf16), jax.ShapeDtypeStruct((4, ROWS_TAIL, 1024), f32)],
        compiler_params=_cparams(("parallel", "arbitrary")))(core, parts, recv)


def _place():
    return lax.axis_index("x"), lax.axis_index("y"), lax.axis_index("c")


def all_gather_blocks(shard):
    rows, cols = shard.shape

    def body(x_ref, out_ref, send_sems, recv_sems, local_sem):
        x, y, c = _place()
        me, sibling = (x, y, c), (x, y, 1 - c)
        chips = [(1 - x, y), (x, 1 - y), (1 - x, 1 - y)]

        def slot(px, py, pc):
            return out_ref.at[4 * px + 2 * py + pc]

        def copy(k, block, to, src=None):
            return pltpu.make_async_remote_copy(
                src_ref=slot(*block) if src is None else src, dst_ref=slot(*block), send_sem=send_sems.at[k],
                recv_sem=recv_sems.at[k], device_id=to, device_id_type=MESH)

        mine = pltpu.make_async_copy(x_ref, slot(*me), local_sem)
        mine.start()
        first = [copy(0, me, sibling, src=x_ref)]
        first += [copy(1 + j, me, (*chip, c), src=x_ref) for j, chip in enumerate(chips)]
        for cp in first:
            cp.start()
        passed = [copy(4 + j, (*chip, c), sibling) for j, chip in enumerate(chips)]
        for j, chip in enumerate(chips):
            copy(1 + j, (*chip, c), me).wait_recv()
            passed[j].start()
        copy(0, sibling, me).wait_recv()
        for j, chip in enumerate(chips):
            copy(4 + j, (*chip, 1 - c), me).wait_recv()
        for cp in first + passed:
            cp.wait_send()
        mine.wait()

    return pl.pallas_call(
        body, name="all_gather_blocks", out_shape=jax.ShapeDtypeStruct((N_DEV, rows, cols), shard.dtype),
        in_specs=[pl.BlockSpec(memory_space=pl.ANY)], out_specs=pl.BlockSpec(memory_space=pl.ANY),
        scratch_shapes=[pltpu.SemaphoreType.DMA((7,)), pltpu.SemaphoreType.DMA((7,)), pltpu.SemaphoreType.DMA],
        compiler_params=pltpu.CompilerParams(has_side_effects=True))(shard)


def pair_exchange(parts):
    _, rows, cols = parts.shape

    def body(p_ref, recv_ref, send_sems, recv_sems):
        x, y, c = _place()
        copies = [pltpu.make_async_remote_copy(
            src_ref=p_ref.at[2 * j + 1 - c], dst_ref=recv_ref.at[j], send_sem=send_sems.at[j], recv_sem=recv_sems.at[j],
            device_id=(x, y, 1 - c), device_id_type=MESH) for j in range(4)]
        for cp in copies:
            cp.start()
        for cp in copies:
            cp.wait_recv()
        for cp in copies:
            cp.wait_send()

    return pl.pallas_call(
        body, name="pair_exchange", out_shape=jax.ShapeDtypeStruct((4, rows, cols), parts.dtype),
        in_specs=[pl.BlockSpec(memory_space=pl.ANY)], out_specs=pl.BlockSpec(memory_space=pl.ANY),
        scratch_shapes=[pltpu.SemaphoreType.DMA((4,)), pltpu.SemaphoreType.DMA((4,))],
        compiler_params=pltpu.CompilerParams(has_side_effects=True))(parts)


def chip_exchange(parts):
    n = len(parts)

    def body(*refs):
        p_refs, out_refs = refs[0:n], refs[n:2 * n]
        send_sems, recv_sems, local_sems = refs[2 * n:]
        x, y, c = _place()
        mine = 2 * x + y
        flips = [(x, 1 - y), (1 - x, y), (1 - x, 1 - y)]

        def copy(a, k, src_slot, dst_slot):
            px, py = flips[k]
            return pltpu.make_async_remote_copy(
                src_ref=p_refs[a].at[src_slot], dst_ref=out_refs[a].at[dst_slot], send_sem=send_sems.at[3 * a + k],
                recv_sem=recv_sems.at[3 * a + k], device_id=(px, py, c), device_id_type=MESH)

        local = [pltpu.make_async_copy(p_refs[a].at[mine], out_refs[a].at[mine], local_sems.at[a]) for a in range(n)]
        sends = [copy(a, k, 2 * flips[k][0] + flips[k][1], mine) for a in range(n) for k in range(3)]
        for cp in local + sends:
            cp.start()
        for a in range(n):
            for k in range(3):
                copy(a, k, mine, 2 * flips[k][0] + flips[k][1]).wait_recv()
        for cp in sends:
            cp.wait_send()
        for cp in local:
            cp.wait()

    return pl.pallas_call(
        body, name="chip_exchange", out_shape=[jax.ShapeDtypeStruct(p.shape, p.dtype) for p in parts],
        in_specs=[pl.BlockSpec(memory_space=pl.ANY)] * n, out_specs=[pl.BlockSpec(memory_space=pl.ANY)] * n,
        scratch_shapes=[pltpu.SemaphoreType.DMA((3 * n,)), pltpu.SemaphoreType.DMA((3 * n,)), pltpu.SemaphoreType.DMA((n,))],
        compiler_params=pltpu.CompilerParams(has_side_effects=True))(*parts)


_HBM = pl.BlockSpec(memory_space=pltpu.HBM)
_SEM = pl.BlockSpec(memory_space=pltpu.SEMAPHORE)


def _peer(k):
    x, y, c = _place()
    px, py, pc = (1 - x if k & 4 else x), (1 - y if k & 2 else y), (1 - c if k & 1 else c)
    return (px, py, pc), 4 * px + 2 * py + pc


def scatter_start(parts, name):
    per_device = parts.ndim == 3

    def body(p_ref, land_ref, send_sems, recv_sems, p_thru, land_thru, token):
        x, y, c = _place()
        me = 4 * x + 2 * y + c
        for k in range(1, N_DEV):
            place, idx = _peer(k)
            pltpu.make_async_remote_copy(src_ref=p_ref.at[idx] if per_device else p_ref, dst_ref=land_ref.at[me],
                                         send_sem=send_sems.at[k - 1], recv_sem=recv_sems.at[k - 1], device_id=place,
                                         device_id_type=MESH).start()
        token[...] = jnp.zeros_like(token)

    land_shape = parts.shape if per_device else (N_DEV,) + parts.shape
    landing = lax.empty(land_shape, parts.dtype)
    return pl.pallas_call(
        body, name=name,
        out_shape=(pltpu.SemaphoreType.DMA((N_DEV - 1,)), pltpu.SemaphoreType.DMA((N_DEV - 1,)),
                   pltpu.HBM(parts.shape, parts.dtype), pltpu.HBM(land_shape, parts.dtype),
                   jax.ShapeDtypeStruct((8, 128), f32)),
        in_specs=(_HBM, _HBM), out_specs=(_SEM, _SEM, _HBM, _HBM, pl.BlockSpec(memory_space=pltpu.VMEM)),
        input_output_aliases={0: 2, 1: 3},
        compiler_params=pltpu.CompilerParams(has_side_effects=pltpu.SideEffectType.DATAFLOW_SIDE_EFFECTING),
    )(pltpu.with_memory_space_constraint(parts, pltpu.HBM), pltpu.with_memory_space_constraint(landing, pltpu.HBM))


def scatter_wait(send_sems, recv_sems, parts_thru, land_thru, after, name):
    per_device = parts_thru.ndim == 3

    def body(p_ref, land_ref, send_sems, recv_sems, after_ref, p_out, land_out):
        for k in range(1, N_DEV):
            place, idx = _peer(k)
            copy = pltpu.make_async_remote_copy(src_ref=p_ref.at[idx] if per_device else p_ref, dst_ref=land_ref.at[idx],
                                                send_sem=send_sems.at[k - 1], recv_sem=recv_sems.at[k - 1],
                                                device_id=place, device_id_type=MESH)
            copy.wait_send()
            copy.wait_recv()

    return pl.pallas_call(
        body, name=name,
        out_shape=(pltpu.HBM(parts_thru.shape, parts_thru.dtype), pltpu.HBM(land_thru.shape, land_thru.dtype)),
        in_specs=(_HBM, _HBM, _SEM, _SEM, pl.BlockSpec(memory_space=pl.ANY)), out_specs=(_HBM, _HBM),
        input_output_aliases={0: 0, 1: 1},
        compiler_params=pltpu.CompilerParams(has_side_effects=pltpu.SideEffectType.DATAFLOW_SIDE_EFFECTING),
    )(parts_thru, land_thru, send_sems, recv_sems, after)


def _tail_rows(conv_part, small, extra):
    lead = conv_part.shape[:-1]
    rep = jnp.concatenate([small[n].reshape(-1).astype(f32) for n in SMALL] + [extra.reshape(1).astype(f32)])
    flat = jnp.concatenate([conv_part, jnp.broadcast_to(rep, lead + rep.shape),
                            jnp.zeros(lead + (ROWS_TAIL * 1024 - TAIL_ELEMS,), f32)], axis=-1)
    return flat.reshape(lead + (ROWS_TAIL, 1024))


def _late_rows(w_in_t, tail):
    lead = tail.shape[:-2]
    zeros = lambda r: jnp.zeros(lead + (r, 1024), f32)
    return jnp.concatenate([w_in_t, zeros(OFF_TAIL - IN_SHARD), tail, zeros(LATE_ROWS - OFF_TAIL - ROWS_TAIL)], axis=-2)


def _early_rows(w_ps, w_out, w_up_t, w_down, w_pa_t):
    return jnp.concatenate([w_ps, w_out, w_up_t, w_down, w_pa_t.reshape(w_pa_t.shape[:-2] + (ROWS_PA, 1024))], axis=-2)


def _pack_shard(vals):
    tail = _tail_rows(vals["conv_w"].reshape(-1), vals, jnp.zeros((), f32))
    return jnp.concatenate([_late_rows(vals["w_in"].T, tail),
                            _early_rows(vals["w_proj_ssd"], vals["w_out"], vals["w_up"].T, vals["w_down"],
                                        vals["w_proj_attn"].T)], axis=0)


def _unpack_shard(late, early):
    e = lambda lo, hi: early[lo - LATE_ROWS:hi - LATE_ROWS]
    out = {"w_in": late[0:IN_SHARD].T, "w_proj_ssd": e(OFF_PS, OFF_OUT), "w_out": e(OFF_OUT, OFF_UP),
           "w_up": e(OFF_UP, OFF_DOWN).T, "w_down": e(OFF_DOWN, OFF_PA),
           "w_proj_attn": e(OFF_PA, PACK_ROWS).reshape(D_MODEL // N_DEV, ATTN_OUT).T}
    flat = late[OFF_TAIL:OFF_TAIL + ROWS_TAIL].reshape(-1)
    out["conv_w"] = flat[0:CONV_SHARD].reshape(D_CONV, CONV_DIM // N_DEV)
    off = CONV_SHARD
    for n in SMALL:
        out[n] = flat[off:off + SMALL_SIZES[n]]
        off += SMALL_SIZES[n]
    out["_extra"] = flat[off]
    return out


def _blocks(g):
    return g.reshape(N_DEV, g.shape[0] // N_DEV, g.shape[1])


def _pack_early_parts(full):
    return _early_rows(_blocks(full["w_proj_ssd"]), _blocks(full["w_out"]), _blocks(full["w_up_t"]),
                       _blocks(full["w_down"]), _blocks(full["w_proj_attn_t"]))


def _pack_late_parts(full, small, extra):
    conv = full["conv_w"].reshape(D_CONV, N_DEV, CONV_DIM // N_DEV).transpose(1, 0, 2).reshape(N_DEV, CONV_SHARD)
    return _late_rows(_blocks(full["w_in_t"]), _tail_rows(conv, small, extra))


def _gather_weights(w):
    conv_bits = lax.bitcast_convert_type(w["conv_w"], bf16).reshape(-1)
    conv_rows = jnp.concatenate([conv_bits, jnp.zeros((16 * 1024 - 2 * CONV_SHARD,), bf16)]).reshape(16, 1024)
    packed = _pack_shard(w)
    first = OFF_TAIL + ROWS_TAIL
    got = all_gather_blocks(jnp.concatenate([packed[0:OFF_TAIL].astype(bf16), conv_rows], axis=0))
    got, rest = lax.optimization_barrier((got, packed[first:].astype(bf16)))
    send_sems, recv_sems, rest_thru, land_thru, token = scatter_start(rest, "gather_start")
    conv =lax.bitcast_convert_type(got[:, OFF_TAIL:OFF_TAIL + 4].reshape(N_DEV, 4096)[:, 0:2 * CONV_SHARD]
                                    .reshape(N_DEV, D_CONV, CONV_DIM // N_DEV, 2), f32)
    now = {"w_in_t": got[:, 0:IN_SHARD].reshape(IN_COLS, 1024), "conv_w": conv.transpose(1, 0, 2).reshape(D_CONV, CONV_DIM)}

    def later(after):
        mine, landed = scatter_wait(send_sems, recv_sems, rest_thru, land_thru, after, "gather_wait")
        x, y, c = _place()
        landed = lax.dynamic_update_slice(landed, mine[None], (4 * x + 2 * y + c, 0, 0))
        whole = lambda lo, hi: landed[:, lo - first:hi - first].reshape(N_DEV * (hi - lo), 1024)
        return {"w_proj_ssd": whole(OFF_PS, OFF_OUT), "w_out": whole(OFF_OUT, OFF_UP), "w_up_t": whole(OFF_UP, OFF_DOWN),
                "w_down": whole(OFF_DOWN, OFF_PA),
                "w_proj_attn_t": landed[:, OFF_PA - first:PACK_ROWS - first].reshape(D_MODEL, ATTN_OUT)}

    return now, later, token


def _row(v, width=None):
    v = v.reshape(1, -1).astype(f32)
    return v if width is None else jnp.pad(v, ((0, 0), (0, width - v.shape[1])))


def _lanes256(vf, vb):
    z = jnp.zeros((96,), f32)
    return jnp.concatenate([vf.astype(f32), z, vb.astype(f32), z]).reshape(1, 256)


def _local_step(x2, tgt, wf, p, send_early=None, late_weights=None, start_token=None):
    t = x2.shape[0]
    o = np.cumsum((0,) + IN_SPLITS)
    wt = wf["w_in_t"]
    wt_z, wt_xbc, wt_dt = wt[o[0]:o[1]], wt[o[1]:o[2]], wt[o[2]:o[4]]
    wt_qkv, wt_gate = wt[o[4]:o[7]], wt[o[7]:o[8]]

    spread = lambda v: jnp.broadcast_to(v.astype(f32)[..., None], v.shape + (128,))
    conv_w_b, conv_b_b = spread(wf["conv_w"]), spread(p["conv_b"])
    dt_bias_b = spread(jnp.concatenate([p["dt_bias_f"], p["dt_bias_b"]]))
    a_f, a_b = -jnp.exp(p["a_log_f"].astype(f32)), -jnp.exp(p["a_log_b"].astype(f32))
    a_coef_b = spread(jnp.concatenate([a_f, a_b]))
    skip_b = spread(jnp.repeat(p["d_skip"], SSD_HEAD_DIM))
    nw_b, bg_row = spread(p["ssd_norm_w"]), _row(p["b_gate"])
    g1, b1, g2, b2 = _row(p["ln1_g"]), _row(p["ln1_b"]), _row(p["ln2_g"]), _row(p["ln2_b"])

    xb = (x2 if start_token is None else x2 + start_token[0, 0]).astype(MXU_DTYPE)
    xt = xb.T
    u_z = mm_nn(wt_z, xt, "in_z")
    u_xbc = mm_nn(wt_xbc, xt, "in_xbc")
    u_dt = mm_nn(wt_dt, xt, "in_dt")
    u_qkv = mm_nt(xb, wt_qkv, "in_qkv")
    u_gate = mm_nt(xb, wt_gate, "in_gate")
    xbc_c = conv_fwd_t(u_xbc, conv_w_b, conv_b_b)
    dt_t = dt_fwd_t(u_dt, dt_bias_b)
    y_f, h_f = ssd_fwd_t(xbc_c, dt_t, a_coef_b, False, "ssd_fwd_f")
    y_scan, h_b = ssd_fwd_t(xbc_c, dt_t, a_coef_b, True, "ssd_fwd_b", prev=y_f)
    yn = tail_fwd_t(y_scan, xbc_c, u_z, skip_b, nw_b)
    if late_weights is not None:
        wf = {**wf, **late_weights(yn)}
    y_ssd = mm_tn(yn, wf["w_proj_ssd"], "proj_ssd")

    def strided(a, dil):
        return a.reshape(t // dil, dil * 256)

    qkv, outs, lses = [], [], []
    for pi, (_, dil) in enumerate(DIL_PATTERNS):
        q, k, v = (strided(u_qkv[:, ATTN_WIDTH * s + 256 * pi: ATTN_WIDTH * s + 256 * (pi + 1)], dil) for s in range(3))
        qkv.append((q, k, v))
        op, lp = attn_fwd(q, k, v, pi, dil, f"attn_fwd_{pi}")
        outs.append(op.reshape(t, 256))
        lses.append(lp.reshape(t, 256))
    ya, lse = attn_combine(outs, lses)
    y_att = mm_nt(ya, wf["w_proj_attn_t"], "proj_attn")
    m = merge_fwd(u_gate, bg_row, y_ssd, y_att)
    mix = mm_nn(m, wf["w_out"], "out_proj")
    h1, h1b = ln1_fwd(x2, mix, g1, b1)
    a_up, p_act = mm_nt(h1b, wf["w_up_t"], "mlp_up", relu2=True)
    f_dn = mm_nn(p_act, wf["w_down"], "mlp_down")
    dr2, dr2b, dg2, db2, loss8 = ln2_loss(h1, f_dn, g2, b2, tgt)

    full, small = {}, {}
    da = mm_nt(dr2b, wf["w_down"], "d_mlp_act", out_dtype=bf16, relu2_of=a_up)
    full["w_down"] = mm_tn(p_act, dr2b, "dw_down")
    full["w_up_t"] = mm_tn(da, h1b, "dw_up")
    dh1 = mm_nn(da, wf["w_up_t"], "d_h1", acc_in=dr2, acc_scale=ALPHA)
    dr1, dr1b, dg1, db1 = ln1_bwd(dh1, x2, mix, g1)
    dm = mm_nt(dr1b, wf["w_out"], "d_merge")
    full["w_out"] = mm_tn(m, dr1b, "dw_out")
    dys, dya_p, dga, dgb, dba, dbb = merge_bwd(dm, u_gate, bg_row, y_ssd, y_att)
    dyn = mm_nt(wf["w_proj_ssd"], dys, "d_yn")
    full["w_proj_ssd"] = mm_nn(yn, dys, "dw_proj_ssd")
    dya = mm_nn(dya_p, wf["w_proj_attn_t"], "d_ya")
    full["w_proj_attn_t"] = mm_tn(dya_p, ya, "dw_proj_attn")
    if send_early is not None:
        skip_b = skip_b + send_early(full)[0, 0]

    dy, dz, dnw, ddx = tail_bwd_t(dyn, y_scan, xbc_c, u_z, skip_b, nw_b)
    dxf, dbf, dcf, ddtf, daf = ssd_bwd_t(xbc_c, dt_t, a_coef_b, dy, h_f, False, "ssd_bwd_f", skip_b=skip_b)
    dxs, dbs, dcs, ddtb, dab = ssd_bwd_t(xbc_c, dt_t, a_coef_b, dy, h_b, True, "ssd_bwd_b", prev=(dxf, dbf, dcf))
    du_xbc, dcw_x, dcb_x = conv_bwd_t(u_xbc, dxs, conv_w_b, conv_b_b, None, "conv_bwd_x", 0)
    du_xbc, dcw_b, dcb_b = conv_bwd_t(u_xbc, dbs, conv_w_b, conv_b_b, du_xbc, "conv_bwd_b", D_INNER)
    du_xbc, dcw_c, dcb_c = conv_bwd_t(u_xbc, dcs, conv_w_b, conv_b_b, du_xbc, "conv_bwd_c", D_INNER + 512)
    du_dt, dbias = dt_bwd_t(ddtf, ddtb, u_dt, dt_bias_b)

    delta = attn_delta(dya, ya)
    dqs, dks, dvs = [], [], []
    for pi, (_, dil) in enumerate(DIL_PATTERNS):
        q, k, v = qkv[pi]
        sd, sl_, sdel = strided(dya, dil), strided(lse, dil), strided(delta, dil)
        dqs.append(attn_dq(q, k, v, sd, sl_, sdel, pi, dil, f"attn_dq_{pi}").reshape(t, 256))
        dk, dv = attn_dkv(q, k, v, sd, sl_, sdel, pi, dil, f"attn_dkv_{pi}")
        dks.append(dk.reshape(t, 256))
        dvs.append(dv.reshape(t, 256))
    du_qkv = jnp.concatenate(dqs + dks + dvs, axis=1)
    du_gate = jnp.concatenate([dga, dgb], axis=1)

    dx = mm_tn(dz, wt_z, "dx_z", acc_in=dr1, acc_scale=ALPHA)
    dx = mm_tn(du_xbc, wt_xbc, "dx_xbc", acc_in=dx)
    dx = mm_tn(du_dt, wt_dt, "dx_dt", acc_in=dx)
    dx = mm_nn(du_qkv, wt_qkv, "dx_qkv", acc_in=dx)
    dx = mm_nn(du_gate, wt_gate, "dx_gate", acc_in=dx)
    full["w_in_t"] = jnp.concatenate(
        [mm_nn(dz, xb, "dw_in_z"), mm_nn(du_xbc, xb, "dw_in_xbc"), mm_nn(du_dt, xb, "dw_in_dt"),
         mm_tn(du_qkv, xb, "dw_in_qkv"), mm_tn(du_gate, xb, "dw_in_gate")], axis=0)
    lanes = lambda v: jnp.sum(v, axis=-1)
    full["conv_w"] = jnp.concatenate([lanes(dcw_x), lanes(dcw_b), lanes(dcw_c)], axis=1)

    small["b_gate"] = jnp.concatenate([dba, dbb], axis=1)
    small["conv_b"] = jnp.concatenate([lanes(dcb_x), lanes(dcb_b), lanes(dcb_c)])
    dbias = lanes(dbias)
    small["dt_bias_f"], small["dt_bias_b"] = dbias[0:32], dbias[32:64]
    small["a_log_f"] = lanes(daf) * a_f
    small["a_log_b"] = lanes(dab) * a_b
    small["d_skip"] = jnp.sum(lanes(ddx).reshape(SSD_HEADS, SSD_HEAD_DIM), axis=1)
    small["ssd_norm_w"] = lanes(dnw)
    small["ln1_g"], small["ln1_b"], small["ln2_g"], small["ln2_b"] = dg1, db1, dg2, db2
    return loss8[0, 0], dx, full, small


def kernel(x, w_in, b_gate, conv_w, conv_b, dt_bias_f, dt_bias_b, a_log_f, a_log_b, d_skip, ssd_norm_w, w_proj_ssd, w_proj_attn, w_out, ln1_g, ln1_b, w_up, w_down, ln2_g, ln2_b, loss_target, m_w_in, m_b_gate, m_conv_w, m_conv_b, m_dt_bias_f, m_dt_bias_b, m_a_log_f, m_a_log_b, m_d_skip, m_ssd_norm_w, m_w_proj_ssd, m_w_proj_attn, m_w_out, m_ln1_g, m_ln1_b, m_w_up, m_w_down, m_ln2_g, m_ln2_b, v_w_in, v_b_gate, v_conv_w, v_conv_b, v_dt_bias_f, v_dt_bias_b, v_a_log_f, v_a_log_b, v_d_skip, v_ssd_norm_w, v_w_proj_ssd, v_w_proj_attn, v_w_out, v_ln1_g, v_ln1_b, v_w_up, v_w_down, v_ln2_g, v_ln2_b):
    given = dict(locals())
    w = {n: given[n] for n in WEIGHTS}
    mom = {n: given["m_" + n] for n in WEIGHTS}
    var = {n: given["v_" + n] for n in WEIGHTS}
    t = x.shape[1]
    wf, late_weights, start_token = _gather_weights(w)
    in_flight = []

    def send_early(full):
        send_sems, recv_sems, parts_thru, land_thru, token = scatter_start(_pack_early_parts(full), "scatter_start")
        in_flight.append((send_sems, recv_sems, parts_thru, land_thru))
        return token

    loss, dx, full, small = _local_step(x.reshape(t, D_MODEL), loss_target.reshape(t, D_MODEL), wf, w, send_early,
                                        late_weights, start_token)
    late = _pack_late_parts(full, small, loss)
    x_, y_, c_ = _place()
    core = c_.astype(jnp.int32).reshape(1)
    me = (4 * x_ + 2 * y_ + c_).astype(jnp.int32).reshape(1)
    wp, mp, vp = _pack_shard(w), _pack_shard(mom), _pack_shard(var)
    early_parts, landed = scatter_wait(*in_flight[0], late, "scatter_wait")
    early_out = adamw_early(landed, early_parts, me, wp, mp, vp)
    parts, tails = chip_exchange(pair_sum(late, pair_exchange(late), core))
    late_out = adamw(parts, tails, wp, mp, vp)
    g, delta, new_m, new_v = (_unpack_shard(a, b) for a, b in zip(late_out, early_out))
    outs = [g["_extra"], dx.reshape(x.shape)]
    for d in (g, delta, new_m, new_v):
        outs += [d[n].reshape(w[n].shape) for n in WEIGHTS]
    return tuple(outs)
```

```python
import functools
import math

import jax
import jax.numpy as jnp
import numpy as np
from jax import lax
from jax.experimental import pallas as pl
from jax.experimental.pallas import tpu as pltpu

f32 = jnp.float32
bf16 = jnp.bfloat16
MXU_DTYPE = jnp.bfloat16

N_DEV = 8
D_MODEL = 1024
D_INNER = 2048
SSD_HEADS = 32
SSD_HEAD_DIM = 64
SSD_GROUPS = 4
D_STATE = 128
D_CONV = 5
CHUNK = 128
CONV_DIM = D_INNER + 2 * SSD_GROUPS * D_STATE
NORM_EPS = 1e-5
ATTN_HEAD_DIM = 64
DIL_PATTERNS = ((128, 1), (512, 4), (2048, 16))
N_PATTERNS = len(DIL_PATTERNS)
HEADS_PER_PATTERN = 4
ATTN_HEADS = 12
ATTN_WIDTH = 768
ATTN_OUT = 256
D_FF = 4096
ALPHA = 2.0 ** 0.25
IN_SPLITS = (D_INNER, CONV_DIM, SSD_HEADS, SSD_HEADS, ATTN_WIDTH, ATTN_WIDTH, ATTN_WIDTH, 2 * D_MODEL)
IN_COLS = sum(IN_SPLITS)
ADAM_LR, ADAM_B1, ADAM_B2, ADAM_EPS, ADAM_WD, ADAM_STEP = 0.001, 0.9, 0.999, 1e-08, 0.01, 10
NEG_BIG = -1e30
VMEM_LIMIT = 56 * 1024 * 1024
MESH = pl.DeviceIdType.MESH

SMALL = ("b_gate", "conv_b", "dt_bias_f", "dt_bias_b", "a_log_f", "a_log_b", "d_skip", "ssd_norm_w",
         "ln1_g", "ln1_b", "ln2_g", "ln2_b")
WEIGHTS = ("w_in", "b_gate", "conv_w", "conv_b", "dt_bias_f", "dt_bias_b", "a_log_f", "a_log_b", "d_skip",
           "ssd_norm_w", "w_proj_ssd", "w_proj_attn", "w_out", "ln1_g", "ln1_b", "w_up", "w_down", "ln2_g", "ln2_b")
SMALL_SIZES = {"b_gate": 2 * D_MODEL, "conv_b": CONV_DIM, "dt_bias_f": 32, "dt_bias_b": 32, "a_log_f": 32, "a_log_b": 32,
               "d_skip": 32, "ssd_norm_w": D_INNER, "ln1_g": D_MODEL, "ln1_b": D_MODEL, "ln2_g": D_MODEL, "ln2_b": D_MODEL}
IN_SHARD = IN_COLS // N_DEV
OFF_TAIL = 1200
ROWS_TAIL = 16
PACK_TILE = 128
LATE_ROWS = 1280
ROWS_PS, ROWS_OUT, ROWS_UP, ROWS_DOWN, ROWS_PA = D_INNER // N_DEV, D_MODEL // N_DEV, D_FF // N_DEV, D_FF // N_DEV, 32
OFF_PS = LATE_ROWS
OFF_OUT = OFF_PS + ROWS_PS
OFF_UP = OFF_OUT + ROWS_OUT
OFF_DOWN = OFF_UP + ROWS_UP
OFF_PA = OFF_DOWN + ROWS_DOWN
PACK_ROWS = OFF_PA + ROWS_PA
EARLY_ROWS = PACK_ROWS - LATE_ROWS
EARLY_TILE = 160
CONV_SHARD = D_CONV * CONV_DIM // N_DEV
TAIL_ELEMS = CONV_SHARD + sum(SMALL_SIZES.values()) + 1


def _cparams(sem=None, **kw):
    return pltpu.CompilerParams(dimension_semantics=sem, vmem_limit_bytes=VMEM_LIMIT, **kw)


def _mx(v):
    return v.astype(MXU_DTYPE)


def _dot(a, b):
    return jnp.dot(_mx(a), _mx(b), preferred_element_type=f32)


def _dot_nt(a, b):
    return lax.dot_general(_mx(a), _mx(b), (((1,), (1,)), ((), ())), preferred_element_type=f32)


def _dot_tn(a, b):
    return lax.dot_general(_mx(a), _mx(b), (((0,), (0,)), ((), ())), preferred_element_type=f32)


def _dot_exact(a, b):
    return jnp.dot(a, b, precision=lax.Precision.HIGHEST, preferred_element_type=f32)


def _sigmoid(v):
    return 1.0 / (1.0 + jnp.exp(-v))


def _pick(n, prefs):
    for p in prefs:
        if n % p == 0:
            return p
    return n


MM_TILE = 1024


def mm_nn(a, b, name, out_dtype=f32, acc_in=None, acc_scale=1.0):
    m, k = a.shape
    n = b.shape[1]
    tm = _pick(m, (MM_TILE, 512, 256, 128, 64))
    tn = _pick(n, (MM_TILE, 512, 256, 128))
    tk = _pick(k, (2048, 1536, 1152, 1024, 768, 512, 256, 128))
    nk = k // tk

    def body(*refs):
        a_ref, b_ref = refs[0:2]
        c_ref = refs[2] if acc_in is not None else None
        o_ref = refs[3] if acc_in is not None else refs[2]

        def finish(r):
            if acc_in is not None:
                r = r + acc_scale * c_ref[...]
            o_ref[...] = r.astype(o_ref.dtype)

        if nk == 1:
            finish(_dot(a_ref[...], b_ref[...]))
            return
        acc_ref = refs[-1]
        kk = pl.program_id(2)

        @pl.when(kk == 0)
        def _():
            acc_ref[...] = jnp.zeros_like(acc_ref)

        acc_ref[...] += _dot(a_ref[...], b_ref[...])

        @pl.when(kk == nk - 1)
        def _():
            finish(acc_ref[...])

    in_specs = [pl.BlockSpec((tm, tk), lambda i, j, kk: (i, kk)), pl.BlockSpec((tk, tn), lambda i, j, kk: (kk, j))]
    args = [a, b]
    if acc_in is not None:
        in_specs.append(pl.BlockSpec((tm, tn), lambda i, j, kk: (i, j)))
        args.append(acc_in)
    return pl.pallas_call(
        body, name=name, grid=(m // tm, n // tn, nk), in_specs=in_specs,
        out_specs=pl.BlockSpec((tm, tn), lambda i, j, kk: (i, j)),
        out_shape=jax.ShapeDtypeStruct((m, n), out_dtype),
        scratch_shapes=[pltpu.VMEM((tm, tn), f32)] if nk > 1 else [],
        compiler_params=_cparams(("parallel", "parallel", "arbitrary")))(*args)


def mm_nt(a, b, name, out_dtype=f32, relu2=None, relu2_of=None):
    m, k = a.shape
    n = b.shape[0]
    tm = MM_TILE
    tn = _pick(n, (MM_TILE, 768, 512, 256, 128))

    def body(*refs):
        r = _dot_nt(refs[0][...], refs[1][...])
        if relu2:
            pos = jnp.maximum(r, 0.0)
            refs[2][...] = pos.astype(refs[2].dtype)
            refs[3][...] = (pos * pos).astype(refs[3].dtype)
        elif relu2_of is not None:
            refs[3][...] = (r * (2.0 * refs[2][...].astype(f32))).astype(refs[3].dtype)
        else:
            refs[2][...] = r.astype(refs[2].dtype)

    blk = pl.BlockSpec((tm, tn), lambda i, j: (i, j))
    in_specs = [pl.BlockSpec((tm, k), lambda i, j: (i, 0)), pl.BlockSpec((tn, k), lambda i, j: (j, 0))]
    args = [a, b]
    if relu2_of is not None:
        in_specs.append(blk)
        args.append(relu2_of)
    if relu2:
        out_specs, out_shape = [blk, blk], [jax.ShapeDtypeStruct((m, n), bf16), jax.ShapeDtypeStruct((m, n), bf16)]
    else:
        out_specs, out_shape = blk, jax.ShapeDtypeStruct((m, n), out_dtype)
    return pl.pallas_call(body, name=name, grid=(m // tm, n // tn), in_specs=in_specs, out_specs=out_specs,
                          out_shape=out_shape, compiler_params=_cparams(("parallel", "parallel")))(*args)


def mm_nt_split(a, b, name, width):
    m, k = a.shape
    n = b.shape[0]
    tm = MM_TILE
    parts = n // width

    def body(a_ref, b_ref, *o_refs):
        r = _dot_nt(a_ref[...], b_ref[...])
        for q in range(parts):
            o_refs[q][...] = r[:, width * q:width * (q + 1)]

    blk = pl.BlockSpec((tm, width), lambda i: (i, 0))
    return pl.pallas_call(
        body, name=name, grid=(m // tm,),
        in_specs=[pl.BlockSpec((tm, k), lambda i: (i, 0)), pl.BlockSpec((n, k), lambda i: (0, 0))],
        out_specs=[blk] * parts, out_shape=[jax.ShapeDtypeStruct((m, width), f32)] * parts,
        compiler_params=_cparams(("parallel",)))(a, b)


def mm_tn(a, b, name, acc_in=None, acc_scale=1.0):
    k, m = a.shape
    n = b.shape[1]
    tm = _pick(m, (MM_TILE, 768, 512, 256, 128))
    tn = _pick(n, (MM_TILE, 512, 256, 128))
    tk = _pick(k, (1024, 768, 512, 256, 128, 64))
    nk = k // tk

    def body(*refs):
        a_ref, b_ref, o_ref = refs[0], refs[1], refs[-1]
        kk = pl.program_id(2)

        @pl.when(kk == 0)
        def _():
            o_ref[...] = jnp.zeros_like(o_ref) if acc_in is None else acc_scale * refs[2][...]

        o_ref[...] += _dot_tn(a_ref[...], b_ref[...])

    in_specs = [pl.BlockSpec((tk, tm), lambda i, j, kk: (kk, i)), pl.BlockSpec((tk, tn), lambda i, j, kk: (kk, j))]
    args = [a, b]
    if acc_in is not None:
        in_specs.append(pl.BlockSpec((tm, tn), lambda i, j, kk: (i, j)))
        args.append(acc_in)
    return pl.pallas_call(
        body, name=name, grid=(m // tm, n // tn, nk), in_specs=in_specs,
        out_specs=pl.BlockSpec((tm, tn), lambda i, j, kk: (i, j)),
        out_shape=jax.ShapeDtypeStruct((m, n), f32),
        compiler_params=_cparams(("parallel", "parallel", "arbitrary")))(*args)


def _halo_specs(tb, cb, nt, off=0):
    r = tb // 8
    return [pl.BlockSpec((8, cb), lambda j, i: (jnp.maximum(i * r - 1, 0), j + off)),
            pl.BlockSpec((tb, cb), lambda j, i: (i, j + off)),
            pl.BlockSpec((8, cb), lambda j, i: (jnp.minimum((i + 1) * r, nt * r - 1), j + off))]


def _with_halo(prev_ref, own_ref, next_ref, i, nt):
    prev = jnp.where(i > 0, prev_ref[...].astype(f32), 0.0)
    nxt = jnp.where(i < nt - 1, next_ref[...].astype(f32), 0.0)
    return jnp.concatenate([prev, own_ref[...].astype(f32), nxt], axis=0)


def _shifted(xcat, s, tb):
    n = xcat.shape[0]
    return pltpu.roll(xcat, (-s) % n, 0)[8:8 + tb]


def conv_fwd(xbc, w8, b_row, tb=512, cb=512):
    t, c = xbc.shape
    nt = t // tb

    def body(prev_ref, own_ref, next_ref, w_ref, b_ref, o_ref):
        i = pl.program_id(1)
        xcat = _with_halo(prev_ref, own_ref, next_ref, i, nt)
        w = w_ref[...]
        pre = b_ref[...] + w[0:1] * _shifted(xcat, -2, tb)
        for k in range(1, D_CONV):
            pre = pre + w[k:k + 1] * _shifted(xcat, k - 2, tb)
        o_ref[...] = pre * _sigmoid(pre)

    return pl.pallas_call(
        body, name="conv_fwd", grid=(c // cb, nt),
        in_specs=_halo_specs(tb, cb, nt) + [pl.BlockSpec((8, cb), lambda j, i: (0, j)), pl.BlockSpec((1, cb), lambda j, i: (0, j))],
        out_specs=pl.BlockSpec((tb, cb), lambda j, i: (i, j)), out_shape=jax.ShapeDtypeStruct((t, c), f32),
        compiler_params=_cparams(("parallel", "parallel")))(xbc, xbc, xbc, w8, b_row)


def conv_bwd(xbc, xoff, grads, scales, w8, b_row, name, tb=512, cb=512):
    t, c = grads[0].shape
    nt = t // tb
    ng = len(grads)
    has_scale = [s is not None for s in scales]

    def body(*refs):
        i = pl.program_id(1)
        xr = refs[0:3]
        gr = [refs[3 + 3 * q: 6 + 3 * q] for q in range(ng)]
        pos = 3 + 3 * ng
        sr = []
        for q in range(ng):
            if has_scale[q]:
                sr.append(refs[pos])
                pos += 1
            else:
                sr.append(None)
        w_ref, b_ref, dx_ref, dw_ref, db_ref = refs[pos:pos + 5]
        xcat = _with_halo(*xr, i, nt)
        gcat = None
        for q in range(ng):
            gq = _with_halo(*gr[q], i, nt)
            if sr[q] is not None:
                gq = gq * sr[q][...]
            gcat = gq if gcat is None else gcat + gq
        w = w_ref[...]
        n = tb + 16
        pre = b_ref[...] + w[0:1] * pltpu.roll(xcat, 2, 0)
        for k in range(1, D_CONV):
            pre = pre + w[k:k + 1] * pltpu.roll(xcat, (2 - k) % n, 0)
        sg = _sigmoid(pre)
        dpre = gcat * sg * (1.0 + pre * (1.0 - sg))
        dx = w[0:1] * _shifted(dpre, 2, tb)
        for k in range(1, D_CONV):
            dx = dx + w[k:k + 1] * _shifted(dpre, 2 - k, tb)
        dx_ref[...] = dx.astype(dx_ref.dtype)
        dp_own = dpre[8:8 + tb]
        rows = [jnp.sum(dp_own * _shifted(xcat, k - 2, tb), axis=0, keepdims=True) for k in range(D_CONV)]
        dw = jnp.concatenate(rows + [jnp.zeros((8 - D_CONV, cb), f32)], axis=0)
        db = jnp.sum(dp_own, axis=0, keepdims=True)

        @pl.when(i == 0)
        def _():
            dw_ref[...] = jnp.zeros_like(dw_ref)
            db_ref[...] = jnp.zeros_like(db_ref)

        dw_ref[...] += dw
        db_ref[...] += db

    in_specs = _halo_specs(tb, cb, nt, xoff)
    args = [xbc] * 3
    for g in grads:
        in_specs += _halo_specs(tb, cb, nt)
        args += [g] * 3
    for s in scales:
        if s is not None:
            in_specs.append(pl.BlockSpec((1, cb), lambda j, i: (0, j)))
            args.append(s)
    in_specs += [pl.BlockSpec((8, cb), lambda j, i: (0, j)), pl.BlockSpec((1, cb), lambda j, i: (0, j))]
    args += [w8, b_row]
    return pl.pallas_call(
        body, name=name, grid=(c // cb, nt), in_specs=in_specs,
        out_specs=[pl.BlockSpec((tb, cb), lambda j, i: (i, j)), pl.BlockSpec((8, cb), lambda j, i: (0, j)),
                   pl.BlockSpec((1, cb), lambda j, i: (0, j))],
        out_shape=[jax.ShapeDtypeStruct((t, c), bf16), jax.ShapeDtypeStruct((8, c), f32), jax.ShapeDtypeStruct((1, c), f32)],
        compiler_params=_cparams(("parallel", "arbitrary")))(*args)


def dt_fwd(u_dt, bias_row, tb=1024):
    t = u_dt.shape[0]

    def body(u_ref, b_ref, o_ref):
        v = u_ref[...] + b_ref[...]
        sp = jnp.maximum(v, 0.0) + jnp.log(1.0 + jnp.exp(-jnp.abs(v)))
        lane = lax.broadcasted_iota(jnp.int32, v.shape, 1)
        o_ref[...] = jnp.where((lane & 127) < SSD_HEADS, sp, 0.0)

    return pl.pallas_call(
        body, name="dt_fwd", grid=(t // tb,),
        in_specs=[pl.BlockSpec((tb, 256), lambda i: (i, 0)), pl.BlockSpec((1, 256), lambda i: (0, 0))],
        out_specs=pl.BlockSpec((tb, 256), lambda i: (i, 0)), out_shape=jax.ShapeDtypeStruct((t, 256), f32),
        compiler_params=_cparams(("parallel",)))(u_dt, bias_row)


def dt_bwd(ddt_f, ddt_b, u_dt, bias_row, tb=1024):
    t = u_dt.shape[0]

    def body(gf_ref, gb_ref, u_ref, b_ref, du_ref, db_ref):
        g = jnp.concatenate([jnp.sum(gf_ref[...], axis=0), jnp.sum(gb_ref[...], axis=0)], axis=1)
        du = g * _sigmoid(u_ref[...] + b_ref[...])
        du_ref[...] = du.astype(du_ref.dtype)

        @pl.when(pl.program_id(0) == 0)
        def _():
            db_ref[...] = jnp.zeros_like(db_ref)

        db_ref[...] += jnp.sum(du, axis=0, keepdims=True)

    return pl.pallas_call(
        body, name="dt_bwd", grid=(t // tb,),
        in_specs=[pl.BlockSpec((4, tb, 128), lambda i: (0, i, 0)), pl.BlockSpec((4, tb, 128), lambda i: (0, i, 0)),
                  pl.BlockSpec((tb, 256), lambda i: (i, 0)), pl.BlockSpec((1, 256), lambda i: (0, 0))],
        out_specs=[pl.BlockSpec((tb, 256), lambda i: (i, 0)), pl.BlockSpec((1, 256), lambda i: (0, 0))],
        out_shape=[jax.ShapeDtypeStruct((t, 256), bf16), jax.ShapeDtypeStruct((1, 256), f32)],
        compiler_params=_cparams(("arbitrary",)))(ddt_f, ddt_b, u_dt, bias_row)


def _ssd_common(dt_blk, a_row, reverse):
    row = lax.broadcasted_iota(jnp.int32, (CHUNK, CHUNK), 0)
    col = lax.broadcasted_iota(jnp.int32, (CHUNK, CHUNK), 1)
    mask = (row <= col) if reverse else (row >= col)
    tri = mask.astype(f32)
    a = dt_blk * a_row
    acs = _dot_exact(tri, a)
    atot = jnp.sum(a, axis=0, keepdims=True)
    return mask, tri, a, acs, atot, col


def _lane_col(mat, lane_idx, h):
    return jnp.sum(jnp.where(lane_idx == h, mat, 0.0), axis=1, keepdims=True)


def ssd_fwd(xbc_c, dt2, a_rows, reverse, name):
    t = xbc_c.shape[0]
    nc = t // CHUNK
    d_off = 1 if reverse else 0

    def cidx(c):
        return nc - 1 - c if reverse else c

    def body(x_ref, b_ref, c_ref, dt_ref, a_ref, y_ref, hp_ref, h_scr, acst_scr):
        g = pl.program_id(0)
        c = pl.program_id(1)

        @pl.when(c == 0)
        def _():
            h_scr[...] = jnp.zeros_like(h_scr)

        dt_blk = dt_ref[...]
        mask, tri, a, acs, atot, lane = _ssd_common(dt_blk, a_ref[...], reverse)
        acst_scr[...] = acs.T
        bm = b_ref[...]
        cm = c_ref[...]
        cb = _dot_nt(cm, bm)
        half = lane >= SSD_HEAD_DIM
        sub_half = lax.broadcasted_iota(jnp.int32, (CHUNK, 1), 0) >= SSD_HEAD_DIM
        for j in range(4):
            x = x_ref[:, 128 * j:128 * (j + 1)]
            cols, dts, tots = [], [], []
            y = None
            for e in range(2):
                h = 8 * g + 2 * j + e
                col_h = _lane_col(acs, lane, h)
                row_h = acst_scr[pl.ds(h, 1), :]
                dt_h = _lane_col(dt_blk, lane, h)
                lmat = jnp.where(mask, jnp.exp(jnp.where(mask, col_h - row_h, 0.0)), 0.0)
                xdt_e = jnp.where(half == (e == 1), x * dt_h, 0.0)
                ye = _dot(cb * lmat, xdt_e)
                y = ye if y is None else y + ye
                cols.append(col_h)
                dts.append(dt_h)
                tots.append(jnp.sum(jnp.where(lane[0:1] == h, atot, 0.0), axis=1, keepdims=True))
            hp = h_scr[j]
            hp_ref[0, j] = hp
            ecol = jnp.where(half, jnp.exp(cols[1]), jnp.exp(cols[0]))
            y = y + _dot_nt(cm, hp) * ecol
            y_ref[:, 128 * j:128 * (j + 1)] = y
            dec = jnp.where(half, jnp.exp(tots[1] - cols[1]), jnp.exp(tots[0] - cols[0]))
            xdt = x * jnp.where(half, dts[1], dts[0])
            s_new = _dot_tn(xdt * dec, bm)
            cd = jnp.where(sub_half, jnp.exp(tots[1]), jnp.exp(tots[0]))
            h_scr[j] = cd * hp + s_new

    return pl.pallas_call(
        body, name=name, grid=(SSD_GROUPS, nc),
        in_specs=[pl.BlockSpec((CHUNK, 512), lambda g, c: (cidx(c), g)),
                  pl.BlockSpec((CHUNK, 128), lambda g, c: (cidx(c), 16 + g)),
                  pl.BlockSpec((CHUNK, 128), lambda g, c: (cidx(c), 20 + g)),
                  pl.BlockSpec((CHUNK, 128), lambda g, c: (cidx(c), d_off)),
                  pl.BlockSpec((1, 128), lambda g, c: (0, d_off))],
        out_specs=[pl.BlockSpec((CHUNK, 512), lambda g, c: (cidx(c), g)),
                   pl.BlockSpec((1, 4, 128, 128), lambda g, c: (cidx(c), g, 0, 0))],
        out_shape=[jax.ShapeDtypeStruct((t, D_INNER), f32), jax.ShapeDtypeStruct((nc, 16, 128, 128), f32)],
        scratch_shapes=[pltpu.VMEM((4, 128, 128), f32), pltpu.VMEM((CHUNK, CHUNK), f32)],
        compiler_params=_cparams(("parallel", "arbitrary")))(xbc_c, xbc_c, xbc_c, dt2, a_rows)


def ssd_bwd(xbc_c, dt2, a_rows, dy, hprev, reverse, name):
    t = xbc_c.shape[0]
    nc = t // CHUNK
    d_off = 1 if reverse else 0

    def cidx(c):
        return c if reverse else nc - 1 - c

    def body(x_ref, b_ref, c_ref, dt_ref, a_ref, dy_ref, hp_ref, dx_ref, db_ref, dc_ref, ddt_ref, da_ref,
             dh_scr, acst_scr):
        g = pl.program_id(0)
        c = pl.program_id(1)

        @pl.when(c == 0)
        def _():
            dh_scr[...] = jnp.zeros_like(dh_scr)
            da_ref[...] = jnp.zeros_like(da_ref)

        dt_blk = dt_ref[...]
        a_row = a_ref[...]
        mask, tri, a, acs, atot, lane = _ssd_common(dt_blk, a_row, reverse)
        acst_scr[...] = acs.T
        sub = lax.broadcasted_iota(jnp.int32, (CHUNK, CHUNK), 0)
        bm = b_ref[...]
        cm = c_ref[...]
        cb = _dot_nt(cm, bm)
        half = lane >= SSD_HEAD_DIM
        sub_half = sub[:, 0:1] >= SSD_HEAD_DIM
        dcb = jnp.zeros((CHUNK, CHUNK), f32)
        dacs = jnp.zeros((CHUNK, CHUNK), f32)
        dacs_t = jnp.zeros((CHUNK, CHUNK), f32)
        dtot = jnp.zeros((1, CHUNK), f32)
        ddt_x = jnp.zeros((CHUNK, CHUNK), f32)
        dbm = jnp.zeros((CHUNK, D_STATE), f32)
        dcm = jnp.zeros((CHUNK, D_STATE), f32)
        for j in range(4):
            x = x_ref[:, 128 * j:128 * (j + 1)]
            dyp = dy_ref[:, 128 * j:128 * (j + 1)]
            hp = hp_ref[0, j]
            dhn = dh_scr[j]
            cols, dts, tots, hs = [], [], [], []
            dxdt = None
            for e in range(2):
                h = 8 * g + 2 * j + e
                sel = half == (e == 1)
                col_h = _lane_col(acs, lane, h)
                row_h = acst_scr[pl.ds(h, 1), :]
                dt_h = _lane_col(dt_blk, lane, h)
                lmat = jnp.where(mask, jnp.exp(jnp.where(mask, col_h - row_h, 0.0)), 0.0)
                xdt_e = jnp.where(sel, x * dt_h, 0.0)
                dy_e = jnp.where(sel, dyp, 0.0)
                ml = _dot_nt(dy_e, xdt_e) * lmat
                dcb = dcb + ml
                w = ml * cb
                dacs = dacs + jnp.where(lane == h, jnp.sum(w, axis=1, keepdims=True), 0.0)
                dacs_t = dacs_t - jnp.where(sub == h, jnp.sum(w, axis=0, keepdims=True), 0.0)
                de = _dot_tn(cb * lmat, dy_e)
                dxdt = de if dxdt is None else dxdt + de
                cols.append(col_h)
                dts.append(dt_h)
                tots.append(jnp.sum(jnp.where(lane[0:1] == h, atot, 0.0), axis=1, keepdims=True))
                hs.append(h)
            ecol = jnp.where(half, jnp.exp(cols[1]), jnp.exp(cols[0]))
            dec = jnp.where(half, jnp.exp(tots[1] - cols[1]), jnp.exp(tots[0] - cols[0]))
            cd = jnp.where(sub_half, jnp.exp(tots[1]), jnp.exp(tots[0]))
            dtp = jnp.where(half, dts[1], dts[0])
            xdt = x * dtp
            yoff = _dot_nt(cm, hp) * ecol
            dye = dyp * ecol
            dcm = dcm + _dot(dye, hp)
            dhp = _dot_tn(dye, cm)
            gmat = _dot_nt(bm, dhn)
            dxdt = dxdt + dec * gmat
            dbm = dbm + _dot(xdt * dec, dhn)
            r_off = dyp * yoff
            r_dec = xdt * gmat * dec
            r_x = dxdt * x
            hh = dhn * hp
            for e in range(2):
                sel = half == (e == 1)
                h = hs[e]
                s_off = jnp.sum(jnp.where(sel, r_off, 0.0), axis=1, keepdims=True)
                s_dec = jnp.sum(jnp.where(sel, r_dec, 0.0), axis=1, keepdims=True)
                dacs = dacs + jnp.where(lane == h, s_off - s_dec, 0.0)
                dcd = jnp.sum(jnp.sum(jnp.where(sub_half == (e == 1), hh, 0.0), axis=1, keepdims=True), axis=0, keepdims=True)
                tot_e = jnp.sum(s_dec, axis=0, keepdims=True) + jnp.exp(tots[e]) * dcd
                dtot = dtot + jnp.where(lane[0:1] == h, tot_e, 0.0)
                ddt_x = ddt_x + jnp.where(lane == h, jnp.sum(jnp.where(sel, r_x, 0.0), axis=1, keepdims=True), 0.0)
            dx_ref[:, 128 * j:128 * (j + 1)] = dxdt * dtp
            dh_scr[j] = cd * dhn + dhp
        dcm = dcm + _dot(dcb, bm)
        dbm = dbm + _dot_tn(dcb, cm)
        db_ref[...] = dbm
        dc_ref[...] = dcm
        dacs = dacs + dacs_t.T
        da = _dot_exact(tri.T, dacs) + dtot
        ddt_ref[0] = da * a_row + ddt_x
        da_ref[0] += jnp.sum(da * dt_blk, axis=0, keepdims=True)

    return pl.pallas_call(
        body, name=name, grid=(SSD_GROUPS, nc),
        in_specs=[pl.BlockSpec((CHUNK, 512), lambda g, c: (cidx(c), g)),
                  pl.BlockSpec((CHUNK, 128), lambda g, c: (cidx(c), 16 + g)),
                  pl.BlockSpec((CHUNK, 128), lambda g, c: (cidx(c), 20 + g)),
                  pl.BlockSpec((CHUNK, 128), lambda g, c: (cidx(c), d_off)),
                  pl.BlockSpec((1, 128), lambda g, c: (0, d_off)),
                  pl.BlockSpec((CHUNK, 512), lambda g, c: (cidx(c), g)),
                  pl.BlockSpec((1, 4, 128, 128), lambda g, c: (cidx(c), g, 0, 0))],
        out_specs=[pl.BlockSpec((CHUNK, 512), lambda g, c: (cidx(c), g)),
                   pl.BlockSpec((CHUNK, 128), lambda g, c: (cidx(c), g)),
                   pl.BlockSpec((CHUNK, 128), lambda g, c: (cidx(c), g)),
                   pl.BlockSpec((1, CHUNK, 128), lambda g, c: (g, cidx(c), 0)),
                   pl.BlockSpec((1, 1, 128), lambda g, c: (g, 0, 0))],
        out_shape=[jax.ShapeDtypeStruct((t, D_INNER), f32), jax.ShapeDtypeStruct((t, 512), f32),
                   jax.ShapeDtypeStruct((t, 512), f32), jax.ShapeDtypeStruct((4, t, 128), f32),
                   jax.ShapeDtypeStruct((4, 1, 128), f32)],
        scratch_shapes=[pltpu.VMEM((4, 128, 128), f32), pltpu.VMEM((CHUNK, CHUNK), f32)],
        compiler_params=_cparams(("parallel", "arbitrary")))(xbc_c, xbc_c, xbc_c, dt2, a_rows, dy, hprev)


def tail_fwd(y_f, y_b, xbc_c, z, dskip_row, nw_row, tb=512):
    t = y_f.shape[0]

    def body(yf_ref, yb_ref, x_ref, z_ref, d_ref, w_ref, o_ref):
        zz = z_ref[...]
        y = (yf_ref[...] + yb_ref[...] + d_ref[...] * x_ref[...]) * (zz * _sigmoid(zz))
        rstd = lax.rsqrt(jnp.mean(y * y, axis=1, keepdims=True) + NORM_EPS)
        o_ref[...] = (y * rstd * w_ref[...]).astype(o_ref.dtype)

    blk = pl.BlockSpec((tb, 512), lambda i, g: (i, g))
    row = pl.BlockSpec((1, 512), lambda i, g: (0, g))
    return pl.pallas_call(
        body, name="tail_fwd", grid=(t // tb, SSD_GROUPS), in_specs=[blk, blk, blk, blk, row, row], out_specs=blk,
        out_shape=jax.ShapeDtypeStruct((t, D_INNER), bf16),
        compiler_params=_cparams(("parallel", "parallel")))(y_f, y_b, xbc_c, z, dskip_row, nw_row)


def tail_bwd(dyn, y_f, y_b, xbc_c, z, dskip_row, nw_row, tb=512):
    t = y_f.shape[0]

    def body(g_ref, yf_ref, yb_ref, x_ref, z_ref, d_ref, w_ref, dy_ref, dz_ref, dw_ref, dd_ref):
        zz = z_ref[...]
        sg = _sigmoid(zz)
        sl = zz * sg
        x = x_ref[...]
        y = yf_ref[...] + yb_ref[...] + d_ref[...] * x
        yz = y * sl
        rstd = lax.rsqrt(jnp.mean(yz * yz, axis=1, keepdims=True) + NORM_EPS)
        yhat = yz * rstd
        g = g_ref[...]
        dyhat = g * w_ref[...]
        dyz = rstd * (dyhat - yhat * jnp.mean(dyhat * yhat, axis=1, keepdims=True))
        dy = dyz * sl
        dy_ref[...] = dy
        dz_ref[...] = (dyz * y * sg * (1.0 + zz * (1.0 - sg))).astype(dz_ref.dtype)

        @pl.when(pl.program_id(1) == 0)
        def _():
            dw_ref[...] = jnp.zeros_like(dw_ref)
            dd_ref[...] = jnp.zeros_like(dd_ref)

        dw_ref[...] += jnp.sum(g * yhat, axis=0, keepdims=True)
        dd_ref[...] += jnp.sum(dy * x, axis=0, keepdims=True)

    blk = pl.BlockSpec((tb, 512), lambda g, i: (i, g))
    row = pl.BlockSpec((1, 512), lambda g, i: (0, g))
    return pl.pallas_call(
        body, name="tail_bwd", grid=(SSD_GROUPS, t // tb), in_specs=[blk, blk, blk, blk, blk, row, row],
        out_specs=[blk, blk, row, row],
        out_shape=[jax.ShapeDtypeStruct((t, D_INNER), f32), jax.ShapeDtypeStruct((t, D_INNER), bf16),
                   jax.ShapeDtypeStruct((1, D_INNER), f32), jax.ShapeDtypeStruct((1, D_INNER), f32)],
        compiler_params=_cparams(("parallel", "arbitrary")))(dyn, y_f, y_b, xbc_c, z, dskip_row, nw_row)


def _slopes(p):
    return [2.0 ** (-8.0 * (HEADS_PER_PATTERN * p + j + 1) / ATTN_HEADS) for j in range(HEADS_PER_PATTERN)]


def _win_specs(nq, col_of):
    return [pl.BlockSpec((64, 256), lambda r, i: (jnp.maximum(2 * i - 1, 0), col_of(r))),
            pl.BlockSpec((128, 256), lambda r, i: (i, col_of(r))),
            pl.BlockSpec((64, 256), lambda r, i: (jnp.minimum(2 * i + 2, 2 * nq - 1), col_of(r)))]


def _lane_head(shape):
    return lax.broadcasted_iota(jnp.int32, shape, 1) >> 6


def _stack_heads(m):
    lane_head = _lane_head(m.shape)
    return jnp.concatenate([jnp.where(lane_head == j, m, 0.0) for j in range(HEADS_PER_PATTERN)], axis=0)


def _unstack_heads(m4, n):
    lane_head = _lane_head((n, 256))
    out = jnp.where(lane_head == 0, m4[0:n], 0.0)
    for j in range(1, HEADS_PER_PATTERN):
        out = out + jnp.where(lane_head == j, m4[j * n:(j + 1) * n], 0.0)
    return out


def _head_cols(m, n):
    lane = lax.broadcasted_iota(jnp.int32, (n, 256), 1)
    return jnp.concatenate([jnp.sum(jnp.where(lane == ATTN_HEAD_DIM * j, m, 0.0), axis=1, keepdims=True)
                            for j in range(HEADS_PER_PATTERN)], axis=0)


def _score_bias(p, dil, by_key):
    slopes = np.asarray(_slopes(p), np.float32)
    if by_key:
        win = np.arange(256)[:, None]
        rel = np.arange(128)[None, :] - (win - 64)
    else:
        win = np.arange(256)[None, :]
        rel = win - 64 - np.arange(128)[:, None]
    band = np.abs(rel) <= 64
    out = []
    for first, last in ((False, False), (True, False), (False, True), (True, True)):
        ok = band & ~(first & (win < 64)) & ~(last & (win >= 192))
        pen = -slopes[:, None, None] * (np.abs(rel) * dil).astype(np.float32)[None]
        out.append(np.where(ok[None], pen, np.float32(NEG_BIG)).reshape(-1, rel.shape[1]))
    return jnp.asarray(np.stack(out), f32)


def _bias_spec(nq, rows, cols):
    return pl.BlockSpec((1, rows, cols), lambda r, i: ((i == 0).astype(jnp.int32) + 2 * (i == nq - 1).astype(jnp.int32), 0, 0))


def attn_fwd(q, k, v, p, dil, name):
    l = q.shape[0]
    nq = l // 128

    def body(q_ref, kp_ref, ko_ref, kn_ref, vp_ref, vo_ref, vn_ref, bias_ref, o_ref, lse_ref):
        kcat = jnp.concatenate([kp_ref[...], ko_ref[...], kn_ref[...]], axis=0)
        vcat = jnp.concatenate([vp_ref[...], vo_ref[...], vn_ref[...]], axis=0)
        s = _dot_nt(_stack_heads(q_ref[...] * 0.125), kcat) + bias_ref[0]
        m = jnp.max(s, axis=1, keepdims=True)
        pr = jnp.exp(s - m)
        den = jnp.sum(pr, axis=1, keepdims=True)
        o4 = _dot(pr, vcat) / den
        o_ref[...] = _unstack_heads(o4, 128)
        lse_ref[...] = _unstack_heads(jnp.broadcast_to(m + jnp.log(den), (512, 256)), 128)

    col = lambda r: r
    return pl.pallas_call(
        body, name=name, grid=(dil, nq),
        in_specs=[pl.BlockSpec((128, 256), lambda r, i: (i, r))] + _win_specs(nq, col) + _win_specs(nq, col)
        + [_bias_spec(nq, 512, 256)],
        out_specs=[pl.BlockSpec((128, 256), lambda r, i: (i, r))] * 2,
        out_shape=[jax.ShapeDtypeStruct(q.shape, f32)] * 2,
        compiler_params=_cparams(("parallel", "parallel")))(q, k, k, k, v, v, v, _score_bias(p, dil, False))


def attn_combine(os_, lses, tb=1024):
    t = os_[0].shape[0]

    def body(o0, o1, o2, l0, l1, l2, y_ref, lse_ref):
        a0, a1, a2 = l0[...], l1[...], l2[...]
        m = jnp.maximum(jnp.maximum(a0, a1), a2)
        e0, e1, e2 = jnp.exp(a0 - m), jnp.exp(a1 - m), jnp.exp(a2 - m)
        den = e0 + e1 + e2
        y_ref[...] = (e0 * o0[...] + e1 * o1[...] + e2 * o2[...]) / den
        lse_ref[...] = m + jnp.log(den)

    blk = pl.BlockSpec((tb, 256), lambda i: (i, 0))
    return pl.pallas_call(
        body, name="attn_combine", grid=(t // tb,), in_specs=[blk] * 6, out_specs=[blk, blk],
        out_shape=[jax.ShapeDtypeStruct((t, 256), f32)] * 2,
        compiler_params=_cparams(("parallel",)))(*os_, *lses)


def attn_delta(dy, y, tb=1024):
    t = dy.shape[0]

    def body(dy_ref, y_ref, d_ref):
        pr = dy_ref[...] * y_ref[...]
        lane_head = _lane_head(pr.shape)
        out = jnp.zeros_like(pr)
        for j in range(HEADS_PER_PATTERN):
            sj = jnp.sum(jnp.where(lane_head == j, pr, 0.0), axis=1, keepdims=True)
            out = out + jnp.where(lane_head == j, sj, 0.0)
        d_ref[...] = out

    blk = pl.BlockSpec((tb, 256), lambda i: (i, 0))
    return pl.pallas_call(body, name="attn_delta", grid=(t // tb,), in_specs=[blk, blk], out_specs=blk,
                          out_shape=jax.ShapeDtypeStruct((t, 256), f32),
                          compiler_params=_cparams(("parallel",)))(dy, y)


def attn_dq(q, k, v, dy, lse, delta, p, dil, name):
    l = q.shape[0]
    nq = l // 128

    def body(q_ref, kp_ref, ko_ref, kn_ref, vp_ref, vo_ref, vn_ref, dy_ref, lse_ref, d_ref, bias_ref, dq_ref):
        kcat = jnp.concatenate([kp_ref[...], ko_ref[...], kn_ref[...]], axis=0)
        vcat = jnp.concatenate([vp_ref[...], vo_ref[...], vn_ref[...]], axis=0)
        s = _dot_nt(_stack_heads(q_ref[...] * 0.125), kcat) + bias_ref[0]
        pr = jnp.exp(s - _head_cols(lse_ref[...], 128))
        dp = _dot_nt(_stack_heads(dy_ref[...]), vcat)
        ds = pr * (dp - _head_cols(d_ref[...], 128))
        dq_ref[...] = (_unstack_heads(_dot(ds, kcat), 128) * 0.125).astype(dq_ref.dtype)

    col = lambda r: r
    own = pl.BlockSpec((128, 256), lambda r, i: (i, r))
    return pl.pallas_call(
        body, name=name, grid=(dil, nq),
        in_specs=[own] + _win_specs(nq, col) + _win_specs(nq, col) + [own, own, own, _bias_spec(nq, 512, 256)],
        out_specs=own, out_shape=jax.ShapeDtypeStruct(q.shape, bf16),
        compiler_params=_cparams(("parallel", "parallel")))(q, k, k, k, v, v, v, dy, lse, delta, _score_bias(p, dil, False))


def attn_dkv(q, k, v, dy, lse, delta, p, dil, name):
    l = q.shape[0]
    nq = l // 128

    def body(qp_ref, qo_ref, qn_ref, gp_ref, go_ref, gn_ref, lp_ref, lo_ref, ln_ref, dp_ref, do_ref, dn_ref,
             k_ref, v_ref, bias_ref, dk_ref, dv_ref):
        cat = lambda a, b, c: jnp.concatenate([a[...], b[...], c[...]], axis=0)
        q4 = _stack_heads(cat(qp_ref, qo_ref, qn_ref) * 0.125)
        dy4 = _stack_heads(cat(gp_ref, go_ref, gn_ref))
        lse4 = _head_cols(cat(lp_ref, lo_ref, ln_ref), 256)
        del4 = _head_cols(cat(dp_ref, do_ref, dn_ref), 256)
        s = _dot_nt(q4, k_ref[...]) + bias_ref[0]
        pr = jnp.exp(s - lse4)
        dpm = _dot_nt(dy4, v_ref[...])
        ds = pr * (dpm - del4)
        dv_ref[...] = _dot_tn(pr, dy4).astype(dv_ref.dtype)
        dk_ref[...] = _dot_tn(ds, q4).astype(dk_ref.dtype)

    col = lambda r: r
    own = pl.BlockSpec((128, 256), lambda r, i: (i, r))
    win = _win_specs(nq, col)
    return pl.pallas_call(
        body, name=name, grid=(dil, nq), in_specs=win * 4 + [own, own, _bias_spec(nq, 1024, 128)], out_specs=[own, own],
        out_shape=[jax.ShapeDtypeStruct(q.shape, bf16)] * 2,
        compiler_params=_cparams(("parallel", "parallel")))(q, q, q, dy, dy, dy, lse, lse, lse, delta, delta, delta, k, v,
                                                            _score_bias(p, dil, True))


def _lanes(v, reps):
    return v if reps == 1 else jnp.tile(v, (1, reps))


def _lane_halo_specs(cb, tb, nt, off=0):
    r = tb // 128
    return [pl.BlockSpec((cb, 128), lambda j, i: (j + off, jnp.maximum(i * r - 1, 0))),
            pl.BlockSpec((cb, tb), lambda j, i: (j + off, i)),
            pl.BlockSpec((cb, 128), lambda j, i: (j + off, jnp.minimum((i + 1) * r, nt * r - 1)))]


def _with_lane_halo(prev_ref, own_ref, next_ref, i, nt):
    prev = jnp.where(i > 0, prev_ref[...].astype(f32), 0.0)
    nxt = jnp.where(i < nt - 1, next_ref[...].astype(f32), 0.0)
    return jnp.concatenate([prev, own_ref[...].astype(f32), nxt], axis=1)


def _lane_shifted(xcat, s, tb):
    n = xcat.shape[1]
    return pltpu.roll(xcat, (-s) % n, 1)[:, 128:128 + tb]


def conv_fwd_t(xbc_t, w_b, b_b, tb=1024, cb=256):
    c, t = xbc_t.shape
    nt = t // tb

    def body(prev_ref, own_ref, next_ref, w_ref, b_ref, o_ref):
        i = pl.program_id(1)
        xcat = _with_lane_halo(prev_ref, own_ref, next_ref, i, nt)
        reps = tb // 128
        pre = _lanes(b_ref[...], reps)
        for k in range(D_CONV):
            pre = pre + _lanes(w_ref[k], reps) * _lane_shifted(xcat, k - 2, tb)
        o_ref[...] = pre * _sigmoid(pre)

    return pl.pallas_call(
        body, name="conv_fwd", grid=(c // cb, nt),
        in_specs=_lane_halo_specs(cb, tb, nt) + [pl.BlockSpec((D_CONV, cb, 128), lambda j, i: (0, j, 0)),
                                                 pl.BlockSpec((cb, 128), lambda j, i: (j, 0))],
        out_specs=pl.BlockSpec((cb, tb), lambda j, i: (j, i)), out_shape=jax.ShapeDtypeStruct((c, t), f32),
        compiler_params=_cparams(("parallel", "parallel")))(xbc_t, xbc_t, xbc_t, w_b, b_b)


def conv_bwd_t(xbc_t, grad_t, w_b, b_b, into, name, row0, tb=1024, cb=256):
    c, t = grad_t.shape
    nt = t // tb
    off = row0 // cb
    reps = tb // 128

    def body(*refs):
        i = pl.program_id(1)
        xr, gr = refs[0:3], refs[3:6]
        w_ref, b_ref = refs[6:8]
        dx_ref, dw_ref, db_ref = refs[-3:]
        xcat = _with_lane_halo(*xr, i, nt)
        gcat = _with_lane_halo(*gr, i, nt)
        n = tb + 256
        wk = [_lanes(w_ref[k], reps + 2) for k in range(D_CONV)]
        pre = _lanes(b_ref[...], reps + 2)
        for k in range(D_CONV):
            pre = pre + wk[k] * pltpu.roll(xcat, (2 - k) % n, 1)
        sg = _sigmoid(pre)
        dpre = gcat * sg * (1.0 + pre * (1.0 - sg))
        def fold(v):
            s = v[:, 0:128]
            for q in range(1, reps):
                s = s + v[:, 128 * q:128 * (q + 1)]
            return s

        @pl.when(i == 0)
        def _():
            dw_ref[...] = jnp.zeros_like(dw_ref)
            db_ref[...] = jnp.zeros_like(db_ref)

        x_own = xcat[:, 128:128 + tb]
        dx = None
        for k in range(D_CONV):
            shifted = _lane_shifted(dpre, 2 - k, tb)
            term = wk[k][:, 128:128 + tb] * shifted
            dx = term if dx is None else dx + term
            dw_ref[k] += fold(shifted * x_own)
        dx_ref[...] = dx.astype(dx_ref.dtype)
        db_ref[...] += fold(dpre[:, 128:128 + tb])

    in_specs = (_lane_halo_specs(cb, tb, nt, off) + _lane_halo_specs(cb, tb, nt)
                + [pl.BlockSpec((D_CONV, cb, 128), lambda j, i: (0, j + off, 0)), pl.BlockSpec((cb, 128), lambda j, i: (j + off, 0))])
    args = [xbc_t] * 3 + [grad_t] * 3 + [w_b, b_b]
    aliases = {}
    if into is not None:
        in_specs.append(pl.BlockSpec(memory_space=pl.ANY))
        args.append(into)
        aliases = {len(args) - 1: 0}
    return pl.pallas_call(
        body, name=name, grid=(c // cb, nt), in_specs=in_specs,
        out_specs=[pl.BlockSpec((cb, tb), lambda j, i: (j + off, i)), pl.BlockSpec((D_CONV, cb, 128), lambda j, i: (0, j, 0)),
                   pl.BlockSpec((cb, 128), lambda j, i: (j, 0))],
        out_shape=[jax.ShapeDtypeStruct((CONV_DIM, t), bf16), jax.ShapeDtypeStruct((D_CONV, c, 128), f32),
                   jax.ShapeDtypeStruct((c, 128), f32)],
        input_output_aliases=aliases, compiler_params=_cparams(("parallel", "arbitrary")))(*args)


def dt_fwd_t(u_dt_t, bias_b, tb=2048):
    r, t = u_dt_t.shape

    def body(u_ref, b_ref, o_ref):
        v = u_ref[...] + _lanes(b_ref[...], tb // 128)
        o_ref[...] = jnp.maximum(v, 0.0) + jnp.log(1.0 + jnp.exp(-jnp.abs(v)))

    return pl.pallas_call(
        body, name="dt_fwd", grid=(t // tb,),
        in_specs=[pl.BlockSpec((r, tb), lambda i: (0, i)), pl.BlockSpec((r, 128), lambda i: (0, 0))],
        out_specs=pl.BlockSpec((r, tb), lambda i: (0, i)), out_shape=jax.ShapeDtypeStruct((r, t), f32),
        compiler_params=_cparams(("parallel",)))(u_dt_t, bias_b)


def dt_bwd_t(ddt_f, ddt_b, u_dt_t, bias_b, tb=2048):
    r, t = u_dt_t.shape
    reps = tb // 128

    def body(gf_ref, gb_ref, u_ref, b_ref, du_ref, db_ref):
        g = jnp.concatenate([gf_ref[...], gb_ref[...]], axis=0)
        du = g * _sigmoid(u_ref[...] + _lanes(b_ref[...], reps))
        du_ref[...] = du.astype(du_ref.dtype)

        @pl.when(pl.program_id(0) == 0)
        def _():
            db_ref[...] = jnp.zeros_like(db_ref)

        s = du[:, 0:128]
        for q in range(1, reps):
            s = s + du[:, 128 * q:128 * (q + 1)]
        db_ref[...] += s

    half = pl.BlockSpec((r // 2, tb), lambda i: (0, i))
    return pl.pallas_call(
        body, name="dt_bwd", grid=(t // tb,),
        in_specs=[half, half, pl.BlockSpec((r, tb), lambda i: (0, i)), pl.BlockSpec((r, 128), lambda i: (0, 0))],
        out_specs=[pl.BlockSpec((r, tb), lambda i: (0, i)), pl.BlockSpec((r, 128), lambda i: (0, 0))],
        out_shape=[jax.ShapeDtypeStruct((r, t), bf16), jax.ShapeDtypeStruct((r, 128), f32)],
        compiler_params=_cparams(("arbitrary",)))(ddt_f, ddt_b, u_dt_t, bias_b)


HEADS_PER_GROUP = SSD_HEADS // SSD_GROUPS


def _group_rows(g, n):
    return pl.ds(pl.multiple_of(g * n, n), n)


def _ssd_decays(dt_blk, a_blk, reverse):
    row = lax.broadcasted_iota(jnp.int32, (CHUNK, CHUNK), 0)
    col = lax.broadcasted_iota(jnp.int32, (CHUNK, CHUNK), 1)
    mask = (row <= col) if reverse else (row >= col)
    tri = mask.astype(f32)
    a8 = dt_blk * a_blk
    a = jnp.concatenate([a8, jnp.zeros((CHUNK - HEADS_PER_GROUP, CHUNK), f32)], axis=0).T
    acs = _dot_exact(tri, a)
    return mask, tri, a8, acs, acs.T, col


def ssd_fwd_t(xbc_ct, dt_t, a_b, reverse, name, prev=None):
    t = xbc_ct.shape[1]
    nc = t // CHUNK
    direction = 1 if reverse else 0

    def cidx(c):
        return nc - 1 - c if reverse else c

    def body(*refs):
        x_ref, b_ref, c_ref, dt_ref, a_ref = refs[0:5]
        prev_ref = refs[5] if prev is not None else None
        y_ref, hp_ref, h_scr = refs[-3:]

        @pl.when(pl.program_id(0) == 0)
        def _():
            h_scr[...] = jnp.zeros_like(h_scr)

        def group(g, carry):
            x_v, y_v = x_ref.at[_group_rows(g, 512)], y_ref.at[_group_rows(g, 512)]
            heads = _group_rows(g, HEADS_PER_GROUP)
            hp_v, h_v = hp_ref.at[0, heads], h_scr.at[heads]
            dt_blk = dt_ref[heads, :]
            mask, tri, a8, acs, acs_t, lane = _ssd_decays(dt_blk, a_ref[heads, :], reverse)
            bm = b_ref[_group_rows(g, 128), :].T
            cm = c_ref[_group_rows(g, 128), :].T
            cb = _dot_nt(cm, bm)
            tot = jnp.sum(a8, axis=1, keepdims=True)
            for j in range(HEADS_PER_GROUP):
                rows = slice(SSD_HEAD_DIM * j, SSD_HEAD_DIM * (j + 1))
                col_j = _lane_col(acs, lane, j)
                row_j = acs_t[j:j + 1, :]
                lmat = jnp.where(mask, jnp.exp(jnp.where(mask, col_j - row_j, 0.0)), 0.0)
                xdt = x_v[rows, :] * dt_blk[j:j + 1, :]
                hp = h_v[j]
                hp_v[j] = hp
                y = _dot_nt(xdt, cb * lmat) + _dot_nt(hp, cm) * jnp.exp(row_j)
                if prev_ref is not None:
                    y = y + prev_ref.at[_group_rows(g, 512)][rows, :]
                y_v[rows, :] = y
                tot_j = tot[j:j + 1, :]
                h_v[j] = jnp.exp(tot_j) * hp + _dot(xdt * jnp.exp(tot_j - row_j), bm)
            return carry

        lax.fori_loop(0, SSD_GROUPS, group, 0)

    big = pl.BlockSpec((D_INNER, CHUNK), lambda c: (0, cidx(c)))
    in_specs = [big, pl.BlockSpec((512, CHUNK), lambda c: (4, cidx(c))), pl.BlockSpec((512, CHUNK), lambda c: (5, cidx(c))),
                pl.BlockSpec((SSD_HEADS, CHUNK), lambda c: (direction, cidx(c))),
                pl.BlockSpec((SSD_HEADS, 128), lambda c: (direction, 0))]
    args = [xbc_ct, xbc_ct, xbc_ct, dt_t, a_b]
    if prev is not None:
        in_specs.append(big)
        args.append(prev)
    return pl.pallas_call(
        body, name=name, grid=(nc,), in_specs=in_specs,
        out_specs=[big, pl.BlockSpec((1, SSD_HEADS, SSD_HEAD_DIM, D_STATE), lambda c: (cidx(c), 0, 0, 0))],
        out_shape=[jax.ShapeDtypeStruct((D_INNER, t), f32), jax.ShapeDtypeStruct((nc, SSD_HEADS, SSD_HEAD_DIM, D_STATE), f32)],
        scratch_shapes=[pltpu.VMEM((SSD_HEADS, SSD_HEAD_DIM, D_STATE), f32)],
        compiler_params=_cparams(("arbitrary",)))(*args)


def ssd_bwd_t(xbc_ct, dt_t, a_b, dy_t, hprev, reverse, name, skip_b=None, prev=None):
    t = xbc_ct.shape[1]
    nc = t // CHUNK
    direction = 1 if reverse else 0

    def cidx(c):
        return c if reverse else nc - 1 - c

    def body(*refs):
        x_ref, b_ref, c_ref, dt_ref, a_ref, dy_ref, hp_ref = refs[0:7]
        pos = 7
        skip_ref = None
        if skip_b is not None:
            skip_ref = refs[pos]
            pos += 1
        prev_refs = None
        if prev is not None:
            prev_refs = refs[pos:pos + 3]
            pos += 3
        dx_ref, db_ref, dc_ref, ddt_ref, da_ref, dh_scr = refs[pos:pos + 6]

        @pl.when(pl.program_id(0) == 0)
        def _():
            dh_scr[...] = jnp.zeros_like(dh_scr)
            da_ref[...] = jnp.zeros_like(da_ref)

        def group(g, carry):
            big, st, heads = _group_rows(g, 512), _group_rows(g, 128), _group_rows(g, HEADS_PER_GROUP)
            x_v, dy_v, dx_v = x_ref.at[big], dy_ref.at[big], dx_ref.at[big]
            hp_v, dh_v = hp_ref.at[0, heads], dh_scr.at[heads]
            dt_blk = dt_ref[heads, :]
            a_blk = a_ref[heads, :]
            mask, tri, a8, acs, acs_t, lane = _ssd_decays(dt_blk, a_blk, reverse)
            sub = lax.broadcasted_iota(jnp.int32, (CHUNK, CHUNK), 0)
            mask_t = (sub >= lane) if reverse else (sub <= lane)
            bm = b_ref[st, :].T
            cm = c_ref[st, :].T
            cb = _dot_nt(cm, bm)
            cb_t = _dot_nt(bm, cm)
            tot = jnp.sum(a8, axis=1, keepdims=True)
            dcb = jnp.zeros((CHUNK, CHUNK), f32)
            dbm = jnp.zeros((CHUNK, D_STATE), f32)
            dcm = jnp.zeros((CHUNK, D_STATE), f32)
            dacs_rows, ddtx_rows = [], []
            for j in range(HEADS_PER_GROUP):
                rows = slice(SSD_HEAD_DIM * j, SSD_HEAD_DIM * (j + 1))
                col_j = _lane_col(acs, lane, j)
                row_j = acs_t[j:j + 1, :]
                dt_j = dt_blk[j:j + 1, :]
                tot_j = tot[j:j + 1, :]
                lmat = jnp.where(mask, jnp.exp(jnp.where(mask, col_j - row_j, 0.0)), 0.0)
                lmat_t = jnp.where(mask_t, jnp.exp(jnp.where(mask_t, row_j - col_j, 0.0)), 0.0)
                x = x_v[rows, :]
                xdt = x * dt_j
                dyh = dy_v[rows, :]
                hp = hp_v[j]
                dhn = dh_v[j]
                ml = _dot_tn(dyh, xdt) * lmat
                w_t = _dot_tn(xdt, dyh) * lmat_t * cb_t
                dcb = dcb + ml
                dacs = jnp.sum(w_t, axis=0, keepdims=True) - jnp.sum(ml * cb, axis=0, keepdims=True)
                ecol = jnp.exp(row_j)
                dec = jnp.exp(tot_j - row_j)
                dye = dyh * ecol
                yoff = _dot_nt(hp, cm) * ecol
                gmat = _dot_nt(dhn, bm)
                dxdt = _dot(dyh, cb * lmat) + dec * gmat
                s_dec = jnp.sum(xdt * gmat, axis=0, keepdims=True) * dec
                dacs = dacs + jnp.sum(dyh * yoff, axis=0, keepdims=True) - s_dec
                dcd = jnp.sum(jnp.sum(dhn * hp, axis=1, keepdims=True), axis=0, keepdims=True)
                dtot = jnp.sum(s_dec, axis=1, keepdims=True) + jnp.exp(tot_j) * dcd
                dacs_rows.append((dacs, dtot))
                ddtx_rows.append(jnp.sum(dxdt * x, axis=0, keepdims=True))
                dcm = dcm + _dot_tn(dye, hp)
                dbm = dbm + _dot_tn(xdt * dec, dhn)
                dxh = dxdt * dt_j
                if skip_ref is not None:
                    dxh = dxh + skip_ref.at[big][rows, :] * dyh
                if prev_refs is not None:
                    dxh = dxh + prev_refs[0].at[big][rows, :]
                dx_v[rows, :] = dxh
                dh_v[j] = jnp.exp(tot_j) * dhn + _dot(dye, cm)
            dcm = dcm + _dot(dcb, bm)
            dbm = dbm + _dot_tn(dcb, cm)
            dbt, dct = dbm.T, dcm.T
            if prev_refs is not None:
                dbt = dbt + prev_refs[1][st, :]
                dct = dct + prev_refs[2][st, :]
            db_ref[st, :] = dbt
            dc_ref[st, :] = dct
            dacs8 = jnp.concatenate([d for d, _ in dacs_rows], axis=0)
            dtot8 = jnp.concatenate([d for _, d in dacs_rows], axis=0)
            da8 = _dot_exact(dacs8, tri) + dtot8
            ddt_ref[heads, :] = da8 * a_blk + jnp.concatenate(ddtx_rows, axis=0)
            da_ref[heads, :] += da8 * dt_blk
            return carry

        lax.fori_loop(0, SSD_GROUPS, group, 0)

    big = pl.BlockSpec((D_INNER, CHUNK), lambda c: (0, cidx(c)))
    st = pl.BlockSpec((512, CHUNK), lambda c: (0, cidx(c)))
    in_specs = [big, pl.BlockSpec((512, CHUNK), lambda c: (4, cidx(c))), pl.BlockSpec((512, CHUNK), lambda c: (5, cidx(c))),
                pl.BlockSpec((SSD_HEADS, CHUNK), lambda c: (direction, cidx(c))),
                pl.BlockSpec((SSD_HEADS, 128), lambda c: (direction, 0)), big,
                pl.BlockSpec((1, SSD_HEADS, SSD_HEAD_DIM, D_STATE), lambda c: (cidx(c), 0, 0, 0))]
    args = [xbc_ct, xbc_ct, xbc_ct, dt_t, a_b, dy_t, hprev]
    if skip_b is not None:
        in_specs.append(pl.BlockSpec((D_INNER, 128), lambda c: (0, 0)))
        args.append(skip_b)
    if prev is not None:
        in_specs += [big, st, st]
        args += list(prev)
    return pl.pallas_call(
        body, name=name, grid=(nc,), in_specs=in_specs,
        out_specs=[big, st, st, pl.BlockSpec((SSD_HEADS, CHUNK), lambda c: (0, cidx(c))),
                   pl.BlockSpec((SSD_HEADS, 128), lambda c: (0, 0))],
        out_shape=[jax.ShapeDtypeStruct((D_INNER, t), f32), jax.ShapeDtypeStruct((512, t), f32),
                   jax.ShapeDtypeStruct((512, t), f32), jax.ShapeDtypeStruct((SSD_HEADS, t), f32),
                   jax.ShapeDtypeStruct((SSD_HEADS, 128), f32)],
        scratch_shapes=[pltpu.VMEM((SSD_HEADS, SSD_HEAD_DIM, D_STATE), f32)],
        compiler_params=_cparams(("arbitrary",)))(*args)


def tail_fwd_t(y_scan, xbc_ct, z_t, skip_b, nw_b, tb=512):
    t = y_scan.shape[1]
    reps = tb // 128

    def body(ys_ref, x_ref, z_ref, d_ref, w_ref, o_ref):
        zz = z_ref[...]
        y = (ys_ref[...] + _lanes(d_ref[...], reps) * x_ref[...]) * (zz * _sigmoid(zz))
        rstd = lax.rsqrt(jnp.mean(y * y, axis=0, keepdims=True) + NORM_EPS)
        o_ref[...] = (y * rstd * _lanes(w_ref[...], reps)).astype(o_ref.dtype)

    blk = pl.BlockSpec((512, tb), lambda g, i: (g, i))
    par = pl.BlockSpec((512, 128), lambda g, i: (g, 0))
    return pl.pallas_call(
        body, name="tail_fwd", grid=(SSD_GROUPS, t // tb), in_specs=[blk, blk, blk, par, par], out_specs=blk,
        out_shape=jax.ShapeDtypeStruct((D_INNER, t), bf16),
        compiler_params=_cparams(("parallel", "parallel")))(y_scan, xbc_ct, z_t, skip_b, nw_b)


def tail_bwd_t(dyn_t, y_scan, xbc_ct, z_t, skip_b, nw_b, tb=512):
    t = y_scan.shape[1]
    reps = tb // 128

    def body(g_ref, ys_ref, x_ref, z_ref, d_ref, w_ref, dy_ref, dz_ref, dw_ref, dd_ref):
        zz = z_ref[...]
        sg = _sigmoid(zz)
        sl = zz * sg
        x = x_ref[...]
        y = ys_ref[...] + _lanes(d_ref[...], reps) * x
        yz = y * sl
        rstd = lax.rsqrt(jnp.mean(yz * yz, axis=0, keepdims=True) + NORM_EPS)
        yhat = yz * rstd
        g = g_ref[...]
        dyhat = g * _lanes(w_ref[...], reps)
        dyz = rstd * (dyhat - yhat * jnp.mean(dyhat * yhat, axis=0, keepdims=True))
        dy = dyz * sl
        dy_ref[...] = dy
        dz_ref[...] = (dyz * y * sg * (1.0 + zz * (1.0 - sg))).astype(dz_ref.dtype)

        def fold(v):
            s = v[:, 0:128]
            for q in range(1, reps):
                s = s + v[:, 128 * q:128 * (q + 1)]
            return s

        @pl.when(pl.program_id(1) == 0)
        def _():
            dw_ref[...] = jnp.zeros_like(dw_ref)
            dd_ref[...] = jnp.zeros_like(dd_ref)

        dw_ref[...] += fold(g * yhat)
        dd_ref[...] += fold(dy * x)

    blk = pl.BlockSpec((512, tb), lambda g, i: (g, i))
    par = pl.BlockSpec((512, 128), lambda g, i: (g, 0))
    return pl.pallas_call(
        body, name="tail_bwd", grid=(SSD_GROUPS, t // tb), in_specs=[blk, blk, blk, blk, par, par],
        out_specs=[blk, blk, par, par],
        out_shape=[jax.ShapeDtypeStruct((D_INNER, t), f32), jax.ShapeDtypeStruct((D_INNER, t), bf16),
                   jax.ShapeDtypeStruct((D_INNER, 128), f32), jax.ShapeDtypeStruct((D_INNER, 128), f32)],
        compiler_params=_cparams(("parallel", "arbitrary")))(dyn_t, y_scan, xbc_ct, z_t, skip_b, nw_b)


def merge_fwd(u_gate, bg_row, y_ssd, y_att, tb=512):
    t = y_ssd.shape[0]

    def body(ga_ref, gb_ref, ba_ref, bb_ref, ys_ref, ya_ref, o_ref):
        o_ref[...] = (_sigmoid(ga_ref[...] + ba_ref[...]) * ys_ref[...]
                      + _sigmoid(gb_ref[...] + bb_ref[...]) * ya_ref[...]).astype(o_ref.dtype)

    blk = pl.BlockSpec((tb, 512), lambda i, j: (i, j))
    blk2 = pl.BlockSpec((tb, 512), lambda i, j: (i, 2 + j))
    row = pl.BlockSpec((1, 512), lambda i, j: (0, j))
    row2 = pl.BlockSpec((1, 512), lambda i, j: (0, 2 + j))
    return pl.pallas_call(
        body, name="merge_fwd", grid=(t // tb, 2), in_specs=[blk, blk2, row, row2, blk, blk], out_specs=blk,
        out_shape=jax.ShapeDtypeStruct((t, D_MODEL), bf16),
        compiler_params=_cparams(("parallel", "parallel")))(u_gate, u_gate, bg_row, bg_row, y_ssd, y_att)


def merge_bwd(dm, u_gate, bg_row, y_ssd, y_att, tb=512):
    t = dm.shape[0]

    def body(dm_ref, ga_ref, gb_ref, ba_ref, bb_ref, ys_ref, ya_ref, dys_ref, dya_ref, dga_ref, dgb_ref, dba_ref, dbb_ref):
        d = dm_ref[...]
        sa = _sigmoid(ga_ref[...] + ba_ref[...])
        sb = _sigmoid(gb_ref[...] + bb_ref[...])
        dys_ref[...] = (d * sa).astype(dys_ref.dtype)
        dya_ref[...] = (d * sb).astype(dya_ref.dtype)
        dla = d * ys_ref[...] * sa * (1.0 - sa)
        dlb = d * ya_ref[...] * sb * (1.0 - sb)
        dga_ref[...] = dla.astype(dga_ref.dtype)
        dgb_ref[...] = dlb.astype(dgb_ref.dtype)

        @pl.when(pl.program_id(1) == 0)
        def _():
            dba_ref[...] = jnp.zeros_like(dba_ref)
            dbb_ref[...] = jnp.zeros_like(dbb_ref)

        dba_ref[...] += jnp.sum(dla, axis=0, keepdims=True)
        dbb_ref[...] += jnp.sum(dlb, axis=0, keepdims=True)

    blk = pl.BlockSpec((tb, 512), lambda j, i: (i, j))
    blk2 = pl.BlockSpec((tb, 512), lambda j, i: (i, 2 + j))
    row = pl.BlockSpec((1, 512), lambda j, i: (0, j))
    row2 = pl.BlockSpec((1, 512), lambda j, i: (0, 2 + j))
    act = jax.ShapeDtypeStruct((t, D_MODEL), bf16)
    vec = jax.ShapeDtypeStruct((1, D_MODEL), f32)
    return pl.pallas_call(
        body, name="merge_bwd", grid=(2, t // tb), in_specs=[blk, blk, blk2, row, row2, blk, blk],
        out_specs=[blk, blk, blk, blk, row, row], out_shape=[act, act, act, act, vec, vec],
        compiler_params=_cparams(("parallel", "arbitrary")))(dm, u_gate, u_gate, bg_row, bg_row, y_ssd, y_att)


def _ln_stats(r):
    mu = jnp.mean(r, axis=1, keepdims=True)
    xc = r - mu
    rstd = lax.rsqrt(jnp.mean(xc * xc, axis=1, keepdims=True) + NORM_EPS)
    return xc * rstd, rstd


def _ln_bwd(dy, xhat, rstd, g_row):
    dxh = dy * g_row
    return rstd * (dxh - jnp.mean(dxh, axis=1, keepdims=True) - xhat * jnp.mean(dxh * xhat, axis=1, keepdims=True))


def ln1_fwd(x, mix, g_row, b_row, tb=512):
    t = x.shape[0]

    def body(x_ref, m_ref, g_ref, b_ref, o_ref, ob_ref):
        xhat, _ = _ln_stats(ALPHA * x_ref[...] + m_ref[...])
        h = xhat * g_ref[...] + b_ref[...]
        o_ref[...] = h
        ob_ref[...] = h.astype(ob_ref.dtype)

    blk = pl.BlockSpec((tb, D_MODEL), lambda i: (i, 0))
    row = pl.BlockSpec((1, D_MODEL), lambda i: (0, 0))
    return pl.pallas_call(body, name="ln1_fwd", grid=(t // tb,), in_specs=[blk, blk, row, row], out_specs=[blk, blk],
                          out_shape=[jax.ShapeDtypeStruct((t, D_MODEL), f32), jax.ShapeDtypeStruct((t, D_MODEL), bf16)],
                          compiler_params=_cparams(("parallel",)))(x, mix, g_row, b_row)


def ln1_bwd(dh, x, mix, g_row, tb=512):
    t = x.shape[0]

    def body(dh_ref, x_ref, m_ref, g_ref, dr_ref, drb_ref, dg_ref, db_ref):
        xhat, rstd = _ln_stats(ALPHA * x_ref[...] + m_ref[...])
        dy = dh_ref[...]
        dr = _ln_bwd(dy, xhat, rstd, g_ref[...])
        dr_ref[...] = dr
        drb_ref[...] = dr.astype(drb_ref.dtype)

        @pl.when(pl.program_id(0) == 0)
        def _():
            dg_ref[...] = jnp.zeros_like(dg_ref)
            db_ref[...] = jnp.zeros_like(db_ref)

        dg_ref[...] += jnp.sum(dy * xhat, axis=0, keepdims=True)
        db_ref[...] += jnp.sum(dy, axis=0, keepdims=True)

    blk = pl.BlockSpec((tb, D_MODEL), lambda i: (i, 0))
    row = pl.BlockSpec((1, D_MODEL), lambda i: (0, 0))
    return pl.pallas_call(
        body, name="ln1_bwd", grid=(t // tb,), in_specs=[blk, blk, blk, row], out_specs=[blk, blk, row, row],
        out_shape=[jax.ShapeDtypeStruct((t, D_MODEL), f32), jax.ShapeDtypeStruct((t, D_MODEL), bf16),
                   jax.ShapeDtypeStruct((1, D_MODEL), f32), jax.ShapeDtypeStruct((1, D_MODEL), f32)],
        compiler_params=_cparams(("arbitrary",)))(dh, x, mix, g_row)


def ln2_loss(h1, f, g_row, b_row, target, tb=512):
    t = h1.shape[0]

    def body(h_ref, f_ref, g_ref, b_ref, t_ref, dr_ref, drb_ref, dg_ref, db_ref, loss_ref):
        xhat, rstd = _ln_stats(ALPHA * h_ref[...] + f_ref[...])
        g = g_ref[...]
        err = xhat * g + b_ref[...] - t_ref[...]
        dy = err * (1.0 / D_MODEL)
        dr = _ln_bwd(dy, xhat, rstd, g)
        dr_ref[...] = dr
        drb_ref[...] = dr.astype(drb_ref.dtype)

        @pl.when(pl.program_id(0) == 0)
        def _():
            dg_ref[...] = jnp.zeros_like(dg_ref)
            db_ref[...] = jnp.zeros_like(db_ref)
            loss_ref[...] = jnp.zeros_like(loss_ref)

        dg_ref[...] += jnp.sum(dy * xhat, axis=0, keepdims=True)
        db_ref[...] += jnp.sum(dy, axis=0, keepdims=True)
        part = jnp.sum(jnp.mean(err * err, axis=1, keepdims=True), axis=0, keepdims=True)
        loss_ref[...] += 0.5 * part

    blk = pl.BlockSpec((tb, D_MODEL), lambda i: (i, 0))
    row = pl.BlockSpec((1, D_MODEL), lambda i: (0, 0))
    return pl.pallas_call(
        body, name="ln2_loss", grid=(t // tb,), in_specs=[blk, blk, row, row, blk],
        out_specs=[blk, blk, row, row, pl.BlockSpec((8, 128), lambda i: (0, 0))],
        out_shape=[jax.ShapeDtypeStruct((t, D_MODEL), f32), jax.ShapeDtypeStruct((t, D_MODEL), bf16),
                   jax.ShapeDtypeStruct((1, D_MODEL), f32), jax.ShapeDtypeStruct((1, D_MODEL), f32),
                   jax.ShapeDtypeStruct((8, 128), f32)],
        compiler_params=_cparams(("arbitrary",)))(h1, f, g_row, b_row, target)


TAIL_BLOCK, TAIL_AT = divmod(OFF_TAIL, PACK_TILE)


def _sum4(ref):
    return ((ref[0].astype(f32) + ref[1].astype(f32)) + ref[2].astype(f32)) + ref[3].astype(f32)


def _adamw_update(g, w_ref, m_ref, v_ref, g_ref, d_ref, nm_ref, nv_ref):
    c1 = 1.0 - ADAM_B1 ** ADAM_STEP
    c2 = 1.0 - ADAM_B2 ** ADAM_STEP
    nm = ADAM_B1 * m_ref[...] + (1.0 - ADAM_B1) * g
    nv = ADAM_B2 * v_ref[...] + (1.0 - ADAM_B2) * (g * g)
    g_ref[...] = g
    nm_ref[...] = nm
    nv_ref[...] = nv
    d_ref[...] = -ADAM_LR * ((nm / c1) / (jnp.sqrt(nv / c2) + ADAM_EPS) + ADAM_WD * w_ref[...])


def adamw_early(landed, parts, me, w, m, v):
    off = LATE_ROWS // EARLY_TILE

    def body(me_ref, *refs):
        src = refs[0:N_DEV]
        own_ref, w_ref, m_ref, v_ref = refs[N_DEV:N_DEV + 4]
        mine = me_ref[0]
        g = None
        for s in range(N_DEV):
            term = jnp.where(mine == s, own_ref[0], src[s][0])
            g = term if g is None else g + term
        _adamw_update(g, w_ref, m_ref, v_ref, *refs[N_DEV + 4:])

    def slot(s):
        return pl.BlockSpec((1, EARLY_TILE, 1024), lambda i, me_ref: (jnp.where(me_ref[0] == s, (s + 1) % N_DEV, s), i, 0))

    shard = pl.BlockSpec((EARLY_TILE, 1024), lambda i, me_ref: (i + off, 0))
    out_blk = pl.BlockSpec((EARLY_TILE, 1024), lambda i, me_ref: (i, 0))
    grid_spec = pltpu.PrefetchScalarGridSpec(
        num_scalar_prefetch=1, grid=(EARLY_ROWS // EARLY_TILE,),
        in_specs=[slot(s) for s in range(N_DEV)]
        + [pl.BlockSpec((1, EARLY_TILE, 1024), lambda i, me_ref: (me_ref[0], i, 0)), shard, shard, shard],
        out_specs=[out_blk] * 4)
    out = jax.ShapeDtypeStruct((EARLY_ROWS, 1024), f32)
    return pl.pallas_call(body, name="adamw_early", grid_spec=grid_spec, out_shape=[out] * 4,
                          compiler_params=_cparams(("parallel",)))(me, *([landed] * N_DEV), parts, w, m, v)


def adamw(parts, tails, w, m, v):
    rows = parts.shape[1]

    def body(p_ref, t_ref, w_ref, m_ref, v_ref, g_ref, d_ref, nm_ref, nv_ref):
        g = _sum4(p_ref)
        with_tail = jnp.concatenate([g[0:TAIL_AT], _sum4(t_ref), g[TAIL_AT + ROWS_TAIL:]], axis=0)
        g = jnp.where(pl.program_id(0) == TAIL_BLOCK, with_tail, g)
        _adamw_update(g, w_ref, m_ref, v_ref, g_ref, d_ref, nm_ref, nv_ref)

    blk = pl.BlockSpec((PACK_TILE, 1024), lambda i: (i, 0))
    out = jax.ShapeDtypeStruct((rows, 1024), f32)
    return pl.pallas_call(
        body, name="adamw", grid=(rows // PACK_TILE,),
        in_specs=[pl.BlockSpec((4, PACK_TILE, 1024), lambda i: (0, i, 0)),
                  pl.BlockSpec((4, ROWS_TAIL, 1024), lambda i: (0, 0, 0)), blk, blk, blk], out_specs=[blk] * 4,
        out_shape=[out] * 4, compiler_params=_cparams(("parallel",)))(parts, tails, w, m, v)


def pair_sum(parts, recv, core):
    rows = parts.shape[1]

    def body(c_ref, a_ref, b_ref, o_ref, t_ref):
        s = a_ref[...] + b_ref[...]
        o_ref[...] = s.astype(o_ref.dtype)

        @pl.when(pl.program_id(1) == TAIL_BLOCK)
        def _():
            t_ref[...] = s[:, TAIL_AT:TAIL_AT + ROWS_TAIL]

    grid_spec = pltpu.PrefetchScalarGridSpec(
        num_scalar_prefetch=1, grid=(4, rows // PACK_TILE),
        in_specs=[pl.BlockSpec((1, PACK_TILE, 1024), lambda j, i, c_ref: (2 * j + c_ref[0], i, 0)),
                  pl.BlockSpec((1, PACK_TILE, 1024), lambda j, i, c_ref: (j, i, 0))],
        out_specs=[pl.BlockSpec((1, PACK_TILE, 1024), lambda j, i, c_ref: (j, i, 0)),
                   pl.BlockSpec((1, ROWS_TAIL, 1024), lambda j, i, c_ref: (j, 0, 0))])
    return pl.pallas_call(
        body, name="pair_sum", grid_spec=grid_spec,
        out_shape=[jax.ShapeDtypeStruct(recv.shape, bf16), jax.ShapeDtypeStruct((4, ROWS_TAIL, 1024), f32)],
        compiler_params=_cparams(("parallel", "arbitrary")))(core, parts, recv)


def _place():
    return lax.axis_index("x"), lax.axis_index("y"), lax.axis_index("c")


def all_gather_blocks(shard):
    rows, cols = shard.shape

    def body(x_ref, out_ref, send_sems, recv_sems, local_sem):
        x, y, c = _place()
        me, sibling = (x, y, c), (x, y, 1 - c)
        chips = [(1 - x, y), (x, 1 - y), (1 - x, 1 - y)]

        def slot(px, py, pc):
            return out_ref.at[4 * px + 2 * py + pc]

        def copy(k, block, to, src=None):
            return pltpu.make_async_remote_copy(
                src_ref=slot(*block) if src is None else src, dst_ref=slot(*block), send_sem=send_sems.at[k],
                recv_sem=recv_sems.at[k], device_id=to, device_id_type=MESH)

        mine = pltpu.make_async_copy(x_ref, slot(*me), local_sem)
        mine.start()
        first = [copy(0, me, sibling, src=x_ref)]
        first += [copy(1 + j, me, (*chip, c), src=x_ref) for j, chip in enumerate(chips)]
        for cp in first:
            cp.start()
        passed = [copy(4 + j, (*chip, c), sibling) for j, chip in enumerate(chips)]
        for j, chip in enumerate(chips):
            copy(1 + j, (*chip, c), me).wait_recv()
            passed[j].start()
        copy(0, sibling, me).wait_recv()
        for j, chip in enumerate(chips):
            copy(4 + j, (*chip, 1 - c), me).wait_recv()
        for cp in first + passed:
            cp.wait_send()
        mine.wait()

    return pl.pallas_call(
        body, name="all_gather_blocks", out_shape=jax.ShapeDtypeStruct((N_DEV, rows, cols), shard.dtype),
        in_specs=[pl.BlockSpec(memory_space=pl.ANY)], out_specs=pl.BlockSpec(memory_space=pl.ANY),
        scratch_shapes=[pltpu.SemaphoreType.DMA((7,)), pltpu.SemaphoreType.DMA((7,)), pltpu.SemaphoreType.DMA],
        compiler_params=pltpu.CompilerParams(has_side_effects=True))(shard)


def pair_exchange(parts):
    _, rows, cols = parts.shape

    def body(p_ref, recv_ref, send_sems, recv_sems):
        x, y, c = _place()
        copies = [pltpu.make_async_remote_copy(
            src_ref=p_ref.at[2 * j + 1 - c], dst_ref=recv_ref.at[j], send_sem=send_sems.at[j], recv_sem=recv_sems.at[j],
            device_id=(x, y, 1 - c), device_id_type=MESH) for j in range(4)]
        for cp in copies:
            cp.start()
        for cp in copies:
            cp.wait_recv()
        for cp in copies:
            cp.wait_send()

    return pl.pallas_call(
        body, name="pair_exchange", out_shape=jax.ShapeDtypeStruct((4, rows, cols), parts.dtype),
        in_specs=[pl.BlockSpec(memory_space=pl.ANY)], out_specs=pl.BlockSpec(memory_space=pl.ANY),
        scratch_shapes=[pltpu.SemaphoreType.DMA((4,)), pltpu.SemaphoreType.DMA((4,))],
        compiler_params=pltpu.CompilerParams(has_side_effects=True))(parts)


def chip_exchange(parts):
    n = len(parts)

    def body(*refs):
        p_refs, out_refs = refs[0:n], refs[n:2 * n]
        send_sems, recv_sems, local_sems = refs[2 * n:]
        x, y, c = _place()
        mine = 2 * x + y
        flips = [(x, 1 - y), (1 - x, y), (1 - x, 1 - y)]

        def copy(a, k, src_slot, dst_slot):
            px, py = flips[k]
            return pltpu.make_async_remote_copy(
                src_ref=p_refs[a].at[src_slot], dst_ref=out_refs[a].at[dst_slot], send_sem=send_sems.at[3 * a + k],
                recv_sem=recv_sems.at[3 * a + k], device_id=(px, py, c), device_id_type=MESH)

        local = [pltpu.make_async_copy(p_refs[a].at[mine], out_refs[a].at[mine], local_sems.at[a]) for a in range(n)]
        sends = [copy(a, k, 2 * flips[k][0] + flips[k][1], mine) for a in range(n) for k in range(3)]
        for cp in local + sends:
            cp.start()
        for a in range(n):
            for k in range(3):
                copy(a, k, mine, 2 * flips[k][0] + flips[k][1]).wait_recv()
        for cp in sends:
            cp.wait_send()
        for cp in local:
            cp.wait()

    return pl.pallas_call(
        body, name="chip_exchange", out_shape=[jax.ShapeDtypeStruct(p.shape, p.dtype) for p in parts],
        in_specs=[pl.BlockSpec(memory_space=pl.ANY)] * n, out_specs=[pl.BlockSpec(memory_space=pl.ANY)] * n,
        scratch_shapes=[pltpu.SemaphoreType.DMA((3 * n,)), pltpu.SemaphoreType.DMA((3 * n,)), pltpu.SemaphoreType.DMA((n,))],
        compiler_params=pltpu.CompilerParams(has_side_effects=True))(*parts)


_HBM = pl.BlockSpec(memory_space=pltpu.HBM)
_SEM = pl.BlockSpec(memory_space=pltpu.SEMAPHORE)


def _peer(k):
    x, y, c = _place()
    px, py, pc = (1 - x if k & 4 else x), (1 - y if k & 2 else y), (1 - c if k & 1 else c)
    return (px, py, pc), 4 * px + 2 * py + pc


def scatter_start(parts, name):
    per_device = parts.ndim == 3

    def body(p_ref, land_ref, send_sems, recv_sems, p_thru, land_thru, token):
        x, y, c = _place()
        me = 4 * x + 2 * y + c
        for k in range(1, N_DEV):
            place, idx = _peer(k)
            pltpu.make_async_remote_copy(src_ref=p_ref.at[idx] if per_device else p_ref, dst_ref=land_ref.at[me],
                                         send_sem=send_sems.at[k - 1], recv_sem=recv_sems.at[k - 1], device_id=place,
                                         device_id_type=MESH).start()
        token[...] = jnp.zeros_like(token)

    land_shape = parts.shape if per_device else (N_DEV,) + parts.shape
    landing = lax.empty(land_shape, parts.dtype)
    return pl.pallas_call(
        body, name=name,
        out_shape=(pltpu.SemaphoreType.DMA((N_DEV - 1,)), pltpu.SemaphoreType.DMA((N_DEV - 1,)),
                   pltpu.HBM(parts.shape, parts.dtype), pltpu.HBM(land_shape, parts.dtype),
                   jax.ShapeDtypeStruct((8, 128), f32)),
        in_specs=(_HBM, _HBM), out_specs=(_SEM, _SEM, _HBM, _HBM, pl.BlockSpec(memory_space=pltpu.VMEM)),
        input_output_aliases={0: 2, 1: 3},
        compiler_params=pltpu.CompilerParams(has_side_effects=pltpu.SideEffectType.DATAFLOW_SIDE_EFFECTING),
    )(pltpu.with_memory_space_constraint(parts, pltpu.HBM), pltpu.with_memory_space_constraint(landing, pltpu.HBM))


def scatter_wait(send_sems, recv_sems, parts_thru, land_thru, after, name):
    per_device = parts_thru.ndim == 3

    def body(p_ref, land_ref, send_sems, recv_sems, after_ref, p_out, land_out):
        for k in range(1, N_DEV):
            place, idx = _peer(k)
            copy = pltpu.make_async_remote_copy(src_ref=p_ref.at[idx] if per_device else p_ref, dst_ref=land_ref.at[idx],
                                                send_sem=send_sems.at[k - 1], recv_sem=recv_sems.at[k - 1],
                                                device_id=place, device_id_type=MESH)
            copy.wait_send()
            copy.wait_recv()

    return pl.pallas_call(
        body, name=name,
        out_shape=(pltpu.HBM(parts_thru.shape, parts_thru.dtype), pltpu.HBM(land_thru.shape, land_thru.dtype)),
        in_specs=(_HBM, _HBM, _SEM, _SEM, pl.BlockSpec(memory_space=pl.ANY)), out_specs=(_HBM, _HBM),
        input_output_aliases={0: 0, 1: 1},
        compiler_params=pltpu.CompilerParams(has_side_effects=pltpu.SideEffectType.DATAFLOW_SIDE_EFFECTING),
    )(parts_thru, land_thru, send_sems, recv_sems, after)


def _tail_rows(conv_part, small, extra):
    lead = conv_part.shape[:-1]
    rep = jnp.concatenate([small[n].reshape(-1).astype(f32) for n in SMALL] + [extra.reshape(1).astype(f32)])
    flat = jnp.concatenate([conv_part, jnp.broadcast_to(rep, lead + rep.shape),
                            jnp.zeros(lead + (ROWS_TAIL * 1024 - TAIL_ELEMS,), f32)], axis=-1)
    return flat.reshape(lead + (ROWS_TAIL, 1024))


def _late_rows(w_in_t, tail):
    lead = tail.shape[:-2]
    zeros = lambda r: jnp.zeros(lead + (r, 1024), f32)
    return jnp.concatenate([w_in_t, zeros(OFF_TAIL - IN_SHARD), tail, zeros(LATE_ROWS - OFF_TAIL - ROWS_TAIL)], axis=-2)


def _early_rows(w_ps, w_out, w_up_t, w_down, w_pa_t):
    return jnp.concatenate([w_ps, w_out, w_up_t, w_down, w_pa_t.reshape(w_pa_t.shape[:-2] + (ROWS_PA, 1024))], axis=-2)


def _pack_shard(vals):
    tail = _tail_rows(vals["conv_w"].reshape(-1), vals, jnp.zeros((), f32))
    return jnp.concatenate([_late_rows(vals["w_in"].T, tail),
                            _early_rows(vals["w_proj_ssd"], vals["w_out"], vals["w_up"].T, vals["w_down"],
                                        vals["w_proj_attn"].T)], axis=0)


def _unpack_shard(late, early):
    e = lambda lo, hi: early[lo - LATE_ROWS:hi - LATE_ROWS]
    out = {"w_in": late[0:IN_SHARD].T, "w_proj_ssd": e(OFF_PS, OFF_OUT), "w_out": e(OFF_OUT, OFF_UP),
           "w_up": e(OFF_UP, OFF_DOWN).T, "w_down": e(OFF_DOWN, OFF_PA),
           "w_proj_attn": e(OFF_PA, PACK_ROWS).reshape(D_MODEL // N_DEV, ATTN_OUT).T}
    flat = late[OFF_TAIL:OFF_TAIL + ROWS_TAIL].reshape(-1)
    out["conv_w"] = flat[0:CONV_SHARD].reshape(D_CONV, CONV_DIM // N_DEV)
    off = CONV_SHARD
    for n in SMALL:
        out[n] = flat[off:off + SMALL_SIZES[n]]
        off += SMALL_SIZES[n]
    out["_extra"] = flat[off]
    return out


def _blocks(g):
    return g.reshape(N_DEV, g.shape[0] // N_DEV, g.shape[1])


def _pack_early_parts(full):
    return _early_rows(_blocks(full["w_proj_ssd"]), _blocks(full["w_out"]), _blocks(full["w_up_t"]),
                       _blocks(full["w_down"]), _blocks(full["w_proj_attn_t"]))


def _pack_late_parts(full, small, extra):
    conv = full["conv_w"].reshape(D_CONV, N_DEV, CONV_DIM // N_DEV).transpose(1, 0, 2).reshape(N_DEV, CONV_SHARD)
    return _late_rows(_blocks(full["w_in_t"]), _tail_rows(conv, small, extra))


def _gather_weights(w):
    conv_bits = lax.bitcast_convert_type(w["conv_w"], bf16).reshape(-1)
    conv_rows = jnp.concatenate([conv_bits, jnp.zeros((16 * 1024 - 2 * CONV_SHARD,), bf16)]).reshape(16, 1024)
    packed = _pack_shard(w)
    first = OFF_TAIL + ROWS_TAIL
    got = all_gather_blocks(jnp.concatenate([packed[0:OFF_TAIL].astype(bf16), conv_rows], axis=0))
    got, rest = lax.optimization_barrier((got, packed[first:].astype(bf16)))
    send_sems, recv_sems, rest_thru, land_thru, token = scatter_start(rest, "gather_start")
    conv =lax.bitcast_convert_type(got[:, OFF_TAIL:OFF_TAIL + 4].reshape(N_DEV, 4096)[:, 0:2 * CONV_SHARD]
                                    .reshape(N_DEV, D_CONV, CONV_DIM // N_DEV, 2), f32)
    now = {"w_in_t": got[:, 0:IN_SHARD].reshape(IN_COLS, 1024), "conv_w": conv.transpose(1, 0, 2).reshape(D_CONV, CONV_DIM)}

    def later(after):
        mine, landed = scatter_wait(send_sems, recv_sems, rest_thru, land_thru, after, "gather_wait")
        x, y, c = _place()
        landed = lax.dynamic_update_slice(landed, mine[None], (4 * x + 2 * y + c, 0, 0))
        whole = lambda lo, hi: landed[:, lo - first:hi - first].reshape(N_DEV * (hi - lo), 1024)
        return {"w_proj_ssd": whole(OFF_PS, OFF_OUT), "w_out": whole(OFF_OUT, OFF_UP), "w_up_t": whole(OFF_UP, OFF_DOWN),
                "w_down": whole(OFF_DOWN, OFF_PA),
                "w_proj_attn_t": landed[:, OFF_PA - first:PACK_ROWS - first].reshape(D_MODEL, ATTN_OUT)}

    return now, later, token


def _row(v, width=None):
    v = v.reshape(1, -1).astype(f32)
    return v if width is None else jnp.pad(v, ((0, 0), (0, width - v.shape[1])))


def _lanes256(vf, vb):
    z = jnp.zeros((96,), f32)
    return jnp.concatenate([vf.astype(f32), z, vb.astype(f32), z]).reshape(1, 256)


def _local_step(x2, tgt, wf, p, send_early=None, late_weights=None, start_token=None):
    t = x2.shape[0]
    o = np.cumsum((0,) + IN_SPLITS)
    wt = wf["w_in_t"]
    wt_z, wt_xbc, wt_dt = wt[o[0]:o[1]], wt[o[1]:o[2]], wt[o[2]:o[4]]
    wt_qkv, wt_gate = wt[o[4]:o[7]], wt[o[7]:o[8]]

    spread = lambda v: jnp.broadcast_to(v.astype(f32)[..., None], v.shape + (128,))
    conv_w_b, conv_b_b = spread(wf["conv_w"]), spread(p["conv_b"])
    dt_bias_b = spread(jnp.concatenate([p["dt_bias_f"], p["dt_bias_b"]]))
    a_f, a_b = -jnp.exp(p["a_log_f"].astype(f32)), -jnp.exp(p["a_log_b"].astype(f32))
    a_coef_b = spread(jnp.concatenate([a_f, a_b]))
    skip_b = spread(jnp.repeat(p["d_skip"], SSD_HEAD_DIM))
    nw_b, bg_row = spread(p["ssd_norm_w"]), _row(p["b_gate"])
    g1, b1, g2, b2 = _row(p["ln1_g"]), _row(p["ln1_b"]), _row(p["ln2_g"]), _row(p["ln2_b"])

    xb = (x2 if start_token is None else x2 + start_token[0, 0]).astype(MXU_DTYPE)
    xt = xb.T
    u_z = mm_nn(wt_z, xt, "in_z")
    u_xbc = mm_nn(wt_xbc, xt, "in_xbc")
    u_dt = mm_nn(wt_dt, xt, "in_dt")
    u_qkv = mm_nt_split(xb, wt_qkv, "in_qkv", 256)
    u_gate = mm_nt(xb, wt_gate, "in_gate")
    xbc_c = conv_fwd_t(u_xbc, conv_w_b, conv_b_b)
    dt_t = dt_fwd_t(u_dt, dt_bias_b)
    y_f, h_f = ssd_fwd_t(xbc_c, dt_t, a_coef_b, False, "ssd_fwd_f")
    y_scan, h_b = ssd_fwd_t(xbc_c, dt_t, a_coef_b, True, "ssd_fwd_b", prev=y_f)
    yn = tail_fwd_t(y_scan, xbc_c, u_z, skip_b, nw_b)
    if late_weights is not None:
        wf = {**wf, **late_weights(yn)}
    y_ssd = mm_tn(yn, wf["w_proj_ssd"], "proj_ssd")

    def strided(a, dil):
        return a.reshape(t // dil, dil * 256)

    qkv, outs, lses = [], [], []
    for pi, (_, dil) in enumerate(DIL_PATTERNS):
        q, k, v = (strided(u_qkv[N_PATTERNS * s + pi], dil) for s in range(3))
        qkv.append((q, k, v))
        op, lp = attn_fwd(q, k, v, pi, dil, f"attn_fwd_{pi}")
        outs.append(op.reshape(t, 256))
        lses.append(lp.reshape(t, 256))
    ya, lse = attn_combine(outs, lses)
    y_att = mm_nt(ya, wf["w_proj_attn_t"], "proj_attn")
    m = merge_fwd(u_gate, bg_row, y_ssd, y_att)
    mix = mm_nn(m, wf["w_out"], "out_proj")
    h1, h1b = ln1_fwd(x2, mix, g1, b1)
    r_up, p_act = mm_nt(h1b, wf["w_up_t"], "mlp_up", relu2=True)
    f_dn = mm_nn(p_act, wf["w_down"], "mlp_down")
    dr2, dr2b, dg2, db2, loss8 = ln2_loss(h1, f_dn, g2, b2, tgt)

    full, small = {}, {}
    da = mm_nt(dr2b, wf["w_down"], "d_mlp_act", out_dtype=bf16, relu2_of=r_up)
    full["w_down"] = mm_tn(p_act, dr2b, "dw_down")
    full["w_up_t"] = mm_tn(da, h1b, "dw_up")
    dh1 = mm_nn(da, wf["w_up_t"], "d_h1", acc_in=dr2, acc_scale=ALPHA)
    dr1, dr1b, dg1, db1 = ln1_bwd(dh1, x2, mix, g1)
    dm = mm_nt(dr1b, wf["w_out"], "d_merge")
    full["w_out"] = mm_tn(m, dr1b, "dw_out")
    dys, dya_p, dga, dgb, dba, dbb = merge_bwd(dm, u_gate, bg_row, y_ssd, y_att)
    dyn = mm_nt(wf["w_proj_ssd"], dys, "d_yn")
    full["w_proj_ssd"] = mm_nn(yn, dys, "dw_proj_ssd")
    dya = mm_nn(dya_p, wf["w_proj_attn_t"], "d_ya")
    full["w_proj_attn_t"] = mm_tn(dya_p, ya, "dw_proj_attn")
    if send_early is not None:
        skip_b = skip_b + send_early(full)[0, 0]

    dy, dz, dnw, ddx = tail_bwd_t(dyn, y_scan, xbc_c, u_z, skip_b, nw_b)
    dxf, dbf, dcf, ddtf, daf = ssd_bwd_t(xbc_c, dt_t, a_coef_b, dy, h_f, False, "ssd_bwd_f", skip_b=skip_b)
    dxs, dbs, dcs, ddtb, dab = ssd_bwd_t(xbc_c, dt_t, a_coef_b, dy, h_b, True, "ssd_bwd_b", prev=(dxf, dbf, dcf))
    du_xbc, dcw_x, dcb_x = conv_bwd_t(u_xbc, dxs, conv_w_b, conv_b_b, None, "conv_bwd_x", 0)
    du_xbc, dcw_b, dcb_b = conv_bwd_t(u_xbc, dbs, conv_w_b, conv_b_b, du_xbc, "conv_bwd_b", D_INNER)
    du_xbc, dcw_c, dcb_c = conv_bwd_t(u_xbc, dcs, conv_w_b, conv_b_b, du_xbc, "conv_bwd_c", D_INNER + 512)
    du_dt, dbias = dt_bwd_t(ddtf, ddtb, u_dt, dt_bias_b)

    delta = attn_delta(dya, ya)
    dqs, dks, dvs = [], [], []
    for pi, (_, dil) in enumerate(DIL_PATTERNS):
        q, k, v = qkv[pi]
        sd, sl_, sdel = strided(dya, dil), strided(lse, dil), strided(delta, dil)
        dqs.append(attn_dq(q, k, v, sd, sl_, sdel, pi, dil, f"attn_dq_{pi}").reshape(t, 256))
        dk, dv = attn_dkv(q, k, v, sd, sl_, sdel, pi, dil, f"attn_dkv_{pi}")
        dks.append(dk.reshape(t, 256))
        dvs.append(dv.reshape(t, 256))
    du_qkv = jnp.concatenate(dqs + dks + dvs, axis=1)
    du_gate = jnp.concatenate([dga, dgb], axis=1)

    dx = mm_tn(dz, wt_z, "dx_z", acc_in=dr1, acc_scale=ALPHA)
    dx = mm_tn(du_xbc, wt_xbc, "dx_xbc", acc_in=dx)
    dx = mm_tn(du_dt, wt_dt, "dx_dt", acc_in=dx)
    dx = mm_nn(du_qkv, wt_qkv, "dx_qkv", acc_in=dx)
    dx = mm_nn(du_gate, wt_gate, "dx_gate", acc_in=dx)
    full["w_in_t"] = jnp.concatenate(
        [mm_nn(dz, xb, "dw_in_z"), mm_nn(du_xbc, xb, "dw_in_xbc"), mm_nn(du_dt, xb, "dw_in_dt"),
         mm_tn(du_qkv, xb, "dw_in_qkv"), mm_tn(du_gate, xb, "dw_in_gate")], axis=0)
    lanes = lambda v: jnp.sum(v, axis=-1)
    full["conv_w"] = jnp.concatenate([lanes(dcw_x), lanes(dcw_b), lanes(dcw_c)], axis=1)

    small["b_gate"] = jnp.concatenate([dba, dbb], axis=1)
    small["conv_b"] = jnp.concatenate([lanes(dcb_x), lanes(dcb_b), lanes(dcb_c)])
    dbias = lanes(dbias)
    small["dt_bias_f"], small["dt_bias_b"] = dbias[0:32], dbias[32:64]
    small["a_log_f"] = lanes(daf) * a_f
    small["a_log_b"] = lanes(dab) * a_b
    small["d_skip"] = jnp.sum(lanes(ddx).reshape(SSD_HEADS, SSD_HEAD_DIM), axis=1)
    small["ssd_norm_w"] = lanes(dnw)
    small["ln1_g"], small["ln1_b"], small["ln2_g"], small["ln2_b"] = dg1, db1, dg2, db2
    return loss8[0, 0], dx, full, small


def kernel(x, w_in, b_gate, conv_w, conv_b, dt_bias_f, dt_bias_b, a_log_f, a_log_b, d_skip, ssd_norm_w, w_proj_ssd, w_proj_attn, w_out, ln1_g, ln1_b, w_up, w_down, ln2_g, ln2_b, loss_target, m_w_in, m_b_gate, m_conv_w, m_conv_b, m_dt_bias_f, m_dt_bias_b, m_a_log_f, m_a_log_b, m_d_skip, m_ssd_norm_w, m_w_proj_ssd, m_w_proj_attn, m_w_out, m_ln1_g, m_ln1_b, m_w_up, m_w_down, m_ln2_g, m_ln2_b, v_w_in, v_b_gate, v_conv_w, v_conv_b, v_dt_bias_f, v_dt_bias_b, v_a_log_f, v_a_log_b, v_d_skip, v_ssd_norm_w, v_w_proj_ssd, v_w_proj_attn, v_w_out, v_ln1_g, v_ln1_b, v_w_up, v_w_down, v_ln2_g, v_ln2_b):
    given = dict(locals())
    w = {n: given[n] for n in WEIGHTS}
    mom = {n: given["m_" + n] for n in WEIGHTS}
    var = {n: given["v_" + n] for n in WEIGHTS}
    t = x.shape[1]
    wf, late_weights, start_token = _gather_weights(w)
    in_flight = []

    def send_early(full):
        send_sems, recv_sems, parts_thru, land_thru, token = scatter_start(_pack_early_parts(full), "scatter_start")
        in_flight.append((send_sems, recv_sems, parts_thru, land_thru))
        return token

    loss, dx, full, small = _local_step(x.reshape(t, D_MODEL), loss_target.reshape(t, D_MODEL), wf, w, send_early,
                                        late_weights, start_token)
    late = _pack_late_parts(full, small, loss)
    x_, y_, c_ = _place()
    core = c_.astype(jnp.int32).reshape(1)
    me = (4 * x_ + 2 * y_ + c_).astype(jnp.int32).reshape(1)
    wp, mp, vp = _pack_shard(w), _pack_shard(mom), _pack_shard(var)
    early_parts, landed = scatter_wait(*in_flight[0], late, "scatter_wait")
    early_out = adamw_early(landed, early_parts, me, wp, mp, vp)
    parts, tails = chip_exchange(pair_sum(late, pair_exchange(late), core))
    late_out = adamw(parts, tails, wp, mp, vp)
    g, delta, new_m, new_v = (_unpack_shard(a, b) for a, b in zip(late_out, early_out))
    outs = [g["_extra"], dx.reshape(x.shape)]
    for d in (g, delta, new_m, new_v):
        outs += [d[n].reshape(w[n].shape) for n in WEIGHTS]
    return tuple(outs)
```

```python
import functools
import math

import jax
import jax.numpy as jnp
import numpy as np
from jax import lax
from jax.experimental import pallas as pl
from jax.experimental.pallas import tpu as pltpu

f32 = jnp.float32
bf16 = jnp.bfloat16
MXU_DTYPE = jnp.bfloat16

N_DEV = 8
D_MODEL = 1024
D_INNER = 2048
SSD_HEADS = 32
SSD_HEAD_DIM = 64
SSD_GROUPS = 4
D_STATE = 128
D_CONV = 5
CHUNK = 128
CONV_DIM = D_INNER + 2 * SSD_GROUPS * D_STATE
NORM_EPS = 1e-5
ATTN_HEAD_DIM = 64
DIL_PATTERNS = ((128, 1), (512, 4), (2048, 16))
N_PATTERNS = len(DIL_PATTERNS)
HEADS_PER_PATTERN = 4
ATTN_HEADS = 12
ATTN_WIDTH = 768
ATTN_OUT = 256
D_FF = 4096
ALPHA = 2.0 ** 0.25
IN_SPLITS = (D_INNER, CONV_DIM, SSD_HEADS, SSD_HEADS, ATTN_WIDTH, ATTN_WIDTH, ATTN_WIDTH, 2 * D_MODEL)
IN_COLS = sum(IN_SPLITS)
SSD_COLS = sum(IN_SPLITS[0:4])
ADAM_LR, ADAM_B1, ADAM_B2, ADAM_EPS, ADAM_WD, ADAM_STEP = 0.001, 0.9, 0.999, 1e-08, 0.01, 10
NEG_BIG = -1e30
VMEM_LIMIT = 56 * 1024 * 1024
MESH = pl.DeviceIdType.MESH

SMALL = ("b_gate", "conv_b", "dt_bias_f", "dt_bias_b", "a_log_f", "a_log_b", "d_skip", "ssd_norm_w",
         "ln1_g", "ln1_b", "ln2_g", "ln2_b")
WEIGHTS = ("w_in", "b_gate", "conv_w", "conv_b", "dt_bias_f", "dt_bias_b", "a_log_f", "a_log_b", "d_skip",
           "ssd_norm_w", "w_proj_ssd", "w_proj_attn", "w_out", "ln1_g", "ln1_b", "w_up", "w_down", "ln2_g", "ln2_b")
SMALL_SIZES = {"b_gate": 2 * D_MODEL, "conv_b": CONV_DIM, "dt_bias_f": 32, "dt_bias_b": 32, "a_log_f": 32, "a_log_b": 32,
               "d_skip": 32, "ssd_norm_w": D_INNER, "ln1_g": D_MODEL, "ln1_b": D_MODEL, "ln2_g": D_MODEL, "ln2_b": D_MODEL}
IN_SHARD = IN_COLS // N_DEV
OFF_TAIL = 1200
ROWS_TAIL = 16
PACK_TILE = 128
LATE_ROWS = 1280
ROWS_PS, ROWS_OUT, ROWS_UP, ROWS_DOWN, ROWS_PA = D_INNER // N_DEV, D_MODEL // N_DEV, D_FF // N_DEV, D_FF // N_DEV, 32
OFF_PS = LATE_ROWS
OFF_OUT = OFF_PS + ROWS_PS
OFF_UP = OFF_OUT + ROWS_OUT
OFF_DOWN = OFF_UP + ROWS_UP
OFF_PA = OFF_DOWN + ROWS_DOWN
PACK_ROWS = OFF_PA + ROWS_PA
EARLY_ROWS = PACK_ROWS - LATE_ROWS
EARLY_TILE = 160
CONV_SHARD = D_CONV * CONV_DIM // N_DEV
TAIL_ELEMS = CONV_SHARD + sum(SMALL_SIZES.values()) + 1


def _cparams(sem=None, **kw):
    return pltpu.CompilerParams(dimension_semantics=sem, vmem_limit_bytes=VMEM_LIMIT, **kw)


def _mx(v):
    return v.astype(MXU_DTYPE)


def _dot(a, b):
    return jnp.dot(_mx(a), _mx(b), preferred_element_type=f32)


def _dot_nt(a, b):
    return lax.dot_general(_mx(a), _mx(b), (((1,), (1,)), ((), ())), preferred_element_type=f32)


def _dot_tn(a, b):
    return lax.dot_general(_mx(a), _mx(b), (((0,), (0,)), ((), ())), preferred_element_type=f32)


def _dot_exact(a, b):
    return jnp.dot(a, b, precision=lax.Precision.HIGHEST, preferred_element_type=f32)


def _sigmoid(v):
    return 1.0 / (1.0 + jnp.exp(-v))


def _pick(n, prefs):
    for p in prefs:
        if n % p == 0:
            return p
    return n


MM_TILE = 1024


def mm_nn(a, b, name, out_dtype=f32, acc_in=None, acc_scale=1.0):
    m, k = a.shape
    n = b.shape[1]
    tm = _pick(m, (MM_TILE, 576, 512, 256, 128, 64))
    tn = _pick(n, (MM_TILE, 512, 256, 128))
    tk = _pick(k, (2048, 1536, 1152, 1024, 768, 512, 256, 128))
    nk = k // tk

    def body(*refs):
        a_ref, b_ref = refs[0:2]
        c_ref = refs[2] if acc_in is not None else None
        o_ref = refs[3] if acc_in is not None else refs[2]

        def finish(r):
            if acc_in is not None:
                r = r + acc_scale * c_ref[...]
            o_ref[...] = r.astype(o_ref.dtype)

        if nk == 1:
            finish(_dot(a_ref[...], b_ref[...]))
            return
        acc_ref = refs[-1]
        kk = pl.program_id(2)

        @pl.when(kk == 0)
        def _():
            acc_ref[...] = jnp.zeros_like(acc_ref)

        acc_ref[...] += _dot(a_ref[...], b_ref[...])

        @pl.when(kk == nk - 1)
        def _():
            finish(acc_ref[...])

    in_specs = [pl.BlockSpec((tm, tk), lambda i, j, kk: (i, kk)), pl.BlockSpec((tk, tn), lambda i, j, kk: (kk, j))]
    args = [a, b]
    if acc_in is not None:
        in_specs.append(pl.BlockSpec((tm, tn), lambda i, j, kk: (i, j)))
        args.append(acc_in)
    return pl.pallas_call(
        body, name=name, grid=(m // tm, n // tn, nk), in_specs=in_specs,
        out_specs=pl.BlockSpec((tm, tn), lambda i, j, kk: (i, j)),
        out_shape=jax.ShapeDtypeStruct((m, n), out_dtype),
        scratch_shapes=[pltpu.VMEM((tm, tn), f32)] if nk > 1 else [],
        compiler_params=_cparams(("parallel", "parallel", "arbitrary")))(*args)


def mm_nt(a, b, name, out_dtype=f32, relu2=None, relu2_of=None):
    m, k = a.shape
    n = b.shape[0]
    tm = MM_TILE
    tn = _pick(n, (MM_TILE, 768, 512, 256, 128))

    def body(*refs):
        r = _dot_nt(refs[0][...], refs[1][...])
        if relu2:
            pos = jnp.maximum(r, 0.0)
            refs[2][...] = pos.astype(refs[2].dtype)
            refs[3][...] = (pos * pos).astype(refs[3].dtype)
        elif relu2_of is not None:
            refs[3][...] = (r * (2.0 * refs[2][...].astype(f32))).astype(refs[3].dtype)
        else:
            refs[2][...] = r.astype(refs[2].dtype)

    blk = pl.BlockSpec((tm, tn), lambda i, j: (i, j))
    in_specs = [pl.BlockSpec((tm, k), lambda i, j: (i, 0)), pl.BlockSpec((tn, k), lambda i, j: (j, 0))]
    args = [a, b]
    if relu2_of is not None:
        in_specs.append(blk)
        args.append(relu2_of)
    if relu2:
        out_specs, out_shape = [blk, blk], [jax.ShapeDtypeStruct((m, n), bf16), jax.ShapeDtypeStruct((m, n), bf16)]
    else:
        out_specs, out_shape = blk, jax.ShapeDtypeStruct((m, n), out_dtype)
    return pl.pallas_call(body, name=name, grid=(m // tm, n // tn), in_specs=in_specs, out_specs=out_specs,
                          out_shape=out_shape, compiler_params=_cparams(("parallel", "parallel")))(*args)


def mm_nt_split(a, b, name, width, out_dtype=f32):
    m, k = a.shape
    n = b.shape[0]
    tm = MM_TILE
    parts = n // width

    def body(a_ref, b_ref, *o_refs):
        r = _dot_nt(a_ref[...], b_ref[...])
        for q in range(parts):
            o_refs[q][...] = r[:, width * q:width * (q + 1)].astype(o_refs[q].dtype)

    blk = pl.BlockSpec((tm, width), lambda i: (i, 0))
    return pl.pallas_call(
        body, name=name, grid=(m // tm,),
        in_specs=[pl.BlockSpec((tm, k), lambda i: (i, 0)), pl.BlockSpec((n, k), lambda i: (0, 0))],
        out_specs=[blk] * parts, out_shape=[jax.ShapeDtypeStruct((m, width), out_dtype)] * parts,
        compiler_params=_cparams(("parallel",)))(a, b)


def mm_tn(a, b, name, acc_in=None, acc_scale=1.0):
    k, m = a.shape
    n = b.shape[1]
    tm = _pick(m, (MM_TILE, 768, 512, 256, 128))
    tn = _pick(n, (MM_TILE, 512, 256, 128))
    tk = _pick(k, (1024, 768, 576, 512, 256, 128, 64))
    nk = k // tk

    def body(*refs):
        a_ref, b_ref, o_ref = refs[0], refs[1], refs[-1]
        kk = pl.program_id(2)

        @pl.when(kk == 0)
        def _():
            o_ref[...] = jnp.zeros_like(o_ref) if acc_in is None else acc_scale * refs[2][...]

        o_ref[...] += _dot_tn(a_ref[...], b_ref[...])

    in_specs = [pl.BlockSpec((tk, tm), lambda i, j, kk: (kk, i)), pl.BlockSpec((tk, tn), lambda i, j, kk: (kk, j))]
    args = [a, b]
    if acc_in is not None:
        in_specs.append(pl.BlockSpec((tm, tn), lambda i, j, kk: (i, j)))
        args.append(acc_in)
    return pl.pallas_call(
        body, name=name, grid=(m // tm, n // tn, nk), in_specs=in_specs,
        out_specs=pl.BlockSpec((tm, tn), lambda i, j, kk: (i, j)),
        out_shape=jax.ShapeDtypeStruct((m, n), f32),
        compiler_params=_cparams(("parallel", "parallel", "arbitrary")))(*args)


def _halo_specs(tb, cb, nt, off=0):
    r = tb // 8
    return [pl.BlockSpec((8, cb), lambda j, i: (jnp.maximum(i * r - 1, 0), j + off)),
            pl.BlockSpec((tb, cb), lambda j, i: (i, j + off)),
            pl.BlockSpec((8, cb), lambda j, i: (jnp.minimum((i + 1) * r, nt * r - 1), j + off))]


def _with_halo(prev_ref, own_ref, next_ref, i, nt):
    prev = jnp.where(i > 0, prev_ref[...].astype(f32), 0.0)
    nxt = jnp.where(i < nt - 1, next_ref[...].astype(f32), 0.0)
    return jnp.concatenate([prev, own_ref[...].astype(f32), nxt], axis=0)


def _shifted(xcat, s, tb):
    n = xcat.shape[0]
    return pltpu.roll(xcat, (-s) % n, 0)[8:8 + tb]


def conv_fwd(xbc, w8, b_row, tb=512, cb=512):
    t, c = xbc.shape
    nt = t // tb

    def body(prev_ref, own_ref, next_ref, w_ref, b_ref, o_ref):
        i = pl.program_id(1)
        xcat = _with_halo(prev_ref, own_ref, next_ref, i, nt)
        w = w_ref[...]
        pre = b_ref[...] + w[0:1] * _shifted(xcat, -2, tb)
        for k in range(1, D_CONV):
            pre = pre + w[k:k + 1] * _shifted(xcat, k - 2, tb)
        o_ref[...] = pre * _sigmoid(pre)

    return pl.pallas_call(
        body, name="conv_fwd", grid=(c // cb, nt),
        in_specs=_halo_specs(tb, cb, nt) + [pl.BlockSpec((8, cb), lambda j, i: (0, j)), pl.BlockSpec((1, cb), lambda j, i: (0, j))],
        out_specs=pl.BlockSpec((tb, cb), lambda j, i: (i, j)), out_shape=jax.ShapeDtypeStruct((t, c), f32),
        compiler_params=_cparams(("parallel", "parallel")))(xbc, xbc, xbc, w8, b_row)


def conv_bwd(xbc, xoff, grads, scales, w8, b_row, name, tb=512, cb=512):
    t, c = grads[0].shape
    nt = t // tb
    ng = len(grads)
    has_scale = [s is not None for s in scales]

    def body(*refs):
        i = pl.program_id(1)
        xr = refs[0:3]
        gr = [refs[3 + 3 * q: 6 + 3 * q] for q in range(ng)]
        pos = 3 + 3 * ng
        sr = []
        for q in range(ng):
            if has_scale[q]:
                sr.append(refs[pos])
                pos += 1
            else:
                sr.append(None)
        w_ref, b_ref, dx_ref, dw_ref, db_ref = refs[pos:pos + 5]
        xcat = _with_halo(*xr, i, nt)
        gcat = None
        for q in range(ng):
            gq = _with_halo(*gr[q], i, nt)
            if sr[q] is not None:
                gq = gq * sr[q][...]
            gcat = gq if gcat is None else gcat + gq
        w = w_ref[...]
        n = tb + 16
        pre = b_ref[...] + w[0:1] * pltpu.roll(xcat, 2, 0)
        for k in range(1, D_CONV):
            pre = pre + w[k:k + 1] * pltpu.roll(xcat, (2 - k) % n, 0)
        sg = _sigmoid(pre)
        dpre = gcat * sg * (1.0 + pre * (1.0 - sg))
        dx = w[0:1] * _shifted(dpre, 2, tb)
        for k in range(1, D_CONV):
            dx = dx + w[k:k + 1] * _shifted(dpre, 2 - k, tb)
        dx_ref[...] = dx.astype(dx_ref.dtype)
        dp_own = dpre[8:8 + tb]
        rows = [jnp.sum(dp_own * _shifted(xcat, k - 2, tb), axis=0, keepdims=True) for k in range(D_CONV)]
        dw = jnp.concatenate(rows + [jnp.zeros((8 - D_CONV, cb), f32)], axis=0)
        db = jnp.sum(dp_own, axis=0, keepdims=True)

        @pl.when(i == 0)
        def _():
            dw_ref[...] = jnp.zeros_like(dw_ref)
            db_ref[...] = jnp.zeros_like(db_ref)

        dw_ref[...] += dw
        db_ref[...] += db

    in_specs = _halo_specs(tb, cb, nt, xoff)
    args = [xbc] * 3
    for g in grads:
        in_specs += _halo_specs(tb, cb, nt)
        args += [g] * 3
    for s in scales:
        if s is not None:
            in_specs.append(pl.BlockSpec((1, cb), lambda j, i: (0, j)))
            args.append(s)
    in_specs += [pl.BlockSpec((8, cb), lambda j, i: (0, j)), pl.BlockSpec((1, cb), lambda j, i: (0, j))]
    args += [w8, b_row]
    return pl.pallas_call(
        body, name=name, grid=(c // cb, nt), in_specs=in_specs,
        out_specs=[pl.BlockSpec((tb, cb), lambda j, i: (i, j)), pl.BlockSpec((8, cb), lambda j, i: (0, j)),
                   pl.BlockSpec((1, cb), lambda j, i: (0, j))],
        out_shape=[jax.ShapeDtypeStruct((t, c), bf16), jax.ShapeDtypeStruct((8, c), f32), jax.ShapeDtypeStruct((1, c), f32)],
        compiler_params=_cparams(("parallel", "arbitrary")))(*args)


def dt_fwd(u_dt, bias_row, tb=1024):
    t = u_dt.shape[0]

    def body(u_ref, b_ref, o_ref):
        v = u_ref[...] + b_ref[...]
        sp = jnp.maximum(v, 0.0) + jnp.log(1.0 + jnp.exp(-jnp.abs(v)))
        lane = lax.broadcasted_iota(jnp.int32, v.shape, 1)
        o_ref[...] = jnp.where((lane & 127) < SSD_HEADS, sp, 0.0)

    return pl.pallas_call(
        body, name="dt_fwd", grid=(t // tb,),
        in_specs=[pl.BlockSpec((tb, 256), lambda i: (i, 0)), pl.BlockSpec((1, 256), lambda i: (0, 0))],
        out_specs=pl.BlockSpec((tb, 256), lambda i: (i, 0)), out_shape=jax.ShapeDtypeStruct((t, 256), f32),
        compiler_params=_cparams(("parallel",)))(u_dt, bias_row)


def dt_bwd(ddt_f, ddt_b, u_dt, bias_row, tb=1024):
    t = u_dt.shape[0]

    def body(gf_ref, gb_ref, u_ref, b_ref, du_ref, db_ref):
        g = jnp.concatenate([jnp.sum(gf_ref[...], axis=0), jnp.sum(gb_ref[...], axis=0)], axis=1)
        du = g * _sigmoid(u_ref[...] + b_ref[...])
        du_ref[...] = du.astype(du_ref.dtype)

        @pl.when(pl.program_id(0) == 0)
        def _():
            db_ref[...] = jnp.zeros_like(db_ref)

        db_ref[...] += jnp.sum(du, axis=0, keepdims=True)

    return pl.pallas_call(
        body, name="dt_bwd", grid=(t // tb,),
        in_specs=[pl.BlockSpec((4, tb, 128), lambda i: (0, i, 0)), pl.BlockSpec((4, tb, 128), lambda i: (0, i, 0)),
                  pl.BlockSpec((tb, 256), lambda i: (i, 0)), pl.BlockSpec((1, 256), lambda i: (0, 0))],
        out_specs=[pl.BlockSpec((tb, 256), lambda i: (i, 0)), pl.BlockSpec((1, 256), lambda i: (0, 0))],
        out_shape=[jax.ShapeDtypeStruct((t, 256), bf16), jax.ShapeDtypeStruct((1, 256), f32)],
        compiler_params=_cparams(("arbitrary",)))(ddt_f, ddt_b, u_dt, bias_row)


def _ssd_common(dt_blk, a_row, reverse):
    row = lax.broadcasted_iota(jnp.int32, (CHUNK, CHUNK), 0)
    col = lax.broadcasted_iota(jnp.int32, (CHUNK, CHUNK), 1)
    mask = (row <= col) if reverse else (row >= col)
    tri = mask.astype(f32)
    a = dt_blk * a_row
    acs = _dot_exact(tri, a)
    atot = jnp.sum(a, axis=0, keepdims=True)
    return mask, tri, a, acs, atot, col


def _lane_col(mat, lane_idx, h):
    return jnp.sum(jnp.where(lane_idx == h, mat, 0.0), axis=1, keepdims=True)


def ssd_fwd(xbc_c, dt2, a_rows, reverse, name):
    t = xbc_c.shape[0]
    nc = t // CHUNK
    d_off = 1 if reverse else 0

    def cidx(c):
        return nc - 1 - c if reverse else c

    def body(x_ref, b_ref, c_ref, dt_ref, a_ref, y_ref, hp_ref, h_scr, acst_scr):
        g = pl.program_id(0)
        c = pl.program_id(1)

        @pl.when(c == 0)
        def _():
            h_scr[...] = jnp.zeros_like(h_scr)

        dt_blk = dt_ref[...]
        mask, tri, a, acs, atot, lane = _ssd_common(dt_blk, a_ref[...], reverse)
        acst_scr[...] = acs.T
        bm = b_ref[...]
        cm = c_ref[...]
        cb = _dot_nt(cm, bm)
        half = lane >= SSD_HEAD_DIM
        sub_half = lax.broadcasted_iota(jnp.int32, (CHUNK, 1), 0) >= SSD_HEAD_DIM
        for j in range(4):
            x = x_ref[:, 128 * j:128 * (j + 1)]
            cols, dts, tots = [], [], []
            y = None
            for e in range(2):
                h = 8 * g + 2 * j + e
                col_h = _lane_col(acs, lane, h)
                row_h = acst_scr[pl.ds(h, 1), :]
                dt_h = _lane_col(dt_blk, lane, h)
                lmat = jnp.where(mask, jnp.exp(jnp.where(mask, col_h - row_h, 0.0)), 0.0)
                xdt_e = jnp.where(half == (e == 1), x * dt_h, 0.0)
                ye = _dot(cb * lmat, xdt_e)
                y = ye if y is None else y + ye
                cols.append(col_h)
                dts.append(dt_h)
                tots.append(jnp.sum(jnp.where(lane[0:1] == h, atot, 0.0), axis=1, keepdims=True))
            hp = h_scr[j]
            hp_ref[0, j] = hp
            ecol = jnp.where(half, jnp.exp(cols[1]), jnp.exp(cols[0]))
            y = y + _dot_nt(cm, hp) * ecol
            y_ref[:, 128 * j:128 * (j + 1)] = y
            dec = jnp.where(half, jnp.exp(tots[1] - cols[1]), jnp.exp(tots[0] - cols[0]))
            xdt = x * jnp.where(half, dts[1], dts[0])
            s_new = _dot_tn(xdt * dec, bm)
            cd = jnp.where(sub_half, jnp.exp(tots[1]), jnp.exp(tots[0]))
            h_scr[j] = cd * hp + s_new

    return pl.pallas_call(
        body, name=name, grid=(SSD_GROUPS, nc),
        in_specs=[pl.BlockSpec((CHUNK, 512), lambda g, c: (cidx(c), g)),
                  pl.BlockSpec((CHUNK, 128), lambda g, c: (cidx(c), 16 + g)),
                  pl.BlockSpec((CHUNK, 128), lambda g, c: (cidx(c), 20 + g)),
                  pl.BlockSpec((CHUNK, 128), lambda g, c: (cidx(c), d_off)),
                  pl.BlockSpec((1, 128), lambda g, c: (0, d_off))],
        out_specs=[pl.BlockSpec((CHUNK, 512), lambda g, c: (cidx(c), g)),
                   pl.BlockSpec((1, 4, 128, 128), lambda g, c: (cidx(c), g, 0, 0))],
        out_shape=[jax.ShapeDtypeStruct((t, D_INNER), f32), jax.ShapeDtypeStruct((nc, 16, 128, 128), f32)],
        scratch_shapes=[pltpu.VMEM((4, 128, 128), f32), pltpu.VMEM((CHUNK, CHUNK), f32)],
        compiler_params=_cparams(("parallel", "arbitrary")))(xbc_c, xbc_c, xbc_c, dt2, a_rows)


def ssd_bwd(xbc_c, dt2, a_rows, dy, hprev, reverse, name):
    t = xbc_c.shape[0]
    nc = t // CHUNK
    d_off = 1 if reverse else 0

    def cidx(c):
        return c if reverse else nc - 1 - c

    def body(x_ref, b_ref, c_ref, dt_ref, a_ref, dy_ref, hp_ref, dx_ref, db_ref, dc_ref, ddt_ref, da_ref,
             dh_scr, acst_scr):
        g = pl.program_id(0)
        c = pl.program_id(1)

        @pl.when(c == 0)
        def _():
            dh_scr[...] = jnp.zeros_like(dh_scr)
            da_ref[...] = jnp.zeros_like(da_ref)

        dt_blk = dt_ref[...]
        a_row = a_ref[...]
        mask, tri, a, acs, atot, lane = _ssd_common(dt_blk, a_row, reverse)
        acst_scr[...] = acs.T
        sub = lax.broadcasted_iota(jnp.int32, (CHUNK, CHUNK), 0)
        bm = b_ref[...]
        cm = c_ref[...]
        cb = _dot_nt(cm, bm)
        half = lane >= SSD_HEAD_DIM
        sub_half = sub[:, 0:1] >= SSD_HEAD_DIM
        dcb = jnp.zeros((CHUNK, CHUNK), f32)
        dacs = jnp.zeros((CHUNK, CHUNK), f32)
        dacs_t = jnp.zeros((CHUNK, CHUNK), f32)
        dtot = jnp.zeros((1, CHUNK), f32)
        ddt_x = jnp.zeros((CHUNK, CHUNK), f32)
        dbm = jnp.zeros((CHUNK, D_STATE), f32)
        dcm = jnp.zeros((CHUNK, D_STATE), f32)
        for j in range(4):
            x = x_ref[:, 128 * j:128 * (j + 1)]
            dyp = dy_ref[:, 128 * j:128 * (j + 1)]
            hp = hp_ref[0, j]
            dhn = dh_scr[j]
            cols, dts, tots, hs = [], [], [], []
            dxdt = None
            for e in range(2):
                h = 8 * g + 2 * j + e
                sel = half == (e == 1)
                col_h = _lane_col(acs, lane, h)
                row_h = acst_scr[pl.ds(h, 1), :]
                dt_h = _lane_col(dt_blk, lane, h)
                lmat = jnp.where(mask, jnp.exp(jnp.where(mask, col_h - row_h, 0.0)), 0.0)
                xdt_e = jnp.where(sel, x * dt_h, 0.0)
                dy_e = jnp.where(sel, dyp, 0.0)
                ml = _dot_nt(dy_e, xdt_e) * lmat
                dcb = dcb + ml
                w = ml * cb
                dacs = dacs + jnp.where(lane == h, jnp.sum(w, axis=1, keepdims=True), 0.0)
                dacs_t = dacs_t - jnp.where(sub == h, jnp.sum(w, axis=0, keepdims=True), 0.0)
                de = _dot_tn(cb * lmat, dy_e)
                dxdt = de if dxdt is None else dxdt + de
                cols.append(col_h)
                dts.append(dt_h)
                tots.append(jnp.sum(jnp.where(lane[0:1] == h, atot, 0.0), axis=1, keepdims=True))
                hs.append(h)
            ecol = jnp.where(half, jnp.exp(cols[1]), jnp.exp(cols[0]))
            dec = jnp.where(half, jnp.exp(tots[1] - cols[1]), jnp.exp(tots[0] - cols[0]))
            cd = jnp.where(sub_half, jnp.exp(tots[1]), jnp.exp(tots[0]))
            dtp = jnp.where(half, dts[1], dts[0])
            xdt = x * dtp
            yoff = _dot_nt(cm, hp) * ecol
            dye = dyp * ecol
            dcm = dcm + _dot(dye, hp)
            dhp = _dot_tn(dye, cm)
            gmat = _dot_nt(bm, dhn)
            dxdt = dxdt + dec * gmat
            dbm = dbm + _dot(xdt * dec, dhn)
            r_off = dyp * yoff
            r_dec = xdt * gmat * dec
            r_x = dxdt * x
            hh = dhn * hp
            for e in range(2):
                sel = half == (e == 1)
                h = hs[e]
                s_off = jnp.sum(jnp.where(sel, r_off, 0.0), axis=1, keepdims=True)
                s_dec = jnp.sum(jnp.where(sel, r_dec, 0.0), axis=1, keepdims=True)
                dacs = dacs + jnp.where(lane == h, s_off - s_dec, 0.0)
                dcd = jnp.sum(jnp.sum(jnp.where(sub_half == (e == 1), hh, 0.0), axis=1, keepdims=True), axis=0, keepdims=True)
                tot_e = jnp.sum(s_dec, axis=0, keepdims=True) + jnp.exp(tots[e]) * dcd
                dtot = dtot + jnp.where(lane[0:1] == h, tot_e, 0.0)
                ddt_x = ddt_x + jnp.where(lane == h, jnp.sum(jnp.where(sel, r_x, 0.0), axis=1, keepdims=True), 0.0)
            dx_ref[:, 128 * j:128 * (j + 1)] = dxdt * dtp
            dh_scr[j] = cd * dhn + dhp
        dcm = dcm + _dot(dcb, bm)
        dbm = dbm + _dot_tn(dcb, cm)
        db_ref[...] = dbm
        dc_ref[...] = dcm
        dacs = dacs + dacs_t.T
        da = _dot_exact(tri.T, dacs) + dtot
        ddt_ref[0] = da * a_row + ddt_x
        da_ref[0] += jnp.sum(da * dt_blk, axis=0, keepdims=True)

    return pl.pallas_call(
        body, name=name, grid=(SSD_GROUPS, nc),
        in_specs=[pl.BlockSpec((CHUNK, 512), lambda g, c: (cidx(c), g)),
                  pl.BlockSpec((CHUNK, 128), lambda g, c: (cidx(c), 16 + g)),
                  pl.BlockSpec((CHUNK, 128), lambda g, c: (cidx(c), 20 + g)),
                  pl.BlockSpec((CHUNK, 128), lambda g, c: (cidx(c), d_off)),
                  pl.BlockSpec((1, 128), lambda g, c: (0, d_off)),
                  pl.BlockSpec((CHUNK, 512), lambda g, c: (cidx(c), g)),
                  pl.BlockSpec((1, 4, 128, 128), lambda g, c: (cidx(c), g, 0, 0))],
        out_specs=[pl.BlockSpec((CHUNK, 512), lambda g, c: (cidx(c), g)),
                   pl.BlockSpec((CHUNK, 128), lambda g, c: (cidx(c), g)),
                   pl.BlockSpec((CHUNK, 128), lambda g, c: (cidx(c), g)),
                   pl.BlockSpec((1, CHUNK, 128), lambda g, c: (g, cidx(c), 0)),
                   pl.BlockSpec((1, 1, 128), lambda g, c: (g, 0, 0))],
        out_shape=[jax.ShapeDtypeStruct((t, D_INNER), f32), jax.ShapeDtypeStruct((t, 512), f32),
                   jax.ShapeDtypeStruct((t, 512), f32), jax.ShapeDtypeStruct((4, t, 128), f32),
                   jax.ShapeDtypeStruct((4, 1, 128), f32)],
        scratch_shapes=[pltpu.VMEM((4, 128, 128), f32), pltpu.VMEM((CHUNK, CHUNK), f32)],
        compiler_params=_cparams(("parallel", "arbitrary")))(xbc_c, xbc_c, xbc_c, dt2, a_rows, dy, hprev)


def tail_fwd(y_f, y_b, xbc_c, z, dskip_row, nw_row, tb=512):
    t = y_f.shape[0]

    def body(yf_ref, yb_ref, x_ref, z_ref, d_ref, w_ref, o_ref):
        zz = z_ref[...]
        y = (yf_ref[...] + yb_ref[...] + d_ref[...] * x_ref[...]) * (zz * _sigmoid(zz))
        rstd = lax.rsqrt(jnp.mean(y * y, axis=1, keepdims=True) + NORM_EPS)
        o_ref[...] = (y * rstd * w_ref[...]).astype(o_ref.dtype)

    blk = pl.BlockSpec((tb, 512), lambda i, g: (i, g))
    row = pl.BlockSpec((1, 512), lambda i, g: (0, g))
    return pl.pallas_call(
        body, name="tail_fwd", grid=(t // tb, SSD_GROUPS), in_specs=[blk, blk, blk, blk, row, row], out_specs=blk,
        out_shape=jax.ShapeDtypeStruct((t, D_INNER), bf16),
        compiler_params=_cparams(("parallel", "parallel")))(y_f, y_b, xbc_c, z, dskip_row, nw_row)


def tail_bwd(dyn, y_f, y_b, xbc_c, z, dskip_row, nw_row, tb=512):
    t = y_f.shape[0]

    def body(g_ref, yf_ref, yb_ref, x_ref, z_ref, d_ref, w_ref, dy_ref, dz_ref, dw_ref, dd_ref):
        zz = z_ref[...]
        sg = _sigmoid(zz)
        sl = zz * sg
        x = x_ref[...]
        y = yf_ref[...] + yb_ref[...] + d_ref[...] * x
        yz = y * sl
        rstd = lax.rsqrt(jnp.mean(yz * yz, axis=1, keepdims=True) + NORM_EPS)
        yhat = yz * rstd
        g = g_ref[...]
        dyhat = g * w_ref[...]
        dyz = rstd * (dyhat - yhat * jnp.mean(dyhat * yhat, axis=1, keepdims=True))
        dy = dyz * sl
        dy_ref[...] = dy
        dz_ref[...] = (dyz * y * sg * (1.0 + zz * (1.0 - sg))).astype(dz_ref.dtype)

        @pl.when(pl.program_id(1) == 0)
        def _():
            dw_ref[...] = jnp.zeros_like(dw_ref)
            dd_ref[...] = jnp.zeros_like(dd_ref)

        dw_ref[...] += jnp.sum(g * yhat, axis=0, keepdims=True)
        dd_ref[...] += jnp.sum(dy * x, axis=0, keepdims=True)

    blk = pl.BlockSpec((tb, 512), lambda g, i: (i, g))
    row = pl.BlockSpec((1, 512), lambda g, i: (0, g))
    return pl.pallas_call(
        body, name="tail_bwd", grid=(SSD_GROUPS, t // tb), in_specs=[blk, blk, blk, blk, blk, row, row],
        out_specs=[blk, blk, row, row],
        out_shape=[jax.ShapeDtypeStruct((t, D_INNER), f32), jax.ShapeDtypeStruct((t, D_INNER), bf16),
                   jax.ShapeDtypeStruct((1, D_INNER), f32), jax.ShapeDtypeStruct((1, D_INNER), f32)],
        compiler_params=_cparams(("parallel", "arbitrary")))(dyn, y_f, y_b, xbc_c, z, dskip_row, nw_row)


def _slopes(p):
    return [2.0 ** (-8.0 * (HEADS_PER_PATTERN * p + j + 1) / ATTN_HEADS) for j in range(HEADS_PER_PATTERN)]


def _win_specs(nq, col_of):
    return [pl.BlockSpec((64, 256), lambda r, i: (jnp.maximum(2 * i - 1, 0), col_of(r))),
            pl.BlockSpec((128, 256), lambda r, i: (i, col_of(r))),
            pl.BlockSpec((64, 256), lambda r, i: (jnp.minimum(2 * i + 2, 2 * nq - 1), col_of(r)))]


def _lane_head(shape):
    return lax.broadcasted_iota(jnp.int32, shape, 1) >> 6


def _stack_heads(m):
    lane_head = _lane_head(m.shape)
    return jnp.concatenate([jnp.where(lane_head == j, m, 0.0) for j in range(HEADS_PER_PATTERN)], axis=0)


def _unstack_heads(m4, n):
    lane_head = _lane_head((n, 256))
    out = jnp.where(lane_head == 0, m4[0:n], 0.0)
    for j in range(1, HEADS_PER_PATTERN):
        out = out + jnp.where(lane_head == j, m4[j * n:(j + 1) * n], 0.0)
    return out


def _head_cols(m, n):
    lane = lax.broadcasted_iota(jnp.int32, (n, 256), 1)
    return jnp.concatenate([jnp.sum(jnp.where(lane == ATTN_HEAD_DIM * j, m, 0.0), axis=1, keepdims=True)
                            for j in range(HEADS_PER_PATTERN)], axis=0)


def _score_bias(p, dil, by_key):
    slopes = np.asarray(_slopes(p), np.float32)
    if by_key:
        win = np.arange(256)[:, None]
        rel = np.arange(128)[None, :] - (win - 64)
    else:
        win = np.arange(256)[None, :]
        rel = win - 64 - np.arange(128)[:, None]
    band = np.abs(rel) <= 64
    out = []
    for first, last in ((False, False), (True, False), (False, True), (True, True)):
        ok = band & ~(first & (win < 64)) & ~(last & (win >= 192))
        pen = -slopes[:, None, None] * (np.abs(rel) * dil).astype(np.float32)[None]
        out.append(np.where(ok[None], pen, np.float32(NEG_BIG)).reshape(-1, rel.shape[1]))
    return jnp.asarray(np.stack(out), f32)


def _bias_spec(nq, rows, cols):
    return pl.BlockSpec((1, rows, cols), lambda r, i: ((i == 0).astype(jnp.int32) + 2 * (i == nq - 1).astype(jnp.int32), 0, 0))


def attn_fwd(q, k, v, p, dil, name):
    l = q.shape[0]
    nq = l // 128

    def body(q_ref, kp_ref, ko_ref, kn_ref, vp_ref, vo_ref, vn_ref, bias_ref, o_ref, lse_ref):
        kcat = jnp.concatenate([kp_ref[...], ko_ref[...], kn_ref[...]], axis=0)
        vcat = jnp.concatenate([vp_ref[...], vo_ref[...], vn_ref[...]], axis=0)
        s = _dot_nt(_stack_heads(q_ref[...] * 0.125), kcat) + bias_ref[0]
        m = jnp.max(s, axis=1, keepdims=True)
        pr = jnp.exp(s - m)
        den = jnp.sum(pr, axis=1, keepdims=True)
        o4 = _dot(pr, vcat) / den
        o_ref[...] = _unstack_heads(o4, 128)
        lse_ref[...] = _unstack_heads(jnp.broadcast_to(m + jnp.log(den), (512, 256)), 128)

    col = lambda r: r
    return pl.pallas_call(
        body, name=name, grid=(dil, nq),
        in_specs=[pl.BlockSpec((128, 256), lambda r, i: (i, r))] + _win_specs(nq, col) + _win_specs(nq, col)
        + [_bias_spec(nq, 512, 256)],
        out_specs=[pl.BlockSpec((128, 256), lambda r, i: (i, r))] * 2,
        out_shape=[jax.ShapeDtypeStruct(q.shape, f32)] * 2,
        compiler_params=_cparams(("parallel", "parallel")))(q, k, k, k, v, v, v, _score_bias(p, dil, False))


def attn_combine(os_, lses, tb=1024):
    t = os_[0].shape[0]

    def body(o0, o1, o2, l0, l1, l2, y_ref, lse_ref):
        a0, a1, a2 = l0[...], l1[...], l2[...]
        m = jnp.maximum(jnp.maximum(a0, a1), a2)
        e0, e1, e2 = jnp.exp(a0 - m), jnp.exp(a1 - m), jnp.exp(a2 - m)
        den = e0 + e1 + e2
        y_ref[...] = (e0 * o0[...] + e1 * o1[...] + e2 * o2[...]) / den
        lse_ref[...] = m + jnp.log(den)

    blk = pl.BlockSpec((tb, 256), lambda i: (i, 0))
    return pl.pallas_call(
        body, name="attn_combine", grid=(t // tb,), in_specs=[blk] * 6, out_specs=[blk, blk],
        out_shape=[jax.ShapeDtypeStruct((t, 256), f32)] * 2,
        compiler_params=_cparams(("parallel",)))(*os_, *lses)


def attn_delta(dy, y, tb=1024):
    t = dy.shape[0]

    def body(dy_ref, y_ref, d_ref):
        pr = dy_ref[...] * y_ref[...]
        lane_head = _lane_head(pr.shape)
        out = jnp.zeros_like(pr)
        for j in range(HEADS_PER_PATTERN):
            sj = jnp.sum(jnp.where(lane_head == j, pr, 0.0), axis=1, keepdims=True)
            out = out + jnp.where(lane_head == j, sj, 0.0)
        d_ref[...] = out

    blk = pl.BlockSpec((tb, 256), lambda i: (i, 0))
    return pl.pallas_call(body, name="attn_delta", grid=(t // tb,), in_specs=[blk, blk], out_specs=blk,
                          out_shape=jax.ShapeDtypeStruct((t, 256), f32),
                          compiler_params=_cparams(("parallel",)))(dy, y)


def attn_dq(q, k, v, dy, lse, delta, p, dil, name):
    l = q.shape[0]
    nq = l // 128

    def body(q_ref, kp_ref, ko_ref, kn_ref, vp_ref, vo_ref, vn_ref, dy_ref, lse_ref, d_ref, bias_ref, dq_ref):
        kcat = jnp.concatenate([kp_ref[...], ko_ref[...], kn_ref[...]], axis=0)
        vcat = jnp.concatenate([vp_ref[...], vo_ref[...], vn_ref[...]], axis=0)
        s = _dot_nt(_stack_heads(q_ref[...] * 0.125), kcat) + bias_ref[0]
        pr = jnp.exp(s - _head_cols(lse_ref[...], 128))
        dp = _dot_nt(_stack_heads(dy_ref[...]), vcat)
        ds = pr * (dp - _head_cols(d_ref[...], 128))
        dq_ref[...] = (_unstack_heads(_dot(ds, kcat), 128) * 0.125).astype(dq_ref.dtype)

    col = lambda r: r
    own = pl.BlockSpec((128, 256), lambda r, i: (i, r))
    return pl.pallas_call(
        body, name=name, grid=(dil, nq),
        in_specs=[own] + _win_specs(nq, col) + _win_specs(nq, col) + [own, own, own, _bias_spec(nq, 512, 256)],
        out_specs=own, out_shape=jax.ShapeDtypeStruct(q.shape, bf16),
        compiler_params=_cparams(("parallel", "parallel")))(q, k, k, k, v, v, v, dy, lse, delta, _score_bias(p, dil, False))


def attn_dkv(q, k, v, dy, lse, delta, p, dil, name):
    l = q.shape[0]
    nq = l // 128

    def body(qp_ref, qo_ref, qn_ref, gp_ref, go_ref, gn_ref, lp_ref, lo_ref, ln_ref, dp_ref, do_ref, dn_ref,
             k_ref, v_ref, bias_ref, dk_ref, dv_ref):
        cat = lambda a, b, c: jnp.concatenate([a[...], b[...], c[...]], axis=0)
        q4 = _stack_heads(cat(qp_ref, qo_ref, qn_ref) * 0.125)
        dy4 = _stack_heads(cat(gp_ref, go_ref, gn_ref))
        lse4 = _head_cols(cat(lp_ref, lo_ref, ln_ref), 256)
        del4 = _head_cols(cat(dp_ref, do_ref, dn_ref), 256)
        s = _dot_nt(q4, k_ref[...]) + bias_ref[0]
        pr = jnp.exp(s - lse4)
        dpm = _dot_nt(dy4, v_ref[...])
        ds = pr * (dpm - del4)
        dv_ref[...] = _dot_tn(pr, dy4).astype(dv_ref.dtype)
        dk_ref[...] = _dot_tn(ds, q4).astype(dk_ref.dtype)

    col = lambda r: r
    own = pl.BlockSpec((128, 256), lambda r, i: (i, r))
    win = _win_specs(nq, col)
    return pl.pallas_call(
        body, name=name, grid=(dil, nq), in_specs=win * 4 + [own, own, _bias_spec(nq, 1024, 128)], out_specs=[own, own],
        out_shape=[jax.ShapeDtypeStruct(q.shape, bf16)] * 2,
        compiler_params=_cparams(("parallel", "parallel")))(q, q, q, dy, dy, dy, lse, lse, lse, delta, delta, delta, k, v,
                                                            _score_bias(p, dil, True))


def _lanes(v, reps):
    return v if reps == 1 else jnp.tile(v, (1, reps))


def _lane_halo_specs(cb, tb, nt, off=0):
    r = tb // 128
    return [pl.BlockSpec((cb, 128), lambda j, i: (j + off, jnp.maximum(i * r - 1, 0))),
            pl.BlockSpec((cb, tb), lambda j, i: (j + off, i)),
            pl.BlockSpec((cb, 128), lambda j, i: (j + off, jnp.minimum((i + 1) * r, nt * r - 1)))]


def _with_lane_halo(prev_ref, own_ref, next_ref, i, nt):
    prev = jnp.where(i > 0, prev_ref[...].astype(f32), 0.0)
    nxt = jnp.where(i < nt - 1, next_ref[...].astype(f32), 0.0)
    return jnp.concatenate([prev, own_ref[...].astype(f32), nxt], axis=1)


def _lane_shifted(xcat, s, tb):
    n = xcat.shape[1]
    return pltpu.roll(xcat, (-s) % n, 1)[:, 128:128 + tb]


def conv_fwd_t(xbc_t, w_b, b_b, tb=1024, cb=256):
    c, t = xbc_t.shape
    nt = t // tb

    def body(prev_ref, own_ref, next_ref, w_ref, b_ref, o_ref):
        i = pl.program_id(1)
        xcat = _with_lane_halo(prev_ref, own_ref, next_ref, i, nt)
        reps = tb // 128
        pre = _lanes(b_ref[...], reps)
        for k in range(D_CONV):
            pre = pre + _lanes(w_ref[k], reps) * _lane_shifted(xcat, k - 2, tb)
        o_ref[...] = pre * _sigmoid(pre)

    return pl.pallas_call(
        body, name="conv_fwd", grid=(c // cb, nt),
        in_specs=_lane_halo_specs(cb, tb, nt) + [pl.BlockSpec((D_CONV, cb, 128), lambda j, i: (0, j, 0)),
                                                 pl.BlockSpec((cb, 128), lambda j, i: (j, 0))],
        out_specs=pl.BlockSpec((cb, tb), lambda j, i: (j, i)), out_shape=jax.ShapeDtypeStruct((c, t), f32),
        compiler_params=_cparams(("parallel", "parallel")))(xbc_t, xbc_t, xbc_t, w_b, b_b)


def conv_bwd_t(xbc_t, grad_t, w_b, b_b, into, name, row0, tb=1024, cb=256):
    c, t = grad_t.shape
    nt = t // tb
    off = row0 // cb
    off_out = (D_INNER + row0) // cb
    reps = tb // 128

    def body(*refs):
        i = pl.program_id(1)
        xr, gr = refs[0:3], refs[3:6]
        w_ref, b_ref = refs[6:8]
        dx_ref, dw_ref, db_ref = refs[-3:]
        xcat = _with_lane_halo(*xr, i, nt)
        gcat = _with_lane_halo(*gr, i, nt)
        n = tb + 256
        wk = [_lanes(w_ref[k], reps + 2) for k in range(D_CONV)]
        pre = _lanes(b_ref[...], reps + 2)
        for k in range(D_CONV):
            pre = pre + wk[k] * pltpu.roll(xcat, (2 - k) % n, 1)
        sg = _sigmoid(pre)
        dpre = gcat * sg * (1.0 + pre * (1.0 - sg))

        def fold(v):
            s = v[:, 0:128]
            for q in range(1, reps):
                s = s + v[:, 128 * q:128 * (q + 1)]
            return s

        @pl.when(i == 0)
        def _():
            dw_ref[...] = jnp.zeros_like(dw_ref)
            db_ref[...] = jnp.zeros_like(db_ref)

        x_own = xcat[:, 128:128 + tb]
        dx = None
        for k in range(D_CONV):
            shifted = _lane_shifted(dpre, 2 - k, tb)
            term = wk[k][:, 128:128 + tb] * shifted
            dx = term if dx is None else dx + term
            dw_ref[k] += fold(shifted * x_own)
        dx_ref[...] = dx.astype(dx_ref.dtype)
        db_ref[...] += fold(dpre[:, 128:128 + tb])

    in_specs = (_lane_halo_specs(cb, tb, nt, off) + _lane_halo_specs(cb, tb, nt)
                + [pl.BlockSpec((D_CONV, cb, 128), lambda j, i: (0, j + off, 0)), pl.BlockSpec((cb, 128), lambda j, i: (j + off, 0))])
    in_specs.append(pl.BlockSpec(memory_space=pl.ANY))
    args = [xbc_t] * 3 + [grad_t] * 3 + [w_b, b_b, into]
    return pl.pallas_call(
        body, name=name, grid=(c // cb, nt), in_specs=in_specs,
        out_specs=[pl.BlockSpec((cb, tb), lambda j, i: (j + off_out, i)),
                   pl.BlockSpec((D_CONV, cb, 128), lambda j, i: (0, j, 0)), pl.BlockSpec((cb, 128), lambda j, i: (j, 0))],
        out_shape=[jax.ShapeDtypeStruct(into.shape, into.dtype), jax.ShapeDtypeStruct((D_CONV, c, 128), f32),
                   jax.ShapeDtypeStruct((c, 128), f32)],
        input_output_aliases={8: 0}, compiler_params=_cparams(("parallel", "arbitrary")))(*args)


def dt_fwd_t(u_dt_t, bias_b, tb=2048):
    r, t = u_dt_t.shape

    def body(u_ref, b_ref, o_ref):
        v = u_ref[...] + _lanes(b_ref[...], tb // 128)
        o_ref[...] = jnp.maximum(v, 0.0) + jnp.log(1.0 + jnp.exp(-jnp.abs(v)))

    return pl.pallas_call(
        body, name="dt_fwd", grid=(t // tb,),
        in_specs=[pl.BlockSpec((r, tb), lambda i: (0, i)), pl.BlockSpec((r, 128), lambda i: (0, 0))],
        out_specs=pl.BlockSpec((r, tb), lambda i: (0, i)), out_shape=jax.ShapeDtypeStruct((r, t), f32),
        compiler_params=_cparams(("parallel",)))(u_dt_t, bias_b)


def dt_bwd_t(ddt_f, ddt_b, u_dt_t, bias_b, into, tb=2048):
    r, t = u_dt_t.shape
    reps = tb // 128
    row_blk = (SSD_COLS - r) // r

    def body(gf_ref, gb_ref, u_ref, b_ref, into_ref, du_ref, db_ref):
        g = jnp.concatenate([gf_ref[...], gb_ref[...]], axis=0)
        du = g * _sigmoid(u_ref[...] + _lanes(b_ref[...], reps))
        du_ref[...] = du.astype(du_ref.dtype)

        @pl.when(pl.program_id(0) == 0)
        def _():
            db_ref[...] = jnp.zeros_like(db_ref)

        s = du[:, 0:128]
        for q in range(1, reps):
            s = s + du[:, 128 * q:128 * (q + 1)]
        db_ref[...] += s

    half = pl.BlockSpec((r // 2, tb), lambda i: (0, i))
    return pl.pallas_call(
        body, name="dt_bwd", grid=(t // tb,),
        in_specs=[half, half, pl.BlockSpec((r, tb), lambda i: (0, i)), pl.BlockSpec((r, 128), lambda i: (0, 0)),
                  pl.BlockSpec(memory_space=pl.ANY)],
        out_specs=[pl.BlockSpec((r, tb), lambda i: (row_blk, i)), pl.BlockSpec((r, 128), lambda i: (0, 0))],
        out_shape=[jax.ShapeDtypeStruct(into.shape, into.dtype), jax.ShapeDtypeStruct((r, 128), f32)],
        input_output_aliases={4: 0}, compiler_params=_cparams(("arbitrary",)))(ddt_f, ddt_b, u_dt_t, bias_b, into)


HEADS_PER_GROUP = SSD_HEADS // SSD_GROUPS


def _group_rows(g, n):
    return pl.ds(pl.multiple_of(g * n, n), n)


def _ssd_decays(dt_blk, a_blk, reverse):
    row = lax.broadcasted_iota(jnp.int32, (CHUNK, CHUNK), 0)
    col = lax.broadcasted_iota(jnp.int32, (CHUNK, CHUNK), 1)
    mask = (row <= col) if reverse else (row >= col)
    tri = mask.astype(f32)
    a8 = dt_blk * a_blk
    a = jnp.concatenate([a8, jnp.zeros((CHUNK - HEADS_PER_GROUP, CHUNK), f32)], axis=0).T
    acs = _dot_exact(tri, a)
    return mask, tri, a8, acs, acs.T, col


def ssd_fwd_t(xbc_ct, dt_t, a_b, reverse, name, prev=None, tail=None):
    t = xbc_ct.shape[1]
    nc = t // CHUNK
    direction = 1 if reverse else 0

    def cidx(c):
        return nc - 1 - c if reverse else c

    def body(*refs):
        x_ref, b_ref, c_ref, dt_ref, a_ref = refs[0:5]
        pos = 5
        prev_ref = None
        if prev is not None:
            prev_ref = refs[pos]
            pos += 1
        if tail is not None:
            z_ref, skip_ref, nw_ref = refs[pos:pos + 3]
            pos += 3
            y_ref, hp_ref, yn_ref, h_scr = refs[pos:pos + 4]
        else:
            y_ref, hp_ref, h_scr = refs[pos:pos + 3]

        @pl.when(pl.program_id(0) == 0)
        def _():
            h_scr[...] = jnp.zeros_like(h_scr)

        def group(g, carry):
            x_v, y_v = x_ref.at[_group_rows(g, 512)], y_ref.at[_group_rows(g, 512)]
            heads = _group_rows(g, HEADS_PER_GROUP)
            hp_v, h_v = hp_ref.at[0, heads], h_scr.at[heads]
            dt_blk = dt_ref[heads, :]
            mask, tri, a8, acs, acs_t, lane = _ssd_decays(dt_blk, a_ref[heads, :], reverse)
            bm = b_ref[_group_rows(g, 128), :].T
            cm = c_ref[_group_rows(g, 128), :].T
            cb = _dot_nt(cm, bm)
            tot = jnp.sum(a8, axis=1, keepdims=True)
            for j in range(HEADS_PER_GROUP):
                rows = slice(SSD_HEAD_DIM * j, SSD_HEAD_DIM * (j + 1))
                col_j = _lane_col(acs, lane, j)
                row_j = acs_t[j:j + 1, :]
                lmat = jnp.where(mask, jnp.exp(jnp.where(mask, col_j - row_j, 0.0)), 0.0)
                xdt = x_v[rows, :] * dt_blk[j:j + 1, :]
                hp = h_v[j]
                hp_v[j] = hp
                y = _dot_nt(xdt, cb * lmat) + _dot_nt(hp, cm) * jnp.exp(row_j)
                if prev_ref is not None:
                    y = y + prev_ref.at[_group_rows(g, 512)][rows, :]
                y_v[rows, :] = y
                tot_j = tot[j:j + 1, :]
                h_v[j] = jnp.exp(tot_j) * hp + _dot(xdt * jnp.exp(tot_j - row_j), bm)
            if tail is not None:
                rows = _group_rows(g, 512)
                zz = z_ref[rows, :]
                yg = (y_v[...] + skip_ref[rows, :] * x_v[...]) * (zz * _sigmoid(zz))
                rstd = lax.rsqrt(jnp.mean(yg * yg, axis=0, keepdims=True) + NORM_EPS)
                yn_ref[rows, :] = (yg * rstd * nw_ref[rows, :]).astype(yn_ref.dtype)
            return carry

        lax.fori_loop(0, SSD_GROUPS, group, 0)

    big = pl.BlockSpec((D_INNER, CHUNK), lambda c: (0, cidx(c)))
    par = pl.BlockSpec((D_INNER, 128), lambda c: (0, 0))
    in_specs = [big, pl.BlockSpec((512, CHUNK), lambda c: (4, cidx(c))), pl.BlockSpec((512, CHUNK), lambda c: (5, cidx(c))),
                pl.BlockSpec((SSD_HEADS, CHUNK), lambda c: (direction, cidx(c))),
                pl.BlockSpec((SSD_HEADS, 128), lambda c: (direction, 0))]
    args = [xbc_ct, xbc_ct, xbc_ct, dt_t, a_b]
    out_specs = [big, pl.BlockSpec((1, SSD_HEADS, SSD_HEAD_DIM, D_STATE), lambda c: (cidx(c), 0, 0, 0))]
    out_shape = [jax.ShapeDtypeStruct((D_INNER, t), f32), jax.ShapeDtypeStruct((nc, SSD_HEADS, SSD_HEAD_DIM, D_STATE), f32)]
    if prev is not None:
        in_specs.append(big)
        args.append(prev)
    if tail is not None:
        in_specs += [big, par, par]
        args += list(tail)
        out_specs.append(big)
        out_shape.append(jax.ShapeDtypeStruct((D_INNER, t), bf16))
    return pl.pallas_call(
        body, name=name, grid=(nc,), in_specs=in_specs, out_specs=out_specs, out_shape=out_shape,
        scratch_shapes=[pltpu.VMEM((SSD_HEADS, SSD_HEAD_DIM, D_STATE), f32)],
        compiler_params=_cparams(("arbitrary",)))(*args)


def ssd_bwd_t(xbc_ct, dt_t, a_b, dy_t, hprev, reverse, name, skip_b=None, prev=None):
    t = xbc_ct.shape[1]
    nc = t // CHUNK
    direction = 1 if reverse else 0

    def cidx(c):
        return c if reverse else nc - 1 - c

    def body(*refs):
        x_ref, b_ref, c_ref, dt_ref, a_ref, dy_ref, hp_ref = refs[0:7]
        pos = 7
        skip_ref = None
        if skip_b is not None:
            skip_ref = refs[pos]
            pos += 1
        prev_refs = None
        if prev is not None:
            prev_refs = refs[pos:pos + 3]
            pos += 3
        dx_ref, db_ref, dc_ref, ddt_ref, da_ref, dh_scr = refs[pos:pos + 6]

        @pl.when(pl.program_id(0) == 0)
        def _():
            dh_scr[...] = jnp.zeros_like(dh_scr)
            da_ref[...] = jnp.zeros_like(da_ref)

        def group(g, carry):
            big, st, heads = _group_rows(g, 512), _group_rows(g, 128), _group_rows(g, HEADS_PER_GROUP)
            x_v, dy_v, dx_v = x_ref.at[big], dy_ref.at[big], dx_ref.at[big]
            hp_v, dh_v = hp_ref.at[0, heads], dh_scr.at[heads]
            dt_blk = dt_ref[heads, :]
            a_blk = a_ref[heads, :]
            mask, tri, a8, acs, acs_t, lane = _ssd_decays(dt_blk, a_blk, reverse)
            sub = lax.broadcasted_iota(jnp.int32, (CHUNK, CHUNK), 0)
            mask_t = (sub >= lane) if reverse else (sub <= lane)
            bm = b_ref[st, :].T
            cm = c_ref[st, :].T
            cb = _dot_nt(cm, bm)
            cb_t = _dot_nt(bm, cm)
            tot = jnp.sum(a8, axis=1, keepdims=True)
            dcb = jnp.zeros((CHUNK, CHUNK), f32)
            dbm = jnp.zeros((CHUNK, D_STATE), f32)
            dcm = jnp.zeros((CHUNK, D_STATE), f32)
            dacs_rows, ddtx_rows = [], []
            for j in range(HEADS_PER_GROUP):
                rows = slice(SSD_HEAD_DIM * j, SSD_HEAD_DIM * (j + 1))
                col_j = _lane_col(acs, lane, j)
                row_j = acs_t[j:j + 1, :]
                dt_j = dt_blk[j:j + 1, :]
                tot_j = tot[j:j + 1, :]
                lmat = jnp.where(mask, jnp.exp(jnp.where(mask, col_j - row_j, 0.0)), 0.0)
                lmat_t = jnp.where(mask_t, jnp.exp(jnp.where(mask_t, row_j - col_j, 0.0)), 0.0)
                x = x_v[rows, :]
                xdt = x * dt_j
                dyh = dy_v[rows, :]
                hp = hp_v[j]
                dhn = dh_v[j]
                ml = _dot_tn(dyh, xdt) * lmat
                w_t = _dot_tn(xdt, dyh) * lmat_t * cb_t
                dcb = dcb + ml
                dacs = jnp.sum(w_t, axis=0, keepdims=True) - jnp.sum(ml * cb, axis=0, keepdims=True)
                ecol = jnp.exp(row_j)
                dec = jnp.exp(tot_j - row_j)
                dye = dyh * ecol
                yoff = _dot_nt(hp, cm) * ecol
                gmat = _dot_nt(dhn, bm)
                dxdt = _dot(dyh, cb * lmat) + dec * gmat
                s_dec = jnp.sum(xdt * gmat, axis=0, keepdims=True) * dec
                dacs = dacs + jnp.sum(dyh * yoff, axis=0, keepdims=True) - s_dec
                dcd = jnp.sum(jnp.sum(dhn * hp, axis=1, keepdims=True), axis=0, keepdims=True)
                dtot = jnp.sum(s_dec, axis=1, keepdims=True) + jnp.exp(tot_j) * dcd
                dacs_rows.append((dacs, dtot))
                ddtx_rows.append(jnp.sum(dxdt * x, axis=0, keepdims=True))
                dcm = dcm + _dot_tn(dye, hp)
                dbm = dbm + _dot_tn(xdt * dec, dhn)
                dxh = dxdt * dt_j
                if skip_ref is not None:
                    dxh = dxh + skip_ref.at[big][rows, :] * dyh
                if prev_refs is not None:
                    dxh = dxh + prev_refs[0].at[big][rows, :]
                dx_v[rows, :] = dxh
                dh_v[j] = jnp.exp(tot_j) * dhn + _dot(dye, cm)
            dcm = dcm + _dot(dcb, bm)
            dbm = dbm + _dot_tn(dcb, cm)
            dbt, dct = dbm.T, dcm.T
            if prev_refs is not None:
                dbt = dbt + prev_refs[1][st, :]
                dct = dct + prev_refs[2][st, :]
            db_ref[st, :] = dbt
            dc_ref[st, :] = dct
            dacs8 = jnp.concatenate([d for d, _ in dacs_rows], axis=0)
            dtot8 = jnp.concatenate([d for _, d in dacs_rows], axis=0)
            da8 = _dot_exact(dacs8, tri) + dtot8
            ddt_ref[heads, :] = da8 * a_blk + jnp.concatenate(ddtx_rows, axis=0)
            da_ref[heads, :] += da8 * dt_blk
            return carry

        lax.fori_loop(0, SSD_GROUPS, group, 0)

    big = pl.BlockSpec((D_INNER, CHUNK), lambda c: (0, cidx(c)))
    st = pl.BlockSpec((512, CHUNK), lambda c: (0, cidx(c)))
    in_specs = [big, pl.BlockSpec((512, CHUNK), lambda c: (4, cidx(c))), pl.BlockSpec((512, CHUNK), lambda c: (5, cidx(c))),
                pl.BlockSpec((SSD_HEADS, CHUNK), lambda c: (direction, cidx(c))),
                pl.BlockSpec((SSD_HEADS, 128), lambda c: (direction, 0)), big,
                pl.BlockSpec((1, SSD_HEADS, SSD_HEAD_DIM, D_STATE), lambda c: (cidx(c), 0, 0, 0))]
    args = [xbc_ct, xbc_ct, xbc_ct, dt_t, a_b, dy_t, hprev]
    if skip_b is not None:
        in_specs.append(pl.BlockSpec((D_INNER, 128), lambda c: (0, 0)))
        args.append(skip_b)
    if prev is not None:
        in_specs += [big, st, st]
        args += list(prev)
    return pl.pallas_call(
        body, name=name, grid=(nc,), in_specs=in_specs,
        out_specs=[big, st, st, pl.BlockSpec((SSD_HEADS, CHUNK), lambda c: (0, cidx(c))),
                   pl.BlockSpec((SSD_HEADS, 128), lambda c: (0, 0))],
        out_shape=[jax.ShapeDtypeStruct((D_INNER, t), f32), jax.ShapeDtypeStruct((512, t), f32),
                   jax.ShapeDtypeStruct((512, t), f32), jax.ShapeDtypeStruct((SSD_HEADS, t), f32),
                   jax.ShapeDtypeStruct((SSD_HEADS, 128), f32)],
        scratch_shapes=[pltpu.VMEM((SSD_HEADS, SSD_HEAD_DIM, D_STATE), f32)],
        compiler_params=_cparams(("arbitrary",)))(*args)


def tail_fwd_t(y_scan, xbc_ct, z_t, skip_b, nw_b, tb=512):
    t = y_scan.shape[1]
    reps = tb // 128

    def body(ys_ref, x_ref, z_ref, d_ref, w_ref, o_ref):
        zz = z_ref[...]
        y = (ys_ref[...] + _lanes(d_ref[...], reps) * x_ref[...]) * (zz * _sigmoid(zz))
        rstd = lax.rsqrt(jnp.mean(y * y, axis=0, keepdims=True) + NORM_EPS)
        o_ref[...] = (y * rstd * _lanes(w_ref[...], reps)).astype(o_ref.dtype)

    blk = pl.BlockSpec((512, tb), lambda g, i: (g, i))
    par = pl.BlockSpec((512, 128), lambda g, i: (g, 0))
    return pl.pallas_call(
        body, name="tail_fwd", grid=(SSD_GROUPS, t // tb), in_specs=[blk, blk, blk, par, par], out_specs=blk,
        out_shape=jax.ShapeDtypeStruct((D_INNER, t), bf16),
        compiler_params=_cparams(("parallel", "parallel")))(y_scan, xbc_ct, z_t, skip_b, nw_b)


def tail_bwd_t(dyn_t, y_scan, xbc_ct, z_t, skip_b, nw_b, tb=512):
    t = y_scan.shape[1]
    reps = tb // 128

    def body(g_ref, ys_ref, x_ref, z_ref, d_ref, w_ref, dy_ref, dz_ref, dw_ref, dd_ref):
        zz = z_ref[...]
        sg = _sigmoid(zz)
        sl = zz * sg
        x = x_ref[...]
        y = ys_ref[...] + _lanes(d_ref[...], reps) * x
        yz = y * sl
        rstd = lax.rsqrt(jnp.mean(yz * yz, axis=0, keepdims=True) + NORM_EPS)
        yhat = yz * rstd
        g = g_ref[...]
        dyhat = g * _lanes(w_ref[...], reps)
        dyz = rstd * (dyhat - yhat * jnp.mean(dyhat * yhat, axis=0, keepdims=True))
        dy = dyz * sl
        dy_ref[...] = dy
        dz_ref[...] = (dyz * y * sg * (1.0 + zz * (1.0 - sg))).astype(dz_ref.dtype)

        def fold(v):
            s = v[:, 0:128]
            for q in range(1, reps):
                s = s + v[:, 128 * q:128 * (q + 1)]
            return s

        @pl.when(pl.program_id(1) == 0)
        def _():
            dw_ref[...] = jnp.zeros_like(dw_ref)
            dd_ref[...] = jnp.zeros_like(dd_ref)

        dw_ref[...] += fold(g * yhat)
        dd_ref[...] += fold(dy * x)

    blk = pl.BlockSpec((512, tb), lambda g, i: (g, i))
    par = pl.BlockSpec((512, 128), lambda g, i: (g, 0))
    return pl.pallas_call(
        body, name="tail_bwd", grid=(SSD_GROUPS, t // tb), in_specs=[blk, blk, blk, blk, par, par],
        out_specs=[blk, blk, par, par],
        out_shape=[jax.ShapeDtypeStruct((D_INNER, t), f32), jax.ShapeDtypeStruct((SSD_COLS, t), bf16),
                   jax.ShapeDtypeStruct((D_INNER, 128), f32), jax.ShapeDtypeStruct((D_INNER, 128), f32)],
        compiler_params=_cparams(("parallel", "arbitrary")))(dyn_t, y_scan, xbc_ct, z_t, skip_b, nw_b)


def merge_fwd(u_gate, bg_row, y_ssd, y_att, tb=512):
    t = y_ssd.shape[0]

    def body(ga_ref, gb_ref, ba_ref, bb_ref, ys_ref, ya_ref, o_ref):
        o_ref[...] = (_sigmoid(ga_ref[...] + ba_ref[...]) * ys_ref[...]
                      + _sigmoid(gb_ref[...] + bb_ref[...]) * ya_ref[...]).astype(o_ref.dtype)

    blk = pl.BlockSpec((tb, 512), lambda i, j: (i, j))
    blk2 = pl.BlockSpec((tb, 512), lambda i, j: (i, 2 + j))
    row = pl.BlockSpec((1, 512), lambda i, j: (0, j))
    row2 = pl.BlockSpec((1, 512), lambda i, j: (0, 2 + j))
    return pl.pallas_call(
        body, name="merge_fwd", grid=(t // tb, 2), in_specs=[blk, blk2, row, row2, blk, blk], out_specs=blk,
        out_shape=jax.ShapeDtypeStruct((t, D_MODEL), bf16),
        compiler_params=_cparams(("parallel", "parallel")))(u_gate, u_gate, bg_row, bg_row, y_ssd, y_att)


def merge_bwd(dm, u_gate, bg_row, y_ssd, y_att, tb=512):
    t = dm.shape[0]

    def body(dm_ref, ga_ref, gb_ref, ba_ref, bb_ref, ys_ref, ya_ref, dys_ref, dya_ref, dga_ref, dgb_ref, dba_ref, dbb_ref):
        d = dm_ref[...]
        sa = _sigmoid(ga_ref[...] + ba_ref[...])
        sb = _sigmoid(gb_ref[...] + bb_ref[...])
        dys_ref[...] = (d * sa).astype(dys_ref.dtype)
        dya_ref[...] = (d * sb).astype(dya_ref.dtype)
        dla = d * ys_ref[...] * sa * (1.0 - sa)
        dlb = d * ya_ref[...] * sb * (1.0 - sb)
        dga_ref[...] = dla.astype(dga_ref.dtype)
        dgb_ref[...] = dlb.astype(dgb_ref.dtype)

        @pl.when(pl.program_id(1) == 0)
        def _():
            dba_ref[...] = jnp.zeros_like(dba_ref)
            dbb_ref[...] = jnp.zeros_like(dbb_ref)

        dba_ref[...] += jnp.sum(dla, axis=0, keepdims=True)
        dbb_ref[...] += jnp.sum(dlb, axis=0, keepdims=True)

    blk = pl.BlockSpec((tb, 512), lambda j, i: (i, j))
    blk2 = pl.BlockSpec((tb, 512), lambda j, i: (i, 2 + j))
    row = pl.BlockSpec((1, 512), lambda j, i: (0, j))
    row2 = pl.BlockSpec((1, 512), lambda j, i: (0, 2 + j))
    act = jax.ShapeDtypeStruct((t, D_MODEL), bf16)
    vec = jax.ShapeDtypeStruct((1, D_MODEL), f32)
    return pl.pallas_call(
        body, name="merge_bwd", grid=(2, t // tb), in_specs=[blk, blk, blk2, row, row2, blk, blk],
        out_specs=[blk, blk, blk, blk, row, row], out_shape=[act, act, act, act, vec, vec],
        compiler_params=_cparams(("parallel", "arbitrary")))(dm, u_gate, u_gate, bg_row, bg_row, y_ssd, y_att)


def _ln_stats(r):
    mu = jnp.mean(r, axis=1, keepdims=True)
    xc = r - mu
    rstd = lax.rsqrt(jnp.mean(xc * xc, axis=1, keepdims=True) + NORM_EPS)
    return xc * rstd, rstd


def _ln_bwd(dy, xhat, rstd, g_row):
    dxh = dy * g_row
    return rstd * (dxh - jnp.mean(dxh, axis=1, keepdims=True) - xhat * jnp.mean(dxh * xhat, axis=1, keepdims=True))


def ln1_fwd(x, mix, g_row, b_row, tb=512):
    t = x.shape[0]

    def body(x_ref, m_ref, g_ref, b_ref, o_ref, ob_ref):
        xhat, _ = _ln_stats(ALPHA * x_ref[...] + m_ref[...])
        h = xhat * g_ref[...] + b_ref[...]
        o_ref[...] = h
        ob_ref[...] = h.astype(ob_ref.dtype)

    blk = pl.BlockSpec((tb, D_MODEL), lambda i: (i, 0))
    row = pl.BlockSpec((1, D_MODEL), lambda i: (0, 0))
    return pl.pallas_call(body, name="ln1_fwd", grid=(t // tb,), in_specs=[blk, blk, row, row], out_specs=[blk, blk],
                          out_shape=[jax.ShapeDtypeStruct((t, D_MODEL), f32), jax.ShapeDtypeStruct((t, D_MODEL), bf16)],
                          compiler_params=_cparams(("parallel",)))(x, mix, g_row, b_row)


def ln1_bwd(dh, x, mix, g_row, tb=512):
    t = x.shape[0]

    def body(dh_ref, x_ref, m_ref, g_ref, dr_ref, drb_ref, dg_ref, db_ref):
        xhat, rstd = _ln_stats(ALPHA * x_ref[...] + m_ref[...])
        dy = dh_ref[...]
        dr = _ln_bwd(dy, xhat, rstd, g_ref[...])
        dr_ref[...] = dr
        drb_ref[...] = dr.astype(drb_ref.dtype)

        @pl.when(pl.program_id(0) == 0)
        def _():
            dg_ref[...] = jnp.zeros_like(dg_ref)
            db_ref[...] = jnp.zeros_like(db_ref)

        dg_ref[...] += jnp.sum(dy * xhat, axis=0, keepdims=True)
        db_ref[...] += jnp.sum(dy, axis=0, keepdims=True)

    blk = pl.BlockSpec((tb, D_MODEL), lambda i: (i, 0))
    row = pl.BlockSpec((1, D_MODEL), lambda i: (0, 0))
    return pl.pallas_call(
        body, name="ln1_bwd", grid=(t // tb,), in_specs=[blk, blk, blk, row], out_specs=[blk, blk, row, row],
        out_shape=[jax.ShapeDtypeStruct((t, D_MODEL), f32), jax.ShapeDtypeStruct((t, D_MODEL), bf16),
                   jax.ShapeDtypeStruct((1, D_MODEL), f32), jax.ShapeDtypeStruct((1, D_MODEL), f32)],
        compiler_params=_cparams(("arbitrary",)))(dh, x, mix, g_row)


def ln2_loss(h1, f, g_row, b_row, target, tb=512):
    t = h1.shape[0]

    def body(h_ref, f_ref, g_ref, b_ref, t_ref, dr_ref, drb_ref, dg_ref, db_ref, loss_ref):
        xhat, rstd = _ln_stats(ALPHA * h_ref[...] + f_ref[...])
        g = g_ref[...]
        err = xhat * g + b_ref[...] - t_ref[...]
        dy = err * (1.0 / D_MODEL)
        dr = _ln_bwd(dy, xhat, rstd, g)
        dr_ref[...] = dr
        drb_ref[...] = dr.astype(drb_ref.dtype)

        @pl.when(pl.program_id(0) == 0)
        def _():
            dg_ref[...] = jnp.zeros_like(dg_ref)
            db_ref[...] = jnp.zeros_like(db_ref)
            loss_ref[...] = jnp.zeros_like(loss_ref)

        dg_ref[...] += jnp.sum(dy * xhat, axis=0, keepdims=True)
        db_ref[...] += jnp.sum(dy, axis=0, keepdims=True)
        part = jnp.sum(jnp.mean(err * err, axis=1, keepdims=True), axis=0, keepdims=True)
        loss_ref[...] += 0.5 * part

    blk = pl.BlockSpec((tb, D_MODEL), lambda i: (i, 0))
    row = pl.BlockSpec((1, D_MODEL), lambda i: (0, 0))
    return pl.pallas_call(
        body, name="ln2_loss", grid=(t // tb,), in_specs=[blk, blk, row, row, blk],
        out_specs=[blk, blk, row, row, pl.BlockSpec((8, 128), lambda i: (0, 0))],
        out_shape=[jax.ShapeDtypeStruct((t, D_MODEL), f32), jax.ShapeDtypeStruct((t, D_MODEL), bf16),
                   jax.ShapeDtypeStruct((1, D_MODEL), f32), jax.ShapeDtypeStruct((1, D_MODEL), f32),
                   jax.ShapeDtypeStruct((8, 128), f32)],
        compiler_params=_cparams(("arbitrary",)))(h1, f, g_row, b_row, target)


TAIL_BLOCK, TAIL_AT = divmod(OFF_TAIL, PACK_TILE)


def _sum4(ref):
    return ((ref[0].astype(f32) + ref[1].astype(f32)) + ref[2].astype(f32)) + ref[3].astype(f32)


def _adamw_update(g, w_ref, m_ref, v_ref, g_ref, d_ref, nm_ref, nv_ref):
    c1 = 1.0 - ADAM_B1 ** ADAM_STEP
    c2 = 1.0 - ADAM_B2 ** ADAM_STEP
    nm = ADAM_B1 * m_ref[...] + (1.0 - ADAM_B1) * g
    nv = ADAM_B2 * v_ref[...] + (1.0 - ADAM_B2) * (g * g)
    g_ref[...] = g
    nm_ref[...] = nm
    nv_ref[...] = nv
    d_ref[...] = -ADAM_LR * ((nm / c1) / (jnp.sqrt(nv / c2) + ADAM_EPS) + ADAM_WD * w_ref[...])


def adamw_early(landed, parts, me, w, m, v):
    off = LATE_ROWS // EARLY_TILE

    def body(me_ref, *refs):
        src = refs[0:N_DEV]
        own_ref, w_ref, m_ref, v_ref = refs[N_DEV:N_DEV + 4]
        mine = me_ref[0]
        g = None
        for s in range(N_DEV):
            term = jnp.where(mine == s, own_ref[0], src[s][0])
            g = term if g is None else g + term
        _adamw_update(g, w_ref, m_ref, v_ref, *refs[N_DEV + 4:])

    def slot(s):
        return pl.BlockSpec((1, EARLY_TILE, 1024), lambda i, me_ref: (jnp.where(me_ref[0] == s, (s + 1) % N_DEV, s), i, 0))

    shard = pl.BlockSpec((EARLY_TILE, 1024), lambda i, me_ref: (i + off, 0))
    out_blk = pl.BlockSpec((EARLY_TILE, 1024), lambda i, me_ref: (i, 0))
    grid_spec = pltpu.PrefetchScalarGridSpec(
        num_scalar_prefetch=1, grid=(EARLY_ROWS // EARLY_TILE,),
        in_specs=[slot(s) for s in range(N_DEV)]
        + [pl.BlockSpec((1, EARLY_TILE, 1024), lambda i, me_ref: (me_ref[0], i, 0)), shard, shard, shard],
        out_specs=[out_blk] * 4)
    out = jax.ShapeDtypeStruct((EARLY_ROWS, 1024), f32)
    return pl.pallas_call(body, name="adamw_early", grid_spec=grid_spec, out_shape=[out] * 4,
                          compiler_params=_cparams(("parallel",)))(me, *([landed] * N_DEV), parts, w, m, v)


def adamw(parts, tails, w, m, v):
    rows = parts.shape[1]

    def body(p_ref, t_ref, w_ref, m_ref, v_ref, g_ref, d_ref, nm_ref, nv_ref):
        g = _sum4(p_ref)
        with_tail = jnp.concatenate([g[0:TAIL_AT], _sum4(t_ref), g[TAIL_AT + ROWS_TAIL:]], axis=0)
        g = jnp.where(pl.program_id(0) == TAIL_BLOCK, with_tail, g)
        _adamw_update(g, w_ref, m_ref, v_ref, g_ref, d_ref, nm_ref, nv_ref)

    blk = pl.BlockSpec((PACK_TILE, 1024), lambda i: (i, 0))
    out = jax.ShapeDtypeStruct((rows, 1024), f32)
    return pl.pallas_call(
        body, name="adamw", grid=(rows // PACK_TILE,),
        in_specs=[pl.BlockSpec((4, PACK_TILE, 1024), lambda i: (0, i, 0)),
                  pl.BlockSpec((4, ROWS_TAIL, 1024), lambda i: (0, 0, 0)), blk, blk, blk], out_specs=[blk] * 4,
        out_shape=[out] * 4, compiler_params=_cparams(("parallel",)))(parts, tails, w, m, v)


def pair_sum(parts, recv, core):
    rows = parts.shape[1]

    def body(c_ref, a_ref, b_ref, o_ref, t_ref):
        s = a_ref[...] + b_ref[...]
        o_ref[...] = s.astype(o_ref.dtype)

        @pl.when(pl.program_id(1) == TAIL_BLOCK)
        def _():
            t_ref[...] = s[:, TAIL_AT:TAIL_AT + ROWS_TAIL]

    grid_spec = pltpu.PrefetchScalarGridSpec(
        num_scalar_prefetch=1, grid=(4, rows // PACK_TILE),
        in_specs=[pl.BlockSpec((1, PACK_TILE, 1024), lambda j, i, c_ref: (2 * j + c_ref[0], i, 0)),
                  pl.BlockSpec((1, PACK_TILE, 1024), lambda j, i, c_ref: (j, i, 0))],
        out_specs=[pl.BlockSpec((1, PACK_TILE, 1024), lambda j, i, c_ref: (j, i, 0)),
                   pl.BlockSpec((1, ROWS_TAIL, 1024), lambda j, i, c_ref: (j, 0, 0))])
    return pl.pallas_call(
        body, name="pair_sum", grid_spec=grid_spec,
        out_shape=[jax.ShapeDtypeStruct(recv.shape, bf16), jax.ShapeDtypeStruct((4, ROWS_TAIL, 1024), f32)],
        compiler_params=_cparams(("parallel", "arbitrary")))(core, parts, recv)


def _place():
    return lax.axis_index("x"), lax.axis_index("y"), lax.axis_index("c")


def all_gather_blocks(shard):
    rows, cols = shard.shape

    def body(x_ref, out_ref, send_sems, recv_sems, local_sem):
        x, y, c = _place()
        me, sibling = (x, y, c), (x, y, 1 - c)
        chips = [(1 - x, y), (x, 1 - y), (1 - x, 1 - y)]

        def slot(px, py, pc):
            return out_ref.at[4 * px + 2 * py + pc]

        def copy(k, block, to, src=None):
            return pltpu.make_async_remote_copy(
                src_ref=slot(*block) if src is None else src, dst_ref=slot(*block), send_sem=send_sems.at[k],
                recv_sem=recv_sems.at[k], device_id=to, device_id_type=MESH)

        mine = pltpu.make_async_copy(x_ref, slot(*me), local_sem)
        mine.start()
        first = [copy(0, me, sibling, src=x_ref)]
        first += [copy(1 + j, me, (*chip, c), src=x_ref) for j, chip in enumerate(chips)]
        for cp in first:
            cp.start()
        passed = [copy(4 + j, (*chip, c), sibling) for j, chip in enumerate(chips)]
        for j, chip in enumerate(chips):
            copy(1 + j, (*chip, c), me).wait_recv()
            passed[j].start()
        copy(0, sibling, me).wait_recv()
        for j, chip in enumerate(chips):
            copy(4 + j, (*chip, 1 - c), me).wait_recv()
        for cp in first + passed:
            cp.wait_send()
        mine.wait()

    return pl.pallas_call(
        body, name="all_gather_blocks", out_shape=jax.ShapeDtypeStruct((N_DEV, rows, cols), shard.dtype),
        in_specs=[pl.BlockSpec(memory_space=pl.ANY)], out_specs=pl.BlockSpec(memory_space=pl.ANY),
        scratch_shapes=[pltpu.SemaphoreType.DMA((7,)), pltpu.SemaphoreType.DMA((7,)), pltpu.SemaphoreType.DMA],
        compiler_params=pltpu.CompilerParams(has_side_effects=True))(shard)


def pair_exchange(parts):
    _, rows, cols = parts.shape

    def body(p_ref, recv_ref, send_sems, recv_sems):
        x, y, c = _place()
        copies = [pltpu.make_async_remote_copy(
            src_ref=p_ref.at[2 * j + 1 - c], dst_ref=recv_ref.at[j], send_sem=send_sems.at[j], recv_sem=recv_sems.at[j],
            device_id=(x, y, 1 - c), device_id_type=MESH) for j in range(4)]
        for cp in copies:
            cp.start()
        for cp in copies:
            cp.wait_recv()
        for cp in copies:
            cp.wait_send()

    return pl.pallas_call(
        body, name="pair_exchange", out_shape=jax.ShapeDtypeStruct((4, rows, cols), parts.dtype),
        in_specs=[pl.BlockSpec(memory_space=pl.ANY)], out_specs=pl.BlockSpec(memory_space=pl.ANY),
        scratch_shapes=[pltpu.SemaphoreType.DMA((4,)), pltpu.SemaphoreType.DMA((4,))],
        compiler_params=pltpu.CompilerParams(has_side_effects=True))(parts)


def chip_exchange(parts):
    n = len(parts)

    def body(*refs):
        p_refs, out_refs = refs[0:n], refs[n:2 * n]
        send_sems, recv_sems, local_sems = refs[2 * n:]
        x, y, c = _place()
        mine = 2 * x + y
        flips = [(x, 1 - y), (1 - x, y), (1 - x, 1 - y)]

        def copy(a, k, src_slot, dst_slot):
            px, py = flips[k]
            return pltpu.make_async_remote_copy(
                src_ref=p_refs[a].at[src_slot], dst_ref=out_refs[a].at[dst_slot], send_sem=send_sems.at[3 * a + k],
                recv_sem=recv_sems.at[3 * a + k], device_id=(px, py, c), device_id_type=MESH)

        local = [pltpu.make_async_copy(p_refs[a].at[mine], out_refs[a].at[mine], local_sems.at[a]) for a in range(n)]
        sends = [copy(a, k, 2 * flips[k][0] + flips[k][1], mine) for a in range(n) for k in range(3)]
        for cp in local + sends:
            cp.start()
        for a in range(n):
            for k in range(3):
                copy(a, k, mine, 2 * flips[k][0] + flips[k][1]).wait_recv()
        for cp in sends:
            cp.wait_send()
        for cp in local:
            cp.wait()

    return pl.pallas_call(
        body, name="chip_exchange", out_shape=[jax.ShapeDtypeStruct(p.shape, p.dtype) for p in parts],
        in_specs=[pl.BlockSpec(memory_space=pl.ANY)] * n, out_specs=[pl.BlockSpec(memory_space=pl.ANY)] * n,
        scratch_shapes=[pltpu.SemaphoreType.DMA((3 * n,)), pltpu.SemaphoreType.DMA((3 * n,)), pltpu.SemaphoreType.DMA((n,))],
        compiler_params=pltpu.CompilerParams(has_side_effects=True))(*parts)


_HBM = pl.BlockSpec(memory_space=pltpu.HBM)
_SEM = pl.BlockSpec(memory_space=pltpu.SEMAPHORE)


def _peer(k):
    x, y, c = _place()
    px, py, pc = (1 - x if k & 4 else x), (1 - y if k & 2 else y), (1 - c if k & 1 else c)
    return (px, py, pc), 4 * px + 2 * py + pc


def scatter_start(parts, name):
    per_device = parts.ndim == 3

    def body(p_ref, land_ref, send_sems, recv_sems, p_thru, land_thru, token):
        x, y, c = _place()
        me = 4 * x + 2 * y + c
        for k in range(1, N_DEV):
            place, idx = _peer(k)
            pltpu.make_async_remote_copy(src_ref=p_ref.at[idx] if per_device else p_ref, dst_ref=land_ref.at[me],
                                         send_sem=send_sems.at[k - 1], recv_sem=recv_sems.at[k - 1], device_id=place,
                                         device_id_type=MESH).start()
        token[...] = jnp.zeros_like(token)

    land_shape = parts.shape if per_device else (N_DEV,) + parts.shape
    landing = lax.empty(land_shape, parts.dtype)
    return pl.pallas_call(
        body, name=name,
        out_shape=(pltpu.SemaphoreType.DMA((N_DEV - 1,)), pltpu.SemaphoreType.DMA((N_DEV - 1,)),
                   pltpu.HBM(parts.shape, parts.dtype), pltpu.HBM(land_shape, parts.dtype),
                   jax.ShapeDtypeStruct((8, 128), f32)),
        in_specs=(_HBM, _HBM), out_specs=(_SEM, _SEM, _HBM, _HBM, pl.BlockSpec(memory_space=pltpu.VMEM)),
        input_output_aliases={0: 2, 1: 3},
        compiler_params=pltpu.CompilerParams(has_side_effects=pltpu.SideEffectType.DATAFLOW_SIDE_EFFECTING),
    )(pltpu.with_memory_space_constraint(parts, pltpu.HBM), pltpu.with_memory_space_constraint(landing, pltpu.HBM))


def scatter_wait(send_sems, recv_sems, parts_thru, land_thru, after, name):
    per_device = parts_thru.ndim == 3

    def body(p_ref, land_ref, send_sems, recv_sems, after_ref, p_out, land_out):
        for k in range(1, N_DEV):
            place, idx = _peer(k)
            copy = pltpu.make_async_remote_copy(src_ref=p_ref.at[idx] if per_device else p_ref, dst_ref=land_ref.at[idx],
                                                send_sem=send_sems.at[k - 1], recv_sem=recv_sems.at[k - 1],
                                                device_id=place, device_id_type=MESH)
            copy.wait_send()
            copy.wait_recv()

    return pl.pallas_call(
        body, name=name,
        out_shape=(pltpu.HBM(parts_thru.shape, parts_thru.dtype), pltpu.HBM(land_thru.shape, land_thru.dtype)),
        in_specs=(_HBM, _HBM, _SEM, _SEM, pl.BlockSpec(memory_space=pl.ANY)), out_specs=(_HBM, _HBM),
        input_output_aliases={0: 0, 1: 1},
        compiler_params=pltpu.CompilerParams(has_side_effects=pltpu.SideEffectType.DATAFLOW_SIDE_EFFECTING),
    )(parts_thru, land_thru, send_sems, recv_sems, after)


def _tail_rows(conv_part, small, extra):
    lead = conv_part.shape[:-1]
    rep = jnp.concatenate([small[n].reshape(-1).astype(f32) for n in SMALL] + [extra.reshape(1).astype(f32)])
    flat = jnp.concatenate([conv_part, jnp.broadcast_to(rep, lead + rep.shape),
                            jnp.zeros(lead + (ROWS_TAIL * 1024 - TAIL_ELEMS,), f32)], axis=-1)
    return flat.reshape(lead + (ROWS_TAIL, 1024))


def _late_rows(w_in_t, tail):
    lead = tail.shape[:-2]
    zeros = lambda r: jnp.zeros(lead + (r, 1024), f32)
    return jnp.concatenate([w_in_t, zeros(OFF_TAIL - IN_SHARD), tail, zeros(LATE_ROWS - OFF_TAIL - ROWS_TAIL)], axis=-2)


def _early_rows(w_ps, w_out, w_up_t, w_down, w_pa_t):
    return jnp.concatenate([w_ps, w_out, w_up_t, w_down, w_pa_t.reshape(w_pa_t.shape[:-2] + (ROWS_PA, 1024))], axis=-2)


def _pack_shard(vals):
    tail = _tail_rows(vals["conv_w"].reshape(-1), vals, jnp.zeros((), f32))
    return jnp.concatenate([_late_rows(vals["w_in"].T, tail),
                            _early_rows(vals["w_proj_ssd"], vals["w_out"], vals["w_up"].T, vals["w_down"],
                                        vals["w_proj_attn"].T)], axis=0)


def _unpack_shard(late, early):
    e = lambda lo, hi: early[lo - LATE_ROWS:hi - LATE_ROWS]
    out = {"w_in": late[0:IN_SHARD].T, "w_proj_ssd": e(OFF_PS, OFF_OUT), "w_out": e(OFF_OUT, OFF_UP),
           "w_up": e(OFF_UP, OFF_DOWN).T, "w_down": e(OFF_DOWN, OFF_PA),
           "w_proj_attn": e(OFF_PA, PACK_ROWS).reshape(D_MODEL // N_DEV, ATTN_OUT).T}
    flat = late[OFF_TAIL:OFF_TAIL + ROWS_TAIL].reshape(-1)
    out["conv_w"] = flat[0:CONV_SHARD].reshape(D_CONV, CONV_DIM // N_DEV)
    off = CONV_SHARD
    for n in SMALL:
        out[n] = flat[off:off + SMALL_SIZES[n]]
        off += SMALL_SIZES[n]
    out["_extra"] = flat[off]
    return out


def _blocks(g):
    return g.reshape(N_DEV, g.shape[0] // N_DEV, g.shape[1])


def _pack_early_parts(full):
    return _early_rows(_blocks(full["w_proj_ssd"]), _blocks(full["w_out"]), _blocks(full["w_up_t"]),
                       _blocks(full["w_down"]), _blocks(full["w_proj_attn_t"]))


def _pack_late_parts(full, small, extra):
    conv = full["conv_w"].reshape(D_CONV, N_DEV, CONV_DIM // N_DEV).transpose(1, 0, 2).reshape(N_DEV, CONV_SHARD)
    return _late_rows(_blocks(full["w_in_t"]), _tail_rows(conv, small, extra))


def _gather_weights(w):
    conv_bits = lax.bitcast_convert_type(w["conv_w"], bf16).reshape(-1)
    conv_rows = jnp.concatenate([conv_bits, jnp.zeros((16 * 1024 - 2 * CONV_SHARD,), bf16)]).reshape(16, 1024)
    packed = _pack_shard(w)
    first = OFF_TAIL + ROWS_TAIL
    got = all_gather_blocks(jnp.concatenate([packed[0:OFF_TAIL].astype(bf16), conv_rows], axis=0))
    got, rest = lax.optimization_barrier((got, packed[first:].astype(bf16)))
    send_sems, recv_sems, rest_thru, land_thru, token = scatter_start(rest, "gather_start")
    conv =lax.bitcast_convert_type(got[:, OFF_TAIL:OFF_TAIL + 4].reshape(N_DEV, 4096)[:, 0:2 * CONV_SHARD]
                                    .reshape(N_DEV, D_CONV, CONV_DIM // N_DEV, 2), f32)
    now = {"w_in_t": got[:, 0:IN_SHARD].reshape(IN_COLS, 1024), "conv_w": conv.transpose(1, 0, 2).reshape(D_CONV, CONV_DIM)}

    def later(after):
        mine, landed = scatter_wait(send_sems, recv_sems, rest_thru, land_thru, after, "gather_wait")
        x, y, c = _place()
        landed = lax.dynamic_update_slice(landed, mine[None], (4 * x + 2 * y + c, 0, 0))
        whole = lambda lo, hi: landed[:, lo - first:hi - first].reshape(N_DEV * (hi - lo), 1024)
        return {"w_proj_ssd": whole(OFF_PS, OFF_OUT), "w_out": whole(OFF_OUT, OFF_UP), "w_up_t": whole(OFF_UP, OFF_DOWN),
                "w_down": whole(OFF_DOWN, OFF_PA),
                "w_proj_attn_t": landed[:, OFF_PA - first:PACK_ROWS - first].reshape(D_MODEL, ATTN_OUT)}

    return now, later, token


def _row(v, width=None):
    v = v.reshape(1, -1).astype(f32)
    return v if width is None else jnp.pad(v, ((0, 0), (0, width - v.shape[1])))


def _lanes256(vf, vb):
    z = jnp.zeros((96,), f32)
    return jnp.concatenate([vf.astype(f32), z, vb.astype(f32), z]).reshape(1, 256)


def _local_step(x2, tgt, wf, p, send_early=None, late_weights=None, start_token=None):
    t = x2.shape[0]
    o = np.cumsum((0,) + IN_SPLITS)
    wt = wf["w_in_t"]
    wt_z, wt_xbc, wt_dt = wt[o[0]:o[1]], wt[o[1]:o[2]], wt[o[2]:o[4]]
    wt_qkv, wt_gate = wt[o[4]:o[7]], wt[o[7]:o[8]]

    spread = lambda v: jnp.broadcast_to(v.astype(f32)[..., None], v.shape + (128,))
    conv_w_b, conv_b_b = spread(wf["conv_w"]), spread(p["conv_b"])
    dt_bias_b = spread(jnp.concatenate([p["dt_bias_f"], p["dt_bias_b"]]))
    a_f, a_b = -jnp.exp(p["a_log_f"].astype(f32)), -jnp.exp(p["a_log_b"].astype(f32))
    a_coef_b = spread(jnp.concatenate([a_f, a_b]))
    skip_b = spread(jnp.repeat(p["d_skip"], SSD_HEAD_DIM))
    nw_b, bg_row = spread(p["ssd_norm_w"]), _row(p["b_gate"])
    g1, b1, g2, b2 = _row(p["ln1_g"]), _row(p["ln1_b"]), _row(p["ln2_g"]), _row(p["ln2_b"])

    xb = (x2 if start_token is None else x2 + start_token[0, 0]).astype(MXU_DTYPE)
    xt = xb.T
    u_z = mm_nn(wt_z, xt, "in_z")
    u_xbc = mm_nn(wt_xbc, xt, "in_xbc")
    u_dt = mm_nn(wt_dt, xt, "in_dt")
    u_qkv = mm_nt_split(xb, wt_qkv, "in_qkv", 256, bf16)
    u_gate = mm_nt(xb, wt_gate, "in_gate")
    xbc_c = conv_fwd_t(u_xbc, conv_w_b, conv_b_b)
    dt_t = dt_fwd_t(u_dt, dt_bias_b)
    y_f, h_f = ssd_fwd_t(xbc_c, dt_t, a_coef_b, False, "ssd_fwd_f")
    y_scan, h_b, yn = ssd_fwd_t(xbc_c, dt_t, a_coef_b, True, "ssd_fwd_b", prev=y_f, tail=(u_z, skip_b, nw_b))
    if late_weights is not None:
        wf = {**wf, **late_weights(yn)}
    y_ssd = mm_tn(yn, wf["w_proj_ssd"], "proj_ssd")

    def strided(a, dil):
        return a.reshape(t // dil, dil * 256)

    qkv, outs, lses = [], [], []
    for pi, (_, dil) in enumerate(DIL_PATTERNS):
        q, k, v = (strided(u_qkv[N_PATTERNS * s + pi], dil) for s in range(3))
        qkv.append((q, k, v))
        op, lp = attn_fwd(q, k, v, pi, dil, f"attn_fwd_{pi}")
        outs.append(op.reshape(t, 256))
        lses.append(lp.reshape(t, 256))
    ya, lse = attn_combine(outs, lses)
    y_att = mm_nt(ya, wf["w_proj_attn_t"], "proj_attn")
    m = merge_fwd(u_gate, bg_row, y_ssd, y_att)
    mix = mm_nn(m, wf["w_out"], "out_proj")
    h1, h1b = ln1_fwd(x2, mix, g1, b1)
    r_up, p_act = mm_nt(h1b, wf["w_up_t"], "mlp_up", relu2=True)
    f_dn = mm_nn(p_act, wf["w_down"], "mlp_down")
    dr2, dr2b, dg2, db2, loss8 = ln2_loss(h1, f_dn, g2, b2, tgt)

    full, small = {}, {}
    da = mm_nt(dr2b, wf["w_down"], "d_mlp_act", out_dtype=bf16, relu2_of=r_up)
    full["w_down"] = mm_tn(p_act, dr2b, "dw_down")
    full["w_up_t"] = mm_tn(da, h1b, "dw_up")
    dh1 = mm_nn(da, wf["w_up_t"], "d_h1", acc_in=dr2, acc_scale=ALPHA)
    dr1, dr1b, dg1, db1 = ln1_bwd(dh1, x2, mix, g1)
    dm = mm_nt(dr1b, wf["w_out"], "d_merge")
    full["w_out"] = mm_tn(m, dr1b, "dw_out")
    dys, dya_p, dga, dgb, dba, dbb = merge_bwd(dm, u_gate, bg_row, y_ssd, y_att)
    dyn = mm_nt(wf["w_proj_ssd"], dys, "d_yn")
    full["w_proj_ssd"] = mm_nn(yn, dys, "dw_proj_ssd")
    dya = mm_nn(dya_p, wf["w_proj_attn_t"], "d_ya")
    full["w_proj_attn_t"] = mm_tn(dya_p, ya, "dw_proj_attn")
    if send_early is not None:
        skip_b = skip_b + send_early(full)[0, 0]

    dy, du_ssd, dnw, ddx = tail_bwd_t(dyn, y_scan, xbc_c, u_z, skip_b, nw_b)
    dxf, dbf, dcf, ddtf, daf = ssd_bwd_t(xbc_c, dt_t, a_coef_b, dy, h_f, False, "ssd_bwd_f", skip_b=skip_b)
    dxs, dbs, dcs, ddtb, dab = ssd_bwd_t(xbc_c, dt_t, a_coef_b, dy, h_b, True, "ssd_bwd_b", prev=(dxf, dbf, dcf))
    du_ssd, dcw_x, dcb_x = conv_bwd_t(u_xbc, dxs, conv_w_b, conv_b_b, du_ssd, "conv_bwd_x", 0)
    du_ssd, dcw_b, dcb_b = conv_bwd_t(u_xbc, dbs, conv_w_b, conv_b_b, du_ssd, "conv_bwd_b", D_INNER)
    du_ssd, dcw_c, dcb_c = conv_bwd_t(u_xbc, dcs, conv_w_b, conv_b_b, du_ssd, "conv_bwd_c", D_INNER + 512)
    du_ssd, dbias = dt_bwd_t(ddtf, ddtb, u_dt, dt_bias_b, du_ssd)

    delta = attn_delta(dya, ya)
    dqs, dks, dvs = [], [], []
    for pi, (_, dil) in enumerate(DIL_PATTERNS):
        q, k, v = qkv[pi]
        sd, sl_, sdel = strided(dya, dil), strided(lse, dil), strided(delta, dil)
        dqs.append(attn_dq(q, k, v, sd, sl_, sdel, pi, dil, f"attn_dq_{pi}").reshape(t, 256))
        dk, dv = attn_dkv(q, k, v, sd, sl_, sdel, pi, dil, f"attn_dkv_{pi}")
        dks.append(dk.reshape(t, 256))
        dvs.append(dv.reshape(t, 256))
    du_qkv = jnp.concatenate(dqs + dks + dvs, axis=1)
    du_gate = jnp.concatenate([dga, dgb], axis=1)

    dx = mm_tn(du_ssd, wt[0:SSD_COLS], "dx_ssd", acc_in=dr1, acc_scale=ALPHA)
    dx = mm_nn(du_qkv, wt_qkv, "dx_qkv", acc_in=dx)
    dx = mm_nn(du_gate, wt_gate, "dx_gate", acc_in=dx)
    full["w_in_t"] = jnp.concatenate(
        [mm_nn(du_ssd, xb, "dw_in_ssd"), mm_tn(du_qkv, xb, "dw_in_qkv"), mm_tn(du_gate, xb, "dw_in_gate")], axis=0)
    lanes = lambda v: jnp.sum(v, axis=-1)
    full["conv_w"] = jnp.concatenate([lanes(dcw_x), lanes(dcw_b), lanes(dcw_c)], axis=1)

    small["b_gate"] = jnp.concatenate([dba, dbb], axis=1)
    small["conv_b"] = jnp.concatenate([lanes(dcb_x), lanes(dcb_b), lanes(dcb_c)])
    dbias = lanes(dbias)
    small["dt_bias_f"], small["dt_bias_b"] = dbias[0:32], dbias[32:64]
    small["a_log_f"] = lanes(daf) * a_f
    small["a_log_b"] = lanes(dab) * a_b
    small["d_skip"] = jnp.sum(lanes(ddx).reshape(SSD_HEADS, SSD_HEAD_DIM), axis=1)
    small["ssd_norm_w"] = lanes(dnw)
    small["ln1_g"], small["ln1_b"], small["ln2_g"], small["ln2_b"] = dg1, db1, dg2, db2
    return loss8[0, 0], dx, full, small


def kernel(x, w_in, b_gate, conv_w, conv_b, dt_bias_f, dt_bias_b, a_log_f, a_log_b, d_skip, ssd_norm_w, w_proj_ssd, w_proj_attn, w_out, ln1_g, ln1_b, w_up, w_down, ln2_g, ln2_b, loss_target, m_w_in, m_b_gate, m_conv_w, m_conv_b, m_dt_bias_f, m_dt_bias_b, m_a_log_f, m_a_log_b, m_d_skip, m_ssd_norm_w, m_w_proj_ssd, m_w_proj_attn, m_w_out, m_ln1_g, m_ln1_b, m_w_up, m_w_down, m_ln2_g, m_ln2_b, v_w_in, v_b_gate, v_conv_w, v_conv_b, v_dt_bias_f, v_dt_bias_b, v_a_log_f, v_a_log_b, v_d_skip, v_ssd_norm_w, v_w_proj_ssd, v_w_proj_attn, v_w_out, v_ln1_g, v_ln1_b, v_w_up, v_w_down, v_ln2_g, v_ln2_b):
    given = dict(locals())
    w = {n: given[n] for n in WEIGHTS}
    mom = {n: given["m_" + n] for n in WEIGHTS}
    var = {n: given["v_" + n] for n in WEIGHTS}
    t = x.shape[1]
    wf, late_weights, start_token = _gather_weights(w)
    in_flight = []

    def send_early(full):
        send_sems, recv_sems, parts_thru, land_thru, token = scatter_start(_pack_early_parts(full), "scatter_start")
        in_flight.append((send_sems, recv_sems, parts_thru, land_thru))
        return token

    loss, dx, full, small = _local_step(x.reshape(t, D_MODEL), loss_target.reshape(t, D_MODEL), wf, w, send_early,
                                        late_weights, start_token)
    late = _pack_late_parts(full, small, loss)
    x_, y_, c_ = _place()
    core = c_.astype(jnp.int32).reshape(1)
    me = (4 * x_ + 2 * y_ + c_).astype(jnp.int32).reshape(1)
    wp, mp, vp = _pack_shard(w), _pack_shard(mom), _pack_shard(var)
    early_parts, landed = scatter_wait(*in_flight[0], late, "scatter_wait")
    early_out = adamw_early(landed, early_parts, me, wp, mp, vp)
    parts, tails = chip_exchange(pair_sum(late, pair_exchange(late), core))
    late_out = adamw(parts, tails, wp, mp, vp)
    g, delta, new_m, new_v = (_unpack_shard(a, b) for a, b in zip(late_out, early_out))
    outs = [g["_extra"], dx.reshape(x.shape)]
    for d in (g, delta, new_m, new_v):
        outs += [d[n].reshape(w[n].shape) for n in WEIGHTS]
    return tuple(outs)
```

```python
import functools
import math

import jax
import jax.numpy as jnp
import numpy as np
from jax import lax
from jax.experimental import pallas as pl
from jax.experimental.pallas import tpu as pltpu

f32 = jnp.float32
bf16 = jnp.bfloat16
MXU_DTYPE = jnp.bfloat16

N_DEV = 8
D_MODEL = 1024
D_INNER = 2048
SSD_HEADS = 32
SSD_HEAD_DIM = 64
SSD_GROUPS = 4
D_STATE = 128
D_CONV = 5
CHUNK = 128
CONV_DIM = D_INNER + 2 * SSD_GROUPS * D_STATE
NORM_EPS = 1e-5
ATTN_HEAD_DIM = 64
DIL_PATTERNS = ((128, 1), (512, 4), (2048, 16))
N_PATTERNS = len(DIL_PATTERNS)
HEADS_PER_PATTERN = 4
ATTN_HEADS = 12
ATTN_WIDTH = 768
ATTN_OUT = 256
D_FF = 4096
ALPHA = 2.0 ** 0.25
IN_SPLITS = (D_INNER, CONV_DIM, SSD_HEADS, SSD_HEADS, ATTN_WIDTH, ATTN_WIDTH, ATTN_WIDTH, 2 * D_MODEL)
IN_COLS = sum(IN_SPLITS)
SSD_COLS = sum(IN_SPLITS[0:4])
ADAM_LR, ADAM_B1, ADAM_B2, ADAM_EPS, ADAM_WD, ADAM_STEP = 0.001, 0.9, 0.999, 1e-08, 0.01, 10
NEG_BIG = -1e30
VMEM_LIMIT = 56 * 1024 * 1024
MESH = pl.DeviceIdType.MESH

SMALL = ("b_gate", "conv_b", "dt_bias_f", "dt_bias_b", "a_log_f", "a_log_b", "d_skip", "ssd_norm_w",
         "ln1_g", "ln1_b", "ln2_g", "ln2_b")
WEIGHTS = ("w_in", "b_gate", "conv_w", "conv_b", "dt_bias_f", "dt_bias_b", "a_log_f", "a_log_b", "d_skip",
           "ssd_norm_w", "w_proj_ssd", "w_proj_attn", "w_out", "ln1_g", "ln1_b", "w_up", "w_down", "ln2_g", "ln2_b")
SMALL_SIZES = {"b_gate": 2 * D_MODEL, "conv_b": CONV_DIM, "dt_bias_f": 32, "dt_bias_b": 32, "a_log_f": 32, "a_log_b": 32,
               "d_skip": 32, "ssd_norm_w": D_INNER, "ln1_g": D_MODEL, "ln1_b": D_MODEL, "ln2_g": D_MODEL, "ln2_b": D_MODEL}
IN_SHARD = IN_COLS // N_DEV
OFF_TAIL = 1200
ROWS_TAIL = 16
PACK_TILE = 128
LATE_ROWS = 1280
ROWS_PS, ROWS_OUT, ROWS_UP, ROWS_DOWN, ROWS_PA = D_INNER // N_DEV, D_MODEL // N_DEV, D_FF // N_DEV, D_FF // N_DEV, 32
OFF_PS = LATE_ROWS
OFF_OUT = OFF_PS + ROWS_PS
OFF_UP = OFF_OUT + ROWS_OUT
OFF_DOWN = OFF_UP + ROWS_UP
OFF_PA = OFF_DOWN + ROWS_DOWN
PACK_ROWS = OFF_PA + ROWS_PA
EARLY_ROWS = PACK_ROWS - LATE_ROWS
EARLY_TILE = 160
CONV_SHARD = D_CONV * CONV_DIM // N_DEV
TAIL_ELEMS = CONV_SHARD + sum(SMALL_SIZES.values()) + 1


def _cparams(sem=None, **kw):
    return pltpu.CompilerParams(dimension_semantics=sem, vmem_limit_bytes=VMEM_LIMIT, **kw)


def _mx(v):
    return v.astype(MXU_DTYPE)


def _dot(a, b):
    return jnp.dot(_mx(a), _mx(b), preferred_element_type=f32)


def _dot_nt(a, b):
    return lax.dot_general(_mx(a), _mx(b), (((1,), (1,)), ((), ())), preferred_element_type=f32)


def _dot_tn(a, b):
    return lax.dot_general(_mx(a), _mx(b), (((0,), (0,)), ((), ())), preferred_element_type=f32)


def _dot_exact(a, b):
    return jnp.dot(a, b, precision=lax.Precision.HIGHEST, preferred_element_type=f32)


def _sigmoid(v):
    return 1.0 / (1.0 + jnp.exp(-v))


def _pick(n, prefs):
    for p in prefs:
        if n % p == 0:
            return p
    return n


MM_TILE = 1024


def mm_nn(a, b, name, out_dtype=f32, acc_in=None, acc_scale=1.0):
    m, k = a.shape
    n = b.shape[1]
    tm = _pick(m, (MM_TILE, 576, 512, 256, 128, 64))
    tn = _pick(n, (MM_TILE, 512, 256, 128))
    tk = _pick(k, (2048, 1536, 1152, 1024, 768, 512, 256, 128))
    nk = k // tk

    def body(*refs):
        a_ref, b_ref = refs[0:2]
        c_ref = refs[2] if acc_in is not None else None
        o_ref = refs[3] if acc_in is not None else refs[2]

        def finish(r):
            if acc_in is not None:
                r = r + acc_scale * c_ref[...]
            o_ref[...] = r.astype(o_ref.dtype)

        if nk == 1:
            finish(_dot(a_ref[...], b_ref[...]))
            return
        acc_ref = refs[-1]
        kk = pl.program_id(2)

        @pl.when(kk == 0)
        def _():
            acc_ref[...] = jnp.zeros_like(acc_ref)

        acc_ref[...] += _dot(a_ref[...], b_ref[...])

        @pl.when(kk == nk - 1)
        def _():
            finish(acc_ref[...])

    in_specs = [pl.BlockSpec((tm, tk), lambda i, j, kk: (i, kk)), pl.BlockSpec((tk, tn), lambda i, j, kk: (kk, j))]
    args = [a, b]
    if acc_in is not None:
        in_specs.append(pl.BlockSpec((tm, tn), lambda i, j, kk: (i, j)))
        args.append(acc_in)
    return pl.pallas_call(
        body, name=name, grid=(m // tm, n // tn, nk), in_specs=in_specs,
        out_specs=pl.BlockSpec((tm, tn), lambda i, j, kk: (i, j)),
        out_shape=jax.ShapeDtypeStruct((m, n), out_dtype),
        scratch_shapes=[pltpu.VMEM((tm, tn), f32)] if nk > 1 else [],
        compiler_params=_cparams(("parallel", "parallel", "arbitrary")))(*args)


def mm_nt(a, b, name, out_dtype=f32, relu2=None, relu2_of=None):
    m, k = a.shape
    n = b.shape[0]
    tm = MM_TILE
    tn = _pick(n, (MM_TILE, 768, 512, 256, 128))

    def body(*refs):
        r = _dot_nt(refs[0][...], refs[1][...])
        if relu2:
            pos = jnp.maximum(r, 0.0)
            refs[2][...] = pos.astype(refs[2].dtype)
            refs[3][...] = (pos * pos).astype(refs[3].dtype)
        elif relu2_of is not None:
            refs[3][...] = (r * (2.0 * refs[2][...].astype(f32))).astype(refs[3].dtype)
        else:
            refs[2][...] = r.astype(refs[2].dtype)

    blk = pl.BlockSpec((tm, tn), lambda i, j: (i, j))
    in_specs = [pl.BlockSpec((tm, k), lambda i, j: (i, 0)), pl.BlockSpec((tn, k), lambda i, j: (j, 0))]
    args = [a, b]
    if relu2_of is not None:
        in_specs.append(blk)
        args.append(relu2_of)
    if relu2:
        out_specs, out_shape = [blk, blk], [jax.ShapeDtypeStruct((m, n), bf16), jax.ShapeDtypeStruct((m, n), bf16)]
    else:
        out_specs, out_shape = blk, jax.ShapeDtypeStruct((m, n), out_dtype)
    return pl.pallas_call(body, name=name, grid=(m // tm, n // tn), in_specs=in_specs, out_specs=out_specs,
                          out_shape=out_shape, compiler_params=_cparams(("parallel", "parallel")))(*args)


def mm_nt_split(a, b, name, width, out_dtype=f32):
    m, k = a.shape
    n = b.shape[0]
    tm = MM_TILE
    parts = n // width

    def body(a_ref, b_ref, *o_refs):
        r = _dot_nt(a_ref[...], b_ref[...])
        for q in range(parts):
            o_refs[q][...] = r[:, width * q:width * (q + 1)].astype(o_refs[q].dtype)

    blk = pl.BlockSpec((tm, width), lambda i: (i, 0))
    return pl.pallas_call(
        body, name=name, grid=(m // tm,),
        in_specs=[pl.BlockSpec((tm, k), lambda i: (i, 0)), pl.BlockSpec((n, k), lambda i: (0, 0))],
        out_specs=[blk] * parts, out_shape=[jax.ShapeDtypeStruct((m, width), out_dtype)] * parts,
        compiler_params=_cparams(("parallel",)))(a, b)


def mm_tn(a, b, name, acc_in=None, acc_scale=1.0):
    k, m = a.shape
    n = b.shape[1]
    tm = _pick(m, (MM_TILE, 768, 512, 256, 128))
    tn = _pick(n, (MM_TILE, 512, 256, 128))
    tk = _pick(k, (1024, 768, 576, 512, 256, 128, 64))
    nk = k // tk

    def body(*refs):
        a_ref, b_ref, o_ref = refs[0], refs[1], refs[-1]
        kk = pl.program_id(2)

        @pl.when(kk == 0)
        def _():
            o_ref[...] = jnp.zeros_like(o_ref) if acc_in is None else acc_scale * refs[2][...]

        o_ref[...] += _dot_tn(a_ref[...], b_ref[...])

    in_specs = [pl.BlockSpec((tk, tm), lambda i, j, kk: (kk, i)), pl.BlockSpec((tk, tn), lambda i, j, kk: (kk, j))]
    args = [a, b]
    if acc_in is not None:
        in_specs.append(pl.BlockSpec((tm, tn), lambda i, j, kk: (i, j)))
        args.append(acc_in)
    return pl.pallas_call(
        body, name=name, grid=(m // tm, n // tn, nk), in_specs=in_specs,
        out_specs=pl.BlockSpec((tm, tn), lambda i, j, kk: (i, j)),
        out_shape=jax.ShapeDtypeStruct((m, n), f32),
        compiler_params=_cparams(("parallel", "parallel", "arbitrary")))(*args)


def _halo_specs(tb, cb, nt, off=0):
    r = tb // 8
    return [pl.BlockSpec((8, cb), lambda j, i: (jnp.maximum(i * r - 1, 0), j + off)),
            pl.BlockSpec((tb, cb), lambda j, i: (i, j + off)),
            pl.BlockSpec((8, cb), lambda j, i: (jnp.minimum((i + 1) * r, nt * r - 1), j + off))]


def _with_halo(prev_ref, own_ref, next_ref, i, nt):
    prev = jnp.where(i > 0, prev_ref[...].astype(f32), 0.0)
    nxt = jnp.where(i < nt - 1, next_ref[...].astype(f32), 0.0)
    return jnp.concatenate([prev, own_ref[...].astype(f32), nxt], axis=0)


def _shifted(xcat, s, tb):
    n = xcat.shape[0]
    return pltpu.roll(xcat, (-s) % n, 0)[8:8 + tb]


def conv_fwd(xbc, w8, b_row, tb=512, cb=512):
    t, c = xbc.shape
    nt = t // tb

    def body(prev_ref, own_ref, next_ref, w_ref, b_ref, o_ref):
        i = pl.program_id(1)
        xcat = _with_halo(prev_ref, own_ref, next_ref, i, nt)
        w = w_ref[...]
        pre = b_ref[...] + w[0:1] * _shifted(xcat, -2, tb)
        for k in range(1, D_CONV):
            pre = pre + w[k:k + 1] * _shifted(xcat, k - 2, tb)
        o_ref[...] = pre * _sigmoid(pre)

    return pl.pallas_call(
        body, name="conv_fwd", grid=(c // cb, nt),
        in_specs=_halo_specs(tb, cb, nt) + [pl.BlockSpec((8, cb), lambda j, i: (0, j)), pl.BlockSpec((1, cb), lambda j, i: (0, j))],
        out_specs=pl.BlockSpec((tb, cb), lambda j, i: (i, j)), out_shape=jax.ShapeDtypeStruct((t, c), f32),
        compiler_params=_cparams(("parallel", "parallel")))(xbc, xbc, xbc, w8, b_row)


def conv_bwd(xbc, xoff, grads, scales, w8, b_row, name, tb=512, cb=512):
    t, c = grads[0].shape
    nt = t // tb
    ng = len(grads)
    has_scale = [s is not None for s in scales]

    def body(*refs):
        i = pl.program_id(1)
        xr = refs[0:3]
        gr = [refs[3 + 3 * q: 6 + 3 * q] for q in range(ng)]
        pos = 3 + 3 * ng
        sr = []
        for q in range(ng):
            if has_scale[q]:
                sr.append(refs[pos])
                pos += 1
            else:
                sr.append(None)
        w_ref, b_ref, dx_ref, dw_ref, db_ref = refs[pos:pos + 5]
        xcat = _with_halo(*xr, i, nt)
        gcat = None
        for q in range(ng):
            gq = _with_halo(*gr[q], i, nt)
            if sr[q] is not None:
                gq = gq * sr[q][...]
            gcat = gq if gcat is None else gcat + gq
        w = w_ref[...]
        n = tb + 16
        pre = b_ref[...] + w[0:1] * pltpu.roll(xcat, 2, 0)
        for k in range(1, D_CONV):
            pre = pre + w[k:k + 1] * pltpu.roll(xcat, (2 - k) % n, 0)
        sg = _sigmoid(pre)
        dpre = gcat * sg * (1.0 + pre * (1.0 - sg))
        dx = w[0:1] * _shifted(dpre, 2, tb)
        for k in range(1, D_CONV):
            dx = dx + w[k:k + 1] * _shifted(dpre, 2 - k, tb)
        dx_ref[...] = dx.astype(dx_ref.dtype)
        dp_own = dpre[8:8 + tb]
        rows = [jnp.sum(dp_own * _shifted(xcat, k - 2, tb), axis=0, keepdims=True) for k in range(D_CONV)]
        dw = jnp.concatenate(rows + [jnp.zeros((8 - D_CONV, cb), f32)], axis=0)
        db = jnp.sum(dp_own, axis=0, keepdims=True)

        @pl.when(i == 0)
        def _():
            dw_ref[...] = jnp.zeros_like(dw_ref)
            db_ref[...] = jnp.zeros_like(db_ref)

        dw_ref[...] += dw
        db_ref[...] += db

    in_specs = _halo_specs(tb, cb, nt, xoff)
    args = [xbc] * 3
    for g in grads:
        in_specs += _halo_specs(tb, cb, nt)
        args += [g] * 3
    for s in scales:
        if s is not None:
            in_specs.append(pl.BlockSpec((1, cb), lambda j, i: (0, j)))
            args.append(s)
    in_specs += [pl.BlockSpec((8, cb), lambda j, i: (0, j)), pl.BlockSpec((1, cb), lambda j, i: (0, j))]
    args += [w8, b_row]
    return pl.pallas_call(
        body, name=name, grid=(c // cb, nt), in_specs=in_specs,
        out_specs=[pl.BlockSpec((tb, cb), lambda j, i: (i, j)), pl.BlockSpec((8, cb), lambda j, i: (0, j)),
                   pl.BlockSpec((1, cb), lambda j, i: (0, j))],
        out_shape=[jax.ShapeDtypeStruct((t, c), bf16), jax.ShapeDtypeStruct((8, c), f32), jax.ShapeDtypeStruct((1, c), f32)],
        compiler_params=_cparams(("parallel", "arbitrary")))(*args)


def dt_fwd(u_dt, bias_row, tb=1024):
    t = u_dt.shape[0]

    def body(u_ref, b_ref, o_ref):
        v = u_ref[...] + b_ref[...]
        sp = jnp.maximum(v, 0.0) + jnp.log(1.0 + jnp.exp(-jnp.abs(v)))
        lane = lax.broadcasted_iota(jnp.int32, v.shape, 1)
        o_ref[...] = jnp.where((lane & 127) < SSD_HEADS, sp, 0.0)

    return pl.pallas_call(
        body, name="dt_fwd", grid=(t // tb,),
        in_specs=[pl.BlockSpec((tb, 256), lambda i: (i, 0)), pl.BlockSpec((1, 256), lambda i: (0, 0))],
        out_specs=pl.BlockSpec((tb, 256), lambda i: (i, 0)), out_shape=jax.ShapeDtypeStruct((t, 256), f32),
        compiler_params=_cparams(("parallel",)))(u_dt, bias_row)


def dt_bwd(ddt_f, ddt_b, u_dt, bias_row, tb=1024):
    t = u_dt.shape[0]

    def body(gf_ref, gb_ref, u_ref, b_ref, du_ref, db_ref):
        g = jnp.concatenate([jnp.sum(gf_ref[...], axis=0), jnp.sum(gb_ref[...], axis=0)], axis=1)
        du = g * _sigmoid(u_ref[...] + b_ref[...])
        du_ref[...] = du.astype(du_ref.dtype)

        @pl.when(pl.program_id(0) == 0)
        def _():
            db_ref[...] = jnp.zeros_like(db_ref)

        db_ref[...] += jnp.sum(du, axis=0, keepdims=True)

    return pl.pallas_call(
        body, name="dt_bwd", grid=(t // tb,),
        in_specs=[pl.BlockSpec((4, tb, 128), lambda i: (0, i, 0)), pl.BlockSpec((4, tb, 128), lambda i: (0, i, 0)),
                  pl.BlockSpec((tb, 256), lambda i: (i, 0)), pl.BlockSpec((1, 256), lambda i: (0, 0))],
        out_specs=[pl.BlockSpec((tb, 256), lambda i: (i, 0)), pl.BlockSpec((1, 256), lambda i: (0, 0))],
        out_shape=[jax.ShapeDtypeStruct((t, 256), bf16), jax.ShapeDtypeStruct((1, 256), f32)],
        compiler_params=_cparams(("arbitrary",)))(ddt_f, ddt_b, u_dt, bias_row)


def _ssd_common(dt_blk, a_row, reverse):
    row = lax.broadcasted_iota(jnp.int32, (CHUNK, CHUNK), 0)
    col = lax.broadcasted_iota(jnp.int32, (CHUNK, CHUNK), 1)
    mask = (row <= col) if reverse else (row >= col)
    tri = mask.astype(f32)
    a = dt_blk * a_row
    acs = _dot_exact(tri, a)
    atot = jnp.sum(a, axis=0, keepdims=True)
    return mask, tri, a, acs, atot, col


def _lane_col(mat, lane_idx, h):
    return jnp.sum(jnp.where(lane_idx == h, mat, 0.0), axis=1, keepdims=True)


def ssd_fwd(xbc_c, dt2, a_rows, reverse, name):
    t = xbc_c.shape[0]
    nc = t // CHUNK
    d_off = 1 if reverse else 0

    def cidx(c):
        return nc - 1 - c if reverse else c

    def body(x_ref, b_ref, c_ref, dt_ref, a_ref, y_ref, hp_ref, h_scr, acst_scr):
        g = pl.program_id(0)
        c = pl.program_id(1)

        @pl.when(c == 0)
        def _():
            h_scr[...] = jnp.zeros_like(h_scr)

        dt_blk = dt_ref[...]
        mask, tri, a, acs, atot, lane = _ssd_common(dt_blk, a_ref[...], reverse)
        acst_scr[...] = acs.T
        bm = b_ref[...]
        cm = c_ref[...]
        cb = _dot_nt(cm, bm)
        half = lane >= SSD_HEAD_DIM
        sub_half = lax.broadcasted_iota(jnp.int32, (CHUNK, 1), 0) >= SSD_HEAD_DIM
        for j in range(4):
            x = x_ref[:, 128 * j:128 * (j + 1)]
            cols, dts, tots = [], [], []
            y = None
            for e in range(2):
                h = 8 * g + 2 * j + e
                col_h = _lane_col(acs, lane, h)
                row_h = acst_scr[pl.ds(h, 1), :]
                dt_h = _lane_col(dt_blk, lane, h)
                lmat = jnp.where(mask, jnp.exp(jnp.where(mask, col_h - row_h, 0.0)), 0.0)
                xdt_e = jnp.where(half == (e == 1), x * dt_h, 0.0)
                ye = _dot(cb * lmat, xdt_e)
                y = ye if y is None else y + ye
                cols.append(col_h)
                dts.append(dt_h)
                tots.append(jnp.sum(jnp.where(lane[0:1] == h, atot, 0.0), axis=1, keepdims=True))
            hp = h_scr[j]
            hp_ref[0, j] = hp
            ecol = jnp.where(half, jnp.exp(cols[1]), jnp.exp(cols[0]))
            y = y + _dot_nt(cm, hp) * ecol
            y_ref[:, 128 * j:128 * (j + 1)] = y
            dec = jnp.where(half, jnp.exp(tots[1] - cols[1]), jnp.exp(tots[0] - cols[0]))
            xdt = x * jnp.where(half, dts[1], dts[0])
            s_new = _dot_tn(xdt * dec, bm)
            cd = jnp.where(sub_half, jnp.exp(tots[1]), jnp.exp(tots[0]))
            h_scr[j] = cd * hp + s_new

    return pl.pallas_call(
        body, name=name, grid=(SSD_GROUPS, nc),
        in_specs=[pl.BlockSpec((CHUNK, 512), lambda g, c: (cidx(c), g)),
                  pl.BlockSpec((CHUNK, 128), lambda g, c: (cidx(c), 16 + g)),
                  pl.BlockSpec((CHUNK, 128), lambda g, c: (cidx(c), 20 + g)),
                  pl.BlockSpec((CHUNK, 128), lambda g, c: (cidx(c), d_off)),
                  pl.BlockSpec((1, 128), lambda g, c: (0, d_off))],
        out_specs=[pl.BlockSpec((CHUNK, 512), lambda g, c: (cidx(c), g)),
                   pl.BlockSpec((1, 4, 128, 128), lambda g, c: (cidx(c), g, 0, 0))],
        out_shape=[jax.ShapeDtypeStruct((t, D_INNER), f32), jax.ShapeDtypeStruct((nc, 16, 128, 128), f32)],
        scratch_shapes=[pltpu.VMEM((4, 128, 128), f32), pltpu.VMEM((CHUNK, CHUNK), f32)],
        compiler_params=_cparams(("parallel", "arbitrary")))(xbc_c, xbc_c, xbc_c, dt2, a_rows)


def ssd_bwd(xbc_c, dt2, a_rows, dy, hprev, reverse, name):
    t = xbc_c.shape[0]
    nc = t // CHUNK
    d_off = 1 if reverse else 0

    def cidx(c):
        return c if reverse else nc - 1 - c

    def body(x_ref, b_ref, c_ref, dt_ref, a_ref, dy_ref, hp_ref, dx_ref, db_ref, dc_ref, ddt_ref, da_ref,
             dh_scr, acst_scr):
        g = pl.program_id(0)
        c = pl.program_id(1)

        @pl.when(c == 0)
        def _():
            dh_scr[...] = jnp.zeros_like(dh_scr)
            da_ref[...] = jnp.zeros_like(da_ref)

        dt_blk = dt_ref[...]
        a_row = a_ref[...]
        mask, tri, a, acs, atot, lane = _ssd_common(dt_blk, a_row, reverse)
        acst_scr[...] = acs.T
        sub = lax.broadcasted_iota(jnp.int32, (CHUNK, CHUNK), 0)
        bm = b_ref[...]
        cm = c_ref[...]
        cb = _dot_nt(cm, bm)
        half = lane >= SSD_HEAD_DIM
        sub_half = sub[:, 0:1] >= SSD_HEAD_DIM
        dcb = jnp.zeros((CHUNK, CHUNK), f32)
        dacs = jnp.zeros((CHUNK, CHUNK), f32)
        dacs_t = jnp.zeros((CHUNK, CHUNK), f32)
        dtot = jnp.zeros((1, CHUNK), f32)
        ddt_x = jnp.zeros((CHUNK, CHUNK), f32)
        dbm = jnp.zeros((CHUNK, D_STATE), f32)
        dcm = jnp.zeros((CHUNK, D_STATE), f32)
        for j in range(4):
            x = x_ref[:, 128 * j:128 * (j + 1)]
            dyp = dy_ref[:, 128 * j:128 * (j + 1)]
            hp = hp_ref[0, j]
            dhn = dh_scr[j]
            cols, dts, tots, hs = [], [], [], []
            dxdt = None
            for e in range(2):
                h = 8 * g + 2 * j + e
                sel = half == (e == 1)
                col_h = _lane_col(acs, lane, h)
                row_h = acst_scr[pl.ds(h, 1), :]
                dt_h = _lane_col(dt_blk, lane, h)
                lmat = jnp.where(mask, jnp.exp(jnp.where(mask, col_h - row_h, 0.0)), 0.0)
                xdt_e = jnp.where(sel, x * dt_h, 0.0)
                dy_e = jnp.where(sel, dyp, 0.0)
                ml = _dot_nt(dy_e, xdt_e) * lmat
                dcb = dcb + ml
                w = ml * cb
                dacs = dacs + jnp.where(lane == h, jnp.sum(w, axis=1, keepdims=True), 0.0)
                dacs_t = dacs_t - jnp.where(sub == h, jnp.sum(w, axis=0, keepdims=True), 0.0)
                de = _dot_tn(cb * lmat, dy_e)
                dxdt = de if dxdt is None else dxdt + de
                cols.append(col_h)
                dts.append(dt_h)
                tots.append(jnp.sum(jnp.where(lane[0:1] == h, atot, 0.0), axis=1, keepdims=True))
                hs.append(h)
            ecol = jnp.where(half, jnp.exp(cols[1]), jnp.exp(cols[0]))
            dec = jnp.where(half, jnp.exp(tots[1] - cols[1]), jnp.exp(tots[0] - cols[0]))
            cd = jnp.where(sub_half, jnp.exp(tots[1]), jnp.exp(tots[0]))
            dtp = jnp.where(half, dts[1], dts[0])
            xdt = x * dtp
            yoff = _dot_nt(cm, hp) * ecol
            dye = dyp * ecol
            dcm = dcm + _dot(dye, hp)
            dhp = _dot_tn(dye, cm)
            gmat = _dot_nt(bm, dhn)
            dxdt = dxdt + dec * gmat
            dbm = dbm + _dot(xdt * dec, dhn)
            r_off = dyp * yoff
            r_dec = xdt * gmat * dec
            r_x = dxdt * x
            hh = dhn * hp
            for e in range(2):
                sel = half == (e == 1)
                h = hs[e]
                s_off = jnp.sum(jnp.where(sel, r_off, 0.0), axis=1, keepdims=True)
                s_dec = jnp.sum(jnp.where(sel, r_dec, 0.0), axis=1, keepdims=True)
                dacs = dacs + jnp.where(lane == h, s_off - s_dec, 0.0)
                dcd = jnp.sum(jnp.sum(jnp.where(sub_half == (e == 1), hh, 0.0), axis=1, keepdims=True), axis=0, keepdims=True)
                tot_e = jnp.sum(s_dec, axis=0, keepdims=True) + jnp.exp(tots[e]) * dcd
                dtot = dtot + jnp.where(lane[0:1] == h, tot_e, 0.0)
                ddt_x = ddt_x + jnp.where(lane == h, jnp.sum(jnp.where(sel, r_x, 0.0), axis=1, keepdims=True), 0.0)
            dx_ref[:, 128 * j:128 * (j + 1)] = dxdt * dtp
            dh_scr[j] = cd * dhn + dhp
        dcm = dcm + _dot(dcb, bm)
        dbm = dbm + _dot_tn(dcb, cm)
        db_ref[...] = dbm
        dc_ref[...] = dcm
        dacs = dacs + dacs_t.T
        da = _dot_exact(tri.T, dacs) + dtot
        ddt_ref[0] = da * a_row + ddt_x
        da_ref[0] += jnp.sum(da * dt_blk, axis=0, keepdims=True)

    return pl.pallas_call(
        body, name=name, grid=(SSD_GROUPS, nc),
        in_specs=[pl.BlockSpec((CHUNK, 512), lambda g, c: (cidx(c), g)),
                  pl.BlockSpec((CHUNK, 128), lambda g, c: (cidx(c), 16 + g)),
                  pl.BlockSpec((CHUNK, 128), lambda g, c: (cidx(c), 20 + g)),
                  pl.BlockSpec((CHUNK, 128), lambda g, c: (cidx(c), d_off)),
                  pl.BlockSpec((1, 128), lambda g, c: (0, d_off)),
                  pl.BlockSpec((CHUNK, 512), lambda g, c: (cidx(c), g)),
                  pl.BlockSpec((1, 4, 128, 128), lambda g, c: (cidx(c), g, 0, 0))],
        out_specs=[pl.BlockSpec((CHUNK, 512), lambda g, c: (cidx(c), g)),
                   pl.BlockSpec((CHUNK, 128), lambda g, c: (cidx(c), g)),
                   pl.BlockSpec((CHUNK, 128), lambda g, c: (cidx(c), g)),
                   pl.BlockSpec((1, CHUNK, 128), lambda g, c: (g, cidx(c), 0)),
                   pl.BlockSpec((1, 1, 128), lambda g, c: (g, 0, 0))],
        out_shape=[jax.ShapeDtypeStruct((t, D_INNER), f32), jax.ShapeDtypeStruct((t, 512), f32),
                   jax.ShapeDtypeStruct((t, 512), f32), jax.ShapeDtypeStruct((4, t, 128), f32),
                   jax.ShapeDtypeStruct((4, 1, 128), f32)],
        scratch_shapes=[pltpu.VMEM((4, 128, 128), f32), pltpu.VMEM((CHUNK, CHUNK), f32)],
        compiler_params=_cparams(("parallel", "arbitrary")))(xbc_c, xbc_c, xbc_c, dt2, a_rows, dy, hprev)


def tail_fwd(y_f, y_b, xbc_c, z, dskip_row, nw_row, tb=512):
    t = y_f.shape[0]

    def body(yf_ref, yb_ref, x_ref, z_ref, d_ref, w_ref, o_ref):
        zz = z_ref[...]
        y = (yf_ref[...] + yb_ref[...] + d_ref[...] * x_ref[...]) * (zz * _sigmoid(zz))
        rstd = lax.rsqrt(jnp.mean(y * y, axis=1, keepdims=True) + NORM_EPS)
        o_ref[...] = (y * rstd * w_ref[...]).astype(o_ref.dtype)

    blk = pl.BlockSpec((tb, 512), lambda i, g: (i, g))
    row = pl.BlockSpec((1, 512), lambda i, g: (0, g))
    return pl.pallas_call(
        body, name="tail_fwd", grid=(t // tb, SSD_GROUPS), in_specs=[blk, blk, blk, blk, row, row], out_specs=blk,
        out_shape=jax.ShapeDtypeStruct((t, D_INNER), bf16),
        compiler_params=_cparams(("parallel", "parallel")))(y_f, y_b, xbc_c, z, dskip_row, nw_row)


def tail_bwd(dyn, y_f, y_b, xbc_c, z, dskip_row, nw_row, tb=512):
    t = y_f.shape[0]

    def body(g_ref, yf_ref, yb_ref, x_ref, z_ref, d_ref, w_ref, dy_ref, dz_ref, dw_ref, dd_ref):
        zz = z_ref[...]
        sg = _sigmoid(zz)
        sl = zz * sg
        x = x_ref[...]
        y = yf_ref[...] + yb_ref[...] + d_ref[...] * x
        yz = y * sl
        rstd = lax.rsqrt(jnp.mean(yz * yz, axis=1, keepdims=True) + NORM_EPS)
        yhat = yz * rstd
        g = g_ref[...]
        dyhat = g * w_ref[...]
        dyz = rstd * (dyhat - yhat * jnp.mean(dyhat * yhat, axis=1, keepdims=True))
        dy = dyz * sl
        dy_ref[...] = dy
        dz_ref[...] = (dyz * y * sg * (1.0 + zz * (1.0 - sg))).astype(dz_ref.dtype)

        @pl.when(pl.program_id(1) == 0)
        def _():
            dw_ref[...] = jnp.zeros_like(dw_ref)
            dd_ref[...] = jnp.zeros_like(dd_ref)

        dw_ref[...] += jnp.sum(g * yhat, axis=0, keepdims=True)
        dd_ref[...] += jnp.sum(dy * x, axis=0, keepdims=True)

    blk = pl.BlockSpec((tb, 512), lambda g, i: (i, g))
    row = pl.BlockSpec((1, 512), lambda g, i: (0, g))
    return pl.pallas_call(
        body, name="tail_bwd", grid=(SSD_GROUPS, t // tb), in_specs=[blk, blk, blk, blk, blk, row, row],
        out_specs=[blk, blk, row, row],
        out_shape=[jax.ShapeDtypeStruct((t, D_INNER), f32), jax.ShapeDtypeStruct((t, D_INNER), bf16),
                   jax.ShapeDtypeStruct((1, D_INNER), f32), jax.ShapeDtypeStruct((1, D_INNER), f32)],
        compiler_params=_cparams(("parallel", "arbitrary")))(dyn, y_f, y_b, xbc_c, z, dskip_row, nw_row)


def _slopes(p):
    return [2.0 ** (-8.0 * (HEADS_PER_PATTERN * p + j + 1) / ATTN_HEADS) for j in range(HEADS_PER_PATTERN)]


def _win_specs(nq, col_of):
    return [pl.BlockSpec((64, 256), lambda r, i: (jnp.maximum(2 * i - 1, 0), col_of(r))),
            pl.BlockSpec((128, 256), lambda r, i: (i, col_of(r))),
            pl.BlockSpec((64, 256), lambda r, i: (jnp.minimum(2 * i + 2, 2 * nq - 1), col_of(r)))]


def _lane_head(shape):
    return lax.broadcasted_iota(jnp.int32, shape, 1) >> 6


def _stack_heads(m):
    lane_head = _lane_head(m.shape)
    return jnp.concatenate([jnp.where(lane_head == j, m, 0.0) for j in range(HEADS_PER_PATTERN)], axis=0)


def _unstack_heads(m4, n):
    lane_head = _lane_head((n, 256))
    out = jnp.where(lane_head == 0, m4[0:n], 0.0)
    for j in range(1, HEADS_PER_PATTERN):
        out = out + jnp.where(lane_head == j, m4[j * n:(j + 1) * n], 0.0)
    return out


def _head_cols(m, n):
    lane = lax.broadcasted_iota(jnp.int32, (n, 256), 1)
    return jnp.concatenate([jnp.sum(jnp.where(lane == ATTN_HEAD_DIM * j, m, 0.0), axis=1, keepdims=True)
                            for j in range(HEADS_PER_PATTERN)], axis=0)


def _score_bias(p, dil, by_key):
    slopes = np.asarray(_slopes(p), np.float32)
    if by_key:
        win = np.arange(256)[:, None]
        rel = np.arange(128)[None, :] - (win - 64)
    else:
        win = np.arange(256)[None, :]
        rel = win - 64 - np.arange(128)[:, None]
    band = np.abs(rel) <= 64
    out = []
    for first, last in ((False, False), (True, False), (False, True), (True, True)):
        ok = band & ~(first & (win < 64)) & ~(last & (win >= 192))
        pen = -slopes[:, None, None] * (np.abs(rel) * dil).astype(np.float32)[None]
        out.append(np.where(ok[None], pen, np.float32(NEG_BIG)).reshape(-1, rel.shape[1]))
    return jnp.asarray(np.stack(out), f32)


def _bias_spec(nq, rows, cols):
    return pl.BlockSpec((1, rows, cols), lambda r, i: ((i == 0).astype(jnp.int32) + 2 * (i == nq - 1).astype(jnp.int32), 0, 0))


def attn_fwd(q, k, v, p, dil, name):
    l = q.shape[0]
    nq = l // 128

    def body(q_ref, kp_ref, ko_ref, kn_ref, vp_ref, vo_ref, vn_ref, bias_ref, o_ref, lse_ref):
        kcat = jnp.concatenate([kp_ref[...], ko_ref[...], kn_ref[...]], axis=0)
        vcat = jnp.concatenate([vp_ref[...], vo_ref[...], vn_ref[...]], axis=0)
        s = _dot_nt(_stack_heads(q_ref[...] * 0.125), kcat) + bias_ref[0]
        m = jnp.max(s, axis=1, keepdims=True)
        pr = jnp.exp(s - m)
        den = jnp.sum(pr, axis=1, keepdims=True)
        o4 = _dot(pr, vcat) / den
        o_ref[...] = _unstack_heads(o4, 128)
        lse_ref[...] = _unstack_heads(jnp.broadcast_to(m + jnp.log(den), (512, 256)), 128)

    col = lambda r: r
    return pl.pallas_call(
        body, name=name, grid=(dil, nq),
        in_specs=[pl.BlockSpec((128, 256), lambda r, i: (i, r))] + _win_specs(nq, col) + _win_specs(nq, col)
        + [_bias_spec(nq, 512, 256)],
        out_specs=[pl.BlockSpec((128, 256), lambda r, i: (i, r))] * 2,
        out_shape=[jax.ShapeDtypeStruct(q.shape, f32)] * 2,
        compiler_params=_cparams(("parallel", "parallel")))(q, k, k, k, v, v, v, _score_bias(p, dil, False))


def attn_combine(os_, lses, tb=1024):
    t = os_[0].shape[0]

    def body(o0, o1, o2, l0, l1, l2, y_ref, lse_ref):
        a0, a1, a2 = l0[...], l1[...], l2[...]
        m = jnp.maximum(jnp.maximum(a0, a1), a2)
        e0, e1, e2 = jnp.exp(a0 - m), jnp.exp(a1 - m), jnp.exp(a2 - m)
        den = e0 + e1 + e2
        y_ref[...] = (e0 * o0[...] + e1 * o1[...] + e2 * o2[...]) / den
        lse_ref[...] = m + jnp.log(den)

    blk = pl.BlockSpec((tb, 256), lambda i: (i, 0))
    return pl.pallas_call(
        body, name="attn_combine", grid=(t // tb,), in_specs=[blk] * 6, out_specs=[blk, blk],
        out_shape=[jax.ShapeDtypeStruct((t, 256), f32)] * 2,
        compiler_params=_cparams(("parallel",)))(*os_, *lses)


def attn_delta(dy, y, tb=1024):
    t = dy.shape[0]

    def body(dy_ref, y_ref, d_ref):
        pr = dy_ref[...] * y_ref[...]
        lane_head = _lane_head(pr.shape)
        out = jnp.zeros_like(pr)
        for j in range(HEADS_PER_PATTERN):
            sj = jnp.sum(jnp.where(lane_head == j, pr, 0.0), axis=1, keepdims=True)
            out = out + jnp.where(lane_head == j, sj, 0.0)
        d_ref[...] = out

    blk = pl.BlockSpec((tb, 256), lambda i: (i, 0))
    return pl.pallas_call(body, name="attn_delta", grid=(t // tb,), in_specs=[blk, blk], out_specs=blk,
                          out_shape=jax.ShapeDtypeStruct((t, 256), f32),
                          compiler_params=_cparams(("parallel",)))(dy, y)


def attn_dq(q, k, v, dy, lse, delta, p, dil, name):
    l = q.shape[0]
    nq = l // 128

    def body(q_ref, kp_ref, ko_ref, kn_ref, vp_ref, vo_ref, vn_ref, dy_ref, lse_ref, d_ref, bias_ref, dq_ref):
        kcat = jnp.concatenate([kp_ref[...], ko_ref[...], kn_ref[...]], axis=0)
        vcat = jnp.concatenate([vp_ref[...], vo_ref[...], vn_ref[...]], axis=0)
        s = _dot_nt(_stack_heads(q_ref[...] * 0.125), kcat) + bias_ref[0]
        pr = jnp.exp(s - _head_cols(lse_ref[...], 128))
        dp = _dot_nt(_stack_heads(dy_ref[...]), vcat)
        ds = pr * (dp - _head_cols(d_ref[...], 128))
        dq_ref[...] = (_unstack_heads(_dot(ds, kcat), 128) * 0.125).astype(dq_ref.dtype)

    col = lambda r: r
    own = pl.BlockSpec((128, 256), lambda r, i: (i, r))
    return pl.pallas_call(
        body, name=name, grid=(dil, nq),
        in_specs=[own] + _win_specs(nq, col) + _win_specs(nq, col) + [own, own, own, _bias_spec(nq, 512, 256)],
        out_specs=own, out_shape=jax.ShapeDtypeStruct(q.shape, bf16),
        compiler_params=_cparams(("parallel", "parallel")))(q, k, k, k, v, v, v, dy, lse, delta, _score_bias(p, dil, False))


def attn_dkv(q, k, v, dy, lse, delta, p, dil, name):
    l = q.shape[0]
    nq = l // 128

    def body(qp_ref, qo_ref, qn_ref, gp_ref, go_ref, gn_ref, lp_ref, lo_ref, ln_ref, dp_ref, do_ref, dn_ref,
             k_ref, v_ref, bias_ref, dk_ref, dv_ref):
        cat = lambda a, b, c: jnp.concatenate([a[...], b[...], c[...]], axis=0)
        q4 = _stack_heads(cat(qp_ref, qo_ref, qn_ref) * 0.125)
        dy4 = _stack_heads(cat(gp_ref, go_ref, gn_ref))
        lse4 = _head_cols(cat(lp_ref, lo_ref, ln_ref), 256)
        del4 = _head_cols(cat(dp_ref, do_ref, dn_ref), 256)
        s = _dot_nt(q4, k_ref[...]) + bias_ref[0]
        pr = jnp.exp(s - lse4)
        dpm = _dot_nt(dy4, v_ref[...])
        ds = pr * (dpm - del4)
        dv_ref[...] = _dot_tn(pr, dy4).astype(dv_ref.dtype)
        dk_ref[...] = _dot_tn(ds, q4).astype(dk_ref.dtype)

    col = lambda r: r
    own = pl.BlockSpec((128, 256), lambda r, i: (i, r))
    win = _win_specs(nq, col)
    return pl.pallas_call(
        body, name=name, grid=(dil, nq), in_specs=win * 4 + [own, own, _bias_spec(nq, 1024, 128)], out_specs=[own, own],
        out_shape=[jax.ShapeDtypeStruct(q.shape, bf16)] * 2,
        compiler_params=_cparams(("parallel", "parallel")))(q, q, q, dy, dy, dy, lse, lse, lse, delta, delta, delta, k, v,
                                                            _score_bias(p, dil, True))


def _lanes(v, reps):
    return v if reps == 1 else jnp.tile(v, (1, reps))


def _lane_halo_specs(cb, tb, nt, off=0):
    r = tb // 128
    return [pl.BlockSpec((cb, 128), lambda j, i: (j + off, jnp.maximum(i * r - 1, 0))),
            pl.BlockSpec((cb, tb), lambda j, i: (j + off, i)),
            pl.BlockSpec((cb, 128), lambda j, i: (j + off, jnp.minimum((i + 1) * r, nt * r - 1)))]


def _with_lane_halo(prev_ref, own_ref, next_ref, i, nt):
    prev = jnp.where(i > 0, prev_ref[...].astype(f32), 0.0)
    nxt = jnp.where(i < nt - 1, next_ref[...].astype(f32), 0.0)
    return jnp.concatenate([prev, own_ref[...].astype(f32), nxt], axis=1)


def _lane_shifted(xcat, s, tb):
    n = xcat.shape[1]
    return pltpu.roll(xcat, (-s) % n, 1)[:, 128:128 + tb]


def conv_fwd_t(xbc_t, w_b, b_b, tb=1024, cb=256):
    c, t = xbc_t.shape
    nt = t // tb

    def body(prev_ref, own_ref, next_ref, w_ref, b_ref, o_ref):
        i = pl.program_id(1)
        xcat = _with_lane_halo(prev_ref, own_ref, next_ref, i, nt)
        reps = tb // 128
        pre = _lanes(b_ref[...], reps)
        for k in range(D_CONV):
            pre = pre + _lanes(w_ref[k], reps) * _lane_shifted(xcat, k - 2, tb)
        o_ref[...] = pre * _sigmoid(pre)

    return pl.pallas_call(
        body, name="conv_fwd", grid=(c // cb, nt),
        in_specs=_lane_halo_specs(cb, tb, nt) + [pl.BlockSpec((D_CONV, cb, 128), lambda j, i: (0, j, 0)),
                                                 pl.BlockSpec((cb, 128), lambda j, i: (j, 0))],
        out_specs=pl.BlockSpec((cb, tb), lambda j, i: (j, i)), out_shape=jax.ShapeDtypeStruct((c, t), f32),
        compiler_params=_cparams(("parallel", "parallel")))(xbc_t, xbc_t, xbc_t, w_b, b_b)


def conv_bwd_t(xbc_t, grad_t, w_b, b_b, into, name, row0, tb=1024, cb=256):
    c, t = grad_t.shape
    nt = t // tb
    off = row0 // cb
    off_out = (D_INNER + row0) // cb
    reps = tb // 128

    def body(*refs):
        i = pl.program_id(1)
        xr, gr = refs[0:3], refs[3:6]
        w_ref, b_ref = refs[6:8]
        dx_ref, dw_ref, db_ref = refs[-3:]
        xcat = _with_lane_halo(*xr, i, nt)
        gcat = _with_lane_halo(*gr, i, nt)
        n = tb + 256
        wk = [_lanes(w_ref[k], reps + 2) for k in range(D_CONV)]
        pre = _lanes(b_ref[...], reps + 2)
        for k in range(D_CONV):
            pre = pre + wk[k] * pltpu.roll(xcat, (2 - k) % n, 1)
        sg = _sigmoid(pre)
        dpre = gcat * sg * (1.0 + pre * (1.0 - sg))

        def fold(v):
            s = v[:, 0:128]
            for q in range(1, reps):
                s = s + v[:, 128 * q:128 * (q + 1)]
            return s

        @pl.when(i == 0)
        def _():
            dw_ref[...] = jnp.zeros_like(dw_ref)
            db_ref[...] = jnp.zeros_like(db_ref)

        x_own = xcat[:, 128:128 + tb]
        dx = None
        for k in range(D_CONV):
            shifted = _lane_shifted(dpre, 2 - k, tb)
            term = wk[k][:, 128:128 + tb] * shifted
            dx = term if dx is None else dx + term
            dw_ref[k] += fold(shifted * x_own)
        dx_ref[...] = dx.astype(dx_ref.dtype)
        db_ref[...] += fold(dpre[:, 128:128 + tb])

    in_specs = (_lane_halo_specs(cb, tb, nt, off) + _lane_halo_specs(cb, tb, nt)
                + [pl.BlockSpec((D_CONV, cb, 128), lambda j, i: (0, j + off, 0)), pl.BlockSpec((cb, 128), lambda j, i: (j + off, 0))])
    in_specs.append(pl.BlockSpec(memory_space=pl.ANY))
    args = [xbc_t] * 3 + [grad_t] * 3 + [w_b, b_b, into]
    return pl.pallas_call(
        body, name=name, grid=(c // cb, nt), in_specs=in_specs,
        out_specs=[pl.BlockSpec((cb, tb), lambda j, i: (j + off_out, i)),
                   pl.BlockSpec((D_CONV, cb, 128), lambda j, i: (0, j, 0)), pl.BlockSpec((cb, 128), lambda j, i: (j, 0))],
        out_shape=[jax.ShapeDtypeStruct(into.shape, into.dtype), jax.ShapeDtypeStruct((D_CONV, c, 128), f32),
                   jax.ShapeDtypeStruct((c, 128), f32)],
        input_output_aliases={8: 0}, compiler_params=_cparams(("parallel", "arbitrary")))(*args)


def dt_fwd_t(u_dt_t, bias_b, tb=2048):
    r, t = u_dt_t.shape

    def body(u_ref, b_ref, o_ref):
        v = u_ref[...] + _lanes(b_ref[...], tb // 128)
        o_ref[...] = jnp.maximum(v, 0.0) + jnp.log(1.0 + jnp.exp(-jnp.abs(v)))

    return pl.pallas_call(
        body, name="dt_fwd", grid=(t // tb,),
        in_specs=[pl.BlockSpec((r, tb), lambda i: (0, i)), pl.BlockSpec((r, 128), lambda i: (0, 0))],
        out_specs=pl.BlockSpec((r, tb), lambda i: (0, i)), out_shape=jax.ShapeDtypeStruct((r, t), f32),
        compiler_params=_cparams(("parallel",)))(u_dt_t, bias_b)


def dt_bwd_t(ddt_f, ddt_b, u_dt_t, bias_b, into, tb=2048):
    r, t = u_dt_t.shape
    reps = tb // 128
    row_blk = (SSD_COLS - r) // r

    def body(gf_ref, gb_ref, u_ref, b_ref, into_ref, du_ref, db_ref):
        g = jnp.concatenate([gf_ref[...], gb_ref[...]], axis=0)
        du = g * _sigmoid(u_ref[...] + _lanes(b_ref[...], reps))
        du_ref[...] = du.astype(du_ref.dtype)

        @pl.when(pl.program_id(0) == 0)
        def _():
            db_ref[...] = jnp.zeros_like(db_ref)

        s = du[:, 0:128]
        for q in range(1, reps):
            s = s + du[:, 128 * q:128 * (q + 1)]
        db_ref[...] += s

    half = pl.BlockSpec((r // 2, tb), lambda i: (0, i))
    return pl.pallas_call(
        body, name="dt_bwd", grid=(t // tb,),
        in_specs=[half, half, pl.BlockSpec((r, tb), lambda i: (0, i)), pl.BlockSpec((r, 128), lambda i: (0, 0)),
                  pl.BlockSpec(memory_space=pl.ANY)],
        out_specs=[pl.BlockSpec((r, tb), lambda i: (row_blk, i)), pl.BlockSpec((r, 128), lambda i: (0, 0))],
        out_shape=[jax.ShapeDtypeStruct(into.shape, into.dtype), jax.ShapeDtypeStruct((r, 128), f32)],
        input_output_aliases={4: 0}, compiler_params=_cparams(("arbitrary",)))(ddt_f, ddt_b, u_dt_t, bias_b, into)


HEADS_PER_GROUP = SSD_HEADS // SSD_GROUPS


def _group_rows(g, n):
    return pl.ds(pl.multiple_of(g * n, n), n)


def _ssd_decays(dt_blk, a_blk, reverse):
    row = lax.broadcasted_iota(jnp.int32, (CHUNK, CHUNK), 0)
    col = lax.broadcasted_iota(jnp.int32, (CHUNK, CHUNK), 1)
    mask = (row <= col) if reverse else (row >= col)
    tri = mask.astype(f32)
    a8 = dt_blk * a_blk
    a = jnp.concatenate([a8, jnp.zeros((CHUNK - HEADS_PER_GROUP, CHUNK), f32)], axis=0).T
    acs = _dot_exact(tri, a)
    return mask, tri, a8, acs, acs.T, col


def ssd_fwd_t(xbc_ct, dt_t, a_b, reverse, name, prev=None, tail=None):
    t = xbc_ct.shape[1]
    nc = t // CHUNK
    direction = 1 if reverse else 0

    def cidx(c):
        return nc - 1 - c if reverse else c

    def body(*refs):
        x_ref, b_ref, c_ref, dt_ref, a_ref = refs[0:5]
        pos = 5
        prev_ref = None
        if prev is not None:
            prev_ref = refs[pos]
            pos += 1
        if tail is not None:
            z_ref, skip_ref, nw_ref = refs[pos:pos + 3]
            pos += 3
            y_ref, hp_ref, yn_ref, h_scr = refs[pos:pos + 4]
        else:
            y_ref, hp_ref, h_scr = refs[pos:pos + 3]

        @pl.when(pl.program_id(0) == 0)
        def _():
            h_scr[...] = jnp.zeros_like(h_scr)

        def group(g, carry):
            x_v, y_v = x_ref.at[_group_rows(g, 512)], y_ref.at[_group_rows(g, 512)]
            heads = _group_rows(g, HEADS_PER_GROUP)
            hp_v, h_v = hp_ref.at[0, heads], h_scr.at[heads]
            dt_blk = dt_ref[heads, :]
            mask, tri, a8, acs, acs_t, lane = _ssd_decays(dt_blk, a_ref[heads, :], reverse)
            bm = b_ref[_group_rows(g, 128), :].T
            cm = c_ref[_group_rows(g, 128), :].T
            cb = _dot_nt(cm, bm)
            tot = jnp.sum(a8, axis=1, keepdims=True)
            for j in range(HEADS_PER_GROUP):
                rows = slice(SSD_HEAD_DIM * j, SSD_HEAD_DIM * (j + 1))
                col_j = _lane_col(acs, lane, j)
                row_j = acs_t[j:j + 1, :]
                lmat = jnp.where(mask, jnp.exp(jnp.where(mask, col_j - row_j, 0.0)), 0.0)
                xdt = x_v[rows, :] * dt_blk[j:j + 1, :]
                hp = h_v[j]
                hp_v[j] = hp
                y = _dot_nt(xdt, cb * lmat) + _dot_nt(hp, cm) * jnp.exp(row_j)
                if prev_ref is not None:
                    y = y + prev_ref.at[_group_rows(g, 512)][rows, :]
                y_v[rows, :] = y
                tot_j = tot[j:j + 1, :]
                h_v[j] = jnp.exp(tot_j) * hp + _dot(xdt * jnp.exp(tot_j - row_j), bm)
            if tail is not None:
                rows = _group_rows(g, 512)
                zz = z_ref[rows, :]
                yg = (y_v[...] + skip_ref[rows, :] * x_v[...]) * (zz * _sigmoid(zz))
                rstd = lax.rsqrt(jnp.mean(yg * yg, axis=0, keepdims=True) + NORM_EPS)
                yn_ref[rows, :] = (yg * rstd * nw_ref[rows, :]).astype(yn_ref.dtype)
            return carry

        lax.fori_loop(0, SSD_GROUPS, group, 0)

    big = pl.BlockSpec((D_INNER, CHUNK), lambda c: (0, cidx(c)))
    par = pl.BlockSpec((D_INNER, 128), lambda c: (0, 0))
    in_specs = [big, pl.BlockSpec((512, CHUNK), lambda c: (4, cidx(c))), pl.BlockSpec((512, CHUNK), lambda c: (5, cidx(c))),
                pl.BlockSpec((SSD_HEADS, CHUNK), lambda c: (direction, cidx(c))),
                pl.BlockSpec((SSD_HEADS, 128), lambda c: (direction, 0))]
    args = [xbc_ct, xbc_ct, xbc_ct, dt_t, a_b]
    out_specs = [big, pl.BlockSpec((1, SSD_HEADS, SSD_HEAD_DIM, D_STATE), lambda c: (cidx(c), 0, 0, 0))]
    out_shape = [jax.ShapeDtypeStruct((D_INNER, t), f32), jax.ShapeDtypeStruct((nc, SSD_HEADS, SSD_HEAD_DIM, D_STATE), f32)]
    if prev is not None:
        in_specs.append(big)
        args.append(prev)
    if tail is not None:
        in_specs += [big, par, par]
        args += list(tail)
        out_specs.append(big)
        out_shape.append(jax.ShapeDtypeStruct((D_INNER, t), bf16))
    return pl.pallas_call(
        body, name=name, grid=(nc,), in_specs=in_specs, out_specs=out_specs, out_shape=out_shape,
        scratch_shapes=[pltpu.VMEM((SSD_HEADS, SSD_HEAD_DIM, D_STATE), f32)],
        compiler_params=_cparams(("arbitrary",)))(*args)


def ssd_bwd_t(xbc_ct, dt_t, a_b, dy_t, hprev, reverse, name, skip_b=None, prev=None, tail=None):
    t = xbc_ct.shape[1]
    nc = t // CHUNK
    direction = 1 if reverse else 0

    def cidx(c):
        return c if reverse else nc - 1 - c

    def body(*refs):
        x_ref, b_ref, c_ref, dt_ref, a_ref, dy_ref, hp_ref = refs[0:7]
        pos = 7
        skip_ref = None
        if skip_b is not None:
            skip_ref = refs[pos]
            pos += 1
        prev_refs = None
        if prev is not None:
            prev_refs = refs[pos:pos + 3]
            pos += 3
        if tail is not None:
            ys_ref, z_ref, nw_ref = refs[pos:pos + 3]
            pos += 3
        dx_ref, db_ref, dc_ref, ddt_ref, da_ref = refs[pos:pos + 5]
        pos += 5
        if tail is not None:
            dyout_ref, dz_ref, dnw_ref, ddx_ref = refs[pos:pos + 4]
            pos += 4
        dh_scr = refs[pos]

        @pl.when(pl.program_id(0) == 0)
        def _():
            dh_scr[...] = jnp.zeros_like(dh_scr)
            da_ref[...] = jnp.zeros_like(da_ref)
            if tail is not None:
                dnw_ref[...] = jnp.zeros_like(dnw_ref)
                ddx_ref[...] = jnp.zeros_like(ddx_ref)

        def group(g, carry):
            big, st, heads = _group_rows(g, 512), _group_rows(g, 128), _group_rows(g, HEADS_PER_GROUP)
            x_v, dy_v, dx_v = x_ref.at[big], dy_ref.at[big], dx_ref.at[big]
            hp_v, dh_v = hp_ref.at[0, heads], dh_scr.at[heads]
            dy_grp = None
            if tail is not None:
                zz = z_ref[big, :]
                sg = _sigmoid(zz)
                sl = zz * sg
                x_all = x_v[...]
                y = ys_ref[big, :] + skip_ref[big, :] * x_all
                yz = y * sl
                rstd = lax.rsqrt(jnp.mean(yz * yz, axis=0, keepdims=True) + NORM_EPS)
                yhat = yz * rstd
                gy = dy_v[...]
                dyhat = gy * nw_ref[big, :]
                dyz = rstd * (dyhat - yhat * jnp.mean(dyhat * yhat, axis=0, keepdims=True))
                dy_grp = dyz * sl
                dyout_ref[big, :] = dy_grp
                dz_ref[big, :] = (dyz * y * sg * (1.0 + zz * (1.0 - sg))).astype(dz_ref.dtype)
                dnw_ref[big, :] += gy * yhat
                ddx_ref[big, :] += dy_grp * x_all
            dt_blk = dt_ref[heads, :]
            a_blk = a_ref[heads, :]
            mask, tri, a8, acs, acs_t, lane = _ssd_decays(dt_blk, a_blk, reverse)
            sub = lax.broadcasted_iota(jnp.int32, (CHUNK, CHUNK), 0)
            mask_t = (sub >= lane) if reverse else (sub <= lane)
            bm = b_ref[st, :].T
            cm = c_ref[st, :].T
            cb = _dot_nt(cm, bm)
            cb_t = _dot_nt(bm, cm)
            tot = jnp.sum(a8, axis=1, keepdims=True)
            dcb = jnp.zeros((CHUNK, CHUNK), f32)
            dbm = jnp.zeros((CHUNK, D_STATE), f32)
            dcm = jnp.zeros((CHUNK, D_STATE), f32)
            dacs_rows, ddtx_rows = [], []
            for j in range(HEADS_PER_GROUP):
                rows = slice(SSD_HEAD_DIM * j, SSD_HEAD_DIM * (j + 1))
                col_j = _lane_col(acs, lane, j)
                row_j = acs_t[j:j + 1, :]
                dt_j = dt_blk[j:j + 1, :]
                tot_j = tot[j:j + 1, :]
                lmat = jnp.where(mask, jnp.exp(jnp.where(mask, col_j - row_j, 0.0)), 0.0)
                lmat_t = jnp.where(mask_t, jnp.exp(jnp.where(mask_t, row_j - col_j, 0.0)), 0.0)
                x = x_v[rows, :]
                xdt = x * dt_j
                dyh = dy_v[rows, :] if dy_grp is None else dy_grp[rows]
                hp = hp_v[j]
                dhn = dh_v[j]
                ml = _dot_tn(dyh, xdt) * lmat
                w_t = _dot_tn(xdt, dyh) * lmat_t * cb_t
                dcb = dcb + ml
                dacs = jnp.sum(w_t, axis=0, keepdims=True) - jnp.sum(ml * cb, axis=0, keepdims=True)
                ecol = jnp.exp(row_j)
                dec = jnp.exp(tot_j - row_j)
                dye = dyh * ecol
                yoff = _dot_nt(hp, cm) * ecol
                gmat = _dot_nt(dhn, bm)
                dxdt = _dot(dyh, cb * lmat) + dec * gmat
                s_dec = jnp.sum(xdt * gmat, axis=0, keepdims=True) * dec
                dacs = dacs + jnp.sum(dyh * yoff, axis=0, keepdims=True) - s_dec
                dcd = jnp.sum(jnp.sum(dhn * hp, axis=1, keepdims=True), axis=0, keepdims=True)
                dtot = jnp.sum(s_dec, axis=1, keepdims=True) + jnp.exp(tot_j) * dcd
                dacs_rows.append((dacs, dtot))
                ddtx_rows.append(jnp.sum(dxdt * x, axis=0, keepdims=True))
                dcm = dcm + _dot_tn(dye, hp)
                dbm = dbm + _dot_tn(xdt * dec, dhn)
                dxh = dxdt * dt_j
                if skip_ref is not None:
                    dxh = dxh + skip_ref.at[big][rows, :] * dyh
                if prev_refs is not None:
                    dxh = dxh + prev_refs[0].at[big][rows, :]
                dx_v[rows, :] = dxh
                dh_v[j] = jnp.exp(tot_j) * dhn + _dot(dye, cm)
            dcm = dcm + _dot(dcb, bm)
            dbm = dbm + _dot_tn(dcb, cm)
            dbt, dct = dbm.T, dcm.T
            if prev_refs is not None:
                dbt = dbt + prev_refs[1][st, :]
                dct = dct + prev_refs[2][st, :]
            db_ref[st, :] = dbt
            dc_ref[st, :] = dct
            dacs8 = jnp.concatenate([d for d, _ in dacs_rows], axis=0)
            dtot8 = jnp.concatenate([d for _, d in dacs_rows], axis=0)
            da8 = _dot_exact(dacs8, tri) + dtot8
            ddt_ref[heads, :] = da8 * a_blk + jnp.concatenate(ddtx_rows, axis=0)
            da_ref[heads, :] += da8 * dt_blk
            return carry

        lax.fori_loop(0, SSD_GROUPS, group, 0)

    big = pl.BlockSpec((D_INNER, CHUNK), lambda c: (0, cidx(c)))
    st = pl.BlockSpec((512, CHUNK), lambda c: (0, cidx(c)))
    in_specs = [big, pl.BlockSpec((512, CHUNK), lambda c: (4, cidx(c))), pl.BlockSpec((512, CHUNK), lambda c: (5, cidx(c))),
                pl.BlockSpec((SSD_HEADS, CHUNK), lambda c: (direction, cidx(c))),
                pl.BlockSpec((SSD_HEADS, 128), lambda c: (direction, 0)), big,
                pl.BlockSpec((1, SSD_HEADS, SSD_HEAD_DIM, D_STATE), lambda c: (cidx(c), 0, 0, 0))]
    args = [xbc_ct, xbc_ct, xbc_ct, dt_t, a_b, dy_t, hprev]
    if skip_b is not None:
        in_specs.append(pl.BlockSpec((D_INNER, 128), lambda c: (0, 0)))
        args.append(skip_b)
    if prev is not None:
        in_specs += [big, st, st]
        args += list(prev)
    par = pl.BlockSpec((D_INNER, 128), lambda c: (0, 0))
    out_specs = [big, st, st, pl.BlockSpec((SSD_HEADS, CHUNK), lambda c: (0, cidx(c))),
                 pl.BlockSpec((SSD_HEADS, 128), lambda c: (0, 0))]
    out_shape = [jax.ShapeDtypeStruct((D_INNER, t), f32), jax.ShapeDtypeStruct((512, t), f32),
                 jax.ShapeDtypeStruct((512, t), f32), jax.ShapeDtypeStruct((SSD_HEADS, t), f32),
                 jax.ShapeDtypeStruct((SSD_HEADS, 128), f32)]
    if tail is not None:
        in_specs += [big, big, par]
        args += list(tail)
        out_specs += [big, big, par, par]
        out_shape += [jax.ShapeDtypeStruct((D_INNER, t), f32), jax.ShapeDtypeStruct((SSD_COLS, t), bf16),
                      jax.ShapeDtypeStruct((D_INNER, 128), f32), jax.ShapeDtypeStruct((D_INNER, 128), f32)]
    return pl.pallas_call(
        body, name=name, grid=(nc,), in_specs=in_specs, out_specs=out_specs, out_shape=out_shape,
        scratch_shapes=[pltpu.VMEM((SSD_HEADS, SSD_HEAD_DIM, D_STATE), f32)],
        compiler_params=_cparams(("arbitrary",)))(*args)


def tail_fwd_t(y_scan, xbc_ct, z_t, skip_b, nw_b, tb=512):
    t = y_scan.shape[1]
    reps = tb // 128

    def body(ys_ref, x_ref, z_ref, d_ref, w_ref, o_ref):
        zz = z_ref[...]
        y = (ys_ref[...] + _lanes(d_ref[...], reps) * x_ref[...]) * (zz * _sigmoid(zz))
        rstd = lax.rsqrt(jnp.mean(y * y, axis=0, keepdims=True) + NORM_EPS)
        o_ref[...] = (y * rstd * _lanes(w_ref[...], reps)).astype(o_ref.dtype)

    blk = pl.BlockSpec((512, tb), lambda g, i: (g, i))
    par = pl.BlockSpec((512, 128), lambda g, i: (g, 0))
    return pl.pallas_call(
        body, name="tail_fwd", grid=(SSD_GROUPS, t // tb), in_specs=[blk, blk, blk, par, par], out_specs=blk,
        out_shape=jax.ShapeDtypeStruct((D_INNER, t), bf16),
        compiler_params=_cparams(("parallel", "parallel")))(y_scan, xbc_ct, z_t, skip_b, nw_b)


def tail_bwd_t(dyn_t, y_scan, xbc_ct, z_t, skip_b, nw_b, tb=512):
    t = y_scan.shape[1]
    reps = tb // 128

    def body(g_ref, ys_ref, x_ref, z_ref, d_ref, w_ref, dy_ref, dz_ref, dw_ref, dd_ref):
        zz = z_ref[...]
        sg = _sigmoid(zz)
        sl = zz * sg
        x = x_ref[...]
        y = ys_ref[...] + _lanes(d_ref[...], reps) * x
        yz = y * sl
        rstd = lax.rsqrt(jnp.mean(yz * yz, axis=0, keepdims=True) + NORM_EPS)
        yhat = yz * rstd
        g = g_ref[...]
        dyhat = g * _lanes(w_ref[...], reps)
        dyz = rstd * (dyhat - yhat * jnp.mean(dyhat * yhat, axis=0, keepdims=True))
        dy = dyz * sl
        dy_ref[...] = dy
        dz_ref[...] = (dyz * y * sg * (1.0 + zz * (1.0 - sg))).astype(dz_ref.dtype)

        def fold(v):
            s = v[:, 0:128]
            for q in range(1, reps):
                s = s + v[:, 128 * q:128 * (q + 1)]
            return s

        @pl.when(pl.program_id(1) == 0)
        def _():
            dw_ref[...] = jnp.zeros_like(dw_ref)
            dd_ref[...] = jnp.zeros_like(dd_ref)

        dw_ref[...] += fold(g * yhat)
        dd_ref[...] += fold(dy * x)

    blk = pl.BlockSpec((512, tb), lambda g, i: (g, i))
    par = pl.BlockSpec((512, 128), lambda g, i: (g, 0))
    return pl.pallas_call(
        body, name="tail_bwd", grid=(SSD_GROUPS, t // tb), in_specs=[blk, blk, blk, blk, par, par],
        out_specs=[blk, blk, par, par],
        out_shape=[jax.ShapeDtypeStruct((D_INNER, t), f32), jax.ShapeDtypeStruct((SSD_COLS, t), bf16),
                   jax.ShapeDtypeStruct((D_INNER, 128), f32), jax.ShapeDtypeStruct((D_INNER, 128), f32)],
        compiler_params=_cparams(("parallel", "arbitrary")))(dyn_t, y_scan, xbc_ct, z_t, skip_b, nw_b)


def merge_fwd(u_gate, bg_row, y_ssd, y_att, tb=512):
    t = y_ssd.shape[0]

    def body(ga_ref, gb_ref, ba_ref, bb_ref, ys_ref, ya_ref, o_ref):
        o_ref[...] = (_sigmoid(ga_ref[...] + ba_ref[...]) * ys_ref[...]
                      + _sigmoid(gb_ref[...] + bb_ref[...]) * ya_ref[...]).astype(o_ref.dtype)

    blk = pl.BlockSpec((tb, 512), lambda i, j: (i, j))
    blk2 = pl.BlockSpec((tb, 512), lambda i, j: (i, 2 + j))
    row = pl.BlockSpec((1, 512), lambda i, j: (0, j))
    row2 = pl.BlockSpec((1, 512), lambda i, j: (0, 2 + j))
    return pl.pallas_call(
        body, name="merge_fwd", grid=(t // tb, 2), in_specs=[blk, blk2, row, row2, blk, blk], out_specs=blk,
        out_shape=jax.ShapeDtypeStruct((t, D_MODEL), bf16),
        compiler_params=_cparams(("parallel", "parallel")))(u_gate, u_gate, bg_row, bg_row, y_ssd, y_att)


def merge_bwd(dm, u_gate, bg_row, y_ssd, y_att, tb=512):
    t = dm.shape[0]

    def body(dm_ref, ga_ref, gb_ref, ba_ref, bb_ref, ys_ref, ya_ref, dys_ref, dya_ref, dga_ref, dgb_ref, dba_ref, dbb_ref):
        d = dm_ref[...]
        sa = _sigmoid(ga_ref[...] + ba_ref[...])
        sb = _sigmoid(gb_ref[...] + bb_ref[...])
        dys_ref[...] = (d * sa).astype(dys_ref.dtype)
        dya_ref[...] = (d * sb).astype(dya_ref.dtype)
        dla = d * ys_ref[...] * sa * (1.0 - sa)
        dlb = d * ya_ref[...] * sb * (1.0 - sb)
        dga_ref[...] = dla.astype(dga_ref.dtype)
        dgb_ref[...] = dlb.astype(dgb_ref.dtype)

        @pl.when(pl.program_id(1) == 0)
        def _():
            dba_ref[...] = jnp.zeros_like(dba_ref)
            dbb_ref[...] = jnp.zeros_like(dbb_ref)

        dba_ref[...] += jnp.sum(dla, axis=0, keepdims=True)
        dbb_ref[...] += jnp.sum(dlb, axis=0, keepdims=True)

    blk = pl.BlockSpec((tb, 512), lambda j, i: (i, j))
    blk2 = pl.BlockSpec((tb, 512), lambda j, i: (i, 2 + j))
    row = pl.BlockSpec((1, 512), lambda j, i: (0, j))
    row2 = pl.BlockSpec((1, 512), lambda j, i: (0, 2 + j))
    act = jax.ShapeDtypeStruct((t, D_MODEL), bf16)
    vec = jax.ShapeDtypeStruct((1, D_MODEL), f32)
    return pl.pallas_call(
        body, name="merge_bwd", grid=(2, t // tb), in_specs=[blk, blk, blk2, row, row2, blk, blk],
        out_specs=[blk, blk, blk, blk, row, row], out_shape=[act, act, act, act, vec, vec],
        compiler_params=_cparams(("parallel", "arbitrary")))(dm, u_gate, u_gate, bg_row, bg_row, y_ssd, y_att)


def _ln_stats(r):
    mu = jnp.mean(r, axis=1, keepdims=True)
    xc = r - mu
    rstd = lax.rsqrt(jnp.mean(xc * xc, axis=1, keepdims=True) + NORM_EPS)
    return xc * rstd, rstd


def _ln_bwd(dy, xhat, rstd, g_row):
    dxh = dy * g_row
    return rstd * (dxh - jnp.mean(dxh, axis=1, keepdims=True) - xhat * jnp.mean(dxh * xhat, axis=1, keepdims=True))


def ln1_fwd(x, mix, g_row, b_row, tb=512):
    t = x.shape[0]

    def body(x_ref, m_ref, g_ref, b_ref, o_ref, ob_ref):
        xhat, _ = _ln_stats(ALPHA * x_ref[...] + m_ref[...])
        h = xhat * g_ref[...] + b_ref[...]
        o_ref[...] = h
        ob_ref[...] = h.astype(ob_ref.dtype)

    blk = pl.BlockSpec((tb, D_MODEL), lambda i: (i, 0))
    row = pl.BlockSpec((1, D_MODEL), lambda i: (0, 0))
    return pl.pallas_call(body, name="ln1_fwd", grid=(t // tb,), in_specs=[blk, blk, row, row], out_specs=[blk, blk],
                          out_shape=[jax.ShapeDtypeStruct((t, D_MODEL), f32), jax.ShapeDtypeStruct((t, D_MODEL), bf16)],
                          compiler_params=_cparams(("parallel",)))(x, mix, g_row, b_row)


def ln1_bwd(dh, x, mix, g_row, tb=512):
    t = x.shape[0]

    def body(dh_ref, x_ref, m_ref, g_ref, dr_ref, drb_ref, dg_ref, db_ref):
        xhat, rstd = _ln_stats(ALPHA * x_ref[...] + m_ref[...])
        dy = dh_ref[...]
        dr = _ln_bwd(dy, xhat, rstd, g_ref[...])
        dr_ref[...] = dr
        drb_ref[...] = dr.astype(drb_ref.dtype)

        @pl.when(pl.program_id(0) == 0)
        def _():
            dg_ref[...] = jnp.zeros_like(dg_ref)
            db_ref[...] = jnp.zeros_like(db_ref)

        dg_ref[...] += jnp.sum(dy * xhat, axis=0, keepdims=True)
        db_ref[...] += jnp.sum(dy, axis=0, keepdims=True)

    blk = pl.BlockSpec((tb, D_MODEL), lambda i: (i, 0))
    row = pl.BlockSpec((1, D_MODEL), lambda i: (0, 0))
    return pl.pallas_call(
        body, name="ln1_bwd", grid=(t // tb,), in_specs=[blk, blk, blk, row], out_specs=[blk, blk, row, row],
        out_shape=[jax.ShapeDtypeStruct((t, D_MODEL), f32), jax.ShapeDtypeStruct((t, D_MODEL), bf16),
                   jax.ShapeDtypeStruct((1, D_MODEL), f32), jax.ShapeDtypeStruct((1, D_MODEL), f32)],
        compiler_params=_cparams(("arbitrary",)))(dh, x, mix, g_row)


def ln2_loss(h1, f, g_row, b_row, target, tb=512):
    t = h1.shape[0]

    def body(h_ref, f_ref, g_ref, b_ref, t_ref, dr_ref, drb_ref, dg_ref, db_ref, loss_ref):
        xhat, rstd = _ln_stats(ALPHA * h_ref[...] + f_ref[...])
        g = g_ref[...]
        err = xhat * g + b_ref[...] - t_ref[...]
        dy = err * (1.0 / D_MODEL)
        dr = _ln_bwd(dy, xhat, rstd, g)
        dr_ref[...] = dr
        drb_ref[...] = dr.astype(drb_ref.dtype)

        @pl.when(pl.program_id(0) == 0)
        def _():
            dg_ref[...] = jnp.zeros_like(dg_ref)
            db_ref[...] = jnp.zeros_like(db_ref)
            loss_ref[...] = jnp.zeros_like(loss_ref)

        dg_ref[...] += jnp.sum(dy * xhat, axis=0, keepdims=True)
        db_ref[...] += jnp.sum(dy, axis=0, keepdims=True)
        part = jnp.sum(jnp.mean(err * err, axis=1, keepdims=True), axis=0, keepdims=True)
        loss_ref[...] += 0.5 * part

    blk = pl.BlockSpec((tb, D_MODEL), lambda i: (i, 0))
    row = pl.BlockSpec((1, D_MODEL), lambda i: (0, 0))
    return pl.pallas_call(
        body, name="ln2_loss", grid=(t // tb,), in_specs=[blk, blk, row, row, blk],
        out_specs=[blk, blk, row, row, pl.BlockSpec((8, 128), lambda i: (0, 0))],
        out_shape=[jax.ShapeDtypeStruct((t, D_MODEL), f32), jax.ShapeDtypeStruct((t, D_MODEL), bf16),
                   jax.ShapeDtypeStruct((1, D_MODEL), f32), jax.ShapeDtypeStruct((1, D_MODEL), f32),
                   jax.ShapeDtypeStruct((8, 128), f32)],
        compiler_params=_cparams(("arbitrary",)))(h1, f, g_row, b_row, target)


TAIL_BLOCK, TAIL_AT = divmod(OFF_TAIL, PACK_TILE)


def _sum4(ref):
    return ((ref[0].astype(f32) + ref[1].astype(f32)) + ref[2].astype(f32)) + ref[3].astype(f32)


def _adamw_update(g, w_ref, m_ref, v_ref, g_ref, d_ref, nm_ref, nv_ref):
    c1 = 1.0 - ADAM_B1 ** ADAM_STEP
    c2 = 1.0 - ADAM_B2 ** ADAM_STEP
    nm = ADAM_B1 * m_ref[...] + (1.0 - ADAM_B1) * g
    nv = ADAM_B2 * v_ref[...] + (1.0 - ADAM_B2) * (g * g)
    g_ref[...] = g
    nm_ref[...] = nm
    nv_ref[...] = nv
    d_ref[...] = -ADAM_LR * ((nm / c1) / (jnp.sqrt(nv / c2) + ADAM_EPS) + ADAM_WD * w_ref[...])


def adamw_early(landed, parts, me, w, m, v):
    off = LATE_ROWS // EARLY_TILE

    def body(me_ref, *refs):
        src = refs[0:N_DEV]
        own_ref, w_ref, m_ref, v_ref = refs[N_DEV:N_DEV + 4]
        mine = me_ref[0]
        g = None
        for s in range(N_DEV):
            term = jnp.where(mine == s, own_ref[0], src[s][0])
            g = term if g is None else g + term
        _adamw_update(g, w_ref, m_ref, v_ref, *refs[N_DEV + 4:])

    def slot(s):
        return pl.BlockSpec((1, EARLY_TILE, 1024), lambda i, me_ref: (jnp.where(me_ref[0] == s, (s + 1) % N_DEV, s), i, 0))

    shard = pl.BlockSpec((EARLY_TILE, 1024), lambda i, me_ref: (i + off, 0))
    out_blk = pl.BlockSpec((EARLY_TILE, 1024), lambda i, me_ref: (i, 0))
    grid_spec = pltpu.PrefetchScalarGridSpec(
        num_scalar_prefetch=1, grid=(EARLY_ROWS // EARLY_TILE,),
        in_specs=[slot(s) for s in range(N_DEV)]
        + [pl.BlockSpec((1, EARLY_TILE, 1024), lambda i, me_ref: (me_ref[0], i, 0)), shard, shard, shard],
        out_specs=[out_blk] * 4)
    out = jax.ShapeDtypeStruct((EARLY_ROWS, 1024), f32)
    return pl.pallas_call(body, name="adamw_early", grid_spec=grid_spec, out_shape=[out] * 4,
                          compiler_params=_cparams(("parallel",)))(me, *([landed] * N_DEV), parts, w, m, v)


def adamw(parts, tails, w, m, v):
    rows = parts.shape[1]

    def body(p_ref, t_ref, w_ref, m_ref, v_ref, g_ref, d_ref, nm_ref, nv_ref):
        g = _sum4(p_ref)
        with_tail = jnp.concatenate([g[0:TAIL_AT], _sum4(t_ref), g[TAIL_AT + ROWS_TAIL:]], axis=0)
        g = jnp.where(pl.program_id(0) == TAIL_BLOCK, with_tail, g)
        _adamw_update(g, w_ref, m_ref, v_ref, g_ref, d_ref, nm_ref, nv_ref)

    blk = pl.BlockSpec((PACK_TILE, 1024), lambda i: (i, 0))
    out = jax.ShapeDtypeStruct((rows, 1024), f32)
    return pl.pallas_call(
        body, name="adamw", grid=(rows // PACK_TILE,),
        in_specs=[pl.BlockSpec((4, PACK_TILE, 1024), lambda i: (0, i, 0)),
                  pl.BlockSpec((4, ROWS_TAIL, 1024), lambda i: (0, 0, 0)), blk, blk, blk], out_specs=[blk] * 4,
        out_shape=[out] * 4, compiler_params=_cparams(("parallel",)))(parts, tails, w, m, v)


def pair_sum(parts, recv, core):
    rows = parts.shape[1]

    def body(c_ref, a_ref, b_ref, o_ref, t_ref):
        s = a_ref[...] + b_ref[...]
        o_ref[...] = s.astype(o_ref.dtype)

        @pl.when(pl.program_id(1) == TAIL_BLOCK)
        def _():
            t_ref[...] = s[:, TAIL_AT:TAIL_AT + ROWS_TAIL]

    grid_spec = pltpu.PrefetchScalarGridSpec(
        num_scalar_prefetch=1, grid=(4, rows // PACK_TILE),
        in_specs=[pl.BlockSpec((1, PACK_TILE, 1024), lambda j, i, c_ref: (2 * j + c_ref[0], i, 0)),
                  pl.BlockSpec((1, PACK_TILE, 1024), lambda j, i, c_ref: (j, i, 0))],
        out_specs=[pl.BlockSpec((1, PACK_TILE, 1024), lambda j, i, c_ref: (j, i, 0)),
                   pl.BlockSpec((1, ROWS_TAIL, 1024), lambda j, i, c_ref: (j, 0, 0))])
    return pl.pallas_call(
        body, name="pair_sum", grid_spec=grid_spec,
        out_shape=[jax.ShapeDtypeStruct(recv.shape, bf16), jax.ShapeDtypeStruct((4, ROWS_TAIL, 1024), f32)],
        compiler_params=_cparams(("parallel", "arbitrary")))(core, parts, recv)


def _place():
    return lax.axis_index("x"), lax.axis_index("y"), lax.axis_index("c")


def all_gather_blocks(shard):
    rows, cols = shard.shape

    def body(x_ref, out_ref, send_sems, recv_sems, local_sem):
        x, y, c = _place()
        me, sibling = (x, y, c), (x, y, 1 - c)
        chips = [(1 - x, y), (x, 1 - y), (1 - x, 1 - y)]

        def slot(px, py, pc):
            return out_ref.at[4 * px + 2 * py + pc]

        def copy(k, block, to, src=None):
            return pltpu.make_async_remote_copy(
                src_ref=slot(*block) if src is None else src, dst_ref=slot(*block), send_sem=send_sems.at[k],
                recv_sem=recv_sems.at[k], device_id=to, device_id_type=MESH)

        mine = pltpu.make_async_copy(x_ref, slot(*me), local_sem)
        mine.start()
        first = [copy(0, me, sibling, src=x_ref)]
        first += [copy(1 + j, me, (*chip, c), src=x_ref) for j, chip in enumerate(chips)]
        for cp in first:
            cp.start()
        passed = [copy(4 + j, (*chip, c), sibling) for j, chip in enumerate(chips)]
        for j, chip in enumerate(chips):
            copy(1 + j, (*chip, c), me).wait_recv()
            passed[j].start()
        copy(0, sibling, me).wait_recv()
        for j, chip in enumerate(chips):
            copy(4 + j, (*chip, 1 - c), me).wait_recv()
        for cp in first + passed:
            cp.wait_send()
        mine.wait()

    return pl.pallas_call(
        body, name="all_gather_blocks", out_shape=jax.ShapeDtypeStruct((N_DEV, rows, cols), shard.dtype),
        in_specs=[pl.BlockSpec(memory_space=pl.ANY)], out_specs=pl.BlockSpec(memory_space=pl.ANY),
        scratch_shapes=[pltpu.SemaphoreType.DMA((7,)), pltpu.SemaphoreType.DMA((7,)), pltpu.SemaphoreType.DMA],
        compiler_params=pltpu.CompilerParams(has_side_effects=True))(shard)


def pair_exchange(parts):
    _, rows, cols = parts.shape

    def body(p_ref, recv_ref, send_sems, recv_sems):
        x, y, c = _place()
        copies = [pltpu.make_async_remote_copy(
            src_ref=p_ref.at[2 * j + 1 - c], dst_ref=recv_ref.at[j], send_sem=send_sems.at[j], recv_sem=recv_sems.at[j],
            device_id=(x, y, 1 - c), device_id_type=MESH) for j in range(4)]
        for cp in copies:
            cp.start()
        for cp in copies:
            cp.wait_recv()
        for cp in copies:
            cp.wait_send()

    return pl.pallas_call(
        body, name="pair_exchange", out_shape=jax.ShapeDtypeStruct((4, rows, cols), parts.dtype),
        in_specs=[pl.BlockSpec(memory_space=pl.ANY)], out_specs=pl.BlockSpec(memory_space=pl.ANY),
        scratch_shapes=[pltpu.SemaphoreType.DMA((4,)), pltpu.SemaphoreType.DMA((4,))],
        compiler_params=pltpu.CompilerParams(has_side_effects=True))(parts)


def chip_exchange(parts):
    n = len(parts)

    def body(*refs):
        p_refs, out_refs = refs[0:n], refs[n:2 * n]
        send_sems, recv_sems, local_sems = refs[2 * n:]
        x, y, c = _place()
        mine = 2 * x + y
        flips = [(x, 1 - y), (1 - x, y), (1 - x, 1 - y)]

        def copy(a, k, src_slot, dst_slot):
            px, py = flips[k]
            return pltpu.make_async_remote_copy(
                src_ref=p_refs[a].at[src_slot], dst_ref=out_refs[a].at[dst_slot], send_sem=send_sems.at[3 * a + k],
                recv_sem=recv_sems.at[3 * a + k], device_id=(px, py, c), device_id_type=MESH)

        local = [pltpu.make_async_copy(p_refs[a].at[mine], out_refs[a].at[mine], local_sems.at[a]) for a in range(n)]
        sends = [copy(a, k, 2 * flips[k][0] + flips[k][1], mine) for a in range(n) for k in range(3)]
        for cp in local + sends:
            cp.start()
        for a in range(n):
            for k in range(3):
                copy(a, k, mine, 2 * flips[k][0] + flips[k][1]).wait_recv()
        for cp in sends:
            cp.wait_send()
        for cp in local:
            cp.wait()

    return pl.pallas_call(
        body, name="chip_exchange", out_shape=[jax.ShapeDtypeStruct(p.shape, p.dtype) for p in parts],
        in_specs=[pl.BlockSpec(memory_space=pl.ANY)] * n, out_specs=[pl.BlockSpec(memory_space=pl.ANY)] * n,
        scratch_shapes=[pltpu.SemaphoreType.DMA((3 * n,)), pltpu.SemaphoreType.DMA((3 * n,)), pltpu.SemaphoreType.DMA((n,))],
        compiler_params=pltpu.CompilerParams(has_side_effects=True))(*parts)


_HBM = pl.BlockSpec(memory_space=pltpu.HBM)
_SEM = pl.BlockSpec(memory_space=pltpu.SEMAPHORE)


def _peer(k):
    x, y, c = _place()
    px, py, pc = (1 - x if k & 4 else x), (1 - y if k & 2 else y), (1 - c if k & 1 else c)
    return (px, py, pc), 4 * px + 2 * py + pc


def scatter_start(parts, name):
    per_device = parts.ndim == 3

    def body(p_ref, land_ref, send_sems, recv_sems, p_thru, land_thru, token):
        x, y, c = _place()
        me = 4 * x + 2 * y + c
        for k in range(1, N_DEV):
            place, idx = _peer(k)
            pltpu.make_async_remote_copy(src_ref=p_ref.at[idx] if per_device else p_ref, dst_ref=land_ref.at[me],
                                         send_sem=send_sems.at[k - 1], recv_sem=recv_sems.at[k - 1], device_id=place,
                                         device_id_type=MESH).start()
        token[...] = jnp.zeros_like(token)

    land_shape = parts.shape if per_device else (N_DEV,) + parts.shape
    landing = lax.empty(land_shape, parts.dtype)
    return pl.pallas_call(
        body, name=name,
        out_shape=(pltpu.SemaphoreType.DMA((N_DEV - 1,)), pltpu.SemaphoreType.DMA((N_DEV - 1,)),
                   pltpu.HBM(parts.shape, parts.dtype), pltpu.HBM(land_shape, parts.dtype),
                   jax.ShapeDtypeStruct((8, 128), f32)),
        in_specs=(_HBM, _HBM), out_specs=(_SEM, _SEM, _HBM, _HBM, pl.BlockSpec(memory_space=pltpu.VMEM)),
        input_output_aliases={0: 2, 1: 3},
        compiler_params=pltpu.CompilerParams(has_side_effects=pltpu.SideEffectType.DATAFLOW_SIDE_EFFECTING),
    )(pltpu.with_memory_space_constraint(parts, pltpu.HBM), pltpu.with_memory_space_constraint(landing, pltpu.HBM))


def scatter_wait(send_sems, recv_sems, parts_thru, land_thru, after, name):
    per_device = parts_thru.ndim == 3

    def body(p_ref, land_ref, send_sems, recv_sems, after_ref, p_out, land_out):
        for k in range(1, N_DEV):
            place, idx = _peer(k)
            copy = pltpu.make_async_remote_copy(src_ref=p_ref.at[idx] if per_device else p_ref, dst_ref=land_ref.at[idx],
                                                send_sem=send_sems.at[k - 1], recv_sem=recv_sems.at[k - 1],
                                                device_id=place, device_id_type=MESH)
            copy.wait_send()
            copy.wait_recv()

    return pl.pallas_call(
        body, name=name,
        out_shape=(pltpu.HBM(parts_thru.shape, parts_thru.dtype), pltpu.HBM(land_thru.shape, land_thru.dtype)),
        in_specs=(_HBM, _HBM, _SEM, _SEM, pl.BlockSpec(memory_space=pl.ANY)), out_specs=(_HBM, _HBM),
        input_output_aliases={0: 0, 1: 1},
        compiler_params=pltpu.CompilerParams(has_side_effects=pltpu.SideEffectType.DATAFLOW_SIDE_EFFECTING),
    )(parts_thru, land_thru, send_sems, recv_sems, after)


def _tail_rows(conv_part, small, extra):
    lead = conv_part.shape[:-1]
    rep = jnp.concatenate([small[n].reshape(-1).astype(f32) for n in SMALL] + [extra.reshape(1).astype(f32)])
    flat = jnp.concatenate([conv_part, jnp.broadcast_to(rep, lead + rep.shape),
                            jnp.zeros(lead + (ROWS_TAIL * 1024 - TAIL_ELEMS,), f32)], axis=-1)
    return flat.reshape(lead + (ROWS_TAIL, 1024))


def _late_rows(w_in_t, tail):
    lead = tail.shape[:-2]
    zeros = lambda r: jnp.zeros(lead + (r, 1024), f32)
    return jnp.concatenate([w_in_t, zeros(OFF_TAIL - IN_SHARD), tail, zeros(LATE_ROWS - OFF_TAIL - ROWS_TAIL)], axis=-2)


def _early_rows(w_ps, w_out, w_up_t, w_down, w_pa_t):
    return jnp.concatenate([w_ps, w_out, w_up_t, w_down, w_pa_t.reshape(w_pa_t.shape[:-2] + (ROWS_PA, 1024))], axis=-2)


def _pack_shard(vals):
    tail = _tail_rows(vals["conv_w"].reshape(-1), vals, jnp.zeros((), f32))
    return jnp.concatenate([_late_rows(vals["w_in"].T, tail),
                            _early_rows(vals["w_proj_ssd"], vals["w_out"], vals["w_up"].T, vals["w_down"],
                                        vals["w_proj_attn"].T)], axis=0)


def _unpack_shard(late, early):
    e = lambda lo, hi: early[lo - LATE_ROWS:hi - LATE_ROWS]
    out = {"w_in": late[0:IN_SHARD].T, "w_proj_ssd": e(OFF_PS, OFF_OUT), "w_out": e(OFF_OUT, OFF_UP),
           "w_up": e(OFF_UP, OFF_DOWN).T, "w_down": e(OFF_DOWN, OFF_PA),
           "w_proj_attn": e(OFF_PA, PACK_ROWS).reshape(D_MODEL // N_DEV, ATTN_OUT).T}
    flat = late[OFF_TAIL:OFF_TAIL + ROWS_TAIL].reshape(-1)
    out["conv_w"] = flat[0:CONV_SHARD].reshape(D_CONV, CONV_DIM // N_DEV)
    off = CONV_SHARD
    for n in SMALL:
        out[n] = flat[off:off + SMALL_SIZES[n]]
        off += SMALL_SIZES[n]
    out["_extra"] = flat[off]
    return out


def _blocks(g):
    return g.reshape(N_DEV, g.shape[0] // N_DEV, g.shape[1])


def _pack_early_parts(full):
    return _early_rows(_blocks(full["w_proj_ssd"]), _blocks(full["w_out"]), _blocks(full["w_up_t"]),
                       _blocks(full["w_down"]), _blocks(full["w_proj_attn_t"]))


def _pack_late_parts(full, small, extra):
    conv = full["conv_w"].reshape(D_CONV, N_DEV, CONV_DIM // N_DEV).transpose(1, 0, 2).reshape(N_DEV, CONV_SHARD)
    return _late_rows(_blocks(full["w_in_t"]), _tail_rows(conv, small, extra))


def _gather_weights(w):
    conv_bits = lax.bitcast_convert_type(w["conv_w"], bf16).reshape(-1)
    conv_rows = jnp.concatenate([conv_bits, jnp.zeros((16 * 1024 - 2 * CONV_SHARD,), bf16)]).reshape(16, 1024)
    packed = _pack_shard(w)
    first = OFF_TAIL + ROWS_TAIL
    got = all_gather_blocks(jnp.concatenate([packed[0:OFF_TAIL].astype(bf16), conv_rows], axis=0))
    got, rest = lax.optimization_barrier((got, packed[first:].astype(bf16)))
    send_sems, recv_sems, rest_thru, land_thru, token = scatter_start(rest, "gather_start")
    conv =lax.bitcast_convert_type(got[:, OFF_TAIL:OFF_TAIL + 4].reshape(N_DEV, 4096)[:, 0:2 * CONV_SHARD]
                                    .reshape(N_DEV, D_CONV, CONV_DIM // N_DEV, 2), f32)
    now = {"w_in_t": got[:, 0:IN_SHARD].reshape(IN_COLS, 1024), "conv_w": conv.transpose(1, 0, 2).reshape(D_CONV, CONV_DIM)}

    def later(after):
        mine, landed = scatter_wait(send_sems, recv_sems, rest_thru, land_thru, after, "gather_wait")
        x, y, c = _place()
        landed = lax.dynamic_update_slice(landed, mine[None], (4 * x + 2 * y + c, 0, 0))
        whole = lambda lo, hi: landed[:, lo - first:hi - first].reshape(N_DEV * (hi - lo), 1024)
        return {"w_proj_ssd": whole(OFF_PS, OFF_OUT), "w_out": whole(OFF_OUT, OFF_UP), "w_up_t": whole(OFF_UP, OFF_DOWN),
                "w_down": whole(OFF_DOWN, OFF_PA),
                "w_proj_attn_t": landed[:, OFF_PA - first:PACK_ROWS - first].reshape(D_MODEL, ATTN_OUT)}

    return now, later, token


def _row(v, width=None):
    v = v.reshape(1, -1).astype(f32)
    return v if width is None else jnp.pad(v, ((0, 0), (0, width - v.shape[1])))


def _lanes256(vf, vb):
    z = jnp.zeros((96,), f32)
    return jnp.concatenate([vf.astype(f32), z, vb.astype(f32), z]).reshape(1, 256)


def _local_step(x2, tgt, wf, p, send_early=None, late_weights=None, start_token=None):
    t = x2.shape[0]
    o = np.cumsum((0,) + IN_SPLITS)
    wt = wf["w_in_t"]
    wt_z, wt_xbc, wt_dt = wt[o[0]:o[1]], wt[o[1]:o[2]], wt[o[2]:o[4]]
    wt_qkv, wt_gate = wt[o[4]:o[7]], wt[o[7]:o[8]]

    spread = lambda v: jnp.broadcast_to(v.astype(f32)[..., None], v.shape + (128,))
    conv_w_b, conv_b_b = spread(wf["conv_w"]), spread(p["conv_b"])
    dt_bias_b = spread(jnp.concatenate([p["dt_bias_f"], p["dt_bias_b"]]))
    a_f, a_b = -jnp.exp(p["a_log_f"].astype(f32)), -jnp.exp(p["a_log_b"].astype(f32))
    a_coef_b = spread(jnp.concatenate([a_f, a_b]))
    skip_b = spread(jnp.repeat(p["d_skip"], SSD_HEAD_DIM))
    nw_b, bg_row = spread(p["ssd_norm_w"]), _row(p["b_gate"])
    g1, b1, g2, b2 = _row(p["ln1_g"]), _row(p["ln1_b"]), _row(p["ln2_g"]), _row(p["ln2_b"])

    xb = (x2 if start_token is None else x2 + start_token[0, 0]).astype(MXU_DTYPE)
    xt = xb.T
    u_z = mm_nn(wt_z, xt, "in_z")
    u_xbc = mm_nn(wt_xbc, xt, "in_xbc")
    u_dt = mm_nn(wt_dt, xt, "in_dt")
    u_qkv = mm_nt_split(xb, wt_qkv, "in_qkv", 256, bf16)
    u_gate = mm_nt(xb, wt_gate, "in_gate")
    xbc_c = conv_fwd_t(u_xbc, conv_w_b, conv_b_b)
    dt_t = dt_fwd_t(u_dt, dt_bias_b)
    y_f, h_f = ssd_fwd_t(xbc_c, dt_t, a_coef_b, False, "ssd_fwd_f")
    y_scan, h_b, yn = ssd_fwd_t(xbc_c, dt_t, a_coef_b, True, "ssd_fwd_b", prev=y_f, tail=(u_z, skip_b, nw_b))
    if late_weights is not None:
        wf = {**wf, **late_weights(yn)}
    y_ssd = mm_tn(yn, wf["w_proj_ssd"], "proj_ssd")

    def strided(a, dil):
        return a.reshape(t // dil, dil * 256)

    qkv, outs, lses = [], [], []
    for pi, (_, dil) in enumerate(DIL_PATTERNS):
        q, k, v = (strided(u_qkv[N_PATTERNS * s + pi], dil) for s in range(3))
        qkv.append((q, k, v))
        op, lp = attn_fwd(q, k, v, pi, dil, f"attn_fwd_{pi}")
        outs.append(op.reshape(t, 256))
        lses.append(lp.reshape(t, 256))
    ya, lse = attn_combine(outs, lses)
    y_att = mm_nt(ya, wf["w_proj_attn_t"], "proj_attn")
    m = merge_fwd(u_gate, bg_row, y_ssd, y_att)
    mix = mm_nn(m, wf["w_out"], "out_proj")
    h1, h1b = ln1_fwd(x2, mix, g1, b1)
    r_up, p_act = mm_nt(h1b, wf["w_up_t"], "mlp_up", relu2=True)
    f_dn = mm_nn(p_act, wf["w_down"], "mlp_down")
    dr2, dr2b, dg2, db2, loss8 = ln2_loss(h1, f_dn, g2, b2, tgt)

    full, small = {}, {}
    da = mm_nt(dr2b, wf["w_down"], "d_mlp_act", out_dtype=bf16, relu2_of=r_up)
    full["w_down"] = mm_tn(p_act, dr2b, "dw_down")
    full["w_up_t"] = mm_tn(da, h1b, "dw_up")
    dh1 = mm_nn(da, wf["w_up_t"], "d_h1", acc_in=dr2, acc_scale=ALPHA)
    dr1, dr1b, dg1, db1 = ln1_bwd(dh1, x2, mix, g1)
    dm = mm_nt(dr1b, wf["w_out"], "d_merge")
    full["w_out"] = mm_tn(m, dr1b, "dw_out")
    dys, dya_p, dga, dgb, dba, dbb = merge_bwd(dm, u_gate, bg_row, y_ssd, y_att)
    dyn = mm_nt(wf["w_proj_ssd"], dys, "d_yn")
    full["w_proj_ssd"] = mm_nn(yn, dys, "dw_proj_ssd")
    dya = mm_nn(dya_p, wf["w_proj_attn_t"], "d_ya")
    full["w_proj_attn_t"] = mm_tn(dya_p, ya, "dw_proj_attn")
    if send_early is not None:
        skip_b = skip_b + send_early(full)[0, 0]

    dxf, dbf, dcf, ddtf, daf, dy, du_ssd, dnw, ddx = ssd_bwd_t(xbc_c, dt_t, a_coef_b, dyn, h_f, False, "ssd_bwd_f",
                                                               skip_b=skip_b, tail=(y_scan, u_z, nw_b))
    dxs, dbs, dcs, ddtb, dab = ssd_bwd_t(xbc_c, dt_t, a_coef_b, dy, h_b, True, "ssd_bwd_b", prev=(dxf, dbf, dcf))
    du_ssd, dcw_x, dcb_x = conv_bwd_t(u_xbc, dxs, conv_w_b, conv_b_b, du_ssd, "conv_bwd_x", 0)
    du_ssd, dcw_b, dcb_b = conv_bwd_t(u_xbc, dbs, conv_w_b, conv_b_b, du_ssd, "conv_bwd_b", D_INNER)
    du_ssd, dcw_c, dcb_c = conv_bwd_t(u_xbc, dcs, conv_w_b, conv_b_b, du_ssd, "conv_bwd_c", D_INNER + 512)
    du_ssd, dbias = dt_bwd_t(ddtf, ddtb, u_dt, dt_bias_b, du_ssd)

    delta = attn_delta(dya, ya)
    dqs, dks, dvs = [], [], []
    for pi, (_, dil) in enumerate(DIL_PATTERNS):
        q, k, v = qkv[pi]
        sd, sl_, sdel = strided(dya, dil), strided(lse, dil), strided(delta, dil)
        dqs.append(attn_dq(q, k, v, sd, sl_, sdel, pi, dil, f"attn_dq_{pi}").reshape(t, 256))
        dk, dv = attn_dkv(q, k, v, sd, sl_, sdel, pi, dil, f"attn_dkv_{pi}")
        dks.append(dk.reshape(t, 256))
        dvs.append(dv.reshape(t, 256))
    du_qkv = jnp.concatenate(dqs + dks + dvs, axis=1)
    du_gate = jnp.concatenate([dga, dgb], axis=1)

    dx = mm_tn(du_ssd, wt[0:SSD_COLS], "dx_ssd", acc_in=dr1, acc_scale=ALPHA)
    dx = mm_nn(du_qkv, wt_qkv, "dx_qkv", acc_in=dx)
    dx = mm_nn(du_gate, wt_gate, "dx_gate", acc_in=dx)
    full["w_in_t"] = jnp.concatenate(
        [mm_nn(du_ssd, xb, "dw_in_ssd"), mm_tn(du_qkv, xb, "dw_in_qkv"), mm_tn(du_gate, xb, "dw_in_gate")], axis=0)
    lanes = lambda v: jnp.sum(v, axis=-1)
    full["conv_w"] = jnp.concatenate([lanes(dcw_x), lanes(dcw_b), lanes(dcw_c)], axis=1)

    small["b_gate"] = jnp.concatenate([dba, dbb], axis=1)
    small["conv_b"] = jnp.concatenate([lanes(dcb_x), lanes(dcb_b), lanes(dcb_c)])
    dbias = lanes(dbias)
    small["dt_bias_f"], small["dt_bias_b"] = dbias[0:32], dbias[32:64]
    small["a_log_f"] = lanes(daf) * a_f
    small["a_log_b"] = lanes(dab) * a_b
    small["d_skip"] = jnp.sum(lanes(ddx).reshape(SSD_HEADS, SSD_HEAD_DIM), axis=1)
    small["ssd_norm_w"] = lanes(dnw)
    small["ln1_g"], small["ln1_b"], small["ln2_g"], small["ln2_b"] = dg1, db1, dg2, db2
    return loss8[0, 0], dx, full, small


def kernel(x, w_in, b_gate, conv_w, conv_b, dt_bias_f, dt_bias_b, a_log_f, a_log_b, d_skip, ssd_norm_w, w_proj_ssd, w_proj_attn, w_out, ln1_g, ln1_b, w_up, w_down, ln2_g, ln2_b, loss_target, m_w_in, m_b_gate, m_conv_w, m_conv_b, m_dt_bias_f, m_dt_bias_b, m_a_log_f, m_a_log_b, m_d_skip, m_ssd_norm_w, m_w_proj_ssd, m_w_proj_attn, m_w_out, m_ln1_g, m_ln1_b, m_w_up, m_w_down, m_ln2_g, m_ln2_b, v_w_in, v_b_gate, v_conv_w, v_conv_b, v_dt_bias_f, v_dt_bias_b, v_a_log_f, v_a_log_b, v_d_skip, v_ssd_norm_w, v_w_proj_ssd, v_w_proj_attn, v_w_out, v_ln1_g, v_ln1_b, v_w_up, v_w_down, v_ln2_g, v_ln2_b):
    given = dict(locals())
    w = {n: given[n] for n in WEIGHTS}
    mom = {n: given["m_" + n] for n in WEIGHTS}
    var = {n: given["v_" + n] for n in WEIGHTS}
    t = x.shape[1]
    wf, late_weights, start_token = _gather_weights(w)
    in_flight = []

    def send_early(full):
        send_sems, recv_sems, parts_thru, land_thru, token = scatter_start(_pack_early_parts(full), "scatter_start")
        in_flight.append((send_sems, recv_sems, parts_thru, land_thru))
        return token

    loss, dx, full, small = _local_step(x.reshape(t, D_MODEL), loss_target.reshape(t, D_MODEL), wf, w, send_early,
                                        late_weights, start_token)
    late = _pack_late_parts(full, small, loss)
    x_, y_, c_ = _place()
    core = c_.astype(jnp.int32).reshape(1)
    me = (4 * x_ + 2 * y_ + c_).astype(jnp.int32).reshape(1)
    wp, mp, vp = _pack_shard(w), _pack_shard(mom), _pack_shard(var)
    early_parts, landed = scatter_wait(*in_flight[0], late, "scatter_wait")
    early_out = adamw_early(landed, early_parts, me, wp, mp, vp)
    parts, tails = chip_exchange(pair_sum(late, pair_exchange(late), core))
    late_out = adamw(parts, tails, wp, mp, vp)
    g, delta, new_m, new_v = (_unpack_shard(a, b) for a, b in zip(late_out, early_out))
    outs = [g["_extra"], dx.reshape(x.shape)]
    for d in (g, delta, new_m, new_v):
        outs += [d[n].reshape(w[n].shape) for n in WEIGHTS]
    return tuple(outs)
```

```python
import functools
import math

import jax
import jax.numpy as jnp
import numpy as np
from jax import lax
from jax.experimental import pallas as pl
from jax.experimental.pallas import tpu as pltpu

f32 = jnp.float32
bf16 = jnp.bfloat16
MXU_DTYPE = jnp.bfloat16

N_DEV = 8
D_MODEL = 1024
D_INNER = 2048
SSD_HEADS = 32
SSD_HEAD_DIM = 64
SSD_GROUPS = 4
D_STATE = 128
D_CONV = 5
CHUNK = 128
CONV_DIM = D_INNER + 2 * SSD_GROUPS * D_STATE
NORM_EPS = 1e-5
ATTN_HEAD_DIM = 64
DIL_PATTERNS = ((128, 1), (512, 4), (2048, 16))
N_PATTERNS = len(DIL_PATTERNS)
HEADS_PER_PATTERN = 4
ATTN_HEADS = 12
ATTN_WIDTH = 768
ATTN_OUT = 256
D_FF = 4096
ALPHA = 2.0 ** 0.25
IN_SPLITS = (D_INNER, CONV_DIM, SSD_HEADS, SSD_HEADS, ATTN_WIDTH, ATTN_WIDTH, ATTN_WIDTH, 2 * D_MODEL)
IN_COLS = sum(IN_SPLITS)
SSD_COLS = sum(IN_SPLITS[0:4])
ADAM_LR, ADAM_B1, ADAM_B2, ADAM_EPS, ADAM_WD, ADAM_STEP = 0.001, 0.9, 0.999, 1e-08, 0.01, 10
NEG_BIG = -1e30
VMEM_LIMIT = 56 * 1024 * 1024
MESH = pl.DeviceIdType.MESH

SMALL = ("b_gate", "conv_b", "dt_bias_f", "dt_bias_b", "a_log_f", "a_log_b", "d_skip", "ssd_norm_w",
         "ln1_g", "ln1_b", "ln2_g", "ln2_b")
WEIGHTS = ("w_in", "b_gate", "conv_w", "conv_b", "dt_bias_f", "dt_bias_b", "a_log_f", "a_log_b", "d_skip",
           "ssd_norm_w", "w_proj_ssd", "w_proj_attn", "w_out", "ln1_g", "ln1_b", "w_up", "w_down", "ln2_g", "ln2_b")
SMALL_SIZES = {"b_gate": 2 * D_MODEL, "conv_b": CONV_DIM, "dt_bias_f": 32, "dt_bias_b": 32, "a_log_f": 32, "a_log_b": 32,
               "d_skip": 32, "ssd_norm_w": D_INNER, "ln1_g": D_MODEL, "ln1_b": D_MODEL, "ln2_g": D_MODEL, "ln2_b": D_MODEL}
IN_SHARD = IN_COLS // N_DEV
OFF_TAIL = 1200
ROWS_TAIL = 16
PACK_TILE = 128
LATE_ROWS = 1280
ROWS_PS, ROWS_OUT, ROWS_UP, ROWS_DOWN, ROWS_PA = D_INNER // N_DEV, D_MODEL // N_DEV, D_FF // N_DEV, D_FF // N_DEV, 32
OFF_PS = LATE_ROWS
OFF_OUT = OFF_PS + ROWS_PS
OFF_UP = OFF_OUT + ROWS_OUT
OFF_DOWN = OFF_UP + ROWS_UP
OFF_PA = OFF_DOWN + ROWS_DOWN
PACK_ROWS = OFF_PA + ROWS_PA
EARLY_ROWS = PACK_ROWS - LATE_ROWS
EARLY_TILE = 160
CONV_SHARD = D_CONV * CONV_DIM // N_DEV
TAIL_ELEMS = CONV_SHARD + sum(SMALL_SIZES.values()) + 1


def _cparams(sem=None, **kw):
    return pltpu.CompilerParams(dimension_semantics=sem, vmem_limit_bytes=VMEM_LIMIT, **kw)


def _mx(v):
    return v.astype(MXU_DTYPE)


def _dot(a, b):
    return jnp.dot(_mx(a), _mx(b), preferred_element_type=f32)


def _dot_nt(a, b):
    return lax.dot_general(_mx(a), _mx(b), (((1,), (1,)), ((), ())), preferred_element_type=f32)


def _dot_tn(a, b):
    return lax.dot_general(_mx(a), _mx(b), (((0,), (0,)), ((), ())), preferred_element_type=f32)


def _dot_exact(a, b):
    return jnp.dot(a, b, precision=lax.Precision.HIGHEST, preferred_element_type=f32)


def _sigmoid(v):
    return 1.0 / (1.0 + jnp.exp(-v))


def _pick(n, prefs):
    for p in prefs:
        if n % p == 0:
            return p
    return n


MM_TILE = 1024


def mm_nn(a, b, name, out_dtype=f32, acc_in=None, acc_scale=1.0):
    m, k = a.shape
    n = b.shape[1]
    tm = _pick(m, (MM_TILE, 576, 512, 256, 128, 64))
    tn = _pick(n, (MM_TILE, 512, 256, 128))
    tk = _pick(k, (2048, 1536, 1152, 1024, 768, 512, 256, 128))
    nk = k // tk

    def body(*refs):
        a_ref, b_ref = refs[0:2]
        c_ref = refs[2] if acc_in is not None else None
        o_ref = refs[3] if acc_in is not None else refs[2]

        def finish(r):
            if acc_in is not None:
                r = r + acc_scale * c_ref[...]
            o_ref[...] = r.astype(o_ref.dtype)

        if nk == 1:
            finish(_dot(a_ref[...], b_ref[...]))
            return
        acc_ref = refs[-1]
        kk = pl.program_id(2)

        @pl.when(kk == 0)
        def _():
            acc_ref[...] = jnp.zeros_like(acc_ref)

        acc_ref[...] += _dot(a_ref[...], b_ref[...])

        @pl.when(kk == nk - 1)
        def _():
            finish(acc_ref[...])

    in_specs = [pl.BlockSpec((tm, tk), lambda i, j, kk: (i, kk)), pl.BlockSpec((tk, tn), lambda i, j, kk: (kk, j))]
    args = [a, b]
    if acc_in is not None:
        in_specs.append(pl.BlockSpec((tm, tn), lambda i, j, kk: (i, j)))
        args.append(acc_in)
    return pl.pallas_call(
        body, name=name, grid=(m // tm, n // tn, nk), in_specs=in_specs,
        out_specs=pl.BlockSpec((tm, tn), lambda i, j, kk: (i, j)),
        out_shape=jax.ShapeDtypeStruct((m, n), out_dtype),
        scratch_shapes=[pltpu.VMEM((tm, tn), f32)] if nk > 1 else [],
        compiler_params=_cparams(("parallel", "parallel", "arbitrary")))(*args)


def mm_nt(a, b, name, out_dtype=f32, relu2=None, relu2_of=None):
    m, k = a.shape
    n = b.shape[0]
    tm = _pick(m, (MM_TILE, 512, 256, 128, 64))
    tn = _pick(n, (MM_TILE, 768, 512, 256, 128))

    def body(*refs):
        r = _dot_nt(refs[0][...], refs[1][...])
        if relu2:
            pos = jnp.maximum(r, 0.0)
            refs[2][...] = pos.astype(refs[2].dtype)
            refs[3][...] = (pos * pos).astype(refs[3].dtype)
        elif relu2_of is not None:
            refs[3][...] = (r * (2.0 * refs[2][...].astype(f32))).astype(refs[3].dtype)
        else:
            refs[2][...] = r.astype(refs[2].dtype)

    blk = pl.BlockSpec((tm, tn), lambda i, j: (i, j))
    in_specs = [pl.BlockSpec((tm, k), lambda i, j: (i, 0)), pl.BlockSpec((tn, k), lambda i, j: (j, 0))]
    args = [a, b]
    if relu2_of is not None:
        in_specs.append(blk)
        args.append(relu2_of)
    if relu2:
        out_specs, out_shape = [blk, blk], [jax.ShapeDtypeStruct((m, n), bf16), jax.ShapeDtypeStruct((m, n), bf16)]
    else:
        out_specs, out_shape = blk, jax.ShapeDtypeStruct((m, n), out_dtype)
    return pl.pallas_call(body, name=name, grid=(m // tm, n // tn), in_specs=in_specs, out_specs=out_specs,
                          out_shape=out_shape, compiler_params=_cparams(("parallel", "parallel")))(*args)


def mm_nt_split(a, b, name, width, out_dtype=f32):
    m, k = a.shape
    n = b.shape[0]
    tm = MM_TILE
    parts = n // width

    def body(a_ref, b_ref, *o_refs):
        r = _dot_nt(a_ref[...], b_ref[...])
        for q in range(parts):
            o_refs[q][...] = r[:, width * q:width * (q + 1)].astype(o_refs[q].dtype)

    blk = pl.BlockSpec((tm, width), lambda i: (i, 0))
    return pl.pallas_call(
        body, name=name, grid=(m // tm,),
        in_specs=[pl.BlockSpec((tm, k), lambda i: (i, 0)), pl.BlockSpec((n, k), lambda i: (0, 0))],
        out_specs=[blk] * parts, out_shape=[jax.ShapeDtypeStruct((m, width), out_dtype)] * parts,
        compiler_params=_cparams(("parallel",)))(a, b)


def mm_tn(a, b, name, acc_in=None, acc_scale=1.0):
    k, m = a.shape
    n = b.shape[1]
    tm = _pick(m, (MM_TILE, 768, 512, 256, 128))
    tn = _pick(n, (MM_TILE, 512, 256, 128))
    tk = _pick(k, (1024, 768, 576, 512, 256, 128, 64))
    nk = k // tk

    def body(*refs):
        a_ref, b_ref, o_ref = refs[0], refs[1], refs[-1]
        kk = pl.program_id(2)

        @pl.when(kk == 0)
        def _():
            o_ref[...] = jnp.zeros_like(o_ref) if acc_in is None else acc_scale * refs[2][...]

        o_ref[...] += _dot_tn(a_ref[...], b_ref[...])

    in_specs = [pl.BlockSpec((tk, tm), lambda i, j, kk: (kk, i)), pl.BlockSpec((tk, tn), lambda i, j, kk: (kk, j))]
    args = [a, b]
    if acc_in is not None:
        in_specs.append(pl.BlockSpec((tm, tn), lambda i, j, kk: (i, j)))
        args.append(acc_in)
    return pl.pallas_call(
        body, name=name, grid=(m // tm, n // tn, nk), in_specs=in_specs,
        out_specs=pl.BlockSpec((tm, tn), lambda i, j, kk: (i, j)),
        out_shape=jax.ShapeDtypeStruct((m, n), f32),
        compiler_params=_cparams(("parallel", "parallel", "arbitrary")))(*args)


def _halo_specs(tb, cb, nt, off=0):
    r = tb // 8
    return [pl.BlockSpec((8, cb), lambda j, i: (jnp.maximum(i * r - 1, 0), j + off)),
            pl.BlockSpec((tb, cb), lambda j, i: (i, j + off)),
            pl.BlockSpec((8, cb), lambda j, i: (jnp.minimum((i + 1) * r, nt * r - 1), j + off))]


def _with_halo(prev_ref, own_ref, next_ref, i, nt):
    prev = jnp.where(i > 0, prev_ref[...].astype(f32), 0.0)
    nxt = jnp.where(i < nt - 1, next_ref[...].astype(f32), 0.0)
    return jnp.concatenate([prev, own_ref[...].astype(f32), nxt], axis=0)


def _shifted(xcat, s, tb):
    n = xcat.shape[0]
    return pltpu.roll(xcat, (-s) % n, 0)[8:8 + tb]


def conv_fwd(xbc, w8, b_row, tb=512, cb=512):
    t, c = xbc.shape
    nt = t // tb

    def body(prev_ref, own_ref, next_ref, w_ref, b_ref, o_ref):
        i = pl.program_id(1)
        xcat = _with_halo(prev_ref, own_ref, next_ref, i, nt)
        w = w_ref[...]
        pre = b_ref[...] + w[0:1] * _shifted(xcat, -2, tb)
        for k in range(1, D_CONV):
            pre = pre + w[k:k + 1] * _shifted(xcat, k - 2, tb)
        o_ref[...] = pre * _sigmoid(pre)

    return pl.pallas_call(
        body, name="conv_fwd", grid=(c // cb, nt),
        in_specs=_halo_specs(tb, cb, nt) + [pl.BlockSpec((8, cb), lambda j, i: (0, j)), pl.BlockSpec((1, cb), lambda j, i: (0, j))],
        out_specs=pl.BlockSpec((tb, cb), lambda j, i: (i, j)), out_shape=jax.ShapeDtypeStruct((t, c), f32),
        compiler_params=_cparams(("parallel", "parallel")))(xbc, xbc, xbc, w8, b_row)


def conv_bwd(xbc, xoff, grads, scales, w8, b_row, name, tb=512, cb=512):
    t, c = grads[0].shape
    nt = t // tb
    ng = len(grads)
    has_scale = [s is not None for s in scales]

    def body(*refs):
        i = pl.program_id(1)
        xr = refs[0:3]
        gr = [refs[3 + 3 * q: 6 + 3 * q] for q in range(ng)]
        pos = 3 + 3 * ng
        sr = []
        for q in range(ng):
            if has_scale[q]:
                sr.append(refs[pos])
                pos += 1
            else:
                sr.append(None)
        w_ref, b_ref, dx_ref, dw_ref, db_ref = refs[pos:pos + 5]
        xcat = _with_halo(*xr, i, nt)
        gcat = None
        for q in range(ng):
            gq = _with_halo(*gr[q], i, nt)
            if sr[q] is not None:
                gq = gq * sr[q][...]
            gcat = gq if gcat is None else gcat + gq
        w = w_ref[...]
        n = tb + 16
        pre = b_ref[...] + w[0:1] * pltpu.roll(xcat, 2, 0)
        for k in range(1, D_CONV):
            pre = pre + w[k:k + 1] * pltpu.roll(xcat, (2 - k) % n, 0)
        sg = _sigmoid(pre)
        dpre = gcat * sg * (1.0 + pre * (1.0 - sg))
        dx = w[0:1] * _shifted(dpre, 2, tb)
        for k in range(1, D_CONV):
            dx = dx + w[k:k + 1] * _shifted(dpre, 2 - k, tb)
        dx_ref[...] = dx.astype(dx_ref.dtype)
        dp_own = dpre[8:8 + tb]
        rows = [jnp.sum(dp_own * _shifted(xcat, k - 2, tb), axis=0, keepdims=True) for k in range(D_CONV)]
        dw = jnp.concatenate(rows + [jnp.zeros((8 - D_CONV, cb), f32)], axis=0)
        db = jnp.sum(dp_own, axis=0, keepdims=True)

        @pl.when(i == 0)
        def _():
            dw_ref[...] = jnp.zeros_like(dw_ref)
            db_ref[...] = jnp.zeros_like(db_ref)

        dw_ref[...] += dw
        db_ref[...] += db

    in_specs = _halo_specs(tb, cb, nt, xoff)
    args = [xbc] * 3
    for g in grads:
        in_specs += _halo_specs(tb, cb, nt)
        args += [g] * 3
    for s in scales:
        if s is not None:
            in_specs.append(pl.BlockSpec((1, cb), lambda j, i: (0, j)))
            args.append(s)
    in_specs += [pl.BlockSpec((8, cb), lambda j, i: (0, j)), pl.BlockSpec((1, cb), lambda j, i: (0, j))]
    args += [w8, b_row]
    return pl.pallas_call(
        body, name=name, grid=(c // cb, nt), in_specs=in_specs,
        out_specs=[pl.BlockSpec((tb, cb), lambda j, i: (i, j)), pl.BlockSpec((8, cb), lambda j, i: (0, j)),
                   pl.BlockSpec((1, cb), lambda j, i: (0, j))],
        out_shape=[jax.ShapeDtypeStruct((t, c), bf16), jax.ShapeDtypeStruct((8, c), f32), jax.ShapeDtypeStruct((1, c), f32)],
        compiler_params=_cparams(("parallel", "arbitrary")))(*args)


def dt_fwd(u_dt, bias_row, tb=1024):
    t = u_dt.shape[0]

    def body(u_ref, b_ref, o_ref):
        v = u_ref[...] + b_ref[...]
        sp = jnp.maximum(v, 0.0) + jnp.log(1.0 + jnp.exp(-jnp.abs(v)))
        lane = lax.broadcasted_iota(jnp.int32, v.shape, 1)
        o_ref[...] = jnp.where((lane & 127) < SSD_HEADS, sp, 0.0)

    return pl.pallas_call(
        body, name="dt_fwd", grid=(t // tb,),
        in_specs=[pl.BlockSpec((tb, 256), lambda i: (i, 0)), pl.BlockSpec((1, 256), lambda i: (0, 0))],
        out_specs=pl.BlockSpec((tb, 256), lambda i: (i, 0)), out_shape=jax.ShapeDtypeStruct((t, 256), f32),
        compiler_params=_cparams(("parallel",)))(u_dt, bias_row)


def dt_bwd(ddt_f, ddt_b, u_dt, bias_row, tb=1024):
    t = u_dt.shape[0]

    def body(gf_ref, gb_ref, u_ref, b_ref, du_ref, db_ref):
        g = jnp.concatenate([jnp.sum(gf_ref[...], axis=0), jnp.sum(gb_ref[...], axis=0)], axis=1)
        du = g * _sigmoid(u_ref[...] + b_ref[...])
        du_ref[...] = du.astype(du_ref.dtype)

        @pl.when(pl.program_id(0) == 0)
        def _():
            db_ref[...] = jnp.zeros_like(db_ref)

        db_ref[...] += jnp.sum(du, axis=0, keepdims=True)

    return pl.pallas_call(
        body, name="dt_bwd", grid=(t // tb,),
        in_specs=[pl.BlockSpec((4, tb, 128), lambda i: (0, i, 0)), pl.BlockSpec((4, tb, 128), lambda i: (0, i, 0)),
                  pl.BlockSpec((tb, 256), lambda i: (i, 0)), pl.BlockSpec((1, 256), lambda i: (0, 0))],
        out_specs=[pl.BlockSpec((tb, 256), lambda i: (i, 0)), pl.BlockSpec((1, 256), lambda i: (0, 0))],
        out_shape=[jax.ShapeDtypeStruct((t, 256), bf16), jax.ShapeDtypeStruct((1, 256), f32)],
        compiler_params=_cparams(("arbitrary",)))(ddt_f, ddt_b, u_dt, bias_row)


def _ssd_common(dt_blk, a_row, reverse):
    row = lax.broadcasted_iota(jnp.int32, (CHUNK, CHUNK), 0)
    col = lax.broadcasted_iota(jnp.int32, (CHUNK, CHUNK), 1)
    mask = (row <= col) if reverse else (row >= col)
    tri = mask.astype(f32)
    a = dt_blk * a_row
    acs = _dot_exact(tri, a)
    atot = jnp.sum(a, axis=0, keepdims=True)
    return mask, tri, a, acs, atot, col


def _lane_col(mat, lane_idx, h):
    return jnp.sum(jnp.where(lane_idx == h, mat, 0.0), axis=1, keepdims=True)


def ssd_fwd(xbc_c, dt2, a_rows, reverse, name):
    t = xbc_c.shape[0]
    nc = t // CHUNK
    d_off = 1 if reverse else 0

    def cidx(c):
        return nc - 1 - c if reverse else c

    def body(x_ref, b_ref, c_ref, dt_ref, a_ref, y_ref, hp_ref, h_scr, acst_scr):
        g = pl.program_id(0)
        c = pl.program_id(1)

        @pl.when(c == 0)
        def _():
            h_scr[...] = jnp.zeros_like(h_scr)

        dt_blk = dt_ref[...]
        mask, tri, a, acs, atot, lane = _ssd_common(dt_blk, a_ref[...], reverse)
        acst_scr[...] = acs.T
        bm = b_ref[...]
        cm = c_ref[...]
        cb = _dot_nt(cm, bm)
        half = lane >= SSD_HEAD_DIM
        sub_half = lax.broadcasted_iota(jnp.int32, (CHUNK, 1), 0) >= SSD_HEAD_DIM
        for j in range(4):
            x = x_ref[:, 128 * j:128 * (j + 1)]
            cols, dts, tots = [], [], []
            y = None
            for e in range(2):
                h = 8 * g + 2 * j + e
                col_h = _lane_col(acs, lane, h)
                row_h = acst_scr[pl.ds(h, 1), :]
                dt_h = _lane_col(dt_blk, lane, h)
                lmat = jnp.where(mask, jnp.exp(jnp.where(mask, col_h - row_h, 0.0)), 0.0)
                xdt_e = jnp.where(half == (e == 1), x * dt_h, 0.0)
                ye = _dot(cb * lmat, xdt_e)
                y = ye if y is None else y + ye
                cols.append(col_h)
                dts.append(dt_h)
                tots.append(jnp.sum(jnp.where(lane[0:1] == h, atot, 0.0), axis=1, keepdims=True))
            hp = h_scr[j]
            hp_ref[0, j] = hp
            ecol = jnp.where(half, jnp.exp(cols[1]), jnp.exp(cols[0]))
            y = y + _dot_nt(cm, hp) * ecol
            y_ref[:, 128 * j:128 * (j + 1)] = y
            dec = jnp.where(half, jnp.exp(tots[1] - cols[1]), jnp.exp(tots[0] - cols[0]))
            xdt = x * jnp.where(half, dts[1], dts[0])
            s_new = _dot_tn(xdt * dec, bm)
            cd = jnp.where(sub_half, jnp.exp(tots[1]), jnp.exp(tots[0]))
            h_scr[j] = cd * hp + s_new

    return pl.pallas_call(
        body, name=name, grid=(SSD_GROUPS, nc),
        in_specs=[pl.BlockSpec((CHUNK, 512), lambda g, c: (cidx(c), g)),
                  pl.BlockSpec((CHUNK, 128), lambda g, c: (cidx(c), 16 + g)),
                  pl.BlockSpec((CHUNK, 128), lambda g, c: (cidx(c), 20 + g)),
                  pl.BlockSpec((CHUNK, 128), lambda g, c: (cidx(c), d_off)),
                  pl.BlockSpec((1, 128), lambda g, c: (0, d_off))],
        out_specs=[pl.BlockSpec((CHUNK, 512), lambda g, c: (cidx(c), g)),
                   pl.BlockSpec((1, 4, 128, 128), lambda g, c: (cidx(c), g, 0, 0))],
        out_shape=[jax.ShapeDtypeStruct((t, D_INNER), f32), jax.ShapeDtypeStruct((nc, 16, 128, 128), f32)],
        scratch_shapes=[pltpu.VMEM((4, 128, 128), f32), pltpu.VMEM((CHUNK, CHUNK), f32)],
        compiler_params=_cparams(("parallel", "arbitrary")))(xbc_c, xbc_c, xbc_c, dt2, a_rows)


def ssd_bwd(xbc_c, dt2, a_rows, dy, hprev, reverse, name):
    t = xbc_c.shape[0]
    nc = t // CHUNK
    d_off = 1 if reverse else 0

    def cidx(c):
        return c if reverse else nc - 1 - c

    def body(x_ref, b_ref, c_ref, dt_ref, a_ref, dy_ref, hp_ref, dx_ref, db_ref, dc_ref, ddt_ref, da_ref,
             dh_scr, acst_scr):
        g = pl.program_id(0)
        c = pl.program_id(1)

        @pl.when(c == 0)
        def _():
            dh_scr[...] = jnp.zeros_like(dh_scr)
            da_ref[...] = jnp.zeros_like(da_ref)

        dt_blk = dt_ref[...]
        a_row = a_ref[...]
        mask, tri, a, acs, atot, lane = _ssd_common(dt_blk, a_row, reverse)
        acst_scr[...] = acs.T
        sub = lax.broadcasted_iota(jnp.int32, (CHUNK, CHUNK), 0)
        bm = b_ref[...]
        cm = c_ref[...]
        cb = _dot_nt(cm, bm)
        half = lane >= SSD_HEAD_DIM
        sub_half = sub[:, 0:1] >= SSD_HEAD_DIM
        dcb = jnp.zeros((CHUNK, CHUNK), f32)
        dacs = jnp.zeros((CHUNK, CHUNK), f32)
        dacs_t = jnp.zeros((CHUNK, CHUNK), f32)
        dtot = jnp.zeros((1, CHUNK), f32)
        ddt_x = jnp.zeros((CHUNK, CHUNK), f32)
        dbm = jnp.zeros((CHUNK, D_STATE), f32)
        dcm = jnp.zeros((CHUNK, D_STATE), f32)
        for j in range(4):
            x = x_ref[:, 128 * j:128 * (j + 1)]
            dyp = dy_ref[:, 128 * j:128 * (j + 1)]
            hp = hp_ref[0, j]
            dhn = dh_scr[j]
            cols, dts, tots, hs = [], [], [], []
            dxdt = None
            for e in range(2):
                h = 8 * g + 2 * j + e
                sel = half == (e == 1)
                col_h = _lane_col(acs, lane, h)
                row_h = acst_scr[pl.ds(h, 1), :]
                dt_h = _lane_col(dt_blk, lane, h)
                lmat = jnp.where(mask, jnp.exp(jnp.where(mask, col_h - row_h, 0.0)), 0.0)
                xdt_e = jnp.where(sel, x * dt_h, 0.0)
                dy_e = jnp.where(sel, dyp, 0.0)
                ml = _dot_nt(dy_e, xdt_e) * lmat
                dcb = dcb + ml
                w = ml * cb
                dacs = dacs + jnp.where(lane == h, jnp.sum(w, axis=1, keepdims=True), 0.0)
                dacs_t = dacs_t - jnp.where(sub == h, jnp.sum(w, axis=0, keepdims=True), 0.0)
                de = _dot_tn(cb * lmat, dy_e)
                dxdt = de if dxdt is None else dxdt + de
                cols.append(col_h)
                dts.append(dt_h)
                tots.append(jnp.sum(jnp.where(lane[0:1] == h, atot, 0.0), axis=1, keepdims=True))
                hs.append(h)
            ecol = jnp.where(half, jnp.exp(cols[1]), jnp.exp(cols[0]))
            dec = jnp.where(half, jnp.exp(tots[1] - cols[1]), jnp.exp(tots[0] - cols[0]))
            cd = jnp.where(sub_half, jnp.exp(tots[1]), jnp.exp(tots[0]))
            dtp = jnp.where(half, dts[1], dts[0])
            xdt = x * dtp
            yoff = _dot_nt(cm, hp) * ecol
            dye = dyp * ecol
            dcm = dcm + _dot(dye, hp)
            dhp = _dot_tn(dye, cm)
            gmat = _dot_nt(bm, dhn)
            dxdt = dxdt + dec * gmat
            dbm = dbm + _dot(xdt * dec, dhn)
            r_off = dyp * yoff
            r_dec = xdt * gmat * dec
            r_x = dxdt * x
            hh = dhn * hp
            for e in range(2):
                sel = half == (e == 1)
                h = hs[e]
                s_off = jnp.sum(jnp.where(sel, r_off, 0.0), axis=1, keepdims=True)
                s_dec = jnp.sum(jnp.where(sel, r_dec, 0.0), axis=1, keepdims=True)
                dacs = dacs + jnp.where(lane == h, s_off - s_dec, 0.0)
                dcd = jnp.sum(jnp.sum(jnp.where(sub_half == (e == 1), hh, 0.0), axis=1, keepdims=True), axis=0, keepdims=True)
                tot_e = jnp.sum(s_dec, axis=0, keepdims=True) + jnp.exp(tots[e]) * dcd
                dtot = dtot + jnp.where(lane[0:1] == h, tot_e, 0.0)
                ddt_x = ddt_x + jnp.where(lane == h, jnp.sum(jnp.where(sel, r_x, 0.0), axis=1, keepdims=True), 0.0)
            dx_ref[:, 128 * j:128 * (j + 1)] = dxdt * dtp
            dh_scr[j] = cd * dhn + dhp
        dcm = dcm + _dot(dcb, bm)
        dbm = dbm + _dot_tn(dcb, cm)
        db_ref[...] = dbm
        dc_ref[...] = dcm
        dacs = dacs + dacs_t.T
        da = _dot_exact(tri.T, dacs) + dtot
        ddt_ref[0] = da * a_row + ddt_x
        da_ref[0] += jnp.sum(da * dt_blk, axis=0, keepdims=True)

    return pl.pallas_call(
        body, name=name, grid=(SSD_GROUPS, nc),
        in_specs=[pl.BlockSpec((CHUNK, 512), lambda g, c: (cidx(c), g)),
                  pl.BlockSpec((CHUNK, 128), lambda g, c: (cidx(c), 16 + g)),
                  pl.BlockSpec((CHUNK, 128), lambda g, c: (cidx(c), 20 + g)),
                  pl.BlockSpec((CHUNK, 128), lambda g, c: (cidx(c), d_off)),
                  pl.BlockSpec((1, 128), lambda g, c: (0, d_off)),
                  pl.BlockSpec((CHUNK, 512), lambda g, c: (cidx(c), g)),
                  pl.BlockSpec((1, 4, 128, 128), lambda g, c: (cidx(c), g, 0, 0))],
        out_specs=[pl.BlockSpec((CHUNK, 512), lambda g, c: (cidx(c), g)),
                   pl.BlockSpec((CHUNK, 128), lambda g, c: (cidx(c), g)),
                   pl.BlockSpec((CHUNK, 128), lambda g, c: (cidx(c), g)),
                   pl.BlockSpec((1, CHUNK, 128), lambda g, c: (g, cidx(c), 0)),
                   pl.BlockSpec((1, 1, 128), lambda g, c: (g, 0, 0))],
        out_shape=[jax.ShapeDtypeStruct((t, D_INNER), f32), jax.ShapeDtypeStruct((t, 512), f32),
                   jax.ShapeDtypeStruct((t, 512), f32), jax.ShapeDtypeStruct((4, t, 128), f32),
                   jax.ShapeDtypeStruct((4, 1, 128), f32)],
        scratch_shapes=[pltpu.VMEM((4, 128, 128), f32), pltpu.VMEM((CHUNK, CHUNK), f32)],
        compiler_params=_cparams(("parallel", "arbitrary")))(xbc_c, xbc_c, xbc_c, dt2, a_rows, dy, hprev)


def tail_fwd(y_f, y_b, xbc_c, z, dskip_row, nw_row, tb=512):
    t = y_f.shape[0]

    def body(yf_ref, yb_ref, x_ref, z_ref, d_ref, w_ref, o_ref):
        zz = z_ref[...]
        y = (yf_ref[...] + yb_ref[...] + d_ref[...] * x_ref[...]) * (zz * _sigmoid(zz))
        rstd = lax.rsqrt(jnp.mean(y * y, axis=1, keepdims=True) + NORM_EPS)
        o_ref[...] = (y * rstd * w_ref[...]).astype(o_ref.dtype)

    blk = pl.BlockSpec((tb, 512), lambda i, g: (i, g))
    row = pl.BlockSpec((1, 512), lambda i, g: (0, g))
    return pl.pallas_call(
        body, name="tail_fwd", grid=(t // tb, SSD_GROUPS), in_specs=[blk, blk, blk, blk, row, row], out_specs=blk,
        out_shape=jax.ShapeDtypeStruct((t, D_INNER), bf16),
        compiler_params=_cparams(("parallel", "parallel")))(y_f, y_b, xbc_c, z, dskip_row, nw_row)


def tail_bwd(dyn, y_f, y_b, xbc_c, z, dskip_row, nw_row, tb=512):
    t = y_f.shape[0]

    def body(g_ref, yf_ref, yb_ref, x_ref, z_ref, d_ref, w_ref, dy_ref, dz_ref, dw_ref, dd_ref):
        zz = z_ref[...]
        sg = _sigmoid(zz)
        sl = zz * sg
        x = x_ref[...]
        y = yf_ref[...] + yb_ref[...] + d_ref[...] * x
        yz = y * sl
        rstd = lax.rsqrt(jnp.mean(yz * yz, axis=1, keepdims=True) + NORM_EPS)
        yhat = yz * rstd
        g = g_ref[...]
        dyhat = g * w_ref[...]
        dyz = rstd * (dyhat - yhat * jnp.mean(dyhat * yhat, axis=1, keepdims=True))
        dy = dyz * sl
        dy_ref[...] = dy
        dz_ref[...] = (dyz * y * sg * (1.0 + zz * (1.0 - sg))).astype(dz_ref.dtype)

        @pl.when(pl.program_id(1) == 0)
        def _():
            dw_ref[...] = jnp.zeros_like(dw_ref)
            dd_ref[...] = jnp.zeros_like(dd_ref)

        dw_ref[...] += jnp.sum(g * yhat, axis=0, keepdims=True)
        dd_ref[...] += jnp.sum(dy * x, axis=0, keepdims=True)

    blk = pl.BlockSpec((tb, 512), lambda g, i: (i, g))
    row = pl.BlockSpec((1, 512), lambda g, i: (0, g))
    return pl.pallas_call(
        body, name="tail_bwd", grid=(SSD_GROUPS, t // tb), in_specs=[blk, blk, blk, blk, blk, row, row],
        out_specs=[blk, blk, row, row],
        out_shape=[jax.ShapeDtypeStruct((t, D_INNER), f32), jax.ShapeDtypeStruct((t, D_INNER), bf16),
                   jax.ShapeDtypeStruct((1, D_INNER), f32), jax.ShapeDtypeStruct((1, D_INNER), f32)],
        compiler_params=_cparams(("parallel", "arbitrary")))(dyn, y_f, y_b, xbc_c, z, dskip_row, nw_row)


def _slopes(p):
    return [2.0 ** (-8.0 * (HEADS_PER_PATTERN * p + j + 1) / ATTN_HEADS) for j in range(HEADS_PER_PATTERN)]


def _win_specs(nq, col_of):
    return [pl.BlockSpec((64, 256), lambda r, i: (jnp.maximum(2 * i - 1, 0), col_of(r))),
            pl.BlockSpec((128, 256), lambda r, i: (i, col_of(r))),
            pl.BlockSpec((64, 256), lambda r, i: (jnp.minimum(2 * i + 2, 2 * nq - 1), col_of(r)))]


def _lane_head(shape):
    return lax.broadcasted_iota(jnp.int32, shape, 1) >> 6


def _stack_heads(m):
    lane_head = _lane_head(m.shape)
    return jnp.concatenate([jnp.where(lane_head == j, m, 0.0) for j in range(HEADS_PER_PATTERN)], axis=0)


def _unstack_heads(m4, n):
    lane_head = _lane_head((n, 256))
    out = jnp.where(lane_head == 0, m4[0:n], 0.0)
    for j in range(1, HEADS_PER_PATTERN):
        out = out + jnp.where(lane_head == j, m4[j * n:(j + 1) * n], 0.0)
    return out


def _head_cols(m, n):
    lane = lax.broadcasted_iota(jnp.int32, (n, 256), 1)
    return jnp.concatenate([jnp.sum(jnp.where(lane == ATTN_HEAD_DIM * j, m, 0.0), axis=1, keepdims=True)
                            for j in range(HEADS_PER_PATTERN)], axis=0)


def _score_bias(p, dil, by_key):
    slopes = np.asarray(_slopes(p), np.float32)
    if by_key:
        win = np.arange(256)[:, None]
        rel = np.arange(128)[None, :] - (win - 64)
    else:
        win = np.arange(256)[None, :]
        rel = win - 64 - np.arange(128)[:, None]
    band = np.abs(rel) <= 64
    out = []
    for first, last in ((False, False), (True, False), (False, True), (True, True)):
        ok = band & ~(first & (win < 64)) & ~(last & (win >= 192))
        pen = -slopes[:, None, None] * (np.abs(rel) * dil).astype(np.float32)[None]
        out.append(np.where(ok[None], pen, np.float32(NEG_BIG)).reshape(-1, rel.shape[1]))
    return jnp.asarray(np.stack(out), f32)


def _bias_spec(nq, rows, cols):
    return pl.BlockSpec((1, rows, cols), lambda r, i: ((i == 0).astype(jnp.int32) + 2 * (i == nq - 1).astype(jnp.int32), 0, 0))


def attn_fwd(q, k, v, p, dil, name):
    l = q.shape[0]
    nq = l // 128

    def body(q_ref, kp_ref, ko_ref, kn_ref, vp_ref, vo_ref, vn_ref, bias_ref, o_ref, lse_ref):
        kcat = jnp.concatenate([kp_ref[...], ko_ref[...], kn_ref[...]], axis=0)
        vcat = jnp.concatenate([vp_ref[...], vo_ref[...], vn_ref[...]], axis=0)
        s = _dot_nt(_stack_heads(q_ref[...] * 0.125), kcat) + bias_ref[0]
        m = jnp.max(s, axis=1, keepdims=True)
        pr = jnp.exp(s - m)
        den = jnp.sum(pr, axis=1, keepdims=True)
        o4 = _dot(pr, vcat) / den
        o_ref[...] = _unstack_heads(o4, 128)
        lse_ref[...] = _unstack_heads(jnp.broadcast_to(m + jnp.log(den), (512, 256)), 128)

    col = lambda r: r
    return pl.pallas_call(
        body, name=name, grid=(dil, nq),
        in_specs=[pl.BlockSpec((128, 256), lambda r, i: (i, r))] + _win_specs(nq, col) + _win_specs(nq, col)
        + [_bias_spec(nq, 512, 256)],
        out_specs=[pl.BlockSpec((128, 256), lambda r, i: (i, r))] * 2,
        out_shape=[jax.ShapeDtypeStruct(q.shape, f32)] * 2,
        compiler_params=_cparams(("parallel", "parallel")))(q, k, k, k, v, v, v, _score_bias(p, dil, False))


def attn_combine(os_, lses, tb=1024):
    t = os_[0].shape[0]

    def body(o0, o1, o2, l0, l1, l2, y_ref, lse_ref):
        a0, a1, a2 = l0[...], l1[...], l2[...]
        m = jnp.maximum(jnp.maximum(a0, a1), a2)
        e0, e1, e2 = jnp.exp(a0 - m), jnp.exp(a1 - m), jnp.exp(a2 - m)
        den = e0 + e1 + e2
        y_ref[...] = (e0 * o0[...] + e1 * o1[...] + e2 * o2[...]) / den
        lse_ref[...] = m + jnp.log(den)

    blk = pl.BlockSpec((tb, 256), lambda i: (i, 0))
    return pl.pallas_call(
        body, name="attn_combine", grid=(t // tb,), in_specs=[blk] * 6, out_specs=[blk, blk],
        out_shape=[jax.ShapeDtypeStruct((t, 256), f32)] * 2,
        compiler_params=_cparams(("parallel",)))(*os_, *lses)


def attn_delta(dy, y, tb=1024):
    t = dy.shape[0]

    def body(dy_ref, y_ref, d_ref):
        pr = dy_ref[...] * y_ref[...]
        lane_head = _lane_head(pr.shape)
        out = jnp.zeros_like(pr)
        for j in range(HEADS_PER_PATTERN):
            sj = jnp.sum(jnp.where(lane_head == j, pr, 0.0), axis=1, keepdims=True)
            out = out + jnp.where(lane_head == j, sj, 0.0)
        d_ref[...] = out

    blk = pl.BlockSpec((tb, 256), lambda i: (i, 0))
    return pl.pallas_call(body, name="attn_delta", grid=(t // tb,), in_specs=[blk, blk], out_specs=blk,
                          out_shape=jax.ShapeDtypeStruct((t, 256), f32),
                          compiler_params=_cparams(("parallel",)))(dy, y)


def attn_dq(q, k, v, dy, lse, delta, p, dil, name):
    l = q.shape[0]
    nq = l // 128

    def body(q_ref, kp_ref, ko_ref, kn_ref, vp_ref, vo_ref, vn_ref, dy_ref, lse_ref, d_ref, bias_ref, dq_ref):
        kcat = jnp.concatenate([kp_ref[...], ko_ref[...], kn_ref[...]], axis=0)
        vcat = jnp.concatenate([vp_ref[...], vo_ref[...], vn_ref[...]], axis=0)
        s = _dot_nt(_stack_heads(q_ref[...] * 0.125), kcat) + bias_ref[0]
        pr = jnp.exp(s - _head_cols(lse_ref[...], 128))
        dp = _dot_nt(_stack_heads(dy_ref[...]), vcat)
        ds = pr * (dp - _head_cols(d_ref[...], 128))
        dq_ref[...] = (_unstack_heads(_dot(ds, kcat), 128) * 0.125).astype(dq_ref.dtype)

    col = lambda r: r
    own = pl.BlockSpec((128, 256), lambda r, i: (i, r))
    return pl.pallas_call(
        body, name=name, grid=(dil, nq),
        in_specs=[own] + _win_specs(nq, col) + _win_specs(nq, col) + [own, own, own, _bias_spec(nq, 512, 256)],
        out_specs=own, out_shape=jax.ShapeDtypeStruct(q.shape, bf16),
        compiler_params=_cparams(("parallel", "parallel")))(q, k, k, k, v, v, v, dy, lse, delta, _score_bias(p, dil, False))


def attn_dkv(q, k, v, dy, lse, delta, p, dil, name):
    l = q.shape[0]
    nq = l // 128

    def body(qp_ref, qo_ref, qn_ref, gp_ref, go_ref, gn_ref, lp_ref, lo_ref, ln_ref, dp_ref, do_ref, dn_ref,
             k_ref, v_ref, bias_ref, dk_ref, dv_ref):
        cat = lambda a, b, c: jnp.concatenate([a[...], b[...], c[...]], axis=0)
        q4 = _stack_heads(cat(qp_ref, qo_ref, qn_ref) * 0.125)
        dy4 = _stack_heads(cat(gp_ref, go_ref, gn_ref))
        lse4 = _head_cols(cat(lp_ref, lo_ref, ln_ref), 256)
        del4 = _head_cols(cat(dp_ref, do_ref, dn_ref), 256)
        s = _dot_nt(q4, k_ref[...]) + bias_ref[0]
        pr = jnp.exp(s - lse4)
        dpm = _dot_nt(dy4, v_ref[...])
        ds = pr * (dpm - del4)
        dv_ref[...] = _dot_tn(pr, dy4).astype(dv_ref.dtype)
        dk_ref[...] = _dot_tn(ds, q4).astype(dk_ref.dtype)

    col = lambda r: r
    own = pl.BlockSpec((128, 256), lambda r, i: (i, r))
    win = _win_specs(nq, col)
    return pl.pallas_call(
        body, name=name, grid=(dil, nq), in_specs=win * 4 + [own, own, _bias_spec(nq, 1024, 128)], out_specs=[own, own],
        out_shape=[jax.ShapeDtypeStruct(q.shape, bf16)] * 2,
        compiler_params=_cparams(("parallel", "parallel")))(q, q, q, dy, dy, dy, lse, lse, lse, delta, delta, delta, k, v,
                                                            _score_bias(p, dil, True))


def _lanes(v, reps):
    return v if reps == 1 else jnp.tile(v, (1, reps))


def _lane_halo_specs(cb, tb, nt, off=0):
    r = tb // 128
    return [pl.BlockSpec((cb, 128), lambda j, i: (j + off, jnp.maximum(i * r - 1, 0))),
            pl.BlockSpec((cb, tb), lambda j, i: (j + off, i)),
            pl.BlockSpec((cb, 128), lambda j, i: (j + off, jnp.minimum((i + 1) * r, nt * r - 1)))]


def _with_lane_halo(prev_ref, own_ref, next_ref, i, nt):
    prev = jnp.where(i > 0, prev_ref[...].astype(f32), 0.0)
    nxt = jnp.where(i < nt - 1, next_ref[...].astype(f32), 0.0)
    return jnp.concatenate([prev, own_ref[...].astype(f32), nxt], axis=1)


def _lane_shifted(xcat, s, tb):
    n = xcat.shape[1]
    return pltpu.roll(xcat, (-s) % n, 1)[:, 128:128 + tb]


def conv_fwd_t(xbc_t, w_b, b_b, tb=1024, cb=256):
    c, t = xbc_t.shape
    nt = t // tb

    def body(prev_ref, own_ref, next_ref, w_ref, b_ref, o_ref, ds_ref):
        i = pl.program_id(1)
        xcat = _with_lane_halo(prev_ref, own_ref, next_ref, i, nt)
        reps = tb // 128
        pre = _lanes(b_ref[...], reps)
        for k in range(D_CONV):
            pre = pre + _lanes(w_ref[k], reps) * _lane_shifted(xcat, k - 2, tb)
        sg = _sigmoid(pre)
        o_ref[...] = pre * sg
        ds_ref[...] = sg * (1.0 + pre * (1.0 - sg))

    blk = pl.BlockSpec((cb, tb), lambda j, i: (j, i))
    return pl.pallas_call(
        body, name="conv_fwd", grid=(c // cb, nt),
        in_specs=_lane_halo_specs(cb, tb, nt) + [pl.BlockSpec((D_CONV, cb, 128), lambda j, i: (0, j, 0)),
                                                 pl.BlockSpec((cb, 128), lambda j, i: (j, 0))],
        out_specs=[blk, blk], out_shape=[jax.ShapeDtypeStruct((c, t), f32)] * 2,
        compiler_params=_cparams(("parallel", "parallel")))(xbc_t, xbc_t, xbc_t, w_b, b_b)


def conv_bwd_t(xbc_t, dsilu_t, grad_t, w_b, into, name, row0, tb=1024, cb=256):
    c, t = grad_t.shape
    nt = t // tb
    off = row0 // cb
    off_out = (D_INNER + row0) // cb
    reps = tb // 128

    def body(*refs):
        i = pl.program_id(1)
        x_ref, sr, gr = refs[0], refs[1:4], refs[4:7]
        w_ref = refs[7]
        dx_ref, dw_ref, db_ref = refs[-3:]
        wk = [_lanes(w_ref[k], reps) for k in range(D_CONV)]
        dpre = _with_lane_halo(*gr, i, nt) * _with_lane_halo(*sr, i, nt)

        def fold(v):
            s = v[:, 0:128]
            for q in range(1, reps):
                s = s + v[:, 128 * q:128 * (q + 1)]
            return s

        @pl.when(i == 0)
        def _():
            dw_ref[...] = jnp.zeros_like(dw_ref)
            db_ref[...] = jnp.zeros_like(db_ref)

        x_own = x_ref[...]
        dx = None
        for k in range(D_CONV):
            shifted = _lane_shifted(dpre, 2 - k, tb)
            term = wk[k] * shifted
            dx = term if dx is None else dx + term
            dw_ref[k] += fold(shifted * x_own)
        dx_ref[...] = dx.astype(dx_ref.dtype)
        db_ref[...] += fold(dpre[:, 128:128 + tb])

    in_specs = ([pl.BlockSpec((cb, tb), lambda j, i: (j + off, i))] + _lane_halo_specs(cb, tb, nt, off)
                + _lane_halo_specs(cb, tb, nt)
                + [pl.BlockSpec((D_CONV, cb, 128), lambda j, i: (0, j + off, 0)), pl.BlockSpec(memory_space=pl.ANY)])
    args = [xbc_t] + [dsilu_t] * 3 + [grad_t] * 3 + [w_b, into]
    return pl.pallas_call(
        body, name=name, grid=(c // cb, nt), in_specs=in_specs,
        out_specs=[pl.BlockSpec((cb, tb), lambda j, i: (j + off_out, i)),
                   pl.BlockSpec((D_CONV, cb, 128), lambda j, i: (0, j, 0)), pl.BlockSpec((cb, 128), lambda j, i: (j, 0))],
        out_shape=[jax.ShapeDtypeStruct(into.shape, into.dtype), jax.ShapeDtypeStruct((D_CONV, c, 128), f32),
                   jax.ShapeDtypeStruct((c, 128), f32)],
        input_output_aliases={8: 0}, compiler_params=_cparams(("parallel", "arbitrary")))(*args)


def dt_fwd_t(u_dt_t, bias_b, tb=2048):
    r, t = u_dt_t.shape

    def body(u_ref, b_ref, o_ref):
        v = u_ref[...] + _lanes(b_ref[...], tb // 128)
        o_ref[...] = jnp.maximum(v, 0.0) + jnp.log(1.0 + jnp.exp(-jnp.abs(v)))

    return pl.pallas_call(
        body, name="dt_fwd", grid=(t // tb,),
        in_specs=[pl.BlockSpec((r, tb), lambda i: (0, i)), pl.BlockSpec((r, 128), lambda i: (0, 0))],
        out_specs=pl.BlockSpec((r, tb), lambda i: (0, i)), out_shape=jax.ShapeDtypeStruct((r, t), f32),
        compiler_params=_cparams(("parallel",)))(u_dt_t, bias_b)


def dt_bwd_t(ddt_f, ddt_b, u_dt_t, bias_b, into, tb=2048):
    r, t = u_dt_t.shape
    reps = tb // 128
    row_blk = (SSD_COLS - r) // r

    def body(gf_ref, gb_ref, u_ref, b_ref, into_ref, du_ref, db_ref):
        g = jnp.concatenate([gf_ref[...], gb_ref[...]], axis=0)
        du = g * _sigmoid(u_ref[...] + _lanes(b_ref[...], reps))
        du_ref[...] = du.astype(du_ref.dtype)

        @pl.when(pl.program_id(0) == 0)
        def _():
            db_ref[...] = jnp.zeros_like(db_ref)

        s = du[:, 0:128]
        for q in range(1, reps):
            s = s + du[:, 128 * q:128 * (q + 1)]
        db_ref[...] += s

    half = pl.BlockSpec((r // 2, tb), lambda i: (0, i))
    return pl.pallas_call(
        body, name="dt_bwd", grid=(t // tb,),
        in_specs=[half, half, pl.BlockSpec((r, tb), lambda i: (0, i)), pl.BlockSpec((r, 128), lambda i: (0, 0)),
                  pl.BlockSpec(memory_space=pl.ANY)],
        out_specs=[pl.BlockSpec((r, tb), lambda i: (row_blk, i)), pl.BlockSpec((r, 128), lambda i: (0, 0))],
        out_shape=[jax.ShapeDtypeStruct(into.shape, into.dtype), jax.ShapeDtypeStruct((r, 128), f32)],
        input_output_aliases={4: 0}, compiler_params=_cparams(("arbitrary",)))(ddt_f, ddt_b, u_dt_t, bias_b, into)


HEADS_PER_GROUP = SSD_HEADS // SSD_GROUPS


def _group_rows(g, n):
    return pl.ds(pl.multiple_of(g * n, n), n)


def _ssd_decays(dt_blk, a_blk, reverse):
    row = lax.broadcasted_iota(jnp.int32, (CHUNK, CHUNK), 0)
    col = lax.broadcasted_iota(jnp.int32, (CHUNK, CHUNK), 1)
    mask = (row <= col) if reverse else (row >= col)
    tri = mask.astype(f32)
    a8 = dt_blk * a_blk
    a = jnp.concatenate([a8, jnp.zeros((CHUNK - HEADS_PER_GROUP, CHUNK), f32)], axis=0).T
    acs = _dot_exact(tri, a)
    return mask, tri, a8, acs, acs.T, col


def ssd_fwd_t(xbc_ct, dt_t, a_b, reverse, name, prev=None, tail=None):
    t = xbc_ct.shape[1]
    nc = t // CHUNK
    direction = 1 if reverse else 0

    def cidx(c):
        return nc - 1 - c if reverse else c

    def body(*refs):
        x_ref, b_ref, c_ref, dt_ref, a_ref = refs[0:5]
        pos = 5
        prev_ref = None
        if prev is not None:
            prev_ref = refs[pos]
            pos += 1
        if tail is not None:
            z_ref, skip_ref, nw_ref = refs[pos:pos + 3]
            pos += 3
            y_ref, hp_ref, yn_ref, h_scr = refs[pos:pos + 4]
        else:
            y_ref, hp_ref, h_scr = refs[pos:pos + 3]

        @pl.when(pl.program_id(0) == 0)
        def _():
            h_scr[...] = jnp.zeros_like(h_scr)

        def group(g, carry):
            x_v, y_v = x_ref.at[_group_rows(g, 512)], y_ref.at[_group_rows(g, 512)]
            heads = _group_rows(g, HEADS_PER_GROUP)
            hp_v, h_v = hp_ref.at[0, heads], h_scr.at[heads]
            dt_blk = dt_ref[heads, :]
            mask, tri, a8, acs, acs_t, lane = _ssd_decays(dt_blk, a_ref[heads, :], reverse)
            bm = b_ref[_group_rows(g, 128), :].T
            cm = c_ref[_group_rows(g, 128), :].T
            cb = _dot_nt(cm, bm)
            tot = jnp.sum(a8, axis=1, keepdims=True)
            for j in range(HEADS_PER_GROUP):
                rows = slice(SSD_HEAD_DIM * j, SSD_HEAD_DIM * (j + 1))
                col_j = _lane_col(acs, lane, j)
                row_j = acs_t[j:j + 1, :]
                lmat = jnp.where(mask, jnp.exp(jnp.where(mask, col_j - row_j, 0.0)), 0.0)
                xdt = x_v[rows, :] * dt_blk[j:j + 1, :]
                hp = h_v[j]
                hp_v[j] = hp
                y = _dot_nt(xdt, cb * lmat) + _dot_nt(hp, cm) * jnp.exp(row_j)
                if prev_ref is not None:
                    y = y + prev_ref.at[_group_rows(g, 512)][rows, :]
                y_v[rows, :] = y
                tot_j = tot[j:j + 1, :]
                h_v[j] = jnp.exp(tot_j) * hp + _dot(xdt * jnp.exp(tot_j - row_j), bm)
            if tail is not None:
                rows = _group_rows(g, 512)
                zz = z_ref[rows, :]
                yg = (y_v[...] + skip_ref[rows, :] * x_v[...]) * (zz * _sigmoid(zz))
                rstd = lax.rsqrt(jnp.mean(yg * yg, axis=0, keepdims=True) + NORM_EPS)
                yn_ref[rows, :] = (yg * rstd * nw_ref[rows, :]).astype(yn_ref.dtype)
            return carry

        lax.fori_loop(0, SSD_GROUPS, group, 0)

    big = pl.BlockSpec((D_INNER, CHUNK), lambda c: (0, cidx(c)))
    par = pl.BlockSpec((D_INNER, 128), lambda c: (0, 0))
    in_specs = [big, pl.BlockSpec((512, CHUNK), lambda c: (4, cidx(c))), pl.BlockSpec((512, CHUNK), lambda c: (5, cidx(c))),
                pl.BlockSpec((SSD_HEADS, CHUNK), lambda c: (direction, cidx(c))),
                pl.BlockSpec((SSD_HEADS, 128), lambda c: (direction, 0))]
    args = [xbc_ct, xbc_ct, xbc_ct, dt_t, a_b]
    out_specs = [big, pl.BlockSpec((1, SSD_HEADS, SSD_HEAD_DIM, D_STATE), lambda c: (cidx(c), 0, 0, 0))]
    out_shape = [jax.ShapeDtypeStruct((D_INNER, t), f32), jax.ShapeDtypeStruct((nc, SSD_HEADS, SSD_HEAD_DIM, D_STATE), f32)]
    if prev is not None:
        in_specs.append(big)
        args.append(prev)
    if tail is not None:
        in_specs += [big, par, par]
        args += list(tail)
        out_specs.append(big)
        out_shape.append(jax.ShapeDtypeStruct((D_INNER, t), bf16))
    return pl.pallas_call(
        body, name=name, grid=(nc,), in_specs=in_specs, out_specs=out_specs, out_shape=out_shape,
        scratch_shapes=[pltpu.VMEM((SSD_HEADS, SSD_HEAD_DIM, D_STATE), f32)],
        compiler_params=_cparams(("arbitrary",)))(*args)


def ssd_bwd_t(xbc_ct, dt_t, a_b, dy_t, hprev, reverse, name, skip_b=None, prev=None, tail=None):
    t = xbc_ct.shape[1]
    nc = t // CHUNK
    direction = 1 if reverse else 0

    def cidx(c):
        return c if reverse else nc - 1 - c

    def body(*refs):
        x_ref, b_ref, c_ref, dt_ref, a_ref, dy_ref, hp_ref = refs[0:7]
        pos = 7
        skip_ref = None
        if skip_b is not None:
            skip_ref = refs[pos]
            pos += 1
        prev_refs = None
        if prev is not None:
            prev_refs = refs[pos:pos + 3]
            pos += 3
        if tail is not None:
            ys_ref, z_ref, nw_ref = refs[pos:pos + 3]
            pos += 3
        dx_ref, db_ref, dc_ref, ddt_ref, da_ref = refs[pos:pos + 5]
        pos += 5
        if tail is not None:
            dyout_ref, dz_ref, dnw_ref, ddx_ref = refs[pos:pos + 4]
            pos += 4
        dh_scr = refs[pos]

        @pl.when(pl.program_id(0) == 0)
        def _():
            dh_scr[...] = jnp.zeros_like(dh_scr)
            da_ref[...] = jnp.zeros_like(da_ref)
            if tail is not None:
                dnw_ref[...] = jnp.zeros_like(dnw_ref)
                ddx_ref[...] = jnp.zeros_like(ddx_ref)

        def group(g, carry):
            big, st, heads = _group_rows(g, 512), _group_rows(g, 128), _group_rows(g, HEADS_PER_GROUP)
            x_v, dy_v, dx_v = x_ref.at[big], dy_ref.at[big], dx_ref.at[big]
            hp_v, dh_v = hp_ref.at[0, heads], dh_scr.at[heads]
            dy_grp = None
            if tail is not None:
                zz = z_ref[big, :]
                sg = _sigmoid(zz)
                sl = zz * sg
                x_all = x_v[...]
                y = ys_ref[big, :] + skip_ref[big, :] * x_all
                yz = y * sl
                rstd = lax.rsqrt(jnp.mean(yz * yz, axis=0, keepdims=True) + NORM_EPS)
                yhat = yz * rstd
                gy = dy_v[...]
                dyhat = gy * nw_ref[big, :]
                dyz = rstd * (dyhat - yhat * jnp.mean(dyhat * yhat, axis=0, keepdims=True))
                dy_grp = dyz * sl
                dyout_ref[big, :] = dy_grp
                dz_ref[big, :] = (dyz * y * sg * (1.0 + zz * (1.0 - sg))).astype(dz_ref.dtype)
                dnw_ref[big, :] += gy * yhat
                ddx_ref[big, :] += dy_grp * x_all
            dt_blk = dt_ref[heads, :]
            a_blk = a_ref[heads, :]
            mask, tri, a8, acs, acs_t, lane = _ssd_decays(dt_blk, a_blk, reverse)
            sub = lax.broadcasted_iota(jnp.int32, (CHUNK, CHUNK), 0)
            mask_t = (sub >= lane) if reverse else (sub <= lane)
            bm = b_ref[st, :].T
            cm = c_ref[st, :].T
            cb = _dot_nt(cm, bm)
            cb_t = _dot_nt(bm, cm)
            tot = jnp.sum(a8, axis=1, keepdims=True)
            dcb = jnp.zeros((CHUNK, CHUNK), f32)
            dbm = jnp.zeros((CHUNK, D_STATE), f32)
            dcm = jnp.zeros((CHUNK, D_STATE), f32)
            dacs_rows, ddtx_rows = [], []
            for j in range(HEADS_PER_GROUP):
                rows = slice(SSD_HEAD_DIM * j, SSD_HEAD_DIM * (j + 1))
                col_j = _lane_col(acs, lane, j)
                row_j = acs_t[j:j + 1, :]
                dt_j = dt_blk[j:j + 1, :]
                tot_j = tot[j:j + 1, :]
                lmat = jnp.where(mask, jnp.exp(jnp.where(mask, col_j - row_j, 0.0)), 0.0)
                lmat_t = jnp.where(mask_t, jnp.exp(jnp.where(mask_t, row_j - col_j, 0.0)), 0.0)
                x = x_v[rows, :]
                xdt = x * dt_j
                dyh = dy_v[rows, :] if dy_grp is None else dy_grp[rows]
                hp = hp_v[j]
                dhn = dh_v[j]
                ml = _dot_tn(dyh, xdt) * lmat
                w_t = _dot_tn(xdt, dyh) * lmat_t * cb_t
                dcb = dcb + ml
                dacs = jnp.sum(w_t, axis=0, keepdims=True) - jnp.sum(ml * cb, axis=0, keepdims=True)
                ecol = jnp.exp(row_j)
                dec = jnp.exp(tot_j - row_j)
                dye = dyh * ecol
                yoff = _dot_nt(hp, cm) * ecol
                gmat = _dot_nt(dhn, bm)
                dxdt = _dot(dyh, cb * lmat) + dec * gmat
                s_dec = jnp.sum(xdt * gmat, axis=0, keepdims=True) * dec
                dacs = dacs + jnp.sum(dyh * yoff, axis=0, keepdims=True) - s_dec
                dcd = jnp.sum(jnp.sum(dhn * hp, axis=1, keepdims=True), axis=0, keepdims=True)
                dtot = jnp.sum(s_dec, axis=1, keepdims=True) + jnp.exp(tot_j) * dcd
                dacs_rows.append((dacs, dtot))
                ddtx_rows.append(jnp.sum(dxdt * x, axis=0, keepdims=True))
                dcm = dcm + _dot_tn(dye, hp)
                dbm = dbm + _dot_tn(xdt * dec, dhn)
                dxh = dxdt * dt_j
                if skip_ref is not None:
                    dxh = dxh + skip_ref.at[big][rows, :] * dyh
                if prev_refs is not None:
                    dxh = dxh + prev_refs[0].at[big][rows, :]
                dx_v[rows, :] = dxh
                dh_v[j] = jnp.exp(tot_j) * dhn + _dot(dye, cm)
            dcm = dcm + _dot(dcb, bm)
            dbm = dbm + _dot_tn(dcb, cm)
            dbt, dct = dbm.T, dcm.T
            if prev_refs is not None:
                dbt = dbt + prev_refs[1][st, :]
                dct = dct + prev_refs[2][st, :]
            db_ref[st, :] = dbt
            dc_ref[st, :] = dct
            dacs8 = jnp.concatenate([d for d, _ in dacs_rows], axis=0)
            dtot8 = jnp.concatenate([d for _, d in dacs_rows], axis=0)
            da8 = _dot_exact(dacs8, tri) + dtot8
            ddt_ref[heads, :] = da8 * a_blk + jnp.concatenate(ddtx_rows, axis=0)
            da_ref[heads, :] += da8 * dt_blk
            return carry

        lax.fori_loop(0, SSD_GROUPS, group, 0)

    big = pl.BlockSpec((D_INNER, CHUNK), lambda c: (0, cidx(c)))
    st = pl.BlockSpec((512, CHUNK), lambda c: (0, cidx(c)))
    in_specs = [big, pl.BlockSpec((512, CHUNK), lambda c: (4, cidx(c))), pl.BlockSpec((512, CHUNK), lambda c: (5, cidx(c))),
                pl.BlockSpec((SSD_HEADS, CHUNK), lambda c: (direction, cidx(c))),
                pl.BlockSpec((SSD_HEADS, 128), lambda c: (direction, 0)), big,
                pl.BlockSpec((1, SSD_HEADS, SSD_HEAD_DIM, D_STATE), lambda c: (cidx(c), 0, 0, 0))]
    args = [xbc_ct, xbc_ct, xbc_ct, dt_t, a_b, dy_t, hprev]
    if skip_b is not None:
        in_specs.append(pl.BlockSpec((D_INNER, 128), lambda c: (0, 0)))
        args.append(skip_b)
    if prev is not None:
        in_specs += [big, st, st]
        args += list(prev)
    par = pl.BlockSpec((D_INNER, 128), lambda c: (0, 0))
    out_specs = [big, st, st, pl.BlockSpec((SSD_HEADS, CHUNK), lambda c: (0, cidx(c))),
                 pl.BlockSpec((SSD_HEADS, 128), lambda c: (0, 0))]
    out_shape = [jax.ShapeDtypeStruct((D_INNER, t), f32), jax.ShapeDtypeStruct((512, t), f32),
                 jax.ShapeDtypeStruct((512, t), f32), jax.ShapeDtypeStruct((SSD_HEADS, t), f32),
                 jax.ShapeDtypeStruct((SSD_HEADS, 128), f32)]
    if tail is not None:
        in_specs += [big, big, par]
        args += list(tail)
        out_specs += [big, big, par, par]
        out_shape += [jax.ShapeDtypeStruct((D_INNER, t), f32), jax.ShapeDtypeStruct((SSD_COLS, t), bf16),
                      jax.ShapeDtypeStruct((D_INNER, 128), f32), jax.ShapeDtypeStruct((D_INNER, 128), f32)]
    return pl.pallas_call(
        body, name=name, grid=(nc,), in_specs=in_specs, out_specs=out_specs, out_shape=out_shape,
        scratch_shapes=[pltpu.VMEM((SSD_HEADS, SSD_HEAD_DIM, D_STATE), f32)],
        compiler_params=_cparams(("arbitrary",)))(*args)


def tail_fwd_t(y_scan, xbc_ct, z_t, skip_b, nw_b, tb=512):
    t = y_scan.shape[1]
    reps = tb // 128

    def body(ys_ref, x_ref, z_ref, d_ref, w_ref, o_ref):
        zz = z_ref[...]
        y = (ys_ref[...] + _lanes(d_ref[...], reps) * x_ref[...]) * (zz * _sigmoid(zz))
        rstd = lax.rsqrt(jnp.mean(y * y, axis=0, keepdims=True) + NORM_EPS)
        o_ref[...] = (y * rstd * _lanes(w_ref[...], reps)).astype(o_ref.dtype)

    blk = pl.BlockSpec((512, tb), lambda g, i: (g, i))
    par = pl.BlockSpec((512, 128), lambda g, i: (g, 0))
    return pl.pallas_call(
        body, name="tail_fwd", grid=(SSD_GROUPS, t // tb), in_specs=[blk, blk, blk, par, par], out_specs=blk,
        out_shape=jax.ShapeDtypeStruct((D_INNER, t), bf16),
        compiler_params=_cparams(("parallel", "parallel")))(y_scan, xbc_ct, z_t, skip_b, nw_b)


def tail_bwd_t(dyn_t, y_scan, xbc_ct, z_t, skip_b, nw_b, tb=512):
    t = y_scan.shape[1]
    reps = tb // 128

    def body(g_ref, ys_ref, x_ref, z_ref, d_ref, w_ref, dy_ref, dz_ref, dw_ref, dd_ref):
        zz = z_ref[...]
        sg = _sigmoid(zz)
        sl = zz * sg
        x = x_ref[...]
        y = ys_ref[...] + _lanes(d_ref[...], reps) * x
        yz = y * sl
        rstd = lax.rsqrt(jnp.mean(yz * yz, axis=0, keepdims=True) + NORM_EPS)
        yhat = yz * rstd
        g = g_ref[...]
        dyhat = g * _lanes(w_ref[...], reps)
        dyz = rstd * (dyhat - yhat * jnp.mean(dyhat * yhat, axis=0, keepdims=True))
        dy = dyz * sl
        dy_ref[...] = dy
        dz_ref[...] = (dyz * y * sg * (1.0 + zz * (1.0 - sg))).astype(dz_ref.dtype)

        def fold(v):
            s = v[:, 0:128]
            for q in range(1, reps):
                s = s + v[:, 128 * q:128 * (q + 1)]
            return s

        @pl.when(pl.program_id(1) == 0)
        def _():
            dw_ref[...] = jnp.zeros_like(dw_ref)
            dd_ref[...] = jnp.zeros_like(dd_ref)

        dw_ref[...] += fold(g * yhat)
        dd_ref[...] += fold(dy * x)

    blk = pl.BlockSpec((512, tb), lambda g, i: (g, i))
    par = pl.BlockSpec((512, 128), lambda g, i: (g, 0))
    return pl.pallas_call(
        body, name="tail_bwd", grid=(SSD_GROUPS, t // tb), in_specs=[blk, blk, blk, blk, par, par],
        out_specs=[blk, blk, par, par],
        out_shape=[jax.ShapeDtypeStruct((D_INNER, t), f32), jax.ShapeDtypeStruct((SSD_COLS, t), bf16),
                   jax.ShapeDtypeStruct((D_INNER, 128), f32), jax.ShapeDtypeStruct((D_INNER, 128), f32)],
        compiler_params=_cparams(("parallel", "arbitrary")))(dyn_t, y_scan, xbc_ct, z_t, skip_b, nw_b)


def merge_fwd(u_gate, bg_row, y_ssd, y_att, tb=512):
    t = y_ssd.shape[0]

    def body(ga_ref, gb_ref, ba_ref, bb_ref, ys_ref, ya_ref, o_ref):
        o_ref[...] = (_sigmoid(ga_ref[...] + ba_ref[...]) * ys_ref[...]
                      + _sigmoid(gb_ref[...] + bb_ref[...]) * ya_ref[...]).astype(o_ref.dtype)

    blk = pl.BlockSpec((tb, 512), lambda i, j: (i, j))
    blk2 = pl.BlockSpec((tb, 512), lambda i, j: (i, 2 + j))
    row = pl.BlockSpec((1, 512), lambda i, j: (0, j))
    row2 = pl.BlockSpec((1, 512), lambda i, j: (0, 2 + j))
    return pl.pallas_call(
        body, name="merge_fwd", grid=(t // tb, 2), in_specs=[blk, blk2, row, row2, blk, blk], out_specs=blk,
        out_shape=jax.ShapeDtypeStruct((t, D_MODEL), bf16),
        compiler_params=_cparams(("parallel", "parallel")))(u_gate, u_gate, bg_row, bg_row, y_ssd, y_att)


def merge_bwd(dm, u_gate, bg_row, y_ssd, y_att, tb=512):
    t = dm.shape[0]

    def body(dm_ref, ga_ref, gb_ref, ba_ref, bb_ref, ys_ref, ya_ref, dys_ref, dya_ref, dga_ref, dgb_ref, dba_ref, dbb_ref):
        d = dm_ref[...]
        sa = _sigmoid(ga_ref[...] + ba_ref[...])
        sb = _sigmoid(gb_ref[...] + bb_ref[...])
        dys_ref[...] = (d * sa).astype(dys_ref.dtype)
        dya_ref[...] = (d * sb).astype(dya_ref.dtype)
        dla = d * ys_ref[...] * sa * (1.0 - sa)
        dlb = d * ya_ref[...] * sb * (1.0 - sb)
        dga_ref[...] = dla.astype(dga_ref.dtype)
        dgb_ref[...] = dlb.astype(dgb_ref.dtype)

        @pl.when(pl.program_id(1) == 0)
        def _():
            dba_ref[...] = jnp.zeros_like(dba_ref)
            dbb_ref[...] = jnp.zeros_like(dbb_ref)

        dba_ref[...] += jnp.sum(dla, axis=0, keepdims=True)
        dbb_ref[...] += jnp.sum(dlb, axis=0, keepdims=True)

    blk = pl.BlockSpec((tb, 512), lambda j, i: (i, j))
    blk2 = pl.BlockSpec((tb, 512), lambda j, i: (i, 2 + j))
    row = pl.BlockSpec((1, 512), lambda j, i: (0, j))
    row2 = pl.BlockSpec((1, 512), lambda j, i: (0, 2 + j))
    act = jax.ShapeDtypeStruct((t, D_MODEL), bf16)
    vec = jax.ShapeDtypeStruct((1, D_MODEL), f32)
    return pl.pallas_call(
        body, name="merge_bwd", grid=(2, t // tb), in_specs=[blk, blk, blk2, row, row2, blk, blk],
        out_specs=[blk, blk, blk, blk, row, row], out_shape=[act, act, act, act, vec, vec],
        compiler_params=_cparams(("parallel", "arbitrary")))(dm, u_gate, u_gate, bg_row, bg_row, y_ssd, y_att)


def _ln_stats(r):
    mu = jnp.mean(r, axis=1, keepdims=True)
    xc = r - mu
    rstd = lax.rsqrt(jnp.mean(xc * xc, axis=1, keepdims=True) + NORM_EPS)
    return xc * rstd, rstd


def _ln_bwd(dy, xhat, rstd, g_row):
    dxh = dy * g_row
    return rstd * (dxh - jnp.mean(dxh, axis=1, keepdims=True) - xhat * jnp.mean(dxh * xhat, axis=1, keepdims=True))


def ln1_fwd(x, mix, g_row, b_row, tb=512):
    t = x.shape[0]

    def body(x_ref, m_ref, g_ref, b_ref, o_ref, ob_ref):
        xhat, _ = _ln_stats(ALPHA * x_ref[...] + m_ref[...])
        h = xhat * g_ref[...] + b_ref[...]
        o_ref[...] = h
        ob_ref[...] = h.astype(ob_ref.dtype)

    blk = pl.BlockSpec((tb, D_MODEL), lambda i: (i, 0))
    row = pl.BlockSpec((1, D_MODEL), lambda i: (0, 0))
    return pl.pallas_call(body, name="ln1_fwd", grid=(t // tb,), in_specs=[blk, blk, row, row], out_specs=[blk, blk],
                          out_shape=[jax.ShapeDtypeStruct((t, D_MODEL), f32), jax.ShapeDtypeStruct((t, D_MODEL), bf16)],
                          compiler_params=_cparams(("parallel",)))(x, mix, g_row, b_row)


def ln1_bwd(dh, x, mix, g_row, tb=512):
    t = x.shape[0]

    def body(dh_ref, x_ref, m_ref, g_ref, dr_ref, drb_ref, dg_ref, db_ref):
        xhat, rstd = _ln_stats(ALPHA * x_ref[...] + m_ref[...])
        dy = dh_ref[...]
        dr = _ln_bwd(dy, xhat, rstd, g_ref[...])
        dr_ref[...] = dr
        drb_ref[...] = dr.astype(drb_ref.dtype)

        @pl.when(pl.program_id(0) == 0)
        def _():
            dg_ref[...] = jnp.zeros_like(dg_ref)
            db_ref[...] = jnp.zeros_like(db_ref)

        dg_ref[...] += jnp.sum(dy * xhat, axis=0, keepdims=True)
        db_ref[...] += jnp.sum(dy, axis=0, keepdims=True)

    blk = pl.BlockSpec((tb, D_MODEL), lambda i: (i, 0))
    row = pl.BlockSpec((1, D_MODEL), lambda i: (0, 0))
    return pl.pallas_call(
        body, name="ln1_bwd", grid=(t // tb,), in_specs=[blk, blk, blk, row], out_specs=[blk, blk, row, row],
        out_shape=[jax.ShapeDtypeStruct((t, D_MODEL), f32), jax.ShapeDtypeStruct((t, D_MODEL), bf16),
                   jax.ShapeDtypeStruct((1, D_MODEL), f32), jax.ShapeDtypeStruct((1, D_MODEL), f32)],
        compiler_params=_cparams(("arbitrary",)))(dh, x, mix, g_row)


def ln2_loss(h1, f, g_row, b_row, target, tb=512):
    t = h1.shape[0]

    def body(h_ref, f_ref, g_ref, b_ref, t_ref, dr_ref, drb_ref, dg_ref, db_ref, loss_ref):
        xhat, rstd = _ln_stats(ALPHA * h_ref[...] + f_ref[...])
        g = g_ref[...]
        err = xhat * g + b_ref[...] - t_ref[...]
        dy = err * (1.0 / D_MODEL)
        dr = _ln_bwd(dy, xhat, rstd, g)
        dr_ref[...] = dr
        drb_ref[...] = dr.astype(drb_ref.dtype)

        @pl.when(pl.program_id(0) == 0)
        def _():
            dg_ref[...] = jnp.zeros_like(dg_ref)
            db_ref[...] = jnp.zeros_like(db_ref)
            loss_ref[...] = jnp.zeros_like(loss_ref)

        dg_ref[...] += jnp.sum(dy * xhat, axis=0, keepdims=True)
        db_ref[...] += jnp.sum(dy, axis=0, keepdims=True)
        part = jnp.sum(jnp.mean(err * err, axis=1, keepdims=True), axis=0, keepdims=True)
        loss_ref[...] += 0.5 * part

    blk = pl.BlockSpec((tb, D_MODEL), lambda i: (i, 0))
    row = pl.BlockSpec((1, D_MODEL), lambda i: (0, 0))
    return pl.pallas_call(
        body, name="ln2_loss", grid=(t // tb,), in_specs=[blk, blk, row, row, blk],
        out_specs=[blk, blk, row, row, pl.BlockSpec((8, 128), lambda i: (0, 0))],
        out_shape=[jax.ShapeDtypeStruct((t, D_MODEL), f32), jax.ShapeDtypeStruct((t, D_MODEL), bf16),
                   jax.ShapeDtypeStruct((1, D_MODEL), f32), jax.ShapeDtypeStruct((1, D_MODEL), f32),
                   jax.ShapeDtypeStruct((8, 128), f32)],
        compiler_params=_cparams(("arbitrary",)))(h1, f, g_row, b_row, target)


TAIL_BLOCK, TAIL_AT = divmod(OFF_TAIL, PACK_TILE)


def _sum4(ref):
    return ((ref[0].astype(f32) + ref[1].astype(f32)) + ref[2].astype(f32)) + ref[3].astype(f32)


def _adamw_update(g, w_ref, m_ref, v_ref, g_ref, d_ref, nm_ref, nv_ref):
    c1 = 1.0 - ADAM_B1 ** ADAM_STEP
    c2 = 1.0 - ADAM_B2 ** ADAM_STEP
    nm = ADAM_B1 * m_ref[...] + (1.0 - ADAM_B1) * g
    nv = ADAM_B2 * v_ref[...] + (1.0 - ADAM_B2) * (g * g)
    g_ref[...] = g
    nm_ref[...] = nm
    nv_ref[...] = nv
    d_ref[...] = -ADAM_LR * ((nm / c1) / (jnp.sqrt(nv / c2) + ADAM_EPS) + ADAM_WD * w_ref[...])


def adamw_early(landed, parts, me, w, m, v):
    off = LATE_ROWS // EARLY_TILE

    def body(me_ref, *refs):
        src = refs[0:N_DEV]
        own_ref, w_ref, m_ref, v_ref = refs[N_DEV:N_DEV + 4]
        mine = me_ref[0]
        g = None
        for s in range(N_DEV):
            term = jnp.where(mine == s, own_ref[0], src[s][0])
            g = term if g is None else g + term
        _adamw_update(g, w_ref, m_ref, v_ref, *refs[N_DEV + 4:])

    def slot(s):
        return pl.BlockSpec((1, EARLY_TILE, 1024), lambda i, me_ref: (jnp.where(me_ref[0] == s, (s + 1) % N_DEV, s), i, 0))

    shard = pl.BlockSpec((EARLY_TILE, 1024), lambda i, me_ref: (i + off, 0))
    out_blk = pl.BlockSpec((EARLY_TILE, 1024), lambda i, me_ref: (i, 0))
    grid_spec = pltpu.PrefetchScalarGridSpec(
        num_scalar_prefetch=1, grid=(EARLY_ROWS // EARLY_TILE,),
        in_specs=[slot(s) for s in range(N_DEV)]
        + [pl.BlockSpec((1, EARLY_TILE, 1024), lambda i, me_ref: (me_ref[0], i, 0)), shard, shard, shard],
        out_specs=[out_blk] * 4)
    out = jax.ShapeDtypeStruct((EARLY_ROWS, 1024), f32)
    return pl.pallas_call(body, name="adamw_early", grid_spec=grid_spec, out_shape=[out] * 4,
                          compiler_params=_cparams(("parallel",)))(me, *([landed] * N_DEV), parts, w, m, v)


def adamw(parts, tails, w, m, v):
    rows = parts.shape[1]

    def body(p_ref, t_ref, w_ref, m_ref, v_ref, g_ref, d_ref, nm_ref, nv_ref):
        g = _sum4(p_ref)
        with_tail = jnp.concatenate([g[0:TAIL_AT], _sum4(t_ref), g[TAIL_AT + ROWS_TAIL:]], axis=0)
        g = jnp.where(pl.program_id(0) == TAIL_BLOCK, with_tail, g)
        _adamw_update(g, w_ref, m_ref, v_ref, g_ref, d_ref, nm_ref, nv_ref)

    blk = pl.BlockSpec((PACK_TILE, 1024), lambda i: (i, 0))
    out = jax.ShapeDtypeStruct((rows, 1024), f32)
    return pl.pallas_call(
        body, name="adamw", grid=(rows // PACK_TILE,),
        in_specs=[pl.BlockSpec((4, PACK_TILE, 1024), lambda i: (0, i, 0)),
                  pl.BlockSpec((4, ROWS_TAIL, 1024), lambda i: (0, 0, 0)), blk, blk, blk], out_specs=[blk] * 4,
        out_shape=[out] * 4, compiler_params=_cparams(("parallel",)))(parts, tails, w, m, v)


def pair_sum(parts, recv, core):
    rows = parts.shape[1]

    def body(c_ref, a_ref, b_ref, o_ref, t_ref):
        s = a_ref[...] + b_ref[...]
        o_ref[...] = s.astype(o_ref.dtype)

        @pl.when(pl.program_id(1) == TAIL_BLOCK)
        def _():
            t_ref[...] = s[:, TAIL_AT:TAIL_AT + ROWS_TAIL]

    grid_spec = pltpu.PrefetchScalarGridSpec(
        num_scalar_prefetch=1, grid=(4, rows // PACK_TILE),
        in_specs=[pl.BlockSpec((1, PACK_TILE, 1024), lambda j, i, c_ref: (2 * j + c_ref[0], i, 0)),
                  pl.BlockSpec((1, PACK_TILE, 1024), lambda j, i, c_ref: (j, i, 0))],
        out_specs=[pl.BlockSpec((1, PACK_TILE, 1024), lambda j, i, c_ref: (j, i, 0)),
                   pl.BlockSpec((1, ROWS_TAIL, 1024), lambda j, i, c_ref: (j, 0, 0))])
    return pl.pallas_call(
        body, name="pair_sum", grid_spec=grid_spec,
        out_shape=[jax.ShapeDtypeStruct(recv.shape, bf16), jax.ShapeDtypeStruct((4, ROWS_TAIL, 1024), f32)],
        compiler_params=_cparams(("parallel", "arbitrary")))(core, parts, recv)


def _place():
    return lax.axis_index("x"), lax.axis_index("y"), lax.axis_index("c")


def all_gather_blocks(shard):
    rows, cols = shard.shape

    def body(x_ref, out_ref, send_sems, recv_sems, local_sem):
        x, y, c = _place()
        me, sibling = (x, y, c), (x, y, 1 - c)
        chips = [(1 - x, y), (x, 1 - y), (1 - x, 1 - y)]

        def slot(px, py, pc):
            return out_ref.at[4 * px + 2 * py + pc]

        def copy(k, block, to, src=None):
            return pltpu.make_async_remote_copy(
                src_ref=slot(*block) if src is None else src, dst_ref=slot(*block), send_sem=send_sems.at[k],
                recv_sem=recv_sems.at[k], device_id=to, device_id_type=MESH)

        mine = pltpu.make_async_copy(x_ref, slot(*me), local_sem)
        mine.start()
        first = [copy(0, me, sibling, src=x_ref)]
        first += [copy(1 + j, me, (*chip, c), src=x_ref) for j, chip in enumerate(chips)]
        for cp in first:
            cp.start()
        passed = [copy(4 + j, (*chip, c), sibling) for j, chip in enumerate(chips)]
        for j, chip in enumerate(chips):
            copy(1 + j, (*chip, c), me).wait_recv()
            passed[j].start()
        copy(0, sibling, me).wait_recv()
        for j, chip in enumerate(chips):
            copy(4 + j, (*chip, 1 - c), me).wait_recv()
        for cp in first + passed:
            cp.wait_send()
        mine.wait()

    return pl.pallas_call(
        body, name="all_gather_blocks", out_shape=jax.ShapeDtypeStruct((N_DEV, rows, cols), shard.dtype),
        in_specs=[pl.BlockSpec(memory_space=pl.ANY)], out_specs=pl.BlockSpec(memory_space=pl.ANY),
        scratch_shapes=[pltpu.SemaphoreType.DMA((7,)), pltpu.SemaphoreType.DMA((7,)), pltpu.SemaphoreType.DMA],
        compiler_params=pltpu.CompilerParams(has_side_effects=True))(shard)


def pair_exchange(parts):
    _, rows, cols = parts.shape

    def body(p_ref, recv_ref, send_sems, recv_sems):
        x, y, c = _place()
        copies = [pltpu.make_async_remote_copy(
            src_ref=p_ref.at[2 * j + 1 - c], dst_ref=recv_ref.at[j], send_sem=send_sems.at[j], recv_sem=recv_sems.at[j],
            device_id=(x, y, 1 - c), device_id_type=MESH) for j in range(4)]
        for cp in copies:
            cp.start()
        for cp in copies:
            cp.wait_recv()
        for cp in copies:
            cp.wait_send()

    return pl.pallas_call(
        body, name="pair_exchange", out_shape=jax.ShapeDtypeStruct((4, rows, cols), parts.dtype),
        in_specs=[pl.BlockSpec(memory_space=pl.ANY)], out_specs=pl.BlockSpec(memory_space=pl.ANY),
        scratch_shapes=[pltpu.SemaphoreType.DMA((4,)), pltpu.SemaphoreType.DMA((4,))],
        compiler_params=pltpu.CompilerParams(has_side_effects=True))(parts)


def chip_exchange(parts):
    n = len(parts)

    def body(*refs):
        p_refs, out_refs = refs[0:n], refs[n:2 * n]
        send_sems, recv_sems, local_sems = refs[2 * n:]
        x, y, c = _place()
        mine = 2 * x + y
        flips = [(x, 1 - y), (1 - x, y), (1 - x, 1 - y)]

        def copy(a, k, src_slot, dst_slot):
            px, py = flips[k]
            return pltpu.make_async_remote_copy(
                src_ref=p_refs[a].at[src_slot], dst_ref=out_refs[a].at[dst_slot], send_sem=send_sems.at[3 * a + k],
                recv_sem=recv_sems.at[3 * a + k], device_id=(px, py, c), device_id_type=MESH)

        local = [pltpu.make_async_copy(p_refs[a].at[mine], out_refs[a].at[mine], local_sems.at[a]) for a in range(n)]
        sends = [copy(a, k, 2 * flips[k][0] + flips[k][1], mine) for a in range(n) for k in range(3)]
        for cp in local + sends:
            cp.start()
        for a in range(n):
            for k in range(3):
                copy(a, k, mine, 2 * flips[k][0] + flips[k][1]).wait_recv()
        for cp in sends:
            cp.wait_send()
        for cp in local:
            cp.wait()

    return pl.pallas_call(
        body, name="chip_exchange", out_shape=[jax.ShapeDtypeStruct(p.shape, p.dtype) for p in parts],
        in_specs=[pl.BlockSpec(memory_space=pl.ANY)] * n, out_specs=[pl.BlockSpec(memory_space=pl.ANY)] * n,
        scratch_shapes=[pltpu.SemaphoreType.DMA((3 * n,)), pltpu.SemaphoreType.DMA((3 * n,)), pltpu.SemaphoreType.DMA((n,))],
        compiler_params=pltpu.CompilerParams(has_side_effects=True))(*parts)


_HBM = pl.BlockSpec(memory_space=pltpu.HBM)
_SEM = pl.BlockSpec(memory_space=pltpu.SEMAPHORE)


def _peer(k):
    x, y, c = _place()
    px, py, pc = (1 - x if k & 4 else x), (1 - y if k & 2 else y), (1 - c if k & 1 else c)
    return (px, py, pc), 4 * px + 2 * py + pc


def scatter_start(parts, name):
    per_device = parts.ndim == 3

    def body(p_ref, land_ref, send_sems, recv_sems, p_thru, land_thru, token):
        x, y, c = _place()
        me = 4 * x + 2 * y + c
        for k in range(1, N_DEV):
            place, idx = _peer(k)
            pltpu.make_async_remote_copy(src_ref=p_ref.at[idx] if per_device else p_ref, dst_ref=land_ref.at[me],
                                         send_sem=send_sems.at[k - 1], recv_sem=recv_sems.at[k - 1], device_id=place,
                                         device_id_type=MESH).start()
        token[...] = jnp.zeros_like(token)

    land_shape = parts.shape if per_device else (N_DEV,) + parts.shape
    landing = lax.empty(land_shape, parts.dtype)
    return pl.pallas_call(
        body, name=name,
        out_shape=(pltpu.SemaphoreType.DMA((N_DEV - 1,)), pltpu.SemaphoreType.DMA((N_DEV - 1,)),
                   pltpu.HBM(parts.shape, parts.dtype), pltpu.HBM(land_shape, parts.dtype),
                   jax.ShapeDtypeStruct((8, 128), f32)),
        in_specs=(_HBM, _HBM), out_specs=(_SEM, _SEM, _HBM, _HBM, pl.BlockSpec(memory_space=pltpu.VMEM)),
        input_output_aliases={0: 2, 1: 3},
        compiler_params=pltpu.CompilerParams(has_side_effects=pltpu.SideEffectType.DATAFLOW_SIDE_EFFECTING),
    )(pltpu.with_memory_space_constraint(parts, pltpu.HBM), pltpu.with_memory_space_constraint(landing, pltpu.HBM))


def scatter_wait(send_sems, recv_sems, parts_thru, land_thru, after, name):
    per_device = parts_thru.ndim == 3

    def body(p_ref, land_ref, send_sems, recv_sems, after_ref, p_out, land_out):
        for k in range(1, N_DEV):
            place, idx = _peer(k)
            copy = pltpu.make_async_remote_copy(src_ref=p_ref.at[idx] if per_device else p_ref, dst_ref=land_ref.at[idx],
                                                send_sem=send_sems.at[k - 1], recv_sem=recv_sems.at[k - 1],
                                                device_id=place, device_id_type=MESH)
            copy.wait_send()
            copy.wait_recv()

    return pl.pallas_call(
        body, name=name,
        out_shape=(pltpu.HBM(parts_thru.shape, parts_thru.dtype), pltpu.HBM(land_thru.shape, land_thru.dtype)),
        in_specs=(_HBM, _HBM, _SEM, _SEM, pl.BlockSpec(memory_space=pl.ANY)), out_specs=(_HBM, _HBM),
        input_output_aliases={0: 0, 1: 1},
        compiler_params=pltpu.CompilerParams(has_side_effects=pltpu.SideEffectType.DATAFLOW_SIDE_EFFECTING),
    )(parts_thru, land_thru, send_sems, recv_sems, after)


def _tail_rows(conv_part, small, extra):
    lead = conv_part.shape[:-1]
    rep = jnp.concatenate([small[n].reshape(-1).astype(f32) for n in SMALL] + [extra.reshape(1).astype(f32)])
    flat = jnp.concatenate([conv_part, jnp.broadcast_to(rep, lead + rep.shape),
                            jnp.zeros(lead + (ROWS_TAIL * 1024 - TAIL_ELEMS,), f32)], axis=-1)
    return flat.reshape(lead + (ROWS_TAIL, 1024))


def _late_rows(w_in_t, tail):
    lead = tail.shape[:-2]
    zeros = lambda r: jnp.zeros(lead + (r, 1024), f32)
    return jnp.concatenate([w_in_t, zeros(OFF_TAIL - IN_SHARD), tail, zeros(LATE_ROWS - OFF_TAIL - ROWS_TAIL)], axis=-2)


def _early_rows(w_ps, w_out, w_up_t, w_down, w_pa_t):
    return jnp.concatenate([w_ps, w_out, w_up_t, w_down, w_pa_t.reshape(w_pa_t.shape[:-2] + (ROWS_PA, 1024))], axis=-2)


def _pack_shard(vals):
    tail = _tail_rows(vals["conv_w"].reshape(-1), vals, jnp.zeros((), f32))
    return jnp.concatenate([_late_rows(vals["w_in"].T, tail),
                            _early_rows(vals["w_proj_ssd"], vals["w_out"], vals["w_up"].T, vals["w_down"],
                                        vals["w_proj_attn"].T)], axis=0)


def _unpack_shard(late, early):
    e = lambda lo, hi: early[lo - LATE_ROWS:hi - LATE_ROWS]
    out = {"w_in": late[0:IN_SHARD].T, "w_proj_ssd": e(OFF_PS, OFF_OUT), "w_out": e(OFF_OUT, OFF_UP),
           "w_up": e(OFF_UP, OFF_DOWN).T, "w_down": e(OFF_DOWN, OFF_PA),
           "w_proj_attn": e(OFF_PA, PACK_ROWS).reshape(D_MODEL // N_DEV, ATTN_OUT).T}
    flat = late[OFF_TAIL:OFF_TAIL + ROWS_TAIL].reshape(-1)
    out["conv_w"] = flat[0:CONV_SHARD].reshape(D_CONV, CONV_DIM // N_DEV)
    off = CONV_SHARD
    for n in SMALL:
        out[n] = flat[off:off + SMALL_SIZES[n]]
        off += SMALL_SIZES[n]
    out["_extra"] = flat[off]
    return out


def _blocks(g):
    return g.reshape(N_DEV, g.shape[0] // N_DEV, g.shape[1])


def _pack_early_parts(full):
    return _early_rows(_blocks(full["w_proj_ssd"]), _blocks(full["w_out"]), _blocks(full["w_up_t"]),
                       _blocks(full["w_down"]), _blocks(full["w_proj_attn_t"]))


def _pack_late_parts(full, small, extra):
    conv = full["conv_w"].reshape(D_CONV, N_DEV, CONV_DIM // N_DEV).transpose(1, 0, 2).reshape(N_DEV, CONV_SHARD)
    return _late_rows(_blocks(full["w_in_t"]), _tail_rows(conv, small, extra))


def _gather_weights(w):
    conv_bits = lax.bitcast_convert_type(w["conv_w"], bf16).reshape(-1)
    conv_rows = jnp.concatenate([conv_bits, jnp.zeros((16 * 1024 - 2 * CONV_SHARD,), bf16)]).reshape(16, 1024)
    packed = _pack_shard(w)
    first = OFF_TAIL + ROWS_TAIL
    got = all_gather_blocks(jnp.concatenate([packed[0:OFF_TAIL].astype(bf16), conv_rows], axis=0))
    got, rest = lax.optimization_barrier((got, packed[first:].astype(bf16)))
    send_sems, recv_sems, rest_thru, land_thru, token = scatter_start(rest, "gather_start")
    conv =lax.bitcast_convert_type(got[:, OFF_TAIL:OFF_TAIL + 4].reshape(N_DEV, 4096)[:, 0:2 * CONV_SHARD]
                                    .reshape(N_DEV, D_CONV, CONV_DIM // N_DEV, 2), f32)
    now = {"w_in_t": got[:, 0:IN_SHARD].reshape(IN_COLS, 1024), "conv_w": conv.transpose(1, 0, 2).reshape(D_CONV, CONV_DIM)}

    def later(after):
        mine, landed = scatter_wait(send_sems, recv_sems, rest_thru, land_thru, after, "gather_wait")
        x, y, c = _place()
        landed = lax.dynamic_update_slice(landed, mine[None], (4 * x + 2 * y + c, 0, 0))
        whole = lambda lo, hi: landed[:, lo - first:hi - first].reshape(N_DEV * (hi - lo), 1024)
        return {"w_proj_ssd": whole(OFF_PS, OFF_OUT), "w_out": whole(OFF_OUT, OFF_UP), "w_up_t": whole(OFF_UP, OFF_DOWN),
                "w_down": whole(OFF_DOWN, OFF_PA),
                "w_proj_attn_t": landed[:, OFF_PA - first:PACK_ROWS - first].reshape(D_MODEL, ATTN_OUT)}

    return now, later, token


def _row(v, width=None):
    v = v.reshape(1, -1).astype(f32)
    return v if width is None else jnp.pad(v, ((0, 0), (0, width - v.shape[1])))


def _lanes256(vf, vb):
    z = jnp.zeros((96,), f32)
    return jnp.concatenate([vf.astype(f32), z, vb.astype(f32), z]).reshape(1, 256)


def _local_step(x2, tgt, wf, p, send_early=None, late_weights=None, start_token=None):
    t = x2.shape[0]
    o = np.cumsum((0,) + IN_SPLITS)
    wt = wf["w_in_t"]
    wt_z, wt_xbc, wt_dt = wt[o[0]:o[1]], wt[o[1]:o[2]], wt[o[2]:o[4]]
    wt_qkv, wt_gate = wt[o[4]:o[7]], wt[o[7]:o[8]]

    spread = lambda v: jnp.broadcast_to(v.astype(f32)[..., None], v.shape + (128,))
    conv_w_b, conv_b_b = spread(wf["conv_w"]), spread(p["conv_b"])
    dt_bias_b = spread(jnp.concatenate([p["dt_bias_f"], p["dt_bias_b"]]))
    a_f, a_b = -jnp.exp(p["a_log_f"].astype(f32)), -jnp.exp(p["a_log_b"].astype(f32))
    a_coef_b = spread(jnp.concatenate([a_f, a_b]))
    skip_b = spread(jnp.repeat(p["d_skip"], SSD_HEAD_DIM))
    nw_b, bg_row = spread(p["ssd_norm_w"]), _row(p["b_gate"])
    g1, b1, g2, b2 = _row(p["ln1_g"]), _row(p["ln1_b"]), _row(p["ln2_g"]), _row(p["ln2_b"])

    xb = (x2 if start_token is None else x2 + start_token[0, 0]).astype(MXU_DTYPE)
    u_z = mm_nt(wt_z, xb, "in_z")
    u_xbc = mm_nt(wt_xbc, xb, "in_xbc")
    u_dt = mm_nt(wt_dt, xb, "in_dt")
    u_qkv = mm_nt_split(xb, wt_qkv, "in_qkv", 256, bf16)
    u_gate = mm_nt(xb, wt_gate, "in_gate")
    xbc_c, dsilu = conv_fwd_t(u_xbc, conv_w_b, conv_b_b)
    dt_t = dt_fwd_t(u_dt, dt_bias_b)
    y_f, h_f = ssd_fwd_t(xbc_c, dt_t, a_coef_b, False, "ssd_fwd_f")
    y_scan, h_b, yn = ssd_fwd_t(xbc_c, dt_t, a_coef_b, True, "ssd_fwd_b", prev=y_f, tail=(u_z, skip_b, nw_b))
    if late_weights is not None:
        wf = {**wf, **late_weights(yn)}
    y_ssd = mm_tn(yn, wf["w_proj_ssd"], "proj_ssd")

    def strided(a, dil):
        return a.reshape(t // dil, dil * 256)

    qkv, outs, lses = [], [], []
    for pi, (_, dil) in enumerate(DIL_PATTERNS):
        q, k, v = (strided(u_qkv[N_PATTERNS * s + pi], dil) for s in range(3))
        qkv.append((q, k, v))
        op, lp = attn_fwd(q, k, v, pi, dil, f"attn_fwd_{pi}")
        outs.append(op.reshape(t, 256))
        lses.append(lp.reshape(t, 256))
    ya, lse = attn_combine(outs, lses)
    y_att = mm_nt(ya, wf["w_proj_attn_t"], "proj_attn")
    m = merge_fwd(u_gate, bg_row, y_ssd, y_att)
    mix = mm_nn(m, wf["w_out"], "out_proj")
    h1, h1b = ln1_fwd(x2, mix, g1, b1)
    r_up, p_act = mm_nt(h1b, wf["w_up_t"], "mlp_up", relu2=True)
    f_dn = mm_nn(p_act, wf["w_down"], "mlp_down")
    dr2, dr2b, dg2, db2, loss8 = ln2_loss(h1, f_dn, g2, b2, tgt)

    full, small = {}, {}
    da = mm_nt(dr2b, wf["w_down"], "d_mlp_act", out_dtype=bf16, relu2_of=r_up)
    full["w_down"] = mm_tn(p_act, dr2b, "dw_down")
    full["w_up_t"] = mm_tn(da, h1b, "dw_up")
    dh1 = mm_nn(da, wf["w_up_t"], "d_h1", acc_in=dr2, acc_scale=ALPHA)
    dr1, dr1b, dg1, db1 = ln1_bwd(dh1, x2, mix, g1)
    dm = mm_nt(dr1b, wf["w_out"], "d_merge")
    full["w_out"] = mm_tn(m, dr1b, "dw_out")
    dys, dya_p, dga, dgb, dba, dbb = merge_bwd(dm, u_gate, bg_row, y_ssd, y_att)
    dyn = mm_nt(wf["w_proj_ssd"], dys, "d_yn")
    full["w_proj_ssd"] = mm_nn(yn, dys, "dw_proj_ssd")
    dya = mm_nn(dya_p, wf["w_proj_attn_t"], "d_ya")
    full["w_proj_attn_t"] = mm_tn(dya_p, ya, "dw_proj_attn")
    if send_early is not None:
        skip_b = skip_b + send_early(full)[0, 0]

    dxf, dbf, dcf, ddtf, daf, dy, du_ssd, dnw, ddx = ssd_bwd_t(xbc_c, dt_t, a_coef_b, dyn, h_f, False, "ssd_bwd_f",
                                                               skip_b=skip_b, tail=(y_scan, u_z, nw_b))
    dxs, dbs, dcs, ddtb, dab = ssd_bwd_t(xbc_c, dt_t, a_coef_b, dy, h_b, True, "ssd_bwd_b", prev=(dxf, dbf, dcf))
    du_ssd, dcw_x, dcb_x = conv_bwd_t(u_xbc, dsilu, dxs, conv_w_b, du_ssd, "conv_bwd_x", 0)
    du_ssd, dcw_b, dcb_b = conv_bwd_t(u_xbc, dsilu, dbs, conv_w_b, du_ssd, "conv_bwd_b", D_INNER)
    du_ssd, dcw_c, dcb_c = conv_bwd_t(u_xbc, dsilu, dcs, conv_w_b, du_ssd, "conv_bwd_c", D_INNER + 512)
    du_ssd, dbias = dt_bwd_t(ddtf, ddtb, u_dt, dt_bias_b, du_ssd)

    delta = attn_delta(dya, ya)
    dqs, dks, dvs = [], [], []
    for pi, (_, dil) in enumerate(DIL_PATTERNS):
        q, k, v = qkv[pi]
        sd, sl_, sdel = strided(dya, dil), strided(lse, dil), strided(delta, dil)
        dqs.append(attn_dq(q, k, v, sd, sl_, sdel, pi, dil, f"attn_dq_{pi}").reshape(t, 256))
        dk, dv = attn_dkv(q, k, v, sd, sl_, sdel, pi, dil, f"attn_dkv_{pi}")
        dks.append(dk.reshape(t, 256))
        dvs.append(dv.reshape(t, 256))
    du_qkv = jnp.concatenate(dqs + dks + dvs, axis=1)
    du_gate = jnp.concatenate([dga, dgb], axis=1)

    dx = mm_tn(du_ssd, wt[0:SSD_COLS], "dx_ssd", acc_in=dr1, acc_scale=ALPHA)
    dx = mm_nn(du_qkv, wt_qkv, "dx_qkv", acc_in=dx)
    dx = mm_nn(du_gate, wt_gate, "dx_gate", acc_in=dx)
    full["w_in_t"] = jnp.concatenate(
        [mm_nn(du_ssd, xb, "dw_in_ssd"), mm_tn(du_qkv, xb, "dw_in_qkv"), mm_tn(du_gate, xb, "dw_in_gate")], axis=0)
    lanes = lambda v: jnp.sum(v, axis=-1)
    full["conv_w"] = jnp.concatenate([lanes(dcw_x), lanes(dcw_b), lanes(dcw_c)], axis=1)

    small["b_gate"] = jnp.concatenate([dba, dbb], axis=1)
    small["conv_b"] = jnp.concatenate([lanes(dcb_x), lanes(dcb_b), lanes(dcb_c)])
    dbias = lanes(dbias)
    small["dt_bias_f"], small["dt_bias_b"] = dbias[0:32], dbias[32:64]
    small["a_log_f"] = lanes(daf) * a_f
    small["a_log_b"] = lanes(dab) * a_b
    small["d_skip"] = jnp.sum(lanes(ddx).reshape(SSD_HEADS, SSD_HEAD_DIM), axis=1)
    small["ssd_norm_w"] = lanes(dnw)
    small["ln1_g"], small["ln1_b"], small["ln2_g"], small["ln2_b"] = dg1, db1, dg2, db2
    return loss8[0, 0], dx, full, small


def kernel(x, w_in, b_gate, conv_w, conv_b, dt_bias_f, dt_bias_b, a_log_f, a_log_b, d_skip, ssd_norm_w, w_proj_ssd, w_proj_attn, w_out, ln1_g, ln1_b, w_up, w_down, ln2_g, ln2_b, loss_target, m_w_in, m_b_gate, m_conv_w, m_conv_b, m_dt_bias_f, m_dt_bias_b, m_a_log_f, m_a_log_b, m_d_skip, m_ssd_norm_w, m_w_proj_ssd, m_w_proj_attn, m_w_out, m_ln1_g, m_ln1_b, m_w_up, m_w_down, m_ln2_g, m_ln2_b, v_w_in, v_b_gate, v_conv_w, v_conv_b, v_dt_bias_f, v_dt_bias_b, v_a_log_f, v_a_log_b, v_d_skip, v_ssd_norm_w, v_w_proj_ssd, v_w_proj_attn, v_w_out, v_ln1_g, v_ln1_b, v_w_up, v_w_down, v_ln2_g, v_ln2_b):
    given = dict(locals())
    w = {n: given[n] for n in WEIGHTS}
    mom = {n: given["m_" + n] for n in WEIGHTS}
    var = {n: given["v_" + n] for n in WEIGHTS}
    t = x.shape[1]
    wf, late_weights, start_token = _gather_weights(w)
    in_flight = []

    def send_early(full):
        send_sems, recv_sems, parts_thru, land_thru, token = scatter_start(_pack_early_parts(full), "scatter_start")
        in_flight.append((send_sems, recv_sems, parts_thru, land_thru))
        return token

    loss, dx, full, small = _local_step(x.reshape(t, D_MODEL), loss_target.reshape(t, D_MODEL), wf, w, send_early,
                                        late_weights, start_token)
    late = _pack_late_parts(full, small, loss)
    x_, y_, c_ = _place()
    core = c_.astype(jnp.int32).reshape(1)
    me = (4 * x_ + 2 * y_ + c_).astype(jnp.int32).reshape(1)
    wp, mp, vp = _pack_shard(w), _pack_shard(mom), _pack_shard(var)
    early_parts, landed = scatter_wait(*in_flight[0], late, "scatter_wait")
    early_out = adamw_early(landed, early_parts, me, wp, mp, vp)
    parts, tails = chip_exchange(pair_sum(late, pair_exchange(late), core))
    late_out = adamw(parts, tails, wp, mp, vp)
    g, delta, new_m, new_v = (_unpack_shard(a, b) for a, b in zip(late_out, early_out))
    outs = [g["_extra"], dx.reshape(x.shape)]
    for d in (g, delta, new_m, new_v):
        outs += [d[n].reshape(w[n].shape) for n in WEIGHTS]
    return tuple(outs)
```

```python
import functools
import math

import jax
import jax.numpy as jnp
import numpy as np
from jax import lax
from jax.experimental import pallas as pl
from jax.experimental.pallas import tpu as pltpu

f32 = jnp.float32
bf16 = jnp.bfloat16
MXU_DTYPE = jnp.bfloat16

N_DEV = 8
D_MODEL = 1024
D_INNER = 2048
SSD_HEADS = 32
SSD_HEAD_DIM = 64
SSD_GROUPS = 4
D_STATE = 128
D_CONV = 5
CHUNK = 128
CONV_DIM = D_INNER + 2 * SSD_GROUPS * D_STATE
NORM_EPS = 1e-5
ATTN_HEAD_DIM = 64
DIL_PATTERNS = ((128, 1), (512, 4), (2048, 16))
N_PATTERNS = len(DIL_PATTERNS)
HEADS_PER_PATTERN = 4
ATTN_HEADS = 12
ATTN_WIDTH = 768
ATTN_OUT = 256
D_FF = 4096
ALPHA = 2.0 ** 0.25
IN_SPLITS = (D_INNER, CONV_DIM, SSD_HEADS, SSD_HEADS, ATTN_WIDTH, ATTN_WIDTH, ATTN_WIDTH, 2 * D_MODEL)
IN_COLS = sum(IN_SPLITS)
SSD_COLS = sum(IN_SPLITS[0:4])
ADAM_LR, ADAM_B1, ADAM_B2, ADAM_EPS, ADAM_WD, ADAM_STEP = 0.001, 0.9, 0.999, 1e-08, 0.01, 10
NEG_BIG = -1e30
VMEM_LIMIT = 56 * 1024 * 1024
MESH = pl.DeviceIdType.MESH

SMALL = ("b_gate", "conv_b", "dt_bias_f", "dt_bias_b", "a_log_f", "a_log_b", "d_skip", "ssd_norm_w",
         "ln1_g", "ln1_b", "ln2_g", "ln2_b")
WEIGHTS = ("w_in", "b_gate", "conv_w", "conv_b", "dt_bias_f", "dt_bias_b", "a_log_f", "a_log_b", "d_skip",
           "ssd_norm_w", "w_proj_ssd", "w_proj_attn", "w_out", "ln1_g", "ln1_b", "w_up", "w_down", "ln2_g", "ln2_b")
SMALL_SIZES = {"b_gate": 2 * D_MODEL, "conv_b": CONV_DIM, "dt_bias_f": 32, "dt_bias_b": 32, "a_log_f": 32, "a_log_b": 32,
               "d_skip": 32, "ssd_norm_w": D_INNER, "ln1_g": D_MODEL, "ln1_b": D_MODEL, "ln2_g": D_MODEL, "ln2_b": D_MODEL}
IN_SHARD = IN_COLS // N_DEV
OFF_TAIL = 1200
ROWS_TAIL = 16
PACK_TILE = 128
LATE_ROWS = 1280
ROWS_PS, ROWS_OUT, ROWS_UP, ROWS_DOWN, ROWS_PA = D_INNER // N_DEV, D_MODEL // N_DEV, D_FF // N_DEV, D_FF // N_DEV, 32
OFF_PS = LATE_ROWS
OFF_OUT = OFF_PS + ROWS_PS
OFF_UP = OFF_OUT + ROWS_OUT
OFF_DOWN = OFF_UP + ROWS_UP
OFF_PA = OFF_DOWN + ROWS_DOWN
PACK_ROWS = OFF_PA + ROWS_PA
EARLY_ROWS = PACK_ROWS - LATE_ROWS
EARLY_TILE = 160
CONV_SHARD = D_CONV * CONV_DIM // N_DEV
TAIL_ELEMS = CONV_SHARD + sum(SMALL_SIZES.values()) + 1


def _cparams(sem=None, **kw):
    return pltpu.CompilerParams(dimension_semantics=sem, vmem_limit_bytes=VMEM_LIMIT, **kw)


def _mx(v):
    return v.astype(MXU_DTYPE)


def _dot(a, b):
    return jnp.dot(_mx(a), _mx(b), preferred_element_type=f32)


def _dot_nt(a, b):
    return lax.dot_general(_mx(a), _mx(b), (((1,), (1,)), ((), ())), preferred_element_type=f32)


def _dot_tn(a, b):
    return lax.dot_general(_mx(a), _mx(b), (((0,), (0,)), ((), ())), preferred_element_type=f32)


def _dot_exact(a, b):
    return jnp.dot(a, b, precision=lax.Precision.HIGHEST, preferred_element_type=f32)


def _sigmoid(v):
    return 1.0 / (1.0 + jnp.exp(-v))


def _pick(n, prefs):
    for p in prefs:
        if n % p == 0:
            return p
    return n


MM_TILE = 1024


def mm_nn(a, b, name, out_dtype=f32, acc_in=None, acc_scale=1.0):
    m, k = a.shape
    n = b.shape[1]
    tm = _pick(m, (MM_TILE, 576, 512, 256, 128, 64))
    tn = _pick(n, (MM_TILE, 512, 256, 128))
    tk = _pick(k, (2048, 1536, 1152, 1024, 768, 512, 256, 128))
    nk = k // tk

    def body(*refs):
        a_ref, b_ref = refs[0:2]
        c_ref = refs[2] if acc_in is not None else None
        o_ref = refs[3] if acc_in is not None else refs[2]

        def finish(r):
            if acc_in is not None:
                r = r + acc_scale * c_ref[...]
            o_ref[...] = r.astype(o_ref.dtype)

        if nk == 1:
            finish(_dot(a_ref[...], b_ref[...]))
            return
        acc_ref = refs[-1]
        kk = pl.program_id(2)

        @pl.when(kk == 0)
        def _():
            acc_ref[...] = jnp.zeros_like(acc_ref)

        acc_ref[...] += _dot(a_ref[...], b_ref[...])

        @pl.when(kk == nk - 1)
        def _():
            finish(acc_ref[...])

    in_specs = [pl.BlockSpec((tm, tk), lambda i, j, kk: (i, kk)), pl.BlockSpec((tk, tn), lambda i, j, kk: (kk, j))]
    args = [a, b]
    if acc_in is not None:
        in_specs.append(pl.BlockSpec((tm, tn), lambda i, j, kk: (i, j)))
        args.append(acc_in)
    return pl.pallas_call(
        body, name=name, grid=(m // tm, n // tn, nk), in_specs=in_specs,
        out_specs=pl.BlockSpec((tm, tn), lambda i, j, kk: (i, j)),
        out_shape=jax.ShapeDtypeStruct((m, n), out_dtype),
        scratch_shapes=[pltpu.VMEM((tm, tn), f32)] if nk > 1 else [],
        compiler_params=_cparams(("parallel", "parallel", "arbitrary")))(*args)


def mm_nt(a, b, name, out_dtype=f32, relu2=None, relu2_of=None):
    m, k = a.shape
    n = b.shape[0]
    tm = _pick(m, (MM_TILE, 512, 256, 128, 64))
    tn = _pick(n, (MM_TILE, 768, 512, 256, 128))

    def body(*refs):
        r = _dot_nt(refs[0][...], refs[1][...])
        if relu2:
            pos = jnp.maximum(r, 0.0)
            refs[2][...] = pos.astype(refs[2].dtype)
            refs[3][...] = (pos * pos).astype(refs[3].dtype)
        elif relu2_of is not None:
            refs[3][...] = (r * (2.0 * refs[2][...].astype(f32))).astype(refs[3].dtype)
        else:
            refs[2][...] = r.astype(refs[2].dtype)

    blk = pl.BlockSpec((tm, tn), lambda i, j: (i, j))
    in_specs = [pl.BlockSpec((tm, k), lambda i, j: (i, 0)), pl.BlockSpec((tn, k), lambda i, j: (j, 0))]
    args = [a, b]
    if relu2_of is not None:
        in_specs.append(blk)
        args.append(relu2_of)
    if relu2:
        out_specs, out_shape = [blk, blk], [jax.ShapeDtypeStruct((m, n), bf16), jax.ShapeDtypeStruct((m, n), bf16)]
    else:
        out_specs, out_shape = blk, jax.ShapeDtypeStruct((m, n), out_dtype)
    return pl.pallas_call(body, name=name, grid=(m // tm, n // tn), in_specs=in_specs, out_specs=out_specs,
                          out_shape=out_shape, compiler_params=_cparams(("parallel", "parallel")))(*args)


def mm_nt_split(a, b, name, width, out_dtype=f32):
    m, k = a.shape
    n = b.shape[0]
    tm = MM_TILE
    parts = n // width

    def body(a_ref, b_ref, *o_refs):
        r = _dot_nt(a_ref[...], b_ref[...])
        for q in range(parts):
            o_refs[q][...] = r[:, width * q:width * (q + 1)].astype(o_refs[q].dtype)

    blk = pl.BlockSpec((tm, width), lambda i: (i, 0))
    return pl.pallas_call(
        body, name=name, grid=(m // tm,),
        in_specs=[pl.BlockSpec((tm, k), lambda i: (i, 0)), pl.BlockSpec((n, k), lambda i: (0, 0))],
        out_specs=[blk] * parts, out_shape=[jax.ShapeDtypeStruct((m, width), out_dtype)] * parts,
        compiler_params=_cparams(("parallel",)))(a, b)


def mm_tn(a, b, name, acc_in=None, acc_scale=1.0):
    k, m = a.shape
    n = b.shape[1]
    tm = _pick(m, (MM_TILE, 768, 512, 256, 128))
    tn = _pick(n, (MM_TILE, 512, 256, 128))
    tk = _pick(k, (1024, 768, 576, 512, 256, 128, 64))
    nk = k // tk

    def body(*refs):
        a_ref, b_ref, o_ref = refs[0], refs[1], refs[-1]
        kk = pl.program_id(2)

        @pl.when(kk == 0)
        def _():
            o_ref[...] = jnp.zeros_like(o_ref) if acc_in is None else acc_scale * refs[2][...]

        o_ref[...] += _dot_tn(a_ref[...], b_ref[...])

    in_specs = [pl.BlockSpec((tk, tm), lambda i, j, kk: (kk, i)), pl.BlockSpec((tk, tn), lambda i, j, kk: (kk, j))]
    args = [a, b]
    if acc_in is not None:
        in_specs.append(pl.BlockSpec((tm, tn), lambda i, j, kk: (i, j)))
        args.append(acc_in)
    return pl.pallas_call(
        body, name=name, grid=(m // tm, n // tn, nk), in_specs=in_specs,
        out_specs=pl.BlockSpec((tm, tn), lambda i, j, kk: (i, j)),
        out_shape=jax.ShapeDtypeStruct((m, n), f32),
        compiler_params=_cparams(("parallel", "parallel", "arbitrary")))(*args)


def _halo_specs(tb, cb, nt, off=0):
    r = tb // 8
    return [pl.BlockSpec((8, cb), lambda j, i: (jnp.maximum(i * r - 1, 0), j + off)),
            pl.BlockSpec((tb, cb), lambda j, i: (i, j + off)),
            pl.BlockSpec((8, cb), lambda j, i: (jnp.minimum((i + 1) * r, nt * r - 1), j + off))]


def _with_halo(prev_ref, own_ref, next_ref, i, nt):
    prev = jnp.where(i > 0, prev_ref[...].astype(f32), 0.0)
    nxt = jnp.where(i < nt - 1, next_ref[...].astype(f32), 0.0)
    return jnp.concatenate([prev, own_ref[...].astype(f32), nxt], axis=0)


def _shifted(xcat, s, tb):
    n = xcat.shape[0]
    return pltpu.roll(xcat, (-s) % n, 0)[8:8 + tb]


def conv_fwd(xbc, w8, b_row, tb=512, cb=512):
    t, c = xbc.shape
    nt = t // tb

    def body(prev_ref, own_ref, next_ref, w_ref, b_ref, o_ref):
        i = pl.program_id(1)
        xcat = _with_halo(prev_ref, own_ref, next_ref, i, nt)
        w = w_ref[...]
        pre = b_ref[...] + w[0:1] * _shifted(xcat, -2, tb)
        for k in range(1, D_CONV):
            pre = pre + w[k:k + 1] * _shifted(xcat, k - 2, tb)
        o_ref[...] = pre * _sigmoid(pre)

    return pl.pallas_call(
        body, name="conv_fwd", grid=(c // cb, nt),
        in_specs=_halo_specs(tb, cb, nt) + [pl.BlockSpec((8, cb), lambda j, i: (0, j)), pl.BlockSpec((1, cb), lambda j, i: (0, j))],
        out_specs=pl.BlockSpec((tb, cb), lambda j, i: (i, j)), out_shape=jax.ShapeDtypeStruct((t, c), f32),
        compiler_params=_cparams(("parallel", "parallel")))(xbc, xbc, xbc, w8, b_row)


def conv_bwd(xbc, xoff, grads, scales, w8, b_row, name, tb=512, cb=512):
    t, c = grads[0].shape
    nt = t // tb
    ng = len(grads)
    has_scale = [s is not None for s in scales]

    def body(*refs):
        i = pl.program_id(1)
        xr = refs[0:3]
        gr = [refs[3 + 3 * q: 6 + 3 * q] for q in range(ng)]
        pos = 3 + 3 * ng
        sr = []
        for q in range(ng):
            if has_scale[q]:
                sr.append(refs[pos])
                pos += 1
            else:
                sr.append(None)
        w_ref, b_ref, dx_ref, dw_ref, db_ref = refs[pos:pos + 5]
        xcat = _with_halo(*xr, i, nt)
        gcat = None
        for q in range(ng):
            gq = _with_halo(*gr[q], i, nt)
            if sr[q] is not None:
                gq = gq * sr[q][...]
            gcat = gq if gcat is None else gcat + gq
        w = w_ref[...]
        n = tb + 16
        pre = b_ref[...] + w[0:1] * pltpu.roll(xcat, 2, 0)
        for k in range(1, D_CONV):
            pre = pre + w[k:k + 1] * pltpu.roll(xcat, (2 - k) % n, 0)
        sg = _sigmoid(pre)
        dpre = gcat * sg * (1.0 + pre * (1.0 - sg))
        dx = w[0:1] * _shifted(dpre, 2, tb)
        for k in range(1, D_CONV):
            dx = dx + w[k:k + 1] * _shifted(dpre, 2 - k, tb)
        dx_ref[...] = dx.astype(dx_ref.dtype)
        dp_own = dpre[8:8 + tb]
        rows = [jnp.sum(dp_own * _shifted(xcat, k - 2, tb), axis=0, keepdims=True) for k in range(D_CONV)]
        dw = jnp.concatenate(rows + [jnp.zeros((8 - D_CONV, cb), f32)], axis=0)
        db = jnp.sum(dp_own, axis=0, keepdims=True)

        @pl.when(i == 0)
        def _():
            dw_ref[...] = jnp.zeros_like(dw_ref)
            db_ref[...] = jnp.zeros_like(db_ref)

        dw_ref[...] += dw
        db_ref[...] += db

    in_specs = _halo_specs(tb, cb, nt, xoff)
    args = [xbc] * 3
    for g in grads:
        in_specs += _halo_specs(tb, cb, nt)
        args += [g] * 3
    for s in scales:
        if s is not None:
            in_specs.append(pl.BlockSpec((1, cb), lambda j, i: (0, j)))
            args.append(s)
    in_specs += [pl.BlockSpec((8, cb), lambda j, i: (0, j)), pl.BlockSpec((1, cb), lambda j, i: (0, j))]
    args += [w8, b_row]
    return pl.pallas_call(
        body, name=name, grid=(c // cb, nt), in_specs=in_specs,
        out_specs=[pl.BlockSpec((tb, cb), lambda j, i: (i, j)), pl.BlockSpec((8, cb), lambda j, i: (0, j)),
                   pl.BlockSpec((1, cb), lambda j, i: (0, j))],
        out_shape=[jax.ShapeDtypeStruct((t, c), bf16), jax.ShapeDtypeStruct((8, c), f32), jax.ShapeDtypeStruct((1, c), f32)],
        compiler_params=_cparams(("parallel", "arbitrary")))(*args)


def dt_fwd(u_dt, bias_row, tb=1024):
    t = u_dt.shape[0]

    def body(u_ref, b_ref, o_ref):
        v = u_ref[...] + b_ref[...]
        sp = jnp.maximum(v, 0.0) + jnp.log(1.0 + jnp.exp(-jnp.abs(v)))
        lane = lax.broadcasted_iota(jnp.int32, v.shape, 1)
        o_ref[...] = jnp.where((lane & 127) < SSD_HEADS, sp, 0.0)

    return pl.pallas_call(
        body, name="dt_fwd", grid=(t // tb,),
        in_specs=[pl.BlockSpec((tb, 256), lambda i: (i, 0)), pl.BlockSpec((1, 256), lambda i: (0, 0))],
        out_specs=pl.BlockSpec((tb, 256), lambda i: (i, 0)), out_shape=jax.ShapeDtypeStruct((t, 256), f32),
        compiler_params=_cparams(("parallel",)))(u_dt, bias_row)


def dt_bwd(ddt_f, ddt_b, u_dt, bias_row, tb=1024):
    t = u_dt.shape[0]

    def body(gf_ref, gb_ref, u_ref, b_ref, du_ref, db_ref):
        g = jnp.concatenate([jnp.sum(gf_ref[...], axis=0), jnp.sum(gb_ref[...], axis=0)], axis=1)
        du = g * _sigmoid(u_ref[...] + b_ref[...])
        du_ref[...] = du.astype(du_ref.dtype)

        @pl.when(pl.program_id(0) == 0)
        def _():
            db_ref[...] = jnp.zeros_like(db_ref)

        db_ref[...] += jnp.sum(du, axis=0, keepdims=True)

    return pl.pallas_call(
        body, name="dt_bwd", grid=(t // tb,),
        in_specs=[pl.BlockSpec((4, tb, 128), lambda i: (0, i, 0)), pl.BlockSpec((4, tb, 128), lambda i: (0, i, 0)),
                  pl.BlockSpec((tb, 256), lambda i: (i, 0)), pl.BlockSpec((1, 256), lambda i: (0, 0))],
        out_specs=[pl.BlockSpec((tb, 256), lambda i: (i, 0)), pl.BlockSpec((1, 256), lambda i: (0, 0))],
        out_shape=[jax.ShapeDtypeStruct((t, 256), bf16), jax.ShapeDtypeStruct((1, 256), f32)],
        compiler_params=_cparams(("arbitrary",)))(ddt_f, ddt_b, u_dt, bias_row)


def _ssd_common(dt_blk, a_row, reverse):
    row = lax.broadcasted_iota(jnp.int32, (CHUNK, CHUNK), 0)
    col = lax.broadcasted_iota(jnp.int32, (CHUNK, CHUNK), 1)
    mask = (row <= col) if reverse else (row >= col)
    tri = mask.astype(f32)
    a = dt_blk * a_row
    acs = _dot_exact(tri, a)
    atot = jnp.sum(a, axis=0, keepdims=True)
    return mask, tri, a, acs, atot, col


def _lane_col(mat, lane_idx, h):
    return jnp.sum(jnp.where(lane_idx == h, mat, 0.0), axis=1, keepdims=True)


def ssd_fwd(xbc_c, dt2, a_rows, reverse, name):
    t = xbc_c.shape[0]
    nc = t // CHUNK
    d_off = 1 if reverse else 0

    def cidx(c):
        return nc - 1 - c if reverse else c

    def body(x_ref, b_ref, c_ref, dt_ref, a_ref, y_ref, hp_ref, h_scr, acst_scr):
        g = pl.program_id(0)
        c = pl.program_id(1)

        @pl.when(c == 0)
        def _():
            h_scr[...] = jnp.zeros_like(h_scr)

        dt_blk = dt_ref[...]
        mask, tri, a, acs, atot, lane = _ssd_common(dt_blk, a_ref[...], reverse)
        acst_scr[...] = acs.T
        bm = b_ref[...]
        cm = c_ref[...]
        cb = _dot_nt(cm, bm)
        half = lane >= SSD_HEAD_DIM
        sub_half = lax.broadcasted_iota(jnp.int32, (CHUNK, 1), 0) >= SSD_HEAD_DIM
        for j in range(4):
            x = x_ref[:, 128 * j:128 * (j + 1)]
            cols, dts, tots = [], [], []
            y = None
            for e in range(2):
                h = 8 * g + 2 * j + e
                col_h = _lane_col(acs, lane, h)
                row_h = acst_scr[pl.ds(h, 1), :]
                dt_h = _lane_col(dt_blk, lane, h)
                lmat = jnp.where(mask, jnp.exp(jnp.where(mask, col_h - row_h, 0.0)), 0.0)
                xdt_e = jnp.where(half == (e == 1), x * dt_h, 0.0)
                ye = _dot(cb * lmat, xdt_e)
                y = ye if y is None else y + ye
                cols.append(col_h)
                dts.append(dt_h)
                tots.append(jnp.sum(jnp.where(lane[0:1] == h, atot, 0.0), axis=1, keepdims=True))
            hp = h_scr[j]
            hp_ref[0, j] = hp
            ecol = jnp.where(half, jnp.exp(cols[1]), jnp.exp(cols[0]))
            y = y + _dot_nt(cm, hp) * ecol
            y_ref[:, 128 * j:128 * (j + 1)] = y
            dec = jnp.where(half, jnp.exp(tots[1] - cols[1]), jnp.exp(tots[0] - cols[0]))
            xdt = x * jnp.where(half, dts[1], dts[0])
            s_new = _dot_tn(xdt * dec, bm)
            cd = jnp.where(sub_half, jnp.exp(tots[1]), jnp.exp(tots[0]))
            h_scr[j] = cd * hp + s_new

    return pl.pallas_call(
        body, name=name, grid=(SSD_GROUPS, nc),
        in_specs=[pl.BlockSpec((CHUNK, 512), lambda g, c: (cidx(c), g)),
                  pl.BlockSpec((CHUNK, 128), lambda g, c: (cidx(c), 16 + g)),
                  pl.BlockSpec((CHUNK, 128), lambda g, c: (cidx(c), 20 + g)),
                  pl.BlockSpec((CHUNK, 128), lambda g, c: (cidx(c), d_off)),
                  pl.BlockSpec((1, 128), lambda g, c: (0, d_off))],
        out_specs=[pl.BlockSpec((CHUNK, 512), lambda g, c: (cidx(c), g)),
                   pl.BlockSpec((1, 4, 128, 128), lambda g, c: (cidx(c), g, 0, 0))],
        out_shape=[jax.ShapeDtypeStruct((t, D_INNER), f32), jax.ShapeDtypeStruct((nc, 16, 128, 128), f32)],
        scratch_shapes=[pltpu.VMEM((4, 128, 128), f32), pltpu.VMEM((CHUNK, CHUNK), f32)],
        compiler_params=_cparams(("parallel", "arbitrary")))(xbc_c, xbc_c, xbc_c, dt2, a_rows)


def ssd_bwd(xbc_c, dt2, a_rows, dy, hprev, reverse, name):
    t = xbc_c.shape[0]
    nc = t // CHUNK
    d_off = 1 if reverse else 0

    def cidx(c):
        return c if reverse else nc - 1 - c

    def body(x_ref, b_ref, c_ref, dt_ref, a_ref, dy_ref, hp_ref, dx_ref, db_ref, dc_ref, ddt_ref, da_ref,
             dh_scr, acst_scr):
        g = pl.program_id(0)
        c = pl.program_id(1)

        @pl.when(c == 0)
        def _():
            dh_scr[...] = jnp.zeros_like(dh_scr)
            da_ref[...] = jnp.zeros_like(da_ref)

        dt_blk = dt_ref[...]
        a_row = a_ref[...]
        mask, tri, a, acs, atot, lane = _ssd_common(dt_blk, a_row, reverse)
        acst_scr[...] = acs.T
        sub = lax.broadcasted_iota(jnp.int32, (CHUNK, CHUNK), 0)
        bm = b_ref[...]
        cm = c_ref[...]
        cb = _dot_nt(cm, bm)
        half = lane >= SSD_HEAD_DIM
        sub_half = sub[:, 0:1] >= SSD_HEAD_DIM
        dcb = jnp.zeros((CHUNK, CHUNK), f32)
        dacs = jnp.zeros((CHUNK, CHUNK), f32)
        dacs_t = jnp.zeros((CHUNK, CHUNK), f32)
        dtot = jnp.zeros((1, CHUNK), f32)
        ddt_x = jnp.zeros((CHUNK, CHUNK), f32)
        dbm = jnp.zeros((CHUNK, D_STATE), f32)
        dcm = jnp.zeros((CHUNK, D_STATE), f32)
        for j in range(4):
            x = x_ref[:, 128 * j:128 * (j + 1)]
            dyp = dy_ref[:, 128 * j:128 * (j + 1)]
            hp = hp_ref[0, j]
            dhn = dh_scr[j]
            cols, dts, tots, hs = [], [], [], []
            dxdt = None
            for e in range(2):
                h = 8 * g + 2 * j + e
                sel = half == (e == 1)
                col_h = _lane_col(acs, lane, h)
                row_h = acst_scr[pl.ds(h, 1), :]
                dt_h = _lane_col(dt_blk, lane, h)
                lmat = jnp.where(mask, jnp.exp(jnp.where(mask, col_h - row_h, 0.0)), 0.0)
                xdt_e = jnp.where(sel, x * dt_h, 0.0)
                dy_e = jnp.where(sel, dyp, 0.0)
                ml = _dot_nt(dy_e, xdt_e) * lmat
                dcb = dcb + ml
                w = ml * cb
                dacs = dacs + jnp.where(lane == h, jnp.sum(w, axis=1, keepdims=True), 0.0)
                dacs_t = dacs_t - jnp.where(sub == h, jnp.sum(w, axis=0, keepdims=True), 0.0)
                de = _dot_tn(cb * lmat, dy_e)
                dxdt = de if dxdt is None else dxdt + de
                cols.append(col_h)
                dts.append(dt_h)
                tots.append(jnp.sum(jnp.where(lane[0:1] == h, atot, 0.0), axis=1, keepdims=True))
                hs.append(h)
            ecol = jnp.where(half, jnp.exp(cols[1]), jnp.exp(cols[0]))
            dec = jnp.where(half, jnp.exp(tots[1] - cols[1]), jnp.exp(tots[0] - cols[0]))
            cd = jnp.where(sub_half, jnp.exp(tots[1]), jnp.exp(tots[0]))
            dtp = jnp.where(half, dts[1], dts[0])
            xdt = x * dtp
            yoff = _dot_nt(cm, hp) * ecol
            dye = dyp * ecol
            dcm = dcm + _dot(dye, hp)
            dhp = _dot_tn(dye, cm)
            gmat = _dot_nt(bm, dhn)
            dxdt = dxdt + dec * gmat
            dbm = dbm + _dot(xdt * dec, dhn)
            r_off = dyp * yoff
            r_dec = xdt * gmat * dec
            r_x = dxdt * x
            hh = dhn * hp
            for e in range(2):
                sel = half == (e == 1)
                h = hs[e]
                s_off = jnp.sum(jnp.where(sel, r_off, 0.0), axis=1, keepdims=True)
                s_dec = jnp.sum(jnp.where(sel, r_dec, 0.0), axis=1, keepdims=True)
                dacs = dacs + jnp.where(lane == h, s_off - s_dec, 0.0)
                dcd = jnp.sum(jnp.sum(jnp.where(sub_half == (e == 1), hh, 0.0), axis=1, keepdims=True), axis=0, keepdims=True)
                tot_e = jnp.sum(s_dec, axis=0, keepdims=True) + jnp.exp(tots[e]) * dcd
                dtot = dtot + jnp.where(lane[0:1] == h, tot_e, 0.0)
                ddt_x = ddt_x + jnp.where(lane == h, jnp.sum(jnp.where(sel, r_x, 0.0), axis=1, keepdims=True), 0.0)
            dx_ref[:, 128 * j:128 * (j + 1)] = dxdt * dtp
            dh_scr[j] = cd * dhn + dhp
        dcm = dcm + _dot(dcb, bm)
        dbm = dbm + _dot_tn(dcb, cm)
        db_ref[...] = dbm
        dc_ref[...] = dcm
        dacs = dacs + dacs_t.T
        da = _dot_exact(tri.T, dacs) + dtot
        ddt_ref[0] = da * a_row + ddt_x
        da_ref[0] += jnp.sum(da * dt_blk, axis=0, keepdims=True)

    return pl.pallas_call(
        body, name=name, grid=(SSD_GROUPS, nc),
        in_specs=[pl.BlockSpec((CHUNK, 512), lambda g, c: (cidx(c), g)),
                  pl.BlockSpec((CHUNK, 128), lambda g, c: (cidx(c), 16 + g)),
                  pl.BlockSpec((CHUNK, 128), lambda g, c: (cidx(c), 20 + g)),
                  pl.BlockSpec((CHUNK, 128), lambda g, c: (cidx(c), d_off)),
                  pl.BlockSpec((1, 128), lambda g, c: (0, d_off)),
                  pl.BlockSpec((CHUNK, 512), lambda g, c: (cidx(c), g)),
                  pl.BlockSpec((1, 4, 128, 128), lambda g, c: (cidx(c), g, 0, 0))],
        out_specs=[pl.BlockSpec((CHUNK, 512), lambda g, c: (cidx(c), g)),
                   pl.BlockSpec((CHUNK, 128), lambda g, c: (cidx(c), g)),
                   pl.BlockSpec((CHUNK, 128), lambda g, c: (cidx(c), g)),
                   pl.BlockSpec((1, CHUNK, 128), lambda g, c: (g, cidx(c), 0)),
                   pl.BlockSpec((1, 1, 128), lambda g, c: (g, 0, 0))],
        out_shape=[jax.ShapeDtypeStruct((t, D_INNER), f32), jax.ShapeDtypeStruct((t, 512), f32),
                   jax.ShapeDtypeStruct((t, 512), f32), jax.ShapeDtypeStruct((4, t, 128), f32),
                   jax.ShapeDtypeStruct((4, 1, 128), f32)],
        scratch_shapes=[pltpu.VMEM((4, 128, 128), f32), pltpu.VMEM((CHUNK, CHUNK), f32)],
        compiler_params=_cparams(("parallel", "arbitrary")))(xbc_c, xbc_c, xbc_c, dt2, a_rows, dy, hprev)


def tail_fwd(y_f, y_b, xbc_c, z, dskip_row, nw_row, tb=512):
    t = y_f.shape[0]

    def body(yf_ref, yb_ref, x_ref, z_ref, d_ref, w_ref, o_ref):
        zz = z_ref[...]
        y = (yf_ref[...] + yb_ref[...] + d_ref[...] * x_ref[...]) * (zz * _sigmoid(zz))
        rstd = lax.rsqrt(jnp.mean(y * y, axis=1, keepdims=True) + NORM_EPS)
        o_ref[...] = (y * rstd * w_ref[...]).astype(o_ref.dtype)

    blk = pl.BlockSpec((tb, 512), lambda i, g: (i, g))
    row = pl.BlockSpec((1, 512), lambda i, g: (0, g))
    return pl.pallas_call(
        body, name="tail_fwd", grid=(t // tb, SSD_GROUPS), in_specs=[blk, blk, blk, blk, row, row], out_specs=blk,
        out_shape=jax.ShapeDtypeStruct((t, D_INNER), bf16),
        compiler_params=_cparams(("parallel", "parallel")))(y_f, y_b, xbc_c, z, dskip_row, nw_row)


def tail_bwd(dyn, y_f, y_b, xbc_c, z, dskip_row, nw_row, tb=512):
    t = y_f.shape[0]

    def body(g_ref, yf_ref, yb_ref, x_ref, z_ref, d_ref, w_ref, dy_ref, dz_ref, dw_ref, dd_ref):
        zz = z_ref[...]
        sg = _sigmoid(zz)
        sl = zz * sg
        x = x_ref[...]
        y = yf_ref[...] + yb_ref[...] + d_ref[...] * x
        yz = y * sl
        rstd = lax.rsqrt(jnp.mean(yz * yz, axis=1, keepdims=True) + NORM_EPS)
        yhat = yz * rstd
        g = g_ref[...]
        dyhat = g * w_ref[...]
        dyz = rstd * (dyhat - yhat * jnp.mean(dyhat * yhat, axis=1, keepdims=True))
        dy = dyz * sl
        dy_ref[...] = dy
        dz_ref[...] = (dyz * y * sg * (1.0 + zz * (1.0 - sg))).astype(dz_ref.dtype)

        @pl.when(pl.program_id(1) == 0)
        def _():
            dw_ref[...] = jnp.zeros_like(dw_ref)
            dd_ref[...] = jnp.zeros_like(dd_ref)

        dw_ref[...] += jnp.sum(g * yhat, axis=0, keepdims=True)
        dd_ref[...] += jnp.sum(dy * x, axis=0, keepdims=True)

    blk = pl.BlockSpec((tb, 512), lambda g, i: (i, g))
    row = pl.BlockSpec((1, 512), lambda g, i: (0, g))
    return pl.pallas_call(
        body, name="tail_bwd", grid=(SSD_GROUPS, t // tb), in_specs=[blk, blk, blk, blk, blk, row, row],
        out_specs=[blk, blk, row, row],
        out_shape=[jax.ShapeDtypeStruct((t, D_INNER), f32), jax.ShapeDtypeStruct((t, D_INNER), bf16),
                   jax.ShapeDtypeStruct((1, D_INNER), f32), jax.ShapeDtypeStruct((1, D_INNER), f32)],
        compiler_params=_cparams(("parallel", "arbitrary")))(dyn, y_f, y_b, xbc_c, z, dskip_row, nw_row)


def _slopes(p):
    return [2.0 ** (-8.0 * (HEADS_PER_PATTERN * p + j + 1) / ATTN_HEADS) for j in range(HEADS_PER_PATTERN)]


def _win_specs(nq, col_of):
    return [pl.BlockSpec((64, 256), lambda r, i: (jnp.maximum(2 * i - 1, 0), col_of(r))),
            pl.BlockSpec((128, 256), lambda r, i: (i, col_of(r))),
            pl.BlockSpec((64, 256), lambda r, i: (jnp.minimum(2 * i + 2, 2 * nq - 1), col_of(r)))]


def _lane_head(shape):
    return lax.broadcasted_iota(jnp.int32, shape, 1) >> 6


def _stack_heads(m):
    lane_head = _lane_head(m.shape)
    return jnp.concatenate([jnp.where(lane_head == j, m, 0.0) for j in range(HEADS_PER_PATTERN)], axis=0)


def _unstack_heads(m4, n):
    lane_head = _lane_head((n, 256))
    out = jnp.where(lane_head == 0, m4[0:n], 0.0)
    for j in range(1, HEADS_PER_PATTERN):
        out = out + jnp.where(lane_head == j, m4[j * n:(j + 1) * n], 0.0)
    return out


def _head_cols(m, n):
    lane = lax.broadcasted_iota(jnp.int32, (n, 256), 1)
    return jnp.concatenate([jnp.sum(jnp.where(lane == ATTN_HEAD_DIM * j, m, 0.0), axis=1, keepdims=True)
                            for j in range(HEADS_PER_PATTERN)], axis=0)


def _score_bias(p, dil, by_key):
    slopes = np.asarray(_slopes(p), np.float32)
    if by_key:
        win = np.arange(256)[:, None]
        rel = np.arange(128)[None, :] - (win - 64)
    else:
        win = np.arange(256)[None, :]
        rel = win - 64 - np.arange(128)[:, None]
    band = np.abs(rel) <= 64
    out = []
    for first, last in ((False, False), (True, False), (False, True), (True, True)):
        ok = band & ~(first & (win < 64)) & ~(last & (win >= 192))
        pen = -slopes[:, None, None] * (np.abs(rel) * dil).astype(np.float32)[None]
        out.append(np.where(ok[None], pen, np.float32(NEG_BIG)).reshape(-1, rel.shape[1]))
    return jnp.asarray(np.stack(out), f32)


def _bias_spec(nq, rows, cols):
    return pl.BlockSpec((1, rows, cols), lambda r, i: ((i == 0).astype(jnp.int32) + 2 * (i == nq - 1).astype(jnp.int32), 0, 0))


def attn_fwd(q, k, v, p, dil, name):
    l = q.shape[0]
    nq = l // 128

    def body(q_ref, kp_ref, ko_ref, kn_ref, vp_ref, vo_ref, vn_ref, bias_ref, o_ref, lse_ref):
        kcat = jnp.concatenate([kp_ref[...], ko_ref[...], kn_ref[...]], axis=0)
        vcat = jnp.concatenate([vp_ref[...], vo_ref[...], vn_ref[...]], axis=0)
        s = _dot_nt(_stack_heads(q_ref[...] * 0.125), kcat) + bias_ref[0]
        m = jnp.max(s, axis=1, keepdims=True)
        pr = jnp.exp(s - m)
        den = jnp.sum(pr, axis=1, keepdims=True)
        o4 = _dot(pr, vcat) / den
        o_ref[...] = _unstack_heads(o4, 128)
        lse_ref[...] = _unstack_heads(jnp.broadcast_to(m + jnp.log(den), (512, 256)), 128)

    col = lambda r: r
    return pl.pallas_call(
        body, name=name, grid=(dil, nq),
        in_specs=[pl.BlockSpec((128, 256), lambda r, i: (i, r))] + _win_specs(nq, col) + _win_specs(nq, col)
        + [_bias_spec(nq, 512, 256)],
        out_specs=[pl.BlockSpec((128, 256), lambda r, i: (i, r))] * 2,
        out_shape=[jax.ShapeDtypeStruct(q.shape, f32)] * 2,
        compiler_params=_cparams(("parallel", "parallel")))(q, k, k, k, v, v, v, _score_bias(p, dil, False))


def attn_combine(os_, lses, tb=1024):
    t = os_[0].shape[0]

    def body(o0, o1, o2, l0, l1, l2, y_ref, lse_ref):
        a0, a1, a2 = l0[...], l1[...], l2[...]
        m = jnp.maximum(jnp.maximum(a0, a1), a2)
        e0, e1, e2 = jnp.exp(a0 - m), jnp.exp(a1 - m), jnp.exp(a2 - m)
        den = e0 + e1 + e2
        y_ref[...] = (e0 * o0[...] + e1 * o1[...] + e2 * o2[...]) / den
        lse_ref[...] = m + jnp.log(den)

    blk = pl.BlockSpec((tb, 256), lambda i: (i, 0))
    return pl.pallas_call(
        body, name="attn_combine", grid=(t // tb,), in_specs=[blk] * 6, out_specs=[blk, blk],
        out_shape=[jax.ShapeDtypeStruct((t, 256), f32)] * 2,
        compiler_params=_cparams(("parallel",)))(*os_, *lses)


def attn_delta(dy, y, tb=1024):
    t = dy.shape[0]

    def body(dy_ref, y_ref, d_ref):
        pr = dy_ref[...] * y_ref[...]
        lane_head = _lane_head(pr.shape)
        out = jnp.zeros_like(pr)
        for j in range(HEADS_PER_PATTERN):
            sj = jnp.sum(jnp.where(lane_head == j, pr, 0.0), axis=1, keepdims=True)
            out = out + jnp.where(lane_head == j, sj, 0.0)
        d_ref[...] = out

    blk = pl.BlockSpec((tb, 256), lambda i: (i, 0))
    return pl.pallas_call(body, name="attn_delta", grid=(t // tb,), in_specs=[blk, blk], out_specs=blk,
                          out_shape=jax.ShapeDtypeStruct((t, 256), f32),
                          compiler_params=_cparams(("parallel",)))(dy, y)


def attn_dq(q, k, v, dy, lse, delta, p, dil, name):
    l = q.shape[0]
    nq = l // 128

    def body(q_ref, kp_ref, ko_ref, kn_ref, vp_ref, vo_ref, vn_ref, dy_ref, lse_ref, d_ref, bias_ref, dq_ref):
        kcat = jnp.concatenate([kp_ref[...], ko_ref[...], kn_ref[...]], axis=0)
        vcat = jnp.concatenate([vp_ref[...], vo_ref[...], vn_ref[...]], axis=0)
        s = _dot_nt(_stack_heads(q_ref[...] * 0.125), kcat) + bias_ref[0]
        pr = jnp.exp(s - _head_cols(lse_ref[...], 128))
        dp = _dot_nt(_stack_heads(dy_ref[...]), vcat)
        ds = pr * (dp - _head_cols(d_ref[...], 128))
        dq_ref[...] = (_unstack_heads(_dot(ds, kcat), 128) * 0.125).astype(dq_ref.dtype)

    col = lambda r: r
    own = pl.BlockSpec((128, 256), lambda r, i: (i, r))
    return pl.pallas_call(
        body, name=name, grid=(dil, nq),
        in_specs=[own] + _win_specs(nq, col) + _win_specs(nq, col) + [own, own, own, _bias_spec(nq, 512, 256)],
        out_specs=own, out_shape=jax.ShapeDtypeStruct(q.shape, bf16),
        compiler_params=_cparams(("parallel", "parallel")))(q, k, k, k, v, v, v, dy, lse, delta, _score_bias(p, dil, False))


def attn_dkv(q, k, v, dy, lse, delta, p, dil, name):
    l = q.shape[0]
    nq = l // 128

    def body(qp_ref, qo_ref, qn_ref, gp_ref, go_ref, gn_ref, lp_ref, lo_ref, ln_ref, dp_ref, do_ref, dn_ref,
             k_ref, v_ref, bias_ref, dk_ref, dv_ref):
        cat = lambda a, b, c: jnp.concatenate([a[...], b[...], c[...]], axis=0)
        q4 = _stack_heads(cat(qp_ref, qo_ref, qn_ref) * 0.125)
        dy4 = _stack_heads(cat(gp_ref, go_ref, gn_ref))
        lse4 = _head_cols(cat(lp_ref, lo_ref, ln_ref), 256)
        del4 = _head_cols(cat(dp_ref, do_ref, dn_ref), 256)
        s = _dot_nt(q4, k_ref[...]) + bias_ref[0]
        pr = jnp.exp(s - lse4)
        dpm = _dot_nt(dy4, v_ref[...])
        ds = pr * (dpm - del4)
        dv_ref[...] = _dot_tn(pr, dy4).astype(dv_ref.dtype)
        dk_ref[...] = _dot_tn(ds, q4).astype(dk_ref.dtype)

    col = lambda r: r
    own = pl.BlockSpec((128, 256), lambda r, i: (i, r))
    win = _win_specs(nq, col)
    return pl.pallas_call(
        body, name=name, grid=(dil, nq), in_specs=win * 4 + [own, own, _bias_spec(nq, 1024, 128)], out_specs=[own, own],
        out_shape=[jax.ShapeDtypeStruct(q.shape, bf16)] * 2,
        compiler_params=_cparams(("parallel", "parallel")))(q, q, q, dy, dy, dy, lse, lse, lse, delta, delta, delta, k, v,
                                                            _score_bias(p, dil, True))


def _lanes(v, reps):
    return v if reps == 1 else jnp.tile(v, (1, reps))


def _lane_halo_specs(cb, tb, nt, off=0):
    r = tb // 128
    return [pl.BlockSpec((cb, 128), lambda j, i: (j + off, jnp.maximum(i * r - 1, 0))),
            pl.BlockSpec((cb, tb), lambda j, i: (j + off, i)),
            pl.BlockSpec((cb, 128), lambda j, i: (j + off, jnp.minimum((i + 1) * r, nt * r - 1)))]


def _with_lane_halo(prev_ref, own_ref, next_ref, i, nt):
    prev = jnp.where(i > 0, prev_ref[...].astype(f32), 0.0)
    nxt = jnp.where(i < nt - 1, next_ref[...].astype(f32), 0.0)
    return jnp.concatenate([prev, own_ref[...].astype(f32), nxt], axis=1)


def _lane_shifted(xcat, s, tb):
    n = xcat.shape[1]
    return pltpu.roll(xcat, (-s) % n, 1)[:, 128:128 + tb]


def conv_fwd_t(xbc_t, w_b, b_b, tb=1024, cb=256):
    c, t = xbc_t.shape
    nt = t // tb

    def body(prev_ref, own_ref, next_ref, w_ref, b_ref, o_ref, ds_ref):
        i = pl.program_id(1)
        xcat = _with_lane_halo(prev_ref, own_ref, next_ref, i, nt)
        reps = tb // 128
        pre = _lanes(b_ref[...], reps)
        for k in range(D_CONV):
            pre = pre + _lanes(w_ref[k], reps) * _lane_shifted(xcat, k - 2, tb)
        sg = _sigmoid(pre)
        o_ref[...] = pre * sg
        ds_ref[...] = sg * (1.0 + pre * (1.0 - sg))

    blk = pl.BlockSpec((cb, tb), lambda j, i: (j, i))
    return pl.pallas_call(
        body, name="conv_fwd", grid=(c // cb, nt),
        in_specs=_lane_halo_specs(cb, tb, nt) + [pl.BlockSpec((D_CONV, cb, 128), lambda j, i: (0, j, 0)),
                                                 pl.BlockSpec((cb, 128), lambda j, i: (j, 0))],
        out_specs=[blk, blk], out_shape=[jax.ShapeDtypeStruct((c, t), f32)] * 2,
        compiler_params=_cparams(("parallel", "parallel")))(xbc_t, xbc_t, xbc_t, w_b, b_b)


def conv_bwd_t(xbc_t, dsilu_t, grad_t, w_b, into, name, row0, tb=1024, cb=256):
    c, t = grad_t.shape
    nt = t // tb
    off = row0 // cb
    off_out = (D_INNER + row0) // cb
    reps = tb // 128

    def body(*refs):
        i = pl.program_id(1)
        x_ref, sr, gr = refs[0], refs[1:4], refs[4:7]
        w_ref = refs[7]
        dx_ref, dw_ref, db_ref = refs[-3:]
        wk = [_lanes(w_ref[k], reps) for k in range(D_CONV)]
        dpre = _with_lane_halo(*gr, i, nt) * _with_lane_halo(*sr, i, nt)

        def fold(v):
            s = v[:, 0:128]
            for q in range(1, reps):
                s = s + v[:, 128 * q:128 * (q + 1)]
            return s

        @pl.when(i == 0)
        def _():
            dw_ref[...] = jnp.zeros_like(dw_ref)
            db_ref[...] = jnp.zeros_like(db_ref)

        x_own = x_ref[...]
        dx = None
        for k in range(D_CONV):
            shifted = _lane_shifted(dpre, 2 - k, tb)
            term = wk[k] * shifted
            dx = term if dx is None else dx + term
            dw_ref[k] += fold(shifted * x_own)
        dx_ref[...] = dx.astype(dx_ref.dtype)
        db_ref[...] += fold(dpre[:, 128:128 + tb])

    in_specs = ([pl.BlockSpec((cb, tb), lambda j, i: (j + off, i))] + _lane_halo_specs(cb, tb, nt, off)
                + _lane_halo_specs(cb, tb, nt)
                + [pl.BlockSpec((D_CONV, cb, 128), lambda j, i: (0, j + off, 0)), pl.BlockSpec(memory_space=pl.ANY)])
    args = [xbc_t] + [dsilu_t] * 3 + [grad_t] * 3 + [w_b, into]
    return pl.pallas_call(
        body, name=name, grid=(c // cb, nt), in_specs=in_specs,
        out_specs=[pl.BlockSpec((cb, tb), lambda j, i: (j + off_out, i)),
                   pl.BlockSpec((D_CONV, cb, 128), lambda j, i: (0, j, 0)), pl.BlockSpec((cb, 128), lambda j, i: (j, 0))],
        out_shape=[jax.ShapeDtypeStruct(into.shape, into.dtype), jax.ShapeDtypeStruct((D_CONV, c, 128), f32),
                   jax.ShapeDtypeStruct((c, 128), f32)],
        input_output_aliases={8: 0}, compiler_params=_cparams(("parallel", "arbitrary")))(*args)


def dt_fwd_t(u_dt_t, bias_b, tb=2048):
    r, t = u_dt_t.shape

    def body(u_ref, b_ref, o_ref):
        v = u_ref[...] + _lanes(b_ref[...], tb // 128)
        o_ref[...] = jnp.maximum(v, 0.0) + jnp.log(1.0 + jnp.exp(-jnp.abs(v)))

    return pl.pallas_call(
        body, name="dt_fwd", grid=(t // tb,),
        in_specs=[pl.BlockSpec((r, tb), lambda i: (0, i)), pl.BlockSpec((r, 128), lambda i: (0, 0))],
        out_specs=pl.BlockSpec((r, tb), lambda i: (0, i)), out_shape=jax.ShapeDtypeStruct((r, t), f32),
        compiler_params=_cparams(("parallel",)))(u_dt_t, bias_b)


def dt_bwd_t(ddt_f, ddt_b, u_dt_t, bias_b, into, tb=2048):
    r, t = u_dt_t.shape
    reps = tb // 128
    row_blk = (SSD_COLS - r) // r

    def body(gf_ref, gb_ref, u_ref, b_ref, into_ref, du_ref, db_ref):
        g = jnp.concatenate([gf_ref[...], gb_ref[...]], axis=0)
        du = g * _sigmoid(u_ref[...] + _lanes(b_ref[...], reps))
        du_ref[...] = du.astype(du_ref.dtype)

        @pl.when(pl.program_id(0) == 0)
        def _():
            db_ref[...] = jnp.zeros_like(db_ref)

        s = du[:, 0:128]
        for q in range(1, reps):
            s = s + du[:, 128 * q:128 * (q + 1)]
        db_ref[...] += s

    half = pl.BlockSpec((r // 2, tb), lambda i: (0, i))
    return pl.pallas_call(
        body, name="dt_bwd", grid=(t // tb,),
        in_specs=[half, half, pl.BlockSpec((r, tb), lambda i: (0, i)), pl.BlockSpec((r, 128), lambda i: (0, 0)),
                  pl.BlockSpec(memory_space=pl.ANY)],
        out_specs=[pl.BlockSpec((r, tb), lambda i: (row_blk, i)), pl.BlockSpec((r, 128), lambda i: (0, 0))],
        out_shape=[jax.ShapeDtypeStruct(into.shape, into.dtype), jax.ShapeDtypeStruct((r, 128), f32)],
        input_output_aliases={4: 0}, compiler_params=_cparams(("arbitrary",)))(ddt_f, ddt_b, u_dt_t, bias_b, into)


HEADS_PER_GROUP = SSD_HEADS // SSD_GROUPS


def _group_rows(g, n):
    return pl.ds(pl.multiple_of(g * n, n), n)


def _ssd_decays(dt_blk, a_blk, reverse):
    row = lax.broadcasted_iota(jnp.int32, (CHUNK, CHUNK), 0)
    col = lax.broadcasted_iota(jnp.int32, (CHUNK, CHUNK), 1)
    mask = (row <= col) if reverse else (row >= col)
    tri = mask.astype(f32)
    a8 = dt_blk * a_blk
    a = jnp.concatenate([a8, jnp.zeros((CHUNK - HEADS_PER_GROUP, CHUNK), f32)], axis=0).T
    acs = _dot_exact(tri, a)
    return mask, tri, a8, acs, acs.T, col


def ssd_fwd_t(xbc_ct, dt_t, a_b, reverse, name, prev=None, tail=None):
    t = xbc_ct.shape[1]
    nc = t // CHUNK
    direction = 1 if reverse else 0

    def cidx(c):
        return nc - 1 - c if reverse else c

    def body(*refs):
        x_ref, b_ref, c_ref, dt_ref, a_ref = refs[0:5]
        pos = 5
        prev_ref = None
        if prev is not None:
            prev_ref = refs[pos]
            pos += 1
        if tail is not None:
            z_ref, skip_ref, nw_ref = refs[pos:pos + 3]
            pos += 3
            y_ref, hp_ref, yn_ref, h_scr = refs[pos:pos + 4]
        else:
            y_ref, hp_ref, h_scr = refs[pos:pos + 3]

        @pl.when(pl.program_id(0) == 0)
        def _():
            h_scr[...] = jnp.zeros_like(h_scr)

        def group(g, carry):
            x_v, y_v = x_ref.at[_group_rows(g, 512)], y_ref.at[_group_rows(g, 512)]
            heads = _group_rows(g, HEADS_PER_GROUP)
            hp_v, h_v = hp_ref.at[0, heads], h_scr.at[heads]
            dt_blk = dt_ref[heads, :]
            mask, tri, a8, acs, acs_t, lane = _ssd_decays(dt_blk, a_ref[heads, :], reverse)
            bm = b_ref[_group_rows(g, 128), :].T
            cm = c_ref[_group_rows(g, 128), :].T
            cb = _dot_nt(cm, bm)
            tot = jnp.sum(a8, axis=1, keepdims=True)
            for j in range(HEADS_PER_GROUP):
                rows = slice(SSD_HEAD_DIM * j, SSD_HEAD_DIM * (j + 1))
                col_j = _lane_col(acs, lane, j)
                row_j = acs_t[j:j + 1, :]
                lmat = jnp.where(mask, jnp.exp(jnp.where(mask, col_j - row_j, 0.0)), 0.0)
                xdt = x_v[rows, :] * dt_blk[j:j + 1, :]
                hp = h_v[j]
                hp_v[j] = hp
                y = _dot_nt(xdt, cb * lmat) + _dot_nt(hp, cm) * jnp.exp(row_j)
                if prev_ref is not None:
                    y = y + prev_ref.at[_group_rows(g, 512)][rows, :]
                y_v[rows, :] = y
                tot_j = tot[j:j + 1, :]
                h_v[j] = jnp.exp(tot_j) * hp + _dot(xdt * jnp.exp(tot_j - row_j), bm)
            if tail is not None:
                rows = _group_rows(g, 512)
                zz = z_ref[rows, :]
                yg = (y_v[...] + skip_ref[rows, :] * x_v[...]) * (zz * _sigmoid(zz))
                rstd = lax.rsqrt(jnp.mean(yg * yg, axis=0, keepdims=True) + NORM_EPS)
                yn_ref[rows, :] = (yg * rstd * nw_ref[rows, :]).astype(yn_ref.dtype)
            return carry

        lax.fori_loop(0, SSD_GROUPS, group, 0)

    big = pl.BlockSpec((D_INNER, CHUNK), lambda c: (0, cidx(c)))
    par = pl.BlockSpec((D_INNER, 128), lambda c: (0, 0))
    in_specs = [big, pl.BlockSpec((512, CHUNK), lambda c: (4, cidx(c))), pl.BlockSpec((512, CHUNK), lambda c: (5, cidx(c))),
                pl.BlockSpec((SSD_HEADS, CHUNK), lambda c: (direction, cidx(c))),
                pl.BlockSpec((SSD_HEADS, 128), lambda c: (direction, 0))]
    args = [xbc_ct, xbc_ct, xbc_ct, dt_t, a_b]
    out_specs = [big, pl.BlockSpec((1, SSD_HEADS, SSD_HEAD_DIM, D_STATE), lambda c: (cidx(c), 0, 0, 0))]
    out_shape = [jax.ShapeDtypeStruct((D_INNER, t), f32), jax.ShapeDtypeStruct((nc, SSD_HEADS, SSD_HEAD_DIM, D_STATE), f32)]
    if prev is not None:
        in_specs.append(big)
        args.append(prev)
    if tail is not None:
        in_specs += [big, par, par]
        args += list(tail)
        out_specs.append(big)
        out_shape.append(jax.ShapeDtypeStruct((D_INNER, t), bf16))
    return pl.pallas_call(
        body, name=name, grid=(nc,), in_specs=in_specs, out_specs=out_specs, out_shape=out_shape,
        scratch_shapes=[pltpu.VMEM((SSD_HEADS, SSD_HEAD_DIM, D_STATE), f32)],
        compiler_params=_cparams(("arbitrary",)))(*args)


def ssd_bwd_t(xbc_ct, dt_t, a_b, dy_t, hprev, reverse, name, skip_b=None, prev=None, tail=None):
    t = xbc_ct.shape[1]
    nc = t // CHUNK
    direction = 1 if reverse else 0

    def cidx(c):
        return c if reverse else nc - 1 - c

    def body(*refs):
        x_ref, b_ref, c_ref, dt_ref, a_ref, dy_ref, hp_ref = refs[0:7]
        pos = 7
        skip_ref = None
        if skip_b is not None:
            skip_ref = refs[pos]
            pos += 1
        prev_refs = None
        if prev is not None:
            prev_refs = refs[pos:pos + 3]
            pos += 3
        if tail is not None:
            ys_ref, z_ref, nw_ref = refs[pos:pos + 3]
            pos += 3
        dx_ref, db_ref, dc_ref, ddt_ref, da_ref = refs[pos:pos + 5]
        pos += 5
        if tail is not None:
            dyout_ref, dz_ref, dnw_ref, ddx_ref = refs[pos:pos + 4]
            pos += 4
        dh_scr = refs[pos]

        @pl.when(pl.program_id(0) == 0)
        def _():
            dh_scr[...] = jnp.zeros_like(dh_scr)
            da_ref[...] = jnp.zeros_like(da_ref)
            if tail is not None:
                dnw_ref[...] = jnp.zeros_like(dnw_ref)
                ddx_ref[...] = jnp.zeros_like(ddx_ref)

        def group(g, carry):
            big, st, heads = _group_rows(g, 512), _group_rows(g, 128), _group_rows(g, HEADS_PER_GROUP)
            x_v, dy_v, dx_v = x_ref.at[big], dy_ref.at[big], dx_ref.at[big]
            hp_v, dh_v = hp_ref.at[0, heads], dh_scr.at[heads]
            dy_grp = None
            if tail is not None:
                zz = z_ref[big, :]
                sg = _sigmoid(zz)
                sl = zz * sg
                x_all = x_v[...]
                y = ys_ref[big, :] + skip_ref[big, :] * x_all
                yz = y * sl
                rstd = lax.rsqrt(jnp.mean(yz * yz, axis=0, keepdims=True) + NORM_EPS)
                yhat = yz * rstd
                gy = dy_v[...]
                dyhat = gy * nw_ref[big, :]
                dyz = rstd * (dyhat - yhat * jnp.mean(dyhat * yhat, axis=0, keepdims=True))
                dy_grp = dyz * sl
                dyout_ref[big, :] = dy_grp
                dz_ref[big, :] = (dyz * y * sg * (1.0 + zz * (1.0 - sg))).astype(dz_ref.dtype)
                dnw_ref[big, :] += gy * yhat
                ddx_ref[big, :] += dy_grp * x_all
            dt_blk = dt_ref[heads, :]
            a_blk = a_ref[heads, :]
            mask, tri, a8, acs, acs_t, lane = _ssd_decays(dt_blk, a_blk, reverse)
            sub = lax.broadcasted_iota(jnp.int32, (CHUNK, CHUNK), 0)
            mask_t = (sub >= lane) if reverse else (sub <= lane)
            bm = b_ref[st, :].T
            cm = c_ref[st, :].T
            cb = _dot_nt(cm, bm)
            cb_t = _dot_nt(bm, cm)
            tot = jnp.sum(a8, axis=1, keepdims=True)
            dcb = jnp.zeros((CHUNK, CHUNK), f32)
            dbm = jnp.zeros((CHUNK, D_STATE), f32)
            dcm = jnp.zeros((CHUNK, D_STATE), f32)
            dacs_rows, ddtx_rows = [], []
            for j in range(HEADS_PER_GROUP):
                rows = slice(SSD_HEAD_DIM * j, SSD_HEAD_DIM * (j + 1))
                col_j = _lane_col(acs, lane, j)
                row_j = acs_t[j:j + 1, :]
                dt_j = dt_blk[j:j + 1, :]
                tot_j = tot[j:j + 1, :]
                lmat = jnp.where(mask, jnp.exp(jnp.where(mask, col_j - row_j, 0.0)), 0.0)
                lmat_t = jnp.where(mask_t, jnp.exp(jnp.where(mask_t, row_j - col_j, 0.0)), 0.0)
                x = x_v[rows, :]
                xdt = x * dt_j
                dyh = dy_v[rows, :] if dy_grp is None else dy_grp[rows]
                hp = hp_v[j]
                dhn = dh_v[j]
                ml = _dot_tn(dyh, xdt) * lmat
                w_t = _dot_tn(xdt, dyh) * lmat_t * cb_t
                dcb = dcb + ml
                dacs = jnp.sum(w_t, axis=0, keepdims=True) - jnp.sum(ml * cb, axis=0, keepdims=True)
                ecol = jnp.exp(row_j)
                dec = jnp.exp(tot_j - row_j)
                dye = dyh * ecol
                yoff = _dot_nt(hp, cm) * ecol
                gmat = _dot_nt(dhn, bm)
                dxdt = _dot(dyh, cb * lmat) + dec * gmat
                s_dec = jnp.sum(xdt * gmat, axis=0, keepdims=True) * dec
                dacs = dacs + jnp.sum(dyh * yoff, axis=0, keepdims=True) - s_dec
                dcd = jnp.sum(jnp.sum(dhn * hp, axis=1, keepdims=True), axis=0, keepdims=True)
                dtot = jnp.sum(s_dec, axis=1, keepdims=True) + jnp.exp(tot_j) * dcd
                dacs_rows.append((dacs, dtot))
                ddtx_rows.append(jnp.sum(dxdt * x, axis=0, keepdims=True))
                dcm = dcm + _dot_tn(dye, hp)
                dbm = dbm + _dot_tn(xdt * dec, dhn)
                dxh = dxdt * dt_j
                if skip_ref is not None:
                    dxh = dxh + skip_ref.at[big][rows, :] * dyh
                if prev_refs is not None:
                    dxh = dxh + prev_refs[0].at[big][rows, :]
                dx_v[rows, :] = dxh
                dh_v[j] = jnp.exp(tot_j) * dhn + _dot(dye, cm)
            dcm = dcm + _dot(dcb, bm)
            dbm = dbm + _dot_tn(dcb, cm)
            dbt, dct = dbm.T, dcm.T
            if prev_refs is not None:
                dbt = dbt + prev_refs[1][st, :]
                dct = dct + prev_refs[2][st, :]
            db_ref[st, :] = dbt
            dc_ref[st, :] = dct
            dacs8 = jnp.concatenate([d for d, _ in dacs_rows], axis=0)
            dtot8 = jnp.concatenate([d for _, d in dacs_rows], axis=0)
            da8 = _dot_exact(dacs8, tri) + dtot8
            ddt_ref[heads, :] = da8 * a_blk + jnp.concatenate(ddtx_rows, axis=0)
            da_ref[heads, :] += da8 * dt_blk
            return carry

        lax.fori_loop(0, SSD_GROUPS, group, 0)

    big = pl.BlockSpec((D_INNER, CHUNK), lambda c: (0, cidx(c)))
    st = pl.BlockSpec((512, CHUNK), lambda c: (0, cidx(c)))
    in_specs = [big, pl.BlockSpec((512, CHUNK), lambda c: (4, cidx(c))), pl.BlockSpec((512, CHUNK), lambda c: (5, cidx(c))),
                pl.BlockSpec((SSD_HEADS, CHUNK), lambda c: (direction, cidx(c))),
                pl.BlockSpec((SSD_HEADS, 128), lambda c: (direction, 0)), big,
                pl.BlockSpec((1, SSD_HEADS, SSD_HEAD_DIM, D_STATE), lambda c: (cidx(c), 0, 0, 0))]
    args = [xbc_ct, xbc_ct, xbc_ct, dt_t, a_b, dy_t, hprev]
    if skip_b is not None:
        in_specs.append(pl.BlockSpec((D_INNER, 128), lambda c: (0, 0)))
        args.append(skip_b)
    if prev is not None:
        in_specs += [big, st, st]
        args += list(prev)
    par = pl.BlockSpec((D_INNER, 128), lambda c: (0, 0))
    out_specs = [big, st, st, pl.BlockSpec((SSD_HEADS, CHUNK), lambda c: (0, cidx(c))),
                 pl.BlockSpec((SSD_HEADS, 128), lambda c: (0, 0))]
    out_shape = [jax.ShapeDtypeStruct((D_INNER, t), f32), jax.ShapeDtypeStruct((512, t), f32),
                 jax.ShapeDtypeStruct((512, t), f32), jax.ShapeDtypeStruct((SSD_HEADS, t), f32),
                 jax.ShapeDtypeStruct((SSD_HEADS, 128), f32)]
    if tail is not None:
        in_specs += [big, big, par]
        args += list(tail)
        out_specs += [big, big, par, par]
        out_shape += [jax.ShapeDtypeStruct((D_INNER, t), f32), jax.ShapeDtypeStruct((SSD_COLS, t), bf16),
                      jax.ShapeDtypeStruct((D_INNER, 128), f32), jax.ShapeDtypeStruct((D_INNER, 128), f32)]
    return pl.pallas_call(
        body, name=name, grid=(nc,), in_specs=in_specs, out_specs=out_specs, out_shape=out_shape,
        scratch_shapes=[pltpu.VMEM((SSD_HEADS, SSD_HEAD_DIM, D_STATE), f32)],
        compiler_params=_cparams(("arbitrary",)))(*args)


def tail_fwd_t(y_scan, xbc_ct, z_t, skip_b, nw_b, tb=512):
    t = y_scan.shape[1]
    reps = tb // 128

    def body(ys_ref, x_ref, z_ref, d_ref, w_ref, o_ref):
        zz = z_ref[...]
        y = (ys_ref[...] + _lanes(d_ref[...], reps) * x_ref[...]) * (zz * _sigmoid(zz))
        rstd = lax.rsqrt(jnp.mean(y * y, axis=0, keepdims=True) + NORM_EPS)
        o_ref[...] = (y * rstd * _lanes(w_ref[...], reps)).astype(o_ref.dtype)

    blk = pl.BlockSpec((512, tb), lambda g, i: (g, i))
    par = pl.BlockSpec((512, 128), lambda g, i: (g, 0))
    return pl.pallas_call(
        body, name="tail_fwd", grid=(SSD_GROUPS, t // tb), in_specs=[blk, blk, blk, par, par], out_specs=blk,
        out_shape=jax.ShapeDtypeStruct((D_INNER, t), bf16),
        compiler_params=_cparams(("parallel", "parallel")))(y_scan, xbc_ct, z_t, skip_b, nw_b)


def tail_bwd_t(dyn_t, y_scan, xbc_ct, z_t, skip_b, nw_b, tb=512):
    t = y_scan.shape[1]
    reps = tb // 128

    def body(g_ref, ys_ref, x_ref, z_ref, d_ref, w_ref, dy_ref, dz_ref, dw_ref, dd_ref):
        zz = z_ref[...]
        sg = _sigmoid(zz)
        sl = zz * sg
        x = x_ref[...]
        y = ys_ref[...] + _lanes(d_ref[...], reps) * x
        yz = y * sl
        rstd = lax.rsqrt(jnp.mean(yz * yz, axis=0, keepdims=True) + NORM_EPS)
        yhat = yz * rstd
        g = g_ref[...]
        dyhat = g * _lanes(w_ref[...], reps)
        dyz = rstd * (dyhat - yhat * jnp.mean(dyhat * yhat, axis=0, keepdims=True))
        dy = dyz * sl
        dy_ref[...] = dy
        dz_ref[...] = (dyz * y * sg * (1.0 + zz * (1.0 - sg))).astype(dz_ref.dtype)

        def fold(v):
            s = v[:, 0:128]
            for q in range(1, reps):
                s = s + v[:, 128 * q:128 * (q + 1)]
            return s

        @pl.when(pl.program_id(1) == 0)
        def _():
            dw_ref[...] = jnp.zeros_like(dw_ref)
            dd_ref[...] = jnp.zeros_like(dd_ref)

        dw_ref[...] += fold(g * yhat)
        dd_ref[...] += fold(dy * x)

    blk = pl.BlockSpec((512, tb), lambda g, i: (g, i))
    par = pl.BlockSpec((512, 128), lambda g, i: (g, 0))
    return pl.pallas_call(
        body, name="tail_bwd", grid=(SSD_GROUPS, t // tb), in_specs=[blk, blk, blk, blk, par, par],
        out_specs=[blk, blk, par, par],
        out_shape=[jax.ShapeDtypeStruct((D_INNER, t), f32), jax.ShapeDtypeStruct((SSD_COLS, t), bf16),
                   jax.ShapeDtypeStruct((D_INNER, 128), f32), jax.ShapeDtypeStruct((D_INNER, 128), f32)],
        compiler_params=_cparams(("parallel", "arbitrary")))(dyn_t, y_scan, xbc_ct, z_t, skip_b, nw_b)


def merge_fwd(u_gate, bg_row, y_ssd, y_att, tb=512):
    t = y_ssd.shape[0]

    def body(ga_ref, gb_ref, ba_ref, bb_ref, ys_ref, ya_ref, o_ref):
        o_ref[...] = (_sigmoid(ga_ref[...] + ba_ref[...]) * ys_ref[...]
                      + _sigmoid(gb_ref[...] + bb_ref[...]) * ya_ref[...]).astype(o_ref.dtype)

    blk = pl.BlockSpec((tb, 512), lambda i, j: (i, j))
    blk2 = pl.BlockSpec((tb, 512), lambda i, j: (i, 2 + j))
    row = pl.BlockSpec((1, 512), lambda i, j: (0, j))
    row2 = pl.BlockSpec((1, 512), lambda i, j: (0, 2 + j))
    return pl.pallas_call(
        body, name="merge_fwd", grid=(t // tb, 2), in_specs=[blk, blk2, row, row2, blk, blk], out_specs=blk,
        out_shape=jax.ShapeDtypeStruct((t, D_MODEL), bf16),
        compiler_params=_cparams(("parallel", "parallel")))(u_gate, u_gate, bg_row, bg_row, y_ssd, y_att)


def merge_bwd(dm, u_gate, bg_row, y_ssd, y_att, tb=512):
    t = dm.shape[0]

    def body(dm_ref, ga_ref, gb_ref, ba_ref, bb_ref, ys_ref, ya_ref, dys_ref, dya_ref, dga_ref, dgb_ref, dba_ref, dbb_ref):
        d = dm_ref[...]
        sa = _sigmoid(ga_ref[...] + ba_ref[...])
        sb = _sigmoid(gb_ref[...] + bb_ref[...])
        dys_ref[...] = (d * sa).astype(dys_ref.dtype)
        dya_ref[...] = (d * sb).astype(dya_ref.dtype)
        dla = d * ys_ref[...] * sa * (1.0 - sa)
        dlb = d * ya_ref[...] * sb * (1.0 - sb)
        dga_ref[...] = dla.astype(dga_ref.dtype)
        dgb_ref[...] = dlb.astype(dgb_ref.dtype)

        @pl.when(pl.program_id(1) == 0)
        def _():
            dba_ref[...] = jnp.zeros_like(dba_ref)
            dbb_ref[...] = jnp.zeros_like(dbb_ref)

        dba_ref[...] += jnp.sum(dla, axis=0, keepdims=True)
        dbb_ref[...] += jnp.sum(dlb, axis=0, keepdims=True)

    blk = pl.BlockSpec((tb, 512), lambda j, i: (i, j))
    blk2 = pl.BlockSpec((tb, 512), lambda j, i: (i, 2 + j))
    row = pl.BlockSpec((1, 512), lambda j, i: (0, j))
    row2 = pl.BlockSpec((1, 512), lambda j, i: (0, 2 + j))
    act = jax.ShapeDtypeStruct((t, D_MODEL), bf16)
    vec = jax.ShapeDtypeStruct((1, D_MODEL), f32)
    return pl.pallas_call(
        body, name="merge_bwd", grid=(2, t // tb), in_specs=[blk, blk, blk2, row, row2, blk, blk],
        out_specs=[blk, blk, blk, blk, row, row], out_shape=[act, act, act, act, vec, vec],
        compiler_params=_cparams(("parallel", "arbitrary")))(dm, u_gate, u_gate, bg_row, bg_row, y_ssd, y_att)


def _ln_stats(r):
    mu = jnp.mean(r, axis=1, keepdims=True)
    xc = r - mu
    rstd = lax.rsqrt(jnp.mean(xc * xc, axis=1, keepdims=True) + NORM_EPS)
    return xc * rstd, rstd


def _ln_bwd(dy, xhat, rstd, g_row):
    dxh = dy * g_row
    return rstd * (dxh - jnp.mean(dxh, axis=1, keepdims=True) - xhat * jnp.mean(dxh * xhat, axis=1, keepdims=True))


def ln1_fwd(x, mix, g_row, b_row, tb=512):
    t = x.shape[0]

    def body(x_ref, m_ref, g_ref, b_ref, o_ref, ob_ref):
        xhat, _ = _ln_stats(ALPHA * x_ref[...] + m_ref[...])
        h = xhat * g_ref[...] + b_ref[...]
        o_ref[...] = h
        ob_ref[...] = h.astype(ob_ref.dtype)

    blk = pl.BlockSpec((tb, D_MODEL), lambda i: (i, 0))
    row = pl.BlockSpec((1, D_MODEL), lambda i: (0, 0))
    return pl.pallas_call(body, name="ln1_fwd", grid=(t // tb,), in_specs=[blk, blk, row, row], out_specs=[blk, blk],
                          out_shape=[jax.ShapeDtypeStruct((t, D_MODEL), f32), jax.ShapeDtypeStruct((t, D_MODEL), bf16)],
                          compiler_params=_cparams(("parallel",)))(x, mix, g_row, b_row)


def ln1_bwd(dh, x, mix, g_row, tb=512):
    t = x.shape[0]

    def body(dh_ref, x_ref, m_ref, g_ref, dr_ref, drb_ref, dg_ref, db_ref):
        xhat, rstd = _ln_stats(ALPHA * x_ref[...] + m_ref[...])
        dy = dh_ref[...]
        dr = _ln_bwd(dy, xhat, rstd, g_ref[...])
        dr_ref[...] = dr
        drb_ref[...] = dr.astype(drb_ref.dtype)

        @pl.when(pl.program_id(0) == 0)
        def _():
            dg_ref[...] = jnp.zeros_like(dg_ref)
            db_ref[...] = jnp.zeros_like(db_ref)

        dg_ref[...] += jnp.sum(dy * xhat, axis=0, keepdims=True)
        db_ref[...] += jnp.sum(dy, axis=0, keepdims=True)

    blk = pl.BlockSpec((tb, D_MODEL), lambda i: (i, 0))
    row = pl.BlockSpec((1, D_MODEL), lambda i: (0, 0))
    return pl.pallas_call(
        body, name="ln1_bwd", grid=(t // tb,), in_specs=[blk, blk, blk, row], out_specs=[blk, blk, row, row],
        out_shape=[jax.ShapeDtypeStruct((t, D_MODEL), f32), jax.ShapeDtypeStruct((t, D_MODEL), bf16),
                   jax.ShapeDtypeStruct((1, D_MODEL), f32), jax.ShapeDtypeStruct((1, D_MODEL), f32)],
        compiler_params=_cparams(("arbitrary",)))(dh, x, mix, g_row)


def ln2_loss(h1, f, g_row, b_row, target, tb=512):
    t = h1.shape[0]

    def body(h_ref, f_ref, g_ref, b_ref, t_ref, dr_ref, drb_ref, dg_ref, db_ref, loss_ref):
        xhat, rstd = _ln_stats(ALPHA * h_ref[...] + f_ref[...])
        g = g_ref[...]
        err = xhat * g + b_ref[...] - t_ref[...]
        dy = err * (1.0 / D_MODEL)
        dr = _ln_bwd(dy, xhat, rstd, g)
        dr_ref[...] = dr
        drb_ref[...] = dr.astype(drb_ref.dtype)

        @pl.when(pl.program_id(0) == 0)
        def _():
            dg_ref[...] = jnp.zeros_like(dg_ref)
            db_ref[...] = jnp.zeros_like(db_ref)
            loss_ref[...] = jnp.zeros_like(loss_ref)

        dg_ref[...] += jnp.sum(dy * xhat, axis=0, keepdims=True)
        db_ref[...] += jnp.sum(dy, axis=0, keepdims=True)
        part = jnp.sum(jnp.mean(err * err, axis=1, keepdims=True), axis=0, keepdims=True)
        loss_ref[...] += 0.5 * part

    blk = pl.BlockSpec((tb, D_MODEL), lambda i: (i, 0))
    row = pl.BlockSpec((1, D_MODEL), lambda i: (0, 0))
    return pl.pallas_call(
        body, name="ln2_loss", grid=(t // tb,), in_specs=[blk, blk, row, row, blk],
        out_specs=[blk, blk, row, row, pl.BlockSpec((8, 128), lambda i: (0, 0))],
        out_shape=[jax.ShapeDtypeStruct((t, D_MODEL), f32), jax.ShapeDtypeStruct((t, D_MODEL), bf16),
                   jax.ShapeDtypeStruct((1, D_MODEL), f32), jax.ShapeDtypeStruct((1, D_MODEL), f32),
                   jax.ShapeDtypeStruct((8, 128), f32)],
        compiler_params=_cparams(("arbitrary",)))(h1, f, g_row, b_row, target)


TAIL_BLOCK, TAIL_AT = divmod(OFF_TAIL, PACK_TILE)


def _sum4(ref):
    return ((ref[0].astype(f32) + ref[1].astype(f32)) + ref[2].astype(f32)) + ref[3].astype(f32)


def _adamw_update(g, w_ref, m_ref, v_ref, g_ref, d_ref, nm_ref, nv_ref):
    c1 = 1.0 - ADAM_B1 ** ADAM_STEP
    c2 = 1.0 - ADAM_B2 ** ADAM_STEP
    nm = ADAM_B1 * m_ref[...] + (1.0 - ADAM_B1) * g
    nv = ADAM_B2 * v_ref[...] + (1.0 - ADAM_B2) * (g * g)
    g_ref[...] = g
    nm_ref[...] = nm
    nv_ref[...] = nv
    d_ref[...] = -ADAM_LR * ((nm / c1) / (jnp.sqrt(nv / c2) + ADAM_EPS) + ADAM_WD * w_ref[...])


def adamw_sum8(landed, parts, me, w, m, v, row0, name, tails=None):
    rows = landed.shape[1]
    off = row0 // EARLY_TILE
    tail_blk, tail_at = divmod(OFF_TAIL - row0, EARLY_TILE)

    def body(me_ref, *refs):
        src = refs[0:N_DEV]
        own_ref = refs[N_DEV]
        pos = N_DEV + 1
        mine = me_ref[0]

        def sum8(own, slots):
            g = None
            for s in range(N_DEV):
                term = jnp.where(mine == s, own, slots(s)).astype(f32)
                g = term if g is None else g + term
            return g

        g = sum8(own_ref[0], lambda s: src[s][0])
        if tails is not None:
            tl_ref, tm_ref = refs[pos:pos + 2]
            pos += 2
            own_tail = tm_ref[0]
            for s in range(1, N_DEV):
                own_tail = jnp.where(mine == s, tm_ref[s], own_tail)
            gt = sum8(own_tail, lambda s: tl_ref[s])
            with_tail = jnp.concatenate([g[0:tail_at], gt, g[tail_at + ROWS_TAIL:]], axis=0)
            g = jnp.where(pl.program_id(0) == tail_blk, with_tail, g)
        w_ref, m_ref, v_ref = refs[pos:pos + 3]
        _adamw_update(g, w_ref, m_ref, v_ref, *refs[pos + 3:])

    def slot(s):
        return pl.BlockSpec((1, EARLY_TILE, 1024), lambda i, me_ref: (jnp.where(me_ref[0] == s, (s + 1) % N_DEV, s), i, 0))

    shard = pl.BlockSpec((EARLY_TILE, 1024), lambda i, me_ref: (i + off, 0))
    out_blk = pl.BlockSpec((EARLY_TILE, 1024), lambda i, me_ref: (i, 0))
    in_specs = [slot(s) for s in range(N_DEV)] + [pl.BlockSpec((1, EARLY_TILE, 1024), lambda i, me_ref: (me_ref[0], i, 0))]
    args = [landed] * N_DEV + [parts]
    if tails is not None:
        whole = pl.BlockSpec((N_DEV, ROWS_TAIL, 1024), lambda i, me_ref: (0, 0, 0))
        in_specs += [whole, whole]
        args += list(tails)
    grid_spec = pltpu.PrefetchScalarGridSpec(num_scalar_prefetch=1, grid=(rows // EARLY_TILE,),
                                             in_specs=in_specs + [shard, shard, shard], out_specs=[out_blk] * 4)
    out = jax.ShapeDtypeStruct((rows, 1024), f32)
    return pl.pallas_call(body, name=name, grid_spec=grid_spec, out_shape=[out] * 4,
                          compiler_params=_cparams(("parallel",)))(me, *args, w, m, v)


def adamw(parts, tails, w, m, v):
    rows = parts.shape[1]

    def body(p_ref, t_ref, w_ref, m_ref, v_ref, g_ref, d_ref, nm_ref, nv_ref):
        g = _sum4(p_ref)
        with_tail = jnp.concatenate([g[0:TAIL_AT], _sum4(t_ref), g[TAIL_AT + ROWS_TAIL:]], axis=0)
        g = jnp.where(pl.program_id(0) == TAIL_BLOCK, with_tail, g)
        _adamw_update(g, w_ref, m_ref, v_ref, g_ref, d_ref, nm_ref, nv_ref)

    blk = pl.BlockSpec((PACK_TILE, 1024), lambda i: (i, 0))
    out = jax.ShapeDtypeStruct((rows, 1024), f32)
    return pl.pallas_call(
        body, name="adamw", grid=(rows // PACK_TILE,),
        in_specs=[pl.BlockSpec((4, PACK_TILE, 1024), lambda i: (0, i, 0)),
                  pl.BlockSpec((4, ROWS_TAIL, 1024), lambda i: (0, 0, 0)), blk, blk, blk], out_specs=[blk] * 4,
        out_shape=[out] * 4, compiler_params=_cparams(("parallel",)))(parts, tails, w, m, v)


def pair_sum(parts, recv, core):
    rows = parts.shape[1]

    def body(c_ref, a_ref, b_ref, o_ref, t_ref):
        s = a_ref[...] + b_ref[...]
        o_ref[...] = s.astype(o_ref.dtype)

        @pl.when(pl.program_id(1) == TAIL_BLOCK)
        def _():
            t_ref[...] = s[:, TAIL_AT:TAIL_AT + ROWS_TAIL]

    grid_spec = pltpu.PrefetchScalarGridSpec(
        num_scalar_prefetch=1, grid=(4, rows // PACK_TILE),
        in_specs=[pl.BlockSpec((1, PACK_TILE, 1024), lambda j, i, c_ref: (2 * j + c_ref[0], i, 0)),
                  pl.BlockSpec((1, PACK_TILE, 1024), lambda j, i, c_ref: (j, i, 0))],
        out_specs=[pl.BlockSpec((1, PACK_TILE, 1024), lambda j, i, c_ref: (j, i, 0)),
                   pl.BlockSpec((1, ROWS_TAIL, 1024), lambda j, i, c_ref: (j, 0, 0))])
    return pl.pallas_call(
        body, name="pair_sum", grid_spec=grid_spec,
        out_shape=[jax.ShapeDtypeStruct(recv.shape, bf16), jax.ShapeDtypeStruct((4, ROWS_TAIL, 1024), f32)],
        compiler_params=_cparams(("parallel", "arbitrary")))(core, parts, recv)


def _place():
    return lax.axis_index("x"), lax.axis_index("y"), lax.axis_index("c")


def all_gather_blocks(shard):
    rows, cols = shard.shape

    def body(x_ref, out_ref, send_sems, recv_sems, local_sem):
        x, y, c = _place()
        me, sibling = (x, y, c), (x, y, 1 - c)
        chips = [(1 - x, y), (x, 1 - y), (1 - x, 1 - y)]

        def slot(px, py, pc):
            return out_ref.at[4 * px + 2 * py + pc]

        def copy(k, block, to, src=None):
            return pltpu.make_async_remote_copy(
                src_ref=slot(*block) if src is None else src, dst_ref=slot(*block), send_sem=send_sems.at[k],
                recv_sem=recv_sems.at[k], device_id=to, device_id_type=MESH)

        mine = pltpu.make_async_copy(x_ref, slot(*me), local_sem)
        mine.start()
        first = [copy(0, me, sibling, src=x_ref)]
        first += [copy(1 + j, me, (*chip, c), src=x_ref) for j, chip in enumerate(chips)]
        for cp in first:
            cp.start()
        passed = [copy(4 + j, (*chip, c), sibling) for j, chip in enumerate(chips)]
        for j, chip in enumerate(chips):
            copy(1 + j, (*chip, c), me).wait_recv()
            passed[j].start()
        copy(0, sibling, me).wait_recv()
        for j, chip in enumerate(chips):
            copy(4 + j, (*chip, 1 - c), me).wait_recv()
        for cp in first + passed:
            cp.wait_send()
        mine.wait()

    return pl.pallas_call(
        body, name="all_gather_blocks", out_shape=jax.ShapeDtypeStruct((N_DEV, rows, cols), shard.dtype),
        in_specs=[pl.BlockSpec(memory_space=pl.ANY)], out_specs=pl.BlockSpec(memory_space=pl.ANY),
        scratch_shapes=[pltpu.SemaphoreType.DMA((7,)), pltpu.SemaphoreType.DMA((7,)), pltpu.SemaphoreType.DMA],
        compiler_params=pltpu.CompilerParams(has_side_effects=True))(shard)


def pair_exchange(parts):
    _, rows, cols = parts.shape

    def body(p_ref, recv_ref, send_sems, recv_sems):
        x, y, c = _place()
        copies = [pltpu.make_async_remote_copy(
            src_ref=p_ref.at[2 * j + 1 - c], dst_ref=recv_ref.at[j], send_sem=send_sems.at[j], recv_sem=recv_sems.at[j],
            device_id=(x, y, 1 - c), device_id_type=MESH) for j in range(4)]
        for cp in copies:
            cp.start()
        for cp in copies:
            cp.wait_recv()
        for cp in copies:
            cp.wait_send()

    return pl.pallas_call(
        body, name="pair_exchange", out_shape=jax.ShapeDtypeStruct((4, rows, cols), parts.dtype),
        in_specs=[pl.BlockSpec(memory_space=pl.ANY)], out_specs=pl.BlockSpec(memory_space=pl.ANY),
        scratch_shapes=[pltpu.SemaphoreType.DMA((4,)), pltpu.SemaphoreType.DMA((4,))],
        compiler_params=pltpu.CompilerParams(has_side_effects=True))(parts)


def chip_exchange(parts):
    n = len(parts)

    def body(*refs):
        p_refs, out_refs = refs[0:n], refs[n:2 * n]
        send_sems, recv_sems, local_sems = refs[2 * n:]
        x, y, c = _place()
        mine = 2 * x + y
        flips = [(x, 1 - y), (1 - x, y), (1 - x, 1 - y)]

        def copy(a, k, src_slot, dst_slot):
            px, py = flips[k]
            return pltpu.make_async_remote_copy(
                src_ref=p_refs[a].at[src_slot], dst_ref=out_refs[a].at[dst_slot], send_sem=send_sems.at[3 * a + k],
                recv_sem=recv_sems.at[3 * a + k], device_id=(px, py, c), device_id_type=MESH)

        local = [pltpu.make_async_copy(p_refs[a].at[mine], out_refs[a].at[mine], local_sems.at[a]) for a in range(n)]
        sends = [copy(a, k, 2 * flips[k][0] + flips[k][1], mine) for a in range(n) for k in range(3)]
        for cp in local + sends:
            cp.start()
        for a in range(n):
            for k in range(3):
                copy(a, k, mine, 2 * flips[k][0] + flips[k][1]).wait_recv()
        for cp in sends:
            cp.wait_send()
        for cp in local:
            cp.wait()

    return pl.pallas_call(
        body, name="chip_exchange", out_shape=[jax.ShapeDtypeStruct(p.shape, p.dtype) for p in parts],
        in_specs=[pl.BlockSpec(memory_space=pl.ANY)] * n, out_specs=[pl.BlockSpec(memory_space=pl.ANY)] * n,
        scratch_shapes=[pltpu.SemaphoreType.DMA((3 * n,)), pltpu.SemaphoreType.DMA((3 * n,)), pltpu.SemaphoreType.DMA((n,))],
        compiler_params=pltpu.CompilerParams(has_side_effects=True))(*parts)


_HBM = pl.BlockSpec(memory_space=pltpu.HBM)
_SEM = pl.BlockSpec(memory_space=pltpu.SEMAPHORE)


def _peer(k):
    x, y, c = _place()
    px, py, pc = (1 - x if k & 4 else x), (1 - y if k & 2 else y), (1 - c if k & 1 else c)
    return (px, py, pc), 4 * px + 2 * py + pc


def scatter_start(parts, name):
    per_device = parts.ndim == 3

    def body(p_ref, land_ref, send_sems, recv_sems, p_thru, land_thru, token):
        x, y, c = _place()
        me = 4 * x + 2 * y + c
        for k in range(1, N_DEV):
            place, idx = _peer(k)
            pltpu.make_async_remote_copy(src_ref=p_ref.at[idx] if per_device else p_ref, dst_ref=land_ref.at[me],
                                         send_sem=send_sems.at[k - 1], recv_sem=recv_sems.at[k - 1], device_id=place,
                                         device_id_type=MESH).start()
        token[...] = jnp.zeros_like(token)

    land_shape = parts.shape if per_device else (N_DEV,) + parts.shape
    landing = lax.empty(land_shape, parts.dtype)
    return pl.pallas_call(
        body, name=name,
        out_shape=(pltpu.SemaphoreType.DMA((N_DEV - 1,)), pltpu.SemaphoreType.DMA((N_DEV - 1,)),
                   pltpu.HBM(parts.shape, parts.dtype), pltpu.HBM(land_shape, parts.dtype),
                   jax.ShapeDtypeStruct((8, 128), f32)),
        in_specs=(_HBM, _HBM), out_specs=(_SEM, _SEM, _HBM, _HBM, pl.BlockSpec(memory_space=pltpu.VMEM)),
        input_output_aliases={0: 2, 1: 3},
        compiler_params=pltpu.CompilerParams(has_side_effects=pltpu.SideEffectType.DATAFLOW_SIDE_EFFECTING),
    )(pltpu.with_memory_space_constraint(parts, pltpu.HBM), pltpu.with_memory_space_constraint(landing, pltpu.HBM))


def scatter_wait(send_sems, recv_sems, parts_thru, land_thru, after, name):
    per_device = parts_thru.ndim == 3

    def body(p_ref, land_ref, send_sems, recv_sems, after_ref, p_out, land_out):
        for k in range(1, N_DEV):
            place, idx = _peer(k)
            copy = pltpu.make_async_remote_copy(src_ref=p_ref.at[idx] if per_device else p_ref, dst_ref=land_ref.at[idx],
                                                send_sem=send_sems.at[k - 1], recv_sem=recv_sems.at[k - 1],
                                                device_id=place, device_id_type=MESH)
            copy.wait_send()
            copy.wait_recv()

    return pl.pallas_call(
        body, name=name,
        out_shape=(pltpu.HBM(parts_thru.shape, parts_thru.dtype), pltpu.HBM(land_thru.shape, land_thru.dtype)),
        in_specs=(_HBM, _HBM, _SEM, _SEM, pl.BlockSpec(memory_space=pl.ANY)), out_specs=(_HBM, _HBM),
        input_output_aliases={0: 0, 1: 1},
        compiler_params=pltpu.CompilerParams(has_side_effects=pltpu.SideEffectType.DATAFLOW_SIDE_EFFECTING),
    )(parts_thru, land_thru, send_sems, recv_sems, after)


def _tail_rows(conv_part, small, extra):
    lead = conv_part.shape[:-1]
    rep = jnp.concatenate([small[n].reshape(-1).astype(f32) for n in SMALL] + [extra.reshape(1).astype(f32)])
    flat = jnp.concatenate([conv_part, jnp.broadcast_to(rep, lead + rep.shape),
                            jnp.zeros(lead + (ROWS_TAIL * 1024 - TAIL_ELEMS,), f32)], axis=-1)
    return flat.reshape(lead + (ROWS_TAIL, 1024))


def _late_rows(w_in_t, tail):
    lead = tail.shape[:-2]
    zeros = lambda r: jnp.zeros(lead + (r, 1024), f32)
    return jnp.concatenate([w_in_t, zeros(OFF_TAIL - IN_SHARD), tail, zeros(LATE_ROWS - OFF_TAIL - ROWS_TAIL)], axis=-2)


def _early_rows(w_ps, w_out, w_up_t, w_down, w_pa_t):
    return jnp.concatenate([w_ps, w_out, w_up_t, w_down, w_pa_t.reshape(w_pa_t.shape[:-2] + (ROWS_PA, 1024))], axis=-2)


def _pack_shard(vals):
    tail = _tail_rows(vals["conv_w"].reshape(-1), vals, jnp.zeros((), f32))
    return jnp.concatenate([_late_rows(vals["w_in"].T, tail),
                            _early_rows(vals["w_proj_ssd"], vals["w_out"], vals["w_up"].T, vals["w_down"],
                                        vals["w_proj_attn"].T)], axis=0)


def _unpack_shard(late, early):
    e = lambda lo, hi: early[lo - LATE_ROWS:hi - LATE_ROWS]
    out = {"w_in": late[0:IN_SHARD].T, "w_proj_ssd": e(OFF_PS, OFF_OUT), "w_out": e(OFF_OUT, OFF_UP),
           "w_up": e(OFF_UP, OFF_DOWN).T, "w_down": e(OFF_DOWN, OFF_PA),
           "w_proj_attn": e(OFF_PA, PACK_ROWS).reshape(D_MODEL // N_DEV, ATTN_OUT).T}
    flat = late[OFF_TAIL:OFF_TAIL + ROWS_TAIL].reshape(-1)
    out["conv_w"] = flat[0:CONV_SHARD].reshape(D_CONV, CONV_DIM // N_DEV)
    off = CONV_SHARD
    for n in SMALL:
        out[n] = flat[off:off + SMALL_SIZES[n]]
        off += SMALL_SIZES[n]
    out["_extra"] = flat[off]
    return out


def _blocks(g):
    return g.reshape(N_DEV, g.shape[0] // N_DEV, g.shape[1])


def _pack_early_parts(full):
    return _early_rows(_blocks(full["w_proj_ssd"]), _blocks(full["w_out"]), _blocks(full["w_up_t"]),
                       _blocks(full["w_down"]), _blocks(full["w_proj_attn_t"]))


def _pack_late_parts(full, small, extra):
    conv = full["conv_w"].reshape(D_CONV, N_DEV, CONV_DIM // N_DEV).transpose(1, 0, 2).reshape(N_DEV, CONV_SHARD)
    return _late_rows(_blocks(full["w_in_t"]), _tail_rows(conv, small, extra))


def _gather_weights(w):
    conv_bits = lax.bitcast_convert_type(w["conv_w"], bf16).reshape(-1)
    conv_rows = jnp.concatenate([conv_bits, jnp.zeros((16 * 1024 - 2 * CONV_SHARD,), bf16)]).reshape(16, 1024)
    packed = _pack_shard(w)
    first = OFF_TAIL + ROWS_TAIL
    got = all_gather_blocks(jnp.concatenate([packed[0:OFF_TAIL].astype(bf16), conv_rows], axis=0))
    got, rest = lax.optimization_barrier((got, packed[first:].astype(bf16)))
    send_sems, recv_sems, rest_thru, land_thru, token = scatter_start(rest, "gather_start")
    conv =lax.bitcast_convert_type(got[:, OFF_TAIL:OFF_TAIL + 4].reshape(N_DEV, 4096)[:, 0:2 * CONV_SHARD]
                                    .reshape(N_DEV, D_CONV, CONV_DIM // N_DEV, 2), f32)
    now = {"w_in_t": got[:, 0:IN_SHARD].reshape(IN_COLS, 1024), "conv_w": conv.transpose(1, 0, 2).reshape(D_CONV, CONV_DIM)}

    def later(after):
        mine, landed = scatter_wait(send_sems, recv_sems, rest_thru, land_thru, after, "gather_wait")
        x, y, c = _place()
        landed = lax.dynamic_update_slice(landed, mine[None], (4 * x + 2 * y + c, 0, 0))
        whole = lambda lo, hi: landed[:, lo - first:hi - first].reshape(N_DEV * (hi - lo), 1024)
        return {"w_proj_ssd": whole(OFF_PS, OFF_OUT), "w_out": whole(OFF_OUT, OFF_UP), "w_up_t": whole(OFF_UP, OFF_DOWN),
                "w_down": whole(OFF_DOWN, OFF_PA),
                "w_proj_attn_t": landed[:, OFF_PA - first:PACK_ROWS - first].reshape(D_MODEL, ATTN_OUT)}

    return now, later, token


def _row(v, width=None):
    v = v.reshape(1, -1).astype(f32)
    return v if width is None else jnp.pad(v, ((0, 0), (0, width - v.shape[1])))


def _lanes256(vf, vb):
    z = jnp.zeros((96,), f32)
    return jnp.concatenate([vf.astype(f32), z, vb.astype(f32), z]).reshape(1, 256)


def _local_step(x2, tgt, wf, p, send_early=None, late_weights=None, start_token=None, send_late=None):
    t = x2.shape[0]
    o = np.cumsum((0,) + IN_SPLITS)
    wt = wf["w_in_t"]
    wt_z, wt_xbc, wt_dt = wt[o[0]:o[1]], wt[o[1]:o[2]], wt[o[2]:o[4]]
    wt_qkv, wt_gate = wt[o[4]:o[7]], wt[o[7]:o[8]]

    spread = lambda v: jnp.broadcast_to(v.astype(f32)[..., None], v.shape + (128,))
    conv_w_b, conv_b_b = spread(wf["conv_w"]), spread(p["conv_b"])
    dt_bias_b = spread(jnp.concatenate([p["dt_bias_f"], p["dt_bias_b"]]))
    a_f, a_b = -jnp.exp(p["a_log_f"].astype(f32)), -jnp.exp(p["a_log_b"].astype(f32))
    a_coef_b = spread(jnp.concatenate([a_f, a_b]))
    skip_b = spread(jnp.repeat(p["d_skip"], SSD_HEAD_DIM))
    nw_b, bg_row = spread(p["ssd_norm_w"]), _row(p["b_gate"])
    g1, b1, g2, b2 = _row(p["ln1_g"]), _row(p["ln1_b"]), _row(p["ln2_g"]), _row(p["ln2_b"])

    xb = (x2 if start_token is None else x2 + start_token[0, 0]).astype(MXU_DTYPE)
    u_z = mm_nt(wt_z, xb, "in_z")
    u_xbc = mm_nt(wt_xbc, xb, "in_xbc")
    u_dt = mm_nt(wt_dt, xb, "in_dt")
    u_qkv = mm_nt_split(xb, wt_qkv, "in_qkv", 256, bf16)
    u_gate = mm_nt(xb, wt_gate, "in_gate")
    xbc_c, dsilu = conv_fwd_t(u_xbc, conv_w_b, conv_b_b)
    dt_t = dt_fwd_t(u_dt, dt_bias_b)
    y_f, h_f = ssd_fwd_t(xbc_c, dt_t, a_coef_b, False, "ssd_fwd_f")
    y_scan, h_b, yn = ssd_fwd_t(xbc_c, dt_t, a_coef_b, True, "ssd_fwd_b", prev=y_f, tail=(u_z, skip_b, nw_b))
    if late_weights is not None:
        wf = {**wf, **late_weights(yn)}
    y_ssd = mm_tn(yn, wf["w_proj_ssd"], "proj_ssd")

    def strided(a, dil):
        return a.reshape(t // dil, dil * 256)

    qkv, outs, lses = [], [], []
    for pi, (_, dil) in enumerate(DIL_PATTERNS):
        q, k, v = (strided(u_qkv[N_PATTERNS * s + pi], dil) for s in range(3))
        qkv.append((q, k, v))
        op, lp = attn_fwd(q, k, v, pi, dil, f"attn_fwd_{pi}")
        outs.append(op.reshape(t, 256))
        lses.append(lp.reshape(t, 256))
    ya, lse = attn_combine(outs, lses)
    y_att = mm_nt(ya, wf["w_proj_attn_t"], "proj_attn")
    m = merge_fwd(u_gate, bg_row, y_ssd, y_att)
    mix = mm_nn(m, wf["w_out"], "out_proj")
    h1, h1b = ln1_fwd(x2, mix, g1, b1)
    r_up, p_act = mm_nt(h1b, wf["w_up_t"], "mlp_up", relu2=True)
    f_dn = mm_nn(p_act, wf["w_down"], "mlp_down")
    dr2, dr2b, dg2, db2, loss8 = ln2_loss(h1, f_dn, g2, b2, tgt)

    full, small = {}, {}
    da = mm_nt(dr2b, wf["w_down"], "d_mlp_act", out_dtype=bf16, relu2_of=r_up)
    full["w_down"] = mm_tn(p_act, dr2b, "dw_down")
    full["w_up_t"] = mm_tn(da, h1b, "dw_up")
    dh1 = mm_nn(da, wf["w_up_t"], "d_h1", acc_in=dr2, acc_scale=ALPHA)
    dr1, dr1b, dg1, db1 = ln1_bwd(dh1, x2, mix, g1)
    dm = mm_nt(dr1b, wf["w_out"], "d_merge")
    full["w_out"] = mm_tn(m, dr1b, "dw_out")
    dys, dya_p, dga, dgb, dba, dbb = merge_bwd(dm, u_gate, bg_row, y_ssd, y_att)
    dyn = mm_nt(wf["w_proj_ssd"], dys, "d_yn")
    full["w_proj_ssd"] = mm_nn(yn, dys, "dw_proj_ssd")
    dya = mm_nn(dya_p, wf["w_proj_attn_t"], "d_ya")
    full["w_proj_attn_t"] = mm_tn(dya_p, ya, "dw_proj_attn")
    if send_early is not None:
        skip_b = skip_b + send_early(full)[0, 0]

    dxf, dbf, dcf, ddtf, daf, dy, du_ssd, dnw, ddx = ssd_bwd_t(xbc_c, dt_t, a_coef_b, dyn, h_f, False, "ssd_bwd_f",
                                                               skip_b=skip_b, tail=(y_scan, u_z, nw_b))
    dxs, dbs, dcs, ddtb, dab = ssd_bwd_t(xbc_c, dt_t, a_coef_b, dy, h_b, True, "ssd_bwd_b", prev=(dxf, dbf, dcf))
    du_ssd, dcw_x, dcb_x = conv_bwd_t(u_xbc, dsilu, dxs, conv_w_b, du_ssd, "conv_bwd_x", 0)
    du_ssd, dcw_b, dcb_b = conv_bwd_t(u_xbc, dsilu, dbs, conv_w_b, du_ssd, "conv_bwd_b", D_INNER)
    du_ssd, dcw_c, dcb_c = conv_bwd_t(u_xbc, dsilu, dcs, conv_w_b, du_ssd, "conv_bwd_c", D_INNER + 512)
    du_ssd, dbias = dt_bwd_t(ddtf, ddtb, u_dt, dt_bias_b, du_ssd)

    delta = attn_delta(dya, ya)
    dqs, dks, dvs = [], [], []
    for pi, (_, dil) in enumerate(DIL_PATTERNS):
        q, k, v = qkv[pi]
        sd, sl_, sdel = strided(dya, dil), strided(lse, dil), strided(delta, dil)
        dqs.append(attn_dq(q, k, v, sd, sl_, sdel, pi, dil, f"attn_dq_{pi}").reshape(t, 256))
        dk, dv = attn_dkv(q, k, v, sd, sl_, sdel, pi, dil, f"attn_dkv_{pi}")
        dks.append(dk.reshape(t, 256))
        dvs.append(dv.reshape(t, 256))
    du_qkv = jnp.concatenate(dqs + dks + dvs, axis=1)
    du_gate = jnp.concatenate([dga, dgb], axis=1)

    full["w_in_t"] = jnp.concatenate(
        [mm_nn(du_ssd, xb, "dw_in_ssd"), mm_tn(du_qkv, xb, "dw_in_qkv"), mm_tn(du_gate, xb, "dw_in_gate")], axis=0)
    lanes = lambda v: jnp.sum(v, axis=-1)
    full["conv_w"] = jnp.concatenate([lanes(dcw_x), lanes(dcw_b), lanes(dcw_c)], axis=1)

    small["b_gate"] = jnp.concatenate([dba, dbb], axis=1)
    small["conv_b"] = jnp.concatenate([lanes(dcb_x), lanes(dcb_b), lanes(dcb_c)])
    dbias = lanes(dbias)
    small["dt_bias_f"], small["dt_bias_b"] = dbias[0:32], dbias[32:64]
    small["a_log_f"] = lanes(daf) * a_f
    small["a_log_b"] = lanes(dab) * a_b
    small["d_skip"] = jnp.sum(lanes(ddx).reshape(SSD_HEADS, SSD_HEAD_DIM), axis=1)
    small["ssd_norm_w"] = lanes(dnw)
    small["ln1_g"], small["ln1_b"], small["ln2_g"], small["ln2_b"] = dg1, db1, dg2, db2

    wt_ssd = wt[0:SSD_COLS]
    if send_late is not None:
        wt_ssd = wt_ssd + send_late(full, small, loss8[0, 0])[0, 0].astype(wt_ssd.dtype)
    dx = mm_tn(du_ssd, wt_ssd, "dx_ssd", acc_in=dr1, acc_scale=ALPHA)
    dx = mm_nn(du_qkv, wt_qkv, "dx_qkv", acc_in=dx)
    dx = mm_nn(du_gate, wt_gate, "dx_gate", acc_in=dx)
    return loss8[0, 0], dx, full, small


def kernel(x, w_in, b_gate, conv_w, conv_b, dt_bias_f, dt_bias_b, a_log_f, a_log_b, d_skip, ssd_norm_w, w_proj_ssd, w_proj_attn, w_out, ln1_g, ln1_b, w_up, w_down, ln2_g, ln2_b, loss_target, m_w_in, m_b_gate, m_conv_w, m_conv_b, m_dt_bias_f, m_dt_bias_b, m_a_log_f, m_a_log_b, m_d_skip, m_ssd_norm_w, m_w_proj_ssd, m_w_proj_attn, m_w_out, m_ln1_g, m_ln1_b, m_w_up, m_w_down, m_ln2_g, m_ln2_b, v_w_in, v_b_gate, v_conv_w, v_conv_b, v_dt_bias_f, v_dt_bias_b, v_a_log_f, v_a_log_b, v_d_skip, v_ssd_norm_w, v_w_proj_ssd, v_w_proj_attn, v_w_out, v_ln1_g, v_ln1_b, v_w_up, v_w_down, v_ln2_g, v_ln2_b):
    given = dict(locals())
    w = {n: given[n] for n in WEIGHTS}
    mom = {n: given["m_" + n] for n in WEIGHTS}
    var = {n: given["v_" + n] for n in WEIGHTS}
    t = x.shape[1]
    wf, late_weights, start_token = _gather_weights(w)
    in_flight = []

    def send_early(full):
        send_sems, recv_sems, parts_thru, land_thru, token = scatter_start(_pack_early_parts(full), "scatter_start")
        in_flight.append((send_sems, recv_sems, parts_thru, land_thru))
        return token

    def send_late(full, small, loss):
        late = _pack_late_parts(full, small, loss)
        rows = scatter_start(late.astype(bf16), "late_start")
        tail = scatter_start(late[:, OFF_TAIL:OFF_TAIL + ROWS_TAIL], "tail_start")
        in_flight.extend([rows[0:4], tail[0:4]])
        return rows[4] + tail[4]

    loss, dx, full, small = _local_step(x.reshape(t, D_MODEL), loss_target.reshape(t, D_MODEL), wf, w, send_early,
                                        late_weights, start_token, send_late)
    x_, y_, c_ = _place()
    me = (4 * x_ + 2 * y_ + c_).astype(jnp.int32).reshape(1)
    wp, mp, vp = _pack_shard(w), _pack_shard(mom), _pack_shard(var)
    early_parts, early_landed = scatter_wait(*in_flight[0], dx, "scatter_wait")
    early_out = adamw_sum8(early_landed, early_parts, me, wp, mp, vp, LATE_ROWS, "adamw_early")
    late_parts, late_landed = scatter_wait(*in_flight[1], dx, "late_wait")
    tail_parts, tail_landed = scatter_wait(*in_flight[2], dx, "tail_wait")
    late_out = adamw_sum8(late_landed, late_parts, me, wp, mp, vp, 0, "adamw_late", tails=(tail_landed, tail_parts))
    g, delta, new_m, new_v = (_unpack_shard(a, b) for a, b in zip(late_out, early_out))
    outs = [g["_extra"], dx.reshape(x.shape)]
    for d in (g, delta, new_m, new_v):
        outs += [d[n].reshape(w[n].shape) for n in WEIGHTS]
    return tuple(outs)
```

```python
import jax
import jax.numpy as jnp
import numpy as np
from jax import lax
from jax.experimental import pallas as pl
from jax.experimental.pallas import tpu as pltpu

f32 = jnp.float32
bf16 = jnp.bfloat16
MXU_DTYPE = jnp.bfloat16

N_DEV = 8
D_MODEL = 1024
D_INNER = 2048
SSD_HEADS = 32
SSD_HEAD_DIM = 64
SSD_GROUPS = 4
D_STATE = 128
D_CONV = 5
CHUNK = 128
CONV_DIM = D_INNER + 2 * SSD_GROUPS * D_STATE
NORM_EPS = 1e-5
ATTN_HEAD_DIM = 64
DIL_PATTERNS = ((128, 1), (512, 4), (2048, 16))
N_PATTERNS = len(DIL_PATTERNS)
HEADS_PER_PATTERN = 4
ATTN_HEADS = 12
ATTN_WIDTH = 768
ATTN_OUT = 256
D_FF = 4096
ALPHA = 2.0 ** 0.25
IN_SPLITS = (D_INNER, CONV_DIM, SSD_HEADS, SSD_HEADS, ATTN_WIDTH, ATTN_WIDTH, ATTN_WIDTH, 2 * D_MODEL)
IN_COLS = sum(IN_SPLITS)
SSD_COLS = sum(IN_SPLITS[0:4])
ADAM_LR, ADAM_B1, ADAM_B2, ADAM_EPS, ADAM_WD, ADAM_STEP = 0.001, 0.9, 0.999, 1e-08, 0.01, 10
NEG_BIG = -1e30
VMEM_LIMIT = 56 * 1024 * 1024
MESH = pl.DeviceIdType.MESH

SMALL = ("b_gate", "conv_b", "dt_bias_f", "dt_bias_b", "a_log_f", "a_log_b", "d_skip", "ssd_norm_w",
         "ln1_g", "ln1_b", "ln2_g", "ln2_b")
WEIGHTS = ("w_in", "b_gate", "conv_w", "conv_b", "dt_bias_f", "dt_bias_b", "a_log_f", "a_log_b", "d_skip",
           "ssd_norm_w", "w_proj_ssd", "w_proj_attn", "w_out", "ln1_g", "ln1_b", "w_up", "w_down", "ln2_g", "ln2_b")
SMALL_SIZES = {"b_gate": 2 * D_MODEL, "conv_b": CONV_DIM, "dt_bias_f": 32, "dt_bias_b": 32, "a_log_f": 32, "a_log_b": 32,
               "d_skip": 32, "ssd_norm_w": D_INNER, "ln1_g": D_MODEL, "ln1_b": D_MODEL, "ln2_g": D_MODEL, "ln2_b": D_MODEL}
IN_SHARD = IN_COLS // N_DEV
OFF_TAIL = 1200
ROWS_TAIL = 16
LATE_ROWS = 1280
ROWS_PS, ROWS_OUT, ROWS_UP, ROWS_DOWN, ROWS_PA = D_INNER // N_DEV, D_MODEL // N_DEV, D_FF // N_DEV, D_FF // N_DEV, 32
OFF_PS = LATE_ROWS
OFF_OUT = OFF_PS + ROWS_PS
OFF_UP = OFF_OUT + ROWS_OUT
OFF_DOWN = OFF_UP + ROWS_UP
OFF_PA = OFF_DOWN + ROWS_DOWN
PACK_ROWS = OFF_PA + ROWS_PA
EARLY_ROWS = PACK_ROWS - LATE_ROWS
EARLY_TILE = 160
CONV_SHARD = D_CONV * CONV_DIM // N_DEV
TAIL_ELEMS = CONV_SHARD + sum(SMALL_SIZES.values()) + 1


def _cparams(sem=None, **kw):
    return pltpu.CompilerParams(dimension_semantics=sem, vmem_limit_bytes=VMEM_LIMIT, **kw)


def _mx(v):
    return v.astype(MXU_DTYPE)


def _dot(a, b):
    return jnp.dot(_mx(a), _mx(b), preferred_element_type=f32)


def _dot_nt(a, b):
    return lax.dot_general(_mx(a), _mx(b), (((1,), (1,)), ((), ())), preferred_element_type=f32)


def _dot_tn(a, b):
    return lax.dot_general(_mx(a), _mx(b), (((0,), (0,)), ((), ())), preferred_element_type=f32)


def _dot_exact(a, b):
    return jnp.dot(a, b, precision=lax.Precision.HIGHEST, preferred_element_type=f32)


def _sigmoid(v):
    return 1.0 / (1.0 + jnp.exp(-v))


def _pick(n, prefs):
    for p in prefs:
        if n % p == 0:
            return p
    return n


MM_TILE = 1024


def mm_nn(a, b, name, out_dtype=f32, acc_in=None, acc_scale=1.0):
    m, k = a.shape
    n = b.shape[1]
    tm = _pick(m, (MM_TILE, 576, 512, 256, 128, 64))
    tn = _pick(n, (MM_TILE, 512, 256, 128))
    tk = _pick(k, (2048, 1536, 1152, 1024, 768, 512, 256, 128))
    nk = k // tk

    def body(*refs):
        a_ref, b_ref = refs[0:2]
        c_ref = refs[2] if acc_in is not None else None
        o_ref = refs[3] if acc_in is not None else refs[2]

        def finish(r):
            if acc_in is not None:
                r = r + acc_scale * c_ref[...]
            o_ref[...] = r.astype(o_ref.dtype)

        if nk == 1:
            finish(_dot(a_ref[...], b_ref[...]))
            return
        acc_ref = refs[-1]
        kk = pl.program_id(2)

        @pl.when(kk == 0)
        def _():
            acc_ref[...] = jnp.zeros_like(acc_ref)

        acc_ref[...] += _dot(a_ref[...], b_ref[...])

        @pl.when(kk == nk - 1)
        def _():
            finish(acc_ref[...])

    in_specs = [pl.BlockSpec((tm, tk), lambda i, j, kk: (i, kk)), pl.BlockSpec((tk, tn), lambda i, j, kk: (kk, j))]
    args = [a, b]
    if acc_in is not None:
        in_specs.append(pl.BlockSpec((tm, tn), lambda i, j, kk: (i, j)))
        args.append(acc_in)
    return pl.pallas_call(
        body, name=name, grid=(m // tm, n // tn, nk), in_specs=in_specs,
        out_specs=pl.BlockSpec((tm, tn), lambda i, j, kk: (i, j)),
        out_shape=jax.ShapeDtypeStruct((m, n), out_dtype),
        scratch_shapes=[pltpu.VMEM((tm, tn), f32)] if nk > 1 else [],
        compiler_params=_cparams(("parallel", "parallel", "arbitrary")))(*args)


def mm_nt(a, b, name, out_dtype=f32, relu2=None, relu2_of=None):
    m, k = a.shape
    n = b.shape[0]
    tm = _pick(m, (MM_TILE, 512, 256, 128, 64))
    tn = _pick(n, (MM_TILE, 768, 512, 256, 128))

    def body(*refs):
        r = _dot_nt(refs[0][...], refs[1][...])
        if relu2:
            pos = jnp.maximum(r, 0.0)
            refs[2][...] = pos.astype(refs[2].dtype)
            refs[3][...] = (pos * pos).astype(refs[3].dtype)
        elif relu2_of is not None:
            refs[3][...] = (r * (2.0 * refs[2][...].astype(f32))).astype(refs[3].dtype)
        else:
            refs[2][...] = r.astype(refs[2].dtype)

    blk = pl.BlockSpec((tm, tn), lambda i, j: (i, j))
    in_specs = [pl.BlockSpec((tm, k), lambda i, j: (i, 0)), pl.BlockSpec((tn, k), lambda i, j: (j, 0))]
    args = [a, b]
    if relu2_of is not None:
        in_specs.append(blk)
        args.append(relu2_of)
    if relu2:
        out_specs, out_shape = [blk, blk], [jax.ShapeDtypeStruct((m, n), bf16), jax.ShapeDtypeStruct((m, n), bf16)]
    else:
        out_specs, out_shape = blk, jax.ShapeDtypeStruct((m, n), out_dtype)
    return pl.pallas_call(body, name=name, grid=(m // tm, n // tn), in_specs=in_specs, out_specs=out_specs,
                          out_shape=out_shape, compiler_params=_cparams(("parallel", "parallel")))(*args)


def mm_nt_split(a, b, name, width, out_dtype=f32):
    m, k = a.shape
    n = b.shape[0]
    tm = MM_TILE
    parts = n // width

    def body(a_ref, b_ref, *o_refs):
        r = _dot_nt(a_ref[...], b_ref[...])
        for q in range(parts):
            o_refs[q][...] = r[:, width * q:width * (q + 1)].astype(o_refs[q].dtype)

    blk = pl.BlockSpec((tm, width), lambda i: (i, 0))
    return pl.pallas_call(
        body, name=name, grid=(m // tm,),
        in_specs=[pl.BlockSpec((tm, k), lambda i: (i, 0)), pl.BlockSpec((n, k), lambda i: (0, 0))],
        out_specs=[blk] * parts, out_shape=[jax.ShapeDtypeStruct((m, width), out_dtype)] * parts,
        compiler_params=_cparams(("parallel",)))(a, b)


def mm_tn(a, b, name, acc_in=None, acc_scale=1.0):
    k, m = a.shape
    n = b.shape[1]
    tm = _pick(m, (MM_TILE, 768, 512, 256, 128))
    tn = _pick(n, (MM_TILE, 512, 256, 128))
    tk = _pick(k, (1728, 1024, 768, 512, 256, 128, 64))
    nk = k // tk

    def body(*refs):
        a_ref, b_ref, o_ref = refs[0], refs[1], refs[-1]
        kk = pl.program_id(2)

        @pl.when(kk == 0)
        def _():
            o_ref[...] = jnp.zeros_like(o_ref) if acc_in is None else acc_scale * refs[2][...]

        o_ref[...] += _dot_tn(a_ref[...], b_ref[...])

    in_specs = [pl.BlockSpec((tk, tm), lambda i, j, kk: (kk, i)), pl.BlockSpec((tk, tn), lambda i, j, kk: (kk, j))]
    args = [a, b]
    if acc_in is not None:
        in_specs.append(pl.BlockSpec((tm, tn), lambda i, j, kk: (i, j)))
        args.append(acc_in)
    return pl.pallas_call(
        body, name=name, grid=(m // tm, n // tn, nk), in_specs=in_specs,
        out_specs=pl.BlockSpec((tm, tn), lambda i, j, kk: (i, j)),
        out_shape=jax.ShapeDtypeStruct((m, n), f32),
        compiler_params=_cparams(("parallel", "parallel", "arbitrary")))(*args)


def _lane_col(mat, lane_idx, h):
    return jnp.sum(jnp.where(lane_idx == h, mat, 0.0), axis=1, keepdims=True)


def _slopes(p):
    return [2.0 ** (-8.0 * (HEADS_PER_PATTERN * p + j + 1) / ATTN_HEADS) for j in range(HEADS_PER_PATTERN)]


def _win_specs(nq, col_of):
    return [pl.BlockSpec((64, 256), lambda r, i: (jnp.maximum(2 * i - 1, 0), col_of(r))),
            pl.BlockSpec((128, 256), lambda r, i: (i, col_of(r))),
            pl.BlockSpec((64, 256), lambda r, i: (jnp.minimum(2 * i + 2, 2 * nq - 1), col_of(r)))]


def _lane_head(shape):
    return lax.broadcasted_iota(jnp.int32, shape, 1) >> 6


def _stack_heads(m):
    lane_head = _lane_head(m.shape)
    return jnp.concatenate([jnp.where(lane_head == j, m, 0.0) for j in range(HEADS_PER_PATTERN)], axis=0)


def _unstack_heads(m4, n):
    lane_head = _lane_head((n, 256))
    out = jnp.where(lane_head == 0, m4[0:n], 0.0)
    for j in range(1, HEADS_PER_PATTERN):
        out = out + jnp.where(lane_head == j, m4[j * n:(j + 1) * n], 0.0)
    return out


def _head_cols(m, n):
    lane = lax.broadcasted_iota(jnp.int32, (n, 256), 1)
    return jnp.concatenate([jnp.sum(jnp.where(lane == ATTN_HEAD_DIM * j, m, 0.0), axis=1, keepdims=True)
                            for j in range(HEADS_PER_PATTERN)], axis=0)


def _score_bias(p, dil, by_key):
    slopes = np.asarray(_slopes(p), np.float32)
    if by_key:
        win = np.arange(256)[:, None]
        rel = np.arange(128)[None, :] - (win - 64)
    else:
        win = np.arange(256)[None, :]
        rel = win - 64 - np.arange(128)[:, None]
    band = np.abs(rel) <= 64
    out = []
    for first, last in ((False, False), (True, False), (False, True), (True, True)):
        ok = band & ~(first & (win < 64)) & ~(last & (win >= 192))
        pen = -slopes[:, None, None] * (np.abs(rel) * dil).astype(np.float32)[None]
        out.append(np.where(ok[None], pen, np.float32(NEG_BIG)).reshape(-1, rel.shape[1]))
    return jnp.asarray(np.stack(out), f32)


def _bias_spec(nq, rows, cols):
    return pl.BlockSpec((1, rows, cols), lambda r, i: ((i == 0).astype(jnp.int32) + 2 * (i == nq - 1).astype(jnp.int32), 0, 0))


def attn_fwd(q, k, v, p, dil, name):
    l = q.shape[0]
    nq = l // 128

    def body(q_ref, kp_ref, ko_ref, kn_ref, vp_ref, vo_ref, vn_ref, bias_ref, o_ref, lse_ref):
        kcat = jnp.concatenate([kp_ref[...], ko_ref[...], kn_ref[...]], axis=0)
        vcat = jnp.concatenate([vp_ref[...], vo_ref[...], vn_ref[...]], axis=0)
        s = _dot_nt(_stack_heads(q_ref[...] * 0.125), kcat) + bias_ref[0]
        m = jnp.max(s, axis=1, keepdims=True)
        pr = jnp.exp(s - m)
        den = jnp.sum(pr, axis=1, keepdims=True)
        o4 = _dot(pr, vcat) / den
        o_ref[...] = _unstack_heads(o4, 128)
        lse_ref[...] = _unstack_heads(jnp.broadcast_to(m + jnp.log(den), (512, 256)), 128)

    col = lambda r: r
    return pl.pallas_call(
        body, name=name, grid=(dil, nq),
        in_specs=[pl.BlockSpec((128, 256), lambda r, i: (i, r))] + _win_specs(nq, col) + _win_specs(nq, col)
        + [_bias_spec(nq, 512, 256)],
        out_specs=[pl.BlockSpec((128, 256), lambda r, i: (i, r))] * 2,
        out_shape=[jax.ShapeDtypeStruct(q.shape, f32)] * 2,
        compiler_params=_cparams(("parallel", "parallel")))(q, k, k, k, v, v, v, _score_bias(p, dil, False))


def attn_combine(os_, lses, tb=1024):
    t = os_[0].shape[0]

    def body(o0, o1, o2, l0, l1, l2, y_ref, lse_ref):
        a0, a1, a2 = l0[...], l1[...], l2[...]
        m = jnp.maximum(jnp.maximum(a0, a1), a2)
        e0, e1, e2 = jnp.exp(a0 - m), jnp.exp(a1 - m), jnp.exp(a2 - m)
        den = e0 + e1 + e2
        y_ref[...] = (e0 * o0[...] + e1 * o1[...] + e2 * o2[...]) / den
        lse_ref[...] = m + jnp.log(den)

    blk = pl.BlockSpec((tb, 256), lambda i: (i, 0))
    return pl.pallas_call(
        body, name="attn_combine", grid=(t // tb,), in_specs=[blk] * 6, out_specs=[blk, blk],
        out_shape=[jax.ShapeDtypeStruct((t, 256), f32)] * 2,
        compiler_params=_cparams(("parallel",)))(*os_, *lses)


def attn_delta(dy, y, tb=1024):
    t = dy.shape[0]

    def body(dy_ref, y_ref, d_ref):
        pr = dy_ref[...] * y_ref[...]
        lane_head = _lane_head(pr.shape)
        out = jnp.zeros_like(pr)
        for j in range(HEADS_PER_PATTERN):
            sj = jnp.sum(jnp.where(lane_head == j, pr, 0.0), axis=1, keepdims=True)
            out = out + jnp.where(lane_head == j, sj, 0.0)
        d_ref[...] = out

    blk = pl.BlockSpec((tb, 256), lambda i: (i, 0))
    return pl.pallas_call(body, name="attn_delta", grid=(t // tb,), in_specs=[blk, blk], out_specs=blk,
                          out_shape=jax.ShapeDtypeStruct((t, 256), f32),
                          compiler_params=_cparams(("parallel",)))(dy, y)


def attn_dq(q, k, v, dy, lse, delta, p, dil, name):
    l = q.shape[0]
    nq = l // 128

    def body(q_ref, kp_ref, ko_ref, kn_ref, vp_ref, vo_ref, vn_ref, dy_ref, lse_ref, d_ref, bias_ref, dq_ref):
        kcat = jnp.concatenate([kp_ref[...], ko_ref[...], kn_ref[...]], axis=0)
        vcat = jnp.concatenate([vp_ref[...], vo_ref[...], vn_ref[...]], axis=0)
        s = _dot_nt(_stack_heads(q_ref[...] * 0.125), kcat) + bias_ref[0]
        pr = jnp.exp(s - _head_cols(lse_ref[...], 128))
        dp = _dot_nt(_stack_heads(dy_ref[...]), vcat)
        ds = pr * (dp - _head_cols(d_ref[...], 128))
        dq_ref[...] = (_unstack_heads(_dot(ds, kcat), 128) * 0.125).astype(dq_ref.dtype)

    col = lambda r: r
    own = pl.BlockSpec((128, 256), lambda r, i: (i, r))
    return pl.pallas_call(
        body, name=name, grid=(dil, nq),
        in_specs=[own] + _win_specs(nq, col) + _win_specs(nq, col) + [own, own, own, _bias_spec(nq, 512, 256)],
        out_specs=own, out_shape=jax.ShapeDtypeStruct(q.shape, bf16),
        compiler_params=_cparams(("parallel", "parallel")))(q, k, k, k, v, v, v, dy, lse, delta, _score_bias(p, dil, False))


def attn_dkv(q, k, v, dy, lse, delta, p, dil, name):
    l = q.shape[0]
    nq = l // 128

    def body(qp_ref, qo_ref, qn_ref, gp_ref, go_ref, gn_ref, lp_ref, lo_ref, ln_ref, dp_ref, do_ref, dn_ref,
             k_ref, v_ref, bias_ref, dk_ref, dv_ref):
        cat = lambda a, b, c: jnp.concatenate([a[...], b[...], c[...]], axis=0)
        q4 = _stack_heads(cat(qp_ref, qo_ref, qn_ref) * 0.125)
        dy4 = _stack_heads(cat(gp_ref, go_ref, gn_ref))
        lse4 = _head_cols(cat(lp_ref, lo_ref, ln_ref), 256)
        del4 = _head_cols(cat(dp_ref, do_ref, dn_ref), 256)
        s = _dot_nt(q4, k_ref[...]) + bias_ref[0]
        pr = jnp.exp(s - lse4)
        dpm = _dot_nt(dy4, v_ref[...])
        ds = pr * (dpm - del4)
        dv_ref[...] = _dot_tn(pr, dy4).astype(dv_ref.dtype)
        dk_ref[...] = _dot_tn(ds, q4).astype(dk_ref.dtype)

    col = lambda r: r
    own = pl.BlockSpec((128, 256), lambda r, i: (i, r))
    win = _win_specs(nq, col)
    return pl.pallas_call(
        body, name=name, grid=(dil, nq), in_specs=win * 4 + [own, own, _bias_spec(nq, 1024, 128)], out_specs=[own, own],
        out_shape=[jax.ShapeDtypeStruct(q.shape, bf16)] * 2,
        compiler_params=_cparams(("parallel", "parallel")))(q, q, q, dy, dy, dy, lse, lse, lse, delta, delta, delta, k, v,
                                                            _score_bias(p, dil, True))


def _lanes(v, reps):
    return v if reps == 1 else jnp.tile(v, (1, reps))


def _lane_halo_specs(cb, tb, nt, off=0):
    r = tb // 128
    return [pl.BlockSpec((cb, 128), lambda j, i: (j + off, jnp.maximum(i * r - 1, 0))),
            pl.BlockSpec((cb, tb), lambda j, i: (j + off, i)),
            pl.BlockSpec((cb, 128), lambda j, i: (j + off, jnp.minimum((i + 1) * r, nt * r - 1)))]


def _with_lane_halo(prev_ref, own_ref, next_ref, i, nt):
    prev = jnp.where(i > 0, prev_ref[...].astype(f32), 0.0)
    nxt = jnp.where(i < nt - 1, next_ref[...].astype(f32), 0.0)
    return jnp.concatenate([prev, own_ref[...].astype(f32), nxt], axis=1)


def _lane_shifted(xcat, s, tb):
    n = xcat.shape[1]
    return pltpu.roll(xcat, (-s) % n, 1)[:, 128:128 + tb]


def conv_fwd_t(xbc_t, w_b, b_b, tb=1024, cb=256):
    c, t = xbc_t.shape
    nt = t // tb

    def body(prev_ref, own_ref, next_ref, w_ref, b_ref, o_ref, ds_ref):
        i = pl.program_id(1)
        xcat = _with_lane_halo(prev_ref, own_ref, next_ref, i, nt)
        reps = tb // 128
        pre = _lanes(b_ref[...], reps)
        for k in range(D_CONV):
            pre = pre + _lanes(w_ref[k], reps) * _lane_shifted(xcat, k - 2, tb)
        sg = _sigmoid(pre)
        o_ref[...] = pre * sg
        ds_ref[...] = sg * (1.0 + pre * (1.0 - sg))

    blk = pl.BlockSpec((cb, tb), lambda j, i: (j, i))
    return pl.pallas_call(
        body, name="conv_fwd", grid=(c // cb, nt),
        in_specs=_lane_halo_specs(cb, tb, nt) + [pl.BlockSpec((D_CONV, cb, 128), lambda j, i: (0, j, 0)),
                                                 pl.BlockSpec((cb, 128), lambda j, i: (j, 0))],
        out_specs=[blk, blk], out_shape=[jax.ShapeDtypeStruct((c, t), f32)] * 2,
        compiler_params=_cparams(("parallel", "parallel")))(xbc_t, xbc_t, xbc_t, w_b, b_b)


def conv_bwd_t(xbc_t, dsilu_t, grad_t, w_b, into, name, row0, tb=1024, cb=256):
    c, t = grad_t.shape
    nt = t // tb
    off = row0 // cb
    off_out = (D_INNER + row0) // cb
    reps = tb // 128

    def body(*refs):
        i = pl.program_id(1)
        x_ref, sr, gr = refs[0], refs[1:4], refs[4:7]
        w_ref = refs[7]
        dx_ref, dw_ref, db_ref = refs[-3:]
        wk = [_lanes(w_ref[k], reps) for k in range(D_CONV)]
        dpre = _with_lane_halo(*gr, i, nt) * _with_lane_halo(*sr, i, nt)

        def fold(v):
            s = v[:, 0:128]
            for q in range(1, reps):
                s = s + v[:, 128 * q:128 * (q + 1)]
            return s

        @pl.when(i == 0)
        def _():
            dw_ref[...] = jnp.zeros_like(dw_ref)
            db_ref[...] = jnp.zeros_like(db_ref)

        x_own = x_ref[...]
        dx = None
        for k in range(D_CONV):
            shifted = _lane_shifted(dpre, 2 - k, tb)
            term = wk[k] * shifted
            dx = term if dx is None else dx + term
            dw_ref[k] += fold(shifted * x_own)
        dx_ref[...] = dx.astype(dx_ref.dtype)
        db_ref[...] += fold(dpre[:, 128:128 + tb])

    in_specs = ([pl.BlockSpec((cb, tb), lambda j, i: (j + off, i))] + _lane_halo_specs(cb, tb, nt, off)
                + _lane_halo_specs(cb, tb, nt)
                + [pl.BlockSpec((D_CONV, cb, 128), lambda j, i: (0, j + off, 0)), pl.BlockSpec(memory_space=pl.ANY)])
    args = [xbc_t] + [dsilu_t] * 3 + [grad_t] * 3 + [w_b, into]
    return pl.pallas_call(
        body, name=name, grid=(c // cb, nt), in_specs=in_specs,
        out_specs=[pl.BlockSpec((cb, tb), lambda j, i: (j + off_out, i)),
                   pl.BlockSpec((D_CONV, cb, 128), lambda j, i: (0, j, 0)), pl.BlockSpec((cb, 128), lambda j, i: (j, 0))],
        out_shape=[jax.ShapeDtypeStruct(into.shape, into.dtype), jax.ShapeDtypeStruct((D_CONV, c, 128), f32),
                   jax.ShapeDtypeStruct((c, 128), f32)],
        input_output_aliases={8: 0}, compiler_params=_cparams(("parallel", "arbitrary")))(*args)


def dt_fwd_t(u_dt_t, bias_b, tb=2048):
    r, t = u_dt_t.shape

    def body(u_ref, b_ref, o_ref):
        v = u_ref[...] + _lanes(b_ref[...], tb // 128)
        o_ref[...] = jnp.maximum(v, 0.0) + jnp.log(1.0 + jnp.exp(-jnp.abs(v)))

    return pl.pallas_call(
        body, name="dt_fwd", grid=(t // tb,),
        in_specs=[pl.BlockSpec((r, tb), lambda i: (0, i)), pl.BlockSpec((r, 128), lambda i: (0, 0))],
        out_specs=pl.BlockSpec((r, tb), lambda i: (0, i)), out_shape=jax.ShapeDtypeStruct((r, t), f32),
        compiler_params=_cparams(("parallel",)))(u_dt_t, bias_b)


def dt_bwd_t(ddt_f, ddt_b, u_dt_t, bias_b, into, tb=2048):
    r, t = u_dt_t.shape
    reps = tb // 128
    row_blk = (SSD_COLS - r) // r

    def body(gf_ref, gb_ref, u_ref, b_ref, into_ref, du_ref, db_ref):
        g = jnp.concatenate([gf_ref[...], gb_ref[...]], axis=0)
        du = g * _sigmoid(u_ref[...] + _lanes(b_ref[...], reps))
        du_ref[...] = du.astype(du_ref.dtype)

        @pl.when(pl.program_id(0) == 0)
        def _():
            db_ref[...] = jnp.zeros_like(db_ref)

        s = du[:, 0:128]
        for q in range(1, reps):
            s = s + du[:, 128 * q:128 * (q + 1)]
        db_ref[...] += s

    half = pl.BlockSpec((r // 2, tb), lambda i: (0, i))
    return pl.pallas_call(
        body, name="dt_bwd", grid=(t // tb,),
        in_specs=[half, half, pl.BlockSpec((r, tb), lambda i: (0, i)), pl.BlockSpec((r, 128), lambda i: (0, 0)),
                  pl.BlockSpec(memory_space=pl.ANY)],
        out_specs=[pl.BlockSpec((r, tb), lambda i: (row_blk, i)), pl.BlockSpec((r, 128), lambda i: (0, 0))],
        out_shape=[jax.ShapeDtypeStruct(into.shape, into.dtype), jax.ShapeDtypeStruct((r, 128), f32)],
        input_output_aliases={4: 0}, compiler_params=_cparams(("arbitrary",)))(ddt_f, ddt_b, u_dt_t, bias_b, into)


HEADS_PER_GROUP = SSD_HEADS // SSD_GROUPS


def _group_rows(g, n):
    return pl.ds(pl.multiple_of(g * n, n), n)


def _ssd_decays(dt_blk, a_blk, reverse):
    row = lax.broadcasted_iota(jnp.int32, (CHUNK, CHUNK), 0)
    col = lax.broadcasted_iota(jnp.int32, (CHUNK, CHUNK), 1)
    mask = (row <= col) if reverse else (row >= col)
    tri = mask.astype(f32)
    a8 = dt_blk * a_blk
    a = jnp.concatenate([a8, jnp.zeros((CHUNK - HEADS_PER_GROUP, CHUNK), f32)], axis=0).T
    acs = _dot_exact(tri, a)
    return mask, tri, a8, acs, acs.T, col


def ssd_fwd_t(xbc_ct, dt_t, a_b, reverse, name, prev=None, tail=None):
    t = xbc_ct.shape[1]
    nc = t // CHUNK
    direction = 1 if reverse else 0

    def cidx(c):
        return nc - 1 - c if reverse else c

    def body(*refs):
        x_ref, b_ref, c_ref, dt_ref, a_ref = refs[0:5]
        pos = 5
        prev_ref = None
        if prev is not None:
            prev_ref = refs[pos]
            pos += 1
        if tail is not None:
            z_ref, skip_ref, nw_ref = refs[pos:pos + 3]
            pos += 3
            y_ref, hp_ref, yn_ref, h_scr = refs[pos:pos + 4]
        else:
            y_ref, hp_ref, h_scr = refs[pos:pos + 3]

        @pl.when(pl.program_id(0) == 0)
        def _():
            h_scr[...] = jnp.zeros_like(h_scr)

        def group(g, carry):
            x_v, y_v = x_ref.at[_group_rows(g, 512)], y_ref.at[_group_rows(g, 512)]
            heads = _group_rows(g, HEADS_PER_GROUP)
            hp_v, h_v = hp_ref.at[0, heads], h_scr.at[heads]
            dt_blk = dt_ref[heads, :]
            mask, tri, a8, acs, acs_t, lane = _ssd_decays(dt_blk, a_ref[heads, :], reverse)
            bm = b_ref[_group_rows(g, 128), :].T
            cm = c_ref[_group_rows(g, 128), :].T
            cb = _dot_nt(cm, bm)
            tot = jnp.sum(a8, axis=1, keepdims=True)
            for j in range(HEADS_PER_GROUP):
                rows = slice(SSD_HEAD_DIM * j, SSD_HEAD_DIM * (j + 1))
                col_j = _lane_col(acs, lane, j)
                row_j = acs_t[j:j + 1, :]
                lmat = jnp.where(mask, jnp.exp(jnp.where(mask, col_j - row_j, 0.0)), 0.0)
                xdt = x_v[rows, :] * dt_blk[j:j + 1, :]
                hp = h_v[j]
                hp_v[j] = hp
                y = _dot_nt(xdt, cb * lmat) + _dot_nt(hp, cm) * jnp.exp(row_j)
                if prev_ref is not None:
                    y = y + prev_ref.at[_group_rows(g, 512)][rows, :]
                y_v[rows, :] = y
                tot_j = tot[j:j + 1, :]
                h_v[j] = jnp.exp(tot_j) * hp + _dot(xdt * jnp.exp(tot_j - row_j), bm)
            if tail is not None:
                rows = _group_rows(g, 512)
                zz = z_ref[rows, :]
                yg = (y_v[...] + skip_ref[rows, :] * x_v[...]) * (zz * _sigmoid(zz))
                rstd = lax.rsqrt(jnp.mean(yg * yg, axis=0, keepdims=True) + NORM_EPS)
                yn_ref[rows, :] = (yg * rstd * nw_ref[rows, :]).astype(yn_ref.dtype)
            return carry

        lax.fori_loop(0, SSD_GROUPS, group, 0, unroll=True)

    big = pl.BlockSpec((D_INNER, CHUNK), lambda c: (0, cidx(c)))
    par = pl.BlockSpec((D_INNER, 128), lambda c: (0, 0))
    in_specs = [big, pl.BlockSpec((512, CHUNK), lambda c: (4, cidx(c))), pl.BlockSpec((512, CHUNK), lambda c: (5, cidx(c))),
                pl.BlockSpec((SSD_HEADS, CHUNK), lambda c: (direction, cidx(c))),
                pl.BlockSpec((SSD_HEADS, 128), lambda c: (direction, 0))]
    args = [xbc_ct, xbc_ct, xbc_ct, dt_t, a_b]
    out_specs = [big, pl.BlockSpec((1, SSD_HEADS, SSD_HEAD_DIM, D_STATE), lambda c: (cidx(c), 0, 0, 0))]
    out_shape = [jax.ShapeDtypeStruct((D_INNER, t), f32), jax.ShapeDtypeStruct((nc, SSD_HEADS, SSD_HEAD_DIM, D_STATE), f32)]
    if prev is not None:
        in_specs.append(big)
        args.append(prev)
    if tail is not None:
        in_specs += [big, par, par]
        args += list(tail)
        out_specs.append(big)
        out_shape.append(jax.ShapeDtypeStruct((D_INNER, t), bf16))
    return pl.pallas_call(
        body, name=name, grid=(nc,), in_specs=in_specs, out_specs=out_specs, out_shape=out_shape,
        scratch_shapes=[pltpu.VMEM((SSD_HEADS, SSD_HEAD_DIM, D_STATE), f32)],
        compiler_params=_cparams(("arbitrary",)))(*args)


def ssd_bwd_t(xbc_ct, dt_t, a_b, dy_t, hprev, reverse, name, skip_b=None, prev=None, tail=None):
    t = xbc_ct.shape[1]
    nc = t // CHUNK
    direction = 1 if reverse else 0

    def cidx(c):
        return c if reverse else nc - 1 - c

    def body(*refs):
        x_ref, b_ref, c_ref, dt_ref, a_ref, dy_ref, hp_ref = refs[0:7]
        pos = 7
        skip_ref = None
        if skip_b is not None:
            skip_ref = refs[pos]
            pos += 1
        prev_refs = None
        if prev is not None:
            prev_refs = refs[pos:pos + 3]
            pos += 3
        if tail is not None:
            ys_ref, z_ref, nw_ref = refs[pos:pos + 3]
            pos += 3
        dx_ref, db_ref, dc_ref, ddt_ref, da_ref = refs[pos:pos + 5]
        pos += 5
        if tail is not None:
            dyout_ref, dz_ref, dnw_ref, ddx_ref = refs[pos:pos + 4]
            pos += 4
        dh_scr = refs[pos]

        @pl.when(pl.program_id(0) == 0)
        def _():
            dh_scr[...] = jnp.zeros_like(dh_scr)
            da_ref[...] = jnp.zeros_like(da_ref)
            if tail is not None:
                dnw_ref[...] = jnp.zeros_like(dnw_ref)
                ddx_ref[...] = jnp.zeros_like(ddx_ref)

        def group(g, carry):
            big, st, heads = _group_rows(g, 512), _group_rows(g, 128), _group_rows(g, HEADS_PER_GROUP)
            x_v, dy_v, dx_v = x_ref.at[big], dy_ref.at[big], dx_ref.at[big]
            hp_v, dh_v = hp_ref.at[0, heads], dh_scr.at[heads]
            dy_grp = None
            if tail is not None:
                zz = z_ref[big, :]
                sg = _sigmoid(zz)
                sl = zz * sg
                x_all = x_v[...]
                y = ys_ref[big, :] + skip_ref[big, :] * x_all
                yz = y * sl
                rstd = lax.rsqrt(jnp.mean(yz * yz, axis=0, keepdims=True) + NORM_EPS)
                yhat = yz * rstd
                gy = dy_v[...]
                dyhat = gy * nw_ref[big, :]
                dyz = rstd * (dyhat - yhat * jnp.mean(dyhat * yhat, axis=0, keepdims=True))
                dy_grp = dyz * sl
                dyout_ref[big, :] = dy_grp
                dz_ref[big, :] = (dyz * y * sg * (1.0 + zz * (1.0 - sg))).astype(dz_ref.dtype)
                dnw_ref[big, :] += gy * yhat
                ddx_ref[big, :] += dy_grp * x_all
            dt_blk = dt_ref[heads, :]
            a_blk = a_ref[heads, :]
            mask, tri, a8, acs, acs_t, lane = _ssd_decays(dt_blk, a_blk, reverse)
            sub = lax.broadcasted_iota(jnp.int32, (CHUNK, CHUNK), 0)
            mask_t = (sub >= lane) if reverse else (sub <= lane)
            bm = b_ref[st, :].T
            cm = c_ref[st, :].T
            cb = _dot_nt(cm, bm)
            cb_t = _dot_nt(bm, cm)
            tot = jnp.sum(a8, axis=1, keepdims=True)
            dcb = jnp.zeros((CHUNK, CHUNK), f32)
            dbm = jnp.zeros((CHUNK, D_STATE), f32)
            dcm = jnp.zeros((CHUNK, D_STATE), f32)
            dacs_rows, ddtx_rows = [], []
            for j in range(HEADS_PER_GROUP):
                rows = slice(SSD_HEAD_DIM * j, SSD_HEAD_DIM * (j + 1))
                col_j = _lane_col(acs, lane, j)
                row_j = acs_t[j:j + 1, :]
                dt_j = dt_blk[j:j + 1, :]
                tot_j = tot[j:j + 1, :]
                lmat = jnp.where(mask, jnp.exp(jnp.where(mask, col_j - row_j, 0.0)), 0.0)
                lmat_t = jnp.where(mask_t, jnp.exp(jnp.where(mask_t, row_j - col_j, 0.0)), 0.0)
                x = x_v[rows, :]
                xdt = x * dt_j
                dyh = dy_v[rows, :] if dy_grp is None else dy_grp[rows]
                hp = hp_v[j]
                dhn = dh_v[j]
                ml = _dot_tn(dyh, xdt) * lmat
                w_t = _dot_tn(xdt, dyh) * lmat_t * cb_t
                dcb = dcb + ml
                dacs = jnp.sum(w_t, axis=0, keepdims=True) - jnp.sum(ml * cb, axis=0, keepdims=True)
                ecol = jnp.exp(row_j)
                dec = jnp.exp(tot_j - row_j)
                dye = dyh * ecol
                yoff = _dot_nt(hp, cm) * ecol
                gmat = _dot_nt(dhn, bm)
                dxdt = _dot(dyh, cb * lmat) + dec * gmat
                s_dec = jnp.sum(xdt * gmat, axis=0, keepdims=True) * dec
                dacs = dacs + jnp.sum(dyh * yoff, axis=0, keepdims=True) - s_dec
                dcd = jnp.sum(jnp.sum(dhn * hp, axis=1, keepdims=True), axis=0, keepdims=True)
                dtot = jnp.sum(s_dec, axis=1, keepdims=True) + jnp.exp(tot_j) * dcd
                dacs_rows.append((dacs, dtot))
                ddtx_rows.append(jnp.sum(dxdt * x, axis=0, keepdims=True))
                dcm = dcm + _dot_tn(dye, hp)
                dbm = dbm + _dot_tn(xdt * dec, dhn)
                dxh = dxdt * dt_j
                if skip_ref is not None:
                    dxh = dxh + skip_ref.at[big][rows, :] * dyh
                if prev_refs is not None:
                    dxh = dxh + prev_refs[0].at[big][rows, :]
                dx_v[rows, :] = dxh
                dh_v[j] = jnp.exp(tot_j) * dhn + _dot(dye, cm)
            dcm = dcm + _dot(dcb, bm)
            dbm = dbm + _dot_tn(dcb, cm)
            dbt, dct = dbm.T, dcm.T
            if prev_refs is not None:
                dbt = dbt + prev_refs[1][st, :]
                dct = dct + prev_refs[2][st, :]
            db_ref[st, :] = dbt
            dc_ref[st, :] = dct
            dacs8 = jnp.concatenate([d for d, _ in dacs_rows], axis=0)
            dtot8 = jnp.concatenate([d for _, d in dacs_rows], axis=0)
            da8 = _dot_exact(dacs8, tri) + dtot8
            ddt_ref[heads, :] = da8 * a_blk + jnp.concatenate(ddtx_rows, axis=0)
            da_ref[heads, :] += da8 * dt_blk
            return carry

        lax.fori_loop(0, SSD_GROUPS, group, 0, unroll=True)

    big = pl.BlockSpec((D_INNER, CHUNK), lambda c: (0, cidx(c)))
    st = pl.BlockSpec((512, CHUNK), lambda c: (0, cidx(c)))
    in_specs = [big, pl.BlockSpec((512, CHUNK), lambda c: (4, cidx(c))), pl.BlockSpec((512, CHUNK), lambda c: (5, cidx(c))),
                pl.BlockSpec((SSD_HEADS, CHUNK), lambda c: (direction, cidx(c))),
                pl.BlockSpec((SSD_HEADS, 128), lambda c: (direction, 0)), big,
                pl.BlockSpec((1, SSD_HEADS, SSD_HEAD_DIM, D_STATE), lambda c: (cidx(c), 0, 0, 0))]
    args = [xbc_ct, xbc_ct, xbc_ct, dt_t, a_b, dy_t, hprev]
    if skip_b is not None:
        in_specs.append(pl.BlockSpec((D_INNER, 128), lambda c: (0, 0)))
        args.append(skip_b)
    if prev is not None:
        in_specs += [big, st, st]
        args += list(prev)
    par = pl.BlockSpec((D_INNER, 128), lambda c: (0, 0))
    out_specs = [big, st, st, pl.BlockSpec((SSD_HEADS, CHUNK), lambda c: (0, cidx(c))),
                 pl.BlockSpec((SSD_HEADS, 128), lambda c: (0, 0))]
    out_shape = [jax.ShapeDtypeStruct((D_INNER, t), f32), jax.ShapeDtypeStruct((512, t), f32),
                 jax.ShapeDtypeStruct((512, t), f32), jax.ShapeDtypeStruct((SSD_HEADS, t), f32),
                 jax.ShapeDtypeStruct((SSD_HEADS, 128), f32)]
    if tail is not None:
        in_specs += [big, big, par]
        args += list(tail)
        out_specs += [big, big, par, par]
        out_shape += [jax.ShapeDtypeStruct((D_INNER, t), f32), jax.ShapeDtypeStruct((SSD_COLS, t), bf16),
                      jax.ShapeDtypeStruct((D_INNER, 128), f32), jax.ShapeDtypeStruct((D_INNER, 128), f32)]
    return pl.pallas_call(
        body, name=name, grid=(nc,), in_specs=in_specs, out_specs=out_specs, out_shape=out_shape,
        scratch_shapes=[pltpu.VMEM((SSD_HEADS, SSD_HEAD_DIM, D_STATE), f32)],
        compiler_params=_cparams(("arbitrary",)))(*args)


def merge_fwd(u_gate, bg_row, y_ssd, y_att, tb=512):
    t = y_ssd.shape[0]

    def body(ga_ref, gb_ref, ba_ref, bb_ref, ys_ref, ya_ref, o_ref):
        o_ref[...] = (_sigmoid(ga_ref[...] + ba_ref[...]) * ys_ref[...]
                      + _sigmoid(gb_ref[...] + bb_ref[...]) * ya_ref[...]).astype(o_ref.dtype)

    blk = pl.BlockSpec((tb, 512), lambda i, j: (i, j))
    blk2 = pl.BlockSpec((tb, 512), lambda i, j: (i, 2 + j))
    row = pl.BlockSpec((1, 512), lambda i, j: (0, j))
    row2 = pl.BlockSpec((1, 512), lambda i, j: (0, 2 + j))
    return pl.pallas_call(
        body, name="merge_fwd", grid=(t // tb, 2), in_specs=[blk, blk2, row, row2, blk, blk], out_specs=blk,
        out_shape=jax.ShapeDtypeStruct((t, D_MODEL), bf16),
        compiler_params=_cparams(("parallel", "parallel")))(u_gate, u_gate, bg_row, bg_row, y_ssd, y_att)


def merge_bwd(dm, u_gate, bg_row, y_ssd, y_att, tb=512):
    t = dm.shape[0]

    def body(dm_ref, ga_ref, gb_ref, ba_ref, bb_ref, ys_ref, ya_ref, dys_ref, dya_ref, dga_ref, dgb_ref, dba_ref, dbb_ref):
        d = dm_ref[...]
        sa = _sigmoid(ga_ref[...] + ba_ref[...])
        sb = _sigmoid(gb_ref[...] + bb_ref[...])
        dys_ref[...] = (d * sa).astype(dys_ref.dtype)
        dya_ref[...] = (d * sb).astype(dya_ref.dtype)
        dla = d * ys_ref[...] * sa * (1.0 - sa)
        dlb = d * ya_ref[...] * sb * (1.0 - sb)
        dga_ref[...] = dla.astype(dga_ref.dtype)
        dgb_ref[...] = dlb.astype(dgb_ref.dtype)

        @pl.when(pl.program_id(1) == 0)
        def _():
            dba_ref[...] = jnp.zeros_like(dba_ref)
            dbb_ref[...] = jnp.zeros_like(dbb_ref)

        dba_ref[...] += jnp.sum(dla, axis=0, keepdims=True)
        dbb_ref[...] += jnp.sum(dlb, axis=0, keepdims=True)

    blk = pl.BlockSpec((tb, 512), lambda j, i: (i, j))
    blk2 = pl.BlockSpec((tb, 512), lambda j, i: (i, 2 + j))
    row = pl.BlockSpec((1, 512), lambda j, i: (0, j))
    row2 = pl.BlockSpec((1, 512), lambda j, i: (0, 2 + j))
    act = jax.ShapeDtypeStruct((t, D_MODEL), bf16)
    vec = jax.ShapeDtypeStruct((1, D_MODEL), f32)
    return pl.pallas_call(
        body, name="merge_bwd", grid=(2, t // tb), in_specs=[blk, blk, blk2, row, row2, blk, blk],
        out_specs=[blk, blk, blk, blk, row, row], out_shape=[act, act, act, act, vec, vec],
        compiler_params=_cparams(("parallel", "arbitrary")))(dm, u_gate, u_gate, bg_row, bg_row, y_ssd, y_att)


def _ln_stats(r):
    mu = jnp.mean(r, axis=1, keepdims=True)
    xc = r - mu
    rstd = lax.rsqrt(jnp.mean(xc * xc, axis=1, keepdims=True) + NORM_EPS)
    return xc * rstd, rstd


def _ln_bwd(dy, xhat, rstd, g_row):
    dxh = dy * g_row
    return rstd * (dxh - jnp.mean(dxh, axis=1, keepdims=True) - xhat * jnp.mean(dxh * xhat, axis=1, keepdims=True))


def ln1_fwd(x, mix, g_row, b_row, tb=512):
    t = x.shape[0]

    def body(x_ref, m_ref, g_ref, b_ref, o_ref, ob_ref):
        xhat, _ = _ln_stats(ALPHA * x_ref[...] + m_ref[...])
        h = xhat * g_ref[...] + b_ref[...]
        o_ref[...] = h
        ob_ref[...] = h.astype(ob_ref.dtype)

    blk = pl.BlockSpec((tb, D_MODEL), lambda i: (i, 0))
    row = pl.BlockSpec((1, D_MODEL), lambda i: (0, 0))
    return pl.pallas_call(body, name="ln1_fwd", grid=(t // tb,), in_specs=[blk, blk, row, row], out_specs=[blk, blk],
                          out_shape=[jax.ShapeDtypeStruct((t, D_MODEL), f32), jax.ShapeDtypeStruct((t, D_MODEL), bf16)],
                          compiler_params=_cparams(("parallel",)))(x, mix, g_row, b_row)


def ln1_bwd(dh, x, mix, g_row, tb=512):
    t = x.shape[0]

    def body(dh_ref, x_ref, m_ref, g_ref, dr_ref, drb_ref, dg_ref, db_ref):
        xhat, rstd = _ln_stats(ALPHA * x_ref[...] + m_ref[...])
        dy = dh_ref[...]
        dr = _ln_bwd(dy, xhat, rstd, g_ref[...])
        dr_ref[...] = dr
        drb_ref[...] = dr.astype(drb_ref.dtype)

        @pl.when(pl.program_id(0) == 0)
        def _():
            dg_ref[...] = jnp.zeros_like(dg_ref)
            db_ref[...] = jnp.zeros_like(db_ref)

        dg_ref[...] += jnp.sum(dy * xhat, axis=0, keepdims=True)
        db_ref[...] += jnp.sum(dy, axis=0, keepdims=True)

    blk = pl.BlockSpec((tb, D_MODEL), lambda i: (i, 0))
    row = pl.BlockSpec((1, D_MODEL), lambda i: (0, 0))
    return pl.pallas_call(
        body, name="ln1_bwd", grid=(t // tb,), in_specs=[blk, blk, blk, row], out_specs=[blk, blk, row, row],
        out_shape=[jax.ShapeDtypeStruct((t, D_MODEL), f32), jax.ShapeDtypeStruct((t, D_MODEL), bf16),
                   jax.ShapeDtypeStruct((1, D_MODEL), f32), jax.ShapeDtypeStruct((1, D_MODEL), f32)],
        compiler_params=_cparams(("arbitrary",)))(dh, x, mix, g_row)


def ln2_loss(h1, f, g_row, b_row, target, tb=512):
    t = h1.shape[0]

    def body(h_ref, f_ref, g_ref, b_ref, t_ref, dr_ref, drb_ref, dg_ref, db_ref, loss_ref):
        xhat, rstd = _ln_stats(ALPHA * h_ref[...] + f_ref[...])
        g = g_ref[...]
        err = xhat * g + b_ref[...] - t_ref[...]
        dy = err * (1.0 / D_MODEL)
        dr = _ln_bwd(dy, xhat, rstd, g)
        dr_ref[...] = dr
        drb_ref[...] = dr.astype(drb_ref.dtype)

        @pl.when(pl.program_id(0) == 0)
        def _():
            dg_ref[...] = jnp.zeros_like(dg_ref)
            db_ref[...] = jnp.zeros_like(db_ref)
            loss_ref[...] = jnp.zeros_like(loss_ref)

        dg_ref[...] += jnp.sum(dy * xhat, axis=0, keepdims=True)
        db_ref[...] += jnp.sum(dy, axis=0, keepdims=True)
        part = jnp.sum(jnp.mean(err * err, axis=1, keepdims=True), axis=0, keepdims=True)
        loss_ref[...] += 0.5 * part

    blk = pl.BlockSpec((tb, D_MODEL), lambda i: (i, 0))
    row = pl.BlockSpec((1, D_MODEL), lambda i: (0, 0))
    return pl.pallas_call(
        body, name="ln2_loss", grid=(t // tb,), in_specs=[blk, blk, row, row, blk],
        out_specs=[blk, blk, row, row, pl.BlockSpec((8, 128), lambda i: (0, 0))],
        out_shape=[jax.ShapeDtypeStruct((t, D_MODEL), f32), jax.ShapeDtypeStruct((t, D_MODEL), bf16),
                   jax.ShapeDtypeStruct((1, D_MODEL), f32), jax.ShapeDtypeStruct((1, D_MODEL), f32),
                   jax.ShapeDtypeStruct((8, 128), f32)],
        compiler_params=_cparams(("arbitrary",)))(h1, f, g_row, b_row, target)


def _adamw_update(g, w_ref, m_ref, v_ref, g_ref, d_ref, nm_ref, nv_ref):
    c1 = 1.0 - ADAM_B1 ** ADAM_STEP
    c2 = 1.0 - ADAM_B2 ** ADAM_STEP
    nm = ADAM_B1 * m_ref[...] + (1.0 - ADAM_B1) * g
    nv = ADAM_B2 * v_ref[...] + (1.0 - ADAM_B2) * (g * g)
    g_ref[...] = g
    nm_ref[...] = nm
    nv_ref[...] = nv
    d_ref[...] = -ADAM_LR * ((nm / c1) / (jnp.sqrt(nv / c2) + ADAM_EPS) + ADAM_WD * w_ref[...])


def adamw_sum8(landed, parts, me, w, m, v, row0, name, tails=None):
    rows = landed.shape[1]
    off = row0 // EARLY_TILE
    tail_blk, tail_at = divmod(OFF_TAIL - row0, EARLY_TILE)

    def body(me_ref, *refs):
        src = refs[0:N_DEV]
        own_ref = refs[N_DEV]
        pos = N_DEV + 1
        mine = me_ref[0]

        def sum8(own, slots):
            g = None
            for s in range(N_DEV):
                term = jnp.where(mine == s, own, slots(s)).astype(f32)
                g = term if g is None else g + term
            return g

        g = sum8(own_ref[0], lambda s: src[s][0])
        if tails is not None:
            tl_ref, tm_ref = refs[pos:pos + 2]
            pos += 2
            own_tail = tm_ref[0]
            for s in range(1, N_DEV):
                own_tail = jnp.where(mine == s, tm_ref[s], own_tail)
            gt = sum8(own_tail, lambda s: tl_ref[s])
            with_tail = jnp.concatenate([g[0:tail_at], gt, g[tail_at + ROWS_TAIL:]], axis=0)
            g = jnp.where(pl.program_id(0) == tail_blk, with_tail, g)
        w_ref, m_ref, v_ref = refs[pos:pos + 3]
        _adamw_update(g, w_ref, m_ref, v_ref, *refs[pos + 3:])

    def slot(s):
        return pl.BlockSpec((1, EARLY_TILE, 1024), lambda i, me_ref: (jnp.where(me_ref[0] == s, (s + 1) % N_DEV, s), i, 0))

    shard = pl.BlockSpec((EARLY_TILE, 1024), lambda i, me_ref: (i + off, 0))
    out_blk = pl.BlockSpec((EARLY_TILE, 1024), lambda i, me_ref: (i, 0))
    in_specs = [slot(s) for s in range(N_DEV)] + [pl.BlockSpec((1, EARLY_TILE, 1024), lambda i, me_ref: (me_ref[0], i, 0))]
    args = [landed] * N_DEV + [parts]
    if tails is not None:
        whole = pl.BlockSpec((N_DEV, ROWS_TAIL, 1024), lambda i, me_ref: (0, 0, 0))
        in_specs += [whole, whole]
        args += list(tails)
    grid_spec = pltpu.PrefetchScalarGridSpec(num_scalar_prefetch=1, grid=(rows // EARLY_TILE,),
                                             in_specs=in_specs + [shard, shard, shard], out_specs=[out_blk] * 4)
    out = jax.ShapeDtypeStruct((rows, 1024), f32)
    return pl.pallas_call(body, name=name, grid_spec=grid_spec, out_shape=[out] * 4,
                          compiler_params=_cparams(("parallel",)))(me, *args, w, m, v)


def _place():
    return lax.axis_index("x"), lax.axis_index("y"), lax.axis_index("c")


def all_gather_blocks(shard):
    rows, cols = shard.shape

    def body(x_ref, out_ref, send_sems, recv_sems, local_sem):
        x, y, c = _place()
        me, sibling = (x, y, c), (x, y, 1 - c)
        chips = [(1 - x, y), (x, 1 - y), (1 - x, 1 - y)]

        def slot(px, py, pc):
            return out_ref.at[4 * px + 2 * py + pc]

        def copy(k, block, to, src=None):
            return pltpu.make_async_remote_copy(
                src_ref=slot(*block) if src is None else src, dst_ref=slot(*block), send_sem=send_sems.at[k],
                recv_sem=recv_sems.at[k], device_id=to, device_id_type=MESH)

        mine = pltpu.make_async_copy(x_ref, slot(*me), local_sem)
        mine.start()
        first = [copy(0, me, sibling, src=x_ref)]
        first += [copy(1 + j, me, (*chip, c), src=x_ref) for j, chip in enumerate(chips)]
        for cp in first:
            cp.start()
        passed = [copy(4 + j, (*chip, c), sibling) for j, chip in enumerate(chips)]
        for j, chip in enumerate(chips):
            copy(1 + j, (*chip, c), me).wait_recv()
            passed[j].start()
        copy(0, sibling, me).wait_recv()
        for j, chip in enumerate(chips):
            copy(4 + j, (*chip, 1 - c), me).wait_recv()
        for cp in first + passed:
            cp.wait_send()
        mine.wait()

    return pl.pallas_call(
        body, name="all_gather_blocks", out_shape=jax.ShapeDtypeStruct((N_DEV, rows, cols), shard.dtype),
        in_specs=[pl.BlockSpec(memory_space=pl.ANY)], out_specs=pl.BlockSpec(memory_space=pl.ANY),
        scratch_shapes=[pltpu.SemaphoreType.DMA((7,)), pltpu.SemaphoreType.DMA((7,)), pltpu.SemaphoreType.DMA],
        compiler_params=pltpu.CompilerParams(has_side_effects=True))(shard)


_HBM = pl.BlockSpec(memory_space=pltpu.HBM)
_SEM = pl.BlockSpec(memory_space=pltpu.SEMAPHORE)


def _peer(k):
    x, y, c = _place()
    px, py, pc = (1 - x if k & 4 else x), (1 - y if k & 2 else y), (1 - c if k & 1 else c)
    return (px, py, pc), 4 * px + 2 * py + pc


def scatter_start(parts, name):
    per_device = parts.ndim == 3

    def body(p_ref, land_ref, send_sems, recv_sems, p_thru, land_thru, token):
        x, y, c = _place()
        me = 4 * x + 2 * y + c
        for k in range(1, N_DEV):
            place, idx = _peer(k)
            pltpu.make_async_remote_copy(src_ref=p_ref.at[idx] if per_device else p_ref, dst_ref=land_ref.at[me],
                                         send_sem=send_sems.at[k - 1], recv_sem=recv_sems.at[k - 1], device_id=place,
                                         device_id_type=MESH).start()
        token[...] = jnp.zeros_like(token)

    land_shape = parts.shape if per_device else (N_DEV,) + parts.shape
    landing = lax.empty(land_shape, parts.dtype)
    return pl.pallas_call(
        body, name=name,
        out_shape=(pltpu.SemaphoreType.DMA((N_DEV - 1,)), pltpu.SemaphoreType.DMA((N_DEV - 1,)),
                   pltpu.HBM(parts.shape, parts.dtype), pltpu.HBM(land_shape, parts.dtype),
                   jax.ShapeDtypeStruct((8, 128), f32)),
        in_specs=(_HBM, _HBM), out_specs=(_SEM, _SEM, _HBM, _HBM, pl.BlockSpec(memory_space=pltpu.VMEM)),
        input_output_aliases={0: 2, 1: 3},
        compiler_params=pltpu.CompilerParams(has_side_effects=pltpu.SideEffectType.DATAFLOW_SIDE_EFFECTING),
    )(pltpu.with_memory_space_constraint(parts, pltpu.HBM), pltpu.with_memory_space_constraint(landing, pltpu.HBM))


def scatter_wait(send_sems, recv_sems, parts_thru, land_thru, after, name):
    per_device = parts_thru.ndim == 3

    def body(p_ref, land_ref, send_sems, recv_sems, after_ref, p_out, land_out):
        for k in range(1, N_DEV):
            place, idx = _peer(k)
            copy = pltpu.make_async_remote_copy(src_ref=p_ref.at[idx] if per_device else p_ref, dst_ref=land_ref.at[idx],
                                                send_sem=send_sems.at[k - 1], recv_sem=recv_sems.at[k - 1],
                                                device_id=place, device_id_type=MESH)
            copy.wait_send()
            copy.wait_recv()

    return pl.pallas_call(
        body, name=name,
        out_shape=(pltpu.HBM(parts_thru.shape, parts_thru.dtype), pltpu.HBM(land_thru.shape, land_thru.dtype)),
        in_specs=(_HBM, _HBM, _SEM, _SEM, pl.BlockSpec(memory_space=pl.ANY)), out_specs=(_HBM, _HBM),
        input_output_aliases={0: 0, 1: 1},
        compiler_params=pltpu.CompilerParams(has_side_effects=pltpu.SideEffectType.DATAFLOW_SIDE_EFFECTING),
    )(parts_thru, land_thru, send_sems, recv_sems, after)


def _tail_rows(conv_part, small, extra):
    lead = conv_part.shape[:-1]
    rep = jnp.concatenate([small[n].reshape(-1).astype(f32) for n in SMALL] + [extra.reshape(1).astype(f32)])
    flat = jnp.concatenate([conv_part, jnp.broadcast_to(rep, lead + rep.shape),
                            jnp.zeros(lead + (ROWS_TAIL * 1024 - TAIL_ELEMS,), f32)], axis=-1)
    return flat.reshape(lead + (ROWS_TAIL, 1024))


def _late_rows(w_in_t, tail):
    lead = tail.shape[:-2]
    zeros = lambda r: jnp.zeros(lead + (r, 1024), f32)
    return jnp.concatenate([w_in_t, zeros(OFF_TAIL - IN_SHARD), tail, zeros(LATE_ROWS - OFF_TAIL - ROWS_TAIL)], axis=-2)


def _early_rows(w_ps, w_out, w_up_t, w_down, w_pa_t):
    return jnp.concatenate([w_ps, w_out, w_up_t, w_down, w_pa_t.reshape(w_pa_t.shape[:-2] + (ROWS_PA, 1024))], axis=-2)


def _pack_shard(vals):
    tail = _tail_rows(vals["conv_w"].reshape(-1), vals, jnp.zeros((), f32))
    return jnp.concatenate([_late_rows(vals["w_in"].T, tail),
                            _early_rows(vals["w_proj_ssd"], vals["w_out"], vals["w_up"].T, vals["w_down"],
                                        vals["w_proj_attn"].T)], axis=0)


def _unpack_shard(late, early):
    e = lambda lo, hi: early[lo - LATE_ROWS:hi - LATE_ROWS]
    out = {"w_in": late[0:IN_SHARD].T, "w_proj_ssd": e(OFF_PS, OFF_OUT), "w_out": e(OFF_OUT, OFF_UP),
           "w_up": e(OFF_UP, OFF_DOWN).T, "w_down": e(OFF_DOWN, OFF_PA),
           "w_proj_attn": e(OFF_PA, PACK_ROWS).reshape(D_MODEL // N_DEV, ATTN_OUT).T}
    flat = late[OFF_TAIL:OFF_TAIL + ROWS_TAIL].reshape(-1)
    out["conv_w"] = flat[0:CONV_SHARD].reshape(D_CONV, CONV_DIM // N_DEV)
    off = CONV_SHARD
    for n in SMALL:
        out[n] = flat[off:off + SMALL_SIZES[n]]
        off += SMALL_SIZES[n]
    out["_extra"] = flat[off]
    return out


def _blocks(g):
    return g.reshape(N_DEV, g.shape[0] // N_DEV, g.shape[1])


def _pack_early_parts(full):
    return _early_rows(_blocks(full["w_proj_ssd"]), _blocks(full["w_out"]), _blocks(full["w_up_t"]),
                       _blocks(full["w_down"]), _blocks(full["w_proj_attn_t"]))


def _pack_late_parts(full, small, extra):
    conv = full["conv_w"].reshape(D_CONV, N_DEV, CONV_DIM // N_DEV).transpose(1, 0, 2).reshape(N_DEV, CONV_SHARD)
    return _late_rows(_blocks(full["w_in_t"]), _tail_rows(conv, small, extra))


def _gather_weights(w):
    conv_bits = lax.bitcast_convert_type(w["conv_w"], bf16).reshape(-1)
    conv_rows = jnp.concatenate([conv_bits, jnp.zeros((16 * 1024 - 2 * CONV_SHARD,), bf16)]).reshape(16, 1024)
    packed = _pack_shard(w)
    first = OFF_TAIL + ROWS_TAIL
    got = all_gather_blocks(jnp.concatenate([packed[0:OFF_TAIL].astype(bf16), conv_rows], axis=0))
    got, rest = lax.optimization_barrier((got, packed[first:].astype(bf16)))
    send_sems, recv_sems, rest_thru, land_thru, token = scatter_start(rest, "gather_start")
    conv =lax.bitcast_convert_type(got[:, OFF_TAIL:OFF_TAIL + 4].reshape(N_DEV, 4096)[:, 0:2 * CONV_SHARD]
                                    .reshape(N_DEV, D_CONV, CONV_DIM // N_DEV, 2), f32)
    now = {"w_in_t": got[:, 0:IN_SHARD].reshape(IN_COLS, 1024), "conv_w": conv.transpose(1, 0, 2).reshape(D_CONV, CONV_DIM)}

    def later(after):
        mine, landed = scatter_wait(send_sems, recv_sems, rest_thru, land_thru, after, "gather_wait")
        x, y, c = _place()
        landed = lax.dynamic_update_slice(landed, mine[None], (4 * x + 2 * y + c, 0, 0))
        whole = lambda lo, hi: landed[:, lo - first:hi - first].reshape(N_DEV * (hi - lo), 1024)
        return {"w_proj_ssd": whole(OFF_PS, OFF_OUT), "w_out": whole(OFF_OUT, OFF_UP), "w_up_t": whole(OFF_UP, OFF_DOWN),
                "w_down": whole(OFF_DOWN, OFF_PA),
                "w_proj_attn_t": landed[:, OFF_PA - first:PACK_ROWS - first].reshape(D_MODEL, ATTN_OUT)}

    return now, later, token


def _row(v, width=None):
    v = v.reshape(1, -1).astype(f32)
    return v if width is None else jnp.pad(v, ((0, 0), (0, width - v.shape[1])))


def _local_step(x2, tgt, wf, p, send_early=None, late_weights=None, start_token=None, send_late=None):
    t = x2.shape[0]
    o = np.cumsum((0,) + IN_SPLITS)
    wt = wf["w_in_t"]
    wt_z, wt_xbc, wt_dt = wt[o[0]:o[1]], wt[o[1]:o[2]], wt[o[2]:o[4]]
    wt_qkv, wt_gate = wt[o[4]:o[7]], wt[o[7]:o[8]]

    spread = lambda v: jnp.broadcast_to(v.astype(f32)[..., None], v.shape + (128,))
    conv_w_b, conv_b_b = spread(wf["conv_w"]), spread(p["conv_b"])
    dt_bias_b = spread(jnp.concatenate([p["dt_bias_f"], p["dt_bias_b"]]))
    a_f, a_b = -jnp.exp(p["a_log_f"].astype(f32)), -jnp.exp(p["a_log_b"].astype(f32))
    a_coef_b = spread(jnp.concatenate([a_f, a_b]))
    skip_b = spread(jnp.repeat(p["d_skip"], SSD_HEAD_DIM))
    nw_b, bg_row = spread(p["ssd_norm_w"]), _row(p["b_gate"])
    g1, b1, g2, b2 = _row(p["ln1_g"]), _row(p["ln1_b"]), _row(p["ln2_g"]), _row(p["ln2_b"])

    xb = (x2 if start_token is None else x2 + start_token[0, 0]).astype(MXU_DTYPE)
    u_z = mm_nt(wt_z, xb, "in_z")
    u_xbc = mm_nt(wt_xbc, xb, "in_xbc")
    u_dt = mm_nt(wt_dt, xb, "in_dt")
    u_qkv = mm_nt_split(xb, wt_qkv, "in_qkv", 256, bf16)
    u_gate = mm_nt(xb, wt_gate, "in_gate")
    xbc_c, dsilu = conv_fwd_t(u_xbc, conv_w_b, conv_b_b)
    dt_t = dt_fwd_t(u_dt, dt_bias_b)
    y_f, h_f = ssd_fwd_t(xbc_c, dt_t, a_coef_b, False, "ssd_fwd_f")
    y_scan, h_b, yn = ssd_fwd_t(xbc_c, dt_t, a_coef_b, True, "ssd_fwd_b", prev=y_f, tail=(u_z, skip_b, nw_b))
    if late_weights is not None:
        wf = {**wf, **late_weights(yn)}
    y_ssd = mm_tn(yn, wf["w_proj_ssd"], "proj_ssd")

    def strided(a, dil):
        return a.reshape(t // dil, dil * 256)

    qkv, outs, lses = [], [], []
    for pi, (_, dil) in enumerate(DIL_PATTERNS):
        q, k, v = (strided(u_qkv[N_PATTERNS * s + pi], dil) for s in range(3))
        qkv.append((q, k, v))
        op, lp = attn_fwd(q, k, v, pi, dil, f"attn_fwd_{pi}")
        outs.append(op.reshape(t, 256))
        lses.append(lp.reshape(t, 256))
    ya, lse = attn_combine(outs, lses)
    y_att = mm_nt(ya, wf["w_proj_attn_t"], "proj_attn")
    m = merge_fwd(u_gate, bg_row, y_ssd, y_att)
    mix = mm_nn(m, wf["w_out"], "out_proj")
    h1, h1b = ln1_fwd(x2, mix, g1, b1)
    r_up, p_act = mm_nt(h1b, wf["w_up_t"], "mlp_up", relu2=True)
    f_dn = mm_nn(p_act, wf["w_down"], "mlp_down")
    dr2, dr2b, dg2, db2, loss8 = ln2_loss(h1, f_dn, g2, b2, tgt)

    full, small = {}, {}
    da = mm_nt(dr2b, wf["w_down"], "d_mlp_act", out_dtype=bf16, relu2_of=r_up)
    full["w_down"] = mm_tn(p_act, dr2b, "dw_down")
    full["w_up_t"] = mm_tn(da, h1b, "dw_up")
    dh1 = mm_nn(da, wf["w_up_t"], "d_h1", acc_in=dr2, acc_scale=ALPHA)
    dr1, dr1b, dg1, db1 = ln1_bwd(dh1, x2, mix, g1)
    dm = mm_nt(dr1b, wf["w_out"], "d_merge")
    full["w_out"] = mm_tn(m, dr1b, "dw_out")
    dys, dya_p, dga, dgb, dba, dbb = merge_bwd(dm, u_gate, bg_row, y_ssd, y_att)
    dyn = mm_nt(wf["w_proj_ssd"], dys, "d_yn")
    full["w_proj_ssd"] = mm_nn(yn, dys, "dw_proj_ssd")
    dya = mm_nn(dya_p, wf["w_proj_attn_t"], "d_ya")
    full["w_proj_attn_t"] = mm_tn(dya_p, ya, "dw_proj_attn")
    if send_early is not None:
        skip_b = skip_b + send_early(full)[0, 0]

    dxf, dbf, dcf, ddtf, daf, dy, du_ssd, dnw, ddx = ssd_bwd_t(xbc_c, dt_t, a_coef_b, dyn, h_f, False, "ssd_bwd_f",
                                                               skip_b=skip_b, tail=(y_scan, u_z, nw_b))
    dxs, dbs, dcs, ddtb, dab = ssd_bwd_t(xbc_c, dt_t, a_coef_b, dy, h_b, True, "ssd_bwd_b", prev=(dxf, dbf, dcf))
    du_ssd, dcw_x, dcb_x = conv_bwd_t(u_xbc, dsilu, dxs, conv_w_b, du_ssd, "conv_bwd_x", 0)
    du_ssd, dcw_b, dcb_b = conv_bwd_t(u_xbc, dsilu, dbs, conv_w_b, du_ssd, "conv_bwd_b", D_INNER)
    du_ssd, dcw_c, dcb_c = conv_bwd_t(u_xbc, dsilu, dcs, conv_w_b, du_ssd, "conv_bwd_c", D_INNER + 512)
    du_ssd, dbias = dt_bwd_t(ddtf, ddtb, u_dt, dt_bias_b, du_ssd)

    delta = attn_delta(dya, ya)
    dqs, dks, dvs = [], [], []
    for pi, (_, dil) in enumerate(DIL_PATTERNS):
        q, k, v = qkv[pi]
        sd, sl_, sdel = strided(dya, dil), strided(lse, dil), strided(delta, dil)
        dqs.append(attn_dq(q, k, v, sd, sl_, sdel, pi, dil, f"attn_dq_{pi}").reshape(t, 256))
        dk, dv = attn_dkv(q, k, v, sd, sl_, sdel, pi, dil, f"attn_dkv_{pi}")
        dks.append(dk.reshape(t, 256))
        dvs.append(dv.reshape(t, 256))
    du_qkv = jnp.concatenate(dqs + dks + dvs, axis=1)
    du_gate = jnp.concatenate([dga, dgb], axis=1)

    full["w_in_t"] = jnp.concatenate(
        [mm_nn(du_ssd, xb, "dw_in_ssd"), mm_tn(du_qkv, xb, "dw_in_qkv"), mm_tn(du_gate, xb, "dw_in_gate")], axis=0)
    lanes = lambda v: jnp.sum(v, axis=-1)
    full["conv_w"] = jnp.concatenate([lanes(dcw_x), lanes(dcw_b), lanes(dcw_c)], axis=1)

    small["b_gate"] = jnp.concatenate([dba, dbb], axis=1)
    small["conv_b"] = jnp.concatenate([lanes(dcb_x), lanes(dcb_b), lanes(dcb_c)])
    dbias = lanes(dbias)
    small["dt_bias_f"], small["dt_bias_b"] = dbias[0:32], dbias[32:64]
    small["a_log_f"] = lanes(daf) * a_f
    small["a_log_b"] = lanes(dab) * a_b
    small["d_skip"] = jnp.sum(lanes(ddx).reshape(SSD_HEADS, SSD_HEAD_DIM), axis=1)
    small["ssd_norm_w"] = lanes(dnw)
    small["ln1_g"], small["ln1_b"], small["ln2_g"], small["ln2_b"] = dg1, db1, dg2, db2

    wt_ssd = wt[0:SSD_COLS]
    if send_late is not None:
        wt_ssd = wt_ssd + send_late(full, small, loss8[0, 0])[0, 0].astype(wt_ssd.dtype)
    dx = mm_tn(du_ssd, wt_ssd, "dx_ssd", acc_in=dr1, acc_scale=ALPHA)
    dx = mm_nn(du_qkv, wt_qkv, "dx_qkv", acc_in=dx)
    dx = mm_nn(du_gate, wt_gate, "dx_gate", acc_in=dx)
    return loss8[0, 0], dx, full, small


def kernel(x, w_in, b_gate, conv_w, conv_b, dt_bias_f, dt_bias_b, a_log_f, a_log_b, d_skip, ssd_norm_w, w_proj_ssd, w_proj_attn, w_out, ln1_g, ln1_b, w_up, w_down, ln2_g, ln2_b, loss_target, m_w_in, m_b_gate, m_conv_w, m_conv_b, m_dt_bias_f, m_dt_bias_b, m_a_log_f, m_a_log_b, m_d_skip, m_ssd_norm_w, m_w_proj_ssd, m_w_proj_attn, m_w_out, m_ln1_g, m_ln1_b, m_w_up, m_w_down, m_ln2_g, m_ln2_b, v_w_in, v_b_gate, v_conv_w, v_conv_b, v_dt_bias_f, v_dt_bias_b, v_a_log_f, v_a_log_b, v_d_skip, v_ssd_norm_w, v_w_proj_ssd, v_w_proj_attn, v_w_out, v_ln1_g, v_ln1_b, v_w_up, v_w_down, v_ln2_g, v_ln2_b):
    given = dict(locals())
    w = {n: given[n] for n in WEIGHTS}
    mom = {n: given["m_" + n] for n in WEIGHTS}
    var = {n: given["v_" + n] for n in WEIGHTS}
    t = x.shape[1]
    wf, late_weights, start_token = _gather_weights(w)
    in_flight = []

    def send_early(full):
        send_sems, recv_sems, parts_thru, land_thru, token = scatter_start(_pack_early_parts(full), "scatter_start")
        in_flight.append((send_sems, recv_sems, parts_thru, land_thru))
        return token

    def send_late(full, small, loss):
        late = _pack_late_parts(full, small, loss)
        rows = scatter_start(late.astype(bf16), "late_start")
        tail = scatter_start(late[:, OFF_TAIL:OFF_TAIL + ROWS_TAIL], "tail_start")
        in_flight.extend([rows[0:4], tail[0:4]])
        return rows[4] + tail[4]

    loss, dx, full, small = _local_step(x.reshape(t, D_MODEL), loss_target.reshape(t, D_MODEL), wf, w, send_early,
                                        late_weights, start_token, send_late)
    x_, y_, c_ = _place()
    me = (4 * x_ + 2 * y_ + c_).astype(jnp.int32).reshape(1)
    wp, mp, vp = _pack_shard(w), _pack_shard(mom), _pack_shard(var)
    early_parts, early_landed = scatter_wait(*in_flight[0], dx, "scatter_wait")
    early_out = adamw_sum8(early_landed, early_parts, me, wp, mp, vp, LATE_ROWS, "adamw_early")
    late_parts, late_landed = scatter_wait(*in_flight[1], dx, "late_wait")
    tail_parts, tail_landed = scatter_wait(*in_flight[2], dx, "tail_wait")
    late_out = adamw_sum8(late_landed, late_parts, me, wp, mp, vp, 0, "adamw_late", tails=(tail_landed, tail_parts))
    g, delta, new_m, new_v = (_unpack_shard(a, b) for a, b in zip(late_out, early_out))
    outs = [g["_extra"], dx.reshape(x.shape)]
    for d in (g, delta, new_m, new_v):
        outs += [d[n].reshape(w[n].shape) for n in WEIGHTS]
    return tuple(outs)
```

```python
import jax
import jax.numpy as jnp
import numpy as np
from jax import lax
from jax.experimental import pallas as pl
from jax.experimental.pallas import tpu as pltpu

f32 = jnp.float32
bf16 = jnp.bfloat16
MXU_DTYPE = jnp.bfloat16

N_DEV = 8
D_MODEL = 1024
D_INNER = 2048
SSD_HEADS = 32
SSD_HEAD_DIM = 64
SSD_GROUPS = 4
D_STATE = 128
D_CONV = 5
CHUNK = 128
CONV_DIM = D_INNER + 2 * SSD_GROUPS * D_STATE
NORM_EPS = 1e-5
ATTN_HEAD_DIM = 64
DIL_PATTERNS = ((128, 1), (512, 4), (2048, 16))
N_PATTERNS = len(DIL_PATTERNS)
HEADS_PER_PATTERN = 4
ATTN_HEADS = 12
ATTN_WIDTH = 768
ATTN_OUT = 256
D_FF = 4096
ALPHA = 2.0 ** 0.25
IN_SPLITS = (D_INNER, CONV_DIM, SSD_HEADS, SSD_HEADS, ATTN_WIDTH, ATTN_WIDTH, ATTN_WIDTH, 2 * D_MODEL)
IN_COLS = sum(IN_SPLITS)
SSD_COLS = sum(IN_SPLITS[0:4])
ADAM_LR, ADAM_B1, ADAM_B2, ADAM_EPS, ADAM_WD, ADAM_STEP = 0.001, 0.9, 0.999, 1e-08, 0.01, 10
NEG_BIG = -1e30
VMEM_LIMIT = 56 * 1024 * 1024
MESH = pl.DeviceIdType.MESH

SMALL = ("b_gate", "conv_b", "dt_bias_f", "dt_bias_b", "a_log_f", "a_log_b", "d_skip", "ssd_norm_w",
         "ln1_g", "ln1_b", "ln2_g", "ln2_b")
WEIGHTS = ("w_in", "b_gate", "conv_w", "conv_b", "dt_bias_f", "dt_bias_b", "a_log_f", "a_log_b", "d_skip",
           "ssd_norm_w", "w_proj_ssd", "w_proj_attn", "w_out", "ln1_g", "ln1_b", "w_up", "w_down", "ln2_g", "ln2_b")
SMALL_SIZES = {"b_gate": 2 * D_MODEL, "conv_b": CONV_DIM, "dt_bias_f": 32, "dt_bias_b": 32, "a_log_f": 32, "a_log_b": 32,
               "d_skip": 32, "ssd_norm_w": D_INNER, "ln1_g": D_MODEL, "ln1_b": D_MODEL, "ln2_g": D_MODEL, "ln2_b": D_MODEL}
IN_SHARD = IN_COLS // N_DEV
OFF_TAIL = 1200
ROWS_TAIL = 16
LATE_ROWS = 1280
ROWS_PS, ROWS_OUT, ROWS_UP, ROWS_DOWN, ROWS_PA = D_INNER // N_DEV, D_MODEL // N_DEV, D_FF // N_DEV, D_FF // N_DEV, 32
OFF_PS = LATE_ROWS
OFF_OUT = OFF_PS + ROWS_PS
OFF_UP = OFF_OUT + ROWS_OUT
OFF_DOWN = OFF_UP + ROWS_UP
OFF_PA = OFF_DOWN + ROWS_DOWN
PACK_ROWS = OFF_PA + ROWS_PA
EARLY_ROWS = PACK_ROWS - LATE_ROWS
EARLY_TILE = 160
CONV_SHARD = D_CONV * CONV_DIM // N_DEV
TAIL_ELEMS = CONV_SHARD + sum(SMALL_SIZES.values()) + 1


def _cparams(sem=None, **kw):
    return pltpu.CompilerParams(dimension_semantics=sem, vmem_limit_bytes=VMEM_LIMIT, **kw)


def _mx(v):
    return v.astype(MXU_DTYPE)


def _dot(a, b):
    return jnp.dot(_mx(a), _mx(b), preferred_element_type=f32)


def _dot_nt(a, b):
    return lax.dot_general(_mx(a), _mx(b), (((1,), (1,)), ((), ())), preferred_element_type=f32)


def _dot_tn(a, b):
    return lax.dot_general(_mx(a), _mx(b), (((0,), (0,)), ((), ())), preferred_element_type=f32)


def _dot_exact(a, b):
    return jnp.dot(a, b, precision=lax.Precision.HIGHEST, preferred_element_type=f32)


def _sigmoid(v):
    return 1.0 / (1.0 + jnp.exp(-v))


def _pick(n, prefs):
    for p in prefs:
        if n % p == 0:
            return p
    return n


MM_TILE = 1024


def mm_nn(a, b, name, out_dtype=f32, acc_in=None, acc_scale=1.0):
    m, k = a.shape
    n = b.shape[1]
    tm = _pick(m, (MM_TILE, 576, 512, 256, 128, 64))
    tn = _pick(n, (MM_TILE, 512, 256, 128))
    tk = _pick(k, (2048, 1536, 1152, 1024, 768, 512, 256, 128))
    nk = k // tk

    def body(*refs):
        a_ref, b_ref = refs[0:2]
        c_ref = refs[2] if acc_in is not None else None
        o_ref = refs[3] if acc_in is not None else refs[2]

        def finish(r):
            if acc_in is not None:
                r = r + acc_scale * c_ref[...]
            o_ref[...] = r.astype(o_ref.dtype)

        if nk == 1:
            finish(_dot(a_ref[...], b_ref[...]))
            return
        acc_ref = refs[-1]
        kk = pl.program_id(2)

        @pl.when(kk == 0)
        def _():
            acc_ref[...] = jnp.zeros_like(acc_ref)

        acc_ref[...] += _dot(a_ref[...], b_ref[...])

        @pl.when(kk == nk - 1)
        def _():
            finish(acc_ref[...])

    in_specs = [pl.BlockSpec((tm, tk), lambda i, j, kk: (i, kk)), pl.BlockSpec((tk, tn), lambda i, j, kk: (kk, j))]
    args = [a, b]
    if acc_in is not None:
        in_specs.append(pl.BlockSpec((tm, tn), lambda i, j, kk: (i, j)))
        args.append(acc_in)
    return pl.pallas_call(
        body, name=name, grid=(m // tm, n // tn, nk), in_specs=in_specs,
        out_specs=pl.BlockSpec((tm, tn), lambda i, j, kk: (i, j)),
        out_shape=jax.ShapeDtypeStruct((m, n), out_dtype),
        scratch_shapes=[pltpu.VMEM((tm, tn), f32)] if nk > 1 else [],
        compiler_params=_cparams(("parallel", "parallel", "arbitrary")))(*args)


def mm_nt(a, b, name, out_dtype=f32, relu2=None, relu2_of=None):
    m, k = a.shape
    n = b.shape[0]
    tm = _pick(m, (MM_TILE, 512, 256, 128, 64))
    tn = _pick(n, (MM_TILE, 768, 512, 256, 128))

    def body(*refs):
        r = _dot_nt(refs[0][...], refs[1][...])
        if relu2:
            pos = jnp.maximum(r, 0.0)
            refs[2][...] = pos.astype(refs[2].dtype)
            refs[3][...] = (pos * pos).astype(refs[3].dtype)
        elif relu2_of is not None:
            refs[3][...] = (r * (2.0 * refs[2][...].astype(f32))).astype(refs[3].dtype)
        else:
            refs[2][...] = r.astype(refs[2].dtype)

    blk = pl.BlockSpec((tm, tn), lambda i, j: (i, j))
    in_specs = [pl.BlockSpec((tm, k), lambda i, j: (i, 0)), pl.BlockSpec((tn, k), lambda i, j: (j, 0))]
    args = [a, b]
    if relu2_of is not None:
        in_specs.append(blk)
        args.append(relu2_of)
    if relu2:
        out_specs, out_shape = [blk, blk], [jax.ShapeDtypeStruct((m, n), bf16), jax.ShapeDtypeStruct((m, n), bf16)]
    else:
        out_specs, out_shape = blk, jax.ShapeDtypeStruct((m, n), out_dtype)
    return pl.pallas_call(body, name=name, grid=(m // tm, n // tn), in_specs=in_specs, out_specs=out_specs,
                          out_shape=out_shape, compiler_params=_cparams(("parallel", "parallel")))(*args)


def mm_nt_split(a, b, name, width, out_dtype=f32):
    m, k = a.shape
    n = b.shape[0]
    tm = MM_TILE
    parts = n // width

    def body(a_ref, b_ref, *o_refs):
        r = _dot_nt(a_ref[...], b_ref[...])
        for q in range(parts):
            o_refs[q][...] = r[:, width * q:width * (q + 1)].astype(o_refs[q].dtype)

    blk = pl.BlockSpec((tm, width), lambda i: (i, 0))
    return pl.pallas_call(
        body, name=name, grid=(m // tm,),
        in_specs=[pl.BlockSpec((tm, k), lambda i: (i, 0)), pl.BlockSpec((n, k), lambda i: (0, 0))],
        out_specs=[blk] * parts, out_shape=[jax.ShapeDtypeStruct((m, width), out_dtype)] * parts,
        compiler_params=_cparams(("parallel",)))(a, b)


def mm_tn(a, b, name, acc_in=None, acc_scale=1.0):
    k, m = a.shape
    n = b.shape[1]
    tm = _pick(m, (MM_TILE, 768, 512, 256, 128))
    tn = _pick(n, (MM_TILE, 512, 256, 128))
    tk = _pick(k, (2048, 1728, 1024, 768, 512, 256, 128, 64))
    nk = k // tk

    def body(*refs):
        a_ref, b_ref, o_ref = refs[0], refs[1], refs[-1]
        kk = pl.program_id(2)

        @pl.when(kk == 0)
        def _():
            o_ref[...] = jnp.zeros_like(o_ref) if acc_in is None else acc_scale * refs[2][...]

        o_ref[...] += _dot_tn(a_ref[...], b_ref[...])

    in_specs = [pl.BlockSpec((tk, tm), lambda i, j, kk: (kk, i)), pl.BlockSpec((tk, tn), lambda i, j, kk: (kk, j))]
    args = [a, b]
    if acc_in is not None:
        in_specs.append(pl.BlockSpec((tm, tn), lambda i, j, kk: (i, j)))
        args.append(acc_in)
    return pl.pallas_call(
        body, name=name, grid=(m // tm, n // tn, nk), in_specs=in_specs,
        out_specs=pl.BlockSpec((tm, tn), lambda i, j, kk: (i, j)),
        out_shape=jax.ShapeDtypeStruct((m, n), f32),
        compiler_params=_cparams(("parallel", "parallel", "arbitrary")))(*args)


def _lane_col(mat, lane_idx, h):
    return jnp.sum(jnp.where(lane_idx == h, mat, 0.0), axis=1, keepdims=True)


def _slopes(p):
    return [2.0 ** (-8.0 * (HEADS_PER_PATTERN * p + j + 1) / ATTN_HEADS) for j in range(HEADS_PER_PATTERN)]


def _win_specs(nq, col_of):
    return [pl.BlockSpec((64, 256), lambda r, i: (jnp.maximum(2 * i - 1, 0), col_of(r))),
            pl.BlockSpec((128, 256), lambda r, i: (i, col_of(r))),
            pl.BlockSpec((64, 256), lambda r, i: (jnp.minimum(2 * i + 2, 2 * nq - 1), col_of(r)))]


def _lane_head(shape):
    return lax.broadcasted_iota(jnp.int32, shape, 1) >> 6


def _stack_heads(m):
    lane_head = _lane_head(m.shape)
    return jnp.concatenate([jnp.where(lane_head == j, m, 0.0) for j in range(HEADS_PER_PATTERN)], axis=0)


def _unstack_heads(m4, n):
    lane_head = _lane_head((n, 256))
    out = jnp.where(lane_head == 0, m4[0:n], 0.0)
    for j in range(1, HEADS_PER_PATTERN):
        out = out + jnp.where(lane_head == j, m4[j * n:(j + 1) * n], 0.0)
    return out


def _head_cols(m, n):
    lane = lax.broadcasted_iota(jnp.int32, (n, 256), 1)
    return jnp.concatenate([jnp.sum(jnp.where(lane == ATTN_HEAD_DIM * j, m, 0.0), axis=1, keepdims=True)
                            for j in range(HEADS_PER_PATTERN)], axis=0)


def _score_bias(p, dil, by_key):
    slopes = np.asarray(_slopes(p), np.float32)
    if by_key:
        win = np.arange(256)[:, None]
        rel = np.arange(128)[None, :] - (win - 64)
    else:
        win = np.arange(256)[None, :]
        rel = win - 64 - np.arange(128)[:, None]
    band = np.abs(rel) <= 64
    out = []
    for first, last in ((False, False), (True, False), (False, True), (True, True)):
        ok = band & ~(first & (win < 64)) & ~(last & (win >= 192))
        pen = -slopes[:, None, None] * (np.abs(rel) * dil).astype(np.float32)[None]
        out.append(np.where(ok[None], pen, np.float32(NEG_BIG)).reshape(-1, rel.shape[1]))
    return jnp.asarray(np.stack(out), f32)


def _bias_spec(nq, rows, cols):
    return pl.BlockSpec((1, rows, cols), lambda r, i: ((i == 0).astype(jnp.int32) + 2 * (i == nq - 1).astype(jnp.int32), 0, 0))


def attn_fwd(q, k, v, p, dil, name):
    l = q.shape[0]
    nq = l // 128

    def body(q_ref, kp_ref, ko_ref, kn_ref, vp_ref, vo_ref, vn_ref, bias_ref, o_ref, lse_ref):
        kcat = jnp.concatenate([kp_ref[...], ko_ref[...], kn_ref[...]], axis=0)
        vcat = jnp.concatenate([vp_ref[...], vo_ref[...], vn_ref[...]], axis=0)
        s = _dot_nt(_stack_heads(q_ref[...] * 0.125), kcat) + bias_ref[0]
        m = jnp.max(s, axis=1, keepdims=True)
        pr = jnp.exp(s - m)
        den = jnp.sum(pr, axis=1, keepdims=True)
        o4 = _dot(pr, vcat) / den
        o_ref[...] = _unstack_heads(o4, 128)
        lse_ref[...] = _unstack_heads(jnp.broadcast_to(m + jnp.log(den), (512, 256)), 128)

    col = lambda r: r
    return pl.pallas_call(
        body, name=name, grid=(dil, nq),
        in_specs=[pl.BlockSpec((128, 256), lambda r, i: (i, r))] + _win_specs(nq, col) + _win_specs(nq, col)
        + [_bias_spec(nq, 512, 256)],
        out_specs=[pl.BlockSpec((128, 256), lambda r, i: (i, r))] * 2,
        out_shape=[jax.ShapeDtypeStruct(q.shape, f32)] * 2,
        compiler_params=_cparams(("parallel", "parallel")))(q, k, k, k, v, v, v, _score_bias(p, dil, False))


def attn_combine(os_, lses, tb=1024):
    t = os_[0].shape[0]

    def body(o0, o1, o2, l0, l1, l2, y_ref, lse_ref):
        a0, a1, a2 = l0[...], l1[...], l2[...]
        m = jnp.maximum(jnp.maximum(a0, a1), a2)
        e0, e1, e2 = jnp.exp(a0 - m), jnp.exp(a1 - m), jnp.exp(a2 - m)
        den = e0 + e1 + e2
        y_ref[...] = (e0 * o0[...] + e1 * o1[...] + e2 * o2[...]) / den
        lse_ref[...] = m + jnp.log(den)

    blk = pl.BlockSpec((tb, 256), lambda i: (i, 0))
    return pl.pallas_call(
        body, name="attn_combine", grid=(t // tb,), in_specs=[blk] * 6, out_specs=[blk, blk],
        out_shape=[jax.ShapeDtypeStruct((t, 256), f32)] * 2,
        compiler_params=_cparams(("parallel",)))(*os_, *lses)


def attn_delta(dy, y, tb=1024):
    t = dy.shape[0]

    def body(dy_ref, y_ref, d_ref):
        pr = dy_ref[...] * y_ref[...]
        lane_head = _lane_head(pr.shape)
        out = jnp.zeros_like(pr)
        for j in range(HEADS_PER_PATTERN):
            sj = jnp.sum(jnp.where(lane_head == j, pr, 0.0), axis=1, keepdims=True)
            out = out + jnp.where(lane_head == j, sj, 0.0)
        d_ref[...] = out

    blk = pl.BlockSpec((tb, 256), lambda i: (i, 0))
    return pl.pallas_call(body, name="attn_delta", grid=(t // tb,), in_specs=[blk, blk], out_specs=blk,
                          out_shape=jax.ShapeDtypeStruct((t, 256), f32),
                          compiler_params=_cparams(("parallel",)))(dy, y)


def attn_dq(q, k, v, dy, lse, delta, p, dil, name):
    l = q.shape[0]
    nq = l // 128

    def body(q_ref, kp_ref, ko_ref, kn_ref, vp_ref, vo_ref, vn_ref, dy_ref, lse_ref, d_ref, bias_ref, dq_ref):
        kcat = jnp.concatenate([kp_ref[...], ko_ref[...], kn_ref[...]], axis=0)
        vcat = jnp.concatenate([vp_ref[...], vo_ref[...], vn_ref[...]], axis=0)
        s = _dot_nt(_stack_heads(q_ref[...] * 0.125), kcat) + bias_ref[0]
        pr = jnp.exp(s - _head_cols(lse_ref[...], 128))
        dp = _dot_nt(_stack_heads(dy_ref[...]), vcat)
        ds = pr * (dp - _head_cols(d_ref[...], 128))
        dq_ref[...] = (_unstack_heads(_dot(ds, kcat), 128) * 0.125).astype(dq_ref.dtype)

    col = lambda r: r
    own = pl.BlockSpec((128, 256), lambda r, i: (i, r))
    return pl.pallas_call(
        body, name=name, grid=(dil, nq),
        in_specs=[own] + _win_specs(nq, col) + _win_specs(nq, col) + [own, own, own, _bias_spec(nq, 512, 256)],
        out_specs=own, out_shape=jax.ShapeDtypeStruct(q.shape, bf16),
        compiler_params=_cparams(("parallel", "parallel")))(q, k, k, k, v, v, v, dy, lse, delta, _score_bias(p, dil, False))


def attn_dkv(q, k, v, dy, lse, delta, p, dil, name):
    l = q.shape[0]
    nq = l // 128

    def body(qp_ref, qo_ref, qn_ref, gp_ref, go_ref, gn_ref, lp_ref, lo_ref, ln_ref, dp_ref, do_ref, dn_ref,
             k_ref, v_ref, bias_ref, dk_ref, dv_ref):
        cat = lambda a, b, c: jnp.concatenate([a[...], b[...], c[...]], axis=0)
        q4 = _stack_heads(cat(qp_ref, qo_ref, qn_ref) * 0.125)
        dy4 = _stack_heads(cat(gp_ref, go_ref, gn_ref))
        lse4 = _head_cols(cat(lp_ref, lo_ref, ln_ref), 256)
        del4 = _head_cols(cat(dp_ref, do_ref, dn_ref), 256)
        s = _dot_nt(q4, k_ref[...]) + bias_ref[0]
        pr = jnp.exp(s - lse4)
        dpm = _dot_nt(dy4, v_ref[...])
        ds = pr * (dpm - del4)
        dv_ref[...] = _dot_tn(pr, dy4).astype(dv_ref.dtype)
        dk_ref[...] = _dot_tn(ds, q4).astype(dk_ref.dtype)

    col = lambda r: r
    own = pl.BlockSpec((128, 256), lambda r, i: (i, r))
    win = _win_specs(nq, col)
    return pl.pallas_call(
        body, name=name, grid=(dil, nq), in_specs=win * 4 + [own, own, _bias_spec(nq, 1024, 128)], out_specs=[own, own],
        out_shape=[jax.ShapeDtypeStruct(q.shape, bf16)] * 2,
        compiler_params=_cparams(("parallel", "parallel")))(q, q, q, dy, dy, dy, lse, lse, lse, delta, delta, delta, k, v,
                                                            _score_bias(p, dil, True))


def _lanes(v, reps):
    return v if reps == 1 else jnp.tile(v, (1, reps))


def _lane_halo_specs(cb, tb, nt, off=0):
    r = tb // 128
    return [pl.BlockSpec((cb, 128), lambda j, i: (j + off, jnp.maximum(i * r - 1, 0))),
            pl.BlockSpec((cb, tb), lambda j, i: (j + off, i)),
            pl.BlockSpec((cb, 128), lambda j, i: (j + off, jnp.minimum((i + 1) * r, nt * r - 1)))]


def _with_lane_halo(prev_ref, own_ref, next_ref, i, nt):
    prev = jnp.where(i > 0, prev_ref[...].astype(f32), 0.0)
    nxt = jnp.where(i < nt - 1, next_ref[...].astype(f32), 0.0)
    return jnp.concatenate([prev, own_ref[...].astype(f32), nxt], axis=1)


def _lane_shifted(xcat, s, tb):
    n = xcat.shape[1]
    return pltpu.roll(xcat, (-s) % n, 1)[:, 128:128 + tb]


def conv_fwd_t(xbc_t, w_b, b_b, tb=1024, cb=256):
    c, t = xbc_t.shape
    nt = t // tb

    def body(prev_ref, own_ref, next_ref, w_ref, b_ref, o_ref, ds_ref):
        i = pl.program_id(1)
        xcat = _with_lane_halo(prev_ref, own_ref, next_ref, i, nt)
        reps = tb // 128
        pre = _lanes(b_ref[...], reps)
        for k in range(D_CONV):
            pre = pre + _lanes(w_ref[k], reps) * _lane_shifted(xcat, k - 2, tb)
        sg = _sigmoid(pre)
        o_ref[...] = pre * sg
        ds_ref[...] = sg * (1.0 + pre * (1.0 - sg))

    blk = pl.BlockSpec((cb, tb), lambda j, i: (j, i))
    return pl.pallas_call(
        body, name="conv_fwd", grid=(c // cb, nt),
        in_specs=_lane_halo_specs(cb, tb, nt) + [pl.BlockSpec((D_CONV, cb, 128), lambda j, i: (0, j, 0)),
                                                 pl.BlockSpec((cb, 128), lambda j, i: (j, 0))],
        out_specs=[blk, blk], out_shape=[jax.ShapeDtypeStruct((c, t), f32)] * 2,
        compiler_params=_cparams(("parallel", "parallel")))(xbc_t, xbc_t, xbc_t, w_b, b_b)


def conv_bwd_t(xbc_t, dsilu_t, grad_t, w_b, into, name, row0, tb=1024, cb=256):
    c, t = grad_t.shape
    nt = t // tb
    off = row0 // cb
    off_out = (D_INNER + row0) // cb
    reps = tb // 128

    def body(*refs):
        i = pl.program_id(1)
        x_ref, sr, gr = refs[0], refs[1:4], refs[4:7]
        w_ref = refs[7]
        dx_ref, dw_ref, db_ref = refs[-3:]
        wk = [_lanes(w_ref[k], reps) for k in range(D_CONV)]
        dpre = _with_lane_halo(*gr, i, nt) * _with_lane_halo(*sr, i, nt)

        def fold(v):
            s = v[:, 0:128]
            for q in range(1, reps):
                s = s + v[:, 128 * q:128 * (q + 1)]
            return s

        @pl.when(i == 0)
        def _():
            dw_ref[...] = jnp.zeros_like(dw_ref)
            db_ref[...] = jnp.zeros_like(db_ref)

        x_own = x_ref[...]
        dx = None
        for k in range(D_CONV):
            shifted = _lane_shifted(dpre, 2 - k, tb)
            term = wk[k] * shifted
            dx = term if dx is None else dx + term
            dw_ref[k] += fold(shifted * x_own)
        dx_ref[...] = dx.astype(dx_ref.dtype)
        db_ref[...] += fold(dpre[:, 128:128 + tb])

    in_specs = ([pl.BlockSpec((cb, tb), lambda j, i: (j + off, i))] + _lane_halo_specs(cb, tb, nt, off)
                + _lane_halo_specs(cb, tb, nt)
                + [pl.BlockSpec((D_CONV, cb, 128), lambda j, i: (0, j + off, 0)), pl.BlockSpec(memory_space=pl.ANY)])
    args = [xbc_t] + [dsilu_t] * 3 + [grad_t] * 3 + [w_b, into]
    return pl.pallas_call(
        body, name=name, grid=(c // cb, nt), in_specs=in_specs,
        out_specs=[pl.BlockSpec((cb, tb), lambda j, i: (j + off_out, i)),
                   pl.BlockSpec((D_CONV, cb, 128), lambda j, i: (0, j, 0)), pl.BlockSpec((cb, 128), lambda j, i: (j, 0))],
        out_shape=[jax.ShapeDtypeStruct(into.shape, into.dtype), jax.ShapeDtypeStruct((D_CONV, c, 128), f32),
                   jax.ShapeDtypeStruct((c, 128), f32)],
        input_output_aliases={8: 0}, compiler_params=_cparams(("parallel", "arbitrary")))(*args)


def dt_fwd_t(u_dt_t, bias_b, tb=2048):
    r, t = u_dt_t.shape

    def body(u_ref, b_ref, o_ref):
        v = u_ref[...] + _lanes(b_ref[...], tb // 128)
        o_ref[...] = jnp.maximum(v, 0.0) + jnp.log(1.0 + jnp.exp(-jnp.abs(v)))

    return pl.pallas_call(
        body, name="dt_fwd", grid=(t // tb,),
        in_specs=[pl.BlockSpec((r, tb), lambda i: (0, i)), pl.BlockSpec((r, 128), lambda i: (0, 0))],
        out_specs=pl.BlockSpec((r, tb), lambda i: (0, i)), out_shape=jax.ShapeDtypeStruct((r, t), f32),
        compiler_params=_cparams(("parallel",)))(u_dt_t, bias_b)


def dt_bwd_t(ddt_f, ddt_b, u_dt_t, bias_b, into, tb=2048):
    r, t = u_dt_t.shape
    reps = tb // 128
    row_blk = (SSD_COLS - r) // r

    def body(gf_ref, gb_ref, u_ref, b_ref, into_ref, du_ref, db_ref):
        g = jnp.concatenate([gf_ref[...], gb_ref[...]], axis=0)
        du = g * _sigmoid(u_ref[...] + _lanes(b_ref[...], reps))
        du_ref[...] = du.astype(du_ref.dtype)

        @pl.when(pl.program_id(0) == 0)
        def _():
            db_ref[...] = jnp.zeros_like(db_ref)

        s = du[:, 0:128]
        for q in range(1, reps):
            s = s + du[:, 128 * q:128 * (q + 1)]
        db_ref[...] += s

    half = pl.BlockSpec((r // 2, tb), lambda i: (0, i))
    return pl.pallas_call(
        body, name="dt_bwd", grid=(t // tb,),
        in_specs=[half, half, pl.BlockSpec((r, tb), lambda i: (0, i)), pl.BlockSpec((r, 128), lambda i: (0, 0)),
                  pl.BlockSpec(memory_space=pl.ANY)],
        out_specs=[pl.BlockSpec((r, tb), lambda i: (row_blk, i)), pl.BlockSpec((r, 128), lambda i: (0, 0))],
        out_shape=[jax.ShapeDtypeStruct(into.shape, into.dtype), jax.ShapeDtypeStruct((r, 128), f32)],
        input_output_aliases={4: 0}, compiler_params=_cparams(("arbitrary",)))(ddt_f, ddt_b, u_dt_t, bias_b, into)


HEADS_PER_GROUP = SSD_HEADS // SSD_GROUPS


def _group_rows(g, n):
    return pl.ds(pl.multiple_of(g * n, n), n)


def _ssd_decays(dt_blk, a_blk, reverse):
    row = lax.broadcasted_iota(jnp.int32, (CHUNK, CHUNK), 0)
    col = lax.broadcasted_iota(jnp.int32, (CHUNK, CHUNK), 1)
    mask = (row <= col) if reverse else (row >= col)
    tri = mask.astype(f32)
    a8 = dt_blk * a_blk
    a = jnp.concatenate([a8, jnp.zeros((CHUNK - HEADS_PER_GROUP, CHUNK), f32)], axis=0).T
    acs = _dot_exact(tri, a)
    return mask, tri, a8, acs, acs.T, col


def ssd_fwd_t(xbc_ct, dt_t, a_b, reverse, name, prev=None, tail=None):
    t = xbc_ct.shape[1]
    nc = t // CHUNK
    direction = 1 if reverse else 0

    def cidx(c):
        return nc - 1 - c if reverse else c

    def body(*refs):
        x_ref, b_ref, c_ref, dt_ref, a_ref = refs[0:5]
        pos = 5
        prev_ref = None
        if prev is not None:
            prev_ref = refs[pos]
            pos += 1
        if tail is not None:
            z_ref, skip_ref, nw_ref = refs[pos:pos + 3]
            pos += 3
            y_ref, hp_ref, yn_ref, h_scr = refs[pos:pos + 4]
        else:
            y_ref, hp_ref, h_scr = refs[pos:pos + 3]

        @pl.when(pl.program_id(0) == 0)
        def _():
            h_scr[...] = jnp.zeros_like(h_scr)

        def group(g, carry):
            x_v, y_v = x_ref.at[_group_rows(g, 512)], y_ref.at[_group_rows(g, 512)]
            heads = _group_rows(g, HEADS_PER_GROUP)
            hp_v, h_v = hp_ref.at[0, heads], h_scr.at[heads]
            dt_blk = dt_ref[heads, :]
            mask, tri, a8, acs, acs_t, lane = _ssd_decays(dt_blk, a_ref[heads, :], reverse)
            bm = b_ref[_group_rows(g, 128), :].T
            cm = c_ref[_group_rows(g, 128), :].T
            cb = _dot_nt(cm, bm)
            tot = jnp.sum(a8, axis=1, keepdims=True)
            for j in range(HEADS_PER_GROUP):
                rows = slice(SSD_HEAD_DIM * j, SSD_HEAD_DIM * (j + 1))
                col_j = _lane_col(acs, lane, j)
                row_j = acs_t[j:j + 1, :]
                lmat = jnp.where(mask, jnp.exp(jnp.where(mask, col_j - row_j, 0.0)), 0.0)
                xdt = x_v[rows, :] * dt_blk[j:j + 1, :]
                hp = h_v[j]
                hp_v[j] = hp
                y = _dot_nt(xdt, cb * lmat) + _dot_nt(hp, cm) * jnp.exp(row_j)
                if prev_ref is not None:
                    y = y + prev_ref.at[_group_rows(g, 512)][rows, :]
                y_v[rows, :] = y
                tot_j = tot[j:j + 1, :]
                h_v[j] = jnp.exp(tot_j) * hp + _dot(xdt * jnp.exp(tot_j - row_j), bm)
            if tail is not None:
                rows = _group_rows(g, 512)
                zz = z_ref[rows, :]
                yg = (y_v[...] + skip_ref[rows, :] * x_v[...]) * (zz * _sigmoid(zz))
                rstd = lax.rsqrt(jnp.mean(yg * yg, axis=0, keepdims=True) + NORM_EPS)
                yn_ref[rows, :] = (yg * rstd * nw_ref[rows, :]).astype(yn_ref.dtype)
            return carry

        lax.fori_loop(0, SSD_GROUPS, group, 0, unroll=True)

    big = pl.BlockSpec((D_INNER, CHUNK), lambda c: (0, cidx(c)))
    par = pl.BlockSpec((D_INNER, 128), lambda c: (0, 0))
    in_specs = [big, pl.BlockSpec((512, CHUNK), lambda c: (4, cidx(c))), pl.BlockSpec((512, CHUNK), lambda c: (5, cidx(c))),
                pl.BlockSpec((SSD_HEADS, CHUNK), lambda c: (direction, cidx(c))),
                pl.BlockSpec((SSD_HEADS, 128), lambda c: (direction, 0))]
    args = [xbc_ct, xbc_ct, xbc_ct, dt_t, a_b]
    out_specs = [big, pl.BlockSpec((1, SSD_HEADS, SSD_HEAD_DIM, D_STATE), lambda c: (cidx(c), 0, 0, 0))]
    out_shape = [jax.ShapeDtypeStruct((D_INNER, t), f32), jax.ShapeDtypeStruct((nc, SSD_HEADS, SSD_HEAD_DIM, D_STATE), f32)]
    if prev is not None:
        in_specs.append(big)
        args.append(prev)
    if tail is not None:
        in_specs += [big, par, par]
        args += list(tail)
        out_specs.append(big)
        out_shape.append(jax.ShapeDtypeStruct((D_INNER, t), bf16))
    return pl.pallas_call(
        body, name=name, grid=(nc,), in_specs=in_specs, out_specs=out_specs, out_shape=out_shape,
        scratch_shapes=[pltpu.VMEM((SSD_HEADS, SSD_HEAD_DIM, D_STATE), f32)],
        compiler_params=_cparams(("arbitrary",)))(*args)


def ssd_bwd_t(xbc_ct, dt_t, a_b, dy_t, hprev, reverse, name, skip_b=None, prev=None, tail=None):
    t = xbc_ct.shape[1]
    nc = t // CHUNK
    direction = 1 if reverse else 0

    def cidx(c):
        return c if reverse else nc - 1 - c

    def body(*refs):
        x_ref, b_ref, c_ref, dt_ref, a_ref, dy_ref, hp_ref = refs[0:7]
        pos = 7
        skip_ref = None
        if skip_b is not None:
            skip_ref = refs[pos]
            pos += 1
        prev_refs = None
        if prev is not None:
            prev_refs = refs[pos:pos + 3]
            pos += 3
        if tail is not None:
            ys_ref, z_ref, nw_ref = refs[pos:pos + 3]
            pos += 3
        dx_ref, db_ref, dc_ref, ddt_ref, da_ref = refs[pos:pos + 5]
        pos += 5
        if tail is not None:
            dyout_ref, dz_ref, dnw_ref, ddx_ref = refs[pos:pos + 4]
            pos += 4
        dh_scr = refs[pos]

        @pl.when(pl.program_id(0) == 0)
        def _():
            dh_scr[...] = jnp.zeros_like(dh_scr)
            da_ref[...] = jnp.zeros_like(da_ref)
            if tail is not None:
                dnw_ref[...] = jnp.zeros_like(dnw_ref)
                ddx_ref[...] = jnp.zeros_like(ddx_ref)

        def group(g, carry):
            big, st, heads = _group_rows(g, 512), _group_rows(g, 128), _group_rows(g, HEADS_PER_GROUP)
            x_v, dy_v, dx_v = x_ref.at[big], dy_ref.at[big], dx_ref.at[big]
            hp_v, dh_v = hp_ref.at[0, heads], dh_scr.at[heads]
            dy_grp = None
            if tail is not None:
                zz = z_ref[big, :]
                sg = _sigmoid(zz)
                sl = zz * sg
                x_all = x_v[...]
                y = ys_ref[big, :] + skip_ref[big, :] * x_all
                yz = y * sl
                rstd = lax.rsqrt(jnp.mean(yz * yz, axis=0, keepdims=True) + NORM_EPS)
                yhat = yz * rstd
                gy = dy_v[...]
                dyhat = gy * nw_ref[big, :]
                dyz = rstd * (dyhat - yhat * jnp.mean(dyhat * yhat, axis=0, keepdims=True))
                dy_grp = dyz * sl
                dyout_ref[big, :] = dy_grp
                dz_ref[big, :] = (dyz * y * sg * (1.0 + zz * (1.0 - sg))).astype(dz_ref.dtype)
                dnw_ref[big, :] += gy * yhat
                ddx_ref[big, :] += dy_grp * x_all
            dt_blk = dt_ref[heads, :]
            a_blk = a_ref[heads, :]
            mask, tri, a8, acs, acs_t, lane = _ssd_decays(dt_blk, a_blk, reverse)
            sub = lax.broadcasted_iota(jnp.int32, (CHUNK, CHUNK), 0)
            mask_t = (sub >= lane) if reverse else (sub <= lane)
            bm = b_ref[st, :].T
            cm = c_ref[st, :].T
            cb = _dot_nt(cm, bm)
            cb_t = _dot_nt(bm, cm)
            tot = jnp.sum(a8, axis=1, keepdims=True)
            dcb = jnp.zeros((CHUNK, CHUNK), f32)
            dbm = jnp.zeros((CHUNK, D_STATE), f32)
            dcm = jnp.zeros((CHUNK, D_STATE), f32)
            dacs_rows, ddtx_rows = [], []
            for j in range(HEADS_PER_GROUP):
                rows = slice(SSD_HEAD_DIM * j, SSD_HEAD_DIM * (j + 1))
                col_j = _lane_col(acs, lane, j)
                row_j = acs_t[j:j + 1, :]
                dt_j = dt_blk[j:j + 1, :]
                tot_j = tot[j:j + 1, :]
                lmat = jnp.where(mask, jnp.exp(jnp.where(mask, col_j - row_j, 0.0)), 0.0)
                lmat_t = jnp.where(mask_t, jnp.exp(jnp.where(mask_t, row_j - col_j, 0.0)), 0.0)
                x = x_v[rows, :]
                xdt = x * dt_j
                dyh = dy_v[rows, :] if dy_grp is None else dy_grp[rows]
                hp = hp_v[j]
                dhn = dh_v[j]
                ml = _dot_tn(dyh, xdt) * lmat
                w_t = _dot_tn(xdt, dyh) * lmat_t * cb_t
                dcb = dcb + ml
                dacs = jnp.sum(w_t, axis=0, keepdims=True) - jnp.sum(ml * cb, axis=0, keepdims=True)
                ecol = jnp.exp(row_j)
                dec = jnp.exp(tot_j - row_j)
                dye = dyh * ecol
                yoff = _dot_nt(hp, cm) * ecol
                gmat = _dot_nt(dhn, bm)
                dxdt = _dot(dyh, cb * lmat) + dec * gmat
                s_dec = jnp.sum(xdt * gmat, axis=0, keepdims=True) * dec
                dacs = dacs + jnp.sum(dyh * yoff, axis=0, keepdims=True) - s_dec
                dcd = jnp.sum(jnp.sum(dhn * hp, axis=1, keepdims=True), axis=0, keepdims=True)
                dtot = jnp.sum(s_dec, axis=1, keepdims=True) + jnp.exp(tot_j) * dcd
                dacs_rows.append((dacs, dtot))
                ddtx_rows.append(jnp.sum(dxdt * x, axis=0, keepdims=True))
                dcm = dcm + _dot_tn(dye, hp)
                dbm = dbm + _dot_tn(xdt * dec, dhn)
                dxh = dxdt * dt_j
                if skip_ref is not None:
                    dxh = dxh + skip_ref.at[big][rows, :] * dyh
                if prev_refs is not None:
                    dxh = dxh + prev_refs[0].at[big][rows, :]
                dx_v[rows, :] = dxh
                dh_v[j] = jnp.exp(tot_j) * dhn + _dot(dye, cm)
            dcm = dcm + _dot(dcb, bm)
            dbm = dbm + _dot_tn(dcb, cm)
            dbt, dct = dbm.T, dcm.T
            if prev_refs is not None:
                dbt = dbt + prev_refs[1][st, :]
                dct = dct + prev_refs[2][st, :]
            db_ref[st, :] = dbt
            dc_ref[st, :] = dct
            dacs8 = jnp.concatenate([d for d, _ in dacs_rows], axis=0)
            dtot8 = jnp.concatenate([d for _, d in dacs_rows], axis=0)
            da8 = _dot_exact(dacs8, tri) + dtot8
            ddt_ref[heads, :] = da8 * a_blk + jnp.concatenate(ddtx_rows, axis=0)
            da_ref[heads, :] += da8 * dt_blk
            return carry

        lax.fori_loop(0, SSD_GROUPS, group, 0, unroll=True)

    big = pl.BlockSpec((D_INNER, CHUNK), lambda c: (0, cidx(c)))
    st = pl.BlockSpec((512, CHUNK), lambda c: (0, cidx(c)))
    in_specs = [big, pl.BlockSpec((512, CHUNK), lambda c: (4, cidx(c))), pl.BlockSpec((512, CHUNK), lambda c: (5, cidx(c))),
                pl.BlockSpec((SSD_HEADS, CHUNK), lambda c: (direction, cidx(c))),
                pl.BlockSpec((SSD_HEADS, 128), lambda c: (direction, 0)), big,
                pl.BlockSpec((1, SSD_HEADS, SSD_HEAD_DIM, D_STATE), lambda c: (cidx(c), 0, 0, 0))]
    args = [xbc_ct, xbc_ct, xbc_ct, dt_t, a_b, dy_t, hprev]
    if skip_b is not None:
        in_specs.append(pl.BlockSpec((D_INNER, 128), lambda c: (0, 0)))
        args.append(skip_b)
    if prev is not None:
        in_specs += [big, st, st]
        args += list(prev)
    par = pl.BlockSpec((D_INNER, 128), lambda c: (0, 0))
    out_specs = [big, st, st, pl.BlockSpec((SSD_HEADS, CHUNK), lambda c: (0, cidx(c))),
                 pl.BlockSpec((SSD_HEADS, 128), lambda c: (0, 0))]
    out_shape = [jax.ShapeDtypeStruct((D_INNER, t), f32), jax.ShapeDtypeStruct((512, t), f32),
                 jax.ShapeDtypeStruct((512, t), f32), jax.ShapeDtypeStruct((SSD_HEADS, t), f32),
                 jax.ShapeDtypeStruct((SSD_HEADS, 128), f32)]
    if tail is not None:
        in_specs += [big, big, par]
        args += list(tail)
        out_specs += [big, big, par, par]
        out_shape += [jax.ShapeDtypeStruct((D_INNER, t), f32), jax.ShapeDtypeStruct((SSD_COLS, t), bf16),
                      jax.ShapeDtypeStruct((D_INNER, 128), f32), jax.ShapeDtypeStruct((D_INNER, 128), f32)]
    return pl.pallas_call(
        body, name=name, grid=(nc,), in_specs=in_specs, out_specs=out_specs, out_shape=out_shape,
        scratch_shapes=[pltpu.VMEM((SSD_HEADS, SSD_HEAD_DIM, D_STATE), f32)],
        compiler_params=_cparams(("arbitrary",)))(*args)


def merge_fwd(u_gate, bg_row, y_ssd, y_att, tb=512):
    t = y_ssd.shape[0]

    def body(ga_ref, gb_ref, ba_ref, bb_ref, ys_ref, ya_ref, o_ref):
        o_ref[...] = (_sigmoid(ga_ref[...] + ba_ref[...]) * ys_ref[...]
                      + _sigmoid(gb_ref[...] + bb_ref[...]) * ya_ref[...]).astype(o_ref.dtype)

    blk = pl.BlockSpec((tb, 512), lambda i, j: (i, j))
    blk2 = pl.BlockSpec((tb, 512), lambda i, j: (i, 2 + j))
    row = pl.BlockSpec((1, 512), lambda i, j: (0, j))
    row2 = pl.BlockSpec((1, 512), lambda i, j: (0, 2 + j))
    return pl.pallas_call(
        body, name="merge_fwd", grid=(t // tb, 2), in_specs=[blk, blk2, row, row2, blk, blk], out_specs=blk,
        out_shape=jax.ShapeDtypeStruct((t, D_MODEL), bf16),
        compiler_params=_cparams(("parallel", "parallel")))(u_gate, u_gate, bg_row, bg_row, y_ssd, y_att)


def merge_bwd(dm, u_gate, bg_row, y_ssd, y_att, tb=512):
    t = dm.shape[0]

    def body(dm_ref, ga_ref, gb_ref, ba_ref, bb_ref, ys_ref, ya_ref, dys_ref, dya_ref, dga_ref, dgb_ref, dba_ref, dbb_ref):
        d = dm_ref[...]
        sa = _sigmoid(ga_ref[...] + ba_ref[...])
        sb = _sigmoid(gb_ref[...] + bb_ref[...])
        dys_ref[...] = (d * sa).astype(dys_ref.dtype)
        dya_ref[...] = (d * sb).astype(dya_ref.dtype)
        dla = d * ys_ref[...] * sa * (1.0 - sa)
        dlb = d * ya_ref[...] * sb * (1.0 - sb)
        dga_ref[...] = dla.astype(dga_ref.dtype)
        dgb_ref[...] = dlb.astype(dgb_ref.dtype)

        @pl.when(pl.program_id(1) == 0)
        def _():
            dba_ref[...] = jnp.zeros_like(dba_ref)
            dbb_ref[...] = jnp.zeros_like(dbb_ref)

        dba_ref[...] += jnp.sum(dla, axis=0, keepdims=True)
        dbb_ref[...] += jnp.sum(dlb, axis=0, keepdims=True)

    blk = pl.BlockSpec((tb, 512), lambda j, i: (i, j))
    blk2 = pl.BlockSpec((tb, 512), lambda j, i: (i, 2 + j))
    row = pl.BlockSpec((1, 512), lambda j, i: (0, j))
    row2 = pl.BlockSpec((1, 512), lambda j, i: (0, 2 + j))
    act = jax.ShapeDtypeStruct((t, D_MODEL), bf16)
    vec = jax.ShapeDtypeStruct((1, D_MODEL), f32)
    return pl.pallas_call(
        body, name="merge_bwd", grid=(2, t // tb), in_specs=[blk, blk, blk2, row, row2, blk, blk],
        out_specs=[blk, blk, blk, blk, row, row], out_shape=[act, act, act, act, vec, vec],
        compiler_params=_cparams(("parallel", "arbitrary")))(dm, u_gate, u_gate, bg_row, bg_row, y_ssd, y_att)


def _ln_stats(r):
    mu = jnp.mean(r, axis=1, keepdims=True)
    xc = r - mu
    rstd = lax.rsqrt(jnp.mean(xc * xc, axis=1, keepdims=True) + NORM_EPS)
    return xc * rstd, rstd


def _ln_bwd(dy, xhat, rstd, g_row):
    dxh = dy * g_row
    return rstd * (dxh - jnp.mean(dxh, axis=1, keepdims=True) - xhat * jnp.mean(dxh * xhat, axis=1, keepdims=True))


def ln1_fwd(x, mix, g_row, b_row, tb=512):
    t = x.shape[0]

    def body(x_ref, m_ref, g_ref, b_ref, o_ref, ob_ref):
        xhat, _ = _ln_stats(ALPHA * x_ref[...] + m_ref[...])
        h = xhat * g_ref[...] + b_ref[...]
        o_ref[...] = h
        ob_ref[...] = h.astype(ob_ref.dtype)

    blk = pl.BlockSpec((tb, D_MODEL), lambda i: (i, 0))
    row = pl.BlockSpec((1, D_MODEL), lambda i: (0, 0))
    return pl.pallas_call(body, name="ln1_fwd", grid=(t // tb,), in_specs=[blk, blk, row, row], out_specs=[blk, blk],
                          out_shape=[jax.ShapeDtypeStruct((t, D_MODEL), f32), jax.ShapeDtypeStruct((t, D_MODEL), bf16)],
                          compiler_params=_cparams(("parallel",)))(x, mix, g_row, b_row)


def ln1_bwd(dh, x, mix, g_row, tb=512):
    t = x.shape[0]

    def body(dh_ref, x_ref, m_ref, g_ref, dr_ref, drb_ref, dg_ref, db_ref):
        xhat, rstd = _ln_stats(ALPHA * x_ref[...] + m_ref[...])
        dy = dh_ref[...]
        dr = _ln_bwd(dy, xhat, rstd, g_ref[...])
        dr_ref[...] = dr
        drb_ref[...] = dr.astype(drb_ref.dtype)

        @pl.when(pl.program_id(0) == 0)
        def _():
            dg_ref[...] = jnp.zeros_like(dg_ref)
            db_ref[...] = jnp.zeros_like(db_ref)

        dg_ref[...] += jnp.sum(dy * xhat, axis=0, keepdims=True)
        db_ref[...] += jnp.sum(dy, axis=0, keepdims=True)

    blk = pl.BlockSpec((tb, D_MODEL), lambda i: (i, 0))
    row = pl.BlockSpec((1, D_MODEL), lambda i: (0, 0))
    return pl.pallas_call(
        body, name="ln1_bwd", grid=(t // tb,), in_specs=[blk, blk, blk, row], out_specs=[blk, blk, row, row],
        out_shape=[jax.ShapeDtypeStruct((t, D_MODEL), f32), jax.ShapeDtypeStruct((t, D_MODEL), bf16),
                   jax.ShapeDtypeStruct((1, D_MODEL), f32), jax.ShapeDtypeStruct((1, D_MODEL), f32)],
        compiler_params=_cparams(("arbitrary",)))(dh, x, mix, g_row)


def ln2_loss(h1, f, g_row, b_row, target, tb=512):
    t = h1.shape[0]

    def body(h_ref, f_ref, g_ref, b_ref, t_ref, dr_ref, drb_ref, dg_ref, db_ref, loss_ref):
        xhat, rstd = _ln_stats(ALPHA * h_ref[...] + f_ref[...])
        g = g_ref[...]
        err = xhat * g + b_ref[...] - t_ref[...]
        dy = err * (1.0 / D_MODEL)
        dr = _ln_bwd(dy, xhat, rstd, g)
        dr_ref[...] = dr
        drb_ref[...] = dr.astype(drb_ref.dtype)

        @pl.when(pl.program_id(0) == 0)
        def _():
            dg_ref[...] = jnp.zeros_like(dg_ref)
            db_ref[...] = jnp.zeros_like(db_ref)
            loss_ref[...] = jnp.zeros_like(loss_ref)

        dg_ref[...] += jnp.sum(dy * xhat, axis=0, keepdims=True)
        db_ref[...] += jnp.sum(dy, axis=0, keepdims=True)
        part = jnp.sum(jnp.mean(err * err, axis=1, keepdims=True), axis=0, keepdims=True)
        loss_ref[...] += 0.5 * part

    blk = pl.BlockSpec((tb, D_MODEL), lambda i: (i, 0))
    row = pl.BlockSpec((1, D_MODEL), lambda i: (0, 0))
    return pl.pallas_call(
        body, name="ln2_loss", grid=(t // tb,), in_specs=[blk, blk, row, row, blk],
        out_specs=[blk, blk, row, row, pl.BlockSpec((8, 128), lambda i: (0, 0))],
        out_shape=[jax.ShapeDtypeStruct((t, D_MODEL), f32), jax.ShapeDtypeStruct((t, D_MODEL), bf16),
                   jax.ShapeDtypeStruct((1, D_MODEL), f32), jax.ShapeDtypeStruct((1, D_MODEL), f32),
                   jax.ShapeDtypeStruct((8, 128), f32)],
        compiler_params=_cparams(("arbitrary",)))(h1, f, g_row, b_row, target)


def _adamw_update(g, w_ref, m_ref, v_ref, g_ref, d_ref, nm_ref, nv_ref):
    c1 = 1.0 - ADAM_B1 ** ADAM_STEP
    c2 = 1.0 - ADAM_B2 ** ADAM_STEP
    nm = ADAM_B1 * m_ref[...] + (1.0 - ADAM_B1) * g
    nv = ADAM_B2 * v_ref[...] + (1.0 - ADAM_B2) * (g * g)
    g_ref[...] = g
    nm_ref[...] = nm
    nv_ref[...] = nv
    d_ref[...] = -ADAM_LR * ((nm / c1) / (jnp.sqrt(nv / c2) + ADAM_EPS) + ADAM_WD * w_ref[...])


def adamw_sum8(landed, parts, me, w, m, v, row0, name, tails=None):
    rows = landed.shape[1]
    off = row0 // EARLY_TILE
    tail_blk, tail_at = divmod(OFF_TAIL - row0, EARLY_TILE)

    def body(me_ref, *refs):
        src = refs[0:N_DEV]
        own_ref = refs[N_DEV]
        pos = N_DEV + 1
        mine = me_ref[0]

        def sum8(own, slots):
            g = None
            for s in range(N_DEV):
                term = jnp.where(mine == s, own, slots(s)).astype(f32)
                g = term if g is None else g + term
            return g

        g = sum8(own_ref[0], lambda s: src[s][0])
        if tails is not None:
            tl_ref, tm_ref = refs[pos:pos + 2]
            pos += 2
            own_tail = tm_ref[0]
            for s in range(1, N_DEV):
                own_tail = jnp.where(mine == s, tm_ref[s], own_tail)
            gt = sum8(own_tail, lambda s: tl_ref[s])
            with_tail = jnp.concatenate([g[0:tail_at], gt, g[tail_at + ROWS_TAIL:]], axis=0)
            g = jnp.where(pl.program_id(0) == tail_blk, with_tail, g)
        w_ref, m_ref, v_ref = refs[pos:pos + 3]
        _adamw_update(g, w_ref, m_ref, v_ref, *refs[pos + 3:])

    def slot(s):
        return pl.BlockSpec((1, EARLY_TILE, 1024), lambda i, me_ref: (jnp.where(me_ref[0] == s, (s + 1) % N_DEV, s), i, 0))

    shard = pl.BlockSpec((EARLY_TILE, 1024), lambda i, me_ref: (i + off, 0))
    out_blk = pl.BlockSpec((EARLY_TILE, 1024), lambda i, me_ref: (i, 0))
    in_specs = [slot(s) for s in range(N_DEV)] + [pl.BlockSpec((1, EARLY_TILE, 1024), lambda i, me_ref: (me_ref[0], i, 0))]
    args = [landed] * N_DEV + [parts]
    if tails is not None:
        whole = pl.BlockSpec((N_DEV, ROWS_TAIL, 1024), lambda i, me_ref: (0, 0, 0))
        in_specs += [whole, whole]
        args += list(tails)
    grid_spec = pltpu.PrefetchScalarGridSpec(num_scalar_prefetch=1, grid=(rows // EARLY_TILE,),
                                             in_specs=in_specs + [shard, shard, shard], out_specs=[out_blk] * 4)
    out = jax.ShapeDtypeStruct((rows, 1024), f32)
    return pl.pallas_call(body, name=name, grid_spec=grid_spec, out_shape=[out] * 4,
                          compiler_params=_cparams(("parallel",)))(me, *args, w, m, v)


def _place():
    return lax.axis_index("x"), lax.axis_index("y"), lax.axis_index("c")


def all_gather_blocks(shard):
    rows, cols = shard.shape

    def body(x_ref, out_ref, send_sems, recv_sems, local_sem):
        x, y, c = _place()
        me, sibling = (x, y, c), (x, y, 1 - c)
        chips = [(1 - x, y), (x, 1 - y), (1 - x, 1 - y)]

        def slot(px, py, pc):
            return out_ref.at[4 * px + 2 * py + pc]

        def copy(k, block, to, src=None):
            return pltpu.make_async_remote_copy(
                src_ref=slot(*block) if src is None else src, dst_ref=slot(*block), send_sem=send_sems.at[k],
                recv_sem=recv_sems.at[k], device_id=to, device_id_type=MESH)

        mine = pltpu.make_async_copy(x_ref, slot(*me), local_sem)
        mine.start()
        first = [copy(0, me, sibling, src=x_ref)]
        first += [copy(1 + j, me, (*chip, c), src=x_ref) for j, chip in enumerate(chips)]
        for cp in first:
            cp.start()
        passed = [copy(4 + j, (*chip, c), sibling) for j, chip in enumerate(chips)]
        for j, chip in enumerate(chips):
            copy(1 + j, (*chip, c), me).wait_recv()
            passed[j].start()
        copy(0, sibling, me).wait_recv()
        for j, chip in enumerate(chips):
            copy(4 + j, (*chip, 1 - c), me).wait_recv()
        for cp in first + passed:
            cp.wait_send()
        mine.wait()

    return pl.pallas_call(
        body, name="all_gather_blocks", out_shape=jax.ShapeDtypeStruct((N_DEV, rows, cols), shard.dtype),
        in_specs=[pl.BlockSpec(memory_space=pl.ANY)], out_specs=pl.BlockSpec(memory_space=pl.ANY),
        scratch_shapes=[pltpu.SemaphoreType.DMA((7,)), pltpu.SemaphoreType.DMA((7,)), pltpu.SemaphoreType.DMA],
        compiler_params=pltpu.CompilerParams(has_side_effects=True))(shard)


_HBM = pl.BlockSpec(memory_space=pltpu.HBM)
_SEM = pl.BlockSpec(memory_space=pltpu.SEMAPHORE)


def _peer(k):
    x, y, c = _place()
    px, py, pc = (1 - x if k & 4 else x), (1 - y if k & 2 else y), (1 - c if k & 1 else c)
    return (px, py, pc), 4 * px + 2 * py + pc


def scatter_start(parts, name):
    per_device = parts.ndim == 3

    def body(p_ref, land_ref, send_sems, recv_sems, p_thru, land_thru, token):
        x, y, c = _place()
        me = 4 * x + 2 * y + c
        for k in range(1, N_DEV):
            place, idx = _peer(k)
            pltpu.make_async_remote_copy(src_ref=p_ref.at[idx] if per_device else p_ref, dst_ref=land_ref.at[me],
                                         send_sem=send_sems.at[k - 1], recv_sem=recv_sems.at[k - 1], device_id=place,
                                         device_id_type=MESH).start()
        token[...] = jnp.zeros_like(token)

    land_shape = parts.shape if per_device else (N_DEV,) + parts.shape
    landing = lax.empty(land_shape, parts.dtype)
    return pl.pallas_call(
        body, name=name,
        out_shape=(pltpu.SemaphoreType.DMA((N_DEV - 1,)), pltpu.SemaphoreType.DMA((N_DEV - 1,)),
                   pltpu.HBM(parts.shape, parts.dtype), pltpu.HBM(land_shape, parts.dtype),
                   jax.ShapeDtypeStruct((8, 128), f32)),
        in_specs=(_HBM, _HBM), out_specs=(_SEM, _SEM, _HBM, _HBM, pl.BlockSpec(memory_space=pltpu.VMEM)),
        input_output_aliases={0: 2, 1: 3},
        compiler_params=pltpu.CompilerParams(has_side_effects=pltpu.SideEffectType.DATAFLOW_SIDE_EFFECTING),
    )(pltpu.with_memory_space_constraint(parts, pltpu.HBM), pltpu.with_memory_space_constraint(landing, pltpu.HBM))


def scatter_wait(send_sems, recv_sems, parts_thru, land_thru, after, name):
    per_device = parts_thru.ndim == 3

    def body(p_ref, land_ref, send_sems, recv_sems, after_ref, p_out, land_out):
        for k in range(1, N_DEV):
            place, idx = _peer(k)
            copy = pltpu.make_async_remote_copy(src_ref=p_ref.at[idx] if per_device else p_ref, dst_ref=land_ref.at[idx],
                                                send_sem=send_sems.at[k - 1], recv_sem=recv_sems.at[k - 1],
                                                device_id=place, device_id_type=MESH)
            copy.wait_send()
            copy.wait_recv()

    return pl.pallas_call(
        body, name=name,
        out_shape=(pltpu.HBM(parts_thru.shape, parts_thru.dtype), pltpu.HBM(land_thru.shape, land_thru.dtype)),
        in_specs=(_HBM, _HBM, _SEM, _SEM, pl.BlockSpec(memory_space=pl.ANY)), out_specs=(_HBM, _HBM),
        input_output_aliases={0: 0, 1: 1},
        compiler_params=pltpu.CompilerParams(has_side_effects=pltpu.SideEffectType.DATAFLOW_SIDE_EFFECTING),
    )(parts_thru, land_thru, send_sems, recv_sems, after)


def _tail_rows(conv_part, small, extra):
    lead = conv_part.shape[:-1]
    rep = jnp.concatenate([small[n].reshape(-1).astype(f32) for n in SMALL] + [extra.reshape(1).astype(f32)])
    flat = jnp.concatenate([conv_part, jnp.broadcast_to(rep, lead + rep.shape),
                            jnp.zeros(lead + (ROWS_TAIL * 1024 - TAIL_ELEMS,), f32)], axis=-1)
    return flat.reshape(lead + (ROWS_TAIL, 1024))


def _late_rows(w_in_t, tail):
    lead = tail.shape[:-2]
    zeros = lambda r: jnp.zeros(lead + (r, 1024), f32)
    return jnp.concatenate([w_in_t, zeros(OFF_TAIL - IN_SHARD), tail, zeros(LATE_ROWS - OFF_TAIL - ROWS_TAIL)], axis=-2)


def _early_rows(w_ps, w_out, w_up_t, w_down, w_pa_t):
    return jnp.concatenate([w_ps, w_out, w_up_t, w_down, w_pa_t.reshape(w_pa_t.shape[:-2] + (ROWS_PA, 1024))], axis=-2)


def _pack_shard(vals):
    tail = _tail_rows(vals["conv_w"].reshape(-1), vals, jnp.zeros((), f32))
    return jnp.concatenate([_late_rows(vals["w_in"].T, tail),
                            _early_rows(vals["w_proj_ssd"], vals["w_out"], vals["w_up"].T, vals["w_down"],
                                        vals["w_proj_attn"].T)], axis=0)


def _unpack_shard(late, early):
    e = lambda lo, hi: early[lo - LATE_ROWS:hi - LATE_ROWS]
    out = {"w_in": late[0:IN_SHARD].T, "w_proj_ssd": e(OFF_PS, OFF_OUT), "w_out": e(OFF_OUT, OFF_UP),
           "w_up": e(OFF_UP, OFF_DOWN).T, "w_down": e(OFF_DOWN, OFF_PA),
           "w_proj_attn": e(OFF_PA, PACK_ROWS).reshape(D_MODEL // N_DEV, ATTN_OUT).T}
    flat = late[OFF_TAIL:OFF_TAIL + ROWS_TAIL].reshape(-1)
    out["conv_w"] = flat[0:CONV_SHARD].reshape(D_CONV, CONV_DIM // N_DEV)
    off = CONV_SHARD
    for n in SMALL:
        out[n] = flat[off:off + SMALL_SIZES[n]]
        off += SMALL_SIZES[n]
    out["_extra"] = flat[off]
    return out


def _blocks(g):
    return g.reshape(N_DEV, g.shape[0] // N_DEV, g.shape[1])


def _pack_early_parts(full):
    return _early_rows(_blocks(full["w_proj_ssd"]), _blocks(full["w_out"]), _blocks(full["w_up_t"]),
                       _blocks(full["w_down"]), _blocks(full["w_proj_attn_t"]))


def _pack_late_parts(full, small, extra):
    conv = full["conv_w"].reshape(D_CONV, N_DEV, CONV_DIM // N_DEV).transpose(1, 0, 2).reshape(N_DEV, CONV_SHARD)
    return _late_rows(_blocks(full["w_in_t"]), _tail_rows(conv, small, extra))


def _gather_weights(w):
    conv_bits = lax.bitcast_convert_type(w["conv_w"], bf16).reshape(-1)
    conv_rows = jnp.concatenate([conv_bits, jnp.zeros((16 * 1024 - 2 * CONV_SHARD,), bf16)]).reshape(16, 1024)
    packed = _pack_shard(w)
    first = OFF_TAIL + ROWS_TAIL
    got = all_gather_blocks(jnp.concatenate([packed[0:OFF_TAIL].astype(bf16), conv_rows], axis=0))
    got, rest = lax.optimization_barrier((got, packed[first:].astype(bf16)))
    send_sems, recv_sems, rest_thru, land_thru, token = scatter_start(rest, "gather_start")
    conv =lax.bitcast_convert_type(got[:, OFF_TAIL:OFF_TAIL + 4].reshape(N_DEV, 4096)[:, 0:2 * CONV_SHARD]
                                    .reshape(N_DEV, D_CONV, CONV_DIM // N_DEV, 2), f32)
    now = {"w_in_t": got[:, 0:IN_SHARD].reshape(IN_COLS, 1024), "conv_w": conv.transpose(1, 0, 2).reshape(D_CONV, CONV_DIM)}

    def later(after):
        mine, landed = scatter_wait(send_sems, recv_sems, rest_thru, land_thru, after, "gather_wait")
        x, y, c = _place()
        landed = lax.dynamic_update_slice(landed, mine[None], (4 * x + 2 * y + c, 0, 0))
        whole = lambda lo, hi: landed[:, lo - first:hi - first].reshape(N_DEV * (hi - lo), 1024)
        return {"w_proj_ssd": whole(OFF_PS, OFF_OUT), "w_out": whole(OFF_OUT, OFF_UP), "w_up_t": whole(OFF_UP, OFF_DOWN),
                "w_down": whole(OFF_DOWN, OFF_PA),
                "w_proj_attn_t": landed[:, OFF_PA - first:PACK_ROWS - first].reshape(D_MODEL, ATTN_OUT)}

    return now, later, token


def _row(v, width=None):
    v = v.reshape(1, -1).astype(f32)
    return v if width is None else jnp.pad(v, ((0, 0), (0, width - v.shape[1])))


def _local_step(x2, tgt, wf, p, send_early=None, late_weights=None, start_token=None, send_late=None):
    t = x2.shape[0]
    o = np.cumsum((0,) + IN_SPLITS)
    wt = wf["w_in_t"]
    wt_z, wt_xbc, wt_dt = wt[o[0]:o[1]], wt[o[1]:o[2]], wt[o[2]:o[4]]
    wt_qkv, wt_gate = wt[o[4]:o[7]], wt[o[7]:o[8]]

    spread = lambda v: jnp.broadcast_to(v.astype(f32)[..., None], v.shape + (128,))
    conv_w_b, conv_b_b = spread(wf["conv_w"]), spread(p["conv_b"])
    dt_bias_b = spread(jnp.concatenate([p["dt_bias_f"], p["dt_bias_b"]]))
    a_f, a_b = -jnp.exp(p["a_log_f"].astype(f32)), -jnp.exp(p["a_log_b"].astype(f32))
    a_coef_b = spread(jnp.concatenate([a_f, a_b]))
    skip_b = spread(jnp.repeat(p["d_skip"], SSD_HEAD_DIM))
    nw_b, bg_row = spread(p["ssd_norm_w"]), _row(p["b_gate"])
    g1, b1, g2, b2 = _row(p["ln1_g"]), _row(p["ln1_b"]), _row(p["ln2_g"]), _row(p["ln2_b"])

    xb = (x2 if start_token is None else x2 + start_token[0, 0]).astype(MXU_DTYPE)
    u_z = mm_nt(wt_z, xb, "in_z")
    u_xbc = mm_nt(wt_xbc, xb, "in_xbc")
    u_dt = mm_nt(wt_dt, xb, "in_dt")
    u_qkv = mm_nt_split(xb, wt_qkv, "in_qkv", 256, bf16)
    u_gate = mm_nt(xb, wt_gate, "in_gate")
    xbc_c, dsilu = conv_fwd_t(u_xbc, conv_w_b, conv_b_b)
    dt_t = dt_fwd_t(u_dt, dt_bias_b)
    y_f, h_f = ssd_fwd_t(xbc_c, dt_t, a_coef_b, False, "ssd_fwd_f")
    y_scan, h_b, yn = ssd_fwd_t(xbc_c, dt_t, a_coef_b, True, "ssd_fwd_b", prev=y_f, tail=(u_z, skip_b, nw_b))
    if late_weights is not None:
        wf = {**wf, **late_weights(yn)}
    y_ssd = mm_tn(yn, wf["w_proj_ssd"], "proj_ssd")

    def strided(a, dil):
        return a.reshape(t // dil, dil * 256)

    qkv, outs, lses = [], [], []
    for pi, (_, dil) in enumerate(DIL_PATTERNS):
        q, k, v = (strided(u_qkv[N_PATTERNS * s + pi], dil) for s in range(3))
        qkv.append((q, k, v))
        op, lp = attn_fwd(q, k, v, pi, dil, f"attn_fwd_{pi}")
        outs.append(op.reshape(t, 256))
        lses.append(lp.reshape(t, 256))
    ya, lse = attn_combine(outs, lses)
    y_att = mm_nt(ya, wf["w_proj_attn_t"], "proj_attn")
    m = merge_fwd(u_gate, bg_row, y_ssd, y_att)
    mix = mm_nn(m, wf["w_out"], "out_proj")
    h1, h1b = ln1_fwd(x2, mix, g1, b1)
    r_up, p_act = mm_nt(h1b, wf["w_up_t"], "mlp_up", relu2=True)
    f_dn = mm_nn(p_act, wf["w_down"], "mlp_down")
    dr2, dr2b, dg2, db2, loss8 = ln2_loss(h1, f_dn, g2, b2, tgt)

    full, small = {}, {}
    da = mm_nt(dr2b, wf["w_down"], "d_mlp_act", out_dtype=bf16, relu2_of=r_up)
    full["w_down"] = mm_tn(p_act, dr2b, "dw_down")
    full["w_up_t"] = mm_tn(da, h1b, "dw_up")
    dh1 = mm_nn(da, wf["w_up_t"], "d_h1", acc_in=dr2, acc_scale=ALPHA)
    dr1, dr1b, dg1, db1 = ln1_bwd(dh1, x2, mix, g1)
    dm = mm_nt(dr1b, wf["w_out"], "d_merge")
    full["w_out"] = mm_tn(m, dr1b, "dw_out")
    dys, dya_p, dga, dgb, dba, dbb = merge_bwd(dm, u_gate, bg_row, y_ssd, y_att)
    dyn = mm_nt(wf["w_proj_ssd"], dys, "d_yn")
    full["w_proj_ssd"] = mm_nn(yn, dys, "dw_proj_ssd")
    dya = mm_nn(dya_p, wf["w_proj_attn_t"], "d_ya")
    full["w_proj_attn_t"] = mm_tn(dya_p, ya, "dw_proj_attn")
    if send_early is not None:
        skip_b = skip_b + send_early(full)[0, 0]

    dxf, dbf, dcf, ddtf, daf, dy, du_ssd, dnw, ddx = ssd_bwd_t(xbc_c, dt_t, a_coef_b, dyn, h_f, False, "ssd_bwd_f",
                                                               skip_b=skip_b, tail=(y_scan, u_z, nw_b))
    dxs, dbs, dcs, ddtb, dab = ssd_bwd_t(xbc_c, dt_t, a_coef_b, dy, h_b, True, "ssd_bwd_b", prev=(dxf, dbf, dcf))
    du_ssd, dcw_x, dcb_x = conv_bwd_t(u_xbc, dsilu, dxs, conv_w_b, du_ssd, "conv_bwd_x", 0)
    du_ssd, dcw_b, dcb_b = conv_bwd_t(u_xbc, dsilu, dbs, conv_w_b, du_ssd, "conv_bwd_b", D_INNER)
    du_ssd, dcw_c, dcb_c = conv_bwd_t(u_xbc, dsilu, dcs, conv_w_b, du_ssd, "conv_bwd_c", D_INNER + 512)
    du_ssd, dbias = dt_bwd_t(ddtf, ddtb, u_dt, dt_bias_b, du_ssd)

    delta = attn_delta(dya, ya)
    dqs, dks, dvs = [], [], []
    for pi, (_, dil) in enumerate(DIL_PATTERNS):
        q, k, v = qkv[pi]
        sd, sl_, sdel = strided(dya, dil), strided(lse, dil), strided(delta, dil)
        dqs.append(attn_dq(q, k, v, sd, sl_, sdel, pi, dil, f"attn_dq_{pi}").reshape(t, 256))
        dk, dv = attn_dkv(q, k, v, sd, sl_, sdel, pi, dil, f"attn_dkv_{pi}")
        dks.append(dk.reshape(t, 256))
        dvs.append(dv.reshape(t, 256))
    du_qkv = jnp.concatenate(dqs + dks + dvs, axis=1)
    du_gate = jnp.concatenate([dga, dgb], axis=1)

    full["w_in_t"] = jnp.concatenate(
        [mm_nn(du_ssd, xb, "dw_in_ssd"), mm_tn(du_qkv, xb, "dw_in_qkv"), mm_tn(du_gate, xb, "dw_in_gate")], axis=0)
    lanes = lambda v: jnp.sum(v, axis=-1)
    full["conv_w"] = jnp.concatenate([lanes(dcw_x), lanes(dcw_b), lanes(dcw_c)], axis=1)

    small["b_gate"] = jnp.concatenate([dba, dbb], axis=1)
    small["conv_b"] = jnp.concatenate([lanes(dcb_x), lanes(dcb_b), lanes(dcb_c)])
    dbias = lanes(dbias)
    small["dt_bias_f"], small["dt_bias_b"] = dbias[0:32], dbias[32:64]
    small["a_log_f"] = lanes(daf) * a_f
    small["a_log_b"] = lanes(dab) * a_b
    small["d_skip"] = jnp.sum(lanes(ddx).reshape(SSD_HEADS, SSD_HEAD_DIM), axis=1)
    small["ssd_norm_w"] = lanes(dnw)
    small["ln1_g"], small["ln1_b"], small["ln2_g"], small["ln2_b"] = dg1, db1, dg2, db2

    wt_ssd = wt[0:SSD_COLS]
    if send_late is not None:
        wt_ssd = wt_ssd + send_late(full, small, loss8[0, 0])[0, 0].astype(wt_ssd.dtype)
    dx = mm_tn(du_ssd, wt_ssd, "dx_ssd", acc_in=dr1, acc_scale=ALPHA)
    dx = mm_nn(du_qkv, wt_qkv, "dx_qkv", acc_in=dx)
    dx = mm_nn(du_gate, wt_gate, "dx_gate", acc_in=dx)
    return loss8[0, 0], dx, full, small


def kernel(x, w_in, b_gate, conv_w, conv_b, dt_bias_f, dt_bias_b, a_log_f, a_log_b, d_skip, ssd_norm_w, w_proj_ssd, w_proj_attn, w_out, ln1_g, ln1_b, w_up, w_down, ln2_g, ln2_b, loss_target, m_w_in, m_b_gate, m_conv_w, m_conv_b, m_dt_bias_f, m_dt_bias_b, m_a_log_f, m_a_log_b, m_d_skip, m_ssd_norm_w, m_w_proj_ssd, m_w_proj_attn, m_w_out, m_ln1_g, m_ln1_b, m_w_up, m_w_down, m_ln2_g, m_ln2_b, v_w_in, v_b_gate, v_conv_w, v_conv_b, v_dt_bias_f, v_dt_bias_b, v_a_log_f, v_a_log_b, v_d_skip, v_ssd_norm_w, v_w_proj_ssd, v_w_proj_attn, v_w_out, v_ln1_g, v_ln1_b, v_w_up, v_w_down, v_ln2_g, v_ln2_b):
    given = dict(locals())
    w = {n: given[n] for n in WEIGHTS}
    mom = {n: given["m_" + n] for n in WEIGHTS}
    var = {n: given["v_" + n] for n in WEIGHTS}
    t = x.shape[1]
    wf, late_weights, start_token = _gather_weights(w)
    in_flight = []

    def send_early(full):
        send_sems, recv_sems, parts_thru, land_thru, token = scatter_start(_pack_early_parts(full), "scatter_start")
        in_flight.append((send_sems, recv_sems, parts_thru, land_thru))
        return token

    def send_late(full, small, loss):
        late = _pack_late_parts(full, small, loss)
        rows = scatter_start(late.astype(bf16), "late_start")
        tail = scatter_start(late[:, OFF_TAIL:OFF_TAIL + ROWS_TAIL], "tail_start")
        in_flight.extend([rows[0:4], tail[0:4]])
        return rows[4] + tail[4]

    loss, dx, full, small = _local_step(x.reshape(t, D_MODEL), loss_target.reshape(t, D_MODEL), wf, w, send_early,
                                        late_weights, start_token, send_late)
    x_, y_, c_ = _place()
    me = (4 * x_ + 2 * y_ + c_).astype(jnp.int32).reshape(1)
    wp, mp, vp = _pack_shard(w), _pack_shard(mom), _pack_shard(var)
    early_parts, early_landed = scatter_wait(*in_flight[0], dx, "scatter_wait")
    early_out = adamw_sum8(early_landed, early_parts, me, wp, mp, vp, LATE_ROWS, "adamw_early")
    late_parts, late_landed = scatter_wait(*in_flight[1], dx, "late_wait")
    tail_parts, tail_landed = scatter_wait(*in_flight[2], dx, "tail_wait")
    late_out = adamw_sum8(late_landed, late_parts, me, wp, mp, vp, 0, "adamw_late", tails=(tail_landed, tail_parts))
    g, delta, new_m, new_v = (_unpack_shard(a, b) for a, b in zip(late_out, early_out))
    outs = [g["_extra"], dx.reshape(x.shape)]
    for d in (g, delta, new_m, new_v):
        outs += [d[n].reshape(w[n].shape) for n in WEIGHTS]
    return tuple(outs)
```

```python
import jax
import jax.numpy as jnp
import numpy as np
from jax import lax
from jax.experimental import pallas as pl
from jax.experimental.pallas import tpu as pltpu

f32 = jnp.float32
bf16 = jnp.bfloat16
MXU_DTYPE = jnp.bfloat16

N_DEV = 8
D_MODEL = 1024
D_INNER = 2048
SSD_HEADS = 32
SSD_HEAD_DIM = 64
SSD_GROUPS = 4
D_STATE = 128
D_CONV = 5
CHUNK = 128
CONV_DIM = D_INNER + 2 * SSD_GROUPS * D_STATE
NORM_EPS = 1e-5
ATTN_HEAD_DIM = 64
DIL_PATTERNS = ((128, 1), (512, 4), (2048, 16))
N_PATTERNS = len(DIL_PATTERNS)
HEADS_PER_PATTERN = 4
ATTN_HEADS = 12
ATTN_WIDTH = 768
ATTN_OUT = 256
D_FF = 4096
ALPHA = 2.0 ** 0.25
IN_SPLITS = (D_INNER, CONV_DIM, SSD_HEADS, SSD_HEADS, ATTN_WIDTH, ATTN_WIDTH, ATTN_WIDTH, 2 * D_MODEL)
IN_COLS = sum(IN_SPLITS)
SSD_COLS = sum(IN_SPLITS[0:4])
ADAM_LR, ADAM_B1, ADAM_B2, ADAM_EPS, ADAM_WD, ADAM_STEP = 0.001, 0.9, 0.999, 1e-08, 0.01, 10
NEG_BIG = -1e30
VMEM_LIMIT = 56 * 1024 * 1024
MESH = pl.DeviceIdType.MESH

SMALL = ("b_gate", "conv_b", "dt_bias_f", "dt_bias_b", "a_log_f", "a_log_b", "d_skip", "ssd_norm_w",
         "ln1_g", "ln1_b", "ln2_g", "ln2_b")
WEIGHTS = ("w_in", "b_gate", "conv_w", "conv_b", "dt_bias_f", "dt_bias_b", "a_log_f", "a_log_b", "d_skip",
           "ssd_norm_w", "w_proj_ssd", "w_proj_attn", "w_out", "ln1_g", "ln1_b", "w_up", "w_down", "ln2_g", "ln2_b")
SMALL_SIZES = {"b_gate": 2 * D_MODEL, "conv_b": CONV_DIM, "dt_bias_f": 32, "dt_bias_b": 32, "a_log_f": 32, "a_log_b": 32,
               "d_skip": 32, "ssd_norm_w": D_INNER, "ln1_g": D_MODEL, "ln1_b": D_MODEL, "ln2_g": D_MODEL, "ln2_b": D_MODEL}
IN_SHARD = IN_COLS // N_DEV
OFF_TAIL = 1200
ROWS_TAIL = 16
LATE_ROWS = 1280
ROWS_PS, ROWS_OUT, ROWS_UP, ROWS_DOWN, ROWS_PA = D_INNER // N_DEV, D_MODEL // N_DEV, D_FF // N_DEV, D_FF // N_DEV, 32
OFF_PS = LATE_ROWS
OFF_OUT = OFF_PS + ROWS_PS
OFF_UP = OFF_OUT + ROWS_OUT
OFF_DOWN = OFF_UP + ROWS_UP
OFF_PA = OFF_DOWN + ROWS_DOWN
PACK_ROWS = OFF_PA + ROWS_PA
EARLY_ROWS = PACK_ROWS - LATE_ROWS
EARLY_TILE = 160
CONV_SHARD = D_CONV * CONV_DIM // N_DEV
TAIL_ELEMS = CONV_SHARD + sum(SMALL_SIZES.values()) + 1


def _cparams(sem=None, **kw):
    return pltpu.CompilerParams(dimension_semantics=sem, vmem_limit_bytes=VMEM_LIMIT, **kw)


def _mx(v):
    return v.astype(MXU_DTYPE)


def _dot(a, b):
    return jnp.dot(_mx(a), _mx(b), preferred_element_type=f32)


def _dot_nt(a, b):
    return lax.dot_general(_mx(a), _mx(b), (((1,), (1,)), ((), ())), preferred_element_type=f32)


def _dot_tn(a, b):
    return lax.dot_general(_mx(a), _mx(b), (((0,), (0,)), ((), ())), preferred_element_type=f32)


def _dot_exact(a, b):
    return jnp.dot(a, b, precision=lax.Precision.HIGHEST, preferred_element_type=f32)


def _sigmoid(v):
    return 1.0 / (1.0 + jnp.exp(-v))


def _pick(n, prefs):
    for p in prefs:
        if n % p == 0:
            return p
    return n


MM_TILE = 1024


def mm_nn(a, b, name, out_dtype=f32, acc_in=None, acc_scale=1.0):
    m, k = a.shape
    n = b.shape[1]
    tm = _pick(m, (MM_TILE, 1728, 512, 256, 128, 64))
    tn = _pick(n, (MM_TILE, 512, 256, 128))
    tk = _pick(k, (2048, 1536, 1152, 1024, 768, 512, 256, 128))
    nk = k // tk

    def body(*refs):
        a_ref, b_ref = refs[0:2]
        c_ref = refs[2] if acc_in is not None else None
        o_ref = refs[3] if acc_in is not None else refs[2]

        def finish(r):
            if acc_in is not None:
                r = r + acc_scale * c_ref[...]
            o_ref[...] = r.astype(o_ref.dtype)

        if nk == 1:
            finish(_dot(a_ref[...], b_ref[...]))
            return
        acc_ref = refs[-1]
        kk = pl.program_id(2)

        @pl.when(kk == 0)
        def _():
            acc_ref[...] = jnp.zeros_like(acc_ref)

        acc_ref[...] += _dot(a_ref[...], b_ref[...])

        @pl.when(kk == nk - 1)
        def _():
            finish(acc_ref[...])

    in_specs = [pl.BlockSpec((tm, tk), lambda i, j, kk: (i, kk)), pl.BlockSpec((tk, tn), lambda i, j, kk: (kk, j))]
    args = [a, b]
    if acc_in is not None:
        in_specs.append(pl.BlockSpec((tm, tn), lambda i, j, kk: (i, j)))
        args.append(acc_in)
    return pl.pallas_call(
        body, name=name, grid=(m // tm, n // tn, nk), in_specs=in_specs,
        out_specs=pl.BlockSpec((tm, tn), lambda i, j, kk: (i, j)),
        out_shape=jax.ShapeDtypeStruct((m, n), out_dtype),
        scratch_shapes=[pltpu.VMEM((tm, tn), f32)] if nk > 1 else [],
        compiler_params=_cparams(("parallel", "parallel", "arbitrary")))(*args)


def mm_nt(a, b, name, out_dtype=f32, relu2=None, relu2_of=None):
    m, k = a.shape
    n = b.shape[0]
    tm = _pick(m, (MM_TILE, 512, 256, 128, 64))
    tn = _pick(n, (MM_TILE, 768, 512, 256, 128))

    def body(*refs):
        r = _dot_nt(refs[0][...], refs[1][...])
        if relu2:
            pos = jnp.maximum(r, 0.0)
            refs[2][...] = pos.astype(refs[2].dtype)
            refs[3][...] = (pos * pos).astype(refs[3].dtype)
        elif relu2_of is not None:
            refs[3][...] = (r * (2.0 * refs[2][...].astype(f32))).astype(refs[3].dtype)
        else:
            refs[2][...] = r.astype(refs[2].dtype)

    blk = pl.BlockSpec((tm, tn), lambda i, j: (i, j))
    in_specs = [pl.BlockSpec((tm, k), lambda i, j: (i, 0)), pl.BlockSpec((tn, k), lambda i, j: (j, 0))]
    args = [a, b]
    if relu2_of is not None:
        in_specs.append(blk)
        args.append(relu2_of)
    if relu2:
        out_specs, out_shape = [blk, blk], [jax.ShapeDtypeStruct((m, n), bf16), jax.ShapeDtypeStruct((m, n), bf16)]
    else:
        out_specs, out_shape = blk, jax.ShapeDtypeStruct((m, n), out_dtype)
    return pl.pallas_call(body, name=name, grid=(m // tm, n // tn), in_specs=in_specs, out_specs=out_specs,
                          out_shape=out_shape, compiler_params=_cparams(("parallel", "parallel")))(*args)


def mm_nt_split(a, b, name, width, out_dtype=f32):
    m, k = a.shape
    n = b.shape[0]
    tm = MM_TILE
    parts = n // width

    def body(a_ref, b_ref, *o_refs):
        r = _dot_nt(a_ref[...], b_ref[...])
        for q in range(parts):
            o_refs[q][...] = r[:, width * q:width * (q + 1)].astype(o_refs[q].dtype)

    blk = pl.BlockSpec((tm, width), lambda i: (i, 0))
    return pl.pallas_call(
        body, name=name, grid=(m // tm,),
        in_specs=[pl.BlockSpec((tm, k), lambda i: (i, 0)), pl.BlockSpec((n, k), lambda i: (0, 0))],
        out_specs=[blk] * parts, out_shape=[jax.ShapeDtypeStruct((m, width), out_dtype)] * parts,
        compiler_params=_cparams(("parallel",)))(a, b)


def mm_tn(a, b, name, acc_in=None, acc_scale=1.0):
    k, m = a.shape
    n = b.shape[1]
    tm = _pick(m, (MM_TILE, 768, 512, 256, 128))
    tn = _pick(n, (MM_TILE, 512, 256, 128))
    tk = _pick(k, (2048, 1728, 1024, 768, 512, 256, 128, 64))
    nk = k // tk

    def body(*refs):
        a_ref, b_ref, o_ref = refs[0], refs[1], refs[-1]
        kk = pl.program_id(2)

        @pl.when(kk == 0)
        def _():
            o_ref[...] = jnp.zeros_like(o_ref) if acc_in is None else acc_scale * refs[2][...]

        o_ref[...] += _dot_tn(a_ref[...], b_ref[...])

    in_specs = [pl.BlockSpec((tk, tm), lambda i, j, kk: (kk, i)), pl.BlockSpec((tk, tn), lambda i, j, kk: (kk, j))]
    args = [a, b]
    if acc_in is not None:
        in_specs.append(pl.BlockSpec((tm, tn), lambda i, j, kk: (i, j)))
        args.append(acc_in)
    return pl.pallas_call(
        body, name=name, grid=(m // tm, n // tn, nk), in_specs=in_specs,
        out_specs=pl.BlockSpec((tm, tn), lambda i, j, kk: (i, j)),
        out_shape=jax.ShapeDtypeStruct((m, n), f32),
        compiler_params=_cparams(("parallel", "parallel", "arbitrary")))(*args)


def _lane_col(mat, lane_idx, h):
    return jnp.sum(jnp.where(lane_idx == h, mat, 0.0), axis=1, keepdims=True)


def _slopes(p):
    return [2.0 ** (-8.0 * (HEADS_PER_PATTERN * p + j + 1) / ATTN_HEADS) for j in range(HEADS_PER_PATTERN)]


def _win_specs(nq, col_of):
    return [pl.BlockSpec((64, 256), lambda r, i: (jnp.maximum(2 * i - 1, 0), col_of(r))),
            pl.BlockSpec((128, 256), lambda r, i: (i, col_of(r))),
            pl.BlockSpec((64, 256), lambda r, i: (jnp.minimum(2 * i + 2, 2 * nq - 1), col_of(r)))]


def _lane_head(shape):
    return lax.broadcasted_iota(jnp.int32, shape, 1) >> 6


def _stack_heads(m):
    lane_head = _lane_head(m.shape)
    return jnp.concatenate([jnp.where(lane_head == j, m, 0.0) for j in range(HEADS_PER_PATTERN)], axis=0)


def _unstack_heads(m4, n):
    lane_head = _lane_head((n, 256))
    out = jnp.where(lane_head == 0, m4[0:n], 0.0)
    for j in range(1, HEADS_PER_PATTERN):
        out = out + jnp.where(lane_head == j, m4[j * n:(j + 1) * n], 0.0)
    return out


def _head_cols(m, n):
    lane = lax.broadcasted_iota(jnp.int32, (n, 256), 1)
    return jnp.concatenate([jnp.sum(jnp.where(lane == ATTN_HEAD_DIM * j, m, 0.0), axis=1, keepdims=True)
                            for j in range(HEADS_PER_PATTERN)], axis=0)


def _score_bias(p, dil, by_key):
    slopes = np.asarray(_slopes(p), np.float32)
    if by_key:
        win = np.arange(256)[:, None]
        rel = np.arange(128)[None, :] - (win - 64)
    else:
        win = np.arange(256)[None, :]
        rel = win - 64 - np.arange(128)[:, None]
    band = np.abs(rel) <= 64
    out = []
    for first, last in ((False, False), (True, False), (False, True), (True, True)):
        ok = band & ~(first & (win < 64)) & ~(last & (win >= 192))
        pen = -slopes[:, None, None] * (np.abs(rel) * dil).astype(np.float32)[None]
        out.append(np.where(ok[None], pen, np.float32(NEG_BIG)).reshape(-1, rel.shape[1]))
    return jnp.asarray(np.stack(out), f32)


def _bias_spec(nq, rows, cols):
    return pl.BlockSpec((1, rows, cols), lambda r, i: ((i == 0).astype(jnp.int32) + 2 * (i == nq - 1).astype(jnp.int32), 0, 0))


def attn_fwd(q, k, v, p, dil, name):
    l = q.shape[0]
    nq = l // 128

    def body(q_ref, kp_ref, ko_ref, kn_ref, vp_ref, vo_ref, vn_ref, bias_ref, o_ref, lse_ref):
        kcat = jnp.concatenate([kp_ref[...], ko_ref[...], kn_ref[...]], axis=0)
        vcat = jnp.concatenate([vp_ref[...], vo_ref[...], vn_ref[...]], axis=0)
        s = _dot_nt(_stack_heads(q_ref[...] * 0.125), kcat) + bias_ref[0]
        m = jnp.max(s, axis=1, keepdims=True)
        pr = jnp.exp(s - m)
        den = jnp.sum(pr, axis=1, keepdims=True)
        o4 = _dot(pr, vcat) / den
        o_ref[...] = _unstack_heads(o4, 128)
        lse_ref[...] = _unstack_heads(jnp.broadcast_to(m + jnp.log(den), (512, 256)), 128)

    col = lambda r: r
    return pl.pallas_call(
        body, name=name, grid=(dil, nq),
        in_specs=[pl.BlockSpec((128, 256), lambda r, i: (i, r))] + _win_specs(nq, col) + _win_specs(nq, col)
        + [_bias_spec(nq, 512, 256)],
        out_specs=[pl.BlockSpec((128, 256), lambda r, i: (i, r))] * 2,
        out_shape=[jax.ShapeDtypeStruct(q.shape, f32)] * 2,
        compiler_params=_cparams(("parallel", "parallel")))(q, k, k, k, v, v, v, _score_bias(p, dil, False))


def attn_combine(os_, lses, tb=1024):
    t = os_[0].shape[0]

    def body(o0, o1, o2, l0, l1, l2, y_ref, lse_ref):
        a0, a1, a2 = l0[...], l1[...], l2[...]
        m = jnp.maximum(jnp.maximum(a0, a1), a2)
        e0, e1, e2 = jnp.exp(a0 - m), jnp.exp(a1 - m), jnp.exp(a2 - m)
        den = e0 + e1 + e2
        y_ref[...] = (e0 * o0[...] + e1 * o1[...] + e2 * o2[...]) / den
        lse_ref[...] = m + jnp.log(den)

    blk = pl.BlockSpec((tb, 256), lambda i: (i, 0))
    return pl.pallas_call(
        body, name="attn_combine", grid=(t // tb,), in_specs=[blk] * 6, out_specs=[blk, blk],
        out_shape=[jax.ShapeDtypeStruct((t, 256), f32)] * 2,
        compiler_params=_cparams(("parallel",)))(*os_, *lses)


def attn_delta(dy, y, tb=1024):
    t = dy.shape[0]

    def body(dy_ref, y_ref, d_ref):
        pr = dy_ref[...] * y_ref[...]
        lane_head = _lane_head(pr.shape)
        out = jnp.zeros_like(pr)
        for j in range(HEADS_PER_PATTERN):
            sj = jnp.sum(jnp.where(lane_head == j, pr, 0.0), axis=1, keepdims=True)
            out = out + jnp.where(lane_head == j, sj, 0.0)
        d_ref[...] = out

    blk = pl.BlockSpec((tb, 256), lambda i: (i, 0))
    return pl.pallas_call(body, name="attn_delta", grid=(t // tb,), in_specs=[blk, blk], out_specs=blk,
                          out_shape=jax.ShapeDtypeStruct((t, 256), f32),
                          compiler_params=_cparams(("parallel",)))(dy, y)


def attn_dq(q, k, v, dy, lse, delta, p, dil, name):
    l = q.shape[0]
    nq = l // 128

    def body(q_ref, kp_ref, ko_ref, kn_ref, vp_ref, vo_ref, vn_ref, dy_ref, lse_ref, d_ref, bias_ref, dq_ref):
        kcat = jnp.concatenate([kp_ref[...], ko_ref[...], kn_ref[...]], axis=0)
        vcat = jnp.concatenate([vp_ref[...], vo_ref[...], vn_ref[...]], axis=0)
        s = _dot_nt(_stack_heads(q_ref[...] * 0.125), kcat) + bias_ref[0]
        pr = jnp.exp(s - _head_cols(lse_ref[...], 128))
        dp = _dot_nt(_stack_heads(dy_ref[...]), vcat)
        ds = pr * (dp - _head_cols(d_ref[...], 128))
        dq_ref[...] = (_unstack_heads(_dot(ds, kcat), 128) * 0.125).astype(dq_ref.dtype)

    col = lambda r: r
    own = pl.BlockSpec((128, 256), lambda r, i: (i, r))
    return pl.pallas_call(
        body, name=name, grid=(dil, nq),
        in_specs=[own] + _win_specs(nq, col) + _win_specs(nq, col) + [own, own, own, _bias_spec(nq, 512, 256)],
        out_specs=own, out_shape=jax.ShapeDtypeStruct(q.shape, bf16),
        compiler_params=_cparams(("parallel", "parallel")))(q, k, k, k, v, v, v, dy, lse, delta, _score_bias(p, dil, False))


def attn_dkv(q, k, v, dy, lse, delta, p, dil, name):
    l = q.shape[0]
    nq = l // 128

    def body(qp_ref, qo_ref, qn_ref, gp_ref, go_ref, gn_ref, lp_ref, lo_ref, ln_ref, dp_ref, do_ref, dn_ref,
             k_ref, v_ref, bias_ref, dk_ref, dv_ref):
        cat = lambda a, b, c: jnp.concatenate([a[...], b[...], c[...]], axis=0)
        q4 = _stack_heads(cat(qp_ref, qo_ref, qn_ref) * 0.125)
        dy4 = _stack_heads(cat(gp_ref, go_ref, gn_ref))
        lse4 = _head_cols(cat(lp_ref, lo_ref, ln_ref), 256)
        del4 = _head_cols(cat(dp_ref, do_ref, dn_ref), 256)
        s = _dot_nt(q4, k_ref[...]) + bias_ref[0]
        pr = jnp.exp(s - lse4)
        dpm = _dot_nt(dy4, v_ref[...])
        ds = pr * (dpm - del4)
        dv_ref[...] = _dot_tn(pr, dy4).astype(dv_ref.dtype)
        dk_ref[...] = _dot_tn(ds, q4).astype(dk_ref.dtype)

    col = lambda r: r
    own = pl.BlockSpec((128, 256), lambda r, i: (i, r))
    win = _win_specs(nq, col)
    return pl.pallas_call(
        body, name=name, grid=(dil, nq), in_specs=win * 4 + [own, own, _bias_spec(nq, 1024, 128)], out_specs=[own, own],
        out_shape=[jax.ShapeDtypeStruct(q.shape, bf16)] * 2,
        compiler_params=_cparams(("parallel", "parallel")))(q, q, q, dy, dy, dy, lse, lse, lse, delta, delta, delta, k, v,
                                                            _score_bias(p, dil, True))


def _lanes(v, reps):
    return v if reps == 1 else jnp.tile(v, (1, reps))


def _lane_halo_specs(cb, tb, nt, off=0):
    r = tb // 128
    return [pl.BlockSpec((cb, 128), lambda j, i: (j + off, jnp.maximum(i * r - 1, 0))),
            pl.BlockSpec((cb, tb), lambda j, i: (j + off, i)),
            pl.BlockSpec((cb, 128), lambda j, i: (j + off, jnp.minimum((i + 1) * r, nt * r - 1)))]


def _with_lane_halo(prev_ref, own_ref, next_ref, i, nt):
    prev = jnp.where(i > 0, prev_ref[...].astype(f32), 0.0)
    nxt = jnp.where(i < nt - 1, next_ref[...].astype(f32), 0.0)
    return jnp.concatenate([prev, own_ref[...].astype(f32), nxt], axis=1)


def _lane_shifted(xcat, s, tb):
    n = xcat.shape[1]
    return pltpu.roll(xcat, (-s) % n, 1)[:, 128:128 + tb]


def conv_fwd_t(xbc_t, w_b, b_b, tb=1024, cb=256):
    c, t = xbc_t.shape
    nt = t // tb

    def body(prev_ref, own_ref, next_ref, w_ref, b_ref, o_ref, ds_ref):
        i = pl.program_id(1)
        xcat = _with_lane_halo(prev_ref, own_ref, next_ref, i, nt)
        reps = tb // 128
        pre = _lanes(b_ref[...], reps)
        for k in range(D_CONV):
            pre = pre + _lanes(w_ref[k], reps) * _lane_shifted(xcat, k - 2, tb)
        sg = _sigmoid(pre)
        o_ref[...] = pre * sg
        ds_ref[...] = sg * (1.0 + pre * (1.0 - sg))

    blk = pl.BlockSpec((cb, tb), lambda j, i: (j, i))
    return pl.pallas_call(
        body, name="conv_fwd", grid=(c // cb, nt),
        in_specs=_lane_halo_specs(cb, tb, nt) + [pl.BlockSpec((D_CONV, cb, 128), lambda j, i: (0, j, 0)),
                                                 pl.BlockSpec((cb, 128), lambda j, i: (j, 0))],
        out_specs=[blk, blk], out_shape=[jax.ShapeDtypeStruct((c, t), f32)] * 2,
        compiler_params=_cparams(("parallel", "parallel")))(xbc_t, xbc_t, xbc_t, w_b, b_b)


def conv_bwd_t(xbc_t, dsilu_t, grad_t, w_b, into, name, row0, tb=1024, cb=256):
    c, t = grad_t.shape
    nt = t // tb
    off = row0 // cb
    off_out = (D_INNER + row0) // cb
    reps = tb // 128

    def body(*refs):
        i = pl.program_id(1)
        x_ref, sr, gr = refs[0], refs[1:4], refs[4:7]
        w_ref = refs[7]
        dx_ref, dw_ref, db_ref = refs[-3:]
        wk = [_lanes(w_ref[k], reps) for k in range(D_CONV)]
        dpre = _with_lane_halo(*gr, i, nt) * _with_lane_halo(*sr, i, nt)

        def fold(v):
            s = v[:, 0:128]
            for q in range(1, reps):
                s = s + v[:, 128 * q:128 * (q + 1)]
            return s

        @pl.when(i == 0)
        def _():
            dw_ref[...] = jnp.zeros_like(dw_ref)
            db_ref[...] = jnp.zeros_like(db_ref)

        x_own = x_ref[...]
        dx = None
        for k in range(D_CONV):
            shifted = _lane_shifted(dpre, 2 - k, tb)
            term = wk[k] * shifted
            dx = term if dx is None else dx + term
            dw_ref[k] += fold(shifted * x_own)
        dx_ref[...] = dx.astype(dx_ref.dtype)
        db_ref[...] += fold(dpre[:, 128:128 + tb])

    in_specs = ([pl.BlockSpec((cb, tb), lambda j, i: (j + off, i))] + _lane_halo_specs(cb, tb, nt, off)
                + _lane_halo_specs(cb, tb, nt)
                + [pl.BlockSpec((D_CONV, cb, 128), lambda j, i: (0, j + off, 0)), pl.BlockSpec(memory_space=pl.ANY)])
    args = [xbc_t] + [dsilu_t] * 3 + [grad_t] * 3 + [w_b, into]
    return pl.pallas_call(
        body, name=name, grid=(c // cb, nt), in_specs=in_specs,
        out_specs=[pl.BlockSpec((cb, tb), lambda j, i: (j + off_out, i)),
                   pl.BlockSpec((D_CONV, cb, 128), lambda j, i: (0, j, 0)), pl.BlockSpec((cb, 128), lambda j, i: (j, 0))],
        out_shape=[jax.ShapeDtypeStruct(into.shape, into.dtype), jax.ShapeDtypeStruct((D_CONV, c, 128), f32),
                   jax.ShapeDtypeStruct((c, 128), f32)],
        input_output_aliases={8: 0}, compiler_params=_cparams(("parallel", "arbitrary")))(*args)


def dt_fwd_t(u_dt_t, bias_b, tb=2048):
    r, t = u_dt_t.shape

    def body(u_ref, b_ref, o_ref):
        v = u_ref[...] + _lanes(b_ref[...], tb // 128)
        o_ref[...] = jnp.maximum(v, 0.0) + jnp.log(1.0 + jnp.exp(-jnp.abs(v)))

    return pl.pallas_call(
        body, name="dt_fwd", grid=(t // tb,),
        in_specs=[pl.BlockSpec((r, tb), lambda i: (0, i)), pl.BlockSpec((r, 128), lambda i: (0, 0))],
        out_specs=pl.BlockSpec((r, tb), lambda i: (0, i)), out_shape=jax.ShapeDtypeStruct((r, t), f32),
        compiler_params=_cparams(("parallel",)))(u_dt_t, bias_b)


def dt_bwd_t(ddt_f, ddt_b, u_dt_t, bias_b, into, tb=2048):
    r, t = u_dt_t.shape
    reps = tb // 128
    row_blk = (SSD_COLS - r) // r

    def body(gf_ref, gb_ref, u_ref, b_ref, into_ref, du_ref, db_ref):
        g = jnp.concatenate([gf_ref[...], gb_ref[...]], axis=0)
        du = g * _sigmoid(u_ref[...] + _lanes(b_ref[...], reps))
        du_ref[...] = du.astype(du_ref.dtype)

        @pl.when(pl.program_id(0) == 0)
        def _():
            db_ref[...] = jnp.zeros_like(db_ref)

        s = du[:, 0:128]
        for q in range(1, reps):
            s = s + du[:, 128 * q:128 * (q + 1)]
        db_ref[...] += s

    half = pl.BlockSpec((r // 2, tb), lambda i: (0, i))
    return pl.pallas_call(
        body, name="dt_bwd", grid=(t // tb,),
        in_specs=[half, half, pl.BlockSpec((r, tb), lambda i: (0, i)), pl.BlockSpec((r, 128), lambda i: (0, 0)),
                  pl.BlockSpec(memory_space=pl.ANY)],
        out_specs=[pl.BlockSpec((r, tb), lambda i: (row_blk, i)), pl.BlockSpec((r, 128), lambda i: (0, 0))],
        out_shape=[jax.ShapeDtypeStruct(into.shape, into.dtype), jax.ShapeDtypeStruct((r, 128), f32)],
        input_output_aliases={4: 0}, compiler_params=_cparams(("arbitrary",)))(ddt_f, ddt_b, u_dt_t, bias_b, into)


HEADS_PER_GROUP = SSD_HEADS // SSD_GROUPS


def _group_rows(g, n):
    return pl.ds(pl.multiple_of(g * n, n), n)


def _ssd_decays(dt_blk, a_blk, reverse):
    row = lax.broadcasted_iota(jnp.int32, (CHUNK, CHUNK), 0)
    col = lax.broadcasted_iota(jnp.int32, (CHUNK, CHUNK), 1)
    mask = (row <= col) if reverse else (row >= col)
    tri = mask.astype(f32)
    a8 = dt_blk * a_blk
    a = jnp.concatenate([a8, jnp.zeros((CHUNK - HEADS_PER_GROUP, CHUNK), f32)], axis=0).T
    acs = _dot_exact(tri, a)
    return mask, tri, a8, acs, acs.T, col


def ssd_fwd_t(xbc_ct, dt_t, a_b, reverse, name, prev=None, tail=None):
    t = xbc_ct.shape[1]
    nc = t // CHUNK
    direction = 1 if reverse else 0

    def cidx(c):
        return nc - 1 - c if reverse else c

    def body(*refs):
        x_ref, b_ref, c_ref, dt_ref, a_ref = refs[0:5]
        pos = 5
        prev_ref = None
        if prev is not None:
            prev_ref = refs[pos]
            pos += 1
        if tail is not None:
            z_ref, skip_ref, nw_ref = refs[pos:pos + 3]
            pos += 3
            y_ref, hp_ref, yn_ref, h_scr = refs[pos:pos + 4]
        else:
            y_ref, hp_ref, h_scr = refs[pos:pos + 3]

        @pl.when(pl.program_id(0) == 0)
        def _():
            h_scr[...] = jnp.zeros_like(h_scr)

        def group(g, carry):
            x_v, y_v = x_ref.at[_group_rows(g, 512)], y_ref.at[_group_rows(g, 512)]
            heads = _group_rows(g, HEADS_PER_GROUP)
            hp_v, h_v = hp_ref.at[0, heads], h_scr.at[heads]
            dt_blk = dt_ref[heads, :]
            mask, tri, a8, acs, acs_t, lane = _ssd_decays(dt_blk, a_ref[heads, :], reverse)
            bm = b_ref[_group_rows(g, 128), :].T
            cm = c_ref[_group_rows(g, 128), :].T
            cb = _dot_nt(cm, bm)
            tot = jnp.sum(a8, axis=1, keepdims=True)
            for j in range(HEADS_PER_GROUP):
                rows = slice(SSD_HEAD_DIM * j, SSD_HEAD_DIM * (j + 1))
                col_j = _lane_col(acs, lane, j)
                row_j = acs_t[j:j + 1, :]
                lmat = jnp.where(mask, jnp.exp(jnp.where(mask, col_j - row_j, 0.0)), 0.0)
                xdt = x_v[rows, :] * dt_blk[j:j + 1, :]
                hp = h_v[j]
                hp_v[j] = hp
                y = _dot_nt(xdt, cb * lmat) + _dot_nt(hp, cm) * jnp.exp(row_j)
                if prev_ref is not None:
                    y = y + prev_ref.at[_group_rows(g, 512)][rows, :]
                y_v[rows, :] = y
                tot_j = tot[j:j + 1, :]
                h_v[j] = jnp.exp(tot_j) * hp + _dot(xdt * jnp.exp(tot_j - row_j), bm)
            if tail is not None:
                rows = _group_rows(g, 512)
                zz = z_ref[rows, :]
                yg = (y_v[...] + skip_ref[rows, :] * x_v[...]) * (zz * _sigmoid(zz))
                rstd = lax.rsqrt(jnp.mean(yg * yg, axis=0, keepdims=True) + NORM_EPS)
                yn_ref[rows, :] = (yg * rstd * nw_ref[rows, :]).astype(yn_ref.dtype)
            return carry

        lax.fori_loop(0, SSD_GROUPS, group, 0, unroll=True)

    big = pl.BlockSpec((D_INNER, CHUNK), lambda c: (0, cidx(c)))
    par = pl.BlockSpec((D_INNER, 128), lambda c: (0, 0))
    in_specs = [big, pl.BlockSpec((512, CHUNK), lambda c: (4, cidx(c))), pl.BlockSpec((512, CHUNK), lambda c: (5, cidx(c))),
                pl.BlockSpec((SSD_HEADS, CHUNK), lambda c: (direction, cidx(c))),
                pl.BlockSpec((SSD_HEADS, 128), lambda c: (direction, 0))]
    args = [xbc_ct, xbc_ct, xbc_ct, dt_t, a_b]
    out_specs = [big, pl.BlockSpec((1, SSD_HEADS, SSD_HEAD_DIM, D_STATE), lambda c: (cidx(c), 0, 0, 0))]
    out_shape = [jax.ShapeDtypeStruct((D_INNER, t), f32), jax.ShapeDtypeStruct((nc, SSD_HEADS, SSD_HEAD_DIM, D_STATE), f32)]
    if prev is not None:
        in_specs.append(big)
        args.append(prev)
    if tail is not None:
        in_specs += [big, par, par]
        args += list(tail)
        out_specs.append(big)
        out_shape.append(jax.ShapeDtypeStruct((D_INNER, t), bf16))
    return pl.pallas_call(
        body, name=name, grid=(nc,), in_specs=in_specs, out_specs=out_specs, out_shape=out_shape,
        scratch_shapes=[pltpu.VMEM((SSD_HEADS, SSD_HEAD_DIM, D_STATE), f32)],
        compiler_params=_cparams(("arbitrary",)))(*args)


def ssd_bwd_t(xbc_ct, dt_t, a_b, dy_t, hprev, reverse, name, skip_b=None, prev=None, tail=None):
    t = xbc_ct.shape[1]
    nc = t // CHUNK
    direction = 1 if reverse else 0

    def cidx(c):
        return c if reverse else nc - 1 - c

    def body(*refs):
        x_ref, b_ref, c_ref, dt_ref, a_ref, dy_ref, hp_ref = refs[0:7]
        pos = 7
        skip_ref = None
        if skip_b is not None:
            skip_ref = refs[pos]
            pos += 1
        prev_refs = None
        if prev is not None:
            prev_refs = refs[pos:pos + 3]
            pos += 3
        if tail is not None:
            ys_ref, z_ref, nw_ref = refs[pos:pos + 3]
            pos += 3
        dx_ref, db_ref, dc_ref, ddt_ref, da_ref = refs[pos:pos + 5]
        pos += 5
        if tail is not None:
            dyout_ref, dz_ref, dnw_ref, ddx_ref = refs[pos:pos + 4]
            pos += 4
        dh_scr = refs[pos]

        @pl.when(pl.program_id(0) == 0)
        def _():
            dh_scr[...] = jnp.zeros_like(dh_scr)
            da_ref[...] = jnp.zeros_like(da_ref)
            if tail is not None:
                dnw_ref[...] = jnp.zeros_like(dnw_ref)
                ddx_ref[...] = jnp.zeros_like(ddx_ref)

        def group(g, carry):
            big, st, heads = _group_rows(g, 512), _group_rows(g, 128), _group_rows(g, HEADS_PER_GROUP)
            x_v, dy_v, dx_v = x_ref.at[big], dy_ref.at[big], dx_ref.at[big]
            hp_v, dh_v = hp_ref.at[0, heads], dh_scr.at[heads]
            dy_grp = None
            if tail is not None:
                zz = z_ref[big, :]
                sg = _sigmoid(zz)
                sl = zz * sg
                x_all = x_v[...]
                y = ys_ref[big, :] + skip_ref[big, :] * x_all
                yz = y * sl
                rstd = lax.rsqrt(jnp.mean(yz * yz, axis=0, keepdims=True) + NORM_EPS)
                yhat = yz * rstd
                gy = dy_v[...]
                dyhat = gy * nw_ref[big, :]
                dyz = rstd * (dyhat - yhat * jnp.mean(dyhat * yhat, axis=0, keepdims=True))
                dy_grp = dyz * sl
                dyout_ref[big, :] = dy_grp
                dz_ref[big, :] = (dyz * y * sg * (1.0 + zz * (1.0 - sg))).astype(dz_ref.dtype)
                dnw_ref[big, :] += gy * yhat
                ddx_ref[big, :] += dy_grp * x_all
            dt_blk = dt_ref[heads, :]
            a_blk = a_ref[heads, :]
            mask, tri, a8, acs, acs_t, lane = _ssd_decays(dt_blk, a_blk, reverse)
            sub = lax.broadcasted_iota(jnp.int32, (CHUNK, CHUNK), 0)
            mask_t = (sub >= lane) if reverse else (sub <= lane)
            bm = b_ref[st, :].T
            cm = c_ref[st, :].T
            cb = _dot_nt(cm, bm)
            cb_t = _dot_nt(bm, cm)
            tot = jnp.sum(a8, axis=1, keepdims=True)
            dcb = jnp.zeros((CHUNK, CHUNK), f32)
            dbm = jnp.zeros((CHUNK, D_STATE), f32)
            dcm = jnp.zeros((CHUNK, D_STATE), f32)
            dacs_rows, ddtx_rows = [], []
            for j in range(HEADS_PER_GROUP):
                rows = slice(SSD_HEAD_DIM * j, SSD_HEAD_DIM * (j + 1))
                col_j = _lane_col(acs, lane, j)
                row_j = acs_t[j:j + 1, :]
                dt_j = dt_blk[j:j + 1, :]
                tot_j = tot[j:j + 1, :]
                lmat = jnp.where(mask, jnp.exp(jnp.where(mask, col_j - row_j, 0.0)), 0.0)
                lmat_t = jnp.where(mask_t, jnp.exp(jnp.where(mask_t, row_j - col_j, 0.0)), 0.0)
                x = x_v[rows, :]
                xdt = x * dt_j
                dyh = dy_v[rows, :] if dy_grp is None else dy_grp[rows]
                hp = hp_v[j]
                dhn = dh_v[j]
                ml = _dot_tn(dyh, xdt) * lmat
                w_t = _dot_tn(xdt, dyh) * lmat_t * cb_t
                dcb = dcb + ml
                dacs = jnp.sum(w_t, axis=0, keepdims=True) - jnp.sum(ml * cb, axis=0, keepdims=True)
                ecol = jnp.exp(row_j)
                dec = jnp.exp(tot_j - row_j)
                dye = dyh * ecol
                yoff = _dot_nt(hp, cm) * ecol
                gmat = _dot_nt(dhn, bm)
                dxdt = _dot(dyh, cb * lmat) + dec * gmat
                s_dec = jnp.sum(xdt * gmat, axis=0, keepdims=True) * dec
                dacs = dacs + jnp.sum(dyh * yoff, axis=0, keepdims=True) - s_dec
                dcd = jnp.sum(jnp.sum(dhn * hp, axis=1, keepdims=True), axis=0, keepdims=True)
                dtot = jnp.sum(s_dec, axis=1, keepdims=True) + jnp.exp(tot_j) * dcd
                dacs_rows.append((dacs, dtot))
                ddtx_rows.append(jnp.sum(dxdt * x, axis=0, keepdims=True))
                dcm = dcm + _dot_tn(dye, hp)
                dbm = dbm + _dot_tn(xdt * dec, dhn)
                dxh = dxdt * dt_j
                if skip_ref is not None:
                    dxh = dxh + skip_ref.at[big][rows, :] * dyh
                if prev_refs is not None:
                    dxh = dxh + prev_refs[0].at[big][rows, :]
                dx_v[rows, :] = dxh
                dh_v[j] = jnp.exp(tot_j) * dhn + _dot(dye, cm)
            dcm = dcm + _dot(dcb, bm)
            dbm = dbm + _dot_tn(dcb, cm)
            dbt, dct = dbm.T, dcm.T
            if prev_refs is not None:
                dbt = dbt + prev_refs[1][st, :]
                dct = dct + prev_refs[2][st, :]
            db_ref[st, :] = dbt
            dc_ref[st, :] = dct
            dacs8 = jnp.concatenate([d for d, _ in dacs_rows], axis=0)
            dtot8 = jnp.concatenate([d for _, d in dacs_rows], axis=0)
            da8 = _dot_exact(dacs8, tri) + dtot8
            ddt_ref[heads, :] = da8 * a_blk + jnp.concatenate(ddtx_rows, axis=0)
            da_ref[heads, :] += da8 * dt_blk
            return carry

        lax.fori_loop(0, SSD_GROUPS, group, 0, unroll=True)

    big = pl.BlockSpec((D_INNER, CHUNK), lambda c: (0, cidx(c)))
    st = pl.BlockSpec((512, CHUNK), lambda c: (0, cidx(c)))
    in_specs = [big, pl.BlockSpec((512, CHUNK), lambda c: (4, cidx(c))), pl.BlockSpec((512, CHUNK), lambda c: (5, cidx(c))),
                pl.BlockSpec((SSD_HEADS, CHUNK), lambda c: (direction, cidx(c))),
                pl.BlockSpec((SSD_HEADS, 128), lambda c: (direction, 0)), big,
                pl.BlockSpec((1, SSD_HEADS, SSD_HEAD_DIM, D_STATE), lambda c: (cidx(c), 0, 0, 0))]
    args = [xbc_ct, xbc_ct, xbc_ct, dt_t, a_b, dy_t, hprev]
    if skip_b is not None:
        in_specs.append(pl.BlockSpec((D_INNER, 128), lambda c: (0, 0)))
        args.append(skip_b)
    if prev is not None:
        in_specs += [big, st, st]
        args += list(prev)
    par = pl.BlockSpec((D_INNER, 128), lambda c: (0, 0))
    out_specs = [big, st, st, pl.BlockSpec((SSD_HEADS, CHUNK), lambda c: (0, cidx(c))),
                 pl.BlockSpec((SSD_HEADS, 128), lambda c: (0, 0))]
    out_shape = [jax.ShapeDtypeStruct((D_INNER, t), f32), jax.ShapeDtypeStruct((512, t), f32),
                 jax.ShapeDtypeStruct((512, t), f32), jax.ShapeDtypeStruct((SSD_HEADS, t), f32),
                 jax.ShapeDtypeStruct((SSD_HEADS, 128), f32)]
    if tail is not None:
        in_specs += [big, big, par]
        args += list(tail)
        out_specs += [big, big, par, par]
        out_shape += [jax.ShapeDtypeStruct((D_INNER, t), f32), jax.ShapeDtypeStruct((SSD_COLS, t), bf16),
                      jax.ShapeDtypeStruct((D_INNER, 128), f32), jax.ShapeDtypeStruct((D_INNER, 128), f32)]
    return pl.pallas_call(
        body, name=name, grid=(nc,), in_specs=in_specs, out_specs=out_specs, out_shape=out_shape,
        scratch_shapes=[pltpu.VMEM((SSD_HEADS, SSD_HEAD_DIM, D_STATE), f32)],
        compiler_params=_cparams(("arbitrary",)))(*args)


def merge_fwd(u_gate, bg_row, y_ssd, y_att, tb=512):
    t = y_ssd.shape[0]

    def body(ga_ref, gb_ref, ba_ref, bb_ref, ys_ref, ya_ref, o_ref):
        o_ref[...] = (_sigmoid(ga_ref[...] + ba_ref[...]) * ys_ref[...]
                      + _sigmoid(gb_ref[...] + bb_ref[...]) * ya_ref[...]).astype(o_ref.dtype)

    blk = pl.BlockSpec((tb, 512), lambda i, j: (i, j))
    blk2 = pl.BlockSpec((tb, 512), lambda i, j: (i, 2 + j))
    row = pl.BlockSpec((1, 512), lambda i, j: (0, j))
    row2 = pl.BlockSpec((1, 512), lambda i, j: (0, 2 + j))
    return pl.pallas_call(
        body, name="merge_fwd", grid=(t // tb, 2), in_specs=[blk, blk2, row, row2, blk, blk], out_specs=blk,
        out_shape=jax.ShapeDtypeStruct((t, D_MODEL), bf16),
        compiler_params=_cparams(("parallel", "parallel")))(u_gate, u_gate, bg_row, bg_row, y_ssd, y_att)


def merge_bwd(dm, u_gate, bg_row, y_ssd, y_att, tb=512):
    t = dm.shape[0]

    def body(dm_ref, ga_ref, gb_ref, ba_ref, bb_ref, ys_ref, ya_ref, dys_ref, dya_ref, dga_ref, dgb_ref, dba_ref, dbb_ref):
        d = dm_ref[...]
        sa = _sigmoid(ga_ref[...] + ba_ref[...])
        sb = _sigmoid(gb_ref[...] + bb_ref[...])
        dys_ref[...] = (d * sa).astype(dys_ref.dtype)
        dya_ref[...] = (d * sb).astype(dya_ref.dtype)
        dla = d * ys_ref[...] * sa * (1.0 - sa)
        dlb = d * ya_ref[...] * sb * (1.0 - sb)
        dga_ref[...] = dla.astype(dga_ref.dtype)
        dgb_ref[...] = dlb.astype(dgb_ref.dtype)

        @pl.when(pl.program_id(1) == 0)
        def _():
            dba_ref[...] = jnp.zeros_like(dba_ref)
            dbb_ref[...] = jnp.zeros_like(dbb_ref)

        dba_ref[...] += jnp.sum(dla, axis=0, keepdims=True)
        dbb_ref[...] += jnp.sum(dlb, axis=0, keepdims=True)

    blk = pl.BlockSpec((tb, 512), lambda j, i: (i, j))
    blk2 = pl.BlockSpec((tb, 512), lambda j, i: (i, 2 + j))
    row = pl.BlockSpec((1, 512), lambda j, i: (0, j))
    row2 = pl.BlockSpec((1, 512), lambda j, i: (0, 2 + j))
    act = jax.ShapeDtypeStruct((t, D_MODEL), bf16)
    vec = jax.ShapeDtypeStruct((1, D_MODEL), f32)
    return pl.pallas_call(
        body, name="merge_bwd", grid=(2, t // tb), in_specs=[blk, blk, blk2, row, row2, blk, blk],
        out_specs=[blk, blk, blk, blk, row, row], out_shape=[act, act, act, act, vec, vec],
        compiler_params=_cparams(("parallel", "arbitrary")))(dm, u_gate, u_gate, bg_row, bg_row, y_ssd, y_att)


def _ln_stats(r):
    mu = jnp.mean(r, axis=1, keepdims=True)
    xc = r - mu
    rstd = lax.rsqrt(jnp.mean(xc * xc, axis=1, keepdims=True) + NORM_EPS)
    return xc * rstd, rstd


def _ln_bwd(dy, xhat, rstd, g_row):
    dxh = dy * g_row
    return rstd * (dxh - jnp.mean(dxh, axis=1, keepdims=True) - xhat * jnp.mean(dxh * xhat, axis=1, keepdims=True))


def ln1_fwd(x, mix, g_row, b_row, tb=512):
    t = x.shape[0]

    def body(x_ref, m_ref, g_ref, b_ref, o_ref, ob_ref):
        xhat, _ = _ln_stats(ALPHA * x_ref[...] + m_ref[...])
        h = xhat * g_ref[...] + b_ref[...]
        o_ref[...] = h
        ob_ref[...] = h.astype(ob_ref.dtype)

    blk = pl.BlockSpec((tb, D_MODEL), lambda i: (i, 0))
    row = pl.BlockSpec((1, D_MODEL), lambda i: (0, 0))
    return pl.pallas_call(body, name="ln1_fwd", grid=(t // tb,), in_specs=[blk, blk, row, row], out_specs=[blk, blk],
                          out_shape=[jax.ShapeDtypeStruct((t, D_MODEL), f32), jax.ShapeDtypeStruct((t, D_MODEL), bf16)],
                          compiler_params=_cparams(("parallel",)))(x, mix, g_row, b_row)


def ln1_bwd(dh, x, mix, g_row, tb=512):
    t = x.shape[0]

    def body(dh_ref, x_ref, m_ref, g_ref, dr_ref, drb_ref, dg_ref, db_ref):
        xhat, rstd = _ln_stats(ALPHA * x_ref[...] + m_ref[...])
        dy = dh_ref[...]
        dr = _ln_bwd(dy, xhat, rstd, g_ref[...])
        dr_ref[...] = dr
        drb_ref[...] = dr.astype(drb_ref.dtype)

        @pl.when(pl.program_id(0) == 0)
        def _():
            dg_ref[...] = jnp.zeros_like(dg_ref)
            db_ref[...] = jnp.zeros_like(db_ref)

        dg_ref[...] += jnp.sum(dy * xhat, axis=0, keepdims=True)
        db_ref[...] += jnp.sum(dy, axis=0, keepdims=True)

    blk = pl.BlockSpec((tb, D_MODEL), lambda i: (i, 0))
    row = pl.BlockSpec((1, D_MODEL), lambda i: (0, 0))
    return pl.pallas_call(
        body, name="ln1_bwd", grid=(t // tb,), in_specs=[blk, blk, blk, row], out_specs=[blk, blk, row, row],
        out_shape=[jax.ShapeDtypeStruct((t, D_MODEL), f32), jax.ShapeDtypeStruct((t, D_MODEL), bf16),
                   jax.ShapeDtypeStruct((1, D_MODEL), f32), jax.ShapeDtypeStruct((1, D_MODEL), f32)],
        compiler_params=_cparams(("arbitrary",)))(dh, x, mix, g_row)


def ln2_loss(h1, f, g_row, b_row, target, tb=512):
    t = h1.shape[0]

    def body(h_ref, f_ref, g_ref, b_ref, t_ref, dr_ref, drb_ref, dg_ref, db_ref, loss_ref):
        xhat, rstd = _ln_stats(ALPHA * h_ref[...] + f_ref[...])
        g = g_ref[...]
        err = xhat * g + b_ref[...] - t_ref[...]
        dy = err * (1.0 / D_MODEL)
        dr = _ln_bwd(dy, xhat, rstd, g)
        dr_ref[...] = dr
        drb_ref[...] = dr.astype(drb_ref.dtype)

        @pl.when(pl.program_id(0) == 0)
        def _():
            dg_ref[...] = jnp.zeros_like(dg_ref)
            db_ref[...] = jnp.zeros_like(db_ref)
            loss_ref[...] = jnp.zeros_like(loss_ref)

        dg_ref[...] += jnp.sum(dy * xhat, axis=0, keepdims=True)
        db_ref[...] += jnp.sum(dy, axis=0, keepdims=True)
        part = jnp.sum(jnp.mean(err * err, axis=1, keepdims=True), axis=0, keepdims=True)
        loss_ref[...] += 0.5 * part

    blk = pl.BlockSpec((tb, D_MODEL), lambda i: (i, 0))
    row = pl.BlockSpec((1, D_MODEL), lambda i: (0, 0))
    return pl.pallas_call(
        body, name="ln2_loss", grid=(t // tb,), in_specs=[blk, blk, row, row, blk],
        out_specs=[blk, blk, row, row, pl.BlockSpec((8, 128), lambda i: (0, 0))],
        out_shape=[jax.ShapeDtypeStruct((t, D_MODEL), f32), jax.ShapeDtypeStruct((t, D_MODEL), bf16),
                   jax.ShapeDtypeStruct((1, D_MODEL), f32), jax.ShapeDtypeStruct((1, D_MODEL), f32),
                   jax.ShapeDtypeStruct((8, 128), f32)],
        compiler_params=_cparams(("arbitrary",)))(h1, f, g_row, b_row, target)


def _adamw_update(g, w_ref, m_ref, v_ref, g_ref, d_ref, nm_ref, nv_ref):
    c1 = 1.0 - ADAM_B1 ** ADAM_STEP
    c2 = 1.0 - ADAM_B2 ** ADAM_STEP
    nm = ADAM_B1 * m_ref[...] + (1.0 - ADAM_B1) * g
    nv = ADAM_B2 * v_ref[...] + (1.0 - ADAM_B2) * (g * g)
    g_ref[...] = g
    nm_ref[...] = nm
    nv_ref[...] = nv
    d_ref[...] = -ADAM_LR * ((nm / c1) / (jnp.sqrt(nv / c2) + ADAM_EPS) + ADAM_WD * w_ref[...])


def adamw_sum8(landed, parts, me, w, m, v, row0, name, tails=None):
    rows = landed.shape[1]
    off = row0 // EARLY_TILE
    tail_blk, tail_at = divmod(OFF_TAIL - row0, EARLY_TILE)

    def body(me_ref, *refs):
        src = refs[0:N_DEV]
        own_ref = refs[N_DEV]
        pos = N_DEV + 1
        mine = me_ref[0]

        def sum8(own, slots):
            g = None
            for s in range(N_DEV):
                term = jnp.where(mine == s, own, slots(s)).astype(f32)
                g = term if g is None else g + term
            return g

        g = sum8(own_ref[0], lambda s: src[s][0])
        if tails is not None:
            tl_ref, tm_ref = refs[pos:pos + 2]
            pos += 2
            own_tail = tm_ref[0]
            for s in range(1, N_DEV):
                own_tail = jnp.where(mine == s, tm_ref[s], own_tail)
            gt = sum8(own_tail, lambda s: tl_ref[s])
            with_tail = jnp.concatenate([g[0:tail_at], gt, g[tail_at + ROWS_TAIL:]], axis=0)
            g = jnp.where(pl.program_id(0) == tail_blk, with_tail, g)
        w_ref, m_ref, v_ref = refs[pos:pos + 3]
        _adamw_update(g, w_ref, m_ref, v_ref, *refs[pos + 3:])

    def slot(s):
        return pl.BlockSpec((1, EARLY_TILE, 1024), lambda i, me_ref: (jnp.where(me_ref[0] == s, (s + 1) % N_DEV, s), i, 0))

    shard = pl.BlockSpec((EARLY_TILE, 1024), lambda i, me_ref: (i + off, 0))
    out_blk = pl.BlockSpec((EARLY_TILE, 1024), lambda i, me_ref: (i, 0))
    in_specs = [slot(s) for s in range(N_DEV)] + [pl.BlockSpec((1, EARLY_TILE, 1024), lambda i, me_ref: (me_ref[0], i, 0))]
    args = [landed] * N_DEV + [parts]
    if tails is not None:
        whole = pl.BlockSpec((N_DEV, ROWS_TAIL, 1024), lambda i, me_ref: (0, 0, 0))
        in_specs += [whole, whole]
        args += list(tails)
    grid_spec = pltpu.PrefetchScalarGridSpec(num_scalar_prefetch=1, grid=(rows // EARLY_TILE,),
                                             in_specs=in_specs + [shard, shard, shard], out_specs=[out_blk] * 4)
    out = jax.ShapeDtypeStruct((rows, 1024), f32)
    return pl.pallas_call(body, name=name, grid_spec=grid_spec, out_shape=[out] * 4,
                          compiler_params=_cparams(("parallel",)))(me, *args, w, m, v)


def _place():
    return lax.axis_index("x"), lax.axis_index("y"), lax.axis_index("c")


def all_gather_blocks(shard):
    rows, cols = shard.shape

    def body(x_ref, out_ref, send_sems, recv_sems, local_sem):
        x, y, c = _place()
        me, sibling = (x, y, c), (x, y, 1 - c)
        chips = [(1 - x, y), (x, 1 - y), (1 - x, 1 - y)]

        def slot(px, py, pc):
            return out_ref.at[4 * px + 2 * py + pc]

        def copy(k, block, to, src=None):
            return pltpu.make_async_remote_copy(
                src_ref=slot(*block) if src is None else src, dst_ref=slot(*block), send_sem=send_sems.at[k],
                recv_sem=recv_sems.at[k], device_id=to, device_id_type=MESH)

        mine = pltpu.make_async_copy(x_ref, slot(*me), local_sem)
        mine.start()
        first = [copy(0, me, sibling, src=x_ref)]
        first += [copy(1 + j, me, (*chip, c), src=x_ref) for j, chip in enumerate(chips)]
        for cp in first:
            cp.start()
        passed = [copy(4 + j, (*chip, c), sibling) for j, chip in enumerate(chips)]
        for j, chip in enumerate(chips):
            copy(1 + j, (*chip, c), me).wait_recv()
            passed[j].start()
        copy(0, sibling, me).wait_recv()
        for j, chip in enumerate(chips):
            copy(4 + j, (*chip, 1 - c), me).wait_recv()
        for cp in first + passed:
            cp.wait_send()
        mine.wait()

    return pl.pallas_call(
        body, name="all_gather_blocks", out_shape=jax.ShapeDtypeStruct((N_DEV, rows, cols), shard.dtype),
        in_specs=[pl.BlockSpec(memory_space=pl.ANY)], out_specs=pl.BlockSpec(memory_space=pl.ANY),
        scratch_shapes=[pltpu.SemaphoreType.DMA((7,)), pltpu.SemaphoreType.DMA((7,)), pltpu.SemaphoreType.DMA],
        compiler_params=pltpu.CompilerParams(has_side_effects=True))(shard)


_HBM = pl.BlockSpec(memory_space=pltpu.HBM)
_SEM = pl.BlockSpec(memory_space=pltpu.SEMAPHORE)


def _peer(k):
    x, y, c = _place()
    px, py, pc = (1 - x if k & 4 else x), (1 - y if k & 2 else y), (1 - c if k & 1 else c)
    return (px, py, pc), 4 * px + 2 * py + pc


def scatter_start(parts, name):
    per_device = parts.ndim == 3

    def body(p_ref, land_ref, send_sems, recv_sems, p_thru, land_thru, token):
        x, y, c = _place()
        me = 4 * x + 2 * y + c
        for k in range(1, N_DEV):
            place, idx = _peer(k)
            pltpu.make_async_remote_copy(src_ref=p_ref.at[idx] if per_device else p_ref, dst_ref=land_ref.at[me],
                                         send_sem=send_sems.at[k - 1], recv_sem=recv_sems.at[k - 1], device_id=place,
                                         device_id_type=MESH).start()
        token[...] = jnp.zeros_like(token)

    land_shape = parts.shape if per_device else (N_DEV,) + parts.shape
    landing = lax.empty(land_shape, parts.dtype)
    return pl.pallas_call(
        body, name=name,
        out_shape=(pltpu.SemaphoreType.DMA((N_DEV - 1,)), pltpu.SemaphoreType.DMA((N_DEV - 1,)),
                   pltpu.HBM(parts.shape, parts.dtype), pltpu.HBM(land_shape, parts.dtype),
                   jax.ShapeDtypeStruct((8, 128), f32)),
        in_specs=(_HBM, _HBM), out_specs=(_SEM, _SEM, _HBM, _HBM, pl.BlockSpec(memory_space=pltpu.VMEM)),
        input_output_aliases={0: 2, 1: 3},
        compiler_params=pltpu.CompilerParams(has_side_effects=pltpu.SideEffectType.DATAFLOW_SIDE_EFFECTING),
    )(pltpu.with_memory_space_constraint(parts, pltpu.HBM), pltpu.with_memory_space_constraint(landing, pltpu.HBM))


def scatter_wait(send_sems, recv_sems, parts_thru, land_thru, after, name):
    per_device = parts_thru.ndim == 3

    def body(p_ref, land_ref, send_sems, recv_sems, after_ref, p_out, land_out):
        for k in range(1, N_DEV):
            place, idx = _peer(k)
            copy = pltpu.make_async_remote_copy(src_ref=p_ref.at[idx] if per_device else p_ref, dst_ref=land_ref.at[idx],
                                                send_sem=send_sems.at[k - 1], recv_sem=recv_sems.at[k - 1],
                                                device_id=place, device_id_type=MESH)
            copy.wait_send()
            copy.wait_recv()

    return pl.pallas_call(
        body, name=name,
        out_shape=(pltpu.HBM(parts_thru.shape, parts_thru.dtype), pltpu.HBM(land_thru.shape, land_thru.dtype)),
        in_specs=(_HBM, _HBM, _SEM, _SEM, pl.BlockSpec(memory_space=pl.ANY)), out_specs=(_HBM, _HBM),
        input_output_aliases={0: 0, 1: 1},
        compiler_params=pltpu.CompilerParams(has_side_effects=pltpu.SideEffectType.DATAFLOW_SIDE_EFFECTING),
    )(parts_thru, land_thru, send_sems, recv_sems, after)


def _tail_rows(conv_part, small, extra):
    lead = conv_part.shape[:-1]
    rep = jnp.concatenate([small[n].reshape(-1).astype(f32) for n in SMALL] + [extra.reshape(1).astype(f32)])
    flat = jnp.concatenate([conv_part, jnp.broadcast_to(rep, lead + rep.shape),
                            jnp.zeros(lead + (ROWS_TAIL * 1024 - TAIL_ELEMS,), f32)], axis=-1)
    return flat.reshape(lead + (ROWS_TAIL, 1024))


def _late_rows(w_in_t, tail):
    lead = tail.shape[:-2]
    zeros = lambda r: jnp.zeros(lead + (r, 1024), f32)
    return jnp.concatenate([w_in_t, zeros(OFF_TAIL - IN_SHARD), tail, zeros(LATE_ROWS - OFF_TAIL - ROWS_TAIL)], axis=-2)


def _early_rows(w_ps, w_out, w_up_t, w_down, w_pa_t):
    return jnp.concatenate([w_ps, w_out, w_up_t, w_down, w_pa_t.reshape(w_pa_t.shape[:-2] + (ROWS_PA, 1024))], axis=-2)


def _pack_shard(vals):
    tail = _tail_rows(vals["conv_w"].reshape(-1), vals, jnp.zeros((), f32))
    return jnp.concatenate([_late_rows(vals["w_in"].T, tail),
                            _early_rows(vals["w_proj_ssd"], vals["w_out"], vals["w_up"].T, vals["w_down"],
                                        vals["w_proj_attn"].T)], axis=0)


def _unpack_shard(late, early):
    e = lambda lo, hi: early[lo - LATE_ROWS:hi - LATE_ROWS]
    out = {"w_in": late[0:IN_SHARD].T, "w_proj_ssd": e(OFF_PS, OFF_OUT), "w_out": e(OFF_OUT, OFF_UP),
           "w_up": e(OFF_UP, OFF_DOWN).T, "w_down": e(OFF_DOWN, OFF_PA),
           "w_proj_attn": e(OFF_PA, PACK_ROWS).reshape(D_MODEL // N_DEV, ATTN_OUT).T}
    flat = late[OFF_TAIL:OFF_TAIL + ROWS_TAIL].reshape(-1)
    out["conv_w"] = flat[0:CONV_SHARD].reshape(D_CONV, CONV_DIM // N_DEV)
    off = CONV_SHARD
    for n in SMALL:
        out[n] = flat[off:off + SMALL_SIZES[n]]
        off += SMALL_SIZES[n]
    out["_extra"] = flat[off]
    return out


def _blocks(g):
    return g.reshape(N_DEV, g.shape[0] // N_DEV, g.shape[1])


def _pack_early_parts(full):
    return _early_rows(_blocks(full["w_proj_ssd"]), _blocks(full["w_out"]), _blocks(full["w_up_t"]),
                       _blocks(full["w_down"]), _blocks(full["w_proj_attn_t"]))


def _pack_late_parts(full, small, extra):
    conv = full["conv_w"].reshape(D_CONV, N_DEV, CONV_DIM // N_DEV).transpose(1, 0, 2).reshape(N_DEV, CONV_SHARD)
    return _late_rows(_blocks(full["w_in_t"]), _tail_rows(conv, small, extra))


def _gather_weights(w):
    conv_bits = lax.bitcast_convert_type(w["conv_w"], bf16).reshape(-1)
    conv_rows = jnp.concatenate([conv_bits, jnp.zeros((16 * 1024 - 2 * CONV_SHARD,), bf16)]).reshape(16, 1024)
    packed = _pack_shard(w)
    first = OFF_TAIL + ROWS_TAIL
    got = all_gather_blocks(jnp.concatenate([packed[0:OFF_TAIL].astype(bf16), conv_rows], axis=0))
    got, rest = lax.optimization_barrier((got, packed[first:].astype(bf16)))
    send_sems, recv_sems, rest_thru, land_thru, token = scatter_start(rest, "gather_start")
    conv =lax.bitcast_convert_type(got[:, OFF_TAIL:OFF_TAIL + 4].reshape(N_DEV, 4096)[:, 0:2 * CONV_SHARD]
                                    .reshape(N_DEV, D_CONV, CONV_DIM // N_DEV, 2), f32)
    now = {"w_in_t": got[:, 0:IN_SHARD].reshape(IN_COLS, 1024), "conv_w": conv.transpose(1, 0, 2).reshape(D_CONV, CONV_DIM)}

    def later(after):
        mine, landed = scatter_wait(send_sems, recv_sems, rest_thru, land_thru, after, "gather_wait")
        x, y, c = _place()
        landed = lax.dynamic_update_slice(landed, mine[None], (4 * x + 2 * y + c, 0, 0))
        whole = lambda lo, hi: landed[:, lo - first:hi - first].reshape(N_DEV * (hi - lo), 1024)
        return {"w_proj_ssd": whole(OFF_PS, OFF_OUT), "w_out": whole(OFF_OUT, OFF_UP), "w_up_t": whole(OFF_UP, OFF_DOWN),
                "w_down": whole(OFF_DOWN, OFF_PA),
                "w_proj_attn_t": landed[:, OFF_PA - first:PACK_ROWS - first].reshape(D_MODEL, ATTN_OUT)}

    return now, later, token


def _row(v, width=None):
    v = v.reshape(1, -1).astype(f32)
    return v if width is None else jnp.pad(v, ((0, 0), (0, width - v.shape[1])))


def _local_step(x2, tgt, wf, p, send_early=None, late_weights=None, start_token=None, send_late=None):
    t = x2.shape[0]
    o = np.cumsum((0,) + IN_SPLITS)
    wt = wf["w_in_t"]
    wt_z, wt_xbc, wt_dt = wt[o[0]:o[1]], wt[o[1]:o[2]], wt[o[2]:o[4]]
    wt_qkv, wt_gate = wt[o[4]:o[7]], wt[o[7]:o[8]]

    spread = lambda v: jnp.broadcast_to(v.astype(f32)[..., None], v.shape + (128,))
    conv_w_b, conv_b_b = spread(wf["conv_w"]), spread(p["conv_b"])
    dt_bias_b = spread(jnp.concatenate([p["dt_bias_f"], p["dt_bias_b"]]))
    a_f, a_b = -jnp.exp(p["a_log_f"].astype(f32)), -jnp.exp(p["a_log_b"].astype(f32))
    a_coef_b = spread(jnp.concatenate([a_f, a_b]))
    skip_b = spread(jnp.repeat(p["d_skip"], SSD_HEAD_DIM))
    nw_b, bg_row = spread(p["ssd_norm_w"]), _row(p["b_gate"])
    g1, b1, g2, b2 = _row(p["ln1_g"]), _row(p["ln1_b"]), _row(p["ln2_g"]), _row(p["ln2_b"])

    xb = (x2 if start_token is None else x2 + start_token[0, 0]).astype(MXU_DTYPE)
    u_z = mm_nt(wt_z, xb, "in_z")
    u_xbc = mm_nt(wt_xbc, xb, "in_xbc")
    u_dt = mm_nt(wt_dt, xb, "in_dt")
    u_qkv = mm_nt_split(xb, wt_qkv, "in_qkv", 256, bf16)
    u_gate = mm_nt(xb, wt_gate, "in_gate")
    xbc_c, dsilu = conv_fwd_t(u_xbc, conv_w_b, conv_b_b)
    dt_t = dt_fwd_t(u_dt, dt_bias_b)
    y_f, h_f = ssd_fwd_t(xbc_c, dt_t, a_coef_b, False, "ssd_fwd_f")
    y_scan, h_b, yn = ssd_fwd_t(xbc_c, dt_t, a_coef_b, True, "ssd_fwd_b", prev=y_f, tail=(u_z, skip_b, nw_b))
    if late_weights is not None:
        wf = {**wf, **late_weights(yn)}
    y_ssd = mm_tn(yn, wf["w_proj_ssd"], "proj_ssd")

    def strided(a, dil):
        return a.reshape(t // dil, dil * 256)

    qkv, outs, lses = [], [], []
    for pi, (_, dil) in enumerate(DIL_PATTERNS):
        q, k, v = (strided(u_qkv[N_PATTERNS * s + pi], dil) for s in range(3))
        qkv.append((q, k, v))
        op, lp = attn_fwd(q, k, v, pi, dil, f"attn_fwd_{pi}")
        outs.append(op.reshape(t, 256))
        lses.append(lp.reshape(t, 256))
    ya, lse = attn_combine(outs, lses)
    y_att = mm_nt(ya, wf["w_proj_attn_t"], "proj_attn")
    m = merge_fwd(u_gate, bg_row, y_ssd, y_att)
    mix = mm_nn(m, wf["w_out"], "out_proj")
    h1, h1b = ln1_fwd(x2, mix, g1, b1)
    r_up, p_act = mm_nt(h1b, wf["w_up_t"], "mlp_up", relu2=True)
    f_dn = mm_nn(p_act, wf["w_down"], "mlp_down")
    dr2, dr2b, dg2, db2, loss8 = ln2_loss(h1, f_dn, g2, b2, tgt)

    full, small = {}, {}
    da = mm_nt(dr2b, wf["w_down"], "d_mlp_act", out_dtype=bf16, relu2_of=r_up)
    full["w_down"] = mm_tn(p_act, dr2b, "dw_down")
    full["w_up_t"] = mm_tn(da, h1b, "dw_up")
    dh1 = mm_nn(da, wf["w_up_t"], "d_h1", acc_in=dr2, acc_scale=ALPHA)
    dr1, dr1b, dg1, db1 = ln1_bwd(dh1, x2, mix, g1)
    dm = mm_nt(dr1b, wf["w_out"], "d_merge")
    full["w_out"] = mm_tn(m, dr1b, "dw_out")
    dys, dya_p, dga, dgb, dba, dbb = merge_bwd(dm, u_gate, bg_row, y_ssd, y_att)
    dyn = mm_nt(wf["w_proj_ssd"], dys, "d_yn")
    full["w_proj_ssd"] = mm_nn(yn, dys, "dw_proj_ssd")
    dya = mm_nn(dya_p, wf["w_proj_attn_t"], "d_ya")
    full["w_proj_attn_t"] = mm_tn(dya_p, ya, "dw_proj_attn")
    if send_early is not None:
        skip_b = skip_b + send_early(full)[0, 0]

    dxf, dbf, dcf, ddtf, daf, dy, du_ssd, dnw, ddx = ssd_bwd_t(xbc_c, dt_t, a_coef_b, dyn, h_f, False, "ssd_bwd_f",
                                                               skip_b=skip_b, tail=(y_scan, u_z, nw_b))
    dxs, dbs, dcs, ddtb, dab = ssd_bwd_t(xbc_c, dt_t, a_coef_b, dy, h_b, True, "ssd_bwd_b", prev=(dxf, dbf, dcf))
    du_ssd, dcw_x, dcb_x = conv_bwd_t(u_xbc, dsilu, dxs, conv_w_b, du_ssd, "conv_bwd_x", 0)
    du_ssd, dcw_b, dcb_b = conv_bwd_t(u_xbc, dsilu, dbs, conv_w_b, du_ssd, "conv_bwd_b", D_INNER)
    du_ssd, dcw_c, dcb_c = conv_bwd_t(u_xbc, dsilu, dcs, conv_w_b, du_ssd, "conv_bwd_c", D_INNER + 512)
    du_ssd, dbias = dt_bwd_t(ddtf, ddtb, u_dt, dt_bias_b, du_ssd)

    delta = attn_delta(dya, ya)
    dqs, dks, dvs = [], [], []
    for pi, (_, dil) in enumerate(DIL_PATTERNS):
        q, k, v = qkv[pi]
        sd, sl_, sdel = strided(dya, dil), strided(lse, dil), strided(delta, dil)
        dqs.append(attn_dq(q, k, v, sd, sl_, sdel, pi, dil, f"attn_dq_{pi}").reshape(t, 256))
        dk, dv = attn_dkv(q, k, v, sd, sl_, sdel, pi, dil, f"attn_dkv_{pi}")
        dks.append(dk.reshape(t, 256))
        dvs.append(dv.reshape(t, 256))
    du_qkv = jnp.concatenate(dqs + dks + dvs, axis=1)
    du_gate = jnp.concatenate([dga, dgb], axis=1)

    full["w_in_t"] = jnp.concatenate(
        [mm_nn(du_ssd, xb, "dw_in_ssd"), mm_tn(du_qkv, xb, "dw_in_qkv"), mm_tn(du_gate, xb, "dw_in_gate")], axis=0)
    lanes = lambda v: jnp.sum(v, axis=-1)
    full["conv_w"] = jnp.concatenate([lanes(dcw_x), lanes(dcw_b), lanes(dcw_c)], axis=1)

    small["b_gate"] = jnp.concatenate([dba, dbb], axis=1)
    small["conv_b"] = jnp.concatenate([lanes(dcb_x), lanes(dcb_b), lanes(dcb_c)])
    dbias = lanes(dbias)
    small["dt_bias_f"], small["dt_bias_b"] = dbias[0:32], dbias[32:64]
    small["a_log_f"] = lanes(daf) * a_f
    small["a_log_b"] = lanes(dab) * a_b
    small["d_skip"] = jnp.sum(lanes(ddx).reshape(SSD_HEADS, SSD_HEAD_DIM), axis=1)
    small["ssd_norm_w"] = lanes(dnw)
    small["ln1_g"], small["ln1_b"], small["ln2_g"], small["ln2_b"] = dg1, db1, dg2, db2

    wt_ssd = wt[0:SSD_COLS]
    if send_late is not None:
        wt_ssd = wt_ssd + send_late(full, small, loss8[0, 0])[0, 0].astype(wt_ssd.dtype)
    dx = mm_tn(du_ssd, wt_ssd, "dx_ssd", acc_in=dr1, acc_scale=ALPHA)
    dx = mm_nn(du_qkv, wt_qkv, "dx_qkv", acc_in=dx)
    dx = mm_nn(du_gate, wt_gate, "dx_gate", acc_in=dx)
    return loss8[0, 0], dx, full, small


def kernel(x, w_in, b_gate, conv_w, conv_b, dt_bias_f, dt_bias_b, a_log_f, a_log_b, d_skip, ssd_norm_w, w_proj_ssd, w_proj_attn, w_out, ln1_g, ln1_b, w_up, w_down, ln2_g, ln2_b, loss_target, m_w_in, m_b_gate, m_conv_w, m_conv_b, m_dt_bias_f, m_dt_bias_b, m_a_log_f, m_a_log_b, m_d_skip, m_ssd_norm_w, m_w_proj_ssd, m_w_proj_attn, m_w_out, m_ln1_g, m_ln1_b, m_w_up, m_w_down, m_ln2_g, m_ln2_b, v_w_in, v_b_gate, v_conv_w, v_conv_b, v_dt_bias_f, v_dt_bias_b, v_a_log_f, v_a_log_b, v_d_skip, v_ssd_norm_w, v_w_proj_ssd, v_w_proj_attn, v_w_out, v_ln1_g, v_ln1_b, v_w_up, v_w_down, v_ln2_g, v_ln2_b):
    given = dict(locals())
    w = {n: given[n] for n in WEIGHTS}
    mom = {n: given["m_" + n] for n in WEIGHTS}
    var = {n: given["v_" + n] for n in WEIGHTS}
    t = x.shape[1]
    wf, late_weights, start_token = _gather_weights(w)
    in_flight = []

    def send_early(full):
        send_sems, recv_sems, parts_thru, land_thru, token = scatter_start(_pack_early_parts(full), "scatter_start")
        in_flight.append((send_sems, recv_sems, parts_thru, land_thru))
        return token

    def send_late(full, small, loss):
        late = _pack_late_parts(full, small, loss)
        rows = scatter_start(late.astype(bf16), "late_start")
        tail = scatter_start(late[:, OFF_TAIL:OFF_TAIL + ROWS_TAIL], "tail_start")
        in_flight.extend([rows[0:4], tail[0:4]])
        return rows[4] + tail[4]

    loss, dx, full, small = _local_step(x.reshape(t, D_MODEL), loss_target.reshape(t, D_MODEL), wf, w, send_early,
                                        late_weights, start_token, send_late)
    x_, y_, c_ = _place()
    me = (4 * x_ + 2 * y_ + c_).astype(jnp.int32).reshape(1)
    wp, mp, vp = _pack_shard(w), _pack_shard(mom), _pack_shard(var)
    early_parts, early_landed = scatter_wait(*in_flight[0], dx, "scatter_wait")
    early_out = adamw_sum8(early_landed, early_parts, me, wp, mp, vp, LATE_ROWS, "adamw_early")
    late_parts, late_landed = scatter_wait(*in_flight[1], dx, "late_wait")
    tail_parts, tail_landed = scatter_wait(*in_flight[2], dx, "tail_wait")
    late_out = adamw_sum8(late_landed, late_parts, me, wp, mp, vp, 0, "adamw_late", tails=(tail_landed, tail_parts))
    g, delta, new_m, new_v = (_unpack_shard(a, b) for a, b in zip(late_out, early_out))
    outs = [g["_extra"], dx.reshape(x.shape)]
    for d in (g, delta, new_m, new_v):
        outs += [d[n].reshape(w[n].shape) for n in WEIGHTS]
    return tuple(outs)
```

```python
import jax
import jax.numpy as jnp
import numpy as np
from jax import lax
from jax.experimental import pallas as pl
from jax.experimental.pallas import tpu as pltpu

f32 = jnp.float32
bf16 = jnp.bfloat16
MXU_DTYPE = jnp.bfloat16

N_DEV = 8
D_MODEL = 1024
D_INNER = 2048
SSD_HEADS = 32
SSD_HEAD_DIM = 64
SSD_GROUPS = 4
D_STATE = 128
D_CONV = 5
CHUNK = 128
CONV_DIM = D_INNER + 2 * SSD_GROUPS * D_STATE
NORM_EPS = 1e-5
ATTN_HEAD_DIM = 64
DIL_PATTERNS = ((128, 1), (512, 4), (2048, 16))
N_PATTERNS = len(DIL_PATTERNS)
HEADS_PER_PATTERN = 4
ATTN_HEADS = 12
ATTN_WIDTH = 768
ATTN_OUT = 256
D_FF = 4096
ALPHA = 2.0 ** 0.25
IN_SPLITS = (D_INNER, CONV_DIM, SSD_HEADS, SSD_HEADS, ATTN_WIDTH, ATTN_WIDTH, ATTN_WIDTH, 2 * D_MODEL)
IN_COLS = sum(IN_SPLITS)
SSD_COLS = sum(IN_SPLITS[0:4])
ADAM_LR, ADAM_B1, ADAM_B2, ADAM_EPS, ADAM_WD, ADAM_STEP = 0.001, 0.9, 0.999, 1e-08, 0.01, 10
NEG_BIG = -1e30
VMEM_LIMIT = 56 * 1024 * 1024
MESH = pl.DeviceIdType.MESH

SMALL = ("b_gate", "conv_b", "dt_bias_f", "dt_bias_b", "a_log_f", "a_log_b", "d_skip", "ssd_norm_w",
         "ln1_g", "ln1_b", "ln2_g", "ln2_b")
WEIGHTS = ("w_in", "b_gate", "conv_w", "conv_b", "dt_bias_f", "dt_bias_b", "a_log_f", "a_log_b", "d_skip",
           "ssd_norm_w", "w_proj_ssd", "w_proj_attn", "w_out", "ln1_g", "ln1_b", "w_up", "w_down", "ln2_g", "ln2_b")
SMALL_SIZES = {"b_gate": 2 * D_MODEL, "conv_b": CONV_DIM, "dt_bias_f": 32, "dt_bias_b": 32, "a_log_f": 32, "a_log_b": 32,
               "d_skip": 32, "ssd_norm_w": D_INNER, "ln1_g": D_MODEL, "ln1_b": D_MODEL, "ln2_g": D_MODEL, "ln2_b": D_MODEL}
IN_SHARD = IN_COLS // N_DEV
OFF_TAIL = 1200
ROWS_TAIL = 16
LATE_ROWS = 1280
ROWS_PS, ROWS_OUT, ROWS_UP, ROWS_DOWN, ROWS_PA = D_INNER // N_DEV, D_MODEL // N_DEV, D_FF // N_DEV, D_FF // N_DEV, 32
OFF_PS = LATE_ROWS
OFF_OUT = OFF_PS + ROWS_PS
OFF_UP = OFF_OUT + ROWS_OUT
OFF_DOWN = OFF_UP + ROWS_UP
OFF_PA = OFF_DOWN + ROWS_DOWN
PACK_ROWS = OFF_PA + ROWS_PA
EARLY_ROWS = PACK_ROWS - LATE_ROWS
EARLY_TILE = 160
CONV_SHARD = D_CONV * CONV_DIM // N_DEV
TAIL_ELEMS = CONV_SHARD + sum(SMALL_SIZES.values()) + 1


def _cparams(sem=None, **kw):
    return pltpu.CompilerParams(dimension_semantics=sem, vmem_limit_bytes=VMEM_LIMIT, **kw)


def _mx(v):
    return v.astype(MXU_DTYPE)


def _dot(a, b):
    return jnp.dot(_mx(a), _mx(b), preferred_element_type=f32)


def _dot_nt(a, b):
    return lax.dot_general(_mx(a), _mx(b), (((1,), (1,)), ((), ())), preferred_element_type=f32)


def _dot_tn(a, b):
    return lax.dot_general(_mx(a), _mx(b), (((0,), (0,)), ((), ())), preferred_element_type=f32)


def _dot_exact(a, b):
    return jnp.dot(a, b, precision=lax.Precision.HIGHEST, preferred_element_type=f32)


def _sigmoid(v):
    return 1.0 / (1.0 + jnp.exp(-v))


def _pick(n, prefs):
    for p in prefs:
        if n % p == 0:
            return p
    return n


MM_TILE = 1024


def mm_nn(a, b, name, out_dtype=f32, acc_in=None, acc_scale=1.0):
    m, k = a.shape
    n = b.shape[1]
    tm = _pick(m, (MM_TILE, 1728, 512, 256, 128, 64))
    tn = _pick(n, (MM_TILE, 512, 256, 128))
    tk = _pick(k, (2048, 1536, 1152, 1024, 768, 512, 256, 128))
    nk = k // tk

    def body(*refs):
        a_ref, b_ref = refs[0:2]
        c_ref = refs[2] if acc_in is not None else None
        o_ref = refs[3] if acc_in is not None else refs[2]

        def finish(r):
            if acc_in is not None:
                r = r + acc_scale * c_ref[...]
            o_ref[...] = r.astype(o_ref.dtype)

        if nk == 1:
            finish(_dot(a_ref[...], b_ref[...]))
            return
        acc_ref = refs[-1]
        kk = pl.program_id(2)

        @pl.when(kk == 0)
        def _():
            acc_ref[...] = jnp.zeros_like(acc_ref)

        acc_ref[...] += _dot(a_ref[...], b_ref[...])

        @pl.when(kk == nk - 1)
        def _():
            finish(acc_ref[...])

    in_specs = [pl.BlockSpec((tm, tk), lambda i, j, kk: (i, kk)), pl.BlockSpec((tk, tn), lambda i, j, kk: (kk, j))]
    args = [a, b]
    if acc_in is not None:
        in_specs.append(pl.BlockSpec((tm, tn), lambda i, j, kk: (i, j)))
        args.append(acc_in)
    return pl.pallas_call(
        body, name=name, grid=(m // tm, n // tn, nk), in_specs=in_specs,
        out_specs=pl.BlockSpec((tm, tn), lambda i, j, kk: (i, j)),
        out_shape=jax.ShapeDtypeStruct((m, n), out_dtype),
        scratch_shapes=[pltpu.VMEM((tm, tn), f32)] if nk > 1 else [],
        compiler_params=_cparams(("parallel", "parallel", "arbitrary")))(*args)


def mm_nt(a, b, name, out_dtype=f32, relu2=None, relu2_of=None):
    m, k = a.shape
    n = b.shape[0]
    tm = _pick(m, (MM_TILE, 512, 256, 128, 64))
    tn = _pick(n, (MM_TILE, 768, 512, 256, 128))

    def body(*refs):
        r = _dot_nt(refs[0][...], refs[1][...])
        if relu2:
            pos = jnp.maximum(r, 0.0)
            refs[2][...] = pos.astype(refs[2].dtype)
            refs[3][...] = (pos * pos).astype(refs[3].dtype)
        elif relu2_of is not None:
            refs[3][...] = (r * (2.0 * refs[2][...].astype(f32))).astype(refs[3].dtype)
        else:
            refs[2][...] = r.astype(refs[2].dtype)

    blk = pl.BlockSpec((tm, tn), lambda i, j: (i, j))
    in_specs = [pl.BlockSpec((tm, k), lambda i, j: (i, 0)), pl.BlockSpec((tn, k), lambda i, j: (j, 0))]
    args = [a, b]
    if relu2_of is not None:
        in_specs.append(blk)
        args.append(relu2_of)
    if relu2:
        out_specs, out_shape = [blk, blk], [jax.ShapeDtypeStruct((m, n), bf16), jax.ShapeDtypeStruct((m, n), bf16)]
    else:
        out_specs, out_shape = blk, jax.ShapeDtypeStruct((m, n), out_dtype)
    return pl.pallas_call(body, name=name, grid=(m // tm, n // tn), in_specs=in_specs, out_specs=out_specs,
                          out_shape=out_shape, compiler_params=_cparams(("parallel", "parallel")))(*args)


def mm_nt_split(a, b, name, width, out_dtype=f32):
    m, k = a.shape
    n = b.shape[0]
    tm = MM_TILE
    parts = n // width

    def body(a_ref, b_ref, *o_refs):
        r = _dot_nt(a_ref[...], b_ref[...])
        for q in range(parts):
            o_refs[q][...] = r[:, width * q:width * (q + 1)].astype(o_refs[q].dtype)

    blk = pl.BlockSpec((tm, width), lambda i: (i, 0))
    return pl.pallas_call(
        body, name=name, grid=(m // tm,),
        in_specs=[pl.BlockSpec((tm, k), lambda i: (i, 0)), pl.BlockSpec((n, k), lambda i: (0, 0))],
        out_specs=[blk] * parts, out_shape=[jax.ShapeDtypeStruct((m, width), out_dtype)] * parts,
        compiler_params=_cparams(("parallel",)))(a, b)


def mm_tn(a, b, name, acc_in=None, acc_scale=1.0):
    k, m = a.shape
    n = b.shape[1]
    tm = _pick(m, (MM_TILE, 768, 512, 256, 128))
    tn = _pick(n, (MM_TILE, 512, 256, 128))
    tk = _pick(k, (2048, 1728, 1024, 768, 512, 256, 128, 64))
    nk = k // tk

    def body(*refs):
        a_ref, b_ref, o_ref = refs[0], refs[1], refs[-1]
        kk = pl.program_id(2)

        @pl.when(kk == 0)
        def _():
            o_ref[...] = jnp.zeros_like(o_ref) if acc_in is None else acc_scale * refs[2][...]

        o_ref[...] += _dot_tn(a_ref[...], b_ref[...])

    in_specs = [pl.BlockSpec((tk, tm), lambda i, j, kk: (kk, i)), pl.BlockSpec((tk, tn), lambda i, j, kk: (kk, j))]
    args = [a, b]
    if acc_in is not None:
        in_specs.append(pl.BlockSpec((tm, tn), lambda i, j, kk: (i, j)))
        args.append(acc_in)
    return pl.pallas_call(
        body, name=name, grid=(m // tm, n // tn, nk), in_specs=in_specs,
        out_specs=pl.BlockSpec((tm, tn), lambda i, j, kk: (i, j)),
        out_shape=jax.ShapeDtypeStruct((m, n), f32),
        compiler_params=_cparams(("parallel", "parallel", "arbitrary")))(*args)


def _lane_col(mat, lane_idx, h):
    return jnp.sum(jnp.where(lane_idx == h, mat, 0.0), axis=1, keepdims=True)


def _slopes(p):
    return [2.0 ** (-8.0 * (HEADS_PER_PATTERN * p + j + 1) / ATTN_HEADS) for j in range(HEADS_PER_PATTERN)]


def _win_specs(nq, col_of):
    return [pl.BlockSpec((64, 256), lambda r, i: (jnp.maximum(2 * i - 1, 0), col_of(r))),
            pl.BlockSpec((128, 256), lambda r, i: (i, col_of(r))),
            pl.BlockSpec((64, 256), lambda r, i: (jnp.minimum(2 * i + 2, 2 * nq - 1), col_of(r)))]


def _lane_head(shape):
    return lax.broadcasted_iota(jnp.int32, shape, 1) >> 6


def _stack_heads(m):
    lane_head = _lane_head(m.shape)
    return jnp.concatenate([jnp.where(lane_head == j, m, 0.0) for j in range(HEADS_PER_PATTERN)], axis=0)


def _unstack_heads(m4, n):
    lane_head = _lane_head((n, 256))
    out = jnp.where(lane_head == 0, m4[0:n], 0.0)
    for j in range(1, HEADS_PER_PATTERN):
        out = out + jnp.where(lane_head == j, m4[j * n:(j + 1) * n], 0.0)
    return out


def _head_cols(m, n):
    lane = lax.broadcasted_iota(jnp.int32, (n, 256), 1)
    return jnp.concatenate([jnp.sum(jnp.where(lane == ATTN_HEAD_DIM * j, m, 0.0), axis=1, keepdims=True)
                            for j in range(HEADS_PER_PATTERN)], axis=0)


def _score_bias(p, dil, by_key):
    slopes = np.asarray(_slopes(p), np.float32)
    if by_key:
        win = np.arange(256)[:, None]
        rel = np.arange(128)[None, :] - (win - 64)
    else:
        win = np.arange(256)[None, :]
        rel = win - 64 - np.arange(128)[:, None]
    band = np.abs(rel) <= 64
    out = []
    for first, last in ((False, False), (True, False), (False, True), (True, True)):
        ok = band & ~(first & (win < 64)) & ~(last & (win >= 192))
        pen = -slopes[:, None, None] * (np.abs(rel) * dil).astype(np.float32)[None]
        out.append(np.where(ok[None], pen, np.float32(NEG_BIG)).reshape(-1, rel.shape[1]))
    return jnp.asarray(np.stack(out), f32)


def _bias_spec(nq, rows, cols):
    return pl.BlockSpec((1, rows, cols), lambda r, i: ((i == 0).astype(jnp.int32) + 2 * (i == nq - 1).astype(jnp.int32), 0, 0))


def attn_fwd(q, k, v, p, dil, name):
    l = q.shape[0]
    nq = l // 128

    def body(q_ref, kp_ref, ko_ref, kn_ref, vp_ref, vo_ref, vn_ref, bias_ref, o_ref, lse_ref):
        kcat = jnp.concatenate([kp_ref[...], ko_ref[...], kn_ref[...]], axis=0)
        vcat = jnp.concatenate([vp_ref[...], vo_ref[...], vn_ref[...]], axis=0)
        s = _dot_nt(_stack_heads(q_ref[...] * 0.125), kcat) + bias_ref[0]
        m = jnp.max(s, axis=1, keepdims=True)
        pr = jnp.exp(s - m)
        den = jnp.sum(pr, axis=1, keepdims=True)
        o4 = _dot(pr, vcat) / den
        o_ref[...] = _unstack_heads(o4, 128)
        lse_ref[...] = _unstack_heads(jnp.broadcast_to(m + jnp.log(den), (512, 256)), 128)

    col = lambda r: r
    return pl.pallas_call(
        body, name=name, grid=(dil, nq),
        in_specs=[pl.BlockSpec((128, 256), lambda r, i: (i, r))] + _win_specs(nq, col) + _win_specs(nq, col)
        + [_bias_spec(nq, 512, 256)],
        out_specs=[pl.BlockSpec((128, 256), lambda r, i: (i, r))] * 2,
        out_shape=[jax.ShapeDtypeStruct(q.shape, f32)] * 2,
        compiler_params=_cparams(("parallel", "parallel")))(q, k, k, k, v, v, v, _score_bias(p, dil, False))


def attn_combine(os_, lses, tb=1024):
    t = os_[0].shape[0]

    def body(o0, o1, o2, l0, l1, l2, y_ref, lse_ref):
        a0, a1, a2 = l0[...], l1[...], l2[...]
        m = jnp.maximum(jnp.maximum(a0, a1), a2)
        e0, e1, e2 = jnp.exp(a0 - m), jnp.exp(a1 - m), jnp.exp(a2 - m)
        den = e0 + e1 + e2
        y_ref[...] = (e0 * o0[...] + e1 * o1[...] + e2 * o2[...]) / den
        lse_ref[...] = m + jnp.log(den)

    blk = pl.BlockSpec((tb, 256), lambda i: (i, 0))
    return pl.pallas_call(
        body, name="attn_combine", grid=(t // tb,), in_specs=[blk] * 6, out_specs=[blk, blk],
        out_shape=[jax.ShapeDtypeStruct((t, 256), f32)] * 2,
        compiler_params=_cparams(("parallel",)))(*os_, *lses)


def attn_delta(dy, y, tb=1024):
    t = dy.shape[0]

    def body(dy_ref, y_ref, d_ref):
        pr = dy_ref[...] * y_ref[...]
        lane_head = _lane_head(pr.shape)
        out = jnp.zeros_like(pr)
        for j in range(HEADS_PER_PATTERN):
            sj = jnp.sum(jnp.where(lane_head == j, pr, 0.0), axis=1, keepdims=True)
            out = out + jnp.where(lane_head == j, sj, 0.0)
        d_ref[...] = out

    blk = pl.BlockSpec((tb, 256), lambda i: (i, 0))
    return pl.pallas_call(body, name="attn_delta", grid=(t // tb,), in_specs=[blk, blk], out_specs=blk,
                          out_shape=jax.ShapeDtypeStruct((t, 256), f32),
                          compiler_params=_cparams(("parallel",)))(dy, y)


def attn_dq(q, k, v, dy, lse, delta, p, dil, name):
    l = q.shape[0]
    nq = l // 128

    def body(q_ref, kp_ref, ko_ref, kn_ref, vp_ref, vo_ref, vn_ref, dy_ref, lse_ref, d_ref, bias_ref, dq_ref):
        kcat = jnp.concatenate([kp_ref[...], ko_ref[...], kn_ref[...]], axis=0)
        vcat = jnp.concatenate([vp_ref[...], vo_ref[...], vn_ref[...]], axis=0)
        s = _dot_nt(_stack_heads(q_ref[...] * 0.125), kcat) + bias_ref[0]
        pr = jnp.exp(s - _head_cols(lse_ref[...], 128))
        dp = _dot_nt(_stack_heads(dy_ref[...]), vcat)
        ds = pr * (dp - _head_cols(d_ref[...], 128))
        dq_ref[...] = (_unstack_heads(_dot(ds, kcat), 128) * 0.125).astype(dq_ref.dtype)

    col = lambda r: r
    own = pl.BlockSpec((128, 256), lambda r, i: (i, r))
    return pl.pallas_call(
        body, name=name, grid=(dil, nq),
        in_specs=[own] + _win_specs(nq, col) + _win_specs(nq, col) + [own, own, own, _bias_spec(nq, 512, 256)],
        out_specs=own, out_shape=jax.ShapeDtypeStruct(q.shape, bf16),
        compiler_params=_cparams(("parallel", "parallel")))(q, k, k, k, v, v, v, dy, lse, delta, _score_bias(p, dil, False))


def attn_dkv(q, k, v, dy, lse, delta, p, dil, name):
    l = q.shape[0]
    nq = l // 128

    def body(qp_ref, qo_ref, qn_ref, gp_ref, go_ref, gn_ref, lp_ref, lo_ref, ln_ref, dp_ref, do_ref, dn_ref,
             k_ref, v_ref, bias_ref, dk_ref, dv_ref):
        cat = lambda a, b, c: jnp.concatenate([a[...], b[...], c[...]], axis=0)
        q4 = _stack_heads(cat(qp_ref, qo_ref, qn_ref) * 0.125)
        dy4 = _stack_heads(cat(gp_ref, go_ref, gn_ref))
        lse4 = _head_cols(cat(lp_ref, lo_ref, ln_ref), 256)
        del4 = _head_cols(cat(dp_ref, do_ref, dn_ref), 256)
        s = _dot_nt(q4, k_ref[...]) + bias_ref[0]
        pr = jnp.exp(s - lse4)
        dpm = _dot_nt(dy4, v_ref[...])
        ds = pr * (dpm - del4)
        dv_ref[...] = _dot_tn(pr, dy4).astype(dv_ref.dtype)
        dk_ref[...] = _dot_tn(ds, q4).astype(dk_ref.dtype)

    col = lambda r: r
    own = pl.BlockSpec((128, 256), lambda r, i: (i, r))
    win = _win_specs(nq, col)
    return pl.pallas_call(
        body, name=name, grid=(dil, nq), in_specs=win * 4 + [own, own, _bias_spec(nq, 1024, 128)], out_specs=[own, own],
        out_shape=[jax.ShapeDtypeStruct(q.shape, bf16)] * 2,
        compiler_params=_cparams(("parallel", "parallel")))(q, q, q, dy, dy, dy, lse, lse, lse, delta, delta, delta, k, v,
                                                            _score_bias(p, dil, True))


def _lanes(v, reps):
    return v if reps == 1 else jnp.tile(v, (1, reps))


def _lane_halo_specs(cb, tb, nt, off=0):
    r = tb // 128
    return [pl.BlockSpec((cb, 128), lambda j, i: (j + off, jnp.maximum(i * r - 1, 0))),
            pl.BlockSpec((cb, tb), lambda j, i: (j + off, i)),
            pl.BlockSpec((cb, 128), lambda j, i: (j + off, jnp.minimum((i + 1) * r, nt * r - 1)))]


def _with_lane_halo(prev_ref, own_ref, next_ref, i, nt):
    prev = jnp.where(i > 0, prev_ref[...].astype(f32), 0.0)
    nxt = jnp.where(i < nt - 1, next_ref[...].astype(f32), 0.0)
    return jnp.concatenate([prev, own_ref[...].astype(f32), nxt], axis=1)


def _lane_shifted(xcat, s, tb):
    n = xcat.shape[1]
    return pltpu.roll(xcat, (-s) % n, 1)[:, 128:128 + tb]


def conv_fwd_t(xbc_t, w_b, b_b, tb=1024, cb=512):
    c, t = xbc_t.shape
    nt = t // tb

    def body(prev_ref, own_ref, next_ref, w_ref, b_ref, o_ref, ds_ref):
        i = pl.program_id(1)
        xcat = _with_lane_halo(prev_ref, own_ref, next_ref, i, nt)
        reps = tb // 128
        pre = _lanes(b_ref[...], reps)
        for k in range(D_CONV):
            pre = pre + _lanes(w_ref[k], reps) * _lane_shifted(xcat, k - 2, tb)
        sg = _sigmoid(pre)
        o_ref[...] = pre * sg
        ds_ref[...] = sg * (1.0 + pre * (1.0 - sg))

    blk = pl.BlockSpec((cb, tb), lambda j, i: (j, i))
    return pl.pallas_call(
        body, name="conv_fwd", grid=(c // cb, nt),
        in_specs=_lane_halo_specs(cb, tb, nt) + [pl.BlockSpec((D_CONV, cb, 128), lambda j, i: (0, j, 0)),
                                                 pl.BlockSpec((cb, 128), lambda j, i: (j, 0))],
        out_specs=[blk, blk], out_shape=[jax.ShapeDtypeStruct((c, t), f32)] * 2,
        compiler_params=_cparams(("parallel", "parallel")))(xbc_t, xbc_t, xbc_t, w_b, b_b)


def conv_bwd_t(xbc_t, dsilu_t, grad_t, w_b, into, name, row0, tb=1024, cb=512):
    c, t = grad_t.shape
    nt = t // tb
    off = row0 // cb
    off_out = (D_INNER + row0) // cb
    reps = tb // 128

    def body(*refs):
        i = pl.program_id(1)
        x_ref, sr, gr = refs[0], refs[1:4], refs[4:7]
        w_ref = refs[7]
        dx_ref, dw_ref, db_ref = refs[-3:]
        wk = [_lanes(w_ref[k], reps) for k in range(D_CONV)]
        dpre = _with_lane_halo(*gr, i, nt) * _with_lane_halo(*sr, i, nt)

        def fold(v):
            s = v[:, 0:128]
            for q in range(1, reps):
                s = s + v[:, 128 * q:128 * (q + 1)]
            return s

        @pl.when(i == 0)
        def _():
            dw_ref[...] = jnp.zeros_like(dw_ref)
            db_ref[...] = jnp.zeros_like(db_ref)

        x_own = x_ref[...]
        dx = None
        for k in range(D_CONV):
            shifted = _lane_shifted(dpre, 2 - k, tb)
            term = wk[k] * shifted
            dx = term if dx is None else dx + term
            dw_ref[k] += fold(shifted * x_own)
        dx_ref[...] = dx.astype(dx_ref.dtype)
        db_ref[...] += fold(dpre[:, 128:128 + tb])

    in_specs = ([pl.BlockSpec((cb, tb), lambda j, i: (j + off, i))] + _lane_halo_specs(cb, tb, nt, off)
                + _lane_halo_specs(cb, tb, nt)
                + [pl.BlockSpec((D_CONV, cb, 128), lambda j, i: (0, j + off, 0)), pl.BlockSpec(memory_space=pl.ANY)])
    args = [xbc_t] + [dsilu_t] * 3 + [grad_t] * 3 + [w_b, into]
    return pl.pallas_call(
        body, name=name, grid=(c // cb, nt), in_specs=in_specs,
        out_specs=[pl.BlockSpec((cb, tb), lambda j, i: (j + off_out, i)),
                   pl.BlockSpec((D_CONV, cb, 128), lambda j, i: (0, j, 0)), pl.BlockSpec((cb, 128), lambda j, i: (j, 0))],
        out_shape=[jax.ShapeDtypeStruct(into.shape, into.dtype), jax.ShapeDtypeStruct((D_CONV, c, 128), f32),
                   jax.ShapeDtypeStruct((c, 128), f32)],
        input_output_aliases={8: 0}, compiler_params=_cparams(("parallel", "arbitrary")))(*args)


def dt_fwd_t(u_dt_t, bias_b, tb=2048):
    r, t = u_dt_t.shape

    def body(u_ref, b_ref, o_ref):
        v = u_ref[...] + _lanes(b_ref[...], tb // 128)
        o_ref[...] = jnp.maximum(v, 0.0) + jnp.log(1.0 + jnp.exp(-jnp.abs(v)))

    return pl.pallas_call(
        body, name="dt_fwd", grid=(t // tb,),
        in_specs=[pl.BlockSpec((r, tb), lambda i: (0, i)), pl.BlockSpec((r, 128), lambda i: (0, 0))],
        out_specs=pl.BlockSpec((r, tb), lambda i: (0, i)), out_shape=jax.ShapeDtypeStruct((r, t), f32),
        compiler_params=_cparams(("parallel",)))(u_dt_t, bias_b)


def dt_bwd_t(ddt_f, ddt_b, u_dt_t, bias_b, into, tb=2048):
    r, t = u_dt_t.shape
    reps = tb // 128
    row_blk = (SSD_COLS - r) // r

    def body(gf_ref, gb_ref, u_ref, b_ref, into_ref, du_ref, db_ref):
        g = jnp.concatenate([gf_ref[...], gb_ref[...]], axis=0)
        du = g * _sigmoid(u_ref[...] + _lanes(b_ref[...], reps))
        du_ref[...] = du.astype(du_ref.dtype)

        @pl.when(pl.program_id(0) == 0)
        def _():
            db_ref[...] = jnp.zeros_like(db_ref)

        s = du[:, 0:128]
        for q in range(1, reps):
            s = s + du[:, 128 * q:128 * (q + 1)]
        db_ref[...] += s

    half = pl.BlockSpec((r // 2, tb), lambda i: (0, i))
    return pl.pallas_call(
        body, name="dt_bwd", grid=(t // tb,),
        in_specs=[half, half, pl.BlockSpec((r, tb), lambda i: (0, i)), pl.BlockSpec((r, 128), lambda i: (0, 0)),
                  pl.BlockSpec(memory_space=pl.ANY)],
        out_specs=[pl.BlockSpec((r, tb), lambda i: (row_blk, i)), pl.BlockSpec((r, 128), lambda i: (0, 0))],
        out_shape=[jax.ShapeDtypeStruct(into.shape, into.dtype), jax.ShapeDtypeStruct((r, 128), f32)],
        input_output_aliases={4: 0}, compiler_params=_cparams(("arbitrary",)))(ddt_f, ddt_b, u_dt_t, bias_b, into)


HEADS_PER_GROUP = SSD_HEADS // SSD_GROUPS


def _group_rows(g, n):
    return pl.ds(pl.multiple_of(g * n, n), n)


def _ssd_decays(dt_blk, a_blk, reverse):
    row = lax.broadcasted_iota(jnp.int32, (CHUNK, CHUNK), 0)
    col = lax.broadcasted_iota(jnp.int32, (CHUNK, CHUNK), 1)
    mask = (row <= col) if reverse else (row >= col)
    tri = mask.astype(f32)
    a8 = dt_blk * a_blk
    a = jnp.concatenate([a8, jnp.zeros((CHUNK - HEADS_PER_GROUP, CHUNK), f32)], axis=0).T
    acs = _dot_exact(tri, a)
    return mask, tri, a8, acs, acs.T, col


def ssd_fwd_t(xbc_ct, dt_t, a_b, reverse, name, prev=None, tail=None):
    t = xbc_ct.shape[1]
    nc = t // CHUNK
    direction = 1 if reverse else 0

    def cidx(c):
        return nc - 1 - c if reverse else c

    def body(*refs):
        x_ref, b_ref, c_ref, dt_ref, a_ref = refs[0:5]
        pos = 5
        prev_ref = None
        if prev is not None:
            prev_ref = refs[pos]
            pos += 1
        if tail is not None:
            z_ref, skip_ref, nw_ref = refs[pos:pos + 3]
            pos += 3
            y_ref, hp_ref, yn_ref, h_scr = refs[pos:pos + 4]
        else:
            y_ref, hp_ref, h_scr = refs[pos:pos + 3]

        @pl.when(pl.program_id(0) == 0)
        def _():
            h_scr[...] = jnp.zeros_like(h_scr)

        def group(g, carry):
            x_v, y_v = x_ref.at[_group_rows(g, 512)], y_ref.at[_group_rows(g, 512)]
            heads = _group_rows(g, HEADS_PER_GROUP)
            hp_v, h_v = hp_ref.at[0, heads], h_scr.at[heads]
            dt_blk = dt_ref[heads, :]
            mask, tri, a8, acs, acs_t, lane = _ssd_decays(dt_blk, a_ref[heads, :], reverse)
            bm = b_ref[_group_rows(g, 128), :].T
            cm = c_ref[_group_rows(g, 128), :].T
            cb = _dot_nt(cm, bm)
            tot = jnp.sum(a8, axis=1, keepdims=True)
            for j in range(HEADS_PER_GROUP):
                rows = slice(SSD_HEAD_DIM * j, SSD_HEAD_DIM * (j + 1))
                col_j = _lane_col(acs, lane, j)
                row_j = acs_t[j:j + 1, :]
                lmat = jnp.where(mask, jnp.exp(jnp.where(mask, col_j - row_j, 0.0)), 0.0)
                xdt = x_v[rows, :] * dt_blk[j:j + 1, :]
                hp = h_v[j]
                hp_v[j] = hp
                y = _dot_nt(xdt, cb * lmat) + _dot_nt(hp, cm) * jnp.exp(row_j)
                if prev_ref is not None:
                    y = y + prev_ref.at[_group_rows(g, 512)][rows, :]
                y_v[rows, :] = y
                tot_j = tot[j:j + 1, :]
                h_v[j] = jnp.exp(tot_j) * hp + _dot(xdt * jnp.exp(tot_j - row_j), bm)
            if tail is not None:
                rows = _group_rows(g, 512)
                zz = z_ref[rows, :]
                yg = (y_v[...] + skip_ref[rows, :] * x_v[...]) * (zz * _sigmoid(zz))
                rstd = lax.rsqrt(jnp.mean(yg * yg, axis=0, keepdims=True) + NORM_EPS)
                yn_ref[rows, :] = (yg * rstd * nw_ref[rows, :]).astype(yn_ref.dtype)
            return carry

        lax.fori_loop(0, SSD_GROUPS, group, 0, unroll=True)

    big = pl.BlockSpec((D_INNER, CHUNK), lambda c: (0, cidx(c)))
    par = pl.BlockSpec((D_INNER, 128), lambda c: (0, 0))
    in_specs = [big, pl.BlockSpec((512, CHUNK), lambda c: (4, cidx(c))), pl.BlockSpec((512, CHUNK), lambda c: (5, cidx(c))),
                pl.BlockSpec((SSD_HEADS, CHUNK), lambda c: (direction, cidx(c))),
                pl.BlockSpec((SSD_HEADS, 128), lambda c: (direction, 0))]
    args = [xbc_ct, xbc_ct, xbc_ct, dt_t, a_b]
    out_specs = [big, pl.BlockSpec((1, SSD_HEADS, SSD_HEAD_DIM, D_STATE), lambda c: (cidx(c), 0, 0, 0))]
    out_shape = [jax.ShapeDtypeStruct((D_INNER, t), f32), jax.ShapeDtypeStruct((nc, SSD_HEADS, SSD_HEAD_DIM, D_STATE), f32)]
    if prev is not None:
        in_specs.append(big)
        args.append(prev)
    if tail is not None:
        in_specs += [big, par, par]
        args += list(tail)
        out_specs.append(big)
        out_shape.append(jax.ShapeDtypeStruct((D_INNER, t), bf16))
    return pl.pallas_call(
        body, name=name, grid=(nc,), in_specs=in_specs, out_specs=out_specs, out_shape=out_shape,
        scratch_shapes=[pltpu.VMEM((SSD_HEADS, SSD_HEAD_DIM, D_STATE), f32)],
        compiler_params=_cparams(("arbitrary",)))(*args)


def ssd_bwd_t(xbc_ct, dt_t, a_b, dy_t, hprev, reverse, name, skip_b=None, prev=None, tail=None):
    t = xbc_ct.shape[1]
    nc = t // CHUNK
    direction = 1 if reverse else 0

    def cidx(c):
        return c if reverse else nc - 1 - c

    def body(*refs):
        x_ref, b_ref, c_ref, dt_ref, a_ref, dy_ref, hp_ref = refs[0:7]
        pos = 7
        skip_ref = None
        if skip_b is not None:
            skip_ref = refs[pos]
            pos += 1
        prev_refs = None
        if prev is not None:
            prev_refs = refs[pos:pos + 3]
            pos += 3
        if tail is not None:
            ys_ref, z_ref, nw_ref = refs[pos:pos + 3]
            pos += 3
        dx_ref, db_ref, dc_ref, ddt_ref, da_ref = refs[pos:pos + 5]
        pos += 5
        if tail is not None:
            dyout_ref, dz_ref, dnw_ref, ddx_ref = refs[pos:pos + 4]
            pos += 4
        dh_scr = refs[pos]

        @pl.when(pl.program_id(0) == 0)
        def _():
            dh_scr[...] = jnp.zeros_like(dh_scr)
            da_ref[...] = jnp.zeros_like(da_ref)
            if tail is not None:
                dnw_ref[...] = jnp.zeros_like(dnw_ref)
                ddx_ref[...] = jnp.zeros_like(ddx_ref)

        def group(g, carry):
            big, st, heads = _group_rows(g, 512), _group_rows(g, 128), _group_rows(g, HEADS_PER_GROUP)
            x_v, dy_v, dx_v = x_ref.at[big], dy_ref.at[big], dx_ref.at[big]
            hp_v, dh_v = hp_ref.at[0, heads], dh_scr.at[heads]
            dy_grp = None
            if tail is not None:
                zz = z_ref[big, :]
                sg = _sigmoid(zz)
                sl = zz * sg
                x_all = x_v[...]
                y = ys_ref[big, :] + skip_ref[big, :] * x_all
                yz = y * sl
                rstd = lax.rsqrt(jnp.mean(yz * yz, axis=0, keepdims=True) + NORM_EPS)
                yhat = yz * rstd
                gy = dy_v[...]
                dyhat = gy * nw_ref[big, :]
                dyz = rstd * (dyhat - yhat * jnp.mean(dyhat * yhat, axis=0, keepdims=True))
                dy_grp = dyz * sl
                dyout_ref[big, :] = dy_grp
                dz_ref[big, :] = (dyz * y * sg * (1.0 + zz * (1.0 - sg))).astype(dz_ref.dtype)
                dnw_ref[big, :] += gy * yhat
                ddx_ref[big, :] += dy_grp * x_all
            dt_blk = dt_ref[heads, :]
            a_blk = a_ref[heads, :]
            mask, tri, a8, acs, acs_t, lane = _ssd_decays(dt_blk, a_blk, reverse)
            sub = lax.broadcasted_iota(jnp.int32, (CHUNK, CHUNK), 0)
            mask_t = (sub >= lane) if reverse else (sub <= lane)
            bm = b_ref[st, :].T
            cm = c_ref[st, :].T
            cb = _dot_nt(cm, bm)
            cb_t = _dot_nt(bm, cm)
            tot = jnp.sum(a8, axis=1, keepdims=True)
            dcb = jnp.zeros((CHUNK, CHUNK), f32)
            dbm = jnp.zeros((CHUNK, D_STATE), f32)
            dcm = jnp.zeros((CHUNK, D_STATE), f32)
            dacs_rows, ddtx_rows = [], []
            for j in range(HEADS_PER_GROUP):
                rows = slice(SSD_HEAD_DIM * j, SSD_HEAD_DIM * (j + 1))
                col_j = _lane_col(acs, lane, j)
                row_j = acs_t[j:j + 1, :]
                dt_j = dt_blk[j:j + 1, :]
                tot_j = tot[j:j + 1, :]
                lmat = jnp.where(mask, jnp.exp(jnp.where(mask, col_j - row_j, 0.0)), 0.0)
                lmat_t = jnp.where(mask_t, jnp.exp(jnp.where(mask_t, row_j - col_j, 0.0)), 0.0)
                x = x_v[rows, :]
                xdt = x * dt_j
                dyh = dy_v[rows, :] if dy_grp is None else dy_grp[rows]
                hp = hp_v[j]
                dhn = dh_v[j]
                ml = _dot_tn(dyh, xdt) * lmat
                w_t = _dot_tn(xdt, dyh) * lmat_t * cb_t
                dcb = dcb + ml
                dacs = jnp.sum(w_t, axis=0, keepdims=True) - jnp.sum(ml * cb, axis=0, keepdims=True)
                ecol = jnp.exp(row_j)
                dec = jnp.exp(tot_j - row_j)
                dye = dyh * ecol
                yoff = _dot_nt(hp, cm) * ecol
                gmat = _dot_nt(dhn, bm)
                dxdt = _dot(dyh, cb * lmat) + dec * gmat
                s_dec = jnp.sum(xdt * gmat, axis=0, keepdims=True) * dec
                dacs = dacs + jnp.sum(dyh * yoff, axis=0, keepdims=True) - s_dec
                dcd = jnp.sum(jnp.sum(dhn * hp, axis=1, keepdims=True), axis=0, keepdims=True)
                dtot = jnp.sum(s_dec, axis=1, keepdims=True) + jnp.exp(tot_j) * dcd
                dacs_rows.append((dacs, dtot))
                ddtx_rows.append(jnp.sum(dxdt * x, axis=0, keepdims=True))
                dcm = dcm + _dot_tn(dye, hp)
                dbm = dbm + _dot_tn(xdt * dec, dhn)
                dxh = dxdt * dt_j
                if skip_ref is not None:
                    dxh = dxh + skip_ref.at[big][rows, :] * dyh
                if prev_refs is not None:
                    dxh = dxh + prev_refs[0].at[big][rows, :]
                dx_v[rows, :] = dxh
                dh_v[j] = jnp.exp(tot_j) * dhn + _dot(dye, cm)
            dcm = dcm + _dot(dcb, bm)
            dbm = dbm + _dot_tn(dcb, cm)
            dbt, dct = dbm.T, dcm.T
            if prev_refs is not None:
                dbt = dbt + prev_refs[1][st, :]
                dct = dct + prev_refs[2][st, :]
            db_ref[st, :] = dbt
            dc_ref[st, :] = dct
            dacs8 = jnp.concatenate([d for d, _ in dacs_rows], axis=0)
            dtot8 = jnp.concatenate([d for _, d in dacs_rows], axis=0)
            da8 = _dot_exact(dacs8, tri) + dtot8
            ddt_ref[heads, :] = da8 * a_blk + jnp.concatenate(ddtx_rows, axis=0)
            da_ref[heads, :] += da8 * dt_blk
            return carry

        lax.fori_loop(0, SSD_GROUPS, group, 0, unroll=True)

    big = pl.BlockSpec((D_INNER, CHUNK), lambda c: (0, cidx(c)))
    st = pl.BlockSpec((512, CHUNK), lambda c: (0, cidx(c)))
    in_specs = [big, pl.BlockSpec((512, CHUNK), lambda c: (4, cidx(c))), pl.BlockSpec((512, CHUNK), lambda c: (5, cidx(c))),
                pl.BlockSpec((SSD_HEADS, CHUNK), lambda c: (direction, cidx(c))),
                pl.BlockSpec((SSD_HEADS, 128), lambda c: (direction, 0)), big,
                pl.BlockSpec((1, SSD_HEADS, SSD_HEAD_DIM, D_STATE), lambda c: (cidx(c), 0, 0, 0))]
    args = [xbc_ct, xbc_ct, xbc_ct, dt_t, a_b, dy_t, hprev]
    if skip_b is not None:
        in_specs.append(pl.BlockSpec((D_INNER, 128), lambda c: (0, 0)))
        args.append(skip_b)
    if prev is not None:
        in_specs += [big, st, st]
        args += list(prev)
    par = pl.BlockSpec((D_INNER, 128), lambda c: (0, 0))
    out_specs = [big, st, st, pl.BlockSpec((SSD_HEADS, CHUNK), lambda c: (0, cidx(c))),
                 pl.BlockSpec((SSD_HEADS, 128), lambda c: (0, 0))]
    out_shape = [jax.ShapeDtypeStruct((D_INNER, t), f32), jax.ShapeDtypeStruct((512, t), f32),
                 jax.ShapeDtypeStruct((512, t), f32), jax.ShapeDtypeStruct((SSD_HEADS, t), f32),
                 jax.ShapeDtypeStruct((SSD_HEADS, 128), f32)]
    if tail is not None:
        in_specs += [big, big, par]
        args += list(tail)
        out_specs += [big, big, par, par]
        out_shape += [jax.ShapeDtypeStruct((D_INNER, t), f32), jax.ShapeDtypeStruct((SSD_COLS, t), bf16),
                      jax.ShapeDtypeStruct((D_INNER, 128), f32), jax.ShapeDtypeStruct((D_INNER, 128), f32)]
    return pl.pallas_call(
        body, name=name, grid=(nc,), in_specs=in_specs, out_specs=out_specs, out_shape=out_shape,
        scratch_shapes=[pltpu.VMEM((SSD_HEADS, SSD_HEAD_DIM, D_STATE), f32)],
        compiler_params=_cparams(("arbitrary",)))(*args)


def merge_fwd(u_gate, bg_row, y_ssd, y_att, tb=512):
    t = y_ssd.shape[0]

    def body(ga_ref, gb_ref, ba_ref, bb_ref, ys_ref, ya_ref, o_ref):
        o_ref[...] = (_sigmoid(ga_ref[...] + ba_ref[...]) * ys_ref[...]
                      + _sigmoid(gb_ref[...] + bb_ref[...]) * ya_ref[...]).astype(o_ref.dtype)

    blk = pl.BlockSpec((tb, 512), lambda i, j: (i, j))
    blk2 = pl.BlockSpec((tb, 512), lambda i, j: (i, 2 + j))
    row = pl.BlockSpec((1, 512), lambda i, j: (0, j))
    row2 = pl.BlockSpec((1, 512), lambda i, j: (0, 2 + j))
    return pl.pallas_call(
        body, name="merge_fwd", grid=(t // tb, 2), in_specs=[blk, blk2, row, row2, blk, blk], out_specs=blk,
        out_shape=jax.ShapeDtypeStruct((t, D_MODEL), bf16),
        compiler_params=_cparams(("parallel", "parallel")))(u_gate, u_gate, bg_row, bg_row, y_ssd, y_att)


def merge_bwd(dm, u_gate, bg_row, y_ssd, y_att, tb=512):
    t = dm.shape[0]

    def body(dm_ref, ga_ref, gb_ref, ba_ref, bb_ref, ys_ref, ya_ref, dys_ref, dya_ref, dga_ref, dgb_ref, dba_ref, dbb_ref):
        d = dm_ref[...]
        sa = _sigmoid(ga_ref[...] + ba_ref[...])
        sb = _sigmoid(gb_ref[...] + bb_ref[...])
        dys_ref[...] = (d * sa).astype(dys_ref.dtype)
        dya_ref[...] = (d * sb).astype(dya_ref.dtype)
        dla = d * ys_ref[...] * sa * (1.0 - sa)
        dlb = d * ya_ref[...] * sb * (1.0 - sb)
        dga_ref[...] = dla.astype(dga_ref.dtype)
        dgb_ref[...] = dlb.astype(dgb_ref.dtype)

        @pl.when(pl.program_id(1) == 0)
        def _():
            dba_ref[...] = jnp.zeros_like(dba_ref)
            dbb_ref[...] = jnp.zeros_like(dbb_ref)

        dba_ref[...] += jnp.sum(dla, axis=0, keepdims=True)
        dbb_ref[...] += jnp.sum(dlb, axis=0, keepdims=True)

    blk = pl.BlockSpec((tb, 512), lambda j, i: (i, j))
    blk2 = pl.BlockSpec((tb, 512), lambda j, i: (i, 2 + j))
    row = pl.BlockSpec((1, 512), lambda j, i: (0, j))
    row2 = pl.BlockSpec((1, 512), lambda j, i: (0, 2 + j))
    act = jax.ShapeDtypeStruct((t, D_MODEL), bf16)
    vec = jax.ShapeDtypeStruct((1, D_MODEL), f32)
    return pl.pallas_call(
        body, name="merge_bwd", grid=(2, t // tb), in_specs=[blk, blk, blk2, row, row2, blk, blk],
        out_specs=[blk, blk, blk, blk, row, row], out_shape=[act, act, act, act, vec, vec],
        compiler_params=_cparams(("parallel", "arbitrary")))(dm, u_gate, u_gate, bg_row, bg_row, y_ssd, y_att)


def _ln_stats(r):
    mu = jnp.mean(r, axis=1, keepdims=True)
    xc = r - mu
    rstd = lax.rsqrt(jnp.mean(xc * xc, axis=1, keepdims=True) + NORM_EPS)
    return xc * rstd, rstd


def _ln_bwd(dy, xhat, rstd, g_row):
    dxh = dy * g_row
    return rstd * (dxh - jnp.mean(dxh, axis=1, keepdims=True) - xhat * jnp.mean(dxh * xhat, axis=1, keepdims=True))


def ln1_fwd(x, mix, g_row, b_row, tb=512):
    t = x.shape[0]

    def body(x_ref, m_ref, g_ref, b_ref, o_ref, ob_ref):
        xhat, _ = _ln_stats(ALPHA * x_ref[...] + m_ref[...])
        h = xhat * g_ref[...] + b_ref[...]
        o_ref[...] = h
        ob_ref[...] = h.astype(ob_ref.dtype)

    blk = pl.BlockSpec((tb, D_MODEL), lambda i: (i, 0))
    row = pl.BlockSpec((1, D_MODEL), lambda i: (0, 0))
    return pl.pallas_call(body, name="ln1_fwd", grid=(t // tb,), in_specs=[blk, blk, row, row], out_specs=[blk, blk],
                          out_shape=[jax.ShapeDtypeStruct((t, D_MODEL), f32), jax.ShapeDtypeStruct((t, D_MODEL), bf16)],
                          compiler_params=_cparams(("parallel",)))(x, mix, g_row, b_row)


def ln1_bwd(dh, x, mix, g_row, tb=512):
    t = x.shape[0]

    def body(dh_ref, x_ref, m_ref, g_ref, dr_ref, drb_ref, dg_ref, db_ref):
        xhat, rstd = _ln_stats(ALPHA * x_ref[...] + m_ref[...])
        dy = dh_ref[...]
        dr = _ln_bwd(dy, xhat, rstd, g_ref[...])
        dr_ref[...] = dr
        drb_ref[...] = dr.astype(drb_ref.dtype)

        @pl.when(pl.program_id(0) == 0)
        def _():
            dg_ref[...] = jnp.zeros_like(dg_ref)
            db_ref[...] = jnp.zeros_like(db_ref)

        dg_ref[...] += jnp.sum(dy * xhat, axis=0, keepdims=True)
        db_ref[...] += jnp.sum(dy, axis=0, keepdims=True)

    blk = pl.BlockSpec((tb, D_MODEL), lambda i: (i, 0))
    row = pl.BlockSpec((1, D_MODEL), lambda i: (0, 0))
    return pl.pallas_call(
        body, name="ln1_bwd", grid=(t // tb,), in_specs=[blk, blk, blk, row], out_specs=[blk, blk, row, row],
        out_shape=[jax.ShapeDtypeStruct((t, D_MODEL), f32), jax.ShapeDtypeStruct((t, D_MODEL), bf16),
                   jax.ShapeDtypeStruct((1, D_MODEL), f32), jax.ShapeDtypeStruct((1, D_MODEL), f32)],
        compiler_params=_cparams(("arbitrary",)))(dh, x, mix, g_row)


def ln2_loss(h1, f, g_row, b_row, target, tb=512):
    t = h1.shape[0]

    def body(h_ref, f_ref, g_ref, b_ref, t_ref, dr_ref, drb_ref, dg_ref, db_ref, loss_ref):
        xhat, rstd = _ln_stats(ALPHA * h_ref[...] + f_ref[...])
        g = g_ref[...]
        err = xhat * g + b_ref[...] - t_ref[...]
        dy = err * (1.0 / D_MODEL)
        dr = _ln_bwd(dy, xhat, rstd, g)
        dr_ref[...] = dr
        drb_ref[...] = dr.astype(drb_ref.dtype)

        @pl.when(pl.program_id(0) == 0)
        def _():
            dg_ref[...] = jnp.zeros_like(dg_ref)
            db_ref[...] = jnp.zeros_like(db_ref)
            loss_ref[...] = jnp.zeros_like(loss_ref)

        dg_ref[...] += jnp.sum(dy * xhat, axis=0, keepdims=True)
        db_ref[...] += jnp.sum(dy, axis=0, keepdims=True)
        part = jnp.sum(jnp.mean(err * err, axis=1, keepdims=True), axis=0, keepdims=True)
        loss_ref[...] += 0.5 * part

    blk = pl.BlockSpec((tb, D_MODEL), lambda i: (i, 0))
    row = pl.BlockSpec((1, D_MODEL), lambda i: (0, 0))
    return pl.pallas_call(
        body, name="ln2_loss", grid=(t // tb,), in_specs=[blk, blk, row, row, blk],
        out_specs=[blk, blk, row, row, pl.BlockSpec((8, 128), lambda i: (0, 0))],
        out_shape=[jax.ShapeDtypeStruct((t, D_MODEL), f32), jax.ShapeDtypeStruct((t, D_MODEL), bf16),
                   jax.ShapeDtypeStruct((1, D_MODEL), f32), jax.ShapeDtypeStruct((1, D_MODEL), f32),
                   jax.ShapeDtypeStruct((8, 128), f32)],
        compiler_params=_cparams(("arbitrary",)))(h1, f, g_row, b_row, target)


def _adamw_update(g, w_ref, m_ref, v_ref, g_ref, d_ref, nm_ref, nv_ref):
    c1 = 1.0 - ADAM_B1 ** ADAM_STEP
    c2 = 1.0 - ADAM_B2 ** ADAM_STEP
    nm = ADAM_B1 * m_ref[...] + (1.0 - ADAM_B1) * g
    nv = ADAM_B2 * v_ref[...] + (1.0 - ADAM_B2) * (g * g)
    g_ref[...] = g
    nm_ref[...] = nm
    nv_ref[...] = nv
    d_ref[...] = -ADAM_LR * ((nm / c1) / (jnp.sqrt(nv / c2) + ADAM_EPS) + ADAM_WD * w_ref[...])


def adamw_sum8(landed, parts, me, w, m, v, row0, name, tails=None):
    rows = landed.shape[1]
    off = row0 // EARLY_TILE
    tail_blk, tail_at = divmod(OFF_TAIL - row0, EARLY_TILE)

    def body(me_ref, *refs):
        src = refs[0:N_DEV]
        own_ref = refs[N_DEV]
        pos = N_DEV + 1
        mine = me_ref[0]

        def sum8(own, slots):
            g = None
            for s in range(N_DEV):
                term = jnp.where(mine == s, own, slots(s)).astype(f32)
                g = term if g is None else g + term
            return g

        g = sum8(own_ref[0], lambda s: src[s][0])
        if tails is not None:
            tl_ref, tm_ref = refs[pos:pos + 2]
            pos += 2
            own_tail = tm_ref[0]
            for s in range(1, N_DEV):
                own_tail = jnp.where(mine == s, tm_ref[s], own_tail)
            gt = sum8(own_tail, lambda s: tl_ref[s])
            with_tail = jnp.concatenate([g[0:tail_at], gt, g[tail_at + ROWS_TAIL:]], axis=0)
            g = jnp.where(pl.program_id(0) == tail_blk, with_tail, g)
        w_ref, m_ref, v_ref = refs[pos:pos + 3]
        _adamw_update(g, w_ref, m_ref, v_ref, *refs[pos + 3:])

    def slot(s):
        return pl.BlockSpec((1, EARLY_TILE, 1024), lambda i, me_ref: (jnp.where(me_ref[0] == s, (s + 1) % N_DEV, s), i, 0))

    shard = pl.BlockSpec((EARLY_TILE, 1024), lambda i, me_ref: (i + off, 0))
    out_blk = pl.BlockSpec((EARLY_TILE, 1024), lambda i, me_ref: (i, 0))
    in_specs = [slot(s) for s in range(N_DEV)] + [pl.BlockSpec((1, EARLY_TILE, 1024), lambda i, me_ref: (me_ref[0], i, 0))]
    args = [landed] * N_DEV + [parts]
    if tails is not None:
        whole = pl.BlockSpec((N_DEV, ROWS_TAIL, 1024), lambda i, me_ref: (0, 0, 0))
        in_specs += [whole, whole]
        args += list(tails)
    grid_spec = pltpu.PrefetchScalarGridSpec(num_scalar_prefetch=1, grid=(rows // EARLY_TILE,),
                                             in_specs=in_specs + [shard, shard, shard], out_specs=[out_blk] * 4)
    out = jax.ShapeDtypeStruct((rows, 1024), f32)
    return pl.pallas_call(body, name=name, grid_spec=grid_spec, out_shape=[out] * 4,
                          compiler_params=_cparams(("parallel",)))(me, *args, w, m, v)


def _place():
    return lax.axis_index("x"), lax.axis_index("y"), lax.axis_index("c")


def all_gather_blocks(shard):
    rows, cols = shard.shape

    def body(x_ref, out_ref, send_sems, recv_sems, local_sem):
        x, y, c = _place()
        me, sibling = (x, y, c), (x, y, 1 - c)
        chips = [(1 - x, y), (x, 1 - y), (1 - x, 1 - y)]

        def slot(px, py, pc):
            return out_ref.at[4 * px + 2 * py + pc]

        def copy(k, block, to, src=None):
            return pltpu.make_async_remote_copy(
                src_ref=slot(*block) if src is None else src, dst_ref=slot(*block), send_sem=send_sems.at[k],
                recv_sem=recv_sems.at[k], device_id=to, device_id_type=MESH)

        mine = pltpu.make_async_copy(x_ref, slot(*me), local_sem)
        mine.start()
        first = [copy(0, me, sibling, src=x_ref)]
        first += [copy(1 + j, me, (*chip, c), src=x_ref) for j, chip in enumerate(chips)]
        for cp in first:
            cp.start()
        passed = [copy(4 + j, (*chip, c), sibling) for j, chip in enumerate(chips)]
        for j, chip in enumerate(chips):
            copy(1 + j, (*chip, c), me).wait_recv()
            passed[j].start()
        copy(0, sibling, me).wait_recv()
        for j, chip in enumerate(chips):
            copy(4 + j, (*chip, 1 - c), me).wait_recv()
        for cp in first + passed:
            cp.wait_send()
        mine.wait()

    return pl.pallas_call(
        body, name="all_gather_blocks", out_shape=jax.ShapeDtypeStruct((N_DEV, rows, cols), shard.dtype),
        in_specs=[pl.BlockSpec(memory_space=pl.ANY)], out_specs=pl.BlockSpec(memory_space=pl.ANY),
        scratch_shapes=[pltpu.SemaphoreType.DMA((7,)), pltpu.SemaphoreType.DMA((7,)), pltpu.SemaphoreType.DMA],
        compiler_params=pltpu.CompilerParams(has_side_effects=True))(shard)


_HBM = pl.BlockSpec(memory_space=pltpu.HBM)
_SEM = pl.BlockSpec(memory_space=pltpu.SEMAPHORE)


def _peer(k):
    x, y, c = _place()
    px, py, pc = (1 - x if k & 4 else x), (1 - y if k & 2 else y), (1 - c if k & 1 else c)
    return (px, py, pc), 4 * px + 2 * py + pc


def scatter_start(parts, name):
    per_device = parts.ndim == 3

    def body(p_ref, land_ref, send_sems, recv_sems, p_thru, land_thru, token):
        x, y, c = _place()
        me = 4 * x + 2 * y + c
        for k in range(1, N_DEV):
            place, idx = _peer(k)
            pltpu.make_async_remote_copy(src_ref=p_ref.at[idx] if per_device else p_ref, dst_ref=land_ref.at[me],
                                         send_sem=send_sems.at[k - 1], recv_sem=recv_sems.at[k - 1], device_id=place,
                                         device_id_type=MESH).start()
        token[...] = jnp.zeros_like(token)

    land_shape = parts.shape if per_device else (N_DEV,) + parts.shape
    landing = lax.empty(land_shape, parts.dtype)
    return pl.pallas_call(
        body, name=name,
        out_shape=(pltpu.SemaphoreType.DMA((N_DEV - 1,)), pltpu.SemaphoreType.DMA((N_DEV - 1,)),
                   pltpu.HBM(parts.shape, parts.dtype), pltpu.HBM(land_shape, parts.dtype),
                   jax.ShapeDtypeStruct((8, 128), f32)),
        in_specs=(_HBM, _HBM), out_specs=(_SEM, _SEM, _HBM, _HBM, pl.BlockSpec(memory_space=pltpu.VMEM)),
        input_output_aliases={0: 2, 1: 3},
        compiler_params=pltpu.CompilerParams(has_side_effects=pltpu.SideEffectType.DATAFLOW_SIDE_EFFECTING),
    )(pltpu.with_memory_space_constraint(parts, pltpu.HBM), pltpu.with_memory_space_constraint(landing, pltpu.HBM))


def scatter_wait(send_sems, recv_sems, parts_thru, land_thru, after, name):
    per_device = parts_thru.ndim == 3

    def body(p_ref, land_ref, send_sems, recv_sems, after_ref, p_out, land_out):
        for k in range(1, N_DEV):
            place, idx = _peer(k)
            copy = pltpu.make_async_remote_copy(src_ref=p_ref.at[idx] if per_device else p_ref, dst_ref=land_ref.at[idx],
                                                send_sem=send_sems.at[k - 1], recv_sem=recv_sems.at[k - 1],
                                                device_id=place, device_id_type=MESH)
            copy.wait_send()
            copy.wait_recv()

    return pl.pallas_call(
        body, name=name,
        out_shape=(pltpu.HBM(parts_thru.shape, parts_thru.dtype), pltpu.HBM(land_thru.shape, land_thru.dtype)),
        in_specs=(_HBM, _HBM, _SEM, _SEM, pl.BlockSpec(memory_space=pl.ANY)), out_specs=(_HBM, _HBM),
        input_output_aliases={0: 0, 1: 1},
        compiler_params=pltpu.CompilerParams(has_side_effects=pltpu.SideEffectType.DATAFLOW_SIDE_EFFECTING),
    )(parts_thru, land_thru, send_sems, recv_sems, after)


def _tail_rows(conv_part, small, extra):
    lead = conv_part.shape[:-1]
    rep = jnp.concatenate([small[n].reshape(-1).astype(f32) for n in SMALL] + [extra.reshape(1).astype(f32)])
    flat = jnp.concatenate([conv_part, jnp.broadcast_to(rep, lead + rep.shape),
                            jnp.zeros(lead + (ROWS_TAIL * 1024 - TAIL_ELEMS,), f32)], axis=-1)
    return flat.reshape(lead + (ROWS_TAIL, 1024))


def _late_rows(w_in_t, tail):
    lead = tail.shape[:-2]
    zeros = lambda r: jnp.zeros(lead + (r, 1024), f32)
    return jnp.concatenate([w_in_t, zeros(OFF_TAIL - IN_SHARD), tail, zeros(LATE_ROWS - OFF_TAIL - ROWS_TAIL)], axis=-2)


def _early_rows(w_ps, w_out, w_up_t, w_down, w_pa_t):
    return jnp.concatenate([w_ps, w_out, w_up_t, w_down, w_pa_t.reshape(w_pa_t.shape[:-2] + (ROWS_PA, 1024))], axis=-2)


def _pack_shard(vals):
    tail = _tail_rows(vals["conv_w"].reshape(-1), vals, jnp.zeros((), f32))
    return jnp.concatenate([_late_rows(vals["w_in"].T, tail),
                            _early_rows(vals["w_proj_ssd"], vals["w_out"], vals["w_up"].T, vals["w_down"],
                                        vals["w_proj_attn"].T)], axis=0)


def _unpack_shard(late, early):
    e = lambda lo, hi: early[lo - LATE_ROWS:hi - LATE_ROWS]
    out = {"w_in": late[0:IN_SHARD].T, "w_proj_ssd": e(OFF_PS, OFF_OUT), "w_out": e(OFF_OUT, OFF_UP),
           "w_up": e(OFF_UP, OFF_DOWN).T, "w_down": e(OFF_DOWN, OFF_PA),
           "w_proj_attn": e(OFF_PA, PACK_ROWS).reshape(D_MODEL // N_DEV, ATTN_OUT).T}
    flat = late[OFF_TAIL:OFF_TAIL + ROWS_TAIL].reshape(-1)
    out["conv_w"] = flat[0:CONV_SHARD].reshape(D_CONV, CONV_DIM // N_DEV)
    off = CONV_SHARD
    for n in SMALL:
        out[n] = flat[off:off + SMALL_SIZES[n]]
        off += SMALL_SIZES[n]
    out["_extra"] = flat[off]
    return out


def _blocks(g):
    return g.reshape(N_DEV, g.shape[0] // N_DEV, g.shape[1])


def _pack_early_parts(full):
    return _early_rows(_blocks(full["w_proj_ssd"]), _blocks(full["w_out"]), _blocks(full["w_up_t"]),
                       _blocks(full["w_down"]), _blocks(full["w_proj_attn_t"]))


def _pack_late_parts(full, small, extra):
    conv = full["conv_w"].reshape(D_CONV, N_DEV, CONV_DIM // N_DEV).transpose(1, 0, 2).reshape(N_DEV, CONV_SHARD)
    return _late_rows(_blocks(full["w_in_t"]), _tail_rows(conv, small, extra))


def _gather_weights(w):
    conv_bits = lax.bitcast_convert_type(w["conv_w"], bf16).reshape(-1)
    conv_rows = jnp.concatenate([conv_bits, jnp.zeros((16 * 1024 - 2 * CONV_SHARD,), bf16)]).reshape(16, 1024)
    packed = _pack_shard(w)
    first = OFF_TAIL + ROWS_TAIL
    got = all_gather_blocks(jnp.concatenate([packed[0:OFF_TAIL].astype(bf16), conv_rows], axis=0))
    got, rest = lax.optimization_barrier((got, packed[first:].astype(bf16)))
    send_sems, recv_sems, rest_thru, land_thru, token = scatter_start(rest, "gather_start")
    conv =lax.bitcast_convert_type(got[:, OFF_TAIL:OFF_TAIL + 4].reshape(N_DEV, 4096)[:, 0:2 * CONV_SHARD]
                                    .reshape(N_DEV, D_CONV, CONV_DIM // N_DEV, 2), f32)
    now = {"w_in_t": got[:, 0:IN_SHARD].reshape(IN_COLS, 1024), "conv_w": conv.transpose(1, 0, 2).reshape(D_CONV, CONV_DIM)}

    def later(after):
        mine, landed = scatter_wait(send_sems, recv_sems, rest_thru, land_thru, after, "gather_wait")
        x, y, c = _place()
        landed = lax.dynamic_update_slice(landed, mine[None], (4 * x + 2 * y + c, 0, 0))
        whole = lambda lo, hi: landed[:, lo - first:hi - first].reshape(N_DEV * (hi - lo), 1024)
        return {"w_proj_ssd": whole(OFF_PS, OFF_OUT), "w_out": whole(OFF_OUT, OFF_UP), "w_up_t": whole(OFF_UP, OFF_DOWN),
                "w_down": whole(OFF_DOWN, OFF_PA),
                "w_proj_attn_t": landed[:, OFF_PA - first:PACK_ROWS - first].reshape(D_MODEL, ATTN_OUT)}

    return now, later, token


def _row(v, width=None):
    v = v.reshape(1, -1).astype(f32)
    return v if width is None else jnp.pad(v, ((0, 0), (0, width - v.shape[1])))


def _local_step(x2, tgt, wf, p, send_early=None, late_weights=None, start_token=None, send_late=None):
    t = x2.shape[0]
    o = np.cumsum((0,) + IN_SPLITS)
    wt = wf["w_in_t"]
    wt_z, wt_xbc, wt_dt = wt[o[0]:o[1]], wt[o[1]:o[2]], wt[o[2]:o[4]]
    wt_qkv, wt_gate = wt[o[4]:o[7]], wt[o[7]:o[8]]

    spread = lambda v: jnp.broadcast_to(v.astype(f32)[..., None], v.shape + (128,))
    conv_w_b, conv_b_b = spread(wf["conv_w"]), spread(p["conv_b"])
    dt_bias_b = spread(jnp.concatenate([p["dt_bias_f"], p["dt_bias_b"]]))
    a_f, a_b = -jnp.exp(p["a_log_f"].astype(f32)), -jnp.exp(p["a_log_b"].astype(f32))
    a_coef_b = spread(jnp.concatenate([a_f, a_b]))
    skip_b = spread(jnp.repeat(p["d_skip"], SSD_HEAD_DIM))
    nw_b, bg_row = spread(p["ssd_norm_w"]), _row(p["b_gate"])
    g1, b1, g2, b2 = _row(p["ln1_g"]), _row(p["ln1_b"]), _row(p["ln2_g"]), _row(p["ln2_b"])

    xb = (x2 if start_token is None else x2 + start_token[0, 0]).astype(MXU_DTYPE)
    u_z = mm_nt(wt_z, xb, "in_z")
    u_xbc = mm_nt(wt_xbc, xb, "in_xbc")
    u_dt = mm_nt(wt_dt, xb, "in_dt")
    u_qkv = mm_nt_split(xb, wt_qkv, "in_qkv", 256, bf16)
    u_gate = mm_nt(xb, wt_gate, "in_gate")
    xbc_c, dsilu = conv_fwd_t(u_xbc, conv_w_b, conv_b_b)
    dt_t = dt_fwd_t(u_dt, dt_bias_b)
    y_f, h_f = ssd_fwd_t(xbc_c, dt_t, a_coef_b, False, "ssd_fwd_f")
    y_scan, h_b, yn = ssd_fwd_t(xbc_c, dt_t, a_coef_b, True, "ssd_fwd_b", prev=y_f, tail=(u_z, skip_b, nw_b))
    if late_weights is not None:
        wf = {**wf, **late_weights(yn)}
    y_ssd = mm_tn(yn, wf["w_proj_ssd"], "proj_ssd")

    def strided(a, dil):
        return a.reshape(t // dil, dil * 256)

    qkv, outs, lses = [], [], []
    for pi, (_, dil) in enumerate(DIL_PATTERNS):
        q, k, v = (strided(u_qkv[N_PATTERNS * s + pi], dil) for s in range(3))
        qkv.append((q, k, v))
        op, lp = attn_fwd(q, k, v, pi, dil, f"attn_fwd_{pi}")
        outs.append(op.reshape(t, 256))
        lses.append(lp.reshape(t, 256))
    ya, lse = attn_combine(outs, lses)
    y_att = mm_nt(ya, wf["w_proj_attn_t"], "proj_attn")
    m = merge_fwd(u_gate, bg_row, y_ssd, y_att)
    mix = mm_nn(m, wf["w_out"], "out_proj")
    h1, h1b = ln1_fwd(x2, mix, g1, b1)
    r_up, p_act = mm_nt(h1b, wf["w_up_t"], "mlp_up", relu2=True)
    f_dn = mm_nn(p_act, wf["w_down"], "mlp_down")
    dr2, dr2b, dg2, db2, loss8 = ln2_loss(h1, f_dn, g2, b2, tgt)

    full, small = {}, {}
    da = mm_nt(dr2b, wf["w_down"], "d_mlp_act", out_dtype=bf16, relu2_of=r_up)
    full["w_down"] = mm_tn(p_act, dr2b, "dw_down")
    full["w_up_t"] = mm_tn(da, h1b, "dw_up")
    dh1 = mm_nn(da, wf["w_up_t"], "d_h1", acc_in=dr2, acc_scale=ALPHA)
    dr1, dr1b, dg1, db1 = ln1_bwd(dh1, x2, mix, g1)
    dm = mm_nt(dr1b, wf["w_out"], "d_merge")
    full["w_out"] = mm_tn(m, dr1b, "dw_out")
    dys, dya_p, dga, dgb, dba, dbb = merge_bwd(dm, u_gate, bg_row, y_ssd, y_att)
    dyn = mm_nt(wf["w_proj_ssd"], dys, "d_yn")
    full["w_proj_ssd"] = mm_nn(yn, dys, "dw_proj_ssd")
    dya = mm_nn(dya_p, wf["w_proj_attn_t"], "d_ya")
    full["w_proj_attn_t"] = mm_tn(dya_p, ya, "dw_proj_attn")
    if send_early is not None:
        skip_b = skip_b + send_early(full)[0, 0]

    dxf, dbf, dcf, ddtf, daf, dy, du_ssd, dnw, ddx = ssd_bwd_t(xbc_c, dt_t, a_coef_b, dyn, h_f, False, "ssd_bwd_f",
                                                               skip_b=skip_b, tail=(y_scan, u_z, nw_b))
    dxs, dbs, dcs, ddtb, dab = ssd_bwd_t(xbc_c, dt_t, a_coef_b, dy, h_b, True, "ssd_bwd_b", prev=(dxf, dbf, dcf))
    du_ssd, dcw_x, dcb_x = conv_bwd_t(u_xbc, dsilu, dxs, conv_w_b, du_ssd, "conv_bwd_x", 0)
    du_ssd, dcw_b, dcb_b = conv_bwd_t(u_xbc, dsilu, dbs, conv_w_b, du_ssd, "conv_bwd_b", D_INNER)
    du_ssd, dcw_c, dcb_c = conv_bwd_t(u_xbc, dsilu, dcs, conv_w_b, du_ssd, "conv_bwd_c", D_INNER + 512)
    du_ssd, dbias = dt_bwd_t(ddtf, ddtb, u_dt, dt_bias_b, du_ssd)

    delta = attn_delta(dya, ya)
    dqs, dks, dvs = [], [], []
    for pi, (_, dil) in enumerate(DIL_PATTERNS):
        q, k, v = qkv[pi]
        sd, sl_, sdel = strided(dya, dil), strided(lse, dil), strided(delta, dil)
        dqs.append(attn_dq(q, k, v, sd, sl_, sdel, pi, dil, f"attn_dq_{pi}").reshape(t, 256))
        dk, dv = attn_dkv(q, k, v, sd, sl_, sdel, pi, dil, f"attn_dkv_{pi}")
        dks.append(dk.reshape(t, 256))
        dvs.append(dv.reshape(t, 256))
    du_qkv = jnp.concatenate(dqs + dks + dvs, axis=1)
    du_gate = jnp.concatenate([dga, dgb], axis=1)

    full["w_in_t"] = jnp.concatenate(
        [mm_nn(du_ssd, xb, "dw_in_ssd"), mm_tn(du_qkv, xb, "dw_in_qkv"), mm_tn(du_gate, xb, "dw_in_gate")], axis=0)
    lanes = lambda v: jnp.sum(v, axis=-1)
    full["conv_w"] = jnp.concatenate([lanes(dcw_x), lanes(dcw_b), lanes(dcw_c)], axis=1)

    small["b_gate"] = jnp.concatenate([dba, dbb], axis=1)
    small["conv_b"] = jnp.concatenate([lanes(dcb_x), lanes(dcb_b), lanes(dcb_c)])
    dbias = lanes(dbias)
    small["dt_bias_f"], small["dt_bias_b"] = dbias[0:32], dbias[32:64]
    small["a_log_f"] = lanes(daf) * a_f
    small["a_log_b"] = lanes(dab) * a_b
    small["d_skip"] = jnp.sum(lanes(ddx).reshape(SSD_HEADS, SSD_HEAD_DIM), axis=1)
    small["ssd_norm_w"] = lanes(dnw)
    small["ln1_g"], small["ln1_b"], small["ln2_g"], small["ln2_b"] = dg1, db1, dg2, db2

    wt_ssd = wt[0:SSD_COLS]
    if send_late is not None:
        wt_ssd = wt_ssd + send_late(full, small, loss8[0, 0])[0, 0].astype(wt_ssd.dtype)
    dx = mm_tn(du_ssd, wt_ssd, "dx_ssd", acc_in=dr1, acc_scale=ALPHA)
    dx = mm_nn(du_qkv, wt_qkv, "dx_qkv", acc_in=dx)
    dx = mm_nn(du_gate, wt_gate, "dx_gate", acc_in=dx)
    return loss8[0, 0], dx, full, small


def kernel(x, w_in, b_gate, conv_w, conv_b, dt_bias_f, dt_bias_b, a_log_f, a_log_b, d_skip, ssd_norm_w, w_proj_ssd, w_proj_attn, w_out, ln1_g, ln1_b, w_up, w_down, ln2_g, ln2_b, loss_target, m_w_in, m_b_gate, m_conv_w, m_conv_b, m_dt_bias_f, m_dt_bias_b, m_a_log_f, m_a_log_b, m_d_skip, m_ssd_norm_w, m_w_proj_ssd, m_w_proj_attn, m_w_out, m_ln1_g, m_ln1_b, m_w_up, m_w_down, m_ln2_g, m_ln2_b, v_w_in, v_b_gate, v_conv_w, v_conv_b, v_dt_bias_f, v_dt_bias_b, v_a_log_f, v_a_log_b, v_d_skip, v_ssd_norm_w, v_w_proj_ssd, v_w_proj_attn, v_w_out, v_ln1_g, v_ln1_b, v_w_up, v_w_down, v_ln2_g, v_ln2_b):
    given = dict(locals())
    w = {n: given[n] for n in WEIGHTS}
    mom = {n: given["m_" + n] for n in WEIGHTS}
    var = {n: given["v_" + n] for n in WEIGHTS}
    t = x.shape[1]
    wf, late_weights, start_token = _gather_weights(w)
    in_flight = []

    def send_early(full):
        send_sems, recv_sems, parts_thru, land_thru, token = scatter_start(_pack_early_parts(full), "scatter_start")
        in_flight.append((send_sems, recv_sems, parts_thru, land_thru))
        return token

    def send_late(full, small, loss):
        late = _pack_late_parts(full, small, loss)
        rows = scatter_start(late.astype(bf16), "late_start")
        tail = scatter_start(late[:, OFF_TAIL:OFF_TAIL + ROWS_TAIL], "tail_start")
        in_flight.extend([rows[0:4], tail[0:4]])
        return rows[4] + tail[4]

    loss, dx, full, small = _local_step(x.reshape(t, D_MODEL), loss_target.reshape(t, D_MODEL), wf, w, send_early,
                                        late_weights, start_token, send_late)
    x_, y_, c_ = _place()
    me = (4 * x_ + 2 * y_ + c_).astype(jnp.int32).reshape(1)
    wp, mp, vp = _pack_shard(w), _pack_shard(mom), _pack_shard(var)
    early_parts, early_landed = scatter_wait(*in_flight[0], dx, "scatter_wait")
    early_out = adamw_sum8(early_landed, early_parts, me, wp, mp, vp, LATE_ROWS, "adamw_early")
    late_parts, late_landed = scatter_wait(*in_flight[1], dx, "late_wait")
    tail_parts, tail_landed = scatter_wait(*in_flight[2], dx, "tail_wait")
    late_out = adamw_sum8(late_landed, late_parts, me, wp, mp, vp, 0, "adamw_late", tails=(tail_landed, tail_parts))
    g, delta, new_m, new_v = (_unpack_shard(a, b) for a, b in zip(late_out, early_out))
    outs = [g["_extra"], dx.reshape(x.shape)]
    for d in (g, delta, new_m, new_v):
        outs += [d[n].reshape(w[n].shape) for n in WEIGHTS]
    return tuple(outs)
```

```python
import jax
import jax.numpy as jnp
import numpy as np
from jax import lax
from jax.experimental import pallas as pl
from jax.experimental.pallas import tpu as pltpu

f32 = jnp.float32
bf16 = jnp.bfloat16
MXU_DTYPE = jnp.bfloat16

N_DEV = 8
D_MODEL = 1024
D_INNER = 2048
SSD_HEADS = 32
SSD_HEAD_DIM = 64
SSD_GROUPS = 4
D_STATE = 128
D_CONV = 5
CHUNK = 128
CONV_DIM = D_INNER + 2 * SSD_GROUPS * D_STATE
NORM_EPS = 1e-5
ATTN_HEAD_DIM = 64
DIL_PATTERNS = ((128, 1), (512, 4), (2048, 16))
N_PATTERNS = len(DIL_PATTERNS)
HEADS_PER_PATTERN = 4
ATTN_HEADS = 12
ATTN_WIDTH = 768
ATTN_OUT = 256
D_FF = 4096
ALPHA = 2.0 ** 0.25
IN_SPLITS = (D_INNER, CONV_DIM, SSD_HEADS, SSD_HEADS, ATTN_WIDTH, ATTN_WIDTH, ATTN_WIDTH, 2 * D_MODEL)
IN_COLS = sum(IN_SPLITS)
SSD_COLS = sum(IN_SPLITS[0:4])
ADAM_LR, ADAM_B1, ADAM_B2, ADAM_EPS, ADAM_WD, ADAM_STEP = 0.001, 0.9, 0.999, 1e-08, 0.01, 10
NEG_BIG = -1e30
VMEM_LIMIT = 56 * 1024 * 1024
MESH = pl.DeviceIdType.MESH

SMALL = ("b_gate", "conv_b", "dt_bias_f", "dt_bias_b", "a_log_f", "a_log_b", "d_skip", "ssd_norm_w",
         "ln1_g", "ln1_b", "ln2_g", "ln2_b")
WEIGHTS = ("w_in", "b_gate", "conv_w", "conv_b", "dt_bias_f", "dt_bias_b", "a_log_f", "a_log_b", "d_skip",
           "ssd_norm_w", "w_proj_ssd", "w_proj_attn", "w_out", "ln1_g", "ln1_b", "w_up", "w_down", "ln2_g", "ln2_b")
SMALL_SIZES = {"b_gate": 2 * D_MODEL, "conv_b": CONV_DIM, "dt_bias_f": 32, "dt_bias_b": 32, "a_log_f": 32, "a_log_b": 32,
               "d_skip": 32, "ssd_norm_w": D_INNER, "ln1_g": D_MODEL, "ln1_b": D_MODEL, "ln2_g": D_MODEL, "ln2_b": D_MODEL}
IN_SHARD = IN_COLS // N_DEV
OFF_TAIL = 1200
ROWS_TAIL = 16
LATE_ROWS = 1280
ROWS_PS, ROWS_OUT, ROWS_UP, ROWS_DOWN, ROWS_PA = D_INNER // N_DEV, D_MODEL // N_DEV, D_FF // N_DEV, D_FF // N_DEV, 32
OFF_PS = LATE_ROWS
OFF_OUT = OFF_PS + ROWS_PS
OFF_UP = OFF_OUT + ROWS_OUT
OFF_DOWN = OFF_UP + ROWS_UP
OFF_PA = OFF_DOWN + ROWS_DOWN
PACK_ROWS = OFF_PA + ROWS_PA
EARLY_ROWS = PACK_ROWS - LATE_ROWS
EARLY_TILE = 160
CONV_SHARD = D_CONV * CONV_DIM // N_DEV
TAIL_ELEMS = CONV_SHARD + sum(SMALL_SIZES.values()) + 1


def _cparams(sem=None, **kw):
    return pltpu.CompilerParams(dimension_semantics=sem, vmem_limit_bytes=VMEM_LIMIT, **kw)


def _mx(v):
    return v.astype(MXU_DTYPE)


def _dot(a, b):
    return jnp.dot(_mx(a), _mx(b), preferred_element_type=f32)


def _dot_nt(a, b):
    return lax.dot_general(_mx(a), _mx(b), (((1,), (1,)), ((), ())), preferred_element_type=f32)


def _dot_tn(a, b):
    return lax.dot_general(_mx(a), _mx(b), (((0,), (0,)), ((), ())), preferred_element_type=f32)


def _dot_exact(a, b):
    return jnp.dot(a, b, precision=lax.Precision.HIGHEST, preferred_element_type=f32)


def _sigmoid(v):
    return 1.0 / (1.0 + jnp.exp(-v))


def _pick(n, prefs):
    for p in prefs:
        if n % p == 0:
            return p
    return n


MM_TILE = 1024


def mm_nn(a, b, name, out_dtype=f32, acc_in=None, acc_scale=1.0):
    m, k = a.shape
    n = b.shape[1]
    tm = _pick(m, (MM_TILE, 1728, 512, 256, 128, 64))
    tn = _pick(n, (MM_TILE, 512, 256, 128))
    tk = _pick(k, (2048, 1536, 1152, 1024, 768, 512, 256, 128))
    nk = k // tk

    def body(*refs):
        a_ref, b_ref = refs[0:2]
        c_ref = refs[2] if acc_in is not None else None
        o_ref = refs[3] if acc_in is not None else refs[2]

        def finish(r):
            if acc_in is not None:
                r = r + acc_scale * c_ref[...]
            o_ref[...] = r.astype(o_ref.dtype)

        if nk == 1:
            finish(_dot(a_ref[...], b_ref[...]))
            return
        acc_ref = refs[-1]
        kk = pl.program_id(2)

        @pl.when(kk == 0)
        def _():
            acc_ref[...] = jnp.zeros_like(acc_ref)

        acc_ref[...] += _dot(a_ref[...], b_ref[...])

        @pl.when(kk == nk - 1)
        def _():
            finish(acc_ref[...])

    in_specs = [pl.BlockSpec((tm, tk), lambda i, j, kk: (i, kk)), pl.BlockSpec((tk, tn), lambda i, j, kk: (kk, j))]
    args = [a, b]
    if acc_in is not None:
        in_specs.append(pl.BlockSpec((tm, tn), lambda i, j, kk: (i, j)))
        args.append(acc_in)
    return pl.pallas_call(
        body, name=name, grid=(m // tm, n // tn, nk), in_specs=in_specs,
        out_specs=pl.BlockSpec((tm, tn), lambda i, j, kk: (i, j)),
        out_shape=jax.ShapeDtypeStruct((m, n), out_dtype),
        scratch_shapes=[pltpu.VMEM((tm, tn), f32)] if nk > 1 else [],
        compiler_params=_cparams(("parallel", "parallel", "arbitrary")))(*args)


def mm_nt(a, b, name, out_dtype=f32, relu2=None, relu2_of=None):
    m, k = a.shape
    n = b.shape[0]
    tm = _pick(m, (MM_TILE, 512, 256, 128, 64))
    tn = _pick(n, (MM_TILE, 768, 512, 256, 128))

    def body(*refs):
        r = _dot_nt(refs[0][...], refs[1][...])
        if relu2:
            pos = jnp.maximum(r, 0.0)
            refs[2][...] = pos.astype(refs[2].dtype)
            refs[3][...] = (pos * pos).astype(refs[3].dtype)
        elif relu2_of is not None:
            refs[3][...] = (r * (2.0 * refs[2][...].astype(f32))).astype(refs[3].dtype)
        else:
            refs[2][...] = r.astype(refs[2].dtype)

    blk = pl.BlockSpec((tm, tn), lambda i, j: (i, j))
    in_specs = [pl.BlockSpec((tm, k), lambda i, j: (i, 0)), pl.BlockSpec((tn, k), lambda i, j: (j, 0))]
    args = [a, b]
    if relu2_of is not None:
        in_specs.append(blk)
        args.append(relu2_of)
    if relu2:
        out_specs, out_shape = [blk, blk], [jax.ShapeDtypeStruct((m, n), bf16), jax.ShapeDtypeStruct((m, n), bf16)]
    else:
        out_specs, out_shape = blk, jax.ShapeDtypeStruct((m, n), out_dtype)
    return pl.pallas_call(body, name=name, grid=(m // tm, n // tn), in_specs=in_specs, out_specs=out_specs,
                          out_shape=out_shape, compiler_params=_cparams(("parallel", "parallel")))(*args)


def mm_nt_split(a, b, name, width, out_dtype=f32):
    m, k = a.shape
    n = b.shape[0]
    tm = MM_TILE
    parts = n // width

    def body(a_ref, b_ref, *o_refs):
        r = _dot_nt(a_ref[...], b_ref[...])
        for q in range(parts):
            o_refs[q][...] = r[:, width * q:width * (q + 1)].astype(o_refs[q].dtype)

    blk = pl.BlockSpec((tm, width), lambda i: (i, 0))
    return pl.pallas_call(
        body, name=name, grid=(m // tm,),
        in_specs=[pl.BlockSpec((tm, k), lambda i: (i, 0)), pl.BlockSpec((n, k), lambda i: (0, 0))],
        out_specs=[blk] * parts, out_shape=[jax.ShapeDtypeStruct((m, width), out_dtype)] * parts,
        compiler_params=_cparams(("parallel",)))(a, b)


def mm_tn(a, b, name, acc_in=None, acc_scale=1.0):
    k, m = a.shape
    n = b.shape[1]
    tm = _pick(m, (MM_TILE, 768, 512, 256, 128))
    tn = _pick(n, (MM_TILE, 512, 256, 128))
    tk = _pick(k, (2048, 1728, 1024, 768, 512, 256, 128, 64))
    nk = k // tk

    def body(*refs):
        a_ref, b_ref, o_ref = refs[0], refs[1], refs[-1]
        kk = pl.program_id(2)

        @pl.when(kk == 0)
        def _():
            o_ref[...] = jnp.zeros_like(o_ref) if acc_in is None else acc_scale * refs[2][...]

        o_ref[...] += _dot_tn(a_ref[...], b_ref[...])

    in_specs = [pl.BlockSpec((tk, tm), lambda i, j, kk: (kk, i)), pl.BlockSpec((tk, tn), lambda i, j, kk: (kk, j))]
    args = [a, b]
    if acc_in is not None:
        in_specs.append(pl.BlockSpec((tm, tn), lambda i, j, kk: (i, j)))
        args.append(acc_in)
    return pl.pallas_call(
        body, name=name, grid=(m // tm, n // tn, nk), in_specs=in_specs,
        out_specs=pl.BlockSpec((tm, tn), lambda i, j, kk: (i, j)),
        out_shape=jax.ShapeDtypeStruct((m, n), f32),
        compiler_params=_cparams(("parallel", "parallel", "arbitrary")))(*args)


def _lane_col(mat, lane_idx, h):
    return jnp.sum(jnp.where(lane_idx == h, mat, 0.0), axis=1, keepdims=True)


def _slopes(p):
    return [2.0 ** (-8.0 * (HEADS_PER_PATTERN * p + j + 1) / ATTN_HEADS) for j in range(HEADS_PER_PATTERN)]


def _win_specs(nq, col_of):
    return [pl.BlockSpec((64, 256), lambda r, i: (jnp.maximum(2 * i - 1, 0), col_of(r))),
            pl.BlockSpec((128, 256), lambda r, i: (i, col_of(r))),
            pl.BlockSpec((64, 256), lambda r, i: (jnp.minimum(2 * i + 2, 2 * nq - 1), col_of(r)))]


def _lane_head(shape):
    return lax.broadcasted_iota(jnp.int32, shape, 1) >> 6


def _stack_heads(m):
    lane_head = _lane_head(m.shape)
    return jnp.concatenate([jnp.where(lane_head == j, m, 0.0) for j in range(HEADS_PER_PATTERN)], axis=0)


def _unstack_heads(m4, n):
    lane_head = _lane_head((n, 256))
    out = jnp.where(lane_head == 0, m4[0:n], 0.0)
    for j in range(1, HEADS_PER_PATTERN):
        out = out + jnp.where(lane_head == j, m4[j * n:(j + 1) * n], 0.0)
    return out


def _head_cols(m, n):
    lane = lax.broadcasted_iota(jnp.int32, (n, 256), 1)
    return jnp.concatenate([jnp.sum(jnp.where(lane == ATTN_HEAD_DIM * j, m, 0.0), axis=1, keepdims=True)
                            for j in range(HEADS_PER_PATTERN)], axis=0)


def _score_bias(p, dil, by_key):
    slopes = np.asarray(_slopes(p), np.float32)
    if by_key:
        win = np.arange(256)[:, None]
        rel = np.arange(128)[None, :] - (win - 64)
    else:
        win = np.arange(256)[None, :]
        rel = win - 64 - np.arange(128)[:, None]
    band = np.abs(rel) <= 64
    out = []
    for first, last in ((False, False), (True, False), (False, True), (True, True)):
        ok = band & ~(first & (win < 64)) & ~(last & (win >= 192))
        pen = -slopes[:, None, None] * (np.abs(rel) * dil).astype(np.float32)[None]
        out.append(np.where(ok[None], pen, np.float32(NEG_BIG)).reshape(-1, rel.shape[1]))
    return jnp.asarray(np.stack(out), f32)


def _bias_spec(nq, rows, cols):
    return pl.BlockSpec((1, rows, cols), lambda r, i: ((i == 0).astype(jnp.int32) + 2 * (i == nq - 1).astype(jnp.int32), 0, 0))


def attn_fwd(q, k, v, p, dil, name):
    l = q.shape[0]
    nq = l // 128

    def body(q_ref, kp_ref, ko_ref, kn_ref, vp_ref, vo_ref, vn_ref, bias_ref, o_ref, lse_ref):
        kcat = jnp.concatenate([kp_ref[...], ko_ref[...], kn_ref[...]], axis=0)
        vcat = jnp.concatenate([vp_ref[...], vo_ref[...], vn_ref[...]], axis=0)
        s = _dot_nt(_stack_heads(q_ref[...] * 0.125), kcat) + bias_ref[0]
        m = jnp.max(s, axis=1, keepdims=True)
        pr = jnp.exp(s - m)
        den = jnp.sum(pr, axis=1, keepdims=True)
        o4 = _dot(pr, vcat) / den
        o_ref[...] = _unstack_heads(o4, 128)
        lse_ref[...] = _unstack_heads(jnp.broadcast_to(m + jnp.log(den), (512, 256)), 128)

    col = lambda r: r
    return pl.pallas_call(
        body, name=name, grid=(dil, nq),
        in_specs=[pl.BlockSpec((128, 256), lambda r, i: (i, r))] + _win_specs(nq, col) + _win_specs(nq, col)
        + [_bias_spec(nq, 512, 256)],
        out_specs=[pl.BlockSpec((128, 256), lambda r, i: (i, r))] * 2,
        out_shape=[jax.ShapeDtypeStruct(q.shape, f32)] * 2,
        compiler_params=_cparams(("parallel", "parallel")))(q, k, k, k, v, v, v, _score_bias(p, dil, False))


def attn_combine(os_, lses, tb=1024):
    t = os_[0].shape[0]

    def body(o0, o1, o2, l0, l1, l2, y_ref, lse_ref):
        a0, a1, a2 = l0[...], l1[...], l2[...]
        m = jnp.maximum(jnp.maximum(a0, a1), a2)
        e0, e1, e2 = jnp.exp(a0 - m), jnp.exp(a1 - m), jnp.exp(a2 - m)
        den = e0 + e1 + e2
        y_ref[...] = (e0 * o0[...] + e1 * o1[...] + e2 * o2[...]) / den
        lse_ref[...] = m + jnp.log(den)

    blk = pl.BlockSpec((tb, 256), lambda i: (i, 0))
    return pl.pallas_call(
        body, name="attn_combine", grid=(t // tb,), in_specs=[blk] * 6, out_specs=[blk, blk],
        out_shape=[jax.ShapeDtypeStruct((t, 256), f32)] * 2,
        compiler_params=_cparams(("parallel",)))(*os_, *lses)


def attn_delta(dy, y, tb=1024):
    t = dy.shape[0]

    def body(dy_ref, y_ref, d_ref):
        pr = dy_ref[...] * y_ref[...]
        lane_head = _lane_head(pr.shape)
        out = jnp.zeros_like(pr)
        for j in range(HEADS_PER_PATTERN):
            sj = jnp.sum(jnp.where(lane_head == j, pr, 0.0), axis=1, keepdims=True)
            out = out + jnp.where(lane_head == j, sj, 0.0)
        d_ref[...] = out

    blk = pl.BlockSpec((tb, 256), lambda i: (i, 0))
    return pl.pallas_call(body, name="attn_delta", grid=(t // tb,), in_specs=[blk, blk], out_specs=blk,
                          out_shape=jax.ShapeDtypeStruct((t, 256), f32),
                          compiler_params=_cparams(("parallel",)))(dy, y)


def attn_dq(q, k, v, dy, lse, delta, p, dil, name):
    l = q.shape[0]
    nq = l // 128

    def body(q_ref, kp_ref, ko_ref, kn_ref, vp_ref, vo_ref, vn_ref, dy_ref, lse_ref, d_ref, bias_ref, dq_ref):
        kcat = jnp.concatenate([kp_ref[...], ko_ref[...], kn_ref[...]], axis=0)
        vcat = jnp.concatenate([vp_ref[...], vo_ref[...], vn_ref[...]], axis=0)
        s = _dot_nt(_stack_heads(q_ref[...] * 0.125), kcat) + bias_ref[0]
        pr = jnp.exp(s - _head_cols(lse_ref[...], 128))
        dp = _dot_nt(_stack_heads(dy_ref[...]), vcat)
        ds = pr * (dp - _head_cols(d_ref[...], 128))
        dq_ref[...] = (_unstack_heads(_dot(ds, kcat), 128) * 0.125).astype(dq_ref.dtype)

    col = lambda r: r
    own = pl.BlockSpec((128, 256), lambda r, i: (i, r))
    return pl.pallas_call(
        body, name=name, grid=(dil, nq),
        in_specs=[own] + _win_specs(nq, col) + _win_specs(nq, col) + [own, own, own, _bias_spec(nq, 512, 256)],
        out_specs=own, out_shape=jax.ShapeDtypeStruct(q.shape, bf16),
        compiler_params=_cparams(("parallel", "parallel")))(q, k, k, k, v, v, v, dy, lse, delta, _score_bias(p, dil, False))


def attn_dkv(q, k, v, dy, lse, delta, p, dil, name):
    l = q.shape[0]
    nq = l // 128

    def body(qp_ref, qo_ref, qn_ref, gp_ref, go_ref, gn_ref, lp_ref, lo_ref, ln_ref, dp_ref, do_ref, dn_ref,
             k_ref, v_ref, bias_ref, dk_ref, dv_ref):
        cat = lambda a, b, c: jnp.concatenate([a[...], b[...], c[...]], axis=0)
        q4 = _stack_heads(cat(qp_ref, qo_ref, qn_ref) * 0.125)
        dy4 = _stack_heads(cat(gp_ref, go_ref, gn_ref))
        lse4 = _head_cols(cat(lp_ref, lo_ref, ln_ref), 256)
        del4 = _head_cols(cat(dp_ref, do_ref, dn_ref), 256)
        s = _dot_nt(q4, k_ref[...]) + bias_ref[0]
        pr = jnp.exp(s - lse4)
        dpm = _dot_nt(dy4, v_ref[...])
        ds = pr * (dpm - del4)
        dv_ref[...] = _dot_tn(pr, dy4).astype(dv_ref.dtype)
        dk_ref[...] = _dot_tn(ds, q4).astype(dk_ref.dtype)

    col = lambda r: r
    own = pl.BlockSpec((128, 256), lambda r, i: (i, r))
    win = _win_specs(nq, col)
    return pl.pallas_call(
        body, name=name, grid=(dil, nq), in_specs=win * 4 + [own, own, _bias_spec(nq, 1024, 128)], out_specs=[own, own],
        out_shape=[jax.ShapeDtypeStruct(q.shape, bf16)] * 2,
        compiler_params=_cparams(("parallel", "parallel")))(q, q, q, dy, dy, dy, lse, lse, lse, delta, delta, delta, k, v,
                                                            _score_bias(p, dil, True))


def _lanes(v, reps):
    return v if reps == 1 else jnp.tile(v, (1, reps))


def _lane_halo_specs(cb, tb, nt, off=0):
    r = tb // 128
    return [pl.BlockSpec((cb, 128), lambda j, i: (j + off, jnp.maximum(i * r - 1, 0))),
            pl.BlockSpec((cb, tb), lambda j, i: (j + off, i)),
            pl.BlockSpec((cb, 128), lambda j, i: (j + off, jnp.minimum((i + 1) * r, nt * r - 1)))]


def _with_lane_halo(prev_ref, own_ref, next_ref, i, nt):
    prev = jnp.where(i > 0, prev_ref[...].astype(f32), 0.0)
    nxt = jnp.where(i < nt - 1, next_ref[...].astype(f32), 0.0)
    return jnp.concatenate([prev, own_ref[...].astype(f32), nxt], axis=1)


def _lane_shifted(xcat, s, tb):
    n = xcat.shape[1]
    return pltpu.roll(xcat, (-s) % n, 1)[:, 128:128 + tb]


def conv_fwd_t(xbc_t, w_b, b_b, tb=2048, cb=256):
    c, t = xbc_t.shape
    nt = t // tb

    def body(prev_ref, own_ref, next_ref, w_ref, b_ref, o_ref, ds_ref):
        i = pl.program_id(1)
        xcat = _with_lane_halo(prev_ref, own_ref, next_ref, i, nt)
        reps = tb // 128
        pre = _lanes(b_ref[...], reps)
        for k in range(D_CONV):
            pre = pre + _lanes(w_ref[k], reps) * _lane_shifted(xcat, k - 2, tb)
        sg = _sigmoid(pre)
        o_ref[...] = pre * sg
        ds_ref[...] = sg * (1.0 + pre * (1.0 - sg))

    blk = pl.BlockSpec((cb, tb), lambda j, i: (j, i))
    return pl.pallas_call(
        body, name="conv_fwd", grid=(c // cb, nt),
        in_specs=_lane_halo_specs(cb, tb, nt) + [pl.BlockSpec((D_CONV, cb, 128), lambda j, i: (0, j, 0)),
                                                 pl.BlockSpec((cb, 128), lambda j, i: (j, 0))],
        out_specs=[blk, blk], out_shape=[jax.ShapeDtypeStruct((c, t), f32)] * 2,
        compiler_params=_cparams(("parallel", "parallel")))(xbc_t, xbc_t, xbc_t, w_b, b_b)


def conv_bwd_t(xbc_t, dsilu_t, grad_t, w_b, into, name, row0, tb=2048, cb=256):
    c, t = grad_t.shape
    nt = t // tb
    off = row0 // cb
    off_out = (D_INNER + row0) // cb
    reps = tb // 128

    def body(*refs):
        i = pl.program_id(1)
        x_ref, sr, gr = refs[0], refs[1:4], refs[4:7]
        w_ref = refs[7]
        dx_ref, dw_ref, db_ref = refs[-3:]
        wk = [_lanes(w_ref[k], reps) for k in range(D_CONV)]
        dpre = _with_lane_halo(*gr, i, nt) * _with_lane_halo(*sr, i, nt)

        def fold(v):
            s = v[:, 0:128]
            for q in range(1, reps):
                s = s + v[:, 128 * q:128 * (q + 1)]
            return s

        @pl.when(i == 0)
        def _():
            dw_ref[...] = jnp.zeros_like(dw_ref)
            db_ref[...] = jnp.zeros_like(db_ref)

        x_own = x_ref[...]
        dx = None
        for k in range(D_CONV):
            shifted = _lane_shifted(dpre, 2 - k, tb)
            term = wk[k] * shifted
            dx = term if dx is None else dx + term
            dw_ref[k] += fold(shifted * x_own)
        dx_ref[...] = dx.astype(dx_ref.dtype)
        db_ref[...] += fold(dpre[:, 128:128 + tb])

    in_specs = ([pl.BlockSpec((cb, tb), lambda j, i: (j + off, i))] + _lane_halo_specs(cb, tb, nt, off)
                + _lane_halo_specs(cb, tb, nt)
                + [pl.BlockSpec((D_CONV, cb, 128), lambda j, i: (0, j + off, 0)), pl.BlockSpec(memory_space=pl.ANY)])
    args = [xbc_t] + [dsilu_t] * 3 + [grad_t] * 3 + [w_b, into]
    return pl.pallas_call(
        body, name=name, grid=(c // cb, nt), in_specs=in_specs,
        out_specs=[pl.BlockSpec((cb, tb), lambda j, i: (j + off_out, i)),
                   pl.BlockSpec((D_CONV, cb, 128), lambda j, i: (0, j, 0)), pl.BlockSpec((cb, 128), lambda j, i: (j, 0))],
        out_shape=[jax.ShapeDtypeStruct(into.shape, into.dtype), jax.ShapeDtypeStruct((D_CONV, c, 128), f32),
                   jax.ShapeDtypeStruct((c, 128), f32)],
        input_output_aliases={8: 0}, compiler_params=_cparams(("parallel", "arbitrary")))(*args)


def dt_fwd_t(u_dt_t, bias_b, tb=2048):
    r, t = u_dt_t.shape

    def body(u_ref, b_ref, o_ref):
        v = u_ref[...] + _lanes(b_ref[...], tb // 128)
        o_ref[...] = jnp.maximum(v, 0.0) + jnp.log(1.0 + jnp.exp(-jnp.abs(v)))

    return pl.pallas_call(
        body, name="dt_fwd", grid=(t // tb,),
        in_specs=[pl.BlockSpec((r, tb), lambda i: (0, i)), pl.BlockSpec((r, 128), lambda i: (0, 0))],
        out_specs=pl.BlockSpec((r, tb), lambda i: (0, i)), out_shape=jax.ShapeDtypeStruct((r, t), f32),
        compiler_params=_cparams(("parallel",)))(u_dt_t, bias_b)


def dt_bwd_t(ddt_f, ddt_b, u_dt_t, bias_b, into, tb=2048):
    r, t = u_dt_t.shape
    reps = tb // 128
    row_blk = (SSD_COLS - r) // r

    def body(gf_ref, gb_ref, u_ref, b_ref, into_ref, du_ref, db_ref):
        g = jnp.concatenate([gf_ref[...], gb_ref[...]], axis=0)
        du = g * _sigmoid(u_ref[...] + _lanes(b_ref[...], reps))
        du_ref[...] = du.astype(du_ref.dtype)

        @pl.when(pl.program_id(0) == 0)
        def _():
            db_ref[...] = jnp.zeros_like(db_ref)

        s = du[:, 0:128]
        for q in range(1, reps):
            s = s + du[:, 128 * q:128 * (q + 1)]
        db_ref[...] += s

    half = pl.BlockSpec((r // 2, tb), lambda i: (0, i))
    return pl.pallas_call(
        body, name="dt_bwd", grid=(t // tb,),
        in_specs=[half, half, pl.BlockSpec((r, tb), lambda i: (0, i)), pl.BlockSpec((r, 128), lambda i: (0, 0)),
                  pl.BlockSpec(memory_space=pl.ANY)],
        out_specs=[pl.BlockSpec((r, tb), lambda i: (row_blk, i)), pl.BlockSpec((r, 128), lambda i: (0, 0))],
        out_shape=[jax.ShapeDtypeStruct(into.shape, into.dtype), jax.ShapeDtypeStruct((r, 128), f32)],
        input_output_aliases={4: 0}, compiler_params=_cparams(("arbitrary",)))(ddt_f, ddt_b, u_dt_t, bias_b, into)


HEADS_PER_GROUP = SSD_HEADS // SSD_GROUPS


def _group_rows(g, n):
    return pl.ds(pl.multiple_of(g * n, n), n)


def _ssd_decays(dt_blk, a_blk, reverse):
    row = lax.broadcasted_iota(jnp.int32, (CHUNK, CHUNK), 0)
    col = lax.broadcasted_iota(jnp.int32, (CHUNK, CHUNK), 1)
    mask = (row <= col) if reverse else (row >= col)
    tri = mask.astype(f32)
    a8 = dt_blk * a_blk
    a = jnp.concatenate([a8, jnp.zeros((CHUNK - HEADS_PER_GROUP, CHUNK), f32)], axis=0).T
    acs = _dot_exact(tri, a)
    return mask, tri, a8, acs, acs.T, col


def ssd_fwd_t(xbc_ct, dt_t, a_b, reverse, name, prev=None, tail=None):
    t = xbc_ct.shape[1]
    nc = t // CHUNK
    direction = 1 if reverse else 0

    def cidx(c):
        return nc - 1 - c if reverse else c

    def body(*refs):
        x_ref, b_ref, c_ref, dt_ref, a_ref = refs[0:5]
        pos = 5
        prev_ref = None
        if prev is not None:
            prev_ref = refs[pos]
            pos += 1
        if tail is not None:
            z_ref, skip_ref, nw_ref = refs[pos:pos + 3]
            pos += 3
            y_ref, hp_ref, yn_ref, h_scr = refs[pos:pos + 4]
        else:
            y_ref, hp_ref, h_scr = refs[pos:pos + 3]

        @pl.when(pl.program_id(0) == 0)
        def _():
            h_scr[...] = jnp.zeros_like(h_scr)

        def group(g, carry):
            x_v, y_v = x_ref.at[_group_rows(g, 512)], y_ref.at[_group_rows(g, 512)]
            heads = _group_rows(g, HEADS_PER_GROUP)
            hp_v, h_v = hp_ref.at[0, heads], h_scr.at[heads]
            dt_blk = dt_ref[heads, :]
            mask, tri, a8, acs, acs_t, lane = _ssd_decays(dt_blk, a_ref[heads, :], reverse)
            bm = b_ref[_group_rows(g, 128), :].T
            cm = c_ref[_group_rows(g, 128), :].T
            cb = _dot_nt(cm, bm)
            tot = jnp.sum(a8, axis=1, keepdims=True)
            for j in range(HEADS_PER_GROUP):
                rows = slice(SSD_HEAD_DIM * j, SSD_HEAD_DIM * (j + 1))
                col_j = _lane_col(acs, lane, j)
                row_j = acs_t[j:j + 1, :]
                lmat = jnp.where(mask, jnp.exp(jnp.where(mask, col_j - row_j, 0.0)), 0.0)
                xdt = x_v[rows, :] * dt_blk[j:j + 1, :]
                hp = h_v[j]
                hp_v[j] = hp
                y = _dot_nt(xdt, cb * lmat) + _dot_nt(hp, cm) * jnp.exp(row_j)
                if prev_ref is not None:
                    y = y + prev_ref.at[_group_rows(g, 512)][rows, :]
                y_v[rows, :] = y
                tot_j = tot[j:j + 1, :]
                h_v[j] = jnp.exp(tot_j) * hp + _dot(xdt * jnp.exp(tot_j - row_j), bm)
            if tail is not None:
                rows = _group_rows(g, 512)
                zz = z_ref[rows, :]
                yg = (y_v[...] + skip_ref[rows, :] * x_v[...]) * (zz * _sigmoid(zz))
                rstd = lax.rsqrt(jnp.mean(yg * yg, axis=0, keepdims=True) + NORM_EPS)
                yn_ref[rows, :] = (yg * rstd * nw_ref[rows, :]).astype(yn_ref.dtype)
            return carry

        lax.fori_loop(0, SSD_GROUPS, group, 0, unroll=True)

    big = pl.BlockSpec((D_INNER, CHUNK), lambda c: (0, cidx(c)))
    par = pl.BlockSpec((D_INNER, 128), lambda c: (0, 0))
    in_specs = [big, pl.BlockSpec((512, CHUNK), lambda c: (4, cidx(c))), pl.BlockSpec((512, CHUNK), lambda c: (5, cidx(c))),
                pl.BlockSpec((SSD_HEADS, CHUNK), lambda c: (direction, cidx(c))),
                pl.BlockSpec((SSD_HEADS, 128), lambda c: (direction, 0))]
    args = [xbc_ct, xbc_ct, xbc_ct, dt_t, a_b]
    out_specs = [big, pl.BlockSpec((1, SSD_HEADS, SSD_HEAD_DIM, D_STATE), lambda c: (cidx(c), 0, 0, 0))]
    out_shape = [jax.ShapeDtypeStruct((D_INNER, t), f32), jax.ShapeDtypeStruct((nc, SSD_HEADS, SSD_HEAD_DIM, D_STATE), f32)]
    if prev is not None:
        in_specs.append(big)
        args.append(prev)
    if tail is not None:
        in_specs += [big, par, par]
        args += list(tail)
        out_specs.append(big)
        out_shape.append(jax.ShapeDtypeStruct((D_INNER, t), bf16))
    return pl.pallas_call(
        body, name=name, grid=(nc,), in_specs=in_specs, out_specs=out_specs, out_shape=out_shape,
        scratch_shapes=[pltpu.VMEM((SSD_HEADS, SSD_HEAD_DIM, D_STATE), f32)],
        compiler_params=_cparams(("arbitrary",)))(*args)


def ssd_bwd_t(xbc_ct, dt_t, a_b, dy_t, hprev, reverse, name, skip_b=None, prev=None, tail=None):
    t = xbc_ct.shape[1]
    nc = t // CHUNK
    direction = 1 if reverse else 0

    def cidx(c):
        return c if reverse else nc - 1 - c

    def body(*refs):
        x_ref, b_ref, c_ref, dt_ref, a_ref, dy_ref, hp_ref = refs[0:7]
        pos = 7
        skip_ref = None
        if skip_b is not None:
            skip_ref = refs[pos]
            pos += 1
        prev_refs = None
        if prev is not None:
            prev_refs = refs[pos:pos + 3]
            pos += 3
        if tail is not None:
            ys_ref, z_ref, nw_ref = refs[pos:pos + 3]
            pos += 3
        dx_ref, db_ref, dc_ref, ddt_ref, da_ref = refs[pos:pos + 5]
        pos += 5
        if tail is not None:
            dyout_ref, dz_ref, dnw_ref, ddx_ref = refs[pos:pos + 4]
            pos += 4
        dh_scr = refs[pos]

        @pl.when(pl.program_id(0) == 0)
        def _():
            dh_scr[...] = jnp.zeros_like(dh_scr)
            da_ref[...] = jnp.zeros_like(da_ref)
            if tail is not None:
                dnw_ref[...] = jnp.zeros_like(dnw_ref)
                ddx_ref[...] = jnp.zeros_like(ddx_ref)

        def group(g, carry):
            big, st, heads = _group_rows(g, 512), _group_rows(g, 128), _group_rows(g, HEADS_PER_GROUP)
            x_v, dy_v, dx_v = x_ref.at[big], dy_ref.at[big], dx_ref.at[big]
            hp_v, dh_v = hp_ref.at[0, heads], dh_scr.at[heads]
            dy_grp = None
            if tail is not None:
                zz = z_ref[big, :]
                sg = _sigmoid(zz)
                sl = zz * sg
                x_all = x_v[...]
                y = ys_ref[big, :] + skip_ref[big, :] * x_all
                yz = y * sl
                rstd = lax.rsqrt(jnp.mean(yz * yz, axis=0, keepdims=True) + NORM_EPS)
                yhat = yz * rstd
                gy = dy_v[...]
                dyhat = gy * nw_ref[big, :]
                dyz = rstd * (dyhat - yhat * jnp.mean(dyhat * yhat, axis=0, keepdims=True))
                dy_grp = dyz * sl
                dyout_ref[big, :] = dy_grp
                dz_ref[big, :] = (dyz * y * sg * (1.0 + zz * (1.0 - sg))).astype(dz_ref.dtype)
                dnw_ref[big, :] += gy * yhat
                ddx_ref[big, :] += dy_grp * x_all
            dt_blk = dt_ref[heads, :]
            a_blk = a_ref[heads, :]
            mask, tri, a8, acs, acs_t, lane = _ssd_decays(dt_blk, a_blk, reverse)
            sub = lax.broadcasted_iota(jnp.int32, (CHUNK, CHUNK), 0)
            mask_t = (sub >= lane) if reverse else (sub <= lane)
            bm = b_ref[st, :].T
            cm = c_ref[st, :].T
            cb = _dot_nt(cm, bm)
            cb_t = _dot_nt(bm, cm)
            tot = jnp.sum(a8, axis=1, keepdims=True)
            dcb = jnp.zeros((CHUNK, CHUNK), f32)
            dbm = jnp.zeros((CHUNK, D_STATE), f32)
            dcm = jnp.zeros((CHUNK, D_STATE), f32)
            dacs_rows, ddtx_rows = [], []
            for j in range(HEADS_PER_GROUP):
                rows = slice(SSD_HEAD_DIM * j, SSD_HEAD_DIM * (j + 1))
                col_j = _lane_col(acs, lane, j)
                row_j = acs_t[j:j + 1, :]
                dt_j = dt_blk[j:j + 1, :]
                tot_j = tot[j:j + 1, :]
                lmat = jnp.where(mask, jnp.exp(jnp.where(mask, col_j - row_j, 0.0)), 0.0)
                lmat_t = jnp.where(mask_t, jnp.exp(jnp.where(mask_t, row_j - col_j, 0.0)), 0.0)
                x = x_v[rows, :]
                xdt = x * dt_j
                dyh = dy_v[rows, :] if dy_grp is None else dy_grp[rows]
                hp = hp_v[j]
                dhn = dh_v[j]
                ml = _dot_tn(dyh, xdt) * lmat
                w_t = _dot_tn(xdt, dyh) * lmat_t * cb_t
                dcb = dcb + ml
                dacs = jnp.sum(w_t, axis=0, keepdims=True) - jnp.sum(ml * cb, axis=0, keepdims=True)
                ecol = jnp.exp(row_j)
                dec = jnp.exp(tot_j - row_j)
                dye = dyh * ecol
                yoff = _dot_nt(hp, cm) * ecol
                gmat = _dot_nt(dhn, bm)
                dxdt = _dot(dyh, cb * lmat) + dec * gmat
                s_dec = jnp.sum(xdt * gmat, axis=0, keepdims=True) * dec
                dacs = dacs + jnp.sum(dyh * yoff, axis=0, keepdims=True) - s_dec
                dcd = jnp.sum(jnp.sum(dhn * hp, axis=1, keepdims=True), axis=0, keepdims=True)
                dtot = jnp.sum(s_dec, axis=1, keepdims=True) + jnp.exp(tot_j) * dcd
                dacs_rows.append((dacs, dtot))
                ddtx_rows.append(jnp.sum(dxdt * x, axis=0, keepdims=True))
                dcm = dcm + _dot_tn(dye, hp)
                dbm = dbm + _dot_tn(xdt * dec, dhn)
                dxh = dxdt * dt_j
                if skip_ref is not None:
                    dxh = dxh + skip_ref.at[big][rows, :] * dyh
                if prev_refs is not None:
                    dxh = dxh + prev_refs[0].at[big][rows, :]
                dx_v[rows, :] = dxh
                dh_v[j] = jnp.exp(tot_j) * dhn + _dot(dye, cm)
            dcm = dcm + _dot(dcb, bm)
            dbm = dbm + _dot_tn(dcb, cm)
            dbt, dct = dbm.T, dcm.T
            if prev_refs is not None:
                dbt = dbt + prev_refs[1][st, :]
                dct = dct + prev_refs[2][st, :]
            db_ref[st, :] = dbt
            dc_ref[st, :] = dct
            dacs8 = jnp.concatenate([d for d, _ in dacs_rows], axis=0)
            dtot8 = jnp.concatenate([d for _, d in dacs_rows], axis=0)
            da8 = _dot_exact(dacs8, tri) + dtot8
            ddt_ref[heads, :] = da8 * a_blk + jnp.concatenate(ddtx_rows, axis=0)
            da_ref[heads, :] += da8 * dt_blk
            return carry

        lax.fori_loop(0, SSD_GROUPS, group, 0, unroll=True)

    big = pl.BlockSpec((D_INNER, CHUNK), lambda c: (0, cidx(c)))
    st = pl.BlockSpec((512, CHUNK), lambda c: (0, cidx(c)))
    in_specs = [big, pl.BlockSpec((512, CHUNK), lambda c: (4, cidx(c))), pl.BlockSpec((512, CHUNK), lambda c: (5, cidx(c))),
                pl.BlockSpec((SSD_HEADS, CHUNK), lambda c: (direction, cidx(c))),
                pl.BlockSpec((SSD_HEADS, 128), lambda c: (direction, 0)), big,
                pl.BlockSpec((1, SSD_HEADS, SSD_HEAD_DIM, D_STATE), lambda c: (cidx(c), 0, 0, 0))]
    args = [xbc_ct, xbc_ct, xbc_ct, dt_t, a_b, dy_t, hprev]
    if skip_b is not None:
        in_specs.append(pl.BlockSpec((D_INNER, 128), lambda c: (0, 0)))
        args.append(skip_b)
    if prev is not None:
        in_specs += [big, st, st]
        args += list(prev)
    par = pl.BlockSpec((D_INNER, 128), lambda c: (0, 0))
    out_specs = [big, st, st, pl.BlockSpec((SSD_HEADS, CHUNK), lambda c: (0, cidx(c))),
                 pl.BlockSpec((SSD_HEADS, 128), lambda c: (0, 0))]
    out_shape = [jax.ShapeDtypeStruct((D_INNER, t), f32), jax.ShapeDtypeStruct((512, t), f32),
                 jax.ShapeDtypeStruct((512, t), f32), jax.ShapeDtypeStruct((SSD_HEADS, t), f32),
                 jax.ShapeDtypeStruct((SSD_HEADS, 128), f32)]
    if tail is not None:
        in_specs += [big, big, par]
        args += list(tail)
        out_specs += [big, big, par, par]
        out_shape += [jax.ShapeDtypeStruct((D_INNER, t), f32), jax.ShapeDtypeStruct((SSD_COLS, t), bf16),
                      jax.ShapeDtypeStruct((D_INNER, 128), f32), jax.ShapeDtypeStruct((D_INNER, 128), f32)]
    return pl.pallas_call(
        body, name=name, grid=(nc,), in_specs=in_specs, out_specs=out_specs, out_shape=out_shape,
        scratch_shapes=[pltpu.VMEM((SSD_HEADS, SSD_HEAD_DIM, D_STATE), f32)],
        compiler_params=_cparams(("arbitrary",)))(*args)


def merge_fwd(u_gate, bg_row, y_ssd, y_att, tb=512):
    t = y_ssd.shape[0]

    def body(ga_ref, gb_ref, ba_ref, bb_ref, ys_ref, ya_ref, o_ref):
        o_ref[...] = (_sigmoid(ga_ref[...] + ba_ref[...]) * ys_ref[...]
                      + _sigmoid(gb_ref[...] + bb_ref[...]) * ya_ref[...]).astype(o_ref.dtype)

    blk = pl.BlockSpec((tb, 512), lambda i, j: (i, j))
    blk2 = pl.BlockSpec((tb, 512), lambda i, j: (i, 2 + j))
    row = pl.BlockSpec((1, 512), lambda i, j: (0, j))
    row2 = pl.BlockSpec((1, 512), lambda i, j: (0, 2 + j))
    return pl.pallas_call(
        body, name="merge_fwd", grid=(t // tb, 2), in_specs=[blk, blk2, row, row2, blk, blk], out_specs=blk,
        out_shape=jax.ShapeDtypeStruct((t, D_MODEL), bf16),
        compiler_params=_cparams(("parallel", "parallel")))(u_gate, u_gate, bg_row, bg_row, y_ssd, y_att)


def merge_bwd(dm, u_gate, bg_row, y_ssd, y_att, tb=512):
    t = dm.shape[0]

    def body(dm_ref, ga_ref, gb_ref, ba_ref, bb_ref, ys_ref, ya_ref, dys_ref, dya_ref, dga_ref, dgb_ref, dba_ref, dbb_ref):
        d = dm_ref[...]
        sa = _sigmoid(ga_ref[...] + ba_ref[...])
        sb = _sigmoid(gb_ref[...] + bb_ref[...])
        dys_ref[...] = (d * sa).astype(dys_ref.dtype)
        dya_ref[...] = (d * sb).astype(dya_ref.dtype)
        dla = d * ys_ref[...] * sa * (1.0 - sa)
        dlb = d * ya_ref[...] * sb * (1.0 - sb)
        dga_ref[...] = dla.astype(dga_ref.dtype)
        dgb_ref[...] = dlb.astype(dgb_ref.dtype)

        @pl.when(pl.program_id(1) == 0)
        def _():
            dba_ref[...] = jnp.zeros_like(dba_ref)
            dbb_ref[...] = jnp.zeros_like(dbb_ref)

        dba_ref[...] += jnp.sum(dla, axis=0, keepdims=True)
        dbb_ref[...] += jnp.sum(dlb, axis=0, keepdims=True)

    blk = pl.BlockSpec((tb, 512), lambda j, i: (i, j))
    blk2 = pl.BlockSpec((tb, 512), lambda j, i: (i, 2 + j))
    row = pl.BlockSpec((1, 512), lambda j, i: (0, j))
    row2 = pl.BlockSpec((1, 512), lambda j, i: (0, 2 + j))
    act = jax.ShapeDtypeStruct((t, D_MODEL), bf16)
    vec = jax.ShapeDtypeStruct((1, D_MODEL), f32)
    return pl.pallas_call(
        body, name="merge_bwd", grid=(2, t // tb), in_specs=[blk, blk, blk2, row, row2, blk, blk],
        out_specs=[blk, blk, blk, blk, row, row], out_shape=[act, act, act, act, vec, vec],
        compiler_params=_cparams(("parallel", "arbitrary")))(dm, u_gate, u_gate, bg_row, bg_row, y_ssd, y_att)


def _ln_stats(r):
    mu = jnp.mean(r, axis=1, keepdims=True)
    xc = r - mu
    rstd = lax.rsqrt(jnp.mean(xc * xc, axis=1, keepdims=True) + NORM_EPS)
    return xc * rstd, rstd


def _ln_bwd(dy, xhat, rstd, g_row):
    dxh = dy * g_row
    return rstd * (dxh - jnp.mean(dxh, axis=1, keepdims=True) - xhat * jnp.mean(dxh * xhat, axis=1, keepdims=True))


def ln1_fwd(x, mix, g_row, b_row, tb=512):
    t = x.shape[0]

    def body(x_ref, m_ref, g_ref, b_ref, o_ref, ob_ref):
        xhat, _ = _ln_stats(ALPHA * x_ref[...] + m_ref[...])
        h = xhat * g_ref[...] + b_ref[...]
        o_ref[...] = h
        ob_ref[...] = h.astype(ob_ref.dtype)

    blk = pl.BlockSpec((tb, D_MODEL), lambda i: (i, 0))
    row = pl.BlockSpec((1, D_MODEL), lambda i: (0, 0))
    return pl.pallas_call(body, name="ln1_fwd", grid=(t // tb,), in_specs=[blk, blk, row, row], out_specs=[blk, blk],
                          out_shape=[jax.ShapeDtypeStruct((t, D_MODEL), f32), jax.ShapeDtypeStruct((t, D_MODEL), bf16)],
                          compiler_params=_cparams(("parallel",)))(x, mix, g_row, b_row)


def ln1_bwd(dh, x, mix, g_row, tb=512):
    t = x.shape[0]

    def body(dh_ref, x_ref, m_ref, g_ref, dr_ref, drb_ref, dg_ref, db_ref):
        xhat, rstd = _ln_stats(ALPHA * x_ref[...] + m_ref[...])
        dy = dh_ref[...]
        dr = _ln_bwd(dy, xhat, rstd, g_ref[...])
        dr_ref[...] = dr
        drb_ref[...] = dr.astype(drb_ref.dtype)

        @pl.when(pl.program_id(0) == 0)
        def _():
            dg_ref[...] = jnp.zeros_like(dg_ref)
            db_ref[...] = jnp.zeros_like(db_ref)

        dg_ref[...] += jnp.sum(dy * xhat, axis=0, keepdims=True)
        db_ref[...] += jnp.sum(dy, axis=0, keepdims=True)

    blk = pl.BlockSpec((tb, D_MODEL), lambda i: (i, 0))
    row = pl.BlockSpec((1, D_MODEL), lambda i: (0, 0))
    return pl.pallas_call(
        body, name="ln1_bwd", grid=(t // tb,), in_specs=[blk, blk, blk, row], out_specs=[blk, blk, row, row],
        out_shape=[jax.ShapeDtypeStruct((t, D_MODEL), f32), jax.ShapeDtypeStruct((t, D_MODEL), bf16),
                   jax.ShapeDtypeStruct((1, D_MODEL), f32), jax.ShapeDtypeStruct((1, D_MODEL), f32)],
        compiler_params=_cparams(("arbitrary",)))(dh, x, mix, g_row)


def ln2_loss(h1, f, g_row, b_row, target, tb=512):
    t = h1.shape[0]

    def body(h_ref, f_ref, g_ref, b_ref, t_ref, dr_ref, drb_ref, dg_ref, db_ref, loss_ref):
        xhat, rstd = _ln_stats(ALPHA * h_ref[...] + f_ref[...])
        g = g_ref[...]
        err = xhat * g + b_ref[...] - t_ref[...]
        dy = err * (1.0 / D_MODEL)
        dr = _ln_bwd(dy, xhat, rstd, g)
        dr_ref[...] = dr
        drb_ref[...] = dr.astype(drb_ref.dtype)

        @pl.when(pl.program_id(0) == 0)
        def _():
            dg_ref[...] = jnp.zeros_like(dg_ref)
            db_ref[...] = jnp.zeros_like(db_ref)
            loss_ref[...] = jnp.zeros_like(loss_ref)

        dg_ref[...] += jnp.sum(dy * xhat, axis=0, keepdims=True)
        db_ref[...] += jnp.sum(dy, axis=0, keepdims=True)
        part = jnp.sum(jnp.mean(err * err, axis=1, keepdims=True), axis=0, keepdims=True)
        loss_ref[...] += 0.5 * part

    blk = pl.BlockSpec((tb, D_MODEL), lambda i: (i, 0))
    row = pl.BlockSpec((1, D_MODEL), lambda i: (0, 0))
    return pl.pallas_call(
        body, name="ln2_loss", grid=(t // tb,), in_specs=[blk, blk, row, row, blk],
        out_specs=[blk, blk, row, row, pl.BlockSpec((8, 128), lambda i: (0, 0))],
        out_shape=[jax.ShapeDtypeStruct((t, D_MODEL), f32), jax.ShapeDtypeStruct((t, D_MODEL), bf16),
                   jax.ShapeDtypeStruct((1, D_MODEL), f32), jax.ShapeDtypeStruct((1, D_MODEL), f32),
                   jax.ShapeDtypeStruct((8, 128), f32)],
        compiler_params=_cparams(("arbitrary",)))(h1, f, g_row, b_row, target)


def _adamw_update(g, w_ref, m_ref, v_ref, g_ref, d_ref, nm_ref, nv_ref):
    c1 = 1.0 - ADAM_B1 ** ADAM_STEP
    c2 = 1.0 - ADAM_B2 ** ADAM_STEP
    nm = ADAM_B1 * m_ref[...] + (1.0 - ADAM_B1) * g
    nv = ADAM_B2 * v_ref[...] + (1.0 - ADAM_B2) * (g * g)
    g_ref[...] = g
    nm_ref[...] = nm
    nv_ref[...] = nv
    d_ref[...] = -ADAM_LR * ((nm / c1) / (jnp.sqrt(nv / c2) + ADAM_EPS) + ADAM_WD * w_ref[...])


def adamw_sum8(landed, parts, me, w, m, v, row0, name, tails=None):
    rows = landed.shape[1]
    off = row0 // EARLY_TILE
    tail_blk, tail_at = divmod(OFF_TAIL - row0, EARLY_TILE)

    def body(me_ref, *refs):
        src = refs[0:N_DEV]
        own_ref = refs[N_DEV]
        pos = N_DEV + 1
        mine = me_ref[0]

        def sum8(own, slots):
            g = None
            for s in range(N_DEV):
                term = jnp.where(mine == s, own, slots(s)).astype(f32)
                g = term if g is None else g + term
            return g

        g = sum8(own_ref[0], lambda s: src[s][0])
        if tails is not None:
            tl_ref, tm_ref = refs[pos:pos + 2]
            pos += 2
            own_tail = tm_ref[0]
            for s in range(1, N_DEV):
                own_tail = jnp.where(mine == s, tm_ref[s], own_tail)
            gt = sum8(own_tail, lambda s: tl_ref[s])
            with_tail = jnp.concatenate([g[0:tail_at], gt, g[tail_at + ROWS_TAIL:]], axis=0)
            g = jnp.where(pl.program_id(0) == tail_blk, with_tail, g)
        w_ref, m_ref, v_ref = refs[pos:pos + 3]
        _adamw_update(g, w_ref, m_ref, v_ref, *refs[pos + 3:])

    def slot(s):
        return pl.BlockSpec((1, EARLY_TILE, 1024), lambda i, me_ref: (jnp.where(me_ref[0] == s, (s + 1) % N_DEV, s), i, 0))

    shard = pl.BlockSpec((EARLY_TILE, 1024), lambda i, me_ref: (i + off, 0))
    out_blk = pl.BlockSpec((EARLY_TILE, 1024), lambda i, me_ref: (i, 0))
    in_specs = [slot(s) for s in range(N_DEV)] + [pl.BlockSpec((1, EARLY_TILE, 1024), lambda i, me_ref: (me_ref[0], i, 0))]
    args = [landed] * N_DEV + [parts]
    if tails is not None:
        whole = pl.BlockSpec((N_DEV, ROWS_TAIL, 1024), lambda i, me_ref: (0, 0, 0))
        in_specs += [whole, whole]
        args += list(tails)
    grid_spec = pltpu.PrefetchScalarGridSpec(num_scalar_prefetch=1, grid=(rows // EARLY_TILE,),
                                             in_specs=in_specs + [shard, shard, shard], out_specs=[out_blk] * 4)
    out = jax.ShapeDtypeStruct((rows, 1024), f32)
    return pl.pallas_call(body, name=name, grid_spec=grid_spec, out_shape=[out] * 4,
                          compiler_params=_cparams(("parallel",)))(me, *args, w, m, v)


def _place():
    return lax.axis_index("x"), lax.axis_index("y"), lax.axis_index("c")


def all_gather_blocks(shard):
    rows, cols = shard.shape

    def body(x_ref, out_ref, send_sems, recv_sems, local_sem):
        x, y, c = _place()
        me, sibling = (x, y, c), (x, y, 1 - c)
        chips = [(1 - x, y), (x, 1 - y), (1 - x, 1 - y)]

        def slot(px, py, pc):
            return out_ref.at[4 * px + 2 * py + pc]

        def copy(k, block, to, src=None):
            return pltpu.make_async_remote_copy(
                src_ref=slot(*block) if src is None else src, dst_ref=slot(*block), send_sem=send_sems.at[k],
                recv_sem=recv_sems.at[k], device_id=to, device_id_type=MESH)

        mine = pltpu.make_async_copy(x_ref, slot(*me), local_sem)
        mine.start()
        first = [copy(0, me, sibling, src=x_ref)]
        first += [copy(1 + j, me, (*chip, c), src=x_ref) for j, chip in enumerate(chips)]
        for cp in first:
            cp.start()
        passed = [copy(4 + j, (*chip, c), sibling) for j, chip in enumerate(chips)]
        for j, chip in enumerate(chips):
            copy(1 + j, (*chip, c), me).wait_recv()
            passed[j].start()
        copy(0, sibling, me).wait_recv()
        for j, chip in enumerate(chips):
            copy(4 + j, (*chip, 1 - c), me).wait_recv()
        for cp in first + passed:
            cp.wait_send()
        mine.wait()

    return pl.pallas_call(
        body, name="all_gather_blocks", out_shape=jax.ShapeDtypeStruct((N_DEV, rows, cols), shard.dtype),
        in_specs=[pl.BlockSpec(memory_space=pl.ANY)], out_specs=pl.BlockSpec(memory_space=pl.ANY),
        scratch_shapes=[pltpu.SemaphoreType.DMA((7,)), pltpu.SemaphoreType.DMA((7,)), pltpu.SemaphoreType.DMA],
        compiler_params=pltpu.CompilerParams(has_side_effects=True))(shard)


_HBM = pl.BlockSpec(memory_space=pltpu.HBM)
_SEM = pl.BlockSpec(memory_space=pltpu.SEMAPHORE)


def _peer(k):
    x, y, c = _place()
    px, py, pc = (1 - x if k & 4 else x), (1 - y if k & 2 else y), (1 - c if k & 1 else c)
    return (px, py, pc), 4 * px + 2 * py + pc


def scatter_start(parts, name):
    per_device = parts.ndim == 3

    def body(p_ref, land_ref, send_sems, recv_sems, p_thru, land_thru, token):
        x, y, c = _place()
        me = 4 * x + 2 * y + c
        for k in range(1, N_DEV):
            place, idx = _peer(k)
            pltpu.make_async_remote_copy(src_ref=p_ref.at[idx] if per_device else p_ref, dst_ref=land_ref.at[me],
                                         send_sem=send_sems.at[k - 1], recv_sem=recv_sems.at[k - 1], device_id=place,
                                         device_id_type=MESH).start()
        token[...] = jnp.zeros_like(token)

    land_shape = parts.shape if per_device else (N_DEV,) + parts.shape
    landing = lax.empty(land_shape, parts.dtype)
    return pl.pallas_call(
        body, name=name,
        out_shape=(pltpu.SemaphoreType.DMA((N_DEV - 1,)), pltpu.SemaphoreType.DMA((N_DEV - 1,)),
                   pltpu.HBM(parts.shape, parts.dtype), pltpu.HBM(land_shape, parts.dtype),
                   jax.ShapeDtypeStruct((8, 128), f32)),
        in_specs=(_HBM, _HBM), out_specs=(_SEM, _SEM, _HBM, _HBM, pl.BlockSpec(memory_space=pltpu.VMEM)),
        input_output_aliases={0: 2, 1: 3},
        compiler_params=pltpu.CompilerParams(has_side_effects=pltpu.SideEffectType.DATAFLOW_SIDE_EFFECTING),
    )(pltpu.with_memory_space_constraint(parts, pltpu.HBM), pltpu.with_memory_space_constraint(landing, pltpu.HBM))


def scatter_wait(send_sems, recv_sems, parts_thru, land_thru, after, name):
    per_device = parts_thru.ndim == 3

    def body(p_ref, land_ref, send_sems, recv_sems, after_ref, p_out, land_out):
        for k in range(1, N_DEV):
            place, idx = _peer(k)
            copy = pltpu.make_async_remote_copy(src_ref=p_ref.at[idx] if per_device else p_ref, dst_ref=land_ref.at[idx],
                                                send_sem=send_sems.at[k - 1], recv_sem=recv_sems.at[k - 1],
                                                device_id=place, device_id_type=MESH)
            copy.wait_send()
            copy.wait_recv()

    return pl.pallas_call(
        body, name=name,
        out_shape=(pltpu.HBM(parts_thru.shape, parts_thru.dtype), pltpu.HBM(land_thru.shape, land_thru.dtype)),
        in_specs=(_HBM, _HBM, _SEM, _SEM, pl.BlockSpec(memory_space=pl.ANY)), out_specs=(_HBM, _HBM),
        input_output_aliases={0: 0, 1: 1},
        compiler_params=pltpu.CompilerParams(has_side_effects=pltpu.SideEffectType.DATAFLOW_SIDE_EFFECTING),
    )(parts_thru, land_thru, send_sems, recv_sems, after)


def _tail_rows(conv_part, small, extra):
    lead = conv_part.shape[:-1]
    rep = jnp.concatenate([small[n].reshape(-1).astype(f32) for n in SMALL] + [extra.reshape(1).astype(f32)])
    flat = jnp.concatenate([conv_part, jnp.broadcast_to(rep, lead + rep.shape),
                            jnp.zeros(lead + (ROWS_TAIL * 1024 - TAIL_ELEMS,), f32)], axis=-1)
    return flat.reshape(lead + (ROWS_TAIL, 1024))


def _late_rows(w_in_t, tail):
    lead = tail.shape[:-2]
    zeros = lambda r: jnp.zeros(lead + (r, 1024), f32)
    return jnp.concatenate([w_in_t, zeros(OFF_TAIL - IN_SHARD), tail, zeros(LATE_ROWS - OFF_TAIL - ROWS_TAIL)], axis=-2)


def _early_rows(w_ps, w_out, w_up_t, w_down, w_pa_t):
    return jnp.concatenate([w_ps, w_out, w_up_t, w_down, w_pa_t.reshape(w_pa_t.shape[:-2] + (ROWS_PA, 1024))], axis=-2)


def _pack_shard(vals):
    tail = _tail_rows(vals["conv_w"].reshape(-1), vals, jnp.zeros((), f32))
    return jnp.concatenate([_late_rows(vals["w_in"].T, tail),
                            _early_rows(vals["w_proj_ssd"], vals["w_out"], vals["w_up"].T, vals["w_down"],
                                        vals["w_proj_attn"].T)], axis=0)


def _unpack_shard(late, early):
    e = lambda lo, hi: early[lo - LATE_ROWS:hi - LATE_ROWS]
    out = {"w_in": late[0:IN_SHARD].T, "w_proj_ssd": e(OFF_PS, OFF_OUT), "w_out": e(OFF_OUT, OFF_UP),
           "w_up": e(OFF_UP, OFF_DOWN).T, "w_down": e(OFF_DOWN, OFF_PA),
           "w_proj_attn": e(OFF_PA, PACK_ROWS).reshape(D_MODEL // N_DEV, ATTN_OUT).T}
    flat = late[OFF_TAIL:OFF_TAIL + ROWS_TAIL].reshape(-1)
    out["conv_w"] = flat[0:CONV_SHARD].reshape(D_CONV, CONV_DIM // N_DEV)
    off = CONV_SHARD
    for n in SMALL:
        out[n] = flat[off:off + SMALL_SIZES[n]]
        off += SMALL_SIZES[n]
    out["_extra"] = flat[off]
    return out


def _blocks(g):
    return g.reshape(N_DEV, g.shape[0] // N_DEV, g.shape[1])


def _pack_early_parts(full):
    return _early_rows(_blocks(full["w_proj_ssd"]), _blocks(full["w_out"]), _blocks(full["w_up_t"]),
                       _blocks(full["w_down"]), _blocks(full["w_proj_attn_t"]))


def _pack_late_parts(full, small, extra):
    conv = full["conv_w"].reshape(D_CONV, N_DEV, CONV_DIM // N_DEV).transpose(1, 0, 2).reshape(N_DEV, CONV_SHARD)
    return _late_rows(_blocks(full["w_in_t"]), _tail_rows(conv, small, extra))


def _gather_weights(w):
    conv_bits = lax.bitcast_convert_type(w["conv_w"], bf16).reshape(-1)
    conv_rows = jnp.concatenate([conv_bits, jnp.zeros((16 * 1024 - 2 * CONV_SHARD,), bf16)]).reshape(16, 1024)
    packed = _pack_shard(w)
    first = OFF_TAIL + ROWS_TAIL
    got = all_gather_blocks(jnp.concatenate([packed[0:OFF_TAIL].astype(bf16), conv_rows], axis=0))
    got, rest = lax.optimization_barrier((got, packed[first:].astype(bf16)))
    send_sems, recv_sems, rest_thru, land_thru, token = scatter_start(rest, "gather_start")
    conv =lax.bitcast_convert_type(got[:, OFF_TAIL:OFF_TAIL + 4].reshape(N_DEV, 4096)[:, 0:2 * CONV_SHARD]
                                    .reshape(N_DEV, D_CONV, CONV_DIM // N_DEV, 2), f32)
    now = {"w_in_t": got[:, 0:IN_SHARD].reshape(IN_COLS, 1024), "conv_w": conv.transpose(1, 0, 2).reshape(D_CONV, CONV_DIM)}

    def later(after):
        mine, landed = scatter_wait(send_sems, recv_sems, rest_thru, land_thru, after, "gather_wait")
        x, y, c = _place()
        landed = lax.dynamic_update_slice(landed, mine[None], (4 * x + 2 * y + c, 0, 0))
        whole = lambda lo, hi: landed[:, lo - first:hi - first].reshape(N_DEV * (hi - lo), 1024)
        return {"w_proj_ssd": whole(OFF_PS, OFF_OUT), "w_out": whole(OFF_OUT, OFF_UP), "w_up_t": whole(OFF_UP, OFF_DOWN),
                "w_down": whole(OFF_DOWN, OFF_PA),
                "w_proj_attn_t": landed[:, OFF_PA - first:PACK_ROWS - first].reshape(D_MODEL, ATTN_OUT)}

    return now, later, token


def _row(v, width=None):
    v = v.reshape(1, -1).astype(f32)
    return v if width is None else jnp.pad(v, ((0, 0), (0, width - v.shape[1])))


def _local_step(x2, tgt, wf, p, send_early=None, late_weights=None, start_token=None, send_late=None):
    t = x2.shape[0]
    o = np.cumsum((0,) + IN_SPLITS)
    wt = wf["w_in_t"]
    wt_z, wt_xbc, wt_dt = wt[o[0]:o[1]], wt[o[1]:o[2]], wt[o[2]:o[4]]
    wt_qkv, wt_gate = wt[o[4]:o[7]], wt[o[7]:o[8]]

    spread = lambda v: jnp.broadcast_to(v.astype(f32)[..., None], v.shape + (128,))
    conv_w_b, conv_b_b = spread(wf["conv_w"]), spread(p["conv_b"])
    dt_bias_b = spread(jnp.concatenate([p["dt_bias_f"], p["dt_bias_b"]]))
    a_f, a_b = -jnp.exp(p["a_log_f"].astype(f32)), -jnp.exp(p["a_log_b"].astype(f32))
    a_coef_b = spread(jnp.concatenate([a_f, a_b]))
    skip_b = spread(jnp.repeat(p["d_skip"], SSD_HEAD_DIM))
    nw_b, bg_row = spread(p["ssd_norm_w"]), _row(p["b_gate"])
    g1, b1, g2, b2 = _row(p["ln1_g"]), _row(p["ln1_b"]), _row(p["ln2_g"]), _row(p["ln2_b"])

    xb = (x2 if start_token is None else x2 + start_token[0, 0]).astype(MXU_DTYPE)
    u_z = mm_nt(wt_z, xb, "in_z")
    u_xbc = mm_nt(wt_xbc, xb, "in_xbc")
    u_dt = mm_nt(wt_dt, xb, "in_dt")
    u_qkv = mm_nt_split(xb, wt_qkv, "in_qkv", 256, bf16)
    u_gate = mm_nt(xb, wt_gate, "in_gate")
    xbc_c, dsilu = conv_fwd_t(u_xbc, conv_w_b, conv_b_b)
    dt_t = dt_fwd_t(u_dt, dt_bias_b)
    y_f, h_f = ssd_fwd_t(xbc_c, dt_t, a_coef_b, False, "ssd_fwd_f")
    y_scan, h_b, yn = ssd_fwd_t(xbc_c, dt_t, a_coef_b, True, "ssd_fwd_b", prev=y_f, tail=(u_z, skip_b, nw_b))
    if late_weights is not None:
        wf = {**wf, **late_weights(yn)}
    y_ssd = mm_tn(yn, wf["w_proj_ssd"], "proj_ssd")

    def strided(a, dil):
        return a.reshape(t // dil, dil * 256)

    qkv, outs, lses = [], [], []
    for pi, (_, dil) in enumerate(DIL_PATTERNS):
        q, k, v = (strided(u_qkv[N_PATTERNS * s + pi], dil) for s in range(3))
        qkv.append((q, k, v))
        op, lp = attn_fwd(q, k, v, pi, dil, f"attn_fwd_{pi}")
        outs.append(op.reshape(t, 256))
        lses.append(lp.reshape(t, 256))
    ya, lse = attn_combine(outs, lses)
    y_att = mm_nt(ya, wf["w_proj_attn_t"], "proj_attn")
    m = merge_fwd(u_gate, bg_row, y_ssd, y_att)
    mix = mm_nn(m, wf["w_out"], "out_proj")
    h1, h1b = ln1_fwd(x2, mix, g1, b1)
    r_up, p_act = mm_nt(h1b, wf["w_up_t"], "mlp_up", relu2=True)
    f_dn = mm_nn(p_act, wf["w_down"], "mlp_down")
    dr2, dr2b, dg2, db2, loss8 = ln2_loss(h1, f_dn, g2, b2, tgt)

    full, small = {}, {}
    da = mm_nt(dr2b, wf["w_down"], "d_mlp_act", out_dtype=bf16, relu2_of=r_up)
    full["w_down"] = mm_tn(p_act, dr2b, "dw_down")
    full["w_up_t"] = mm_tn(da, h1b, "dw_up")
    dh1 = mm_nn(da, wf["w_up_t"], "d_h1", acc_in=dr2, acc_scale=ALPHA)
    dr1, dr1b, dg1, db1 = ln1_bwd(dh1, x2, mix, g1)
    dm = mm_nt(dr1b, wf["w_out"], "d_merge")
    full["w_out"] = mm_tn(m, dr1b, "dw_out")
    dys, dya_p, dga, dgb, dba, dbb = merge_bwd(dm, u_gate, bg_row, y_ssd, y_att)
    dyn = mm_nt(wf["w_proj_ssd"], dys, "d_yn")
    full["w_proj_ssd"] = mm_nn(yn, dys, "dw_proj_ssd")
    dya = mm_nn(dya_p, wf["w_proj_attn_t"], "d_ya")
    full["w_proj_attn_t"] = mm_tn(dya_p, ya, "dw_proj_attn")
    if send_early is not None:
        skip_b = skip_b + send_early(full)[0, 0]

    dxf, dbf, dcf, ddtf, daf, dy, du_ssd, dnw, ddx = ssd_bwd_t(xbc_c, dt_t, a_coef_b, dyn, h_f, False, "ssd_bwd_f",
                                                               skip_b=skip_b, tail=(y_scan, u_z, nw_b))
    dxs, dbs, dcs, ddtb, dab = ssd_bwd_t(xbc_c, dt_t, a_coef_b, dy, h_b, True, "ssd_bwd_b", prev=(dxf, dbf, dcf))
    du_ssd, dcw_x, dcb_x = conv_bwd_t(u_xbc, dsilu, dxs, conv_w_b, du_ssd, "conv_bwd_x", 0)
    du_ssd, dcw_b, dcb_b = conv_bwd_t(u_xbc, dsilu, dbs, conv_w_b, du_ssd, "conv_bwd_b", D_INNER)
    du_ssd, dcw_c, dcb_c = conv_bwd_t(u_xbc, dsilu, dcs, conv_w_b, du_ssd, "conv_bwd_c", D_INNER + 512)
    du_ssd, dbias = dt_bwd_t(ddtf, ddtb, u_dt, dt_bias_b, du_ssd)

    delta = attn_delta(dya, ya)
    dqs, dks, dvs = [], [], []
    for pi, (_, dil) in enumerate(DIL_PATTERNS):
        q, k, v = qkv[pi]
        sd, sl_, sdel = strided(dya, dil), strided(lse, dil), strided(delta, dil)
        dqs.append(attn_dq(q, k, v, sd, sl_, sdel, pi, dil, f"attn_dq_{pi}").reshape(t, 256))
        dk, dv = attn_dkv(q, k, v, sd, sl_, sdel, pi, dil, f"attn_dkv_{pi}")
        dks.append(dk.reshape(t, 256))
        dvs.append(dv.reshape(t, 256))
    du_qkv = jnp.concatenate(dqs + dks + dvs, axis=1)
    du_gate = jnp.concatenate([dga, dgb], axis=1)

    full["w_in_t"] = jnp.concatenate(
        [mm_nn(du_ssd, xb, "dw_in_ssd"), mm_tn(du_qkv, xb, "dw_in_qkv"), mm_tn(du_gate, xb, "dw_in_gate")], axis=0)
    lanes = lambda v: jnp.sum(v, axis=-1)
    full["conv_w"] = jnp.concatenate([lanes(dcw_x), lanes(dcw_b), lanes(dcw_c)], axis=1)

    small["b_gate"] = jnp.concatenate([dba, dbb], axis=1)
    small["conv_b"] = jnp.concatenate([lanes(dcb_x), lanes(dcb_b), lanes(dcb_c)])
    dbias = lanes(dbias)
    small["dt_bias_f"], small["dt_bias_b"] = dbias[0:32], dbias[32:64]
    small["a_log_f"] = lanes(daf) * a_f
    small["a_log_b"] = lanes(dab) * a_b
    small["d_skip"] = jnp.sum(lanes(ddx).reshape(SSD_HEADS, SSD_HEAD_DIM), axis=1)
    small["ssd_norm_w"] = lanes(dnw)
    small["ln1_g"], small["ln1_b"], small["ln2_g"], small["ln2_b"] = dg1, db1, dg2, db2

    wt_ssd = wt[0:SSD_COLS]
    if send_late is not None:
        wt_ssd = wt_ssd + send_late(full, small, loss8[0, 0])[0, 0].astype(wt_ssd.dtype)
    dx = mm_tn(du_ssd, wt_ssd, "dx_ssd", acc_in=dr1, acc_scale=ALPHA)
    dx = mm_nn(du_qkv, wt_qkv, "dx_qkv", acc_in=dx)
    dx = mm_nn(du_gate, wt_gate, "dx_gate", acc_in=dx)
    return loss8[0, 0], dx, full, small


def kernel(x, w_in, b_gate, conv_w, conv_b, dt_bias_f, dt_bias_b, a_log_f, a_log_b, d_skip, ssd_norm_w, w_proj_ssd, w_proj_attn, w_out, ln1_g, ln1_b, w_up, w_down, ln2_g, ln2_b, loss_target, m_w_in, m_b_gate, m_conv_w, m_conv_b, m_dt_bias_f, m_dt_bias_b, m_a_log_f, m_a_log_b, m_d_skip, m_ssd_norm_w, m_w_proj_ssd, m_w_proj_attn, m_w_out, m_ln1_g, m_ln1_b, m_w_up, m_w_down, m_ln2_g, m_ln2_b, v_w_in, v_b_gate, v_conv_w, v_conv_b, v_dt_bias_f, v_dt_bias_b, v_a_log_f, v_a_log_b, v_d_skip, v_ssd_norm_w, v_w_proj_ssd, v_w_proj_attn, v_w_out, v_ln1_g, v_ln1_b, v_w_up, v_w_down, v_ln2_g, v_ln2_b):
    given = dict(locals())
    w = {n: given[n] for n in WEIGHTS}
    mom = {n: given["m_" + n] for n in WEIGHTS}
    var = {n: given["v_" + n] for n in WEIGHTS}
    t = x.shape[1]
    wf, late_weights, start_token = _gather_weights(w)
    in_flight = []

    def send_early(full):
        send_sems, recv_sems, parts_thru, land_thru, token = scatter_start(_pack_early_parts(full), "scatter_start")
        in_flight.append((send_sems, recv_sems, parts_thru, land_thru))
        return token

    def send_late(full, small, loss):
        late = _pack_late_parts(full, small, loss)
        rows = scatter_start(late.astype(bf16), "late_start")
        tail = scatter_start(late[:, OFF_TAIL:OFF_TAIL + ROWS_TAIL], "tail_start")
        in_flight.extend([rows[0:4], tail[0:4]])
        return rows[4] + tail[4]

    loss, dx, full, small = _local_step(x.reshape(t, D_MODEL), loss_target.reshape(t, D_MODEL), wf, w, send_early,
                                        late_weights, start_token, send_late)
    x_, y_, c_ = _place()
    me = (4 * x_ + 2 * y_ + c_).astype(jnp.int32).reshape(1)
    wp, mp, vp = _pack_shard(w), _pack_shard(mom), _pack_shard(var)
    early_parts, early_landed = scatter_wait(*in_flight[0], dx, "scatter_wait")
    early_out = adamw_sum8(early_landed, early_parts, me, wp, mp, vp, LATE_ROWS, "adamw_early")
    late_parts, late_landed = scatter_wait(*in_flight[1], dx, "late_wait")
    tail_parts, tail_landed = scatter_wait(*in_flight[2], dx, "tail_wait")
    late_out = adamw_sum8(late_landed, late_parts, me, wp, mp, vp, 0, "adamw_late", tails=(tail_landed, tail_parts))
    g, delta, new_m, new_v = (_unpack_shard(a, b) for a, b in zip(late_out, early_out))
    outs = [g["_extra"], dx.reshape(x.shape)]
    for d in (g, delta, new_m, new_v):
        outs += [d[n].reshape(w[n].shape) for n in WEIGHTS]
    return tuple(outs)
```
